```python
import jax, jax.numpy as jnp
from jax import lax
import numpy as np

D_MODEL = 2048
BATCH = 8
SEQ = 4096
DEPTH = 1

D_MIX = D_MODEL
D_ATTN = D_MIX // 2
D_POOL = D_MIX - D_ATTN
HEAD_DIM = 128
N_HEADS = D_ATTN // HEAD_DIM
ROPE_DIM = HEAD_DIM // 4
ROPE_THETA = 500000.0
DILATED_PATTERNS = ((128, 1), (512, 4), (2048, 16))
POOL_WINDOWS = (2, 4, 8, 16)
N_POOL_GROUPS = len(POOL_WINDOWS)
POOL_GROUP_DIM = D_POOL // N_POOL_GROUPS
D_IN = 3 * D_ATTN + D_POOL + D_MIX
LN_EPS = 1e-5
DEEPNORM_ALPHA = (2.0 * DEPTH) ** 0.25
DEEPNORM_BETA = (8.0 * DEPTH) ** -0.25

kernel_name = "hybrid_dilated_attn_pool_deepnorm"


def rotary_partial(t, positions):
    half = ROPE_DIM // 2
    inv_freq = ROPE_THETA ** (-(2.0 * jnp.arange(half, dtype=jnp.float32)) / ROPE_DIM)
    ang = positions.astype(jnp.float32)[:, None] * inv_freq[None, :]
    cos = jnp.cos(ang)[None, :, None, :]
    sin = jnp.sin(ang)[None, :, None, :]
    t32 = t.astype(jnp.float32)
    t1, t2, rest = t32[..., :half], t32[..., half:ROPE_DIM], t32[..., ROPE_DIM:]
    out = jnp.concatenate([t1 * cos - t2 * sin, t1 * sin + t2 * cos, rest], axis=-1)
    return out.astype(t.dtype)


def banded_causal_attention(q, k, v, n_keys):
    G, N, Dh = q.shape
    W = n_keys
    nb = -(-N // W)
    Np = nb * W
    pad = Np - N
    qp = jnp.pad(q, ((0, 0), (0, pad), (0, 0)))
    kp = jnp.pad(k, ((0, 0), (W, pad), (0, 0)))
    vp = jnp.pad(v, ((0, 0), (W, pad), (0, 0)))
    qb = qp.reshape(G, nb, W, Dh)
    kb = jnp.concatenate([kp[:, :Np].reshape(G, nb, W, Dh), kp[:, W:].reshape(G, nb, W, Dh)], axis=2)
    vb = jnp.concatenate([vp[:, :Np].reshape(G, nb, W, Dh), vp[:, W:].reshape(G, nb, W, Dh)], axis=2)
    s = jnp.einsum('gcqd,gckd->gcqk', qb, kb).astype(jnp.float32) * (Dh ** -0.5)
    qi = jnp.arange(W)[:, None]
    kj = jnp.arange(2 * W)[None, :]
    dist = W + qi - kj
    key_pos = (jnp.arange(nb)[:, None, None] - 1) * W + kj[None]
    mask = (dist >= 0)[None] & (dist <= W)[None] & (key_pos >= 0)
    s = jnp.where(mask[None], s, -jnp.inf)
    m = jnp.max(s, axis=-1, keepdims=True)
    p = jnp.exp(s - m)
    den = jnp.sum(p, axis=-1, keepdims=True)
    o = jnp.einsum('gcqk,gckd->gcqd', p, vb.astype(jnp.float32)) / den
    lse = (m + jnp.log(den))[..., 0]
    return o.reshape(G, Np, Dh)[:, :N], lse.reshape(G, Np)[:, :N]


def dilated_attention(q, k, v, window, dilation):
    B, H, S, Dh = q.shape
    n_sub = S // dilation

    def to_residues(t):
        return t.reshape(B, H, n_sub, dilation, Dh).transpose(0, 1, 3, 2, 4).reshape(B * H * dilation, n_sub, Dh)

    o, lse = banded_causal_attention(to_residues(q), to_residues(k), to_residues(v), window // dilation)
    o = o.reshape(B, H, dilation, n_sub, Dh).transpose(0, 1, 3, 2, 4).reshape(B, H, S, Dh)
    lse = lse.reshape(B, H, dilation, n_sub).transpose(0, 1, 3, 2).reshape(B, H, S)
    return o, lse


def dilated_attention_mixture(q, k, v):
    outs, lses = [], []
    for window, dilation in DILATED_PATTERNS:
        o, lse = dilated_attention(q, k, v, window, dilation)
        outs.append(o)
        lses.append(lse)
    w = jax.nn.softmax(jnp.stack(lses, axis=0), axis=0)
    return jnp.einsum('pbhs,pbhsd->bhsd', w, jnp.stack(outs, axis=0))


def causal_pool_minus_identity(u, window):
    S = u.shape[1]
    u32 = u.astype(jnp.float32)
    c = jnp.cumsum(u32, axis=1)
    c_shift = jnp.pad(c, ((0, 0), (window, 0), (0, 0)))[:, :S]
    count = jnp.minimum(jnp.arange(S) + 1, window).astype(jnp.float32)
    return (c - c_shift) / count[None, :, None] - u32


def layer_norm(h, gain, bias):
    h32 = h.astype(jnp.float32)
    mu = jnp.mean(h32, axis=-1, keepdims=True)
    var = jnp.mean(jnp.square(h32 - mu), axis=-1, keepdims=True)
    return (h32 - mu) * lax.rsqrt(var + LN_EPS) * gain.astype(jnp.float32) + bias.astype(jnp.float32)


def _fwd_setup_inputs(seed: int = 0) -> dict:
    key = jax.random.key(seed)
    ks = jax.random.split(key, 8)
    x = jax.random.normal(ks[0], (BATCH, SEQ, D_MODEL), jnp.float32)
    w_in = jax.random.normal(ks[1], (DEPTH, D_MODEL, D_IN), jnp.float32) * D_MODEL ** -0.5
    w_pool = jax.random.normal(ks[2], (DEPTH, N_POOL_GROUPS, POOL_GROUP_DIM, POOL_GROUP_DIM), jnp.float32) * POOL_GROUP_DIM ** -0.5
    pool_scale = 1.0 + 0.02 * jax.random.normal(ks[3], (DEPTH, D_POOL), jnp.float32)
    w_out = jax.random.normal(ks[4], (DEPTH, D_MIX, D_MODEL), jnp.float32) * (D_MIX ** -0.5) * DEEPNORM_BETA
    ln_gain = 1.0 + 0.02 * jax.random.normal(ks[5], (DEPTH, D_MODEL), jnp.float32)
    ln_bias = 0.02 * jax.random.normal(ks[6], (DEPTH, D_MODEL), jnp.float32)
    return {"x": x, "w_in": w_in, "w_pool": w_pool, "pool_scale": pool_scale,
            "w_out": w_out, "ln_gain": ln_gain, "ln_bias": ln_bias}


def _fwd_reference(x, w_in, w_pool, pool_scale, w_out, ln_gain, ln_bias):
    B, S, _ = x.shape
    positions = jnp.arange(S, dtype=jnp.int32)
    for layer in range(DEPTH):
        h = jnp.einsum('bsd,de->bse', x, w_in[layer])
        q, k, v, u_pool, gate = jnp.split(
            h, [D_ATTN, 2 * D_ATTN, 3 * D_ATTN, 3 * D_ATTN + D_POOL], axis=-1)

        q = rotary_partial(q.reshape(B, S, N_HEADS, HEAD_DIM), positions).transpose(0, 2, 1, 3)
        k = rotary_partial(k.reshape(B, S, N_HEADS, HEAD_DIM), positions).transpose(0, 2, 1, 3)
        v = v.reshape(B, S, N_HEADS, HEAD_DIM).transpose(0, 2, 1, 3)
        attn = dilated_attention_mixture(q, k, v)
        attn = attn.transpose(0, 2, 1, 3).reshape(B, S, D_ATTN)

        u_groups = u_pool.reshape(B, S, N_POOL_GROUPS, POOL_GROUP_DIM)
        pooled = jnp.stack([causal_pool_minus_identity(u_groups[:, :, g], POOL_WINDOWS[g])
                            for g in range(N_POOL_GROUPS)], axis=2)
        pool_out = jnp.einsum('bsgc,gcd->bsgd', pooled, w_pool[layer].astype(jnp.float32))
        pool_out = pool_out.reshape(B, S, D_POOL) * pool_scale[layer].astype(jnp.float32)

        y = jnp.concatenate([attn, pool_out], axis=-1) * jax.nn.silu(gate.astype(jnp.float32))
        out = jnp.einsum('bse,ed->bsd', y.astype(x.dtype), w_out[layer])

        x = layer_norm(DEEPNORM_ALPHA * x.astype(jnp.float32) + out.astype(jnp.float32),
                       ln_gain[layer], ln_bias[layer]).astype(x.dtype)
    return x


import jax as _jax
import jax.numpy as _jnp

TWIN_FORMAT = 'train_step'
FWD_PARAMS = ['x', 'w_in', 'w_pool', 'pool_scale', 'w_out', 'ln_gain', 'ln_bias']
TWIN_WEIGHTS = ['w_in', 'w_pool', 'pool_scale', 'w_out', 'ln_gain', 'ln_bias']
TWIN_DIFF_INPUT = 'x'
TWIN_INPUTS = ['x', 'w_in', 'w_pool', 'pool_scale', 'w_out', 'ln_gain', 'ln_bias', 'loss_target', 'm_w_in', 'm_w_pool', 'm_pool_scale', 'm_w_out', 'm_ln_gain', 'm_ln_bias', 'v_w_in', 'v_w_pool', 'v_pool_scale', 'v_w_out', 'v_ln_gain', 'v_ln_bias']
TWIN_OUTPUTS = ['loss', 'grad_x', 'grad_w_in', 'grad_w_pool', 'grad_pool_scale', 'grad_w_out', 'grad_ln_gain', 'grad_ln_bias', 'delta_w_in', 'delta_w_pool', 'delta_pool_scale', 'delta_w_out', 'delta_ln_gain', 'delta_ln_bias', 'new_m_w_in', 'new_m_w_pool', 'new_m_pool_scale', 'new_m_w_out', 'new_m_ln_gain', 'new_m_ln_bias', 'new_v_w_in', 'new_v_w_pool', 'new_v_pool_scale', 'new_v_w_out', 'new_v_ln_gain', 'new_v_ln_bias']
TWIN_LEAF_KINDS = {'loss': 'loss', 'grad_x': 'grad_x', 'grad_w_in': 'grad_w', 'grad_w_pool': 'grad_w', 'grad_pool_scale': 'grad_w', 'grad_w_out': 'grad_w', 'grad_ln_gain': 'grad_w', 'grad_ln_bias': 'grad_w', 'delta_w_in': 'delta_w', 'delta_w_pool': 'delta_w', 'delta_pool_scale': 'delta_w', 'delta_w_out': 'delta_w', 'delta_ln_gain': 'delta_w', 'delta_ln_bias': 'delta_w', 'new_m_w_in': 'new_m', 'new_m_w_pool': 'new_m', 'new_m_pool_scale': 'new_m', 'new_m_w_out': 'new_m', 'new_m_ln_gain': 'new_m', 'new_m_ln_bias': 'new_m', 'new_v_w_in': 'new_v', 'new_v_w_pool': 'new_v', 'new_v_pool_scale': 'new_v', 'new_v_w_out': 'new_v', 'new_v_ln_gain': 'new_v', 'new_v_ln_bias': 'new_v'}


def _forward(args):
    return _fwd_reference(*[args[k] for k in FWD_PARAMS])


def _output_shape():
    def fwd():
        inp = _fwd_setup_inputs(0)
        return _fwd_reference(*[inp[k] for k in FWD_PARAMS])
    out = _jax.eval_shape(fwd)
    return out.shape, out.dtype

N_MICROBATCH = 1
ADAM_LR = 0.001
ADAM_B1 = 0.9
ADAM_B2 = 0.999
ADAM_EPS = 1e-08
ADAM_WD = 0.01
ADAM_STEP = 10
PER_EXAMPLE_BATCH_AXIS = {'x': 0, 'loss_target': 0}
SHARED_INPUTS = []
_WEIGHT_DTYPES = {'w_in': _jnp.float32, 'w_pool': _jnp.float32, 'pool_scale': _jnp.float32, 'w_out': _jnp.float32, 'ln_gain': _jnp.float32, 'ln_bias': _jnp.float32}
MOMENT_SCALE = {'w_in': 1.401414e-02, 'w_pool': 2.273959e-02, 'pool_scale': 2.454956e-02, 'w_out': 2.778350e-02, 'ln_gain': 1.598752e+01, 'ln_bias': 2.850008e-01}


def _to_microbatches(a, axis):
    t = _jnp.moveaxis(a, axis, 0)
    t = t.reshape((N_MICROBATCH, t.shape[0] // N_MICROBATCH) + t.shape[1:])
    return _jnp.moveaxis(t, 1, axis + 1)


def setup_inputs(seed: int = 0) -> dict:
    inp = _fwd_setup_inputs(seed)
    key = _jax.random.fold_in(_jax.random.key(seed), 7919)
    shape, _ = _output_shape()
    out = dict(inp)
    out["loss_target"] = _jax.random.normal(_jax.random.fold_in(key, 0), shape, _jnp.float32)
    for i, name in enumerate(TWIN_WEIGHTS):
        w = inp[name].astype(_jnp.float32)
        if MOMENT_SCALE is None:
            s = _jnp.sqrt(_jnp.mean(_jnp.square(w)) + 1e-30)
        else:
            s = MOMENT_SCALE[name]
        km, kv = _jax.random.split(_jax.random.fold_in(key, i + 1))
        out[name] = w
        out["m_" + name] = s * _jax.random.normal(km, w.shape, _jnp.float32)
        out["v_" + name] = (s * s) * _jax.random.uniform(kv, w.shape, _jnp.float32, 0.5, 1.5)
    if N_MICROBATCH > 1:
        for name, axis in PER_EXAMPLE_BATCH_AXIS.items():
            out[name] = _to_microbatches(out[name], axis)
    return {'x': out['x'], 'w_in': out['w_in'], 'w_pool': out['w_pool'], 'pool_scale': out['pool_scale'], 'w_out': out['w_out'], 'ln_gain': out['ln_gain'], 'ln_bias': out['ln_bias'], 'loss_target': out['loss_target'], 'm_w_in': out['m_w_in'], 'm_w_pool': out['m_w_pool'], 'm_pool_scale': out['m_pool_scale'], 'm_w_out': out['m_w_out'], 'm_ln_gain': out['m_ln_gain'], 'm_ln_bias': out['m_ln_bias'], 'v_w_in': out['v_w_in'], 'v_w_pool': out['v_w_pool'], 'v_pool_scale': out['v_pool_scale'], 'v_w_out': out['v_w_out'], 'v_ln_gain': out['v_ln_gain'], 'v_ln_bias': out['v_ln_bias']}


def _loss(weights, diff, rest, loss_target):
    with _jax.named_scope("forward"):
        args = {**rest, TWIN_DIFF_INPUT: diff, **{k: w.astype(_WEIGHT_DTYPES[k]) for k, w in weights.items()}}
        y = _forward(args)
    with _jax.named_scope("loss_head"):
        err = _jnp.square(y.astype(_jnp.float32) - loss_target)
        return 0.5 * _jnp.sum(_jnp.mean(err, axis=-1)) if err.ndim else 0.5 * err


def _adamw(w, g, m, v):
    m = ADAM_B1 * m + (1.0 - ADAM_B1) * g
    v = ADAM_B2 * v + (1.0 - ADAM_B2) * _jnp.square(g)
    m_hat = m / (1.0 - ADAM_B1 ** ADAM_STEP)
    v_hat = v / (1.0 - ADAM_B2 ** ADAM_STEP)
    delta = -ADAM_LR * (m_hat / (_jnp.sqrt(v_hat) + ADAM_EPS) + ADAM_WD * w)
    return delta, m, v


def reference(x, w_in, w_pool, pool_scale, w_out, ln_gain, ln_bias, loss_target, m_w_in, m_w_pool, m_pool_scale, m_w_out, m_ln_gain, m_ln_bias, v_w_in, v_w_pool, v_pool_scale, v_w_out, v_ln_gain, v_ln_bias):
    given = dict(x=x, w_in=w_in, w_pool=w_pool, pool_scale=pool_scale, w_out=w_out, ln_gain=ln_gain, ln_bias=ln_bias, loss_target=loss_target, m_w_in=m_w_in, m_w_pool=m_w_pool, m_pool_scale=m_pool_scale, m_w_out=m_w_out, m_ln_gain=m_ln_gain, m_ln_bias=m_ln_bias, v_w_in=v_w_in, v_w_pool=v_w_pool, v_pool_scale=v_pool_scale, v_w_out=v_w_out, v_ln_gain=v_ln_gain, v_ln_bias=v_ln_bias)
    weights = {n: given[n] for n in TWIN_WEIGHTS}
    shared = {n: given[n] for n in SHARED_INPUTS}
    per_example = {n: given[n] for n in ['x']}
    grad_fn = _jax.value_and_grad(_loss, argnums=(0, 1))

    def one_microbatch(ex, loss_target):
        ex = dict(ex)
        diff = ex.pop(TWIN_DIFF_INPUT)
        return grad_fn(weights, diff, {**shared, **ex}, loss_target)

    if N_MICROBATCH == 1:
        loss, (grad_w, grad_x) = one_microbatch(per_example, given["loss_target"])
    else:
        def body(carry, xs):
            loss_sum, grad_sum = carry
            l_k, (gw_k, gx_k) = one_microbatch(xs[0], xs[1])
            with _jax.named_scope("update"):
                return (loss_sum + l_k, _jax.tree.map(_jnp.add, grad_sum, gw_k)), gx_k

        init = (_jnp.zeros((), _jnp.float32), _jax.tree.map(_jnp.zeros_like, weights))
        (loss, grad_w), grad_x = _jax.lax.scan(body, init, (per_example, given["loss_target"]))
    with _jax.named_scope("update"):
        delta_w, new_m, new_v = {}, {}, {}
        for n in TWIN_WEIGHTS:
            delta_w[n], new_m[n], new_v[n] = _adamw(weights[n], grad_w[n], given["m_" + n], given["v_" + n])
    return (loss, grad_x, *[grad_w[n] for n in TWIN_WEIGHTS], *[delta_w[n] for n in TWIN_WEIGHTS],
            *[new_m[n] for n in TWIN_WEIGHTS], *[new_v[n] for n in TWIN_WEIGHTS])
```

```python
import functools

import jax
import jax.numpy as jnp
from jax import lax
from jax.experimental import pallas as pl
from jax.experimental.pallas import tpu as pltpu

F32 = jnp.float32
BF16 = jnp.bfloat16
MESH = pl.DeviceIdType.MESH
ANY = pl.BlockSpec(memory_space=pl.ANY)

D_MODEL = 2048
D_ATTN = 1024
D_POOL = 1024
HEAD_DIM = 128
N_HEADS = 8
ROPE_DIM = 32
ROPE_THETA = 500000.0
DILATIONS = (1, 4, 16)
KEY_BLOCK = 128
POOL_WINDOWS = (2, 4, 8, 16)
POOL_GROUP_DIM = 256
POOL_HALO = 16
D_QKV = 3 * D_ATTN
D_UG = D_POOL + D_MODEL
D_IN = D_QKV + D_UG
N_SHARDS = 4
SHARD_IN = D_IN // N_SHARDS
LN_EPS = 1e-5
DEEPNORM_ALPHA = 2.0 ** 0.25
ADAM_LR = 0.001
ADAM_B1 = 0.9
ADAM_B2 = 0.999
ADAM_EPS = 1e-08
ADAM_WD = 0.01
ADAM_STEP = 10
NEG = -1e30
MIB = 1024 * 1024


def _params(sem, vmem_mib):
    return pltpu.CompilerParams(dimension_semantics=sem, vmem_limit_bytes=vmem_mib * MIB)


def _dot_nn(a, b):
    return jnp.dot(a, b, preferred_element_type=F32)


def _dot_nt(a, b):
    return lax.dot_general(a, b, (((1,), (1,)), ((), ())), preferred_element_type=F32)


def _dot_tn(a, b):
    return lax.dot_general(a, b, (((0,), (0,)), ((), ())), preferred_element_type=F32)


def _fold_rows(a):
    r, c = a.shape
    return jnp.sum(a.reshape(r // 8, 8, c), axis=0)


def _cast_bf16(a, name, rows):
    r, c = a.shape

    def body(a_ref, o_ref):
        o_ref[...] = a_ref[...].astype(BF16)

    return pl.pallas_call(
        body, name=name, grid=(r // rows,),
        in_specs=[pl.BlockSpec((rows, c), lambda i: (i, 0))],
        out_specs=pl.BlockSpec((rows, c), lambda i: (i, 0)),
        out_shape=jax.ShapeDtypeStruct((r, c), BF16),
        compiler_params=_params(("parallel",), 32),
    )(a)


def _allgather_weights(shards):
    n = len(shards)

    def body(*refs):
        src = refs[:n]
        dst = refs[n:2 * n]
        local_sems, ici_send, ici_recv, d2d_send, d2d_recv = refs[2 * n:]
        x, y, c = lax.axis_index("x"), lax.axis_index("y"), lax.axis_index("c")
        mine = 2 * x + y
        chips = [(1 - x, y), (x, 1 - y), (1 - x, 1 - y)]

        def half(a, core):
            rows = shards[a].shape[0] // 2
            return pl.ds(core * rows, rows)

        locals_ = []
        sends = []
        for a in range(n):
            cp = pltpu.make_async_copy(src[a], dst[a].at[mine], local_sems.at[a])
            cp.start()
            locals_.append(cp)
            for k, (cx, cy) in enumerate(chips):
                cp = pltpu.make_async_remote_copy(
                    src_ref=src[a].at[half(a, c)], dst_ref=dst[a].at[mine, half(a, c)],
                    send_sem=ici_send.at[a, k], recv_sem=ici_recv.at[a, k],
                    device_id=(cx, cy, c), device_id_type=MESH)
                cp.start()
                sends.append(cp)
        for k, (cx, cy) in enumerate(chips):
            theirs = 2 * cx + cy
            for a in range(n):
                landed = dst[a].at[theirs, half(a, c)]
                pltpu.make_async_remote_copy(
                    src_ref=landed, dst_ref=landed, send_sem=ici_send.at[a, k], recv_sem=ici_recv.at[a, k],
                    device_id=(cx, cy, c), device_id_type=MESH).wait_recv()
                cp = pltpu.make_async_remote_copy(
                    src_ref=landed, dst_ref=landed, send_sem=d2d_send.at[a, k], recv_sem=d2d_recv.at[a, k],
                    device_id=(x, y, 1 - c), device_id_type=MESH)
                cp.start()
                sends.append(cp)
        for k, (cx, cy) in enumerate(chips):
            theirs = 2 * cx + cy
            for a in range(n):
                passed = dst[a].at[theirs, half(a, 1 - c)]
                pltpu.make_async_remote_copy(
                    src_ref=passed, dst_ref=passed, send_sem=d2d_send.at[a, k], recv_sem=d2d_recv.at[a, k],
                    device_id=(x, y, 1 - c), device_id_type=MESH).wait_recv()
        for cp in sends:
            cp.wait_send()
        for cp in locals_:
            cp.wait()

    return pl.pallas_call(
        body, name="allgather_weights",
        in_specs=[ANY] * n, out_specs=[ANY] * n,
        out_shape=[jax.ShapeDtypeStruct((N_SHARDS,) + s.shape, s.dtype) for s in shards],
        scratch_shapes=[pltpu.SemaphoreType.DMA((n,)), pltpu.SemaphoreType.DMA((n, 3)), pltpu.SemaphoreType.DMA((n, 3)),
                        pltpu.SemaphoreType.DMA((n, 3)), pltpu.SemaphoreType.DMA((n, 3))],
    )(*shards)


def _rope_tables(seq):
    half = ROPE_DIM // 2
    inv_freq = ROPE_THETA ** (-(2.0 * jnp.arange(half, dtype=F32)) / ROPE_DIM)
    ang = jnp.arange(seq, dtype=jnp.int32).astype(F32)[:, None] * inv_freq[None, :]
    cos, sin = jnp.cos(ang), jnp.sin(ang)
    pad = jnp.zeros((seq, HEAD_DIM - ROPE_DIM), F32)
    zeros = jnp.zeros((seq, half), F32)
    c_tab = jnp.concatenate([cos, cos, pad + 1.0], axis=1)
    up_tab = jnp.concatenate([-sin, zeros, pad], axis=1)
    down_tab = jnp.concatenate([zeros, sin, pad], axis=1)
    return c_tab, up_tab, down_tab


def _rotate_heads(t, c_tab, up_tab, down_tab):
    outs = []
    for h in range(t.shape[1] // HEAD_DIM):
        th = t[:, h * HEAD_DIM:(h + 1) * HEAD_DIM]
        up = pltpu.roll(th, HEAD_DIM - ROPE_DIM // 2, axis=1)
        down = pltpu.roll(th, ROPE_DIM // 2, axis=1)
        outs.append(th * c_tab + up * up_tab + down * down_tab)
    return outs[0] if len(outs) == 1 else jnp.concatenate(outs, axis=1)


def _in_proj(x, w_in_g, tabs, first_tile, n_tiles, n_rot_tiles, out_dtype, name):
    seq = x.shape[0]
    tm, tn = 512, 512
    per_shard = SHARD_IN // tn

    def body(x_ref, w_ref, c_ref, up_ref, down_ref, o_ref, xb_ref):
        j = pl.program_id(1)

        @pl.when(j == 0)
        def _():
            xb_ref[...] = x_ref[...].astype(BF16)

        acc = _dot_nn(xb_ref[...], w_ref[...])
        if n_rot_tiles == 0:
            o_ref[...] = acc.astype(out_dtype)
        else:
            @pl.when(j < n_rot_tiles)
            def _():
                o_ref[...] = _rotate_heads(acc, c_ref[...], up_ref[...], down_ref[...]).astype(out_dtype)

            @pl.when(j >= n_rot_tiles)
            def _():
                o_ref[...] = acc.astype(out_dtype)

    tab_spec = pl.BlockSpec((tm, HEAD_DIM), lambda i, j: (i, 0))
    return pl.pallas_call(
        body, name=name, grid=(seq // tm, n_tiles),
        in_specs=[pl.BlockSpec((tm, D_MODEL), lambda i, j: (i, 0)),
                  pl.BlockSpec((None, D_MODEL, tn),
                               lambda i, j: ((j + first_tile) // per_shard, 0, (j + first_tile) % per_shard)),
                  tab_spec, tab_spec, tab_spec],
        out_specs=pl.BlockSpec((tm, tn), lambda i, j: (i, j)),
        out_shape=jax.ShapeDtypeStruct((seq, n_tiles * tn), out_dtype),
        scratch_shapes=[pltpu.VMEM((tm, D_MODEL), BF16)],
        compiler_params=_params(("parallel", "arbitrary"), 40),
    )(x, w_in_g, *tabs)


def _band_masks():
    row = lax.broadcasted_iota(jnp.int32, (KEY_BLOCK, KEY_BLOCK), 0)
    col = lax.broadcasted_iota(jnp.int32, (KEY_BLOCK, KEY_BLOCK), 1)
    return col <= row, col >= row


def _attn_fwd(qkv, dil, name):
    seq = qkv.shape[0]
    n = seq // dil
    nb = n // KEY_BLOCK
    groups = dil * N_HEADS
    blocks_qkv = D_QKV // HEAD_DIM
    scale = HEAD_DIM ** -0.5

    def body(q_ref, k_ref, v_ref, o_ref, l_ref):
        cur_mask, prev_mask = _band_masks()

        def step(i, carry):
            r0 = pl.multiple_of(i * KEY_BLOCK, KEY_BLOCK)
            p0 = pl.multiple_of(jnp.maximum(i - 1, 0) * KEY_BLOCK, KEY_BLOCK)
            q = q_ref[pl.ds(r0, KEY_BLOCK), :]
            s_cur = jnp.where(cur_mask, _dot_nt(q, k_ref[pl.ds(r0, KEY_BLOCK), :]) * scale, NEG)
            s_prev = jnp.where(jnp.logical_and(prev_mask, i > 0),
                               _dot_nt(q, k_ref[pl.ds(p0, KEY_BLOCK), :]) * scale, NEG)
            m = jnp.maximum(jnp.max(s_cur, axis=-1, keepdims=True), jnp.max(s_prev, axis=-1, keepdims=True))
            p_cur = jnp.exp(s_cur - m)
            p_prev = jnp.exp(s_prev - m)
            den = jnp.sum(p_cur, axis=-1, keepdims=True) + jnp.sum(p_prev, axis=-1, keepdims=True)
            o = _dot_nn(p_cur.astype(BF16), v_ref[pl.ds(r0, KEY_BLOCK), :]) \
                + _dot_nn(p_prev.astype(BF16), v_ref[pl.ds(p0, KEY_BLOCK), :])
            o_ref[pl.ds(r0, KEY_BLOCK), :] = o / den
            l_ref[pl.ds(r0, KEY_BLOCK), :] = jnp.broadcast_to(m + jnp.log(den), (KEY_BLOCK, HEAD_DIM))
            return carry

        lax.fori_loop(0, nb, step, 0)

    def col(offset):
        return lambda g: (0, (g // N_HEADS) * blocks_qkv + offset + g % N_HEADS)

    spec = lambda offset: pl.BlockSpec((n, HEAD_DIM), col(offset))
    out_spec = pl.BlockSpec((n, HEAD_DIM), lambda g: (0, g))
    view = qkv.reshape(n, dil * D_QKV)
    o, lse = pl.pallas_call(
        body, name=name, grid=(groups,),
        in_specs=[spec(0), spec(N_HEADS), spec(2 * N_HEADS)],
        out_specs=[out_spec, out_spec],
        out_shape=[jax.ShapeDtypeStruct((n, dil * D_ATTN), F32)] * 2,
        compiler_params=_params(("parallel",), 40),
    )(view, view, view)
    return o.reshape(seq, D_ATTN), lse.reshape(seq, D_ATTN)


def _attn_bwd(qkv, dmix, mixpre, lse_all, dil, name):
    seq = qkv.shape[0]
    n = seq // dil
    nb = n // KEY_BLOCK
    groups = dil * N_HEADS
    blocks_qkv = D_QKV // HEAD_DIM
    blocks_mix = D_MODEL // HEAD_DIM
    scale = HEAD_DIM ** -0.5

    def body(q_ref, k_ref, v_ref, do_ref, o_ref, l_ref, dq_ref, dk_ref, dv_ref):
        cur_mask, prev_mask = _band_masks()
        dk_ref[...] = jnp.zeros_like(dk_ref)
        dv_ref[...] = jnp.zeros_like(dv_ref)

        def step(i, carry):
            r0 = pl.multiple_of(i * KEY_BLOCK, KEY_BLOCK)
            p0 = pl.multiple_of(jnp.maximum(i - 1, 0) * KEY_BLOCK, KEY_BLOCK)
            cur, prev = pl.ds(r0, KEY_BLOCK), pl.ds(p0, KEY_BLOCK)
            q = q_ref[cur, :]
            k_cur, k_prev = k_ref[cur, :], k_ref[prev, :]
            v_cur, v_prev = v_ref[cur, :], v_ref[prev, :]
            do = do_ref[cur, :]
            delta = jnp.sum(do * o_ref[cur, :], axis=-1, keepdims=True)
            do_b = do.astype(BF16)
            lse = l_ref[cur, :]
            s_cur = jnp.where(cur_mask, _dot_nt(q, k_cur) * scale, NEG)
            s_prev = jnp.where(jnp.logical_and(prev_mask, i > 0), _dot_nt(q, k_prev) * scale, NEG)
            p_cur = jnp.exp(s_cur - lse)
            p_prev = jnp.exp(s_prev - lse)
            ds_cur = (p_cur * (_dot_nt(do_b, v_cur) - delta) * scale).astype(BF16)
            ds_prev = (p_prev * (_dot_nt(do_b, v_prev) - delta) * scale).astype(BF16)
            dq_ref[cur, :] = _dot_nn(ds_cur, k_cur) + _dot_nn(ds_prev, k_prev)
            dk_ref[cur, :] += _dot_tn(ds_cur, q)
            dk_ref[prev, :] += _dot_tn(ds_prev, q)
            dv_ref[cur, :] += _dot_tn(p_cur.astype(BF16), do_b)
            dv_ref[prev, :] += _dot_tn(p_prev.astype(BF16), do_b)
            return carry

        lax.fori_loop(0, nb, step, 0)

    def col(width_blocks, offset):
        return lambda g: (0, (g // N_HEADS) * width_blocks + offset + g % N_HEADS)

    qkv_spec = lambda offset: pl.BlockSpec((n, HEAD_DIM), col(blocks_qkv, offset))
    mix_spec = pl.BlockSpec((n, HEAD_DIM), col(blocks_mix, 0))
    out_spec = pl.BlockSpec((n, HEAD_DIM), lambda g: (0, g))
    qkv_v = qkv.reshape(n, dil * D_QKV)
    dq, dk, dv = pl.pallas_call(
        body, name=name, grid=(groups,),
        in_specs=[qkv_spec(0), qkv_spec(N_HEADS), qkv_spec(2 * N_HEADS), mix_spec, mix_spec, out_spec],
        out_specs=[out_spec] * 3,
        out_shape=[jax.ShapeDtypeStruct((n, dil * D_ATTN), F32)] * 3,
        compiler_params=_params(("parallel",), 48),
    )(qkv_v, qkv_v, qkv_v, dmix.reshape(n, dil * D_MODEL), mixpre.reshape(n, dil * D_MODEL),
      lse_all.reshape(n, dil * D_ATTN))
    return dq.reshape(seq, D_ATTN), dk.reshape(seq, D_ATTN), dv.reshape(seq, D_ATTN)


def _window_sums(ext, window, backward):
    rows = ext.shape[0]
    acc, span = ext, 1
    while span < window:
        acc = acc + pltpu.roll(acc, (rows - span) if backward else span, axis=0)
        span *= 2
    return acc


def _mix_gate(o_list, lse_list, hug, w_pool_g, pool_scale):
    seq = hug.shape[0]
    tm = 256
    halo_blocks = tm // POOL_HALO

    def body(o1, o2, o3, l1, l2, l3, u_ref, halo_ref, ga_ref, gp_ref, wp_ref, sc_ref, y_ref, mix_ref, lse_ref, pooled_ref):
        i = pl.program_id(0)
        la, lb, lc = l1[...], l2[...], l3[...]
        mx = jnp.maximum(jnp.maximum(la, lb), lc)
        ea, eb, ec = jnp.exp(la - mx), jnp.exp(lb - mx), jnp.exp(lc - mx)
        tot = ea + eb + ec
        attn = (ea * o1[...] + eb * o2[...] + ec * o3[...]) / tot
        lse_ref[...] = mx + jnp.log(tot)

        u = u_ref[...]
        halo = jnp.where(i > 0, halo_ref[...], 0.0)
        ext = jnp.concatenate([halo, u], axis=0)
        pos = i * tm + lax.broadcasted_iota(jnp.int32, (tm, 1), 0)
        pre = []
        for g, window in enumerate(POOL_WINDOWS):
            cols = slice(g * POOL_GROUP_DIM, (g + 1) * POOL_GROUP_DIM)
            sums = _window_sums(ext[:, cols], window, backward=False)[POOL_HALO:, :]
            count = jnp.minimum(pos + 1, window).astype(F32)
            pooled = (sums / count - u[:, cols]).astype(BF16)
            pooled_ref[:, cols] = pooled
            pre.append(_dot_nn(pooled, wp_ref[g]))
        pool_pre = jnp.concatenate(pre, axis=1)
        mix_ref[:, :D_ATTN] = attn
        mix_ref[:, D_ATTN:] = pool_pre
        ga, gp = ga_ref[...], gp_ref[...]
        y_ref[:, :D_ATTN] = (attn * (ga * jax.nn.sigmoid(ga))).astype(BF16)
        y_ref[:, D_ATTN:] = (pool_pre * sc_ref[...] * (gp * jax.nn.sigmoid(gp))).astype(BF16)

    row = lambda width, cb=0: pl.BlockSpec((tm, width), lambda i: (i, cb))
    return pl.pallas_call(
        body, name="mix_gate", grid=(seq // tm,),
        in_specs=[row(D_ATTN)] * 6 + [
            row(D_POOL),
            pl.BlockSpec((POOL_HALO, D_POOL), lambda i: (jnp.maximum(i * halo_blocks - 1, 0), 0)),
            row(D_ATTN, 1), row(D_POOL, 2),
            pl.BlockSpec((len(POOL_WINDOWS), POOL_GROUP_DIM, POOL_GROUP_DIM), lambda i: (0, 0, 0)),
            pl.BlockSpec((1, D_POOL), lambda i: (0, 0))],
        out_specs=[row(D_MODEL), row(D_MODEL), row(D_ATTN), row(D_POOL)],
        out_shape=[jax.ShapeDtypeStruct((seq, D_MODEL), BF16), jax.ShapeDtypeStruct((seq, D_MODEL), F32),
                   jax.ShapeDtypeStruct((seq, D_ATTN), F32), jax.ShapeDtypeStruct((seq, D_POOL), BF16)],
        compiler_params=_params(("parallel",), 48),
    )(*o_list, *lse_list, hug, hug, hug, hug, w_pool_g, pool_scale)


def _out_proj_loss(y, w_out_g, x, target, gain, bias):
    seq = x.shape[0]
    tm = 256

    def body(y_ref, w_ref, x_ref, t_ref, g_ref, b_ref, dz_ref, dzb_ref, gg_ref, gb_ref, loss_ref):
        @pl.when(pl.program_id(0) == 0)
        def _():
            gg_ref[...] = jnp.zeros_like(gg_ref)
            gb_ref[...] = jnp.zeros_like(gb_ref)
            loss_ref[...] = jnp.zeros_like(loss_ref)

        z = DEEPNORM_ALPHA * x_ref[...] + _dot_nn(y_ref[...], w_ref[...])
        mu = jnp.mean(z, axis=-1, keepdims=True)
        zc = z - mu
        rstd = lax.rsqrt(jnp.mean(zc * zc, axis=-1, keepdims=True) + LN_EPS)
        xhat = zc * rstd
        gain_v = g_ref[...]
        diff = xhat * gain_v + b_ref[...] - t_ref[...]
        sq = _fold_rows(diff * diff)
        part = sq[:, :128]
        for k in range(1, D_MODEL // 128):
            part = part + sq[:, k * 128:(k + 1) * 128]
        loss_ref[...] += part
        dln = diff * (1.0 / D_MODEL)
        gg_ref[...] += _fold_rows(dln * xhat)
        gb_ref[...] += _fold_rows(dln)
        dxhat = dln * gain_v
        dz = rstd * (dxhat - jnp.mean(dxhat, axis=-1, keepdims=True)
                     - xhat * jnp.mean(dxhat * xhat, axis=-1, keepdims=True))
        dz_ref[...] = dz
        dzb_ref[...] = dz.astype(BF16)

    row = lambda: pl.BlockSpec((tm, D_MODEL), lambda i: (i, 0))
    vec = lambda: pl.BlockSpec((1, D_MODEL), lambda i: (0, 0))
    acc = lambda width: pl.BlockSpec((8, width), lambda i: (0, 0))
    return pl.pallas_call(
        body, name="out_proj_loss", grid=(seq // tm,),
        in_specs=[row(), pl.BlockSpec((D_MODEL, D_MODEL), lambda i: (0, 0)), row(), row(), vec(), vec()],
        out_specs=[row(), row(), acc(D_MODEL), acc(D_MODEL), acc(128)],
        out_shape=[jax.ShapeDtypeStruct((seq, D_MODEL), F32), jax.ShapeDtypeStruct((seq, D_MODEL), BF16),
                   jax.ShapeDtypeStruct((8, D_MODEL), F32), jax.ShapeDtypeStruct((8, D_MODEL), F32),
                   jax.ShapeDtypeStruct((8, 128), F32)],
        compiler_params=_params(("arbitrary",), 48),
    )(y, w_out_g.reshape(D_MODEL, D_MODEL), x, target, gain, bias)


def _dy_gate_bwd(dzb, w_out_g, hug, mixpre, scale_vec):
    seq = dzb.shape[0]
    tm = 256

    def body(dz_ref, w_ref, ga_ref, gp_ref, mix_ref, sc_ref, dgate_ref, dmix_ref):
        dy = _dot_nt(dz_ref[...], w_ref[...])
        for part, g_ref in enumerate((ga_ref, gp_ref)):
            cols = slice(part * D_ATTN, (part + 1) * D_ATTN)
            gate = g_ref[...]
            sig = jax.nn.sigmoid(gate)
            dyp = dy[:, cols]
            dmix_ref[:, cols] = dyp * (gate * sig)
            dgate_ref[:, cols] = (dyp * (mix_ref[:, cols] * sc_ref[:, cols])
                                  * (sig * (1.0 + gate * (1.0 - sig)))).astype(BF16)

    row = lambda width, cb=0: pl.BlockSpec((tm, width), lambda i: (i, cb))
    return pl.pallas_call(
        body, name="dy_gate_bwd", grid=(seq // tm,),
        in_specs=[row(D_MODEL), pl.BlockSpec((D_MODEL, D_MODEL), lambda i: (0, 0)),
                  row(D_ATTN, 1), row(D_POOL, 2), row(D_MODEL), pl.BlockSpec((1, D_MODEL), lambda i: (0, 0))],
        out_specs=[row(D_MODEL), row(D_MODEL)],
        out_shape=[jax.ShapeDtypeStruct((seq, D_MODEL), BF16), jax.ShapeDtypeStruct((seq, D_MODEL), F32)],
        compiler_params=_params(("parallel",), 48),
    )(dzb, w_out_g.reshape(D_MODEL, D_MODEL), hug, hug, mixpre, scale_vec)


def _pool_bwd(dmix, mixpre, pooled, w_pool_g, pool_scale):
    seq = dmix.shape[0]
    tm = 256
    halo_blocks = tm // POOL_HALO
    last = seq // tm - 1
    n_groups = len(POOL_WINDOWS)

    def body(dpo_ref, halo_ref, pre_ref, pooled_ref, wp_ref, sc_ref, du_ref, gw_ref, gs_ref):
        i = pl.program_id(0)

        @pl.when(i == 0)
        def _():
            gw_ref[...] = jnp.zeros_like(gw_ref)
            gs_ref[...] = jnp.zeros_like(gs_ref)

        dpo = dpo_ref[...]
        scale = sc_ref[...]
        gs_ref[...] += _fold_rows(dpo * pre_ref[...])
        halo = jnp.where(i < last, halo_ref[...], 0.0)
        dpw = (jnp.concatenate([dpo, halo], axis=0) * scale).astype(BF16)
        pos = i * tm + lax.broadcasted_iota(jnp.int32, (tm + POOL_HALO, 1), 0)
        for g, window in enumerate(POOL_WINDOWS):
            cols = slice(g * POOL_GROUP_DIM, (g + 1) * POOL_GROUP_DIM)
            dpw_g = dpw[:, cols]
            gw_ref[g] += _dot_tn(pooled_ref[:, cols], dpw_g[:tm, :])
            dpooled = _dot_nt(dpw_g, wp_ref[g])
            count = jnp.minimum(pos + 1, window).astype(F32)
            sums = _window_sums(dpooled / count, window, backward=True)
            du_ref[:, cols] = (sums[:tm, :] - dpooled[:tm, :]).astype(BF16)

    row = lambda width, cb=0: pl.BlockSpec((tm, width), lambda i: (i, cb))
    return pl.pallas_call(
        body, name="pool_bwd", grid=(seq // tm,),
        in_specs=[row(D_POOL, 1),
                  pl.BlockSpec((POOL_HALO, D_POOL),
                               lambda i: (jnp.minimum((i + 1) * halo_blocks, seq // POOL_HALO - 1), 1)),
                  row(D_POOL, 1), row(D_POOL),
                  pl.BlockSpec((n_groups, POOL_GROUP_DIM, POOL_GROUP_DIM), lambda i: (0, 0, 0)),
                  pl.BlockSpec((1, D_POOL), lambda i: (0, 0))],
        out_specs=[row(D_POOL),
                   pl.BlockSpec((n_groups, POOL_GROUP_DIM, POOL_GROUP_DIM), lambda i: (0, 0, 0)),
                   pl.BlockSpec((8, D_POOL), lambda i: (0, 0))],
        out_shape=[jax.ShapeDtypeStruct((seq, D_POOL), BF16),
                   jax.ShapeDtypeStruct((n_groups, POOL_GROUP_DIM, POOL_GROUP_DIM), F32),
                   jax.ShapeDtypeStruct((8, D_POOL), F32)],
        compiler_params=_params(("arbitrary",), 40),
    )(dmix, dmix, mixpre, pooled, w_pool_g, pool_scale)


def _sum_patterns(parts, tabs, unrotate, name):
    seq = parts[0].shape[0]
    tm, tn = 256, 512

    def body(a_ref, b_ref, c_ref, ct_ref, up_ref, down_ref, o_ref):
        tot = a_ref[...] + b_ref[...] + c_ref[...]
        if unrotate:
            tot = _rotate_heads(tot, ct_ref[...], -up_ref[...], -down_ref[...])
        o_ref[...] = tot.astype(BF16)

    spec = pl.BlockSpec((tm, tn), lambda i, j: (i, j))
    tab = pl.BlockSpec((tm, HEAD_DIM), lambda i, j: (i, 0))
    return pl.pallas_call(
        body, name=name, grid=(seq // tm, D_ATTN // tn),
        in_specs=[spec, spec, spec, tab, tab, tab], out_specs=spec,
        out_shape=jax.ShapeDtypeStruct((seq, D_ATTN), BF16),
        compiler_params=_params(("parallel", "parallel"), 32),
    )(*parts, *tabs)


def _grad_w_in(x, dh):
    seq = x.shape[0]
    ts, td, te = 512, D_MODEL // 2, SHARD_IN
    nk = seq // ts

    def body(x_ref, dh_ref, o_ref, acc_ref):
        k = pl.program_id(2)

        @pl.when(k == 0)
        def _():
            acc_ref[...] = jnp.zeros_like(acc_ref)

        acc_ref[...] += _dot_tn(x_ref[...].astype(BF16), dh_ref[...])

        @pl.when(k == nk - 1)
        def _():
            o_ref[...] = acc_ref[...]

    return pl.pallas_call(
        body, name="grad_w_in", grid=(N_SHARDS, 2, nk),
        in_specs=[pl.BlockSpec((ts, td), lambda e, d, k: (k, d)), pl.BlockSpec((ts, te), lambda e, d, k: (k, e))],
        out_specs=pl.BlockSpec((None, None, td, te), lambda e, d, k: (e, d, 0, 0)),
        out_shape=jax.ShapeDtypeStruct((N_SHARDS, 2, td, te), F32),
        scratch_shapes=[pltpu.VMEM((td, te), F32)],
        compiler_params=_params(("parallel", "parallel", "arbitrary"), 48),
    )(x, dh)


def _grad_w_out(y, dzb):
    seq = y.shape[0]
    ts, te = 512, 1024
    nk = seq // ts

    def body(y_ref, dz_ref, o_ref, acc_ref):
        k = pl.program_id(1)

        @pl.when(k == 0)
        def _():
            acc_ref[...] = jnp.zeros_like(acc_ref)

        acc_ref[...] += _dot_tn(y_ref[...], dz_ref[...])

        @pl.when(k == nk - 1)
        def _():
            o_ref[...] = acc_ref[...]

    return pl.pallas_call(
        body, name="grad_w_out", grid=(D_MODEL // te, nk),
        in_specs=[pl.BlockSpec((ts, te), lambda e, k: (k, e)), pl.BlockSpec((ts, D_MODEL), lambda e, k: (k, 0))],
        out_specs=pl.BlockSpec((te, D_MODEL), lambda e, k: (e, 0)),
        out_shape=jax.ShapeDtypeStruct((D_MODEL, D_MODEL), F32),
        scratch_shapes=[pltpu.VMEM((te, D_MODEL), F32)],
        compiler_params=_params(("parallel", "arbitrary"), 48),
    )(y, dzb)


def _grad_x(dh, w_in_g, dz):
    seq = dh.shape[0]
    tm, tk = 512, 512
    per_shard = SHARD_IN // tk
    nk = D_IN // tk

    def body(dh_ref, w_ref, dz_ref, o_ref, acc_ref):
        k = pl.program_id(1)

        @pl.when(k == 0)
        def _():
            acc_ref[...] = DEEPNORM_ALPHA * dz_ref[...]

        acc_ref[...] += _dot_nt(dh_ref[...], w_ref[...])

        @pl.when(k == nk - 1)
        def _():
            o_ref[...] = acc_ref[...]

    return pl.pallas_call(
        body, name="grad_x", grid=(seq // tm, nk),
        in_specs=[pl.BlockSpec((tm, tk), lambda i, k: (i, k)),
                  pl.BlockSpec((None, D_MODEL, tk), lambda i, k: (k // per_shard, 0, k % per_shard)),
                  pl.BlockSpec((tm, D_MODEL), lambda i, k: (i, 0))],
        out_specs=pl.BlockSpec((tm, D_MODEL), lambda i, k: (i, 0)),
        out_shape=jax.ShapeDtypeStruct((seq, D_MODEL), F32),
        scratch_shapes=[pltpu.VMEM((tm, D_MODEL), F32)],
        compiler_params=_params(("parallel", "arbitrary"), 48),
    )(dh, w_in_g, dz)


def _local_step(x, target, w_in_g, w_out_g, w_pool_g, pool_scale, gain, bias):
    seq = x.shape[0]
    tabs = _rope_tables(seq)
    n_qkv_tiles, n_ug_tiles = D_QKV // 512, D_UG // 512
    qkv = _in_proj(x, w_in_g, tabs, 0, n_qkv_tiles, 2 * D_ATTN // 512, BF16, "in_proj_qkv")
    hug = _in_proj(x, w_in_g, tabs, n_qkv_tiles, n_ug_tiles, 0, F32, "in_proj_pool_gate")
    o_list, lse_list = [], []
    for dil in DILATIONS:
        o, lse = _attn_fwd(qkv, dil, "attn_fwd_d%d" % dil)
        o_list.append(o)
        lse_list.append(lse)
    y, mixpre, lse_all, pooled = _mix_gate(o_list, lse_list, hug, w_pool_g, pool_scale)
    dz, dzb, gain_part, bias_part, loss_part = _out_proj_loss(y, w_out_g, x, target, gain, bias)
    scale_vec = jnp.concatenate([jnp.ones((1, D_ATTN), F32), pool_scale], axis=1)
    dgate, dmix = _dy_gate_bwd(dzb, w_out_g, hug, mixpre, scale_vec)
    g_w_out = _grad_w_out(y, dzb)
    du, g_w_pool, scale_part = _pool_bwd(dmix, mixpre, pooled, w_pool_g, pool_scale)
    parts = []
    for dil in DILATIONS:
        parts.extend(_attn_bwd(qkv, dmix, mixpre, lse_all, dil, "attn_bwd_d%d" % dil))
    dq = _sum_patterns(parts[0::3], tabs, True, "sum_dq")
    dk = _sum_patterns(parts[1::3], tabs, True, "sum_dk")
    dv = _sum_patterns(parts[2::3], tabs, False, "sum_dv")
    dh = jnp.concatenate([dq, dk, dv, du, dgate], axis=1)
    g_w_in = _grad_w_in(x, dh)
    g_x = _grad_x(dh, w_in_g, dz)
    small = jnp.concatenate([scale_part, gain_part, bias_part, loss_part], axis=1)
    return g_x, g_w_in, g_w_out, g_w_pool, small


def _exchange_halves(grads):
    n = len(grads)

    def body(*refs):
        src = refs[:n]
        dst = refs[n:2 * n]
        send_sems, recv_sems = refs[2 * n:]
        x, y, c = lax.axis_index("x"), lax.axis_index("y"), lax.axis_index("c")
        copies = []
        for a in range(n):
            for j in range(N_SHARDS):
                cp = pltpu.make_async_remote_copy(
                    src_ref=src[a].at[j, 1 - c], dst_ref=dst[a].at[j],
                    send_sem=send_sems.at[a, j], recv_sem=recv_sems.at[a, j],
                    device_id=(x, y, 1 - c), device_id_type=MESH)
                cp.start()
                copies.append(cp)
        for cp in copies:
            cp.wait()

    return pl.pallas_call(
        body, name="exchange_halves",
        in_specs=[ANY] * n, out_specs=[ANY] * n,
        out_shape=[jax.ShapeDtypeStruct((N_SHARDS,) + g.shape[2:], g.dtype) for g in grads],
        scratch_shapes=[pltpu.SemaphoreType.DMA((n, N_SHARDS)), pltpu.SemaphoreType.DMA((n, N_SHARDS))],
    )(*grads)


def _add_own_half(grad, recv, core, name):
    _, _, r, c = grad.shape
    tr = min(r, 256)

    def body(core_ref, g_ref, r_ref, o_ref):
        o_ref[...] = g_ref[...] + r_ref[...]

    return pl.pallas_call(
        body, name=name,
        grid_spec=pltpu.PrefetchScalarGridSpec(
            num_scalar_prefetch=1, grid=(N_SHARDS, r // tr),
            in_specs=[pl.BlockSpec((None, None, tr, c), lambda j, i, core_ref: (j, core_ref[0], i, 0)),
                      pl.BlockSpec((None, tr, c), lambda j, i, core_ref: (j, i, 0))],
            out_specs=pl.BlockSpec((None, tr, c), lambda j, i, core_ref: (j, i, 0))),
        out_shape=jax.ShapeDtypeStruct((N_SHARDS, r, c), F32),
        compiler_params=_params(("parallel", "parallel"), 32),
    )(core, grad, recv)


def _scatter_to_chips(sums):
    n = len(sums)

    def body(*refs):
        src = refs[:n]
        dst = refs[n:2 * n]
        send_sems, recv_sems = refs[2 * n:]
        x, y, c = lax.axis_index("x"), lax.axis_index("y"), lax.axis_index("c")
        chips = [(1 - x, y), (x, 1 - y), (1 - x, 1 - y)]
        copies = []
        for a in range(n):
            for k, (cx, cy) in enumerate(chips):
                cp = pltpu.make_async_remote_copy(
                    src_ref=src[a].at[2 * cx + cy], dst_ref=dst[a].at[k],
                    send_sem=send_sems.at[a, k], recv_sem=recv_sems.at[a, k],
                    device_id=(cx, cy, c), device_id_type=MESH)
                cp.start()
                copies.append(cp)
        for cp in copies:
            cp.wait()

    return pl.pallas_call(
        body, name="scatter_to_chips",
        in_specs=[ANY] * n, out_specs=[ANY] * n,
        out_shape=[jax.ShapeDtypeStruct((3,) + s.shape[1:], s.dtype) for s in sums],
        scratch_shapes=[pltpu.SemaphoreType.DMA((n, 3)), pltpu.SemaphoreType.DMA((n, 3))],
    )(*sums)


def _add_chips(sums, recv, chip, name):
    _, r, c = sums.shape
    tr = min(r, 256)

    def body(chip_ref, s_ref, r_ref, o_ref):
        o_ref[...] = ((s_ref[...] + r_ref[0]) + r_ref[1]) + r_ref[2]

    return pl.pallas_call(
        body, name=name,
        grid_spec=pltpu.PrefetchScalarGridSpec(
            num_scalar_prefetch=1, grid=(r // tr,),
            in_specs=[pl.BlockSpec((None, tr, c), lambda i, chip_ref: (chip_ref[0], i, 0)),
                      pl.BlockSpec((3, tr, c), lambda i, chip_ref: (0, i, 0))],
            out_specs=pl.BlockSpec((tr, c), lambda i, chip_ref: (i, 0))),
        out_shape=jax.ShapeDtypeStruct((r, c), F32),
        compiler_params=_params(("parallel",), 32),
    )(chip, sums, recv)


def _share_with_sibling(reduced):
    n = len(reduced)

    def body(*refs):
        src = refs[:n]
        dst = refs[n:2 * n]
        local_sems, send_sems, recv_sems = refs[2 * n:]
        x, y, c = lax.axis_index("x"), lax.axis_index("y"), lax.axis_index("c")
        copies = []
        for a in range(n):
            cp = pltpu.make_async_copy(src[a], dst[a].at[c], local_sems.at[a])
            cp.start()
            copies.append(cp)
            cp = pltpu.make_async_remote_copy(
                src_ref=src[a], dst_ref=dst[a].at[c], send_sem=send_sems.at[a], recv_sem=recv_sems.at[a],
                device_id=(x, y, 1 - c), device_id_type=MESH)
            cp.start()
            copies.append(cp)
        for a in range(n):
            other = dst[a].at[1 - c]
            pltpu.make_async_remote_copy(
                src_ref=other, dst_ref=other, send_sem=send_sems.at[a], recv_sem=recv_sems.at[a],
                device_id=(x, y, 1 - c), device_id_type=MESH).wait_recv()
        for a in range(n):
            copies[2 * a].wait()
            copies[2 * a + 1].wait_send()

    return pl.pallas_call(
        body, name="share_with_sibling",
        in_specs=[ANY] * n, out_specs=[ANY] * n,
        out_shape=[jax.ShapeDtypeStruct((2,) + r.shape, r.dtype) for r in reduced],
        scratch_shapes=[pltpu.SemaphoreType.DMA((n,)), pltpu.SemaphoreType.DMA((n,)), pltpu.SemaphoreType.DMA((n,))],
    )(*reduced)


def _adam_math(w, g, m, v):
    m = ADAM_B1 * m + (1.0 - ADAM_B1) * g
    v = ADAM_B2 * v + (1.0 - ADAM_B2) * (g * g)
    m_hat = m / (1.0 - ADAM_B1 ** ADAM_STEP)
    v_hat = v / (1.0 - ADAM_B2 ** ADAM_STEP)
    delta = -ADAM_LR * (m_hat / (jnp.sqrt(v_hat) + ADAM_EPS) + ADAM_WD * w)
    return delta, m, v


def _small_allreduce_adamw(small, w_vec, m_vec, v_vec):
    width = small.shape[1]
    n_par = w_vec.shape[1]

    def body(s_ref, w_ref, m_ref, v_ref, loss_ref, g_ref, d_ref, nm_ref, nv_ref, gather_ref, send_sems, recv_sems):
        x, y, c = lax.axis_index("x"), lax.axis_index("y"), lax.axis_index("c")
        me = 4 * x + 2 * y + c
        gather_ref[me] = s_ref[...]
        copies = []
        for r in range(1, 8):
            bx, by, bc = (r >> 2) & 1, (r >> 1) & 1, r & 1
            peer = (x ^ bx, y ^ by, c ^ bc)
            cp = pltpu.make_async_remote_copy(
                src_ref=s_ref, dst_ref=gather_ref.at[me], send_sem=send_sems.at[r - 1], recv_sem=recv_sems.at[r - 1],
                device_id=peer, device_id_type=MESH)
            cp.start()
            copies.append(cp)
        for r in range(1, 8):
            bx, by, bc = (r >> 2) & 1, (r >> 1) & 1, r & 1
            theirs = gather_ref.at[4 * (x ^ bx) + 2 * (y ^ by) + (c ^ bc)]
            pltpu.make_async_remote_copy(
                src_ref=theirs, dst_ref=theirs, send_sem=send_sems.at[r - 1], recv_sem=recv_sems.at[r - 1],
                device_id=(x ^ bx, y ^ by, c ^ bc), device_id_type=MESH).wait_recv()
        for cp in copies:
            cp.wait_send()
        tot = gather_ref[0]
        for d in range(1, 8):
            tot = tot + gather_ref[d]
        tot = jnp.sum(tot, axis=0, keepdims=True)
        sq = jnp.sum(tot[:, n_par:], axis=1, keepdims=True)
        loss_ref[...] = jnp.broadcast_to(sq * (0.5 / D_MODEL), loss_ref.shape)
        g = tot[:, :n_par]
        g_ref[...] = g
        d_ref[...], nm_ref[...], nv_ref[...] = _adam_math(w_ref[...], g, m_ref[...], v_ref[...])

    vm = pl.BlockSpec(memory_space=pltpu.VMEM)
    vec = jax.ShapeDtypeStruct((1, n_par), F32)
    return pl.pallas_call(
        body, name="small_allreduce_adamw",
        in_specs=[vm] * 4, out_specs=[vm] * 5,
        out_shape=[jax.ShapeDtypeStruct((1, 128), F32), vec, vec, vec, vec],
        scratch_shapes=[pltpu.VMEM((8, 8, width), F32), pltpu.SemaphoreType.DMA((7,)), pltpu.SemaphoreType.DMA((7,))],
    )(small, w_vec, m_vec, v_vec)


def _adamw(w, g, m, v, name):
    r, c = w.shape
    tr = min(r, 256)

    def body(w_ref, g_ref, m_ref, v_ref, d_ref, nm_ref, nv_ref):
        d_ref[...], nm_ref[...], nv_ref[...] = _adam_math(w_ref[...], g_ref[...], m_ref[...], v_ref[...])

    spec = pl.BlockSpec((tr, c), lambda i: (i, 0))
    shape = jax.ShapeDtypeStruct((r, c), F32)
    return pl.pallas_call(
        body, name=name, grid=(r // tr,),
        in_specs=[spec] * 4, out_specs=[spec] * 3, out_shape=[shape] * 3,
        compiler_params=_params(("parallel",), 32),
    )(w, g, m, v)


def kernel(x, w_in, w_pool, pool_scale, w_out, ln_gain, ln_bias, loss_target, m_w_in, m_w_pool, m_pool_scale, m_w_out, m_ln_gain, m_ln_bias, v_w_in, v_w_pool, v_pool_scale, v_w_out, v_ln_gain, v_ln_bias):
    xi, yi, ci = lax.axis_index("x"), lax.axis_index("y"), lax.axis_index("c")
    chip = (2 * xi + yi).astype(jnp.int32).reshape(1)
    core = ci.astype(jnp.int32).reshape(1)
    n_groups = len(POOL_WINDOWS)
    shard_c = w_pool.shape[2]

    w_in_b = _cast_bf16(w_in[0], "cast_w_in", 256)
    w_out_b = _cast_bf16(w_out[0], "cast_w_out", 256)
    w_pool_b = _cast_bf16(w_pool[0].reshape(n_groups * shard_c, POOL_GROUP_DIM), "cast_w_pool", 256)
    w_in_g, w_out_g, w_pool_sh = _allgather_weights([w_in_b, w_out_b, w_pool_b])
    w_pool_g = (w_pool_sh.reshape(N_SHARDS, n_groups, shard_c, POOL_GROUP_DIM).transpose(1, 0, 2, 3)
                .reshape(n_groups, POOL_GROUP_DIM, POOL_GROUP_DIM))

    g_x, g_w_in, g_w_out, g_w_pool, small = _local_step(
        x[0], loss_target[0], w_in_g, w_out_g, w_pool_g, pool_scale, ln_gain, ln_bias)

    half_c = shard_c // 2
    g_w_out_p = g_w_out.reshape(N_SHARDS, 2, D_MODEL // (2 * N_SHARDS), D_MODEL)
    g_w_pool_p = (g_w_pool.reshape(n_groups, N_SHARDS, 2, half_c, POOL_GROUP_DIM).transpose(1, 2, 0, 3, 4)
                  .reshape(N_SHARDS, 2, n_groups * half_c, POOL_GROUP_DIM))
    grads = [g_w_in, g_w_out_p, g_w_pool_p]
    recv = _exchange_halves(grads)
    sums = [_add_own_half(g, r, core, "add_own_half_%d" % a) for a, (g, r) in enumerate(zip(grads, recv))]
    recv = _scatter_to_chips(sums)
    reduced = [_add_chips(s, r, chip, "add_chips_%d" % a) for a, (s, r) in enumerate(zip(sums, recv))]
    full = _share_with_sibling(reduced)
    grad_w_in = full[0].reshape(D_MODEL, SHARD_IN)
    grad_w_out = full[1].reshape(D_MODEL // N_SHARDS, D_MODEL)
    grad_w_pool = (full[2].reshape(2, n_groups, half_c, POOL_GROUP_DIM).transpose(1, 0, 2, 3)
                   .reshape(n_groups * shard_c, POOL_GROUP_DIM))

    d_in, nm_in, nv_in = _adamw(w_in[0], grad_w_in, m_w_in[0], v_w_in[0], "adamw_w_in")
    d_out, nm_out, nv_out = _adamw(w_out[0], grad_w_out, m_w_out[0], v_w_out[0], "adamw_w_out")
    flat = lambda t: t[0].reshape(n_groups * shard_c, POOL_GROUP_DIM)
    d_pool, nm_pool, nv_pool = _adamw(flat(w_pool), grad_w_pool, flat(m_w_pool), flat(v_w_pool), "adamw_w_pool")

    cat = lambda a, b, c: jnp.concatenate([a, b, c], axis=1)
    loss_v, g_vec, d_vec, nm_vec, nv_vec = _small_allreduce_adamw(
        small, cat(pool_scale, ln_gain, ln_bias), cat(m_pool_scale, m_ln_gain, m_ln_bias),
        cat(v_pool_scale, v_ln_gain, v_ln_bias))

    def split(vec):
        return vec[:, :D_POOL], vec[:, D_POOL:D_POOL + D_MODEL], vec[:, D_POOL + D_MODEL:]

    g_scale, g_gain, g_bias = split(g_vec)
    d_scale, d_gain, d_bias = split(d_vec)
    nm_scale, nm_gain, nm_bias = split(nm_vec)
    nv_scale, nv_gain, nv_bias = split(nv_vec)
    pool_shape = w_pool.shape
    return (loss_v[0, 0], g_x[None],
            grad_w_in[None], grad_w_pool.reshape(pool_shape), g_scale, grad_w_out[None], g_gain, g_bias,
            d_in[None], d_pool.reshape(pool_shape), d_scale, d_out[None], d_gain, d_bias,
            nm_in[None], nm_pool.reshape(pool_shape), nm_scale, nm_out[None], nm_gain, nm_bias,
            nv_in[None], nv_pool.reshape(pool_shape), nv_scale, nv_out[None], nv_gain, nv_bias)
```

```python
import functools

import jax
import jax.numpy as jnp
from jax import lax
from jax.experimental import pallas as pl
from jax.experimental.pallas import tpu as pltpu

F32 = jnp.float32
BF16 = jnp.bfloat16
MESH = pl.DeviceIdType.MESH
ANY = pl.BlockSpec(memory_space=pl.ANY)

D_MODEL = 2048
D_ATTN = 1024
D_POOL = 1024
HEAD_DIM = 128
N_HEADS = 8
ROPE_DIM = 32
ROPE_THETA = 500000.0
DILATIONS = (1, 4, 16)
KEY_BLOCK = 128
CHUNK = 2 * KEY_BLOCK
STAT_LANES = 128
POOL_WINDOWS = (2, 4, 8, 16)
POOL_GROUP_DIM = 256
POOL_HALO = 16
D_QKV = 3 * D_ATTN
D_UG = D_POOL + D_MODEL
D_IN = D_QKV + D_UG
N_SHARDS = 4
SHARD_IN = D_IN // N_SHARDS
LN_EPS = 1e-5
DEEPNORM_ALPHA = 2.0 ** 0.25
ADAM_LR = 0.001
ADAM_B1 = 0.9
ADAM_B2 = 0.999
ADAM_EPS = 1e-08
ADAM_WD = 0.01
ADAM_STEP = 10
NEG = -1e30
MIB = 1024 * 1024


def _params(sem, vmem_mib):
    return pltpu.CompilerParams(dimension_semantics=sem, vmem_limit_bytes=vmem_mib * MIB)


def _dot_nn(a, b):
    return jnp.dot(a, b, preferred_element_type=F32)


def _dot_nt(a, b):
    return lax.dot_general(a, b, (((1,), (1,)), ((), ())), preferred_element_type=F32)


def _dot_tn(a, b):
    return lax.dot_general(a, b, (((0,), (0,)), ((), ())), preferred_element_type=F32)


def _fold_rows(a):
    r, c = a.shape
    return jnp.sum(a.reshape(r // 8, 8, c), axis=0)


def _cast_bf16(a, name, rows):
    r, c = a.shape

    def body(a_ref, o_ref):
        o_ref[...] = a_ref[...].astype(BF16)

    return pl.pallas_call(
        body, name=name, grid=(r // rows,),
        in_specs=[pl.BlockSpec((rows, c), lambda i: (i, 0))],
        out_specs=pl.BlockSpec((rows, c), lambda i: (i, 0)),
        out_shape=jax.ShapeDtypeStruct((r, c), BF16),
        compiler_params=_params(("parallel",), 32),
    )(a)


def _allgather_weights(shards):
    n = len(shards)

    def body(*refs):
        src = refs[:n]
        dst = refs[n:2 * n]
        local_sems, ici_send, ici_recv, d2d_send, d2d_recv = refs[2 * n:]
        x, y, c = lax.axis_index("x"), lax.axis_index("y"), lax.axis_index("c")
        mine = 2 * x + y
        chips = [(1 - x, y), (x, 1 - y), (1 - x, 1 - y)]

        def half(a, core):
            rows = shards[a].shape[0] // 2
            return pl.ds(core * rows, rows)

        locals_ = []
        sends = []
        for a in range(n):
            cp = pltpu.make_async_copy(src[a], dst[a].at[mine], local_sems.at[a])
            cp.start()
            locals_.append(cp)
            for k, (cx, cy) in enumerate(chips):
                cp = pltpu.make_async_remote_copy(
                    src_ref=src[a].at[half(a, c)], dst_ref=dst[a].at[mine, half(a, c)],
                    send_sem=ici_send.at[a, k], recv_sem=ici_recv.at[a, k],
                    device_id=(cx, cy, c), device_id_type=MESH)
                cp.start()
                sends.append(cp)
        for k, (cx, cy) in enumerate(chips):
            theirs = 2 * cx + cy
            for a in range(n):
                landed = dst[a].at[theirs, half(a, c)]
                pltpu.make_async_remote_copy(
                    src_ref=landed, dst_ref=landed, send_sem=ici_send.at[a, k], recv_sem=ici_recv.at[a, k],
                    device_id=(cx, cy, c), device_id_type=MESH).wait_recv()
                cp = pltpu.make_async_remote_copy(
                    src_ref=landed, dst_ref=landed, send_sem=d2d_send.at[a, k], recv_sem=d2d_recv.at[a, k],
                    device_id=(x, y, 1 - c), device_id_type=MESH)
                cp.start()
                sends.append(cp)
        for k, (cx, cy) in enumerate(chips):
            theirs = 2 * cx + cy
            for a in range(n):
                passed = dst[a].at[theirs, half(a, 1 - c)]
                pltpu.make_async_remote_copy(
                    src_ref=passed, dst_ref=passed, send_sem=d2d_send.at[a, k], recv_sem=d2d_recv.at[a, k],
                    device_id=(x, y, 1 - c), device_id_type=MESH).wait_recv()
        for cp in sends:
            cp.wait_send()
        for cp in locals_:
            cp.wait()

    return pl.pallas_call(
        body, name="allgather_weights",
        in_specs=[ANY] * n, out_specs=[ANY] * n,
        out_shape=[jax.ShapeDtypeStruct((N_SHARDS,) + s.shape, s.dtype) for s in shards],
        scratch_shapes=[pltpu.SemaphoreType.DMA((n,)), pltpu.SemaphoreType.DMA((n, 3)), pltpu.SemaphoreType.DMA((n, 3)),
                        pltpu.SemaphoreType.DMA((n, 3)), pltpu.SemaphoreType.DMA((n, 3))],
    )(*shards)


def _rope_tables(seq):
    half = ROPE_DIM // 2
    inv_freq = ROPE_THETA ** (-(2.0 * jnp.arange(half, dtype=F32)) / ROPE_DIM)
    ang = jnp.arange(seq, dtype=jnp.int32).astype(F32)[:, None] * inv_freq[None, :]
    cos, sin = jnp.cos(ang), jnp.sin(ang)
    pad = jnp.zeros((seq, HEAD_DIM - ROPE_DIM), F32)
    zeros = jnp.zeros((seq, half), F32)
    c_tab = jnp.concatenate([cos, cos, pad + 1.0], axis=1)
    up_tab = jnp.concatenate([-sin, zeros, pad], axis=1)
    down_tab = jnp.concatenate([zeros, sin, pad], axis=1)
    return c_tab, up_tab, down_tab


def _rotate_heads(t, c_tab, up_tab, down_tab):
    outs = []
    for h in range(t.shape[1] // HEAD_DIM):
        th = t[:, h * HEAD_DIM:(h + 1) * HEAD_DIM]
        up = pltpu.roll(th, HEAD_DIM - ROPE_DIM // 2, axis=1)
        down = pltpu.roll(th, ROPE_DIM // 2, axis=1)
        outs.append(th * c_tab + up * up_tab + down * down_tab)
    return outs[0] if len(outs) == 1 else jnp.concatenate(outs, axis=1)


def _to_pattern(slabs_ref, dst_ref, dil, dtype):
    n_slabs, rows, _ = slabs_ref.shape
    for s in range(n_slabs):
        for r in range(dil):
            dst_ref[r, :, s * 128:(s + 1) * 128] = slabs_ref[s, pl.ds(r, rows // dil, dil), :].astype(dtype)


def _from_pattern(src_ref, slabs_ref, dil):
    n_slabs, rows, _ = slabs_ref.shape
    for s in range(n_slabs):
        for r in range(dil):
            slabs_ref[s, pl.ds(r, rows // dil, dil), :] = src_ref[r, :, s * 128:(s + 1) * 128]


def _store_slabs(slabs_ref, value):
    for s in range(slabs_ref.shape[0]):
        slabs_ref[s] = value[:, s * 128:(s + 1) * 128]


def _in_proj_qkv(x, w_in_g, tabs):
    seq = x.shape[0]
    tm, tn = 512, 512
    per_shard = SHARD_IN // tn
    n_tiles = D_QKV // tn
    n_rot_tiles = 2 * D_ATTN // tn
    d4, d16 = DILATIONS[1], DILATIONS[2]

    def body(x_ref, w_ref, c_ref, up_ref, down_ref, o1_ref, o4_ref, o16_ref, xb_ref, res_ref):
        j = pl.program_id(1)

        @pl.when(j == 0)
        def _():
            xb_ref[...] = x_ref[...].astype(BF16)

        acc = _dot_nn(xb_ref[...], w_ref[...])

        @pl.when(j < n_rot_tiles)
        def _():
            res = _rotate_heads(acc, c_ref[...], up_ref[...], down_ref[...])
            o1_ref[...] = res.astype(BF16)
            _store_slabs(res_ref, res)

        @pl.when(j >= n_rot_tiles)
        def _():
            o1_ref[...] = acc.astype(BF16)
            _store_slabs(res_ref, acc)

        _to_pattern(res_ref, o4_ref, d4, BF16)
        _to_pattern(res_ref, o16_ref, d16, BF16)

    tab_spec = pl.BlockSpec((tm, HEAD_DIM), lambda i, j: (i, 0))
    o1, o4, o16 = pl.pallas_call(
        body, name="in_proj_qkv", grid=(seq // tm, n_tiles),
        in_specs=[pl.BlockSpec((tm, D_MODEL), lambda i, j: (i, 0)),
                  pl.BlockSpec((None, D_MODEL, tn), lambda i, j: (j // per_shard, 0, j % per_shard)),
                  tab_spec, tab_spec, tab_spec],
        out_specs=[pl.BlockSpec((tm, tn), lambda i, j: (i, j)),
                   pl.BlockSpec((d4, tm // d4, tn), lambda i, j: (0, i, j)),
                   pl.BlockSpec((d16, tm // d16, tn), lambda i, j: (0, i, j))],
        out_shape=[jax.ShapeDtypeStruct((seq, D_QKV), BF16),
                   jax.ShapeDtypeStruct((d4, seq // d4, D_QKV), BF16),
                   jax.ShapeDtypeStruct((d16, seq // d16, D_QKV), BF16)],
        scratch_shapes=[pltpu.VMEM((tm, D_MODEL), BF16), pltpu.VMEM((tn // 128, tm, 128), F32)],
        compiler_params=_params(("parallel", "arbitrary"), 40),
    )(x, w_in_g, *tabs)
    return [o1[None], o4, o16]


def _in_proj_pool_gate(x, w_in_g):
    seq = x.shape[0]
    tm, tn = 512, 512
    per_shard = SHARD_IN // tn
    first_tile = D_QKV // tn

    def body(x_ref, w_ref, o_ref, xb_ref):
        @pl.when(pl.program_id(1) == 0)
        def _():
            xb_ref[...] = x_ref[...].astype(BF16)

        o_ref[...] = _dot_nn(xb_ref[...], w_ref[...])

    return pl.pallas_call(
        body, name="in_proj_pool_gate", grid=(seq // tm, D_UG // tn),
        in_specs=[pl.BlockSpec((tm, D_MODEL), lambda i, j: (i, 0)),
                  pl.BlockSpec((None, D_MODEL, tn),
                               lambda i, j: ((j + first_tile) // per_shard, 0, (j + first_tile) % per_shard))],
        out_specs=pl.BlockSpec((tm, tn), lambda i, j: (i, j)),
        out_shape=jax.ShapeDtypeStruct((seq, D_UG), F32),
        scratch_shapes=[pltpu.VMEM((tm, D_MODEL), BF16)],
        compiler_params=_params(("parallel", "arbitrary"), 40),
    )(x, w_in_g)


def _band_masks():
    row = lax.broadcasted_iota(jnp.int32, (KEY_BLOCK, KEY_BLOCK), 0)
    col = lax.broadcasted_iota(jnp.int32, (KEY_BLOCK, KEY_BLOCK), 1)
    return col <= row, col >= row


def _attn_fwd(qkv, name):
    dil, n, _ = qkv.shape
    scale = HEAD_DIM ** -0.5
    lo, hi = slice(0, KEY_BLOCK), slice(KEY_BLOCK, CHUNK)

    def body(q_ref, k_ref, v_ref, kb_ref, vb_ref, o_ref, st_ref):
        i = pl.program_id(1)
        cur_mask, prev_mask = _band_masks()
        before_mask = jnp.logical_and(prev_mask, i > 0)
        lane = lax.broadcasted_iota(jnp.int32, (KEY_BLOCK, STAT_LANES), 1)
        for rows in (lo, hi):
            stats = jnp.zeros((KEY_BLOCK, STAT_LANES), F32)
            for h in range(N_HEADS):
                cols = slice(h * HEAD_DIM, (h + 1) * HEAD_DIM)
                q = q_ref[rows, cols]
                if rows is lo:
                    k_prev, v_prev, mask = kb_ref[:, cols], vb_ref[:, cols], before_mask
                else:
                    k_prev, v_prev, mask = k_ref[lo, cols], v_ref[lo, cols], prev_mask
                s_prev = jnp.where(mask, _dot_nt(q, k_prev) * scale, NEG)
                s_cur = jnp.where(cur_mask, _dot_nt(q, k_ref[rows, cols]) * scale, NEG)
                m = jnp.maximum(jnp.max(s_cur, axis=-1, keepdims=True), jnp.max(s_prev, axis=-1, keepdims=True))
                p_cur = jnp.exp(s_cur - m)
                p_prev = jnp.exp(s_prev - m)
                den = jnp.sum(p_cur, axis=-1, keepdims=True) + jnp.sum(p_prev, axis=-1, keepdims=True)
                o = _dot_nn(p_cur.astype(BF16), v_ref[rows, cols]) + _dot_nn(p_prev.astype(BF16), v_prev)
                o_ref[rows, cols] = o / den
                stats = jnp.where(lane == h, m + jnp.log(den), stats)
            st_ref[rows, :] = stats

    main = lambda cb: pl.BlockSpec((None, CHUNK, D_ATTN), lambda r, i: (r, i, cb))
    before = lambda cb: pl.BlockSpec((None, KEY_BLOCK, D_ATTN), lambda r, i: (r, jnp.maximum(2 * i - 1, 0), cb))
    return pl.pallas_call(
        body, name=name, grid=(dil, n // CHUNK),
        in_specs=[main(0), main(1), main(2), before(1), before(2)],
        out_specs=[main(0), pl.BlockSpec((None, CHUNK, STAT_LANES), lambda r, i: (r, i, 0))],
        out_shape=[jax.ShapeDtypeStruct((dil, n, D_ATTN), F32), jax.ShapeDtypeStruct((dil, n, STAT_LANES), F32)],
        compiler_params=_params(("parallel", "parallel"), 40),
    )(qkv, qkv, qkv, qkv, qkv)


def _attn_bwd(qkv, do, stats, name):
    dil, n, _ = qkv.shape
    n_blocks = n // KEY_BLOCK
    last = n // CHUNK - 1
    scale = HEAD_DIM ** -0.5
    lo, hi = slice(0, KEY_BLOCK), slice(KEY_BLOCK, CHUNK)

    def body(q_ref, k_ref, v_ref, kb_ref, vb_ref, qa_ref, do_ref, doa_ref, st_ref, sta_ref, dq_ref, dk_ref, dv_ref):
        i = pl.program_id(1)
        cur_mask, prev_mask = _band_masks()
        before_mask = jnp.logical_and(prev_mask, i > 0)
        after_mask = jnp.logical_and(prev_mask, i < last)

        def pair(q, k, v, do_b, lse, delta, mask):
            p = jnp.exp(jnp.where(mask, _dot_nt(q, k) * scale, NEG) - lse)
            ds = p * (_dot_nt(do_b, v) - delta) * scale
            return p.astype(BF16), ds.astype(BF16)

        for h in range(N_HEADS):
            cols = slice(h * HEAD_DIM, (h + 1) * HEAD_DIM)
            lse_c, del_c = slice(h, h + 1), slice(N_HEADS + h, N_HEADS + h + 1)
            q0, q1, qa = q_ref[lo, cols], q_ref[hi, cols], qa_ref[:, cols]
            k0, k1, kb = k_ref[lo, cols], k_ref[hi, cols], kb_ref[:, cols]
            v0, v1, vb = v_ref[lo, cols], v_ref[hi, cols], vb_ref[:, cols]
            do0, do1, doa = do_ref[lo, cols], do_ref[hi, cols], doa_ref[:, cols]
            st0 = (st_ref[lo, lse_c], st_ref[lo, del_c])
            st1 = (st_ref[hi, lse_c], st_ref[hi, del_c])
            sta = (sta_ref[:, lse_c], sta_ref[:, del_c])
            _, ds_0b = pair(q0, kb, vb, do0, *st0, before_mask)
            p_00, ds_00 = pair(q0, k0, v0, do0, *st0, cur_mask)
            p_10, ds_10 = pair(q1, k0, v0, do1, *st1, prev_mask)
            p_11, ds_11 = pair(q1, k1, v1, do1, *st1, cur_mask)
            p_a1, ds_a1 = pair(qa, k1, v1, doa, *sta, after_mask)
            dq_ref[lo, cols] = _dot_nn(ds_0b, kb) + _dot_nn(ds_00, k0)
            dq_ref[hi, cols] = _dot_nn(ds_10, k0) + _dot_nn(ds_11, k1)
            dk_ref[lo, cols] = _dot_tn(ds_00, q0) + _dot_tn(ds_10, q1)
            dk_ref[hi, cols] = _dot_tn(ds_11, q1) + _dot_tn(ds_a1, qa)
            dv_ref[lo, cols] = _dot_tn(p_00, do0) + _dot_tn(p_10, do1)
            dv_ref[hi, cols] = _dot_tn(p_11, do1) + _dot_tn(p_a1, doa)

    def spec(rows, width, row_of, cb):
        return pl.BlockSpec((None, rows, width), lambda r, i: (r, row_of(i), cb))

    same = lambda i: i
    before = lambda i: jnp.maximum(2 * i - 1, 0)
    after = lambda i: jnp.minimum(2 * i + 2, n_blocks - 1)
    out = spec(CHUNK, D_ATTN, same, 0)
    return pl.pallas_call(
        body, name=name, grid=(dil, n // CHUNK),
        in_specs=[spec(CHUNK, D_ATTN, same, 0), spec(CHUNK, D_ATTN, same, 1), spec(CHUNK, D_ATTN, same, 2),
                  spec(KEY_BLOCK, D_ATTN, before, 1), spec(KEY_BLOCK, D_ATTN, before, 2),
                  spec(KEY_BLOCK, D_ATTN, after, 0),
                  spec(CHUNK, D_ATTN, same, 0), spec(KEY_BLOCK, D_ATTN, after, 0),
                  spec(CHUNK, STAT_LANES, same, 0), spec(KEY_BLOCK, STAT_LANES, after, 0)],
        out_specs=[out, out, out],
        out_shape=[jax.ShapeDtypeStruct((dil, n, D_ATTN), F32)] * 3,
        compiler_params=_params(("parallel", "parallel"), 40),
    )(qkv, qkv, qkv, qkv, qkv, qkv, do, do, stats, stats)


def _window_sums(ext, window, backward):
    rows = ext.shape[0]
    acc, span = ext, 1
    while span < window:
        acc = acc + pltpu.roll(acc, (rows - span) if backward else span, axis=0)
        span *= 2
    return acc


def _mix_gate(o_list, st_list, hug, w_pool_g, pool_scale):
    seq = hug.shape[0]
    tm = 256
    halo_blocks = tm // POOL_HALO
    d4, d16 = DILATIONS[1], DILATIONS[2]

    def body(o1_ref, o4_ref, o16_ref, l1_ref, l4_ref, l16_ref, u_ref, halo_ref, ga_ref, gp_ref, wp_ref, sc_ref,
             y_ref, mix_ref, lse_ref, pooled_ref, n4_ref, n16_ref, nl4_ref, nl16_ref):
        i = pl.program_id(0)
        _from_pattern(o4_ref, n4_ref, d4)
        _from_pattern(o16_ref, n16_ref, d16)
        _from_pattern(l4_ref, nl4_ref, d4)
        _from_pattern(l16_ref, nl16_ref, d16)
        la, lb, lc = l1_ref[...], nl4_ref[0], nl16_ref[0]
        mx = jnp.maximum(jnp.maximum(la, lb), lc)
        ea, eb, ec = jnp.exp(la - mx), jnp.exp(lb - mx), jnp.exp(lc - mx)
        tot = ea + eb + ec
        lse_ref[...] = mx + jnp.log(tot)
        wa, wb, wc = ea / tot, eb / tot, ec / tot
        ga = ga_ref[...]
        silu_a = ga * jax.nn.sigmoid(ga)
        for h in range(N_HEADS):
            cols = slice(h * HEAD_DIM, (h + 1) * HEAD_DIM)
            hc = slice(h, h + 1)
            attn = wa[:, hc] * o1_ref[:, cols] + wb[:, hc] * n4_ref[h] + wc[:, hc] * n16_ref[h]
            mix_ref[:, cols] = attn
            y_ref[:, cols] = (attn * silu_a[:, cols]).astype(BF16)

        u = u_ref[...]
        halo = jnp.where(i > 0, halo_ref[...], 0.0)
        ext = jnp.concatenate([halo, u], axis=0)
        pos = i * tm + lax.broadcasted_iota(jnp.int32, (tm, 1), 0)
        gp = gp_ref[...]
        gated_scale = sc_ref[...] * (gp * jax.nn.sigmoid(gp))
        for g, window in enumerate(POOL_WINDOWS):
            cols = slice(g * POOL_GROUP_DIM, (g + 1) * POOL_GROUP_DIM)
            sums = _window_sums(ext[:, cols], window, backward=False)[POOL_HALO:, :]
            count = jnp.minimum(pos + 1, window).astype(F32)
            pooled = (sums / count - u[:, cols]).astype(BF16)
            pooled_ref[:, cols] = pooled
            pre = _dot_nn(pooled, wp_ref[g])
            out_cols = slice(D_ATTN + g * POOL_GROUP_DIM, D_ATTN + (g + 1) * POOL_GROUP_DIM)
            mix_ref[:, out_cols] = pre
            y_ref[:, out_cols] = (pre * gated_scale[:, cols]).astype(BF16)

    row = lambda width, cb=0: pl.BlockSpec((tm, width), lambda i: (i, cb))
    pat = lambda d, width: pl.BlockSpec((d, tm // d, width), lambda i: (0, i, 0))
    return pl.pallas_call(
        body, name="mix_gate", grid=(seq // tm,),
        in_specs=[row(D_ATTN), pat(d4, D_ATTN), pat(d16, D_ATTN),
                  row(STAT_LANES), pat(d4, STAT_LANES), pat(d16, STAT_LANES),
                  row(D_POOL),
                  pl.BlockSpec((POOL_HALO, D_POOL), lambda i: (jnp.maximum(i * halo_blocks - 1, 0), 0)),
                  row(D_ATTN, 1), row(D_POOL, 2),
                  pl.BlockSpec((len(POOL_WINDOWS), POOL_GROUP_DIM, POOL_GROUP_DIM), lambda i: (0, 0, 0)),
                  pl.BlockSpec((1, D_POOL), lambda i: (0, 0))],
        out_specs=[row(D_MODEL), row(D_MODEL), row(STAT_LANES), row(D_POOL)],
        out_shape=[jax.ShapeDtypeStruct((seq, D_MODEL), BF16), jax.ShapeDtypeStruct((seq, D_MODEL), F32),
                   jax.ShapeDtypeStruct((seq, STAT_LANES), F32), jax.ShapeDtypeStruct((seq, D_POOL), BF16)],
        scratch_shapes=[pltpu.VMEM((N_HEADS, tm, HEAD_DIM), F32), pltpu.VMEM((N_HEADS, tm, HEAD_DIM), F32),
                        pltpu.VMEM((1, tm, STAT_LANES), F32), pltpu.VMEM((1, tm, STAT_LANES), F32)],
        compiler_params=_params(("parallel",), 48),
    )(o_list[0][0], o_list[1], o_list[2], st_list[0][0], st_list[1], st_list[2],
      hug, hug, hug, hug, w_pool_g, pool_scale)


def _out_proj_loss(y, w_out_g, x, target, gain, bias):
    seq = x.shape[0]
    tm = 256

    def body(y_ref, w_ref, x_ref, t_ref, g_ref, b_ref, dz_ref, dzb_ref, gg_ref, gb_ref, loss_ref):
        @pl.when(pl.program_id(0) == 0)
        def _():
            gg_ref[...] = jnp.zeros_like(gg_ref)
            gb_ref[...] = jnp.zeros_like(gb_ref)
            loss_ref[...] = jnp.zeros_like(loss_ref)

        z = DEEPNORM_ALPHA * x_ref[...] + _dot_nn(y_ref[...], w_ref[...])
        mu = jnp.mean(z, axis=-1, keepdims=True)
        zc = z - mu
        rstd = lax.rsqrt(jnp.mean(zc * zc, axis=-1, keepdims=True) + LN_EPS)
        xhat = zc * rstd
        gain_v = g_ref[...]
        diff = xhat * gain_v + b_ref[...] - t_ref[...]
        sq = _fold_rows(diff * diff)
        part = sq[:, :128]
        for k in range(1, D_MODEL // 128):
            part = part + sq[:, k * 128:(k + 1) * 128]
        loss_ref[...] += part
        dln = diff * (1.0 / D_MODEL)
        gg_ref[...] += _fold_rows(dln * xhat)
        gb_ref[...] += _fold_rows(dln)
        dxhat = dln * gain_v
        dz = rstd * (dxhat - jnp.mean(dxhat, axis=-1, keepdims=True)
                     - xhat * jnp.mean(dxhat * xhat, axis=-1, keepdims=True))
        dz_ref[...] = dz
        dzb_ref[...] = dz.astype(BF16)

    row = lambda: pl.BlockSpec((tm, D_MODEL), lambda i: (i, 0))
    vec = lambda: pl.BlockSpec((1, D_MODEL), lambda i: (0, 0))
    acc = lambda width: pl.BlockSpec((8, width), lambda i: (0, 0))
    return pl.pallas_call(
        body, name="out_proj_loss", grid=(seq // tm,),
        in_specs=[row(), pl.BlockSpec((D_MODEL, D_MODEL), lambda i: (0, 0)), row(), row(), vec(), vec()],
        out_specs=[row(), row(), acc(D_MODEL), acc(D_MODEL), acc(128)],
        out_shape=[jax.ShapeDtypeStruct((seq, D_MODEL), F32), jax.ShapeDtypeStruct((seq, D_MODEL), BF16),
                   jax.ShapeDtypeStruct((8, D_MODEL), F32), jax.ShapeDtypeStruct((8, D_MODEL), F32),
                   jax.ShapeDtypeStruct((8, 128), F32)],
        compiler_params=_params(("arbitrary",), 48),
    )(y, w_out_g.reshape(D_MODEL, D_MODEL), x, target, gain, bias)


def _dy_gate_bwd(dzb, w_out_g, hug, mixpre, pool_scale, lse_all):
    seq = dzb.shape[0]
    tm = 256
    d4, d16 = DILATIONS[1], DILATIONS[2]

    def body(dz_ref, w_ref, ga_ref, gp_ref, mix_ref, sc_ref, lse_ref,
             dh_ref, dpo_ref, do1_ref, do4_ref, do16_ref, st1_ref, st4_ref, st16_ref, da_ref, st_ref):
        dy = _dot_nt(dz_ref[...], w_ref[...])
        ga = ga_ref[...]
        sig = jax.nn.sigmoid(ga)
        attn = mix_ref[:, :D_ATTN]
        dya = dy[:, :D_ATTN]
        dattn = dya * (ga * sig)
        dh_ref[:, :D_ATTN] = (dya * attn * (sig * (1.0 + ga * (1.0 - sig)))).astype(BF16)
        _store_slabs(da_ref, dattn)
        lane = lax.broadcasted_iota(jnp.int32, (tm, STAT_LANES), 1)
        stats = lse_ref[...]
        prod = dattn * attn
        for h in range(N_HEADS):
            delta = jnp.sum(prod[:, h * HEAD_DIM:(h + 1) * HEAD_DIM], axis=-1, keepdims=True)
            stats = jnp.where(lane == N_HEADS + h, delta, stats)
        st_ref[0] = stats
        do1_ref[...] = dattn.astype(BF16)
        st1_ref[...] = stats
        _to_pattern(da_ref, do4_ref, d4, BF16)
        _to_pattern(da_ref, do16_ref, d16, BF16)
        _to_pattern(st_ref, st4_ref, d4, F32)
        _to_pattern(st_ref, st16_ref, d16, F32)

        gp = gp_ref[...]
        sig = jax.nn.sigmoid(gp)
        dyp = dy[:, D_ATTN:]
        dpo_ref[...] = dyp * (gp * sig)
        dh_ref[:, D_ATTN:] = (dyp * (mix_ref[:, D_ATTN:] * sc_ref[...])
                              * (sig * (1.0 + gp * (1.0 - sig)))).astype(BF16)

    row = lambda width, cb=0: pl.BlockSpec((tm, width), lambda i: (i, cb))
    pat = lambda d, width: pl.BlockSpec((d, tm // d, width), lambda i: (0, i, 0))
    pat_shape = lambda d, width, dtype: jax.ShapeDtypeStruct((d, seq // d, width), dtype)
    outs = pl.pallas_call(
        body, name="dy_gate_bwd", grid=(seq // tm,),
        in_specs=[row(D_MODEL), pl.BlockSpec((D_MODEL, D_MODEL), lambda i: (0, 0)),
                  row(D_ATTN, 1), row(D_POOL, 2), row(D_MODEL), pl.BlockSpec((1, D_POOL), lambda i: (0, 0)),
                  row(STAT_LANES)],
        out_specs=[row(D_MODEL, D_IN // D_MODEL - 1), row(D_POOL),
                   row(D_ATTN), pat(d4, D_ATTN), pat(d16, D_ATTN),
                   row(STAT_LANES), pat(d4, STAT_LANES), pat(d16, STAT_LANES)],
        out_shape=[jax.ShapeDtypeStruct((seq, D_IN), BF16), jax.ShapeDtypeStruct((seq, D_POOL), F32),
                   jax.ShapeDtypeStruct((seq, D_ATTN), BF16), pat_shape(d4, D_ATTN, BF16), pat_shape(d16, D_ATTN, BF16),
                   jax.ShapeDtypeStruct((seq, STAT_LANES), F32), pat_shape(d4, STAT_LANES, F32),
                   pat_shape(d16, STAT_LANES, F32)],
        scratch_shapes=[pltpu.VMEM((N_HEADS, tm, HEAD_DIM), F32), pltpu.VMEM((1, tm, STAT_LANES), F32)],
        compiler_params=_params(("parallel",), 48),
    )(dzb, w_out_g.reshape(D_MODEL, D_MODEL), hug, hug, mixpre, pool_scale, lse_all)
    dh, dpo, do1, do4, do16, st1, st4, st16 = outs
    return dh, dpo, [do1[None], do4, do16], [st1[None], st4, st16]


def _pool_bwd(dh, dpo, mixpre, pooled, w_pool_g, pool_scale):
    seq = dpo.shape[0]
    tm = 256
    halo_blocks = tm // POOL_HALO
    last = seq // tm - 1
    n_groups = len(POOL_WINDOWS)

    def body(dh_in_ref, dpo_ref, halo_ref, pre_ref, pooled_ref, wp_ref, sc_ref, du_ref, gw_ref, gs_ref):
        i = pl.program_id(0)

        @pl.when(i == 0)
        def _():
            gw_ref[...] = jnp.zeros_like(gw_ref)
            gs_ref[...] = jnp.zeros_like(gs_ref)

        dpo = dpo_ref[...]
        scale = sc_ref[...]
        gs_ref[...] += _fold_rows(dpo * pre_ref[...])
        halo = jnp.where(i < last, halo_ref[...], 0.0)
        dpw = (jnp.concatenate([dpo, halo], axis=0) * scale).astype(BF16)
        pos = i * tm + lax.broadcasted_iota(jnp.int32, (tm + POOL_HALO, 1), 0)
        for g, window in enumerate(POOL_WINDOWS):
            cols = slice(g * POOL_GROUP_DIM, (g + 1) * POOL_GROUP_DIM)
            dpw_g = dpw[:, cols]
            gw_ref[g] += _dot_tn(pooled_ref[:, cols], dpw_g[:tm, :])
            dpooled = _dot_nt(dpw_g, wp_ref[g])
            count = jnp.minimum(pos + 1, window).astype(F32)
            sums = _window_sums(dpooled / count, window, backward=True)
            du_ref[:, cols] = (sums[:tm, :] - dpooled[:tm, :]).astype(BF16)

    row = lambda width, cb=0: pl.BlockSpec((tm, width), lambda i: (i, cb))
    return pl.pallas_call(
        body, name="pool_bwd", grid=(seq // tm,),
        in_specs=[ANY, row(D_POOL),
                  pl.BlockSpec((POOL_HALO, D_POOL),
                               lambda i: (jnp.minimum((i + 1) * halo_blocks, seq // POOL_HALO - 1), 0)),
                  row(D_POOL, 1), row(D_POOL),
                  pl.BlockSpec((n_groups, POOL_GROUP_DIM, POOL_GROUP_DIM), lambda i: (0, 0, 0)),
                  pl.BlockSpec((1, D_POOL), lambda i: (0, 0))],
        out_specs=[row(D_POOL, D_QKV // D_POOL),
                   pl.BlockSpec((n_groups, POOL_GROUP_DIM, POOL_GROUP_DIM), lambda i: (0, 0, 0)),
                   pl.BlockSpec((8, D_POOL), lambda i: (0, 0))],
        out_shape=[jax.ShapeDtypeStruct(dh.shape, dh.dtype),
                   jax.ShapeDtypeStruct((n_groups, POOL_GROUP_DIM, POOL_GROUP_DIM), F32),
                   jax.ShapeDtypeStruct((8, D_POOL), F32)],
        input_output_aliases={0: 0},
        compiler_params=_params(("arbitrary",), 40),
    )(dh, dpo, dpo, mixpre, pooled, w_pool_g, pool_scale)


def _sum_patterns(dh, parts, tabs, unrotate, col_block, name):
    seq = dh.shape[0]
    tm, tn = 256, 512
    per = D_ATTN // tn
    d4, d16 = DILATIONS[1], DILATIONS[2]

    def body(dh_in_ref, a1_ref, a4_ref, a16_ref, ct_ref, up_ref, down_ref, o_ref, n4_ref, n16_ref):
        _from_pattern(a4_ref, n4_ref, d4)
        _from_pattern(a16_ref, n16_ref, d16)
        for s in range(tn // HEAD_DIM):
            cols = slice(s * HEAD_DIM, (s + 1) * HEAD_DIM)
            tot = a1_ref[:, cols] + n4_ref[s] + n16_ref[s]
            if unrotate:
                tot = _rotate_heads(tot, ct_ref[...], -up_ref[...], -down_ref[...])
            o_ref[:, cols] = tot.astype(BF16)

    tab = pl.BlockSpec((tm, HEAD_DIM), lambda i, j: (i, 0))
    pat = lambda d: pl.BlockSpec((d, tm // d, tn), lambda i, j: (0, i, j))
    return pl.pallas_call(
        body, name=name, grid=(seq // tm, per),
        in_specs=[ANY, pl.BlockSpec((tm, tn), lambda i, j: (i, j)), pat(d4), pat(d16), tab, tab, tab],
        out_specs=pl.BlockSpec((tm, tn), lambda i, j: (i, col_block * per + j)),
        out_shape=jax.ShapeDtypeStruct(dh.shape, dh.dtype),
        scratch_shapes=[pltpu.VMEM((tn // HEAD_DIM, tm, HEAD_DIM), F32), pltpu.VMEM((tn // HEAD_DIM, tm, HEAD_DIM), F32)],
        input_output_aliases={0: 0},
        compiler_params=_params(("parallel", "parallel"), 32),
    )(dh, parts[0][0], parts[1], parts[2], *tabs)


def _grad_w_in(x, dh):
    seq = x.shape[0]
    ts, td, te = 512, D_MODEL // 2, SHARD_IN
    nk = seq // ts

    def body(x_ref, dh_ref, o_ref, acc_ref):
        k = pl.program_id(2)

        @pl.when(k == 0)
        def _():
            acc_ref[...] = jnp.zeros_like(acc_ref)

        acc_ref[...] += _dot_tn(x_ref[...].astype(BF16), dh_ref[...])

        @pl.when(k == nk - 1)
        def _():
            o_ref[...] = acc_ref[...]

    return pl.pallas_call(
        body, name="grad_w_in", grid=(N_SHARDS, 2, nk),
        in_specs=[pl.BlockSpec((ts, td), lambda e, d, k: (k, d)), pl.BlockSpec((ts, te), lambda e, d, k: (k, e))],
        out_specs=pl.BlockSpec((None, None, td, te), lambda e, d, k: (e, d, 0, 0)),
        out_shape=jax.ShapeDtypeStruct((N_SHARDS, 2, td, te), F32),
        scratch_shapes=[pltpu.VMEM((td, te), F32)],
        compiler_params=_params(("parallel", "parallel", "arbitrary"), 48),
    )(x, dh)


def _grad_w_out(y, dzb):
    seq = y.shape[0]
    ts, te = 512, 1024
    nk = seq // ts

    def body(y_ref, dz_ref, o_ref, acc_ref):
        k = pl.program_id(1)

        @pl.when(k == 0)
        def _():
            acc_ref[...] = jnp.zeros_like(acc_ref)

        acc_ref[...] += _dot_tn(y_ref[...], dz_ref[...])

        @pl.when(k == nk - 1)
        def _():
            o_ref[...] = acc_ref[...]

    return pl.pallas_call(
        body, name="grad_w_out", grid=(D_MODEL // te, nk),
        in_specs=[pl.BlockSpec((ts, te), lambda e, k: (k, e)), pl.BlockSpec((ts, D_MODEL), lambda e, k: (k, 0))],
        out_specs=pl.BlockSpec((te, D_MODEL), lambda e, k: (e, 0)),
        out_shape=jax.ShapeDtypeStruct((D_MODEL, D_MODEL), F32),
        scratch_shapes=[pltpu.VMEM((te, D_MODEL), F32)],
        compiler_params=_params(("parallel", "arbitrary"), 48),
    )(y, dzb)


def _grad_x(dh, w_in_g, dz):
    seq = dh.shape[0]
    tm, tk = 512, 512
    per_shard = SHARD_IN // tk
    nk = D_IN // tk

    def body(dh_ref, w_ref, dz_ref, o_ref, acc_ref):
        k = pl.program_id(1)

        @pl.when(k == 0)
        def _():
            acc_ref[...] = DEEPNORM_ALPHA * dz_ref[...]

        acc_ref[...] += _dot_nt(dh_ref[...], w_ref[...])

        @pl.when(k == nk - 1)
        def _():
            o_ref[...] = acc_ref[...]

    return pl.pallas_call(
        body, name="grad_x", grid=(seq // tm, nk),
        in_specs=[pl.BlockSpec((tm, tk), lambda i, k: (i, k)),
                  pl.BlockSpec((None, D_MODEL, tk), lambda i, k: (k // per_shard, 0, k % per_shard)),
                  pl.BlockSpec((tm, D_MODEL), lambda i, k: (i, 0))],
        out_specs=pl.BlockSpec((tm, D_MODEL), lambda i, k: (i, 0)),
        out_shape=jax.ShapeDtypeStruct((seq, D_MODEL), F32),
        scratch_shapes=[pltpu.VMEM((tm, D_MODEL), F32)],
        compiler_params=_params(("parallel", "arbitrary"), 48),
    )(dh, w_in_g, dz)


def _local_step(x, target, w_in_g, w_out_g, w_pool_g, pool_scale, gain, bias):
    seq = x.shape[0]
    tabs = _rope_tables(seq)
    qkv = _in_proj_qkv(x, w_in_g, tabs)
    hug = _in_proj_pool_gate(x, w_in_g)
    o_list, st_list = [], []
    for p, dil in enumerate(DILATIONS):
        o, st = _attn_fwd(qkv[p], "attn_fwd_d%d" % dil)
        o_list.append(o)
        st_list.append(st)
    y, mixpre, lse_all, pooled = _mix_gate(o_list, st_list, hug, w_pool_g, pool_scale)
    dz, dzb, gain_part, bias_part, loss_part = _out_proj_loss(y, w_out_g, x, target, gain, bias)
    dh, dpo, do_list, stat_list = _dy_gate_bwd(dzb, w_out_g, hug, mixpre, pool_scale, lse_all)
    g_w_out = _grad_w_out(y, dzb)
    dh, g_w_pool, scale_part = _pool_bwd(dh, dpo, mixpre, pooled, w_pool_g, pool_scale)
    parts = [_attn_bwd(qkv[p], do_list[p], stat_list[p], "attn_bwd_d%d" % dil) for p, dil in enumerate(DILATIONS)]
    dh = _sum_patterns(dh, [t[0] for t in parts], tabs, True, 0, "sum_dq")
    dh = _sum_patterns(dh, [t[1] for t in parts], tabs, True, 1, "sum_dk")
    dh = _sum_patterns(dh, [t[2] for t in parts], tabs, False, 2, "sum_dv")
    g_w_in = _grad_w_in(x, dh)
    g_x = _grad_x(dh, w_in_g, dz)
    small = jnp.concatenate([scale_part, gain_part, bias_part, loss_part], axis=1)
    return g_x, g_w_in, g_w_out, g_w_pool, small


def _exchange_halves(grads):
    n = len(grads)

    def body(*refs):
        src = refs[:n]
        dst = refs[n:2 * n]
        send_sems, recv_sems = refs[2 * n:]
        x, y, c = lax.axis_index("x"), lax.axis_index("y"), lax.axis_index("c")
        copies = []
        for a in range(n):
            for j in range(N_SHARDS):
                cp = pltpu.make_async_remote_copy(
                    src_ref=src[a].at[j, 1 - c], dst_ref=dst[a].at[j],
                    send_sem=send_sems.at[a, j], recv_sem=recv_sems.at[a, j],
                    device_id=(x, y, 1 - c), device_id_type=MESH)
                cp.start()
                copies.append(cp)
        for cp in copies:
            cp.wait()

    return pl.pallas_call(
        body, name="exchange_halves",
        in_specs=[ANY] * n, out_specs=[ANY] * n,
        out_shape=[jax.ShapeDtypeStruct((N_SHARDS,) + g.shape[2:], g.dtype) for g in grads],
        scratch_shapes=[pltpu.SemaphoreType.DMA((n, N_SHARDS)), pltpu.SemaphoreType.DMA((n, N_SHARDS))],
    )(*grads)


def _add_own_half(grad, recv, core, name):
    _, _, r, c = grad.shape
    tr = min(r, 256)

    def body(core_ref, g_ref, r_ref, o_ref, ob_ref):
        tot = g_ref[...] + r_ref[...]
        o_ref[...] = tot
        ob_ref[...] = tot.astype(BF16)

    out = pl.BlockSpec((None, tr, c), lambda j, i, core_ref: (j, i, 0))
    return pl.pallas_call(
        body, name=name,
        grid_spec=pltpu.PrefetchScalarGridSpec(
            num_scalar_prefetch=1, grid=(N_SHARDS, r // tr),
            in_specs=[pl.BlockSpec((None, None, tr, c), lambda j, i, core_ref: (j, core_ref[0], i, 0)),
                      pl.BlockSpec((None, tr, c), lambda j, i, core_ref: (j, i, 0))],
            out_specs=[out, out]),
        out_shape=[jax.ShapeDtypeStruct((N_SHARDS, r, c), F32), jax.ShapeDtypeStruct((N_SHARDS, r, c), BF16)],
        compiler_params=_params(("parallel", "parallel"), 32),
    )(core, grad, recv)


def _scatter_to_chips(sums):
    n = len(sums)

    def body(*refs):
        src = refs[:n]
        dst = refs[n:2 * n]
        send_sems, recv_sems = refs[2 * n:]
        x, y, c = lax.axis_index("x"), lax.axis_index("y"), lax.axis_index("c")
        chips = [(1 - x, y), (x, 1 - y), (1 - x, 1 - y)]
        copies = []
        for a in range(n):
            for k, (cx, cy) in enumerate(chips):
                cp = pltpu.make_async_remote_copy(
                    src_ref=src[a].at[2 * cx + cy], dst_ref=dst[a].at[k],
                    send_sem=send_sems.at[a, k], recv_sem=recv_sems.at[a, k],
                    device_id=(cx, cy, c), device_id_type=MESH)
                cp.start()
                copies.append(cp)
        for cp in copies:
            cp.wait()

    return pl.pallas_call(
        body, name="scatter_to_chips",
        in_specs=[ANY] * n, out_specs=[ANY] * n,
        out_shape=[jax.ShapeDtypeStruct((3,) + s.shape[1:], s.dtype) for s in sums],
        scratch_shapes=[pltpu.SemaphoreType.DMA((n, 3)), pltpu.SemaphoreType.DMA((n, 3))],
    )(*sums)


def _add_chips(sums, recv, chip, name):
    _, r, c = sums.shape
    tr = min(r, 256)

    def body(chip_ref, s_ref, r_ref, o_ref):
        o_ref[...] = ((s_ref[...] + r_ref[0].astype(F32)) + r_ref[1].astype(F32)) + r_ref[2].astype(F32)

    return pl.pallas_call(
        body, name=name,
        grid_spec=pltpu.PrefetchScalarGridSpec(
            num_scalar_prefetch=1, grid=(r // tr,),
            in_specs=[pl.BlockSpec((None, tr, c), lambda i, chip_ref: (chip_ref[0], i, 0)),
                      pl.BlockSpec((3, tr, c), lambda i, chip_ref: (0, i, 0))],
            out_specs=pl.BlockSpec((tr, c), lambda i, chip_ref: (i, 0))),
        out_shape=jax.ShapeDtypeStruct((r, c), F32),
        compiler_params=_params(("parallel",), 32),
    )(chip, sums, recv)


def _share_with_sibling(reduced):
    n = len(reduced)

    def body(*refs):
        src = refs[:n]
        dst = refs[n:2 * n]
        local_sems, send_sems, recv_sems = refs[2 * n:]
        x, y, c = lax.axis_index("x"), lax.axis_index("y"), lax.axis_index("c")
        copies = []
        for a in range(n):
            cp = pltpu.make_async_copy(src[a], dst[a].at[c], local_sems.at[a])
            cp.start()
            copies.append(cp)
            cp = pltpu.make_async_remote_copy(
                src_ref=src[a], dst_ref=dst[a].at[c], send_sem=send_sems.at[a], recv_sem=recv_sems.at[a],
                device_id=(x, y, 1 - c), device_id_type=MESH)
            cp.start()
            copies.append(cp)
        for a in range(n):
            other = dst[a].at[1 - c]
            pltpu.make_async_remote_copy(
                src_ref=other, dst_ref=other, send_sem=send_sems.at[a], recv_sem=recv_sems.at[a],
                device_id=(x, y, 1 - c), device_id_type=MESH).wait_recv()
        for a in range(n):
            copies[2 * a].wait()
            copies[2 * a + 1].wait_send()

    return pl.pallas_call(
        body, name="share_with_sibling",
        in_specs=[ANY] * n, out_specs=[ANY] * n,
        out_shape=[jax.ShapeDtypeStruct((2,) + r.shape, r.dtype) for r in reduced],
        scratch_shapes=[pltpu.SemaphoreType.DMA((n,)), pltpu.SemaphoreType.DMA((n,)), pltpu.SemaphoreType.DMA((n,))],
    )(*reduced)


def _adam_math(w, g, m, v):
    m = ADAM_B1 * m + (1.0 - ADAM_B1) * g
    v = ADAM_B2 * v + (1.0 - ADAM_B2) * (g * g)
    m_hat = m / (1.0 - ADAM_B1 ** ADAM_STEP)
    v_hat = v / (1.0 - ADAM_B2 ** ADAM_STEP)
    delta = -ADAM_LR * (m_hat / (jnp.sqrt(v_hat) + ADAM_EPS) + ADAM_WD * w)
    return delta, m, v


def _small_allreduce_adamw(small, w_vec, m_vec, v_vec):
    width = small.shape[1]
    n_par = w_vec.shape[1]

    def body(s_ref, w_ref, m_ref, v_ref, loss_ref, g_ref, d_ref, nm_ref, nv_ref, gather_ref, send_sems, recv_sems):
        x, y, c = lax.axis_index("x"), lax.axis_index("y"), lax.axis_index("c")
        me = 4 * x + 2 * y + c
        gather_ref[me] = s_ref[...]
        copies = []
        for r in range(1, 8):
            bx, by, bc = (r >> 2) & 1, (r >> 1) & 1, r & 1
            peer = (x ^ bx, y ^ by, c ^ bc)
            cp = pltpu.make_async_remote_copy(
                src_ref=s_ref, dst_ref=gather_ref.at[me], send_sem=send_sems.at[r - 1], recv_sem=recv_sems.at[r - 1],
                device_id=peer, device_id_type=MESH)
            cp.start()
            copies.append(cp)
        for r in range(1, 8):
            bx, by, bc = (r >> 2) & 1, (r >> 1) & 1, r & 1
            theirs = gather_ref.at[4 * (x ^ bx) + 2 * (y ^ by) + (c ^ bc)]
            pltpu.make_async_remote_copy(
                src_ref=theirs, dst_ref=theirs, send_sem=send_sems.at[r - 1], recv_sem=recv_sems.at[r - 1],
                device_id=(x ^ bx, y ^ by, c ^ bc), device_id_type=MESH).wait_recv()
        for cp in copies:
            cp.wait_send()
        tot = gather_ref[0]
        for d in range(1, 8):
            tot = tot + gather_ref[d]
        tot = jnp.sum(tot, axis=0, keepdims=True)
        sq = jnp.sum(tot[:, n_par:], axis=1, keepdims=True)
        loss_ref[...] = jnp.broadcast_to(sq * (0.5 / D_MODEL), loss_ref.shape)
        g = tot[:, :n_par]
        g_ref[...] = g
        d_ref[...], nm_ref[...], nv_ref[...] = _adam_math(w_ref[...], g, m_ref[...], v_ref[...])

    vm = pl.BlockSpec(memory_space=pltpu.VMEM)
    vec = jax.ShapeDtypeStruct((1, n_par), F32)
    return pl.pallas_call(
        body, name="small_allreduce_adamw",
        in_specs=[vm] * 4, out_specs=[vm] * 5,
        out_shape=[jax.ShapeDtypeStruct((1, 128), F32), vec, vec, vec, vec],
        scratch_shapes=[pltpu.VMEM((8, 8, width), F32), pltpu.SemaphoreType.DMA((7,)), pltpu.SemaphoreType.DMA((7,))],
    )(small, w_vec, m_vec, v_vec)


def _adamw(w, g, m, v, name):
    r, c = w.shape
    tr = min(r, 256)

    def body(w_ref, g_ref, m_ref, v_ref, d_ref, nm_ref, nv_ref):
        d_ref[...], nm_ref[...], nv_ref[...] = _adam_math(w_ref[...], g_ref[...], m_ref[...], v_ref[...])

    spec = pl.BlockSpec((tr, c), lambda i: (i, 0))
    shape = jax.ShapeDtypeStruct((r, c), F32)
    return pl.pallas_call(
        body, name=name, grid=(r // tr,),
        in_specs=[spec] * 4, out_specs=[spec] * 3, out_shape=[shape] * 3,
        compiler_params=_params(("parallel",), 32),
    )(w, g, m, v)


def kernel(x, w_in, w_pool, pool_scale, w_out, ln_gain, ln_bias, loss_target, m_w_in, m_w_pool, m_pool_scale, m_w_out, m_ln_gain, m_ln_bias, v_w_in, v_w_pool, v_pool_scale, v_w_out, v_ln_gain, v_ln_bias):
    xi, yi, ci = lax.axis_index("x"), lax.axis_index("y"), lax.axis_index("c")
    chip = (2 * xi + yi).astype(jnp.int32).reshape(1)
    core = ci.astype(jnp.int32).reshape(1)
    n_groups = len(POOL_WINDOWS)
    shard_c = w_pool.shape[2]

    w_in_b = _cast_bf16(w_in[0], "cast_w_in", 256)
    w_out_b = _cast_bf16(w_out[0], "cast_w_out", 256)
    w_pool_b = _cast_bf16(w_pool[0].reshape(n_groups * shard_c, POOL_GROUP_DIM), "cast_w_pool", 256)
    w_in_g, w_out_g, w_pool_sh = _allgather_weights([w_in_b, w_out_b, w_pool_b])
    w_pool_g = (w_pool_sh.reshape(N_SHARDS, n_groups, shard_c, POOL_GROUP_DIM).transpose(1, 0, 2, 3)
                .reshape(n_groups, POOL_GROUP_DIM, POOL_GROUP_DIM))

    g_x, g_w_in, g_w_out, g_w_pool, small = _local_step(
        x[0], loss_target[0], w_in_g, w_out_g, w_pool_g, pool_scale, ln_gain, ln_bias)

    half_c = shard_c // 2
    g_w_out_p = g_w_out.reshape(N_SHARDS, 2, D_MODEL // (2 * N_SHARDS), D_MODEL)
    g_w_pool_p = (g_w_pool.reshape(n_groups, N_SHARDS, 2, half_c, POOL_GROUP_DIM).transpose(1, 2, 0, 3, 4)
                  .reshape(N_SHARDS, 2, n_groups * half_c, POOL_GROUP_DIM))
    grads = [g_w_in, g_w_out_p, g_w_pool_p]
    recv = _exchange_halves(grads)
    sums = [_add_own_half(g, r, core, "add_own_half_%d" % a) for a, (g, r) in enumerate(zip(grads, recv))]
    recv = _scatter_to_chips([s[1] for s in sums])
    reduced = [_add_chips(s[0], r, chip, "add_chips_%d" % a) for a, (s, r) in enumerate(zip(sums, recv))]
    full = _share_with_sibling(reduced)
    grad_w_in = full[0].reshape(D_MODEL, SHARD_IN)
    grad_w_out = full[1].reshape(D_MODEL // N_SHARDS, D_MODEL)
    grad_w_pool = (full[2].reshape(2, n_groups, half_c, POOL_GROUP_DIM).transpose(1, 0, 2, 3)
                   .reshape(n_groups * shard_c, POOL_GROUP_DIM))

    d_in, nm_in, nv_in = _adamw(w_in[0], grad_w_in, m_w_in[0], v_w_in[0], "adamw_w_in")
    d_out, nm_out, nv_out = _adamw(w_out[0], grad_w_out, m_w_out[0], v_w_out[0], "adamw_w_out")
    flat = lambda t: t[0].reshape(n_groups * shard_c, POOL_GROUP_DIM)
    d_pool, nm_pool, nv_pool = _adamw(flat(w_pool), grad_w_pool, flat(m_w_pool), flat(v_w_pool), "adamw_w_pool")

    cat = lambda a, b, c: jnp.concatenate([a, b, c], axis=1)
    loss_v, g_vec, d_vec, nm_vec, nv_vec = _small_allreduce_adamw(
        small, cat(pool_scale, ln_gain, ln_bias), cat(m_pool_scale, m_ln_gain, m_ln_bias),
        cat(v_pool_scale, v_ln_gain, v_ln_bias))

    def split(vec):
        return vec[:, :D_POOL], vec[:, D_POOL:D_POOL + D_MODEL], vec[:, D_POOL + D_MODEL:]

    g_scale, g_gain, g_bias = split(g_vec)
    d_scale, d_gain, d_bias = split(d_vec)
    nm_scale, nm_gain, nm_bias = split(nm_vec)
    nv_scale, nv_gain, nv_bias = split(nv_vec)
    pool_shape = w_pool.shape
    return (loss_v[0, 0], g_x[None],
            grad_w_in[None], grad_w_pool.reshape(pool_shape), g_scale, grad_w_out[None], g_gain, g_bias,
            d_in[None], d_pool.reshape(pool_shape), d_scale, d_out[None], d_gain, d_bias,
            nm_in[None], nm_pool.reshape(pool_shape), nm_scale, nm_out[None], nm_gain, nm_bias,
            nv_in[None], nv_pool.reshape(pool_shape), nv_scale, nv_out[None], nv_gain, nv_bias)
```

```python
import functools

import jax
import jax.numpy as jnp
from jax import lax
from jax.experimental import pallas as pl
from jax.experimental.pallas import tpu as pltpu

F32 = jnp.float32
BF16 = jnp.bfloat16
MESH = pl.DeviceIdType.MESH
ANY = pl.BlockSpec(memory_space=pl.ANY)

D_MODEL = 2048
D_ATTN = 1024
D_POOL = 1024
HEAD_DIM = 128
N_HEADS = 8
ROPE_DIM = 32
ROPE_THETA = 500000.0
DILATIONS = (1, 4, 16)
KEY_BLOCK = 128
CHUNK = 2 * KEY_BLOCK
STAT_LANES = 128
POOL_WINDOWS = (2, 4, 8, 16)
POOL_GROUP_DIM = 256
POOL_HALO = 16
D_QKV = 3 * D_ATTN
D_UG = D_POOL + D_MODEL
D_IN = D_QKV + D_UG
N_SHARDS = 4
SHARD_IN = D_IN // N_SHARDS
LN_EPS = 1e-5
DEEPNORM_ALPHA = 2.0 ** 0.25
ADAM_LR = 0.001
ADAM_B1 = 0.9
ADAM_B2 = 0.999
ADAM_EPS = 1e-08
ADAM_WD = 0.01
ADAM_STEP = 10
NEG = -1e30
MIB = 1024 * 1024


def _params(sem, vmem_mib):
    return pltpu.CompilerParams(dimension_semantics=sem, vmem_limit_bytes=vmem_mib * MIB)


def _dot_nn(a, b):
    return jnp.dot(a, b, preferred_element_type=F32)


def _dot_nt(a, b):
    return lax.dot_general(a, b, (((1,), (1,)), ((), ())), preferred_element_type=F32)


def _dot_tn(a, b):
    return lax.dot_general(a, b, (((0,), (0,)), ((), ())), preferred_element_type=F32)


def _fold_rows(a):
    r, c = a.shape
    return jnp.sum(a.reshape(r // 8, 8, c), axis=0)


def _cast_bf16(a, chip, name, rows):
    r, c = a.shape

    def body(chip_ref, a_ref, o_ref):
        o_ref[...] = a_ref[...].astype(BF16)

    return pl.pallas_call(
        body, name=name,
        grid_spec=pltpu.PrefetchScalarGridSpec(
            num_scalar_prefetch=1, grid=(r // rows,),
            in_specs=[pl.BlockSpec((rows, c), lambda i, chip_ref: (i, 0))],
            out_specs=pl.BlockSpec((None, rows, c), lambda i, chip_ref: (chip_ref[0], i, 0))),
        out_shape=jax.ShapeDtypeStruct((N_SHARDS, r, c), BF16),
        compiler_params=_params(("parallel",), 32),
    )(chip, a)


def _allgather_weights(bufs):
    n = len(bufs)

    def body(*refs):
        dst = refs[n:2 * n]
        ici_send, ici_recv, d2d_send, d2d_recv = refs[2 * n:]
        x, y, c = lax.axis_index("x"), lax.axis_index("y"), lax.axis_index("c")
        mine = 2 * x + y
        chips = [(1 - x, y), (x, 1 - y), (1 - x, 1 - y)]

        def half(a, core):
            rows = bufs[a].shape[1] // 2
            return pl.ds(core * rows, rows)

        sends = []
        for a in range(n):
            for k, (cx, cy) in enumerate(chips):
                own = dst[a].at[mine, half(a, c)]
                cp = pltpu.make_async_remote_copy(
                    src_ref=own, dst_ref=own, send_sem=ici_send.at[a, k], recv_sem=ici_recv.at[a, k],
                    device_id=(cx, cy, c), device_id_type=MESH)
                cp.start()
                sends.append(cp)
        for k, (cx, cy) in enumerate(chips):
            theirs = 2 * cx + cy
            for a in range(n):
                landed = dst[a].at[theirs, half(a, c)]
                pltpu.make_async_remote_copy(
                    src_ref=landed, dst_ref=landed, send_sem=ici_send.at[a, k], recv_sem=ici_recv.at[a, k],
                    device_id=(cx, cy, c), device_id_type=MESH).wait_recv()
                cp = pltpu.make_async_remote_copy(
                    src_ref=landed, dst_ref=landed, send_sem=d2d_send.at[a, k], recv_sem=d2d_recv.at[a, k],
                    device_id=(x, y, 1 - c), device_id_type=MESH)
                cp.start()
                sends.append(cp)
        for k, (cx, cy) in enumerate(chips):
            theirs = 2 * cx + cy
            for a in range(n):
                passed = dst[a].at[theirs, half(a, 1 - c)]
                pltpu.make_async_remote_copy(
                    src_ref=passed, dst_ref=passed, send_sem=d2d_send.at[a, k], recv_sem=d2d_recv.at[a, k],
                    device_id=(x, y, 1 - c), device_id_type=MESH).wait_recv()
        for cp in sends:
            cp.wait_send()

    return pl.pallas_call(
        body, name="allgather_weights",
        in_specs=[ANY] * n, out_specs=[ANY] * n,
        out_shape=[jax.ShapeDtypeStruct(b.shape, b.dtype) for b in bufs],
        input_output_aliases={a: a for a in range(n)},
        scratch_shapes=[pltpu.SemaphoreType.DMA((n, 3)), pltpu.SemaphoreType.DMA((n, 3)),
                        pltpu.SemaphoreType.DMA((n, 3)), pltpu.SemaphoreType.DMA((n, 3))],
    )(*bufs)


def _rope_tables(seq):
    half = ROPE_DIM // 2
    inv_freq = ROPE_THETA ** (-(2.0 * jnp.arange(half, dtype=F32)) / ROPE_DIM)
    ang = jnp.arange(seq, dtype=jnp.int32).astype(F32)[:, None] * inv_freq[None, :]
    cos, sin = jnp.cos(ang), jnp.sin(ang)
    pad = jnp.zeros((seq, HEAD_DIM - ROPE_DIM), F32)
    zeros = jnp.zeros((seq, half), F32)
    c_tab = jnp.concatenate([cos, cos, pad + 1.0], axis=1)
    up_tab = jnp.concatenate([-sin, zeros, pad], axis=1)
    down_tab = jnp.concatenate([zeros, sin, pad], axis=1)
    return c_tab, up_tab, down_tab


def _rotate_heads(t, c_tab, up_tab, down_tab):
    outs = []
    for h in range(t.shape[1] // HEAD_DIM):
        th = t[:, h * HEAD_DIM:(h + 1) * HEAD_DIM]
        up = pltpu.roll(th, HEAD_DIM - ROPE_DIM // 2, axis=1)
        down = pltpu.roll(th, ROPE_DIM // 2, axis=1)
        outs.append(th * c_tab + up * up_tab + down * down_tab)
    return outs[0] if len(outs) == 1 else jnp.concatenate(outs, axis=1)


def _to_pattern(slabs_ref, dst_ref, dil, dtype):
    n_slabs, rows, _ = slabs_ref.shape
    for s in range(n_slabs):
        for r in range(dil):
            dst_ref[r, :, s * 128:(s + 1) * 128] = slabs_ref[s, pl.ds(r, rows // dil, dil), :].astype(dtype)


def _from_pattern(src_ref, slabs_ref, dil):
    n_slabs, rows, _ = slabs_ref.shape
    for s in range(n_slabs):
        for r in range(dil):
            slabs_ref[s, pl.ds(r, rows // dil, dil), :] = src_ref[r, :, s * 128:(s + 1) * 128]


def _store_slabs(slabs_ref, value):
    for s in range(slabs_ref.shape[0]):
        slabs_ref[s] = value[:, s * 128:(s + 1) * 128]


def _in_proj_qkv(x, w_in_g, tabs):
    seq = x.shape[0]
    tm, tn = 512, 512
    per_shard = SHARD_IN // tn
    n_tiles = D_QKV // tn
    n_rot_tiles = 2 * D_ATTN // tn
    d4, d16 = DILATIONS[1], DILATIONS[2]

    def body(x_ref, w_ref, c_ref, up_ref, down_ref, o1_ref, o4_ref, o16_ref, xb_ref, res_ref):
        j = pl.program_id(1)

        @pl.when(j == 0)
        def _():
            xb_ref[...] = x_ref[...].astype(BF16)

        acc = _dot_nn(xb_ref[...], w_ref[...])

        @pl.when(j < n_rot_tiles)
        def _():
            res = _rotate_heads(acc, c_ref[...], up_ref[...], down_ref[...])
            o1_ref[...] = res.astype(BF16)
            _store_slabs(res_ref, res)

        @pl.when(j >= n_rot_tiles)
        def _():
            o1_ref[...] = acc.astype(BF16)
            _store_slabs(res_ref, acc)

        _to_pattern(res_ref, o4_ref, d4, BF16)
        _to_pattern(res_ref, o16_ref, d16, BF16)

    tab_spec = pl.BlockSpec((tm, HEAD_DIM), lambda i, j: (i, 0))
    o1, o4, o16 = pl.pallas_call(
        body, name="in_proj_qkv", grid=(seq // tm, n_tiles),
        in_specs=[pl.BlockSpec((tm, D_MODEL), lambda i, j: (i, 0)),
                  pl.BlockSpec((None, D_MODEL, tn), lambda i, j: (j // per_shard, 0, j % per_shard)),
                  tab_spec, tab_spec, tab_spec],
        out_specs=[pl.BlockSpec((tm, tn), lambda i, j: (i, j)),
                   pl.BlockSpec((d4, tm // d4, tn), lambda i, j: (0, i, j)),
                   pl.BlockSpec((d16, tm // d16, tn), lambda i, j: (0, i, j))],
        out_shape=[jax.ShapeDtypeStruct((seq, D_QKV), BF16),
                   jax.ShapeDtypeStruct((d4, seq // d4, D_QKV), BF16),
                   jax.ShapeDtypeStruct((d16, seq // d16, D_QKV), BF16)],
        scratch_shapes=[pltpu.VMEM((tm, D_MODEL), BF16), pltpu.VMEM((tn // 128, tm, 128), F32)],
        compiler_params=_params(("parallel", "arbitrary"), 40),
    )(x, w_in_g, *tabs)
    return [o1[None], o4, o16]


def _in_proj_pool_gate(x, w_in_g):
    seq = x.shape[0]
    tm, tn = 512, 512
    per_shard = SHARD_IN // tn
    first_tile = D_QKV // tn

    def body(x_ref, w_ref, o_ref, xb_ref):
        @pl.when(pl.program_id(1) == 0)
        def _():
            xb_ref[...] = x_ref[...].astype(BF16)

        o_ref[...] = _dot_nn(xb_ref[...], w_ref[...])

    return pl.pallas_call(
        body, name="in_proj_pool_gate", grid=(seq // tm, D_UG // tn),
        in_specs=[pl.BlockSpec((tm, D_MODEL), lambda i, j: (i, 0)),
                  pl.BlockSpec((None, D_MODEL, tn),
                               lambda i, j: ((j + first_tile) // per_shard, 0, (j + first_tile) % per_shard))],
        out_specs=pl.BlockSpec((tm, tn), lambda i, j: (i, j)),
        out_shape=jax.ShapeDtypeStruct((seq, D_UG), F32),
        scratch_shapes=[pltpu.VMEM((tm, D_MODEL), BF16)],
        compiler_params=_params(("parallel", "arbitrary"), 40),
    )(x, w_in_g)


def _band_masks():
    row = lax.broadcasted_iota(jnp.int32, (KEY_BLOCK, KEY_BLOCK), 0)
    col = lax.broadcasted_iota(jnp.int32, (KEY_BLOCK, KEY_BLOCK), 1)
    return col <= row, col >= row


def _attn_fwd(qkv, name):
    dil, n, _ = qkv.shape
    scale = HEAD_DIM ** -0.5
    lo, hi = slice(0, KEY_BLOCK), slice(KEY_BLOCK, CHUNK)

    def body(q_ref, k_ref, v_ref, kb_ref, vb_ref, o_ref, st_ref):
        i = pl.program_id(1)
        cur_mask, prev_mask = _band_masks()
        before_mask = jnp.logical_and(prev_mask, i > 0)
        lane = lax.broadcasted_iota(jnp.int32, (KEY_BLOCK, STAT_LANES), 1)
        for rows in (lo, hi):
            stats = jnp.zeros((KEY_BLOCK, STAT_LANES), F32)
            for h in range(N_HEADS):
                cols = slice(h * HEAD_DIM, (h + 1) * HEAD_DIM)
                q = q_ref[rows, cols]
                if rows is lo:
                    k_prev, v_prev, mask = kb_ref[:, cols], vb_ref[:, cols], before_mask
                else:
                    k_prev, v_prev, mask = k_ref[lo, cols], v_ref[lo, cols], prev_mask
                s_prev = jnp.where(mask, _dot_nt(q, k_prev) * scale, NEG)
                s_cur = jnp.where(cur_mask, _dot_nt(q, k_ref[rows, cols]) * scale, NEG)
                m = jnp.maximum(jnp.max(s_cur, axis=-1, keepdims=True), jnp.max(s_prev, axis=-1, keepdims=True))
                p_cur = jnp.exp(s_cur - m)
                p_prev = jnp.exp(s_prev - m)
                den = jnp.sum(p_cur, axis=-1, keepdims=True) + jnp.sum(p_prev, axis=-1, keepdims=True)
                o = _dot_nn(p_cur.astype(BF16), v_ref[rows, cols]) + _dot_nn(p_prev.astype(BF16), v_prev)
                o_ref[rows, cols] = o / den
                stats = jnp.where(lane == h, m + jnp.log(den), stats)
            st_ref[rows, :] = stats

    main = lambda cb: pl.BlockSpec((None, CHUNK, D_ATTN), lambda r, i: (r, i, cb))
    before = lambda cb: pl.BlockSpec((None, KEY_BLOCK, D_ATTN), lambda r, i: (r, jnp.maximum(2 * i - 1, 0), cb))
    return pl.pallas_call(
        body, name=name, grid=(dil, n // CHUNK),
        in_specs=[main(0), main(1), main(2), before(1), before(2)],
        out_specs=[main(0), pl.BlockSpec((None, CHUNK, STAT_LANES), lambda r, i: (r, i, 0))],
        out_shape=[jax.ShapeDtypeStruct((dil, n, D_ATTN), F32), jax.ShapeDtypeStruct((dil, n, STAT_LANES), F32)],
        compiler_params=_params(("parallel", "parallel"), 40),
    )(qkv, qkv, qkv, qkv, qkv)


def _attn_bwd(qkv, do, stats, name):
    dil, n, _ = qkv.shape
    n_blocks = n // KEY_BLOCK
    last = n // CHUNK - 1
    scale = HEAD_DIM ** -0.5
    lo, hi = slice(0, KEY_BLOCK), slice(KEY_BLOCK, CHUNK)

    def body(q_ref, k_ref, v_ref, kb_ref, vb_ref, qa_ref, do_ref, doa_ref, st_ref, sta_ref, dq_ref, dk_ref, dv_ref):
        i = pl.program_id(1)
        cur_mask, prev_mask = _band_masks()
        before_mask = jnp.logical_and(prev_mask, i > 0)
        after_mask = jnp.logical_and(prev_mask, i < last)

        def pair(q, k, v, do_b, lse, delta, mask):
            p = jnp.exp(jnp.where(mask, _dot_nt(q, k) * scale, NEG) - lse)
            ds = p * (_dot_nt(do_b, v) - delta) * scale
            return p.astype(BF16), ds.astype(BF16)

        for h in range(N_HEADS):
            cols = slice(h * HEAD_DIM, (h + 1) * HEAD_DIM)
            lse_c, del_c = slice(h, h + 1), slice(N_HEADS + h, N_HEADS + h + 1)
            q0, q1, qa = q_ref[lo, cols], q_ref[hi, cols], qa_ref[:, cols]
            k0, k1, kb = k_ref[lo, cols], k_ref[hi, cols], kb_ref[:, cols]
            v0, v1, vb = v_ref[lo, cols], v_ref[hi, cols], vb_ref[:, cols]
            do0, do1, doa = do_ref[lo, cols], do_ref[hi, cols], doa_ref[:, cols]
            st0 = (st_ref[lo, lse_c], st_ref[lo, del_c])
            st1 = (st_ref[hi, lse_c], st_ref[hi, del_c])
            sta = (sta_ref[:, lse_c], sta_ref[:, del_c])
            _, ds_0b = pair(q0, kb, vb, do0, *st0, before_mask)
            p_00, ds_00 = pair(q0, k0, v0, do0, *st0, cur_mask)
            p_10, ds_10 = pair(q1, k0, v0, do1, *st1, prev_mask)
            p_11, ds_11 = pair(q1, k1, v1, do1, *st1, cur_mask)
            p_a1, ds_a1 = pair(qa, k1, v1, doa, *sta, after_mask)
            dq_ref[lo, cols] = _dot_nn(ds_0b, kb) + _dot_nn(ds_00, k0)
            dq_ref[hi, cols] = _dot_nn(ds_10, k0) + _dot_nn(ds_11, k1)
            dk_ref[lo, cols] = _dot_tn(ds_00, q0) + _dot_tn(ds_10, q1)
            dk_ref[hi, cols] = _dot_tn(ds_11, q1) + _dot_tn(ds_a1, qa)
            dv_ref[lo, cols] = _dot_tn(p_00, do0) + _dot_tn(p_10, do1)
            dv_ref[hi, cols] = _dot_tn(p_11, do1) + _dot_tn(p_a1, doa)

    def spec(rows, width, row_of, cb):
        return pl.BlockSpec((None, rows, width), lambda r, i: (r, row_of(i), cb))

    same = lambda i: i
    before = lambda i: jnp.maximum(2 * i - 1, 0)
    after = lambda i: jnp.minimum(2 * i + 2, n_blocks - 1)
    out = spec(CHUNK, D_ATTN, same, 0)
    return pl.pallas_call(
        body, name=name, grid=(dil, n // CHUNK),
        in_specs=[spec(CHUNK, D_ATTN, same, 0), spec(CHUNK, D_ATTN, same, 1), spec(CHUNK, D_ATTN, same, 2),
                  spec(KEY_BLOCK, D_ATTN, before, 1), spec(KEY_BLOCK, D_ATTN, before, 2),
                  spec(KEY_BLOCK, D_ATTN, after, 0),
                  spec(CHUNK, D_ATTN, same, 0), spec(KEY_BLOCK, D_ATTN, after, 0),
                  spec(CHUNK, STAT_LANES, same, 0), spec(KEY_BLOCK, STAT_LANES, after, 0)],
        out_specs=[out, out, out],
        out_shape=[jax.ShapeDtypeStruct((dil, n, D_ATTN), F32)] * 3,
        compiler_params=_params(("parallel", "parallel"), 40),
    )(qkv, qkv, qkv, qkv, qkv, qkv, do, do, stats, stats)


def _window_sums(ext, window, backward):
    rows = ext.shape[0]
    acc, span = ext, 1
    while span < window:
        acc = acc + pltpu.roll(acc, (rows - span) if backward else span, axis=0)
        span *= 2
    return acc


def _mix_gate(o_list, st_list, hug, w_pool_g, pool_scale):
    seq = hug.shape[0]
    tm = 256
    halo_blocks = tm // POOL_HALO
    d4, d16 = DILATIONS[1], DILATIONS[2]

    def body(o1_ref, o4_ref, o16_ref, l1_ref, l4_ref, l16_ref, u_ref, halo_ref, ga_ref, gp_ref, wp_ref, sc_ref,
             y_ref, mix_ref, lse_ref, pooled_ref, n4_ref, n16_ref, nl4_ref, nl16_ref):
        i = pl.program_id(0)
        _from_pattern(o4_ref, n4_ref, d4)
        _from_pattern(o16_ref, n16_ref, d16)
        _from_pattern(l4_ref, nl4_ref, d4)
        _from_pattern(l16_ref, nl16_ref, d16)
        la, lb, lc = l1_ref[...], nl4_ref[0], nl16_ref[0]
        mx = jnp.maximum(jnp.maximum(la, lb), lc)
        ea, eb, ec = jnp.exp(la - mx), jnp.exp(lb - mx), jnp.exp(lc - mx)
        tot = ea + eb + ec
        lse_ref[...] = mx + jnp.log(tot)
        wa, wb, wc = ea / tot, eb / tot, ec / tot
        ga = ga_ref[...]
        silu_a = ga * jax.nn.sigmoid(ga)
        for h in range(N_HEADS):
            cols = slice(h * HEAD_DIM, (h + 1) * HEAD_DIM)
            hc = slice(h, h + 1)
            attn = wa[:, hc] * o1_ref[:, cols] + wb[:, hc] * n4_ref[h] + wc[:, hc] * n16_ref[h]
            mix_ref[:, cols] = attn
            y_ref[:, cols] = (attn * silu_a[:, cols]).astype(BF16)

        u = u_ref[...]
        halo = jnp.where(i > 0, halo_ref[...], 0.0)
        ext = jnp.concatenate([halo, u], axis=0)
        pos = i * tm + lax.broadcasted_iota(jnp.int32, (tm, 1), 0)
        gp = gp_ref[...]
        gated_scale = sc_ref[...] * (gp * jax.nn.sigmoid(gp))
        for g, window in enumerate(POOL_WINDOWS):
            cols = slice(g * POOL_GROUP_DIM, (g + 1) * POOL_GROUP_DIM)
            sums = _window_sums(ext[:, cols], window, backward=False)[POOL_HALO:, :]
            count = jnp.minimum(pos + 1, window).astype(F32)
            pooled = (sums / count - u[:, cols]).astype(BF16)
            pooled_ref[:, cols] = pooled
            pre = _dot_nn(pooled, wp_ref[g])
            out_cols = slice(D_ATTN + g * POOL_GROUP_DIM, D_ATTN + (g + 1) * POOL_GROUP_DIM)
            mix_ref[:, out_cols] = pre
            y_ref[:, out_cols] = (pre * gated_scale[:, cols]).astype(BF16)

    row = lambda width, cb=0: pl.BlockSpec((tm, width), lambda i: (i, cb))
    pat = lambda d, width: pl.BlockSpec((d, tm // d, width), lambda i: (0, i, 0))
    return pl.pallas_call(
        body, name="mix_gate", grid=(seq // tm,),
        in_specs=[row(D_ATTN), pat(d4, D_ATTN), pat(d16, D_ATTN),
                  row(STAT_LANES), pat(d4, STAT_LANES), pat(d16, STAT_LANES),
                  row(D_POOL),
                  pl.BlockSpec((POOL_HALO, D_POOL), lambda i: (jnp.maximum(i * halo_blocks - 1, 0), 0)),
                  row(D_ATTN, 1), row(D_POOL, 2),
                  pl.BlockSpec((len(POOL_WINDOWS), POOL_GROUP_DIM, POOL_GROUP_DIM), lambda i: (0, 0, 0)),
                  pl.BlockSpec((1, D_POOL), lambda i: (0, 0))],
        out_specs=[row(D_MODEL), row(D_MODEL), row(STAT_LANES), row(D_POOL)],
        out_shape=[jax.ShapeDtypeStruct((seq, D_MODEL), BF16), jax.ShapeDtypeStruct((seq, D_MODEL), F32),
                   jax.ShapeDtypeStruct((seq, STAT_LANES), F32), jax.ShapeDtypeStruct((seq, D_POOL), BF16)],
        scratch_shapes=[pltpu.VMEM((N_HEADS, tm, HEAD_DIM), F32), pltpu.VMEM((N_HEADS, tm, HEAD_DIM), F32),
                        pltpu.VMEM((1, tm, STAT_LANES), F32), pltpu.VMEM((1, tm, STAT_LANES), F32)],
        compiler_params=_params(("parallel",), 48),
    )(o_list[0][0], o_list[1], o_list[2], st_list[0][0], st_list[1], st_list[2],
      hug, hug, hug, hug, w_pool_g, pool_scale)


def _out_proj_loss(y, w_out_g, x, target, gain, bias):
    seq = x.shape[0]
    tm = 256

    def body(y_ref, w_ref, x_ref, t_ref, g_ref, b_ref, dz_ref, dzb_ref, gg_ref, gb_ref, loss_ref):
        @pl.when(pl.program_id(0) == 0)
        def _():
            gg_ref[...] = jnp.zeros_like(gg_ref)
            gb_ref[...] = jnp.zeros_like(gb_ref)
            loss_ref[...] = jnp.zeros_like(loss_ref)

        z = DEEPNORM_ALPHA * x_ref[...] + _dot_nn(y_ref[...], w_ref[...])
        mu = jnp.mean(z, axis=-1, keepdims=True)
        zc = z - mu
        rstd = lax.rsqrt(jnp.mean(zc * zc, axis=-1, keepdims=True) + LN_EPS)
        xhat = zc * rstd
        gain_v = g_ref[...]
        diff = xhat * gain_v + b_ref[...] - t_ref[...]
        sq = _fold_rows(diff * diff)
        part = sq[:, :128]
        for k in range(1, D_MODEL // 128):
            part = part + sq[:, k * 128:(k + 1) * 128]
        loss_ref[...] += part
        dln = diff * (1.0 / D_MODEL)
        gg_ref[...] += _fold_rows(dln * xhat)
        gb_ref[...] += _fold_rows(dln)
        dxhat = dln * gain_v
        dz = rstd * (dxhat - jnp.mean(dxhat, axis=-1, keepdims=True)
                     - xhat * jnp.mean(dxhat * xhat, axis=-1, keepdims=True))
        dz_ref[...] = dz
        dzb_ref[...] = dz.astype(BF16)

    row = lambda: pl.BlockSpec((tm, D_MODEL), lambda i: (i, 0))
    vec = lambda: pl.BlockSpec((1, D_MODEL), lambda i: (0, 0))
    acc = lambda width: pl.BlockSpec((8, width), lambda i: (0, 0))
    return pl.pallas_call(
        body, name="out_proj_loss", grid=(seq // tm,),
        in_specs=[row(), pl.BlockSpec((D_MODEL, D_MODEL), lambda i: (0, 0)), row(), row(), vec(), vec()],
        out_specs=[row(), row(), acc(D_MODEL), acc(D_MODEL), acc(128)],
        out_shape=[jax.ShapeDtypeStruct((seq, D_MODEL), F32), jax.ShapeDtypeStruct((seq, D_MODEL), BF16),
                   jax.ShapeDtypeStruct((8, D_MODEL), F32), jax.ShapeDtypeStruct((8, D_MODEL), F32),
                   jax.ShapeDtypeStruct((8, 128), F32)],
        compiler_params=_params(("arbitrary",), 48),
    )(y, w_out_g.reshape(D_MODEL, D_MODEL), x, target, gain, bias)


def _dy_gate_bwd(dzb, w_out_g, hug, mixpre, pool_scale, lse_all):
    seq = dzb.shape[0]
    tm = 256
    d4, d16 = DILATIONS[1], DILATIONS[2]

    def body(dz_ref, w_ref, ga_ref, gp_ref, mix_ref, sc_ref, lse_ref,
             dh_ref, dpo_ref, do1_ref, do4_ref, do16_ref, st1_ref, st4_ref, st16_ref, da_ref, st_ref):
        dy = _dot_nt(dz_ref[...], w_ref[...])
        ga = ga_ref[...]
        sig = jax.nn.sigmoid(ga)
        attn = mix_ref[:, :D_ATTN]
        dya = dy[:, :D_ATTN]
        dattn = dya * (ga * sig)
        dh_ref[:, :D_ATTN] = (dya * attn * (sig * (1.0 + ga * (1.0 - sig)))).astype(BF16)
        _store_slabs(da_ref, dattn)
        lane = lax.broadcasted_iota(jnp.int32, (tm, STAT_LANES), 1)
        stats = lse_ref[...]
        prod = dattn * attn
        for h in range(N_HEADS):
            delta = jnp.sum(prod[:, h * HEAD_DIM:(h + 1) * HEAD_DIM], axis=-1, keepdims=True)
            stats = jnp.where(lane == N_HEADS + h, delta, stats)
        st_ref[0] = stats
        do1_ref[...] = dattn.astype(BF16)
        st1_ref[...] = stats
        _to_pattern(da_ref, do4_ref, d4, BF16)
        _to_pattern(da_ref, do16_ref, d16, BF16)
        _to_pattern(st_ref, st4_ref, d4, F32)
        _to_pattern(st_ref, st16_ref, d16, F32)

        gp = gp_ref[...]
        sig = jax.nn.sigmoid(gp)
        dyp = dy[:, D_ATTN:]
        dpo_ref[...] = dyp * (gp * sig)
        dh_ref[:, D_ATTN:] = (dyp * (mix_ref[:, D_ATTN:] * sc_ref[...])
                              * (sig * (1.0 + gp * (1.0 - sig)))).astype(BF16)

    row = lambda width, cb=0: pl.BlockSpec((tm, width), lambda i: (i, cb))
    pat = lambda d, width: pl.BlockSpec((d, tm // d, width), lambda i: (0, i, 0))
    pat_shape = lambda d, width, dtype: jax.ShapeDtypeStruct((d, seq // d, width), dtype)
    outs = pl.pallas_call(
        body, name="dy_gate_bwd", grid=(seq // tm,),
        in_specs=[row(D_MODEL), pl.BlockSpec((D_MODEL, D_MODEL), lambda i: (0, 0)),
                  row(D_ATTN, 1), row(D_POOL, 2), row(D_MODEL), pl.BlockSpec((1, D_POOL), lambda i: (0, 0)),
                  row(STAT_LANES)],
        out_specs=[row(D_MODEL, D_IN // D_MODEL - 1), row(D_POOL),
                   row(D_ATTN), pat(d4, D_ATTN), pat(d16, D_ATTN),
                   row(STAT_LANES), pat(d4, STAT_LANES), pat(d16, STAT_LANES)],
        out_shape=[jax.ShapeDtypeStruct((seq, D_IN), BF16), jax.ShapeDtypeStruct((seq, D_POOL), F32),
                   jax.ShapeDtypeStruct((seq, D_ATTN), BF16), pat_shape(d4, D_ATTN, BF16), pat_shape(d16, D_ATTN, BF16),
                   jax.ShapeDtypeStruct((seq, STAT_LANES), F32), pat_shape(d4, STAT_LANES, F32),
                   pat_shape(d16, STAT_LANES, F32)],
        scratch_shapes=[pltpu.VMEM((N_HEADS, tm, HEAD_DIM), F32), pltpu.VMEM((1, tm, STAT_LANES), F32)],
        compiler_params=_params(("parallel",), 48),
    )(dzb, w_out_g.reshape(D_MODEL, D_MODEL), hug, hug, mixpre, pool_scale, lse_all)
    dh, dpo, do1, do4, do16, st1, st4, st16 = outs
    return dh, dpo, [do1[None], do4, do16], [st1[None], st4, st16]


def _pool_bwd(dh, dpo, mixpre, pooled, w_pool_g, pool_scale):
    seq = dpo.shape[0]
    tm = 256
    halo_blocks = tm // POOL_HALO
    last = seq // tm - 1
    n_groups = len(POOL_WINDOWS)

    def body(dh_in_ref, dpo_ref, halo_ref, pre_ref, pooled_ref, wp_ref, sc_ref, du_ref, gw_ref, gs_ref):
        i = pl.program_id(0)

        @pl.when(i == 0)
        def _():
            gw_ref[...] = jnp.zeros_like(gw_ref)
            gs_ref[...] = jnp.zeros_like(gs_ref)

        dpo = dpo_ref[...]
        scale = sc_ref[...]
        gs_ref[...] += _fold_rows(dpo * pre_ref[...])
        halo = jnp.where(i < last, halo_ref[...], 0.0)
        dpw = (jnp.concatenate([dpo, halo], axis=0) * scale).astype(BF16)
        pos = i * tm + lax.broadcasted_iota(jnp.int32, (tm + POOL_HALO, 1), 0)
        for g, window in enumerate(POOL_WINDOWS):
            cols = slice(g * POOL_GROUP_DIM, (g + 1) * POOL_GROUP_DIM)
            dpw_g = dpw[:, cols]
            gw_ref[g] += _dot_tn(pooled_ref[:, cols], dpw_g[:tm, :])
            dpooled = _dot_nt(dpw_g, wp_ref[g])
            count = jnp.minimum(pos + 1, window).astype(F32)
            sums = _window_sums(dpooled / count, window, backward=True)
            du_ref[:, cols] = (sums[:tm, :] - dpooled[:tm, :]).astype(BF16)

    row = lambda width, cb=0: pl.BlockSpec((tm, width), lambda i: (i, cb))
    return pl.pallas_call(
        body, name="pool_bwd", grid=(seq // tm,),
        in_specs=[ANY, row(D_POOL),
                  pl.BlockSpec((POOL_HALO, D_POOL),
                               lambda i: (jnp.minimum((i + 1) * halo_blocks, seq // POOL_HALO - 1), 0)),
                  row(D_POOL, 1), row(D_POOL),
                  pl.BlockSpec((n_groups, POOL_GROUP_DIM, POOL_GROUP_DIM), lambda i: (0, 0, 0)),
                  pl.BlockSpec((1, D_POOL), lambda i: (0, 0))],
        out_specs=[row(D_POOL, D_QKV // D_POOL),
                   pl.BlockSpec((n_groups, POOL_GROUP_DIM, POOL_GROUP_DIM), lambda i: (0, 0, 0)),
                   pl.BlockSpec((8, D_POOL), lambda i: (0, 0))],
        out_shape=[jax.ShapeDtypeStruct(dh.shape, dh.dtype),
                   jax.ShapeDtypeStruct((n_groups, POOL_GROUP_DIM, POOL_GROUP_DIM), F32),
                   jax.ShapeDtypeStruct((8, D_POOL), F32)],
        input_output_aliases={0: 0},
        compiler_params=_params(("arbitrary",), 40),
    )(dh, dpo, dpo, mixpre, pooled, w_pool_g, pool_scale)


def _sum_patterns(dh, parts, tabs, unrotate, col_block, name):
    seq = dh.shape[0]
    tm, tn = 256, 512
    per = D_ATTN // tn
    d4, d16 = DILATIONS[1], DILATIONS[2]

    def body(dh_in_ref, a1_ref, a4_ref, a16_ref, ct_ref, up_ref, down_ref, o_ref, n4_ref, n16_ref):
        _from_pattern(a4_ref, n4_ref, d4)
        _from_pattern(a16_ref, n16_ref, d16)
        for s in range(tn // HEAD_DIM):
            cols = slice(s * HEAD_DIM, (s + 1) * HEAD_DIM)
            tot = a1_ref[:, cols] + n4_ref[s] + n16_ref[s]
            if unrotate:
                tot = _rotate_heads(tot, ct_ref[...], -up_ref[...], -down_ref[...])
            o_ref[:, cols] = tot.astype(BF16)

    tab = pl.BlockSpec((tm, HEAD_DIM), lambda i, j: (i, 0))
    pat = lambda d: pl.BlockSpec((d, tm // d, tn), lambda i, j: (0, i, j))
    return pl.pallas_call(
        body, name=name, grid=(seq // tm, per),
        in_specs=[ANY, pl.BlockSpec((tm, tn), lambda i, j: (i, j)), pat(d4), pat(d16), tab, tab, tab],
        out_specs=pl.BlockSpec((tm, tn), lambda i, j: (i, col_block * per + j)),
        out_shape=jax.ShapeDtypeStruct(dh.shape, dh.dtype),
        scratch_shapes=[pltpu.VMEM((tn // HEAD_DIM, tm, HEAD_DIM), F32), pltpu.VMEM((tn // HEAD_DIM, tm, HEAD_DIM), F32)],
        input_output_aliases={0: 0},
        compiler_params=_params(("parallel", "parallel"), 32),
    )(dh, parts[0][0], parts[1], parts[2], *tabs)


def _grad_w_in(x, dh):
    seq = x.shape[0]
    ts, td, te = 512, D_MODEL // 2, SHARD_IN
    nk = seq // ts

    def body(x_ref, dh_ref, o_ref, acc_ref):
        k = pl.program_id(2)

        @pl.when(k == 0)
        def _():
            acc_ref[...] = jnp.zeros_like(acc_ref)

        acc_ref[...] += _dot_tn(x_ref[...].astype(BF16), dh_ref[...])

        @pl.when(k == nk - 1)
        def _():
            o_ref[...] = acc_ref[...]

    return pl.pallas_call(
        body, name="grad_w_in", grid=(N_SHARDS, 2, nk),
        in_specs=[pl.BlockSpec((ts, td), lambda e, d, k: (k, d)), pl.BlockSpec((ts, te), lambda e, d, k: (k, e))],
        out_specs=pl.BlockSpec((None, None, td, te), lambda e, d, k: (e, d, 0, 0)),
        out_shape=jax.ShapeDtypeStruct((N_SHARDS, 2, td, te), F32),
        scratch_shapes=[pltpu.VMEM((td, te), F32)],
        compiler_params=_params(("parallel", "parallel", "arbitrary"), 48),
    )(x, dh)


def _grad_w_out(y, dzb):
    seq = y.shape[0]
    ts, te = 512, 1024
    nk = seq // ts

    def body(y_ref, dz_ref, o_ref, acc_ref):
        k = pl.program_id(1)

        @pl.when(k == 0)
        def _():
            acc_ref[...] = jnp.zeros_like(acc_ref)

        acc_ref[...] += _dot_tn(y_ref[...], dz_ref[...])

        @pl.when(k == nk - 1)
        def _():
            o_ref[...] = acc_ref[...]

    return pl.pallas_call(
        body, name="grad_w_out", grid=(D_MODEL // te, nk),
        in_specs=[pl.BlockSpec((ts, te), lambda e, k: (k, e)), pl.BlockSpec((ts, D_MODEL), lambda e, k: (k, 0))],
        out_specs=pl.BlockSpec((te, D_MODEL), lambda e, k: (e, 0)),
        out_shape=jax.ShapeDtypeStruct((D_MODEL, D_MODEL), F32),
        scratch_shapes=[pltpu.VMEM((te, D_MODEL), F32)],
        compiler_params=_params(("parallel", "arbitrary"), 48),
    )(y, dzb)


def _grad_x(dh, w_in_g, dz):
    seq = dh.shape[0]
    tm, tk = 512, 512
    per_shard = SHARD_IN // tk
    nk = D_IN // tk

    def body(dh_ref, w_ref, dz_ref, o_ref, acc_ref):
        k = pl.program_id(1)

        @pl.when(k == 0)
        def _():
            acc_ref[...] = DEEPNORM_ALPHA * dz_ref[...]

        acc_ref[...] += _dot_nt(dh_ref[...], w_ref[...])

        @pl.when(k == nk - 1)
        def _():
            o_ref[...] = acc_ref[...]

    return pl.pallas_call(
        body, name="grad_x", grid=(seq // tm, nk),
        in_specs=[pl.BlockSpec((tm, tk), lambda i, k: (i, k)),
                  pl.BlockSpec((None, D_MODEL, tk), lambda i, k: (k // per_shard, 0, k % per_shard)),
                  pl.BlockSpec((tm, D_MODEL), lambda i, k: (i, 0))],
        out_specs=pl.BlockSpec((tm, D_MODEL), lambda i, k: (i, 0)),
        out_shape=jax.ShapeDtypeStruct((seq, D_MODEL), F32),
        scratch_shapes=[pltpu.VMEM((tm, D_MODEL), F32)],
        compiler_params=_params(("parallel", "arbitrary"), 48),
    )(dh, w_in_g, dz)


def _local_step(x, target, w_in_g, w_out_g, w_pool_g, pool_scale, gain, bias):
    seq = x.shape[0]
    tabs = _rope_tables(seq)
    qkv = _in_proj_qkv(x, w_in_g, tabs)
    hug = _in_proj_pool_gate(x, w_in_g)
    o_list, st_list = [], []
    for p, dil in enumerate(DILATIONS):
        o, st = _attn_fwd(qkv[p], "attn_fwd_d%d" % dil)
        o_list.append(o)
        st_list.append(st)
    y, mixpre, lse_all, pooled = _mix_gate(o_list, st_list, hug, w_pool_g, pool_scale)
    dz, dzb, gain_part, bias_part, loss_part = _out_proj_loss(y, w_out_g, x, target, gain, bias)
    dh, dpo, do_list, stat_list = _dy_gate_bwd(dzb, w_out_g, hug, mixpre, pool_scale, lse_all)
    g_w_out = _grad_w_out(y, dzb)
    dh, g_w_pool, scale_part = _pool_bwd(dh, dpo, mixpre, pooled, w_pool_g, pool_scale)
    parts = [_attn_bwd(qkv[p], do_list[p], stat_list[p], "attn_bwd_d%d" % dil) for p, dil in enumerate(DILATIONS)]
    dh = _sum_patterns(dh, [t[0] for t in parts], tabs, True, 0, "sum_dq")
    dh = _sum_patterns(dh, [t[1] for t in parts], tabs, True, 1, "sum_dk")
    dh = _sum_patterns(dh, [t[2] for t in parts], tabs, False, 2, "sum_dv")
    g_w_in = _grad_w_in(x, dh)
    g_x = _grad_x(dh, w_in_g, dz)
    small = jnp.concatenate([scale_part, gain_part, bias_part, loss_part], axis=1)
    return g_x, g_w_in, g_w_out, g_w_pool, small


def _exchange_halves(grads):
    n = len(grads)

    def body(*refs):
        src = refs[:n]
        dst = refs[n:2 * n]
        send_sems, recv_sems = refs[2 * n:]
        x, y, c = lax.axis_index("x"), lax.axis_index("y"), lax.axis_index("c")
        copies = []
        for a in range(n):
            for j in range(N_SHARDS):
                cp = pltpu.make_async_remote_copy(
                    src_ref=src[a].at[j, 1 - c], dst_ref=dst[a].at[j],
                    send_sem=send_sems.at[a, j], recv_sem=recv_sems.at[a, j],
                    device_id=(x, y, 1 - c), device_id_type=MESH)
                cp.start()
                copies.append(cp)
        for cp in copies:
            cp.wait()

    return pl.pallas_call(
        body, name="exchange_halves",
        in_specs=[ANY] * n, out_specs=[ANY] * n,
        out_shape=[jax.ShapeDtypeStruct((N_SHARDS,) + g.shape[2:], g.dtype) for g in grads],
        scratch_shapes=[pltpu.SemaphoreType.DMA((n, N_SHARDS)), pltpu.SemaphoreType.DMA((n, N_SHARDS))],
    )(*grads)


def _add_own_half(grad, recv, core, name):
    _, _, r, c = grad.shape
    tr = min(r, 256)

    def body(core_ref, g_ref, r_ref, o_ref, ob_ref):
        tot = g_ref[...] + r_ref[...]
        o_ref[...] = tot
        ob_ref[...] = tot.astype(BF16)

    out = pl.BlockSpec((None, tr, c), lambda j, i, core_ref: (j, i, 0))
    return pl.pallas_call(
        body, name=name,
        grid_spec=pltpu.PrefetchScalarGridSpec(
            num_scalar_prefetch=1, grid=(N_SHARDS, r // tr),
            in_specs=[pl.BlockSpec((None, None, tr, c), lambda j, i, core_ref: (j, core_ref[0], i, 0)),
                      pl.BlockSpec((None, tr, c), lambda j, i, core_ref: (j, i, 0))],
            out_specs=[out, out]),
        out_shape=[jax.ShapeDtypeStruct((N_SHARDS, r, c), F32), jax.ShapeDtypeStruct((N_SHARDS, r, c), BF16)],
        compiler_params=_params(("parallel", "parallel"), 32),
    )(core, grad, recv)


def _scatter_to_chips(sums):
    n = len(sums)

    def body(*refs):
        src = refs[:n]
        dst = refs[n:2 * n]
        send_sems, recv_sems = refs[2 * n:]
        x, y, c = lax.axis_index("x"), lax.axis_index("y"), lax.axis_index("c")
        chips = [(1 - x, y), (x, 1 - y), (1 - x, 1 - y)]
        copies = []
        for a in range(n):
            for k, (cx, cy) in enumerate(chips):
                cp = pltpu.make_async_remote_copy(
                    src_ref=src[a].at[2 * cx + cy], dst_ref=dst[a].at[k],
                    send_sem=send_sems.at[a, k], recv_sem=recv_sems.at[a, k],
                    device_id=(cx, cy, c), device_id_type=MESH)
                cp.start()
                copies.append(cp)
        for cp in copies:
            cp.wait()

    return pl.pallas_call(
        body, name="scatter_to_chips",
        in_specs=[ANY] * n, out_specs=[ANY] * n,
        out_shape=[jax.ShapeDtypeStruct((3,) + s.shape[1:], s.dtype) for s in sums],
        scratch_shapes=[pltpu.SemaphoreType.DMA((n, 3)), pltpu.SemaphoreType.DMA((n, 3))],
    )(*sums)


def _add_chips(sums, recv, chip_core, name):
    _, r, c = sums.shape
    tr = min(r, 256)

    def body(cc_ref, s_ref, r_ref, o_ref):
        o_ref[...] = ((s_ref[...] + r_ref[0].astype(F32)) + r_ref[1].astype(F32)) + r_ref[2].astype(F32)

    return pl.pallas_call(
        body, name=name,
        grid_spec=pltpu.PrefetchScalarGridSpec(
            num_scalar_prefetch=1, grid=(r // tr,),
            in_specs=[pl.BlockSpec((None, tr, c), lambda i, cc_ref: (cc_ref[0], i, 0)),
                      pl.BlockSpec((3, tr, c), lambda i, cc_ref: (0, i, 0))],
            out_specs=pl.BlockSpec((None, tr, c), lambda i, cc_ref: (cc_ref[1], i, 0))),
        out_shape=jax.ShapeDtypeStruct((2, r, c), F32),
        compiler_params=_params(("parallel",), 32),
    )(chip_core, sums, recv)


def _share_with_sibling(bufs):
    n = len(bufs)

    def body(*refs):
        dst = refs[n:2 * n]
        send_sems, recv_sems = refs[2 * n:]
        x, y, c = lax.axis_index("x"), lax.axis_index("y"), lax.axis_index("c")
        copies = []
        for a in range(n):
            cp = pltpu.make_async_remote_copy(
                src_ref=dst[a].at[c], dst_ref=dst[a].at[c], send_sem=send_sems.at[a], recv_sem=recv_sems.at[a],
                device_id=(x, y, 1 - c), device_id_type=MESH)
            cp.start()
            copies.append(cp)
        for a in range(n):
            other = dst[a].at[1 - c]
            pltpu.make_async_remote_copy(
                src_ref=other, dst_ref=other, send_sem=send_sems.at[a], recv_sem=recv_sems.at[a],
                device_id=(x, y, 1 - c), device_id_type=MESH).wait_recv()
        for cp in copies:
            cp.wait_send()

    return pl.pallas_call(
        body, name="share_with_sibling",
        in_specs=[ANY] * n, out_specs=[ANY] * n,
        out_shape=[jax.ShapeDtypeStruct(b.shape, b.dtype) for b in bufs],
        input_output_aliases={a: a for a in range(n)},
        scratch_shapes=[pltpu.SemaphoreType.DMA((n,)), pltpu.SemaphoreType.DMA((n,))],
    )(*bufs)


def _adam_math(w, g, m, v):
    m = ADAM_B1 * m + (1.0 - ADAM_B1) * g
    v = ADAM_B2 * v + (1.0 - ADAM_B2) * (g * g)
    m_hat = m / (1.0 - ADAM_B1 ** ADAM_STEP)
    v_hat = v / (1.0 - ADAM_B2 ** ADAM_STEP)
    delta = -ADAM_LR * (m_hat / (jnp.sqrt(v_hat) + ADAM_EPS) + ADAM_WD * w)
    return delta, m, v


def _small_allreduce_adamw(small, w_vec, m_vec, v_vec):
    width = small.shape[1]
    n_par = w_vec.shape[1]

    def body(s_ref, w_ref, m_ref, v_ref, loss_ref, g_ref, d_ref, nm_ref, nv_ref, gather_ref, send_sems, recv_sems):
        x, y, c = lax.axis_index("x"), lax.axis_index("y"), lax.axis_index("c")
        me = 4 * x + 2 * y + c
        gather_ref[me] = s_ref[...]
        copies = []
        for r in range(1, 8):
            bx, by, bc = (r >> 2) & 1, (r >> 1) & 1, r & 1
            peer = (x ^ bx, y ^ by, c ^ bc)
            cp = pltpu.make_async_remote_copy(
                src_ref=s_ref, dst_ref=gather_ref.at[me], send_sem=send_sems.at[r - 1], recv_sem=recv_sems.at[r - 1],
                device_id=peer, device_id_type=MESH)
            cp.start()
            copies.append(cp)
        for r in range(1, 8):
            bx, by, bc = (r >> 2) & 1, (r >> 1) & 1, r & 1
            theirs = gather_ref.at[4 * (x ^ bx) + 2 * (y ^ by) + (c ^ bc)]
            pltpu.make_async_remote_copy(
                src_ref=theirs, dst_ref=theirs, send_sem=send_sems.at[r - 1], recv_sem=recv_sems.at[r - 1],
                device_id=(x ^ bx, y ^ by, c ^ bc), device_id_type=MESH).wait_recv()
        for cp in copies:
            cp.wait_send()
        tot = gather_ref[0]
        for d in range(1, 8):
            tot = tot + gather_ref[d]
        tot = jnp.sum(tot, axis=0, keepdims=True)
        sq = jnp.sum(tot[:, n_par:], axis=1, keepdims=True)
        loss_ref[...] = jnp.broadcast_to(sq * (0.5 / D_MODEL), loss_ref.shape)
        g = tot[:, :n_par]
        g_ref[...] = g
        d_ref[...], nm_ref[...], nv_ref[...] = _adam_math(w_ref[...], g, m_ref[...], v_ref[...])

    vm = pl.BlockSpec(memory_space=pltpu.VMEM)
    vec = jax.ShapeDtypeStruct((1, n_par), F32)
    return pl.pallas_call(
        body, name="small_allreduce_adamw",
        in_specs=[vm] * 4, out_specs=[vm] * 5,
        out_shape=[jax.ShapeDtypeStruct((1, 128), F32), vec, vec, vec, vec],
        scratch_shapes=[pltpu.VMEM((8, 8, width), F32), pltpu.SemaphoreType.DMA((7,)), pltpu.SemaphoreType.DMA((7,))],
    )(small, w_vec, m_vec, v_vec)


def _adamw(w, g, m, v, name):
    r, c = w.shape
    tr = min(r, 256)

    def body(w_ref, g_ref, m_ref, v_ref, d_ref, nm_ref, nv_ref):
        d_ref[...], nm_ref[...], nv_ref[...] = _adam_math(w_ref[...], g_ref[...], m_ref[...], v_ref[...])

    spec = pl.BlockSpec((tr, c), lambda i: (i, 0))
    shape = jax.ShapeDtypeStruct((r, c), F32)
    return pl.pallas_call(
        body, name=name, grid=(r // tr,),
        in_specs=[spec] * 4, out_specs=[spec] * 3, out_shape=[shape] * 3,
        compiler_params=_params(("parallel",), 32),
    )(w, g, m, v)


def kernel(x, w_in, w_pool, pool_scale, w_out, ln_gain, ln_bias, loss_target, m_w_in, m_w_pool, m_pool_scale, m_w_out, m_ln_gain, m_ln_bias, v_w_in, v_w_pool, v_pool_scale, v_w_out, v_ln_gain, v_ln_bias):
    xi, yi, ci = lax.axis_index("x"), lax.axis_index("y"), lax.axis_index("c")
    chip = (2 * xi + yi).astype(jnp.int32).reshape(1)
    core = ci.astype(jnp.int32).reshape(1)
    n_groups = len(POOL_WINDOWS)
    shard_c = w_pool.shape[2]

    w_in_b = _cast_bf16(w_in[0], chip, "cast_w_in", 256)
    w_out_b = _cast_bf16(w_out[0], chip, "cast_w_out", 256)
    w_pool_b = _cast_bf16(w_pool[0].reshape(n_groups * shard_c, POOL_GROUP_DIM), chip, "cast_w_pool", 256)
    w_in_g, w_out_g, w_pool_sh = _allgather_weights([w_in_b, w_out_b, w_pool_b])
    w_pool_g = (w_pool_sh.reshape(N_SHARDS, n_groups, shard_c, POOL_GROUP_DIM).transpose(1, 0, 2, 3)
                .reshape(n_groups, POOL_GROUP_DIM, POOL_GROUP_DIM))

    g_x, g_w_in, g_w_out, g_w_pool, small = _local_step(
        x[0], loss_target[0], w_in_g, w_out_g, w_pool_g, pool_scale, ln_gain, ln_bias)

    half_c = shard_c // 2
    g_w_out_p = g_w_out.reshape(N_SHARDS, 2, D_MODEL // (2 * N_SHARDS), D_MODEL)
    g_w_pool_p = (g_w_pool.reshape(n_groups, N_SHARDS, 2, half_c, POOL_GROUP_DIM).transpose(1, 2, 0, 3, 4)
                  .reshape(N_SHARDS, 2, n_groups * half_c, POOL_GROUP_DIM))
    grads = [g_w_in, g_w_out_p, g_w_pool_p]
    recv = _exchange_halves(grads)
    sums = [_add_own_half(g, r, core, "add_own_half_%d" % a) for a, (g, r) in enumerate(zip(grads, recv))]
    recv = _scatter_to_chips([s[1] for s in sums])
    chip_core = jnp.concatenate([chip, core])
    reduced = [_add_chips(s[0], r, chip_core, "add_chips_%d" % a) for a, (s, r) in enumerate(zip(sums, recv))]
    full = _share_with_sibling(reduced)
    grad_w_in = full[0].reshape(D_MODEL, SHARD_IN)
    grad_w_out = full[1].reshape(D_MODEL // N_SHARDS, D_MODEL)
    grad_w_pool = (full[2].reshape(2, n_groups, half_c, POOL_GROUP_DIM).transpose(1, 0, 2, 3)
                   .reshape(n_groups * shard_c, POOL_GROUP_DIM))

    d_in, nm_in, nv_in = _adamw(w_in[0], grad_w_in, m_w_in[0], v_w_in[0], "adamw_w_in")
    d_out, nm_out, nv_out = _adamw(w_out[0], grad_w_out, m_w_out[0], v_w_out[0], "adamw_w_out")
    flat = lambda t: t[0].reshape(n_groups * shard_c, POOL_GROUP_DIM)
    d_pool, nm_pool, nv_pool = _adamw(flat(w_pool), grad_w_pool, flat(m_w_pool), flat(v_w_pool), "adamw_w_pool")

    cat = lambda a, b, c: jnp.concatenate([a, b, c], axis=1)
    loss_v, g_vec, d_vec, nm_vec, nv_vec = _small_allreduce_adamw(
        small, cat(pool_scale, ln_gain, ln_bias), cat(m_pool_scale, m_ln_gain, m_ln_bias),
        cat(v_pool_scale, v_ln_gain, v_ln_bias))

    def split(vec):
        return vec[:, :D_POOL], vec[:, D_POOL:D_POOL + D_MODEL], vec[:, D_POOL + D_MODEL:]

    g_scale, g_gain, g_bias = split(g_vec)
    d_scale, d_gain, d_bias = split(d_vec)
    nm_scale, nm_gain, nm_bias = split(nm_vec)
    nv_scale, nv_gain, nv_bias = split(nv_vec)
    pool_shape = w_pool.shape
    return (loss_v[0, 0], g_x[None],
            grad_w_in[None], grad_w_pool.reshape(pool_shape), g_scale, grad_w_out[None], g_gain, g_bias,
            d_in[None], d_pool.reshape(pool_shape), d_scale, d_out[None], d_gain, d_bias,
            nm_in[None], nm_pool.reshape(pool_shape), nm_scale, nm_out[None], nm_gain, nm_bias,
            nv_in[None], nv_pool.reshape(pool_shape), nv_scale, nv_out[None], nv_gain, nv_bias)
```

```python
import functools

import jax
import jax.numpy as jnp
from jax import lax
from jax.experimental import pallas as pl
from jax.experimental.pallas import tpu as pltpu

F32 = jnp.float32
BF16 = jnp.bfloat16
MESH = pl.DeviceIdType.MESH
ANY = pl.BlockSpec(memory_space=pl.ANY)

D_MODEL = 2048
D_ATTN = 1024
D_POOL = 1024
HEAD_DIM = 128
N_HEADS = 8
ROPE_DIM = 32
ROPE_THETA = 500000.0
DILATIONS = (1, 4, 16)
KEY_BLOCK = 128
CHUNK = 2 * KEY_BLOCK
STAT_LANES = 128
POOL_WINDOWS = (2, 4, 8, 16)
POOL_GROUP_DIM = 256
POOL_HALO = 16
D_QKV = 3 * D_ATTN
D_UG = D_POOL + D_MODEL
D_IN = D_QKV + D_UG
N_SHARDS = 4
SHARD_IN = D_IN // N_SHARDS
LN_EPS = 1e-5
DEEPNORM_ALPHA = 2.0 ** 0.25
ADAM_LR = 0.001
ADAM_B1 = 0.9
ADAM_B2 = 0.999
ADAM_EPS = 1e-08
ADAM_WD = 0.01
ADAM_STEP = 10
NEG = -1e30
MIB = 1024 * 1024


def _params(sem, vmem_mib):
    return pltpu.CompilerParams(dimension_semantics=sem, vmem_limit_bytes=vmem_mib * MIB)


class _Exchange:
    def __init__(self, operands, out_shape, aliases, sems, start, finish):
        self.operands, self.out_shape, self.aliases, self.sems = list(operands), list(out_shape), dict(aliases), list(sems)
        self.start, self.finish = start, finish


def _run_exchange(comm, name):
    n_in, n_out = len(comm.operands), len(comm.out_shape)

    def body(*refs):
        ins, outs, sems = refs[:n_in], refs[n_in:n_in + n_out], refs[n_in + n_out:]
        comm.start(ins, outs, sems)
        comm.finish(ins, outs, sems)

    return pl.pallas_call(
        body, name=name, in_specs=[ANY] * n_in, out_specs=[ANY] * n_out, out_shape=comm.out_shape,
        input_output_aliases=comm.aliases, scratch_shapes=comm.sems,
    )(*comm.operands)


def _call(body, *, name, grid, in_specs, out_specs, out_shape, scratch_shapes, semantics, vmem_mib, args,
          aliases=None, comm=None):
    aliases = dict(aliases or {})
    if comm is None:
        outs = pl.pallas_call(
            body, name=name, grid=grid, in_specs=in_specs, out_specs=out_specs, out_shape=out_shape,
            scratch_shapes=scratch_shapes, input_output_aliases=aliases, compiler_params=_params(semantics, vmem_mib),
        )(*args)
        return list(outs), []
    n_in, n_out, n_scr = len(in_specs), len(out_specs), len(scratch_shapes)
    c_in, c_out = len(comm.operands), len(comm.out_shape)

    def hosted(*refs):
        a = n_in
        b = a + c_in
        c = b + n_out
        d = c + c_out
        e = d + n_scr
        ids = [pl.program_id(k) for k in range(len(grid))]
        first = functools.reduce(jnp.logical_and, [i == 0 for i in ids])
        last = functools.reduce(jnp.logical_and, [i == g - 1 for i, g in zip(ids, grid)])

        @pl.when(first)
        def _():
            comm.start(refs[a:b], refs[c:d], refs[e:])

        body(*refs[:a], *refs[b:c], *refs[d:e])

        @pl.when(last)
        def _():
            comm.finish(refs[a:b], refs[c:d], refs[e:])

    for i, o in comm.aliases.items():
        aliases[n_in + i] = n_out + o
    outs = pl.pallas_call(
        hosted, name=name, grid=grid, in_specs=list(in_specs) + [ANY] * c_in, out_specs=list(out_specs) + [ANY] * c_out,
        out_shape=list(out_shape) + comm.out_shape, scratch_shapes=list(scratch_shapes) + comm.sems,
        input_output_aliases=aliases, compiler_params=_params(("arbitrary",) * len(grid), vmem_mib),
    )(*args, *comm.operands)
    return list(outs[:n_out]), list(outs[n_out:])


def _dot_nn(a, b):
    return jnp.dot(a, b, preferred_element_type=F32)


def _dot_nt(a, b):
    return lax.dot_general(a, b, (((1,), (1,)), ((), ())), preferred_element_type=F32)


def _dot_tn(a, b):
    return lax.dot_general(a, b, (((0,), (0,)), ((), ())), preferred_element_type=F32)


def _fold_rows(a):
    r, c = a.shape
    return jnp.sum(a.reshape(r // 8, 8, c), axis=0)


def _cast_bf16(a, chip, name, rows):
    r, c = a.shape

    def body(chip_ref, a_ref, o_ref):
        o_ref[...] = a_ref[...].astype(BF16)

    return pl.pallas_call(
        body, name=name,
        grid_spec=pltpu.PrefetchScalarGridSpec(
            num_scalar_prefetch=1, grid=(r // rows,),
            in_specs=[pl.BlockSpec((rows, c), lambda i, chip_ref: (i, 0))],
            out_specs=pl.BlockSpec((None, rows, c), lambda i, chip_ref: (chip_ref[0], i, 0))),
        out_shape=jax.ShapeDtypeStruct((N_SHARDS, r, c), BF16),
        compiler_params=_params(("parallel",), 32),
    )(chip, a)


def _mesh_place():
    x, y, c = lax.axis_index("x"), lax.axis_index("y"), lax.axis_index("c")
    return x, y, c, [(1 - x, y), (x, 1 - y), (1 - x, 1 - y)]


def _remote(src, dst, send_sem, recv_sem, to):
    return pltpu.make_async_remote_copy(src_ref=src, dst_ref=dst, send_sem=send_sem, recv_sem=recv_sem,
                                        device_id=to, device_id_type=MESH)


def _allgather_weights(bufs):
    n = len(bufs)

    def half(a, core):
        rows = bufs[a].shape[1] // 2
        return pl.ds(core * rows, rows)

    def ici_copies(dst, sems):
        x, y, c, chips = _mesh_place()
        own = lambda a: dst[a].at[2 * x + y, half(a, c)]
        return [_remote(own(a), own(a), sems[0].at[a, k], sems[1].at[a, k], (cx, cy, c))
                for a in range(n) for k, (cx, cy) in enumerate(chips)]

    def start(ins, dst, sems):
        for cp in ici_copies(dst, sems):
            cp.start()

    def finish(ins, dst, sems):
        x, y, c, chips = _mesh_place()
        sibling = (x, y, 1 - c)
        passed_on = []
        for k, (cx, cy) in enumerate(chips):
            for a in range(n):
                landed = dst[a].at[2 * cx + cy, half(a, c)]
                _remote(landed, landed, sems[0].at[a, k], sems[1].at[a, k], (cx, cy, c)).wait_recv()
                cp = _remote(landed, landed, sems[2].at[a, k], sems[3].at[a, k], sibling)
                cp.start()
                passed_on.append(cp)
        for k, (cx, cy) in enumerate(chips):
            for a in range(n):
                passed = dst[a].at[2 * cx + cy, half(a, 1 - c)]
                _remote(passed, passed, sems[2].at[a, k], sems[3].at[a, k], sibling).wait_recv()
        for cp in ici_copies(dst, sems) + passed_on:
            cp.wait_send()

    return _Exchange(bufs, [jax.ShapeDtypeStruct(b.shape, b.dtype) for b in bufs], {a: a for a in range(n)},
                     [pltpu.SemaphoreType.DMA((n, 3))] * 4, start, finish)


def _rope_tables(seq):
    half = ROPE_DIM // 2
    inv_freq = ROPE_THETA ** (-(2.0 * jnp.arange(half, dtype=F32)) / ROPE_DIM)
    ang = jnp.arange(seq, dtype=jnp.int32).astype(F32)[:, None] * inv_freq[None, :]
    cos, sin = jnp.cos(ang), jnp.sin(ang)
    pad = jnp.zeros((seq, HEAD_DIM - ROPE_DIM), F32)
    zeros = jnp.zeros((seq, half), F32)
    c_tab = jnp.concatenate([cos, cos, pad + 1.0], axis=1)
    up_tab = jnp.concatenate([-sin, zeros, pad], axis=1)
    down_tab = jnp.concatenate([zeros, sin, pad], axis=1)
    return c_tab, up_tab, down_tab


def _rotate_heads(t, c_tab, up_tab, down_tab):
    outs = []
    for h in range(t.shape[1] // HEAD_DIM):
        th = t[:, h * HEAD_DIM:(h + 1) * HEAD_DIM]
        up = pltpu.roll(th, HEAD_DIM - ROPE_DIM // 2, axis=1)
        down = pltpu.roll(th, ROPE_DIM // 2, axis=1)
        outs.append(th * c_tab + up * up_tab + down * down_tab)
    return outs[0] if len(outs) == 1 else jnp.concatenate(outs, axis=1)


def _to_pattern(slabs_ref, dst_ref, dil, dtype):
    n_slabs, rows, _ = slabs_ref.shape
    for s in range(n_slabs):
        for r in range(dil):
            dst_ref[r, :, s * 128:(s + 1) * 128] = slabs_ref[s, pl.ds(r, rows // dil, dil), :].astype(dtype)


def _from_pattern(src_ref, slabs_ref, dil):
    n_slabs, rows, _ = slabs_ref.shape
    for s in range(n_slabs):
        for r in range(dil):
            slabs_ref[s, pl.ds(r, rows // dil, dil), :] = src_ref[r, :, s * 128:(s + 1) * 128]


def _store_slabs(slabs_ref, value):
    for s in range(slabs_ref.shape[0]):
        slabs_ref[s] = value[:, s * 128:(s + 1) * 128]


def _in_proj_qkv(x, w_in_g, tabs, comm=None):
    seq = x.shape[0]
    tm, tn = 512, 512
    per_shard = SHARD_IN // tn
    n_tiles = D_QKV // tn
    n_rot_tiles = 2 * D_ATTN // tn
    d4, d16 = DILATIONS[1], DILATIONS[2]

    def body(x_ref, w_ref, c_ref, up_ref, down_ref, o1_ref, o4_ref, o16_ref, xb_ref, res_ref):
        j = pl.program_id(1)

        @pl.when(j == 0)
        def _():
            xb_ref[...] = x_ref[...].astype(BF16)

        acc = _dot_nn(xb_ref[...], w_ref[...])

        @pl.when(j < n_rot_tiles)
        def _():
            res = _rotate_heads(acc, c_ref[...], up_ref[...], down_ref[...])
            o1_ref[...] = res.astype(BF16)
            _store_slabs(res_ref, res)

        @pl.when(j >= n_rot_tiles)
        def _():
            o1_ref[...] = acc.astype(BF16)
            _store_slabs(res_ref, acc)

        _to_pattern(res_ref, o4_ref, d4, BF16)
        _to_pattern(res_ref, o16_ref, d16, BF16)

    tab_spec = pl.BlockSpec((tm, HEAD_DIM), lambda i, j: (i, 0))
    (o1, o4, o16), exchanged = _call(
        body, name="in_proj_qkv", grid=(seq // tm, n_tiles),
        in_specs=[pl.BlockSpec((tm, D_MODEL), lambda i, j: (i, 0)),
                  pl.BlockSpec((None, D_MODEL, tn), lambda i, j: (j // per_shard, 0, j % per_shard)),
                  tab_spec, tab_spec, tab_spec],
        out_specs=[pl.BlockSpec((tm, tn), lambda i, j: (i, j)),
                   pl.BlockSpec((d4, tm // d4, tn), lambda i, j: (0, i, j)),
                   pl.BlockSpec((d16, tm // d16, tn), lambda i, j: (0, i, j))],
        out_shape=[jax.ShapeDtypeStruct((seq, D_QKV), BF16),
                   jax.ShapeDtypeStruct((d4, seq // d4, D_QKV), BF16),
                   jax.ShapeDtypeStruct((d16, seq // d16, D_QKV), BF16)],
        scratch_shapes=[pltpu.VMEM((tm, D_MODEL), BF16), pltpu.VMEM((tn // 128, tm, 128), F32)],
        semantics=("parallel", "arbitrary"), vmem_mib=40, args=(x, w_in_g, *tabs), comm=comm)
    return [o1[None], o4, o16], exchanged


def _in_proj_pool_gate(x, w_in_g):
    seq = x.shape[0]
    tm, tn = 512, 512
    per_shard = SHARD_IN // tn
    first_tile = D_QKV // tn

    def body(x_ref, w_ref, o_ref, xb_ref):
        @pl.when(pl.program_id(1) == 0)
        def _():
            xb_ref[...] = x_ref[...].astype(BF16)

        o_ref[...] = _dot_nn(xb_ref[...], w_ref[...])

    return pl.pallas_call(
        body, name="in_proj_pool_gate", grid=(seq // tm, D_UG // tn),
        in_specs=[pl.BlockSpec((tm, D_MODEL), lambda i, j: (i, 0)),
                  pl.BlockSpec((None, D_MODEL, tn),
                               lambda i, j: ((j + first_tile) // per_shard, 0, (j + first_tile) % per_shard))],
        out_specs=pl.BlockSpec((tm, tn), lambda i, j: (i, j)),
        out_shape=jax.ShapeDtypeStruct((seq, D_UG), F32),
        scratch_shapes=[pltpu.VMEM((tm, D_MODEL), BF16)],
        compiler_params=_params(("parallel", "arbitrary"), 40),
    )(x, w_in_g)


def _band_masks():
    row = lax.broadcasted_iota(jnp.int32, (KEY_BLOCK, KEY_BLOCK), 0)
    col = lax.broadcasted_iota(jnp.int32, (KEY_BLOCK, KEY_BLOCK), 1)
    return col <= row, col >= row


def _attn_fwd(qkv, name):
    dil, n, _ = qkv.shape
    scale = HEAD_DIM ** -0.5
    lo, hi = slice(0, KEY_BLOCK), slice(KEY_BLOCK, CHUNK)

    def body(q_ref, k_ref, v_ref, kb_ref, vb_ref, o_ref, st_ref):
        i = pl.program_id(1)
        cur_mask, prev_mask = _band_masks()
        before_mask = jnp.logical_and(prev_mask, i > 0)
        lane = lax.broadcasted_iota(jnp.int32, (KEY_BLOCK, STAT_LANES), 1)
        for rows in (lo, hi):
            stats = jnp.zeros((KEY_BLOCK, STAT_LANES), F32)
            for h in range(N_HEADS):
                cols = slice(h * HEAD_DIM, (h + 1) * HEAD_DIM)
                q = q_ref[rows, cols]
                if rows is lo:
                    k_prev, v_prev, mask = kb_ref[:, cols], vb_ref[:, cols], before_mask
                else:
                    k_prev, v_prev, mask = k_ref[lo, cols], v_ref[lo, cols], prev_mask
                s_prev = jnp.where(mask, _dot_nt(q, k_prev) * scale, NEG)
                s_cur = jnp.where(cur_mask, _dot_nt(q, k_ref[rows, cols]) * scale, NEG)
                m = jnp.maximum(jnp.max(s_cur, axis=-1, keepdims=True), jnp.max(s_prev, axis=-1, keepdims=True))
                p_cur = jnp.exp(s_cur - m)
                p_prev = jnp.exp(s_prev - m)
                den = jnp.sum(p_cur, axis=-1, keepdims=True) + jnp.sum(p_prev, axis=-1, keepdims=True)
                o = _dot_nn(p_cur.astype(BF16), v_ref[rows, cols]) + _dot_nn(p_prev.astype(BF16), v_prev)
                o_ref[rows, cols] = o / den
                stats = jnp.where(lane == h, m + jnp.log(den), stats)
            st_ref[rows, :] = stats

    main = lambda cb: pl.BlockSpec((None, CHUNK, D_ATTN), lambda r, i: (r, i, cb))
    before = lambda cb: pl.BlockSpec((None, KEY_BLOCK, D_ATTN), lambda r, i: (r, jnp.maximum(2 * i - 1, 0), cb))
    return pl.pallas_call(
        body, name=name, grid=(dil, n // CHUNK),
        in_specs=[main(0), main(1), main(2), before(1), before(2)],
        out_specs=[main(0), pl.BlockSpec((None, CHUNK, STAT_LANES), lambda r, i: (r, i, 0))],
        out_shape=[jax.ShapeDtypeStruct((dil, n, D_ATTN), F32), jax.ShapeDtypeStruct((dil, n, STAT_LANES), F32)],
        compiler_params=_params(("parallel", "parallel"), 40),
    )(qkv, qkv, qkv, qkv, qkv)


def _attn_bwd(qkv, do, stats, name, comm=None):
    dil, n, _ = qkv.shape
    n_blocks = n // KEY_BLOCK
    last = n // CHUNK - 1
    scale = HEAD_DIM ** -0.5
    lo, hi = slice(0, KEY_BLOCK), slice(KEY_BLOCK, CHUNK)

    def body(q_ref, k_ref, v_ref, kb_ref, vb_ref, qa_ref, do_ref, doa_ref, st_ref, sta_ref, dq_ref, dk_ref, dv_ref):
        i = pl.program_id(1)
        cur_mask, prev_mask = _band_masks()
        before_mask = jnp.logical_and(prev_mask, i > 0)
        after_mask = jnp.logical_and(prev_mask, i < last)

        def pair(q, k, v, do_b, lse, delta, mask):
            p = jnp.exp(jnp.where(mask, _dot_nt(q, k) * scale, NEG) - lse)
            ds = p * (_dot_nt(do_b, v) - delta) * scale
            return p.astype(BF16), ds.astype(BF16)

        for h in range(N_HEADS):
            cols = slice(h * HEAD_DIM, (h + 1) * HEAD_DIM)
            lse_c, del_c = slice(h, h + 1), slice(N_HEADS + h, N_HEADS + h + 1)
            q0, q1, qa = q_ref[lo, cols], q_ref[hi, cols], qa_ref[:, cols]
            k0, k1, kb = k_ref[lo, cols], k_ref[hi, cols], kb_ref[:, cols]
            v0, v1, vb = v_ref[lo, cols], v_ref[hi, cols], vb_ref[:, cols]
            do0, do1, doa = do_ref[lo, cols], do_ref[hi, cols], doa_ref[:, cols]
            st0 = (st_ref[lo, lse_c], st_ref[lo, del_c])
            st1 = (st_ref[hi, lse_c], st_ref[hi, del_c])
            sta = (sta_ref[:, lse_c], sta_ref[:, del_c])
            _, ds_0b = pair(q0, kb, vb, do0, *st0, before_mask)
            p_00, ds_00 = pair(q0, k0, v0, do0, *st0, cur_mask)
            p_10, ds_10 = pair(q1, k0, v0, do1, *st1, prev_mask)
            p_11, ds_11 = pair(q1, k1, v1, do1, *st1, cur_mask)
            p_a1, ds_a1 = pair(qa, k1, v1, doa, *sta, after_mask)
            dq_ref[lo, cols] = _dot_nn(ds_0b, kb) + _dot_nn(ds_00, k0)
            dq_ref[hi, cols] = _dot_nn(ds_10, k0) + _dot_nn(ds_11, k1)
            dk_ref[lo, cols] = _dot_tn(ds_00, q0) + _dot_tn(ds_10, q1)
            dk_ref[hi, cols] = _dot_tn(ds_11, q1) + _dot_tn(ds_a1, qa)
            dv_ref[lo, cols] = _dot_tn(p_00, do0) + _dot_tn(p_10, do1)
            dv_ref[hi, cols] = _dot_tn(p_11, do1) + _dot_tn(p_a1, doa)

    def spec(rows, width, row_of, cb):
        return pl.BlockSpec((None, rows, width), lambda r, i: (r, row_of(i), cb))

    same = lambda i: i
    before = lambda i: jnp.maximum(2 * i - 1, 0)
    after = lambda i: jnp.minimum(2 * i + 2, n_blocks - 1)
    out = spec(CHUNK, D_ATTN, same, 0)
    return _call(
        body, name=name, grid=(dil, n // CHUNK),
        in_specs=[spec(CHUNK, D_ATTN, same, 0), spec(CHUNK, D_ATTN, same, 1), spec(CHUNK, D_ATTN, same, 2),
                  spec(KEY_BLOCK, D_ATTN, before, 1), spec(KEY_BLOCK, D_ATTN, before, 2),
                  spec(KEY_BLOCK, D_ATTN, after, 0),
                  spec(CHUNK, D_ATTN, same, 0), spec(KEY_BLOCK, D_ATTN, after, 0),
                  spec(CHUNK, STAT_LANES, same, 0), spec(KEY_BLOCK, STAT_LANES, after, 0)],
        out_specs=[out, out, out],
        out_shape=[jax.ShapeDtypeStruct((dil, n, D_ATTN), F32)] * 3,
        scratch_shapes=[], semantics=("parallel", "parallel"), vmem_mib=40,
        args=(qkv, qkv, qkv, qkv, qkv, qkv, do, do, stats, stats), comm=comm)


def _window_sums(ext, window, backward):
    rows = ext.shape[0]
    acc, span = ext, 1
    while span < window:
        acc = acc + pltpu.roll(acc, (rows - span) if backward else span, axis=0)
        span *= 2
    return acc


def _mix_gate(o_list, st_list, hug, w_pool_g, pool_scale):
    seq = hug.shape[0]
    tm = 256
    halo_blocks = tm // POOL_HALO
    d4, d16 = DILATIONS[1], DILATIONS[2]

    def body(o1_ref, o4_ref, o16_ref, l1_ref, l4_ref, l16_ref, u_ref, halo_ref, ga_ref, gp_ref, wp_ref, sc_ref,
             y_ref, mix_ref, lse_ref, pooled_ref, n4_ref, n16_ref, nl4_ref, nl16_ref):
        i = pl.program_id(0)
        _from_pattern(o4_ref, n4_ref, d4)
        _from_pattern(o16_ref, n16_ref, d16)
        _from_pattern(l4_ref, nl4_ref, d4)
        _from_pattern(l16_ref, nl16_ref, d16)
        la, lb, lc = l1_ref[...], nl4_ref[0], nl16_ref[0]
        mx = jnp.maximum(jnp.maximum(la, lb), lc)
        ea, eb, ec = jnp.exp(la - mx), jnp.exp(lb - mx), jnp.exp(lc - mx)
        tot = ea + eb + ec
        lse_ref[...] = mx + jnp.log(tot)
        wa, wb, wc = ea / tot, eb / tot, ec / tot
        ga = ga_ref[...]
        silu_a = ga * jax.nn.sigmoid(ga)
        for h in range(N_HEADS):
            cols = slice(h * HEAD_DIM, (h + 1) * HEAD_DIM)
            hc = slice(h, h + 1)
            attn = wa[:, hc] * o1_ref[:, cols] + wb[:, hc] * n4_ref[h] + wc[:, hc] * n16_ref[h]
            mix_ref[:, cols] = attn
            y_ref[:, cols] = (attn * silu_a[:, cols]).astype(BF16)

        u = u_ref[...]
        halo = jnp.where(i > 0, halo_ref[...], 0.0)
        ext = jnp.concatenate([halo, u], axis=0)
        pos = i * tm + lax.broadcasted_iota(jnp.int32, (tm, 1), 0)
        gp = gp_ref[...]
        gated_scale = sc_ref[...] * (gp * jax.nn.sigmoid(gp))
        for g, window in enumerate(POOL_WINDOWS):
            cols = slice(g * POOL_GROUP_DIM, (g + 1) * POOL_GROUP_DIM)
            sums = _window_sums(ext[:, cols], window, backward=False)[POOL_HALO:, :]
            count = jnp.minimum(pos + 1, window).astype(F32)
            pooled = (sums / count - u[:, cols]).astype(BF16)
            pooled_ref[:, cols] = pooled
            pre = _dot_nn(pooled, wp_ref[g])
            out_cols = slice(D_ATTN + g * POOL_GROUP_DIM, D_ATTN + (g + 1) * POOL_GROUP_DIM)
            mix_ref[:, out_cols] = pre
            y_ref[:, out_cols] = (pre * gated_scale[:, cols]).astype(BF16)

    row = lambda width, cb=0: pl.BlockSpec((tm, width), lambda i: (i, cb))
    pat = lambda d, width: pl.BlockSpec((d, tm // d, width), lambda i: (0, i, 0))
    return pl.pallas_call(
        body, name="mix_gate", grid=(seq // tm,),
        in_specs=[row(D_ATTN), pat(d4, D_ATTN), pat(d16, D_ATTN),
                  row(STAT_LANES), pat(d4, STAT_LANES), pat(d16, STAT_LANES),
                  row(D_POOL),
                  pl.BlockSpec((POOL_HALO, D_POOL), lambda i: (jnp.maximum(i * halo_blocks - 1, 0), 0)),
                  row(D_ATTN, 1), row(D_POOL, 2),
                  pl.BlockSpec((len(POOL_WINDOWS), POOL_GROUP_DIM, POOL_GROUP_DIM), lambda i: (0, 0, 0)),
                  pl.BlockSpec((1, D_POOL), lambda i: (0, 0))],
        out_specs=[row(D_MODEL), row(D_MODEL), row(STAT_LANES), row(D_POOL)],
        out_shape=[jax.ShapeDtypeStruct((seq, D_MODEL), BF16), jax.ShapeDtypeStruct((seq, D_MODEL), F32),
                   jax.ShapeDtypeStruct((seq, STAT_LANES), F32), jax.ShapeDtypeStruct((seq, D_POOL), BF16)],
        scratch_shapes=[pltpu.VMEM((N_HEADS, tm, HEAD_DIM), F32), pltpu.VMEM((N_HEADS, tm, HEAD_DIM), F32),
                        pltpu.VMEM((1, tm, STAT_LANES), F32), pltpu.VMEM((1, tm, STAT_LANES), F32)],
        compiler_params=_params(("parallel",), 48),
    )(o_list[0][0], o_list[1], o_list[2], st_list[0][0], st_list[1], st_list[2],
      hug, hug, hug, hug, w_pool_g, pool_scale)


def _out_proj_loss(y, w_out_g, x, target, gain, bias):
    seq = x.shape[0]
    tm = 256

    def body(y_ref, w_ref, x_ref, t_ref, g_ref, b_ref, dz_ref, dzb_ref, gg_ref, gb_ref, loss_ref):
        @pl.when(pl.program_id(0) == 0)
        def _():
            gg_ref[...] = jnp.zeros_like(gg_ref)
            gb_ref[...] = jnp.zeros_like(gb_ref)
            loss_ref[...] = jnp.zeros_like(loss_ref)

        z = DEEPNORM_ALPHA * x_ref[...] + _dot_nn(y_ref[...], w_ref[...])
        mu = jnp.mean(z, axis=-1, keepdims=True)
        zc = z - mu
        rstd = lax.rsqrt(jnp.mean(zc * zc, axis=-1, keepdims=True) + LN_EPS)
        xhat = zc * rstd
        gain_v = g_ref[...]
        diff = xhat * gain_v + b_ref[...] - t_ref[...]
        sq = _fold_rows(diff * diff)
        part = sq[:, :128]
        for k in range(1, D_MODEL // 128):
            part = part + sq[:, k * 128:(k + 1) * 128]
        loss_ref[...] += part
        dln = diff * (1.0 / D_MODEL)
        gg_ref[...] += _fold_rows(dln * xhat)
        gb_ref[...] += _fold_rows(dln)
        dxhat = dln * gain_v
        dz = rstd * (dxhat - jnp.mean(dxhat, axis=-1, keepdims=True)
                     - xhat * jnp.mean(dxhat * xhat, axis=-1, keepdims=True))
        dz_ref[...] = dz
        dzb_ref[...] = dz.astype(BF16)

    row = lambda: pl.BlockSpec((tm, D_MODEL), lambda i: (i, 0))
    vec = lambda: pl.BlockSpec((1, D_MODEL), lambda i: (0, 0))
    acc = lambda width: pl.BlockSpec((8, width), lambda i: (0, 0))
    return pl.pallas_call(
        body, name="out_proj_loss", grid=(seq // tm,),
        in_specs=[row(), pl.BlockSpec((D_MODEL, D_MODEL), lambda i: (0, 0)), row(), row(), vec(), vec()],
        out_specs=[row(), row(), acc(D_MODEL), acc(D_MODEL), acc(128)],
        out_shape=[jax.ShapeDtypeStruct((seq, D_MODEL), F32), jax.ShapeDtypeStruct((seq, D_MODEL), BF16),
                   jax.ShapeDtypeStruct((8, D_MODEL), F32), jax.ShapeDtypeStruct((8, D_MODEL), F32),
                   jax.ShapeDtypeStruct((8, 128), F32)],
        compiler_params=_params(("arbitrary",), 48),
    )(y, w_out_g.reshape(D_MODEL, D_MODEL), x, target, gain, bias)


def _dy_gate_bwd(dzb, w_out_g, hug, mixpre, pool_scale, lse_all):
    seq = dzb.shape[0]
    tm = 256
    d4, d16 = DILATIONS[1], DILATIONS[2]

    def body(dz_ref, w_ref, ga_ref, gp_ref, mix_ref, sc_ref, lse_ref,
             dh_ref, dpo_ref, do1_ref, do4_ref, do16_ref, st1_ref, st4_ref, st16_ref, da_ref, st_ref):
        dy = _dot_nt(dz_ref[...], w_ref[...])
        ga = ga_ref[...]
        sig = jax.nn.sigmoid(ga)
        attn = mix_ref[:, :D_ATTN]
        dya = dy[:, :D_ATTN]
        dattn = dya * (ga * sig)
        dh_ref[:, :D_ATTN] = (dya * attn * (sig * (1.0 + ga * (1.0 - sig)))).astype(BF16)
        _store_slabs(da_ref, dattn)
        lane = lax.broadcasted_iota(jnp.int32, (tm, STAT_LANES), 1)
        stats = lse_ref[...]
        prod = dattn * attn
        for h in range(N_HEADS):
            delta = jnp.sum(prod[:, h * HEAD_DIM:(h + 1) * HEAD_DIM], axis=-1, keepdims=True)
            stats = jnp.where(lane == N_HEADS + h, delta, stats)
        st_ref[0] = stats
        do1_ref[...] = dattn.astype(BF16)
        st1_ref[...] = stats
        _to_pattern(da_ref, do4_ref, d4, BF16)
        _to_pattern(da_ref, do16_ref, d16, BF16)
        _to_pattern(st_ref, st4_ref, d4, F32)
        _to_pattern(st_ref, st16_ref, d16, F32)

        gp = gp_ref[...]
        sig = jax.nn.sigmoid(gp)
        dyp = dy[:, D_ATTN:]
        dpo_ref[...] = dyp * (gp * sig)
        dh_ref[:, D_ATTN:] = (dyp * (mix_ref[:, D_ATTN:] * sc_ref[...])
                              * (sig * (1.0 + gp * (1.0 - sig)))).astype(BF16)

    row = lambda width, cb=0: pl.BlockSpec((tm, width), lambda i: (i, cb))
    pat = lambda d, width: pl.BlockSpec((d, tm // d, width), lambda i: (0, i, 0))
    pat_shape = lambda d, width, dtype: jax.ShapeDtypeStruct((d, seq // d, width), dtype)
    outs = pl.pallas_call(
        body, name="dy_gate_bwd", grid=(seq // tm,),
        in_specs=[row(D_MODEL), pl.BlockSpec((D_MODEL, D_MODEL), lambda i: (0, 0)),
                  row(D_ATTN, 1), row(D_POOL, 2), row(D_MODEL), pl.BlockSpec((1, D_POOL), lambda i: (0, 0)),
                  row(STAT_LANES)],
        out_specs=[row(D_MODEL, D_IN // D_MODEL - 1), row(D_POOL),
                   row(D_ATTN), pat(d4, D_ATTN), pat(d16, D_ATTN),
                   row(STAT_LANES), pat(d4, STAT_LANES), pat(d16, STAT_LANES)],
        out_shape=[jax.ShapeDtypeStruct((seq, D_IN), BF16), jax.ShapeDtypeStruct((seq, D_POOL), F32),
                   jax.ShapeDtypeStruct((seq, D_ATTN), BF16), pat_shape(d4, D_ATTN, BF16), pat_shape(d16, D_ATTN, BF16),
                   jax.ShapeDtypeStruct((seq, STAT_LANES), F32), pat_shape(d4, STAT_LANES, F32),
                   pat_shape(d16, STAT_LANES, F32)],
        scratch_shapes=[pltpu.VMEM((N_HEADS, tm, HEAD_DIM), F32), pltpu.VMEM((1, tm, STAT_LANES), F32)],
        compiler_params=_params(("parallel",), 48),
    )(dzb, w_out_g.reshape(D_MODEL, D_MODEL), hug, hug, mixpre, pool_scale, lse_all)
    dh, dpo, do1, do4, do16, st1, st4, st16 = outs
    return dh, dpo, [do1[None], do4, do16], [st1[None], st4, st16]


def _pool_bwd(dh, dpo, mixpre, pooled, w_pool_g, pool_scale):
    seq = dpo.shape[0]
    tm = 256
    halo_blocks = tm // POOL_HALO
    last = seq // tm - 1
    n_groups = len(POOL_WINDOWS)

    def body(dh_in_ref, dpo_ref, halo_ref, pre_ref, pooled_ref, wp_ref, sc_ref, du_ref, gw_ref, gs_ref):
        i = pl.program_id(0)

        @pl.when(i == 0)
        def _():
            gw_ref[...] = jnp.zeros_like(gw_ref)
            gs_ref[...] = jnp.zeros_like(gs_ref)

        dpo = dpo_ref[...]
        scale = sc_ref[...]
        gs_ref[...] += _fold_rows(dpo * pre_ref[...])
        halo = jnp.where(i < last, halo_ref[...], 0.0)
        dpw = (jnp.concatenate([dpo, halo], axis=0) * scale).astype(BF16)
        pos = i * tm + lax.broadcasted_iota(jnp.int32, (tm + POOL_HALO, 1), 0)
        for g, window in enumerate(POOL_WINDOWS):
            cols = slice(g * POOL_GROUP_DIM, (g + 1) * POOL_GROUP_DIM)
            dpw_g = dpw[:, cols]
            gw_ref[g] += _dot_tn(pooled_ref[:, cols], dpw_g[:tm, :])
            dpooled = _dot_nt(dpw_g, wp_ref[g])
            count = jnp.minimum(pos + 1, window).astype(F32)
            sums = _window_sums(dpooled / count, window, backward=True)
            du_ref[:, cols] = (sums[:tm, :] - dpooled[:tm, :]).astype(BF16)

    row = lambda width, cb=0: pl.BlockSpec((tm, width), lambda i: (i, cb))
    return pl.pallas_call(
        body, name="pool_bwd", grid=(seq // tm,),
        in_specs=[ANY, row(D_POOL),
                  pl.BlockSpec((POOL_HALO, D_POOL),
                               lambda i: (jnp.minimum((i + 1) * halo_blocks, seq // POOL_HALO - 1), 0)),
                  row(D_POOL, 1), row(D_POOL),
                  pl.BlockSpec((n_groups, POOL_GROUP_DIM, POOL_GROUP_DIM), lambda i: (0, 0, 0)),
                  pl.BlockSpec((1, D_POOL), lambda i: (0, 0))],
        out_specs=[row(D_POOL, D_QKV // D_POOL),
                   pl.BlockSpec((n_groups, POOL_GROUP_DIM, POOL_GROUP_DIM), lambda i: (0, 0, 0)),
                   pl.BlockSpec((8, D_POOL), lambda i: (0, 0))],
        out_shape=[jax.ShapeDtypeStruct(dh.shape, dh.dtype),
                   jax.ShapeDtypeStruct((n_groups, POOL_GROUP_DIM, POOL_GROUP_DIM), F32),
                   jax.ShapeDtypeStruct((8, D_POOL), F32)],
        input_output_aliases={0: 0},
        compiler_params=_params(("arbitrary",), 40),
    )(dh, dpo, dpo, mixpre, pooled, w_pool_g, pool_scale)


def _sum_patterns(dh, parts, tabs, unrotate, col_block, name):
    seq = dh.shape[0]
    tm, tn = 256, 512
    per = D_ATTN // tn
    d4, d16 = DILATIONS[1], DILATIONS[2]

    def body(dh_in_ref, a1_ref, a4_ref, a16_ref, ct_ref, up_ref, down_ref, o_ref, n4_ref, n16_ref):
        _from_pattern(a4_ref, n4_ref, d4)
        _from_pattern(a16_ref, n16_ref, d16)
        for s in range(tn // HEAD_DIM):
            cols = slice(s * HEAD_DIM, (s + 1) * HEAD_DIM)
            tot = a1_ref[:, cols] + n4_ref[s] + n16_ref[s]
            if unrotate:
                tot = _rotate_heads(tot, ct_ref[...], -up_ref[...], -down_ref[...])
            o_ref[:, cols] = tot.astype(BF16)

    tab = pl.BlockSpec((tm, HEAD_DIM), lambda i, j: (i, 0))
    pat = lambda d: pl.BlockSpec((d, tm // d, tn), lambda i, j: (0, i, j))
    return pl.pallas_call(
        body, name=name, grid=(seq // tm, per),
        in_specs=[ANY, pl.BlockSpec((tm, tn), lambda i, j: (i, j)), pat(d4), pat(d16), tab, tab, tab],
        out_specs=pl.BlockSpec((tm, tn), lambda i, j: (i, col_block * per + j)),
        out_shape=jax.ShapeDtypeStruct(dh.shape, dh.dtype),
        scratch_shapes=[pltpu.VMEM((tn // HEAD_DIM, tm, HEAD_DIM), F32), pltpu.VMEM((tn // HEAD_DIM, tm, HEAD_DIM), F32)],
        input_output_aliases={0: 0},
        compiler_params=_params(("parallel", "parallel"), 32),
    )(dh, parts[0][0], parts[1], parts[2], *tabs)


def _grad_w_in(x, dh):
    seq = x.shape[0]
    ts, td, te = 512, D_MODEL // 2, SHARD_IN
    nk = seq // ts

    def body(x_ref, dh_ref, o_ref, acc_ref):
        k = pl.program_id(2)

        @pl.when(k == 0)
        def _():
            acc_ref[...] = jnp.zeros_like(acc_ref)

        acc_ref[...] += _dot_tn(x_ref[...].astype(BF16), dh_ref[...])

        @pl.when(k == nk - 1)
        def _():
            o_ref[...] = acc_ref[...]

    return pl.pallas_call(
        body, name="grad_w_in", grid=(N_SHARDS, 2, nk),
        in_specs=[pl.BlockSpec((ts, td), lambda e, d, k: (k, d)), pl.BlockSpec((ts, te), lambda e, d, k: (k, e))],
        out_specs=pl.BlockSpec((None, None, td, te), lambda e, d, k: (e, d, 0, 0)),
        out_shape=jax.ShapeDtypeStruct((N_SHARDS, 2, td, te), F32),
        scratch_shapes=[pltpu.VMEM((td, te), F32)],
        compiler_params=_params(("parallel", "parallel", "arbitrary"), 48),
    )(x, dh)


def _grad_w_out(y, dzb):
    seq = y.shape[0]
    ts, te = 512, 1024
    nk = seq // ts

    def body(y_ref, dz_ref, o_ref, acc_ref):
        k = pl.program_id(1)

        @pl.when(k == 0)
        def _():
            acc_ref[...] = jnp.zeros_like(acc_ref)

        acc_ref[...] += _dot_tn(y_ref[...], dz_ref[...])

        @pl.when(k == nk - 1)
        def _():
            o_ref[...] = acc_ref[...]

    return pl.pallas_call(
        body, name="grad_w_out", grid=(D_MODEL // te, nk),
        in_specs=[pl.BlockSpec((ts, te), lambda e, k: (k, e)), pl.BlockSpec((ts, D_MODEL), lambda e, k: (k, 0))],
        out_specs=pl.BlockSpec((te, D_MODEL), lambda e, k: (e, 0)),
        out_shape=jax.ShapeDtypeStruct((D_MODEL, D_MODEL), F32),
        scratch_shapes=[pltpu.VMEM((te, D_MODEL), F32)],
        compiler_params=_params(("parallel", "arbitrary"), 48),
    )(y, dzb)


def _grad_x(dh, w_in_g, dz, part=0, n_parts=1, prev=None, comm=None):
    seq = dh.shape[0]
    tm, tk = 512, 512
    per_shard = SHARD_IN // tk
    nk = D_IN // tk
    tiles = seq // tm // n_parts
    first = part * tiles

    def body(*refs):
        dh_ref, w_ref, dz_ref, o_ref, acc_ref = refs[-5:]
        k = pl.program_id(1)

        @pl.when(k == 0)
        def _():
            acc_ref[...] = DEEPNORM_ALPHA * dz_ref[...]

        acc_ref[...] += _dot_nt(dh_ref[...], w_ref[...])

        @pl.when(k == nk - 1)
        def _():
            o_ref[...] = acc_ref[...]

    carried = [] if prev is None else [prev]
    (g_x,), exchanged = _call(
        body, name="grad_x_%d" % part, grid=(tiles, nk),
        in_specs=[ANY] * len(carried) + [
            pl.BlockSpec((tm, tk), lambda i, k: (i + first, k)),
            pl.BlockSpec((None, D_MODEL, tk), lambda i, k: (k // per_shard, 0, k % per_shard)),
            pl.BlockSpec((tm, D_MODEL), lambda i, k: (i + first, 0))],
        out_specs=[pl.BlockSpec((tm, D_MODEL), lambda i, k: (i + first, 0))],
        out_shape=[jax.ShapeDtypeStruct((seq, D_MODEL), F32)],
        scratch_shapes=[pltpu.VMEM((tm, D_MODEL), F32)],
        semantics=("parallel", "arbitrary"), vmem_mib=48, args=(*carried, dh, w_in_g, dz),
        aliases={0: 0} if carried else None, comm=comm)
    return g_x, exchanged


def _pool_weight(w_pool_sh):
    n_groups = len(POOL_WINDOWS)
    shard_c = POOL_GROUP_DIM // N_SHARDS
    return (w_pool_sh.reshape(N_SHARDS, n_groups, shard_c, POOL_GROUP_DIM).transpose(1, 0, 2, 3)
            .reshape(n_groups, POOL_GROUP_DIM, POOL_GROUP_DIM))


def _pool_grad_pieces(g_w_pool):
    n_groups = len(POOL_WINDOWS)
    half_c = POOL_GROUP_DIM // N_SHARDS // 2
    return (g_w_pool.reshape(n_groups, N_SHARDS, 2, half_c, POOL_GROUP_DIM).transpose(1, 2, 0, 3, 4)
            .reshape(N_SHARDS, 2, n_groups * half_c, POOL_GROUP_DIM))


def _step(x, target, w_in_g, w_rest, pool_scale, gain, bias, place=None):
    seq = x.shape[0]
    tabs = _rope_tables(seq)
    qkv, gathered = _in_proj_qkv(x, w_in_g, tabs, comm=_allgather_weights(w_rest) if place else None)
    w_out_g, w_pool_sh = gathered if place else w_rest
    w_pool_g = _pool_weight(w_pool_sh)
    hug = _in_proj_pool_gate(x, w_in_g)
    o_list, st_list = [], []
    for p, dil in enumerate(DILATIONS):
        o, st = _attn_fwd(qkv[p], "attn_fwd_d%d" % dil)
        o_list.append(o)
        st_list.append(st)
    y, mixpre, lse_all, pooled = _mix_gate(o_list, st_list, hug, w_pool_g, pool_scale)
    dz, dzb, gain_part, bias_part, loss_part = _out_proj_loss(y, w_out_g, x, target, gain, bias)
    dh, dpo, do_list, stat_list = _dy_gate_bwd(dzb, w_out_g, hug, mixpre, pool_scale, lse_all)
    g_w_out = _grad_w_out(y, dzb)
    dh, g_w_pool, scale_part = _pool_bwd(dh, dpo, mixpre, pooled, w_pool_g, pool_scale)
    small = jnp.concatenate([scale_part, gain_part, bias_part, loss_part], axis=1)
    early = [g_w_out.reshape(N_SHARDS, 2, D_MODEL // (2 * N_SHARDS), D_MODEL), _pool_grad_pieces(g_w_pool)]

    bwd = lambda p, comm: _attn_bwd(qkv[p], do_list[p], stat_list[p], "attn_bwd_d%d" % DILATIONS[p], comm)
    if place is None:
        parts = [bwd(p, None)[0] for p in range(3)]
    else:
        core, chip_core = place
        part_a, recv = bwd(0, _exchange_halves(early))
        sums = [_add_own_half(g, r, core, "add_own_half_%d" % a) for a, (g, r) in enumerate(zip(early, recv))]
        part_b, recv = bwd(1, _scatter_to_chips([s[1] for s in sums]))
        bufs = [_add_chips(s[0], r, chip_core, "add_chips_%d" % a) for a, (s, r) in enumerate(zip(sums, recv))]
        part_c, early = bwd(2, _share_with_sibling(bufs))
        parts = [part_a, part_b, part_c]
    dh = _sum_patterns(dh, [t[0] for t in parts], tabs, True, 0, "sum_dq")
    dh = _sum_patterns(dh, [t[1] for t in parts], tabs, True, 1, "sum_dk")
    dh = _sum_patterns(dh, [t[2] for t in parts], tabs, False, 2, "sum_dv")
    g_w_in = _grad_w_in(x, dh)
    if place is None:
        g_x, _ = _grad_x(dh, w_in_g, dz)
    else:
        g_x, recv = _grad_x(dh, w_in_g, dz, 0, 2, None, _exchange_halves([g_w_in]))
        total, total_b = _add_own_half(g_w_in, recv[0], core, "add_own_half_w_in")
        g_x, recv = _grad_x(dh, w_in_g, dz, 1, 2, g_x, _scatter_to_chips([total_b]))
        buf = _add_chips(total, recv[0], chip_core, "add_chips_w_in")
        g_w_in = _run_exchange(_share_with_sibling([buf]), "share_w_in")[0]
    return g_x, g_w_in, early[0], early[1], small


def _exchange_halves(grads):
    n = len(grads)

    def copies(src, dst, sems):
        x, y, c, _ = _mesh_place()
        return [_remote(src[a].at[j, 1 - c], dst[a].at[j], sems[0].at[a, j], sems[1].at[a, j], (x, y, 1 - c))
                for a in range(n) for j in range(N_SHARDS)]

    def start(src, dst, sems):
        for cp in copies(src, dst, sems):
            cp.start()

    def finish(src, dst, sems):
        for cp in copies(src, dst, sems):
            cp.wait()

    return _Exchange(grads, [jax.ShapeDtypeStruct((N_SHARDS,) + g.shape[2:], g.dtype) for g in grads], {},
                     [pltpu.SemaphoreType.DMA((n, N_SHARDS))] * 2, start, finish)


def _add_own_half(grad, recv, core, name):
    _, _, r, c = grad.shape
    tr = min(r, 256)

    def body(core_ref, g_ref, r_ref, o_ref, ob_ref):
        tot = g_ref[...] + r_ref[...]
        o_ref[...] = tot
        ob_ref[...] = tot.astype(BF16)

    out = pl.BlockSpec((None, tr, c), lambda j, i, core_ref: (j, i, 0))
    return pl.pallas_call(
        body, name=name,
        grid_spec=pltpu.PrefetchScalarGridSpec(
            num_scalar_prefetch=1, grid=(N_SHARDS, r // tr),
            in_specs=[pl.BlockSpec((None, None, tr, c), lambda j, i, core_ref: (j, core_ref[0], i, 0)),
                      pl.BlockSpec((None, tr, c), lambda j, i, core_ref: (j, i, 0))],
            out_specs=[out, out]),
        out_shape=[jax.ShapeDtypeStruct((N_SHARDS, r, c), F32), jax.ShapeDtypeStruct((N_SHARDS, r, c), BF16)],
        compiler_params=_params(("parallel", "parallel"), 32),
    )(core, grad, recv)


def _scatter_to_chips(sums):
    n = len(sums)

    def copies(src, dst, sems):
        x, y, c, chips = _mesh_place()
        return [_remote(src[a].at[2 * cx + cy], dst[a].at[k], sems[0].at[a, k], sems[1].at[a, k], (cx, cy, c))
                for a in range(n) for k, (cx, cy) in enumerate(chips)]

    def start(src, dst, sems):
        for cp in copies(src, dst, sems):
            cp.start()

    def finish(src, dst, sems):
        for cp in copies(src, dst, sems):
            cp.wait()

    return _Exchange(sums, [jax.ShapeDtypeStruct((3,) + s.shape[1:], s.dtype) for s in sums], {},
                     [pltpu.SemaphoreType.DMA((n, 3))] * 2, start, finish)


def _add_chips(sums, recv, chip_core, name):
    _, r, c = sums.shape
    tr = min(r, 256)

    def body(cc_ref, s_ref, r_ref, o_ref):
        o_ref[...] = ((s_ref[...] + r_ref[0].astype(F32)) + r_ref[1].astype(F32)) + r_ref[2].astype(F32)

    return pl.pallas_call(
        body, name=name,
        grid_spec=pltpu.PrefetchScalarGridSpec(
            num_scalar_prefetch=1, grid=(r // tr,),
            in_specs=[pl.BlockSpec((None, tr, c), lambda i, cc_ref: (cc_ref[0], i, 0)),
                      pl.BlockSpec((3, tr, c), lambda i, cc_ref: (0, i, 0))],
            out_specs=pl.BlockSpec((None, tr, c), lambda i, cc_ref: (cc_ref[1], i, 0))),
        out_shape=jax.ShapeDtypeStruct((2, r, c), F32),
        compiler_params=_params(("parallel",), 32),
    )(chip_core, sums, recv)


def _share_with_sibling(bufs):
    n = len(bufs)

    def copies(dst, sems, half):
        x, y, c, _ = _mesh_place()
        h = c if half == "mine" else 1 - c
        return [_remote(dst[a].at[h], dst[a].at[h], sems[0].at[a], sems[1].at[a], (x, y, 1 - c)) for a in range(n)]

    def start(ins, dst, sems):
        for cp in copies(dst, sems, "mine"):
            cp.start()

    def finish(ins, dst, sems):
        for cp in copies(dst, sems, "theirs"):
            cp.wait_recv()
        for cp in copies(dst, sems, "mine"):
            cp.wait_send()

    return _Exchange(bufs, [jax.ShapeDtypeStruct(b.shape, b.dtype) for b in bufs], {a: a for a in range(n)},
                     [pltpu.SemaphoreType.DMA((n,))] * 2, start, finish)


def _adam_math(w, g, m, v):
    m = ADAM_B1 * m + (1.0 - ADAM_B1) * g
    v = ADAM_B2 * v + (1.0 - ADAM_B2) * (g * g)
    m_hat = m / (1.0 - ADAM_B1 ** ADAM_STEP)
    v_hat = v / (1.0 - ADAM_B2 ** ADAM_STEP)
    delta = -ADAM_LR * (m_hat / (jnp.sqrt(v_hat) + ADAM_EPS) + ADAM_WD * w)
    return delta, m, v


def _small_allreduce_adamw(small, w_vec, m_vec, v_vec):
    width = small.shape[1]
    n_par = w_vec.shape[1]

    def body(s_ref, w_ref, m_ref, v_ref, loss_ref, g_ref, d_ref, nm_ref, nv_ref, gather_ref, send_sems, recv_sems):
        x, y, c = lax.axis_index("x"), lax.axis_index("y"), lax.axis_index("c")
        me = 4 * x + 2 * y + c
        gather_ref[me] = s_ref[...]
        copies = []
        for r in range(1, 8):
            bx, by, bc = (r >> 2) & 1, (r >> 1) & 1, r & 1
            peer = (x ^ bx, y ^ by, c ^ bc)
            cp = pltpu.make_async_remote_copy(
                src_ref=s_ref, dst_ref=gather_ref.at[me], send_sem=send_sems.at[r - 1], recv_sem=recv_sems.at[r - 1],
                device_id=peer, device_id_type=MESH)
            cp.start()
            copies.append(cp)
        for r in range(1, 8):
            bx, by, bc = (r >> 2) & 1, (r >> 1) & 1, r & 1
            theirs = gather_ref.at[4 * (x ^ bx) + 2 * (y ^ by) + (c ^ bc)]
            pltpu.make_async_remote_copy(
                src_ref=theirs, dst_ref=theirs, send_sem=send_sems.at[r - 1], recv_sem=recv_sems.at[r - 1],
                device_id=(x ^ bx, y ^ by, c ^ bc), device_id_type=MESH).wait_recv()
        for cp in copies:
            cp.wait_send()
        tot = gather_ref[0]
        for d in range(1, 8):
            tot = tot + gather_ref[d]
        tot = jnp.sum(tot, axis=0, keepdims=True)
        sq = jnp.sum(tot[:, n_par:], axis=1, keepdims=True)
        loss_ref[...] = jnp.broadcast_to(sq * (0.5 / D_MODEL), loss_ref.shape)
        g = tot[:, :n_par]
        g_ref[...] = g
        d_ref[...], nm_ref[...], nv_ref[...] = _adam_math(w_ref[...], g, m_ref[...], v_ref[...])

    vm = pl.BlockSpec(memory_space=pltpu.VMEM)
    vec = jax.ShapeDtypeStruct((1, n_par), F32)
    return pl.pallas_call(
        body, name="small_allreduce_adamw",
        in_specs=[vm] * 4, out_specs=[vm] * 5,
        out_shape=[jax.ShapeDtypeStruct((1, 128), F32), vec, vec, vec, vec],
        scratch_shapes=[pltpu.VMEM((8, 8, width), F32), pltpu.SemaphoreType.DMA((7,)), pltpu.SemaphoreType.DMA((7,))],
    )(small, w_vec, m_vec, v_vec)


def _adamw(w, g, m, v, name):
    r, c = w.shape
    tr = min(r, 256)

    def body(w_ref, g_ref, m_ref, v_ref, d_ref, nm_ref, nv_ref):
        d_ref[...], nm_ref[...], nv_ref[...] = _adam_math(w_ref[...], g_ref[...], m_ref[...], v_ref[...])

    spec = pl.BlockSpec((tr, c), lambda i: (i, 0))
    shape = jax.ShapeDtypeStruct((r, c), F32)
    return pl.pallas_call(
        body, name=name, grid=(r // tr,),
        in_specs=[spec] * 4, out_specs=[spec] * 3, out_shape=[shape] * 3,
        compiler_params=_params(("parallel",), 32),
    )(w, g, m, v)


def kernel(x, w_in, w_pool, pool_scale, w_out, ln_gain, ln_bias, loss_target, m_w_in, m_w_pool, m_pool_scale, m_w_out, m_ln_gain, m_ln_bias, v_w_in, v_w_pool, v_pool_scale, v_w_out, v_ln_gain, v_ln_bias):
    xi, yi, ci = lax.axis_index("x"), lax.axis_index("y"), lax.axis_index("c")
    chip = (2 * xi + yi).astype(jnp.int32).reshape(1)
    core = ci.astype(jnp.int32).reshape(1)
    n_groups = len(POOL_WINDOWS)
    shard_c = w_pool.shape[2]

    w_in_b = _cast_bf16(w_in[0], chip, "cast_w_in", 256)
    w_out_b = _cast_bf16(w_out[0], chip, "cast_w_out", 256)
    w_pool_b = _cast_bf16(w_pool[0].reshape(n_groups * shard_c, POOL_GROUP_DIM), chip, "cast_w_pool", 256)
    w_in_g = _run_exchange(_allgather_weights([w_in_b]), "allgather_w_in")[0]

    chip_core = jnp.concatenate([chip, core])
    g_x, full_in, full_out, full_pool, small = _step(
        x[0], loss_target[0], w_in_g, [w_out_b, w_pool_b], pool_scale, ln_gain, ln_bias, (core, chip_core))
    half_c = shard_c // 2
    grad_w_in = full_in.reshape(D_MODEL, SHARD_IN)
    grad_w_out = full_out.reshape(D_MODEL // N_SHARDS, D_MODEL)
    grad_w_pool = (full_pool.reshape(2, n_groups, half_c, POOL_GROUP_DIM).transpose(1, 0, 2, 3)
                   .reshape(n_groups * shard_c, POOL_GROUP_DIM))

    d_in, nm_in, nv_in = _adamw(w_in[0], grad_w_in, m_w_in[0], v_w_in[0], "adamw_w_in")
    d_out, nm_out, nv_out = _adamw(w_out[0], grad_w_out, m_w_out[0], v_w_out[0], "adamw_w_out")
    flat = lambda t: t[0].reshape(n_groups * shard_c, POOL_GROUP_DIM)
    d_pool, nm_pool, nv_pool = _adamw(flat(w_pool), grad_w_pool, flat(m_w_pool), flat(v_w_pool), "adamw_w_pool")

    cat = lambda a, b, c: jnp.concatenate([a, b, c], axis=1)
    loss_v, g_vec, d_vec, nm_vec, nv_vec = _small_allreduce_adamw(
        small, cat(pool_scale, ln_gain, ln_bias), cat(m_pool_scale, m_ln_gain, m_ln_bias),
        cat(v_pool_scale, v_ln_gain, v_ln_bias))

    def split(vec):
        return vec[:, :D_POOL], vec[:, D_POOL:D_POOL + D_MODEL], vec[:, D_POOL + D_MODEL:]

    g_scale, g_gain, g_bias = split(g_vec)
    d_scale, d_gain, d_bias = split(d_vec)
    nm_scale, nm_gain, nm_bias = split(nm_vec)
    nv_scale, nv_gain, nv_bias = split(nv_vec)
    pool_shape = w_pool.shape
    return (loss_v[0, 0], g_x[None],
            grad_w_in[None], grad_w_pool.reshape(pool_shape), g_scale, grad_w_out[None], g_gain, g_bias,
            d_in[None], d_pool.reshape(pool_shape), d_scale, d_out[None], d_gain, d_bias,
            nm_in[None], nm_pool.reshape(pool_shape), nm_scale, nm_out[None], nm_gain, nm_bias,
            nv_in[None], nv_pool.reshape(pool_shape), nv_scale, nv_out[None], nv_gain, nv_bias)
```

```python
import functools

import jax
import jax.numpy as jnp
from jax import lax
from jax.experimental import pallas as pl
from jax.experimental.pallas import tpu as pltpu

F32 = jnp.float32
BF16 = jnp.bfloat16
MESH = pl.DeviceIdType.MESH
ANY = pl.BlockSpec(memory_space=pl.ANY)

D_MODEL = 2048
D_ATTN = 1024
D_POOL = 1024
HEAD_DIM = 128
N_HEADS = 8
ROPE_DIM = 32
ROPE_THETA = 500000.0
DILATIONS = (1, 4, 16)
KEY_BLOCK = 128
CHUNK = 2 * KEY_BLOCK
STAT_LANES = 128
POOL_WINDOWS = (2, 4, 8, 16)
POOL_GROUP_DIM = 256
POOL_HALO = 16
D_QKV = 3 * D_ATTN
D_UG = D_POOL + D_MODEL
D_IN = D_QKV + D_UG
N_SHARDS = 4
SHARD_IN = D_IN // N_SHARDS
LN_EPS = 1e-5
DEEPNORM_ALPHA = 2.0 ** 0.25
ADAM_LR = 0.001
ADAM_B1 = 0.9
ADAM_B2 = 0.999
ADAM_EPS = 1e-08
ADAM_WD = 0.01
ADAM_STEP = 10
NEG = -1e30
MIB = 1024 * 1024


def _params(sem, vmem_mib):
    return pltpu.CompilerParams(dimension_semantics=sem, vmem_limit_bytes=vmem_mib * MIB)


class _Exchange:
    def __init__(self, operands, out_shape, aliases, sems, start, finish):
        self.operands, self.out_shape, self.aliases, self.sems = list(operands), list(out_shape), dict(aliases), list(sems)
        self.start, self.finish = start, finish


def _run_exchange(comm, name):
    n_in, n_out = len(comm.operands), len(comm.out_shape)

    def body(*refs):
        ins, outs, sems = refs[:n_in], refs[n_in:n_in + n_out], refs[n_in + n_out:]
        comm.start(ins, outs, sems)
        comm.finish(ins, outs, sems)

    return pl.pallas_call(
        body, name=name, in_specs=[ANY] * n_in, out_specs=[ANY] * n_out, out_shape=comm.out_shape,
        input_output_aliases=comm.aliases, scratch_shapes=comm.sems,
    )(*comm.operands)


def _call(body, *, name, grid, in_specs, out_specs, out_shape, scratch_shapes, semantics, vmem_mib, args,
          aliases=None, comm=None, prefetch=()):
    aliases = dict(aliases or {})
    n_pre, n_in, n_out, n_scr = len(prefetch), len(in_specs), len(out_specs), len(scratch_shapes)
    c_in, c_out = (len(comm.operands), len(comm.out_shape)) if comm else (0, 0)
    c_shapes, c_sems, c_operands = (comm.out_shape, comm.sems, comm.operands) if comm else ([], [], [])

    def hosted(*refs):
        pre, refs = refs[:n_pre], refs[n_pre:]
        a = n_in
        b = a + c_in
        c = b + n_out
        d = c + c_out
        e = d + n_scr
        if comm is None:
            body(*pre, *refs)
            return
        ids = [pl.program_id(k) for k in range(len(grid))]
        first = functools.reduce(jnp.logical_and, [i == 0 for i in ids])
        last = functools.reduce(jnp.logical_and, [i == g - 1 for i, g in zip(ids, grid)])

        @pl.when(first)
        def _():
            comm.start(refs[a:b], refs[c:d], refs[e:])

        body(*pre, *refs[:a], *refs[b:c], *refs[d:e])

        @pl.when(last)
        def _():
            comm.finish(refs[a:b], refs[c:d], refs[e:])

    if comm:
        semantics = ("arbitrary",) * len(grid)
        for i, o in comm.aliases.items():
            aliases[n_pre + n_in + i] = n_out + o
    outs = pl.pallas_call(
        hosted, name=name,
        grid_spec=pltpu.PrefetchScalarGridSpec(
            num_scalar_prefetch=n_pre, grid=grid, in_specs=list(in_specs) + [ANY] * c_in,
            out_specs=list(out_specs) + [ANY] * c_out, scratch_shapes=list(scratch_shapes) + c_sems),
        out_shape=list(out_shape) + c_shapes, input_output_aliases=aliases,
        compiler_params=_params(semantics, vmem_mib),
    )(*prefetch, *args, *c_operands)
    return list(outs[:n_out]), list(outs[n_out:])


def _dot_nn(a, b):
    return jnp.dot(a, b, preferred_element_type=F32)


def _dot_nt(a, b):
    return lax.dot_general(a, b, (((1,), (1,)), ((), ())), preferred_element_type=F32)


def _dot_tn(a, b):
    return lax.dot_general(a, b, (((0,), (0,)), ((), ())), preferred_element_type=F32)


def _fold_rows(a):
    r, c = a.shape
    return jnp.sum(a.reshape(r // 8, 8, c), axis=0)


def _cast_bf16(a, chip, name, rows):
    r, c = a.shape

    def body(chip_ref, a_ref, o_ref):
        o_ref[...] = a_ref[...].astype(BF16)

    return pl.pallas_call(
        body, name=name,
        grid_spec=pltpu.PrefetchScalarGridSpec(
            num_scalar_prefetch=1, grid=(r // rows,),
            in_specs=[pl.BlockSpec((rows, c), lambda i, chip_ref: (i, 0))],
            out_specs=pl.BlockSpec((None, rows, c), lambda i, chip_ref: (chip_ref[0], i, 0))),
        out_shape=jax.ShapeDtypeStruct((N_SHARDS, r, c), BF16),
        compiler_params=_params(("parallel",), 32),
    )(chip, a)


def _mesh_place():
    x, y, c = lax.axis_index("x"), lax.axis_index("y"), lax.axis_index("c")
    return x, y, c, [(1 - x, y), (x, 1 - y), (1 - x, 1 - y)]


def _remote(src, dst, send_sem, recv_sem, to):
    return pltpu.make_async_remote_copy(src_ref=src, dst_ref=dst, send_sem=send_sem, recv_sem=recv_sem,
                                        device_id=to, device_id_type=MESH)


def _allgather_weights(bufs):
    n = len(bufs)

    def half(a, core):
        rows = bufs[a].shape[1] // 2
        return pl.ds(core * rows, rows)

    def ici_copies(dst, sems):
        x, y, c, chips = _mesh_place()
        own = lambda a: dst[a].at[2 * x + y, half(a, c)]
        return [_remote(own(a), own(a), sems[0].at[a, k], sems[1].at[a, k], (cx, cy, c))
                for a in range(n) for k, (cx, cy) in enumerate(chips)]

    def start(ins, dst, sems):
        for cp in ici_copies(dst, sems):
            cp.start()

    def finish(ins, dst, sems):
        x, y, c, chips = _mesh_place()
        sibling = (x, y, 1 - c)
        passed_on = []
        for k, (cx, cy) in enumerate(chips):
            for a in range(n):
                landed = dst[a].at[2 * cx + cy, half(a, c)]
                _remote(landed, landed, sems[0].at[a, k], sems[1].at[a, k], (cx, cy, c)).wait_recv()
                cp = _remote(landed, landed, sems[2].at[a, k], sems[3].at[a, k], sibling)
                cp.start()
                passed_on.append(cp)
        for k, (cx, cy) in enumerate(chips):
            for a in range(n):
                passed = dst[a].at[2 * cx + cy, half(a, 1 - c)]
                _remote(passed, passed, sems[2].at[a, k], sems[3].at[a, k], sibling).wait_recv()
        for cp in ici_copies(dst, sems) + passed_on:
            cp.wait_send()

    return _Exchange(bufs, [jax.ShapeDtypeStruct(b.shape, b.dtype) for b in bufs], {a: a for a in range(n)},
                     [pltpu.SemaphoreType.DMA((n, 3))] * 4, start, finish)


def _rope_tables(seq):
    half = ROPE_DIM // 2
    inv_freq = ROPE_THETA ** (-(2.0 * jnp.arange(half, dtype=F32)) / ROPE_DIM)
    ang = jnp.arange(seq, dtype=jnp.int32).astype(F32)[:, None] * inv_freq[None, :]
    cos, sin = jnp.cos(ang), jnp.sin(ang)
    pad = jnp.zeros((seq, HEAD_DIM - ROPE_DIM), F32)
    zeros = jnp.zeros((seq, half), F32)
    c_tab = jnp.concatenate([cos, cos, pad + 1.0], axis=1)
    up_tab = jnp.concatenate([-sin, zeros, pad], axis=1)
    down_tab = jnp.concatenate([zeros, sin, pad], axis=1)
    return c_tab, up_tab, down_tab


def _rotate_heads(t, c_tab, up_tab, down_tab):
    outs = []
    for h in range(t.shape[1] // HEAD_DIM):
        th = t[:, h * HEAD_DIM:(h + 1) * HEAD_DIM]
        up = pltpu.roll(th, HEAD_DIM - ROPE_DIM // 2, axis=1)
        down = pltpu.roll(th, ROPE_DIM // 2, axis=1)
        outs.append(th * c_tab + up * up_tab + down * down_tab)
    return outs[0] if len(outs) == 1 else jnp.concatenate(outs, axis=1)


def _to_pattern(slabs_ref, dst_ref, dil, dtype):
    n_slabs, rows, _ = slabs_ref.shape
    for s in range(n_slabs):
        for r in range(dil):
            dst_ref[r, :, s * 128:(s + 1) * 128] = slabs_ref[s, pl.ds(r, rows // dil, dil), :].astype(dtype)


def _from_pattern(src_ref, slabs_ref, dil):
    n_slabs, rows, _ = slabs_ref.shape
    for s in range(n_slabs):
        for r in range(dil):
            slabs_ref[s, pl.ds(r, rows // dil, dil), :] = src_ref[r, :, s * 128:(s + 1) * 128]


def _store_slabs(slabs_ref, value):
    for s in range(slabs_ref.shape[0]):
        slabs_ref[s] = value[:, s * 128:(s + 1) * 128]


def _in_proj_qkv(x, w_in_g, tabs, comm=None):
    seq = x.shape[0]
    tm, tn = 512, SHARD_IN
    heads = tn // HEAD_DIM
    k_heads_in_second = 2 * D_ATTN // HEAD_DIM - heads
    d4, d16 = DILATIONS[1], DILATIONS[2]

    def body(x_ref, w_ref, c_ref, up_ref, down_ref, o1_ref, o4_ref, o16_ref, res_ref):
        shard = pl.program_id(0)
        acc = _dot_nn(x_ref[...].astype(BF16), w_ref[...])

        def emit(rotated_heads):
            for h in range(heads):
                th = acc[:, h * HEAD_DIM:(h + 1) * HEAD_DIM]
                if h < rotated_heads:
                    th = _rotate_heads(th, c_ref[...], up_ref[...], down_ref[...])
                res_ref[h] = th
                o1_ref[:, h * HEAD_DIM:(h + 1) * HEAD_DIM] = th.astype(BF16)

        @pl.when(shard == 0)
        def _():
            emit(heads)

        @pl.when(shard == 1)
        def _():
            emit(k_heads_in_second)

        _to_pattern(res_ref, o4_ref, d4, BF16)
        _to_pattern(res_ref, o16_ref, d16, BF16)

    tab_spec = pl.BlockSpec((tm, HEAD_DIM), lambda s, i: (i, 0))
    (o1, o4, o16), exchanged = _call(
        body, name="in_proj_qkv", grid=(D_QKV // tn, seq // tm),
        in_specs=[pl.BlockSpec((tm, D_MODEL), lambda s, i: (i, 0)),
                  pl.BlockSpec((None, D_MODEL, tn), lambda s, i: (s, 0, 0)),
                  tab_spec, tab_spec, tab_spec],
        out_specs=[pl.BlockSpec((tm, tn), lambda s, i: (i, s)),
                   pl.BlockSpec((d4, tm // d4, tn), lambda s, i: (0, i, s)),
                   pl.BlockSpec((d16, tm // d16, tn), lambda s, i: (0, i, s))],
        out_shape=[jax.ShapeDtypeStruct((seq, D_QKV), BF16),
                   jax.ShapeDtypeStruct((d4, seq // d4, D_QKV), BF16),
                   jax.ShapeDtypeStruct((d16, seq // d16, D_QKV), BF16)],
        scratch_shapes=[pltpu.VMEM((heads, tm, HEAD_DIM), F32)],
        semantics=("parallel", "parallel"), vmem_mib=52, args=(x, w_in_g, *tabs), comm=comm)
    return [o1[None], o4, o16], exchanged


def _in_proj_pool_gate(x, w_in_g):
    seq = x.shape[0]
    tm, tn = 512, SHARD_IN
    first_shard = D_QKV // tn

    def body(x_ref, w_ref, o_ref):
        o_ref[...] = _dot_nn(x_ref[...].astype(BF16), w_ref[...])

    return pl.pallas_call(
        body, name="in_proj_pool_gate", grid=(D_UG // tn, seq // tm),
        in_specs=[pl.BlockSpec((tm, D_MODEL), lambda s, i: (i, 0)),
                  pl.BlockSpec((None, D_MODEL, tn), lambda s, i: (s + first_shard, 0, 0))],
        out_specs=pl.BlockSpec((tm, tn), lambda s, i: (i, s)),
        out_shape=jax.ShapeDtypeStruct((seq, D_UG), F32),
        compiler_params=_params(("parallel", "parallel"), 48),
    )(x, w_in_g)


def _band_masks():
    row = lax.broadcasted_iota(jnp.int32, (KEY_BLOCK, KEY_BLOCK), 0)
    col = lax.broadcasted_iota(jnp.int32, (KEY_BLOCK, KEY_BLOCK), 1)
    return col <= row, col >= row


def _attn_fwd(qkv, name):
    dil, n, _ = qkv.shape
    scale = HEAD_DIM ** -0.5
    lo, hi = slice(0, KEY_BLOCK), slice(KEY_BLOCK, CHUNK)

    def body(q_ref, k_ref, v_ref, kb_ref, vb_ref, o_ref, st_ref):
        i = pl.program_id(1)
        cur_mask, prev_mask = _band_masks()
        before_mask = jnp.logical_and(prev_mask, i > 0)
        lane = lax.broadcasted_iota(jnp.int32, (KEY_BLOCK, STAT_LANES), 1)
        for rows in (lo, hi):
            stats = jnp.zeros((KEY_BLOCK, STAT_LANES), F32)
            for h in range(N_HEADS):
                cols = slice(h * HEAD_DIM, (h + 1) * HEAD_DIM)
                q = q_ref[rows, cols]
                if rows is lo:
                    k_prev, v_prev, mask = kb_ref[:, cols], vb_ref[:, cols], before_mask
                else:
                    k_prev, v_prev, mask = k_ref[lo, cols], v_ref[lo, cols], prev_mask
                s_prev = jnp.where(mask, _dot_nt(q, k_prev) * scale, NEG)
                s_cur = jnp.where(cur_mask, _dot_nt(q, k_ref[rows, cols]) * scale, NEG)
                m = jnp.maximum(jnp.max(s_cur, axis=-1, keepdims=True), jnp.max(s_prev, axis=-1, keepdims=True))
                p_cur = jnp.exp(s_cur - m)
                p_prev = jnp.exp(s_prev - m)
                den = jnp.sum(p_cur, axis=-1, keepdims=True) + jnp.sum(p_prev, axis=-1, keepdims=True)
                o = _dot_nn(p_cur.astype(BF16), v_ref[rows, cols]) + _dot_nn(p_prev.astype(BF16), v_prev)
                o_ref[rows, cols] = o / den
                stats = jnp.where(lane == h, m + jnp.log(den), stats)
            st_ref[rows, :] = stats

    main = lambda cb: pl.BlockSpec((None, CHUNK, D_ATTN), lambda r, i: (r, i, cb))
    before = lambda cb: pl.BlockSpec((None, KEY_BLOCK, D_ATTN), lambda r, i: (r, jnp.maximum(2 * i - 1, 0), cb))
    return pl.pallas_call(
        body, name=name, grid=(dil, n // CHUNK),
        in_specs=[main(0), main(1), main(2), before(1), before(2)],
        out_specs=[main(0), pl.BlockSpec((None, CHUNK, STAT_LANES), lambda r, i: (r, i, 0))],
        out_shape=[jax.ShapeDtypeStruct((dil, n, D_ATTN), F32), jax.ShapeDtypeStruct((dil, n, STAT_LANES), F32)],
        compiler_params=_params(("parallel", "parallel"), 40),
    )(qkv, qkv, qkv, qkv, qkv)


def _attn_bwd(qkv, do, stats, name, comm=None):
    dil, n, _ = qkv.shape
    n_blocks = n // KEY_BLOCK
    last = n // CHUNK - 1
    scale = HEAD_DIM ** -0.5
    lo, hi = slice(0, KEY_BLOCK), slice(KEY_BLOCK, CHUNK)

    def body(q_ref, k_ref, v_ref, kb_ref, vb_ref, qa_ref, do_ref, doa_ref, st_ref, sta_ref, dq_ref, dk_ref, dv_ref):
        i = pl.program_id(1)
        cur_mask, prev_mask = _band_masks()
        before_mask = jnp.logical_and(prev_mask, i > 0)
        after_mask = jnp.logical_and(prev_mask, i < last)

        def pair(q, k, v, do_b, lse, delta, mask):
            p = jnp.exp(jnp.where(mask, _dot_nt(q, k) * scale, NEG) - lse)
            ds = p * (_dot_nt(do_b, v) - delta) * scale
            return p.astype(BF16), ds.astype(BF16)

        for h in range(N_HEADS):
            cols = slice(h * HEAD_DIM, (h + 1) * HEAD_DIM)
            lse_c, del_c = slice(h, h + 1), slice(N_HEADS + h, N_HEADS + h + 1)
            q0, q1, qa = q_ref[lo, cols], q_ref[hi, cols], qa_ref[:, cols]
            k0, k1, kb = k_ref[lo, cols], k_ref[hi, cols], kb_ref[:, cols]
            v0, v1, vb = v_ref[lo, cols], v_ref[hi, cols], vb_ref[:, cols]
            do0, do1, doa = do_ref[lo, cols], do_ref[hi, cols], doa_ref[:, cols]
            st0 = (st_ref[lo, lse_c], st_ref[lo, del_c])
            st1 = (st_ref[hi, lse_c], st_ref[hi, del_c])
            sta = (sta_ref[:, lse_c], sta_ref[:, del_c])
            _, ds_0b = pair(q0, kb, vb, do0, *st0, before_mask)
            p_00, ds_00 = pair(q0, k0, v0, do0, *st0, cur_mask)
            p_10, ds_10 = pair(q1, k0, v0, do1, *st1, prev_mask)
            p_11, ds_11 = pair(q1, k1, v1, do1, *st1, cur_mask)
            p_a1, ds_a1 = pair(qa, k1, v1, doa, *sta, after_mask)
            dq_ref[lo, cols] = _dot_nn(ds_0b, kb) + _dot_nn(ds_00, k0)
            dq_ref[hi, cols] = _dot_nn(ds_10, k0) + _dot_nn(ds_11, k1)
            dk_ref[lo, cols] = _dot_tn(ds_00, q0) + _dot_tn(ds_10, q1)
            dk_ref[hi, cols] = _dot_tn(ds_11, q1) + _dot_tn(ds_a1, qa)
            dv_ref[lo, cols] = _dot_tn(p_00, do0) + _dot_tn(p_10, do1)
            dv_ref[hi, cols] = _dot_tn(p_11, do1) + _dot_tn(p_a1, doa)

    def spec(rows, width, row_of, cb):
        return pl.BlockSpec((None, rows, width), lambda r, i: (r, row_of(i), cb))

    same = lambda i: i
    before = lambda i: jnp.maximum(2 * i - 1, 0)
    after = lambda i: jnp.minimum(2 * i + 2, n_blocks - 1)
    out = spec(CHUNK, D_ATTN, same, 0)
    return _call(
        body, name=name, grid=(dil, n // CHUNK),
        in_specs=[spec(CHUNK, D_ATTN, same, 0), spec(CHUNK, D_ATTN, same, 1), spec(CHUNK, D_ATTN, same, 2),
                  spec(KEY_BLOCK, D_ATTN, before, 1), spec(KEY_BLOCK, D_ATTN, before, 2),
                  spec(KEY_BLOCK, D_ATTN, after, 0),
                  spec(CHUNK, D_ATTN, same, 0), spec(KEY_BLOCK, D_ATTN, after, 0),
                  spec(CHUNK, STAT_LANES, same, 0), spec(KEY_BLOCK, STAT_LANES, after, 0)],
        out_specs=[out, out, out],
        out_shape=[jax.ShapeDtypeStruct((dil, n, D_ATTN), F32)] * 3,
        scratch_shapes=[], semantics=("parallel", "parallel"), vmem_mib=40,
        args=(qkv, qkv, qkv, qkv, qkv, qkv, do, do, stats, stats), comm=comm)


def _window_sums(ext, window, backward):
    rows = ext.shape[0]
    acc, span = ext, 1
    while span < window:
        acc = acc + pltpu.roll(acc, (rows - span) if backward else span, axis=0)
        span *= 2
    return acc


def _mix_gate(o_list, st_list, hug, w_pool_g, pool_scale):
    seq = hug.shape[0]
    tm = 256
    halo_blocks = tm // POOL_HALO
    d4, d16 = DILATIONS[1], DILATIONS[2]

    def body(o1_ref, o4_ref, o16_ref, l1_ref, l4_ref, l16_ref, u_ref, halo_ref, ga_ref, gp_ref, wp_ref, sc_ref,
             y_ref, mix_ref, lse_ref, pooled_ref, n4_ref, n16_ref, nl4_ref, nl16_ref):
        i = pl.program_id(0)
        _from_pattern(o4_ref, n4_ref, d4)
        _from_pattern(o16_ref, n16_ref, d16)
        _from_pattern(l4_ref, nl4_ref, d4)
        _from_pattern(l16_ref, nl16_ref, d16)
        la, lb, lc = l1_ref[...], nl4_ref[0], nl16_ref[0]
        mx = jnp.maximum(jnp.maximum(la, lb), lc)
        ea, eb, ec = jnp.exp(la - mx), jnp.exp(lb - mx), jnp.exp(lc - mx)
        tot = ea + eb + ec
        lse_ref[...] = mx + jnp.log(tot)
        wa, wb, wc = ea / tot, eb / tot, ec / tot
        ga = ga_ref[...]
        silu_a = ga * jax.nn.sigmoid(ga)
        for h in range(N_HEADS):
            cols = slice(h * HEAD_DIM, (h + 1) * HEAD_DIM)
            hc = slice(h, h + 1)
            attn = wa[:, hc] * o1_ref[:, cols] + wb[:, hc] * n4_ref[h] + wc[:, hc] * n16_ref[h]
            mix_ref[:, cols] = attn
            y_ref[:, cols] = (attn * silu_a[:, cols]).astype(BF16)

        u = u_ref[...]
        halo = jnp.where(i > 0, halo_ref[...], 0.0)
        ext = jnp.concatenate([halo, u], axis=0)
        pos = i * tm + lax.broadcasted_iota(jnp.int32, (tm, 1), 0)
        gp = gp_ref[...]
        gated_scale = sc_ref[...] * (gp * jax.nn.sigmoid(gp))
        for g, window in enumerate(POOL_WINDOWS):
            cols = slice(g * POOL_GROUP_DIM, (g + 1) * POOL_GROUP_DIM)
            sums = _window_sums(ext[:, cols], window, backward=False)[POOL_HALO:, :]
            count = jnp.minimum(pos + 1, window).astype(F32)
            pooled = (sums / count - u[:, cols]).astype(BF16)
            pooled_ref[:, cols] = pooled
            pre = _dot_nn(pooled, wp_ref[g])
            out_cols = slice(D_ATTN + g * POOL_GROUP_DIM, D_ATTN + (g + 1) * POOL_GROUP_DIM)
            mix_ref[:, out_cols] = pre
            y_ref[:, out_cols] = (pre * gated_scale[:, cols]).astype(BF16)

    row = lambda width, cb=0: pl.BlockSpec((tm, width), lambda i: (i, cb))
    pat = lambda d, width: pl.BlockSpec((d, tm // d, width), lambda i: (0, i, 0))
    return pl.pallas_call(
        body, name="mix_gate", grid=(seq // tm,),
        in_specs=[row(D_ATTN), pat(d4, D_ATTN), pat(d16, D_ATTN),
                  row(STAT_LANES), pat(d4, STAT_LANES), pat(d16, STAT_LANES),
                  row(D_POOL),
                  pl.BlockSpec((POOL_HALO, D_POOL), lambda i: (jnp.maximum(i * halo_blocks - 1, 0), 0)),
                  row(D_ATTN, 1), row(D_POOL, 2),
                  pl.BlockSpec((len(POOL_WINDOWS), POOL_GROUP_DIM, POOL_GROUP_DIM), lambda i: (0, 0, 0)),
                  pl.BlockSpec((1, D_POOL), lambda i: (0, 0))],
        out_specs=[row(D_MODEL), row(D_MODEL), row(STAT_LANES), row(D_POOL)],
        out_shape=[jax.ShapeDtypeStruct((seq, D_MODEL), BF16), jax.ShapeDtypeStruct((seq, D_MODEL), F32),
                   jax.ShapeDtypeStruct((seq, STAT_LANES), F32), jax.ShapeDtypeStruct((seq, D_POOL), BF16)],
        scratch_shapes=[pltpu.VMEM((N_HEADS, tm, HEAD_DIM), F32), pltpu.VMEM((N_HEADS, tm, HEAD_DIM), F32),
                        pltpu.VMEM((1, tm, STAT_LANES), F32), pltpu.VMEM((1, tm, STAT_LANES), F32)],
        compiler_params=_params(("parallel",), 48),
    )(o_list[0][0], o_list[1], o_list[2], st_list[0][0], st_list[1], st_list[2],
      hug, hug, hug, hug, w_pool_g, pool_scale)


def _out_proj_loss(y, w_out_g, x, target, gain, bias):
    seq = x.shape[0]
    tm = 256

    def body(y_ref, w_ref, x_ref, t_ref, g_ref, b_ref, dz_ref, dzb_ref, gg_ref, gb_ref, loss_ref):
        @pl.when(pl.program_id(0) == 0)
        def _():
            gg_ref[...] = jnp.zeros_like(gg_ref)
            gb_ref[...] = jnp.zeros_like(gb_ref)
            loss_ref[...] = jnp.zeros_like(loss_ref)

        z = DEEPNORM_ALPHA * x_ref[...] + _dot_nn(y_ref[...], w_ref[...])
        mu = jnp.mean(z, axis=-1, keepdims=True)
        zc = z - mu
        rstd = lax.rsqrt(jnp.mean(zc * zc, axis=-1, keepdims=True) + LN_EPS)
        xhat = zc * rstd
        gain_v = g_ref[...]
        diff = xhat * gain_v + b_ref[...] - t_ref[...]
        sq = _fold_rows(diff * diff)
        part = sq[:, :128]
        for k in range(1, D_MODEL // 128):
            part = part + sq[:, k * 128:(k + 1) * 128]
        loss_ref[...] += part
        dln = diff * (1.0 / D_MODEL)
        gg_ref[...] += _fold_rows(dln * xhat)
        gb_ref[...] += _fold_rows(dln)
        dxhat = dln * gain_v
        dz = rstd * (dxhat - jnp.mean(dxhat, axis=-1, keepdims=True)
                     - xhat * jnp.mean(dxhat * xhat, axis=-1, keepdims=True))
        dz_ref[...] = dz
        dzb_ref[...] = dz.astype(BF16)

    row = lambda: pl.BlockSpec((tm, D_MODEL), lambda i: (i, 0))
    vec = lambda: pl.BlockSpec((1, D_MODEL), lambda i: (0, 0))
    acc = lambda width: pl.BlockSpec((8, width), lambda i: (0, 0))
    return pl.pallas_call(
        body, name="out_proj_loss", grid=(seq // tm,),
        in_specs=[row(), pl.BlockSpec((D_MODEL, D_MODEL), lambda i: (0, 0)), row(), row(), vec(), vec()],
        out_specs=[row(), row(), acc(D_MODEL), acc(D_MODEL), acc(128)],
        out_shape=[jax.ShapeDtypeStruct((seq, D_MODEL), F32), jax.ShapeDtypeStruct((seq, D_MODEL), BF16),
                   jax.ShapeDtypeStruct((8, D_MODEL), F32), jax.ShapeDtypeStruct((8, D_MODEL), F32),
                   jax.ShapeDtypeStruct((8, 128), F32)],
        compiler_params=_params(("arbitrary",), 48),
    )(y, w_out_g.reshape(D_MODEL, D_MODEL), x, target, gain, bias)


def _dy_gate_bwd(dzb, w_out_g, hug, mixpre, pool_scale, lse_all):
    seq = dzb.shape[0]
    tm = 256
    d4, d16 = DILATIONS[1], DILATIONS[2]

    def body(dz_ref, w_ref, ga_ref, gp_ref, mix_ref, sc_ref, lse_ref,
             dh_ref, dpo_ref, do1_ref, do4_ref, do16_ref, st1_ref, st4_ref, st16_ref, da_ref, st_ref):
        dy = _dot_nt(dz_ref[...], w_ref[...])
        ga = ga_ref[...]
        sig = jax.nn.sigmoid(ga)
        attn = mix_ref[:, :D_ATTN]
        dya = dy[:, :D_ATTN]
        dattn = dya * (ga * sig)
        dh_ref[:, :D_ATTN] = (dya * attn * (sig * (1.0 + ga * (1.0 - sig)))).astype(BF16)
        _store_slabs(da_ref, dattn)
        lane = lax.broadcasted_iota(jnp.int32, (tm, STAT_LANES), 1)
        stats = lse_ref[...]
        prod = dattn * attn
        for h in range(N_HEADS):
            delta = jnp.sum(prod[:, h * HEAD_DIM:(h + 1) * HEAD_DIM], axis=-1, keepdims=True)
            stats = jnp.where(lane == N_HEADS + h, delta, stats)
        st_ref[0] = stats
        do1_ref[...] = dattn.astype(BF16)
        st1_ref[...] = stats
        _to_pattern(da_ref, do4_ref, d4, BF16)
        _to_pattern(da_ref, do16_ref, d16, BF16)
        _to_pattern(st_ref, st4_ref, d4, F32)
        _to_pattern(st_ref, st16_ref, d16, F32)

        gp = gp_ref[...]
        sig = jax.nn.sigmoid(gp)
        dyp = dy[:, D_ATTN:]
        dpo_ref[...] = dyp * (gp * sig)
        dh_ref[:, D_ATTN:] = (dyp * (mix_ref[:, D_ATTN:] * sc_ref[...])
                              * (sig * (1.0 + gp * (1.0 - sig)))).astype(BF16)

    row = lambda width, cb=0: pl.BlockSpec((tm, width), lambda i: (i, cb))
    pat = lambda d, width: pl.BlockSpec((d, tm // d, width), lambda i: (0, i, 0))
    pat_shape = lambda d, width, dtype: jax.ShapeDtypeStruct((d, seq // d, width), dtype)
    outs = pl.pallas_call(
        body, name="dy_gate_bwd", grid=(seq // tm,),
        in_specs=[row(D_MODEL), pl.BlockSpec((D_MODEL, D_MODEL), lambda i: (0, 0)),
                  row(D_ATTN, 1), row(D_POOL, 2), row(D_MODEL), pl.BlockSpec((1, D_POOL), lambda i: (0, 0)),
                  row(STAT_LANES)],
        out_specs=[row(D_MODEL, D_IN // D_MODEL - 1), row(D_POOL),
                   row(D_ATTN), pat(d4, D_ATTN), pat(d16, D_ATTN),
                   row(STAT_LANES), pat(d4, STAT_LANES), pat(d16, STAT_LANES)],
        out_shape=[jax.ShapeDtypeStruct((seq, D_IN), BF16), jax.ShapeDtypeStruct((seq, D_POOL), F32),
                   jax.ShapeDtypeStruct((seq, D_ATTN), BF16), pat_shape(d4, D_ATTN, BF16), pat_shape(d16, D_ATTN, BF16),
                   jax.ShapeDtypeStruct((seq, STAT_LANES), F32), pat_shape(d4, STAT_LANES, F32),
                   pat_shape(d16, STAT_LANES, F32)],
        scratch_shapes=[pltpu.VMEM((N_HEADS, tm, HEAD_DIM), F32), pltpu.VMEM((1, tm, STAT_LANES), F32)],
        compiler_params=_params(("parallel",), 48),
    )(dzb, w_out_g.reshape(D_MODEL, D_MODEL), hug, hug, mixpre, pool_scale, lse_all)
    dh, dpo, do1, do4, do16, st1, st4, st16 = outs
    return dh, dpo, [do1[None], do4, do16], [st1[None], st4, st16]


def _pool_bwd(dh, dpo, mixpre, pooled, w_pool_g, pool_scale):
    seq = dpo.shape[0]
    tm = 256
    halo_blocks = tm // POOL_HALO
    last = seq // tm - 1
    n_groups = len(POOL_WINDOWS)

    def body(dh_in_ref, dpo_ref, halo_ref, pre_ref, pooled_ref, wp_ref, sc_ref, du_ref, gw_ref, gs_ref):
        i = pl.program_id(0)

        @pl.when(i == 0)
        def _():
            gw_ref[...] = jnp.zeros_like(gw_ref)
            gs_ref[...] = jnp.zeros_like(gs_ref)

        dpo = dpo_ref[...]
        scale = sc_ref[...]
        gs_ref[...] += _fold_rows(dpo * pre_ref[...])
        halo = jnp.where(i < last, halo_ref[...], 0.0)
        dpw = (jnp.concatenate([dpo, halo], axis=0) * scale).astype(BF16)
        pos = i * tm + lax.broadcasted_iota(jnp.int32, (tm + POOL_HALO, 1), 0)
        for g, window in enumerate(POOL_WINDOWS):
            cols = slice(g * POOL_GROUP_DIM, (g + 1) * POOL_GROUP_DIM)
            dpw_g = dpw[:, cols]
            gw_ref[g] += _dot_tn(pooled_ref[:, cols], dpw_g[:tm, :])
            dpooled = _dot_nt(dpw_g, wp_ref[g])
            count = jnp.minimum(pos + 1, window).astype(F32)
            sums = _window_sums(dpooled / count, window, backward=True)
            du_ref[:, cols] = (sums[:tm, :] - dpooled[:tm, :]).astype(BF16)

    row = lambda width, cb=0: pl.BlockSpec((tm, width), lambda i: (i, cb))
    return pl.pallas_call(
        body, name="pool_bwd", grid=(seq // tm,),
        in_specs=[ANY, row(D_POOL),
                  pl.BlockSpec((POOL_HALO, D_POOL),
                               lambda i: (jnp.minimum((i + 1) * halo_blocks, seq // POOL_HALO - 1), 0)),
                  row(D_POOL, 1), row(D_POOL),
                  pl.BlockSpec((n_groups, POOL_GROUP_DIM, POOL_GROUP_DIM), lambda i: (0, 0, 0)),
                  pl.BlockSpec((1, D_POOL), lambda i: (0, 0))],
        out_specs=[row(D_POOL, D_QKV // D_POOL),
                   pl.BlockSpec((n_groups, POOL_GROUP_DIM, POOL_GROUP_DIM), lambda i: (0, 0, 0)),
                   pl.BlockSpec((8, D_POOL), lambda i: (0, 0))],
        out_shape=[jax.ShapeDtypeStruct(dh.shape, dh.dtype),
                   jax.ShapeDtypeStruct((n_groups, POOL_GROUP_DIM, POOL_GROUP_DIM), F32),
                   jax.ShapeDtypeStruct((8, D_POOL), F32)],
        input_output_aliases={0: 0},
        compiler_params=_params(("arbitrary",), 40),
    )(dh, dpo, dpo, mixpre, pooled, w_pool_g, pool_scale)


def _sum_patterns(dh, parts, tabs, unrotate, col_block, name):
    seq = dh.shape[0]
    tm, tn = 256, 512
    per = D_ATTN // tn
    d4, d16 = DILATIONS[1], DILATIONS[2]

    def body(dh_in_ref, a1_ref, a4_ref, a16_ref, ct_ref, up_ref, down_ref, o_ref, n4_ref, n16_ref):
        _from_pattern(a4_ref, n4_ref, d4)
        _from_pattern(a16_ref, n16_ref, d16)
        for s in range(tn // HEAD_DIM):
            cols = slice(s * HEAD_DIM, (s + 1) * HEAD_DIM)
            tot = a1_ref[:, cols] + n4_ref[s] + n16_ref[s]
            if unrotate:
                tot = _rotate_heads(tot, ct_ref[...], -up_ref[...], -down_ref[...])
            o_ref[:, cols] = tot.astype(BF16)

    tab = pl.BlockSpec((tm, HEAD_DIM), lambda i, j: (i, 0))
    pat = lambda d: pl.BlockSpec((d, tm // d, tn), lambda i, j: (0, i, j))
    return pl.pallas_call(
        body, name=name, grid=(seq // tm, per),
        in_specs=[ANY, pl.BlockSpec((tm, tn), lambda i, j: (i, j)), pat(d4), pat(d16), tab, tab, tab],
        out_specs=pl.BlockSpec((tm, tn), lambda i, j: (i, col_block * per + j)),
        out_shape=jax.ShapeDtypeStruct(dh.shape, dh.dtype),
        scratch_shapes=[pltpu.VMEM((tn // HEAD_DIM, tm, HEAD_DIM), F32), pltpu.VMEM((tn // HEAD_DIM, tm, HEAD_DIM), F32)],
        input_output_aliases={0: 0},
        compiler_params=_params(("parallel", "parallel"), 32),
    )(dh, parts[0][0], parts[1], parts[2], *tabs)


def _grad_w_in(x, dh, half, name, comm=None):
    seq = x.shape[0]
    ts, td, te = 1024, D_MODEL // 2, SHARD_IN

    def body(half_ref, x_ref, dh_ref, o_ref):
        k = pl.program_id(1)
        part = _dot_tn(x_ref[...].astype(BF16), dh_ref[...])

        @pl.when(k == 0)
        def _():
            o_ref[...] = part

        @pl.when(k > 0)
        def _():
            o_ref[...] += part

    (g,), exchanged = _call(
        body, name=name, grid=(N_SHARDS, seq // ts),
        in_specs=[pl.BlockSpec((ts, td), lambda e, k, half_ref: (k, half_ref[0])),
                  pl.BlockSpec((ts, te), lambda e, k, half_ref: (k, e))],
        out_specs=[pl.BlockSpec((None, td, te), lambda e, k, half_ref: (e, 0, 0))],
        out_shape=[jax.ShapeDtypeStruct((N_SHARDS, td, te), F32)],
        scratch_shapes=[], semantics=("parallel", "arbitrary"), vmem_mib=48, args=(x, dh), comm=comm,
        prefetch=(half,))
    return g, exchanged


def _grad_w_out(y, dzb):
    seq = y.shape[0]
    ts, te = 512, 1024
    nk = seq // ts

    def body(y_ref, dz_ref, o_ref, acc_ref):
        k = pl.program_id(1)

        @pl.when(k == 0)
        def _():
            acc_ref[...] = jnp.zeros_like(acc_ref)

        acc_ref[...] += _dot_tn(y_ref[...], dz_ref[...])

        @pl.when(k == nk - 1)
        def _():
            o_ref[...] = acc_ref[...]

    return pl.pallas_call(
        body, name="grad_w_out", grid=(D_MODEL // te, nk),
        in_specs=[pl.BlockSpec((ts, te), lambda e, k: (k, e)), pl.BlockSpec((ts, D_MODEL), lambda e, k: (k, 0))],
        out_specs=pl.BlockSpec((te, D_MODEL), lambda e, k: (e, 0)),
        out_shape=jax.ShapeDtypeStruct((D_MODEL, D_MODEL), F32),
        scratch_shapes=[pltpu.VMEM((te, D_MODEL), F32)],
        compiler_params=_params(("parallel", "arbitrary"), 48),
    )(y, dzb)


def _grad_x(dh, w_in_g, dz, first=0, tiles=None, prev=None, comm=None):
    seq = dh.shape[0]
    tm, tk = 512, SHARD_IN
    tiles = seq // tm if tiles is None else tiles

    def body(*refs):
        dh_ref, w_ref, dz_ref, o_ref = refs[-4:]
        k = pl.program_id(1)
        part = _dot_nt(dh_ref[...], w_ref[...])

        @pl.when(k == 0)
        def _():
            o_ref[...] = DEEPNORM_ALPHA * dz_ref[...] + part

        @pl.when(k > 0)
        def _():
            o_ref[...] += part

    carried = [] if prev is None else [prev]
    (g_x,), exchanged = _call(
        body, name="grad_x_%d" % first, grid=(tiles, N_SHARDS),
        in_specs=[ANY] * len(carried) + [
            pl.BlockSpec((tm, tk), lambda i, k: (i + first, k)),
            pl.BlockSpec((None, D_MODEL, tk), lambda i, k: (k, 0, 0)),
            pl.BlockSpec((tm, D_MODEL), lambda i, k: (i + first, 0))],
        out_specs=[pl.BlockSpec((tm, D_MODEL), lambda i, k: (i + first, 0))],
        out_shape=[jax.ShapeDtypeStruct((seq, D_MODEL), F32)],
        scratch_shapes=[], semantics=("parallel", "arbitrary"), vmem_mib=48, args=(*carried, dh, w_in_g, dz),
        aliases={0: 0} if carried else None, comm=comm)
    return g_x, exchanged


def _pool_weight(w_pool_sh):
    n_groups = len(POOL_WINDOWS)
    shard_c = POOL_GROUP_DIM // N_SHARDS
    return (w_pool_sh.reshape(N_SHARDS, n_groups, shard_c, POOL_GROUP_DIM).transpose(1, 0, 2, 3)
            .reshape(n_groups, POOL_GROUP_DIM, POOL_GROUP_DIM))


def _pool_grad_pieces(g_w_pool):
    n_groups = len(POOL_WINDOWS)
    half_c = POOL_GROUP_DIM // N_SHARDS // 2
    return (g_w_pool.reshape(n_groups, N_SHARDS, 2, half_c, POOL_GROUP_DIM).transpose(1, 2, 0, 3, 4)
            .reshape(N_SHARDS, 2, n_groups * half_c, POOL_GROUP_DIM))


def _step(x, target, w_in_g, w_rest, pool_scale, gain, bias, place=None):
    seq = x.shape[0]
    tabs = _rope_tables(seq)
    qkv, gathered = _in_proj_qkv(x, w_in_g, tabs, comm=_allgather_weights(w_rest) if place else None)
    w_out_g, w_pool_sh = gathered if place else w_rest
    w_pool_g = _pool_weight(w_pool_sh)
    hug = _in_proj_pool_gate(x, w_in_g)
    o_list, st_list = [], []
    for p, dil in enumerate(DILATIONS):
        o, st = _attn_fwd(qkv[p], "attn_fwd_d%d" % dil)
        o_list.append(o)
        st_list.append(st)
    y, mixpre, lse_all, pooled = _mix_gate(o_list, st_list, hug, w_pool_g, pool_scale)
    dz, dzb, gain_part, bias_part, loss_part = _out_proj_loss(y, w_out_g, x, target, gain, bias)
    dh, dpo, do_list, stat_list = _dy_gate_bwd(dzb, w_out_g, hug, mixpre, pool_scale, lse_all)
    g_w_out = _grad_w_out(y, dzb)
    dh, g_w_pool, scale_part = _pool_bwd(dh, dpo, mixpre, pooled, w_pool_g, pool_scale)
    small = jnp.concatenate([scale_part, gain_part, bias_part, loss_part], axis=1)
    early = [g_w_out.reshape(N_SHARDS, 2, D_MODEL // (2 * N_SHARDS), D_MODEL), _pool_grad_pieces(g_w_pool)]

    bwd = lambda p, comm: _attn_bwd(qkv[p], do_list[p], stat_list[p], "attn_bwd_d%d" % DILATIONS[p], comm)
    if place is None:
        parts = [bwd(p, None)[0] for p in range(3)]
    else:
        core, chip_core = place
        part_a, recv = bwd(0, _exchange_halves(early))
        sums = [_add_own_half(g, r, core, "add_own_half_%d" % a) for a, (g, r) in enumerate(zip(early, recv))]
        part_b, recv = bwd(1, _scatter_to_chips([s[1] for s in sums]))
        bufs = [_add_chips(s[0], r, chip_core, "add_chips_%d" % a) for a, (s, r) in enumerate(zip(sums, recv))]
        part_c, early = bwd(2, _share_with_sibling(bufs))
        parts = [part_a, part_b, part_c]
    dh = _sum_patterns(dh, [t[0] for t in parts], tabs, True, 0, "sum_dq")
    dh = _sum_patterns(dh, [t[1] for t in parts], tabs, True, 1, "sum_dk")
    dh = _sum_patterns(dh, [t[2] for t in parts], tabs, False, 2, "sum_dv")
    if place is None:
        halves = [_grad_w_in(x, dh, jnp.full((1,), h, jnp.int32), "grad_w_in_%d" % h)[0] for h in range(2)]
        g_w_in = jnp.stack(halves, axis=1)
        g_x, _ = _grad_x(dh, w_in_g, dz)
    else:
        give, _ = _grad_w_in(x, dh, 1 - core, "grad_w_in_give")
        keep, recv = _grad_w_in(x, dh, core, "grad_w_in_keep", _send_to_sibling([give]))
        total, total_b = _add_pair(keep, recv[0], "add_own_half_w_in")
        rows = total.shape[1]
        cut = rows // 2
        g_x, recv = _grad_x(dh, w_in_g, dz, 0, 3, None, _scatter_to_chips([total_b], (0, cut)))
        g_x, recv = _grad_x(dh, w_in_g, dz, 3, 3, g_x, _scatter_to_chips([total_b], (cut, rows - cut), recv))
        buf = _add_chips(total, recv[0], chip_core, "add_chips_w_in")
        g_x, _ = _grad_x(dh, w_in_g, dz, 6, seq // 512 - 6, g_x)
        g_w_in = _run_exchange(_share_with_sibling([buf]), "share_w_in")[0]
    return g_x, g_w_in, early[0], early[1], small


def _exchange_halves(grads):
    n = len(grads)

    def copies(src, dst, sems):
        x, y, c, _ = _mesh_place()
        return [_remote(src[a].at[j, 1 - c], dst[a].at[j], sems[0].at[a, j], sems[1].at[a, j], (x, y, 1 - c))
                for a in range(n) for j in range(N_SHARDS)]

    def start(src, dst, sems):
        for cp in copies(src, dst, sems):
            cp.start()

    def finish(src, dst, sems):
        for cp in copies(src, dst, sems):
            cp.wait()

    return _Exchange(grads, [jax.ShapeDtypeStruct((N_SHARDS,) + g.shape[2:], g.dtype) for g in grads], {},
                     [pltpu.SemaphoreType.DMA((n, N_SHARDS))] * 2, start, finish)


def _add_own_half(grad, recv, core, name):
    _, _, r, c = grad.shape
    tr = min(r, 256)

    def body(core_ref, g_ref, r_ref, o_ref, ob_ref):
        tot = g_ref[...] + r_ref[...]
        o_ref[...] = tot
        ob_ref[...] = tot.astype(BF16)

    out = pl.BlockSpec((None, tr, c), lambda j, i, core_ref: (j, i, 0))
    return pl.pallas_call(
        body, name=name,
        grid_spec=pltpu.PrefetchScalarGridSpec(
            num_scalar_prefetch=1, grid=(N_SHARDS, r // tr),
            in_specs=[pl.BlockSpec((None, None, tr, c), lambda j, i, core_ref: (j, core_ref[0], i, 0)),
                      pl.BlockSpec((None, tr, c), lambda j, i, core_ref: (j, i, 0))],
            out_specs=[out, out]),
        out_shape=[jax.ShapeDtypeStruct((N_SHARDS, r, c), F32), jax.ShapeDtypeStruct((N_SHARDS, r, c), BF16)],
        compiler_params=_params(("parallel", "parallel"), 32),
    )(core, grad, recv)


def _send_to_sibling(arrays):
    n = len(arrays)

    def copies(src, dst, sems):
        x, y, c, _ = _mesh_place()
        return [_remote(src[a], dst[a], sems[0].at[a], sems[1].at[a], (x, y, 1 - c)) for a in range(n)]

    def start(src, dst, sems):
        for cp in copies(src, dst, sems):
            cp.start()

    def finish(src, dst, sems):
        for cp in copies(src, dst, sems):
            cp.wait()

    return _Exchange(arrays, [jax.ShapeDtypeStruct(t.shape, t.dtype) for t in arrays], {},
                     [pltpu.SemaphoreType.DMA((n,))] * 2, start, finish)


def _add_pair(a, b, name):
    _, r, c = a.shape
    tr = min(r, 256)

    def body(a_ref, b_ref, o_ref, ob_ref):
        tot = a_ref[...] + b_ref[...]
        o_ref[...] = tot
        ob_ref[...] = tot.astype(BF16)

    spec = pl.BlockSpec((None, tr, c), lambda j, i: (j, i, 0))
    return pl.pallas_call(
        body, name=name, grid=(N_SHARDS, r // tr), in_specs=[spec, spec], out_specs=[spec, spec],
        out_shape=[jax.ShapeDtypeStruct(a.shape, F32), jax.ShapeDtypeStruct(a.shape, BF16)],
        compiler_params=_params(("parallel", "parallel"), 32),
    )(a, b)


def _scatter_to_chips(sums, rows=None, into=None):
    n = len(sums)

    def copies(src, dst, sems):
        x, y, c, chips = _mesh_place()
        part = (lambda ref: ref) if rows is None else (lambda ref: ref.at[pl.ds(rows[0], rows[1])])
        return [_remote(part(src[a].at[2 * cx + cy]), part(dst[a].at[k]), sems[0].at[a, k], sems[1].at[a, k],
                        (cx, cy, c))
                for a in range(n) for k, (cx, cy) in enumerate(chips)]

    def start(src, dst, sems):
        for cp in copies(src, dst, sems):
            cp.start()

    def finish(src, dst, sems):
        for cp in copies(src, dst, sems):
            cp.wait()

    return _Exchange(sums + (into or []), [jax.ShapeDtypeStruct((3,) + s.shape[1:], s.dtype) for s in sums],
                     {n + a: a for a in range(n)} if into else {},
                     [pltpu.SemaphoreType.DMA((n, 3))] * 2, start, finish)


def _add_chips(sums, recv, chip_core, name):
    _, r, c = sums.shape
    tr = min(r, 256)

    def body(cc_ref, s_ref, r_ref, o_ref):
        o_ref[...] = ((s_ref[...] + r_ref[0].astype(F32)) + r_ref[1].astype(F32)) + r_ref[2].astype(F32)

    return pl.pallas_call(
        body, name=name,
        grid_spec=pltpu.PrefetchScalarGridSpec(
            num_scalar_prefetch=1, grid=(r // tr,),
            in_specs=[pl.BlockSpec((None, tr, c), lambda i, cc_ref: (cc_ref[0], i, 0)),
                      pl.BlockSpec((3, tr, c), lambda i, cc_ref: (0, i, 0))],
            out_specs=pl.BlockSpec((None, tr, c), lambda i, cc_ref: (cc_ref[1], i, 0))),
        out_shape=jax.ShapeDtypeStruct((2, r, c), F32),
        compiler_params=_params(("parallel",), 32),
    )(chip_core, sums, recv)


def _share_with_sibling(bufs):
    n = len(bufs)

    def copies(dst, sems, half):
        x, y, c, _ = _mesh_place()
        h = c if half == "mine" else 1 - c
        return [_remote(dst[a].at[h], dst[a].at[h], sems[0].at[a], sems[1].at[a], (x, y, 1 - c)) for a in range(n)]

    def start(ins, dst, sems):
        for cp in copies(dst, sems, "mine"):
            cp.start()

    def finish(ins, dst, sems):
        for cp in copies(dst, sems, "theirs"):
            cp.wait_recv()
        for cp in copies(dst, sems, "mine"):
            cp.wait_send()

    return _Exchange(bufs, [jax.ShapeDtypeStruct(b.shape, b.dtype) for b in bufs], {a: a for a in range(n)},
                     [pltpu.SemaphoreType.DMA((n,))] * 2, start, finish)


def _adam_math(w, g, m, v):
    m = ADAM_B1 * m + (1.0 - ADAM_B1) * g
    v = ADAM_B2 * v + (1.0 - ADAM_B2) * (g * g)
    m_hat = m / (1.0 - ADAM_B1 ** ADAM_STEP)
    v_hat = v / (1.0 - ADAM_B2 ** ADAM_STEP)
    delta = -ADAM_LR * (m_hat / (jnp.sqrt(v_hat) + ADAM_EPS) + ADAM_WD * w)
    return delta, m, v


def _small_allreduce_adamw(small, w_vec, m_vec, v_vec):
    width = small.shape[1]
    n_par = w_vec.shape[1]

    def body(s_ref, w_ref, m_ref, v_ref, loss_ref, g_ref, d_ref, nm_ref, nv_ref, gather_ref, send_sems, recv_sems):
        x, y, c = lax.axis_index("x"), lax.axis_index("y"), lax.axis_index("c")
        me = 4 * x + 2 * y + c
        gather_ref[me] = s_ref[...]
        copies = []
        for r in range(1, 8):
            bx, by, bc = (r >> 2) & 1, (r >> 1) & 1, r & 1
            peer = (x ^ bx, y ^ by, c ^ bc)
            cp = pltpu.make_async_remote_copy(
                src_ref=s_ref, dst_ref=gather_ref.at[me], send_sem=send_sems.at[r - 1], recv_sem=recv_sems.at[r - 1],
                device_id=peer, device_id_type=MESH)
            cp.start()
            copies.append(cp)
        for r in range(1, 8):
            bx, by, bc = (r >> 2) & 1, (r >> 1) & 1, r & 1
            theirs = gather_ref.at[4 * (x ^ bx) + 2 * (y ^ by) + (c ^ bc)]
            pltpu.make_async_remote_copy(
                src_ref=theirs, dst_ref=theirs, send_sem=send_sems.at[r - 1], recv_sem=recv_sems.at[r - 1],
                device_id=(x ^ bx, y ^ by, c ^ bc), device_id_type=MESH).wait_recv()
        for cp in copies:
            cp.wait_send()
        tot = gather_ref[0]
        for d in range(1, 8):
            tot = tot + gather_ref[d]
        tot = jnp.sum(tot, axis=0, keepdims=True)
        sq = jnp.sum(tot[:, n_par:], axis=1, keepdims=True)
        loss_ref[...] = jnp.broadcast_to(sq * (0.5 / D_MODEL), loss_ref.shape)
        g = tot[:, :n_par]
        g_ref[...] = g
        d_ref[...], nm_ref[...], nv_ref[...] = _adam_math(w_ref[...], g, m_ref[...], v_ref[...])

    vm = pl.BlockSpec(memory_space=pltpu.VMEM)
    vec = jax.ShapeDtypeStruct((1, n_par), F32)
    return pl.pallas_call(
        body, name="small_allreduce_adamw",
        in_specs=[vm] * 4, out_specs=[vm] * 5,
        out_shape=[jax.ShapeDtypeStruct((1, 128), F32), vec, vec, vec, vec],
        scratch_shapes=[pltpu.VMEM((8, 8, width), F32), pltpu.SemaphoreType.DMA((7,)), pltpu.SemaphoreType.DMA((7,))],
    )(small, w_vec, m_vec, v_vec)


def _adamw(w, g, m, v, name):
    r, c = w.shape
    tr = min(r, 256)

    def body(w_ref, g_ref, m_ref, v_ref, d_ref, nm_ref, nv_ref):
        d_ref[...], nm_ref[...], nv_ref[...] = _adam_math(w_ref[...], g_ref[...], m_ref[...], v_ref[...])

    spec = pl.BlockSpec((tr, c), lambda i: (i, 0))
    shape = jax.ShapeDtypeStruct((r, c), F32)
    return pl.pallas_call(
        body, name=name, grid=(r // tr,),
        in_specs=[spec] * 4, out_specs=[spec] * 3, out_shape=[shape] * 3,
        compiler_params=_params(("parallel",), 32),
    )(w, g, m, v)


def kernel(x, w_in, w_pool, pool_scale, w_out, ln_gain, ln_bias, loss_target, m_w_in, m_w_pool, m_pool_scale, m_w_out, m_ln_gain, m_ln_bias, v_w_in, v_w_pool, v_pool_scale, v_w_out, v_ln_gain, v_ln_bias):
    xi, yi, ci = lax.axis_index("x"), lax.axis_index("y"), lax.axis_index("c")
    chip = (2 * xi + yi).astype(jnp.int32).reshape(1)
    core = ci.astype(jnp.int32).reshape(1)
    n_groups = len(POOL_WINDOWS)
    shard_c = w_pool.shape[2]

    w_in_b = _cast_bf16(w_in[0], chip, "cast_w_in", 256)
    w_out_b = _cast_bf16(w_out[0], chip, "cast_w_out", 256)
    w_pool_b = _cast_bf16(w_pool[0].reshape(n_groups * shard_c, POOL_GROUP_DIM), chip, "cast_w_pool", 256)
    w_in_g = _run_exchange(_allgather_weights([w_in_b]), "allgather_w_in")[0]

    chip_core = jnp.concatenate([chip, core])
    g_x, full_in, full_out, full_pool, small = _step(
        x[0], loss_target[0], w_in_g, [w_out_b, w_pool_b], pool_scale, ln_gain, ln_bias, (core, chip_core))
    half_c = shard_c // 2
    grad_w_in = full_in.reshape(D_MODEL, SHARD_IN)
    grad_w_out = full_out.reshape(D_MODEL // N_SHARDS, D_MODEL)
    grad_w_pool = (full_pool.reshape(2, n_groups, half_c, POOL_GROUP_DIM).transpose(1, 0, 2, 3)
                   .reshape(n_groups * shard_c, POOL_GROUP_DIM))

    d_in, nm_in, nv_in = _adamw(w_in[0], grad_w_in, m_w_in[0], v_w_in[0], "adamw_w_in")
    d_out, nm_out, nv_out = _adamw(w_out[0], grad_w_out, m_w_out[0], v_w_out[0], "adamw_w_out")
    flat = lambda t: t[0].reshape(n_groups * shard_c, POOL_GROUP_DIM)
    d_pool, nm_pool, nv_pool = _adamw(flat(w_pool), grad_w_pool, flat(m_w_pool), flat(v_w_pool), "adamw_w_pool")

    cat = lambda a, b, c: jnp.concatenate([a, b, c], axis=1)
    loss_v, g_vec, d_vec, nm_vec, nv_vec = _small_allreduce_adamw(
        small, cat(pool_scale, ln_gain, ln_bias), cat(m_pool_scale, m_ln_gain, m_ln_bias),
        cat(v_pool_scale, v_ln_gain, v_ln_bias))

    def split(vec):
        return vec[:, :D_POOL], vec[:, D_POOL:D_POOL + D_MODEL], vec[:, D_POOL + D_MODEL:]

    g_scale, g_gain, g_bias = split(g_vec)
    d_scale, d_gain, d_bias = split(d_vec)
    nm_scale, nm_gain, nm_bias = split(nm_vec)
    nv_scale, nv_gain, nv_bias = split(nv_vec)
    pool_shape = w_pool.shape
    return (loss_v[0, 0], g_x[None],
            grad_w_in[None], grad_w_pool.reshape(pool_shape), g_scale, grad_w_out[None], g_gain, g_bias,
            d_in[None], d_pool.reshape(pool_shape), d_scale, d_out[None], d_gain, d_bias,
            nm_in[None], nm_pool.reshape(pool_shape), nm_scale, nm_out[None], nm_gain, nm_bias,
            nv_in[None], nv_pool.reshape(pool_shape), nv_scale, nv_out[None], nv_gain, nv_bias)
```

```python
import functools

import jax
import jax.numpy as jnp
from jax import lax
from jax.experimental import pallas as pl
from jax.experimental.pallas import tpu as pltpu

F32 = jnp.float32
BF16 = jnp.bfloat16
MESH = pl.DeviceIdType.MESH
ANY = pl.BlockSpec(memory_space=pl.ANY)

D_MODEL = 2048
D_ATTN = 1024
D_POOL = 1024
HEAD_DIM = 128
N_HEADS = 8
ROPE_DIM = 32
ROPE_THETA = 500000.0
DILATIONS = (1, 4, 16)
KEY_BLOCK = 128
CHUNK = 2 * KEY_BLOCK
STAT_LANES = 128
POOL_WINDOWS = (2, 4, 8, 16)
POOL_GROUP_DIM = 256
POOL_HALO = 16
D_QKV = 3 * D_ATTN
D_UG = D_POOL + D_MODEL
D_IN = D_QKV + D_UG
N_SHARDS = 4
SHARD_IN = D_IN // N_SHARDS
LN_EPS = 1e-5
DEEPNORM_ALPHA = 2.0 ** 0.25
ADAM_LR = 0.001
ADAM_B1 = 0.9
ADAM_B2 = 0.999
ADAM_EPS = 1e-08
ADAM_WD = 0.01
ADAM_STEP = 10
NEG = -1e30
MIB = 1024 * 1024


def _params(sem, vmem_mib):
    return pltpu.CompilerParams(dimension_semantics=sem, vmem_limit_bytes=vmem_mib * MIB)


class _Exchange:
    def __init__(self, operands, out_shape, aliases, sems, start, finish):
        self.operands, self.out_shape, self.aliases, self.sems = list(operands), list(out_shape), dict(aliases), list(sems)
        self.start, self.finish = start, finish


def _run_exchange(comm, name):
    n_in, n_out = len(comm.operands), len(comm.out_shape)

    def body(*refs):
        ins, outs, sems = refs[:n_in], refs[n_in:n_in + n_out], refs[n_in + n_out:]
        comm.start(ins, outs, sems)
        comm.finish(ins, outs, sems)

    return pl.pallas_call(
        body, name=name, in_specs=[ANY] * n_in, out_specs=[ANY] * n_out, out_shape=comm.out_shape,
        input_output_aliases=comm.aliases, scratch_shapes=comm.sems,
    )(*comm.operands)


def _call(body, *, name, grid, in_specs, out_specs, out_shape, scratch_shapes, semantics, vmem_mib, args,
          aliases=None, comm=None, prefetch=()):
    aliases = dict(aliases or {})
    n_pre, n_in, n_out, n_scr = len(prefetch), len(in_specs), len(out_specs), len(scratch_shapes)
    c_in, c_out = (len(comm.operands), len(comm.out_shape)) if comm else (0, 0)
    c_shapes, c_sems, c_operands = (comm.out_shape, comm.sems, comm.operands) if comm else ([], [], [])

    def hosted(*refs):
        pre, refs = refs[:n_pre], refs[n_pre:]
        a = n_in
        b = a + c_in
        c = b + n_out
        d = c + c_out
        e = d + n_scr
        if comm is None:
            body(*pre, *refs)
            return
        ids = [pl.program_id(k) for k in range(len(grid))]
        first = functools.reduce(jnp.logical_and, [i == 0 for i in ids])
        last = functools.reduce(jnp.logical_and, [i == g - 1 for i, g in zip(ids, grid)])

        @pl.when(first)
        def _():
            comm.start(refs[a:b], refs[c:d], refs[e:])

        body(*pre, *refs[:a], *refs[b:c], *refs[d:e])

        @pl.when(last)
        def _():
            comm.finish(refs[a:b], refs[c:d], refs[e:])

    if comm:
        semantics = ("arbitrary",) * len(grid)
        for i, o in comm.aliases.items():
            aliases[n_pre + n_in + i] = n_out + o
    outs = pl.pallas_call(
        hosted, name=name,
        grid_spec=pltpu.PrefetchScalarGridSpec(
            num_scalar_prefetch=n_pre, grid=grid, in_specs=list(in_specs) + [ANY] * c_in,
            out_specs=list(out_specs) + [ANY] * c_out, scratch_shapes=list(scratch_shapes) + c_sems),
        out_shape=list(out_shape) + c_shapes, input_output_aliases=aliases,
        compiler_params=_params(semantics, vmem_mib),
    )(*prefetch, *args, *c_operands)
    return list(outs[:n_out]), list(outs[n_out:])


def _dot_nn(a, b):
    return jnp.dot(a, b, preferred_element_type=F32)


def _dot_nt(a, b):
    return lax.dot_general(a, b, (((1,), (1,)), ((), ())), preferred_element_type=F32)


def _dot_tn(a, b):
    return lax.dot_general(a, b, (((0,), (0,)), ((), ())), preferred_element_type=F32)


def _fold_rows(a):
    r, c = a.shape
    return jnp.sum(a.reshape(r // 8, 8, c), axis=0)


def _cast_bf16(a, chip, name, rows):
    r, c = a.shape

    def body(chip_ref, a_ref, o_ref):
        o_ref[...] = a_ref[...].astype(BF16)

    return pl.pallas_call(
        body, name=name,
        grid_spec=pltpu.PrefetchScalarGridSpec(
            num_scalar_prefetch=1, grid=(r // rows,),
            in_specs=[pl.BlockSpec((rows, c), lambda i, chip_ref: (i, 0))],
            out_specs=pl.BlockSpec((None, rows, c), lambda i, chip_ref: (chip_ref[0], i, 0))),
        out_shape=jax.ShapeDtypeStruct((N_SHARDS, r, c), BF16),
        compiler_params=_params(("parallel",), 32),
    )(chip, a)


def _mesh_place():
    x, y, c = lax.axis_index("x"), lax.axis_index("y"), lax.axis_index("c")
    return x, y, c, [(1 - x, y), (x, 1 - y), (1 - x, 1 - y)]


def _remote(src, dst, send_sem, recv_sem, to):
    return pltpu.make_async_remote_copy(src_ref=src, dst_ref=dst, send_sem=send_sem, recv_sem=recv_sem,
                                        device_id=to, device_id_type=MESH)


def _allgather_weights(bufs):
    n = len(bufs)

    def half(a, core):
        rows = bufs[a].shape[1] // 2
        return pl.ds(core * rows, rows)

    def ici_copies(dst, sems):
        x, y, c, chips = _mesh_place()
        own = lambda a: dst[a].at[2 * x + y, half(a, c)]
        return [_remote(own(a), own(a), sems[0].at[a, k], sems[1].at[a, k], (cx, cy, c))
                for a in range(n) for k, (cx, cy) in enumerate(chips)]

    def start(ins, dst, sems):
        for cp in ici_copies(dst, sems):
            cp.start()

    def finish(ins, dst, sems):
        x, y, c, chips = _mesh_place()
        sibling = (x, y, 1 - c)
        passed_on = []
        for k, (cx, cy) in enumerate(chips):
            for a in range(n):
                landed = dst[a].at[2 * cx + cy, half(a, c)]
                _remote(landed, landed, sems[0].at[a, k], sems[1].at[a, k], (cx, cy, c)).wait_recv()
                cp = _remote(landed, landed, sems[2].at[a, k], sems[3].at[a, k], sibling)
                cp.start()
                passed_on.append(cp)
        for k, (cx, cy) in enumerate(chips):
            for a in range(n):
                passed = dst[a].at[2 * cx + cy, half(a, 1 - c)]
                _remote(passed, passed, sems[2].at[a, k], sems[3].at[a, k], sibling).wait_recv()
        for cp in ici_copies(dst, sems) + passed_on:
            cp.wait_send()

    return _Exchange(bufs, [jax.ShapeDtypeStruct(b.shape, b.dtype) for b in bufs], {a: a for a in range(n)},
                     [pltpu.SemaphoreType.DMA((n, 3))] * 4, start, finish)


def _rope_tables(seq):
    half = ROPE_DIM // 2
    inv_freq = ROPE_THETA ** (-(2.0 * jnp.arange(half, dtype=F32)) / ROPE_DIM)
    ang = jnp.arange(seq, dtype=jnp.int32).astype(F32)[:, None] * inv_freq[None, :]
    cos, sin = jnp.cos(ang), jnp.sin(ang)
    pad = jnp.zeros((seq, HEAD_DIM - ROPE_DIM), F32)
    zeros = jnp.zeros((seq, half), F32)
    c_tab = jnp.concatenate([cos, cos, pad + 1.0], axis=1)
    up_tab = jnp.concatenate([-sin, zeros, pad], axis=1)
    down_tab = jnp.concatenate([zeros, sin, pad], axis=1)
    return c_tab, up_tab, down_tab


def _rotate_heads(t, c_tab, up_tab, down_tab):
    outs = []
    for h in range(t.shape[1] // HEAD_DIM):
        th = t[:, h * HEAD_DIM:(h + 1) * HEAD_DIM]
        up = pltpu.roll(th, HEAD_DIM - ROPE_DIM // 2, axis=1)
        down = pltpu.roll(th, ROPE_DIM // 2, axis=1)
        outs.append(th * c_tab + up * up_tab + down * down_tab)
    return outs[0] if len(outs) == 1 else jnp.concatenate(outs, axis=1)


def _to_pattern(slabs_ref, dst_ref, dil, dtype):
    n_slabs, rows, _ = slabs_ref.shape
    for s in range(n_slabs):
        for r in range(dil):
            dst_ref[r, :, s * 128:(s + 1) * 128] = slabs_ref[s, pl.ds(r, rows // dil, dil), :].astype(dtype)


def _from_pattern(src_ref, slabs_ref, dil):
    n_slabs, rows, _ = slabs_ref.shape
    for s in range(n_slabs):
        for r in range(dil):
            slabs_ref[s, pl.ds(r, rows // dil, dil), :] = src_ref[r, :, s * 128:(s + 1) * 128]


def _store_slabs(slabs_ref, value):
    for s in range(slabs_ref.shape[0]):
        slabs_ref[s] = value[:, s * 128:(s + 1) * 128]


def _in_proj_qkv(x, w_in_g, tabs, comm=None):
    seq = x.shape[0]
    tm, tn = 512, SHARD_IN
    heads = tn // HEAD_DIM
    k_heads_in_second = 2 * D_ATTN // HEAD_DIM - heads
    d4, d16 = DILATIONS[1], DILATIONS[2]

    def body(x_ref, w_ref, c_ref, up_ref, down_ref, o1_ref, o4_ref, o16_ref, res_ref):
        shard = pl.program_id(0)
        acc = _dot_nn(x_ref[...].astype(BF16), w_ref[...])

        def emit(rotated_heads):
            for h in range(heads):
                th = acc[:, h * HEAD_DIM:(h + 1) * HEAD_DIM]
                if h < rotated_heads:
                    th = _rotate_heads(th, c_ref[...], up_ref[...], down_ref[...])
                res_ref[h] = th
                o1_ref[:, h * HEAD_DIM:(h + 1) * HEAD_DIM] = th.astype(BF16)

        @pl.when(shard == 0)
        def _():
            emit(heads)

        @pl.when(shard == 1)
        def _():
            emit(k_heads_in_second)

        _to_pattern(res_ref, o4_ref, d4, BF16)
        _to_pattern(res_ref, o16_ref, d16, BF16)

    tab_spec = pl.BlockSpec((tm, HEAD_DIM), lambda s, i: (i, 0))
    (o1, o4, o16), exchanged = _call(
        body, name="in_proj_qkv", grid=(D_QKV // tn, seq // tm),
        in_specs=[pl.BlockSpec((tm, D_MODEL), lambda s, i: (i, 0)),
                  pl.BlockSpec((None, D_MODEL, tn), lambda s, i: (s, 0, 0)),
                  tab_spec, tab_spec, tab_spec],
        out_specs=[pl.BlockSpec((tm, tn), lambda s, i: (i, s)),
                   pl.BlockSpec((d4, tm // d4, tn), lambda s, i: (0, i, s)),
                   pl.BlockSpec((d16, tm // d16, tn), lambda s, i: (0, i, s))],
        out_shape=[jax.ShapeDtypeStruct((seq, D_QKV), BF16),
                   jax.ShapeDtypeStruct((d4, seq // d4, D_QKV), BF16),
                   jax.ShapeDtypeStruct((d16, seq // d16, D_QKV), BF16)],
        scratch_shapes=[pltpu.VMEM((heads, tm, HEAD_DIM), F32)],
        semantics=("parallel", "parallel"), vmem_mib=52, args=(x, w_in_g, *tabs), comm=comm)
    return [o1[None], o4, o16], exchanged


def _in_proj_pool_gate(x, w_in_g):
    seq = x.shape[0]
    tm, tn = 512, SHARD_IN
    first_shard = D_QKV // tn

    def body(x_ref, w_ref, o_ref):
        o_ref[...] = _dot_nn(x_ref[...].astype(BF16), w_ref[...])

    return pl.pallas_call(
        body, name="in_proj_pool_gate", grid=(D_UG // tn, seq // tm),
        in_specs=[pl.BlockSpec((tm, D_MODEL), lambda s, i: (i, 0)),
                  pl.BlockSpec((None, D_MODEL, tn), lambda s, i: (s + first_shard, 0, 0))],
        out_specs=pl.BlockSpec((tm, tn), lambda s, i: (i, s)),
        out_shape=jax.ShapeDtypeStruct((seq, D_UG), F32),
        compiler_params=_params(("parallel", "parallel"), 48),
    )(x, w_in_g)


def _band_masks():
    row = lax.broadcasted_iota(jnp.int32, (KEY_BLOCK, KEY_BLOCK), 0)
    col = lax.broadcasted_iota(jnp.int32, (KEY_BLOCK, KEY_BLOCK), 1)
    return col <= row, col >= row


def _attn_fwd(qkv, name):
    dil, n, _ = qkv.shape
    scale = HEAD_DIM ** -0.5
    lo, hi = slice(0, KEY_BLOCK), slice(KEY_BLOCK, CHUNK)

    def body(q_ref, k_ref, v_ref, kb_ref, vb_ref, o_ref, st_ref):
        i = pl.program_id(1)
        cur_mask, prev_mask = _band_masks()
        before_mask = jnp.logical_and(prev_mask, i > 0)
        lane = lax.broadcasted_iota(jnp.int32, (KEY_BLOCK, STAT_LANES), 1)
        tasks = [(rows, h) for rows in (lo, hi) for h in range(N_HEADS)]
        head = lambda h: slice(h * HEAD_DIM, (h + 1) * HEAD_DIM)

        def prev_of(rows, h):
            if rows is lo:
                return kb_ref[:, head(h)], vb_ref[:, head(h)], before_mask
            return k_ref[lo, head(h)], v_ref[lo, head(h)], prev_mask

        scores = []
        for rows, h in tasks:
            q = q_ref[rows, head(h)]
            scores.append((_dot_nt(q, prev_of(rows, h)[0]), _dot_nt(q, k_ref[rows, head(h)])))
        probs = []
        for (rows, h), (qk_prev, qk_cur) in zip(tasks, scores):
            s_prev = jnp.where(prev_of(rows, h)[2], qk_prev * scale, NEG)
            s_cur = jnp.where(cur_mask, qk_cur * scale, NEG)
            m = jnp.max(jnp.maximum(s_prev, s_cur), axis=-1, keepdims=True)
            p_prev = jnp.exp(s_prev - m)
            p_cur = jnp.exp(s_cur - m)
            den = jnp.sum(p_prev + p_cur, axis=-1, keepdims=True)
            probs.append((p_prev.astype(BF16), p_cur.astype(BF16), den, m + jnp.log(den)))
        stats = [jnp.zeros((KEY_BLOCK, STAT_LANES), F32), jnp.zeros((KEY_BLOCK, STAT_LANES), F32)]
        for (rows, h), (p_prev, p_cur, den, lse) in zip(tasks, probs):
            o = _dot_nn(p_cur, v_ref[rows, head(h)]) + _dot_nn(p_prev, prev_of(rows, h)[1])
            o_ref[rows, head(h)] = o / den
            b = 0 if rows is lo else 1
            stats[b] = jnp.where(lane == h, lse, stats[b])
        st_ref[lo, :] = stats[0]
        st_ref[hi, :] = stats[1]

    main = lambda cb: pl.BlockSpec((None, CHUNK, D_ATTN), lambda r, i: (r, i, cb))
    before = lambda cb: pl.BlockSpec((None, KEY_BLOCK, D_ATTN), lambda r, i: (r, jnp.maximum(2 * i - 1, 0), cb))
    return pl.pallas_call(
        body, name=name, grid=(dil, n // CHUNK),
        in_specs=[main(0), main(1), main(2), before(1), before(2)],
        out_specs=[main(0), pl.BlockSpec((None, CHUNK, STAT_LANES), lambda r, i: (r, i, 0))],
        out_shape=[jax.ShapeDtypeStruct((dil, n, D_ATTN), F32), jax.ShapeDtypeStruct((dil, n, STAT_LANES), F32)],
        compiler_params=_params(("parallel", "parallel"), 40),
    )(qkv, qkv, qkv, qkv, qkv)


def _attn_bwd(qkv, do, stats, name, comm=None):
    dil, n, _ = qkv.shape
    n_blocks = n // KEY_BLOCK
    last = n // CHUNK - 1
    scale = HEAD_DIM ** -0.5
    lo, hi = slice(0, KEY_BLOCK), slice(KEY_BLOCK, CHUNK)

    def body(q_ref, k_ref, v_ref, kb_ref, vb_ref, qa_ref, do_ref, doa_ref, st_ref, sta_ref, dq_ref, dk_ref, dv_ref):
        i = pl.program_id(1)
        cur_mask, prev_mask = _band_masks()
        before_mask = jnp.logical_and(prev_mask, i > 0)
        after_mask = jnp.logical_and(prev_mask, i < last)

        def operands(h):
            cols = slice(h * HEAD_DIM, (h + 1) * HEAD_DIM)
            lse_c, del_c = slice(h, h + 1), slice(N_HEADS + h, N_HEADS + h + 1)
            q = {"0": q_ref[lo, cols], "1": q_ref[hi, cols], "a": qa_ref[:, cols]}
            k = {"0": k_ref[lo, cols], "1": k_ref[hi, cols], "b": kb_ref[:, cols]}
            v = {"0": v_ref[lo, cols], "1": v_ref[hi, cols], "b": vb_ref[:, cols]}
            do = {"0": do_ref[lo, cols], "1": do_ref[hi, cols], "a": doa_ref[:, cols]}
            st = {"0": (st_ref[lo, lse_c], st_ref[lo, del_c]), "1": (st_ref[hi, lse_c], st_ref[hi, del_c]),
                  "a": (sta_ref[:, lse_c], sta_ref[:, del_c])}
            return cols, q, k, v, do, st

        pairs = [("0", "b", before_mask), ("0", "0", cur_mask), ("1", "0", prev_mask), ("1", "1", cur_mask),
                 ("a", "1", after_mask)]
        group = N_HEADS // 2
        for first_head in range(0, N_HEADS, group):
            heads = range(first_head, first_head + group)
            raw = {}
            for h in heads:
                _, q, k, v, do, _ = operands(h)
                for qi, ki, _ in pairs:
                    raw[h, qi, ki] = (_dot_nt(q[qi], k[ki]), _dot_nt(do[qi], v[ki]))
            grads = {}
            for h in heads:
                st = operands(h)[5]
                for qi, ki, mask in pairs:
                    qk, dp = raw[h, qi, ki]
                    lse, delta = st[qi]
                    p = jnp.exp(jnp.where(mask, qk * scale, NEG) - lse)
                    grads[h, qi, ki] = (p.astype(BF16), (p * (dp - delta) * scale).astype(BF16))
            for h in heads:
                cols, q, k, v, do, _ = operands(h)
                p = lambda qi, ki: grads[h, qi, ki][0]
                ds = lambda qi, ki: grads[h, qi, ki][1]
                dq_ref[lo, cols] = _dot_nn(ds("0", "b"), k["b"]) + _dot_nn(ds("0", "0"), k["0"])
                dq_ref[hi, cols] = _dot_nn(ds("1", "0"), k["0"]) + _dot_nn(ds("1", "1"), k["1"])
                dk_ref[lo, cols] = _dot_tn(ds("0", "0"), q["0"]) + _dot_tn(ds("1", "0"), q["1"])
                dk_ref[hi, cols] = _dot_tn(ds("1", "1"), q["1"]) + _dot_tn(ds("a", "1"), q["a"])
                dv_ref[lo, cols] = _dot_tn(p("0", "0"), do["0"]) + _dot_tn(p("1", "0"), do["1"])
                dv_ref[hi, cols] = _dot_tn(p("1", "1"), do["1"]) + _dot_tn(p("a", "1"), do["a"])

    def spec(rows, width, row_of, cb):
        return pl.BlockSpec((None, rows, width), lambda r, i: (r, row_of(i), cb))

    same = lambda i: i
    before = lambda i: jnp.maximum(2 * i - 1, 0)
    after = lambda i: jnp.minimum(2 * i + 2, n_blocks - 1)
    out = spec(CHUNK, D_ATTN, same, 0)
    return _call(
        body, name=name, grid=(dil, n // CHUNK),
        in_specs=[spec(CHUNK, D_ATTN, same, 0), spec(CHUNK, D_ATTN, same, 1), spec(CHUNK, D_ATTN, same, 2),
                  spec(KEY_BLOCK, D_ATTN, before, 1), spec(KEY_BLOCK, D_ATTN, before, 2),
                  spec(KEY_BLOCK, D_ATTN, after, 0),
                  spec(CHUNK, D_ATTN, same, 0), spec(KEY_BLOCK, D_ATTN, after, 0),
                  spec(CHUNK, STAT_LANES, same, 0), spec(KEY_BLOCK, STAT_LANES, after, 0)],
        out_specs=[out, out, out],
        out_shape=[jax.ShapeDtypeStruct((dil, n, D_ATTN), F32)] * 3,
        scratch_shapes=[], semantics=("parallel", "parallel"), vmem_mib=40,
        args=(qkv, qkv, qkv, qkv, qkv, qkv, do, do, stats, stats), comm=comm)


def _window_sums(ext, window, backward):
    rows = ext.shape[0]
    acc, span = ext, 1
    while span < window:
        acc = acc + pltpu.roll(acc, (rows - span) if backward else span, axis=0)
        span *= 2
    return acc


def _mix_gate(o_list, st_list, hug, w_pool_g, pool_scale):
    seq = hug.shape[0]
    tm = 256
    halo_blocks = tm // POOL_HALO
    d4, d16 = DILATIONS[1], DILATIONS[2]

    def body(o1_ref, o4_ref, o16_ref, l1_ref, l4_ref, l16_ref, u_ref, halo_ref, ga_ref, gp_ref, wp_ref, sc_ref,
             y_ref, mix_ref, lse_ref, pooled_ref, n4_ref, n16_ref, nl4_ref, nl16_ref):
        i = pl.program_id(0)
        _from_pattern(o4_ref, n4_ref, d4)
        _from_pattern(o16_ref, n16_ref, d16)
        _from_pattern(l4_ref, nl4_ref, d4)
        _from_pattern(l16_ref, nl16_ref, d16)
        la, lb, lc = l1_ref[...], nl4_ref[0], nl16_ref[0]
        mx = jnp.maximum(jnp.maximum(la, lb), lc)
        ea, eb, ec = jnp.exp(la - mx), jnp.exp(lb - mx), jnp.exp(lc - mx)
        tot = ea + eb + ec
        lse_ref[...] = mx + jnp.log(tot)
        wa, wb, wc = ea / tot, eb / tot, ec / tot
        ga = ga_ref[...]
        silu_a = ga * jax.nn.sigmoid(ga)
        for h in range(N_HEADS):
            cols = slice(h * HEAD_DIM, (h + 1) * HEAD_DIM)
            hc = slice(h, h + 1)
            attn = wa[:, hc] * o1_ref[:, cols] + wb[:, hc] * n4_ref[h] + wc[:, hc] * n16_ref[h]
            mix_ref[:, cols] = attn
            y_ref[:, cols] = (attn * silu_a[:, cols]).astype(BF16)

        u = u_ref[...]
        halo = jnp.where(i > 0, halo_ref[...], 0.0)
        ext = jnp.concatenate([halo, u], axis=0)
        pos = i * tm + lax.broadcasted_iota(jnp.int32, (tm, 1), 0)
        gp = gp_ref[...]
        gated_scale = sc_ref[...] * (gp * jax.nn.sigmoid(gp))
        for g, window in enumerate(POOL_WINDOWS):
            cols = slice(g * POOL_GROUP_DIM, (g + 1) * POOL_GROUP_DIM)
            sums = _window_sums(ext[:, cols], window, backward=False)[POOL_HALO:, :]
            count = jnp.minimum(pos + 1, window).astype(F32)
            pooled = (sums / count - u[:, cols]).astype(BF16)
            pooled_ref[:, cols] = pooled
            pre = _dot_nn(pooled, wp_ref[g])
            out_cols = slice(D_ATTN + g * POOL_GROUP_DIM, D_ATTN + (g + 1) * POOL_GROUP_DIM)
            mix_ref[:, out_cols] = pre
            y_ref[:, out_cols] = (pre * gated_scale[:, cols]).astype(BF16)

    row = lambda width, cb=0: pl.BlockSpec((tm, width), lambda i: (i, cb))
    pat = lambda d, width: pl.BlockSpec((d, tm // d, width), lambda i: (0, i, 0))
    return pl.pallas_call(
        body, name="mix_gate", grid=(seq // tm,),
        in_specs=[row(D_ATTN), pat(d4, D_ATTN), pat(d16, D_ATTN),
                  row(STAT_LANES), pat(d4, STAT_LANES), pat(d16, STAT_LANES),
                  row(D_POOL),
                  pl.BlockSpec((POOL_HALO, D_POOL), lambda i: (jnp.maximum(i * halo_blocks - 1, 0), 0)),
                  row(D_ATTN, 1), row(D_POOL, 2),
                  pl.BlockSpec((len(POOL_WINDOWS), POOL_GROUP_DIM, POOL_GROUP_DIM), lambda i: (0, 0, 0)),
                  pl.BlockSpec((1, D_POOL), lambda i: (0, 0))],
        out_specs=[row(D_MODEL), row(D_MODEL), row(STAT_LANES), row(D_POOL)],
        out_shape=[jax.ShapeDtypeStruct((seq, D_MODEL), BF16), jax.ShapeDtypeStruct((seq, D_MODEL), F32),
                   jax.ShapeDtypeStruct((seq, STAT_LANES), F32), jax.ShapeDtypeStruct((seq, D_POOL), BF16)],
        scratch_shapes=[pltpu.VMEM((N_HEADS, tm, HEAD_DIM), F32), pltpu.VMEM((N_HEADS, tm, HEAD_DIM), F32),
                        pltpu.VMEM((1, tm, STAT_LANES), F32), pltpu.VMEM((1, tm, STAT_LANES), F32)],
        compiler_params=_params(("parallel",), 48),
    )(o_list[0][0], o_list[1], o_list[2], st_list[0][0], st_list[1], st_list[2],
      hug, hug, hug, hug, w_pool_g, pool_scale)


def _out_proj_loss(y, w_out_g, x, target, gain, bias):
    seq = x.shape[0]
    tm = 256

    def body(y_ref, w_ref, x_ref, t_ref, g_ref, b_ref, dz_ref, dzb_ref, gg_ref, gb_ref, loss_ref):
        @pl.when(pl.program_id(0) == 0)
        def _():
            gg_ref[...] = jnp.zeros_like(gg_ref)
            gb_ref[...] = jnp.zeros_like(gb_ref)
            loss_ref[...] = jnp.zeros_like(loss_ref)

        z = DEEPNORM_ALPHA * x_ref[...] + _dot_nn(y_ref[...], w_ref[...])
        mu = jnp.mean(z, axis=-1, keepdims=True)
        zc = z - mu
        rstd = lax.rsqrt(jnp.mean(zc * zc, axis=-1, keepdims=True) + LN_EPS)
        xhat = zc * rstd
        gain_v = g_ref[...]
        diff = xhat * gain_v + b_ref[...] - t_ref[...]
        sq = _fold_rows(diff * diff)
        part = sq[:, :128]
        for k in range(1, D_MODEL // 128):
            part = part + sq[:, k * 128:(k + 1) * 128]
        loss_ref[...] += part
        dln = diff * (1.0 / D_MODEL)
        gg_ref[...] += _fold_rows(dln * xhat)
        gb_ref[...] += _fold_rows(dln)
        dxhat = dln * gain_v
        dz = rstd * (dxhat - jnp.mean(dxhat, axis=-1, keepdims=True)
                     - xhat * jnp.mean(dxhat * xhat, axis=-1, keepdims=True))
        dz_ref[...] = dz
        dzb_ref[...] = dz.astype(BF16)

    row = lambda: pl.BlockSpec((tm, D_MODEL), lambda i: (i, 0))
    vec = lambda: pl.BlockSpec((1, D_MODEL), lambda i: (0, 0))
    acc = lambda width: pl.BlockSpec((8, width), lambda i: (0, 0))
    return pl.pallas_call(
        body, name="out_proj_loss", grid=(seq // tm,),
        in_specs=[row(), pl.BlockSpec((D_MODEL, D_MODEL), lambda i: (0, 0)), row(), row(), vec(), vec()],
        out_specs=[row(), row(), acc(D_MODEL), acc(D_MODEL), acc(128)],
        out_shape=[jax.ShapeDtypeStruct((seq, D_MODEL), F32), jax.ShapeDtypeStruct((seq, D_MODEL), BF16),
                   jax.ShapeDtypeStruct((8, D_MODEL), F32), jax.ShapeDtypeStruct((8, D_MODEL), F32),
                   jax.ShapeDtypeStruct((8, 128), F32)],
        compiler_params=_params(("arbitrary",), 48),
    )(y, w_out_g.reshape(D_MODEL, D_MODEL), x, target, gain, bias)


def _dy_gate_bwd(dzb, w_out_g, hug, mixpre, pool_scale, lse_all):
    seq = dzb.shape[0]
    tm = 256
    d4, d16 = DILATIONS[1], DILATIONS[2]

    def body(dz_ref, w_ref, ga_ref, gp_ref, mix_ref, sc_ref, lse_ref,
             dh_ref, dpo_ref, do1_ref, do4_ref, do16_ref, st1_ref, st4_ref, st16_ref, da_ref, st_ref):
        dy = _dot_nt(dz_ref[...], w_ref[...])
        ga = ga_ref[...]
        sig = jax.nn.sigmoid(ga)
        attn = mix_ref[:, :D_ATTN]
        dya = dy[:, :D_ATTN]
        dattn = dya * (ga * sig)
        dh_ref[:, :D_ATTN] = (dya * attn * (sig * (1.0 + ga * (1.0 - sig)))).astype(BF16)
        _store_slabs(da_ref, dattn)
        lane = lax.broadcasted_iota(jnp.int32, (tm, STAT_LANES), 1)
        stats = lse_ref[...]
        prod = dattn * attn
        for h in range(N_HEADS):
            delta = jnp.sum(prod[:, h * HEAD_DIM:(h + 1) * HEAD_DIM], axis=-1, keepdims=True)
            stats = jnp.where(lane == N_HEADS + h, delta, stats)
        st_ref[0] = stats
        do1_ref[...] = dattn.astype(BF16)
        st1_ref[...] = stats
        _to_pattern(da_ref, do4_ref, d4, BF16)
        _to_pattern(da_ref, do16_ref, d16, BF16)
        _to_pattern(st_ref, st4_ref, d4, F32)
        _to_pattern(st_ref, st16_ref, d16, F32)

        gp = gp_ref[...]
        sig = jax.nn.sigmoid(gp)
        dyp = dy[:, D_ATTN:]
        dpo_ref[...] = dyp * (gp * sig)
        dh_ref[:, D_ATTN:] = (dyp * (mix_ref[:, D_ATTN:] * sc_ref[...])
                              * (sig * (1.0 + gp * (1.0 - sig)))).astype(BF16)

    row = lambda width, cb=0: pl.BlockSpec((tm, width), lambda i: (i, cb))
    pat = lambda d, width: pl.BlockSpec((d, tm // d, width), lambda i: (0, i, 0))
    pat_shape = lambda d, width, dtype: jax.ShapeDtypeStruct((d, seq // d, width), dtype)
    outs = pl.pallas_call(
        body, name="dy_gate_bwd", grid=(seq // tm,),
        in_specs=[row(D_MODEL), pl.BlockSpec((D_MODEL, D_MODEL), lambda i: (0, 0)),
                  row(D_ATTN, 1), row(D_POOL, 2), row(D_MODEL), pl.BlockSpec((1, D_POOL), lambda i: (0, 0)),
                  row(STAT_LANES)],
        out_specs=[row(D_MODEL, D_IN // D_MODEL - 1), row(D_POOL),
                   row(D_ATTN), pat(d4, D_ATTN), pat(d16, D_ATTN),
                   row(STAT_LANES), pat(d4, STAT_LANES), pat(d16, STAT_LANES)],
        out_shape=[jax.ShapeDtypeStruct((seq, D_IN), BF16), jax.ShapeDtypeStruct((seq, D_POOL), F32),
                   jax.ShapeDtypeStruct((seq, D_ATTN), BF16), pat_shape(d4, D_ATTN, BF16), pat_shape(d16, D_ATTN, BF16),
                   jax.ShapeDtypeStruct((seq, STAT_LANES), F32), pat_shape(d4, STAT_LANES, F32),
                   pat_shape(d16, STAT_LANES, F32)],
        scratch_shapes=[pltpu.VMEM((N_HEADS, tm, HEAD_DIM), F32), pltpu.VMEM((1, tm, STAT_LANES), F32)],
        compiler_params=_params(("parallel",), 48),
    )(dzb, w_out_g.reshape(D_MODEL, D_MODEL), hug, hug, mixpre, pool_scale, lse_all)
    dh, dpo, do1, do4, do16, st1, st4, st16 = outs
    return dh, dpo, [do1[None], do4, do16], [st1[None], st4, st16]


def _pool_bwd(dh, dpo, mixpre, pooled, w_pool_g, pool_scale):
    seq = dpo.shape[0]
    tm = 256
    halo_blocks = tm // POOL_HALO
    last = seq // tm - 1
    n_groups = len(POOL_WINDOWS)

    def body(dh_in_ref, dpo_ref, halo_ref, pre_ref, pooled_ref, wp_ref, sc_ref, du_ref, gw_ref, gs_ref):
        i = pl.program_id(0)

        @pl.when(i == 0)
        def _():
            gw_ref[...] = jnp.zeros_like(gw_ref)
            gs_ref[...] = jnp.zeros_like(gs_ref)

        dpo = dpo_ref[...]
        scale = sc_ref[...]
        gs_ref[...] += _fold_rows(dpo * pre_ref[...])
        halo = jnp.where(i < last, halo_ref[...], 0.0)
        dpw = (jnp.concatenate([dpo, halo], axis=0) * scale).astype(BF16)
        pos = i * tm + lax.broadcasted_iota(jnp.int32, (tm + POOL_HALO, 1), 0)
        for g, window in enumerate(POOL_WINDOWS):
            cols = slice(g * POOL_GROUP_DIM, (g + 1) * POOL_GROUP_DIM)
            dpw_g = dpw[:, cols]
            gw_ref[g] += _dot_tn(pooled_ref[:, cols], dpw_g[:tm, :])
            dpooled = _dot_nt(dpw_g, wp_ref[g])
            count = jnp.minimum(pos + 1, window).astype(F32)
            sums = _window_sums(dpooled / count, window, backward=True)
            du_ref[:, cols] = (sums[:tm, :] - dpooled[:tm, :]).astype(BF16)

    row = lambda width, cb=0: pl.BlockSpec((tm, width), lambda i: (i, cb))
    return pl.pallas_call(
        body, name="pool_bwd", grid=(seq // tm,),
        in_specs=[ANY, row(D_POOL),
                  pl.BlockSpec((POOL_HALO, D_POOL),
                               lambda i: (jnp.minimum((i + 1) * halo_blocks, seq // POOL_HALO - 1), 0)),
                  row(D_POOL, 1), row(D_POOL),
                  pl.BlockSpec((n_groups, POOL_GROUP_DIM, POOL_GROUP_DIM), lambda i: (0, 0, 0)),
                  pl.BlockSpec((1, D_POOL), lambda i: (0, 0))],
        out_specs=[row(D_POOL, D_QKV // D_POOL),
                   pl.BlockSpec((n_groups, POOL_GROUP_DIM, POOL_GROUP_DIM), lambda i: (0, 0, 0)),
                   pl.BlockSpec((8, D_POOL), lambda i: (0, 0))],
        out_shape=[jax.ShapeDtypeStruct(dh.shape, dh.dtype),
                   jax.ShapeDtypeStruct((n_groups, POOL_GROUP_DIM, POOL_GROUP_DIM), F32),
                   jax.ShapeDtypeStruct((8, D_POOL), F32)],
        input_output_aliases={0: 0},
        compiler_params=_params(("arbitrary",), 40),
    )(dh, dpo, dpo, mixpre, pooled, w_pool_g, pool_scale)


def _sum_patterns(dh, parts, tabs, unrotate, col_block, name):
    seq = dh.shape[0]
    tm, tn = 256, 512
    per = D_ATTN // tn
    d4, d16 = DILATIONS[1], DILATIONS[2]

    def body(dh_in_ref, a1_ref, a4_ref, a16_ref, ct_ref, up_ref, down_ref, o_ref, n4_ref, n16_ref):
        _from_pattern(a4_ref, n4_ref, d4)
        _from_pattern(a16_ref, n16_ref, d16)
        for s in range(tn // HEAD_DIM):
            cols = slice(s * HEAD_DIM, (s + 1) * HEAD_DIM)
            tot = a1_ref[:, cols] + n4_ref[s] + n16_ref[s]
            if unrotate:
                tot = _rotate_heads(tot, ct_ref[...], -up_ref[...], -down_ref[...])
            o_ref[:, cols] = tot.astype(BF16)

    tab = pl.BlockSpec((tm, HEAD_DIM), lambda i, j: (i, 0))
    pat = lambda d: pl.BlockSpec((d, tm // d, tn), lambda i, j: (0, i, j))
    return pl.pallas_call(
        body, name=name, grid=(seq // tm, per),
        in_specs=[ANY, pl.BlockSpec((tm, tn), lambda i, j: (i, j)), pat(d4), pat(d16), tab, tab, tab],
        out_specs=pl.BlockSpec((tm, tn), lambda i, j: (i, col_block * per + j)),
        out_shape=jax.ShapeDtypeStruct(dh.shape, dh.dtype),
        scratch_shapes=[pltpu.VMEM((tn // HEAD_DIM, tm, HEAD_DIM), F32), pltpu.VMEM((tn // HEAD_DIM, tm, HEAD_DIM), F32)],
        input_output_aliases={0: 0},
        compiler_params=_params(("parallel", "parallel"), 32),
    )(dh, parts[0][0], parts[1], parts[2], *tabs)


def _grad_w_in(x, dh, half, name, comm=None):
    seq = x.shape[0]
    ts, td, te = 1024, D_MODEL // 2, SHARD_IN

    def body(half_ref, x_ref, dh_ref, o_ref):
        k = pl.program_id(1)
        part = _dot_tn(x_ref[...].astype(BF16), dh_ref[...])

        @pl.when(k == 0)
        def _():
            o_ref[...] = part

        @pl.when(k > 0)
        def _():
            o_ref[...] += part

    (g,), exchanged = _call(
        body, name=name, grid=(N_SHARDS, seq // ts),
        in_specs=[pl.BlockSpec((ts, td), lambda e, k, half_ref: (k, half_ref[0])),
                  pl.BlockSpec((ts, te), lambda e, k, half_ref: (k, e))],
        out_specs=[pl.BlockSpec((None, td, te), lambda e, k, half_ref: (e, 0, 0))],
        out_shape=[jax.ShapeDtypeStruct((N_SHARDS, td, te), F32)],
        scratch_shapes=[], semantics=("parallel", "arbitrary"), vmem_mib=48, args=(x, dh), comm=comm,
        prefetch=(half,))
    return g, exchanged


def _grad_w_out(y, dzb):
    seq = y.shape[0]
    ts, te = 512, 1024
    nk = seq // ts

    def body(y_ref, dz_ref, o_ref, acc_ref):
        k = pl.program_id(1)

        @pl.when(k == 0)
        def _():
            acc_ref[...] = jnp.zeros_like(acc_ref)

        acc_ref[...] += _dot_tn(y_ref[...], dz_ref[...])

        @pl.when(k == nk - 1)
        def _():
            o_ref[...] = acc_ref[...]

    return pl.pallas_call(
        body, name="grad_w_out", grid=(D_MODEL // te, nk),
        in_specs=[pl.BlockSpec((ts, te), lambda e, k: (k, e)), pl.BlockSpec((ts, D_MODEL), lambda e, k: (k, 0))],
        out_specs=pl.BlockSpec((te, D_MODEL), lambda e, k: (e, 0)),
        out_shape=jax.ShapeDtypeStruct((D_MODEL, D_MODEL), F32),
        scratch_shapes=[pltpu.VMEM((te, D_MODEL), F32)],
        compiler_params=_params(("parallel", "arbitrary"), 48),
    )(y, dzb)


def _grad_x(dh, w_in_g, dz, first=0, tiles=None, prev=None, comm=None):
    seq = dh.shape[0]
    tm, tk = 512, SHARD_IN
    tiles = seq // tm if tiles is None else tiles

    def body(*refs):
        dh_ref, w_ref, dz_ref, o_ref = refs[-4:]
        k = pl.program_id(1)
        part = _dot_nt(dh_ref[...], w_ref[...])

        @pl.when(k == 0)
        def _():
            o_ref[...] = DEEPNORM_ALPHA * dz_ref[...] + part

        @pl.when(k > 0)
        def _():
            o_ref[...] += part

    carried = [] if prev is None else [prev]
    (g_x,), exchanged = _call(
        body, name="grad_x_%d" % first, grid=(tiles, N_SHARDS),
        in_specs=[ANY] * len(carried) + [
            pl.BlockSpec((tm, tk), lambda i, k: (i + first, k)),
            pl.BlockSpec((None, D_MODEL, tk), lambda i, k: (k, 0, 0)),
            pl.BlockSpec((tm, D_MODEL), lambda i, k: (i + first, 0))],
        out_specs=[pl.BlockSpec((tm, D_MODEL), lambda i, k: (i + first, 0))],
        out_shape=[jax.ShapeDtypeStruct((seq, D_MODEL), F32)],
        scratch_shapes=[], semantics=("parallel", "arbitrary"), vmem_mib=48, args=(*carried, dh, w_in_g, dz),
        aliases={0: 0} if carried else None, comm=comm)
    return g_x, exchanged


def _pool_weight(w_pool_sh):
    n_groups = len(POOL_WINDOWS)
    shard_c = POOL_GROUP_DIM // N_SHARDS
    return (w_pool_sh.reshape(N_SHARDS, n_groups, shard_c, POOL_GROUP_DIM).transpose(1, 0, 2, 3)
            .reshape(n_groups, POOL_GROUP_DIM, POOL_GROUP_DIM))


def _pool_grad_pieces(g_w_pool):
    n_groups = len(POOL_WINDOWS)
    half_c = POOL_GROUP_DIM // N_SHARDS // 2
    return (g_w_pool.reshape(n_groups, N_SHARDS, 2, half_c, POOL_GROUP_DIM).transpose(1, 2, 0, 3, 4)
            .reshape(N_SHARDS, 2, n_groups * half_c, POOL_GROUP_DIM))


def _step(x, target, w_in_g, w_rest, pool_scale, gain, bias, place=None):
    seq = x.shape[0]
    tabs = _rope_tables(seq)
    qkv, gathered = _in_proj_qkv(x, w_in_g, tabs, comm=_allgather_weights(w_rest) if place else None)
    w_out_g, w_pool_sh = gathered if place else w_rest
    w_pool_g = _pool_weight(w_pool_sh)
    hug = _in_proj_pool_gate(x, w_in_g)
    o_list, st_list = [], []
    for p, dil in enumerate(DILATIONS):
        o, st = _attn_fwd(qkv[p], "attn_fwd_d%d" % dil)
        o_list.append(o)
        st_list.append(st)
    y, mixpre, lse_all, pooled = _mix_gate(o_list, st_list, hug, w_pool_g, pool_scale)
    dz, dzb, gain_part, bias_part, loss_part = _out_proj_loss(y, w_out_g, x, target, gain, bias)
    dh, dpo, do_list, stat_list = _dy_gate_bwd(dzb, w_out_g, hug, mixpre, pool_scale, lse_all)
    g_w_out = _grad_w_out(y, dzb)
    dh, g_w_pool, scale_part = _pool_bwd(dh, dpo, mixpre, pooled, w_pool_g, pool_scale)
    small = jnp.concatenate([scale_part, gain_part, bias_part, loss_part], axis=1)
    early = [g_w_out.reshape(N_SHARDS, 2, D_MODEL // (2 * N_SHARDS), D_MODEL), _pool_grad_pieces(g_w_pool)]

    bwd = lambda p, comm: _attn_bwd(qkv[p], do_list[p], stat_list[p], "attn_bwd_d%d" % DILATIONS[p], comm)
    if place is None:
        parts = [bwd(p, None)[0] for p in range(3)]
    else:
        core, chip_core = place
        part_a, recv = bwd(0, _exchange_halves(early))
        sums = [_add_own_half(g, r, core, "add_own_half_%d" % a) for a, (g, r) in enumerate(zip(early, recv))]
        part_b, recv = bwd(1, _scatter_to_chips([s[1] for s in sums]))
        bufs = [_add_chips(s[0], r, chip_core, "add_chips_%d" % a) for a, (s, r) in enumerate(zip(sums, recv))]
        part_c, early = bwd(2, _share_with_sibling(bufs))
        parts = [part_a, part_b, part_c]
    dh = _sum_patterns(dh, [t[0] for t in parts], tabs, True, 0, "sum_dq")
    dh = _sum_patterns(dh, [t[1] for t in parts], tabs, True, 1, "sum_dk")
    dh = _sum_patterns(dh, [t[2] for t in parts], tabs, False, 2, "sum_dv")
    if place is None:
        halves = [_grad_w_in(x, dh, jnp.full((1,), h, jnp.int32), "grad_w_in_%d" % h)[0] for h in range(2)]
        g_w_in = jnp.stack(halves, axis=1)
        g_x, _ = _grad_x(dh, w_in_g, dz)
    else:
        give, _ = _grad_w_in(x, dh, 1 - core, "grad_w_in_give")
        keep, recv = _grad_w_in(x, dh, core, "grad_w_in_keep", _send_to_sibling([give]))
        total, total_b = _add_pair(keep, recv[0], "add_own_half_w_in")
        rows = total.shape[1]
        cut = rows // 2
        g_x, recv = _grad_x(dh, w_in_g, dz, 0, 3, None, _scatter_to_chips([total_b], (0, cut)))
        g_x, recv = _grad_x(dh, w_in_g, dz, 3, 3, g_x, _scatter_to_chips([total_b], (cut, rows - cut), recv))
        buf = _add_chips(total, recv[0], chip_core, "add_chips_w_in")
        g_x, _ = _grad_x(dh, w_in_g, dz, 6, seq // 512 - 6, g_x)
        g_w_in = _run_exchange(_share_with_sibling([buf]), "share_w_in")[0]
    return g_x, g_w_in, early[0], early[1], small


def _exchange_halves(grads):
    n = len(grads)

    def copies(src, dst, sems):
        x, y, c, _ = _mesh_place()
        return [_remote(src[a].at[j, 1 - c], dst[a].at[j], sems[0].at[a, j], sems[1].at[a, j], (x, y, 1 - c))
                for a in range(n) for j in range(N_SHARDS)]

    def start(src, dst, sems):
        for cp in copies(src, dst, sems):
            cp.start()

    def finish(src, dst, sems):
        for cp in copies(src, dst, sems):
            cp.wait()

    return _Exchange(grads, [jax.ShapeDtypeStruct((N_SHARDS,) + g.shape[2:], g.dtype) for g in grads], {},
                     [pltpu.SemaphoreType.DMA((n, N_SHARDS))] * 2, start, finish)


def _add_own_half(grad, recv, core, name):
    _, _, r, c = grad.shape
    tr = min(r, 256)

    def body(core_ref, g_ref, r_ref, o_ref, ob_ref):
        tot = g_ref[...] + r_ref[...]
        o_ref[...] = tot
        ob_ref[...] = tot.astype(BF16)

    out = pl.BlockSpec((None, tr, c), lambda j, i, core_ref: (j, i, 0))
    return pl.pallas_call(
        body, name=name,
        grid_spec=pltpu.PrefetchScalarGridSpec(
            num_scalar_prefetch=1, grid=(N_SHARDS, r // tr),
            in_specs=[pl.BlockSpec((None, None, tr, c), lambda j, i, core_ref: (j, core_ref[0], i, 0)),
                      pl.BlockSpec((None, tr, c), lambda j, i, core_ref: (j, i, 0))],
            out_specs=[out, out]),
        out_shape=[jax.ShapeDtypeStruct((N_SHARDS, r, c), F32), jax.ShapeDtypeStruct((N_SHARDS, r, c), BF16)],
        compiler_params=_params(("parallel", "parallel"), 32),
    )(core, grad, recv)


def _send_to_sibling(arrays):
    n = len(arrays)

    def copies(src, dst, sems):
        x, y, c, _ = _mesh_place()
        return [_remote(src[a], dst[a], sems[0].at[a], sems[1].at[a], (x, y, 1 - c)) for a in range(n)]

    def start(src, dst, sems):
        for cp in copies(src, dst, sems):
            cp.start()

    def finish(src, dst, sems):
        for cp in copies(src, dst, sems):
            cp.wait()

    return _Exchange(arrays, [jax.ShapeDtypeStruct(t.shape, t.dtype) for t in arrays], {},
                     [pltpu.SemaphoreType.DMA((n,))] * 2, start, finish)


def _add_pair(a, b, name):
    _, r, c = a.shape
    tr = min(r, 256)

    def body(a_ref, b_ref, o_ref, ob_ref):
        tot = a_ref[...] + b_ref[...]
        o_ref[...] = tot
        ob_ref[...] = tot.astype(BF16)

    spec = pl.BlockSpec((None, tr, c), lambda j, i: (j, i, 0))
    return pl.pallas_call(
        body, name=name, grid=(N_SHARDS, r // tr), in_specs=[spec, spec], out_specs=[spec, spec],
        out_shape=[jax.ShapeDtypeStruct(a.shape, F32), jax.ShapeDtypeStruct(a.shape, BF16)],
        compiler_params=_params(("parallel", "parallel"), 32),
    )(a, b)


def _scatter_to_chips(sums, rows=None, into=None):
    n = len(sums)

    def copies(src, dst, sems):
        x, y, c, chips = _mesh_place()
        part = (lambda ref: ref) if rows is None else (lambda ref: ref.at[pl.ds(rows[0], rows[1])])
        return [_remote(part(src[a].at[2 * cx + cy]), part(dst[a].at[k]), sems[0].at[a, k], sems[1].at[a, k],
                        (cx, cy, c))
                for a in range(n) for k, (cx, cy) in enumerate(chips)]

    def start(src, dst, sems):
        for cp in copies(src, dst, sems):
            cp.start()

    def finish(src, dst, sems):
        for cp in copies(src, dst, sems):
            cp.wait()

    return _Exchange(sums + (into or []), [jax.ShapeDtypeStruct((3,) + s.shape[1:], s.dtype) for s in sums],
                     {n + a: a for a in range(n)} if into else {},
                     [pltpu.SemaphoreType.DMA((n, 3))] * 2, start, finish)


def _add_chips(sums, recv, chip_core, name):
    _, r, c = sums.shape
    tr = min(r, 256)

    def body(cc_ref, s_ref, r_ref, o_ref):
        o_ref[...] = ((s_ref[...] + r_ref[0].astype(F32)) + r_ref[1].astype(F32)) + r_ref[2].astype(F32)

    return pl.pallas_call(
        body, name=name,
        grid_spec=pltpu.PrefetchScalarGridSpec(
            num_scalar_prefetch=1, grid=(r // tr,),
            in_specs=[pl.BlockSpec((None, tr, c), lambda i, cc_ref: (cc_ref[0], i, 0)),
                      pl.BlockSpec((3, tr, c), lambda i, cc_ref: (0, i, 0))],
            out_specs=pl.BlockSpec((None, tr, c), lambda i, cc_ref: (cc_ref[1], i, 0))),
        out_shape=jax.ShapeDtypeStruct((2, r, c), F32),
        compiler_params=_params(("parallel",), 32),
    )(chip_core, sums, recv)


def _share_with_sibling(bufs):
    n = len(bufs)

    def copies(dst, sems, half):
        x, y, c, _ = _mesh_place()
        h = c if half == "mine" else 1 - c
        return [_remote(dst[a].at[h], dst[a].at[h], sems[0].at[a], sems[1].at[a], (x, y, 1 - c)) for a in range(n)]

    def start(ins, dst, sems):
        for cp in copies(dst, sems, "mine"):
            cp.start()

    def finish(ins, dst, sems):
        for cp in copies(dst, sems, "theirs"):
            cp.wait_recv()
        for cp in copies(dst, sems, "mine"):
            cp.wait_send()

    return _Exchange(bufs, [jax.ShapeDtypeStruct(b.shape, b.dtype) for b in bufs], {a: a for a in range(n)},
                     [pltpu.SemaphoreType.DMA((n,))] * 2, start, finish)


def _adam_math(w, g, m, v):
    m = ADAM_B1 * m + (1.0 - ADAM_B1) * g
    v = ADAM_B2 * v + (1.0 - ADAM_B2) * (g * g)
    m_hat = m / (1.0 - ADAM_B1 ** ADAM_STEP)
    v_hat = v / (1.0 - ADAM_B2 ** ADAM_STEP)
    delta = -ADAM_LR * (m_hat / (jnp.sqrt(v_hat) + ADAM_EPS) + ADAM_WD * w)
    return delta, m, v


def _small_allreduce_adamw(small, w_vec, m_vec, v_vec):
    width = small.shape[1]
    n_par = w_vec.shape[1]

    def body(s_ref, w_ref, m_ref, v_ref, loss_ref, g_ref, d_ref, nm_ref, nv_ref, gather_ref, send_sems, recv_sems):
        x, y, c = lax.axis_index("x"), lax.axis_index("y"), lax.axis_index("c")
        me = 4 * x + 2 * y + c
        gather_ref[me] = s_ref[...]
        copies = []
        for r in range(1, 8):
            bx, by, bc = (r >> 2) & 1, (r >> 1) & 1, r & 1
            peer = (x ^ bx, y ^ by, c ^ bc)
            cp = pltpu.make_async_remote_copy(
                src_ref=s_ref, dst_ref=gather_ref.at[me], send_sem=send_sems.at[r - 1], recv_sem=recv_sems.at[r - 1],
                device_id=peer, device_id_type=MESH)
            cp.start()
            copies.append(cp)
        for r in range(1, 8):
            bx, by, bc = (r >> 2) & 1, (r >> 1) & 1, r & 1
            theirs = gather_ref.at[4 * (x ^ bx) + 2 * (y ^ by) + (c ^ bc)]
            pltpu.make_async_remote_copy(
                src_ref=theirs, dst_ref=theirs, send_sem=send_sems.at[r - 1], recv_sem=recv_sems.at[r - 1],
                device_id=(x ^ bx, y ^ by, c ^ bc), device_id_type=MESH).wait_recv()
        for cp in copies:
            cp.wait_send()
        tot = gather_ref[0]
        for d in range(1, 8):
            tot = tot + gather_ref[d]
        tot = jnp.sum(tot, axis=0, keepdims=True)
        sq = jnp.sum(tot[:, n_par:], axis=1, keepdims=True)
        loss_ref[...] = jnp.broadcast_to(sq * (0.5 / D_MODEL), loss_ref.shape)
        g = tot[:, :n_par]
        g_ref[...] = g
        d_ref[...], nm_ref[...], nv_ref[...] = _adam_math(w_ref[...], g, m_ref[...], v_ref[...])

    vm = pl.BlockSpec(memory_space=pltpu.VMEM)
    vec = jax.ShapeDtypeStruct((1, n_par), F32)
    return pl.pallas_call(
        body, name="small_allreduce_adamw",
        in_specs=[vm] * 4, out_specs=[vm] * 5,
        out_shape=[jax.ShapeDtypeStruct((1, 128), F32), vec, vec, vec, vec],
        scratch_shapes=[pltpu.VMEM((8, 8, width), F32), pltpu.SemaphoreType.DMA((7,)), pltpu.SemaphoreType.DMA((7,))],
    )(small, w_vec, m_vec, v_vec)


def _adamw(w, g, m, v, name):
    r, c = w.shape
    tr = min(r, 256)

    def body(w_ref, g_ref, m_ref, v_ref, d_ref, nm_ref, nv_ref):
        d_ref[...], nm_ref[...], nv_ref[...] = _adam_math(w_ref[...], g_ref[...], m_ref[...], v_ref[...])

    spec = pl.BlockSpec((tr, c), lambda i: (i, 0))
    shape = jax.ShapeDtypeStruct((r, c), F32)
    return pl.pallas_call(
        body, name=name, grid=(r // tr,),
        in_specs=[spec] * 4, out_specs=[spec] * 3, out_shape=[shape] * 3,
        compiler_params=_params(("parallel",), 32),
    )(w, g, m, v)


def kernel(x, w_in, w_pool, pool_scale, w_out, ln_gain, ln_bias, loss_target, m_w_in, m_w_pool, m_pool_scale, m_w_out, m_ln_gain, m_ln_bias, v_w_in, v_w_pool, v_pool_scale, v_w_out, v_ln_gain, v_ln_bias):
    xi, yi, ci = lax.axis_index("x"), lax.axis_index("y"), lax.axis_index("c")
    chip = (2 * xi + yi).astype(jnp.int32).reshape(1)
    core = ci.astype(jnp.int32).reshape(1)
    n_groups = len(POOL_WINDOWS)
    shard_c = w_pool.shape[2]

    w_in_b = _cast_bf16(w_in[0], chip, "cast_w_in", 256)
    w_out_b = _cast_bf16(w_out[0], chip, "cast_w_out", 256)
    w_pool_b = _cast_bf16(w_pool[0].reshape(n_groups * shard_c, POOL_GROUP_DIM), chip, "cast_w_pool", 256)
    w_in_g = _run_exchange(_allgather_weights([w_in_b]), "allgather_w_in")[0]

    chip_core = jnp.concatenate([chip, core])
    g_x, full_in, full_out, full_pool, small = _step(
        x[0], loss_target[0], w_in_g, [w_out_b, w_pool_b], pool_scale, ln_gain, ln_bias, (core, chip_core))
    half_c = shard_c // 2
    grad_w_in = full_in.reshape(D_MODEL, SHARD_IN)
    grad_w_out = full_out.reshape(D_MODEL // N_SHARDS, D_MODEL)
    grad_w_pool = (full_pool.reshape(2, n_groups, half_c, POOL_GROUP_DIM).transpose(1, 0, 2, 3)
                   .reshape(n_groups * shard_c, POOL_GROUP_DIM))

    d_in, nm_in, nv_in = _adamw(w_in[0], grad_w_in, m_w_in[0], v_w_in[0], "adamw_w_in")
    d_out, nm_out, nv_out = _adamw(w_out[0], grad_w_out, m_w_out[0], v_w_out[0], "adamw_w_out")
    flat = lambda t: t[0].reshape(n_groups * shard_c, POOL_GROUP_DIM)
    d_pool, nm_pool, nv_pool = _adamw(flat(w_pool), grad_w_pool, flat(m_w_pool), flat(v_w_pool), "adamw_w_pool")

    cat = lambda a, b, c: jnp.concatenate([a, b, c], axis=1)
    loss_v, g_vec, d_vec, nm_vec, nv_vec = _small_allreduce_adamw(
        small, cat(pool_scale, ln_gain, ln_bias), cat(m_pool_scale, m_ln_gain, m_ln_bias),
        cat(v_pool_scale, v_ln_gain, v_ln_bias))

    def split(vec):
        return vec[:, :D_POOL], vec[:, D_POOL:D_POOL + D_MODEL], vec[:, D_POOL + D_MODEL:]

    g_scale, g_gain, g_bias = split(g_vec)
    d_scale, d_gain, d_bias = split(d_vec)
    nm_scale, nm_gain, nm_bias = split(nm_vec)
    nv_scale, nv_gain, nv_bias = split(nv_vec)
    pool_shape = w_pool.shape
    return (loss_v[0, 0], g_x[None],
            grad_w_in[None], grad_w_pool.reshape(pool_shape), g_scale, grad_w_out[None], g_gain, g_bias,
            d_in[None], d_pool.reshape(pool_shape), d_scale, d_out[None], d_gain, d_bias,
            nm_in[None], nm_pool.reshape(pool_shape), nm_scale, nm_out[None], nm_gain, nm_bias,
            nv_in[None], nv_pool.reshape(pool_shape), nv_scale, nv_out[None], nv_gain, nv_bias)
```

```python
import functools

import jax
import jax.numpy as jnp
from jax import lax
from jax.experimental import pallas as pl
from jax.experimental.pallas import tpu as pltpu

F32 = jnp.float32
BF16 = jnp.bfloat16
MESH = pl.DeviceIdType.MESH
ANY = pl.BlockSpec(memory_space=pl.ANY)

D_MODEL = 2048
D_ATTN = 1024
D_POOL = 1024
HEAD_DIM = 128
N_HEADS = 8
ROPE_DIM = 32
ROPE_THETA = 500000.0
DILATIONS = (1, 4, 16)
KEY_BLOCK = 128
CHUNK = 2 * KEY_BLOCK
STAT_LANES = 128
POOL_WINDOWS = (2, 4, 8, 16)
POOL_GROUP_DIM = 256
POOL_HALO = 16
D_QKV = 3 * D_ATTN
D_UG = D_POOL + D_MODEL
D_IN = D_QKV + D_UG
N_SHARDS = 4
SHARD_IN = D_IN // N_SHARDS
LN_EPS = 1e-5
DEEPNORM_ALPHA = 2.0 ** 0.25
ADAM_LR = 0.001
ADAM_B1 = 0.9
ADAM_B2 = 0.999
ADAM_EPS = 1e-08
ADAM_WD = 0.01
ADAM_STEP = 10
NEG = -1e30
MIB = 1024 * 1024


def _params(sem, vmem_mib):
    return pltpu.CompilerParams(dimension_semantics=sem, vmem_limit_bytes=vmem_mib * MIB)


class _Exchange:
    def __init__(self, operands, out_shape, aliases, sems, start, finish):
        self.operands, self.out_shape, self.aliases, self.sems = list(operands), list(out_shape), dict(aliases), list(sems)
        self.start, self.finish = start, finish


def _run_exchange(comm, name):
    n_in, n_out = len(comm.operands), len(comm.out_shape)

    def body(*refs):
        ins, outs, sems = refs[:n_in], refs[n_in:n_in + n_out], refs[n_in + n_out:]
        comm.start(ins, outs, sems)
        comm.finish(ins, outs, sems)

    return pl.pallas_call(
        body, name=name, in_specs=[ANY] * n_in, out_specs=[ANY] * n_out, out_shape=comm.out_shape,
        input_output_aliases=comm.aliases, scratch_shapes=comm.sems,
    )(*comm.operands)


def _call(body, *, name, grid, in_specs, out_specs, out_shape, scratch_shapes, semantics, vmem_mib, args,
          aliases=None, comm=None, prefetch=()):
    aliases = dict(aliases or {})
    n_pre, n_in, n_out, n_scr = len(prefetch), len(in_specs), len(out_specs), len(scratch_shapes)
    c_in, c_out = (len(comm.operands), len(comm.out_shape)) if comm else (0, 0)
    c_shapes, c_sems, c_operands = (comm.out_shape, comm.sems, comm.operands) if comm else ([], [], [])

    def hosted(*refs):
        pre, refs = refs[:n_pre], refs[n_pre:]
        a = n_in
        b = a + c_in
        c = b + n_out
        d = c + c_out
        e = d + n_scr
        if comm is None:
            body(*pre, *refs)
            return
        ids = [pl.program_id(k) for k in range(len(grid))]
        first = functools.reduce(jnp.logical_and, [i == 0 for i in ids])
        last = functools.reduce(jnp.logical_and, [i == g - 1 for i, g in zip(ids, grid)])

        @pl.when(first)
        def _():
            comm.start(refs[a:b], refs[c:d], refs[e:])

        body(*pre, *refs[:a], *refs[b:c], *refs[d:e])

        @pl.when(last)
        def _():
            comm.finish(refs[a:b], refs[c:d], refs[e:])

    if comm:
        semantics = ("arbitrary",) * len(grid)
        for i, o in comm.aliases.items():
            aliases[n_pre + n_in + i] = n_out + o
    outs = pl.pallas_call(
        hosted, name=name,
        grid_spec=pltpu.PrefetchScalarGridSpec(
            num_scalar_prefetch=n_pre, grid=grid, in_specs=list(in_specs) + [ANY] * c_in,
            out_specs=list(out_specs) + [ANY] * c_out, scratch_shapes=list(scratch_shapes) + c_sems),
        out_shape=list(out_shape) + c_shapes, input_output_aliases=aliases,
        compiler_params=_params(semantics, vmem_mib),
    )(*prefetch, *args, *c_operands)
    return list(outs[:n_out]), list(outs[n_out:])


def _dot_nn(a, b):
    return jnp.dot(a, b, preferred_element_type=F32)


def _dot_nt(a, b):
    return lax.dot_general(a, b, (((1,), (1,)), ((), ())), preferred_element_type=F32)


def _dot_tn(a, b):
    return lax.dot_general(a, b, (((0,), (0,)), ((), ())), preferred_element_type=F32)


def _fold_rows(a):
    r, c = a.shape
    return jnp.sum(a.reshape(r // 8, 8, c), axis=0)


def _cast_bf16(a, chip, name, rows):
    r, c = a.shape

    def body(chip_ref, a_ref, o_ref):
        o_ref[...] = a_ref[...].astype(BF16)

    return pl.pallas_call(
        body, name=name,
        grid_spec=pltpu.PrefetchScalarGridSpec(
            num_scalar_prefetch=1, grid=(r // rows,),
            in_specs=[pl.BlockSpec((rows, c), lambda i, chip_ref: (i, 0))],
            out_specs=pl.BlockSpec((None, rows, c), lambda i, chip_ref: (chip_ref[0], i, 0))),
        out_shape=jax.ShapeDtypeStruct((N_SHARDS, r, c), BF16),
        compiler_params=_params(("parallel",), 32),
    )(chip, a)


def _mesh_place():
    x, y, c = lax.axis_index("x"), lax.axis_index("y"), lax.axis_index("c")
    return x, y, c, [(1 - x, y), (x, 1 - y), (1 - x, 1 - y)]


def _remote(src, dst, send_sem, recv_sem, to):
    return pltpu.make_async_remote_copy(src_ref=src, dst_ref=dst, send_sem=send_sem, recv_sem=recv_sem,
                                        device_id=to, device_id_type=MESH)


def _allgather_weights(bufs):
    n = len(bufs)

    def half(a, core):
        rows = bufs[a].shape[1] // 2
        return pl.ds(core * rows, rows)

    def ici_copies(dst, sems):
        x, y, c, chips = _mesh_place()
        own = lambda a: dst[a].at[2 * x + y, half(a, c)]
        return [_remote(own(a), own(a), sems[0].at[a, k], sems[1].at[a, k], (cx, cy, c))
                for a in range(n) for k, (cx, cy) in enumerate(chips)]

    def start(ins, dst, sems):
        for cp in ici_copies(dst, sems):
            cp.start()

    def finish(ins, dst, sems):
        x, y, c, chips = _mesh_place()
        sibling = (x, y, 1 - c)
        passed_on = []
        for k, (cx, cy) in enumerate(chips):
            for a in range(n):
                landed = dst[a].at[2 * cx + cy, half(a, c)]
                _remote(landed, landed, sems[0].at[a, k], sems[1].at[a, k], (cx, cy, c)).wait_recv()
                cp = _remote(landed, landed, sems[2].at[a, k], sems[3].at[a, k], sibling)
                cp.start()
                passed_on.append(cp)
        for k, (cx, cy) in enumerate(chips):
            for a in range(n):
                passed = dst[a].at[2 * cx + cy, half(a, 1 - c)]
                _remote(passed, passed, sems[2].at[a, k], sems[3].at[a, k], sibling).wait_recv()
        for cp in ici_copies(dst, sems) + passed_on:
            cp.wait_send()

    return _Exchange(bufs, [jax.ShapeDtypeStruct(b.shape, b.dtype) for b in bufs], {a: a for a in range(n)},
                     [pltpu.SemaphoreType.DMA((n, 3))] * 4, start, finish)


def _rope_tables(seq):
    half = ROPE_DIM // 2
    inv_freq = ROPE_THETA ** (-(2.0 * jnp.arange(half, dtype=F32)) / ROPE_DIM)
    ang = jnp.arange(seq, dtype=jnp.int32).astype(F32)[:, None] * inv_freq[None, :]
    cos, sin = jnp.cos(ang), jnp.sin(ang)
    pad = jnp.zeros((seq, HEAD_DIM - ROPE_DIM), F32)
    zeros = jnp.zeros((seq, half), F32)
    c_tab = jnp.concatenate([cos, cos, pad + 1.0], axis=1)
    up_tab = jnp.concatenate([-sin, zeros, pad], axis=1)
    down_tab = jnp.concatenate([zeros, sin, pad], axis=1)
    return c_tab, up_tab, down_tab


def _rotate_heads(t, c_tab, up_tab, down_tab):
    outs = []
    for h in range(t.shape[1] // HEAD_DIM):
        th = t[:, h * HEAD_DIM:(h + 1) * HEAD_DIM]
        up = pltpu.roll(th, HEAD_DIM - ROPE_DIM // 2, axis=1)
        down = pltpu.roll(th, ROPE_DIM // 2, axis=1)
        outs.append(th * c_tab + up * up_tab + down * down_tab)
    return outs[0] if len(outs) == 1 else jnp.concatenate(outs, axis=1)


def _to_pattern(slabs_ref, dst_ref, dil, dtype):
    n_slabs, rows, _ = slabs_ref.shape
    for s in range(n_slabs):
        for r in range(dil):
            dst_ref[r, :, s * 128:(s + 1) * 128] = slabs_ref[s, pl.ds(r, rows // dil, dil), :].astype(dtype)


def _from_pattern(src_ref, slabs_ref, dil):
    n_slabs, rows, _ = slabs_ref.shape
    for s in range(n_slabs):
        for r in range(dil):
            slabs_ref[s, pl.ds(r, rows // dil, dil), :] = src_ref[r, :, s * 128:(s + 1) * 128].astype(F32)


def _store_slabs(slabs_ref, value):
    for s in range(slabs_ref.shape[0]):
        slabs_ref[s] = value[:, s * 128:(s + 1) * 128]


def _in_proj_qkv(x, w_in_g, tabs, comm=None):
    seq = x.shape[0]
    tm, tn = 512, SHARD_IN
    heads = tn // HEAD_DIM
    k_heads_in_second = 2 * D_ATTN // HEAD_DIM - heads
    d4, d16 = DILATIONS[1], DILATIONS[2]

    def body(x_ref, w_ref, c_ref, up_ref, down_ref, o1_ref, o4_ref, o16_ref, res_ref):
        shard = pl.program_id(0)
        xb = x_ref[...].astype(BF16)
        group = 4 * HEAD_DIM
        accs = [_dot_nn(xb, w_ref[:, g * group:(g + 1) * group]) for g in range(tn // group)]

        plain = shard == 1
        c_plain = jnp.where(plain, 1.0, c_ref[...])
        up_plain = jnp.where(plain, 0.0, up_ref[...])
        down_plain = jnp.where(plain, 0.0, down_ref[...])
        for h in range(heads):
            lanes = (h * HEAD_DIM) % group
            th = accs[h * HEAD_DIM // group][:, lanes:lanes + HEAD_DIM]
            if h < k_heads_in_second:
                th = _rotate_heads(th, c_ref[...], up_ref[...], down_ref[...])
            else:
                th = _rotate_heads(th, c_plain, up_plain, down_plain)
            res_ref[h] = th
            o1_ref[:, h * HEAD_DIM:(h + 1) * HEAD_DIM] = th.astype(BF16)
        _to_pattern(res_ref, o4_ref, d4, BF16)
        _to_pattern(res_ref, o16_ref, d16, BF16)

    tab_spec = pl.BlockSpec((tm, HEAD_DIM), lambda s, i: (i, 0))
    (o1, o4, o16), exchanged = _call(
        body, name="in_proj_qkv", grid=(D_QKV // tn, seq // tm),
        in_specs=[pl.BlockSpec((tm, D_MODEL), lambda s, i: (i, 0)),
                  pl.BlockSpec((None, D_MODEL, tn), lambda s, i: (s, 0, 0)),
                  tab_spec, tab_spec, tab_spec],
        out_specs=[pl.BlockSpec((tm, tn), lambda s, i: (i, s)),
                   pl.BlockSpec((d4, tm // d4, tn), lambda s, i: (0, i, s)),
                   pl.BlockSpec((d16, tm // d16, tn), lambda s, i: (0, i, s))],
        out_shape=[jax.ShapeDtypeStruct((seq, D_QKV), BF16),
                   jax.ShapeDtypeStruct((d4, seq // d4, D_QKV), BF16),
                   jax.ShapeDtypeStruct((d16, seq // d16, D_QKV), BF16)],
        scratch_shapes=[pltpu.VMEM((heads, tm, HEAD_DIM), F32)],
        semantics=("parallel", "parallel"), vmem_mib=52, args=(x, w_in_g, *tabs), comm=comm)
    return [o1[None], o4, o16], exchanged


def _in_proj_pool_gate(x, w_in_g):
    seq = x.shape[0]
    tm, tn = 512, SHARD_IN
    first_shard = D_QKV // tn

    def body(x_ref, w_ref, o_ref):
        o_ref[...] = _dot_nn(x_ref[...].astype(BF16), w_ref[...])

    return pl.pallas_call(
        body, name="in_proj_pool_gate", grid=(D_UG // tn, seq // tm),
        in_specs=[pl.BlockSpec((tm, D_MODEL), lambda s, i: (i, 0)),
                  pl.BlockSpec((None, D_MODEL, tn), lambda s, i: (s + first_shard, 0, 0))],
        out_specs=pl.BlockSpec((tm, tn), lambda s, i: (i, s)),
        out_shape=jax.ShapeDtypeStruct((seq, D_UG), F32),
        compiler_params=_params(("parallel", "parallel"), 48),
    )(x, w_in_g)


def _band_masks():
    row = lax.broadcasted_iota(jnp.int32, (KEY_BLOCK, KEY_BLOCK), 0)
    col = lax.broadcasted_iota(jnp.int32, (KEY_BLOCK, KEY_BLOCK), 1)
    return col <= row, col >= row


def _attn_fwd(qkv, name):
    dil, n, _ = qkv.shape
    scale = HEAD_DIM ** -0.5
    lo, hi = slice(0, KEY_BLOCK), slice(KEY_BLOCK, CHUNK)

    def body(q_ref, k_ref, v_ref, kb_ref, vb_ref, o_ref, st_ref):
        i = pl.program_id(1)
        cur_mask, prev_mask = _band_masks()
        before_mask = jnp.logical_and(prev_mask, i > 0)
        lane = lax.broadcasted_iota(jnp.int32, (KEY_BLOCK, STAT_LANES), 1)
        tasks = [(rows, h) for rows in (lo, hi) for h in range(N_HEADS)]
        head = lambda h: slice(h * HEAD_DIM, (h + 1) * HEAD_DIM)

        def prev_of(rows, h):
            if rows is lo:
                return kb_ref[:, head(h)], vb_ref[:, head(h)], before_mask
            return k_ref[lo, head(h)], v_ref[lo, head(h)], prev_mask

        scores = []
        for rows, h in tasks:
            q = q_ref[rows, head(h)]
            scores.append((_dot_nt(q, prev_of(rows, h)[0]), _dot_nt(q, k_ref[rows, head(h)])))
        probs = []
        for (rows, h), (qk_prev, qk_cur) in zip(tasks, scores):
            s_prev = jnp.where(prev_of(rows, h)[2], qk_prev * scale, NEG)
            s_cur = jnp.where(cur_mask, qk_cur * scale, NEG)
            m = jnp.max(jnp.maximum(s_prev, s_cur), axis=-1, keepdims=True)
            p_prev = jnp.exp(s_prev - m)
            p_cur = jnp.exp(s_cur - m)
            den = jnp.sum(p_prev + p_cur, axis=-1, keepdims=True)
            probs.append((p_prev.astype(BF16), p_cur.astype(BF16), den, m + jnp.log(den)))
        stats = [jnp.zeros((KEY_BLOCK, STAT_LANES), F32), jnp.zeros((KEY_BLOCK, STAT_LANES), F32)]
        for (rows, h), (p_prev, p_cur, den, lse) in zip(tasks, probs):
            o = _dot_nn(p_cur, v_ref[rows, head(h)]) + _dot_nn(p_prev, prev_of(rows, h)[1])
            o_ref[rows, head(h)] = o / den
            b = 0 if rows is lo else 1
            stats[b] = jnp.where(lane == h, lse, stats[b])
        st_ref[lo, :] = stats[0]
        st_ref[hi, :] = stats[1]

    main = lambda cb: pl.BlockSpec((None, CHUNK, D_ATTN), lambda r, i: (r, i, cb))
    before = lambda cb: pl.BlockSpec((None, KEY_BLOCK, D_ATTN), lambda r, i: (r, jnp.maximum(2 * i - 1, 0), cb))
    return pl.pallas_call(
        body, name=name, grid=(dil, n // CHUNK),
        in_specs=[main(0), main(1), main(2), before(1), before(2)],
        out_specs=[main(0), pl.BlockSpec((None, CHUNK, STAT_LANES), lambda r, i: (r, i, 0))],
        out_shape=[jax.ShapeDtypeStruct((dil, n, D_ATTN), F32), jax.ShapeDtypeStruct((dil, n, STAT_LANES), F32)],
        compiler_params=_params(("parallel", "parallel"), 40),
    )(qkv, qkv, qkv, qkv, qkv)


def _attn_bwd(qkv, do, stats, name, comm=None):
    dil, n, _ = qkv.shape
    n_blocks = n // KEY_BLOCK
    last = n // CHUNK - 1
    scale = HEAD_DIM ** -0.5
    lo, hi = slice(0, KEY_BLOCK), slice(KEY_BLOCK, CHUNK)

    def body(q_ref, k_ref, v_ref, kb_ref, vb_ref, qa_ref, do_ref, doa_ref, st_ref, sta_ref, dq_ref, dk_ref, dv_ref):
        i = pl.program_id(1)
        cur_mask, prev_mask = _band_masks()
        before_mask = jnp.logical_and(prev_mask, i > 0)
        after_mask = jnp.logical_and(prev_mask, i < last)

        def operands(h):
            cols = slice(h * HEAD_DIM, (h + 1) * HEAD_DIM)
            lse_c, del_c = slice(h, h + 1), slice(N_HEADS + h, N_HEADS + h + 1)
            q = {"0": q_ref[lo, cols], "1": q_ref[hi, cols], "a": qa_ref[:, cols]}
            k = {"0": k_ref[lo, cols], "1": k_ref[hi, cols], "b": kb_ref[:, cols]}
            v = {"0": v_ref[lo, cols], "1": v_ref[hi, cols], "b": vb_ref[:, cols]}
            do = {"0": do_ref[lo, cols], "1": do_ref[hi, cols], "a": doa_ref[:, cols]}
            st = {"0": (st_ref[lo, lse_c], st_ref[lo, del_c]), "1": (st_ref[hi, lse_c], st_ref[hi, del_c]),
                  "a": (sta_ref[:, lse_c], sta_ref[:, del_c])}
            return cols, q, k, v, do, st

        pairs = [("0", "b", before_mask), ("0", "0", cur_mask), ("1", "0", prev_mask), ("1", "1", cur_mask),
                 ("a", "1", after_mask)]
        group = N_HEADS // 2
        for first_head in range(0, N_HEADS, group):
            heads = range(first_head, first_head + group)
            raw = {}
            for h in heads:
                _, q, k, v, do, _ = operands(h)
                for qi, ki, _ in pairs:
                    raw[h, qi, ki] = (_dot_nt(q[qi], k[ki]), _dot_nt(do[qi], v[ki]))
            grads = {}
            for h in heads:
                st = operands(h)[5]
                for qi, ki, mask in pairs:
                    qk, dp = raw[h, qi, ki]
                    lse, delta = st[qi]
                    p = jnp.exp(jnp.where(mask, qk * scale, NEG) - lse)
                    grads[h, qi, ki] = (p.astype(BF16), (p * (dp - delta) * scale).astype(BF16))
            for h in heads:
                cols, q, k, v, do, _ = operands(h)
                p = lambda qi, ki: grads[h, qi, ki][0]
                ds = lambda qi, ki: grads[h, qi, ki][1]
                def put(ref, rows, val, cols=cols):
                    ref[rows, cols] = val.astype(ref.dtype)

                put(dq_ref, lo, _dot_nn(ds("0", "b"), k["b"]) + _dot_nn(ds("0", "0"), k["0"]))
                put(dq_ref, hi, _dot_nn(ds("1", "0"), k["0"]) + _dot_nn(ds("1", "1"), k["1"]))
                put(dk_ref, lo, _dot_tn(ds("0", "0"), q["0"]) + _dot_tn(ds("1", "0"), q["1"]))
                put(dk_ref, hi, _dot_tn(ds("1", "1"), q["1"]) + _dot_tn(ds("a", "1"), q["a"]))
                put(dv_ref, lo, _dot_tn(p("0", "0"), do["0"]) + _dot_tn(p("1", "0"), do["1"]))
                put(dv_ref, hi, _dot_tn(p("1", "1"), do["1"]) + _dot_tn(p("a", "1"), do["a"]))

    def spec(rows, width, row_of, cb):
        return pl.BlockSpec((None, rows, width), lambda r, i: (r, row_of(i), cb))

    same = lambda i: i
    before = lambda i: jnp.maximum(2 * i - 1, 0)
    after = lambda i: jnp.minimum(2 * i + 2, n_blocks - 1)
    out = spec(CHUNK, D_ATTN, same, 0)
    return _call(
        body, name=name, grid=(dil, n // CHUNK),
        in_specs=[spec(CHUNK, D_ATTN, same, 0), spec(CHUNK, D_ATTN, same, 1), spec(CHUNK, D_ATTN, same, 2),
                  spec(KEY_BLOCK, D_ATTN, before, 1), spec(KEY_BLOCK, D_ATTN, before, 2),
                  spec(KEY_BLOCK, D_ATTN, after, 0),
                  spec(CHUNK, D_ATTN, same, 0), spec(KEY_BLOCK, D_ATTN, after, 0),
                  spec(CHUNK, STAT_LANES, same, 0), spec(KEY_BLOCK, STAT_LANES, after, 0)],
        out_specs=[out, out, out],
        out_shape=[jax.ShapeDtypeStruct((dil, n, D_ATTN), BF16)] * 3,
        scratch_shapes=[], semantics=("parallel", "parallel"), vmem_mib=40,
        args=(qkv, qkv, qkv, qkv, qkv, qkv, do, do, stats, stats), comm=comm)


def _window_sums(ext, window, backward):
    rows = ext.shape[0]
    acc, span = ext, 1
    while span < window:
        acc = acc + pltpu.roll(acc, (rows - span) if backward else span, axis=0)
        span *= 2
    return acc


def _mix_gate(o_list, st_list, hug, w_pool_g, pool_scale):
    seq = hug.shape[0]
    tm = 256
    halo_blocks = tm // POOL_HALO
    d4, d16 = DILATIONS[1], DILATIONS[2]

    def body(o1_ref, o4_ref, o16_ref, l1_ref, l4_ref, l16_ref, u_ref, halo_ref, ga_ref, gp_ref, wp_ref, sc_ref,
             y_ref, mix_ref, lse_ref, pooled_ref, n4_ref, n16_ref, nl4_ref, nl16_ref):
        i = pl.program_id(0)
        _from_pattern(o4_ref, n4_ref, d4)
        _from_pattern(o16_ref, n16_ref, d16)
        _from_pattern(l4_ref, nl4_ref, d4)
        _from_pattern(l16_ref, nl16_ref, d16)
        la, lb, lc = l1_ref[...], nl4_ref[0], nl16_ref[0]
        mx = jnp.maximum(jnp.maximum(la, lb), lc)
        ea, eb, ec = jnp.exp(la - mx), jnp.exp(lb - mx), jnp.exp(lc - mx)
        tot = ea + eb + ec
        lse_ref[...] = mx + jnp.log(tot)
        wa, wb, wc = ea / tot, eb / tot, ec / tot
        ga = ga_ref[...]
        silu_a = ga * jax.nn.sigmoid(ga)
        for h in range(N_HEADS):
            cols = slice(h * HEAD_DIM, (h + 1) * HEAD_DIM)
            hc = slice(h, h + 1)
            attn = wa[:, hc] * o1_ref[:, cols] + wb[:, hc] * n4_ref[h] + wc[:, hc] * n16_ref[h]
            mix_ref[:, cols] = attn
            y_ref[:, cols] = (attn * silu_a[:, cols]).astype(BF16)

        u = u_ref[...]
        halo = jnp.where(i > 0, halo_ref[...], 0.0)
        ext = jnp.concatenate([halo, u], axis=0)
        pos = i * tm + lax.broadcasted_iota(jnp.int32, (tm, 1), 0)
        gp = gp_ref[...]
        gated_scale = sc_ref[...] * (gp * jax.nn.sigmoid(gp))
        for g, window in enumerate(POOL_WINDOWS):
            cols = slice(g * POOL_GROUP_DIM, (g + 1) * POOL_GROUP_DIM)
            sums = _window_sums(ext[:, cols], window, backward=False)[POOL_HALO:, :]
            count = jnp.minimum(pos + 1, window).astype(F32)
            pooled = (sums / count - u[:, cols]).astype(BF16)
            pooled_ref[:, cols] = pooled
            pre = _dot_nn(pooled, wp_ref[g])
            out_cols = slice(D_ATTN + g * POOL_GROUP_DIM, D_ATTN + (g + 1) * POOL_GROUP_DIM)
            mix_ref[:, out_cols] = pre
            y_ref[:, out_cols] = (pre * gated_scale[:, cols]).astype(BF16)

    row = lambda width, cb=0: pl.BlockSpec((tm, width), lambda i: (i, cb))
    pat = lambda d, width: pl.BlockSpec((d, tm // d, width), lambda i: (0, i, 0))
    return pl.pallas_call(
        body, name="mix_gate", grid=(seq // tm,),
        in_specs=[row(D_ATTN), pat(d4, D_ATTN), pat(d16, D_ATTN),
                  row(STAT_LANES), pat(d4, STAT_LANES), pat(d16, STAT_LANES),
                  row(D_POOL),
                  pl.BlockSpec((POOL_HALO, D_POOL), lambda i: (jnp.maximum(i * halo_blocks - 1, 0), 0)),
                  row(D_ATTN, 1), row(D_POOL, 2),
                  pl.BlockSpec((len(POOL_WINDOWS), POOL_GROUP_DIM, POOL_GROUP_DIM), lambda i: (0, 0, 0)),
                  pl.BlockSpec((1, D_POOL), lambda i: (0, 0))],
        out_specs=[row(D_MODEL), row(D_MODEL), row(STAT_LANES), row(D_POOL)],
        out_shape=[jax.ShapeDtypeStruct((seq, D_MODEL), BF16), jax.ShapeDtypeStruct((seq, D_MODEL), F32),
                   jax.ShapeDtypeStruct((seq, STAT_LANES), F32), jax.ShapeDtypeStruct((seq, D_POOL), BF16)],
        scratch_shapes=[pltpu.VMEM((N_HEADS, tm, HEAD_DIM), F32), pltpu.VMEM((N_HEADS, tm, HEAD_DIM), F32),
                        pltpu.VMEM((1, tm, STAT_LANES), F32), pltpu.VMEM((1, tm, STAT_LANES), F32)],
        compiler_params=_params(("parallel",), 48),
    )(o_list[0][0], o_list[1], o_list[2], st_list[0][0], st_list[1], st_list[2],
      hug, hug, hug, hug, w_pool_g, pool_scale)


def _out_proj_loss(y, w_out_g, x, target, gain, bias):
    seq = x.shape[0]
    tm = 512

    def body(y_ref, w_ref, x_ref, t_ref, g_ref, b_ref, dz_ref, dzb_ref, gg_ref, gb_ref, loss_ref):
        @pl.when(pl.program_id(0) == 0)
        def _():
            gg_ref[...] = jnp.zeros_like(gg_ref)
            gb_ref[...] = jnp.zeros_like(gb_ref)
            loss_ref[...] = jnp.zeros_like(loss_ref)

        halves = [slice(0, tm // 2), slice(tm // 2, tm)]
        projected = [_dot_nn(y_ref[rows, :], w_ref[...]) for rows in halves]
        for rows, out in zip(halves, projected):
            z = DEEPNORM_ALPHA * x_ref[rows, :] + out
            mu = jnp.mean(z, axis=-1, keepdims=True)
            zc = z - mu
            rstd = lax.rsqrt(jnp.mean(zc * zc, axis=-1, keepdims=True) + LN_EPS)
            xhat = zc * rstd
            gain_v = g_ref[...]
            diff = xhat * gain_v + b_ref[...] - t_ref[rows, :]
            sq = _fold_rows(diff * diff)
            part = sq[:, :128]
            for k in range(1, D_MODEL // 128):
                part = part + sq[:, k * 128:(k + 1) * 128]
            loss_ref[...] += part
            dln = diff * (1.0 / D_MODEL)
            gg_ref[...] += _fold_rows(dln * xhat)
            gb_ref[...] += _fold_rows(dln)
            dxhat = dln * gain_v
            dz = rstd * (dxhat - jnp.mean(dxhat, axis=-1, keepdims=True)
                         - xhat * jnp.mean(dxhat * xhat, axis=-1, keepdims=True))
            dz_ref[rows, :] = dz
            dzb_ref[rows, :] = dz.astype(BF16)

    row = lambda: pl.BlockSpec((tm, D_MODEL), lambda i: (i, 0))
    vec = lambda: pl.BlockSpec((1, D_MODEL), lambda i: (0, 0))
    acc = lambda width: pl.BlockSpec((8, width), lambda i: (0, 0))
    return pl.pallas_call(
        body, name="out_proj_loss", grid=(seq // tm,),
        in_specs=[row(), pl.BlockSpec((D_MODEL, D_MODEL), lambda i: (0, 0), pipeline_mode=pl.Buffered(1)),
                  row(), row(), vec(), vec()],
        out_specs=[row(), row(), acc(D_MODEL), acc(D_MODEL), acc(128)],
        out_shape=[jax.ShapeDtypeStruct((seq, D_MODEL), F32), jax.ShapeDtypeStruct((seq, D_MODEL), BF16),
                   jax.ShapeDtypeStruct((8, D_MODEL), F32), jax.ShapeDtypeStruct((8, D_MODEL), F32),
                   jax.ShapeDtypeStruct((8, 128), F32)],
        compiler_params=_params(("arbitrary",), 56),
    )(y, w_out_g.reshape(D_MODEL, D_MODEL), x, target, gain, bias)


def _dy_gate_bwd(dzb, w_out_g, hug, mixpre, pool_scale, lse_all):
    seq = dzb.shape[0]
    tm = 256
    d4, d16 = DILATIONS[1], DILATIONS[2]

    def body(dz_ref, w_ref, ga_ref, gp_ref, mix_ref, sc_ref, lse_ref,
             dh_ref, dpo_ref, do1_ref, do4_ref, do16_ref, st1_ref, st4_ref, st16_ref, da_ref, st_ref):
        dy = _dot_nt(dz_ref[...], w_ref[...])
        ga = ga_ref[...]
        sig = jax.nn.sigmoid(ga)
        attn = mix_ref[:, :D_ATTN]
        dya = dy[:, :D_ATTN]
        dattn = dya * (ga * sig)
        dh_ref[:, :D_ATTN] = (dya * attn * (sig * (1.0 + ga * (1.0 - sig)))).astype(BF16)
        _store_slabs(da_ref, dattn)
        lane = lax.broadcasted_iota(jnp.int32, (tm, STAT_LANES), 1)
        stats = lse_ref[...]
        prod = dattn * attn
        for h in range(N_HEADS):
            delta = jnp.sum(prod[:, h * HEAD_DIM:(h + 1) * HEAD_DIM], axis=-1, keepdims=True)
            stats = jnp.where(lane == N_HEADS + h, delta, stats)
        st_ref[0] = stats
        do1_ref[...] = dattn.astype(BF16)
        st1_ref[...] = stats
        _to_pattern(da_ref, do4_ref, d4, BF16)
        _to_pattern(da_ref, do16_ref, d16, BF16)
        _to_pattern(st_ref, st4_ref, d4, F32)
        _to_pattern(st_ref, st16_ref, d16, F32)

        gp = gp_ref[...]
        sig = jax.nn.sigmoid(gp)
        dyp = dy[:, D_ATTN:]
        dpo_ref[...] = dyp * (gp * sig)
        dh_ref[:, D_ATTN:] = (dyp * (mix_ref[:, D_ATTN:] * sc_ref[...])
                              * (sig * (1.0 + gp * (1.0 - sig)))).astype(BF16)

    row = lambda width, cb=0: pl.BlockSpec((tm, width), lambda i: (i, cb))
    pat = lambda d, width: pl.BlockSpec((d, tm // d, width), lambda i: (0, i, 0))
    pat_shape = lambda d, width, dtype: jax.ShapeDtypeStruct((d, seq // d, width), dtype)
    outs = pl.pallas_call(
        body, name="dy_gate_bwd", grid=(seq // tm,),
        in_specs=[row(D_MODEL), pl.BlockSpec((D_MODEL, D_MODEL), lambda i: (0, 0)),
                  row(D_ATTN, 1), row(D_POOL, 2), row(D_MODEL), pl.BlockSpec((1, D_POOL), lambda i: (0, 0)),
                  row(STAT_LANES)],
        out_specs=[row(D_MODEL, D_IN // D_MODEL - 1), row(D_POOL),
                   row(D_ATTN), pat(d4, D_ATTN), pat(d16, D_ATTN),
                   row(STAT_LANES), pat(d4, STAT_LANES), pat(d16, STAT_LANES)],
        out_shape=[jax.ShapeDtypeStruct((seq, D_IN), BF16), jax.ShapeDtypeStruct((seq, D_POOL), F32),
                   jax.ShapeDtypeStruct((seq, D_ATTN), BF16), pat_shape(d4, D_ATTN, BF16), pat_shape(d16, D_ATTN, BF16),
                   jax.ShapeDtypeStruct((seq, STAT_LANES), F32), pat_shape(d4, STAT_LANES, F32),
                   pat_shape(d16, STAT_LANES, F32)],
        scratch_shapes=[pltpu.VMEM((N_HEADS, tm, HEAD_DIM), F32), pltpu.VMEM((1, tm, STAT_LANES), F32)],
        compiler_params=_params(("parallel",), 48),
    )(dzb, w_out_g.reshape(D_MODEL, D_MODEL), hug, hug, mixpre, pool_scale, lse_all)
    dh, dpo, do1, do4, do16, st1, st4, st16 = outs
    return dh, dpo, [do1[None], do4, do16], [st1[None], st4, st16]


def _pool_bwd(dh, dpo, mixpre, pooled, w_pool_g, pool_scale):
    seq = dpo.shape[0]
    tm = 256
    halo_blocks = tm // POOL_HALO
    last = seq // tm - 1
    n_groups = len(POOL_WINDOWS)

    def body(dh_in_ref, dpo_ref, halo_ref, pre_ref, pooled_ref, wp_ref, sc_ref, du_ref, gw_ref, gs_ref):
        i = pl.program_id(0)

        @pl.when(i == 0)
        def _():
            gw_ref[...] = jnp.zeros_like(gw_ref)
            gs_ref[...] = jnp.zeros_like(gs_ref)

        dpo = dpo_ref[...]
        scale = sc_ref[...]
        gs_ref[...] += _fold_rows(dpo * pre_ref[...])
        halo = jnp.where(i < last, halo_ref[...], 0.0)
        dpw = (jnp.concatenate([dpo, halo], axis=0) * scale).astype(BF16)
        pos = i * tm + lax.broadcasted_iota(jnp.int32, (tm + POOL_HALO, 1), 0)
        for g, window in enumerate(POOL_WINDOWS):
            cols = slice(g * POOL_GROUP_DIM, (g + 1) * POOL_GROUP_DIM)
            dpw_g = dpw[:, cols]
            gw_ref[g] += _dot_tn(pooled_ref[:, cols], dpw_g[:tm, :])
            dpooled = _dot_nt(dpw_g, wp_ref[g])
            count = jnp.minimum(pos + 1, window).astype(F32)
            sums = _window_sums(dpooled / count, window, backward=True)
            du_ref[:, cols] = (sums[:tm, :] - dpooled[:tm, :]).astype(BF16)

    row = lambda width, cb=0: pl.BlockSpec((tm, width), lambda i: (i, cb))
    return pl.pallas_call(
        body, name="pool_bwd", grid=(seq // tm,),
        in_specs=[ANY, row(D_POOL),
                  pl.BlockSpec((POOL_HALO, D_POOL),
                               lambda i: (jnp.minimum((i + 1) * halo_blocks, seq // POOL_HALO - 1), 0)),
                  row(D_POOL, 1), row(D_POOL),
                  pl.BlockSpec((n_groups, POOL_GROUP_DIM, POOL_GROUP_DIM), lambda i: (0, 0, 0)),
                  pl.BlockSpec((1, D_POOL), lambda i: (0, 0))],
        out_specs=[row(D_POOL, D_QKV // D_POOL),
                   pl.BlockSpec((n_groups, POOL_GROUP_DIM, POOL_GROUP_DIM), lambda i: (0, 0, 0)),
                   pl.BlockSpec((8, D_POOL), lambda i: (0, 0))],
        out_shape=[jax.ShapeDtypeStruct(dh.shape, dh.dtype),
                   jax.ShapeDtypeStruct((n_groups, POOL_GROUP_DIM, POOL_GROUP_DIM), F32),
                   jax.ShapeDtypeStruct((8, D_POOL), F32)],
        input_output_aliases={0: 0},
        compiler_params=_params(("arbitrary",), 40),
    )(dh, dpo, dpo, mixpre, pooled, w_pool_g, pool_scale)


def _sum_patterns(dh, parts, tabs, unrotate, col_block, name):
    seq = dh.shape[0]
    tm, tn = 256, D_ATTN
    per = D_ATTN // tn
    d4, d16 = DILATIONS[1], DILATIONS[2]

    def body(dh_in_ref, a1_ref, a4_ref, a16_ref, ct_ref, up_ref, down_ref, o_ref, n4_ref, n16_ref):
        _from_pattern(a4_ref, n4_ref, d4)
        _from_pattern(a16_ref, n16_ref, d16)
        for s in range(tn // HEAD_DIM):
            cols = slice(s * HEAD_DIM, (s + 1) * HEAD_DIM)
            tot = a1_ref[:, cols].astype(F32) + n4_ref[s] + n16_ref[s]
            if unrotate:
                tot = _rotate_heads(tot, ct_ref[...], -up_ref[...], -down_ref[...])
            o_ref[:, cols] = tot.astype(BF16)

    tab = pl.BlockSpec((tm, HEAD_DIM), lambda i, j: (i, 0))
    pat = lambda d: pl.BlockSpec((d, tm // d, tn), lambda i, j: (0, i, j))
    return pl.pallas_call(
        body, name=name, grid=(seq // tm, per),
        in_specs=[ANY, pl.BlockSpec((tm, tn), lambda i, j: (i, j)), pat(d4), pat(d16), tab, tab, tab],
        out_specs=pl.BlockSpec((tm, tn), lambda i, j: (i, col_block * per + j)),
        out_shape=jax.ShapeDtypeStruct(dh.shape, dh.dtype),
        scratch_shapes=[pltpu.VMEM((tn // HEAD_DIM, tm, HEAD_DIM), F32), pltpu.VMEM((tn // HEAD_DIM, tm, HEAD_DIM), F32)],
        input_output_aliases={0: 0},
        compiler_params=_params(("parallel", "parallel"), 32),
    )(dh, parts[0][0], parts[1], parts[2], *tabs)


def _grad_w_in(x, dh, half, name, comm=None):
    seq = x.shape[0]
    ts, td, te = 1024, D_MODEL // 2, SHARD_IN

    def body(half_ref, x_ref, dh_ref, o_ref):
        k = pl.program_id(1)
        part = _dot_tn(x_ref[...].astype(BF16), dh_ref[...])

        @pl.when(k == 0)
        def _():
            o_ref[...] = part

        @pl.when(k > 0)
        def _():
            o_ref[...] += part

    (g,), exchanged = _call(
        body, name=name, grid=(N_SHARDS, seq // ts),
        in_specs=[pl.BlockSpec((ts, td), lambda e, k, half_ref: (k, half_ref[0])),
                  pl.BlockSpec((ts, te), lambda e, k, half_ref: (k, e))],
        out_specs=[pl.BlockSpec((None, td, te), lambda e, k, half_ref: (e, 0, 0))],
        out_shape=[jax.ShapeDtypeStruct((N_SHARDS, td, te), F32)],
        scratch_shapes=[], semantics=("parallel", "arbitrary"), vmem_mib=48, args=(x, dh), comm=comm,
        prefetch=(half,))
    return g, exchanged


def _grad_w_out(y, dzb):
    seq = y.shape[0]
    ts, te = 512, 1024
    nk = seq // ts

    def body(y_ref, dz_ref, o_ref, acc_ref):
        k = pl.program_id(1)

        @pl.when(k == 0)
        def _():
            acc_ref[...] = jnp.zeros_like(acc_ref)

        acc_ref[...] += _dot_tn(y_ref[...], dz_ref[...])

        @pl.when(k == nk - 1)
        def _():
            o_ref[...] = acc_ref[...]

    return pl.pallas_call(
        body, name="grad_w_out", grid=(D_MODEL // te, nk),
        in_specs=[pl.BlockSpec((ts, te), lambda e, k: (k, e)), pl.BlockSpec((ts, D_MODEL), lambda e, k: (k, 0))],
        out_specs=pl.BlockSpec((te, D_MODEL), lambda e, k: (e, 0)),
        out_shape=jax.ShapeDtypeStruct((D_MODEL, D_MODEL), F32),
        scratch_shapes=[pltpu.VMEM((te, D_MODEL), F32)],
        compiler_params=_params(("parallel", "arbitrary"), 48),
    )(y, dzb)


def _grad_x(dh, w_in_g, dz, first=0, tiles=None, prev=None, comm=None):
    seq = dh.shape[0]
    tm, tk = 512, SHARD_IN
    tiles = seq // tm if tiles is None else tiles

    def body(*refs):
        dh_ref, w_ref, dz_ref, o_ref = refs[-4:]
        k = pl.program_id(1)
        part = _dot_nt(dh_ref[...], w_ref[...])

        @pl.when(k == 0)
        def _():
            o_ref[...] = DEEPNORM_ALPHA * dz_ref[...] + part

        @pl.when(k > 0)
        def _():
            o_ref[...] += part

    carried = [] if prev is None else [prev]
    (g_x,), exchanged = _call(
        body, name="grad_x_%d" % first, grid=(tiles, N_SHARDS),
        in_specs=[ANY] * len(carried) + [
            pl.BlockSpec((tm, tk), lambda i, k: (i + first, k)),
            pl.BlockSpec((None, D_MODEL, tk), lambda i, k: (k, 0, 0)),
            pl.BlockSpec((tm, D_MODEL), lambda i, k: (i + first, 0))],
        out_specs=[pl.BlockSpec((tm, D_MODEL), lambda i, k: (i + first, 0))],
        out_shape=[jax.ShapeDtypeStruct((seq, D_MODEL), F32)],
        scratch_shapes=[], semantics=("parallel", "arbitrary"), vmem_mib=48, args=(*carried, dh, w_in_g, dz),
        aliases={0: 0} if carried else None, comm=comm)
    return g_x, exchanged


def _pool_weight(w_pool_sh):
    n_groups = len(POOL_WINDOWS)
    shard_c = POOL_GROUP_DIM // N_SHARDS
    return (w_pool_sh.reshape(N_SHARDS, n_groups, shard_c, POOL_GROUP_DIM).transpose(1, 0, 2, 3)
            .reshape(n_groups, POOL_GROUP_DIM, POOL_GROUP_DIM))


def _pool_grad_pieces(g_w_pool):
    n_groups = len(POOL_WINDOWS)
    half_c = POOL_GROUP_DIM // N_SHARDS // 2
    return (g_w_pool.reshape(n_groups, N_SHARDS, 2, half_c, POOL_GROUP_DIM).transpose(1, 2, 0, 3, 4)
            .reshape(N_SHARDS, 2, n_groups * half_c, POOL_GROUP_DIM))


def _step(x, target, w_in_g, w_rest, pool_scale, gain, bias, place=None):
    seq = x.shape[0]
    tabs = _rope_tables(seq)
    qkv, gathered = _in_proj_qkv(x, w_in_g, tabs, comm=_allgather_weights(w_rest) if place else None)
    w_out_g, w_pool_sh = gathered if place else w_rest
    w_pool_g = _pool_weight(w_pool_sh)
    hug = _in_proj_pool_gate(x, w_in_g)
    o_list, st_list = [], []
    for p, dil in enumerate(DILATIONS):
        o, st = _attn_fwd(qkv[p], "attn_fwd_d%d" % dil)
        o_list.append(o)
        st_list.append(st)
    y, mixpre, lse_all, pooled = _mix_gate(o_list, st_list, hug, w_pool_g, pool_scale)
    dz, dzb, gain_part, bias_part, loss_part = _out_proj_loss(y, w_out_g, x, target, gain, bias)
    dh, dpo, do_list, stat_list = _dy_gate_bwd(dzb, w_out_g, hug, mixpre, pool_scale, lse_all)
    g_w_out = _grad_w_out(y, dzb)
    dh, g_w_pool, scale_part = _pool_bwd(dh, dpo, mixpre, pooled, w_pool_g, pool_scale)
    small = jnp.concatenate([scale_part, gain_part, bias_part, loss_part], axis=1)
    early = [g_w_out.reshape(N_SHARDS, 2, D_MODEL // (2 * N_SHARDS), D_MODEL), _pool_grad_pieces(g_w_pool)]

    bwd = lambda p, comm: _attn_bwd(qkv[p], do_list[p], stat_list[p], "attn_bwd_d%d" % DILATIONS[p], comm)
    if place is None:
        parts = [bwd(p, None)[0] for p in range(3)]
    else:
        core, chip_core = place
        part_a, recv = bwd(0, _exchange_halves(early))
        sums = [_add_own_half(g, r, core, "add_own_half_%d" % a) for a, (g, r) in enumerate(zip(early, recv))]
        part_b, recv = bwd(1, _scatter_to_chips([s[1] for s in sums]))
        bufs = [_add_chips(s[0], r, chip_core, "add_chips_%d" % a) for a, (s, r) in enumerate(zip(sums, recv))]
        part_c, early = bwd(2, _share_with_sibling(bufs))
        parts = [part_a, part_b, part_c]
    dh = _sum_patterns(dh, [t[0] for t in parts], tabs, True, 0, "sum_dq")
    dh = _sum_patterns(dh, [t[1] for t in parts], tabs, True, 1, "sum_dk")
    dh = _sum_patterns(dh, [t[2] for t in parts], tabs, False, 2, "sum_dv")
    if place is None:
        halves = [_grad_w_in(x, dh, jnp.full((1,), h, jnp.int32), "grad_w_in_%d" % h)[0] for h in range(2)]
        g_w_in = jnp.stack(halves, axis=1)
        g_x, _ = _grad_x(dh, w_in_g, dz)
    else:
        give, _ = _grad_w_in(x, dh, 1 - core, "grad_w_in_give")
        keep, recv = _grad_w_in(x, dh, core, "grad_w_in_keep", _send_to_sibling([give]))
        total, total_b = _add_pair(keep, recv[0], "add_own_half_w_in")
        rows = total.shape[1]
        cut = rows // 2
        g_x, recv = _grad_x(dh, w_in_g, dz, 0, 3, None, _scatter_to_chips([total_b], (0, cut)))
        g_x, recv = _grad_x(dh, w_in_g, dz, 3, 3, g_x, _scatter_to_chips([total_b], (cut, rows - cut), recv))
        buf = _add_chips(total, recv[0], chip_core, "add_chips_w_in")
        g_x, _ = _grad_x(dh, w_in_g, dz, 6, seq // 512 - 6, g_x)
        g_w_in = _run_exchange(_share_with_sibling([buf]), "share_w_in")[0]
    return g_x, g_w_in, early[0], early[1], small


def _exchange_halves(grads):
    n = len(grads)

    def copies(src, dst, sems):
        x, y, c, _ = _mesh_place()
        return [_remote(src[a].at[j, 1 - c], dst[a].at[j], sems[0].at[a, j], sems[1].at[a, j], (x, y, 1 - c))
                for a in range(n) for j in range(N_SHARDS)]

    def start(src, dst, sems):
        for cp in copies(src, dst, sems):
            cp.start()

    def finish(src, dst, sems):
        for cp in copies(src, dst, sems):
            cp.wait()

    return _Exchange(grads, [jax.ShapeDtypeStruct((N_SHARDS,) + g.shape[2:], g.dtype) for g in grads], {},
                     [pltpu.SemaphoreType.DMA((n, N_SHARDS))] * 2, start, finish)


def _add_own_half(grad, recv, core, name):
    _, _, r, c = grad.shape
    tr = min(r, 256)

    def body(core_ref, g_ref, r_ref, o_ref, ob_ref):
        tot = g_ref[...] + r_ref[...]
        o_ref[...] = tot
        ob_ref[...] = tot.astype(BF16)

    out = pl.BlockSpec((None, tr, c), lambda j, i, core_ref: (j, i, 0))
    return pl.pallas_call(
        body, name=name,
        grid_spec=pltpu.PrefetchScalarGridSpec(
            num_scalar_prefetch=1, grid=(N_SHARDS, r // tr),
            in_specs=[pl.BlockSpec((None, None, tr, c), lambda j, i, core_ref: (j, core_ref[0], i, 0)),
                      pl.BlockSpec((None, tr, c), lambda j, i, core_ref: (j, i, 0))],
            out_specs=[out, out]),
        out_shape=[jax.ShapeDtypeStruct((N_SHARDS, r, c), F32), jax.ShapeDtypeStruct((N_SHARDS, r, c), BF16)],
        compiler_params=_params(("parallel", "parallel"), 32),
    )(core, grad, recv)


def _send_to_sibling(arrays):
    n = len(arrays)

    def copies(src, dst, sems):
        x, y, c, _ = _mesh_place()
        return [_remote(src[a], dst[a], sems[0].at[a], sems[1].at[a], (x, y, 1 - c)) for a in range(n)]

    def start(src, dst, sems):
        for cp in copies(src, dst, sems):
            cp.start()

    def finish(src, dst, sems):
        for cp in copies(src, dst, sems):
            cp.wait()

    return _Exchange(arrays, [jax.ShapeDtypeStruct(t.shape, t.dtype) for t in arrays], {},
                     [pltpu.SemaphoreType.DMA((n,))] * 2, start, finish)


def _add_pair(a, b, name):
    _, r, c = a.shape
    tr = min(r, 256)

    def body(a_ref, b_ref, o_ref, ob_ref):
        tot = a_ref[...] + b_ref[...]
        o_ref[...] = tot
        ob_ref[...] = tot.astype(BF16)

    spec = pl.BlockSpec((None, tr, c), lambda j, i: (j, i, 0))
    return pl.pallas_call(
        body, name=name, grid=(N_SHARDS, r // tr), in_specs=[spec, spec], out_specs=[spec, spec],
        out_shape=[jax.ShapeDtypeStruct(a.shape, F32), jax.ShapeDtypeStruct(a.shape, BF16)],
        compiler_params=_params(("parallel", "parallel"), 32),
    )(a, b)


def _scatter_to_chips(sums, rows=None, into=None):
    n = len(sums)

    def copies(src, dst, sems):
        x, y, c, chips = _mesh_place()
        part = (lambda ref: ref) if rows is None else (lambda ref: ref.at[pl.ds(rows[0], rows[1])])
        return [_remote(part(src[a].at[2 * cx + cy]), part(dst[a].at[k]), sems[0].at[a, k], sems[1].at[a, k],
                        (cx, cy, c))
                for a in range(n) for k, (cx, cy) in enumerate(chips)]

    def start(src, dst, sems):
        for cp in copies(src, dst, sems):
            cp.start()

    def finish(src, dst, sems):
        for cp in copies(src, dst, sems):
            cp.wait()

    return _Exchange(sums + (into or []), [jax.ShapeDtypeStruct((3,) + s.shape[1:], s.dtype) for s in sums],
                     {n + a: a for a in range(n)} if into else {},
                     [pltpu.SemaphoreType.DMA((n, 3))] * 2, start, finish)


def _add_chips(sums, recv, chip_core, name):
    _, r, c = sums.shape
    tr = min(r, 256)

    def body(cc_ref, s_ref, r_ref, o_ref):
        o_ref[...] = ((s_ref[...] + r_ref[0].astype(F32)) + r_ref[1].astype(F32)) + r_ref[2].astype(F32)

    return pl.pallas_call(
        body, name=name,
        grid_spec=pltpu.PrefetchScalarGridSpec(
            num_scalar_prefetch=1, grid=(r // tr,),
            in_specs=[pl.BlockSpec((None, tr, c), lambda i, cc_ref: (cc_ref[0], i, 0)),
                      pl.BlockSpec((3, tr, c), lambda i, cc_ref: (0, i, 0))],
            out_specs=pl.BlockSpec((None, tr, c), lambda i, cc_ref: (cc_ref[1], i, 0))),
        out_shape=jax.ShapeDtypeStruct((2, r, c), F32),
        compiler_params=_params(("parallel",), 32),
    )(chip_core, sums, recv)


def _share_with_sibling(bufs):
    n = len(bufs)

    def copies(dst, sems, half):
        x, y, c, _ = _mesh_place()
        h = c if half == "mine" else 1 - c
        return [_remote(dst[a].at[h], dst[a].at[h], sems[0].at[a], sems[1].at[a], (x, y, 1 - c)) for a in range(n)]

    def start(ins, dst, sems):
        for cp in copies(dst, sems, "mine"):
            cp.start()

    def finish(ins, dst, sems):
        for cp in copies(dst, sems, "theirs"):
            cp.wait_recv()
        for cp in copies(dst, sems, "mine"):
            cp.wait_send()

    return _Exchange(bufs, [jax.ShapeDtypeStruct(b.shape, b.dtype) for b in bufs], {a: a for a in range(n)},
                     [pltpu.SemaphoreType.DMA((n,))] * 2, start, finish)


def _adam_math(w, g, m, v):
    m = ADAM_B1 * m + (1.0 - ADAM_B1) * g
    v = ADAM_B2 * v + (1.0 - ADAM_B2) * (g * g)
    m_hat = m / (1.0 - ADAM_B1 ** ADAM_STEP)
    v_hat = v / (1.0 - ADAM_B2 ** ADAM_STEP)
    delta = -ADAM_LR * (m_hat / (jnp.sqrt(v_hat) + ADAM_EPS) + ADAM_WD * w)
    return delta, m, v


def _small_allreduce_adamw(small, w_vec, m_vec, v_vec):
    width = small.shape[1]
    n_par = w_vec.shape[1]

    def body(s_ref, w_ref, m_ref, v_ref, loss_ref, g_ref, d_ref, nm_ref, nv_ref, gather_ref, send_sems, recv_sems):
        x, y, c = lax.axis_index("x"), lax.axis_index("y"), lax.axis_index("c")
        me = 4 * x + 2 * y + c
        gather_ref[me] = s_ref[...]
        copies = []
        for r in range(1, 8):
            bx, by, bc = (r >> 2) & 1, (r >> 1) & 1, r & 1
            peer = (x ^ bx, y ^ by, c ^ bc)
            cp = pltpu.make_async_remote_copy(
                src_ref=s_ref, dst_ref=gather_ref.at[me], send_sem=send_sems.at[r - 1], recv_sem=recv_sems.at[r - 1],
                device_id=peer, device_id_type=MESH)
            cp.start()
            copies.append(cp)
        for r in range(1, 8):
            bx, by, bc = (r >> 2) & 1, (r >> 1) & 1, r & 1
            theirs = gather_ref.at[4 * (x ^ bx) + 2 * (y ^ by) + (c ^ bc)]
            pltpu.make_async_remote_copy(
                src_ref=theirs, dst_ref=theirs, send_sem=send_sems.at[r - 1], recv_sem=recv_sems.at[r - 1],
                device_id=(x ^ bx, y ^ by, c ^ bc), device_id_type=MESH).wait_recv()
        for cp in copies:
            cp.wait_send()
        tot = gather_ref[0]
        for d in range(1, 8):
            tot = tot + gather_ref[d]
        tot = jnp.sum(tot, axis=0, keepdims=True)
        sq = jnp.sum(tot[:, n_par:], axis=1, keepdims=True)
        loss_ref[...] = jnp.broadcast_to(sq * (0.5 / D_MODEL), loss_ref.shape)
        g = tot[:, :n_par]
        g_ref[...] = g
        d_ref[...], nm_ref[...], nv_ref[...] = _adam_math(w_ref[...], g, m_ref[...], v_ref[...])

    vm = pl.BlockSpec(memory_space=pltpu.VMEM)
    vec = jax.ShapeDtypeStruct((1, n_par), F32)
    return pl.pallas_call(
        body, name="small_allreduce_adamw",
        in_specs=[vm] * 4, out_specs=[vm] * 5,
        out_shape=[jax.ShapeDtypeStruct((1, 128), F32), vec, vec, vec, vec],
        scratch_shapes=[pltpu.VMEM((8, 8, width), F32), pltpu.SemaphoreType.DMA((7,)), pltpu.SemaphoreType.DMA((7,))],
    )(small, w_vec, m_vec, v_vec)


def _adamw(w, g, m, v, name):
    r, c = w.shape
    tr = min(r, 256)

    def body(w_ref, g_ref, m_ref, v_ref, d_ref, nm_ref, nv_ref):
        d_ref[...], nm_ref[...], nv_ref[...] = _adam_math(w_ref[...], g_ref[...], m_ref[...], v_ref[...])

    spec = pl.BlockSpec((tr, c), lambda i: (i, 0))
    shape = jax.ShapeDtypeStruct((r, c), F32)
    return pl.pallas_call(
        body, name=name, grid=(r // tr,),
        in_specs=[spec] * 4, out_specs=[spec] * 3, out_shape=[shape] * 3,
        compiler_params=_params(("parallel",), 32),
    )(w, g, m, v)


def kernel(x, w_in, w_pool, pool_scale, w_out, ln_gain, ln_bias, loss_target, m_w_in, m_w_pool, m_pool_scale, m_w_out, m_ln_gain, m_ln_bias, v_w_in, v_w_pool, v_pool_scale, v_w_out, v_ln_gain, v_ln_bias):
    xi, yi, ci = lax.axis_index("x"), lax.axis_index("y"), lax.axis_index("c")
    chip = (2 * xi + yi).astype(jnp.int32).reshape(1)
    core = ci.astype(jnp.int32).reshape(1)
    n_groups = len(POOL_WINDOWS)
    shard_c = w_pool.shape[2]

    w_in_b = _cast_bf16(w_in[0], chip, "cast_w_in", 256)
    w_out_b = _cast_bf16(w_out[0], chip, "cast_w_out", 256)
    w_pool_b = _cast_bf16(w_pool[0].reshape(n_groups * shard_c, POOL_GROUP_DIM), chip, "cast_w_pool", 256)
    w_in_g = _run_exchange(_allgather_weights([w_in_b]), "allgather_w_in")[0]

    chip_core = jnp.concatenate([chip, core])
    g_x, full_in, full_out, full_pool, small = _step(
        x[0], loss_target[0], w_in_g, [w_out_b, w_pool_b], pool_scale, ln_gain, ln_bias, (core, chip_core))
    half_c = shard_c // 2
    grad_w_in = full_in.reshape(D_MODEL, SHARD_IN)
    grad_w_out = full_out.reshape(D_MODEL // N_SHARDS, D_MODEL)
    grad_w_pool = (full_pool.reshape(2, n_groups, half_c, POOL_GROUP_DIM).transpose(1, 0, 2, 3)
                   .reshape(n_groups * shard_c, POOL_GROUP_DIM))

    d_in, nm_in, nv_in = _adamw(w_in[0], grad_w_in, m_w_in[0], v_w_in[0], "adamw_w_in")
    d_out, nm_out, nv_out = _adamw(w_out[0], grad_w_out, m_w_out[0], v_w_out[0], "adamw_w_out")
    flat = lambda t: t[0].reshape(n_groups * shard_c, POOL_GROUP_DIM)
    d_pool, nm_pool, nv_pool = _adamw(flat(w_pool), grad_w_pool, flat(m_w_pool), flat(v_w_pool), "adamw_w_pool")

    cat = lambda a, b, c: jnp.concatenate([a, b, c], axis=1)
    loss_v, g_vec, d_vec, nm_vec, nv_vec = _small_allreduce_adamw(
        small, cat(pool_scale, ln_gain, ln_bias), cat(m_pool_scale, m_ln_gain, m_ln_bias),
        cat(v_pool_scale, v_ln_gain, v_ln_bias))

    def split(vec):
        return vec[:, :D_POOL], vec[:, D_POOL:D_POOL + D_MODEL], vec[:, D_POOL + D_MODEL:]

    g_scale, g_gain, g_bias = split(g_vec)
    d_scale, d_gain, d_bias = split(d_vec)
    nm_scale, nm_gain, nm_bias = split(nm_vec)
    nv_scale, nv_gain, nv_bias = split(nv_vec)
    pool_shape = w_pool.shape
    return (loss_v[0, 0], g_x[None],
            grad_w_in[None], grad_w_pool.reshape(pool_shape), g_scale, grad_w_out[None], g_gain, g_bias,
            d_in[None], d_pool.reshape(pool_shape), d_scale, d_out[None], d_gain, d_bias,
            nm_in[None], nm_pool.reshape(pool_shape), nm_scale, nm_out[None], nm_gain, nm_bias,
            nv_in[None], nv_pool.reshape(pool_shape), nv_scale, nv_out[None], nv_gain, nv_bias)
```

```python
import functools

import jax
import jax.numpy as jnp
from jax import lax
from jax.experimental import pallas as pl
from jax.experimental.pallas import tpu as pltpu

F32 = jnp.float32
BF16 = jnp.bfloat16
MESH = pl.DeviceIdType.MESH
ANY = pl.BlockSpec(memory_space=pl.ANY)

D_MODEL = 2048
D_ATTN = 1024
D_POOL = 1024
HEAD_DIM = 128
N_HEADS = 8
ROPE_DIM = 32
ROPE_THETA = 500000.0
DILATIONS = (1, 4, 16)
KEY_BLOCK = 128
CHUNK = 2 * KEY_BLOCK
STAT_LANES = 128
POOL_WINDOWS = (2, 4, 8, 16)
POOL_GROUP_DIM = 256
POOL_HALO = 16
D_QKV = 3 * D_ATTN
D_UG = D_POOL + D_MODEL
D_IN = D_QKV + D_UG
N_SHARDS = 4
SHARD_IN = D_IN // N_SHARDS
LN_EPS = 1e-5
DEEPNORM_ALPHA = 2.0 ** 0.25
ADAM_LR = 0.001
ADAM_B1 = 0.9
ADAM_B2 = 0.999
ADAM_EPS = 1e-08
ADAM_WD = 0.01
ADAM_STEP = 10
NEG = -1e30
MIB = 1024 * 1024


def _params(sem, vmem_mib):
    return pltpu.CompilerParams(dimension_semantics=sem, vmem_limit_bytes=vmem_mib * MIB)


def _pallas(body, **kwargs):
    pin = lambda s: pltpu.HBM(s.shape, s.dtype) if len(s.shape) >= 2 else s
    out_shape = kwargs.pop("out_shape")
    out_shape = [pin(s) for s in out_shape] if isinstance(out_shape, (list, tuple)) else pin(out_shape)
    call = pl.pallas_call(body, out_shape=out_shape, **kwargs)

    def run(*operands):
        return call(*[pltpu.with_memory_space_constraint(o, pltpu.HBM) if o.ndim >= 2 else o for o in operands])

    return run


class _Exchange:
    def __init__(self, operands, out_shape, aliases, sems, start, finish):
        self.operands, self.out_shape, self.aliases, self.sems = list(operands), list(out_shape), dict(aliases), list(sems)
        self.start, self.finish = start, finish


def _run_exchange(comm, name):
    n_in, n_out = len(comm.operands), len(comm.out_shape)

    def body(*refs):
        ins, outs, sems = refs[:n_in], refs[n_in:n_in + n_out], refs[n_in + n_out:]
        comm.start(ins, outs, sems)
        comm.finish(ins, outs, sems)

    return _pallas(
        body, name=name, in_specs=[ANY] * n_in, out_specs=[ANY] * n_out, out_shape=comm.out_shape,
        input_output_aliases=comm.aliases, scratch_shapes=comm.sems,
    )(*comm.operands)


def _call(body, *, name, grid, in_specs, out_specs, out_shape, scratch_shapes, semantics, vmem_mib, args,
          aliases=None, comm=None, prefetch=()):
    aliases = dict(aliases or {})
    n_pre, n_in, n_out, n_scr = len(prefetch), len(in_specs), len(out_specs), len(scratch_shapes)
    c_in, c_out = (len(comm.operands), len(comm.out_shape)) if comm else (0, 0)
    c_shapes, c_sems, c_operands = (comm.out_shape, comm.sems, comm.operands) if comm else ([], [], [])

    def hosted(*refs):
        pre, refs = refs[:n_pre], refs[n_pre:]
        a = n_in
        b = a + c_in
        c = b + n_out
        d = c + c_out
        e = d + n_scr
        if comm is None:
            body(*pre, *refs)
            return
        ids = [pl.program_id(k) for k in range(len(grid))]
        first = functools.reduce(jnp.logical_and, [i == 0 for i in ids])
        last = functools.reduce(jnp.logical_and, [i == g - 1 for i, g in zip(ids, grid)])

        @pl.when(first)
        def _():
            comm.start(refs[a:b], refs[c:d], refs[e:])

        body(*pre, *refs[:a], *refs[b:c], *refs[d:e])

        @pl.when(last)
        def _():
            comm.finish(refs[a:b], refs[c:d], refs[e:])

    if comm:
        semantics = ("arbitrary",) * len(grid)
        for i, o in comm.aliases.items():
            aliases[n_pre + n_in + i] = n_out + o
    outs = _pallas(
        hosted, name=name,
        grid_spec=pltpu.PrefetchScalarGridSpec(
            num_scalar_prefetch=n_pre, grid=grid, in_specs=list(in_specs) + [ANY] * c_in,
            out_specs=list(out_specs) + [ANY] * c_out, scratch_shapes=list(scratch_shapes) + c_sems),
        out_shape=list(out_shape) + c_shapes, input_output_aliases=aliases,
        compiler_params=_params(semantics, vmem_mib),
    )(*prefetch, *args, *c_operands)
    return list(outs[:n_out]), list(outs[n_out:])


def _dot_nn(a, b):
    return jnp.dot(a, b, preferred_element_type=F32)


def _dot_nt(a, b):
    return lax.dot_general(a, b, (((1,), (1,)), ((), ())), preferred_element_type=F32)


def _dot_tn(a, b):
    return lax.dot_general(a, b, (((0,), (0,)), ((), ())), preferred_element_type=F32)


def _fold_rows(a):
    r, c = a.shape
    return jnp.sum(a.reshape(r // 8, 8, c), axis=0)


def _cast_bf16(a, chip, name, rows):
    r, c = a.shape

    def body(chip_ref, a_ref, o_ref):
        o_ref[...] = a_ref[...].astype(BF16)

    return _pallas(
        body, name=name,
        grid_spec=pltpu.PrefetchScalarGridSpec(
            num_scalar_prefetch=1, grid=(r // rows,),
            in_specs=[pl.BlockSpec((rows, c), lambda i, chip_ref: (i, 0))],
            out_specs=pl.BlockSpec((None, rows, c), lambda i, chip_ref: (chip_ref[0], i, 0))),
        out_shape=jax.ShapeDtypeStruct((N_SHARDS, r, c), BF16),
        compiler_params=_params(("parallel",), 32),
    )(chip, a)


def _mesh_place():
    x, y, c = lax.axis_index("x"), lax.axis_index("y"), lax.axis_index("c")
    return x, y, c, [(1 - x, y), (x, 1 - y), (1 - x, 1 - y)]


def _remote(src, dst, send_sem, recv_sem, to):
    return pltpu.make_async_remote_copy(src_ref=src, dst_ref=dst, send_sem=send_sem, recv_sem=recv_sem,
                                        device_id=to, device_id_type=MESH)


def _allgather_weights(bufs):
    n = len(bufs)

    def half(a, core):
        rows = bufs[a].shape[1] // 2
        return pl.ds(core * rows, rows)

    def ici_copies(dst, sems):
        x, y, c, chips = _mesh_place()
        own = lambda a: dst[a].at[2 * x + y, half(a, c)]
        return [_remote(own(a), own(a), sems[0].at[a, k], sems[1].at[a, k], (cx, cy, c))
                for a in range(n) for k, (cx, cy) in enumerate(chips)]

    def start(ins, dst, sems):
        for cp in ici_copies(dst, sems):
            cp.start()

    def finish(ins, dst, sems):
        x, y, c, chips = _mesh_place()
        sibling = (x, y, 1 - c)
        passed_on = []
        for k, (cx, cy) in enumerate(chips):
            for a in range(n):
                landed = dst[a].at[2 * cx + cy, half(a, c)]
                _remote(landed, landed, sems[0].at[a, k], sems[1].at[a, k], (cx, cy, c)).wait_recv()
                cp = _remote(landed, landed, sems[2].at[a, k], sems[3].at[a, k], sibling)
                cp.start()
                passed_on.append(cp)
        for k, (cx, cy) in enumerate(chips):
            for a in range(n):
                passed = dst[a].at[2 * cx + cy, half(a, 1 - c)]
                _remote(passed, passed, sems[2].at[a, k], sems[3].at[a, k], sibling).wait_recv()
        for cp in ici_copies(dst, sems) + passed_on:
            cp.wait_send()

    return _Exchange(bufs, [jax.ShapeDtypeStruct(b.shape, b.dtype) for b in bufs], {a: a for a in range(n)},
                     [pltpu.SemaphoreType.DMA((n, 3))] * 4, start, finish)


def _rope_tables(seq):
    half = ROPE_DIM // 2
    inv_freq = ROPE_THETA ** (-(2.0 * jnp.arange(half, dtype=F32)) / ROPE_DIM)
    ang = jnp.arange(seq, dtype=jnp.int32).astype(F32)[:, None] * inv_freq[None, :]
    cos, sin = jnp.cos(ang), jnp.sin(ang)
    pad = jnp.zeros((seq, HEAD_DIM - ROPE_DIM), F32)
    zeros = jnp.zeros((seq, half), F32)
    c_tab = jnp.concatenate([cos, cos, pad + 1.0], axis=1)
    up_tab = jnp.concatenate([-sin, zeros, pad], axis=1)
    down_tab = jnp.concatenate([zeros, sin, pad], axis=1)
    return c_tab, up_tab, down_tab


def _rotate_heads(t, c_tab, up_tab, down_tab):
    outs = []
    for h in range(t.shape[1] // HEAD_DIM):
        th = t[:, h * HEAD_DIM:(h + 1) * HEAD_DIM]
        up = pltpu.roll(th, HEAD_DIM - ROPE_DIM // 2, axis=1)
        down = pltpu.roll(th, ROPE_DIM // 2, axis=1)
        outs.append(th * c_tab + up * up_tab + down * down_tab)
    return outs[0] if len(outs) == 1 else jnp.concatenate(outs, axis=1)


def _to_pattern(slabs_ref, dst_ref, dil, dtype):
    n_slabs, rows, _ = slabs_ref.shape
    for s in range(n_slabs):
        for r in range(dil):
            dst_ref[r, :, s * 128:(s + 1) * 128] = slabs_ref[s, pl.ds(r, rows // dil, dil), :].astype(dtype)


def _from_pattern(src_ref, slabs_ref, dil):
    n_slabs, rows, _ = slabs_ref.shape
    for s in range(n_slabs):
        for r in range(dil):
            slabs_ref[s, pl.ds(r, rows // dil, dil), :] = src_ref[r, :, s * 128:(s + 1) * 128].astype(F32)


def _store_slabs(slabs_ref, value):
    for s in range(slabs_ref.shape[0]):
        slabs_ref[s] = value[:, s * 128:(s + 1) * 128]


def _in_proj_qkv(x, w_in_g, tabs, comm=None):
    seq = x.shape[0]
    tm, tn = 512, SHARD_IN
    heads = tn // HEAD_DIM
    k_heads_in_second = 2 * D_ATTN // HEAD_DIM - heads
    d4, d16 = DILATIONS[1], DILATIONS[2]

    def body(x_ref, w_ref, c_ref, up_ref, down_ref, o1_ref, o4_ref, o16_ref, res_ref):
        shard = pl.program_id(0)
        xb = x_ref[...].astype(BF16)
        group = 4 * HEAD_DIM
        accs = [_dot_nn(xb, w_ref[:, g * group:(g + 1) * group]) for g in range(tn // group)]

        plain = shard == 1
        c_plain = jnp.where(plain, 1.0, c_ref[...])
        up_plain = jnp.where(plain, 0.0, up_ref[...])
        down_plain = jnp.where(plain, 0.0, down_ref[...])
        for h in range(heads):
            lanes = (h * HEAD_DIM) % group
            th = accs[h * HEAD_DIM // group][:, lanes:lanes + HEAD_DIM]
            if h < k_heads_in_second:
                th = _rotate_heads(th, c_ref[...], up_ref[...], down_ref[...])
            else:
                th = _rotate_heads(th, c_plain, up_plain, down_plain)
            res_ref[h] = th
            o1_ref[:, h * HEAD_DIM:(h + 1) * HEAD_DIM] = th.astype(BF16)
        _to_pattern(res_ref, o4_ref, d4, BF16)
        _to_pattern(res_ref, o16_ref, d16, BF16)

    tab_spec = pl.BlockSpec((tm, HEAD_DIM), lambda s, i: (i, 0))
    (o1, o4, o16), exchanged = _call(
        body, name="in_proj_qkv", grid=(D_QKV // tn, seq // tm),
        in_specs=[pl.BlockSpec((tm, D_MODEL), lambda s, i: (i, 0)),
                  pl.BlockSpec((None, D_MODEL, tn), lambda s, i: (s, 0, 0)),
                  tab_spec, tab_spec, tab_spec],
        out_specs=[pl.BlockSpec((tm, tn), lambda s, i: (i, s)),
                   pl.BlockSpec((d4, tm // d4, tn), lambda s, i: (0, i, s)),
                   pl.BlockSpec((d16, tm // d16, tn), lambda s, i: (0, i, s))],
        out_shape=[jax.ShapeDtypeStruct((seq, D_QKV), BF16),
                   jax.ShapeDtypeStruct((d4, seq // d4, D_QKV), BF16),
                   jax.ShapeDtypeStruct((d16, seq // d16, D_QKV), BF16)],
        scratch_shapes=[pltpu.VMEM((heads, tm, HEAD_DIM), F32)],
        semantics=("parallel", "parallel"), vmem_mib=52, args=(x, w_in_g, *tabs), comm=comm)
    return [o1[None], o4, o16], exchanged


def _in_proj_pool_gate(x, w_in_g):
    seq = x.shape[0]
    tm, tn = 512, SHARD_IN
    first_shard = D_QKV // tn

    def body(x_ref, w_ref, o_ref):
        o_ref[...] = _dot_nn(x_ref[...].astype(BF16), w_ref[...])

    return _pallas(
        body, name="in_proj_pool_gate", grid=(D_UG // tn, seq // tm),
        in_specs=[pl.BlockSpec((tm, D_MODEL), lambda s, i: (i, 0)),
                  pl.BlockSpec((None, D_MODEL, tn), lambda s, i: (s + first_shard, 0, 0))],
        out_specs=pl.BlockSpec((tm, tn), lambda s, i: (i, s)),
        out_shape=jax.ShapeDtypeStruct((seq, D_UG), F32),
        compiler_params=_params(("parallel", "parallel"), 48),
    )(x, w_in_g)


def _band_masks():
    row = lax.broadcasted_iota(jnp.int32, (KEY_BLOCK, KEY_BLOCK), 0)
    col = lax.broadcasted_iota(jnp.int32, (KEY_BLOCK, KEY_BLOCK), 1)
    return col <= row, col >= row


def _attn_fwd(qkv, name):
    dil, n, _ = qkv.shape
    scale = HEAD_DIM ** -0.5
    lo, hi = slice(0, KEY_BLOCK), slice(KEY_BLOCK, CHUNK)

    def body(q_ref, k_ref, v_ref, kb_ref, vb_ref, o_ref, st_ref):
        i = pl.program_id(1)
        cur_mask, prev_mask = _band_masks()
        before_mask = jnp.logical_and(prev_mask, i > 0)
        lane = lax.broadcasted_iota(jnp.int32, (KEY_BLOCK, STAT_LANES), 1)
        tasks = [(rows, h) for rows in (lo, hi) for h in range(N_HEADS)]
        head = lambda h: slice(h * HEAD_DIM, (h + 1) * HEAD_DIM)

        def prev_of(rows, h):
            if rows is lo:
                return kb_ref[:, head(h)], vb_ref[:, head(h)], before_mask
            return k_ref[lo, head(h)], v_ref[lo, head(h)], prev_mask

        scores = []
        for rows, h in tasks:
            q = q_ref[rows, head(h)]
            scores.append((_dot_nt(q, prev_of(rows, h)[0]), _dot_nt(q, k_ref[rows, head(h)])))
        probs = []
        for (rows, h), (qk_prev, qk_cur) in zip(tasks, scores):
            s_prev = jnp.where(prev_of(rows, h)[2], qk_prev * scale, NEG)
            s_cur = jnp.where(cur_mask, qk_cur * scale, NEG)
            m = jnp.max(jnp.maximum(s_prev, s_cur), axis=-1, keepdims=True)
            p_prev = jnp.exp(s_prev - m)
            p_cur = jnp.exp(s_cur - m)
            den = jnp.sum(p_prev + p_cur, axis=-1, keepdims=True)
            probs.append((p_prev.astype(BF16), p_cur.astype(BF16), den, m + jnp.log(den)))
        stats = [jnp.zeros((KEY_BLOCK, STAT_LANES), F32), jnp.zeros((KEY_BLOCK, STAT_LANES), F32)]
        for (rows, h), (p_prev, p_cur, den, lse) in zip(tasks, probs):
            o = _dot_nn(p_cur, v_ref[rows, head(h)]) + _dot_nn(p_prev, prev_of(rows, h)[1])
            o_ref[rows, head(h)] = o / den
            b = 0 if rows is lo else 1
            stats[b] = jnp.where(lane == h, lse, stats[b])
        st_ref[lo, :] = stats[0]
        st_ref[hi, :] = stats[1]

    main = lambda cb: pl.BlockSpec((None, CHUNK, D_ATTN), lambda r, i: (r, i, cb))
    before = lambda cb: pl.BlockSpec((None, KEY_BLOCK, D_ATTN), lambda r, i: (r, jnp.maximum(2 * i - 1, 0), cb))
    return _pallas(
        body, name=name, grid=(dil, n // CHUNK),
        in_specs=[main(0), main(1), main(2), before(1), before(2)],
        out_specs=[main(0), pl.BlockSpec((None, CHUNK, STAT_LANES), lambda r, i: (r, i, 0))],
        out_shape=[jax.ShapeDtypeStruct((dil, n, D_ATTN), F32), jax.ShapeDtypeStruct((dil, n, STAT_LANES), F32)],
        compiler_params=_params(("parallel", "parallel"), 40),
    )(qkv, qkv, qkv, qkv, qkv)


def _attn_bwd(qkv, do, stats, name, comm=None):
    dil, n, _ = qkv.shape
    n_blocks = n // KEY_BLOCK
    last = n // CHUNK - 1
    scale = HEAD_DIM ** -0.5
    lo, hi = slice(0, KEY_BLOCK), slice(KEY_BLOCK, CHUNK)

    def body(q_ref, k_ref, v_ref, kb_ref, vb_ref, qa_ref, do_ref, doa_ref, st_ref, sta_ref, dq_ref, dk_ref, dv_ref):
        i = pl.program_id(1)
        cur_mask, prev_mask = _band_masks()
        before_mask = jnp.logical_and(prev_mask, i > 0)
        after_mask = jnp.logical_and(prev_mask, i < last)

        def operands(h):
            cols = slice(h * HEAD_DIM, (h + 1) * HEAD_DIM)
            lse_c, del_c = slice(h, h + 1), slice(N_HEADS + h, N_HEADS + h + 1)
            q = {"0": q_ref[lo, cols], "1": q_ref[hi, cols], "a": qa_ref[:, cols]}
            k = {"0": k_ref[lo, cols], "1": k_ref[hi, cols], "b": kb_ref[:, cols]}
            v = {"0": v_ref[lo, cols], "1": v_ref[hi, cols], "b": vb_ref[:, cols]}
            do = {"0": do_ref[lo, cols], "1": do_ref[hi, cols], "a": doa_ref[:, cols]}
            st = {"0": (st_ref[lo, lse_c], st_ref[lo, del_c]), "1": (st_ref[hi, lse_c], st_ref[hi, del_c]),
                  "a": (sta_ref[:, lse_c], sta_ref[:, del_c])}
            return cols, q, k, v, do, st

        pairs = [("0", "b", before_mask), ("0", "0", cur_mask), ("1", "0", prev_mask), ("1", "1", cur_mask),
                 ("a", "1", after_mask)]
        group = N_HEADS // 2
        for first_head in range(0, N_HEADS, group):
            heads = range(first_head, first_head + group)
            raw = {}
            for h in heads:
                _, q, k, v, do, _ = operands(h)
                for qi, ki, _ in pairs:
                    raw[h, qi, ki] = (_dot_nt(q[qi], k[ki]), _dot_nt(do[qi], v[ki]))
            grads = {}
            for h in heads:
                st = operands(h)[5]
                for qi, ki, mask in pairs:
                    qk, dp = raw[h, qi, ki]
                    lse, delta = st[qi]
                    p = jnp.exp(jnp.where(mask, qk * scale, NEG) - lse)
                    grads[h, qi, ki] = (p.astype(BF16), (p * (dp - delta) * scale).astype(BF16))
            for h in heads:
                cols, q, k, v, do, _ = operands(h)
                p = lambda qi, ki: grads[h, qi, ki][0]
                ds = lambda qi, ki: grads[h, qi, ki][1]
                def put(ref, rows, val, cols=cols):
                    ref[rows, cols] = val.astype(ref.dtype)

                put(dq_ref, lo, _dot_nn(ds("0", "b"), k["b"]) + _dot_nn(ds("0", "0"), k["0"]))
                put(dq_ref, hi, _dot_nn(ds("1", "0"), k["0"]) + _dot_nn(ds("1", "1"), k["1"]))
                put(dk_ref, lo, _dot_tn(ds("0", "0"), q["0"]) + _dot_tn(ds("1", "0"), q["1"]))
                put(dk_ref, hi, _dot_tn(ds("1", "1"), q["1"]) + _dot_tn(ds("a", "1"), q["a"]))
                put(dv_ref, lo, _dot_tn(p("0", "0"), do["0"]) + _dot_tn(p("1", "0"), do["1"]))
                put(dv_ref, hi, _dot_tn(p("1", "1"), do["1"]) + _dot_tn(p("a", "1"), do["a"]))

    def spec(rows, width, row_of, cb):
        return pl.BlockSpec((None, rows, width), lambda r, i: (r, row_of(i), cb))

    same = lambda i: i
    before = lambda i: jnp.maximum(2 * i - 1, 0)
    after = lambda i: jnp.minimum(2 * i + 2, n_blocks - 1)
    out = spec(CHUNK, D_ATTN, same, 0)
    return _call(
        body, name=name, grid=(dil, n // CHUNK),
        in_specs=[spec(CHUNK, D_ATTN, same, 0), spec(CHUNK, D_ATTN, same, 1), spec(CHUNK, D_ATTN, same, 2),
                  spec(KEY_BLOCK, D_ATTN, before, 1), spec(KEY_BLOCK, D_ATTN, before, 2),
                  spec(KEY_BLOCK, D_ATTN, after, 0),
                  spec(CHUNK, D_ATTN, same, 0), spec(KEY_BLOCK, D_ATTN, after, 0),
                  spec(CHUNK, STAT_LANES, same, 0), spec(KEY_BLOCK, STAT_LANES, after, 0)],
        out_specs=[out, out, out],
        out_shape=[jax.ShapeDtypeStruct((dil, n, D_ATTN), BF16)] * 3,
        scratch_shapes=[], semantics=("parallel", "parallel"), vmem_mib=40,
        args=(qkv, qkv, qkv, qkv, qkv, qkv, do, do, stats, stats), comm=comm)


def _window_sums(ext, window, backward):
    rows = ext.shape[0]
    acc, span = ext, 1
    while span < window:
        acc = acc + pltpu.roll(acc, (rows - span) if backward else span, axis=0)
        span *= 2
    return acc


def _mix_gate(o_list, st_list, hug, w_pool_g, pool_scale):
    seq = hug.shape[0]
    tm = 256
    halo_blocks = tm // POOL_HALO
    d4, d16 = DILATIONS[1], DILATIONS[2]

    def body(o1_ref, o4_ref, o16_ref, l1_ref, l4_ref, l16_ref, u_ref, halo_ref, ga_ref, gp_ref, wp_ref, sc_ref,
             y_ref, mix_ref, lse_ref, pooled_ref, n4_ref, n16_ref, nl4_ref, nl16_ref):
        i = pl.program_id(0)
        _from_pattern(o4_ref, n4_ref, d4)
        _from_pattern(o16_ref, n16_ref, d16)
        _from_pattern(l4_ref, nl4_ref, d4)
        _from_pattern(l16_ref, nl16_ref, d16)
        la, lb, lc = l1_ref[...], nl4_ref[0], nl16_ref[0]
        mx = jnp.maximum(jnp.maximum(la, lb), lc)
        ea, eb, ec = jnp.exp(la - mx), jnp.exp(lb - mx), jnp.exp(lc - mx)
        tot = ea + eb + ec
        lse_ref[...] = mx + jnp.log(tot)
        wa, wb, wc = ea / tot, eb / tot, ec / tot
        ga = ga_ref[...]
        silu_a = ga * jax.nn.sigmoid(ga)
        for h in range(N_HEADS):
            cols = slice(h * HEAD_DIM, (h + 1) * HEAD_DIM)
            hc = slice(h, h + 1)
            attn = wa[:, hc] * o1_ref[:, cols] + wb[:, hc] * n4_ref[h] + wc[:, hc] * n16_ref[h]
            mix_ref[:, cols] = attn
            y_ref[:, cols] = (attn * silu_a[:, cols]).astype(BF16)

        u = u_ref[...]
        halo = jnp.where(i > 0, halo_ref[...], 0.0)
        ext = jnp.concatenate([halo, u], axis=0)
        pos = i * tm + lax.broadcasted_iota(jnp.int32, (tm, 1), 0)
        gp = gp_ref[...]
        gated_scale = sc_ref[...] * (gp * jax.nn.sigmoid(gp))
        for g, window in enumerate(POOL_WINDOWS):
            cols = slice(g * POOL_GROUP_DIM, (g + 1) * POOL_GROUP_DIM)
            sums = _window_sums(ext[:, cols], window, backward=False)[POOL_HALO:, :]
            count = jnp.minimum(pos + 1, window).astype(F32)
            pooled = (sums / count - u[:, cols]).astype(BF16)
            pooled_ref[:, cols] = pooled
            pre = _dot_nn(pooled, wp_ref[g])
            out_cols = slice(D_ATTN + g * POOL_GROUP_DIM, D_ATTN + (g + 1) * POOL_GROUP_DIM)
            mix_ref[:, out_cols] = pre
            y_ref[:, out_cols] = (pre * gated_scale[:, cols]).astype(BF16)

    row = lambda width, cb=0: pl.BlockSpec((tm, width), lambda i: (i, cb))
    pat = lambda d, width: pl.BlockSpec((d, tm // d, width), lambda i: (0, i, 0))
    return _pallas(
        body, name="mix_gate", grid=(seq // tm,),
        in_specs=[row(D_ATTN), pat(d4, D_ATTN), pat(d16, D_ATTN),
                  row(STAT_LANES), pat(d4, STAT_LANES), pat(d16, STAT_LANES),
                  row(D_POOL),
                  pl.BlockSpec((POOL_HALO, D_POOL), lambda i: (jnp.maximum(i * halo_blocks - 1, 0), 0)),
                  row(D_ATTN, 1), row(D_POOL, 2),
                  pl.BlockSpec((len(POOL_WINDOWS), POOL_GROUP_DIM, POOL_GROUP_DIM), lambda i: (0, 0, 0)),
                  pl.BlockSpec((1, D_POOL), lambda i: (0, 0))],
        out_specs=[row(D_MODEL), row(D_MODEL), row(STAT_LANES), row(D_POOL)],
        out_shape=[jax.ShapeDtypeStruct((seq, D_MODEL), BF16), jax.ShapeDtypeStruct((seq, D_MODEL), F32),
                   jax.ShapeDtypeStruct((seq, STAT_LANES), F32), jax.ShapeDtypeStruct((seq, D_POOL), BF16)],
        scratch_shapes=[pltpu.VMEM((N_HEADS, tm, HEAD_DIM), F32), pltpu.VMEM((N_HEADS, tm, HEAD_DIM), F32),
                        pltpu.VMEM((1, tm, STAT_LANES), F32), pltpu.VMEM((1, tm, STAT_LANES), F32)],
        compiler_params=_params(("parallel",), 48),
    )(o_list[0][0], o_list[1], o_list[2], st_list[0][0], st_list[1], st_list[2],
      hug, hug, hug, hug, w_pool_g, pool_scale)


def _out_proj_loss(y, w_out_g, x, target, gain, bias):
    seq = x.shape[0]
    tm = 512

    def body(y_ref, w_ref, x_ref, t_ref, g_ref, b_ref, dz_ref, dzb_ref, gg_ref, gb_ref, loss_ref):
        @pl.when(pl.program_id(0) == 0)
        def _():
            gg_ref[...] = jnp.zeros_like(gg_ref)
            gb_ref[...] = jnp.zeros_like(gb_ref)
            loss_ref[...] = jnp.zeros_like(loss_ref)

        halves = [slice(0, tm // 2), slice(tm // 2, tm)]
        projected = [_dot_nn(y_ref[rows, :], w_ref[...]) for rows in halves]
        for rows, out in zip(halves, projected):
            z = DEEPNORM_ALPHA * x_ref[rows, :] + out
            mu = jnp.mean(z, axis=-1, keepdims=True)
            zc = z - mu
            rstd = lax.rsqrt(jnp.mean(zc * zc, axis=-1, keepdims=True) + LN_EPS)
            xhat = zc * rstd
            gain_v = g_ref[...]
            diff = xhat * gain_v + b_ref[...] - t_ref[rows, :]
            sq = _fold_rows(diff * diff)
            part = sq[:, :128]
            for k in range(1, D_MODEL // 128):
                part = part + sq[:, k * 128:(k + 1) * 128]
            loss_ref[...] += part
            dln = diff * (1.0 / D_MODEL)
            gg_ref[...] += _fold_rows(dln * xhat)
            gb_ref[...] += _fold_rows(dln)
            dxhat = dln * gain_v
            dz = rstd * (dxhat - jnp.mean(dxhat, axis=-1, keepdims=True)
                         - xhat * jnp.mean(dxhat * xhat, axis=-1, keepdims=True))
            dz_ref[rows, :] = dz
            dzb_ref[rows, :] = dz.astype(BF16)

    row = lambda: pl.BlockSpec((tm, D_MODEL), lambda i: (i, 0))
    vec = lambda: pl.BlockSpec((1, D_MODEL), lambda i: (0, 0))
    acc = lambda width: pl.BlockSpec((8, width), lambda i: (0, 0))
    return _pallas(
        body, name="out_proj_loss", grid=(seq // tm,),
        in_specs=[row(), pl.BlockSpec((D_MODEL, D_MODEL), lambda i: (0, 0), pipeline_mode=pl.Buffered(1)),
                  row(), row(), vec(), vec()],
        out_specs=[row(), row(), acc(D_MODEL), acc(D_MODEL), acc(128)],
        out_shape=[jax.ShapeDtypeStruct((seq, D_MODEL), F32), jax.ShapeDtypeStruct((seq, D_MODEL), BF16),
                   jax.ShapeDtypeStruct((8, D_MODEL), F32), jax.ShapeDtypeStruct((8, D_MODEL), F32),
                   jax.ShapeDtypeStruct((8, 128), F32)],
        compiler_params=_params(("arbitrary",), 56),
    )(y, w_out_g.reshape(D_MODEL, D_MODEL), x, target, gain, bias)


def _dy_gate_bwd(dzb, w_out_g, hug, mixpre, pool_scale, lse_all):
    seq = dzb.shape[0]
    tm = 256
    d4, d16 = DILATIONS[1], DILATIONS[2]

    def body(dz_ref, w_ref, ga_ref, gp_ref, mix_ref, sc_ref, lse_ref,
             dh_ref, dpo_ref, do1_ref, do4_ref, do16_ref, st1_ref, st4_ref, st16_ref, da_ref, st_ref):
        dy = _dot_nt(dz_ref[...], w_ref[...])
        ga = ga_ref[...]
        sig = jax.nn.sigmoid(ga)
        attn = mix_ref[:, :D_ATTN]
        dya = dy[:, :D_ATTN]
        dattn = dya * (ga * sig)
        dh_ref[:, :D_ATTN] = (dya * attn * (sig * (1.0 + ga * (1.0 - sig)))).astype(BF16)
        _store_slabs(da_ref, dattn)
        lane = lax.broadcasted_iota(jnp.int32, (tm, STAT_LANES), 1)
        stats = lse_ref[...]
        prod = dattn * attn
        for h in range(N_HEADS):
            delta = jnp.sum(prod[:, h * HEAD_DIM:(h + 1) * HEAD_DIM], axis=-1, keepdims=True)
            stats = jnp.where(lane == N_HEADS + h, delta, stats)
        st_ref[0] = stats
        do1_ref[...] = dattn.astype(BF16)
        st1_ref[...] = stats
        _to_pattern(da_ref, do4_ref, d4, BF16)
        _to_pattern(da_ref, do16_ref, d16, BF16)
        _to_pattern(st_ref, st4_ref, d4, F32)
        _to_pattern(st_ref, st16_ref, d16, F32)

        gp = gp_ref[...]
        sig = jax.nn.sigmoid(gp)
        dyp = dy[:, D_ATTN:]
        dpo_ref[...] = dyp * (gp * sig)
        dh_ref[:, D_ATTN:] = (dyp * (mix_ref[:, D_ATTN:] * sc_ref[...])
                              * (sig * (1.0 + gp * (1.0 - sig)))).astype(BF16)

    row = lambda width, cb=0: pl.BlockSpec((tm, width), lambda i: (i, cb))
    pat = lambda d, width: pl.BlockSpec((d, tm // d, width), lambda i: (0, i, 0))
    pat_shape = lambda d, width, dtype: jax.ShapeDtypeStruct((d, seq // d, width), dtype)
    outs = _pallas(
        body, name="dy_gate_bwd", grid=(seq // tm,),
        in_specs=[row(D_MODEL), pl.BlockSpec((D_MODEL, D_MODEL), lambda i: (0, 0)),
                  row(D_ATTN, 1), row(D_POOL, 2), row(D_MODEL), pl.BlockSpec((1, D_POOL), lambda i: (0, 0)),
                  row(STAT_LANES)],
        out_specs=[row(D_MODEL, D_IN // D_MODEL - 1), row(D_POOL),
                   row(D_ATTN), pat(d4, D_ATTN), pat(d16, D_ATTN),
                   row(STAT_LANES), pat(d4, STAT_LANES), pat(d16, STAT_LANES)],
        out_shape=[jax.ShapeDtypeStruct((seq, D_IN), BF16), jax.ShapeDtypeStruct((seq, D_POOL), F32),
                   jax.ShapeDtypeStruct((seq, D_ATTN), BF16), pat_shape(d4, D_ATTN, BF16), pat_shape(d16, D_ATTN, BF16),
                   jax.ShapeDtypeStruct((seq, STAT_LANES), F32), pat_shape(d4, STAT_LANES, F32),
                   pat_shape(d16, STAT_LANES, F32)],
        scratch_shapes=[pltpu.VMEM((N_HEADS, tm, HEAD_DIM), F32), pltpu.VMEM((1, tm, STAT_LANES), F32)],
        compiler_params=_params(("parallel",), 48),
    )(dzb, w_out_g.reshape(D_MODEL, D_MODEL), hug, hug, mixpre, pool_scale, lse_all)
    dh, dpo, do1, do4, do16, st1, st4, st16 = outs
    return dh, dpo, [do1[None], do4, do16], [st1[None], st4, st16]


def _pool_bwd(dh, dpo, mixpre, pooled, w_pool_g, pool_scale):
    seq = dpo.shape[0]
    tm = 256
    halo_blocks = tm // POOL_HALO
    last = seq // tm - 1
    n_groups = len(POOL_WINDOWS)

    def body(dh_in_ref, dpo_ref, halo_ref, pre_ref, pooled_ref, wp_ref, sc_ref, du_ref, gw_ref, gs_ref):
        i = pl.program_id(0)

        @pl.when(i == 0)
        def _():
            gw_ref[...] = jnp.zeros_like(gw_ref)
            gs_ref[...] = jnp.zeros_like(gs_ref)

        dpo = dpo_ref[...]
        scale = sc_ref[...]
        gs_ref[...] += _fold_rows(dpo * pre_ref[...])
        halo = jnp.where(i < last, halo_ref[...], 0.0)
        dpw = (jnp.concatenate([dpo, halo], axis=0) * scale).astype(BF16)
        pos = i * tm + lax.broadcasted_iota(jnp.int32, (tm + POOL_HALO, 1), 0)
        for g, window in enumerate(POOL_WINDOWS):
            cols = slice(g * POOL_GROUP_DIM, (g + 1) * POOL_GROUP_DIM)
            dpw_g = dpw[:, cols]
            gw_ref[g] += _dot_tn(pooled_ref[:, cols], dpw_g[:tm, :])
            dpooled = _dot_nt(dpw_g, wp_ref[g])
            count = jnp.minimum(pos + 1, window).astype(F32)
            sums = _window_sums(dpooled / count, window, backward=True)
            du_ref[:, cols] = (sums[:tm, :] - dpooled[:tm, :]).astype(BF16)

    row = lambda width, cb=0: pl.BlockSpec((tm, width), lambda i: (i, cb))
    return _pallas(
        body, name="pool_bwd", grid=(seq // tm,),
        in_specs=[ANY, row(D_POOL),
                  pl.BlockSpec((POOL_HALO, D_POOL),
                               lambda i: (jnp.minimum((i + 1) * halo_blocks, seq // POOL_HALO - 1), 0)),
                  row(D_POOL, 1), row(D_POOL),
                  pl.BlockSpec((n_groups, POOL_GROUP_DIM, POOL_GROUP_DIM), lambda i: (0, 0, 0)),
                  pl.BlockSpec((1, D_POOL), lambda i: (0, 0))],
        out_specs=[row(D_POOL, D_QKV // D_POOL),
                   pl.BlockSpec((n_groups, POOL_GROUP_DIM, POOL_GROUP_DIM), lambda i: (0, 0, 0)),
                   pl.BlockSpec((8, D_POOL), lambda i: (0, 0))],
        out_shape=[jax.ShapeDtypeStruct(dh.shape, dh.dtype),
                   jax.ShapeDtypeStruct((n_groups, POOL_GROUP_DIM, POOL_GROUP_DIM), F32),
                   jax.ShapeDtypeStruct((8, D_POOL), F32)],
        input_output_aliases={0: 0},
        compiler_params=_params(("arbitrary",), 40),
    )(dh, dpo, dpo, mixpre, pooled, w_pool_g, pool_scale)


def _sum_patterns(dh, parts, tabs, unrotate, col_block, name):
    seq = dh.shape[0]
    tm, tn = 256, D_ATTN
    per = D_ATTN // tn
    d4, d16 = DILATIONS[1], DILATIONS[2]

    def body(dh_in_ref, a1_ref, a4_ref, a16_ref, ct_ref, up_ref, down_ref, o_ref, n4_ref, n16_ref):
        _from_pattern(a4_ref, n4_ref, d4)
        _from_pattern(a16_ref, n16_ref, d16)
        for s in range(tn // HEAD_DIM):
            cols = slice(s * HEAD_DIM, (s + 1) * HEAD_DIM)
            tot = a1_ref[:, cols].astype(F32) + n4_ref[s] + n16_ref[s]
            if unrotate:
                tot = _rotate_heads(tot, ct_ref[...], -up_ref[...], -down_ref[...])
            o_ref[:, cols] = tot.astype(BF16)

    tab = pl.BlockSpec((tm, HEAD_DIM), lambda i, j: (i, 0))
    pat = lambda d: pl.BlockSpec((d, tm // d, tn), lambda i, j: (0, i, j))
    return _pallas(
        body, name=name, grid=(seq // tm, per),
        in_specs=[ANY, pl.BlockSpec((tm, tn), lambda i, j: (i, j)), pat(d4), pat(d16), tab, tab, tab],
        out_specs=pl.BlockSpec((tm, tn), lambda i, j: (i, col_block * per + j)),
        out_shape=jax.ShapeDtypeStruct(dh.shape, dh.dtype),
        scratch_shapes=[pltpu.VMEM((tn // HEAD_DIM, tm, HEAD_DIM), F32), pltpu.VMEM((tn // HEAD_DIM, tm, HEAD_DIM), F32)],
        input_output_aliases={0: 0},
        compiler_params=_params(("parallel", "parallel"), 32),
    )(dh, parts[0][0], parts[1], parts[2], *tabs)


def _grad_w_in(x, dh, half, name, comm=None):
    seq = x.shape[0]
    ts, td, te = 1024, D_MODEL // 2, SHARD_IN

    def body(half_ref, x_ref, dh_ref, o_ref):
        k = pl.program_id(1)
        part = _dot_tn(x_ref[...].astype(BF16), dh_ref[...])

        @pl.when(k == 0)
        def _():
            o_ref[...] = part

        @pl.when(k > 0)
        def _():
            o_ref[...] += part

    (g,), exchanged = _call(
        body, name=name, grid=(N_SHARDS, seq // ts),
        in_specs=[pl.BlockSpec((ts, td), lambda e, k, half_ref: (k, half_ref[0])),
                  pl.BlockSpec((ts, te), lambda e, k, half_ref: (k, e))],
        out_specs=[pl.BlockSpec((None, td, te), lambda e, k, half_ref: (e, 0, 0))],
        out_shape=[jax.ShapeDtypeStruct((N_SHARDS, td, te), F32)],
        scratch_shapes=[], semantics=("parallel", "arbitrary"), vmem_mib=48, args=(x, dh), comm=comm,
        prefetch=(half,))
    return g, exchanged


def _grad_w_out(y, dzb):
    seq = y.shape[0]
    ts, te = 512, 1024
    nk = seq // ts

    def body(y_ref, dz_ref, o_ref, acc_ref):
        k = pl.program_id(1)

        @pl.when(k == 0)
        def _():
            acc_ref[...] = jnp.zeros_like(acc_ref)

        acc_ref[...] += _dot_tn(y_ref[...], dz_ref[...])

        @pl.when(k == nk - 1)
        def _():
            o_ref[...] = acc_ref[...]

    return _pallas(
        body, name="grad_w_out", grid=(D_MODEL // te, nk),
        in_specs=[pl.BlockSpec((ts, te), lambda e, k: (k, e)), pl.BlockSpec((ts, D_MODEL), lambda e, k: (k, 0))],
        out_specs=pl.BlockSpec((te, D_MODEL), lambda e, k: (e, 0)),
        out_shape=jax.ShapeDtypeStruct((D_MODEL, D_MODEL), F32),
        scratch_shapes=[pltpu.VMEM((te, D_MODEL), F32)],
        compiler_params=_params(("parallel", "arbitrary"), 48),
    )(y, dzb)


def _grad_x(dh, w_in_g, dz, first=0, tiles=None, prev=None, comm=None):
    seq = dh.shape[0]
    tm, tk = 512, SHARD_IN
    tiles = seq // tm if tiles is None else tiles

    def body(*refs):
        dh_ref, w_ref, dz_ref, o_ref = refs[-4:]
        k = pl.program_id(1)
        part = _dot_nt(dh_ref[...], w_ref[...])

        @pl.when(k == 0)
        def _():
            o_ref[...] = DEEPNORM_ALPHA * dz_ref[...] + part

        @pl.when(k > 0)
        def _():
            o_ref[...] += part

    carried = [] if prev is None else [prev]
    (g_x,), exchanged = _call(
        body, name="grad_x_%d" % first, grid=(tiles, N_SHARDS),
        in_specs=[ANY] * len(carried) + [
            pl.BlockSpec((tm, tk), lambda i, k: (i + first, k)),
            pl.BlockSpec((None, D_MODEL, tk), lambda i, k: (k, 0, 0)),
            pl.BlockSpec((tm, D_MODEL), lambda i, k: (i + first, 0))],
        out_specs=[pl.BlockSpec((tm, D_MODEL), lambda i, k: (i + first, 0))],
        out_shape=[jax.ShapeDtypeStruct((seq, D_MODEL), F32)],
        scratch_shapes=[], semantics=("parallel", "arbitrary"), vmem_mib=48, args=(*carried, dh, w_in_g, dz),
        aliases={0: 0} if carried else None, comm=comm)
    return g_x, exchanged


def _pool_weight(w_pool_sh):
    n_groups = len(POOL_WINDOWS)
    shard_c = POOL_GROUP_DIM // N_SHARDS
    return (w_pool_sh.reshape(N_SHARDS, n_groups, shard_c, POOL_GROUP_DIM).transpose(1, 0, 2, 3)
            .reshape(n_groups, POOL_GROUP_DIM, POOL_GROUP_DIM))


def _pool_grad_pieces(g_w_pool):
    n_groups = len(POOL_WINDOWS)
    half_c = POOL_GROUP_DIM // N_SHARDS // 2
    return (g_w_pool.reshape(n_groups, N_SHARDS, 2, half_c, POOL_GROUP_DIM).transpose(1, 2, 0, 3, 4)
            .reshape(N_SHARDS, 2, n_groups * half_c, POOL_GROUP_DIM))


def _step(x, target, w_in_g, w_rest, pool_scale, gain, bias, place=None):
    seq = x.shape[0]
    tabs = _rope_tables(seq)
    qkv, gathered = _in_proj_qkv(x, w_in_g, tabs, comm=_allgather_weights(w_rest) if place else None)
    w_out_g, w_pool_sh = gathered if place else w_rest
    w_pool_g = _pool_weight(w_pool_sh)
    hug = _in_proj_pool_gate(x, w_in_g)
    o_list, st_list = [], []
    for p, dil in enumerate(DILATIONS):
        o, st = _attn_fwd(qkv[p], "attn_fwd_d%d" % dil)
        o_list.append(o)
        st_list.append(st)
    y, mixpre, lse_all, pooled = _mix_gate(o_list, st_list, hug, w_pool_g, pool_scale)
    dz, dzb, gain_part, bias_part, loss_part = _out_proj_loss(y, w_out_g, x, target, gain, bias)
    dh, dpo, do_list, stat_list = _dy_gate_bwd(dzb, w_out_g, hug, mixpre, pool_scale, lse_all)
    g_w_out = _grad_w_out(y, dzb)
    dh, g_w_pool, scale_part = _pool_bwd(dh, dpo, mixpre, pooled, w_pool_g, pool_scale)
    small = jnp.concatenate([scale_part, gain_part, bias_part, loss_part], axis=1)
    early = [g_w_out.reshape(N_SHARDS, 2, D_MODEL // (2 * N_SHARDS), D_MODEL), _pool_grad_pieces(g_w_pool)]

    bwd = lambda p, comm: _attn_bwd(qkv[p], do_list[p], stat_list[p], "attn_bwd_d%d" % DILATIONS[p], comm)
    if place is None:
        parts = [bwd(p, None)[0] for p in range(3)]
    else:
        core, chip_core = place
        part_a, recv = bwd(0, _exchange_halves(early))
        sums = [_add_own_half(g, r, core, "add_own_half_%d" % a) for a, (g, r) in enumerate(zip(early, recv))]
        part_b, recv = bwd(1, _scatter_to_chips([s[1] for s in sums]))
        bufs = [_add_chips(s[0], r, chip_core, "add_chips_%d" % a) for a, (s, r) in enumerate(zip(sums, recv))]
        part_c, early = bwd(2, _share_with_sibling(bufs))
        parts = [part_a, part_b, part_c]
    dh = _sum_patterns(dh, [t[0] for t in parts], tabs, True, 0, "sum_dq")
    dh = _sum_patterns(dh, [t[1] for t in parts], tabs, True, 1, "sum_dk")
    dh = _sum_patterns(dh, [t[2] for t in parts], tabs, False, 2, "sum_dv")
    if place is None:
        halves = [_grad_w_in(x, dh, jnp.full((1,), h, jnp.int32), "grad_w_in_%d" % h)[0] for h in range(2)]
        g_w_in = jnp.stack(halves, axis=1)
        g_x, _ = _grad_x(dh, w_in_g, dz)
    else:
        give, _ = _grad_w_in(x, dh, 1 - core, "grad_w_in_give")
        keep, recv = _grad_w_in(x, dh, core, "grad_w_in_keep", _send_to_sibling([give]))
        total, total_b = _add_pair(keep, recv[0], "add_own_half_w_in")
        rows = total.shape[1]
        cut = rows // 2
        g_x, recv = _grad_x(dh, w_in_g, dz, 0, 3, None, _scatter_to_chips([total_b], (0, cut)))
        g_x, recv = _grad_x(dh, w_in_g, dz, 3, 3, g_x, _scatter_to_chips([total_b], (cut, rows - cut), recv))
        buf = _add_chips(total, recv[0], chip_core, "add_chips_w_in")
        g_x, _ = _grad_x(dh, w_in_g, dz, 6, seq // 512 - 6, g_x)
        g_w_in = _run_exchange(_share_with_sibling([buf]), "share_w_in")[0]
    return g_x, g_w_in, early[0], early[1], small


def _exchange_halves(grads):
    n = len(grads)

    def copies(src, dst, sems):
        x, y, c, _ = _mesh_place()
        return [_remote(src[a].at[j, 1 - c], dst[a].at[j], sems[0].at[a, j], sems[1].at[a, j], (x, y, 1 - c))
                for a in range(n) for j in range(N_SHARDS)]

    def start(src, dst, sems):
        for cp in copies(src, dst, sems):
            cp.start()

    def finish(src, dst, sems):
        for cp in copies(src, dst, sems):
            cp.wait()

    return _Exchange(grads, [jax.ShapeDtypeStruct((N_SHARDS,) + g.shape[2:], g.dtype) for g in grads], {},
                     [pltpu.SemaphoreType.DMA((n, N_SHARDS))] * 2, start, finish)


def _add_own_half(grad, recv, core, name):
    _, _, r, c = grad.shape
    tr = min(r, 256)

    def body(core_ref, g_ref, r_ref, o_ref, ob_ref):
        tot = g_ref[...] + r_ref[...]
        o_ref[...] = tot
        ob_ref[...] = tot.astype(BF16)

    out = pl.BlockSpec((None, tr, c), lambda j, i, core_ref: (j, i, 0))
    return _pallas(
        body, name=name,
        grid_spec=pltpu.PrefetchScalarGridSpec(
            num_scalar_prefetch=1, grid=(N_SHARDS, r // tr),
            in_specs=[pl.BlockSpec((None, None, tr, c), lambda j, i, core_ref: (j, core_ref[0], i, 0)),
                      pl.BlockSpec((None, tr, c), lambda j, i, core_ref: (j, i, 0))],
            out_specs=[out, out]),
        out_shape=[jax.ShapeDtypeStruct((N_SHARDS, r, c), F32), jax.ShapeDtypeStruct((N_SHARDS, r, c), BF16)],
        compiler_params=_params(("parallel", "parallel"), 32),
    )(core, grad, recv)


def _send_to_sibling(arrays):
    n = len(arrays)

    def copies(src, dst, sems):
        x, y, c, _ = _mesh_place()
        return [_remote(src[a], dst[a], sems[0].at[a], sems[1].at[a], (x, y, 1 - c)) for a in range(n)]

    def start(src, dst, sems):
        for cp in copies(src, dst, sems):
            cp.start()

    def finish(src, dst, sems):
        for cp in copies(src, dst, sems):
            cp.wait()

    return _Exchange(arrays, [jax.ShapeDtypeStruct(t.shape, t.dtype) for t in arrays], {},
                     [pltpu.SemaphoreType.DMA((n,))] * 2, start, finish)


def _add_pair(a, b, name):
    _, r, c = a.shape
    tr = min(r, 256)

    def body(a_ref, b_ref, o_ref, ob_ref):
        tot = a_ref[...] + b_ref[...]
        o_ref[...] = tot
        ob_ref[...] = tot.astype(BF16)

    spec = pl.BlockSpec((None, tr, c), lambda j, i: (j, i, 0))
    return _pallas(
        body, name=name, grid=(N_SHARDS, r // tr), in_specs=[spec, spec], out_specs=[spec, spec],
        out_shape=[jax.ShapeDtypeStruct(a.shape, F32), jax.ShapeDtypeStruct(a.shape, BF16)],
        compiler_params=_params(("parallel", "parallel"), 32),
    )(a, b)


def _scatter_to_chips(sums, rows=None, into=None):
    n = len(sums)

    def copies(src, dst, sems):
        x, y, c, chips = _mesh_place()
        part = (lambda ref: ref) if rows is None else (lambda ref: ref.at[pl.ds(rows[0], rows[1])])
        return [_remote(part(src[a].at[2 * cx + cy]), part(dst[a].at[k]), sems[0].at[a, k], sems[1].at[a, k],
                        (cx, cy, c))
                for a in range(n) for k, (cx, cy) in enumerate(chips)]

    def start(src, dst, sems):
        for cp in copies(src, dst, sems):
            cp.start()

    def finish(src, dst, sems):
        for cp in copies(src, dst, sems):
            cp.wait()

    return _Exchange(sums + (into or []), [jax.ShapeDtypeStruct((3,) + s.shape[1:], s.dtype) for s in sums],
                     {n + a: a for a in range(n)} if into else {},
                     [pltpu.SemaphoreType.DMA((n, 3))] * 2, start, finish)


def _add_chips(sums, recv, chip_core, name):
    _, r, c = sums.shape
    tr = min(r, 256)

    def body(cc_ref, s_ref, r_ref, o_ref):
        o_ref[...] = ((s_ref[...] + r_ref[0].astype(F32)) + r_ref[1].astype(F32)) + r_ref[2].astype(F32)

    return _pallas(
        body, name=name,
        grid_spec=pltpu.PrefetchScalarGridSpec(
            num_scalar_prefetch=1, grid=(r // tr,),
            in_specs=[pl.BlockSpec((None, tr, c), lambda i, cc_ref: (cc_ref[0], i, 0)),
                      pl.BlockSpec((3, tr, c), lambda i, cc_ref: (0, i, 0))],
            out_specs=pl.BlockSpec((None, tr, c), lambda i, cc_ref: (cc_ref[1], i, 0))),
        out_shape=jax.ShapeDtypeStruct((2, r, c), F32),
        compiler_params=_params(("parallel",), 32),
    )(chip_core, sums, recv)


def _share_with_sibling(bufs):
    n = len(bufs)

    def copies(dst, sems, half):
        x, y, c, _ = _mesh_place()
        h = c if half == "mine" else 1 - c
        return [_remote(dst[a].at[h], dst[a].at[h], sems[0].at[a], sems[1].at[a], (x, y, 1 - c)) for a in range(n)]

    def start(ins, dst, sems):
        for cp in copies(dst, sems, "mine"):
            cp.start()

    def finish(ins, dst, sems):
        for cp in copies(dst, sems, "theirs"):
            cp.wait_recv()
        for cp in copies(dst, sems, "mine"):
            cp.wait_send()

    return _Exchange(bufs, [jax.ShapeDtypeStruct(b.shape, b.dtype) for b in bufs], {a: a for a in range(n)},
                     [pltpu.SemaphoreType.DMA((n,))] * 2, start, finish)


def _adam_math(w, g, m, v):
    m = ADAM_B1 * m + (1.0 - ADAM_B1) * g
    v = ADAM_B2 * v + (1.0 - ADAM_B2) * (g * g)
    m_hat = m / (1.0 - ADAM_B1 ** ADAM_STEP)
    v_hat = v / (1.0 - ADAM_B2 ** ADAM_STEP)
    delta = -ADAM_LR * (m_hat / (jnp.sqrt(v_hat) + ADAM_EPS) + ADAM_WD * w)
    return delta, m, v


def _small_allreduce_adamw(small, w_vec, m_vec, v_vec):
    width = small.shape[1]
    n_par = w_vec.shape[1]

    def body(s_ref, w_ref, m_ref, v_ref, loss_ref, g_ref, d_ref, nm_ref, nv_ref, gather_ref, send_sems, recv_sems):
        x, y, c = lax.axis_index("x"), lax.axis_index("y"), lax.axis_index("c")
        me = 4 * x + 2 * y + c
        gather_ref[me] = s_ref[...]
        copies = []
        for r in range(1, 8):
            bx, by, bc = (r >> 2) & 1, (r >> 1) & 1, r & 1
            peer = (x ^ bx, y ^ by, c ^ bc)
            cp = pltpu.make_async_remote_copy(
                src_ref=s_ref, dst_ref=gather_ref.at[me], send_sem=send_sems.at[r - 1], recv_sem=recv_sems.at[r - 1],
                device_id=peer, device_id_type=MESH)
            cp.start()
            copies.append(cp)
        for r in range(1, 8):
            bx, by, bc = (r >> 2) & 1, (r >> 1) & 1, r & 1
            theirs = gather_ref.at[4 * (x ^ bx) + 2 * (y ^ by) + (c ^ bc)]
            pltpu.make_async_remote_copy(
                src_ref=theirs, dst_ref=theirs, send_sem=send_sems.at[r - 1], recv_sem=recv_sems.at[r - 1],
                device_id=(x ^ bx, y ^ by, c ^ bc), device_id_type=MESH).wait_recv()
        for cp in copies:
            cp.wait_send()
        tot = gather_ref[0]
        for d in range(1, 8):
            tot = tot + gather_ref[d]
        tot = jnp.sum(tot, axis=0, keepdims=True)
        sq = jnp.sum(tot[:, n_par:], axis=1, keepdims=True)
        loss_ref[...] = jnp.broadcast_to(sq * (0.5 / D_MODEL), loss_ref.shape)
        g = tot[:, :n_par]
        g_ref[...] = g
        d_ref[...], nm_ref[...], nv_ref[...] = _adam_math(w_ref[...], g, m_ref[...], v_ref[...])

    vm = pl.BlockSpec(memory_space=pltpu.VMEM)
    vec = jax.ShapeDtypeStruct((1, n_par), F32)
    return _pallas(
        body, name="small_allreduce_adamw",
        in_specs=[vm] * 4, out_specs=[vm] * 5,
        out_shape=[jax.ShapeDtypeStruct((1, 128), F32), vec, vec, vec, vec],
        scratch_shapes=[pltpu.VMEM((8, 8, width), F32), pltpu.SemaphoreType.DMA((7,)), pltpu.SemaphoreType.DMA((7,))],
    )(small, w_vec, m_vec, v_vec)


def _adamw(w, g, m, v, name):
    r, c = w.shape
    tr = min(r, 256)

    def body(w_ref, g_ref, m_ref, v_ref, d_ref, nm_ref, nv_ref):
        d_ref[...], nm_ref[...], nv_ref[...] = _adam_math(w_ref[...], g_ref[...], m_ref[...], v_ref[...])

    spec = pl.BlockSpec((tr, c), lambda i: (i, 0))
    shape = jax.ShapeDtypeStruct((r, c), F32)
    return _pallas(
        body, name=name, grid=(r // tr,),
        in_specs=[spec] * 4, out_specs=[spec] * 3, out_shape=[shape] * 3,
        compiler_params=_params(("parallel",), 48),
    )(w, g, m, v)


def kernel(x, w_in, w_pool, pool_scale, w_out, ln_gain, ln_bias, loss_target, m_w_in, m_w_pool, m_pool_scale, m_w_out, m_ln_gain, m_ln_bias, v_w_in, v_w_pool, v_pool_scale, v_w_out, v_ln_gain, v_ln_bias):
    xi, yi, ci = lax.axis_index("x"), lax.axis_index("y"), lax.axis_index("c")
    chip = (2 * xi + yi).astype(jnp.int32).reshape(1)
    core = ci.astype(jnp.int32).reshape(1)
    n_groups = len(POOL_WINDOWS)
    shard_c = w_pool.shape[2]

    w_in_b = _cast_bf16(w_in[0], chip, "cast_w_in", 256)
    w_out_b = _cast_bf16(w_out[0], chip, "cast_w_out", 256)
    w_pool_b = _cast_bf16(w_pool[0].reshape(n_groups * shard_c, POOL_GROUP_DIM), chip, "cast_w_pool", 256)
    w_in_g = _run_exchange(_allgather_weights([w_in_b]), "allgather_w_in")[0]

    chip_core = jnp.concatenate([chip, core])
    g_x, full_in, full_out, full_pool, small = _step(
        x[0], loss_target[0], w_in_g, [w_out_b, w_pool_b], pool_scale, ln_gain, ln_bias, (core, chip_core))
    half_c = shard_c // 2
    grad_w_in = full_in.reshape(D_MODEL, SHARD_IN)
    grad_w_out = full_out.reshape(D_MODEL // N_SHARDS, D_MODEL)
    grad_w_pool = (full_pool.reshape(2, n_groups, half_c, POOL_GROUP_DIM).transpose(1, 0, 2, 3)
                   .reshape(n_groups * shard_c, POOL_GROUP_DIM))

    d_in, nm_in, nv_in = _adamw(w_in[0], grad_w_in, m_w_in[0], v_w_in[0], "adamw_w_in")
    d_out, nm_out, nv_out = _adamw(w_out[0], grad_w_out, m_w_out[0], v_w_out[0], "adamw_w_out")
    flat = lambda t: t[0].reshape(n_groups * shard_c, POOL_GROUP_DIM)
    d_pool, nm_pool, nv_pool = _adamw(flat(w_pool), grad_w_pool, flat(m_w_pool), flat(v_w_pool), "adamw_w_pool")

    cat = lambda a, b, c: jnp.concatenate([a, b, c], axis=1)
    loss_v, g_vec, d_vec, nm_vec, nv_vec = _small_allreduce_adamw(
        small, cat(pool_scale, ln_gain, ln_bias), cat(m_pool_scale, m_ln_gain, m_ln_bias),
        cat(v_pool_scale, v_ln_gain, v_ln_bias))

    def split(vec):
        return vec[:, :D_POOL], vec[:, D_POOL:D_POOL + D_MODEL], vec[:, D_POOL + D_MODEL:]

    g_scale, g_gain, g_bias = split(g_vec)
    d_scale, d_gain, d_bias = split(d_vec)
    nm_scale, nm_gain, nm_bias = split(nm_vec)
    nv_scale, nv_gain, nv_bias = split(nv_vec)
    pool_shape = w_pool.shape
    return (loss_v[0, 0], g_x[None],
            grad_w_in[None], grad_w_pool.reshape(pool_shape), g_scale, grad_w_out[None], g_gain, g_bias,
            d_in[None], d_pool.reshape(pool_shape), d_scale, d_out[None], d_gain, d_bias,
            nm_in[None], nm_pool.reshape(pool_shape), nm_scale, nm_out[None], nm_gain, nm_bias,
            nv_in[None], nv_pool.reshape(pool_shape), nv_scale, nv_out[None], nv_gain, nv_bias)
```

```python
import functools

import jax
import jax.numpy as jnp
from jax import lax
from jax.experimental import pallas as pl
from jax.experimental.pallas import tpu as pltpu

F32 = jnp.float32
BF16 = jnp.bfloat16
MESH = pl.DeviceIdType.MESH
ANY = pl.BlockSpec(memory_space=pl.ANY)

D_MODEL = 2048
D_ATTN = 1024
D_POOL = 1024
HEAD_DIM = 128
N_HEADS = 8
ROPE_DIM = 32
ROPE_THETA = 500000.0
DILATIONS = (1, 4, 16)
KEY_BLOCK = 128
CHUNK = 2 * KEY_BLOCK
STAT_LANES = 128
POOL_WINDOWS = (2, 4, 8, 16)
POOL_GROUP_DIM = 256
POOL_HALO = 16
D_QKV = 3 * D_ATTN
D_UG = D_POOL + D_MODEL
D_IN = D_QKV + D_UG
N_SHARDS = 4
SHARD_IN = D_IN // N_SHARDS
LN_EPS = 1e-5
DEEPNORM_ALPHA = 2.0 ** 0.25
ADAM_LR = 0.001
ADAM_B1 = 0.9
ADAM_B2 = 0.999
ADAM_EPS = 1e-08
ADAM_WD = 0.01
ADAM_STEP = 10
NEG = -1e30
MIB = 1024 * 1024


def _params(sem, vmem_mib):
    return pltpu.CompilerParams(dimension_semantics=sem, vmem_limit_bytes=vmem_mib * MIB)


def _pallas(body, **kwargs):
    pin = lambda s: pltpu.HBM(s.shape, s.dtype) if len(s.shape) >= 2 else s
    out_shape = kwargs.pop("out_shape")
    out_shape = [pin(s) for s in out_shape] if isinstance(out_shape, (list, tuple)) else pin(out_shape)
    call = pl.pallas_call(body, out_shape=out_shape, **kwargs)

    def run(*operands):
        return call(*[pltpu.with_memory_space_constraint(o, pltpu.HBM) if o.ndim >= 2 else o for o in operands])

    return run


class _Exchange:
    def __init__(self, operands, out_shape, aliases, sems, start, finish):
        self.operands, self.out_shape, self.aliases, self.sems = list(operands), list(out_shape), dict(aliases), list(sems)
        self.start, self.finish = start, finish


def _run_exchange(comm, name):
    n_in, n_out = len(comm.operands), len(comm.out_shape)

    def body(*refs):
        ins, outs, sems = refs[:n_in], refs[n_in:n_in + n_out], refs[n_in + n_out:]
        comm.start(ins, outs, sems)
        comm.finish(ins, outs, sems)

    return _pallas(
        body, name=name, in_specs=[ANY] * n_in, out_specs=[ANY] * n_out, out_shape=comm.out_shape,
        input_output_aliases=comm.aliases, scratch_shapes=comm.sems,
    )(*comm.operands)


def _call(body, *, name, grid, in_specs, out_specs, out_shape, scratch_shapes, semantics, vmem_mib, args,
          aliases=None, comm=None, prefetch=()):
    aliases = dict(aliases or {})
    n_pre, n_in, n_out, n_scr = len(prefetch), len(in_specs), len(out_specs), len(scratch_shapes)
    c_in, c_out = (len(comm.operands), len(comm.out_shape)) if comm else (0, 0)
    c_shapes, c_sems, c_operands = (comm.out_shape, comm.sems, comm.operands) if comm else ([], [], [])

    def hosted(*refs):
        pre, refs = refs[:n_pre], refs[n_pre:]
        a = n_in
        b = a + c_in
        c = b + n_out
        d = c + c_out
        e = d + n_scr
        if comm is None:
            body(*pre, *refs)
            return
        ids = [pl.program_id(k) for k in range(len(grid))]
        first = functools.reduce(jnp.logical_and, [i == 0 for i in ids])
        last = functools.reduce(jnp.logical_and, [i == g - 1 for i, g in zip(ids, grid)])

        @pl.when(first)
        def _():
            comm.start(refs[a:b], refs[c:d], refs[e:])

        body(*pre, *refs[:a], *refs[b:c], *refs[d:e])

        @pl.when(last)
        def _():
            comm.finish(refs[a:b], refs[c:d], refs[e:])

    if comm:
        semantics = ("arbitrary",) * len(grid)
        for i, o in comm.aliases.items():
            aliases[n_pre + n_in + i] = n_out + o
    outs = _pallas(
        hosted, name=name,
        grid_spec=pltpu.PrefetchScalarGridSpec(
            num_scalar_prefetch=n_pre, grid=grid, in_specs=list(in_specs) + [ANY] * c_in,
            out_specs=list(out_specs) + [ANY] * c_out, scratch_shapes=list(scratch_shapes) + c_sems),
        out_shape=list(out_shape) + c_shapes, input_output_aliases=aliases,
        compiler_params=_params(semantics, vmem_mib),
    )(*prefetch, *args, *c_operands)
    return list(outs[:n_out]), list(outs[n_out:])


def _dot_nn(a, b):
    return jnp.dot(a, b, preferred_element_type=F32)


def _dot_nt(a, b):
    return lax.dot_general(a, b, (((1,), (1,)), ((), ())), preferred_element_type=F32)


def _dot_tn(a, b):
    return lax.dot_general(a, b, (((0,), (0,)), ((), ())), preferred_element_type=F32)


def _fold_rows(a):
    r, c = a.shape
    return jnp.sum(a.reshape(r // 8, 8, c), axis=0)


def _cast_bf16(a, chip, name, rows):
    r, c = a.shape

    def body(chip_ref, a_ref, o_ref):
        o_ref[...] = a_ref[...].astype(BF16)

    return _pallas(
        body, name=name,
        grid_spec=pltpu.PrefetchScalarGridSpec(
            num_scalar_prefetch=1, grid=(r // rows,),
            in_specs=[pl.BlockSpec((rows, c), lambda i, chip_ref: (i, 0))],
            out_specs=pl.BlockSpec((None, rows, c), lambda i, chip_ref: (chip_ref[0], i, 0))),
        out_shape=jax.ShapeDtypeStruct((N_SHARDS, r, c), BF16),
        compiler_params=_params(("parallel",), 32),
    )(chip, a)


def _mesh_place():
    x, y, c = lax.axis_index("x"), lax.axis_index("y"), lax.axis_index("c")
    return x, y, c, [(1 - x, y), (x, 1 - y), (1 - x, 1 - y)]


def _remote(src, dst, send_sem, recv_sem, to):
    return pltpu.make_async_remote_copy(src_ref=src, dst_ref=dst, send_sem=send_sem, recv_sem=recv_sem,
                                        device_id=to, device_id_type=MESH)


def _allgather_weights(bufs):
    n = len(bufs)

    def half(a, core):
        rows = bufs[a].shape[1] // 2
        return pl.ds(core * rows, rows)

    DIAGONAL = 2

    def to_neighbours(dst, sems):
        x, y, c, chips = _mesh_place()
        own = lambda a: dst[a].at[2 * x + y, half(a, c)]
        return [_remote(own(a), own(a), sems[0].at[a, k], sems[1].at[a, k], (cx, cy, c))
                for a in range(n) for k, (cx, cy) in enumerate(chips[:DIAGONAL])]

    def relayed(dst, sems):
        x, y, c, _ = _mesh_place()
        owner = 2 * (x ^ (1 - c)) + (y ^ c)
        piece = lambda a: dst[a].at[owner, half(a, c)]
        return [_remote(piece(a), piece(a), sems[0].at[a, DIAGONAL], sems[1].at[a, DIAGONAL], (x ^ c, y ^ (1 - c), c))
                for a in range(n)]

    def start(ins, dst, sems):
        for cp in to_neighbours(dst, sems):
            cp.start()

    def finish(ins, dst, sems):
        x, y, c, chips = _mesh_place()
        sibling = (x, y, 1 - c)
        passed_on = []

        def landed_then_pass_on(k):
            cx, cy = chips[k]
            for a in range(n):
                landed = dst[a].at[2 * cx + cy, half(a, c)]
                _remote(landed, landed, sems[0].at[a, k], sems[1].at[a, k], (cx, cy, c)).wait_recv()
                cp = _remote(landed, landed, sems[2].at[a, k], sems[3].at[a, k], sibling)
                cp.start()
                passed_on.append(cp)

        for k in range(DIAGONAL):
            landed_then_pass_on(k)
        for cp in relayed(dst, sems):
            cp.start()
        landed_then_pass_on(DIAGONAL)
        for k, (cx, cy) in enumerate(chips):
            for a in range(n):
                passed = dst[a].at[2 * cx + cy, half(a, 1 - c)]
                _remote(passed, passed, sems[2].at[a, k], sems[3].at[a, k], sibling).wait_recv()
        for cp in to_neighbours(dst, sems) + relayed(dst, sems) + passed_on:
            cp.wait_send()

    return _Exchange(bufs, [jax.ShapeDtypeStruct(b.shape, b.dtype) for b in bufs], {a: a for a in range(n)},
                     [pltpu.SemaphoreType.DMA((n, 3))] * 4, start, finish)


def _rope_tables(seq):
    half = ROPE_DIM // 2
    inv_freq = ROPE_THETA ** (-(2.0 * jnp.arange(half, dtype=F32)) / ROPE_DIM)
    ang = jnp.arange(seq, dtype=jnp.int32).astype(F32)[:, None] * inv_freq[None, :]
    cos, sin = jnp.cos(ang), jnp.sin(ang)
    pad = jnp.zeros((seq, HEAD_DIM - ROPE_DIM), F32)
    zeros = jnp.zeros((seq, half), F32)
    c_tab = jnp.concatenate([cos, cos, pad + 1.0], axis=1)
    up_tab = jnp.concatenate([-sin, zeros, pad], axis=1)
    down_tab = jnp.concatenate([zeros, sin, pad], axis=1)
    return c_tab, up_tab, down_tab


def _rotate_heads(t, c_tab, up_tab, down_tab):
    outs = []
    for h in range(t.shape[1] // HEAD_DIM):
        th = t[:, h * HEAD_DIM:(h + 1) * HEAD_DIM]
        up = pltpu.roll(th, HEAD_DIM - ROPE_DIM // 2, axis=1)
        down = pltpu.roll(th, ROPE_DIM // 2, axis=1)
        outs.append(th * c_tab + up * up_tab + down * down_tab)
    return outs[0] if len(outs) == 1 else jnp.concatenate(outs, axis=1)


def _to_pattern(slabs_ref, dst_ref, dil, dtype):
    n_slabs, rows, _ = slabs_ref.shape
    for s in range(n_slabs):
        for r in range(dil):
            dst_ref[r, :, s * 128:(s + 1) * 128] = slabs_ref[s, pl.ds(r, rows // dil, dil), :].astype(dtype)


def _from_pattern(src_ref, slabs_ref, dil):
    n_slabs, rows, _ = slabs_ref.shape
    for s in range(n_slabs):
        for r in range(dil):
            slabs_ref[s, pl.ds(r, rows // dil, dil), :] = src_ref[r, :, s * 128:(s + 1) * 128].astype(F32)


def _store_slabs(slabs_ref, value):
    for s in range(slabs_ref.shape[0]):
        slabs_ref[s] = value[:, s * 128:(s + 1) * 128]


def _in_proj_qkv(x, w_in_g, tabs, comm=None):
    seq = x.shape[0]
    tm, tn = 512, SHARD_IN
    heads = tn // HEAD_DIM
    k_heads_in_second = 2 * D_ATTN // HEAD_DIM - heads
    d4, d16 = DILATIONS[1], DILATIONS[2]

    def body(x_ref, w_ref, c_ref, up_ref, down_ref, o1_ref, o4_ref, o16_ref, res_ref):
        shard = pl.program_id(0)
        xb = x_ref[...].astype(BF16)
        group = 4 * HEAD_DIM
        accs = [_dot_nn(xb, w_ref[:, g * group:(g + 1) * group]) for g in range(tn // group)]

        plain = shard == 1
        c_plain = jnp.where(plain, 1.0, c_ref[...])
        up_plain = jnp.where(plain, 0.0, up_ref[...])
        down_plain = jnp.where(plain, 0.0, down_ref[...])
        for h in range(heads):
            lanes = (h * HEAD_DIM) % group
            th = accs[h * HEAD_DIM // group][:, lanes:lanes + HEAD_DIM]
            if h < k_heads_in_second:
                th = _rotate_heads(th, c_ref[...], up_ref[...], down_ref[...])
            else:
                th = _rotate_heads(th, c_plain, up_plain, down_plain)
            res_ref[h] = th
            o1_ref[:, h * HEAD_DIM:(h + 1) * HEAD_DIM] = th.astype(BF16)
        _to_pattern(res_ref, o4_ref, d4, BF16)
        _to_pattern(res_ref, o16_ref, d16, BF16)

    tab_spec = pl.BlockSpec((tm, HEAD_DIM), lambda s, i: (i, 0))
    (o1, o4, o16), exchanged = _call(
        body, name="in_proj_qkv", grid=(D_QKV // tn, seq // tm),
        in_specs=[pl.BlockSpec((tm, D_MODEL), lambda s, i: (i, 0)),
                  pl.BlockSpec((None, D_MODEL, tn), lambda s, i: (s, 0, 0)),
                  tab_spec, tab_spec, tab_spec],
        out_specs=[pl.BlockSpec((tm, tn), lambda s, i: (i, s)),
                   pl.BlockSpec((d4, tm // d4, tn), lambda s, i: (0, i, s)),
                   pl.BlockSpec((d16, tm // d16, tn), lambda s, i: (0, i, s))],
        out_shape=[jax.ShapeDtypeStruct((seq, D_QKV), BF16),
                   jax.ShapeDtypeStruct((d4, seq // d4, D_QKV), BF16),
                   jax.ShapeDtypeStruct((d16, seq // d16, D_QKV), BF16)],
        scratch_shapes=[pltpu.VMEM((heads, tm, HEAD_DIM), F32)],
        semantics=("parallel", "parallel"), vmem_mib=52, args=(x, w_in_g, *tabs), comm=comm)
    return [o1[None], o4, o16], exchanged


def _in_proj_pool_gate(x, w_in_g):
    seq = x.shape[0]
    tm, tn = 512, SHARD_IN
    first_shard = D_QKV // tn

    def body(x_ref, w_ref, o_ref):
        o_ref[...] = _dot_nn(x_ref[...].astype(BF16), w_ref[...])

    return _pallas(
        body, name="in_proj_pool_gate", grid=(D_UG // tn, seq // tm),
        in_specs=[pl.BlockSpec((tm, D_MODEL), lambda s, i: (i, 0)),
                  pl.BlockSpec((None, D_MODEL, tn), lambda s, i: (s + first_shard, 0, 0))],
        out_specs=pl.BlockSpec((tm, tn), lambda s, i: (i, s)),
        out_shape=jax.ShapeDtypeStruct((seq, D_UG), F32),
        compiler_params=_params(("parallel", "parallel"), 48),
    )(x, w_in_g)


def _band_masks():
    row = lax.broadcasted_iota(jnp.int32, (KEY_BLOCK, KEY_BLOCK), 0)
    col = lax.broadcasted_iota(jnp.int32, (KEY_BLOCK, KEY_BLOCK), 1)
    return col <= row, col >= row


def _attn_fwd(qkv, name):
    dil, n, _ = qkv.shape
    scale = HEAD_DIM ** -0.5
    lo, hi = slice(0, KEY_BLOCK), slice(KEY_BLOCK, CHUNK)

    def body(q_ref, k_ref, v_ref, kb_ref, vb_ref, o_ref, st_ref):
        i = pl.program_id(1)
        cur_mask, prev_mask = _band_masks()
        before_mask = jnp.logical_and(prev_mask, i > 0)
        lane = lax.broadcasted_iota(jnp.int32, (KEY_BLOCK, STAT_LANES), 1)
        tasks = [(rows, h) for rows in (lo, hi) for h in range(N_HEADS)]
        head = lambda h: slice(h * HEAD_DIM, (h + 1) * HEAD_DIM)

        def prev_of(rows, h):
            if rows is lo:
                return kb_ref[:, head(h)], vb_ref[:, head(h)], before_mask
            return k_ref[lo, head(h)], v_ref[lo, head(h)], prev_mask

        scores = []
        for rows, h in tasks:
            q = q_ref[rows, head(h)]
            scores.append((_dot_nt(q, prev_of(rows, h)[0]), _dot_nt(q, k_ref[rows, head(h)])))
        probs = []
        for (rows, h), (qk_prev, qk_cur) in zip(tasks, scores):
            s_prev = jnp.where(prev_of(rows, h)[2], qk_prev * scale, NEG)
            s_cur = jnp.where(cur_mask, qk_cur * scale, NEG)
            m = jnp.max(jnp.maximum(s_prev, s_cur), axis=-1, keepdims=True)
            p_prev = jnp.exp(s_prev - m)
            p_cur = jnp.exp(s_cur - m)
            den = jnp.sum(p_prev + p_cur, axis=-1, keepdims=True)
            probs.append((p_prev.astype(BF16), p_cur.astype(BF16), den, m + jnp.log(den)))
        stats = [jnp.zeros((KEY_BLOCK, STAT_LANES), F32), jnp.zeros((KEY_BLOCK, STAT_LANES), F32)]
        for (rows, h), (p_prev, p_cur, den, lse) in zip(tasks, probs):
            o = _dot_nn(p_cur, v_ref[rows, head(h)]) + _dot_nn(p_prev, prev_of(rows, h)[1])
            o_ref[rows, head(h)] = o / den
            b = 0 if rows is lo else 1
            stats[b] = jnp.where(lane == h, lse, stats[b])
        st_ref[lo, :] = stats[0]
        st_ref[hi, :] = stats[1]

    main = lambda cb: pl.BlockSpec((None, CHUNK, D_ATTN), lambda r, i: (r, i, cb))
    before = lambda cb: pl.BlockSpec((None, KEY_BLOCK, D_ATTN), lambda r, i: (r, jnp.maximum(2 * i - 1, 0), cb))
    return _pallas(
        body, name=name, grid=(dil, n // CHUNK),
        in_specs=[main(0), main(1), main(2), before(1), before(2)],
        out_specs=[main(0), pl.BlockSpec((None, CHUNK, STAT_LANES), lambda r, i: (r, i, 0))],
        out_shape=[jax.ShapeDtypeStruct((dil, n, D_ATTN), F32), jax.ShapeDtypeStruct((dil, n, STAT_LANES), F32)],
        compiler_params=_params(("parallel", "parallel"), 40),
    )(qkv, qkv, qkv, qkv, qkv)


def _attn_bwd(qkv, do, stats, name, comm=None):
    dil, n, _ = qkv.shape
    n_blocks = n // KEY_BLOCK
    last = n // CHUNK - 1
    scale = HEAD_DIM ** -0.5
    lo, hi = slice(0, KEY_BLOCK), slice(KEY_BLOCK, CHUNK)

    def body(q_ref, k_ref, v_ref, kb_ref, vb_ref, qa_ref, do_ref, doa_ref, st_ref, sta_ref, dq_ref, dk_ref, dv_ref):
        i = pl.program_id(1)
        cur_mask, prev_mask = _band_masks()
        before_mask = jnp.logical_and(prev_mask, i > 0)
        after_mask = jnp.logical_and(prev_mask, i < last)

        def operands(h):
            cols = slice(h * HEAD_DIM, (h + 1) * HEAD_DIM)
            lse_c, del_c = slice(h, h + 1), slice(N_HEADS + h, N_HEADS + h + 1)
            q = {"0": q_ref[lo, cols], "1": q_ref[hi, cols], "a": qa_ref[:, cols]}
            k = {"0": k_ref[lo, cols], "1": k_ref[hi, cols], "b": kb_ref[:, cols]}
            v = {"0": v_ref[lo, cols], "1": v_ref[hi, cols], "b": vb_ref[:, cols]}
            do = {"0": do_ref[lo, cols], "1": do_ref[hi, cols], "a": doa_ref[:, cols]}
            st = {"0": (st_ref[lo, lse_c], st_ref[lo, del_c]), "1": (st_ref[hi, lse_c], st_ref[hi, del_c]),
                  "a": (sta_ref[:, lse_c], sta_ref[:, del_c])}
            return cols, q, k, v, do, st

        pairs = [("0", "b", before_mask), ("0", "0", cur_mask), ("1", "0", prev_mask), ("1", "1", cur_mask),
                 ("a", "1", after_mask)]
        group = N_HEADS // 2
        for first_head in range(0, N_HEADS, group):
            heads = range(first_head, first_head + group)
            raw = {}
            for h in heads:
                _, q, k, v, do, _ = operands(h)
                for qi, ki, _ in pairs:
                    raw[h, qi, ki] = (_dot_nt(q[qi], k[ki]), _dot_nt(do[qi], v[ki]))
            grads = {}
            for h in heads:
                st = operands(h)[5]
                for qi, ki, mask in pairs:
                    qk, dp = raw[h, qi, ki]
                    lse, delta = st[qi]
                    p = jnp.exp(jnp.where(mask, qk * scale, NEG) - lse)
                    grads[h, qi, ki] = (p.astype(BF16), (p * (dp - delta) * scale).astype(BF16))
            for h in heads:
                cols, q, k, v, do, _ = operands(h)
                p = lambda qi, ki: grads[h, qi, ki][0]
                ds = lambda qi, ki: grads[h, qi, ki][1]
                def put(ref, rows, val, cols=cols):
                    ref[rows, cols] = val.astype(ref.dtype)

                put(dq_ref, lo, _dot_nn(ds("0", "b"), k["b"]) + _dot_nn(ds("0", "0"), k["0"]))
                put(dq_ref, hi, _dot_nn(ds("1", "0"), k["0"]) + _dot_nn(ds("1", "1"), k["1"]))
                put(dk_ref, lo, _dot_tn(ds("0", "0"), q["0"]) + _dot_tn(ds("1", "0"), q["1"]))
                put(dk_ref, hi, _dot_tn(ds("1", "1"), q["1"]) + _dot_tn(ds("a", "1"), q["a"]))
                put(dv_ref, lo, _dot_tn(p("0", "0"), do["0"]) + _dot_tn(p("1", "0"), do["1"]))
                put(dv_ref, hi, _dot_tn(p("1", "1"), do["1"]) + _dot_tn(p("a", "1"), do["a"]))

    def spec(rows, width, row_of, cb):
        return pl.BlockSpec((None, rows, width), lambda r, i: (r, row_of(i), cb))

    same = lambda i: i
    before = lambda i: jnp.maximum(2 * i - 1, 0)
    after = lambda i: jnp.minimum(2 * i + 2, n_blocks - 1)
    out = spec(CHUNK, D_ATTN, same, 0)
    return _call(
        body, name=name, grid=(dil, n // CHUNK),
        in_specs=[spec(CHUNK, D_ATTN, same, 0), spec(CHUNK, D_ATTN, same, 1), spec(CHUNK, D_ATTN, same, 2),
                  spec(KEY_BLOCK, D_ATTN, before, 1), spec(KEY_BLOCK, D_ATTN, before, 2),
                  spec(KEY_BLOCK, D_ATTN, after, 0),
                  spec(CHUNK, D_ATTN, same, 0), spec(KEY_BLOCK, D_ATTN, after, 0),
                  spec(CHUNK, STAT_LANES, same, 0), spec(KEY_BLOCK, STAT_LANES, after, 0)],
        out_specs=[out, out, out],
        out_shape=[jax.ShapeDtypeStruct((dil, n, D_ATTN), BF16)] * 3,
        scratch_shapes=[], semantics=("parallel", "parallel"), vmem_mib=40,
        args=(qkv, qkv, qkv, qkv, qkv, qkv, do, do, stats, stats), comm=comm)


def _window_sums(ext, window, backward):
    rows = ext.shape[0]
    acc, span = ext, 1
    while span < window:
        acc = acc + pltpu.roll(acc, (rows - span) if backward else span, axis=0)
        span *= 2
    return acc


def _mix_gate(o_list, st_list, hug, w_pool_g, pool_scale):
    seq = hug.shape[0]
    tm = 256
    halo_blocks = tm // POOL_HALO
    d4, d16 = DILATIONS[1], DILATIONS[2]

    def body(o1_ref, o4_ref, o16_ref, l1_ref, l4_ref, l16_ref, u_ref, halo_ref, ga_ref, gp_ref, wp_ref, sc_ref,
             y_ref, mix_ref, lse_ref, pooled_ref, n4_ref, n16_ref, nl4_ref, nl16_ref):
        i = pl.program_id(0)
        _from_pattern(o4_ref, n4_ref, d4)
        _from_pattern(o16_ref, n16_ref, d16)
        _from_pattern(l4_ref, nl4_ref, d4)
        _from_pattern(l16_ref, nl16_ref, d16)
        la, lb, lc = l1_ref[...], nl4_ref[0], nl16_ref[0]
        mx = jnp.maximum(jnp.maximum(la, lb), lc)
        ea, eb, ec = jnp.exp(la - mx), jnp.exp(lb - mx), jnp.exp(lc - mx)
        tot = ea + eb + ec
        lse_ref[...] = mx + jnp.log(tot)
        wa, wb, wc = ea / tot, eb / tot, ec / tot
        ga = ga_ref[...]
        silu_a = ga * jax.nn.sigmoid(ga)
        for h in range(N_HEADS):
            cols = slice(h * HEAD_DIM, (h + 1) * HEAD_DIM)
            hc = slice(h, h + 1)
            attn = wa[:, hc] * o1_ref[:, cols] + wb[:, hc] * n4_ref[h] + wc[:, hc] * n16_ref[h]
            mix_ref[:, cols] = attn
            y_ref[:, cols] = (attn * silu_a[:, cols]).astype(BF16)

        u = u_ref[...]
        halo = jnp.where(i > 0, halo_ref[...], 0.0)
        ext = jnp.concatenate([halo, u], axis=0)
        pos = i * tm + lax.broadcasted_iota(jnp.int32, (tm, 1), 0)
        gp = gp_ref[...]
        gated_scale = sc_ref[...] * (gp * jax.nn.sigmoid(gp))
        for g, window in enumerate(POOL_WINDOWS):
            cols = slice(g * POOL_GROUP_DIM, (g + 1) * POOL_GROUP_DIM)
            sums = _window_sums(ext[:, cols], window, backward=False)[POOL_HALO:, :]
            count = jnp.minimum(pos + 1, window).astype(F32)
            pooled = (sums / count - u[:, cols]).astype(BF16)
            pooled_ref[:, cols] = pooled
            pre = _dot_nn(pooled, wp_ref[g])
            out_cols = slice(D_ATTN + g * POOL_GROUP_DIM, D_ATTN + (g + 1) * POOL_GROUP_DIM)
            mix_ref[:, out_cols] = pre
            y_ref[:, out_cols] = (pre * gated_scale[:, cols]).astype(BF16)

    row = lambda width, cb=0: pl.BlockSpec((tm, width), lambda i: (i, cb))
    pat = lambda d, width: pl.BlockSpec((d, tm // d, width), lambda i: (0, i, 0))
    return _pallas(
        body, name="mix_gate", grid=(seq // tm,),
        in_specs=[row(D_ATTN), pat(d4, D_ATTN), pat(d16, D_ATTN),
                  row(STAT_LANES), pat(d4, STAT_LANES), pat(d16, STAT_LANES),
                  row(D_POOL),
                  pl.BlockSpec((POOL_HALO, D_POOL), lambda i: (jnp.maximum(i * halo_blocks - 1, 0), 0)),
                  row(D_ATTN, 1), row(D_POOL, 2),
                  pl.BlockSpec((len(POOL_WINDOWS), POOL_GROUP_DIM, POOL_GROUP_DIM), lambda i: (0, 0, 0)),
                  pl.BlockSpec((1, D_POOL), lambda i: (0, 0))],
        out_specs=[row(D_MODEL), row(D_MODEL), row(STAT_LANES), row(D_POOL)],
        out_shape=[jax.ShapeDtypeStruct((seq, D_MODEL), BF16), jax.ShapeDtypeStruct((seq, D_MODEL), F32),
                   jax.ShapeDtypeStruct((seq, STAT_LANES), F32), jax.ShapeDtypeStruct((seq, D_POOL), BF16)],
        scratch_shapes=[pltpu.VMEM((N_HEADS, tm, HEAD_DIM), F32), pltpu.VMEM((N_HEADS, tm, HEAD_DIM), F32),
                        pltpu.VMEM((1, tm, STAT_LANES), F32), pltpu.VMEM((1, tm, STAT_LANES), F32)],
        compiler_params=_params(("parallel",), 48),
    )(o_list[0][0], o_list[1], o_list[2], st_list[0][0], st_list[1], st_list[2],
      hug, hug, hug, hug, w_pool_g, pool_scale)


def _out_proj_loss(y, w_out_g, x, target, gain, bias):
    seq = x.shape[0]
    tm = 512

    def body(y_ref, w_ref, x_ref, t_ref, g_ref, b_ref, dz_ref, dzb_ref, gg_ref, gb_ref, loss_ref):
        @pl.when(pl.program_id(0) == 0)
        def _():
            gg_ref[...] = jnp.zeros_like(gg_ref)
            gb_ref[...] = jnp.zeros_like(gb_ref)
            loss_ref[...] = jnp.zeros_like(loss_ref)

        halves = [slice(0, tm // 2), slice(tm // 2, tm)]
        projected = [_dot_nn(y_ref[rows, :], w_ref[...]) for rows in halves]
        for rows, out in zip(halves, projected):
            z = DEEPNORM_ALPHA * x_ref[rows, :] + out
            mu = jnp.mean(z, axis=-1, keepdims=True)
            zc = z - mu
            rstd = lax.rsqrt(jnp.mean(zc * zc, axis=-1, keepdims=True) + LN_EPS)
            xhat = zc * rstd
            gain_v = g_ref[...]
            diff = xhat * gain_v + b_ref[...] - t_ref[rows, :]
            sq = _fold_rows(diff * diff)
            part = sq[:, :128]
            for k in range(1, D_MODEL // 128):
                part = part + sq[:, k * 128:(k + 1) * 128]
            loss_ref[...] += part
            dln = diff * (1.0 / D_MODEL)
            gg_ref[...] += _fold_rows(dln * xhat)
            gb_ref[...] += _fold_rows(dln)
            dxhat = dln * gain_v
            dz = rstd * (dxhat - jnp.mean(dxhat, axis=-1, keepdims=True)
                         - xhat * jnp.mean(dxhat * xhat, axis=-1, keepdims=True))
            dz_ref[rows, :] = dz
            dzb_ref[rows, :] = dz.astype(BF16)

    row = lambda: pl.BlockSpec((tm, D_MODEL), lambda i: (i, 0))
    vec = lambda: pl.BlockSpec((1, D_MODEL), lambda i: (0, 0))
    acc = lambda width: pl.BlockSpec((8, width), lambda i: (0, 0))
    return _pallas(
        body, name="out_proj_loss", grid=(seq // tm,),
        in_specs=[row(), pl.BlockSpec((D_MODEL, D_MODEL), lambda i: (0, 0), pipeline_mode=pl.Buffered(1)),
                  row(), row(), vec(), vec()],
        out_specs=[row(), row(), acc(D_MODEL), acc(D_MODEL), acc(128)],
        out_shape=[jax.ShapeDtypeStruct((seq, D_MODEL), F32), jax.ShapeDtypeStruct((seq, D_MODEL), BF16),
                   jax.ShapeDtypeStruct((8, D_MODEL), F32), jax.ShapeDtypeStruct((8, D_MODEL), F32),
                   jax.ShapeDtypeStruct((8, 128), F32)],
        compiler_params=_params(("arbitrary",), 56),
    )(y, w_out_g.reshape(D_MODEL, D_MODEL), x, target, gain, bias)


def _dy_gate_bwd(dzb, w_out_g, hug, mixpre, pool_scale, lse_all):
    seq = dzb.shape[0]
    tm = 256
    d4, d16 = DILATIONS[1], DILATIONS[2]

    def body(dz_ref, w_ref, ga_ref, gp_ref, mix_ref, sc_ref, lse_ref,
             dh_ref, dpo_ref, do1_ref, do4_ref, do16_ref, st1_ref, st4_ref, st16_ref, da_ref, st_ref):
        dy = _dot_nt(dz_ref[...], w_ref[...])
        ga = ga_ref[...]
        sig = jax.nn.sigmoid(ga)
        attn = mix_ref[:, :D_ATTN]
        dya = dy[:, :D_ATTN]
        dattn = dya * (ga * sig)
        dh_ref[:, :D_ATTN] = (dya * attn * (sig * (1.0 + ga * (1.0 - sig)))).astype(BF16)
        _store_slabs(da_ref, dattn)
        lane = lax.broadcasted_iota(jnp.int32, (tm, STAT_LANES), 1)
        stats = lse_ref[...]
        prod = dattn * attn
        for h in range(N_HEADS):
            delta = jnp.sum(prod[:, h * HEAD_DIM:(h + 1) * HEAD_DIM], axis=-1, keepdims=True)
            stats = jnp.where(lane == N_HEADS + h, delta, stats)
        st_ref[0] = stats
        do1_ref[...] = dattn.astype(BF16)
        st1_ref[...] = stats
        _to_pattern(da_ref, do4_ref, d4, BF16)
        _to_pattern(da_ref, do16_ref, d16, BF16)
        _to_pattern(st_ref, st4_ref, d4, F32)
        _to_pattern(st_ref, st16_ref, d16, F32)

        gp = gp_ref[...]
        sig = jax.nn.sigmoid(gp)
        dyp = dy[:, D_ATTN:]
        dpo_ref[...] = dyp * (gp * sig)
        dh_ref[:, D_ATTN:] = (dyp * (mix_ref[:, D_ATTN:] * sc_ref[...])
                              * (sig * (1.0 + gp * (1.0 - sig)))).astype(BF16)

    row = lambda width, cb=0: pl.BlockSpec((tm, width), lambda i: (i, cb))
    pat = lambda d, width: pl.BlockSpec((d, tm // d, width), lambda i: (0, i, 0))
    pat_shape = lambda d, width, dtype: jax.ShapeDtypeStruct((d, seq // d, width), dtype)
    outs = _pallas(
        body, name="dy_gate_bwd", grid=(seq // tm,),
        in_specs=[row(D_MODEL), pl.BlockSpec((D_MODEL, D_MODEL), lambda i: (0, 0)),
                  row(D_ATTN, 1), row(D_POOL, 2), row(D_MODEL), pl.BlockSpec((1, D_POOL), lambda i: (0, 0)),
                  row(STAT_LANES)],
        out_specs=[row(D_MODEL, D_IN // D_MODEL - 1), row(D_POOL),
                   row(D_ATTN), pat(d4, D_ATTN), pat(d16, D_ATTN),
                   row(STAT_LANES), pat(d4, STAT_LANES), pat(d16, STAT_LANES)],
        out_shape=[jax.ShapeDtypeStruct((seq, D_IN), BF16), jax.ShapeDtypeStruct((seq, D_POOL), F32),
                   jax.ShapeDtypeStruct((seq, D_ATTN), BF16), pat_shape(d4, D_ATTN, BF16), pat_shape(d16, D_ATTN, BF16),
                   jax.ShapeDtypeStruct((seq, STAT_LANES), F32), pat_shape(d4, STAT_LANES, F32),
                   pat_shape(d16, STAT_LANES, F32)],
        scratch_shapes=[pltpu.VMEM((N_HEADS, tm, HEAD_DIM), F32), pltpu.VMEM((1, tm, STAT_LANES), F32)],
        compiler_params=_params(("parallel",), 48),
    )(dzb, w_out_g.reshape(D_MODEL, D_MODEL), hug, hug, mixpre, pool_scale, lse_all)
    dh, dpo, do1, do4, do16, st1, st4, st16 = outs
    return dh, dpo, [do1[None], do4, do16], [st1[None], st4, st16]


def _pool_bwd(dh, dpo, mixpre, pooled, w_pool_g, pool_scale):
    seq = dpo.shape[0]
    tm = 256
    halo_blocks = tm // POOL_HALO
    last = seq // tm - 1
    n_groups = len(POOL_WINDOWS)

    def body(dh_in_ref, dpo_ref, halo_ref, pre_ref, pooled_ref, wp_ref, sc_ref, du_ref, gw_ref, gs_ref):
        i = pl.program_id(0)

        @pl.when(i == 0)
        def _():
            gw_ref[...] = jnp.zeros_like(gw_ref)
            gs_ref[...] = jnp.zeros_like(gs_ref)

        dpo = dpo_ref[...]
        scale = sc_ref[...]
        gs_ref[...] += _fold_rows(dpo * pre_ref[...])
        halo = jnp.where(i < last, halo_ref[...], 0.0)
        dpw = (jnp.concatenate([dpo, halo], axis=0) * scale).astype(BF16)
        pos = i * tm + lax.broadcasted_iota(jnp.int32, (tm + POOL_HALO, 1), 0)
        for g, window in enumerate(POOL_WINDOWS):
            cols = slice(g * POOL_GROUP_DIM, (g + 1) * POOL_GROUP_DIM)
            dpw_g = dpw[:, cols]
            gw_ref[g] += _dot_tn(pooled_ref[:, cols], dpw_g[:tm, :])
            dpooled = _dot_nt(dpw_g, wp_ref[g])
            count = jnp.minimum(pos + 1, window).astype(F32)
            sums = _window_sums(dpooled / count, window, backward=True)
            du_ref[:, cols] = (sums[:tm, :] - dpooled[:tm, :]).astype(BF16)

    row = lambda width, cb=0: pl.BlockSpec((tm, width), lambda i: (i, cb))
    return _pallas(
        body, name="pool_bwd", grid=(seq // tm,),
        in_specs=[ANY, row(D_POOL),
                  pl.BlockSpec((POOL_HALO, D_POOL),
                               lambda i: (jnp.minimum((i + 1) * halo_blocks, seq // POOL_HALO - 1), 0)),
                  row(D_POOL, 1), row(D_POOL),
                  pl.BlockSpec((n_groups, POOL_GROUP_DIM, POOL_GROUP_DIM), lambda i: (0, 0, 0)),
                  pl.BlockSpec((1, D_POOL), lambda i: (0, 0))],
        out_specs=[row(D_POOL, D_QKV // D_POOL),
                   pl.BlockSpec((n_groups, POOL_GROUP_DIM, POOL_GROUP_DIM), lambda i: (0, 0, 0)),
                   pl.BlockSpec((8, D_POOL), lambda i: (0, 0))],
        out_shape=[jax.ShapeDtypeStruct(dh.shape, dh.dtype),
                   jax.ShapeDtypeStruct((n_groups, POOL_GROUP_DIM, POOL_GROUP_DIM), F32),
                   jax.ShapeDtypeStruct((8, D_POOL), F32)],
        input_output_aliases={0: 0},
        compiler_params=_params(("arbitrary",), 40),
    )(dh, dpo, dpo, mixpre, pooled, w_pool_g, pool_scale)


def _sum_patterns(dh, parts, tabs, unrotate, col_block, name):
    seq = dh.shape[0]
    tm, tn = 256, D_ATTN
    per = D_ATTN // tn
    d4, d16 = DILATIONS[1], DILATIONS[2]

    def body(dh_in_ref, a1_ref, a4_ref, a16_ref, ct_ref, up_ref, down_ref, o_ref, n4_ref, n16_ref):
        _from_pattern(a4_ref, n4_ref, d4)
        _from_pattern(a16_ref, n16_ref, d16)
        for s in range(tn // HEAD_DIM):
            cols = slice(s * HEAD_DIM, (s + 1) * HEAD_DIM)
            tot = a1_ref[:, cols].astype(F32) + n4_ref[s] + n16_ref[s]
            if unrotate:
                tot = _rotate_heads(tot, ct_ref[...], -up_ref[...], -down_ref[...])
            o_ref[:, cols] = tot.astype(BF16)

    tab = pl.BlockSpec((tm, HEAD_DIM), lambda i, j: (i, 0))
    pat = lambda d: pl.BlockSpec((d, tm // d, tn), lambda i, j: (0, i, j))
    return _pallas(
        body, name=name, grid=(seq // tm, per),
        in_specs=[ANY, pl.BlockSpec((tm, tn), lambda i, j: (i, j)), pat(d4), pat(d16), tab, tab, tab],
        out_specs=pl.BlockSpec((tm, tn), lambda i, j: (i, col_block * per + j)),
        out_shape=jax.ShapeDtypeStruct(dh.shape, dh.dtype),
        scratch_shapes=[pltpu.VMEM((tn // HEAD_DIM, tm, HEAD_DIM), F32), pltpu.VMEM((tn // HEAD_DIM, tm, HEAD_DIM), F32)],
        input_output_aliases={0: 0},
        compiler_params=_params(("parallel", "parallel"), 32),
    )(dh, parts[0][0], parts[1], parts[2], *tabs)


def _grad_w_in(x, dh, half, name, comm=None):
    seq = x.shape[0]
    ts, td, te = 2048, D_MODEL // 2, SHARD_IN

    def body(half_ref, x_ref, dh_ref, o_ref):
        k = pl.program_id(1)
        part = _dot_tn(x_ref[...].astype(BF16), dh_ref[...])

        @pl.when(k == 0)
        def _():
            o_ref[...] = part

        @pl.when(k > 0)
        def _():
            o_ref[...] += part

    (g,), exchanged = _call(
        body, name=name, grid=(N_SHARDS, seq // ts),
        in_specs=[pl.BlockSpec((ts, td), lambda e, k, half_ref: (k, half_ref[0])),
                  pl.BlockSpec((ts, te), lambda e, k, half_ref: (k, e))],
        out_specs=[pl.BlockSpec((None, td, te), lambda e, k, half_ref: (e, 0, 0))],
        out_shape=[jax.ShapeDtypeStruct((N_SHARDS, td, te), F32)],
        scratch_shapes=[], semantics=("parallel", "arbitrary"), vmem_mib=56, args=(x, dh), comm=comm,
        prefetch=(half,))
    return g, exchanged


def _grad_w_out(y, dzb):
    seq = y.shape[0]
    ts, te = 512, 1024
    nk = seq // ts

    def body(y_ref, dz_ref, o_ref, acc_ref):
        k = pl.program_id(1)

        @pl.when(k == 0)
        def _():
            acc_ref[...] = jnp.zeros_like(acc_ref)

        acc_ref[...] += _dot_tn(y_ref[...], dz_ref[...])

        @pl.when(k == nk - 1)
        def _():
            o_ref[...] = acc_ref[...]

    return _pallas(
        body, name="grad_w_out", grid=(D_MODEL // te, nk),
        in_specs=[pl.BlockSpec((ts, te), lambda e, k: (k, e)), pl.BlockSpec((ts, D_MODEL), lambda e, k: (k, 0))],
        out_specs=pl.BlockSpec((te, D_MODEL), lambda e, k: (e, 0)),
        out_shape=jax.ShapeDtypeStruct((D_MODEL, D_MODEL), F32),
        scratch_shapes=[pltpu.VMEM((te, D_MODEL), F32)],
        compiler_params=_params(("parallel", "arbitrary"), 48),
    )(y, dzb)


GRAD_X_LATE_SHARDS = 1


def _grad_x_partial(dh, w_in_g, first, tiles, prev=None, comm=None):
    seq = dh.shape[0]
    tm, tk = 512, SHARD_IN

    def body(*refs):
        dh_ref, w_ref, o_ref = refs[-3:]
        k = pl.program_id(1)
        part = _dot_nt(dh_ref[...], w_ref[...])

        @pl.when(k == 0)
        def _():
            o_ref[...] = part

        @pl.when(k > 0)
        def _():
            o_ref[...] += part

    carried = [] if prev is None else [prev]
    (partial,), exchanged = _call(
        body, name="grad_x_partial_%d" % first, grid=(tiles, N_SHARDS - GRAD_X_LATE_SHARDS),
        in_specs=[ANY] * len(carried) + [
            pl.BlockSpec((tm, tk), lambda i, k: (i + first, k)),
            pl.BlockSpec((None, D_MODEL, tk), lambda i, k: (k, 0, 0))],
        out_specs=[pl.BlockSpec((tm, D_MODEL), lambda i, k: (i + first, 0))],
        out_shape=[jax.ShapeDtypeStruct((seq, D_MODEL), F32)],
        scratch_shapes=[], semantics=("parallel", "arbitrary"), vmem_mib=48, args=(*carried, dh, w_in_g),
        aliases={0: 0} if carried else None, comm=comm)
    return partial, exchanged


def _grad_x_final(dh, w_in_g, dz, partial):
    seq = dh.shape[0]
    tm, tk = 512, SHARD_IN
    k0 = N_SHARDS - GRAD_X_LATE_SHARDS

    def body(dh_ref, w_ref, dz_ref, p_ref, o_ref):
        k = pl.program_id(1)
        part = _dot_nt(dh_ref[...], w_ref[...])

        @pl.when(k == 0)
        def _():
            o_ref[...] = (DEEPNORM_ALPHA * dz_ref[...] + p_ref[...]) + part

        @pl.when(k > 0)
        def _():
            o_ref[...] += part

    row = pl.BlockSpec((tm, D_MODEL), lambda i, k: (i, 0))
    return _pallas(
        body, name="grad_x_final", grid=(seq // tm, GRAD_X_LATE_SHARDS),
        in_specs=[pl.BlockSpec((tm, tk), lambda i, k: (i, k + k0)),
                  pl.BlockSpec((None, D_MODEL, tk), lambda i, k: (k + k0, 0, 0)), row, row],
        out_specs=row, out_shape=jax.ShapeDtypeStruct((seq, D_MODEL), F32),
        compiler_params=_params(("parallel", "arbitrary"), 48),
    )(dh, w_in_g, dz, partial)


def _pool_weight(w_pool_sh):
    n_groups = len(POOL_WINDOWS)
    shard_c = POOL_GROUP_DIM // N_SHARDS
    return (w_pool_sh.reshape(N_SHARDS, n_groups, shard_c, POOL_GROUP_DIM).transpose(1, 0, 2, 3)
            .reshape(n_groups, POOL_GROUP_DIM, POOL_GROUP_DIM))


def _pool_grad_pieces(g_w_pool):
    n_groups = len(POOL_WINDOWS)
    half_c = POOL_GROUP_DIM // N_SHARDS // 2
    return (g_w_pool.reshape(n_groups, N_SHARDS, 2, half_c, POOL_GROUP_DIM).transpose(1, 2, 0, 3, 4)
            .reshape(N_SHARDS, 2, n_groups * half_c, POOL_GROUP_DIM))


def _step(x, target, w_in_g, w_rest, pool_scale, gain, bias, place=None):
    seq = x.shape[0]
    tabs = _rope_tables(seq)
    qkv, gathered = _in_proj_qkv(x, w_in_g, tabs, comm=_allgather_weights(w_rest) if place else None)
    w_out_g, w_pool_sh = gathered if place else w_rest
    w_pool_g = _pool_weight(w_pool_sh)
    hug = _in_proj_pool_gate(x, w_in_g)
    o_list, st_list = [], []
    for p, dil in enumerate(DILATIONS):
        o, st = _attn_fwd(qkv[p], "attn_fwd_d%d" % dil)
        o_list.append(o)
        st_list.append(st)
    y, mixpre, lse_all, pooled = _mix_gate(o_list, st_list, hug, w_pool_g, pool_scale)
    dz, dzb, gain_part, bias_part, loss_part = _out_proj_loss(y, w_out_g, x, target, gain, bias)
    dh, dpo, do_list, stat_list = _dy_gate_bwd(dzb, w_out_g, hug, mixpre, pool_scale, lse_all)
    g_w_out = _grad_w_out(y, dzb)
    dh, g_w_pool, scale_part = _pool_bwd(dh, dpo, mixpre, pooled, w_pool_g, pool_scale)
    small = jnp.concatenate([scale_part, gain_part, bias_part, loss_part], axis=1)
    early = [g_w_out.reshape(N_SHARDS, 2, D_MODEL // (2 * N_SHARDS), D_MODEL), _pool_grad_pieces(g_w_pool)]

    bwd = lambda p, comm: _attn_bwd(qkv[p], do_list[p], stat_list[p], "attn_bwd_d%d" % DILATIONS[p], comm)
    if place is None:
        parts = [bwd(p, None)[0] for p in range(3)]
    else:
        core, chip_core = place
        part_a, recv = bwd(0, _exchange_halves(early))
        sums = [_add_own_half(g, r, core, "add_own_half_%d" % a) for a, (g, r) in enumerate(zip(early, recv))]
        part_b, recv = bwd(1, _scatter_to_chips([s[1] for s in sums]))
        bufs = [_add_chips(s[0], r, chip_core, "add_chips_%d" % a) for a, (s, r) in enumerate(zip(sums, recv))]
        part_c, early = bwd(2, _share_with_sibling(bufs))
        parts = [part_a, part_b, part_c]
    dh = _sum_patterns(dh, [t[0] for t in parts], tabs, True, 0, "sum_dq")
    dh = _sum_patterns(dh, [t[1] for t in parts], tabs, True, 1, "sum_dk")
    dh = _sum_patterns(dh, [t[2] for t in parts], tabs, False, 2, "sum_dv")
    if place is None:
        halves = [_grad_w_in(x, dh, jnp.full((1,), h, jnp.int32), "grad_w_in_%d" % h)[0] for h in range(2)]
        g_w_in = jnp.stack(halves, axis=1)
        g_x = _grad_x_final(dh, w_in_g, dz, _grad_x_partial(dh, w_in_g, 0, seq // 512)[0])
    else:
        give, _ = _grad_w_in(x, dh, 1 - core, "grad_w_in_give")
        keep, recv = _grad_w_in(x, dh, core, "grad_w_in_keep", _send_to_sibling([give]))
        total, total_b = _add_pair(keep, recv[0], "add_own_half_w_in")
        rows = total.shape[1]
        cut = rows // 2
        tiles = seq // 512 // 2
        part, recv = _grad_x_partial(dh, w_in_g, 0, tiles, None, _scatter_to_chips([total_b], (0, cut)))
        part, recv = _grad_x_partial(dh, w_in_g, tiles, tiles, part,
                                     _scatter_to_chips([total_b], (cut, rows - cut), recv))
        buf = _add_chips(total, recv[0], chip_core, "add_chips_w_in")
        g_x = _grad_x_final(dh, w_in_g, dz, part)
        g_w_in = _run_exchange(_share_with_sibling([buf]), "share_w_in")[0]
    return g_x, g_w_in, early[0], early[1], small


def _exchange_halves(grads):
    n = len(grads)

    def copies(src, dst, sems):
        x, y, c, _ = _mesh_place()
        return [_remote(src[a].at[j, 1 - c], dst[a].at[j], sems[0].at[a, j], sems[1].at[a, j], (x, y, 1 - c))
                for a in range(n) for j in range(N_SHARDS)]

    def start(src, dst, sems):
        for cp in copies(src, dst, sems):
            cp.start()

    def finish(src, dst, sems):
        for cp in copies(src, dst, sems):
            cp.wait()

    return _Exchange(grads, [jax.ShapeDtypeStruct((N_SHARDS,) + g.shape[2:], g.dtype) for g in grads], {},
                     [pltpu.SemaphoreType.DMA((n, N_SHARDS))] * 2, start, finish)


def _add_own_half(grad, recv, core, name):
    _, _, r, c = grad.shape
    tr = min(r, 256)

    def body(core_ref, g_ref, r_ref, o_ref, ob_ref):
        tot = g_ref[...] + r_ref[...]
        o_ref[...] = tot
        ob_ref[...] = tot.astype(BF16)

    out = pl.BlockSpec((None, tr, c), lambda j, i, core_ref: (j, i, 0))
    return _pallas(
        body, name=name,
        grid_spec=pltpu.PrefetchScalarGridSpec(
            num_scalar_prefetch=1, grid=(N_SHARDS, r // tr),
            in_specs=[pl.BlockSpec((None, None, tr, c), lambda j, i, core_ref: (j, core_ref[0], i, 0)),
                      pl.BlockSpec((None, tr, c), lambda j, i, core_ref: (j, i, 0))],
            out_specs=[out, out]),
        out_shape=[jax.ShapeDtypeStruct((N_SHARDS, r, c), F32), jax.ShapeDtypeStruct((N_SHARDS, r, c), BF16)],
        compiler_params=_params(("parallel", "parallel"), 32),
    )(core, grad, recv)


def _send_to_sibling(arrays):
    n = len(arrays)

    def copies(src, dst, sems):
        x, y, c, _ = _mesh_place()
        return [_remote(src[a], dst[a], sems[0].at[a], sems[1].at[a], (x, y, 1 - c)) for a in range(n)]

    def start(src, dst, sems):
        for cp in copies(src, dst, sems):
            cp.start()

    def finish(src, dst, sems):
        for cp in copies(src, dst, sems):
            cp.wait()

    return _Exchange(arrays, [jax.ShapeDtypeStruct(t.shape, t.dtype) for t in arrays], {},
                     [pltpu.SemaphoreType.DMA((n,))] * 2, start, finish)


def _add_pair(a, b, name):
    _, r, c = a.shape
    tr = min(r, 256)

    def body(a_ref, b_ref, o_ref, ob_ref):
        tot = a_ref[...] + b_ref[...]
        o_ref[...] = tot
        ob_ref[...] = tot.astype(BF16)

    spec = pl.BlockSpec((None, tr, c), lambda j, i: (j, i, 0))
    return _pallas(
        body, name=name, grid=(N_SHARDS, r // tr), in_specs=[spec, spec], out_specs=[spec, spec],
        out_shape=[jax.ShapeDtypeStruct(a.shape, F32), jax.ShapeDtypeStruct(a.shape, BF16)],
        compiler_params=_params(("parallel", "parallel"), 32),
    )(a, b)


def _scatter_to_chips(sums, rows=None, into=None):
    n = len(sums)

    def copies(src, dst, sems):
        x, y, c, chips = _mesh_place()
        part = (lambda ref: ref) if rows is None else (lambda ref: ref.at[pl.ds(rows[0], rows[1])])
        return [_remote(part(src[a].at[2 * cx + cy]), part(dst[a].at[k]), sems[0].at[a, k], sems[1].at[a, k],
                        (cx, cy, c))
                for a in range(n) for k, (cx, cy) in enumerate(chips)]

    def start(src, dst, sems):
        for cp in copies(src, dst, sems):
            cp.start()

    def finish(src, dst, sems):
        for cp in copies(src, dst, sems):
            cp.wait()

    return _Exchange(sums + (into or []), [jax.ShapeDtypeStruct((3,) + s.shape[1:], s.dtype) for s in sums],
                     {n + a: a for a in range(n)} if into else {},
                     [pltpu.SemaphoreType.DMA((n, 3))] * 2, start, finish)


def _add_chips(sums, recv, chip_core, name):
    _, r, c = sums.shape
    tr = min(r, 256)

    def body(cc_ref, s_ref, r_ref, o_ref):
        o_ref[...] = ((s_ref[...] + r_ref[0].astype(F32)) + r_ref[1].astype(F32)) + r_ref[2].astype(F32)

    return _pallas(
        body, name=name,
        grid_spec=pltpu.PrefetchScalarGridSpec(
            num_scalar_prefetch=1, grid=(r // tr,),
            in_specs=[pl.BlockSpec((None, tr, c), lambda i, cc_ref: (cc_ref[0], i, 0)),
                      pl.BlockSpec((3, tr, c), lambda i, cc_ref: (0, i, 0))],
            out_specs=pl.BlockSpec((None, tr, c), lambda i, cc_ref: (cc_ref[1], i, 0))),
        out_shape=jax.ShapeDtypeStruct((2, r, c), F32),
        compiler_params=_params(("parallel",), 32),
    )(chip_core, sums, recv)


def _share_with_sibling(bufs):
    n = len(bufs)

    def copies(dst, sems, half):
        x, y, c, _ = _mesh_place()
        h = c if half == "mine" else 1 - c
        return [_remote(dst[a].at[h], dst[a].at[h], sems[0].at[a], sems[1].at[a], (x, y, 1 - c)) for a in range(n)]

    def start(ins, dst, sems):
        for cp in copies(dst, sems, "mine"):
            cp.start()

    def finish(ins, dst, sems):
        for cp in copies(dst, sems, "theirs"):
            cp.wait_recv()
        for cp in copies(dst, sems, "mine"):
            cp.wait_send()

    return _Exchange(bufs, [jax.ShapeDtypeStruct(b.shape, b.dtype) for b in bufs], {a: a for a in range(n)},
                     [pltpu.SemaphoreType.DMA((n,))] * 2, start, finish)


def _adam_math(w, g, m, v):
    m = ADAM_B1 * m + (1.0 - ADAM_B1) * g
    v = ADAM_B2 * v + (1.0 - ADAM_B2) * (g * g)
    m_hat = m / (1.0 - ADAM_B1 ** ADAM_STEP)
    v_hat = v / (1.0 - ADAM_B2 ** ADAM_STEP)
    delta = -ADAM_LR * (m_hat / (jnp.sqrt(v_hat) + ADAM_EPS) + ADAM_WD * w)
    return delta, m, v


def _small_allreduce_adamw(small, w_vec, m_vec, v_vec):
    width = small.shape[1]
    n_par = w_vec.shape[1]

    def body(s_ref, w_ref, m_ref, v_ref, loss_ref, g_ref, d_ref, nm_ref, nv_ref, gather_ref, send_sems, recv_sems):
        x, y, c = lax.axis_index("x"), lax.axis_index("y"), lax.axis_index("c")
        me = 4 * x + 2 * y + c
        gather_ref[me] = s_ref[...]
        copies = []
        for r in range(1, 8):
            bx, by, bc = (r >> 2) & 1, (r >> 1) & 1, r & 1
            peer = (x ^ bx, y ^ by, c ^ bc)
            cp = pltpu.make_async_remote_copy(
                src_ref=s_ref, dst_ref=gather_ref.at[me], send_sem=send_sems.at[r - 1], recv_sem=recv_sems.at[r - 1],
                device_id=peer, device_id_type=MESH)
            cp.start()
            copies.append(cp)
        for r in range(1, 8):
            bx, by, bc = (r >> 2) & 1, (r >> 1) & 1, r & 1
            theirs = gather_ref.at[4 * (x ^ bx) + 2 * (y ^ by) + (c ^ bc)]
            pltpu.make_async_remote_copy(
                src_ref=theirs, dst_ref=theirs, send_sem=send_sems.at[r - 1], recv_sem=recv_sems.at[r - 1],
                device_id=(x ^ bx, y ^ by, c ^ bc), device_id_type=MESH).wait_recv()
        for cp in copies:
            cp.wait_send()
        tot = gather_ref[0]
        for d in range(1, 8):
            tot = tot + gather_ref[d]
        tot = jnp.sum(tot, axis=0, keepdims=True)
        sq = jnp.sum(tot[:, n_par:], axis=1, keepdims=True)
        loss_ref[...] = jnp.broadcast_to(sq * (0.5 / D_MODEL), loss_ref.shape)
        g = tot[:, :n_par]
        g_ref[...] = g
        d_ref[...], nm_ref[...], nv_ref[...] = _adam_math(w_ref[...], g, m_ref[...], v_ref[...])

    vm = pl.BlockSpec(memory_space=pltpu.VMEM)
    vec = jax.ShapeDtypeStruct((1, n_par), F32)
    return _pallas(
        body, name="small_allreduce_adamw",
        in_specs=[vm] * 4, out_specs=[vm] * 5,
        out_shape=[jax.ShapeDtypeStruct((1, 128), F32), vec, vec, vec, vec],
        scratch_shapes=[pltpu.VMEM((8, 8, width), F32), pltpu.SemaphoreType.DMA((7,)), pltpu.SemaphoreType.DMA((7,))],
    )(small, w_vec, m_vec, v_vec)


def _adamw(w, g, m, v, name):
    r, c = w.shape
    tr = min(r, 256)

    def body(w_ref, g_ref, m_ref, v_ref, d_ref, nm_ref, nv_ref):
        d_ref[...], nm_ref[...], nv_ref[...] = _adam_math(w_ref[...], g_ref[...], m_ref[...], v_ref[...])

    spec = pl.BlockSpec((tr, c), lambda i: (i, 0))
    shape = jax.ShapeDtypeStruct((r, c), F32)
    return _pallas(
        body, name=name, grid=(r // tr,),
        in_specs=[spec] * 4, out_specs=[spec] * 3, out_shape=[shape] * 3,
        compiler_params=_params(("parallel",), 48),
    )(w, g, m, v)


def kernel(x, w_in, w_pool, pool_scale, w_out, ln_gain, ln_bias, loss_target, m_w_in, m_w_pool, m_pool_scale, m_w_out, m_ln_gain, m_ln_bias, v_w_in, v_w_pool, v_pool_scale, v_w_out, v_ln_gain, v_ln_bias):
    xi, yi, ci = lax.axis_index("x"), lax.axis_index("y"), lax.axis_index("c")
    chip = (2 * xi + yi).astype(jnp.int32).reshape(1)
    core = ci.astype(jnp.int32).reshape(1)
    n_groups = len(POOL_WINDOWS)
    shard_c = w_pool.shape[2]

    w_in_b = _cast_bf16(w_in[0], chip, "cast_w_in", 256)
    w_out_b = _cast_bf16(w_out[0], chip, "cast_w_out", 256)
    w_pool_b = _cast_bf16(w_pool[0].reshape(n_groups * shard_c, POOL_GROUP_DIM), chip, "cast_w_pool", 256)
    w_in_g = _run_exchange(_allgather_weights([w_in_b]), "allgather_w_in")[0]

    chip_core = jnp.concatenate([chip, core])
    g_x, full_in, full_out, full_pool, small = _step(
        x[0], loss_target[0], w_in_g, [w_out_b, w_pool_b], pool_scale, ln_gain, ln_bias, (core, chip_core))
    half_c = shard_c // 2
    grad_w_in = full_in.reshape(D_MODEL, SHARD_IN)
    grad_w_out = full_out.reshape(D_MODEL // N_SHARDS, D_MODEL)
    grad_w_pool = (full_pool.reshape(2, n_groups, half_c, POOL_GROUP_DIM).transpose(1, 0, 2, 3)
                   .reshape(n_groups * shard_c, POOL_GROUP_DIM))

    d_in, nm_in, nv_in = _adamw(w_in[0], grad_w_in, m_w_in[0], v_w_in[0], "adamw_w_in")
    d_out, nm_out, nv_out = _adamw(w_out[0], grad_w_out, m_w_out[0], v_w_out[0], "adamw_w_out")
    flat = lambda t: t[0].reshape(n_groups * shard_c, POOL_GROUP_DIM)
    d_pool, nm_pool, nv_pool = _adamw(flat(w_pool), grad_w_pool, flat(m_w_pool), flat(v_w_pool), "adamw_w_pool")

    cat = lambda a, b, c: jnp.concatenate([a, b, c], axis=1)
    loss_v, g_vec, d_vec, nm_vec, nv_vec = _small_allreduce_adamw(
        small, cat(pool_scale, ln_gain, ln_bias), cat(m_pool_scale, m_ln_gain, m_ln_bias),
        cat(v_pool_scale, v_ln_gain, v_ln_bias))

    def split(vec):
        return vec[:, :D_POOL], vec[:, D_POOL:D_POOL + D_MODEL], vec[:, D_POOL + D_MODEL:]

    g_scale, g_gain, g_bias = split(g_vec)
    d_scale, d_gain, d_bias = split(d_vec)
    nm_scale, nm_gain, nm_bias = split(nm_vec)
    nv_scale, nv_gain, nv_bias = split(nv_vec)
    pool_shape = w_pool.shape
    return (loss_v[0, 0], g_x[None],
            grad_w_in[None], grad_w_pool.reshape(pool_shape), g_scale, grad_w_out[None], g_gain, g_bias,
            d_in[None], d_pool.reshape(pool_shape), d_scale, d_out[None], d_gain, d_bias,
            nm_in[None], nm_pool.reshape(pool_shape), nm_scale, nm_out[None], nm_gain, nm_bias,
            nv_in[None], nv_pool.reshape(pool_shape), nv_scale, nv_out[None], nv_gain, nv_bias)
```

```python
import functools

import jax
import jax.numpy as jnp
from jax import lax
from jax.experimental import pallas as pl
from jax.experimental.pallas import tpu as pltpu

F32 = jnp.float32
BF16 = jnp.bfloat16
MESH = pl.DeviceIdType.MESH
ANY = pl.BlockSpec(memory_space=pl.ANY)

D_MODEL = 2048
D_ATTN = 1024
D_POOL = 1024
HEAD_DIM = 128
N_HEADS = 8
ROPE_DIM = 32
ROPE_THETA = 500000.0
DILATIONS = (1, 4, 16)
KEY_BLOCK = 128
CHUNK = 2 * KEY_BLOCK
STAT_LANES = 128
POOL_WINDOWS = (2, 4, 8, 16)
POOL_GROUP_DIM = 256
POOL_HALO = 16
D_QKV = 3 * D_ATTN
D_UG = D_POOL + D_MODEL
D_IN = D_QKV + D_UG
N_SHARDS = 4
SHARD_IN = D_IN // N_SHARDS
LN_EPS = 1e-5
DEEPNORM_ALPHA = 2.0 ** 0.25
ADAM_LR = 0.001
ADAM_B1 = 0.9
ADAM_B2 = 0.999
ADAM_EPS = 1e-08
ADAM_WD = 0.01
ADAM_STEP = 10
NEG = -1e30
MIB = 1024 * 1024


def _params(sem, vmem_mib):
    return pltpu.CompilerParams(dimension_semantics=sem, vmem_limit_bytes=vmem_mib * MIB)


def _pallas(body, **kwargs):
    pin = lambda s: pltpu.HBM(s.shape, s.dtype) if len(s.shape) >= 2 else s
    out_shape = kwargs.pop("out_shape")
    out_shape = [pin(s) for s in out_shape] if isinstance(out_shape, (list, tuple)) else pin(out_shape)
    call = pl.pallas_call(body, out_shape=out_shape, **kwargs)

    def run(*operands):
        return call(*[pltpu.with_memory_space_constraint(o, pltpu.HBM) if o.ndim >= 2 else o for o in operands])

    return run


class _Exchange:
    def __init__(self, operands, out_shape, aliases, sems, start, finish):
        self.operands, self.out_shape, self.aliases, self.sems = list(operands), list(out_shape), dict(aliases), list(sems)
        self.start, self.finish = start, finish


def _run_exchange(comm, name):
    n_in, n_out = len(comm.operands), len(comm.out_shape)

    def body(*refs):
        ins, outs, sems = refs[:n_in], refs[n_in:n_in + n_out], refs[n_in + n_out:]
        comm.start(ins, outs, sems)
        comm.finish(ins, outs, sems)

    return _pallas(
        body, name=name, in_specs=[ANY] * n_in, out_specs=[ANY] * n_out, out_shape=comm.out_shape,
        input_output_aliases=comm.aliases, scratch_shapes=comm.sems,
    )(*comm.operands)


def _call(body, *, name, grid, in_specs, out_specs, out_shape, scratch_shapes, semantics, vmem_mib, args,
          aliases=None, comm=None, prefetch=()):
    aliases = dict(aliases or {})
    n_pre, n_in, n_out, n_scr = len(prefetch), len(in_specs), len(out_specs), len(scratch_shapes)
    c_in, c_out = (len(comm.operands), len(comm.out_shape)) if comm else (0, 0)
    c_shapes, c_sems, c_operands = (comm.out_shape, comm.sems, comm.operands) if comm else ([], [], [])

    def hosted(*refs):
        pre, refs = refs[:n_pre], refs[n_pre:]
        a = n_in
        b = a + c_in
        c = b + n_out
        d = c + c_out
        e = d + n_scr
        if comm is None:
            body(*pre, *refs)
            return
        ids = [pl.program_id(k) for k in range(len(grid))]
        first = functools.reduce(jnp.logical_and, [i == 0 for i in ids])
        last = functools.reduce(jnp.logical_and, [i == g - 1 for i, g in zip(ids, grid)])

        @pl.when(first)
        def _():
            comm.start(refs[a:b], refs[c:d], refs[e:])

        body(*pre, *refs[:a], *refs[b:c], *refs[d:e])

        @pl.when(last)
        def _():
            comm.finish(refs[a:b], refs[c:d], refs[e:])

    if comm:
        semantics = ("arbitrary",) * len(grid)
        for i, o in comm.aliases.items():
            aliases[n_pre + n_in + i] = n_out + o
    outs = _pallas(
        hosted, name=name,
        grid_spec=pltpu.PrefetchScalarGridSpec(
            num_scalar_prefetch=n_pre, grid=grid, in_specs=list(in_specs) + [ANY] * c_in,
            out_specs=list(out_specs) + [ANY] * c_out, scratch_shapes=list(scratch_shapes) + c_sems),
        out_shape=list(out_shape) + c_shapes, input_output_aliases=aliases,
        compiler_params=_params(semantics, vmem_mib),
    )(*prefetch, *args, *c_operands)
    return list(outs[:n_out]), list(outs[n_out:])


def _dot_nn(a, b):
    return jnp.dot(a, b, preferred_element_type=F32)


def _dot_nt(a, b):
    return lax.dot_general(a, b, (((1,), (1,)), ((), ())), preferred_element_type=F32)


def _dot_tn(a, b):
    return lax.dot_general(a, b, (((0,), (0,)), ((), ())), preferred_element_type=F32)


def _fold_rows(a):
    r, c = a.shape
    return jnp.sum(a.reshape(r // 8, 8, c), axis=0)


def _cast_bf16(a, chip, name, rows):
    r, c = a.shape

    def body(chip_ref, a_ref, o_ref):
        o_ref[...] = a_ref[...].astype(BF16)

    return _pallas(
        body, name=name,
        grid_spec=pltpu.PrefetchScalarGridSpec(
            num_scalar_prefetch=1, grid=(r // rows,),
            in_specs=[pl.BlockSpec((rows, c), lambda i, chip_ref: (i, 0))],
            out_specs=pl.BlockSpec((None, rows, c), lambda i, chip_ref: (chip_ref[0], i, 0))),
        out_shape=jax.ShapeDtypeStruct((N_SHARDS, r, c), BF16),
        compiler_params=_params(("parallel",), 32),
    )(chip, a)


def _mesh_place():
    x, y, c = lax.axis_index("x"), lax.axis_index("y"), lax.axis_index("c")
    return x, y, c, [(1 - x, y), (x, 1 - y), (1 - x, 1 - y)]


def _remote(src, dst, send_sem, recv_sem, to):
    return pltpu.make_async_remote_copy(src_ref=src, dst_ref=dst, send_sem=send_sem, recv_sem=recv_sem,
                                        device_id=to, device_id_type=MESH)


def _allgather_weights(bufs):
    n = len(bufs)

    def half(a, core):
        rows = bufs[a].shape[1] // 2
        return pl.ds(core * rows, rows)

    DIAGONAL = 2

    def to_neighbours(dst, sems):
        x, y, c, chips = _mesh_place()
        own = lambda a: dst[a].at[2 * x + y, half(a, c)]
        return [_remote(own(a), own(a), sems[0].at[a, k], sems[1].at[a, k], (cx, cy, c))
                for a in range(n) for k, (cx, cy) in enumerate(chips[:DIAGONAL])]

    def relayed(dst, sems):
        x, y, c, _ = _mesh_place()
        owner = 2 * (x ^ (1 - c)) + (y ^ c)
        piece = lambda a: dst[a].at[owner, half(a, c)]
        return [_remote(piece(a), piece(a), sems[0].at[a, DIAGONAL], sems[1].at[a, DIAGONAL], (x ^ c, y ^ (1 - c), c))
                for a in range(n)]

    def start(ins, dst, sems):
        for cp in to_neighbours(dst, sems):
            cp.start()

    def finish(ins, dst, sems):
        x, y, c, chips = _mesh_place()
        sibling = (x, y, 1 - c)
        passed_on = []

        def landed_then_pass_on(k):
            cx, cy = chips[k]
            for a in range(n):
                landed = dst[a].at[2 * cx + cy, half(a, c)]
                _remote(landed, landed, sems[0].at[a, k], sems[1].at[a, k], (cx, cy, c)).wait_recv()
                cp = _remote(landed, landed, sems[2].at[a, k], sems[3].at[a, k], sibling)
                cp.start()
                passed_on.append(cp)

        for k in range(DIAGONAL):
            landed_then_pass_on(k)
        for cp in relayed(dst, sems):
            cp.start()
        landed_then_pass_on(DIAGONAL)
        for k, (cx, cy) in enumerate(chips):
            for a in range(n):
                passed = dst[a].at[2 * cx + cy, half(a, 1 - c)]
                _remote(passed, passed, sems[2].at[a, k], sems[3].at[a, k], sibling).wait_recv()
        for cp in to_neighbours(dst, sems) + relayed(dst, sems) + passed_on:
            cp.wait_send()

    return _Exchange(bufs, [jax.ShapeDtypeStruct(b.shape, b.dtype) for b in bufs], {a: a for a in range(n)},
                     [pltpu.SemaphoreType.DMA((n, 3))] * 4, start, finish)


def _rope_tables(seq):
    half = ROPE_DIM // 2
    inv_freq = ROPE_THETA ** (-(2.0 * jnp.arange(half, dtype=F32)) / ROPE_DIM)
    ang = jnp.arange(seq, dtype=jnp.int32).astype(F32)[:, None] * inv_freq[None, :]
    cos, sin = jnp.cos(ang), jnp.sin(ang)
    pad = jnp.zeros((seq, HEAD_DIM - ROPE_DIM), F32)
    zeros = jnp.zeros((seq, half), F32)
    c_tab = jnp.concatenate([cos, cos, pad + 1.0], axis=1)
    up_tab = jnp.concatenate([-sin, zeros, pad], axis=1)
    down_tab = jnp.concatenate([zeros, sin, pad], axis=1)
    return c_tab, up_tab, down_tab


def _rotate_heads(t, c_tab, up_tab, down_tab):
    outs = []
    for h in range(t.shape[1] // HEAD_DIM):
        th = t[:, h * HEAD_DIM:(h + 1) * HEAD_DIM]
        up = pltpu.roll(th, HEAD_DIM - ROPE_DIM // 2, axis=1)
        down = pltpu.roll(th, ROPE_DIM // 2, axis=1)
        outs.append(th * c_tab + up * up_tab + down * down_tab)
    return outs[0] if len(outs) == 1 else jnp.concatenate(outs, axis=1)


def _to_pattern(slabs_ref, dst_ref, dil, dtype):
    n_slabs, rows, _ = slabs_ref.shape
    for s in range(n_slabs):
        for r in range(dil):
            dst_ref[r, :, s * 128:(s + 1) * 128] = slabs_ref[s, pl.ds(r, rows // dil, dil), :].astype(dtype)


def _from_pattern(src_ref, slabs_ref, dil):
    n_slabs, rows, _ = slabs_ref.shape
    for s in range(n_slabs):
        for r in range(dil):
            slabs_ref[s, pl.ds(r, rows // dil, dil), :] = src_ref[r, :, s * 128:(s + 1) * 128].astype(F32)


def _store_slabs(slabs_ref, value):
    for s in range(slabs_ref.shape[0]):
        slabs_ref[s] = value[:, s * 128:(s + 1) * 128]


def _in_proj_qkv(x, w_in_g, tabs, comm=None):
    seq = x.shape[0]
    tm, tn = 512, SHARD_IN
    heads = tn // HEAD_DIM
    k_heads_in_second = 2 * D_ATTN // HEAD_DIM - heads
    d4, d16 = DILATIONS[1], DILATIONS[2]

    def body(x_ref, w_ref, c_ref, up_ref, down_ref, o1_ref, o4_ref, o16_ref, res_ref):
        shard = pl.program_id(0)
        xb = x_ref[...].astype(BF16)
        group = 4 * HEAD_DIM
        accs = [_dot_nn(xb, w_ref[:, g * group:(g + 1) * group]) for g in range(tn // group)]

        plain = shard == 1
        c_plain = jnp.where(plain, 1.0, c_ref[...])
        up_plain = jnp.where(plain, 0.0, up_ref[...])
        down_plain = jnp.where(plain, 0.0, down_ref[...])
        for h in range(heads):
            lanes = (h * HEAD_DIM) % group
            th = accs[h * HEAD_DIM // group][:, lanes:lanes + HEAD_DIM]
            if h < k_heads_in_second:
                th = _rotate_heads(th, c_ref[...], up_ref[...], down_ref[...])
            else:
                th = _rotate_heads(th, c_plain, up_plain, down_plain)
            res_ref[h] = th
            o1_ref[:, h * HEAD_DIM:(h + 1) * HEAD_DIM] = th.astype(BF16)
        _to_pattern(res_ref, o4_ref, d4, BF16)
        _to_pattern(res_ref, o16_ref, d16, BF16)

    tab_spec = pl.BlockSpec((tm, HEAD_DIM), lambda s, i: (i, 0))
    (o1, o4, o16), exchanged = _call(
        body, name="in_proj_qkv", grid=(D_QKV // tn, seq // tm),
        in_specs=[pl.BlockSpec((tm, D_MODEL), lambda s, i: (i, 0)),
                  pl.BlockSpec((None, D_MODEL, tn), lambda s, i: (s, 0, 0)),
                  tab_spec, tab_spec, tab_spec],
        out_specs=[pl.BlockSpec((tm, tn), lambda s, i: (i, s)),
                   pl.BlockSpec((d4, tm // d4, tn), lambda s, i: (0, i, s)),
                   pl.BlockSpec((d16, tm // d16, tn), lambda s, i: (0, i, s))],
        out_shape=[jax.ShapeDtypeStruct((seq, D_QKV), BF16),
                   jax.ShapeDtypeStruct((d4, seq // d4, D_QKV), BF16),
                   jax.ShapeDtypeStruct((d16, seq // d16, D_QKV), BF16)],
        scratch_shapes=[pltpu.VMEM((heads, tm, HEAD_DIM), F32)],
        semantics=("parallel", "parallel"), vmem_mib=52, args=(x, w_in_g, *tabs), comm=comm)
    return [o1[None], o4, o16], exchanged


def _in_proj_pool_gate(x, w_in_g):
    seq = x.shape[0]
    tm, tn = 512, SHARD_IN
    first_shard = D_QKV // tn

    def body(x_ref, w_ref, o_ref):
        o_ref[...] = _dot_nn(x_ref[...].astype(BF16), w_ref[...])

    return _pallas(
        body, name="in_proj_pool_gate", grid=(D_UG // tn, seq // tm),
        in_specs=[pl.BlockSpec((tm, D_MODEL), lambda s, i: (i, 0)),
                  pl.BlockSpec((None, D_MODEL, tn), lambda s, i: (s + first_shard, 0, 0))],
        out_specs=pl.BlockSpec((tm, tn), lambda s, i: (i, s)),
        out_shape=jax.ShapeDtypeStruct((seq, D_UG), F32),
        compiler_params=_params(("parallel", "parallel"), 48),
    )(x, w_in_g)


def _band_masks():
    row = lax.broadcasted_iota(jnp.int32, (KEY_BLOCK, KEY_BLOCK), 0)
    col = lax.broadcasted_iota(jnp.int32, (KEY_BLOCK, KEY_BLOCK), 1)
    return col <= row, col >= row


def _attn_fwd(qkv, name):
    dil, n, _ = qkv.shape
    scale = HEAD_DIM ** -0.5
    lo, hi = slice(0, KEY_BLOCK), slice(KEY_BLOCK, CHUNK)

    def body(q_ref, k_ref, v_ref, kb_ref, vb_ref, o_ref, st_ref):
        i = pl.program_id(1)
        cur_mask, prev_mask = _band_masks()
        before_mask = jnp.logical_and(prev_mask, i > 0)
        lane = lax.broadcasted_iota(jnp.int32, (KEY_BLOCK, STAT_LANES), 1)
        tasks = [(rows, h) for rows in (lo, hi) for h in range(N_HEADS)]
        head = lambda h: slice(h * HEAD_DIM, (h + 1) * HEAD_DIM)

        def prev_of(rows, h):
            if rows is lo:
                return kb_ref[:, head(h)], vb_ref[:, head(h)], before_mask
            return k_ref[lo, head(h)], v_ref[lo, head(h)], prev_mask

        scores = []
        for rows, h in tasks:
            q = q_ref[rows, head(h)]
            scores.append((_dot_nt(q, prev_of(rows, h)[0]), _dot_nt(q, k_ref[rows, head(h)])))
        probs = []
        for (rows, h), (qk_prev, qk_cur) in zip(tasks, scores):
            s_prev = jnp.where(prev_of(rows, h)[2], qk_prev * scale, NEG)
            s_cur = jnp.where(cur_mask, qk_cur * scale, NEG)
            m = jnp.max(jnp.maximum(s_prev, s_cur), axis=-1, keepdims=True)
            p_prev = jnp.exp(s_prev - m)
            p_cur = jnp.exp(s_cur - m)
            den = jnp.sum(p_prev + p_cur, axis=-1, keepdims=True)
            probs.append((p_prev.astype(BF16), p_cur.astype(BF16), den, m + jnp.log(den)))
        stats = [jnp.zeros((KEY_BLOCK, STAT_LANES), F32), jnp.zeros((KEY_BLOCK, STAT_LANES), F32)]
        for (rows, h), (p_prev, p_cur, den, lse) in zip(tasks, probs):
            o = _dot_nn(p_cur, v_ref[rows, head(h)]) + _dot_nn(p_prev, prev_of(rows, h)[1])
            o_ref[rows, head(h)] = (o / den).astype(BF16)
            b = 0 if rows is lo else 1
            stats[b] = jnp.where(lane == h, lse, stats[b])
        st_ref[lo, :] = stats[0]
        st_ref[hi, :] = stats[1]

    main = lambda cb: pl.BlockSpec((None, CHUNK, D_ATTN), lambda r, i: (r, i, cb))
    before = lambda cb: pl.BlockSpec((None, KEY_BLOCK, D_ATTN), lambda r, i: (r, jnp.maximum(2 * i - 1, 0), cb))
    return _pallas(
        body, name=name, grid=(dil, n // CHUNK),
        in_specs=[main(0), main(1), main(2), before(1), before(2)],
        out_specs=[main(0), pl.BlockSpec((None, CHUNK, STAT_LANES), lambda r, i: (r, i, 0))],
        out_shape=[jax.ShapeDtypeStruct((dil, n, D_ATTN), BF16), jax.ShapeDtypeStruct((dil, n, STAT_LANES), F32)],
        compiler_params=_params(("parallel", "parallel"), 40),
    )(qkv, qkv, qkv, qkv, qkv)


def _attn_bwd(qkv, do, stats, name, comm=None):
    dil, n, _ = qkv.shape
    n_blocks = n // KEY_BLOCK
    last = n // CHUNK - 1
    scale = HEAD_DIM ** -0.5
    lo, hi = slice(0, KEY_BLOCK), slice(KEY_BLOCK, CHUNK)

    def body(q_ref, k_ref, v_ref, kb_ref, vb_ref, qa_ref, do_ref, doa_ref, st_ref, sta_ref, dq_ref, dk_ref, dv_ref):
        i = pl.program_id(1)
        cur_mask, prev_mask = _band_masks()
        before_mask = jnp.logical_and(prev_mask, i > 0)
        after_mask = jnp.logical_and(prev_mask, i < last)

        rows_cat = lambda a, b: jnp.concatenate([a, b], axis=0)
        masks = (jnp.concatenate([before_mask, cur_mask], axis=1), jnp.concatenate([prev_mask, cur_mask], axis=1),
                 after_mask)

        def operands(h):
            cols = slice(h * HEAD_DIM, (h + 1) * HEAD_DIM)
            lse_c, del_c = slice(h, h + 1), slice(N_HEADS + h, N_HEADS + h + 1)
            q = (q_ref[lo, cols], q_ref[hi, cols], qa_ref[:, cols])
            do = (do_ref[lo, cols], do_ref[hi, cols], doa_ref[:, cols])
            keys = (rows_cat(kb_ref[:, cols], k_ref[lo, cols]), k_ref[:, cols], k_ref[hi, cols])
            vals = (rows_cat(vb_ref[:, cols], v_ref[lo, cols]), v_ref[:, cols], v_ref[hi, cols])
            st = ((st_ref[lo, lse_c], st_ref[lo, del_c]), (st_ref[hi, lse_c], st_ref[hi, del_c]),
                  (sta_ref[:, lse_c], sta_ref[:, del_c]))
            return cols, q, do, keys, vals, st

        group = N_HEADS // 2
        for first_head in range(0, N_HEADS, group):
            heads = range(first_head, first_head + group)
            raw = {}
            for h in heads:
                _, q, do, keys, vals, _ = operands(h)
                raw[h] = [(_dot_nt(q[j], keys[j]), _dot_nt(do[j], vals[j])) for j in range(3)]
            grads = {}
            for h in heads:
                st = operands(h)[5]
                grads[h] = []
                for j in range(3):
                    qk, dp = raw[h][j]
                    lse, delta = st[j]
                    p = jnp.exp(jnp.where(masks[j], qk * scale, NEG) - lse)
                    grads[h].append((p.astype(BF16), (p * (dp - delta) * scale).astype(BF16)))
            for h in heads:
                cols, q, do, keys, _, _ = operands(h)
                (p0, ds0), (p1, ds1), (pa, dsa) = grads[h]
                own, nxt = slice(KEY_BLOCK, CHUNK), slice(0, KEY_BLOCK)

                def put(ref, rows, val, cols=cols):
                    ref[rows, cols] = val.astype(ref.dtype)

                put(dq_ref, lo, _dot_nn(ds0, keys[0]))
                put(dq_ref, hi, _dot_nn(ds1, keys[1]))
                put(dk_ref, lo, _dot_tn(rows_cat(ds0[:, own], ds1[:, nxt]), q_ref[:, cols]))
                put(dk_ref, hi, _dot_tn(rows_cat(ds1[:, own], dsa), rows_cat(q[1], q[2])))
                put(dv_ref, lo, _dot_tn(rows_cat(p0[:, own], p1[:, nxt]), do_ref[:, cols]))
                put(dv_ref, hi, _dot_tn(rows_cat(p1[:, own], pa), rows_cat(do[1], do[2])))

    def spec(rows, width, row_of, cb):
        return pl.BlockSpec((None, rows, width), lambda r, i: (r, row_of(i), cb))

    same = lambda i: i
    before = lambda i: jnp.maximum(2 * i - 1, 0)
    after = lambda i: jnp.minimum(2 * i + 2, n_blocks - 1)
    out = spec(CHUNK, D_ATTN, same, 0)
    return _call(
        body, name=name, grid=(dil, n // CHUNK),
        in_specs=[spec(CHUNK, D_ATTN, same, 0), spec(CHUNK, D_ATTN, same, 1), spec(CHUNK, D_ATTN, same, 2),
                  spec(KEY_BLOCK, D_ATTN, before, 1), spec(KEY_BLOCK, D_ATTN, before, 2),
                  spec(KEY_BLOCK, D_ATTN, after, 0),
                  spec(CHUNK, D_ATTN, same, 0), spec(KEY_BLOCK, D_ATTN, after, 0),
                  spec(CHUNK, STAT_LANES, same, 0), spec(KEY_BLOCK, STAT_LANES, after, 0)],
        out_specs=[out, out, out],
        out_shape=[jax.ShapeDtypeStruct((dil, n, D_ATTN), BF16)] * 3,
        scratch_shapes=[], semantics=("parallel", "parallel"), vmem_mib=40,
        args=(qkv, qkv, qkv, qkv, qkv, qkv, do, do, stats, stats), comm=comm)


def _window_sums(ext, window, backward):
    rows = ext.shape[0]
    acc, span = ext, 1
    while span < window:
        acc = acc + pltpu.roll(acc, (rows - span) if backward else span, axis=0)
        span *= 2
    return acc


def _mix_gate(o_list, st_list, hug, w_pool_g, pool_scale):
    seq = hug.shape[0]
    tm = 256
    halo_blocks = tm // POOL_HALO
    d4, d16 = DILATIONS[1], DILATIONS[2]

    def body(o1_ref, o4_ref, o16_ref, l1_ref, l4_ref, l16_ref, u_ref, halo_ref, ga_ref, gp_ref, wp_ref, sc_ref,
             y_ref, mix_ref, lse_ref, pooled_ref, n4_ref, n16_ref, nl4_ref, nl16_ref):
        i = pl.program_id(0)
        _from_pattern(o4_ref, n4_ref, d4)
        _from_pattern(o16_ref, n16_ref, d16)
        _from_pattern(l4_ref, nl4_ref, d4)
        _from_pattern(l16_ref, nl16_ref, d16)
        la, lb, lc = l1_ref[...], nl4_ref[0], nl16_ref[0]
        mx = jnp.maximum(jnp.maximum(la, lb), lc)
        ea, eb, ec = jnp.exp(la - mx), jnp.exp(lb - mx), jnp.exp(lc - mx)
        tot = ea + eb + ec
        lse_ref[...] = mx + jnp.log(tot)
        wa, wb, wc = ea / tot, eb / tot, ec / tot
        ga = ga_ref[...]
        silu_a = ga * jax.nn.sigmoid(ga)
        for h in range(N_HEADS):
            cols = slice(h * HEAD_DIM, (h + 1) * HEAD_DIM)
            hc = slice(h, h + 1)
            attn = wa[:, hc] * o1_ref[:, cols].astype(F32) + wb[:, hc] * n4_ref[h] + wc[:, hc] * n16_ref[h]
            mix_ref[:, cols] = attn
            y_ref[:, cols] = (attn * silu_a[:, cols]).astype(BF16)

        u = u_ref[...]
        halo = jnp.where(i > 0, halo_ref[...], 0.0)
        ext = jnp.concatenate([halo, u], axis=0)
        pos = i * tm + lax.broadcasted_iota(jnp.int32, (tm, 1), 0)
        gp = gp_ref[...]
        gated_scale = sc_ref[...] * (gp * jax.nn.sigmoid(gp))
        for g, window in enumerate(POOL_WINDOWS):
            cols = slice(g * POOL_GROUP_DIM, (g + 1) * POOL_GROUP_DIM)
            sums = _window_sums(ext[:, cols], window, backward=False)[POOL_HALO:, :]
            count = jnp.minimum(pos + 1, window).astype(F32)
            pooled = (sums / count - u[:, cols]).astype(BF16)
            pooled_ref[:, cols] = pooled
            pre = _dot_nn(pooled, wp_ref[g])
            out_cols = slice(D_ATTN + g * POOL_GROUP_DIM, D_ATTN + (g + 1) * POOL_GROUP_DIM)
            mix_ref[:, out_cols] = pre
            y_ref[:, out_cols] = (pre * gated_scale[:, cols]).astype(BF16)

    row = lambda width, cb=0: pl.BlockSpec((tm, width), lambda i: (i, cb))
    pat = lambda d, width: pl.BlockSpec((d, tm // d, width), lambda i: (0, i, 0))
    return _pallas(
        body, name="mix_gate", grid=(seq // tm,),
        in_specs=[row(D_ATTN), pat(d4, D_ATTN), pat(d16, D_ATTN),
                  row(STAT_LANES), pat(d4, STAT_LANES), pat(d16, STAT_LANES),
                  row(D_POOL),
                  pl.BlockSpec((POOL_HALO, D_POOL), lambda i: (jnp.maximum(i * halo_blocks - 1, 0), 0)),
                  row(D_ATTN, 1), row(D_POOL, 2),
                  pl.BlockSpec((len(POOL_WINDOWS), POOL_GROUP_DIM, POOL_GROUP_DIM), lambda i: (0, 0, 0)),
                  pl.BlockSpec((1, D_POOL), lambda i: (0, 0))],
        out_specs=[row(D_MODEL), row(D_MODEL), row(STAT_LANES), row(D_POOL)],
        out_shape=[jax.ShapeDtypeStruct((seq, D_MODEL), BF16), jax.ShapeDtypeStruct((seq, D_MODEL), F32),
                   jax.ShapeDtypeStruct((seq, STAT_LANES), F32), jax.ShapeDtypeStruct((seq, D_POOL), BF16)],
        scratch_shapes=[pltpu.VMEM((N_HEADS, tm, HEAD_DIM), F32), pltpu.VMEM((N_HEADS, tm, HEAD_DIM), F32),
                        pltpu.VMEM((1, tm, STAT_LANES), F32), pltpu.VMEM((1, tm, STAT_LANES), F32)],
        compiler_params=_params(("parallel",), 48),
    )(o_list[0][0], o_list[1], o_list[2], st_list[0][0], st_list[1], st_list[2],
      hug, hug, hug, hug, w_pool_g, pool_scale)


def _out_proj_loss(y, w_out_g, x, target, gain, bias):
    seq = x.shape[0]
    tm = 512

    def body(y_ref, w_ref, x_ref, t_ref, g_ref, b_ref, dz_ref, dzb_ref, gg_ref, gb_ref, loss_ref):
        @pl.when(pl.program_id(0) == 0)
        def _():
            gg_ref[...] = jnp.zeros_like(gg_ref)
            gb_ref[...] = jnp.zeros_like(gb_ref)
            loss_ref[...] = jnp.zeros_like(loss_ref)

        halves = [slice(0, tm // 2), slice(tm // 2, tm)]
        projected = [_dot_nn(y_ref[rows, :], w_ref[...]) for rows in halves]
        for rows, out in zip(halves, projected):
            z = DEEPNORM_ALPHA * x_ref[rows, :] + out
            mu = jnp.mean(z, axis=-1, keepdims=True)
            zc = z - mu
            rstd = lax.rsqrt(jnp.mean(zc * zc, axis=-1, keepdims=True) + LN_EPS)
            xhat = zc * rstd
            gain_v = g_ref[...]
            diff = xhat * gain_v + b_ref[...] - t_ref[rows, :]
            sq = _fold_rows(diff * diff)
            part = sq[:, :128]
            for k in range(1, D_MODEL // 128):
                part = part + sq[:, k * 128:(k + 1) * 128]
            loss_ref[...] += part
            dln = diff * (1.0 / D_MODEL)
            gg_ref[...] += _fold_rows(dln * xhat)
            gb_ref[...] += _fold_rows(dln)
            dxhat = dln * gain_v
            dz = rstd * (dxhat - jnp.mean(dxhat, axis=-1, keepdims=True)
                         - xhat * jnp.mean(dxhat * xhat, axis=-1, keepdims=True))
            dz_ref[rows, :] = dz
            dzb_ref[rows, :] = dz.astype(BF16)

    row = lambda: pl.BlockSpec((tm, D_MODEL), lambda i: (i, 0))
    vec = lambda: pl.BlockSpec((1, D_MODEL), lambda i: (0, 0))
    acc = lambda width: pl.BlockSpec((8, width), lambda i: (0, 0))
    return _pallas(
        body, name="out_proj_loss", grid=(seq // tm,),
        in_specs=[row(), pl.BlockSpec((D_MODEL, D_MODEL), lambda i: (0, 0), pipeline_mode=pl.Buffered(1)),
                  row(), row(), vec(), vec()],
        out_specs=[row(), row(), acc(D_MODEL), acc(D_MODEL), acc(128)],
        out_shape=[jax.ShapeDtypeStruct((seq, D_MODEL), F32), jax.ShapeDtypeStruct((seq, D_MODEL), BF16),
                   jax.ShapeDtypeStruct((8, D_MODEL), F32), jax.ShapeDtypeStruct((8, D_MODEL), F32),
                   jax.ShapeDtypeStruct((8, 128), F32)],
        compiler_params=_params(("arbitrary",), 56),
    )(y, w_out_g.reshape(D_MODEL, D_MODEL), x, target, gain, bias)


def _dy_gate_bwd(dzb, w_out_g, hug, mixpre, pool_scale, lse_all):
    seq = dzb.shape[0]
    tm = 256
    d4, d16 = DILATIONS[1], DILATIONS[2]

    def body(dz_ref, w_ref, ga_ref, gp_ref, mix_ref, sc_ref, lse_ref,
             dh_ref, dpo_ref, do1_ref, do4_ref, do16_ref, st1_ref, st4_ref, st16_ref, da_ref, st_ref):
        dy = _dot_nt(dz_ref[...], w_ref[...])
        ga = ga_ref[...]
        sig = jax.nn.sigmoid(ga)
        attn = mix_ref[:, :D_ATTN]
        dya = dy[:, :D_ATTN]
        dattn = dya * (ga * sig)
        dh_ref[:, :D_ATTN] = (dya * attn * (sig * (1.0 + ga * (1.0 - sig)))).astype(BF16)
        _store_slabs(da_ref, dattn)
        lane = lax.broadcasted_iota(jnp.int32, (tm, STAT_LANES), 1)
        stats = lse_ref[...]
        prod = dattn * attn
        for h in range(N_HEADS):
            delta = jnp.sum(prod[:, h * HEAD_DIM:(h + 1) * HEAD_DIM], axis=-1, keepdims=True)
            stats = jnp.where(lane == N_HEADS + h, delta, stats)
        st_ref[0] = stats
        do1_ref[...] = dattn.astype(BF16)
        st1_ref[...] = stats
        _to_pattern(da_ref, do4_ref, d4, BF16)
        _to_pattern(da_ref, do16_ref, d16, BF16)
        _to_pattern(st_ref, st4_ref, d4, F32)
        _to_pattern(st_ref, st16_ref, d16, F32)

        gp = gp_ref[...]
        sig = jax.nn.sigmoid(gp)
        dyp = dy[:, D_ATTN:]
        dpo_ref[...] = dyp * (gp * sig)
        dh_ref[:, D_ATTN:] = (dyp * (mix_ref[:, D_ATTN:] * sc_ref[...])
                              * (sig * (1.0 + gp * (1.0 - sig)))).astype(BF16)

    row = lambda width, cb=0: pl.BlockSpec((tm, width), lambda i: (i, cb))
    pat = lambda d, width: pl.BlockSpec((d, tm // d, width), lambda i: (0, i, 0))
    pat_shape = lambda d, width, dtype: jax.ShapeDtypeStruct((d, seq // d, width), dtype)
    outs = _pallas(
        body, name="dy_gate_bwd", grid=(seq // tm,),
        in_specs=[row(D_MODEL), pl.BlockSpec((D_MODEL, D_MODEL), lambda i: (0, 0)),
                  row(D_ATTN, 1), row(D_POOL, 2), row(D_MODEL), pl.BlockSpec((1, D_POOL), lambda i: (0, 0)),
                  row(STAT_LANES)],
        out_specs=[row(D_MODEL, D_IN // D_MODEL - 1), row(D_POOL),
                   row(D_ATTN), pat(d4, D_ATTN), pat(d16, D_ATTN),
                   row(STAT_LANES), pat(d4, STAT_LANES), pat(d16, STAT_LANES)],
        out_shape=[jax.ShapeDtypeStruct((seq, D_IN), BF16), jax.ShapeDtypeStruct((seq, D_POOL), F32),
                   jax.ShapeDtypeStruct((seq, D_ATTN), BF16), pat_shape(d4, D_ATTN, BF16), pat_shape(d16, D_ATTN, BF16),
                   jax.ShapeDtypeStruct((seq, STAT_LANES), F32), pat_shape(d4, STAT_LANES, F32),
                   pat_shape(d16, STAT_LANES, F32)],
        scratch_shapes=[pltpu.VMEM((N_HEADS, tm, HEAD_DIM), F32), pltpu.VMEM((1, tm, STAT_LANES), F32)],
        compiler_params=_params(("parallel",), 48),
    )(dzb, w_out_g.reshape(D_MODEL, D_MODEL), hug, hug, mixpre, pool_scale, lse_all)
    dh, dpo, do1, do4, do16, st1, st4, st16 = outs
    return dh, dpo, [do1[None], do4, do16], [st1[None], st4, st16]


def _pool_bwd(dh, dpo, mixpre, pooled, w_pool_g, pool_scale):
    seq = dpo.shape[0]
    tm = 256
    halo_blocks = tm // POOL_HALO
    last = seq // tm - 1
    n_groups = len(POOL_WINDOWS)

    def body(dh_in_ref, dpo_ref, halo_ref, pre_ref, pooled_ref, wp_ref, sc_ref, du_ref, gw_ref, gs_ref):
        i = pl.program_id(0)

        @pl.when(i == 0)
        def _():
            gw_ref[...] = jnp.zeros_like(gw_ref)
            gs_ref[...] = jnp.zeros_like(gs_ref)

        dpo = dpo_ref[...]
        scale = sc_ref[...]
        gs_ref[...] += _fold_rows(dpo * pre_ref[...])
        halo = jnp.where(i < last, halo_ref[...], 0.0)
        dpw = (jnp.concatenate([dpo, halo], axis=0) * scale).astype(BF16)
        pos = i * tm + lax.broadcasted_iota(jnp.int32, (tm + POOL_HALO, 1), 0)
        for g, window in enumerate(POOL_WINDOWS):
            cols = slice(g * POOL_GROUP_DIM, (g + 1) * POOL_GROUP_DIM)
            dpw_g = dpw[:, cols]
            gw_ref[g] += _dot_tn(pooled_ref[:, cols], dpw_g[:tm, :])
            dpooled = _dot_nt(dpw_g, wp_ref[g])
            count = jnp.minimum(pos + 1, window).astype(F32)
            sums = _window_sums(dpooled / count, window, backward=True)
            du_ref[:, cols] = (sums[:tm, :] - dpooled[:tm, :]).astype(BF16)

    row = lambda width, cb=0: pl.BlockSpec((tm, width), lambda i: (i, cb))
    return _pallas(
        body, name="pool_bwd", grid=(seq // tm,),
        in_specs=[ANY, row(D_POOL),
                  pl.BlockSpec((POOL_HALO, D_POOL),
                               lambda i: (jnp.minimum((i + 1) * halo_blocks, seq // POOL_HALO - 1), 0)),
                  row(D_POOL, 1), row(D_POOL),
                  pl.BlockSpec((n_groups, POOL_GROUP_DIM, POOL_GROUP_DIM), lambda i: (0, 0, 0)),
                  pl.BlockSpec((1, D_POOL), lambda i: (0, 0))],
        out_specs=[row(D_POOL, D_QKV // D_POOL),
                   pl.BlockSpec((n_groups, POOL_GROUP_DIM, POOL_GROUP_DIM), lambda i: (0, 0, 0)),
                   pl.BlockSpec((8, D_POOL), lambda i: (0, 0))],
        out_shape=[jax.ShapeDtypeStruct(dh.shape, dh.dtype),
                   jax.ShapeDtypeStruct((n_groups, POOL_GROUP_DIM, POOL_GROUP_DIM), F32),
                   jax.ShapeDtypeStruct((8, D_POOL), F32)],
        input_output_aliases={0: 0},
        compiler_params=_params(("arbitrary",), 40),
    )(dh, dpo, dpo, mixpre, pooled, w_pool_g, pool_scale)


def _sum_patterns(dh, parts, tabs, unrotate, col_block, name, comm=None):
    seq = dh.shape[0]
    tm, tn = 256, D_ATTN
    per = D_ATTN // tn
    d4, d16 = DILATIONS[1], DILATIONS[2]

    def body(dh_in_ref, a1_ref, a4_ref, a16_ref, ct_ref, up_ref, down_ref, o_ref, n4_ref, n16_ref):
        _from_pattern(a4_ref, n4_ref, d4)
        _from_pattern(a16_ref, n16_ref, d16)
        for s in range(tn // HEAD_DIM):
            cols = slice(s * HEAD_DIM, (s + 1) * HEAD_DIM)
            tot = a1_ref[:, cols].astype(F32) + n4_ref[s] + n16_ref[s]
            if unrotate:
                tot = _rotate_heads(tot, ct_ref[...], -up_ref[...], -down_ref[...])
            o_ref[:, cols] = tot.astype(BF16)

    tab = pl.BlockSpec((tm, HEAD_DIM), lambda i, j: (i, 0))
    pat = lambda d: pl.BlockSpec((d, tm // d, tn), lambda i, j: (0, i, j))
    (dh,), exchanged = _call(
        body, name=name, grid=(seq // tm, per),
        in_specs=[ANY, pl.BlockSpec((tm, tn), lambda i, j: (i, j)), pat(d4), pat(d16), tab, tab, tab],
        out_specs=[pl.BlockSpec((tm, tn), lambda i, j: (i, col_block * per + j))],
        out_shape=[jax.ShapeDtypeStruct(dh.shape, dh.dtype)],
        scratch_shapes=[pltpu.VMEM((tn // HEAD_DIM, tm, HEAD_DIM), F32), pltpu.VMEM((tn // HEAD_DIM, tm, HEAD_DIM), F32)],
        semantics=("parallel", "parallel"), vmem_mib=32, args=(dh, parts[0][0], parts[1], parts[2], *tabs),
        aliases={0: 0}, comm=comm)
    return dh, exchanged


def _grad_w_in(x, dh, half, name, comm=None):
    seq = x.shape[0]
    ts, td, te = 2048, D_MODEL // 2, SHARD_IN

    def body(half_ref, x_ref, dh_ref, o_ref):
        k = pl.program_id(1)
        part = _dot_tn(x_ref[...].astype(BF16), dh_ref[...])

        @pl.when(k == 0)
        def _():
            o_ref[...] = part

        @pl.when(k > 0)
        def _():
            o_ref[...] += part

    (g,), exchanged = _call(
        body, name=name, grid=(N_SHARDS, seq // ts),
        in_specs=[pl.BlockSpec((ts, td), lambda e, k, half_ref: (k, half_ref[0])),
                  pl.BlockSpec((ts, te), lambda e, k, half_ref: (k, e))],
        out_specs=[pl.BlockSpec((None, td, te), lambda e, k, half_ref: (e, 0, 0))],
        out_shape=[jax.ShapeDtypeStruct((N_SHARDS, td, te), F32)],
        scratch_shapes=[], semantics=("parallel", "arbitrary"), vmem_mib=56, args=(x, dh), comm=comm,
        prefetch=(half,))
    return g, exchanged


def _grad_w_out(y, dzb):
    seq = y.shape[0]
    ts, te = 512, 1024
    nk = seq // ts

    def body(y_ref, dz_ref, o_ref, acc_ref):
        k = pl.program_id(1)

        @pl.when(k == 0)
        def _():
            acc_ref[...] = jnp.zeros_like(acc_ref)

        acc_ref[...] += _dot_tn(y_ref[...], dz_ref[...])

        @pl.when(k == nk - 1)
        def _():
            o_ref[...] = acc_ref[...]

    return _pallas(
        body, name="grad_w_out", grid=(D_MODEL // te, nk),
        in_specs=[pl.BlockSpec((ts, te), lambda e, k: (k, e)), pl.BlockSpec((ts, D_MODEL), lambda e, k: (k, 0))],
        out_specs=pl.BlockSpec((te, D_MODEL), lambda e, k: (e, 0)),
        out_shape=jax.ShapeDtypeStruct((D_MODEL, D_MODEL), F32),
        scratch_shapes=[pltpu.VMEM((te, D_MODEL), F32)],
        compiler_params=_params(("parallel", "arbitrary"), 48),
    )(y, dzb)


GRAD_X_LATE_SHARDS = 1


def _grad_x_partial(dh, w_in_g, first, tiles, prev=None, comm=None):
    seq = dh.shape[0]
    tm, tk = 512, SHARD_IN

    def body(*refs):
        dh_ref, w_ref, o_ref = refs[-3:]
        k = pl.program_id(1)
        part = _dot_nt(dh_ref[...], w_ref[...])

        @pl.when(k == 0)
        def _():
            o_ref[...] = part

        @pl.when(k > 0)
        def _():
            o_ref[...] += part

    carried = [] if prev is None else [prev]
    (partial,), exchanged = _call(
        body, name="grad_x_partial_%d" % first, grid=(tiles, N_SHARDS - GRAD_X_LATE_SHARDS),
        in_specs=[ANY] * len(carried) + [
            pl.BlockSpec((tm, tk), lambda i, k: (i + first, k)),
            pl.BlockSpec((None, D_MODEL, tk), lambda i, k: (k, 0, 0))],
        out_specs=[pl.BlockSpec((tm, D_MODEL), lambda i, k: (i + first, 0))],
        out_shape=[jax.ShapeDtypeStruct((seq, D_MODEL), F32)],
        scratch_shapes=[], semantics=("parallel", "arbitrary"), vmem_mib=48, args=(*carried, dh, w_in_g),
        aliases={0: 0} if carried else None, comm=comm)
    return partial, exchanged


def _grad_x_final(dh, w_in_g, dz, partial):
    seq = dh.shape[0]
    tm, tk = 512, SHARD_IN
    k0 = N_SHARDS - GRAD_X_LATE_SHARDS

    def body(dh_ref, w_ref, dz_ref, p_ref, o_ref):
        k = pl.program_id(1)
        part = _dot_nt(dh_ref[...], w_ref[...])

        @pl.when(k == 0)
        def _():
            o_ref[...] = (DEEPNORM_ALPHA * dz_ref[...] + p_ref[...]) + part

        @pl.when(k > 0)
        def _():
            o_ref[...] += part

    row = pl.BlockSpec((tm, D_MODEL), lambda i, k: (i, 0))
    return _pallas(
        body, name="grad_x_final", grid=(seq // tm, GRAD_X_LATE_SHARDS),
        in_specs=[pl.BlockSpec((tm, tk), lambda i, k: (i, k + k0)),
                  pl.BlockSpec((None, D_MODEL, tk), lambda i, k: (k + k0, 0, 0)), row, row],
        out_specs=row, out_shape=jax.ShapeDtypeStruct((seq, D_MODEL), F32),
        compiler_params=_params(("parallel", "arbitrary"), 48),
    )(dh, w_in_g, dz, partial)


def _pool_weight(w_pool_sh):
    n_groups = len(POOL_WINDOWS)
    shard_c = POOL_GROUP_DIM // N_SHARDS
    return (w_pool_sh.reshape(N_SHARDS, n_groups, shard_c, POOL_GROUP_DIM).transpose(1, 0, 2, 3)
            .reshape(n_groups, POOL_GROUP_DIM, POOL_GROUP_DIM))


def _pool_grad_pieces(g_w_pool):
    n_groups = len(POOL_WINDOWS)
    half_c = POOL_GROUP_DIM // N_SHARDS // 2
    return (g_w_pool.reshape(n_groups, N_SHARDS, 2, half_c, POOL_GROUP_DIM).transpose(1, 2, 0, 3, 4)
            .reshape(N_SHARDS, 2, n_groups * half_c, POOL_GROUP_DIM))


def _step(x, target, w_in_g, w_rest, pool_scale, gain, bias, place=None):
    seq = x.shape[0]
    tabs = _rope_tables(seq)
    qkv, gathered = _in_proj_qkv(x, w_in_g, tabs, comm=_allgather_weights(w_rest) if place else None)
    w_out_g, w_pool_sh = gathered if place else w_rest
    w_pool_g = _pool_weight(w_pool_sh)
    hug = _in_proj_pool_gate(x, w_in_g)
    o_list, st_list = [], []
    for p, dil in enumerate(DILATIONS):
        o, st = _attn_fwd(qkv[p], "attn_fwd_d%d" % dil)
        o_list.append(o)
        st_list.append(st)
    y, mixpre, lse_all, pooled = _mix_gate(o_list, st_list, hug, w_pool_g, pool_scale)
    dz, dzb, gain_part, bias_part, loss_part = _out_proj_loss(y, w_out_g, x, target, gain, bias)
    dh, dpo, do_list, stat_list = _dy_gate_bwd(dzb, w_out_g, hug, mixpre, pool_scale, lse_all)
    g_w_out = _grad_w_out(y, dzb)
    dh, g_w_pool, scale_part = _pool_bwd(dh, dpo, mixpre, pooled, w_pool_g, pool_scale)
    small = jnp.concatenate([scale_part, gain_part, bias_part, loss_part], axis=1)
    early = [g_w_out.reshape(N_SHARDS, 2, D_MODEL // (2 * N_SHARDS), D_MODEL), _pool_grad_pieces(g_w_pool)]

    bwd = lambda p, comm: _attn_bwd(qkv[p], do_list[p], stat_list[p], "attn_bwd_d%d" % DILATIONS[p], comm)
    if place is None:
        parts = [bwd(p, None)[0] for p in range(3)]
    else:
        core, chip_core = place
        part_a, recv = bwd(0, _exchange_halves(early))
        sums = [_add_own_half(g, r, core, "add_own_half_%d" % a) for a, (g, r) in enumerate(zip(early, recv))]
        part_b, recv = bwd(1, _scatter_to_chips([s[1] for s in sums]))
        bufs = [_add_chips(s[0], r, chip_core, "add_chips_%d" % a) for a, (s, r) in enumerate(zip(sums, recv))]
        part_c, early = bwd(2, _share_with_sibling(bufs))
        parts = [part_a, part_b, part_c]
    dh, gathered = _sum_patterns(dh, [t[0] for t in parts], tabs, True, 0, "sum_dq",
                                 _gather_small(small) if place else None)
    dh, _ = _sum_patterns(dh, [t[1] for t in parts], tabs, True, 1, "sum_dk")
    dh, _ = _sum_patterns(dh, [t[2] for t in parts], tabs, False, 2, "sum_dv")
    if place:
        small = (small, gathered[0])
    if place is None:
        halves = [_grad_w_in(x, dh, jnp.full((1,), h, jnp.int32), "grad_w_in_%d" % h)[0] for h in range(2)]
        g_w_in = jnp.stack(halves, axis=1)
        g_x = _grad_x_final(dh, w_in_g, dz, _grad_x_partial(dh, w_in_g, 0, seq // 512)[0])
    else:
        give, _ = _grad_w_in(x, dh, 1 - core, "grad_w_in_give")
        keep, recv = _grad_w_in(x, dh, core, "grad_w_in_keep", _send_to_sibling([give]))
        total, total_b = _add_pair(keep, recv[0], "add_own_half_w_in")
        rows = total.shape[1]
        cut = rows // 2
        tiles = seq // 512 // 2
        part, recv = _grad_x_partial(dh, w_in_g, 0, tiles, None, _scatter_to_chips([total_b], (0, cut)))
        part, recv = _grad_x_partial(dh, w_in_g, tiles, tiles, part,
                                     _scatter_to_chips([total_b], (cut, rows - cut), recv))
        buf = _add_chips(total, recv[0], chip_core, "add_chips_w_in")
        g_x = _grad_x_final(dh, w_in_g, dz, part)
        g_w_in = _run_exchange(_share_with_sibling([buf]), "share_w_in")[0]
    return g_x, g_w_in, early[0], early[1], small


def _exchange_halves(grads):
    n = len(grads)

    def copies(src, dst, sems):
        x, y, c, _ = _mesh_place()
        return [_remote(src[a].at[j, 1 - c], dst[a].at[j], sems[0].at[a, j], sems[1].at[a, j], (x, y, 1 - c))
                for a in range(n) for j in range(N_SHARDS)]

    def start(src, dst, sems):
        for cp in copies(src, dst, sems):
            cp.start()

    def finish(src, dst, sems):
        for cp in copies(src, dst, sems):
            cp.wait()

    return _Exchange(grads, [jax.ShapeDtypeStruct((N_SHARDS,) + g.shape[2:], g.dtype) for g in grads], {},
                     [pltpu.SemaphoreType.DMA((n, N_SHARDS))] * 2, start, finish)


def _add_own_half(grad, recv, core, name):
    _, _, r, c = grad.shape
    tr = min(r, 256)

    def body(core_ref, g_ref, r_ref, o_ref, ob_ref):
        tot = g_ref[...] + r_ref[...]
        o_ref[...] = tot
        ob_ref[...] = tot.astype(BF16)

    out = pl.BlockSpec((None, tr, c), lambda j, i, core_ref: (j, i, 0))
    return _pallas(
        body, name=name,
        grid_spec=pltpu.PrefetchScalarGridSpec(
            num_scalar_prefetch=1, grid=(N_SHARDS, r // tr),
            in_specs=[pl.BlockSpec((None, None, tr, c), lambda j, i, core_ref: (j, core_ref[0], i, 0)),
                      pl.BlockSpec((None, tr, c), lambda j, i, core_ref: (j, i, 0))],
            out_specs=[out, out]),
        out_shape=[jax.ShapeDtypeStruct((N_SHARDS, r, c), F32), jax.ShapeDtypeStruct((N_SHARDS, r, c), BF16)],
        compiler_params=_params(("parallel", "parallel"), 32),
    )(core, grad, recv)


def _send_to_sibling(arrays):
    n = len(arrays)

    def copies(src, dst, sems):
        x, y, c, _ = _mesh_place()
        return [_remote(src[a], dst[a], sems[0].at[a], sems[1].at[a], (x, y, 1 - c)) for a in range(n)]

    def start(src, dst, sems):
        for cp in copies(src, dst, sems):
            cp.start()

    def finish(src, dst, sems):
        for cp in copies(src, dst, sems):
            cp.wait()

    return _Exchange(arrays, [jax.ShapeDtypeStruct(t.shape, t.dtype) for t in arrays], {},
                     [pltpu.SemaphoreType.DMA((n,))] * 2, start, finish)


def _add_pair(a, b, name):
    _, r, c = a.shape
    tr = min(r, 256)

    def body(a_ref, b_ref, o_ref, ob_ref):
        tot = a_ref[...] + b_ref[...]
        o_ref[...] = tot
        ob_ref[...] = tot.astype(BF16)

    spec = pl.BlockSpec((None, tr, c), lambda j, i: (j, i, 0))
    return _pallas(
        body, name=name, grid=(N_SHARDS, r // tr), in_specs=[spec, spec], out_specs=[spec, spec],
        out_shape=[jax.ShapeDtypeStruct(a.shape, F32), jax.ShapeDtypeStruct(a.shape, BF16)],
        compiler_params=_params(("parallel", "parallel"), 32),
    )(a, b)


def _scatter_to_chips(sums, rows=None, into=None):
    n = len(sums)

    def copies(src, dst, sems):
        x, y, c, chips = _mesh_place()
        part = (lambda ref: ref) if rows is None else (lambda ref: ref.at[pl.ds(rows[0], rows[1])])
        return [_remote(part(src[a].at[2 * cx + cy]), part(dst[a].at[k]), sems[0].at[a, k], sems[1].at[a, k],
                        (cx, cy, c))
                for a in range(n) for k, (cx, cy) in enumerate(chips)]

    def start(src, dst, sems):
        for cp in copies(src, dst, sems):
            cp.start()

    def finish(src, dst, sems):
        for cp in copies(src, dst, sems):
            cp.wait()

    return _Exchange(sums + (into or []), [jax.ShapeDtypeStruct((3,) + s.shape[1:], s.dtype) for s in sums],
                     {n + a: a for a in range(n)} if into else {},
                     [pltpu.SemaphoreType.DMA((n, 3))] * 2, start, finish)


def _add_chips(sums, recv, chip_core, name):
    _, r, c = sums.shape
    tr = min(r, 256)

    def body(cc_ref, s_ref, r_ref, o_ref):
        o_ref[...] = ((s_ref[...] + r_ref[0].astype(F32)) + r_ref[1].astype(F32)) + r_ref[2].astype(F32)

    return _pallas(
        body, name=name,
        grid_spec=pltpu.PrefetchScalarGridSpec(
            num_scalar_prefetch=1, grid=(r // tr,),
            in_specs=[pl.BlockSpec((None, tr, c), lambda i, cc_ref: (cc_ref[0], i, 0)),
                      pl.BlockSpec((3, tr, c), lambda i, cc_ref: (0, i, 0))],
            out_specs=pl.BlockSpec((None, tr, c), lambda i, cc_ref: (cc_ref[1], i, 0))),
        out_shape=jax.ShapeDtypeStruct((2, r, c), F32),
        compiler_params=_params(("parallel",), 32),
    )(chip_core, sums, recv)


def _share_with_sibling(bufs):
    n = len(bufs)

    def copies(dst, sems, half):
        x, y, c, _ = _mesh_place()
        h = c if half == "mine" else 1 - c
        return [_remote(dst[a].at[h], dst[a].at[h], sems[0].at[a], sems[1].at[a], (x, y, 1 - c)) for a in range(n)]

    def start(ins, dst, sems):
        for cp in copies(dst, sems, "mine"):
            cp.start()

    def finish(ins, dst, sems):
        for cp in copies(dst, sems, "theirs"):
            cp.wait_recv()
        for cp in copies(dst, sems, "mine"):
            cp.wait_send()

    return _Exchange(bufs, [jax.ShapeDtypeStruct(b.shape, b.dtype) for b in bufs], {a: a for a in range(n)},
                     [pltpu.SemaphoreType.DMA((n,))] * 2, start, finish)


def _adam_math(w, g, m, v):
    m = ADAM_B1 * m + (1.0 - ADAM_B1) * g
    v = ADAM_B2 * v + (1.0 - ADAM_B2) * (g * g)
    m_hat = m / (1.0 - ADAM_B1 ** ADAM_STEP)
    v_hat = v / (1.0 - ADAM_B2 ** ADAM_STEP)
    delta = -ADAM_LR * (m_hat / (jnp.sqrt(v_hat) + ADAM_EPS) + ADAM_WD * w)
    return delta, m, v


def _gather_small(small):
    def peers():
        x, y, c, _ = _mesh_place()
        return [(x ^ ((r >> 2) & 1), y ^ ((r >> 1) & 1), c ^ (r & 1)) for r in range(1, 8)], 4 * x + 2 * y + c

    def start(src, dst, sems):
        to, me = peers()
        for r, peer in enumerate(to):
            _remote(src[0], dst[0].at[me], sems[0].at[r], sems[1].at[r], peer).start()

    def finish(src, dst, sems):
        to, me = peers()
        for r, (px, py, pc) in enumerate(to):
            theirs = dst[0].at[4 * px + 2 * py + pc]
            _remote(theirs, theirs, sems[0].at[r], sems[1].at[r], (px, py, pc)).wait_recv()
        for r, peer in enumerate(to):
            _remote(src[0], dst[0].at[me], sems[0].at[r], sems[1].at[r], peer).wait_send()

    return _Exchange([small], [jax.ShapeDtypeStruct((8,) + small.shape, small.dtype)], {},
                     [pltpu.SemaphoreType.DMA((7,))] * 2, start, finish)


def _small_adamw(gathered, small, me, w_vec, m_vec, v_vec):
    n_par = w_vec.shape[1]

    def body(me_ref, a_ref, s_ref, w_ref, m_ref, v_ref, loss_ref, g_ref, d_ref, nm_ref, nv_ref):
        mine = s_ref[...]
        tot = jnp.where(me_ref[0] == 0, mine, a_ref[0])
        for d in range(1, 8):
            tot = tot + jnp.where(me_ref[0] == d, mine, a_ref[d])
        tot = jnp.sum(tot, axis=0, keepdims=True)
        sq = jnp.sum(tot[:, n_par:], axis=1, keepdims=True)
        loss_ref[...] = jnp.broadcast_to(sq * (0.5 / D_MODEL), loss_ref.shape)
        g = tot[:, :n_par]
        g_ref[...] = g
        d_ref[...], nm_ref[...], nv_ref[...] = _adam_math(w_ref[...], g, m_ref[...], v_ref[...])

    vm = pl.BlockSpec(memory_space=pltpu.VMEM)
    vec = jax.ShapeDtypeStruct((1, n_par), F32)
    return pl.pallas_call(
        body, name="small_adamw",
        grid_spec=pltpu.PrefetchScalarGridSpec(num_scalar_prefetch=1, grid=(), in_specs=[vm] * 5, out_specs=[vm] * 5),
        out_shape=[jax.ShapeDtypeStruct((1, 128), F32), vec, vec, vec, vec],
    )(me, gathered, small, w_vec, m_vec, v_vec)


def _adamw(w, g, m, v, name):
    r, c = w.shape
    tr = min(r, 256)

    def body(w_ref, g_ref, m_ref, v_ref, d_ref, nm_ref, nv_ref):
        d_ref[...], nm_ref[...], nv_ref[...] = _adam_math(w_ref[...], g_ref[...], m_ref[...], v_ref[...])

    spec = pl.BlockSpec((tr, c), lambda i: (i, 0))
    shape = jax.ShapeDtypeStruct((r, c), F32)
    return _pallas(
        body, name=name, grid=(r // tr,),
        in_specs=[spec] * 4, out_specs=[spec] * 3, out_shape=[shape] * 3,
        compiler_params=_params(("parallel",), 48),
    )(w, g, m, v)


def kernel(x, w_in, w_pool, pool_scale, w_out, ln_gain, ln_bias, loss_target, m_w_in, m_w_pool, m_pool_scale, m_w_out, m_ln_gain, m_ln_bias, v_w_in, v_w_pool, v_pool_scale, v_w_out, v_ln_gain, v_ln_bias):
    xi, yi, ci = lax.axis_index("x"), lax.axis_index("y"), lax.axis_index("c")
    chip = (2 * xi + yi).astype(jnp.int32).reshape(1)
    core = ci.astype(jnp.int32).reshape(1)
    n_groups = len(POOL_WINDOWS)
    shard_c = w_pool.shape[2]

    w_in_b = _cast_bf16(w_in[0], chip, "cast_w_in", 256)
    w_out_b = _cast_bf16(w_out[0], chip, "cast_w_out", 256)
    w_pool_b = _cast_bf16(w_pool[0].reshape(n_groups * shard_c, POOL_GROUP_DIM), chip, "cast_w_pool", 256)
    w_in_g = _run_exchange(_allgather_weights([w_in_b]), "allgather_w_in")[0]

    chip_core = jnp.concatenate([chip, core])
    g_x, full_in, full_out, full_pool, small = _step(
        x[0], loss_target[0], w_in_g, [w_out_b, w_pool_b], pool_scale, ln_gain, ln_bias, (core, chip_core))
    half_c = shard_c // 2
    grad_w_in = full_in.reshape(D_MODEL, SHARD_IN)
    grad_w_out = full_out.reshape(D_MODEL // N_SHARDS, D_MODEL)
    grad_w_pool = (full_pool.reshape(2, n_groups, half_c, POOL_GROUP_DIM).transpose(1, 0, 2, 3)
                   .reshape(n_groups * shard_c, POOL_GROUP_DIM))

    d_in, nm_in, nv_in = _adamw(w_in[0], grad_w_in, m_w_in[0], v_w_in[0], "adamw_w_in")
    d_out, nm_out, nv_out = _adamw(w_out[0], grad_w_out, m_w_out[0], v_w_out[0], "adamw_w_out")
    flat = lambda t: t[0].reshape(n_groups * shard_c, POOL_GROUP_DIM)
    d_pool, nm_pool, nv_pool = _adamw(flat(w_pool), grad_w_pool, flat(m_w_pool), flat(v_w_pool), "adamw_w_pool")

    cat = lambda a, b, c: jnp.concatenate([a, b, c], axis=1)
    me = (4 * xi + 2 * yi + ci).astype(jnp.int32).reshape(1)
    loss_v, g_vec, d_vec, nm_vec, nv_vec = _small_adamw(
        small[1], small[0], me, cat(pool_scale, ln_gain, ln_bias), cat(m_pool_scale, m_ln_gain, m_ln_bias),
        cat(v_pool_scale, v_ln_gain, v_ln_bias))

    def split(vec):
        return vec[:, :D_POOL], vec[:, D_POOL:D_POOL + D_MODEL], vec[:, D_POOL + D_MODEL:]

    g_scale, g_gain, g_bias = split(g_vec)
    d_scale, d_gain, d_bias = split(d_vec)
    nm_scale, nm_gain, nm_bias = split(nm_vec)
    nv_scale, nv_gain, nv_bias = split(nv_vec)
    pool_shape = w_pool.shape
    return (loss_v[0, 0], g_x[None],
            grad_w_in[None], grad_w_pool.reshape(pool_shape), g_scale, grad_w_out[None], g_gain, g_bias,
            d_in[None], d_pool.reshape(pool_shape), d_scale, d_out[None], d_gain, d_bias,
            nm_in[None], nm_pool.reshape(pool_shape), nm_scale, nm_out[None], nm_gain, nm_bias,
            nv_in[None], nv_pool.reshape(pool_shape), nv_scale, nv_out[None], nv_gain, nv_bias)
```

```python
import functools

import jax
import jax.numpy as jnp
from jax import lax
from jax.experimental import pallas as pl
from jax.experimental.pallas import tpu as pltpu

F32 = jnp.float32
BF16 = jnp.bfloat16
MESH = pl.DeviceIdType.MESH
ANY = pl.BlockSpec(memory_space=pl.ANY)

D_MODEL = 2048
D_ATTN = 1024
D_POOL = 1024
HEAD_DIM = 128
N_HEADS = 8
ROPE_DIM = 32
ROPE_THETA = 500000.0
DILATIONS = (1, 4, 16)
KEY_BLOCK = 128
CHUNK = 2 * KEY_BLOCK
STAT_LANES = 128
POOL_WINDOWS = (2, 4, 8, 16)
POOL_GROUP_DIM = 256
POOL_HALO = 16
D_QKV = 3 * D_ATTN
D_UG = D_POOL + D_MODEL
D_IN = D_QKV + D_UG
N_SHARDS = 4
SHARD_IN = D_IN // N_SHARDS
LN_EPS = 1e-5
DEEPNORM_ALPHA = 2.0 ** 0.25
ADAM_LR = 0.001
ADAM_B1 = 0.9
ADAM_B2 = 0.999
ADAM_EPS = 1e-08
ADAM_WD = 0.01
ADAM_STEP = 10
NEG = -1e30
MIB = 1024 * 1024


def _params(sem, vmem_mib):
    return pltpu.CompilerParams(dimension_semantics=sem, vmem_limit_bytes=vmem_mib * MIB)


def _pallas(body, **kwargs):
    pin = lambda s: pltpu.HBM(s.shape, s.dtype) if len(s.shape) >= 2 else s
    out_shape = kwargs.pop("out_shape")
    out_shape = [pin(s) for s in out_shape] if isinstance(out_shape, (list, tuple)) else pin(out_shape)
    call = pl.pallas_call(body, out_shape=out_shape, **kwargs)

    def run(*operands):
        return call(*[pltpu.with_memory_space_constraint(o, pltpu.HBM) if o.ndim >= 2 else o for o in operands])

    return run


class _Exchange:
    def __init__(self, operands, out_shape, aliases, sems, start, finish):
        self.operands, self.out_shape, self.aliases, self.sems = list(operands), list(out_shape), dict(aliases), list(sems)
        self.start, self.finish = start, finish


def _run_exchange(comm, name):
    n_in, n_out = len(comm.operands), len(comm.out_shape)

    def body(*refs):
        ins, outs, sems = refs[:n_in], refs[n_in:n_in + n_out], refs[n_in + n_out:]
        comm.start(ins, outs, sems)
        comm.finish(ins, outs, sems)

    return _pallas(
        body, name=name, in_specs=[ANY] * n_in, out_specs=[ANY] * n_out, out_shape=comm.out_shape,
        input_output_aliases=comm.aliases, scratch_shapes=comm.sems,
    )(*comm.operands)


def _call(body, *, name, grid, in_specs, out_specs, out_shape, scratch_shapes, semantics, vmem_mib, args,
          aliases=None, comm=None, prefetch=()):
    aliases = dict(aliases or {})
    n_pre, n_in, n_out, n_scr = len(prefetch), len(in_specs), len(out_specs), len(scratch_shapes)
    c_in, c_out = (len(comm.operands), len(comm.out_shape)) if comm else (0, 0)
    c_shapes, c_sems, c_operands = (comm.out_shape, comm.sems, comm.operands) if comm else ([], [], [])

    def hosted(*refs):
        pre, refs = refs[:n_pre], refs[n_pre:]
        a = n_in
        b = a + c_in
        c = b + n_out
        d = c + c_out
        e = d + n_scr
        if comm is None:
            body(*pre, *refs)
            return
        ids = [pl.program_id(k) for k in range(len(grid))]
        first = functools.reduce(jnp.logical_and, [i == 0 for i in ids])
        last = functools.reduce(jnp.logical_and, [i == g - 1 for i, g in zip(ids, grid)])

        @pl.when(first)
        def _():
            comm.start(refs[a:b], refs[c:d], refs[e:])

        body(*pre, *refs[:a], *refs[b:c], *refs[d:e])

        @pl.when(last)
        def _():
            comm.finish(refs[a:b], refs[c:d], refs[e:])

    if comm:
        semantics = ("arbitrary",) * len(grid)
        for i, o in comm.aliases.items():
            aliases[n_pre + n_in + i] = n_out + o
    outs = _pallas(
        hosted, name=name,
        grid_spec=pltpu.PrefetchScalarGridSpec(
            num_scalar_prefetch=n_pre, grid=grid, in_specs=list(in_specs) + [ANY] * c_in,
            out_specs=list(out_specs) + [ANY] * c_out, scratch_shapes=list(scratch_shapes) + c_sems),
        out_shape=list(out_shape) + c_shapes, input_output_aliases=aliases,
        compiler_params=_params(semantics, vmem_mib),
    )(*prefetch, *args, *c_operands)
    return list(outs[:n_out]), list(outs[n_out:])


def _dot_nn(a, b):
    return jnp.dot(a, b, preferred_element_type=F32)


def _dot_nt(a, b):
    return lax.dot_general(a, b, (((1,), (1,)), ((), ())), preferred_element_type=F32)


def _dot_tn(a, b):
    return lax.dot_general(a, b, (((0,), (0,)), ((), ())), preferred_element_type=F32)


def _fold_rows(a):
    r, c = a.shape
    return jnp.sum(a.reshape(r // 8, 8, c), axis=0)


def _cast_bf16(a, chip, name, rows):
    r, c = a.shape

    def body(chip_ref, a_ref, o_ref):
        o_ref[...] = a_ref[...].astype(BF16)

    return _pallas(
        body, name=name,
        grid_spec=pltpu.PrefetchScalarGridSpec(
            num_scalar_prefetch=1, grid=(r // rows,),
            in_specs=[pl.BlockSpec((rows, c), lambda i, chip_ref: (i, 0))],
            out_specs=pl.BlockSpec((None, rows, c), lambda i, chip_ref: (chip_ref[0], i, 0))),
        out_shape=jax.ShapeDtypeStruct((N_SHARDS, r, c), BF16),
        compiler_params=_params(("parallel",), 32),
    )(chip, a)


def _mesh_place():
    x, y, c = lax.axis_index("x"), lax.axis_index("y"), lax.axis_index("c")
    return x, y, c, [(1 - x, y), (x, 1 - y), (1 - x, 1 - y)]


def _remote(src, dst, send_sem, recv_sem, to):
    return pltpu.make_async_remote_copy(src_ref=src, dst_ref=dst, send_sem=send_sem, recv_sem=recv_sem,
                                        device_id=to, device_id_type=MESH)


def _allgather_weights(bufs):
    n = len(bufs)

    def half(a, core):
        rows = bufs[a].shape[1] // 2
        return pl.ds(core * rows, rows)

    DIAGONAL = 2

    def to_neighbours(dst, sems):
        x, y, c, chips = _mesh_place()
        own = lambda a: dst[a].at[2 * x + y, half(a, c)]
        return [_remote(own(a), own(a), sems[0].at[a, k], sems[1].at[a, k], (cx, cy, c))
                for a in range(n) for k, (cx, cy) in enumerate(chips[:DIAGONAL])]

    def relayed(dst, sems):
        x, y, c, _ = _mesh_place()
        owner = 2 * (x ^ (1 - c)) + (y ^ c)
        piece = lambda a: dst[a].at[owner, half(a, c)]
        return [_remote(piece(a), piece(a), sems[0].at[a, DIAGONAL], sems[1].at[a, DIAGONAL], (x ^ c, y ^ (1 - c), c))
                for a in range(n)]

    def start(ins, dst, sems):
        for cp in to_neighbours(dst, sems):
            cp.start()

    def finish(ins, dst, sems):
        x, y, c, chips = _mesh_place()
        sibling = (x, y, 1 - c)
        passed_on = []

        def landed_then_pass_on(k):
            cx, cy = chips[k]
            for a in range(n):
                landed = dst[a].at[2 * cx + cy, half(a, c)]
                _remote(landed, landed, sems[0].at[a, k], sems[1].at[a, k], (cx, cy, c)).wait_recv()
                cp = _remote(landed, landed, sems[2].at[a, k], sems[3].at[a, k], sibling)
                cp.start()
                passed_on.append(cp)

        for k in range(DIAGONAL):
            landed_then_pass_on(k)
        for cp in relayed(dst, sems):
            cp.start()
        landed_then_pass_on(DIAGONAL)
        for k, (cx, cy) in enumerate(chips):
            for a in range(n):
                passed = dst[a].at[2 * cx + cy, half(a, 1 - c)]
                _remote(passed, passed, sems[2].at[a, k], sems[3].at[a, k], sibling).wait_recv()
        for cp in to_neighbours(dst, sems) + relayed(dst, sems) + passed_on:
            cp.wait_send()

    return _Exchange(bufs, [jax.ShapeDtypeStruct(b.shape, b.dtype) for b in bufs], {a: a for a in range(n)},
                     [pltpu.SemaphoreType.DMA((n, 3))] * 4, start, finish)


def _rope_tables(seq):
    half = ROPE_DIM // 2
    inv_freq = ROPE_THETA ** (-(2.0 * jnp.arange(half, dtype=F32)) / ROPE_DIM)
    ang = jnp.arange(seq, dtype=jnp.int32).astype(F32)[:, None] * inv_freq[None, :]
    cos, sin = jnp.cos(ang), jnp.sin(ang)
    pad = jnp.zeros((seq, HEAD_DIM - ROPE_DIM), F32)
    zeros = jnp.zeros((seq, half), F32)
    c_tab = jnp.concatenate([cos, cos, pad + 1.0], axis=1)
    up_tab = jnp.concatenate([-sin, zeros, pad], axis=1)
    down_tab = jnp.concatenate([zeros, sin, pad], axis=1)
    return c_tab, up_tab, down_tab


def _rotate_heads(t, c_tab, up_tab, down_tab):
    outs = []
    for h in range(t.shape[1] // HEAD_DIM):
        th = t[:, h * HEAD_DIM:(h + 1) * HEAD_DIM]
        up = pltpu.roll(th, HEAD_DIM - ROPE_DIM // 2, axis=1)
        down = pltpu.roll(th, ROPE_DIM // 2, axis=1)
        outs.append(th * c_tab + up * up_tab + down * down_tab)
    return outs[0] if len(outs) == 1 else jnp.concatenate(outs, axis=1)


def _to_pattern(slabs_ref, dst_ref, dil, dtype):
    n_slabs, rows, _ = slabs_ref.shape
    for s in range(n_slabs):
        for r in range(dil):
            dst_ref[r, :, s * 128:(s + 1) * 128] = slabs_ref[s, pl.ds(r, rows // dil, dil), :].astype(dtype)


def _from_pattern(src_ref, slabs_ref, dil):
    n_slabs, rows, _ = slabs_ref.shape
    for s in range(n_slabs):
        for r in range(dil):
            slabs_ref[s, pl.ds(r, rows // dil, dil), :] = src_ref[r, :, s * 128:(s + 1) * 128].astype(F32)


def _store_slabs(slabs_ref, value):
    for s in range(slabs_ref.shape[0]):
        slabs_ref[s] = value[:, s * 128:(s + 1) * 128]


def _in_proj_qkv(x, w_in_g, tabs, comm=None):
    seq = x.shape[0]
    tm, tn = 512, SHARD_IN
    heads = tn // HEAD_DIM
    k_heads_in_second = 2 * D_ATTN // HEAD_DIM - heads
    d4, d16 = DILATIONS[1], DILATIONS[2]

    def body(x_ref, w_ref, c_ref, up_ref, down_ref, o1_ref, o4_ref, o16_ref, res_ref):
        shard = pl.program_id(0)
        xb = x_ref[...].astype(BF16)
        group = 4 * HEAD_DIM
        accs = [_dot_nn(xb, w_ref[:, g * group:(g + 1) * group]) for g in range(tn // group)]

        plain = shard == 1
        c_plain = jnp.where(plain, 1.0, c_ref[...])
        up_plain = jnp.where(plain, 0.0, up_ref[...])
        down_plain = jnp.where(plain, 0.0, down_ref[...])
        for h in range(heads):
            lanes = (h * HEAD_DIM) % group
            th = accs[h * HEAD_DIM // group][:, lanes:lanes + HEAD_DIM]
            if h < k_heads_in_second:
                th = _rotate_heads(th, c_ref[...], up_ref[...], down_ref[...])
            else:
                th = _rotate_heads(th, c_plain, up_plain, down_plain)
            res_ref[h] = th
            o1_ref[:, h * HEAD_DIM:(h + 1) * HEAD_DIM] = th.astype(BF16)
        _to_pattern(res_ref, o4_ref, d4, BF16)
        _to_pattern(res_ref, o16_ref, d16, BF16)

    tab_spec = pl.BlockSpec((tm, HEAD_DIM), lambda s, i: (i, 0))
    (o1, o4, o16), exchanged = _call(
        body, name="in_proj_qkv", grid=(D_QKV // tn, seq // tm),
        in_specs=[pl.BlockSpec((tm, D_MODEL), lambda s, i: (i, 0)),
                  pl.BlockSpec((None, D_MODEL, tn), lambda s, i: (s, 0, 0)),
                  tab_spec, tab_spec, tab_spec],
        out_specs=[pl.BlockSpec((tm, tn), lambda s, i: (i, s)),
                   pl.BlockSpec((d4, tm // d4, tn), lambda s, i: (0, i, s)),
                   pl.BlockSpec((d16, tm // d16, tn), lambda s, i: (0, i, s))],
        out_shape=[jax.ShapeDtypeStruct((seq, D_QKV), BF16),
                   jax.ShapeDtypeStruct((d4, seq // d4, D_QKV), BF16),
                   jax.ShapeDtypeStruct((d16, seq // d16, D_QKV), BF16)],
        scratch_shapes=[pltpu.VMEM((heads, tm, HEAD_DIM), F32)],
        semantics=("parallel", "parallel"), vmem_mib=52, args=(x, w_in_g, *tabs), comm=comm)
    return [o1[None], o4, o16], exchanged


def _in_proj_pool_gate(x, w_in_g):
    seq = x.shape[0]
    tm, tn = 512, SHARD_IN
    first_shard = D_QKV // tn

    def body(x_ref, w_ref, o_ref):
        o_ref[...] = _dot_nn(x_ref[...].astype(BF16), w_ref[...])

    return _pallas(
        body, name="in_proj_pool_gate", grid=(D_UG // tn, seq // tm),
        in_specs=[pl.BlockSpec((tm, D_MODEL), lambda s, i: (i, 0)),
                  pl.BlockSpec((None, D_MODEL, tn), lambda s, i: (s + first_shard, 0, 0))],
        out_specs=pl.BlockSpec((tm, tn), lambda s, i: (i, s)),
        out_shape=jax.ShapeDtypeStruct((seq, D_UG), F32),
        compiler_params=_params(("parallel", "parallel"), 48),
    )(x, w_in_g)


def _band_masks():
    row = lax.broadcasted_iota(jnp.int32, (KEY_BLOCK, KEY_BLOCK), 0)
    col = lax.broadcasted_iota(jnp.int32, (KEY_BLOCK, KEY_BLOCK), 1)
    return col <= row, col >= row


def _attn_fwd(qkv, name):
    dil, n, _ = qkv.shape
    scale = HEAD_DIM ** -0.5
    lo, hi = slice(0, KEY_BLOCK), slice(KEY_BLOCK, CHUNK)

    def body(q_ref, k_ref, v_ref, kb_ref, vb_ref, o_ref, st_ref):
        i = pl.program_id(1)
        cur_mask, prev_mask = _band_masks()
        before_mask = jnp.logical_and(prev_mask, i > 0)
        lane = lax.broadcasted_iota(jnp.int32, (KEY_BLOCK, STAT_LANES), 1)
        tasks = [(rows, h) for rows in (lo, hi) for h in range(N_HEADS)]
        head = lambda h: slice(h * HEAD_DIM, (h + 1) * HEAD_DIM)

        def prev_of(rows, h):
            if rows is lo:
                return kb_ref[:, head(h)], vb_ref[:, head(h)], before_mask
            return k_ref[lo, head(h)], v_ref[lo, head(h)], prev_mask

        scores = []
        for rows, h in tasks:
            q = q_ref[rows, head(h)]
            scores.append((_dot_nt(q, prev_of(rows, h)[0]), _dot_nt(q, k_ref[rows, head(h)])))
        probs = []
        for (rows, h), (qk_prev, qk_cur) in zip(tasks, scores):
            s_prev = jnp.where(prev_of(rows, h)[2], qk_prev * scale, NEG)
            s_cur = jnp.where(cur_mask, qk_cur * scale, NEG)
            m = jnp.max(jnp.maximum(s_prev, s_cur), axis=-1, keepdims=True)
            p_prev = jnp.exp(s_prev - m)
            p_cur = jnp.exp(s_cur - m)
            den = jnp.sum(p_prev + p_cur, axis=-1, keepdims=True)
            probs.append((p_prev.astype(BF16), p_cur.astype(BF16), den, m + jnp.log(den)))
        stats = [jnp.zeros((KEY_BLOCK, STAT_LANES), F32), jnp.zeros((KEY_BLOCK, STAT_LANES), F32)]
        for (rows, h), (p_prev, p_cur, den, lse) in zip(tasks, probs):
            o = _dot_nn(p_cur, v_ref[rows, head(h)]) + _dot_nn(p_prev, prev_of(rows, h)[1])
            o_ref[rows, head(h)] = (o / den).astype(BF16)
            b = 0 if rows is lo else 1
            stats[b] = jnp.where(lane == h, lse, stats[b])
        st_ref[lo, :] = stats[0]
        st_ref[hi, :] = stats[1]

    main = lambda cb: pl.BlockSpec((None, CHUNK, D_ATTN), lambda r, i: (r, i, cb))
    before = lambda cb: pl.BlockSpec((None, KEY_BLOCK, D_ATTN), lambda r, i: (r, jnp.maximum(2 * i - 1, 0), cb))
    return _pallas(
        body, name=name, grid=(dil, n // CHUNK),
        in_specs=[main(0), main(1), main(2), before(1), before(2)],
        out_specs=[main(0), pl.BlockSpec((None, CHUNK, STAT_LANES), lambda r, i: (r, i, 0))],
        out_shape=[jax.ShapeDtypeStruct((dil, n, D_ATTN), BF16), jax.ShapeDtypeStruct((dil, n, STAT_LANES), F32)],
        compiler_params=_params(("parallel", "parallel"), 40),
    )(qkv, qkv, qkv, qkv, qkv)


def _attn_bwd(qkv, do, stats, name, comm=None):
    dil, n, _ = qkv.shape
    n_blocks = n // KEY_BLOCK
    last = n // CHUNK - 1
    scale = HEAD_DIM ** -0.5
    lo, hi = slice(0, KEY_BLOCK), slice(KEY_BLOCK, CHUNK)

    def body(q_ref, k_ref, v_ref, kb_ref, vb_ref, qa_ref, do_ref, doa_ref, st_ref, sta_ref, dq_ref, dk_ref, dv_ref):
        i = pl.program_id(1)
        cur_mask, prev_mask = _band_masks()
        before_mask = jnp.logical_and(prev_mask, i > 0)
        after_mask = jnp.logical_and(prev_mask, i < last)

        rows_cat = lambda a, b: jnp.concatenate([a, b], axis=0)
        masks = (jnp.concatenate([before_mask, cur_mask], axis=1), jnp.concatenate([prev_mask, cur_mask], axis=1),
                 after_mask)

        def operands(h):
            cols = slice(h * HEAD_DIM, (h + 1) * HEAD_DIM)
            lse_c, del_c = slice(h, h + 1), slice(N_HEADS + h, N_HEADS + h + 1)
            q = (q_ref[lo, cols], q_ref[hi, cols], qa_ref[:, cols])
            do = (do_ref[lo, cols], do_ref[hi, cols], doa_ref[:, cols])
            keys = (rows_cat(kb_ref[:, cols], k_ref[lo, cols]), k_ref[:, cols], k_ref[hi, cols])
            vals = (rows_cat(vb_ref[:, cols], v_ref[lo, cols]), v_ref[:, cols], v_ref[hi, cols])
            st = ((st_ref[lo, lse_c], st_ref[lo, del_c]), (st_ref[hi, lse_c], st_ref[hi, del_c]),
                  (sta_ref[:, lse_c], sta_ref[:, del_c]))
            return cols, q, do, keys, vals, st

        group = N_HEADS // 2
        for first_head in range(0, N_HEADS, group):
            heads = range(first_head, first_head + group)
            raw = {}
            for h in heads:
                _, q, do, keys, vals, _ = operands(h)
                raw[h] = [(_dot_nt(q[j], keys[j]), _dot_nt(do[j], vals[j])) for j in range(3)]
            grads = {}
            for h in heads:
                st = operands(h)[5]
                grads[h] = []
                for j in range(3):
                    qk, dp = raw[h][j]
                    lse, delta = st[j]
                    p = jnp.exp(jnp.where(masks[j], qk * scale, NEG) - lse)
                    grads[h].append((p.astype(BF16), (p * (dp - delta) * scale).astype(BF16)))
            for h in heads:
                cols, q, do, keys, _, _ = operands(h)
                (p0, ds0), (p1, ds1), (pa, dsa) = grads[h]
                own, nxt = slice(KEY_BLOCK, CHUNK), slice(0, KEY_BLOCK)

                def put(ref, rows, val, cols=cols):
                    ref[rows, cols] = val.astype(ref.dtype)

                put(dq_ref, lo, _dot_nn(ds0, keys[0]))
                put(dq_ref, hi, _dot_nn(ds1, keys[1]))
                put(dk_ref, lo, _dot_tn(rows_cat(ds0[:, own], ds1[:, nxt]), q_ref[:, cols]))
                put(dk_ref, hi, _dot_tn(rows_cat(ds1[:, own], dsa), rows_cat(q[1], q[2])))
                put(dv_ref, lo, _dot_tn(rows_cat(p0[:, own], p1[:, nxt]), do_ref[:, cols]))
                put(dv_ref, hi, _dot_tn(rows_cat(p1[:, own], pa), rows_cat(do[1], do[2])))

    def spec(rows, width, row_of, cb):
        return pl.BlockSpec((None, rows, width), lambda r, i: (r, row_of(i), cb))

    same = lambda i: i
    before = lambda i: jnp.maximum(2 * i - 1, 0)
    after = lambda i: jnp.minimum(2 * i + 2, n_blocks - 1)
    out = spec(CHUNK, D_ATTN, same, 0)
    return _call(
        body, name=name, grid=(dil, n // CHUNK),
        in_specs=[spec(CHUNK, D_ATTN, same, 0), spec(CHUNK, D_ATTN, same, 1), spec(CHUNK, D_ATTN, same, 2),
                  spec(KEY_BLOCK, D_ATTN, before, 1), spec(KEY_BLOCK, D_ATTN, before, 2),
                  spec(KEY_BLOCK, D_ATTN, after, 0),
                  spec(CHUNK, D_ATTN, same, 0), spec(KEY_BLOCK, D_ATTN, after, 0),
                  spec(CHUNK, STAT_LANES, same, 0), spec(KEY_BLOCK, STAT_LANES, after, 0)],
        out_specs=[out, out, out],
        out_shape=[jax.ShapeDtypeStruct((dil, n, D_ATTN), BF16)] * 3,
        scratch_shapes=[], semantics=("parallel", "parallel"), vmem_mib=40,
        args=(qkv, qkv, qkv, qkv, qkv, qkv, do, do, stats, stats), comm=comm)


def _window_sums(ext, window, backward):
    rows = ext.shape[0]
    acc, span = ext, 1
    while span < window:
        acc = acc + pltpu.roll(acc, (rows - span) if backward else span, axis=0)
        span *= 2
    return acc


def _mix_gate(o_list, st_list, hug, w_pool_g, pool_scale):
    seq = hug.shape[0]
    tm = 256
    halo_blocks = tm // POOL_HALO
    d4, d16 = DILATIONS[1], DILATIONS[2]

    def body(o1_ref, o4_ref, o16_ref, l1_ref, l4_ref, l16_ref, u_ref, halo_ref, ga_ref, gp_ref, wp_ref, sc_ref,
             y_ref, mix_ref, lse_ref, pooled_ref, n4_ref, n16_ref, nl4_ref, nl16_ref):
        i = pl.program_id(0)
        _from_pattern(o4_ref, n4_ref, d4)
        _from_pattern(o16_ref, n16_ref, d16)
        _from_pattern(l4_ref, nl4_ref, d4)
        _from_pattern(l16_ref, nl16_ref, d16)
        la, lb, lc = l1_ref[...], nl4_ref[0], nl16_ref[0]
        mx = jnp.maximum(jnp.maximum(la, lb), lc)
        ea, eb, ec = jnp.exp(la - mx), jnp.exp(lb - mx), jnp.exp(lc - mx)
        tot = ea + eb + ec
        lse_ref[...] = mx + jnp.log(tot)
        wa, wb, wc = ea / tot, eb / tot, ec / tot
        ga = ga_ref[...]
        silu_a = ga * jax.nn.sigmoid(ga)
        for h in range(N_HEADS):
            cols = slice(h * HEAD_DIM, (h + 1) * HEAD_DIM)
            hc = slice(h, h + 1)
            attn = wa[:, hc] * o1_ref[:, cols].astype(F32) + wb[:, hc] * n4_ref[h] + wc[:, hc] * n16_ref[h]
            mix_ref[:, cols] = attn
            y_ref[:, cols] = (attn * silu_a[:, cols]).astype(BF16)

        u = u_ref[...]
        halo = jnp.where(i > 0, halo_ref[...], 0.0)
        ext = jnp.concatenate([halo, u], axis=0)
        pos = i * tm + lax.broadcasted_iota(jnp.int32, (tm, 1), 0)
        gp = gp_ref[...]
        gated_scale = sc_ref[...] * (gp * jax.nn.sigmoid(gp))
        for g, window in enumerate(POOL_WINDOWS):
            cols = slice(g * POOL_GROUP_DIM, (g + 1) * POOL_GROUP_DIM)
            sums = _window_sums(ext[:, cols], window, backward=False)[POOL_HALO:, :]
            count = jnp.minimum(pos + 1, window).astype(F32)
            pooled = (sums / count - u[:, cols]).astype(BF16)
            pooled_ref[:, cols] = pooled
            pre = _dot_nn(pooled, wp_ref[g])
            out_cols = slice(D_ATTN + g * POOL_GROUP_DIM, D_ATTN + (g + 1) * POOL_GROUP_DIM)
            mix_ref[:, out_cols] = pre
            y_ref[:, out_cols] = (pre * gated_scale[:, cols]).astype(BF16)

    row = lambda width, cb=0: pl.BlockSpec((tm, width), lambda i: (i, cb))
    pat = lambda d, width: pl.BlockSpec((d, tm // d, width), lambda i: (0, i, 0))
    return _pallas(
        body, name="mix_gate", grid=(seq // tm,),
        in_specs=[row(D_ATTN), pat(d4, D_ATTN), pat(d16, D_ATTN),
                  row(STAT_LANES), pat(d4, STAT_LANES), pat(d16, STAT_LANES),
                  row(D_POOL),
                  pl.BlockSpec((POOL_HALO, D_POOL), lambda i: (jnp.maximum(i * halo_blocks - 1, 0), 0)),
                  row(D_ATTN, 1), row(D_POOL, 2),
                  pl.BlockSpec((len(POOL_WINDOWS), POOL_GROUP_DIM, POOL_GROUP_DIM), lambda i: (0, 0, 0)),
                  pl.BlockSpec((1, D_POOL), lambda i: (0, 0))],
        out_specs=[row(D_MODEL), row(D_MODEL), row(STAT_LANES), row(D_POOL)],
        out_shape=[jax.ShapeDtypeStruct((seq, D_MODEL), BF16), jax.ShapeDtypeStruct((seq, D_MODEL), F32),
                   jax.ShapeDtypeStruct((seq, STAT_LANES), F32), jax.ShapeDtypeStruct((seq, D_POOL), BF16)],
        scratch_shapes=[pltpu.VMEM((N_HEADS, tm, HEAD_DIM), F32), pltpu.VMEM((N_HEADS, tm, HEAD_DIM), F32),
                        pltpu.VMEM((1, tm, STAT_LANES), F32), pltpu.VMEM((1, tm, STAT_LANES), F32)],
        compiler_params=_params(("parallel",), 48),
    )(o_list[0][0], o_list[1], o_list[2], st_list[0][0], st_list[1], st_list[2],
      hug, hug, hug, hug, w_pool_g, pool_scale)


def _out_proj_loss(y, w_out_g, x, target, gain, bias):
    seq = x.shape[0]
    tm = 512

    def body(y_ref, w_ref, x_ref, t_ref, g_ref, b_ref, dz_ref, dzb_ref, gg_ref, gb_ref, loss_ref):
        @pl.when(pl.program_id(0) == 0)
        def _():
            gg_ref[...] = jnp.zeros_like(gg_ref)
            gb_ref[...] = jnp.zeros_like(gb_ref)
            loss_ref[...] = jnp.zeros_like(loss_ref)

        halves = [slice(0, tm // 2), slice(tm // 2, tm)]
        projected = [_dot_nn(y_ref[rows, :], w_ref[...]) for rows in halves]
        for rows, out in zip(halves, projected):
            z = DEEPNORM_ALPHA * x_ref[rows, :] + out
            mu = jnp.mean(z, axis=-1, keepdims=True)
            zc = z - mu
            rstd = lax.rsqrt(jnp.mean(zc * zc, axis=-1, keepdims=True) + LN_EPS)
            xhat = zc * rstd
            gain_v = g_ref[...]
            diff = xhat * gain_v + b_ref[...] - t_ref[rows, :]
            sq = _fold_rows(diff * diff)
            part = sq[:, :128]
            for k in range(1, D_MODEL // 128):
                part = part + sq[:, k * 128:(k + 1) * 128]
            loss_ref[...] += part
            dln = diff * (1.0 / D_MODEL)
            gg_ref[...] += _fold_rows(dln * xhat)
            gb_ref[...] += _fold_rows(dln)
            dxhat = dln * gain_v
            dz = rstd * (dxhat - jnp.mean(dxhat, axis=-1, keepdims=True)
                         - xhat * jnp.mean(dxhat * xhat, axis=-1, keepdims=True))
            dz_ref[rows, :] = dz
            dzb_ref[rows, :] = dz.astype(BF16)

    row = lambda: pl.BlockSpec((tm, D_MODEL), lambda i: (i, 0))
    vec = lambda: pl.BlockSpec((1, D_MODEL), lambda i: (0, 0))
    acc = lambda width: pl.BlockSpec((8, width), lambda i: (0, 0))
    return _pallas(
        body, name="out_proj_loss", grid=(seq // tm,),
        in_specs=[row(), pl.BlockSpec((D_MODEL, D_MODEL), lambda i: (0, 0), pipeline_mode=pl.Buffered(1)),
                  row(), row(), vec(), vec()],
        out_specs=[row(), row(), acc(D_MODEL), acc(D_MODEL), acc(128)],
        out_shape=[jax.ShapeDtypeStruct((seq, D_MODEL), F32), jax.ShapeDtypeStruct((seq, D_MODEL), BF16),
                   jax.ShapeDtypeStruct((8, D_MODEL), F32), jax.ShapeDtypeStruct((8, D_MODEL), F32),
                   jax.ShapeDtypeStruct((8, 128), F32)],
        compiler_params=_params(("arbitrary",), 56),
    )(y, w_out_g.reshape(D_MODEL, D_MODEL), x, target, gain, bias)


def _dy_gate_bwd(dzb, w_out_g, hug, mixpre, pool_scale, lse_all):
    seq = dzb.shape[0]
    tm = 256
    d4, d16 = DILATIONS[1], DILATIONS[2]

    def body(dz_ref, w_ref, ga_ref, gp_ref, mix_ref, sc_ref, lse_ref,
             dh_ref, dpo_ref, do1_ref, do4_ref, do16_ref, st1_ref, st4_ref, st16_ref, da_ref, st_ref):
        dy = _dot_nt(dz_ref[...], w_ref[...])
        ga = ga_ref[...]
        sig = jax.nn.sigmoid(ga)
        attn = mix_ref[:, :D_ATTN]
        dya = dy[:, :D_ATTN]
        dattn = dya * (ga * sig)
        dh_ref[:, :D_ATTN] = (dya * attn * (sig * (1.0 + ga * (1.0 - sig)))).astype(BF16)
        _store_slabs(da_ref, dattn)
        lane = lax.broadcasted_iota(jnp.int32, (tm, STAT_LANES), 1)
        stats = lse_ref[...]
        prod = dattn * attn
        for h in range(N_HEADS):
            delta = jnp.sum(prod[:, h * HEAD_DIM:(h + 1) * HEAD_DIM], axis=-1, keepdims=True)
            stats = jnp.where(lane == N_HEADS + h, delta, stats)
        st_ref[0] = stats
        do1_ref[...] = dattn.astype(BF16)
        st1_ref[...] = stats
        _to_pattern(da_ref, do4_ref, d4, BF16)
        _to_pattern(da_ref, do16_ref, d16, BF16)
        _to_pattern(st_ref, st4_ref, d4, F32)
        _to_pattern(st_ref, st16_ref, d16, F32)

        gp = gp_ref[...]
        sig = jax.nn.sigmoid(gp)
        dyp = dy[:, D_ATTN:]
        dpo_ref[...] = dyp * (gp * sig)
        dh_ref[:, D_ATTN:] = (dyp * (mix_ref[:, D_ATTN:] * sc_ref[...])
                              * (sig * (1.0 + gp * (1.0 - sig)))).astype(BF16)

    row = lambda width, cb=0: pl.BlockSpec((tm, width), lambda i: (i, cb))
    pat = lambda d, width: pl.BlockSpec((d, tm // d, width), lambda i: (0, i, 0))
    pat_shape = lambda d, width, dtype: jax.ShapeDtypeStruct((d, seq // d, width), dtype)
    outs = _pallas(
        body, name="dy_gate_bwd", grid=(seq // tm,),
        in_specs=[row(D_MODEL), pl.BlockSpec((D_MODEL, D_MODEL), lambda i: (0, 0)),
                  row(D_ATTN, 1), row(D_POOL, 2), row(D_MODEL), pl.BlockSpec((1, D_POOL), lambda i: (0, 0)),
                  row(STAT_LANES)],
        out_specs=[row(D_MODEL, D_IN // D_MODEL - 1), row(D_POOL),
                   row(D_ATTN), pat(d4, D_ATTN), pat(d16, D_ATTN),
                   row(STAT_LANES), pat(d4, STAT_LANES), pat(d16, STAT_LANES)],
        out_shape=[jax.ShapeDtypeStruct((seq, D_IN), BF16), jax.ShapeDtypeStruct((seq, D_POOL), F32),
                   jax.ShapeDtypeStruct((seq, D_ATTN), BF16), pat_shape(d4, D_ATTN, BF16), pat_shape(d16, D_ATTN, BF16),
                   jax.ShapeDtypeStruct((seq, STAT_LANES), F32), pat_shape(d4, STAT_LANES, F32),
                   pat_shape(d16, STAT_LANES, F32)],
        scratch_shapes=[pltpu.VMEM((N_HEADS, tm, HEAD_DIM), F32), pltpu.VMEM((1, tm, STAT_LANES), F32)],
        compiler_params=_params(("parallel",), 48),
    )(dzb, w_out_g.reshape(D_MODEL, D_MODEL), hug, hug, mixpre, pool_scale, lse_all)
    dh, dpo, do1, do4, do16, st1, st4, st16 = outs
    return dh, dpo, [do1[None], do4, do16], [st1[None], st4, st16]


def _pool_bwd(dh, dpo, mixpre, pooled, w_pool_g, pool_scale):
    seq = dpo.shape[0]
    tm = 256
    halo_blocks = tm // POOL_HALO
    last = seq // tm - 1
    n_groups = len(POOL_WINDOWS)

    def body(dh_in_ref, dpo_ref, halo_ref, pre_ref, pooled_ref, wp_ref, sc_ref, du_ref, gw_ref, gs_ref):
        i = pl.program_id(0)

        @pl.when(i == 0)
        def _():
            gw_ref[...] = jnp.zeros_like(gw_ref)
            gs_ref[...] = jnp.zeros_like(gs_ref)

        dpo = dpo_ref[...]
        scale = sc_ref[...]
        gs_ref[...] += _fold_rows(dpo * pre_ref[...])
        halo = jnp.where(i < last, halo_ref[...], 0.0)
        dpw = (jnp.concatenate([dpo, halo], axis=0) * scale).astype(BF16)
        pos = i * tm + lax.broadcasted_iota(jnp.int32, (tm + POOL_HALO, 1), 0)
        for g, window in enumerate(POOL_WINDOWS):
            cols = slice(g * POOL_GROUP_DIM, (g + 1) * POOL_GROUP_DIM)
            dpw_g = dpw[:, cols]
            gw_ref[g] += _dot_tn(pooled_ref[:, cols], dpw_g[:tm, :])
            dpooled = _dot_nt(dpw_g, wp_ref[g])
            count = jnp.minimum(pos + 1, window).astype(F32)
            sums = _window_sums(dpooled / count, window, backward=True)
            du_ref[:, cols] = (sums[:tm, :] - dpooled[:tm, :]).astype(BF16)

    row = lambda width, cb=0: pl.BlockSpec((tm, width), lambda i: (i, cb))
    return _pallas(
        body, name="pool_bwd", grid=(seq // tm,),
        in_specs=[ANY, row(D_POOL),
                  pl.BlockSpec((POOL_HALO, D_POOL),
                               lambda i: (jnp.minimum((i + 1) * halo_blocks, seq // POOL_HALO - 1), 0)),
                  row(D_POOL, 1), row(D_POOL),
                  pl.BlockSpec((n_groups, POOL_GROUP_DIM, POOL_GROUP_DIM), lambda i: (0, 0, 0)),
                  pl.BlockSpec((1, D_POOL), lambda i: (0, 0))],
        out_specs=[row(D_POOL, D_QKV // D_POOL),
                   pl.BlockSpec((n_groups, POOL_GROUP_DIM, POOL_GROUP_DIM), lambda i: (0, 0, 0)),
                   pl.BlockSpec((8, D_POOL), lambda i: (0, 0))],
        out_shape=[jax.ShapeDtypeStruct(dh.shape, dh.dtype),
                   jax.ShapeDtypeStruct((n_groups, POOL_GROUP_DIM, POOL_GROUP_DIM), F32),
                   jax.ShapeDtypeStruct((8, D_POOL), F32)],
        input_output_aliases={0: 0},
        compiler_params=_params(("arbitrary",), 40),
    )(dh, dpo, dpo, mixpre, pooled, w_pool_g, pool_scale)


def _sum_patterns(dh, parts, tabs, unrotate, col_block, name, comm=None):
    seq = dh.shape[0]
    tm, tn = 256, D_ATTN
    per = D_ATTN // tn
    d4, d16 = DILATIONS[1], DILATIONS[2]

    def body(dh_in_ref, a1_ref, a4_ref, a16_ref, ct_ref, up_ref, down_ref, o_ref, n4_ref, n16_ref):
        _from_pattern(a4_ref, n4_ref, d4)
        _from_pattern(a16_ref, n16_ref, d16)
        for s in range(tn // HEAD_DIM):
            cols = slice(s * HEAD_DIM, (s + 1) * HEAD_DIM)
            tot = a1_ref[:, cols].astype(F32) + n4_ref[s] + n16_ref[s]
            if unrotate:
                tot = _rotate_heads(tot, ct_ref[...], -up_ref[...], -down_ref[...])
            o_ref[:, cols] = tot.astype(BF16)

    tab = pl.BlockSpec((tm, HEAD_DIM), lambda i, j: (i, 0))
    pat = lambda d: pl.BlockSpec((d, tm // d, tn), lambda i, j: (0, i, j))
    (dh,), exchanged = _call(
        body, name=name, grid=(seq // tm, per),
        in_specs=[ANY, pl.BlockSpec((tm, tn), lambda i, j: (i, j)), pat(d4), pat(d16), tab, tab, tab],
        out_specs=[pl.BlockSpec((tm, tn), lambda i, j: (i, col_block * per + j))],
        out_shape=[jax.ShapeDtypeStruct(dh.shape, dh.dtype)],
        scratch_shapes=[pltpu.VMEM((tn // HEAD_DIM, tm, HEAD_DIM), F32), pltpu.VMEM((tn // HEAD_DIM, tm, HEAD_DIM), F32)],
        semantics=("parallel", "parallel"), vmem_mib=32, args=(dh, parts[0][0], parts[1], parts[2], *tabs),
        aliases={0: 0}, comm=comm)
    return dh, exchanged


def _grad_w_in(x, dh, half, name, comm=None):
    seq = x.shape[0]
    ts, td, te = 2048, D_MODEL // 2, SHARD_IN

    def body(half_ref, x_ref, dh_ref, o_ref):
        k = pl.program_id(1)
        part = _dot_tn(x_ref[...].astype(BF16), dh_ref[...])

        @pl.when(k == 0)
        def _():
            o_ref[...] = part

        @pl.when(k > 0)
        def _():
            o_ref[...] += part

    (g,), exchanged = _call(
        body, name=name, grid=(N_SHARDS, seq // ts),
        in_specs=[pl.BlockSpec((ts, td), lambda e, k, half_ref: (k, half_ref[0])),
                  pl.BlockSpec((ts, te), lambda e, k, half_ref: (k, e))],
        out_specs=[pl.BlockSpec((None, td, te), lambda e, k, half_ref: (e, 0, 0))],
        out_shape=[jax.ShapeDtypeStruct((N_SHARDS, td, te), F32)],
        scratch_shapes=[], semantics=("parallel", "arbitrary"), vmem_mib=56, args=(x, dh), comm=comm,
        prefetch=(half,))
    return g, exchanged


def _grad_w_out(y, dzb):
    seq = y.shape[0]
    ts, te = 512, 1024
    nk = seq // ts

    def body(y_ref, dz_ref, o_ref, acc_ref):
        k = pl.program_id(1)

        @pl.when(k == 0)
        def _():
            acc_ref[...] = jnp.zeros_like(acc_ref)

        acc_ref[...] += _dot_tn(y_ref[...], dz_ref[...])

        @pl.when(k == nk - 1)
        def _():
            o_ref[...] = acc_ref[...]

    return _pallas(
        body, name="grad_w_out", grid=(D_MODEL // te, nk),
        in_specs=[pl.BlockSpec((ts, te), lambda e, k: (k, e)), pl.BlockSpec((ts, D_MODEL), lambda e, k: (k, 0))],
        out_specs=pl.BlockSpec((te, D_MODEL), lambda e, k: (e, 0)),
        out_shape=jax.ShapeDtypeStruct((D_MODEL, D_MODEL), F32),
        scratch_shapes=[pltpu.VMEM((te, D_MODEL), F32)],
        compiler_params=_params(("parallel", "arbitrary"), 48),
    )(y, dzb)


GRAD_X_LATE_SHARDS = 1


def _grad_x_partial(dh, w_in_g, first, tiles, prev=None, comm=None):
    seq = dh.shape[0]
    tm, tk = 512, SHARD_IN

    def body(*refs):
        dh_ref, w_ref, o_ref = refs[-3:]
        k = pl.program_id(1)
        part = _dot_nt(dh_ref[...], w_ref[...])

        @pl.when(k == 0)
        def _():
            o_ref[...] = part

        @pl.when(k > 0)
        def _():
            o_ref[...] += part

    carried = [] if prev is None else [prev]
    (partial,), exchanged = _call(
        body, name="grad_x_partial_%d" % first, grid=(tiles, N_SHARDS - GRAD_X_LATE_SHARDS),
        in_specs=[ANY] * len(carried) + [
            pl.BlockSpec((tm, tk), lambda i, k: (i + first, k)),
            pl.BlockSpec((None, D_MODEL, tk), lambda i, k: (k, 0, 0))],
        out_specs=[pl.BlockSpec((tm, D_MODEL), lambda i, k: (i + first, 0))],
        out_shape=[jax.ShapeDtypeStruct((seq, D_MODEL), F32)],
        scratch_shapes=[], semantics=("parallel", "arbitrary"), vmem_mib=48, args=(*carried, dh, w_in_g),
        aliases={0: 0} if carried else None, comm=comm)
    return partial, exchanged


def _grad_x_final(dh, w_in_g, dz, partial):
    seq = dh.shape[0]
    tm, tk = 512, SHARD_IN
    k0 = N_SHARDS - GRAD_X_LATE_SHARDS

    def body(dh_ref, w_ref, dz_ref, p_ref, o_ref):
        k = pl.program_id(1)
        part = _dot_nt(dh_ref[...], w_ref[...])

        @pl.when(k == 0)
        def _():
            o_ref[...] = (DEEPNORM_ALPHA * dz_ref[...] + p_ref[...]) + part

        @pl.when(k > 0)
        def _():
            o_ref[...] += part

    row = pl.BlockSpec((tm, D_MODEL), lambda i, k: (i, 0))
    return _pallas(
        body, name="grad_x_final", grid=(seq // tm, GRAD_X_LATE_SHARDS),
        in_specs=[pl.BlockSpec((tm, tk), lambda i, k: (i, k + k0)),
                  pl.BlockSpec((None, D_MODEL, tk), lambda i, k: (k + k0, 0, 0)), row, row],
        out_specs=row, out_shape=jax.ShapeDtypeStruct((seq, D_MODEL), F32),
        compiler_params=_params(("parallel", "arbitrary"), 48),
    )(dh, w_in_g, dz, partial)


def _pool_weight(w_pool_sh):
    n_groups = len(POOL_WINDOWS)
    shard_c = POOL_GROUP_DIM // N_SHARDS
    return (w_pool_sh.reshape(N_SHARDS, n_groups, shard_c, POOL_GROUP_DIM).transpose(1, 0, 2, 3)
            .reshape(n_groups, POOL_GROUP_DIM, POOL_GROUP_DIM))


def _pool_grad_pieces(g_w_pool):
    n_groups = len(POOL_WINDOWS)
    half_c = POOL_GROUP_DIM // N_SHARDS // 2
    return (g_w_pool.reshape(n_groups, N_SHARDS, 2, half_c, POOL_GROUP_DIM).transpose(1, 2, 0, 3, 4)
            .reshape(N_SHARDS, 2, n_groups * half_c, POOL_GROUP_DIM))


def _step(x, target, w_in_g, w_rest, pool_scale, gain, bias, place=None):
    seq = x.shape[0]
    tabs = _rope_tables(seq)
    qkv, gathered = _in_proj_qkv(x, w_in_g, tabs, comm=_allgather_weights(w_rest) if place else None)
    w_out_g, w_pool_sh = gathered if place else w_rest
    w_pool_g = _pool_weight(w_pool_sh)
    hug = _in_proj_pool_gate(x, w_in_g)
    o_list, st_list = [], []
    for p, dil in enumerate(DILATIONS):
        o, st = _attn_fwd(qkv[p], "attn_fwd_d%d" % dil)
        o_list.append(o)
        st_list.append(st)
    y, mixpre, lse_all, pooled = _mix_gate(o_list, st_list, hug, w_pool_g, pool_scale)
    dz, dzb, gain_part, bias_part, loss_part = _out_proj_loss(y, w_out_g, x, target, gain, bias)
    dh, dpo, do_list, stat_list = _dy_gate_bwd(dzb, w_out_g, hug, mixpre, pool_scale, lse_all)
    g_w_out = _grad_w_out(y, dzb)
    dh, g_w_pool, scale_part = _pool_bwd(dh, dpo, mixpre, pooled, w_pool_g, pool_scale)
    small = jnp.concatenate([scale_part, gain_part, bias_part, loss_part], axis=1)
    early = [g_w_out.reshape(N_SHARDS, 2, D_MODEL // (2 * N_SHARDS), D_MODEL), _pool_grad_pieces(g_w_pool)]

    bwd = lambda p, comm: _attn_bwd(qkv[p], do_list[p], stat_list[p], "attn_bwd_d%d" % DILATIONS[p], comm)
    if place is None:
        parts = [bwd(p, None)[0] for p in range(3)]
    else:
        core, chip_core, onward = place
        part_a, recv = bwd(0, _exchange_halves(early))
        sums = [_add_own_half(g, r, core, "add_own_half_%d" % a) for a, (g, r) in enumerate(zip(early, recv))]
        part_b, recv = bwd(1, _scatter_to_chips([s[1] for s in sums]))
        bufs = [_add_chips(s[0], r, chip_core, "add_chips_%d" % a) for a, (s, r) in enumerate(zip(sums, recv))]
        part_c, early = bwd(2, _share_with_sibling(bufs))
        parts = [part_a, part_b, part_c]
    dh, gathered = _sum_patterns(dh, [t[0] for t in parts], tabs, True, 0, "sum_dq",
                                 _gather_small(small) if place else None)
    dh, _ = _sum_patterns(dh, [t[1] for t in parts], tabs, True, 1, "sum_dk")
    dh, _ = _sum_patterns(dh, [t[2] for t in parts], tabs, False, 2, "sum_dv")
    if place:
        small = (small, gathered[0])
    if place is None:
        halves = [_grad_w_in(x, dh, jnp.full((1,), h, jnp.int32), "grad_w_in_%d" % h)[0] for h in range(2)]
        g_w_in = jnp.stack(halves, axis=1)
        g_x = _grad_x_final(dh, w_in_g, dz, _grad_x_partial(dh, w_in_g, 0, seq // 512)[0])
    else:
        give, _ = _grad_w_in(x, dh, 1 - core, "grad_w_in_give")
        keep, recv = _grad_w_in(x, dh, core, "grad_w_in_keep", _send_to_sibling([give]))
        total, total_b = _add_pair(keep, recv[0], "add_own_half_w_in")
        tiles = seq // 512 // 2
        part, relayed = _grad_x_partial(dh, w_in_g, 0, tiles, None, _relay_diagonal(total_b))
        total_b = _fold_relayed(total, total_b, relayed[0], onward)
        part, recv = _grad_x_partial(dh, w_in_g, tiles, tiles, part, _scatter_to_neighbours(total_b))
        buf = _add_chips(total, recv[0], chip_core, "add_chips_w_in")
        g_x = _grad_x_final(dh, w_in_g, dz, part)
        g_w_in = _run_exchange(_share_with_sibling([buf]), "share_w_in")[0]
    return g_x, g_w_in, early[0], early[1], small


def _exchange_halves(grads):
    n = len(grads)

    def copies(src, dst, sems):
        x, y, c, _ = _mesh_place()
        return [_remote(src[a].at[j, 1 - c], dst[a].at[j], sems[0].at[a, j], sems[1].at[a, j], (x, y, 1 - c))
                for a in range(n) for j in range(N_SHARDS)]

    def start(src, dst, sems):
        for cp in copies(src, dst, sems):
            cp.start()

    def finish(src, dst, sems):
        for cp in copies(src, dst, sems):
            cp.wait()

    return _Exchange(grads, [jax.ShapeDtypeStruct((N_SHARDS,) + g.shape[2:], g.dtype) for g in grads], {},
                     [pltpu.SemaphoreType.DMA((n, N_SHARDS))] * 2, start, finish)


def _add_own_half(grad, recv, core, name):
    _, _, r, c = grad.shape
    tr = min(r, 256)

    def body(core_ref, g_ref, r_ref, o_ref, ob_ref):
        tot = g_ref[...] + r_ref[...]
        o_ref[...] = tot
        ob_ref[...] = tot.astype(BF16)

    out = pl.BlockSpec((None, tr, c), lambda j, i, core_ref: (j, i, 0))
    return _pallas(
        body, name=name,
        grid_spec=pltpu.PrefetchScalarGridSpec(
            num_scalar_prefetch=1, grid=(N_SHARDS, r // tr),
            in_specs=[pl.BlockSpec((None, None, tr, c), lambda j, i, core_ref: (j, core_ref[0], i, 0)),
                      pl.BlockSpec((None, tr, c), lambda j, i, core_ref: (j, i, 0))],
            out_specs=[out, out]),
        out_shape=[jax.ShapeDtypeStruct((N_SHARDS, r, c), F32), jax.ShapeDtypeStruct((N_SHARDS, r, c), BF16)],
        compiler_params=_params(("parallel", "parallel"), 32),
    )(core, grad, recv)


def _send_to_sibling(arrays):
    n = len(arrays)

    def copies(src, dst, sems):
        x, y, c, _ = _mesh_place()
        return [_remote(src[a], dst[a], sems[0].at[a], sems[1].at[a], (x, y, 1 - c)) for a in range(n)]

    def start(src, dst, sems):
        for cp in copies(src, dst, sems):
            cp.start()

    def finish(src, dst, sems):
        for cp in copies(src, dst, sems):
            cp.wait()

    return _Exchange(arrays, [jax.ShapeDtypeStruct(t.shape, t.dtype) for t in arrays], {},
                     [pltpu.SemaphoreType.DMA((n,))] * 2, start, finish)


def _add_pair(a, b, name):
    _, r, c = a.shape
    tr = min(r, 256)

    def body(a_ref, b_ref, o_ref, ob_ref):
        tot = a_ref[...] + b_ref[...]
        o_ref[...] = tot
        ob_ref[...] = tot.astype(BF16)

    spec = pl.BlockSpec((None, tr, c), lambda j, i: (j, i, 0))
    return _pallas(
        body, name=name, grid=(N_SHARDS, r // tr), in_specs=[spec, spec], out_specs=[spec, spec],
        out_shape=[jax.ShapeDtypeStruct(a.shape, F32), jax.ShapeDtypeStruct(a.shape, BF16)],
        compiler_params=_params(("parallel", "parallel"), 32),
    )(a, b)


def _scatter_to_chips(sums, rows=None, into=None):
    n = len(sums)

    def copies(src, dst, sems):
        x, y, c, chips = _mesh_place()
        part = (lambda ref: ref) if rows is None else (lambda ref: ref.at[pl.ds(rows[0], rows[1])])
        return [_remote(part(src[a].at[2 * cx + cy]), part(dst[a].at[k]), sems[0].at[a, k], sems[1].at[a, k],
                        (cx, cy, c))
                for a in range(n) for k, (cx, cy) in enumerate(chips)]

    def start(src, dst, sems):
        for cp in copies(src, dst, sems):
            cp.start()

    def finish(src, dst, sems):
        for cp in copies(src, dst, sems):
            cp.wait()

    return _Exchange(sums + (into or []), [jax.ShapeDtypeStruct((3,) + s.shape[1:], s.dtype) for s in sums],
                     {n + a: a for a in range(n)} if into else {},
                     [pltpu.SemaphoreType.DMA((n, 3))] * 2, start, finish)


def _add_chips(sums, recv, chip_core, name):
    _, r, c = sums.shape
    n_recv = recv.shape[0]
    tr = min(r, 256)

    def body(cc_ref, s_ref, r_ref, o_ref):
        tot = s_ref[...]
        for k in range(n_recv):
            tot = tot + r_ref[k].astype(F32)
        o_ref[...] = tot

    return _pallas(
        body, name=name,
        grid_spec=pltpu.PrefetchScalarGridSpec(
            num_scalar_prefetch=1, grid=(r // tr,),
            in_specs=[pl.BlockSpec((None, tr, c), lambda i, cc_ref: (cc_ref[0], i, 0)),
                      pl.BlockSpec((n_recv, tr, c), lambda i, cc_ref: (0, i, 0))],
            out_specs=pl.BlockSpec((None, tr, c), lambda i, cc_ref: (cc_ref[1], i, 0))),
        out_shape=jax.ShapeDtypeStruct((2, r, c), F32),
        compiler_params=_params(("parallel",), 32),
    )(chip_core, sums, recv)


def _relay_diagonal(sums_b):
    def copy(src, dst, sems):
        x, y, c, _ = _mesh_place()
        diagonal = 2 * (1 - x) + (1 - y)
        return _remote(src[0].at[diagonal], dst[0], sems[0].at[0], sems[1].at[0], (x ^ (1 - c), y ^ c, c))

    def start(src, dst, sems):
        copy(src, dst, sems).start()

    def finish(src, dst, sems):
        copy(src, dst, sems).wait()

    return _Exchange([sums_b], [jax.ShapeDtypeStruct(sums_b.shape[1:], sums_b.dtype)], {},
                     [pltpu.SemaphoreType.DMA((1,))] * 2, start, finish)


def _fold_relayed(sums, sums_b, relayed, onward):
    _, r, c = sums.shape
    tr = min(r, 256)

    def body(on_ref, b_in_ref, s_ref, r_ref, o_ref):
        o_ref[...] = (s_ref[...] + r_ref[...].astype(F32)).astype(BF16)

    return _pallas(
        body, name="fold_relayed",
        grid_spec=pltpu.PrefetchScalarGridSpec(
            num_scalar_prefetch=1, grid=(r // tr,),
            in_specs=[ANY, pl.BlockSpec((None, tr, c), lambda i, on_ref: (on_ref[0], i, 0)),
                      pl.BlockSpec((tr, c), lambda i, on_ref: (i, 0))],
            out_specs=pl.BlockSpec((None, tr, c), lambda i, on_ref: (on_ref[0], i, 0))),
        out_shape=jax.ShapeDtypeStruct(sums_b.shape, sums_b.dtype),
        input_output_aliases={1: 0},
        compiler_params=_params(("parallel",), 32),
    )(onward, sums_b, sums, relayed)


def _scatter_to_neighbours(sums_b):
    def copies(src, dst, sems):
        x, y, c, chips = _mesh_place()
        return [_remote(src[0].at[2 * cx + cy], dst[0].at[k], sems[0].at[k], sems[1].at[k], (cx, cy, c))
                for k, (cx, cy) in enumerate(chips[:2])]

    def start(src, dst, sems):
        for cp in copies(src, dst, sems):
            cp.start()

    def finish(src, dst, sems):
        for cp in copies(src, dst, sems):
            cp.wait()

    return _Exchange([sums_b], [jax.ShapeDtypeStruct((2,) + sums_b.shape[1:], sums_b.dtype)], {},
                     [pltpu.SemaphoreType.DMA((2,))] * 2, start, finish)


def _share_with_sibling(bufs):
    n = len(bufs)

    def copies(dst, sems, half):
        x, y, c, _ = _mesh_place()
        h = c if half == "mine" else 1 - c
        return [_remote(dst[a].at[h], dst[a].at[h], sems[0].at[a], sems[1].at[a], (x, y, 1 - c)) for a in range(n)]

    def start(ins, dst, sems):
        for cp in copies(dst, sems, "mine"):
            cp.start()

    def finish(ins, dst, sems):
        for cp in copies(dst, sems, "theirs"):
            cp.wait_recv()
        for cp in copies(dst, sems, "mine"):
            cp.wait_send()

    return _Exchange(bufs, [jax.ShapeDtypeStruct(b.shape, b.dtype) for b in bufs], {a: a for a in range(n)},
                     [pltpu.SemaphoreType.DMA((n,))] * 2, start, finish)


def _adam_math(w, g, m, v):
    m = ADAM_B1 * m + (1.0 - ADAM_B1) * g
    v = ADAM_B2 * v + (1.0 - ADAM_B2) * (g * g)
    m_hat = m / (1.0 - ADAM_B1 ** ADAM_STEP)
    v_hat = v / (1.0 - ADAM_B2 ** ADAM_STEP)
    delta = -ADAM_LR * (m_hat / (jnp.sqrt(v_hat) + ADAM_EPS) + ADAM_WD * w)
    return delta, m, v


def _gather_small(small):
    def peers():
        x, y, c, _ = _mesh_place()
        return [(x ^ ((r >> 2) & 1), y ^ ((r >> 1) & 1), c ^ (r & 1)) for r in range(1, 8)], 4 * x + 2 * y + c

    def start(src, dst, sems):
        to, me = peers()
        for r, peer in enumerate(to):
            _remote(src[0], dst[0].at[me], sems[0].at[r], sems[1].at[r], peer).start()

    def finish(src, dst, sems):
        to, me = peers()
        for r, (px, py, pc) in enumerate(to):
            theirs = dst[0].at[4 * px + 2 * py + pc]
            _remote(theirs, theirs, sems[0].at[r], sems[1].at[r], (px, py, pc)).wait_recv()
        for r, peer in enumerate(to):
            _remote(src[0], dst[0].at[me], sems[0].at[r], sems[1].at[r], peer).wait_send()

    return _Exchange([small], [jax.ShapeDtypeStruct((8,) + small.shape, small.dtype)], {},
                     [pltpu.SemaphoreType.DMA((7,))] * 2, start, finish)


def _small_adamw(gathered, small, me, w_vec, m_vec, v_vec):
    n_par = w_vec.shape[1]

    def body(me_ref, a_ref, s_ref, w_ref, m_ref, v_ref, loss_ref, g_ref, d_ref, nm_ref, nv_ref):
        mine = s_ref[...]
        tot = jnp.where(me_ref[0] == 0, mine, a_ref[0])
        for d in range(1, 8):
            tot = tot + jnp.where(me_ref[0] == d, mine, a_ref[d])
        tot = jnp.sum(tot, axis=0, keepdims=True)
        sq = jnp.sum(tot[:, n_par:], axis=1, keepdims=True)
        loss_ref[...] = jnp.broadcast_to(sq * (0.5 / D_MODEL), loss_ref.shape)
        g = tot[:, :n_par]
        g_ref[...] = g
        d_ref[...], nm_ref[...], nv_ref[...] = _adam_math(w_ref[...], g, m_ref[...], v_ref[...])

    vm = pl.BlockSpec(memory_space=pltpu.VMEM)
    vec = jax.ShapeDtypeStruct((1, n_par), F32)
    return pl.pallas_call(
        body, name="small_adamw",
        grid_spec=pltpu.PrefetchScalarGridSpec(num_scalar_prefetch=1, grid=(), in_specs=[vm] * 5, out_specs=[vm] * 5),
        out_shape=[jax.ShapeDtypeStruct((1, 128), F32), vec, vec, vec, vec],
    )(me, gathered, small, w_vec, m_vec, v_vec)


def _adamw(w, g, m, v, name):
    r, c = w.shape
    tr = min(r, 256)

    def body(w_ref, g_ref, m_ref, v_ref, d_ref, nm_ref, nv_ref):
        d_ref[...], nm_ref[...], nv_ref[...] = _adam_math(w_ref[...], g_ref[...], m_ref[...], v_ref[...])

    spec = pl.BlockSpec((tr, c), lambda i: (i, 0))
    shape = jax.ShapeDtypeStruct((r, c), F32)
    return _pallas(
        body, name=name, grid=(r // tr,),
        in_specs=[spec] * 4, out_specs=[spec] * 3, out_shape=[shape] * 3,
        compiler_params=_params(("parallel",), 48),
    )(w, g, m, v)


def kernel(x, w_in, w_pool, pool_scale, w_out, ln_gain, ln_bias, loss_target, m_w_in, m_w_pool, m_pool_scale, m_w_out, m_ln_gain, m_ln_bias, v_w_in, v_w_pool, v_pool_scale, v_w_out, v_ln_gain, v_ln_bias):
    xi, yi, ci = lax.axis_index("x"), lax.axis_index("y"), lax.axis_index("c")
    chip = (2 * xi + yi).astype(jnp.int32).reshape(1)
    core = ci.astype(jnp.int32).reshape(1)
    n_groups = len(POOL_WINDOWS)
    shard_c = w_pool.shape[2]

    w_in_b = _cast_bf16(w_in[0], chip, "cast_w_in", 256)
    w_out_b = _cast_bf16(w_out[0], chip, "cast_w_out", 256)
    w_pool_b = _cast_bf16(w_pool[0].reshape(n_groups * shard_c, POOL_GROUP_DIM), chip, "cast_w_pool", 256)
    w_in_g = _run_exchange(_allgather_weights([w_in_b]), "allgather_w_in")[0]

    chip_core = jnp.concatenate([chip, core])
    onward = (2 * (xi ^ ci) + (yi ^ (1 - ci))).astype(jnp.int32).reshape(1)
    g_x, full_in, full_out, full_pool, small = _step(
        x[0], loss_target[0], w_in_g, [w_out_b, w_pool_b], pool_scale, ln_gain, ln_bias, (core, chip_core, onward))
    half_c = shard_c // 2
    grad_w_in = full_in.reshape(D_MODEL, SHARD_IN)
    grad_w_out = full_out.reshape(D_MODEL // N_SHARDS, D_MODEL)
    grad_w_pool = (full_pool.reshape(2, n_groups, half_c, POOL_GROUP_DIM).transpose(1, 0, 2, 3)
                   .reshape(n_groups * shard_c, POOL_GROUP_DIM))

    d_in, nm_in, nv_in = _adamw(w_in[0], grad_w_in, m_w_in[0], v_w_in[0], "adamw_w_in")
    d_out, nm_out, nv_out = _adamw(w_out[0], grad_w_out, m_w_out[0], v_w_out[0], "adamw_w_out")
    flat = lambda t: t[0].reshape(n_groups * shard_c, POOL_GROUP_DIM)
    d_pool, nm_pool, nv_pool = _adamw(flat(w_pool), grad_w_pool, flat(m_w_pool), flat(v_w_pool), "adamw_w_pool")

    cat = lambda a, b, c: jnp.concatenate([a, b, c], axis=1)
    me = (4 * xi + 2 * yi + ci).astype(jnp.int32).reshape(1)
    loss_v, g_vec, d_vec, nm_vec, nv_vec = _small_adamw(
        small[1], small[0], me, cat(pool_scale, ln_gain, ln_bias), cat(m_pool_scale, m_ln_gain, m_ln_bias),
        cat(v_pool_scale, v_ln_gain, v_ln_bias))

    def split(vec):
        return vec[:, :D_POOL], vec[:, D_POOL:D_POOL + D_MODEL], vec[:, D_POOL + D_MODEL:]

    g_scale, g_gain, g_bias = split(g_vec)
    d_scale, d_gain, d_bias = split(d_vec)
    nm_scale, nm_gain, nm_bias = split(nm_vec)
    nv_scale, nv_gain, nv_bias = split(nv_vec)
    pool_shape = w_pool.shape
    return (loss_v[0, 0], g_x[None],
            grad_w_in[None], grad_w_pool.reshape(pool_shape), g_scale, grad_w_out[None], g_gain, g_bias,
            d_in[None], d_pool.reshape(pool_shape), d_scale, d_out[None], d_gain, d_bias,
            nm_in[None], nm_pool.reshape(pool_shape), nm_scale, nm_out[None], nm_gain, nm_bias,
            nv_in[None], nv_pool.reshape(pool_shape), nv_scale, nv_out[None], nv_gain, nv_bias)
```

```python
import functools

import jax
import jax.numpy as jnp
from jax import lax
from jax.experimental import pallas as pl
from jax.experimental.pallas import tpu as pltpu

F32 = jnp.float32
BF16 = jnp.bfloat16
MESH = pl.DeviceIdType.MESH
ANY = pl.BlockSpec(memory_space=pl.ANY)

D_MODEL = 2048
D_ATTN = 1024
D_POOL = 1024
HEAD_DIM = 128
N_HEADS = 8
ROPE_DIM = 32
ROPE_THETA = 500000.0
DILATIONS = (1, 4, 16)
KEY_BLOCK = 128
CHUNK = 2 * KEY_BLOCK
STAT_LANES = 128
POOL_WINDOWS = (2, 4, 8, 16)
POOL_GROUP_DIM = 256
POOL_HALO = 16
D_QKV = 3 * D_ATTN
D_UG = D_POOL + D_MODEL
D_IN = D_QKV + D_UG
N_SHARDS = 4
SHARD_IN = D_IN // N_SHARDS
LN_EPS = 1e-5
DEEPNORM_ALPHA = 2.0 ** 0.25
ADAM_LR = 0.001
ADAM_B1 = 0.9
ADAM_B2 = 0.999
ADAM_EPS = 1e-08
ADAM_WD = 0.01
ADAM_STEP = 10
NEG = -1e30
MIB = 1024 * 1024


def _params(sem, vmem_mib):
    return pltpu.CompilerParams(dimension_semantics=sem, vmem_limit_bytes=vmem_mib * MIB)


def _pallas(body, **kwargs):
    pin = lambda s: pltpu.HBM(s.shape, s.dtype) if len(s.shape) >= 2 else s
    out_shape = kwargs.pop("out_shape")
    out_shape = [pin(s) for s in out_shape] if isinstance(out_shape, (list, tuple)) else pin(out_shape)
    call = pl.pallas_call(body, out_shape=out_shape, **kwargs)

    def run(*operands):
        return call(*[pltpu.with_memory_space_constraint(o, pltpu.HBM) if o.ndim >= 2 else o for o in operands])

    return run


class _Exchange:
    def __init__(self, operands, out_shape, aliases, sems, start, finish):
        self.operands, self.out_shape, self.aliases, self.sems = list(operands), list(out_shape), dict(aliases), list(sems)
        self.start, self.finish = start, finish


def _run_exchange(comm, name):
    n_in, n_out = len(comm.operands), len(comm.out_shape)

    def body(*refs):
        ins, outs, sems = refs[:n_in], refs[n_in:n_in + n_out], refs[n_in + n_out:]
        comm.start(ins, outs, sems)
        comm.finish(ins, outs, sems)

    return _pallas(
        body, name=name, in_specs=[ANY] * n_in, out_specs=[ANY] * n_out, out_shape=comm.out_shape,
        input_output_aliases=comm.aliases, scratch_shapes=comm.sems,
    )(*comm.operands)


def _call(body, *, name, grid, in_specs, out_specs, out_shape, scratch_shapes, semantics, vmem_mib, args,
          aliases=None, comm=None, prefetch=()):
    aliases = dict(aliases or {})
    n_pre, n_in, n_out, n_scr = len(prefetch), len(in_specs), len(out_specs), len(scratch_shapes)
    c_in, c_out = (len(comm.operands), len(comm.out_shape)) if comm else (0, 0)
    c_shapes, c_sems, c_operands = (comm.out_shape, comm.sems, comm.operands) if comm else ([], [], [])

    def hosted(*refs):
        pre, refs = refs[:n_pre], refs[n_pre:]
        a = n_in
        b = a + c_in
        c = b + n_out
        d = c + c_out
        e = d + n_scr
        if comm is None:
            body(*pre, *refs)
            return
        ids = [pl.program_id(k) for k in range(len(grid))]
        first = functools.reduce(jnp.logical_and, [i == 0 for i in ids])
        last = functools.reduce(jnp.logical_and, [i == g - 1 for i, g in zip(ids, grid)])

        @pl.when(first)
        def _():
            comm.start(refs[a:b], refs[c:d], refs[e:])

        body(*pre, *refs[:a], *refs[b:c], *refs[d:e])

        @pl.when(last)
        def _():
            comm.finish(refs[a:b], refs[c:d], refs[e:])

    if comm:
        semantics = ("arbitrary",) * len(grid)
        for i, o in comm.aliases.items():
            aliases[n_pre + n_in + i] = n_out + o
    outs = _pallas(
        hosted, name=name,
        grid_spec=pltpu.PrefetchScalarGridSpec(
            num_scalar_prefetch=n_pre, grid=grid, in_specs=list(in_specs) + [ANY] * c_in,
            out_specs=list(out_specs) + [ANY] * c_out, scratch_shapes=list(scratch_shapes) + c_sems),
        out_shape=list(out_shape) + c_shapes, input_output_aliases=aliases,
        compiler_params=_params(semantics, vmem_mib),
    )(*prefetch, *args, *c_operands)
    return list(outs[:n_out]), list(outs[n_out:])


def _dot_nn(a, b):
    return jnp.dot(a, b, preferred_element_type=F32)


def _dot_nt(a, b):
    return lax.dot_general(a, b, (((1,), (1,)), ((), ())), preferred_element_type=F32)


def _dot_tn(a, b):
    return lax.dot_general(a, b, (((0,), (0,)), ((), ())), preferred_element_type=F32)


def _fold_rows(a):
    r, c = a.shape
    return jnp.sum(a.reshape(r // 8, 8, c), axis=0)


def _cast_bf16(a, chip, name, rows):
    r, c = a.shape

    def body(chip_ref, a_ref, o_ref):
        o_ref[...] = a_ref[...].astype(BF16)

    return _pallas(
        body, name=name,
        grid_spec=pltpu.PrefetchScalarGridSpec(
            num_scalar_prefetch=1, grid=(r // rows,),
            in_specs=[pl.BlockSpec((rows, c), lambda i, chip_ref: (i, 0))],
            out_specs=pl.BlockSpec((None, rows, c), lambda i, chip_ref: (chip_ref[0], i, 0))),
        out_shape=jax.ShapeDtypeStruct((N_SHARDS, r, c), BF16),
        compiler_params=_params(("parallel",), 32),
    )(chip, a)


def _mesh_place():
    x, y, c = lax.axis_index("x"), lax.axis_index("y"), lax.axis_index("c")
    return x, y, c, [(1 - x, y), (x, 1 - y), (1 - x, 1 - y)]


def _remote(src, dst, send_sem, recv_sem, to):
    return pltpu.make_async_remote_copy(src_ref=src, dst_ref=dst, send_sem=send_sem, recv_sem=recv_sem,
                                        device_id=to, device_id_type=MESH)


def _allgather_weights(bufs, phase="all"):
    n = len(bufs)

    def half(a, core):
        rows = bufs[a].shape[1] // 2
        return pl.ds(core * rows, rows)

    DIAGONAL = 2

    def to_neighbours(dst, sems):
        x, y, c, chips = _mesh_place()
        own = lambda a: dst[a].at[2 * x + y, half(a, c)]
        return [_remote(own(a), own(a), sems[0].at[a, k], sems[1].at[a, k], (cx, cy, c))
                for a in range(n) for k, (cx, cy) in enumerate(chips[:DIAGONAL])]

    def relayed(dst, sems):
        x, y, c, _ = _mesh_place()
        owner = 2 * (x ^ (1 - c)) + (y ^ c)
        piece = lambda a: dst[a].at[owner, half(a, c)]
        return [_remote(piece(a), piece(a), sems[0].at[a, DIAGONAL], sems[1].at[a, DIAGONAL], (x ^ c, y ^ (1 - c), c))
                for a in range(n)]

    def start(ins, dst, sems):
        for cp in (relayed if phase == "diagonal" else to_neighbours)(dst, sems):
            cp.start()

    def finish(ins, dst, sems):
        x, y, c, chips = _mesh_place()
        sibling = (x, y, 1 - c)
        passed_on = []

        def landed_then_pass_on(k):
            cx, cy = chips[k]
            for a in range(n):
                landed = dst[a].at[2 * cx + cy, half(a, c)]
                _remote(landed, landed, sems[0].at[a, k], sems[1].at[a, k], (cx, cy, c)).wait_recv()
                cp = _remote(landed, landed, sems[2].at[a, k], sems[3].at[a, k], sibling)
                cp.start()
                passed_on.append(cp)

        sent = []
        if phase != "diagonal":
            for k in range(DIAGONAL):
                landed_then_pass_on(k)
            sent += to_neighbours(dst, sems)
        if phase == "all":
            for cp in relayed(dst, sems):
                cp.start()
        if phase != "neighbours":
            landed_then_pass_on(DIAGONAL)
            sent += relayed(dst, sems)
        for k in {"all": (0, 1, 2), "neighbours": (0, 1), "diagonal": (2,)}[phase]:
            cx, cy = chips[k]
            for a in range(n):
                passed = dst[a].at[2 * cx + cy, half(a, 1 - c)]
                _remote(passed, passed, sems[2].at[a, k], sems[3].at[a, k], sibling).wait_recv()
        for cp in sent + passed_on:
            cp.wait_send()

    return _Exchange(bufs, [jax.ShapeDtypeStruct(b.shape, b.dtype) for b in bufs], {a: a for a in range(n)},
                     [pltpu.SemaphoreType.DMA((n, 3))] * 4, start, finish)


def _rope_tables(seq):
    half = ROPE_DIM // 2
    inv_freq = ROPE_THETA ** (-(2.0 * jnp.arange(half, dtype=F32)) / ROPE_DIM)
    ang = jnp.arange(seq, dtype=jnp.int32).astype(F32)[:, None] * inv_freq[None, :]
    cos, sin = jnp.cos(ang), jnp.sin(ang)
    pad = jnp.zeros((seq, HEAD_DIM - ROPE_DIM), F32)
    zeros = jnp.zeros((seq, half), F32)
    c_tab = jnp.concatenate([cos, cos, pad + 1.0], axis=1)
    up_tab = jnp.concatenate([-sin, zeros, pad], axis=1)
    down_tab = jnp.concatenate([zeros, sin, pad], axis=1)
    return c_tab, up_tab, down_tab


def _rotate_heads(t, c_tab, up_tab, down_tab):
    outs = []
    for h in range(t.shape[1] // HEAD_DIM):
        th = t[:, h * HEAD_DIM:(h + 1) * HEAD_DIM]
        up = pltpu.roll(th, HEAD_DIM - ROPE_DIM // 2, axis=1)
        down = pltpu.roll(th, ROPE_DIM // 2, axis=1)
        outs.append(th * c_tab + up * up_tab + down * down_tab)
    return outs[0] if len(outs) == 1 else jnp.concatenate(outs, axis=1)


def _to_pattern(slabs_ref, dst_ref, dil, dtype):
    n_slabs, rows, _ = slabs_ref.shape
    for s in range(n_slabs):
        for r in range(dil):
            dst_ref[r, :, s * 128:(s + 1) * 128] = slabs_ref[s, pl.ds(r, rows // dil, dil), :].astype(dtype)


def _from_pattern(src_ref, slabs_ref, dil):
    n_slabs, rows, _ = slabs_ref.shape
    for s in range(n_slabs):
        for r in range(dil):
            slabs_ref[s, pl.ds(r, rows // dil, dil), :] = src_ref[r, :, s * 128:(s + 1) * 128].astype(F32)


def _store_slabs(slabs_ref, value):
    for s in range(slabs_ref.shape[0]):
        slabs_ref[s] = value[:, s * 128:(s + 1) * 128]


def _in_proj_qkv(x, w_in_g, tabs, comm=None):
    seq = x.shape[0]
    tm, tn = 512, SHARD_IN
    heads = tn // HEAD_DIM
    k_heads_in_second = 2 * D_ATTN // HEAD_DIM - heads
    d4, d16 = DILATIONS[1], DILATIONS[2]

    def body(x_ref, w_ref, c_ref, up_ref, down_ref, o1_ref, o4_ref, o16_ref, res_ref):
        shard = pl.program_id(0)
        xb = x_ref[...].astype(BF16)
        group = 4 * HEAD_DIM
        accs = [_dot_nn(xb, w_ref[:, g * group:(g + 1) * group]) for g in range(tn // group)]

        plain = shard == 1
        c_plain = jnp.where(plain, 1.0, c_ref[...])
        up_plain = jnp.where(plain, 0.0, up_ref[...])
        down_plain = jnp.where(plain, 0.0, down_ref[...])
        for h in range(heads):
            lanes = (h * HEAD_DIM) % group
            th = accs[h * HEAD_DIM // group][:, lanes:lanes + HEAD_DIM]
            if h < k_heads_in_second:
                th = _rotate_heads(th, c_ref[...], up_ref[...], down_ref[...])
            else:
                th = _rotate_heads(th, c_plain, up_plain, down_plain)
            res_ref[h] = th
            o1_ref[:, h * HEAD_DIM:(h + 1) * HEAD_DIM] = th.astype(BF16)
        _to_pattern(res_ref, o4_ref, d4, BF16)
        _to_pattern(res_ref, o16_ref, d16, BF16)

    tab_spec = pl.BlockSpec((tm, HEAD_DIM), lambda s, i: (i, 0))
    (o1, o4, o16), exchanged = _call(
        body, name="in_proj_qkv", grid=(D_QKV // tn, seq // tm),
        in_specs=[pl.BlockSpec((tm, D_MODEL), lambda s, i: (i, 0)),
                  pl.BlockSpec((None, D_MODEL, tn), lambda s, i: (s, 0, 0)),
                  tab_spec, tab_spec, tab_spec],
        out_specs=[pl.BlockSpec((tm, tn), lambda s, i: (i, s)),
                   pl.BlockSpec((d4, tm // d4, tn), lambda s, i: (0, i, s)),
                   pl.BlockSpec((d16, tm // d16, tn), lambda s, i: (0, i, s))],
        out_shape=[jax.ShapeDtypeStruct((seq, D_QKV), BF16),
                   jax.ShapeDtypeStruct((d4, seq // d4, D_QKV), BF16),
                   jax.ShapeDtypeStruct((d16, seq // d16, D_QKV), BF16)],
        scratch_shapes=[pltpu.VMEM((heads, tm, HEAD_DIM), F32)],
        semantics=("parallel", "parallel"), vmem_mib=52, args=(x, w_in_g, *tabs), comm=comm)
    return [o1[None], o4, o16], exchanged


def _in_proj_pool_gate(x, w_in_g, comm=None):
    seq = x.shape[0]
    tm, tn = 512, SHARD_IN
    first_shard = D_QKV // tn

    def body(x_ref, w_ref, o_ref):
        o_ref[...] = _dot_nn(x_ref[...].astype(BF16), w_ref[...])

    (hug,), exchanged = _call(
        body, name="in_proj_pool_gate", grid=(D_UG // tn, seq // tm),
        in_specs=[pl.BlockSpec((tm, D_MODEL), lambda s, i: (i, 0)),
                  pl.BlockSpec((None, D_MODEL, tn), lambda s, i: (s + first_shard, 0, 0))],
        out_specs=[pl.BlockSpec((tm, tn), lambda s, i: (i, s))],
        out_shape=[jax.ShapeDtypeStruct((seq, D_UG), F32)],
        scratch_shapes=[], semantics=("parallel", "parallel"), vmem_mib=48, args=(x, w_in_g), comm=comm)
    return hug, exchanged


def _band_masks():
    row = lax.broadcasted_iota(jnp.int32, (KEY_BLOCK, KEY_BLOCK), 0)
    col = lax.broadcasted_iota(jnp.int32, (KEY_BLOCK, KEY_BLOCK), 1)
    return col <= row, col >= row


def _attn_fwd(qkv, name):
    dil, n, _ = qkv.shape
    scale = HEAD_DIM ** -0.5
    lo, hi = slice(0, KEY_BLOCK), slice(KEY_BLOCK, CHUNK)

    def body(q_ref, k_ref, v_ref, kb_ref, vb_ref, o_ref, st_ref):
        i = pl.program_id(1)
        cur_mask, prev_mask = _band_masks()
        before_mask = jnp.logical_and(prev_mask, i > 0)
        lane = lax.broadcasted_iota(jnp.int32, (KEY_BLOCK, STAT_LANES), 1)
        tasks = [(rows, h) for rows in (lo, hi) for h in range(N_HEADS)]
        head = lambda h: slice(h * HEAD_DIM, (h + 1) * HEAD_DIM)

        def prev_of(rows, h):
            if rows is lo:
                return kb_ref[:, head(h)], vb_ref[:, head(h)], before_mask
            return k_ref[lo, head(h)], v_ref[lo, head(h)], prev_mask

        scores = []
        for rows, h in tasks:
            q = q_ref[rows, head(h)]
            scores.append((_dot_nt(q, prev_of(rows, h)[0]), _dot_nt(q, k_ref[rows, head(h)])))
        probs = []
        for (rows, h), (qk_prev, qk_cur) in zip(tasks, scores):
            s_prev = jnp.where(prev_of(rows, h)[2], qk_prev * scale, NEG)
            s_cur = jnp.where(cur_mask, qk_cur * scale, NEG)
            m = jnp.max(jnp.maximum(s_prev, s_cur), axis=-1, keepdims=True)
            p_prev = jnp.exp(s_prev - m)
            p_cur = jnp.exp(s_cur - m)
            den = jnp.sum(p_prev + p_cur, axis=-1, keepdims=True)
            probs.append((p_prev.astype(BF16), p_cur.astype(BF16), den, m + jnp.log(den)))
        stats = [jnp.zeros((KEY_BLOCK, STAT_LANES), F32), jnp.zeros((KEY_BLOCK, STAT_LANES), F32)]
        for (rows, h), (p_prev, p_cur, den, lse) in zip(tasks, probs):
            o = _dot_nn(p_cur, v_ref[rows, head(h)]) + _dot_nn(p_prev, prev_of(rows, h)[1])
            o_ref[rows, head(h)] = (o / den).astype(BF16)
            b = 0 if rows is lo else 1
            stats[b] = jnp.where(lane == h, lse, stats[b])
        st_ref[lo, :] = stats[0]
        st_ref[hi, :] = stats[1]

    main = lambda cb: pl.BlockSpec((None, CHUNK, D_ATTN), lambda r, i: (r, i, cb))
    before = lambda cb: pl.BlockSpec((None, KEY_BLOCK, D_ATTN), lambda r, i: (r, jnp.maximum(2 * i - 1, 0), cb))
    return _pallas(
        body, name=name, grid=(dil, n // CHUNK),
        in_specs=[main(0), main(1), main(2), before(1), before(2)],
        out_specs=[main(0), pl.BlockSpec((None, CHUNK, STAT_LANES), lambda r, i: (r, i, 0))],
        out_shape=[jax.ShapeDtypeStruct((dil, n, D_ATTN), BF16), jax.ShapeDtypeStruct((dil, n, STAT_LANES), F32)],
        compiler_params=_params(("parallel", "parallel"), 40),
    )(qkv, qkv, qkv, qkv, qkv)


def _attn_bwd(qkv, do, stats, name, comm=None):
    dil, n, _ = qkv.shape
    n_blocks = n // KEY_BLOCK
    last = n // CHUNK - 1
    scale = HEAD_DIM ** -0.5
    lo, hi = slice(0, KEY_BLOCK), slice(KEY_BLOCK, CHUNK)

    def body(q_ref, k_ref, v_ref, kb_ref, vb_ref, qa_ref, do_ref, doa_ref, st_ref, sta_ref, dq_ref, dk_ref, dv_ref):
        i = pl.program_id(1)
        cur_mask, prev_mask = _band_masks()
        before_mask = jnp.logical_and(prev_mask, i > 0)
        after_mask = jnp.logical_and(prev_mask, i < last)

        rows_cat = lambda a, b: jnp.concatenate([a, b], axis=0)
        masks = (jnp.concatenate([before_mask, cur_mask], axis=1), jnp.concatenate([prev_mask, cur_mask], axis=1),
                 after_mask)

        def operands(h):
            cols = slice(h * HEAD_DIM, (h + 1) * HEAD_DIM)
            lse_c, del_c = slice(h, h + 1), slice(N_HEADS + h, N_HEADS + h + 1)
            q = (q_ref[lo, cols], q_ref[hi, cols], qa_ref[:, cols])
            do = (do_ref[lo, cols], do_ref[hi, cols], doa_ref[:, cols])
            keys = (rows_cat(kb_ref[:, cols], k_ref[lo, cols]), k_ref[:, cols], k_ref[hi, cols])
            vals = (rows_cat(vb_ref[:, cols], v_ref[lo, cols]), v_ref[:, cols], v_ref[hi, cols])
            st = ((st_ref[lo, lse_c], st_ref[lo, del_c]), (st_ref[hi, lse_c], st_ref[hi, del_c]),
                  (sta_ref[:, lse_c], sta_ref[:, del_c]))
            return cols, q, do, keys, vals, st

        group = N_HEADS // 2
        for first_head in range(0, N_HEADS, group):
            heads = range(first_head, first_head + group)
            raw = {}
            for h in heads:
                _, q, do, keys, vals, _ = operands(h)
                raw[h] = [(_dot_nt(q[j], keys[j]), _dot_nt(do[j], vals[j])) for j in range(3)]
            grads = {}
            for h in heads:
                st = operands(h)[5]
                grads[h] = []
                for j in range(3):
                    qk, dp = raw[h][j]
                    lse, delta = st[j]
                    p = jnp.exp(jnp.where(masks[j], qk * scale, NEG) - lse)
                    grads[h].append((p.astype(BF16), (p * (dp - delta) * scale).astype(BF16)))
            for h in heads:
                cols, q, do, keys, _, _ = operands(h)
                (p0, ds0), (p1, ds1), (pa, dsa) = grads[h]
                own, nxt = slice(KEY_BLOCK, CHUNK), slice(0, KEY_BLOCK)

                def put(ref, rows, val, cols=cols):
                    ref[rows, cols] = val.astype(ref.dtype)

                put(dq_ref, lo, _dot_nn(ds0, keys[0]))
                put(dq_ref, hi, _dot_nn(ds1, keys[1]))
                put(dk_ref, lo, _dot_tn(rows_cat(ds0[:, own], ds1[:, nxt]), q_ref[:, cols]))
                put(dk_ref, hi, _dot_tn(rows_cat(ds1[:, own], dsa), rows_cat(q[1], q[2])))
                put(dv_ref, lo, _dot_tn(rows_cat(p0[:, own], p1[:, nxt]), do_ref[:, cols]))
                put(dv_ref, hi, _dot_tn(rows_cat(p1[:, own], pa), rows_cat(do[1], do[2])))

    def spec(rows, width, row_of, cb):
        return pl.BlockSpec((None, rows, width), lambda r, i: (r, row_of(i), cb))

    same = lambda i: i
    before = lambda i: jnp.maximum(2 * i - 1, 0)
    after = lambda i: jnp.minimum(2 * i + 2, n_blocks - 1)
    out = spec(CHUNK, D_ATTN, same, 0)
    return _call(
        body, name=name, grid=(dil, n // CHUNK),
        in_specs=[spec(CHUNK, D_ATTN, same, 0), spec(CHUNK, D_ATTN, same, 1), spec(CHUNK, D_ATTN, same, 2),
                  spec(KEY_BLOCK, D_ATTN, before, 1), spec(KEY_BLOCK, D_ATTN, before, 2),
                  spec(KEY_BLOCK, D_ATTN, after, 0),
                  spec(CHUNK, D_ATTN, same, 0), spec(KEY_BLOCK, D_ATTN, after, 0),
                  spec(CHUNK, STAT_LANES, same, 0), spec(KEY_BLOCK, STAT_LANES, after, 0)],
        out_specs=[out, out, out],
        out_shape=[jax.ShapeDtypeStruct((dil, n, D_ATTN), BF16)] * 3,
        scratch_shapes=[], semantics=("parallel", "parallel"), vmem_mib=40,
        args=(qkv, qkv, qkv, qkv, qkv, qkv, do, do, stats, stats), comm=comm)


def _window_sums(ext, window, backward):
    rows = ext.shape[0]
    acc, span = ext, 1
    while span < window:
        acc = acc + pltpu.roll(acc, (rows - span) if backward else span, axis=0)
        span *= 2
    return acc


def _mix_gate(o_list, st_list, hug, w_pool_g, pool_scale):
    seq = hug.shape[0]
    tm = 256
    halo_blocks = tm // POOL_HALO
    d4, d16 = DILATIONS[1], DILATIONS[2]

    def body(o1_ref, o4_ref, o16_ref, l1_ref, l4_ref, l16_ref, u_ref, halo_ref, ga_ref, gp_ref, wp_ref, sc_ref,
             y_ref, mix_ref, lse_ref, pooled_ref, n4_ref, n16_ref, nl4_ref, nl16_ref):
        i = pl.program_id(0)
        _from_pattern(o4_ref, n4_ref, d4)
        _from_pattern(o16_ref, n16_ref, d16)
        _from_pattern(l4_ref, nl4_ref, d4)
        _from_pattern(l16_ref, nl16_ref, d16)
        la, lb, lc = l1_ref[...], nl4_ref[0], nl16_ref[0]
        mx = jnp.maximum(jnp.maximum(la, lb), lc)
        ea, eb, ec = jnp.exp(la - mx), jnp.exp(lb - mx), jnp.exp(lc - mx)
        tot = ea + eb + ec
        lse_ref[...] = mx + jnp.log(tot)
        wa, wb, wc = ea / tot, eb / tot, ec / tot
        ga = ga_ref[...]
        silu_a = ga * jax.nn.sigmoid(ga)
        for h in range(N_HEADS):
            cols = slice(h * HEAD_DIM, (h + 1) * HEAD_DIM)
            hc = slice(h, h + 1)
            attn = wa[:, hc] * o1_ref[:, cols].astype(F32) + wb[:, hc] * n4_ref[h] + wc[:, hc] * n16_ref[h]
            mix_ref[:, cols] = attn
            y_ref[:, cols] = (attn * silu_a[:, cols]).astype(BF16)

        u = u_ref[...]
        halo = jnp.where(i > 0, halo_ref[...], 0.0)
        ext = jnp.concatenate([halo, u], axis=0)
        pos = i * tm + lax.broadcasted_iota(jnp.int32, (tm, 1), 0)
        gp = gp_ref[...]
        gated_scale = sc_ref[...] * (gp * jax.nn.sigmoid(gp))
        for g, window in enumerate(POOL_WINDOWS):
            cols = slice(g * POOL_GROUP_DIM, (g + 1) * POOL_GROUP_DIM)
            sums = _window_sums(ext[:, cols], window, backward=False)[POOL_HALO:, :]
            count = jnp.minimum(pos + 1, window).astype(F32)
            pooled = (sums / count - u[:, cols]).astype(BF16)
            pooled_ref[:, cols] = pooled
            pre = _dot_nn(pooled, wp_ref[g])
            out_cols = slice(D_ATTN + g * POOL_GROUP_DIM, D_ATTN + (g + 1) * POOL_GROUP_DIM)
            mix_ref[:, out_cols] = pre
            y_ref[:, out_cols] = (pre * gated_scale[:, cols]).astype(BF16)

    row = lambda width, cb=0: pl.BlockSpec((tm, width), lambda i: (i, cb))
    pat = lambda d, width: pl.BlockSpec((d, tm // d, width), lambda i: (0, i, 0))
    return _pallas(
        body, name="mix_gate", grid=(seq // tm,),
        in_specs=[row(D_ATTN), pat(d4, D_ATTN), pat(d16, D_ATTN),
                  row(STAT_LANES), pat(d4, STAT_LANES), pat(d16, STAT_LANES),
                  row(D_POOL),
                  pl.BlockSpec((POOL_HALO, D_POOL), lambda i: (jnp.maximum(i * halo_blocks - 1, 0), 0)),
                  row(D_ATTN, 1), row(D_POOL, 2),
                  pl.BlockSpec((len(POOL_WINDOWS), POOL_GROUP_DIM, POOL_GROUP_DIM), lambda i: (0, 0, 0)),
                  pl.BlockSpec((1, D_POOL), lambda i: (0, 0))],
        out_specs=[row(D_MODEL), row(D_MODEL), row(STAT_LANES), row(D_POOL)],
        out_shape=[jax.ShapeDtypeStruct((seq, D_MODEL), BF16), jax.ShapeDtypeStruct((seq, D_MODEL), F32),
                   jax.ShapeDtypeStruct((seq, STAT_LANES), F32), jax.ShapeDtypeStruct((seq, D_POOL), BF16)],
        scratch_shapes=[pltpu.VMEM((N_HEADS, tm, HEAD_DIM), F32), pltpu.VMEM((N_HEADS, tm, HEAD_DIM), F32),
                        pltpu.VMEM((1, tm, STAT_LANES), F32), pltpu.VMEM((1, tm, STAT_LANES), F32)],
        compiler_params=_params(("parallel",), 48),
    )(o_list[0][0], o_list[1], o_list[2], st_list[0][0], st_list[1], st_list[2],
      hug, hug, hug, hug, w_pool_g, pool_scale)


def _out_proj_loss(y, w_out_g, x, target, gain, bias):
    seq = x.shape[0]
    tm = 512

    def body(y_ref, w_ref, x_ref, t_ref, g_ref, b_ref, dz_ref, dzb_ref, gg_ref, gb_ref, loss_ref):
        @pl.when(pl.program_id(0) == 0)
        def _():
            gg_ref[...] = jnp.zeros_like(gg_ref)
            gb_ref[...] = jnp.zeros_like(gb_ref)
            loss_ref[...] = jnp.zeros_like(loss_ref)

        halves = [slice(0, tm // 2), slice(tm // 2, tm)]
        projected = [_dot_nn(y_ref[rows, :], w_ref[...]) for rows in halves]
        for rows, out in zip(halves, projected):
            z = DEEPNORM_ALPHA * x_ref[rows, :] + out
            mu = jnp.mean(z, axis=-1, keepdims=True)
            zc = z - mu
            rstd = lax.rsqrt(jnp.mean(zc * zc, axis=-1, keepdims=True) + LN_EPS)
            xhat = zc * rstd
            gain_v = g_ref[...]
            diff = xhat * gain_v + b_ref[...] - t_ref[rows, :]
            sq = _fold_rows(diff * diff)
            part = sq[:, :128]
            for k in range(1, D_MODEL // 128):
                part = part + sq[:, k * 128:(k + 1) * 128]
            loss_ref[...] += part
            dln = diff * (1.0 / D_MODEL)
            gg_ref[...] += _fold_rows(dln * xhat)
            gb_ref[...] += _fold_rows(dln)
            dxhat = dln * gain_v
            dz = rstd * (dxhat - jnp.mean(dxhat, axis=-1, keepdims=True)
                         - xhat * jnp.mean(dxhat * xhat, axis=-1, keepdims=True))
            dz_ref[rows, :] = dz
            dzb_ref[rows, :] = dz.astype(BF16)

    row = lambda: pl.BlockSpec((tm, D_MODEL), lambda i: (i, 0))
    vec = lambda: pl.BlockSpec((1, D_MODEL), lambda i: (0, 0))
    acc = lambda width: pl.BlockSpec((8, width), lambda i: (0, 0))
    return _pallas(
        body, name="out_proj_loss", grid=(seq // tm,),
        in_specs=[row(), pl.BlockSpec((D_MODEL, D_MODEL), lambda i: (0, 0), pipeline_mode=pl.Buffered(1)),
                  row(), row(), vec(), vec()],
        out_specs=[row(), row(), acc(D_MODEL), acc(D_MODEL), acc(128)],
        out_shape=[jax.ShapeDtypeStruct((seq, D_MODEL), F32), jax.ShapeDtypeStruct((seq, D_MODEL), BF16),
                   jax.ShapeDtypeStruct((8, D_MODEL), F32), jax.ShapeDtypeStruct((8, D_MODEL), F32),
                   jax.ShapeDtypeStruct((8, 128), F32)],
        compiler_params=_params(("arbitrary",), 56),
    )(y, w_out_g.reshape(D_MODEL, D_MODEL), x, target, gain, bias)


def _dy_gate_bwd(dzb, w_out_g, hug, mixpre, pool_scale, lse_all):
    seq = dzb.shape[0]
    tm = 256
    d4, d16 = DILATIONS[1], DILATIONS[2]

    def body(dz_ref, w_ref, ga_ref, gp_ref, mix_ref, sc_ref, lse_ref,
             dh_ref, dpo_ref, do1_ref, do4_ref, do16_ref, st1_ref, st4_ref, st16_ref, da_ref, st_ref):
        dy = _dot_nt(dz_ref[...], w_ref[...])
        ga = ga_ref[...]
        sig = jax.nn.sigmoid(ga)
        attn = mix_ref[:, :D_ATTN]
        dya = dy[:, :D_ATTN]
        dattn = dya * (ga * sig)
        dh_ref[:, :D_ATTN] = (dya * attn * (sig * (1.0 + ga * (1.0 - sig)))).astype(BF16)
        _store_slabs(da_ref, dattn)
        lane = lax.broadcasted_iota(jnp.int32, (tm, STAT_LANES), 1)
        stats = lse_ref[...]
        prod = dattn * attn
        for h in range(N_HEADS):
            delta = jnp.sum(prod[:, h * HEAD_DIM:(h + 1) * HEAD_DIM], axis=-1, keepdims=True)
            stats = jnp.where(lane == N_HEADS + h, delta, stats)
        st_ref[0] = stats
        do1_ref[...] = dattn.astype(BF16)
        st1_ref[...] = stats
        _to_pattern(da_ref, do4_ref, d4, BF16)
        _to_pattern(da_ref, do16_ref, d16, BF16)
        _to_pattern(st_ref, st4_ref, d4, F32)
        _to_pattern(st_ref, st16_ref, d16, F32)

        gp = gp_ref[...]
        sig = jax.nn.sigmoid(gp)
        dyp = dy[:, D_ATTN:]
        dpo_ref[...] = dyp * (gp * sig)
        dh_ref[:, D_ATTN:] = (dyp * (mix_ref[:, D_ATTN:] * sc_ref[...])
                              * (sig * (1.0 + gp * (1.0 - sig)))).astype(BF16)

    row = lambda width, cb=0: pl.BlockSpec((tm, width), lambda i: (i, cb))
    pat = lambda d, width: pl.BlockSpec((d, tm // d, width), lambda i: (0, i, 0))
    pat_shape = lambda d, width, dtype: jax.ShapeDtypeStruct((d, seq // d, width), dtype)
    outs = _pallas(
        body, name="dy_gate_bwd", grid=(seq // tm,),
        in_specs=[row(D_MODEL), pl.BlockSpec((D_MODEL, D_MODEL), lambda i: (0, 0)),
                  row(D_ATTN, 1), row(D_POOL, 2), row(D_MODEL), pl.BlockSpec((1, D_POOL), lambda i: (0, 0)),
                  row(STAT_LANES)],
        out_specs=[row(D_MODEL, D_IN // D_MODEL - 1), row(D_POOL),
                   row(D_ATTN), pat(d4, D_ATTN), pat(d16, D_ATTN),
                   row(STAT_LANES), pat(d4, STAT_LANES), pat(d16, STAT_LANES)],
        out_shape=[jax.ShapeDtypeStruct((seq, D_IN), BF16), jax.ShapeDtypeStruct((seq, D_POOL), F32),
                   jax.ShapeDtypeStruct((seq, D_ATTN), BF16), pat_shape(d4, D_ATTN, BF16), pat_shape(d16, D_ATTN, BF16),
                   jax.ShapeDtypeStruct((seq, STAT_LANES), F32), pat_shape(d4, STAT_LANES, F32),
                   pat_shape(d16, STAT_LANES, F32)],
        scratch_shapes=[pltpu.VMEM((N_HEADS, tm, HEAD_DIM), F32), pltpu.VMEM((1, tm, STAT_LANES), F32)],
        compiler_params=_params(("parallel",), 48),
    )(dzb, w_out_g.reshape(D_MODEL, D_MODEL), hug, hug, mixpre, pool_scale, lse_all)
    dh, dpo, do1, do4, do16, st1, st4, st16 = outs
    return dh, dpo, [do1[None], do4, do16], [st1[None], st4, st16]


def _pool_bwd(dh, dpo, mixpre, pooled, w_pool_g, pool_scale):
    seq = dpo.shape[0]
    tm = 256
    halo_blocks = tm // POOL_HALO
    last = seq // tm - 1
    n_groups = len(POOL_WINDOWS)

    def body(dh_in_ref, dpo_ref, halo_ref, pre_ref, pooled_ref, wp_ref, sc_ref, du_ref, gw_ref, gs_ref):
        i = pl.program_id(0)

        @pl.when(i == 0)
        def _():
            gw_ref[...] = jnp.zeros_like(gw_ref)
            gs_ref[...] = jnp.zeros_like(gs_ref)

        dpo = dpo_ref[...]
        scale = sc_ref[...]
        gs_ref[...] += _fold_rows(dpo * pre_ref[...])
        halo = jnp.where(i < last, halo_ref[...], 0.0)
        dpw = (jnp.concatenate([dpo, halo], axis=0) * scale).astype(BF16)
        pos = i * tm + lax.broadcasted_iota(jnp.int32, (tm + POOL_HALO, 1), 0)
        for g, window in enumerate(POOL_WINDOWS):
            cols = slice(g * POOL_GROUP_DIM, (g + 1) * POOL_GROUP_DIM)
            dpw_g = dpw[:, cols]
            gw_ref[g] += _dot_tn(pooled_ref[:, cols], dpw_g[:tm, :])
            dpooled = _dot_nt(dpw_g, wp_ref[g])
            count = jnp.minimum(pos + 1, window).astype(F32)
            sums = _window_sums(dpooled / count, window, backward=True)
            du_ref[:, cols] = (sums[:tm, :] - dpooled[:tm, :]).astype(BF16)

    row = lambda width, cb=0: pl.BlockSpec((tm, width), lambda i: (i, cb))
    return _pallas(
        body, name="pool_bwd", grid=(seq // tm,),
        in_specs=[ANY, row(D_POOL),
                  pl.BlockSpec((POOL_HALO, D_POOL),
                               lambda i: (jnp.minimum((i + 1) * halo_blocks, seq // POOL_HALO - 1), 0)),
                  row(D_POOL, 1), row(D_POOL),
                  pl.BlockSpec((n_groups, POOL_GROUP_DIM, POOL_GROUP_DIM), lambda i: (0, 0, 0)),
                  pl.BlockSpec((1, D_POOL), lambda i: (0, 0))],
        out_specs=[row(D_POOL, D_QKV // D_POOL),
                   pl.BlockSpec((n_groups, POOL_GROUP_DIM, POOL_GROUP_DIM), lambda i: (0, 0, 0)),
                   pl.BlockSpec((8, D_POOL), lambda i: (0, 0))],
        out_shape=[jax.ShapeDtypeStruct(dh.shape, dh.dtype),
                   jax.ShapeDtypeStruct((n_groups, POOL_GROUP_DIM, POOL_GROUP_DIM), F32),
                   jax.ShapeDtypeStruct((8, D_POOL), F32)],
        input_output_aliases={0: 0},
        compiler_params=_params(("arbitrary",), 40),
    )(dh, dpo, dpo, mixpre, pooled, w_pool_g, pool_scale)


def _sum_patterns(dh, parts, tabs, unrotate, col_block, name, comm=None):
    seq = dh.shape[0]
    tm, tn = 256, D_ATTN
    per = D_ATTN // tn
    d4, d16 = DILATIONS[1], DILATIONS[2]

    def body(dh_in_ref, a1_ref, a4_ref, a16_ref, ct_ref, up_ref, down_ref, o_ref, n4_ref, n16_ref):
        _from_pattern(a4_ref, n4_ref, d4)
        _from_pattern(a16_ref, n16_ref, d16)
        for s in range(tn // HEAD_DIM):
            cols = slice(s * HEAD_DIM, (s + 1) * HEAD_DIM)
            tot = a1_ref[:, cols].astype(F32) + n4_ref[s] + n16_ref[s]
            if unrotate:
                tot = _rotate_heads(tot, ct_ref[...], -up_ref[...], -down_ref[...])
            o_ref[:, cols] = tot.astype(BF16)

    tab = pl.BlockSpec((tm, HEAD_DIM), lambda i, j: (i, 0))
    pat = lambda d: pl.BlockSpec((d, tm // d, tn), lambda i, j: (0, i, j))
    (dh,), exchanged = _call(
        body, name=name, grid=(seq // tm, per),
        in_specs=[ANY, pl.BlockSpec((tm, tn), lambda i, j: (i, j)), pat(d4), pat(d16), tab, tab, tab],
        out_specs=[pl.BlockSpec((tm, tn), lambda i, j: (i, col_block * per + j))],
        out_shape=[jax.ShapeDtypeStruct(dh.shape, dh.dtype)],
        scratch_shapes=[pltpu.VMEM((tn // HEAD_DIM, tm, HEAD_DIM), F32), pltpu.VMEM((tn // HEAD_DIM, tm, HEAD_DIM), F32)],
        semantics=("parallel", "parallel"), vmem_mib=32, args=(dh, parts[0][0], parts[1], parts[2], *tabs),
        aliases={0: 0}, comm=comm)
    return dh, exchanged


def _grad_w_in(x, dh, half, name, comm=None):
    seq = x.shape[0]
    ts, td, te = 2048, D_MODEL // 2, SHARD_IN

    def body(half_ref, x_ref, dh_ref, o_ref):
        k = pl.program_id(1)
        part = _dot_tn(x_ref[...].astype(BF16), dh_ref[...])

        @pl.when(k == 0)
        def _():
            o_ref[...] = part

        @pl.when(k > 0)
        def _():
            o_ref[...] += part

    (g,), exchanged = _call(
        body, name=name, grid=(N_SHARDS, seq // ts),
        in_specs=[pl.BlockSpec((ts, td), lambda e, k, half_ref: (k, half_ref[0])),
                  pl.BlockSpec((ts, te), lambda e, k, half_ref: (k, e))],
        out_specs=[pl.BlockSpec((None, td, te), lambda e, k, half_ref: (e, 0, 0))],
        out_shape=[jax.ShapeDtypeStruct((N_SHARDS, td, te), F32)],
        scratch_shapes=[], semantics=("parallel", "arbitrary"), vmem_mib=56, args=(x, dh), comm=comm,
        prefetch=(half,))
    return g, exchanged


def _grad_w_out(y, dzb):
    seq = y.shape[0]
    ts, te = 512, 1024
    nk = seq // ts

    def body(y_ref, dz_ref, o_ref, acc_ref):
        k = pl.program_id(1)

        @pl.when(k == 0)
        def _():
            acc_ref[...] = jnp.zeros_like(acc_ref)

        acc_ref[...] += _dot_tn(y_ref[...], dz_ref[...])

        @pl.when(k == nk - 1)
        def _():
            o_ref[...] = acc_ref[...]

    return _pallas(
        body, name="grad_w_out", grid=(D_MODEL // te, nk),
        in_specs=[pl.BlockSpec((ts, te), lambda e, k: (k, e)), pl.BlockSpec((ts, D_MODEL), lambda e, k: (k, 0))],
        out_specs=pl.BlockSpec((te, D_MODEL), lambda e, k: (e, 0)),
        out_shape=jax.ShapeDtypeStruct((D_MODEL, D_MODEL), F32),
        scratch_shapes=[pltpu.VMEM((te, D_MODEL), F32)],
        compiler_params=_params(("parallel", "arbitrary"), 48),
    )(y, dzb)


GRAD_X_LATE_SHARDS = 1


def _grad_x_partial(dh, w_in_g, first, tiles, prev=None, comm=None):
    seq = dh.shape[0]
    tm, tk = 512, SHARD_IN

    def body(*refs):
        dh_ref, w_ref, o_ref = refs[-3:]
        k = pl.program_id(1)
        part = _dot_nt(dh_ref[...], w_ref[...])

        @pl.when(k == 0)
        def _():
            o_ref[...] = part

        @pl.when(k > 0)
        def _():
            o_ref[...] += part

    carried = [] if prev is None else [prev]
    (partial,), exchanged = _call(
        body, name="grad_x_partial_%d" % first, grid=(tiles, N_SHARDS - GRAD_X_LATE_SHARDS),
        in_specs=[ANY] * len(carried) + [
            pl.BlockSpec((tm, tk), lambda i, k: (i + first, k)),
            pl.BlockSpec((None, D_MODEL, tk), lambda i, k: (k, 0, 0))],
        out_specs=[pl.BlockSpec((tm, D_MODEL), lambda i, k: (i + first, 0))],
        out_shape=[jax.ShapeDtypeStruct((seq, D_MODEL), F32)],
        scratch_shapes=[], semantics=("parallel", "arbitrary"), vmem_mib=48, args=(*carried, dh, w_in_g),
        aliases={0: 0} if carried else None, comm=comm)
    return partial, exchanged


def _grad_x_final(dh, w_in_g, dz, partial):
    seq = dh.shape[0]
    tm, tk = 512, SHARD_IN
    k0 = N_SHARDS - GRAD_X_LATE_SHARDS

    def body(dh_ref, w_ref, dz_ref, p_ref, o_ref):
        k = pl.program_id(1)
        part = _dot_nt(dh_ref[...], w_ref[...])

        @pl.when(k == 0)
        def _():
            o_ref[...] = (DEEPNORM_ALPHA * dz_ref[...] + p_ref[...]) + part

        @pl.when(k > 0)
        def _():
            o_ref[...] += part

    row = pl.BlockSpec((tm, D_MODEL), lambda i, k: (i, 0))
    return _pallas(
        body, name="grad_x_final", grid=(seq // tm, GRAD_X_LATE_SHARDS),
        in_specs=[pl.BlockSpec((tm, tk), lambda i, k: (i, k + k0)),
                  pl.BlockSpec((None, D_MODEL, tk), lambda i, k: (k + k0, 0, 0)), row, row],
        out_specs=row, out_shape=jax.ShapeDtypeStruct((seq, D_MODEL), F32),
        compiler_params=_params(("parallel", "arbitrary"), 48),
    )(dh, w_in_g, dz, partial)


def _pool_weight(w_pool_sh):
    n_groups = len(POOL_WINDOWS)
    shard_c = POOL_GROUP_DIM // N_SHARDS
    return (w_pool_sh.reshape(N_SHARDS, n_groups, shard_c, POOL_GROUP_DIM).transpose(1, 0, 2, 3)
            .reshape(n_groups, POOL_GROUP_DIM, POOL_GROUP_DIM))


def _pool_grad_pieces(g_w_pool):
    n_groups = len(POOL_WINDOWS)
    half_c = POOL_GROUP_DIM // N_SHARDS // 2
    return (g_w_pool.reshape(n_groups, N_SHARDS, 2, half_c, POOL_GROUP_DIM).transpose(1, 2, 0, 3, 4)
            .reshape(N_SHARDS, 2, n_groups * half_c, POOL_GROUP_DIM))


def _step(x, target, w_in_g, w_rest, pool_scale, gain, bias, place=None):
    seq = x.shape[0]
    tabs = _rope_tables(seq)
    qkv, gathered = _in_proj_qkv(x, w_in_g, tabs, comm=_allgather_weights(w_rest, "neighbours") if place else None)
    hug, gathered = _in_proj_pool_gate(x, w_in_g, _allgather_weights(gathered, "diagonal") if place else None)
    w_out_g, w_pool_sh = gathered if place else w_rest
    w_pool_g = _pool_weight(w_pool_sh)
    o_list, st_list = [], []
    for p, dil in enumerate(DILATIONS):
        o, st = _attn_fwd(qkv[p], "attn_fwd_d%d" % dil)
        o_list.append(o)
        st_list.append(st)
    y, mixpre, lse_all, pooled = _mix_gate(o_list, st_list, hug, w_pool_g, pool_scale)
    dz, dzb, gain_part, bias_part, loss_part = _out_proj_loss(y, w_out_g, x, target, gain, bias)
    dh, dpo, do_list, stat_list = _dy_gate_bwd(dzb, w_out_g, hug, mixpre, pool_scale, lse_all)
    g_w_out = _grad_w_out(y, dzb)
    dh, g_w_pool, scale_part = _pool_bwd(dh, dpo, mixpre, pooled, w_pool_g, pool_scale)
    small = jnp.concatenate([scale_part, gain_part, bias_part, loss_part], axis=1)
    early = [g_w_out.reshape(N_SHARDS, 2, D_MODEL // (2 * N_SHARDS), D_MODEL), _pool_grad_pieces(g_w_pool)]

    bwd = lambda p, comm: _attn_bwd(qkv[p], do_list[p], stat_list[p], "attn_bwd_d%d" % DILATIONS[p], comm)
    if place is None:
        parts = [bwd(p, None)[0] for p in range(3)]
    else:
        core, chip_core, onward = place
        part_a, recv = bwd(0, _exchange_halves(early))
        sums = [_add_own_half(g, r, core, "add_own_half_%d" % a) for a, (g, r) in enumerate(zip(early, recv))]
        part_b, recv = bwd(1, _scatter_to_chips([s[1] for s in sums]))
        bufs = [_add_chips(s[0], r, chip_core, "add_chips_%d" % a) for a, (s, r) in enumerate(zip(sums, recv))]
        part_c, early = bwd(2, _share_with_sibling(bufs))
        parts = [part_a, part_b, part_c]
    dh, gathered = _sum_patterns(dh, [t[0] for t in parts], tabs, True, 0, "sum_dq",
                                 _gather_small(small) if place else None)
    dh, _ = _sum_patterns(dh, [t[1] for t in parts], tabs, True, 1, "sum_dk")
    dh, _ = _sum_patterns(dh, [t[2] for t in parts], tabs, False, 2, "sum_dv")
    if place:
        small = (small, gathered[0])
    if place is None:
        halves = [_grad_w_in(x, dh, jnp.full((1,), h, jnp.int32), "grad_w_in_%d" % h)[0] for h in range(2)]
        g_w_in = jnp.stack(halves, axis=1)
        g_x = _grad_x_final(dh, w_in_g, dz, _grad_x_partial(dh, w_in_g, 0, seq // 512)[0])
    else:
        give, _ = _grad_w_in(x, dh, 1 - core, "grad_w_in_give")
        keep, recv = _grad_w_in(x, dh, core, "grad_w_in_keep", _send_to_sibling([give]))
        total, total_b = _add_pair(keep, recv[0], "add_own_half_w_in")
        n_tiles = seq // 512
        tiles = 3 * n_tiles // 8
        part, relayed = _grad_x_partial(dh, w_in_g, 0, tiles, None, _relay_diagonal(total_b))
        total_b = _fold_relayed(total, total_b, relayed[0], onward)
        part, recv = _grad_x_partial(dh, w_in_g, tiles, n_tiles - tiles, part, _scatter_to_neighbours(total_b))
        buf = _add_chips(total, recv[0], chip_core, "add_chips_w_in")
        g_x = _grad_x_final(dh, w_in_g, dz, part)
        g_w_in = _run_exchange(_share_with_sibling([buf]), "share_w_in")[0]
    return g_x, g_w_in, early[0], early[1], small


def _exchange_halves(grads):
    n = len(grads)

    def copies(src, dst, sems):
        x, y, c, _ = _mesh_place()
        return [_remote(src[a].at[j, 1 - c], dst[a].at[j], sems[0].at[a, j], sems[1].at[a, j], (x, y, 1 - c))
                for a in range(n) for j in range(N_SHARDS)]

    def start(src, dst, sems):
        for cp in copies(src, dst, sems):
            cp.start()

    def finish(src, dst, sems):
        for cp in copies(src, dst, sems):
            cp.wait()

    return _Exchange(grads, [jax.ShapeDtypeStruct((N_SHARDS,) + g.shape[2:], g.dtype) for g in grads], {},
                     [pltpu.SemaphoreType.DMA((n, N_SHARDS))] * 2, start, finish)


def _add_own_half(grad, recv, core, name):
    _, _, r, c = grad.shape
    tr = min(r, 256)

    def body(core_ref, g_ref, r_ref, o_ref, ob_ref):
        tot = g_ref[...] + r_ref[...]
        o_ref[...] = tot
        ob_ref[...] = tot.astype(BF16)

    out = pl.BlockSpec((None, tr, c), lambda j, i, core_ref: (j, i, 0))
    return _pallas(
        body, name=name,
        grid_spec=pltpu.PrefetchScalarGridSpec(
            num_scalar_prefetch=1, grid=(N_SHARDS, r // tr),
            in_specs=[pl.BlockSpec((None, None, tr, c), lambda j, i, core_ref: (j, core_ref[0], i, 0)),
                      pl.BlockSpec((None, tr, c), lambda j, i, core_ref: (j, i, 0))],
            out_specs=[out, out]),
        out_shape=[jax.ShapeDtypeStruct((N_SHARDS, r, c), F32), jax.ShapeDtypeStruct((N_SHARDS, r, c), BF16)],
        compiler_params=_params(("parallel", "parallel"), 32),
    )(core, grad, recv)


def _send_to_sibling(arrays):
    n = len(arrays)

    def copies(src, dst, sems):
        x, y, c, _ = _mesh_place()
        return [_remote(src[a], dst[a], sems[0].at[a], sems[1].at[a], (x, y, 1 - c)) for a in range(n)]

    def start(src, dst, sems):
        for cp in copies(src, dst, sems):
            cp.start()

    def finish(src, dst, sems):
        for cp in copies(src, dst, sems):
            cp.wait()

    return _Exchange(arrays, [jax.ShapeDtypeStruct(t.shape, t.dtype) for t in arrays], {},
                     [pltpu.SemaphoreType.DMA((n,))] * 2, start, finish)


def _add_pair(a, b, name):
    _, r, c = a.shape
    tr = min(r, 256)

    def body(a_ref, b_ref, o_ref, ob_ref):
        tot = a_ref[...] + b_ref[...]
        o_ref[...] = tot
        ob_ref[...] = tot.astype(BF16)

    spec = pl.BlockSpec((None, tr, c), lambda j, i: (j, i, 0))
    return _pallas(
        body, name=name, grid=(N_SHARDS, r // tr), in_specs=[spec, spec], out_specs=[spec, spec],
        out_shape=[jax.ShapeDtypeStruct(a.shape, F32), jax.ShapeDtypeStruct(a.shape, BF16)],
        compiler_params=_params(("parallel", "parallel"), 32),
    )(a, b)


def _scatter_to_chips(sums, rows=None, into=None):
    n = len(sums)

    def copies(src, dst, sems):
        x, y, c, chips = _mesh_place()
        part = (lambda ref: ref) if rows is None else (lambda ref: ref.at[pl.ds(rows[0], rows[1])])
        return [_remote(part(src[a].at[2 * cx + cy]), part(dst[a].at[k]), sems[0].at[a, k], sems[1].at[a, k],
                        (cx, cy, c))
                for a in range(n) for k, (cx, cy) in enumerate(chips)]

    def start(src, dst, sems):
        for cp in copies(src, dst, sems):
            cp.start()

    def finish(src, dst, sems):
        for cp in copies(src, dst, sems):
            cp.wait()

    return _Exchange(sums + (into or []), [jax.ShapeDtypeStruct((3,) + s.shape[1:], s.dtype) for s in sums],
                     {n + a: a for a in range(n)} if into else {},
                     [pltpu.SemaphoreType.DMA((n, 3))] * 2, start, finish)


def _add_chips(sums, recv, chip_core, name):
    _, r, c = sums.shape
    n_recv = recv.shape[0]
    tr = min(r, 256)

    def body(cc_ref, s_ref, r_ref, o_ref):
        tot = s_ref[...]
        for k in range(n_recv):
            tot = tot + r_ref[k].astype(F32)
        o_ref[...] = tot

    return _pallas(
        body, name=name,
        grid_spec=pltpu.PrefetchScalarGridSpec(
            num_scalar_prefetch=1, grid=(r // tr,),
            in_specs=[pl.BlockSpec((None, tr, c), lambda i, cc_ref: (cc_ref[0], i, 0)),
                      pl.BlockSpec((n_recv, tr, c), lambda i, cc_ref: (0, i, 0))],
            out_specs=pl.BlockSpec((None, tr, c), lambda i, cc_ref: (cc_ref[1], i, 0))),
        out_shape=jax.ShapeDtypeStruct((2, r, c), F32),
        compiler_params=_params(("parallel",), 32),
    )(chip_core, sums, recv)


def _relay_diagonal(sums_b):
    def copy(src, dst, sems):
        x, y, c, _ = _mesh_place()
        diagonal = 2 * (1 - x) + (1 - y)
        return _remote(src[0].at[diagonal], dst[0], sems[0].at[0], sems[1].at[0], (x ^ (1 - c), y ^ c, c))

    def start(src, dst, sems):
        copy(src, dst, sems).start()

    def finish(src, dst, sems):
        copy(src, dst, sems).wait()

    return _Exchange([sums_b], [jax.ShapeDtypeStruct(sums_b.shape[1:], sums_b.dtype)], {},
                     [pltpu.SemaphoreType.DMA((1,))] * 2, start, finish)


def _fold_relayed(sums, sums_b, relayed, onward):
    _, r, c = sums.shape
    tr = min(r, 256)

    def body(on_ref, b_in_ref, s_ref, r_ref, o_ref):
        o_ref[...] = (s_ref[...] + r_ref[...].astype(F32)).astype(BF16)

    return _pallas(
        body, name="fold_relayed",
        grid_spec=pltpu.PrefetchScalarGridSpec(
            num_scalar_prefetch=1, grid=(r // tr,),
            in_specs=[ANY, pl.BlockSpec((None, tr, c), lambda i, on_ref: (on_ref[0], i, 0)),
                      pl.BlockSpec((tr, c), lambda i, on_ref: (i, 0))],
            out_specs=pl.BlockSpec((None, tr, c), lambda i, on_ref: (on_ref[0], i, 0))),
        out_shape=jax.ShapeDtypeStruct(sums_b.shape, sums_b.dtype),
        input_output_aliases={1: 0},
        compiler_params=_params(("parallel",), 32),
    )(onward, sums_b, sums, relayed)


def _scatter_to_neighbours(sums_b):
    def copies(src, dst, sems):
        x, y, c, chips = _mesh_place()
        return [_remote(src[0].at[2 * cx + cy], dst[0].at[k], sems[0].at[k], sems[1].at[k], (cx, cy, c))
                for k, (cx, cy) in enumerate(chips[:2])]

    def start(src, dst, sems):
        for cp in copies(src, dst, sems):
            cp.start()

    def finish(src, dst, sems):
        for cp in copies(src, dst, sems):
            cp.wait()

    return _Exchange([sums_b], [jax.ShapeDtypeStruct((2,) + sums_b.shape[1:], sums_b.dtype)], {},
                     [pltpu.SemaphoreType.DMA((2,))] * 2, start, finish)


def _share_with_sibling(bufs):
    n = len(bufs)

    def copies(dst, sems, half):
        x, y, c, _ = _mesh_place()
        h = c if half == "mine" else 1 - c
        return [_remote(dst[a].at[h], dst[a].at[h], sems[0].at[a], sems[1].at[a], (x, y, 1 - c)) for a in range(n)]

    def start(ins, dst, sems):
        for cp in copies(dst, sems, "mine"):
            cp.start()

    def finish(ins, dst, sems):
        for cp in copies(dst, sems, "theirs"):
            cp.wait_recv()
        for cp in copies(dst, sems, "mine"):
            cp.wait_send()

    return _Exchange(bufs, [jax.ShapeDtypeStruct(b.shape, b.dtype) for b in bufs], {a: a for a in range(n)},
                     [pltpu.SemaphoreType.DMA((n,))] * 2, start, finish)


def _adam_math(w, g, m, v):
    m = ADAM_B1 * m + (1.0 - ADAM_B1) * g
    v = ADAM_B2 * v + (1.0 - ADAM_B2) * (g * g)
    m_hat = m / (1.0 - ADAM_B1 ** ADAM_STEP)
    v_hat = v / (1.0 - ADAM_B2 ** ADAM_STEP)
    delta = -ADAM_LR * (m_hat / (jnp.sqrt(v_hat) + ADAM_EPS) + ADAM_WD * w)
    return delta, m, v


def _gather_small(small):
    def peers():
        x, y, c, _ = _mesh_place()
        return [(x ^ ((r >> 2) & 1), y ^ ((r >> 1) & 1), c ^ (r & 1)) for r in range(1, 8)], 4 * x + 2 * y + c

    def start(src, dst, sems):
        to, me = peers()
        for r, peer in enumerate(to):
            _remote(src[0], dst[0].at[me], sems[0].at[r], sems[1].at[r], peer).start()

    def finish(src, dst, sems):
        to, me = peers()
        for r, (px, py, pc) in enumerate(to):
            theirs = dst[0].at[4 * px + 2 * py + pc]
            _remote(theirs, theirs, sems[0].at[r], sems[1].at[r], (px, py, pc)).wait_recv()
        for r, peer in enumerate(to):
            _remote(src[0], dst[0].at[me], sems[0].at[r], sems[1].at[r], peer).wait_send()

    return _Exchange([small], [jax.ShapeDtypeStruct((8,) + small.shape, small.dtype)], {},
                     [pltpu.SemaphoreType.DMA((7,))] * 2, start, finish)


def _small_adamw(gathered, small, me, w_vec, m_vec, v_vec):
    n_par = w_vec.shape[1]

    def body(me_ref, a_ref, s_ref, w_ref, m_ref, v_ref, loss_ref, g_ref, d_ref, nm_ref, nv_ref):
        mine = s_ref[...]
        tot = jnp.where(me_ref[0] == 0, mine, a_ref[0])
        for d in range(1, 8):
            tot = tot + jnp.where(me_ref[0] == d, mine, a_ref[d])
        tot = jnp.sum(tot, axis=0, keepdims=True)
        sq = jnp.sum(tot[:, n_par:], axis=1, keepdims=True)
        loss_ref[...] = jnp.broadcast_to(sq * (0.5 / D_MODEL), loss_ref.shape)
        g = tot[:, :n_par]
        g_ref[...] = g
        d_ref[...], nm_ref[...], nv_ref[...] = _adam_math(w_ref[...], g, m_ref[...], v_ref[...])

    vm = pl.BlockSpec(memory_space=pltpu.VMEM)
    vec = jax.ShapeDtypeStruct((1, n_par), F32)
    return pl.pallas_call(
        body, name="small_adamw",
        grid_spec=pltpu.PrefetchScalarGridSpec(num_scalar_prefetch=1, grid=(), in_specs=[vm] * 5, out_specs=[vm] * 5),
        out_shape=[jax.ShapeDtypeStruct((1, 128), F32), vec, vec, vec, vec],
    )(me, gathered, small, w_vec, m_vec, v_vec)


def _adamw(w, g, m, v, name):
    r, c = w.shape
    tr = min(r, 256)

    def body(w_ref, g_ref, m_ref, v_ref, d_ref, nm_ref, nv_ref):
        d_ref[...], nm_ref[...], nv_ref[...] = _adam_math(w_ref[...], g_ref[...], m_ref[...], v_ref[...])

    spec = pl.BlockSpec((tr, c), lambda i: (i, 0))
    shape = jax.ShapeDtypeStruct((r, c), F32)
    return _pallas(
        body, name=name, grid=(r // tr,),
        in_specs=[spec] * 4, out_specs=[spec] * 3, out_shape=[shape] * 3,
        compiler_params=_params(("parallel",), 48),
    )(w, g, m, v)


def kernel(x, w_in, w_pool, pool_scale, w_out, ln_gain, ln_bias, loss_target, m_w_in, m_w_pool, m_pool_scale, m_w_out, m_ln_gain, m_ln_bias, v_w_in, v_w_pool, v_pool_scale, v_w_out, v_ln_gain, v_ln_bias):
    xi, yi, ci = lax.axis_index("x"), lax.axis_index("y"), lax.axis_index("c")
    chip = (2 * xi + yi).astype(jnp.int32).reshape(1)
    core = ci.astype(jnp.int32).reshape(1)
    n_groups = len(POOL_WINDOWS)
    shard_c = w_pool.shape[2]

    w_in_b = _cast_bf16(w_in[0], chip, "cast_w_in", 256)
    w_out_b = _cast_bf16(w_out[0], chip, "cast_w_out", 256)
    w_pool_b = _cast_bf16(w_pool[0].reshape(n_groups * shard_c, POOL_GROUP_DIM), chip, "cast_w_pool", 256)
    w_in_g = _run_exchange(_allgather_weights([w_in_b]), "allgather_w_in")[0]

    chip_core = jnp.concatenate([chip, core])
    onward = (2 * (xi ^ ci) + (yi ^ (1 - ci))).astype(jnp.int32).reshape(1)
    g_x, full_in, full_out, full_pool, small = _step(
        x[0], loss_target[0], w_in_g, [w_out_b, w_pool_b], pool_scale, ln_gain, ln_bias, (core, chip_core, onward))
    half_c = shard_c // 2
    grad_w_in = full_in.reshape(D_MODEL, SHARD_IN)
    grad_w_out = full_out.reshape(D_MODEL // N_SHARDS, D_MODEL)
    grad_w_pool = (full_pool.reshape(2, n_groups, half_c, POOL_GROUP_DIM).transpose(1, 0, 2, 3)
                   .reshape(n_groups * shard_c, POOL_GROUP_DIM))

    d_in, nm_in, nv_in = _adamw(w_in[0], grad_w_in, m_w_in[0], v_w_in[0], "adamw_w_in")
    d_out, nm_out, nv_out = _adamw(w_out[0], grad_w_out, m_w_out[0], v_w_out[0], "adamw_w_out")
    flat = lambda t: t[0].reshape(n_groups * shard_c, POOL_GROUP_DIM)
    d_pool, nm_pool, nv_pool = _adamw(flat(w_pool), grad_w_pool, flat(m_w_pool), flat(v_w_pool), "adamw_w_pool")

    cat = lambda a, b, c: jnp.concatenate([a, b, c], axis=1)
    me = (4 * xi + 2 * yi + ci).astype(jnp.int32).reshape(1)
    loss_v, g_vec, d_vec, nm_vec, nv_vec = _small_adamw(
        small[1], small[0], me, cat(pool_scale, ln_gain, ln_bias), cat(m_pool_scale, m_ln_gain, m_ln_bias),
        cat(v_pool_scale, v_ln_gain, v_ln_bias))

    def split(vec):
        return vec[:, :D_POOL], vec[:, D_POOL:D_POOL + D_MODEL], vec[:, D_POOL + D_MODEL:]

    g_scale, g_gain, g_bias = split(g_vec)
    d_scale, d_gain, d_bias = split(d_vec)
    nm_scale, nm_gain, nm_bias = split(nm_vec)
    nv_scale, nv_gain, nv_bias = split(nv_vec)
    pool_shape = w_pool.shape
    return (loss_v[0, 0], g_x[None],
            grad_w_in[None], grad_w_pool.reshape(pool_shape), g_scale, grad_w_out[None], g_gain, g_bias,
            d_in[None], d_pool.reshape(pool_shape), d_scale, d_out[None], d_gain, d_bias,
            nm_in[None], nm_pool.reshape(pool_shape), nm_scale, nm_out[None], nm_gain, nm_bias,
            nv_in[None], nv_pool.reshape(pool_shape), nv_scale, nv_out[None], nv_gain, nv_bias)
```

```python
import functools

import jax
import jax.numpy as jnp
from jax import lax
from jax.experimental import pallas as pl
from jax.experimental.pallas import tpu as pltpu

F32 = jnp.float32
BF16 = jnp.bfloat16
MESH = pl.DeviceIdType.MESH
ANY = pl.BlockSpec(memory_space=pl.ANY)

D_MODEL = 2048
D_ATTN = 1024
D_POOL = 1024
HEAD_DIM = 128
N_HEADS = 8
ROPE_DIM = 32
ROPE_THETA = 500000.0
DILATIONS = (1, 4, 16)
KEY_BLOCK = 128
CHUNK = 2 * KEY_BLOCK
STAT_LANES = 128
POOL_WINDOWS = (2, 4, 8, 16)
POOL_GROUP_DIM = 256
POOL_HALO = 16
D_QKV = 3 * D_ATTN
D_UG = D_POOL + D_MODEL
D_IN = D_QKV + D_UG
N_SHARDS = 4
SHARD_IN = D_IN // N_SHARDS
LN_EPS = 1e-5
DEEPNORM_ALPHA = 2.0 ** 0.25
ADAM_LR = 0.001
ADAM_B1 = 0.9
ADAM_B2 = 0.999
ADAM_EPS = 1e-08
ADAM_WD = 0.01
ADAM_STEP = 10
NEG = -1e30
MIB = 1024 * 1024


def _params(sem, vmem_mib):
    return pltpu.CompilerParams(dimension_semantics=sem, vmem_limit_bytes=vmem_mib * MIB)


def _pallas(body, **kwargs):
    pin = lambda s: pltpu.HBM(s.shape, s.dtype) if len(s.shape) >= 2 else s
    out_shape = kwargs.pop("out_shape")
    out_shape = [pin(s) for s in out_shape] if isinstance(out_shape, (list, tuple)) else pin(out_shape)
    call = pl.pallas_call(body, out_shape=out_shape, **kwargs)

    def run(*operands):
        return call(*[pltpu.with_memory_space_constraint(o, pltpu.HBM) if o.ndim >= 2 else o for o in operands])

    return run


class _Exchange:
    def __init__(self, operands, out_shape, aliases, sems, start, finish):
        self.operands, self.out_shape, self.aliases, self.sems = list(operands), list(out_shape), dict(aliases), list(sems)
        self.start, self.finish = start, finish


def _run_exchange(comm, name):
    n_in, n_out = len(comm.operands), len(comm.out_shape)

    def body(*refs):
        ins, outs, sems = refs[:n_in], refs[n_in:n_in + n_out], refs[n_in + n_out:]
        comm.start(ins, outs, sems)
        comm.finish(ins, outs, sems)

    return _pallas(
        body, name=name, in_specs=[ANY] * n_in, out_specs=[ANY] * n_out, out_shape=comm.out_shape,
        input_output_aliases=comm.aliases, scratch_shapes=comm.sems,
    )(*comm.operands)


def _call(body, *, name, grid, in_specs, out_specs, out_shape, scratch_shapes, semantics, vmem_mib, args,
          aliases=None, comm=None, prefetch=()):
    aliases = dict(aliases or {})
    n_pre, n_in, n_out, n_scr = len(prefetch), len(in_specs), len(out_specs), len(scratch_shapes)
    c_in, c_out = (len(comm.operands), len(comm.out_shape)) if comm else (0, 0)
    c_shapes, c_sems, c_operands = (comm.out_shape, comm.sems, comm.operands) if comm else ([], [], [])

    def hosted(*refs):
        pre, refs = refs[:n_pre], refs[n_pre:]
        a = n_in
        b = a + c_in
        c = b + n_out
        d = c + c_out
        e = d + n_scr
        if comm is None:
            body(*pre, *refs)
            return
        ids = [pl.program_id(k) for k in range(len(grid))]
        first = functools.reduce(jnp.logical_and, [i == 0 for i in ids])
        last = functools.reduce(jnp.logical_and, [i == g - 1 for i, g in zip(ids, grid)])

        @pl.when(first)
        def _():
            comm.start(refs[a:b], refs[c:d], refs[e:])

        body(*pre, *refs[:a], *refs[b:c], *refs[d:e])

        @pl.when(last)
        def _():
            comm.finish(refs[a:b], refs[c:d], refs[e:])

    if comm:
        semantics = ("arbitrary",) * len(grid)
        for i, o in comm.aliases.items():
            aliases[n_pre + n_in + i] = n_out + o
    outs = _pallas(
        hosted, name=name,
        grid_spec=pltpu.PrefetchScalarGridSpec(
            num_scalar_prefetch=n_pre, grid=grid, in_specs=list(in_specs) + [ANY] * c_in,
            out_specs=list(out_specs) + [ANY] * c_out, scratch_shapes=list(scratch_shapes) + c_sems),
        out_shape=list(out_shape) + c_shapes, input_output_aliases=aliases,
        compiler_params=_params(semantics, vmem_mib),
    )(*prefetch, *args, *c_operands)
    return list(outs[:n_out]), list(outs[n_out:])


def _dot_nn(a, b):
    return jnp.dot(a, b, preferred_element_type=F32)


def _dot_nt(a, b):
    return lax.dot_general(a, b, (((1,), (1,)), ((), ())), preferred_element_type=F32)


def _dot_tn(a, b):
    return lax.dot_general(a, b, (((0,), (0,)), ((), ())), preferred_element_type=F32)


def _fold_rows(a):
    r, c = a.shape
    return jnp.sum(a.reshape(r // 8, 8, c), axis=0)


def _cast_bf16(a, chip, name, rows):
    r, c = a.shape

    def body(chip_ref, a_ref, o_ref):
        o_ref[...] = a_ref[...].astype(BF16)

    return _pallas(
        body, name=name,
        grid_spec=pltpu.PrefetchScalarGridSpec(
            num_scalar_prefetch=1, grid=(r // rows,),
            in_specs=[pl.BlockSpec((rows, c), lambda i, chip_ref: (i, 0))],
            out_specs=pl.BlockSpec((None, rows, c), lambda i, chip_ref: (chip_ref[0], i, 0))),
        out_shape=jax.ShapeDtypeStruct((N_SHARDS, r, c), BF16),
        compiler_params=_params(("parallel",), 32),
    )(chip, a)


def _mesh_place():
    x, y, c = lax.axis_index("x"), lax.axis_index("y"), lax.axis_index("c")
    return x, y, c, [(1 - x, y), (x, 1 - y), (1 - x, 1 - y)]


def _remote(src, dst, send_sem, recv_sem, to):
    return pltpu.make_async_remote_copy(src_ref=src, dst_ref=dst, send_sem=send_sem, recv_sem=recv_sem,
                                        device_id=to, device_id_type=MESH)


def _allgather_weights(bufs, phase="all", chunks=1):
    n = len(bufs)
    items = [(a, q) for q in range(chunks) for a in range(n)]

    def rows(item, core):
        a, q = item
        size = bufs[a].shape[1] // 2 // chunks
        return pl.ds(core * chunks * size + q * size, size)

    def sem(sems, which, item, k):
        a, q = item
        return sems[which].at[a * chunks + q, k]

    DIAGONAL = 2

    def to_neighbours(dst, sems, item):
        x, y, c, chips = _mesh_place()
        own = dst[item[0]].at[2 * x + y, rows(item, c)]
        return [_remote(own, own, sem(sems, 0, item, k), sem(sems, 1, item, k), (cx, cy, c))
                for k, (cx, cy) in enumerate(chips[:DIAGONAL])]

    def relayed(dst, sems, item):
        x, y, c, _ = _mesh_place()
        piece = dst[item[0]].at[2 * (x ^ (1 - c)) + (y ^ c), rows(item, c)]
        return _remote(piece, piece, sem(sems, 0, item, DIAGONAL), sem(sems, 1, item, DIAGONAL), (x ^ c, y ^ (1 - c), c))

    def start(ins, dst, sems):
        for item in items:
            for cp in ([relayed(dst, sems, item)] if phase == "diagonal" else to_neighbours(dst, sems, item)):
                cp.start()

    def finish(ins, dst, sems):
        x, y, c, chips = _mesh_place()
        sibling = (x, y, 1 - c)
        passed_on = []

        def landed_then_pass_on(item, k):
            cx, cy = chips[k]
            landed = dst[item[0]].at[2 * cx + cy, rows(item, c)]
            _remote(landed, landed, sem(sems, 0, item, k), sem(sems, 1, item, k), (cx, cy, c)).wait_recv()
            cp = _remote(landed, landed, sem(sems, 2, item, k), sem(sems, 3, item, k), sibling)
            cp.start()
            passed_on.append(cp)

        sent = []
        for item in items:
            if phase != "diagonal":
                for k in range(DIAGONAL):
                    landed_then_pass_on(item, k)
                sent += to_neighbours(dst, sems, item)
            if phase == "all":
                relayed(dst, sems, item).start()
        for item in items:
            if phase != "neighbours":
                landed_then_pass_on(item, DIAGONAL)
                sent.append(relayed(dst, sems, item))
        for k in {"all": (0, 1, 2), "neighbours": (0, 1), "diagonal": (2,)}[phase]:
            cx, cy = chips[k]
            for item in items:
                passed = dst[item[0]].at[2 * cx + cy, rows(item, 1 - c)]
                _remote(passed, passed, sem(sems, 2, item, k), sem(sems, 3, item, k), sibling).wait_recv()
        for cp in sent + passed_on:
            cp.wait_send()

    return _Exchange(bufs, [jax.ShapeDtypeStruct(b.shape, b.dtype) for b in bufs], {a: a for a in range(n)},
                     [pltpu.SemaphoreType.DMA((n * chunks, 3))] * 4, start, finish)


def _rope_tables(seq):
    half = ROPE_DIM // 2
    inv_freq = ROPE_THETA ** (-(2.0 * jnp.arange(half, dtype=F32)) / ROPE_DIM)
    ang = jnp.arange(seq, dtype=jnp.int32).astype(F32)[:, None] * inv_freq[None, :]
    cos, sin = jnp.cos(ang), jnp.sin(ang)
    pad = jnp.zeros((seq, HEAD_DIM - ROPE_DIM), F32)
    zeros = jnp.zeros((seq, half), F32)
    c_tab = jnp.concatenate([cos, cos, pad + 1.0], axis=1)
    up_tab = jnp.concatenate([-sin, zeros, pad], axis=1)
    down_tab = jnp.concatenate([zeros, sin, pad], axis=1)
    return c_tab, up_tab, down_tab


def _rotate_heads(t, c_tab, up_tab, down_tab):
    outs = []
    for h in range(t.shape[1] // HEAD_DIM):
        th = t[:, h * HEAD_DIM:(h + 1) * HEAD_DIM]
        up = pltpu.roll(th, HEAD_DIM - ROPE_DIM // 2, axis=1)
        down = pltpu.roll(th, ROPE_DIM // 2, axis=1)
        outs.append(th * c_tab + up * up_tab + down * down_tab)
    return outs[0] if len(outs) == 1 else jnp.concatenate(outs, axis=1)


def _to_pattern(slabs_ref, dst_ref, dil, dtype):
    n_slabs, rows, _ = slabs_ref.shape
    for s in range(n_slabs):
        for r in range(dil):
            dst_ref[r, :, s * 128:(s + 1) * 128] = slabs_ref[s, pl.ds(r, rows // dil, dil), :].astype(dtype)


def _from_pattern(src_ref, slabs_ref, dil):
    n_slabs, rows, _ = slabs_ref.shape
    for s in range(n_slabs):
        for r in range(dil):
            slabs_ref[s, pl.ds(r, rows // dil, dil), :] = src_ref[r, :, s * 128:(s + 1) * 128].astype(F32)


def _store_slabs(slabs_ref, value):
    for s in range(slabs_ref.shape[0]):
        slabs_ref[s] = value[:, s * 128:(s + 1) * 128]


def _in_proj_qkv(x, w_in_g, tabs, comm=None):
    seq = x.shape[0]
    tm, tn = 512, SHARD_IN
    heads = tn // HEAD_DIM
    k_heads_in_second = 2 * D_ATTN // HEAD_DIM - heads
    d4, d16 = DILATIONS[1], DILATIONS[2]

    def body(x_ref, w_ref, c_ref, up_ref, down_ref, o1_ref, o4_ref, o16_ref, res_ref):
        shard = pl.program_id(0)
        xb = x_ref[...].astype(BF16)
        group = 4 * HEAD_DIM
        accs = [_dot_nn(xb, w_ref[:, g * group:(g + 1) * group]) for g in range(tn // group)]

        plain = shard == 1
        c_plain = jnp.where(plain, 1.0, c_ref[...])
        up_plain = jnp.where(plain, 0.0, up_ref[...])
        down_plain = jnp.where(plain, 0.0, down_ref[...])
        for h in range(heads):
            lanes = (h * HEAD_DIM) % group
            th = accs[h * HEAD_DIM // group][:, lanes:lanes + HEAD_DIM]
            if h < k_heads_in_second:
                th = _rotate_heads(th, c_ref[...], up_ref[...], down_ref[...])
            else:
                th = _rotate_heads(th, c_plain, up_plain, down_plain)
            res_ref[h] = th
            o1_ref[:, h * HEAD_DIM:(h + 1) * HEAD_DIM] = th.astype(BF16)
        _to_pattern(res_ref, o4_ref, d4, BF16)
        _to_pattern(res_ref, o16_ref, d16, BF16)

    tab_spec = pl.BlockSpec((tm, HEAD_DIM), lambda s, i: (i, 0))
    (o1, o4, o16), exchanged = _call(
        body, name="in_proj_qkv", grid=(D_QKV // tn, seq // tm),
        in_specs=[pl.BlockSpec((tm, D_MODEL), lambda s, i: (i, 0)),
                  pl.BlockSpec((None, D_MODEL, tn), lambda s, i: (s, 0, 0)),
                  tab_spec, tab_spec, tab_spec],
        out_specs=[pl.BlockSpec((tm, tn), lambda s, i: (i, s)),
                   pl.BlockSpec((d4, tm // d4, tn), lambda s, i: (0, i, s)),
                   pl.BlockSpec((d16, tm // d16, tn), lambda s, i: (0, i, s))],
        out_shape=[jax.ShapeDtypeStruct((seq, D_QKV), BF16),
                   jax.ShapeDtypeStruct((d4, seq // d4, D_QKV), BF16),
                   jax.ShapeDtypeStruct((d16, seq // d16, D_QKV), BF16)],
        scratch_shapes=[pltpu.VMEM((heads, tm, HEAD_DIM), F32)],
        semantics=("parallel", "parallel"), vmem_mib=52, args=(x, w_in_g, *tabs), comm=comm)
    return [o1[None], o4, o16], exchanged


def _in_proj_pool_gate(x, w_in_g, comm=None):
    seq = x.shape[0]
    tm, tn = 512, SHARD_IN
    first_shard = D_QKV // tn

    def body(x_ref, w_ref, o_ref):
        o_ref[...] = _dot_nn(x_ref[...].astype(BF16), w_ref[...])

    (hug,), exchanged = _call(
        body, name="in_proj_pool_gate", grid=(D_UG // tn, seq // tm),
        in_specs=[pl.BlockSpec((tm, D_MODEL), lambda s, i: (i, 0)),
                  pl.BlockSpec((None, D_MODEL, tn), lambda s, i: (s + first_shard, 0, 0))],
        out_specs=[pl.BlockSpec((tm, tn), lambda s, i: (i, s))],
        out_shape=[jax.ShapeDtypeStruct((seq, D_UG), F32)],
        scratch_shapes=[], semantics=("parallel", "parallel"), vmem_mib=48, args=(x, w_in_g), comm=comm)
    return hug, exchanged


def _band_masks():
    row = lax.broadcasted_iota(jnp.int32, (KEY_BLOCK, KEY_BLOCK), 0)
    col = lax.broadcasted_iota(jnp.int32, (KEY_BLOCK, KEY_BLOCK), 1)
    return col <= row, col >= row


def _attn_fwd(qkv, name):
    dil, n, _ = qkv.shape
    scale = HEAD_DIM ** -0.5
    lo, hi = slice(0, KEY_BLOCK), slice(KEY_BLOCK, CHUNK)

    def body(q_ref, k_ref, v_ref, kb_ref, vb_ref, o_ref, st_ref):
        i = pl.program_id(1)
        cur_mask, prev_mask = _band_masks()
        before_mask = jnp.logical_and(prev_mask, i > 0)
        lane = lax.broadcasted_iota(jnp.int32, (KEY_BLOCK, STAT_LANES), 1)
        tasks = [(rows, h) for rows in (lo, hi) for h in range(N_HEADS)]
        head = lambda h: slice(h * HEAD_DIM, (h + 1) * HEAD_DIM)

        def prev_of(rows, h):
            if rows is lo:
                return kb_ref[:, head(h)], vb_ref[:, head(h)], before_mask
            return k_ref[lo, head(h)], v_ref[lo, head(h)], prev_mask

        scores = []
        for rows, h in tasks:
            q = q_ref[rows, head(h)]
            scores.append((_dot_nt(q, prev_of(rows, h)[0]), _dot_nt(q, k_ref[rows, head(h)])))
        probs = []
        for (rows, h), (qk_prev, qk_cur) in zip(tasks, scores):
            s_prev = jnp.where(prev_of(rows, h)[2], qk_prev * scale, NEG)
            s_cur = jnp.where(cur_mask, qk_cur * scale, NEG)
            m = jnp.max(jnp.maximum(s_prev, s_cur), axis=-1, keepdims=True)
            p_prev = jnp.exp(s_prev - m)
            p_cur = jnp.exp(s_cur - m)
            den = jnp.sum(p_prev + p_cur, axis=-1, keepdims=True)
            probs.append((p_prev.astype(BF16), p_cur.astype(BF16), den, m + jnp.log(den)))
        stats = [jnp.zeros((KEY_BLOCK, STAT_LANES), F32), jnp.zeros((KEY_BLOCK, STAT_LANES), F32)]
        for (rows, h), (p_prev, p_cur, den, lse) in zip(tasks, probs):
            o = _dot_nn(p_cur, v_ref[rows, head(h)]) + _dot_nn(p_prev, prev_of(rows, h)[1])
            o_ref[rows, head(h)] = (o / den).astype(BF16)
            b = 0 if rows is lo else 1
            stats[b] = jnp.where(lane == h, lse, stats[b])
        st_ref[lo, :] = stats[0]
        st_ref[hi, :] = stats[1]

    main = lambda cb: pl.BlockSpec((None, CHUNK, D_ATTN), lambda r, i: (r, i, cb))
    before = lambda cb: pl.BlockSpec((None, KEY_BLOCK, D_ATTN), lambda r, i: (r, jnp.maximum(2 * i - 1, 0), cb))
    return _pallas(
        body, name=name, grid=(dil, n // CHUNK),
        in_specs=[main(0), main(1), main(2), before(1), before(2)],
        out_specs=[main(0), pl.BlockSpec((None, CHUNK, STAT_LANES), lambda r, i: (r, i, 0))],
        out_shape=[jax.ShapeDtypeStruct((dil, n, D_ATTN), BF16), jax.ShapeDtypeStruct((dil, n, STAT_LANES), F32)],
        compiler_params=_params(("parallel", "parallel"), 40),
    )(qkv, qkv, qkv, qkv, qkv)


def _attn_bwd(qkv, do, stats, name, comm=None):
    dil, n, _ = qkv.shape
    n_blocks = n // KEY_BLOCK
    last = n // CHUNK - 1
    scale = HEAD_DIM ** -0.5
    lo, hi = slice(0, KEY_BLOCK), slice(KEY_BLOCK, CHUNK)

    def body(q_ref, k_ref, v_ref, kb_ref, vb_ref, qa_ref, do_ref, doa_ref, st_ref, sta_ref, dq_ref, dk_ref, dv_ref):
        i = pl.program_id(1)
        cur_mask, prev_mask = _band_masks()
        before_mask = jnp.logical_and(prev_mask, i > 0)
        after_mask = jnp.logical_and(prev_mask, i < last)

        rows_cat = lambda a, b: jnp.concatenate([a, b], axis=0)
        masks = (jnp.concatenate([before_mask, cur_mask], axis=1), jnp.concatenate([prev_mask, cur_mask], axis=1),
                 after_mask)

        def operands(h):
            cols = slice(h * HEAD_DIM, (h + 1) * HEAD_DIM)
            lse_c, del_c = slice(h, h + 1), slice(N_HEADS + h, N_HEADS + h + 1)
            q = (q_ref[lo, cols], q_ref[hi, cols], qa_ref[:, cols])
            do = (do_ref[lo, cols], do_ref[hi, cols], doa_ref[:, cols])
            keys = (rows_cat(kb_ref[:, cols], k_ref[lo, cols]), k_ref[:, cols], k_ref[hi, cols])
            vals = (rows_cat(vb_ref[:, cols], v_ref[lo, cols]), v_ref[:, cols], v_ref[hi, cols])
            st = ((st_ref[lo, lse_c], st_ref[lo, del_c]), (st_ref[hi, lse_c], st_ref[hi, del_c]),
                  (sta_ref[:, lse_c], sta_ref[:, del_c]))
            return cols, q, do, keys, vals, st

        group = N_HEADS // 2
        for first_head in range(0, N_HEADS, group):
            heads = range(first_head, first_head + group)
            raw = {}
            for h in heads:
                _, q, do, keys, vals, _ = operands(h)
                raw[h] = [(_dot_nt(q[j], keys[j]), _dot_nt(do[j], vals[j])) for j in range(3)]
            grads = {}
            for h in heads:
                st = operands(h)[5]
                grads[h] = []
                for j in range(3):
                    qk, dp = raw[h][j]
                    lse, delta = st[j]
                    p = jnp.exp(jnp.where(masks[j], qk * scale, NEG) - lse)
                    grads[h].append((p.astype(BF16), (p * (dp - delta) * scale).astype(BF16)))
            for h in heads:
                cols, q, do, keys, _, _ = operands(h)
                (p0, ds0), (p1, ds1), (pa, dsa) = grads[h]
                own, nxt = slice(KEY_BLOCK, CHUNK), slice(0, KEY_BLOCK)

                def put(ref, rows, val, cols=cols):
                    ref[rows, cols] = val.astype(ref.dtype)

                put(dq_ref, lo, _dot_nn(ds0, keys[0]))
                put(dq_ref, hi, _dot_nn(ds1, keys[1]))
                put(dk_ref, lo, _dot_tn(rows_cat(ds0[:, own], ds1[:, nxt]), q_ref[:, cols]))
                put(dk_ref, hi, _dot_tn(rows_cat(ds1[:, own], dsa), rows_cat(q[1], q[2])))
                put(dv_ref, lo, _dot_tn(rows_cat(p0[:, own], p1[:, nxt]), do_ref[:, cols]))
                put(dv_ref, hi, _dot_tn(rows_cat(p1[:, own], pa), rows_cat(do[1], do[2])))

    def spec(rows, width, row_of, cb):
        return pl.BlockSpec((None, rows, width), lambda r, i: (r, row_of(i), cb))

    same = lambda i: i
    before = lambda i: jnp.maximum(2 * i - 1, 0)
    after = lambda i: jnp.minimum(2 * i + 2, n_blocks - 1)
    out = spec(CHUNK, D_ATTN, same, 0)
    return _call(
        body, name=name, grid=(dil, n // CHUNK),
        in_specs=[spec(CHUNK, D_ATTN, same, 0), spec(CHUNK, D_ATTN, same, 1), spec(CHUNK, D_ATTN, same, 2),
                  spec(KEY_BLOCK, D_ATTN, before, 1), spec(KEY_BLOCK, D_ATTN, before, 2),
                  spec(KEY_BLOCK, D_ATTN, after, 0),
                  spec(CHUNK, D_ATTN, same, 0), spec(KEY_BLOCK, D_ATTN, after, 0),
                  spec(CHUNK, STAT_LANES, same, 0), spec(KEY_BLOCK, STAT_LANES, after, 0)],
        out_specs=[out, out, out],
        out_shape=[jax.ShapeDtypeStruct((dil, n, D_ATTN), BF16)] * 3,
        scratch_shapes=[], semantics=("parallel", "parallel"), vmem_mib=40,
        args=(qkv, qkv, qkv, qkv, qkv, qkv, do, do, stats, stats), comm=comm)


def _window_sums(ext, window, backward):
    rows = ext.shape[0]
    acc, span = ext, 1
    while span < window:
        acc = acc + pltpu.roll(acc, (rows - span) if backward else span, axis=0)
        span *= 2
    return acc


def _mix_gate(o_list, st_list, hug, w_pool_g, pool_scale):
    seq = hug.shape[0]
    tm = 256
    halo_blocks = tm // POOL_HALO
    d4, d16 = DILATIONS[1], DILATIONS[2]

    def body(o1_ref, o4_ref, o16_ref, l1_ref, l4_ref, l16_ref, u_ref, halo_ref, ga_ref, gp_ref, wp_ref, sc_ref,
             y_ref, mix_ref, lse_ref, pooled_ref, n4_ref, n16_ref, nl4_ref, nl16_ref):
        i = pl.program_id(0)
        _from_pattern(o4_ref, n4_ref, d4)
        _from_pattern(o16_ref, n16_ref, d16)
        _from_pattern(l4_ref, nl4_ref, d4)
        _from_pattern(l16_ref, nl16_ref, d16)
        la, lb, lc = l1_ref[...], nl4_ref[0], nl16_ref[0]
        mx = jnp.maximum(jnp.maximum(la, lb), lc)
        ea, eb, ec = jnp.exp(la - mx), jnp.exp(lb - mx), jnp.exp(lc - mx)
        tot = ea + eb + ec
        lse_ref[...] = mx + jnp.log(tot)
        wa, wb, wc = ea / tot, eb / tot, ec / tot
        ga = ga_ref[...]
        silu_a = ga * jax.nn.sigmoid(ga)
        for h in range(N_HEADS):
            cols = slice(h * HEAD_DIM, (h + 1) * HEAD_DIM)
            hc = slice(h, h + 1)
            attn = wa[:, hc] * o1_ref[:, cols].astype(F32) + wb[:, hc] * n4_ref[h] + wc[:, hc] * n16_ref[h]
            mix_ref[:, cols] = attn
            y_ref[:, cols] = (attn * silu_a[:, cols]).astype(BF16)

        u = u_ref[...]
        halo = jnp.where(i > 0, halo_ref[...], 0.0)
        ext = jnp.concatenate([halo, u], axis=0)
        pos = i * tm + lax.broadcasted_iota(jnp.int32, (tm, 1), 0)
        gp = gp_ref[...]
        gated_scale = sc_ref[...] * (gp * jax.nn.sigmoid(gp))
        for g, window in enumerate(POOL_WINDOWS):
            cols = slice(g * POOL_GROUP_DIM, (g + 1) * POOL_GROUP_DIM)
            sums = _window_sums(ext[:, cols], window, backward=False)[POOL_HALO:, :]
            count = jnp.minimum(pos + 1, window).astype(F32)
            pooled = (sums / count - u[:, cols]).astype(BF16)
            pooled_ref[:, cols] = pooled
            pre = _dot_nn(pooled, wp_ref[g])
            out_cols = slice(D_ATTN + g * POOL_GROUP_DIM, D_ATTN + (g + 1) * POOL_GROUP_DIM)
            mix_ref[:, out_cols] = pre
            y_ref[:, out_cols] = (pre * gated_scale[:, cols]).astype(BF16)

    row = lambda width, cb=0: pl.BlockSpec((tm, width), lambda i: (i, cb))
    pat = lambda d, width: pl.BlockSpec((d, tm // d, width), lambda i: (0, i, 0))
    return _pallas(
        body, name="mix_gate", grid=(seq // tm,),
        in_specs=[row(D_ATTN), pat(d4, D_ATTN), pat(d16, D_ATTN),
                  row(STAT_LANES), pat(d4, STAT_LANES), pat(d16, STAT_LANES),
                  row(D_POOL),
                  pl.BlockSpec((POOL_HALO, D_POOL), lambda i: (jnp.maximum(i * halo_blocks - 1, 0), 0)),
                  row(D_ATTN, 1), row(D_POOL, 2),
                  pl.BlockSpec((len(POOL_WINDOWS), POOL_GROUP_DIM, POOL_GROUP_DIM), lambda i: (0, 0, 0)),
                  pl.BlockSpec((1, D_POOL), lambda i: (0, 0))],
        out_specs=[row(D_MODEL), row(D_MODEL), row(STAT_LANES), row(D_POOL)],
        out_shape=[jax.ShapeDtypeStruct((seq, D_MODEL), BF16), jax.ShapeDtypeStruct((seq, D_MODEL), F32),
                   jax.ShapeDtypeStruct((seq, STAT_LANES), F32), jax.ShapeDtypeStruct((seq, D_POOL), BF16)],
        scratch_shapes=[pltpu.VMEM((N_HEADS, tm, HEAD_DIM), F32), pltpu.VMEM((N_HEADS, tm, HEAD_DIM), F32),
                        pltpu.VMEM((1, tm, STAT_LANES), F32), pltpu.VMEM((1, tm, STAT_LANES), F32)],
        compiler_params=_params(("parallel",), 48),
    )(o_list[0][0], o_list[1], o_list[2], st_list[0][0], st_list[1], st_list[2],
      hug, hug, hug, hug, w_pool_g, pool_scale)


def _out_proj_loss(y, w_out_g, x, target, gain, bias):
    seq = x.shape[0]
    tm = 512

    def body(y_ref, w_ref, x_ref, t_ref, g_ref, b_ref, dz_ref, dzb_ref, gg_ref, gb_ref, loss_ref):
        @pl.when(pl.program_id(0) == 0)
        def _():
            gg_ref[...] = jnp.zeros_like(gg_ref)
            gb_ref[...] = jnp.zeros_like(gb_ref)
            loss_ref[...] = jnp.zeros_like(loss_ref)

        halves = [slice(0, tm // 2), slice(tm // 2, tm)]
        projected = [_dot_nn(y_ref[rows, :], w_ref[...]) for rows in halves]
        for rows, out in zip(halves, projected):
            z = DEEPNORM_ALPHA * x_ref[rows, :] + out
            mu = jnp.mean(z, axis=-1, keepdims=True)
            zc = z - mu
            rstd = lax.rsqrt(jnp.mean(zc * zc, axis=-1, keepdims=True) + LN_EPS)
            xhat = zc * rstd
            gain_v = g_ref[...]
            diff = xhat * gain_v + b_ref[...] - t_ref[rows, :]
            sq = _fold_rows(diff * diff)
            part = sq[:, :128]
            for k in range(1, D_MODEL // 128):
                part = part + sq[:, k * 128:(k + 1) * 128]
            loss_ref[...] += part
            dln = diff * (1.0 / D_MODEL)
            gg_ref[...] += _fold_rows(dln * xhat)
            gb_ref[...] += _fold_rows(dln)
            dxhat = dln * gain_v
            dz = rstd * (dxhat - jnp.mean(dxhat, axis=-1, keepdims=True)
                         - xhat * jnp.mean(dxhat * xhat, axis=-1, keepdims=True))
            dz_ref[rows, :] = dz
            dzb_ref[rows, :] = dz.astype(BF16)

    row = lambda: pl.BlockSpec((tm, D_MODEL), lambda i: (i, 0))
    vec = lambda: pl.BlockSpec((1, D_MODEL), lambda i: (0, 0))
    acc = lambda width: pl.BlockSpec((8, width), lambda i: (0, 0))
    return _pallas(
        body, name="out_proj_loss", grid=(seq // tm,),
        in_specs=[row(), pl.BlockSpec((D_MODEL, D_MODEL), lambda i: (0, 0), pipeline_mode=pl.Buffered(1)),
                  row(), row(), vec(), vec()],
        out_specs=[row(), row(), acc(D_MODEL), acc(D_MODEL), acc(128)],
        out_shape=[jax.ShapeDtypeStruct((seq, D_MODEL), F32), jax.ShapeDtypeStruct((seq, D_MODEL), BF16),
                   jax.ShapeDtypeStruct((8, D_MODEL), F32), jax.ShapeDtypeStruct((8, D_MODEL), F32),
                   jax.ShapeDtypeStruct((8, 128), F32)],
        compiler_params=_params(("arbitrary",), 56),
    )(y, w_out_g.reshape(D_MODEL, D_MODEL), x, target, gain, bias)


def _dy_gate_bwd(dzb, w_out_g, hug, mixpre, pool_scale, lse_all):
    seq = dzb.shape[0]
    tm = 256
    d4, d16 = DILATIONS[1], DILATIONS[2]

    def body(dz_ref, w_ref, ga_ref, gp_ref, mix_ref, sc_ref, lse_ref,
             dh_ref, dpo_ref, do1_ref, do4_ref, do16_ref, st1_ref, st4_ref, st16_ref, da_ref, st_ref):
        dy = _dot_nt(dz_ref[...], w_ref[...])
        ga = ga_ref[...]
        sig = jax.nn.sigmoid(ga)
        attn = mix_ref[:, :D_ATTN]
        dya = dy[:, :D_ATTN]
        dattn = dya * (ga * sig)
        dh_ref[:, :D_ATTN] = (dya * attn * (sig * (1.0 + ga * (1.0 - sig)))).astype(BF16)
        _store_slabs(da_ref, dattn)
        lane = lax.broadcasted_iota(jnp.int32, (tm, STAT_LANES), 1)
        stats = lse_ref[...]
        prod = dattn * attn
        for h in range(N_HEADS):
            delta = jnp.sum(prod[:, h * HEAD_DIM:(h + 1) * HEAD_DIM], axis=-1, keepdims=True)
            stats = jnp.where(lane == N_HEADS + h, delta, stats)
        st_ref[0] = stats
        do1_ref[...] = dattn.astype(BF16)
        st1_ref[...] = stats
        _to_pattern(da_ref, do4_ref, d4, BF16)
        _to_pattern(da_ref, do16_ref, d16, BF16)
        _to_pattern(st_ref, st4_ref, d4, F32)
        _to_pattern(st_ref, st16_ref, d16, F32)

        gp = gp_ref[...]
        sig = jax.nn.sigmoid(gp)
        dyp = dy[:, D_ATTN:]
        dpo_ref[...] = dyp * (gp * sig)
        dh_ref[:, D_ATTN:] = (dyp * (mix_ref[:, D_ATTN:] * sc_ref[...])
                              * (sig * (1.0 + gp * (1.0 - sig)))).astype(BF16)

    row = lambda width, cb=0: pl.BlockSpec((tm, width), lambda i: (i, cb))
    pat = lambda d, width: pl.BlockSpec((d, tm // d, width), lambda i: (0, i, 0))
    pat_shape = lambda d, width, dtype: jax.ShapeDtypeStruct((d, seq // d, width), dtype)
    outs = _pallas(
        body, name="dy_gate_bwd", grid=(seq // tm,),
        in_specs=[row(D_MODEL), pl.BlockSpec((D_MODEL, D_MODEL), lambda i: (0, 0)),
                  row(D_ATTN, 1), row(D_POOL, 2), row(D_MODEL), pl.BlockSpec((1, D_POOL), lambda i: (0, 0)),
                  row(STAT_LANES)],
        out_specs=[row(D_MODEL, D_IN // D_MODEL - 1), row(D_POOL),
                   row(D_ATTN), pat(d4, D_ATTN), pat(d16, D_ATTN),
                   row(STAT_LANES), pat(d4, STAT_LANES), pat(d16, STAT_LANES)],
        out_shape=[jax.ShapeDtypeStruct((seq, D_IN), BF16), jax.ShapeDtypeStruct((seq, D_POOL), F32),
                   jax.ShapeDtypeStruct((seq, D_ATTN), BF16), pat_shape(d4, D_ATTN, BF16), pat_shape(d16, D_ATTN, BF16),
                   jax.ShapeDtypeStruct((seq, STAT_LANES), F32), pat_shape(d4, STAT_LANES, F32),
                   pat_shape(d16, STAT_LANES, F32)],
        scratch_shapes=[pltpu.VMEM((N_HEADS, tm, HEAD_DIM), F32), pltpu.VMEM((1, tm, STAT_LANES), F32)],
        compiler_params=_params(("parallel",), 48),
    )(dzb, w_out_g.reshape(D_MODEL, D_MODEL), hug, hug, mixpre, pool_scale, lse_all)
    dh, dpo, do1, do4, do16, st1, st4, st16 = outs
    return dh, dpo, [do1[None], do4, do16], [st1[None], st4, st16]


def _pool_bwd(dh, dpo, mixpre, pooled, w_pool_g, pool_scale):
    seq = dpo.shape[0]
    tm = 256
    halo_blocks = tm // POOL_HALO
    last = seq // tm - 1
    n_groups = len(POOL_WINDOWS)

    def body(dh_in_ref, dpo_ref, halo_ref, pre_ref, pooled_ref, wp_ref, sc_ref, du_ref, gw_ref, gs_ref):
        i = pl.program_id(0)

        @pl.when(i == 0)
        def _():
            gw_ref[...] = jnp.zeros_like(gw_ref)
            gs_ref[...] = jnp.zeros_like(gs_ref)

        dpo = dpo_ref[...]
        scale = sc_ref[...]
        gs_ref[...] += _fold_rows(dpo * pre_ref[...])
        halo = jnp.where(i < last, halo_ref[...], 0.0)
        dpw = (jnp.concatenate([dpo, halo], axis=0) * scale).astype(BF16)
        pos = i * tm + lax.broadcasted_iota(jnp.int32, (tm + POOL_HALO, 1), 0)
        for g, window in enumerate(POOL_WINDOWS):
            cols = slice(g * POOL_GROUP_DIM, (g + 1) * POOL_GROUP_DIM)
            dpw_g = dpw[:, cols]
            gw_ref[g] += _dot_tn(pooled_ref[:, cols], dpw_g[:tm, :])
            dpooled = _dot_nt(dpw_g, wp_ref[g])
            count = jnp.minimum(pos + 1, window).astype(F32)
            sums = _window_sums(dpooled / count, window, backward=True)
            du_ref[:, cols] = (sums[:tm, :] - dpooled[:tm, :]).astype(BF16)

    row = lambda width, cb=0: pl.BlockSpec((tm, width), lambda i: (i, cb))
    return _pallas(
        body, name="pool_bwd", grid=(seq // tm,),
        in_specs=[ANY, row(D_POOL),
                  pl.BlockSpec((POOL_HALO, D_POOL),
                               lambda i: (jnp.minimum((i + 1) * halo_blocks, seq // POOL_HALO - 1), 0)),
                  row(D_POOL, 1), row(D_POOL),
                  pl.BlockSpec((n_groups, POOL_GROUP_DIM, POOL_GROUP_DIM), lambda i: (0, 0, 0)),
                  pl.BlockSpec((1, D_POOL), lambda i: (0, 0))],
        out_specs=[row(D_POOL, D_QKV // D_POOL),
                   pl.BlockSpec((n_groups, POOL_GROUP_DIM, POOL_GROUP_DIM), lambda i: (0, 0, 0)),
                   pl.BlockSpec((8, D_POOL), lambda i: (0, 0))],
        out_shape=[jax.ShapeDtypeStruct(dh.shape, dh.dtype),
                   jax.ShapeDtypeStruct((n_groups, POOL_GROUP_DIM, POOL_GROUP_DIM), F32),
                   jax.ShapeDtypeStruct((8, D_POOL), F32)],
        input_output_aliases={0: 0},
        compiler_params=_params(("arbitrary",), 40),
    )(dh, dpo, dpo, mixpre, pooled, w_pool_g, pool_scale)


def _sum_patterns(dh, parts, tabs, unrotate, col_block, name, comm=None):
    seq = dh.shape[0]
    tm, tn = 256, D_ATTN
    per = D_ATTN // tn
    d4, d16 = DILATIONS[1], DILATIONS[2]

    def body(dh_in_ref, a1_ref, a4_ref, a16_ref, ct_ref, up_ref, down_ref, o_ref, n4_ref, n16_ref):
        _from_pattern(a4_ref, n4_ref, d4)
        _from_pattern(a16_ref, n16_ref, d16)
        for s in range(tn // HEAD_DIM):
            cols = slice(s * HEAD_DIM, (s + 1) * HEAD_DIM)
            tot = a1_ref[:, cols].astype(F32) + n4_ref[s] + n16_ref[s]
            if unrotate:
                tot = _rotate_heads(tot, ct_ref[...], -up_ref[...], -down_ref[...])
            o_ref[:, cols] = tot.astype(BF16)

    tab = pl.BlockSpec((tm, HEAD_DIM), lambda i, j: (i, 0))
    pat = lambda d: pl.BlockSpec((d, tm // d, tn), lambda i, j: (0, i, j))
    (dh,), exchanged = _call(
        body, name=name, grid=(seq // tm, per),
        in_specs=[ANY, pl.BlockSpec((tm, tn), lambda i, j: (i, j)), pat(d4), pat(d16), tab, tab, tab],
        out_specs=[pl.BlockSpec((tm, tn), lambda i, j: (i, col_block * per + j))],
        out_shape=[jax.ShapeDtypeStruct(dh.shape, dh.dtype)],
        scratch_shapes=[pltpu.VMEM((tn // HEAD_DIM, tm, HEAD_DIM), F32), pltpu.VMEM((tn // HEAD_DIM, tm, HEAD_DIM), F32)],
        semantics=("parallel", "parallel"), vmem_mib=32, args=(dh, parts[0][0], parts[1], parts[2], *tabs),
        aliases={0: 0}, comm=comm)
    return dh, exchanged


def _grad_w_in(x, dh, half, name, comm=None):
    seq = x.shape[0]
    ts, td, te = 2048, D_MODEL // 2, SHARD_IN

    def body(half_ref, x_ref, dh_ref, o_ref):
        k = pl.program_id(1)
        part = _dot_tn(x_ref[...].astype(BF16), dh_ref[...])

        @pl.when(k == 0)
        def _():
            o_ref[...] = part

        @pl.when(k > 0)
        def _():
            o_ref[...] += part

    (g,), exchanged = _call(
        body, name=name, grid=(N_SHARDS, seq // ts),
        in_specs=[pl.BlockSpec((ts, td), lambda e, k, half_ref: (k, half_ref[0])),
                  pl.BlockSpec((ts, te), lambda e, k, half_ref: (k, e))],
        out_specs=[pl.BlockSpec((None, td, te), lambda e, k, half_ref: (e, 0, 0))],
        out_shape=[jax.ShapeDtypeStruct((N_SHARDS, td, te), F32)],
        scratch_shapes=[], semantics=("parallel", "arbitrary"), vmem_mib=56, args=(x, dh), comm=comm,
        prefetch=(half,))
    return g, exchanged


def _grad_w_out(y, dzb):
    seq = y.shape[0]
    ts, te = 512, 1024
    nk = seq // ts

    def body(y_ref, dz_ref, o_ref, acc_ref):
        k = pl.program_id(1)

        @pl.when(k == 0)
        def _():
            acc_ref[...] = jnp.zeros_like(acc_ref)

        acc_ref[...] += _dot_tn(y_ref[...], dz_ref[...])

        @pl.when(k == nk - 1)
        def _():
            o_ref[...] = acc_ref[...]

    return _pallas(
        body, name="grad_w_out", grid=(D_MODEL // te, nk),
        in_specs=[pl.BlockSpec((ts, te), lambda e, k: (k, e)), pl.BlockSpec((ts, D_MODEL), lambda e, k: (k, 0))],
        out_specs=pl.BlockSpec((te, D_MODEL), lambda e, k: (e, 0)),
        out_shape=jax.ShapeDtypeStruct((D_MODEL, D_MODEL), F32),
        scratch_shapes=[pltpu.VMEM((te, D_MODEL), F32)],
        compiler_params=_params(("parallel", "arbitrary"), 48),
    )(y, dzb)


GRAD_X_LATE_SHARDS = 1


def _grad_x_partial(dh, w_in_g, first, tiles, prev=None, comm=None):
    seq = dh.shape[0]
    tm, tk = 512, SHARD_IN

    def body(*refs):
        dh_ref, w_ref, o_ref = refs[-3:]
        k = pl.program_id(1)
        part = _dot_nt(dh_ref[...], w_ref[...])

        @pl.when(k == 0)
        def _():
            o_ref[...] = part

        @pl.when(k > 0)
        def _():
            o_ref[...] += part

    carried = [] if prev is None else [prev]
    (partial,), exchanged = _call(
        body, name="grad_x_partial_%d" % first, grid=(tiles, N_SHARDS - GRAD_X_LATE_SHARDS),
        in_specs=[ANY] * len(carried) + [
            pl.BlockSpec((tm, tk), lambda i, k: (i + first, k)),
            pl.BlockSpec((None, D_MODEL, tk), lambda i, k: (k, 0, 0))],
        out_specs=[pl.BlockSpec((tm, D_MODEL), lambda i, k: (i + first, 0))],
        out_shape=[jax.ShapeDtypeStruct((seq, D_MODEL), F32)],
        scratch_shapes=[], semantics=("parallel", "arbitrary"), vmem_mib=48, args=(*carried, dh, w_in_g),
        aliases={0: 0} if carried else None, comm=comm)
    return partial, exchanged


def _grad_x_final(dh, w_in_g, dz, partial):
    seq = dh.shape[0]
    tm, tk = 512, SHARD_IN
    k0 = N_SHARDS - GRAD_X_LATE_SHARDS

    def body(dh_ref, w_ref, dz_ref, p_ref, o_ref):
        k = pl.program_id(1)
        part = _dot_nt(dh_ref[...], w_ref[...])

        @pl.when(k == 0)
        def _():
            o_ref[...] = (DEEPNORM_ALPHA * dz_ref[...] + p_ref[...]) + part

        @pl.when(k > 0)
        def _():
            o_ref[...] += part

    row = pl.BlockSpec((tm, D_MODEL), lambda i, k: (i, 0))
    return _pallas(
        body, name="grad_x_final", grid=(seq // tm, GRAD_X_LATE_SHARDS),
        in_specs=[pl.BlockSpec((tm, tk), lambda i, k: (i, k + k0)),
                  pl.BlockSpec((None, D_MODEL, tk), lambda i, k: (k + k0, 0, 0)), row, row],
        out_specs=row, out_shape=jax.ShapeDtypeStruct((seq, D_MODEL), F32),
        compiler_params=_params(("parallel", "arbitrary"), 48),
    )(dh, w_in_g, dz, partial)


def _pool_weight(w_pool_sh):
    n_groups = len(POOL_WINDOWS)
    shard_c = POOL_GROUP_DIM // N_SHARDS
    return (w_pool_sh.reshape(N_SHARDS, n_groups, shard_c, POOL_GROUP_DIM).transpose(1, 0, 2, 3)
            .reshape(n_groups, POOL_GROUP_DIM, POOL_GROUP_DIM))


def _pool_grad_pieces(g_w_pool):
    n_groups = len(POOL_WINDOWS)
    half_c = POOL_GROUP_DIM // N_SHARDS // 2
    return (g_w_pool.reshape(n_groups, N_SHARDS, 2, half_c, POOL_GROUP_DIM).transpose(1, 2, 0, 3, 4)
            .reshape(N_SHARDS, 2, n_groups * half_c, POOL_GROUP_DIM))


def _step(x, target, w_in_g, w_rest, pool_scale, gain, bias, place=None):
    seq = x.shape[0]
    tabs = _rope_tables(seq)
    qkv, gathered = _in_proj_qkv(x, w_in_g, tabs, comm=_allgather_weights(w_rest, "neighbours") if place else None)
    hug, gathered = _in_proj_pool_gate(x, w_in_g, _allgather_weights(gathered, "diagonal") if place else None)
    w_out_g, w_pool_sh = gathered if place else w_rest
    w_pool_g = _pool_weight(w_pool_sh)
    o_list, st_list = [], []
    for p, dil in enumerate(DILATIONS):
        o, st = _attn_fwd(qkv[p], "attn_fwd_d%d" % dil)
        o_list.append(o)
        st_list.append(st)
    y, mixpre, lse_all, pooled = _mix_gate(o_list, st_list, hug, w_pool_g, pool_scale)
    dz, dzb, gain_part, bias_part, loss_part = _out_proj_loss(y, w_out_g, x, target, gain, bias)
    dh, dpo, do_list, stat_list = _dy_gate_bwd(dzb, w_out_g, hug, mixpre, pool_scale, lse_all)
    g_w_out = _grad_w_out(y, dzb)
    dh, g_w_pool, scale_part = _pool_bwd(dh, dpo, mixpre, pooled, w_pool_g, pool_scale)
    small = jnp.concatenate([scale_part, gain_part, bias_part, loss_part], axis=1)
    early = [g_w_out.reshape(N_SHARDS, 2, D_MODEL // (2 * N_SHARDS), D_MODEL), _pool_grad_pieces(g_w_pool)]

    bwd = lambda p, comm: _attn_bwd(qkv[p], do_list[p], stat_list[p], "attn_bwd_d%d" % DILATIONS[p], comm)
    if place is None:
        parts = [bwd(p, None)[0] for p in range(3)]
    else:
        core, chip_core, onward = place
        part_a, recv = bwd(0, _exchange_halves(early))
        sums = [_add_own_half(g, r, core, "add_own_half_%d" % a) for a, (g, r) in enumerate(zip(early, recv))]
        part_b, recv = bwd(1, _scatter_to_chips([s[1] for s in sums]))
        bufs = [_add_chips(s[0], r, chip_core, "add_chips_%d" % a) for a, (s, r) in enumerate(zip(sums, recv))]
        part_c, early = bwd(2, _share_with_sibling(bufs))
        parts = [part_a, part_b, part_c]
    dh, gathered = _sum_patterns(dh, [t[0] for t in parts], tabs, True, 0, "sum_dq",
                                 _gather_small(small) if place else None)
    dh, _ = _sum_patterns(dh, [t[1] for t in parts], tabs, True, 1, "sum_dk")
    dh, _ = _sum_patterns(dh, [t[2] for t in parts], tabs, False, 2, "sum_dv")
    if place:
        small = (small, gathered[0])
    if place is None:
        halves = [_grad_w_in(x, dh, jnp.full((1,), h, jnp.int32), "grad_w_in_%d" % h)[0] for h in range(2)]
        g_w_in = jnp.stack(halves, axis=1)
        g_x = _grad_x_final(dh, w_in_g, dz, _grad_x_partial(dh, w_in_g, 0, seq // 512)[0])
    else:
        give, _ = _grad_w_in(x, dh, 1 - core, "grad_w_in_give")
        keep, recv = _grad_w_in(x, dh, core, "grad_w_in_keep", _send_to_sibling([give]))
        total, total_b = _add_pair(keep, recv[0], "add_own_half_w_in")
        n_tiles = seq // 512
        tiles = 3 * n_tiles // 8
        part, relayed = _grad_x_partial(dh, w_in_g, 0, tiles, None, _relay_diagonal(total_b))
        total_b = _fold_relayed(total, total_b, relayed[0], onward)
        part, recv = _grad_x_partial(dh, w_in_g, tiles, n_tiles - tiles, part, _scatter_to_neighbours(total_b))
        buf = _add_chips(total, recv[0], chip_core, "add_chips_w_in")
        g_x = _grad_x_final(dh, w_in_g, dz, part)
        g_w_in = _run_exchange(_share_with_sibling([buf]), "share_w_in")[0]
    return g_x, g_w_in, early[0], early[1], small


def _exchange_halves(grads):
    n = len(grads)

    def copies(src, dst, sems):
        x, y, c, _ = _mesh_place()
        return [_remote(src[a].at[j, 1 - c], dst[a].at[j], sems[0].at[a, j], sems[1].at[a, j], (x, y, 1 - c))
                for a in range(n) for j in range(N_SHARDS)]

    def start(src, dst, sems):
        for cp in copies(src, dst, sems):
            cp.start()

    def finish(src, dst, sems):
        for cp in copies(src, dst, sems):
            cp.wait()

    return _Exchange(grads, [jax.ShapeDtypeStruct((N_SHARDS,) + g.shape[2:], g.dtype) for g in grads], {},
                     [pltpu.SemaphoreType.DMA((n, N_SHARDS))] * 2, start, finish)


def _add_own_half(grad, recv, core, name):
    _, _, r, c = grad.shape
    tr = min(r, 256)

    def body(core_ref, g_ref, r_ref, o_ref, ob_ref):
        tot = g_ref[...] + r_ref[...]
        o_ref[...] = tot
        ob_ref[...] = tot.astype(BF16)

    out = pl.BlockSpec((None, tr, c), lambda j, i, core_ref: (j, i, 0))
    return _pallas(
        body, name=name,
        grid_spec=pltpu.PrefetchScalarGridSpec(
            num_scalar_prefetch=1, grid=(N_SHARDS, r // tr),
            in_specs=[pl.BlockSpec((None, None, tr, c), lambda j, i, core_ref: (j, core_ref[0], i, 0)),
                      pl.BlockSpec((None, tr, c), lambda j, i, core_ref: (j, i, 0))],
            out_specs=[out, out]),
        out_shape=[jax.ShapeDtypeStruct((N_SHARDS, r, c), F32), jax.ShapeDtypeStruct((N_SHARDS, r, c), BF16)],
        compiler_params=_params(("parallel", "parallel"), 32),
    )(core, grad, recv)


def _send_to_sibling(arrays):
    n = len(arrays)

    def copies(src, dst, sems):
        x, y, c, _ = _mesh_place()
        return [_remote(src[a], dst[a], sems[0].at[a], sems[1].at[a], (x, y, 1 - c)) for a in range(n)]

    def start(src, dst, sems):
        for cp in copies(src, dst, sems):
            cp.start()

    def finish(src, dst, sems):
        for cp in copies(src, dst, sems):
            cp.wait()

    return _Exchange(arrays, [jax.ShapeDtypeStruct(t.shape, t.dtype) for t in arrays], {},
                     [pltpu.SemaphoreType.DMA((n,))] * 2, start, finish)


def _add_pair(a, b, name):
    _, r, c = a.shape
    tr = min(r, 256)

    def body(a_ref, b_ref, o_ref, ob_ref):
        tot = a_ref[...] + b_ref[...]
        o_ref[...] = tot
        ob_ref[...] = tot.astype(BF16)

    spec = pl.BlockSpec((None, tr, c), lambda j, i: (j, i, 0))
    return _pallas(
        body, name=name, grid=(N_SHARDS, r // tr), in_specs=[spec, spec], out_specs=[spec, spec],
        out_shape=[jax.ShapeDtypeStruct(a.shape, F32), jax.ShapeDtypeStruct(a.shape, BF16)],
        compiler_params=_params(("parallel", "parallel"), 32),
    )(a, b)


def _scatter_to_chips(sums, rows=None, into=None):
    n = len(sums)

    def copies(src, dst, sems):
        x, y, c, chips = _mesh_place()
        part = (lambda ref: ref) if rows is None else (lambda ref: ref.at[pl.ds(rows[0], rows[1])])
        return [_remote(part(src[a].at[2 * cx + cy]), part(dst[a].at[k]), sems[0].at[a, k], sems[1].at[a, k],
                        (cx, cy, c))
                for a in range(n) for k, (cx, cy) in enumerate(chips)]

    def start(src, dst, sems):
        for cp in copies(src, dst, sems):
            cp.start()

    def finish(src, dst, sems):
        for cp in copies(src, dst, sems):
            cp.wait()

    return _Exchange(sums + (into or []), [jax.ShapeDtypeStruct((3,) + s.shape[1:], s.dtype) for s in sums],
                     {n + a: a for a in range(n)} if into else {},
                     [pltpu.SemaphoreType.DMA((n, 3))] * 2, start, finish)


def _add_chips(sums, recv, chip_core, name):
    _, r, c = sums.shape
    n_recv = recv.shape[0]
    tr = min(r, 256)

    def body(cc_ref, s_ref, r_ref, o_ref):
        tot = s_ref[...]
        for k in range(n_recv):
            tot = tot + r_ref[k].astype(F32)
        o_ref[...] = tot

    return _pallas(
        body, name=name,
        grid_spec=pltpu.PrefetchScalarGridSpec(
            num_scalar_prefetch=1, grid=(r // tr,),
            in_specs=[pl.BlockSpec((None, tr, c), lambda i, cc_ref: (cc_ref[0], i, 0)),
                      pl.BlockSpec((n_recv, tr, c), lambda i, cc_ref: (0, i, 0))],
            out_specs=pl.BlockSpec((None, tr, c), lambda i, cc_ref: (cc_ref[1], i, 0))),
        out_shape=jax.ShapeDtypeStruct((2, r, c), F32),
        compiler_params=_params(("parallel",), 32),
    )(chip_core, sums, recv)


def _relay_diagonal(sums_b):
    def copy(src, dst, sems):
        x, y, c, _ = _mesh_place()
        diagonal = 2 * (1 - x) + (1 - y)
        return _remote(src[0].at[diagonal], dst[0], sems[0].at[0], sems[1].at[0], (x ^ (1 - c), y ^ c, c))

    def start(src, dst, sems):
        copy(src, dst, sems).start()

    def finish(src, dst, sems):
        copy(src, dst, sems).wait()

    return _Exchange([sums_b], [jax.ShapeDtypeStruct(sums_b.shape[1:], sums_b.dtype)], {},
                     [pltpu.SemaphoreType.DMA((1,))] * 2, start, finish)


def _fold_relayed(sums, sums_b, relayed, onward):
    _, r, c = sums.shape
    tr = min(r, 256)

    def body(on_ref, b_in_ref, s_ref, r_ref, o_ref):
        o_ref[...] = (s_ref[...] + r_ref[...].astype(F32)).astype(BF16)

    return _pallas(
        body, name="fold_relayed",
        grid_spec=pltpu.PrefetchScalarGridSpec(
            num_scalar_prefetch=1, grid=(r // tr,),
            in_specs=[ANY, pl.BlockSpec((None, tr, c), lambda i, on_ref: (on_ref[0], i, 0)),
                      pl.BlockSpec((tr, c), lambda i, on_ref: (i, 0))],
            out_specs=pl.BlockSpec((None, tr, c), lambda i, on_ref: (on_ref[0], i, 0))),
        out_shape=jax.ShapeDtypeStruct(sums_b.shape, sums_b.dtype),
        input_output_aliases={1: 0},
        compiler_params=_params(("parallel",), 32),
    )(onward, sums_b, sums, relayed)


def _scatter_to_neighbours(sums_b):
    def copies(src, dst, sems):
        x, y, c, chips = _mesh_place()
        return [_remote(src[0].at[2 * cx + cy], dst[0].at[k], sems[0].at[k], sems[1].at[k], (cx, cy, c))
                for k, (cx, cy) in enumerate(chips[:2])]

    def start(src, dst, sems):
        for cp in copies(src, dst, sems):
            cp.start()

    def finish(src, dst, sems):
        for cp in copies(src, dst, sems):
            cp.wait()

    return _Exchange([sums_b], [jax.ShapeDtypeStruct((2,) + sums_b.shape[1:], sums_b.dtype)], {},
                     [pltpu.SemaphoreType.DMA((2,))] * 2, start, finish)


def _share_with_sibling(bufs):
    n = len(bufs)

    def copies(dst, sems, half):
        x, y, c, _ = _mesh_place()
        h = c if half == "mine" else 1 - c
        return [_remote(dst[a].at[h], dst[a].at[h], sems[0].at[a], sems[1].at[a], (x, y, 1 - c)) for a in range(n)]

    def start(ins, dst, sems):
        for cp in copies(dst, sems, "mine"):
            cp.start()

    def finish(ins, dst, sems):
        for cp in copies(dst, sems, "theirs"):
            cp.wait_recv()
        for cp in copies(dst, sems, "mine"):
            cp.wait_send()

    return _Exchange(bufs, [jax.ShapeDtypeStruct(b.shape, b.dtype) for b in bufs], {a: a for a in range(n)},
                     [pltpu.SemaphoreType.DMA((n,))] * 2, start, finish)


def _adam_math(w, g, m, v):
    m = ADAM_B1 * m + (1.0 - ADAM_B1) * g
    v = ADAM_B2 * v + (1.0 - ADAM_B2) * (g * g)
    m_hat = m / (1.0 - ADAM_B1 ** ADAM_STEP)
    v_hat = v / (1.0 - ADAM_B2 ** ADAM_STEP)
    delta = -ADAM_LR * (m_hat / (jnp.sqrt(v_hat) + ADAM_EPS) + ADAM_WD * w)
    return delta, m, v


def _gather_small(small):
    def peers():
        x, y, c, _ = _mesh_place()
        return [(x ^ ((r >> 2) & 1), y ^ ((r >> 1) & 1), c ^ (r & 1)) for r in range(1, 8)], 4 * x + 2 * y + c

    def start(src, dst, sems):
        to, me = peers()
        for r, peer in enumerate(to):
            _remote(src[0], dst[0].at[me], sems[0].at[r], sems[1].at[r], peer).start()

    def finish(src, dst, sems):
        to, me = peers()
        for r, (px, py, pc) in enumerate(to):
            theirs = dst[0].at[4 * px + 2 * py + pc]
            _remote(theirs, theirs, sems[0].at[r], sems[1].at[r], (px, py, pc)).wait_recv()
        for r, peer in enumerate(to):
            _remote(src[0], dst[0].at[me], sems[0].at[r], sems[1].at[r], peer).wait_send()

    return _Exchange([small], [jax.ShapeDtypeStruct((8,) + small.shape, small.dtype)], {},
                     [pltpu.SemaphoreType.DMA((7,))] * 2, start, finish)


def _small_adamw(gathered, small, me, w_vec, m_vec, v_vec):
    n_par = w_vec.shape[1]

    def body(me_ref, a_ref, s_ref, w_ref, m_ref, v_ref, loss_ref, g_ref, d_ref, nm_ref, nv_ref):
        mine = s_ref[...]
        tot = jnp.where(me_ref[0] == 0, mine, a_ref[0])
        for d in range(1, 8):
            tot = tot + jnp.where(me_ref[0] == d, mine, a_ref[d])
        tot = jnp.sum(tot, axis=0, keepdims=True)
        sq = jnp.sum(tot[:, n_par:], axis=1, keepdims=True)
        loss_ref[...] = jnp.broadcast_to(sq * (0.5 / D_MODEL), loss_ref.shape)
        g = tot[:, :n_par]
        g_ref[...] = g
        d_ref[...], nm_ref[...], nv_ref[...] = _adam_math(w_ref[...], g, m_ref[...], v_ref[...])

    vm = pl.BlockSpec(memory_space=pltpu.VMEM)
    vec = jax.ShapeDtypeStruct((1, n_par), F32)
    return pl.pallas_call(
        body, name="small_adamw",
        grid_spec=pltpu.PrefetchScalarGridSpec(num_scalar_prefetch=1, grid=(), in_specs=[vm] * 5, out_specs=[vm] * 5),
        out_shape=[jax.ShapeDtypeStruct((1, 128), F32), vec, vec, vec, vec],
    )(me, gathered, small, w_vec, m_vec, v_vec)


def _adamw(w, g, m, v, name):
    r, c = w.shape
    tr = min(r, 256)

    def body(w_ref, g_ref, m_ref, v_ref, d_ref, nm_ref, nv_ref):
        d_ref[...], nm_ref[...], nv_ref[...] = _adam_math(w_ref[...], g_ref[...], m_ref[...], v_ref[...])

    spec = pl.BlockSpec((tr, c), lambda i: (i, 0))
    shape = jax.ShapeDtypeStruct((r, c), F32)
    return _pallas(
        body, name=name, grid=(r // tr,),
        in_specs=[spec] * 4, out_specs=[spec] * 3, out_shape=[shape] * 3,
        compiler_params=_params(("parallel",), 48),
    )(w, g, m, v)


def kernel(x, w_in, w_pool, pool_scale, w_out, ln_gain, ln_bias, loss_target, m_w_in, m_w_pool, m_pool_scale, m_w_out, m_ln_gain, m_ln_bias, v_w_in, v_w_pool, v_pool_scale, v_w_out, v_ln_gain, v_ln_bias):
    xi, yi, ci = lax.axis_index("x"), lax.axis_index("y"), lax.axis_index("c")
    chip = (2 * xi + yi).astype(jnp.int32).reshape(1)
    core = ci.astype(jnp.int32).reshape(1)
    n_groups = len(POOL_WINDOWS)
    shard_c = w_pool.shape[2]

    w_in_b = _cast_bf16(w_in[0], chip, "cast_w_in", 256)
    w_out_b = _cast_bf16(w_out[0], chip, "cast_w_out", 256)
    w_pool_b = _cast_bf16(w_pool[0].reshape(n_groups * shard_c, POOL_GROUP_DIM), chip, "cast_w_pool", 256)
    w_in_g = _run_exchange(_allgather_weights([w_in_b], chunks=4), "allgather_w_in")[0]

    chip_core = jnp.concatenate([chip, core])
    onward = (2 * (xi ^ ci) + (yi ^ (1 - ci))).astype(jnp.int32).reshape(1)
    g_x, full_in, full_out, full_pool, small = _step(
        x[0], loss_target[0], w_in_g, [w_out_b, w_pool_b], pool_scale, ln_gain, ln_bias, (core, chip_core, onward))
    half_c = shard_c // 2
    grad_w_in = full_in.reshape(D_MODEL, SHARD_IN)
    grad_w_out = full_out.reshape(D_MODEL // N_SHARDS, D_MODEL)
    grad_w_pool = (full_pool.reshape(2, n_groups, half_c, POOL_GROUP_DIM).transpose(1, 0, 2, 3)
                   .reshape(n_groups * shard_c, POOL_GROUP_DIM))

    d_in, nm_in, nv_in = _adamw(w_in[0], grad_w_in, m_w_in[0], v_w_in[0], "adamw_w_in")
    d_out, nm_out, nv_out = _adamw(w_out[0], grad_w_out, m_w_out[0], v_w_out[0], "adamw_w_out")
    flat = lambda t: t[0].reshape(n_groups * shard_c, POOL_GROUP_DIM)
    d_pool, nm_pool, nv_pool = _adamw(flat(w_pool), grad_w_pool, flat(m_w_pool), flat(v_w_pool), "adamw_w_pool")

    cat = lambda a, b, c: jnp.concatenate([a, b, c], axis=1)
    me = (4 * xi + 2 * yi + ci).astype(jnp.int32).reshape(1)
    loss_v, g_vec, d_vec, nm_vec, nv_vec = _small_adamw(
        small[1], small[0], me, cat(pool_scale, ln_gain, ln_bias), cat(m_pool_scale, m_ln_gain, m_ln_bias),
        cat(v_pool_scale, v_ln_gain, v_ln_bias))

    def split(vec):
        return vec[:, :D_POOL], vec[:, D_POOL:D_POOL + D_MODEL], vec[:, D_POOL + D_MODEL:]

    g_scale, g_gain, g_bias = split(g_vec)
    d_scale, d_gain, d_bias = split(d_vec)
    nm_scale, nm_gain, nm_bias = split(nm_vec)
    nv_scale, nv_gain, nv_bias = split(nv_vec)
    pool_shape = w_pool.shape
    return (loss_v[0, 0], g_x[None],
            grad_w_in[None], grad_w_pool.reshape(pool_shape), g_scale, grad_w_out[None], g_gain, g_bias,
            d_in[None], d_pool.reshape(pool_shape), d_scale, d_out[None], d_gain, d_bias,
            nm_in[None], nm_pool.reshape(pool_shape), nm_scale, nm_out[None], nm_gain, nm_bias,
            nv_in[None], nv_pool.reshape(pool_shape), nv_scale, nv_out[None], nv_gain, nv_bias)
```

```python
import functools

import jax
import jax.numpy as jnp
from jax import lax
from jax.experimental import pallas as pl
from jax.experimental.pallas import tpu as pltpu

F32 = jnp.float32
BF16 = jnp.bfloat16
MESH = pl.DeviceIdType.MESH
ANY = pl.BlockSpec(memory_space=pl.ANY)

D_MODEL = 2048
D_ATTN = 1024
D_POOL = 1024
HEAD_DIM = 128
N_HEADS = 8
ROPE_DIM = 32
ROPE_THETA = 500000.0
DILATIONS = (1, 4, 16)
KEY_BLOCK = 128
CHUNK = 2 * KEY_BLOCK
STAT_LANES = 128
POOL_WINDOWS = (2, 4, 8, 16)
POOL_GROUP_DIM = 256
POOL_HALO = 16
D_QKV = 3 * D_ATTN
D_UG = D_POOL + D_MODEL
D_IN = D_QKV + D_UG
N_SHARDS = 4
SHARD_IN = D_IN // N_SHARDS
LN_EPS = 1e-5
DEEPNORM_ALPHA = 2.0 ** 0.25
ADAM_LR = 0.001
ADAM_B1 = 0.9
ADAM_B2 = 0.999
ADAM_EPS = 1e-08
ADAM_WD = 0.01
ADAM_STEP = 10
NEG = -1e30
MIB = 1024 * 1024


def _params(sem, vmem_mib):
    return pltpu.CompilerParams(dimension_semantics=sem, vmem_limit_bytes=vmem_mib * MIB)


def _pallas(body, **kwargs):
    pin = lambda s: pltpu.HBM(s.shape, s.dtype) if len(s.shape) >= 2 else s
    out_shape = kwargs.pop("out_shape")
    out_shape = [pin(s) for s in out_shape] if isinstance(out_shape, (list, tuple)) else pin(out_shape)
    call = pl.pallas_call(body, out_shape=out_shape, **kwargs)

    def run(*operands):
        return call(*[pltpu.with_memory_space_constraint(o, pltpu.HBM) if o.ndim >= 2 else o for o in operands])

    return run


class _Exchange:
    def __init__(self, operands, out_shape, aliases, sems, start, finish):
        self.operands, self.out_shape, self.aliases, self.sems = list(operands), list(out_shape), dict(aliases), list(sems)
        self.start, self.finish = start, finish


def _run_exchange(comm, name):
    n_in, n_out = len(comm.operands), len(comm.out_shape)

    def body(*refs):
        ins, outs, sems = refs[:n_in], refs[n_in:n_in + n_out], refs[n_in + n_out:]
        comm.start(ins, outs, sems)
        comm.finish(ins, outs, sems)

    return _pallas(
        body, name=name, in_specs=[ANY] * n_in, out_specs=[ANY] * n_out, out_shape=comm.out_shape,
        input_output_aliases=comm.aliases, scratch_shapes=comm.sems,
    )(*comm.operands)


def _call(body, *, name, grid, in_specs, out_specs, out_shape, scratch_shapes, semantics, vmem_mib, args,
          aliases=None, comm=None, prefetch=()):
    aliases = dict(aliases or {})
    n_pre, n_in, n_out, n_scr = len(prefetch), len(in_specs), len(out_specs), len(scratch_shapes)
    c_in, c_out = (len(comm.operands), len(comm.out_shape)) if comm else (0, 0)
    c_shapes, c_sems, c_operands = (comm.out_shape, comm.sems, comm.operands) if comm else ([], [], [])

    def hosted(*refs):
        pre, refs = refs[:n_pre], refs[n_pre:]
        a = n_in
        b = a + c_in
        c = b + n_out
        d = c + c_out
        e = d + n_scr
        if comm is None:
            body(*pre, *refs)
            return
        ids = [pl.program_id(k) for k in range(len(grid))]
        first = functools.reduce(jnp.logical_and, [i == 0 for i in ids])
        last = functools.reduce(jnp.logical_and, [i == g - 1 for i, g in zip(ids, grid)])

        @pl.when(first)
        def _():
            comm.start(refs[a:b], refs[c:d], refs[e:])

        body(*pre, *refs[:a], *refs[b:c], *refs[d:e])

        @pl.when(last)
        def _():
            comm.finish(refs[a:b], refs[c:d], refs[e:])

    if comm:
        semantics = ("arbitrary",) * len(grid)
        for i, o in comm.aliases.items():
            aliases[n_pre + n_in + i] = n_out + o
    outs = _pallas(
        hosted, name=name,
        grid_spec=pltpu.PrefetchScalarGridSpec(
            num_scalar_prefetch=n_pre, grid=grid, in_specs=list(in_specs) + [ANY] * c_in,
            out_specs=list(out_specs) + [ANY] * c_out, scratch_shapes=list(scratch_shapes) + c_sems),
        out_shape=list(out_shape) + c_shapes, input_output_aliases=aliases,
        compiler_params=_params(semantics, vmem_mib),
    )(*prefetch, *args, *c_operands)
    return list(outs[:n_out]), list(outs[n_out:])


def _dot_nn(a, b):
    return jnp.dot(a, b, preferred_element_type=F32)


def _dot_nt(a, b):
    return lax.dot_general(a, b, (((1,), (1,)), ((), ())), preferred_element_type=F32)


def _dot_tn(a, b):
    return lax.dot_general(a, b, (((0,), (0,)), ((), ())), preferred_element_type=F32)


def _fold_rows(a):
    r, c = a.shape
    return jnp.sum(a.reshape(r // 8, 8, c), axis=0)


def _cast_bf16(a, chip, name, rows):
    r, c = a.shape

    def body(chip_ref, a_ref, o_ref):
        o_ref[...] = a_ref[...].astype(BF16)

    return _pallas(
        body, name=name,
        grid_spec=pltpu.PrefetchScalarGridSpec(
            num_scalar_prefetch=1, grid=(r // rows,),
            in_specs=[pl.BlockSpec((rows, c), lambda i, chip_ref: (i, 0))],
            out_specs=pl.BlockSpec((None, rows, c), lambda i, chip_ref: (chip_ref[0], i, 0))),
        out_shape=jax.ShapeDtypeStruct((N_SHARDS, r, c), BF16),
        compiler_params=_params(("parallel",), 32),
    )(chip, a)


def _mesh_place():
    x, y, c = lax.axis_index("x"), lax.axis_index("y"), lax.axis_index("c")
    return x, y, c, [(1 - x, y), (x, 1 - y), (1 - x, 1 - y)]


def _remote(src, dst, send_sem, recv_sem, to):
    return pltpu.make_async_remote_copy(src_ref=src, dst_ref=dst, send_sem=send_sem, recv_sem=recv_sem,
                                        device_id=to, device_id_type=MESH)


def _allgather_weights(bufs, phase="all", chunks=1):
    n = len(bufs)
    items = [(a, q) for q in range(chunks) for a in range(n)]

    def rows(item, core):
        a, q = item
        size = bufs[a].shape[1] // 2 // chunks
        return pl.ds(core * chunks * size + q * size, size)

    def sem(sems, which, item, k):
        a, q = item
        return sems[which].at[a * chunks + q, k]

    DIAGONAL = 2

    def to_neighbours(dst, sems, item):
        x, y, c, chips = _mesh_place()
        own = dst[item[0]].at[2 * x + y, rows(item, c)]
        return [_remote(own, own, sem(sems, 0, item, k), sem(sems, 1, item, k), (cx, cy, c))
                for k, (cx, cy) in enumerate(chips[:DIAGONAL])]

    def relayed(dst, sems, item):
        x, y, c, _ = _mesh_place()
        piece = dst[item[0]].at[2 * (x ^ (1 - c)) + (y ^ c), rows(item, c)]
        return _remote(piece, piece, sem(sems, 0, item, DIAGONAL), sem(sems, 1, item, DIAGONAL), (x ^ c, y ^ (1 - c), c))

    def start(ins, dst, sems):
        for item in items:
            for cp in ([relayed(dst, sems, item)] if phase == "diagonal" else to_neighbours(dst, sems, item)):
                cp.start()

    def finish(ins, dst, sems):
        x, y, c, chips = _mesh_place()
        sibling = (x, y, 1 - c)
        passed_on = []

        def landed_then_pass_on(item, k):
            cx, cy = chips[k]
            landed = dst[item[0]].at[2 * cx + cy, rows(item, c)]
            _remote(landed, landed, sem(sems, 0, item, k), sem(sems, 1, item, k), (cx, cy, c)).wait_recv()
            cp = _remote(landed, landed, sem(sems, 2, item, k), sem(sems, 3, item, k), sibling)
            cp.start()
            passed_on.append(cp)

        sent = []
        for item in items:
            if phase != "diagonal":
                for k in range(DIAGONAL):
                    landed_then_pass_on(item, k)
                sent += to_neighbours(dst, sems, item)
            if phase == "all":
                relayed(dst, sems, item).start()
        for item in items:
            if phase != "neighbours":
                landed_then_pass_on(item, DIAGONAL)
                sent.append(relayed(dst, sems, item))
        for k in {"all": (0, 1, 2), "neighbours": (0, 1), "diagonal": (2,)}[phase]:
            cx, cy = chips[k]
            for item in items:
                passed = dst[item[0]].at[2 * cx + cy, rows(item, 1 - c)]
                _remote(passed, passed, sem(sems, 2, item, k), sem(sems, 3, item, k), sibling).wait_recv()
        for cp in sent + passed_on:
            cp.wait_send()

    return _Exchange(bufs, [jax.ShapeDtypeStruct(b.shape, b.dtype) for b in bufs], {a: a for a in range(n)},
                     [pltpu.SemaphoreType.DMA((n * chunks, 3))] * 4, start, finish)


def _rope_tables(seq):
    half = ROPE_DIM // 2
    inv_freq = ROPE_THETA ** (-(2.0 * jnp.arange(half, dtype=F32)) / ROPE_DIM)
    ang = jnp.arange(seq, dtype=jnp.int32).astype(F32)[:, None] * inv_freq[None, :]
    cos, sin = jnp.cos(ang), jnp.sin(ang)
    pad = jnp.zeros((seq, HEAD_DIM - ROPE_DIM), F32)
    zeros = jnp.zeros((seq, half), F32)
    c_tab = jnp.concatenate([cos, cos, pad + 1.0], axis=1)
    up_tab = jnp.concatenate([-sin, zeros, pad], axis=1)
    down_tab = jnp.concatenate([zeros, sin, pad], axis=1)
    return c_tab, up_tab, down_tab


def _rotate_heads(t, c_tab, up_tab, down_tab):
    outs = []
    for h in range(t.shape[1] // HEAD_DIM):
        th = t[:, h * HEAD_DIM:(h + 1) * HEAD_DIM]
        up = pltpu.roll(th, HEAD_DIM - ROPE_DIM // 2, axis=1)
        down = pltpu.roll(th, ROPE_DIM // 2, axis=1)
        outs.append(th * c_tab + up * up_tab + down * down_tab)
    return outs[0] if len(outs) == 1 else jnp.concatenate(outs, axis=1)


def _to_pattern(slabs_ref, dst_ref, dil, dtype):
    n_slabs, rows, _ = slabs_ref.shape
    for s in range(n_slabs):
        for r in range(dil):
            dst_ref[r, :, s * 128:(s + 1) * 128] = slabs_ref[s, pl.ds(r, rows // dil, dil), :].astype(dtype)


def _from_pattern(src_ref, slabs_ref, dil):
    n_slabs, rows, _ = slabs_ref.shape
    for s in range(n_slabs):
        for r in range(dil):
            slabs_ref[s, pl.ds(r, rows // dil, dil), :] = src_ref[r, :, s * 128:(s + 1) * 128].astype(F32)


def _store_slabs(slabs_ref, value):
    for s in range(slabs_ref.shape[0]):
        slabs_ref[s] = value[:, s * 128:(s + 1) * 128]


def _in_proj_qkv(x, w_in_g, tabs, comm=None):
    seq = x.shape[0]
    tm, tn = 512, SHARD_IN
    heads = tn // HEAD_DIM
    k_heads_in_second = 2 * D_ATTN // HEAD_DIM - heads
    d4, d16 = DILATIONS[1], DILATIONS[2]

    def body(x_ref, w_ref, c_ref, up_ref, down_ref, o1_ref, o4_ref, o16_ref, res_ref):
        shard = pl.program_id(0)
        xb = x_ref[...].astype(BF16)
        group = 4 * HEAD_DIM
        accs = [_dot_nn(xb, w_ref[:, g * group:(g + 1) * group]) for g in range(tn // group)]

        plain = shard == 1
        c_plain = jnp.where(plain, 1.0, c_ref[...])
        up_plain = jnp.where(plain, 0.0, up_ref[...])
        down_plain = jnp.where(plain, 0.0, down_ref[...])
        for h in range(heads):
            lanes = (h * HEAD_DIM) % group
            th = accs[h * HEAD_DIM // group][:, lanes:lanes + HEAD_DIM]
            if h < k_heads_in_second:
                th = _rotate_heads(th, c_ref[...], up_ref[...], down_ref[...])
            else:
                th = _rotate_heads(th, c_plain, up_plain, down_plain)
            res_ref[h] = th
            o1_ref[:, h * HEAD_DIM:(h + 1) * HEAD_DIM] = th.astype(BF16)
        _to_pattern(res_ref, o4_ref, d4, BF16)
        _to_pattern(res_ref, o16_ref, d16, BF16)

    tab_spec = pl.BlockSpec((tm, HEAD_DIM), lambda s, i: (i, 0))
    (o1, o4, o16), exchanged = _call(
        body, name="in_proj_qkv", grid=(D_QKV // tn, seq // tm),
        in_specs=[pl.BlockSpec((tm, D_MODEL), lambda s, i: (i, 0)),
                  pl.BlockSpec((None, D_MODEL, tn), lambda s, i: (s, 0, 0)),
                  tab_spec, tab_spec, tab_spec],
        out_specs=[pl.BlockSpec((tm, tn), lambda s, i: (i, s)),
                   pl.BlockSpec((d4, tm // d4, tn), lambda s, i: (0, i, s)),
                   pl.BlockSpec((d16, tm // d16, tn), lambda s, i: (0, i, s))],
        out_shape=[jax.ShapeDtypeStruct((seq, D_QKV), BF16),
                   jax.ShapeDtypeStruct((d4, seq // d4, D_QKV), BF16),
                   jax.ShapeDtypeStruct((d16, seq // d16, D_QKV), BF16)],
        scratch_shapes=[pltpu.VMEM((heads, tm, HEAD_DIM), F32)],
        semantics=("parallel", "parallel"), vmem_mib=52, args=(x, w_in_g, *tabs), comm=comm)
    return [o1[None], o4, o16], exchanged


def _in_proj_pool_gate(x, w_in_g, comm=None):
    seq = x.shape[0]
    tm, tn = 512, SHARD_IN
    first_shard = D_QKV // tn

    def body(x_ref, w_ref, o_ref):
        o_ref[...] = _dot_nn(x_ref[...].astype(BF16), w_ref[...]).astype(BF16)

    (hug,), exchanged = _call(
        body, name="in_proj_pool_gate", grid=(D_UG // tn, seq // tm),
        in_specs=[pl.BlockSpec((tm, D_MODEL), lambda s, i: (i, 0)),
                  pl.BlockSpec((None, D_MODEL, tn), lambda s, i: (s + first_shard, 0, 0))],
        out_specs=[pl.BlockSpec((tm, tn), lambda s, i: (i, s))],
        out_shape=[jax.ShapeDtypeStruct((seq, D_UG), BF16)],
        scratch_shapes=[], semantics=("parallel", "parallel"), vmem_mib=48, args=(x, w_in_g), comm=comm)
    return hug, exchanged


def _band_masks():
    row = lax.broadcasted_iota(jnp.int32, (KEY_BLOCK, KEY_BLOCK), 0)
    col = lax.broadcasted_iota(jnp.int32, (KEY_BLOCK, KEY_BLOCK), 1)
    return col <= row, col >= row


def _attn_fwd(qkv, name):
    dil, n, _ = qkv.shape
    scale = HEAD_DIM ** -0.5
    lo, hi = slice(0, KEY_BLOCK), slice(KEY_BLOCK, CHUNK)

    def body(q_ref, k_ref, v_ref, kb_ref, vb_ref, o_ref, st_ref):
        i = pl.program_id(1)
        cur_mask, prev_mask = _band_masks()
        before_mask = jnp.logical_and(prev_mask, i > 0)
        lane = lax.broadcasted_iota(jnp.int32, (KEY_BLOCK, STAT_LANES), 1)
        tasks = [(rows, h) for rows in (lo, hi) for h in range(N_HEADS)]
        head = lambda h: slice(h * HEAD_DIM, (h + 1) * HEAD_DIM)

        def prev_of(rows, h):
            if rows is lo:
                return kb_ref[:, head(h)], vb_ref[:, head(h)], before_mask
            return k_ref[lo, head(h)], v_ref[lo, head(h)], prev_mask

        scores = []
        for rows, h in tasks:
            q = q_ref[rows, head(h)]
            scores.append((_dot_nt(q, prev_of(rows, h)[0]), _dot_nt(q, k_ref[rows, head(h)])))
        probs = []
        for (rows, h), (qk_prev, qk_cur) in zip(tasks, scores):
            s_prev = jnp.where(prev_of(rows, h)[2], qk_prev * scale, NEG)
            s_cur = jnp.where(cur_mask, qk_cur * scale, NEG)
            m = jnp.max(jnp.maximum(s_prev, s_cur), axis=-1, keepdims=True)
            p_prev = jnp.exp(s_prev - m)
            p_cur = jnp.exp(s_cur - m)
            den = jnp.sum(p_prev + p_cur, axis=-1, keepdims=True)
            probs.append((p_prev.astype(BF16), p_cur.astype(BF16), den, m + jnp.log(den)))
        stats = [jnp.zeros((KEY_BLOCK, STAT_LANES), F32), jnp.zeros((KEY_BLOCK, STAT_LANES), F32)]
        for (rows, h), (p_prev, p_cur, den, lse) in zip(tasks, probs):
            o = _dot_nn(p_cur, v_ref[rows, head(h)]) + _dot_nn(p_prev, prev_of(rows, h)[1])
            o_ref[rows, head(h)] = (o / den).astype(BF16)
            b = 0 if rows is lo else 1
            stats[b] = jnp.where(lane == h, lse, stats[b])
        st_ref[lo, :] = stats[0]
        st_ref[hi, :] = stats[1]

    main = lambda cb: pl.BlockSpec((None, CHUNK, D_ATTN), lambda r, i: (r, i, cb))
    before = lambda cb: pl.BlockSpec((None, KEY_BLOCK, D_ATTN), lambda r, i: (r, jnp.maximum(2 * i - 1, 0), cb))
    return _pallas(
        body, name=name, grid=(dil, n // CHUNK),
        in_specs=[main(0), main(1), main(2), before(1), before(2)],
        out_specs=[main(0), pl.BlockSpec((None, CHUNK, STAT_LANES), lambda r, i: (r, i, 0))],
        out_shape=[jax.ShapeDtypeStruct((dil, n, D_ATTN), BF16), jax.ShapeDtypeStruct((dil, n, STAT_LANES), F32)],
        compiler_params=_params(("parallel", "parallel"), 40),
    )(qkv, qkv, qkv, qkv, qkv)


def _attn_bwd(qkv, do, stats, name, comm=None):
    dil, n, _ = qkv.shape
    n_blocks = n // KEY_BLOCK
    last = n // CHUNK - 1
    scale = HEAD_DIM ** -0.5
    lo, hi = slice(0, KEY_BLOCK), slice(KEY_BLOCK, CHUNK)

    def body(q_ref, k_ref, v_ref, kb_ref, vb_ref, qa_ref, do_ref, doa_ref, st_ref, sta_ref, dq_ref, dk_ref, dv_ref):
        i = pl.program_id(1)
        cur_mask, prev_mask = _band_masks()
        before_mask = jnp.logical_and(prev_mask, i > 0)
        after_mask = jnp.logical_and(prev_mask, i < last)

        rows_cat = lambda a, b: jnp.concatenate([a, b], axis=0)
        masks = (jnp.concatenate([before_mask, cur_mask], axis=1), jnp.concatenate([prev_mask, cur_mask], axis=1),
                 after_mask)

        def operands(h):
            cols = slice(h * HEAD_DIM, (h + 1) * HEAD_DIM)
            lse_c, del_c = slice(h, h + 1), slice(N_HEADS + h, N_HEADS + h + 1)
            q = (q_ref[lo, cols], q_ref[hi, cols], qa_ref[:, cols])
            do = (do_ref[lo, cols], do_ref[hi, cols], doa_ref[:, cols])
            keys = (rows_cat(kb_ref[:, cols], k_ref[lo, cols]), k_ref[:, cols], k_ref[hi, cols])
            vals = (rows_cat(vb_ref[:, cols], v_ref[lo, cols]), v_ref[:, cols], v_ref[hi, cols])
            st = ((st_ref[lo, lse_c], st_ref[lo, del_c]), (st_ref[hi, lse_c], st_ref[hi, del_c]),
                  (sta_ref[:, lse_c], sta_ref[:, del_c]))
            return cols, q, do, keys, vals, st

        group = N_HEADS // 2
        for first_head in range(0, N_HEADS, group):
            heads = range(first_head, first_head + group)
            raw = {}
            for h in heads:
                _, q, do, keys, vals, _ = operands(h)
                raw[h] = [(_dot_nt(q[j], keys[j]), _dot_nt(do[j], vals[j])) for j in range(3)]
            grads = {}
            for h in heads:
                st = operands(h)[5]
                grads[h] = []
                for j in range(3):
                    qk, dp = raw[h][j]
                    lse, delta = st[j]
                    p = jnp.exp(jnp.where(masks[j], qk * scale, NEG) - lse)
                    grads[h].append((p.astype(BF16), (p * (dp - delta) * scale).astype(BF16)))
            for h in heads:
                cols, q, do, keys, _, _ = operands(h)
                (p0, ds0), (p1, ds1), (pa, dsa) = grads[h]
                own, nxt = slice(KEY_BLOCK, CHUNK), slice(0, KEY_BLOCK)

                def put(ref, rows, val, cols=cols):
                    ref[rows, cols] = val.astype(ref.dtype)

                put(dq_ref, lo, _dot_nn(ds0, keys[0]))
                put(dq_ref, hi, _dot_nn(ds1, keys[1]))
                put(dk_ref, lo, _dot_tn(rows_cat(ds0[:, own], ds1[:, nxt]), q_ref[:, cols]))
                put(dk_ref, hi, _dot_tn(rows_cat(ds1[:, own], dsa), rows_cat(q[1], q[2])))
                put(dv_ref, lo, _dot_tn(rows_cat(p0[:, own], p1[:, nxt]), do_ref[:, cols]))
                put(dv_ref, hi, _dot_tn(rows_cat(p1[:, own], pa), rows_cat(do[1], do[2])))

    def spec(rows, width, row_of, cb):
        return pl.BlockSpec((None, rows, width), lambda r, i: (r, row_of(i), cb))

    same = lambda i: i
    before = lambda i: jnp.maximum(2 * i - 1, 0)
    after = lambda i: jnp.minimum(2 * i + 2, n_blocks - 1)
    out = spec(CHUNK, D_ATTN, same, 0)
    return _call(
        body, name=name, grid=(dil, n // CHUNK),
        in_specs=[spec(CHUNK, D_ATTN, same, 0), spec(CHUNK, D_ATTN, same, 1), spec(CHUNK, D_ATTN, same, 2),
                  spec(KEY_BLOCK, D_ATTN, before, 1), spec(KEY_BLOCK, D_ATTN, before, 2),
                  spec(KEY_BLOCK, D_ATTN, after, 0),
                  spec(CHUNK, D_ATTN, same, 0), spec(KEY_BLOCK, D_ATTN, after, 0),
                  spec(CHUNK, STAT_LANES, same, 0), spec(KEY_BLOCK, STAT_LANES, after, 0)],
        out_specs=[out, out, out],
        out_shape=[jax.ShapeDtypeStruct((dil, n, D_ATTN), BF16)] * 3,
        scratch_shapes=[], semantics=("parallel", "parallel"), vmem_mib=40,
        args=(qkv, qkv, qkv, qkv, qkv, qkv, do, do, stats, stats), comm=comm)


def _window_sums(ext, window, backward):
    rows = ext.shape[0]
    acc, span = ext, 1
    while span < window:
        acc = acc + pltpu.roll(acc, (rows - span) if backward else span, axis=0)
        span *= 2
    return acc


def _mix_gate(o_list, st_list, hug, w_pool_g, pool_scale):
    seq = hug.shape[0]
    tm = 256
    halo_blocks = tm // POOL_HALO
    d4, d16 = DILATIONS[1], DILATIONS[2]

    def body(o1_ref, o4_ref, o16_ref, l1_ref, l4_ref, l16_ref, u_ref, halo_ref, ga_ref, gp_ref, wp_ref, sc_ref,
             y_ref, mix_ref, lse_ref, pooled_ref, n4_ref, n16_ref, nl4_ref, nl16_ref):
        i = pl.program_id(0)
        _from_pattern(o4_ref, n4_ref, d4)
        _from_pattern(o16_ref, n16_ref, d16)
        _from_pattern(l4_ref, nl4_ref, d4)
        _from_pattern(l16_ref, nl16_ref, d16)
        la, lb, lc = l1_ref[...], nl4_ref[0], nl16_ref[0]
        mx = jnp.maximum(jnp.maximum(la, lb), lc)
        ea, eb, ec = jnp.exp(la - mx), jnp.exp(lb - mx), jnp.exp(lc - mx)
        tot = ea + eb + ec
        lse_ref[...] = mx + jnp.log(tot)
        wa, wb, wc = ea / tot, eb / tot, ec / tot
        ga = ga_ref[...].astype(F32)
        silu_a = ga * jax.nn.sigmoid(ga)
        for h in range(N_HEADS):
            cols = slice(h * HEAD_DIM, (h + 1) * HEAD_DIM)
            hc = slice(h, h + 1)
            attn = wa[:, hc] * o1_ref[:, cols].astype(F32) + wb[:, hc] * n4_ref[h] + wc[:, hc] * n16_ref[h]
            mix_ref[:, cols] = attn.astype(BF16)
            y_ref[:, cols] = (attn * silu_a[:, cols]).astype(BF16)

        u = u_ref[...].astype(F32)
        halo = jnp.where(i > 0, halo_ref[...].astype(F32), 0.0)
        ext = jnp.concatenate([halo, u], axis=0)
        pos = i * tm + lax.broadcasted_iota(jnp.int32, (tm, 1), 0)
        gp = gp_ref[...].astype(F32)
        gated_scale = sc_ref[...] * (gp * jax.nn.sigmoid(gp))
        for g, window in enumerate(POOL_WINDOWS):
            cols = slice(g * POOL_GROUP_DIM, (g + 1) * POOL_GROUP_DIM)
            sums = _window_sums(ext[:, cols], window, backward=False)[POOL_HALO:, :]
            count = jnp.minimum(pos + 1, window).astype(F32)
            pooled = (sums / count - u[:, cols]).astype(BF16)
            pooled_ref[:, cols] = pooled
            pre = _dot_nn(pooled, wp_ref[g])
            out_cols = slice(D_ATTN + g * POOL_GROUP_DIM, D_ATTN + (g + 1) * POOL_GROUP_DIM)
            mix_ref[:, out_cols] = pre.astype(BF16)
            y_ref[:, out_cols] = (pre * gated_scale[:, cols]).astype(BF16)

    row = lambda width, cb=0: pl.BlockSpec((tm, width), lambda i: (i, cb))
    pat = lambda d, width: pl.BlockSpec((d, tm // d, width), lambda i: (0, i, 0))
    return _pallas(
        body, name="mix_gate", grid=(seq // tm,),
        in_specs=[row(D_ATTN), pat(d4, D_ATTN), pat(d16, D_ATTN),
                  row(STAT_LANES), pat(d4, STAT_LANES), pat(d16, STAT_LANES),
                  row(D_POOL),
                  pl.BlockSpec((POOL_HALO, D_POOL), lambda i: (jnp.maximum(i * halo_blocks - 1, 0), 0)),
                  row(D_ATTN, 1), row(D_POOL, 2),
                  pl.BlockSpec((len(POOL_WINDOWS), POOL_GROUP_DIM, POOL_GROUP_DIM), lambda i: (0, 0, 0)),
                  pl.BlockSpec((1, D_POOL), lambda i: (0, 0))],
        out_specs=[row(D_MODEL), row(D_MODEL), row(STAT_LANES), row(D_POOL)],
        out_shape=[jax.ShapeDtypeStruct((seq, D_MODEL), BF16), jax.ShapeDtypeStruct((seq, D_MODEL), BF16),
                   jax.ShapeDtypeStruct((seq, STAT_LANES), F32), jax.ShapeDtypeStruct((seq, D_POOL), BF16)],
        scratch_shapes=[pltpu.VMEM((N_HEADS, tm, HEAD_DIM), F32), pltpu.VMEM((N_HEADS, tm, HEAD_DIM), F32),
                        pltpu.VMEM((1, tm, STAT_LANES), F32), pltpu.VMEM((1, tm, STAT_LANES), F32)],
        compiler_params=_params(("parallel",), 48),
    )(o_list[0][0], o_list[1], o_list[2], st_list[0][0], st_list[1], st_list[2],
      hug, hug, hug, hug, w_pool_g, pool_scale)


def _out_proj_loss(y, w_out_g, x, target, gain, bias):
    seq = x.shape[0]
    tm = 512

    def body(y_ref, w_ref, x_ref, t_ref, g_ref, b_ref, dz_ref, dzb_ref, gg_ref, gb_ref, loss_ref):
        @pl.when(pl.program_id(0) == 0)
        def _():
            gg_ref[...] = jnp.zeros_like(gg_ref)
            gb_ref[...] = jnp.zeros_like(gb_ref)
            loss_ref[...] = jnp.zeros_like(loss_ref)

        halves = [slice(0, tm // 2), slice(tm // 2, tm)]
        projected = [_dot_nn(y_ref[rows, :], w_ref[...]) for rows in halves]
        for rows, out in zip(halves, projected):
            z = DEEPNORM_ALPHA * x_ref[rows, :] + out
            mu = jnp.mean(z, axis=-1, keepdims=True)
            zc = z - mu
            rstd = lax.rsqrt(jnp.mean(zc * zc, axis=-1, keepdims=True) + LN_EPS)
            xhat = zc * rstd
            gain_v = g_ref[...]
            diff = xhat * gain_v + b_ref[...] - t_ref[rows, :]
            sq = _fold_rows(diff * diff)
            part = sq[:, :128]
            for k in range(1, D_MODEL // 128):
                part = part + sq[:, k * 128:(k + 1) * 128]
            loss_ref[...] += part
            dln = diff * (1.0 / D_MODEL)
            gg_ref[...] += _fold_rows(dln * xhat)
            gb_ref[...] += _fold_rows(dln)
            dxhat = dln * gain_v
            dz = rstd * (dxhat - jnp.mean(dxhat, axis=-1, keepdims=True)
                         - xhat * jnp.mean(dxhat * xhat, axis=-1, keepdims=True))
            dz_ref[rows, :] = dz
            dzb_ref[rows, :] = dz.astype(BF16)

    row = lambda: pl.BlockSpec((tm, D_MODEL), lambda i: (i, 0))
    vec = lambda: pl.BlockSpec((1, D_MODEL), lambda i: (0, 0))
    acc = lambda width: pl.BlockSpec((8, width), lambda i: (0, 0))
    return _pallas(
        body, name="out_proj_loss", grid=(seq // tm,),
        in_specs=[row(), pl.BlockSpec((D_MODEL, D_MODEL), lambda i: (0, 0), pipeline_mode=pl.Buffered(1)),
                  row(), row(), vec(), vec()],
        out_specs=[row(), row(), acc(D_MODEL), acc(D_MODEL), acc(128)],
        out_shape=[jax.ShapeDtypeStruct((seq, D_MODEL), F32), jax.ShapeDtypeStruct((seq, D_MODEL), BF16),
                   jax.ShapeDtypeStruct((8, D_MODEL), F32), jax.ShapeDtypeStruct((8, D_MODEL), F32),
                   jax.ShapeDtypeStruct((8, 128), F32)],
        compiler_params=_params(("arbitrary",), 56),
    )(y, w_out_g.reshape(D_MODEL, D_MODEL), x, target, gain, bias)


def _dy_gate_bwd(dzb, w_out_g, hug, mixpre, pool_scale, lse_all):
    seq = dzb.shape[0]
    tm = 256
    d4, d16 = DILATIONS[1], DILATIONS[2]

    def body(dz_ref, w_ref, ga_ref, gp_ref, mix_ref, sc_ref, lse_ref,
             dh_ref, dpo_ref, do1_ref, do4_ref, do16_ref, st1_ref, st4_ref, st16_ref, da_ref, st_ref):
        dy = _dot_nt(dz_ref[...], w_ref[...])
        ga = ga_ref[...].astype(F32)
        sig = jax.nn.sigmoid(ga)
        attn = mix_ref[:, :D_ATTN].astype(F32)
        dya = dy[:, :D_ATTN]
        dattn = dya * (ga * sig)
        dh_ref[:, :D_ATTN] = (dya * attn * (sig * (1.0 + ga * (1.0 - sig)))).astype(BF16)
        _store_slabs(da_ref, dattn)
        lane = lax.broadcasted_iota(jnp.int32, (tm, STAT_LANES), 1)
        stats = lse_ref[...]
        prod = dattn * attn
        for h in range(N_HEADS):
            delta = jnp.sum(prod[:, h * HEAD_DIM:(h + 1) * HEAD_DIM], axis=-1, keepdims=True)
            stats = jnp.where(lane == N_HEADS + h, delta, stats)
        st_ref[0] = stats
        do1_ref[...] = dattn.astype(BF16)
        st1_ref[...] = stats
        _to_pattern(da_ref, do4_ref, d4, BF16)
        _to_pattern(da_ref, do16_ref, d16, BF16)
        _to_pattern(st_ref, st4_ref, d4, F32)
        _to_pattern(st_ref, st16_ref, d16, F32)

        gp = gp_ref[...].astype(F32)
        sig = jax.nn.sigmoid(gp)
        dyp = dy[:, D_ATTN:]
        dpo_ref[...] = (dyp * (gp * sig)).astype(BF16)
        dh_ref[:, D_ATTN:] = (dyp * (mix_ref[:, D_ATTN:].astype(F32) * sc_ref[...])
                              * (sig * (1.0 + gp * (1.0 - sig)))).astype(BF16)

    row = lambda width, cb=0: pl.BlockSpec((tm, width), lambda i: (i, cb))
    pat = lambda d, width: pl.BlockSpec((d, tm // d, width), lambda i: (0, i, 0))
    pat_shape = lambda d, width, dtype: jax.ShapeDtypeStruct((d, seq // d, width), dtype)
    outs = _pallas(
        body, name="dy_gate_bwd", grid=(seq // tm,),
        in_specs=[row(D_MODEL), pl.BlockSpec((D_MODEL, D_MODEL), lambda i: (0, 0)),
                  row(D_ATTN, 1), row(D_POOL, 2), row(D_MODEL), pl.BlockSpec((1, D_POOL), lambda i: (0, 0)),
                  row(STAT_LANES)],
        out_specs=[row(D_MODEL, D_IN // D_MODEL - 1), row(D_POOL),
                   row(D_ATTN), pat(d4, D_ATTN), pat(d16, D_ATTN),
                   row(STAT_LANES), pat(d4, STAT_LANES), pat(d16, STAT_LANES)],
        out_shape=[jax.ShapeDtypeStruct((seq, D_IN), BF16), jax.ShapeDtypeStruct((seq, D_POOL), BF16),
                   jax.ShapeDtypeStruct((seq, D_ATTN), BF16), pat_shape(d4, D_ATTN, BF16), pat_shape(d16, D_ATTN, BF16),
                   jax.ShapeDtypeStruct((seq, STAT_LANES), F32), pat_shape(d4, STAT_LANES, F32),
                   pat_shape(d16, STAT_LANES, F32)],
        scratch_shapes=[pltpu.VMEM((N_HEADS, tm, HEAD_DIM), F32), pltpu.VMEM((1, tm, STAT_LANES), F32)],
        compiler_params=_params(("parallel",), 48),
    )(dzb, w_out_g.reshape(D_MODEL, D_MODEL), hug, hug, mixpre, pool_scale, lse_all)
    dh, dpo, do1, do4, do16, st1, st4, st16 = outs
    return dh, dpo, [do1[None], do4, do16], [st1[None], st4, st16]


def _pool_bwd(dh, dpo, mixpre, pooled, w_pool_g, pool_scale):
    seq = dpo.shape[0]
    tm = 256
    halo_blocks = tm // POOL_HALO
    last = seq // tm - 1
    n_groups = len(POOL_WINDOWS)

    def body(dh_in_ref, dpo_ref, halo_ref, pre_ref, pooled_ref, wp_ref, sc_ref, du_ref, gw_ref, gs_ref):
        i = pl.program_id(0)

        @pl.when(i == 0)
        def _():
            gw_ref[...] = jnp.zeros_like(gw_ref)
            gs_ref[...] = jnp.zeros_like(gs_ref)

        dpo = dpo_ref[...].astype(F32)
        scale = sc_ref[...]
        gs_ref[...] += _fold_rows(dpo * pre_ref[...].astype(F32))
        halo = jnp.where(i < last, halo_ref[...].astype(F32), 0.0)
        dpw = (jnp.concatenate([dpo, halo], axis=0) * scale).astype(BF16)
        pos = i * tm + lax.broadcasted_iota(jnp.int32, (tm + POOL_HALO, 1), 0)
        for g, window in enumerate(POOL_WINDOWS):
            cols = slice(g * POOL_GROUP_DIM, (g + 1) * POOL_GROUP_DIM)
            dpw_g = dpw[:, cols]
            gw_ref[g] += _dot_tn(pooled_ref[:, cols], dpw_g[:tm, :])
            dpooled = _dot_nt(dpw_g, wp_ref[g])
            count = jnp.minimum(pos + 1, window).astype(F32)
            sums = _window_sums(dpooled / count, window, backward=True)
            du_ref[:, cols] = (sums[:tm, :] - dpooled[:tm, :]).astype(BF16)

    row = lambda width, cb=0: pl.BlockSpec((tm, width), lambda i: (i, cb))
    return _pallas(
        body, name="pool_bwd", grid=(seq // tm,),
        in_specs=[ANY, row(D_POOL),
                  pl.BlockSpec((POOL_HALO, D_POOL),
                               lambda i: (jnp.minimum((i + 1) * halo_blocks, seq // POOL_HALO - 1), 0)),
                  row(D_POOL, 1), row(D_POOL),
                  pl.BlockSpec((n_groups, POOL_GROUP_DIM, POOL_GROUP_DIM), lambda i: (0, 0, 0)),
                  pl.BlockSpec((1, D_POOL), lambda i: (0, 0))],
        out_specs=[row(D_POOL, D_QKV // D_POOL),
                   pl.BlockSpec((n_groups, POOL_GROUP_DIM, POOL_GROUP_DIM), lambda i: (0, 0, 0)),
                   pl.BlockSpec((8, D_POOL), lambda i: (0, 0))],
        out_shape=[jax.ShapeDtypeStruct(dh.shape, dh.dtype),
                   jax.ShapeDtypeStruct((n_groups, POOL_GROUP_DIM, POOL_GROUP_DIM), F32),
                   jax.ShapeDtypeStruct((8, D_POOL), F32)],
        input_output_aliases={0: 0},
        compiler_params=_params(("arbitrary",), 40),
    )(dh, dpo, dpo, mixpre, pooled, w_pool_g, pool_scale)


def _sum_patterns(dh, parts, tabs, unrotate, col_block, name, comm=None):
    seq = dh.shape[0]
    tm, tn = 256, D_ATTN
    per = D_ATTN // tn
    d4, d16 = DILATIONS[1], DILATIONS[2]

    def body(dh_in_ref, a1_ref, a4_ref, a16_ref, ct_ref, up_ref, down_ref, o_ref, n4_ref, n16_ref):
        _from_pattern(a4_ref, n4_ref, d4)
        _from_pattern(a16_ref, n16_ref, d16)
        for s in range(tn // HEAD_DIM):
            cols = slice(s * HEAD_DIM, (s + 1) * HEAD_DIM)
            tot = a1_ref[:, cols].astype(F32) + n4_ref[s] + n16_ref[s]
            if unrotate:
                tot = _rotate_heads(tot, ct_ref[...], -up_ref[...], -down_ref[...])
            o_ref[:, cols] = tot.astype(BF16)

    tab = pl.BlockSpec((tm, HEAD_DIM), lambda i, j: (i, 0))
    pat = lambda d: pl.BlockSpec((d, tm // d, tn), lambda i, j: (0, i, j))
    (dh,), exchanged = _call(
        body, name=name, grid=(seq // tm, per),
        in_specs=[ANY, pl.BlockSpec((tm, tn), lambda i, j: (i, j)), pat(d4), pat(d16), tab, tab, tab],
        out_specs=[pl.BlockSpec((tm, tn), lambda i, j: (i, col_block * per + j))],
        out_shape=[jax.ShapeDtypeStruct(dh.shape, dh.dtype)],
        scratch_shapes=[pltpu.VMEM((tn // HEAD_DIM, tm, HEAD_DIM), F32), pltpu.VMEM((tn // HEAD_DIM, tm, HEAD_DIM), F32)],
        semantics=("parallel", "parallel"), vmem_mib=32, args=(dh, parts[0][0], parts[1], parts[2], *tabs),
        aliases={0: 0}, comm=comm)
    return dh, exchanged


def _grad_w_in(x, dh, half, name, comm=None):
    seq = x.shape[0]
    ts, td, te = 2048, D_MODEL // 2, SHARD_IN

    def body(half_ref, x_ref, dh_ref, o_ref):
        k = pl.program_id(1)
        part = _dot_tn(x_ref[...].astype(BF16), dh_ref[...])

        @pl.when(k == 0)
        def _():
            o_ref[...] = part

        @pl.when(k > 0)
        def _():
            o_ref[...] += part

    (g,), exchanged = _call(
        body, name=name, grid=(N_SHARDS, seq // ts),
        in_specs=[pl.BlockSpec((ts, td), lambda e, k, half_ref: (k, half_ref[0])),
                  pl.BlockSpec((ts, te), lambda e, k, half_ref: (k, e))],
        out_specs=[pl.BlockSpec((None, td, te), lambda e, k, half_ref: (e, 0, 0))],
        out_shape=[jax.ShapeDtypeStruct((N_SHARDS, td, te), F32)],
        scratch_shapes=[], semantics=("parallel", "arbitrary"), vmem_mib=56, args=(x, dh), comm=comm,
        prefetch=(half,))
    return g, exchanged


def _grad_w_out(y, dzb):
    seq = y.shape[0]
    ts, te = 512, 1024
    nk = seq // ts

    def body(y_ref, dz_ref, o_ref, acc_ref):
        k = pl.program_id(1)

        @pl.when(k == 0)
        def _():
            acc_ref[...] = jnp.zeros_like(acc_ref)

        acc_ref[...] += _dot_tn(y_ref[...], dz_ref[...])

        @pl.when(k == nk - 1)
        def _():
            o_ref[...] = acc_ref[...]

    return _pallas(
        body, name="grad_w_out", grid=(D_MODEL // te, nk),
        in_specs=[pl.BlockSpec((ts, te), lambda e, k: (k, e)), pl.BlockSpec((ts, D_MODEL), lambda e, k: (k, 0))],
        out_specs=pl.BlockSpec((te, D_MODEL), lambda e, k: (e, 0)),
        out_shape=jax.ShapeDtypeStruct((D_MODEL, D_MODEL), F32),
        scratch_shapes=[pltpu.VMEM((te, D_MODEL), F32)],
        compiler_params=_params(("parallel", "arbitrary"), 48),
    )(y, dzb)


GRAD_X_LATE_SHARDS = 1


def _grad_x_partial(dh, w_in_g, first, tiles, prev=None, comm=None):
    seq = dh.shape[0]
    tm, tk = 512, SHARD_IN

    def body(*refs):
        dh_ref, w_ref, o_ref = refs[-3:]
        k = pl.program_id(1)
        part = _dot_nt(dh_ref[...], w_ref[...])

        @pl.when(k == 0)
        def _():
            o_ref[...] = part

        @pl.when(k > 0)
        def _():
            o_ref[...] += part

    carried = [] if prev is None else [prev]
    (partial,), exchanged = _call(
        body, name="grad_x_partial_%d" % first, grid=(tiles, N_SHARDS - GRAD_X_LATE_SHARDS),
        in_specs=[ANY] * len(carried) + [
            pl.BlockSpec((tm, tk), lambda i, k: (i + first, k)),
            pl.BlockSpec((None, D_MODEL, tk), lambda i, k: (k, 0, 0))],
        out_specs=[pl.BlockSpec((tm, D_MODEL), lambda i, k: (i + first, 0))],
        out_shape=[jax.ShapeDtypeStruct((seq, D_MODEL), F32)],
        scratch_shapes=[], semantics=("parallel", "arbitrary"), vmem_mib=48, args=(*carried, dh, w_in_g),
        aliases={0: 0} if carried else None, comm=comm)
    return partial, exchanged


def _grad_x_final(dh, w_in_g, dz, partial):
    seq = dh.shape[0]
    tm, tk = 512, SHARD_IN
    k0 = N_SHARDS - GRAD_X_LATE_SHARDS

    def body(dh_ref, w_ref, dz_ref, p_ref, o_ref):
        k = pl.program_id(1)
        part = _dot_nt(dh_ref[...], w_ref[...])

        @pl.when(k == 0)
        def _():
            o_ref[...] = (DEEPNORM_ALPHA * dz_ref[...] + p_ref[...]) + part

        @pl.when(k > 0)
        def _():
            o_ref[...] += part

    row = pl.BlockSpec((tm, D_MODEL), lambda i, k: (i, 0))
    return _pallas(
        body, name="grad_x_final", grid=(seq // tm, GRAD_X_LATE_SHARDS),
        in_specs=[pl.BlockSpec((tm, tk), lambda i, k: (i, k + k0)),
                  pl.BlockSpec((None, D_MODEL, tk), lambda i, k: (k + k0, 0, 0)), row, row],
        out_specs=row, out_shape=jax.ShapeDtypeStruct((seq, D_MODEL), F32),
        compiler_params=_params(("parallel", "arbitrary"), 48),
    )(dh, w_in_g, dz, partial)


def _pool_weight(w_pool_sh):
    n_groups = len(POOL_WINDOWS)
    shard_c = POOL_GROUP_DIM // N_SHARDS
    return (w_pool_sh.reshape(N_SHARDS, n_groups, shard_c, POOL_GROUP_DIM).transpose(1, 0, 2, 3)
            .reshape(n_groups, POOL_GROUP_DIM, POOL_GROUP_DIM))


def _pool_grad_pieces(g_w_pool):
    n_groups = len(POOL_WINDOWS)
    half_c = POOL_GROUP_DIM // N_SHARDS // 2
    return (g_w_pool.reshape(n_groups, N_SHARDS, 2, half_c, POOL_GROUP_DIM).transpose(1, 2, 0, 3, 4)
            .reshape(N_SHARDS, 2, n_groups * half_c, POOL_GROUP_DIM))


def _step(x, target, w_in_g, w_rest, pool_scale, gain, bias, place=None):
    seq = x.shape[0]
    tabs = _rope_tables(seq)
    qkv, gathered = _in_proj_qkv(x, w_in_g, tabs, comm=_allgather_weights(w_rest, "neighbours") if place else None)
    hug, gathered = _in_proj_pool_gate(x, w_in_g, _allgather_weights(gathered, "diagonal") if place else None)
    w_out_g, w_pool_sh = gathered if place else w_rest
    w_pool_g = _pool_weight(w_pool_sh)
    o_list, st_list = [], []
    for p, dil in enumerate(DILATIONS):
        o, st = _attn_fwd(qkv[p], "attn_fwd_d%d" % dil)
        o_list.append(o)
        st_list.append(st)
    y, mixpre, lse_all, pooled = _mix_gate(o_list, st_list, hug, w_pool_g, pool_scale)
    dz, dzb, gain_part, bias_part, loss_part = _out_proj_loss(y, w_out_g, x, target, gain, bias)
    dh, dpo, do_list, stat_list = _dy_gate_bwd(dzb, w_out_g, hug, mixpre, pool_scale, lse_all)
    g_w_out = _grad_w_out(y, dzb)
    dh, g_w_pool, scale_part = _pool_bwd(dh, dpo, mixpre, pooled, w_pool_g, pool_scale)
    small = jnp.concatenate([scale_part, gain_part, bias_part, loss_part], axis=1)
    early = [g_w_out.reshape(N_SHARDS, 2, D_MODEL // (2 * N_SHARDS), D_MODEL), _pool_grad_pieces(g_w_pool)]

    bwd = lambda p, comm: _attn_bwd(qkv[p], do_list[p], stat_list[p], "attn_bwd_d%d" % DILATIONS[p], comm)
    if place is None:
        parts = [bwd(p, None)[0] for p in range(3)]
    else:
        core, chip_core, onward = place
        part_a, recv = bwd(0, _exchange_halves(early))
        sums = [_add_own_half(g, r, core, "add_own_half_%d" % a) for a, (g, r) in enumerate(zip(early, recv))]
        part_b, recv = bwd(1, _scatter_to_chips([s[1] for s in sums]))
        bufs = [_add_chips(s[0], r, chip_core, "add_chips_%d" % a) for a, (s, r) in enumerate(zip(sums, recv))]
        part_c, early = bwd(2, _share_with_sibling(bufs))
        parts = [part_a, part_b, part_c]
    dh, gathered = _sum_patterns(dh, [t[0] for t in parts], tabs, True, 0, "sum_dq",
                                 _gather_small(small) if place else None)
    dh, _ = _sum_patterns(dh, [t[1] for t in parts], tabs, True, 1, "sum_dk")
    dh, _ = _sum_patterns(dh, [t[2] for t in parts], tabs, False, 2, "sum_dv")
    if place:
        small = (small, gathered[0])
    if place is None:
        halves = [_grad_w_in(x, dh, jnp.full((1,), h, jnp.int32), "grad_w_in_%d" % h)[0] for h in range(2)]
        g_w_in = jnp.stack(halves, axis=1)
        g_x = _grad_x_final(dh, w_in_g, dz, _grad_x_partial(dh, w_in_g, 0, seq // 512)[0])
    else:
        give, _ = _grad_w_in(x, dh, 1 - core, "grad_w_in_give")
        keep, recv = _grad_w_in(x, dh, core, "grad_w_in_keep", _send_to_sibling([give]))
        total, total_b = _add_pair(keep, recv[0], "add_own_half_w_in")
        n_tiles = seq // 512
        tiles = 3 * n_tiles // 8
        part, relayed = _grad_x_partial(dh, w_in_g, 0, tiles, None, _relay_diagonal(total_b))
        total_b = _fold_relayed(total, total_b, relayed[0], onward)
        part, recv = _grad_x_partial(dh, w_in_g, tiles, n_tiles - tiles, part, _scatter_to_neighbours(total_b))
        buf = _add_chips(total, recv[0], chip_core, "add_chips_w_in")
        g_x = _grad_x_final(dh, w_in_g, dz, part)
        g_w_in = _run_exchange(_share_with_sibling([buf]), "share_w_in")[0]
    return g_x, g_w_in, early[0], early[1], small


def _exchange_halves(grads):
    n = len(grads)

    def copies(src, dst, sems):
        x, y, c, _ = _mesh_place()
        return [_remote(src[a].at[j, 1 - c], dst[a].at[j], sems[0].at[a, j], sems[1].at[a, j], (x, y, 1 - c))
                for a in range(n) for j in range(N_SHARDS)]

    def start(src, dst, sems):
        for cp in copies(src, dst, sems):
            cp.start()

    def finish(src, dst, sems):
        for cp in copies(src, dst, sems):
            cp.wait()

    return _Exchange(grads, [jax.ShapeDtypeStruct((N_SHARDS,) + g.shape[2:], g.dtype) for g in grads], {},
                     [pltpu.SemaphoreType.DMA((n, N_SHARDS))] * 2, start, finish)


def _add_own_half(grad, recv, core, name):
    _, _, r, c = grad.shape
    tr = min(r, 256)

    def body(core_ref, g_ref, r_ref, o_ref, ob_ref):
        tot = g_ref[...] + r_ref[...]
        o_ref[...] = tot
        ob_ref[...] = tot.astype(BF16)

    out = pl.BlockSpec((None, tr, c), lambda j, i, core_ref: (j, i, 0))
    return _pallas(
        body, name=name,
        grid_spec=pltpu.PrefetchScalarGridSpec(
            num_scalar_prefetch=1, grid=(N_SHARDS, r // tr),
            in_specs=[pl.BlockSpec((None, None, tr, c), lambda j, i, core_ref: (j, core_ref[0], i, 0)),
                      pl.BlockSpec((None, tr, c), lambda j, i, core_ref: (j, i, 0))],
            out_specs=[out, out]),
        out_shape=[jax.ShapeDtypeStruct((N_SHARDS, r, c), F32), jax.ShapeDtypeStruct((N_SHARDS, r, c), BF16)],
        compiler_params=_params(("parallel", "parallel"), 32),
    )(core, grad, recv)


def _send_to_sibling(arrays):
    n = len(arrays)

    def copies(src, dst, sems):
        x, y, c, _ = _mesh_place()
        return [_remote(src[a], dst[a], sems[0].at[a], sems[1].at[a], (x, y, 1 - c)) for a in range(n)]

    def start(src, dst, sems):
        for cp in copies(src, dst, sems):
            cp.start()

    def finish(src, dst, sems):
        for cp in copies(src, dst, sems):
            cp.wait()

    return _Exchange(arrays, [jax.ShapeDtypeStruct(t.shape, t.dtype) for t in arrays], {},
                     [pltpu.SemaphoreType.DMA((n,))] * 2, start, finish)


def _add_pair(a, b, name):
    _, r, c = a.shape
    tr = min(r, 256)

    def body(a_ref, b_ref, o_ref, ob_ref):
        tot = a_ref[...] + b_ref[...]
        o_ref[...] = tot
        ob_ref[...] = tot.astype(BF16)

    spec = pl.BlockSpec((None, tr, c), lambda j, i: (j, i, 0))
    return _pallas(
        body, name=name, grid=(N_SHARDS, r // tr), in_specs=[spec, spec], out_specs=[spec, spec],
        out_shape=[jax.ShapeDtypeStruct(a.shape, F32), jax.ShapeDtypeStruct(a.shape, BF16)],
        compiler_params=_params(("parallel", "parallel"), 32),
    )(a, b)


def _scatter_to_chips(sums, rows=None, into=None):
    n = len(sums)

    def copies(src, dst, sems):
        x, y, c, chips = _mesh_place()
        part = (lambda ref: ref) if rows is None else (lambda ref: ref.at[pl.ds(rows[0], rows[1])])
        return [_remote(part(src[a].at[2 * cx + cy]), part(dst[a].at[k]), sems[0].at[a, k], sems[1].at[a, k],
                        (cx, cy, c))
                for a in range(n) for k, (cx, cy) in enumerate(chips)]

    def start(src, dst, sems):
        for cp in copies(src, dst, sems):
            cp.start()

    def finish(src, dst, sems):
        for cp in copies(src, dst, sems):
            cp.wait()

    return _Exchange(sums + (into or []), [jax.ShapeDtypeStruct((3,) + s.shape[1:], s.dtype) for s in sums],
                     {n + a: a for a in range(n)} if into else {},
                     [pltpu.SemaphoreType.DMA((n, 3))] * 2, start, finish)


def _add_chips(sums, recv, chip_core, name):
    _, r, c = sums.shape
    n_recv = recv.shape[0]
    tr = min(r, 256)

    def body(cc_ref, s_ref, r_ref, o_ref):
        tot = s_ref[...]
        for k in range(n_recv):
            tot = tot + r_ref[k].astype(F32)
        o_ref[...] = tot

    return _pallas(
        body, name=name,
        grid_spec=pltpu.PrefetchScalarGridSpec(
            num_scalar_prefetch=1, grid=(r // tr,),
            in_specs=[pl.BlockSpec((None, tr, c), lambda i, cc_ref: (cc_ref[0], i, 0)),
                      pl.BlockSpec((n_recv, tr, c), lambda i, cc_ref: (0, i, 0))],
            out_specs=pl.BlockSpec((None, tr, c), lambda i, cc_ref: (cc_ref[1], i, 0))),
        out_shape=jax.ShapeDtypeStruct((2, r, c), F32),
        compiler_params=_params(("parallel",), 32),
    )(chip_core, sums, recv)


def _relay_diagonal(sums_b):
    def copy(src, dst, sems):
        x, y, c, _ = _mesh_place()
        diagonal = 2 * (1 - x) + (1 - y)
        return _remote(src[0].at[diagonal], dst[0], sems[0].at[0], sems[1].at[0], (x ^ (1 - c), y ^ c, c))

    def start(src, dst, sems):
        copy(src, dst, sems).start()

    def finish(src, dst, sems):
        copy(src, dst, sems).wait()

    return _Exchange([sums_b], [jax.ShapeDtypeStruct(sums_b.shape[1:], sums_b.dtype)], {},
                     [pltpu.SemaphoreType.DMA((1,))] * 2, start, finish)


def _fold_relayed(sums, sums_b, relayed, onward):
    _, r, c = sums.shape
    tr = min(r, 256)

    def body(on_ref, b_in_ref, s_ref, r_ref, o_ref):
        o_ref[...] = (s_ref[...] + r_ref[...].astype(F32)).astype(BF16)

    return _pallas(
        body, name="fold_relayed",
        grid_spec=pltpu.PrefetchScalarGridSpec(
            num_scalar_prefetch=1, grid=(r // tr,),
            in_specs=[ANY, pl.BlockSpec((None, tr, c), lambda i, on_ref: (on_ref[0], i, 0)),
                      pl.BlockSpec((tr, c), lambda i, on_ref: (i, 0))],
            out_specs=pl.BlockSpec((None, tr, c), lambda i, on_ref: (on_ref[0], i, 0))),
        out_shape=jax.ShapeDtypeStruct(sums_b.shape, sums_b.dtype),
        input_output_aliases={1: 0},
        compiler_params=_params(("parallel",), 32),
    )(onward, sums_b, sums, relayed)


def _scatter_to_neighbours(sums_b):
    def copies(src, dst, sems):
        x, y, c, chips = _mesh_place()
        return [_remote(src[0].at[2 * cx + cy], dst[0].at[k], sems[0].at[k], sems[1].at[k], (cx, cy, c))
                for k, (cx, cy) in enumerate(chips[:2])]

    def start(src, dst, sems):
        for cp in copies(src, dst, sems):
            cp.start()

    def finish(src, dst, sems):
        for cp in copies(src, dst, sems):
            cp.wait()

    return _Exchange([sums_b], [jax.ShapeDtypeStruct((2,) + sums_b.shape[1:], sums_b.dtype)], {},
                     [pltpu.SemaphoreType.DMA((2,))] * 2, start, finish)


def _share_with_sibling(bufs):
    n = len(bufs)

    def copies(dst, sems, half):
        x, y, c, _ = _mesh_place()
        h = c if half == "mine" else 1 - c
        return [_remote(dst[a].at[h], dst[a].at[h], sems[0].at[a], sems[1].at[a], (x, y, 1 - c)) for a in range(n)]

    def start(ins, dst, sems):
        for cp in copies(dst, sems, "mine"):
            cp.start()

    def finish(ins, dst, sems):
        for cp in copies(dst, sems, "theirs"):
            cp.wait_recv()
        for cp in copies(dst, sems, "mine"):
            cp.wait_send()

    return _Exchange(bufs, [jax.ShapeDtypeStruct(b.shape, b.dtype) for b in bufs], {a: a for a in range(n)},
                     [pltpu.SemaphoreType.DMA((n,))] * 2, start, finish)


def _adam_math(w, g, m, v):
    m = ADAM_B1 * m + (1.0 - ADAM_B1) * g
    v = ADAM_B2 * v + (1.0 - ADAM_B2) * (g * g)
    m_hat = m / (1.0 - ADAM_B1 ** ADAM_STEP)
    v_hat = v / (1.0 - ADAM_B2 ** ADAM_STEP)
    delta = -ADAM_LR * (m_hat / (jnp.sqrt(v_hat) + ADAM_EPS) + ADAM_WD * w)
    return delta, m, v


def _gather_small(small):
    def peers():
        x, y, c, _ = _mesh_place()
        return [(x ^ ((r >> 2) & 1), y ^ ((r >> 1) & 1), c ^ (r & 1)) for r in range(1, 8)], 4 * x + 2 * y + c

    def start(src, dst, sems):
        to, me = peers()
        for r, peer in enumerate(to):
            _remote(src[0], dst[0].at[me], sems[0].at[r], sems[1].at[r], peer).start()

    def finish(src, dst, sems):
        to, me = peers()
        for r, (px, py, pc) in enumerate(to):
            theirs = dst[0].at[4 * px + 2 * py + pc]
            _remote(theirs, theirs, sems[0].at[r], sems[1].at[r], (px, py, pc)).wait_recv()
        for r, peer in enumerate(to):
            _remote(src[0], dst[0].at[me], sems[0].at[r], sems[1].at[r], peer).wait_send()

    return _Exchange([small], [jax.ShapeDtypeStruct((8,) + small.shape, small.dtype)], {},
                     [pltpu.SemaphoreType.DMA((7,))] * 2, start, finish)


def _small_adamw(gathered, small, me, w_vec, m_vec, v_vec):
    n_par = w_vec.shape[1]

    def body(me_ref, a_ref, s_ref, w_ref, m_ref, v_ref, loss_ref, g_ref, d_ref, nm_ref, nv_ref):
        mine = s_ref[...]
        tot = jnp.where(me_ref[0] == 0, mine, a_ref[0])
        for d in range(1, 8):
            tot = tot + jnp.where(me_ref[0] == d, mine, a_ref[d])
        tot = jnp.sum(tot, axis=0, keepdims=True)
        sq = jnp.sum(tot[:, n_par:], axis=1, keepdims=True)
        loss_ref[...] = jnp.broadcast_to(sq * (0.5 / D_MODEL), loss_ref.shape)
        g = tot[:, :n_par]
        g_ref[...] = g
        d_ref[...], nm_ref[...], nv_ref[...] = _adam_math(w_ref[...], g, m_ref[...], v_ref[...])

    vm = pl.BlockSpec(memory_space=pltpu.VMEM)
    vec = jax.ShapeDtypeStruct((1, n_par), F32)
    return pl.pallas_call(
        body, name="small_adamw",
        grid_spec=pltpu.PrefetchScalarGridSpec(num_scalar_prefetch=1, grid=(), in_specs=[vm] * 5, out_specs=[vm] * 5),
        out_shape=[jax.ShapeDtypeStruct((1, 128), F32), vec, vec, vec, vec],
    )(me, gathered, small, w_vec, m_vec, v_vec)


def _adamw(w, g, m, v, name):
    r, c = w.shape
    tr = min(r, 256)

    def body(w_ref, g_ref, m_ref, v_ref, d_ref, nm_ref, nv_ref):
        d_ref[...], nm_ref[...], nv_ref[...] = _adam_math(w_ref[...], g_ref[...], m_ref[...], v_ref[...])

    spec = pl.BlockSpec((tr, c), lambda i: (i, 0))
    shape = jax.ShapeDtypeStruct((r, c), F32)
    return _pallas(
        body, name=name, grid=(r // tr,),
        in_specs=[spec] * 4, out_specs=[spec] * 3, out_shape=[shape] * 3,
        compiler_params=_params(("parallel",), 48),
    )(w, g, m, v)


def kernel(x, w_in, w_pool, pool_scale, w_out, ln_gain, ln_bias, loss_target, m_w_in, m_w_pool, m_pool_scale, m_w_out, m_ln_gain, m_ln_bias, v_w_in, v_w_pool, v_pool_scale, v_w_out, v_ln_gain, v_ln_bias):
    xi, yi, ci = lax.axis_index("x"), lax.axis_index("y"), lax.axis_index("c")
    chip = (2 * xi + yi).astype(jnp.int32).reshape(1)
    core = ci.astype(jnp.int32).reshape(1)
    n_groups = len(POOL_WINDOWS)
    shard_c = w_pool.shape[2]

    w_in_b = _cast_bf16(w_in[0], chip, "cast_w_in", 256)
    w_out_b = _cast_bf16(w_out[0], chip, "cast_w_out", 256)
    w_pool_b = _cast_bf16(w_pool[0].reshape(n_groups * shard_c, POOL_GROUP_DIM), chip, "cast_w_pool", 256)
    w_in_g = _run_exchange(_allgather_weights([w_in_b], chunks=4), "allgather_w_in")[0]

    chip_core = jnp.concatenate([chip, core])
    onward = (2 * (xi ^ ci) + (yi ^ (1 - ci))).astype(jnp.int32).reshape(1)
    g_x, full_in, full_out, full_pool, small = _step(
        x[0], loss_target[0], w_in_g, [w_out_b, w_pool_b], pool_scale, ln_gain, ln_bias, (core, chip_core, onward))
    half_c = shard_c // 2
    grad_w_in = full_in.reshape(D_MODEL, SHARD_IN)
    grad_w_out = full_out.reshape(D_MODEL // N_SHARDS, D_MODEL)
    grad_w_pool = (full_pool.reshape(2, n_groups, half_c, POOL_GROUP_DIM).transpose(1, 0, 2, 3)
                   .reshape(n_groups * shard_c, POOL_GROUP_DIM))

    d_in, nm_in, nv_in = _adamw(w_in[0], grad_w_in, m_w_in[0], v_w_in[0], "adamw_w_in")
    d_out, nm_out, nv_out = _adamw(w_out[0], grad_w_out, m_w_out[0], v_w_out[0], "adamw_w_out")
    flat = lambda t: t[0].reshape(n_groups * shard_c, POOL_GROUP_DIM)
    d_pool, nm_pool, nv_pool = _adamw(flat(w_pool), grad_w_pool, flat(m_w_pool), flat(v_w_pool), "adamw_w_pool")

    cat = lambda a, b, c: jnp.concatenate([a, b, c], axis=1)
    me = (4 * xi + 2 * yi + ci).astype(jnp.int32).reshape(1)
    loss_v, g_vec, d_vec, nm_vec, nv_vec = _small_adamw(
        small[1], small[0], me, cat(pool_scale, ln_gain, ln_bias), cat(m_pool_scale, m_ln_gain, m_ln_bias),
        cat(v_pool_scale, v_ln_gain, v_ln_bias))

    def split(vec):
        return vec[:, :D_POOL], vec[:, D_POOL:D_POOL + D_MODEL], vec[:, D_POOL + D_MODEL:]

    g_scale, g_gain, g_bias = split(g_vec)
    d_scale, d_gain, d_bias = split(d_vec)
    nm_scale, nm_gain, nm_bias = split(nm_vec)
    nv_scale, nv_gain, nv_bias = split(nv_vec)
    pool_shape = w_pool.shape
    return (loss_v[0, 0], g_x[None],
            grad_w_in[None], grad_w_pool.reshape(pool_shape), g_scale, grad_w_out[None], g_gain, g_bias,
            d_in[None], d_pool.reshape(pool_shape), d_scale, d_out[None], d_gain, d_bias,
            nm_in[None], nm_pool.reshape(pool_shape), nm_scale, nm_out[None], nm_gain, nm_bias,
            nv_in[None], nv_pool.reshape(pool_shape), nv_scale, nv_out[None], nv_gain, nv_bias)
```

```python
import functools

import jax
import jax.numpy as jnp
from jax import lax
from jax.experimental import pallas as pl
from jax.experimental.pallas import tpu as pltpu

F32 = jnp.float32
BF16 = jnp.bfloat16
MESH = pl.DeviceIdType.MESH
ANY = pl.BlockSpec(memory_space=pl.ANY)

D_MODEL = 2048
D_ATTN = 1024
D_POOL = 1024
HEAD_DIM = 128
N_HEADS = 8
ROPE_DIM = 32
ROPE_THETA = 500000.0
DILATIONS = (1, 4, 16)
KEY_BLOCK = 128
CHUNK = 2 * KEY_BLOCK
STAT_LANES = 128
POOL_WINDOWS = (2, 4, 8, 16)
POOL_GROUP_DIM = 256
POOL_HALO = 16
D_QKV = 3 * D_ATTN
D_UG = D_POOL + D_MODEL
D_IN = D_QKV + D_UG
N_SHARDS = 4
SHARD_IN = D_IN // N_SHARDS
LN_EPS = 1e-5
DEEPNORM_ALPHA = 2.0 ** 0.25
ADAM_LR = 0.001
ADAM_B1 = 0.9
ADAM_B2 = 0.999
ADAM_EPS = 1e-08
ADAM_WD = 0.01
ADAM_STEP = 10
NEG = -1e30
MIB = 1024 * 1024


def _params(sem, vmem_mib):
    return pltpu.CompilerParams(dimension_semantics=sem, vmem_limit_bytes=vmem_mib * MIB)


def _pallas(body, **kwargs):
    pin = lambda s: pltpu.HBM(s.shape, s.dtype) if len(s.shape) >= 2 else s
    out_shape = kwargs.pop("out_shape")
    out_shape = [pin(s) for s in out_shape] if isinstance(out_shape, (list, tuple)) else pin(out_shape)
    call = pl.pallas_call(body, out_shape=out_shape, **kwargs)

    def run(*operands):
        return call(*[pltpu.with_memory_space_constraint(o, pltpu.HBM) if o.ndim >= 2 else o for o in operands])

    return run


class _Exchange:
    def __init__(self, operands, out_shape, aliases, sems, start, finish):
        self.operands, self.out_shape, self.aliases, self.sems = list(operands), list(out_shape), dict(aliases), list(sems)
        self.start, self.finish = start, finish


def _run_exchange(comm, name):
    n_in, n_out = len(comm.operands), len(comm.out_shape)

    def body(*refs):
        ins, outs, sems = refs[:n_in], refs[n_in:n_in + n_out], refs[n_in + n_out:]
        comm.start(ins, outs, sems)
        comm.finish(ins, outs, sems)

    return _pallas(
        body, name=name, in_specs=[ANY] * n_in, out_specs=[ANY] * n_out, out_shape=comm.out_shape,
        input_output_aliases=comm.aliases, scratch_shapes=comm.sems,
    )(*comm.operands)


def _call(body, *, name, grid, in_specs, out_specs, out_shape, scratch_shapes, semantics, vmem_mib, args,
          aliases=None, comm=None, prefetch=()):
    aliases = dict(aliases or {})
    n_pre, n_in, n_out, n_scr = len(prefetch), len(in_specs), len(out_specs), len(scratch_shapes)
    c_in, c_out = (len(comm.operands), len(comm.out_shape)) if comm else (0, 0)
    c_shapes, c_sems, c_operands = (comm.out_shape, comm.sems, comm.operands) if comm else ([], [], [])

    def hosted(*refs):
        pre, refs = refs[:n_pre], refs[n_pre:]
        a = n_in
        b = a + c_in
        c = b + n_out
        d = c + c_out
        e = d + n_scr
        if comm is None:
            body(*pre, *refs)
            return
        ids = [pl.program_id(k) for k in range(len(grid))]
        first = functools.reduce(jnp.logical_and, [i == 0 for i in ids])
        last = functools.reduce(jnp.logical_and, [i == g - 1 for i, g in zip(ids, grid)])

        @pl.when(first)
        def _():
            comm.start(refs[a:b], refs[c:d], refs[e:])

        body(*pre, *refs[:a], *refs[b:c], *refs[d:e])

        @pl.when(last)
        def _():
            comm.finish(refs[a:b], refs[c:d], refs[e:])

    if comm:
        semantics = ("arbitrary",) * len(grid)
        for i, o in comm.aliases.items():
            aliases[n_pre + n_in + i] = n_out + o
    outs = _pallas(
        hosted, name=name,
        grid_spec=pltpu.PrefetchScalarGridSpec(
            num_scalar_prefetch=n_pre, grid=grid, in_specs=list(in_specs) + [ANY] * c_in,
            out_specs=list(out_specs) + [ANY] * c_out, scratch_shapes=list(scratch_shapes) + c_sems),
        out_shape=list(out_shape) + c_shapes, input_output_aliases=aliases,
        compiler_params=_params(semantics, vmem_mib),
    )(*prefetch, *args, *c_operands)
    return list(outs[:n_out]), list(outs[n_out:])


def _dot_nn(a, b):
    return jnp.dot(a, b, preferred_element_type=F32)


def _dot_nt(a, b):
    return lax.dot_general(a, b, (((1,), (1,)), ((), ())), preferred_element_type=F32)


def _dot_tn(a, b):
    return lax.dot_general(a, b, (((0,), (0,)), ((), ())), preferred_element_type=F32)


def _fold_rows(a):
    r, c = a.shape
    return jnp.sum(a.reshape(r // 8, 8, c), axis=0)


def _cast_bf16(a, chip, name, rows):
    r, c = a.shape

    def body(chip_ref, a_ref, o_ref):
        o_ref[...] = a_ref[...].astype(BF16)

    return _pallas(
        body, name=name,
        grid_spec=pltpu.PrefetchScalarGridSpec(
            num_scalar_prefetch=1, grid=(r // rows,),
            in_specs=[pl.BlockSpec((rows, c), lambda i, chip_ref: (i, 0))],
            out_specs=pl.BlockSpec((None, rows, c), lambda i, chip_ref: (chip_ref[0], i, 0))),
        out_shape=jax.ShapeDtypeStruct((N_SHARDS, r, c), BF16),
        compiler_params=_params(("parallel",), 32),
    )(chip, a)


def _mesh_place():
    x, y, c = lax.axis_index("x"), lax.axis_index("y"), lax.axis_index("c")
    return x, y, c, [(1 - x, y), (x, 1 - y), (1 - x, 1 - y)]


def _remote(src, dst, send_sem, recv_sem, to):
    return pltpu.make_async_remote_copy(src_ref=src, dst_ref=dst, send_sem=send_sem, recv_sem=recv_sem,
                                        device_id=to, device_id_type=MESH)


def _allgather_weights(bufs, phase="all", chunks=1):
    n = len(bufs)
    items = [(a, q) for q in range(chunks) for a in range(n)]

    def rows(item, core):
        a, q = item
        size = bufs[a].shape[1] // 2 // chunks
        return pl.ds(core * chunks * size + q * size, size)

    def sem(sems, which, item, k):
        a, q = item
        return sems[which].at[a * chunks + q, k]

    DIAGONAL = 2

    def to_neighbours(dst, sems, item):
        x, y, c, chips = _mesh_place()
        own = dst[item[0]].at[2 * x + y, rows(item, c)]
        return [_remote(own, own, sem(sems, 0, item, k), sem(sems, 1, item, k), (cx, cy, c))
                for k, (cx, cy) in enumerate(chips[:DIAGONAL])]

    def relayed(dst, sems, item):
        x, y, c, _ = _mesh_place()
        piece = dst[item[0]].at[2 * (x ^ (1 - c)) + (y ^ c), rows(item, c)]
        return _remote(piece, piece, sem(sems, 0, item, DIAGONAL), sem(sems, 1, item, DIAGONAL), (x ^ c, y ^ (1 - c), c))

    def start(ins, dst, sems):
        for item in items:
            for cp in ([relayed(dst, sems, item)] if phase == "diagonal" else to_neighbours(dst, sems, item)):
                cp.start()

    def finish(ins, dst, sems):
        x, y, c, chips = _mesh_place()
        sibling = (x, y, 1 - c)
        passed_on = []

        def landed_then_pass_on(item, k):
            cx, cy = chips[k]
            landed = dst[item[0]].at[2 * cx + cy, rows(item, c)]
            _remote(landed, landed, sem(sems, 0, item, k), sem(sems, 1, item, k), (cx, cy, c)).wait_recv()
            cp = _remote(landed, landed, sem(sems, 2, item, k), sem(sems, 3, item, k), sibling)
            cp.start()
            passed_on.append(cp)

        sent = []
        for item in items:
            if phase != "diagonal":
                for k in range(DIAGONAL):
                    landed_then_pass_on(item, k)
                sent += to_neighbours(dst, sems, item)
            if phase == "all":
                relayed(dst, sems, item).start()
        for item in items:
            if phase != "neighbours":
                landed_then_pass_on(item, DIAGONAL)
                sent.append(relayed(dst, sems, item))
        for k in {"all": (0, 1, 2), "neighbours": (0, 1), "diagonal": (2,)}[phase]:
            cx, cy = chips[k]
            for item in items:
                passed = dst[item[0]].at[2 * cx + cy, rows(item, 1 - c)]
                _remote(passed, passed, sem(sems, 2, item, k), sem(sems, 3, item, k), sibling).wait_recv()
        for cp in sent + passed_on:
            cp.wait_send()

    return _Exchange(bufs, [jax.ShapeDtypeStruct(b.shape, b.dtype) for b in bufs], {a: a for a in range(n)},
                     [pltpu.SemaphoreType.DMA((n * chunks, 3))] * 4, start, finish)


def _rope_tables(seq):
    half = ROPE_DIM // 2
    inv_freq = ROPE_THETA ** (-(2.0 * jnp.arange(half, dtype=F32)) / ROPE_DIM)
    ang = jnp.arange(seq, dtype=jnp.int32).astype(F32)[:, None] * inv_freq[None, :]
    cos, sin = jnp.cos(ang), jnp.sin(ang)
    pad = jnp.zeros((seq, HEAD_DIM - ROPE_DIM), F32)
    zeros = jnp.zeros((seq, half), F32)
    c_tab = jnp.concatenate([cos, cos, pad + 1.0], axis=1)
    up_tab = jnp.concatenate([-sin, zeros, pad], axis=1)
    down_tab = jnp.concatenate([zeros, sin, pad], axis=1)
    return c_tab, up_tab, down_tab


def _rotate_heads(t, c_tab, up_tab, down_tab):
    outs = []
    for h in range(t.shape[1] // HEAD_DIM):
        th = t[:, h * HEAD_DIM:(h + 1) * HEAD_DIM]
        up = pltpu.roll(th, HEAD_DIM - ROPE_DIM // 2, axis=1)
        down = pltpu.roll(th, ROPE_DIM // 2, axis=1)
        outs.append(th * c_tab + up * up_tab + down * down_tab)
    return outs[0] if len(outs) == 1 else jnp.concatenate(outs, axis=1)


def _to_pattern(slabs_ref, dst_ref, dil, dtype):
    n_slabs, rows, _ = slabs_ref.shape
    for s in range(n_slabs):
        for r in range(dil):
            dst_ref[r, :, s * 128:(s + 1) * 128] = slabs_ref[s, pl.ds(r, rows // dil, dil), :].astype(dtype)


def _from_pattern(src_ref, slabs_ref, dil):
    n_slabs, rows, _ = slabs_ref.shape
    for s in range(n_slabs):
        for r in range(dil):
            slabs_ref[s, pl.ds(r, rows // dil, dil), :] = src_ref[r, :, s * 128:(s + 1) * 128].astype(F32)


def _store_slabs(slabs_ref, value):
    for s in range(slabs_ref.shape[0]):
        slabs_ref[s] = value[:, s * 128:(s + 1) * 128]


def _in_proj_qkv(x, w_in_g, tabs, comm=None):
    seq = x.shape[0]
    tm, tn = 512, SHARD_IN
    heads = tn // HEAD_DIM
    k_heads_in_second = 2 * D_ATTN // HEAD_DIM - heads
    d4, d16 = DILATIONS[1], DILATIONS[2]

    def body(x_ref, w_ref, c_ref, up_ref, down_ref, o1_ref, o4_ref, o16_ref, res_ref):
        shard = pl.program_id(0)
        xb = x_ref[...].astype(BF16)
        group = 4 * HEAD_DIM
        accs = [_dot_nn(xb, w_ref[:, g * group:(g + 1) * group]) for g in range(tn // group)]

        plain = shard == 1
        c_plain = jnp.where(plain, 1.0, c_ref[...])
        up_plain = jnp.where(plain, 0.0, up_ref[...])
        down_plain = jnp.where(plain, 0.0, down_ref[...])
        for h in range(heads):
            lanes = (h * HEAD_DIM) % group
            th = accs[h * HEAD_DIM // group][:, lanes:lanes + HEAD_DIM]
            if h < k_heads_in_second:
                th = _rotate_heads(th, c_ref[...], up_ref[...], down_ref[...])
            else:
                th = _rotate_heads(th, c_plain, up_plain, down_plain)
            res_ref[h] = th
            o1_ref[:, h * HEAD_DIM:(h + 1) * HEAD_DIM] = th.astype(BF16)
        _to_pattern(res_ref, o4_ref, d4, BF16)
        _to_pattern(res_ref, o16_ref, d16, BF16)

    tab_spec = pl.BlockSpec((tm, HEAD_DIM), lambda s, i: (i, 0))
    (o1, o4, o16), exchanged = _call(
        body, name="in_proj_qkv", grid=(D_QKV // tn, seq // tm),
        in_specs=[pl.BlockSpec((tm, D_MODEL), lambda s, i: (i, 0)),
                  pl.BlockSpec((None, D_MODEL, tn), lambda s, i: (s, 0, 0)),
                  tab_spec, tab_spec, tab_spec],
        out_specs=[pl.BlockSpec((tm, tn), lambda s, i: (i, s)),
                   pl.BlockSpec((d4, tm // d4, tn), lambda s, i: (0, i, s)),
                   pl.BlockSpec((d16, tm // d16, tn), lambda s, i: (0, i, s))],
        out_shape=[jax.ShapeDtypeStruct((seq, D_QKV), BF16),
                   jax.ShapeDtypeStruct((d4, seq // d4, D_QKV), BF16),
                   jax.ShapeDtypeStruct((d16, seq // d16, D_QKV), BF16)],
        scratch_shapes=[pltpu.VMEM((heads, tm, HEAD_DIM), F32)],
        semantics=("parallel", "parallel"), vmem_mib=52, args=(x, w_in_g, *tabs), comm=comm)
    return [o1[None], o4, o16], exchanged


def _in_proj_pool_gate(x, w_in_g, comm=None):
    seq = x.shape[0]
    tm, tn = 512, SHARD_IN
    first_shard = D_QKV // tn

    def body(x_ref, w_ref, o_ref):
        o_ref[...] = _dot_nn(x_ref[...].astype(BF16), w_ref[...]).astype(BF16)

    (hug,), exchanged = _call(
        body, name="in_proj_pool_gate", grid=(D_UG // tn, seq // tm),
        in_specs=[pl.BlockSpec((tm, D_MODEL), lambda s, i: (i, 0)),
                  pl.BlockSpec((None, D_MODEL, tn), lambda s, i: (s + first_shard, 0, 0))],
        out_specs=[pl.BlockSpec((tm, tn), lambda s, i: (i, s))],
        out_shape=[jax.ShapeDtypeStruct((seq, D_UG), BF16)],
        scratch_shapes=[], semantics=("parallel", "parallel"), vmem_mib=48, args=(x, w_in_g), comm=comm)
    return hug, exchanged


def _band_masks():
    row = lax.broadcasted_iota(jnp.int32, (KEY_BLOCK, KEY_BLOCK), 0)
    col = lax.broadcasted_iota(jnp.int32, (KEY_BLOCK, KEY_BLOCK), 1)
    return col <= row, col >= row


def _attn_fwd(qkv, name):
    dil, n, _ = qkv.shape
    scale = HEAD_DIM ** -0.5
    lo, hi = slice(0, KEY_BLOCK), slice(KEY_BLOCK, CHUNK)

    def body(q_ref, k_ref, v_ref, kb_ref, vb_ref, o_ref, st_ref):
        i = pl.program_id(1)
        cur_mask, prev_mask = _band_masks()
        before_mask = jnp.logical_and(prev_mask, i > 0)
        lane = lax.broadcasted_iota(jnp.int32, (KEY_BLOCK, STAT_LANES), 1)
        tasks = [(rows, h) for rows in (lo, hi) for h in range(N_HEADS)]
        head = lambda h: slice(h * HEAD_DIM, (h + 1) * HEAD_DIM)

        def prev_of(rows, h):
            if rows is lo:
                return kb_ref[:, head(h)], vb_ref[:, head(h)], before_mask
            return k_ref[lo, head(h)], v_ref[lo, head(h)], prev_mask

        scores = []
        for rows, h in tasks:
            q = q_ref[rows, head(h)]
            scores.append((_dot_nt(q, prev_of(rows, h)[0]), _dot_nt(q, k_ref[rows, head(h)])))
        probs = []
        for (rows, h), (qk_prev, qk_cur) in zip(tasks, scores):
            s_prev = jnp.where(prev_of(rows, h)[2], qk_prev * scale, NEG)
            s_cur = jnp.where(cur_mask, qk_cur * scale, NEG)
            m = jnp.max(jnp.maximum(s_prev, s_cur), axis=-1, keepdims=True)
            p_prev = jnp.exp(s_prev - m)
            p_cur = jnp.exp(s_cur - m)
            den = jnp.sum(p_prev + p_cur, axis=-1, keepdims=True)
            probs.append((p_prev.astype(BF16), p_cur.astype(BF16), den, m + jnp.log(den)))
        stats = [jnp.zeros((KEY_BLOCK, STAT_LANES), F32), jnp.zeros((KEY_BLOCK, STAT_LANES), F32)]
        for (rows, h), (p_prev, p_cur, den, lse) in zip(tasks, probs):
            o = _dot_nn(p_cur, v_ref[rows, head(h)]) + _dot_nn(p_prev, prev_of(rows, h)[1])
            o_ref[rows, head(h)] = (o / den).astype(BF16)
            b = 0 if rows is lo else 1
            stats[b] = jnp.where(lane == h, lse, stats[b])
        st_ref[lo, :] = stats[0]
        st_ref[hi, :] = stats[1]

    main = lambda cb: pl.BlockSpec((None, CHUNK, D_ATTN), lambda r, i: (r, i, cb))
    before = lambda cb: pl.BlockSpec((None, KEY_BLOCK, D_ATTN), lambda r, i: (r, jnp.maximum(2 * i - 1, 0), cb))
    return _pallas(
        body, name=name, grid=(dil, n // CHUNK),
        in_specs=[main(0), main(1), main(2), before(1), before(2)],
        out_specs=[main(0), pl.BlockSpec((None, CHUNK, STAT_LANES), lambda r, i: (r, i, 0))],
        out_shape=[jax.ShapeDtypeStruct((dil, n, D_ATTN), BF16), jax.ShapeDtypeStruct((dil, n, STAT_LANES), F32)],
        compiler_params=_params(("parallel", "parallel"), 40),
    )(qkv, qkv, qkv, qkv, qkv)


def _attn_bwd(qkv, do, stats, name, comm=None):
    dil, n, _ = qkv.shape
    n_blocks = n // KEY_BLOCK
    last = n // CHUNK - 1
    scale = HEAD_DIM ** -0.5
    lo, hi = slice(0, KEY_BLOCK), slice(KEY_BLOCK, CHUNK)

    def body(q_ref, k_ref, v_ref, kb_ref, vb_ref, qa_ref, do_ref, doa_ref, st_ref, sta_ref, dq_ref, dk_ref, dv_ref):
        i = pl.program_id(1)
        cur_mask, prev_mask = _band_masks()
        before_mask = jnp.logical_and(prev_mask, i > 0)
        after_mask = jnp.logical_and(prev_mask, i < last)

        rows_cat = lambda a, b: jnp.concatenate([a, b], axis=0)
        masks = (jnp.concatenate([before_mask, cur_mask], axis=1), jnp.concatenate([prev_mask, cur_mask], axis=1),
                 after_mask)

        def operands(h):
            cols = slice(h * HEAD_DIM, (h + 1) * HEAD_DIM)
            lse_c, del_c = slice(h, h + 1), slice(N_HEADS + h, N_HEADS + h + 1)
            q = (q_ref[lo, cols], q_ref[hi, cols], qa_ref[:, cols])
            do = (do_ref[lo, cols], do_ref[hi, cols], doa_ref[:, cols])
            keys = (rows_cat(kb_ref[:, cols], k_ref[lo, cols]), k_ref[:, cols], k_ref[hi, cols])
            vals = (rows_cat(vb_ref[:, cols], v_ref[lo, cols]), v_ref[:, cols], v_ref[hi, cols])
            st = ((st_ref[lo, lse_c], st_ref[lo, del_c]), (st_ref[hi, lse_c], st_ref[hi, del_c]),
                  (sta_ref[:, lse_c], sta_ref[:, del_c]))
            return cols, q, do, keys, vals, st

        group = N_HEADS // 2
        for first_head in range(0, N_HEADS, group):
            heads = range(first_head, first_head + group)
            raw = {}
            for h in heads:
                _, q, do, keys, vals, _ = operands(h)
                raw[h] = [(_dot_nt(q[j], keys[j]), _dot_nt(do[j], vals[j])) for j in range(3)]
            grads = {}
            for h in heads:
                st = operands(h)[5]
                grads[h] = []
                for j in range(3):
                    qk, dp = raw[h][j]
                    lse, delta = st[j]
                    p = jnp.exp(jnp.where(masks[j], qk * scale, NEG) - lse)
                    grads[h].append((p.astype(BF16), (p * (dp - delta) * scale).astype(BF16)))
            for h in heads:
                cols, q, do, keys, _, _ = operands(h)
                (p0, ds0), (p1, ds1), (pa, dsa) = grads[h]
                own, nxt = slice(KEY_BLOCK, CHUNK), slice(0, KEY_BLOCK)

                def put(ref, rows, val, cols=cols):
                    ref[rows, cols] = val.astype(ref.dtype)

                put(dq_ref, lo, _dot_nn(ds0, keys[0]))
                put(dq_ref, hi, _dot_nn(ds1, keys[1]))
                put(dk_ref, lo, _dot_tn(rows_cat(ds0[:, own], ds1[:, nxt]), q_ref[:, cols]))
                put(dk_ref, hi, _dot_tn(rows_cat(ds1[:, own], dsa), rows_cat(q[1], q[2])))
                put(dv_ref, lo, _dot_tn(rows_cat(p0[:, own], p1[:, nxt]), do_ref[:, cols]))
                put(dv_ref, hi, _dot_tn(rows_cat(p1[:, own], pa), rows_cat(do[1], do[2])))

    def spec(rows, width, row_of, cb):
        return pl.BlockSpec((None, rows, width), lambda r, i: (r, row_of(i), cb))

    same = lambda i: i
    before = lambda i: jnp.maximum(2 * i - 1, 0)
    after = lambda i: jnp.minimum(2 * i + 2, n_blocks - 1)
    out = spec(CHUNK, D_ATTN, same, 0)
    return _call(
        body, name=name, grid=(dil, n // CHUNK),
        in_specs=[spec(CHUNK, D_ATTN, same, 0), spec(CHUNK, D_ATTN, same, 1), spec(CHUNK, D_ATTN, same, 2),
                  spec(KEY_BLOCK, D_ATTN, before, 1), spec(KEY_BLOCK, D_ATTN, before, 2),
                  spec(KEY_BLOCK, D_ATTN, after, 0),
                  spec(CHUNK, D_ATTN, same, 0), spec(KEY_BLOCK, D_ATTN, after, 0),
                  spec(CHUNK, STAT_LANES, same, 0), spec(KEY_BLOCK, STAT_LANES, after, 0)],
        out_specs=[out, out, out],
        out_shape=[jax.ShapeDtypeStruct((dil, n, D_ATTN), BF16)] * 3,
        scratch_shapes=[], semantics=("parallel", "parallel"), vmem_mib=40,
        args=(qkv, qkv, qkv, qkv, qkv, qkv, do, do, stats, stats), comm=comm)


def _window_sums(ext, window, backward):
    rows = ext.shape[0]
    acc, span = ext, 1
    while span < window:
        acc = acc + pltpu.roll(acc, (rows - span) if backward else span, axis=0)
        span *= 2
    return acc


def _mix_gate(o_list, st_list, hug, w_pool_g, pool_scale):
    seq = hug.shape[0]
    tm = 256
    halo_blocks = tm // POOL_HALO
    d4, d16 = DILATIONS[1], DILATIONS[2]

    def body(o1_ref, o4_ref, o16_ref, l1_ref, l4_ref, l16_ref, u_ref, halo_ref, ga_ref, gp_ref, wp_ref, sc_ref,
             y_ref, mix_ref, lse_ref, pooled_ref, n4_ref, n16_ref, nl4_ref, nl16_ref):
        i = pl.program_id(0)
        _from_pattern(o4_ref, n4_ref, d4)
        _from_pattern(o16_ref, n16_ref, d16)
        _from_pattern(l4_ref, nl4_ref, d4)
        _from_pattern(l16_ref, nl16_ref, d16)
        la, lb, lc = l1_ref[...], nl4_ref[0], nl16_ref[0]
        mx = jnp.maximum(jnp.maximum(la, lb), lc)
        ea, eb, ec = jnp.exp(la - mx), jnp.exp(lb - mx), jnp.exp(lc - mx)
        tot = ea + eb + ec
        lse_ref[...] = mx + jnp.log(tot)
        wa, wb, wc = ea / tot, eb / tot, ec / tot
        ga = ga_ref[...].astype(F32)
        silu_a = ga * jax.nn.sigmoid(ga)
        for h in range(N_HEADS):
            cols = slice(h * HEAD_DIM, (h + 1) * HEAD_DIM)
            hc = slice(h, h + 1)
            attn = wa[:, hc] * o1_ref[:, cols].astype(F32) + wb[:, hc] * n4_ref[h] + wc[:, hc] * n16_ref[h]
            mix_ref[:, cols] = attn.astype(BF16)
            y_ref[:, cols] = (attn * silu_a[:, cols]).astype(BF16)

        u = u_ref[...].astype(F32)
        halo = jnp.where(i > 0, halo_ref[...].astype(F32), 0.0)
        ext = jnp.concatenate([halo, u], axis=0)
        pos = i * tm + lax.broadcasted_iota(jnp.int32, (tm, 1), 0)
        gp = gp_ref[...].astype(F32)
        gated_scale = sc_ref[...] * (gp * jax.nn.sigmoid(gp))
        for g, window in enumerate(POOL_WINDOWS):
            cols = slice(g * POOL_GROUP_DIM, (g + 1) * POOL_GROUP_DIM)
            sums = _window_sums(ext[:, cols], window, backward=False)[POOL_HALO:, :]
            count = jnp.minimum(pos + 1, window).astype(F32)
            pooled = (sums / count - u[:, cols]).astype(BF16)
            pooled_ref[:, cols] = pooled
            pre = _dot_nn(pooled, wp_ref[g])
            out_cols = slice(D_ATTN + g * POOL_GROUP_DIM, D_ATTN + (g + 1) * POOL_GROUP_DIM)
            mix_ref[:, out_cols] = pre.astype(BF16)
            y_ref[:, out_cols] = (pre * gated_scale[:, cols]).astype(BF16)

    row = lambda width, cb=0: pl.BlockSpec((tm, width), lambda i: (i, cb))
    pat = lambda d, width: pl.BlockSpec((d, tm // d, width), lambda i: (0, i, 0))
    return _pallas(
        body, name="mix_gate", grid=(seq // tm,),
        in_specs=[row(D_ATTN), pat(d4, D_ATTN), pat(d16, D_ATTN),
                  row(STAT_LANES), pat(d4, STAT_LANES), pat(d16, STAT_LANES),
                  row(D_POOL),
                  pl.BlockSpec((POOL_HALO, D_POOL), lambda i: (jnp.maximum(i * halo_blocks - 1, 0), 0)),
                  row(D_ATTN, 1), row(D_POOL, 2),
                  pl.BlockSpec((len(POOL_WINDOWS), POOL_GROUP_DIM, POOL_GROUP_DIM), lambda i: (0, 0, 0)),
                  pl.BlockSpec((1, D_POOL), lambda i: (0, 0))],
        out_specs=[row(D_MODEL), row(D_MODEL), row(STAT_LANES), row(D_POOL)],
        out_shape=[jax.ShapeDtypeStruct((seq, D_MODEL), BF16), jax.ShapeDtypeStruct((seq, D_MODEL), BF16),
                   jax.ShapeDtypeStruct((seq, STAT_LANES), F32), jax.ShapeDtypeStruct((seq, D_POOL), BF16)],
        scratch_shapes=[pltpu.VMEM((N_HEADS, tm, HEAD_DIM), F32), pltpu.VMEM((N_HEADS, tm, HEAD_DIM), F32),
                        pltpu.VMEM((1, tm, STAT_LANES), F32), pltpu.VMEM((1, tm, STAT_LANES), F32)],
        compiler_params=_params(("parallel",), 48),
    )(o_list[0][0], o_list[1], o_list[2], st_list[0][0], st_list[1], st_list[2],
      hug, hug, hug, hug, w_pool_g, pool_scale)


def _out_proj_loss(y, w_out_g, x, target, gain, bias):
    seq = x.shape[0]
    tm = 512

    def body(y_ref, w_ref, x_ref, t_ref, g_ref, b_ref, dz_ref, dzb_ref, gg_ref, gb_ref, loss_ref):
        @pl.when(pl.program_id(0) == 0)
        def _():
            gg_ref[...] = jnp.zeros_like(gg_ref)
            gb_ref[...] = jnp.zeros_like(gb_ref)
            loss_ref[...] = jnp.zeros_like(loss_ref)

        halves = [slice(0, tm // 2), slice(tm // 2, tm)]
        projected = [_dot_nn(y_ref[rows, :], w_ref[...]) for rows in halves]
        for rows, out in zip(halves, projected):
            z = DEEPNORM_ALPHA * x_ref[rows, :] + out
            mu = jnp.mean(z, axis=-1, keepdims=True)
            zc = z - mu
            rstd = lax.rsqrt(jnp.mean(zc * zc, axis=-1, keepdims=True) + LN_EPS)
            xhat = zc * rstd
            gain_v = g_ref[...]
            diff = xhat * gain_v + b_ref[...] - t_ref[rows, :]
            sq = _fold_rows(diff * diff)
            part = sq[:, :128]
            for k in range(1, D_MODEL // 128):
                part = part + sq[:, k * 128:(k + 1) * 128]
            loss_ref[...] += part
            dln = diff * (1.0 / D_MODEL)
            gg_ref[...] += _fold_rows(dln * xhat)
            gb_ref[...] += _fold_rows(dln)
            dxhat = dln * gain_v
            dz = rstd * (dxhat - jnp.mean(dxhat, axis=-1, keepdims=True)
                         - xhat * jnp.mean(dxhat * xhat, axis=-1, keepdims=True))
            dz_ref[rows, :] = dz
            dzb_ref[rows, :] = dz.astype(BF16)

    row = lambda: pl.BlockSpec((tm, D_MODEL), lambda i: (i, 0))
    vec = lambda: pl.BlockSpec((1, D_MODEL), lambda i: (0, 0))
    acc = lambda width: pl.BlockSpec((8, width), lambda i: (0, 0))
    return _pallas(
        body, name="out_proj_loss", grid=(seq // tm,),
        in_specs=[row(), pl.BlockSpec((D_MODEL, D_MODEL), lambda i: (0, 0), pipeline_mode=pl.Buffered(1)),
                  row(), row(), vec(), vec()],
        out_specs=[row(), row(), acc(D_MODEL), acc(D_MODEL), acc(128)],
        out_shape=[jax.ShapeDtypeStruct((seq, D_MODEL), F32), jax.ShapeDtypeStruct((seq, D_MODEL), BF16),
                   jax.ShapeDtypeStruct((8, D_MODEL), F32), jax.ShapeDtypeStruct((8, D_MODEL), F32),
                   jax.ShapeDtypeStruct((8, 128), F32)],
        compiler_params=_params(("arbitrary",), 56),
    )(y, w_out_g.reshape(D_MODEL, D_MODEL), x, target, gain, bias)


def _dy_gate_bwd(dzb, w_out_g, hug, mixpre, pool_scale, lse_all):
    seq = dzb.shape[0]
    tm = 256
    d4, d16 = DILATIONS[1], DILATIONS[2]

    def body(dz_ref, w_ref, ga_ref, gp_ref, mix_ref, sc_ref, lse_ref,
             dh_ref, dpo_ref, do1_ref, do4_ref, do16_ref, st1_ref, st4_ref, st16_ref, da_ref, st_ref):
        dy = _dot_nt(dz_ref[...], w_ref[...])
        ga = ga_ref[...].astype(F32)
        sig = jax.nn.sigmoid(ga)
        attn = mix_ref[:, :D_ATTN].astype(F32)
        dya = dy[:, :D_ATTN]
        dattn = dya * (ga * sig)
        dh_ref[:, :D_ATTN] = (dya * attn * (sig * (1.0 + ga * (1.0 - sig)))).astype(BF16)
        _store_slabs(da_ref, dattn)
        lane = lax.broadcasted_iota(jnp.int32, (tm, STAT_LANES), 1)
        stats = lse_ref[...]
        prod = dattn * attn
        for h in range(N_HEADS):
            delta = jnp.sum(prod[:, h * HEAD_DIM:(h + 1) * HEAD_DIM], axis=-1, keepdims=True)
            stats = jnp.where(lane == N_HEADS + h, delta, stats)
        st_ref[0] = stats
        do1_ref[...] = dattn.astype(BF16)
        st1_ref[...] = stats
        _to_pattern(da_ref, do4_ref, d4, BF16)
        _to_pattern(da_ref, do16_ref, d16, BF16)
        _to_pattern(st_ref, st4_ref, d4, F32)
        _to_pattern(st_ref, st16_ref, d16, F32)

        gp = gp_ref[...].astype(F32)
        sig = jax.nn.sigmoid(gp)
        dyp = dy[:, D_ATTN:]
        dpo_ref[...] = (dyp * (gp * sig)).astype(BF16)
        dh_ref[:, D_ATTN:] = (dyp * (mix_ref[:, D_ATTN:].astype(F32) * sc_ref[...])
                              * (sig * (1.0 + gp * (1.0 - sig)))).astype(BF16)

    row = lambda width, cb=0: pl.BlockSpec((tm, width), lambda i: (i, cb))
    pat = lambda d, width: pl.BlockSpec((d, tm // d, width), lambda i: (0, i, 0))
    pat_shape = lambda d, width, dtype: jax.ShapeDtypeStruct((d, seq // d, width), dtype)
    outs = _pallas(
        body, name="dy_gate_bwd", grid=(seq // tm,),
        in_specs=[row(D_MODEL), pl.BlockSpec((D_MODEL, D_MODEL), lambda i: (0, 0)),
                  row(D_ATTN, 1), row(D_POOL, 2), row(D_MODEL), pl.BlockSpec((1, D_POOL), lambda i: (0, 0)),
                  row(STAT_LANES)],
        out_specs=[row(D_MODEL, D_IN // D_MODEL - 1), row(D_POOL),
                   row(D_ATTN), pat(d4, D_ATTN), pat(d16, D_ATTN),
                   row(STAT_LANES), pat(d4, STAT_LANES), pat(d16, STAT_LANES)],
        out_shape=[jax.ShapeDtypeStruct((seq, D_IN), BF16), jax.ShapeDtypeStruct((seq, D_POOL), BF16),
                   jax.ShapeDtypeStruct((seq, D_ATTN), BF16), pat_shape(d4, D_ATTN, BF16), pat_shape(d16, D_ATTN, BF16),
                   jax.ShapeDtypeStruct((seq, STAT_LANES), F32), pat_shape(d4, STAT_LANES, F32),
                   pat_shape(d16, STAT_LANES, F32)],
        scratch_shapes=[pltpu.VMEM((N_HEADS, tm, HEAD_DIM), F32), pltpu.VMEM((1, tm, STAT_LANES), F32)],
        compiler_params=_params(("parallel",), 48),
    )(dzb, w_out_g.reshape(D_MODEL, D_MODEL), hug, hug, mixpre, pool_scale, lse_all)
    dh, dpo, do1, do4, do16, st1, st4, st16 = outs
    return dh, dpo, [do1[None], do4, do16], [st1[None], st4, st16]


def _pool_bwd(dh, dpo, mixpre, pooled, w_pool_g, pool_scale):
    seq = dpo.shape[0]
    tm = 256
    halo_blocks = tm // POOL_HALO
    last = seq // tm - 1
    n_groups = len(POOL_WINDOWS)

    def body(dh_in_ref, dpo_ref, halo_ref, pre_ref, pooled_ref, wp_ref, sc_ref, du_ref, gw_ref, gs_ref):
        i = pl.program_id(0)

        @pl.when(i == 0)
        def _():
            gw_ref[...] = jnp.zeros_like(gw_ref)
            gs_ref[...] = jnp.zeros_like(gs_ref)

        dpo = dpo_ref[...].astype(F32)
        scale = sc_ref[...]
        gs_ref[...] += _fold_rows(dpo * pre_ref[...].astype(F32))
        halo = jnp.where(i < last, halo_ref[...].astype(F32), 0.0)
        dpw = (jnp.concatenate([dpo, halo], axis=0) * scale).astype(BF16)
        pos = i * tm + lax.broadcasted_iota(jnp.int32, (tm + POOL_HALO, 1), 0)
        for g, window in enumerate(POOL_WINDOWS):
            cols = slice(g * POOL_GROUP_DIM, (g + 1) * POOL_GROUP_DIM)
            dpw_g = dpw[:, cols]
            gw_ref[g] += _dot_tn(pooled_ref[:, cols], dpw_g[:tm, :])
            dpooled = _dot_nt(dpw_g, wp_ref[g])
            count = jnp.minimum(pos + 1, window).astype(F32)
            sums = _window_sums(dpooled / count, window, backward=True)
            du_ref[:, cols] = (sums[:tm, :] - dpooled[:tm, :]).astype(BF16)

    row = lambda width, cb=0: pl.BlockSpec((tm, width), lambda i: (i, cb))
    return _pallas(
        body, name="pool_bwd", grid=(seq // tm,),
        in_specs=[ANY, row(D_POOL),
                  pl.BlockSpec((POOL_HALO, D_POOL),
                               lambda i: (jnp.minimum((i + 1) * halo_blocks, seq // POOL_HALO - 1), 0)),
                  row(D_POOL, 1), row(D_POOL),
                  pl.BlockSpec((n_groups, POOL_GROUP_DIM, POOL_GROUP_DIM), lambda i: (0, 0, 0)),
                  pl.BlockSpec((1, D_POOL), lambda i: (0, 0))],
        out_specs=[row(D_POOL, D_QKV // D_POOL),
                   pl.BlockSpec((n_groups, POOL_GROUP_DIM, POOL_GROUP_DIM), lambda i: (0, 0, 0)),
                   pl.BlockSpec((8, D_POOL), lambda i: (0, 0))],
        out_shape=[jax.ShapeDtypeStruct(dh.shape, dh.dtype),
                   jax.ShapeDtypeStruct((n_groups, POOL_GROUP_DIM, POOL_GROUP_DIM), F32),
                   jax.ShapeDtypeStruct((8, D_POOL), F32)],
        input_output_aliases={0: 0},
        compiler_params=_params(("arbitrary",), 40),
    )(dh, dpo, dpo, mixpre, pooled, w_pool_g, pool_scale)


def _sum_patterns(dh, parts, tabs, unrotate, col_block, name, comm=None):
    seq = dh.shape[0]
    tm, tn = 256, D_ATTN
    per = D_ATTN // tn
    d4, d16 = DILATIONS[1], DILATIONS[2]

    def body(dh_in_ref, a1_ref, a4_ref, a16_ref, ct_ref, up_ref, down_ref, o_ref, n4_ref, n16_ref):
        _from_pattern(a4_ref, n4_ref, d4)
        _from_pattern(a16_ref, n16_ref, d16)
        for s in range(tn // HEAD_DIM):
            cols = slice(s * HEAD_DIM, (s + 1) * HEAD_DIM)
            tot = a1_ref[:, cols].astype(F32) + n4_ref[s] + n16_ref[s]
            if unrotate:
                tot = _rotate_heads(tot, ct_ref[...], -up_ref[...], -down_ref[...])
            o_ref[:, cols] = tot.astype(BF16)

    tab = pl.BlockSpec((tm, HEAD_DIM), lambda i, j: (i, 0))
    pat = lambda d: pl.BlockSpec((d, tm // d, tn), lambda i, j: (0, i, j))
    (dh,), exchanged = _call(
        body, name=name, grid=(seq // tm, per),
        in_specs=[ANY, pl.BlockSpec((tm, tn), lambda i, j: (i, j)), pat(d4), pat(d16), tab, tab, tab],
        out_specs=[pl.BlockSpec((tm, tn), lambda i, j: (i, col_block * per + j))],
        out_shape=[jax.ShapeDtypeStruct(dh.shape, dh.dtype)],
        scratch_shapes=[pltpu.VMEM((tn // HEAD_DIM, tm, HEAD_DIM), F32), pltpu.VMEM((tn // HEAD_DIM, tm, HEAD_DIM), F32)],
        semantics=("parallel", "parallel"), vmem_mib=32, args=(dh, parts[0][0], parts[1], parts[2], *tabs),
        aliases={0: 0}, comm=comm)
    return dh, exchanged


def _grad_w_in(x, dh, half, name, comm=None):
    seq = x.shape[0]
    ts, td, te = 2048, D_MODEL // 2, SHARD_IN

    def body(half_ref, x_ref, dh_ref, o_ref):
        k = pl.program_id(1)
        part = _dot_tn(x_ref[...].astype(BF16), dh_ref[...])

        @pl.when(k == 0)
        def _():
            o_ref[...] = part

        @pl.when(k > 0)
        def _():
            o_ref[...] += part

    (g,), exchanged = _call(
        body, name=name, grid=(N_SHARDS, seq // ts),
        in_specs=[pl.BlockSpec((ts, td), lambda e, k, half_ref: (k, half_ref[0])),
                  pl.BlockSpec((ts, te), lambda e, k, half_ref: (k, e))],
        out_specs=[pl.BlockSpec((None, td, te), lambda e, k, half_ref: (e, 0, 0))],
        out_shape=[jax.ShapeDtypeStruct((N_SHARDS, td, te), F32)],
        scratch_shapes=[], semantics=("parallel", "arbitrary"), vmem_mib=56, args=(x, dh), comm=comm,
        prefetch=(half,))
    return g, exchanged


def _grad_w_out(y, dzb):
    seq = y.shape[0]
    ts, te = 512, 1024
    nk = seq // ts

    def body(y_ref, dz_ref, o_ref, acc_ref):
        k = pl.program_id(1)

        @pl.when(k == 0)
        def _():
            acc_ref[...] = jnp.zeros_like(acc_ref)

        acc_ref[...] += _dot_tn(y_ref[...], dz_ref[...])

        @pl.when(k == nk - 1)
        def _():
            o_ref[...] = acc_ref[...]

    return _pallas(
        body, name="grad_w_out", grid=(D_MODEL // te, nk),
        in_specs=[pl.BlockSpec((ts, te), lambda e, k: (k, e)), pl.BlockSpec((ts, D_MODEL), lambda e, k: (k, 0))],
        out_specs=pl.BlockSpec((te, D_MODEL), lambda e, k: (e, 0)),
        out_shape=jax.ShapeDtypeStruct((D_MODEL, D_MODEL), F32),
        scratch_shapes=[pltpu.VMEM((te, D_MODEL), F32)],
        compiler_params=_params(("parallel", "arbitrary"), 48),
    )(y, dzb)


GRAD_X_LATE_SHARDS = 1
GRAD_X_PARTIAL_ROWS = 1024


def _grad_x_partial(dh, w_in_g, first, tiles, prev=None, comm=None):
    seq = dh.shape[0]
    tm, tk = GRAD_X_PARTIAL_ROWS, SHARD_IN

    def body(*refs):
        dh_ref, w_ref, o_ref = refs[-3:]
        k = pl.program_id(1)
        part = _dot_nt(dh_ref[...], w_ref[...])

        @pl.when(k == 0)
        def _():
            o_ref[...] = part

        @pl.when(k > 0)
        def _():
            o_ref[...] += part

    carried = [] if prev is None else [prev]
    (partial,), exchanged = _call(
        body, name="grad_x_partial_%d" % first, grid=(tiles, N_SHARDS - GRAD_X_LATE_SHARDS),
        in_specs=[ANY] * len(carried) + [
            pl.BlockSpec((tm, tk), lambda i, k: (i + first, k)),
            pl.BlockSpec((None, D_MODEL, tk), lambda i, k: (k, 0, 0))],
        out_specs=[pl.BlockSpec((tm, D_MODEL), lambda i, k: (i + first, 0))],
        out_shape=[jax.ShapeDtypeStruct((seq, D_MODEL), F32)],
        scratch_shapes=[], semantics=("parallel", "arbitrary"), vmem_mib=48, args=(*carried, dh, w_in_g),
        aliases={0: 0} if carried else None, comm=comm)
    return partial, exchanged


def _grad_x_final(dh, w_in_g, dz, partial):
    seq = dh.shape[0]
    tm, tk = 512, SHARD_IN
    k0 = N_SHARDS - GRAD_X_LATE_SHARDS

    def body(dh_ref, w_ref, dz_ref, p_ref, o_ref):
        k = pl.program_id(1)
        part = _dot_nt(dh_ref[...], w_ref[...])

        @pl.when(k == 0)
        def _():
            o_ref[...] = (DEEPNORM_ALPHA * dz_ref[...] + p_ref[...]) + part

        @pl.when(k > 0)
        def _():
            o_ref[...] += part

    row = pl.BlockSpec((tm, D_MODEL), lambda i, k: (i, 0))
    return _pallas(
        body, name="grad_x_final", grid=(seq // tm, GRAD_X_LATE_SHARDS),
        in_specs=[pl.BlockSpec((tm, tk), lambda i, k: (i, k + k0)),
                  pl.BlockSpec((None, D_MODEL, tk), lambda i, k: (k + k0, 0, 0)), row, row],
        out_specs=row, out_shape=jax.ShapeDtypeStruct((seq, D_MODEL), F32),
        compiler_params=_params(("parallel", "arbitrary"), 48),
    )(dh, w_in_g, dz, partial)


def _pool_weight(w_pool_sh):
    n_groups = len(POOL_WINDOWS)
    shard_c = POOL_GROUP_DIM // N_SHARDS
    return (w_pool_sh.reshape(N_SHARDS, n_groups, shard_c, POOL_GROUP_DIM).transpose(1, 0, 2, 3)
            .reshape(n_groups, POOL_GROUP_DIM, POOL_GROUP_DIM))


def _pool_grad_pieces(g_w_pool):
    n_groups = len(POOL_WINDOWS)
    half_c = POOL_GROUP_DIM // N_SHARDS // 2
    return (g_w_pool.reshape(n_groups, N_SHARDS, 2, half_c, POOL_GROUP_DIM).transpose(1, 2, 0, 3, 4)
            .reshape(N_SHARDS, 2, n_groups * half_c, POOL_GROUP_DIM))


def _step(x, target, w_in_g, w_rest, pool_scale, gain, bias, place=None):
    seq = x.shape[0]
    tabs = _rope_tables(seq)
    qkv, gathered = _in_proj_qkv(x, w_in_g, tabs, comm=_allgather_weights(w_rest, "neighbours") if place else None)
    hug, gathered = _in_proj_pool_gate(x, w_in_g, _allgather_weights(gathered, "diagonal") if place else None)
    w_out_g, w_pool_sh = gathered if place else w_rest
    w_pool_g = _pool_weight(w_pool_sh)
    o_list, st_list = [], []
    for p, dil in enumerate(DILATIONS):
        o, st = _attn_fwd(qkv[p], "attn_fwd_d%d" % dil)
        o_list.append(o)
        st_list.append(st)
    y, mixpre, lse_all, pooled = _mix_gate(o_list, st_list, hug, w_pool_g, pool_scale)
    dz, dzb, gain_part, bias_part, loss_part = _out_proj_loss(y, w_out_g, x, target, gain, bias)
    dh, dpo, do_list, stat_list = _dy_gate_bwd(dzb, w_out_g, hug, mixpre, pool_scale, lse_all)
    g_w_out = _grad_w_out(y, dzb)
    dh, g_w_pool, scale_part = _pool_bwd(dh, dpo, mixpre, pooled, w_pool_g, pool_scale)
    small = jnp.concatenate([scale_part, gain_part, bias_part, loss_part], axis=1)
    early = [g_w_out.reshape(N_SHARDS, 2, D_MODEL // (2 * N_SHARDS), D_MODEL), _pool_grad_pieces(g_w_pool)]

    bwd = lambda p, comm: _attn_bwd(qkv[p], do_list[p], stat_list[p], "attn_bwd_d%d" % DILATIONS[p], comm)
    if place is None:
        parts = [bwd(p, None)[0] for p in range(3)]
    else:
        core, chip_core, onward = place
        part_a, recv = bwd(0, _exchange_halves(early))
        sums = [_add_own_half(g, r, core, "add_own_half_%d" % a) for a, (g, r) in enumerate(zip(early, recv))]
        part_b, recv = bwd(1, _scatter_to_chips([s[1] for s in sums]))
        bufs = [_add_chips(s[0], r, chip_core, "add_chips_%d" % a) for a, (s, r) in enumerate(zip(sums, recv))]
        part_c, early = bwd(2, _share_with_sibling(bufs))
        parts = [part_a, part_b, part_c]
    dh, gathered = _sum_patterns(dh, [t[0] for t in parts], tabs, True, 0, "sum_dq",
                                 _gather_small(small) if place else None)
    dh, _ = _sum_patterns(dh, [t[1] for t in parts], tabs, True, 1, "sum_dk")
    dh, _ = _sum_patterns(dh, [t[2] for t in parts], tabs, False, 2, "sum_dv")
    if place:
        small = (small, gathered[0])
    if place is None:
        halves = [_grad_w_in(x, dh, jnp.full((1,), h, jnp.int32), "grad_w_in_%d" % h)[0] for h in range(2)]
        g_w_in = jnp.stack(halves, axis=1)
        g_x = _grad_x_final(dh, w_in_g, dz, _grad_x_partial(dh, w_in_g, 0, seq // GRAD_X_PARTIAL_ROWS)[0])
    else:
        give, _ = _grad_w_in(x, dh, 1 - core, "grad_w_in_give")
        keep, recv = _grad_w_in(x, dh, core, "grad_w_in_keep", _send_to_sibling([give]))
        total, total_b = _add_pair(keep, recv[0], "add_own_half_w_in")
        n_tiles = seq // GRAD_X_PARTIAL_ROWS
        tiles = max(n_tiles // 4, 1)
        part, relayed = _grad_x_partial(dh, w_in_g, 0, tiles, None, _relay_diagonal(total_b))
        total_b = _fold_relayed(total, total_b, relayed[0], onward)
        part, recv = _grad_x_partial(dh, w_in_g, tiles, n_tiles - tiles, part, _scatter_to_neighbours(total_b))
        buf = _add_chips(total, recv[0], chip_core, "add_chips_w_in")
        g_x = _grad_x_final(dh, w_in_g, dz, part)
        g_w_in = _run_exchange(_share_with_sibling([buf]), "share_w_in")[0]
    return g_x, g_w_in, early[0], early[1], small


def _exchange_halves(grads):
    n = len(grads)

    def copies(src, dst, sems):
        x, y, c, _ = _mesh_place()
        return [_remote(src[a].at[j, 1 - c], dst[a].at[j], sems[0].at[a, j], sems[1].at[a, j], (x, y, 1 - c))
                for a in range(n) for j in range(N_SHARDS)]

    def start(src, dst, sems):
        for cp in copies(src, dst, sems):
            cp.start()

    def finish(src, dst, sems):
        for cp in copies(src, dst, sems):
            cp.wait()

    return _Exchange(grads, [jax.ShapeDtypeStruct((N_SHARDS,) + g.shape[2:], g.dtype) for g in grads], {},
                     [pltpu.SemaphoreType.DMA((n, N_SHARDS))] * 2, start, finish)


def _add_own_half(grad, recv, core, name):
    _, _, r, c = grad.shape
    tr = min(r, 256)

    def body(core_ref, g_ref, r_ref, o_ref, ob_ref):
        tot = g_ref[...] + r_ref[...]
        o_ref[...] = tot
        ob_ref[...] = tot.astype(BF16)

    out = pl.BlockSpec((None, tr, c), lambda j, i, core_ref: (j, i, 0))
    return _pallas(
        body, name=name,
        grid_spec=pltpu.PrefetchScalarGridSpec(
            num_scalar_prefetch=1, grid=(N_SHARDS, r // tr),
            in_specs=[pl.BlockSpec((None, None, tr, c), lambda j, i, core_ref: (j, core_ref[0], i, 0)),
                      pl.BlockSpec((None, tr, c), lambda j, i, core_ref: (j, i, 0))],
            out_specs=[out, out]),
        out_shape=[jax.ShapeDtypeStruct((N_SHARDS, r, c), F32), jax.ShapeDtypeStruct((N_SHARDS, r, c), BF16)],
        compiler_params=_params(("parallel", "parallel"), 32),
    )(core, grad, recv)


def _send_to_sibling(arrays):
    n = len(arrays)

    def copies(src, dst, sems):
        x, y, c, _ = _mesh_place()
        return [_remote(src[a], dst[a], sems[0].at[a], sems[1].at[a], (x, y, 1 - c)) for a in range(n)]

    def start(src, dst, sems):
        for cp in copies(src, dst, sems):
            cp.start()

    def finish(src, dst, sems):
        for cp in copies(src, dst, sems):
            cp.wait()

    return _Exchange(arrays, [jax.ShapeDtypeStruct(t.shape, t.dtype) for t in arrays], {},
                     [pltpu.SemaphoreType.DMA((n,))] * 2, start, finish)


def _add_pair(a, b, name):
    _, r, c = a.shape
    tr = min(r, 256)

    def body(a_ref, b_ref, o_ref, ob_ref):
        tot = a_ref[...] + b_ref[...]
        o_ref[...] = tot
        ob_ref[...] = tot.astype(BF16)

    spec = pl.BlockSpec((None, tr, c), lambda j, i: (j, i, 0))
    return _pallas(
        body, name=name, grid=(N_SHARDS, r // tr), in_specs=[spec, spec], out_specs=[spec, spec],
        out_shape=[jax.ShapeDtypeStruct(a.shape, F32), jax.ShapeDtypeStruct(a.shape, BF16)],
        compiler_params=_params(("parallel", "parallel"), 32),
    )(a, b)


def _scatter_to_chips(sums, rows=None, into=None):
    n = len(sums)

    def copies(src, dst, sems):
        x, y, c, chips = _mesh_place()
        part = (lambda ref: ref) if rows is None else (lambda ref: ref.at[pl.ds(rows[0], rows[1])])
        return [_remote(part(src[a].at[2 * cx + cy]), part(dst[a].at[k]), sems[0].at[a, k], sems[1].at[a, k],
                        (cx, cy, c))
                for a in range(n) for k, (cx, cy) in enumerate(chips)]

    def start(src, dst, sems):
        for cp in copies(src, dst, sems):
            cp.start()

    def finish(src, dst, sems):
        for cp in copies(src, dst, sems):
            cp.wait()

    return _Exchange(sums + (into or []), [jax.ShapeDtypeStruct((3,) + s.shape[1:], s.dtype) for s in sums],
                     {n + a: a for a in range(n)} if into else {},
                     [pltpu.SemaphoreType.DMA((n, 3))] * 2, start, finish)


def _add_chips(sums, recv, chip_core, name):
    _, r, c = sums.shape
    n_recv = recv.shape[0]
    tr = min(r, 256)

    def body(cc_ref, s_ref, r_ref, o_ref):
        tot = s_ref[...]
        for k in range(n_recv):
            tot = tot + r_ref[k].astype(F32)
        o_ref[...] = tot

    return _pallas(
        body, name=name,
        grid_spec=pltpu.PrefetchScalarGridSpec(
            num_scalar_prefetch=1, grid=(r // tr,),
            in_specs=[pl.BlockSpec((None, tr, c), lambda i, cc_ref: (cc_ref[0], i, 0)),
                      pl.BlockSpec((n_recv, tr, c), lambda i, cc_ref: (0, i, 0))],
            out_specs=pl.BlockSpec((None, tr, c), lambda i, cc_ref: (cc_ref[1], i, 0))),
        out_shape=jax.ShapeDtypeStruct((2, r, c), F32),
        compiler_params=_params(("parallel",), 32),
    )(chip_core, sums, recv)


def _relay_diagonal(sums_b):
    def copy(src, dst, sems):
        x, y, c, _ = _mesh_place()
        diagonal = 2 * (1 - x) + (1 - y)
        return _remote(src[0].at[diagonal], dst[0], sems[0].at[0], sems[1].at[0], (x ^ (1 - c), y ^ c, c))

    def start(src, dst, sems):
        copy(src, dst, sems).start()

    def finish(src, dst, sems):
        copy(src, dst, sems).wait()

    return _Exchange([sums_b], [jax.ShapeDtypeStruct(sums_b.shape[1:], sums_b.dtype)], {},
                     [pltpu.SemaphoreType.DMA((1,))] * 2, start, finish)


def _fold_relayed(sums, sums_b, relayed, onward):
    _, r, c = sums.shape
    tr = min(r, 256)

    def body(on_ref, b_in_ref, s_ref, r_ref, o_ref):
        o_ref[...] = (s_ref[...] + r_ref[...].astype(F32)).astype(BF16)

    return _pallas(
        body, name="fold_relayed",
        grid_spec=pltpu.PrefetchScalarGridSpec(
            num_scalar_prefetch=1, grid=(r // tr,),
            in_specs=[ANY, pl.BlockSpec((None, tr, c), lambda i, on_ref: (on_ref[0], i, 0)),
                      pl.BlockSpec((tr, c), lambda i, on_ref: (i, 0))],
            out_specs=pl.BlockSpec((None, tr, c), lambda i, on_ref: (on_ref[0], i, 0))),
        out_shape=jax.ShapeDtypeStruct(sums_b.shape, sums_b.dtype),
        input_output_aliases={1: 0},
        compiler_params=_params(("parallel",), 32),
    )(onward, sums_b, sums, relayed)


def _scatter_to_neighbours(sums_b):
    def copies(src, dst, sems):
        x, y, c, chips = _mesh_place()
        return [_remote(src[0].at[2 * cx + cy], dst[0].at[k], sems[0].at[k], sems[1].at[k], (cx, cy, c))
                for k, (cx, cy) in enumerate(chips[:2])]

    def start(src, dst, sems):
        for cp in copies(src, dst, sems):
            cp.start()

    def finish(src, dst, sems):
        for cp in copies(src, dst, sems):
            cp.wait()

    return _Exchange([sums_b], [jax.ShapeDtypeStruct((2,) + sums_b.shape[1:], sums_b.dtype)], {},
                     [pltpu.SemaphoreType.DMA((2,))] * 2, start, finish)


def _share_with_sibling(bufs):
    n = len(bufs)

    def copies(dst, sems, half):
        x, y, c, _ = _mesh_place()
        h = c if half == "mine" else 1 - c
        return [_remote(dst[a].at[h], dst[a].at[h], sems[0].at[a], sems[1].at[a], (x, y, 1 - c)) for a in range(n)]

    def start(ins, dst, sems):
        for cp in copies(dst, sems, "mine"):
            cp.start()

    def finish(ins, dst, sems):
        for cp in copies(dst, sems, "theirs"):
            cp.wait_recv()
        for cp in copies(dst, sems, "mine"):
            cp.wait_send()

    return _Exchange(bufs, [jax.ShapeDtypeStruct(b.shape, b.dtype) for b in bufs], {a: a for a in range(n)},
                     [pltpu.SemaphoreType.DMA((n,))] * 2, start, finish)


def _adam_math(w, g, m, v):
    m = ADAM_B1 * m + (1.0 - ADAM_B1) * g
    v = ADAM_B2 * v + (1.0 - ADAM_B2) * (g * g)
    m_hat = m / (1.0 - ADAM_B1 ** ADAM_STEP)
    v_hat = v / (1.0 - ADAM_B2 ** ADAM_STEP)
    delta = -ADAM_LR * (m_hat / (jnp.sqrt(v_hat) + ADAM_EPS) + ADAM_WD * w)
    return delta, m, v


def _gather_small(small):
    def peers():
        x, y, c, _ = _mesh_place()
        return [(x ^ ((r >> 2) & 1), y ^ ((r >> 1) & 1), c ^ (r & 1)) for r in range(1, 8)], 4 * x + 2 * y + c

    def start(src, dst, sems):
        to, me = peers()
        for r, peer in enumerate(to):
            _remote(src[0], dst[0].at[me], sems[0].at[r], sems[1].at[r], peer).start()

    def finish(src, dst, sems):
        to, me = peers()
        for r, (px, py, pc) in enumerate(to):
            theirs = dst[0].at[4 * px + 2 * py + pc]
            _remote(theirs, theirs, sems[0].at[r], sems[1].at[r], (px, py, pc)).wait_recv()
        for r, peer in enumerate(to):
            _remote(src[0], dst[0].at[me], sems[0].at[r], sems[1].at[r], peer).wait_send()

    return _Exchange([small], [jax.ShapeDtypeStruct((8,) + small.shape, small.dtype)], {},
                     [pltpu.SemaphoreType.DMA((7,))] * 2, start, finish)


def _small_adamw(gathered, small, me, w_vec, m_vec, v_vec):
    n_par = w_vec.shape[1]

    def body(me_ref, a_ref, s_ref, w_ref, m_ref, v_ref, loss_ref, g_ref, d_ref, nm_ref, nv_ref):
        mine = s_ref[...]
        tot = jnp.where(me_ref[0] == 0, mine, a_ref[0])
        for d in range(1, 8):
            tot = tot + jnp.where(me_ref[0] == d, mine, a_ref[d])
        tot = jnp.sum(tot, axis=0, keepdims=True)
        sq = jnp.sum(tot[:, n_par:], axis=1, keepdims=True)
        loss_ref[...] = jnp.broadcast_to(sq * (0.5 / D_MODEL), loss_ref.shape)
        g = tot[:, :n_par]
        g_ref[...] = g
        d_ref[...], nm_ref[...], nv_ref[...] = _adam_math(w_ref[...], g, m_ref[...], v_ref[...])

    vm = pl.BlockSpec(memory_space=pltpu.VMEM)
    vec = jax.ShapeDtypeStruct((1, n_par), F32)
    return pl.pallas_call(
        body, name="small_adamw",
        grid_spec=pltpu.PrefetchScalarGridSpec(num_scalar_prefetch=1, grid=(), in_specs=[vm] * 5, out_specs=[vm] * 5),
        out_shape=[jax.ShapeDtypeStruct((1, 128), F32), vec, vec, vec, vec],
    )(me, gathered, small, w_vec, m_vec, v_vec)


def _adamw(w, g, m, v, name):
    r, c = w.shape
    tr = min(r, 256)

    def body(w_ref, g_ref, m_ref, v_ref, d_ref, nm_ref, nv_ref):
        d_ref[...], nm_ref[...], nv_ref[...] = _adam_math(w_ref[...], g_ref[...], m_ref[...], v_ref[...])

    spec = pl.BlockSpec((tr, c), lambda i: (i, 0))
    shape = jax.ShapeDtypeStruct((r, c), F32)
    return _pallas(
        body, name=name, grid=(r // tr,),
        in_specs=[spec] * 4, out_specs=[spec] * 3, out_shape=[shape] * 3,
        compiler_params=_params(("parallel",), 48),
    )(w, g, m, v)


def kernel(x, w_in, w_pool, pool_scale, w_out, ln_gain, ln_bias, loss_target, m_w_in, m_w_pool, m_pool_scale, m_w_out, m_ln_gain, m_ln_bias, v_w_in, v_w_pool, v_pool_scale, v_w_out, v_ln_gain, v_ln_bias):
    xi, yi, ci = lax.axis_index("x"), lax.axis_index("y"), lax.axis_index("c")
    chip = (2 * xi + yi).astype(jnp.int32).reshape(1)
    core = ci.astype(jnp.int32).reshape(1)
    n_groups = len(POOL_WINDOWS)
    shard_c = w_pool.shape[2]

    w_in_b = _cast_bf16(w_in[0], chip, "cast_w_in", 256)
    w_out_b = _cast_bf16(w_out[0], chip, "cast_w_out", 256)
    w_pool_b = _cast_bf16(w_pool[0].reshape(n_groups * shard_c, POOL_GROUP_DIM), chip, "cast_w_pool", 256)
    w_in_g = _run_exchange(_allgather_weights([w_in_b], chunks=4), "allgather_w_in")[0]

    chip_core = jnp.concatenate([chip, core])
    onward = (2 * (xi ^ ci) + (yi ^ (1 - ci))).astype(jnp.int32).reshape(1)
    g_x, full_in, full_out, full_pool, small = _step(
        x[0], loss_target[0], w_in_g, [w_out_b, w_pool_b], pool_scale, ln_gain, ln_bias, (core, chip_core, onward))
    half_c = shard_c // 2
    grad_w_in = full_in.reshape(D_MODEL, SHARD_IN)
    grad_w_out = full_out.reshape(D_MODEL // N_SHARDS, D_MODEL)
    grad_w_pool = (full_pool.reshape(2, n_groups, half_c, POOL_GROUP_DIM).transpose(1, 0, 2, 3)
                   .reshape(n_groups * shard_c, POOL_GROUP_DIM))

    d_in, nm_in, nv_in = _adamw(w_in[0], grad_w_in, m_w_in[0], v_w_in[0], "adamw_w_in")
    d_out, nm_out, nv_out = _adamw(w_out[0], grad_w_out, m_w_out[0], v_w_out[0], "adamw_w_out")
    flat = lambda t: t[0].reshape(n_groups * shard_c, POOL_GROUP_DIM)
    d_pool, nm_pool, nv_pool = _adamw(flat(w_pool), grad_w_pool, flat(m_w_pool), flat(v_w_pool), "adamw_w_pool")

    cat = lambda a, b, c: jnp.concatenate([a, b, c], axis=1)
    me = (4 * xi + 2 * yi + ci).astype(jnp.int32).reshape(1)
    loss_v, g_vec, d_vec, nm_vec, nv_vec = _small_adamw(
        small[1], small[0], me, cat(pool_scale, ln_gain, ln_bias), cat(m_pool_scale, m_ln_gain, m_ln_bias),
        cat(v_pool_scale, v_ln_gain, v_ln_bias))

    def split(vec):
        return vec[:, :D_POOL], vec[:, D_POOL:D_POOL + D_MODEL], vec[:, D_POOL + D_MODEL:]

    g_scale, g_gain, g_bias = split(g_vec)
    d_scale, d_gain, d_bias = split(d_vec)
    nm_scale, nm_gain, nm_bias = split(nm_vec)
    nv_scale, nv_gain, nv_bias = split(nv_vec)
    pool_shape = w_pool.shape
    return (loss_v[0, 0], g_x[None],
            grad_w_in[None], grad_w_pool.reshape(pool_shape), g_scale, grad_w_out[None], g_gain, g_bias,
            d_in[None], d_pool.reshape(pool_shape), d_scale, d_out[None], d_gain, d_bias,
            nm_in[None], nm_pool.reshape(pool_shape), nm_scale, nm_out[None], nm_gain, nm_bias,
            nv_in[None], nv_pool.reshape(pool_shape), nv_scale, nv_out[None], nv_gain, nv_bias)
```

```python
import functools

import jax
import jax.numpy as jnp
from jax import lax
from jax.experimental import pallas as pl
from jax.experimental.pallas import tpu as pltpu

F32 = jnp.float32
BF16 = jnp.bfloat16
MESH = pl.DeviceIdType.MESH
ANY = pl.BlockSpec(memory_space=pl.ANY)

D_MODEL = 2048
D_ATTN = 1024
D_POOL = 1024
HEAD_DIM = 128
N_HEADS = 8
ROPE_DIM = 32
ROPE_THETA = 500000.0
DILATIONS = (1, 4, 16)
KEY_BLOCK = 128
CHUNK = 2 * KEY_BLOCK
STAT_LANES = 128
POOL_WINDOWS = (2, 4, 8, 16)
POOL_GROUP_DIM = 256
POOL_HALO = 16
D_QKV = 3 * D_ATTN
D_UG = D_POOL + D_MODEL
D_IN = D_QKV + D_UG
N_SHARDS = 4
SHARD_IN = D_IN // N_SHARDS
LN_EPS = 1e-5
DEEPNORM_ALPHA = 2.0 ** 0.25
ADAM_LR = 0.001
ADAM_B1 = 0.9
ADAM_B2 = 0.999
ADAM_EPS = 1e-08
ADAM_WD = 0.01
ADAM_STEP = 10
NEG = -1e30
MIB = 1024 * 1024


def _params(sem, vmem_mib):
    return pltpu.CompilerParams(dimension_semantics=sem, vmem_limit_bytes=vmem_mib * MIB)


def _pallas(body, **kwargs):
    pin = lambda s: pltpu.HBM(s.shape, s.dtype) if len(s.shape) >= 2 else s
    out_shape = kwargs.pop("out_shape")
    out_shape = [pin(s) for s in out_shape] if isinstance(out_shape, (list, tuple)) else pin(out_shape)
    call = pl.pallas_call(body, out_shape=out_shape, **kwargs)

    def run(*operands):
        return call(*[pltpu.with_memory_space_constraint(o, pltpu.HBM) if o.ndim >= 2 else o for o in operands])

    return run


class _Exchange:
    def __init__(self, operands, out_shape, aliases, sems, start, finish):
        self.operands, self.out_shape, self.aliases, self.sems = list(operands), list(out_shape), dict(aliases), list(sems)
        self.start, self.finish = start, finish


def _run_exchange(comm, name):
    n_in, n_out = len(comm.operands), len(comm.out_shape)

    def body(*refs):
        ins, outs, sems = refs[:n_in], refs[n_in:n_in + n_out], refs[n_in + n_out:]
        comm.start(ins, outs, sems)
        comm.finish(ins, outs, sems)

    return _pallas(
        body, name=name, in_specs=[ANY] * n_in, out_specs=[ANY] * n_out, out_shape=comm.out_shape,
        input_output_aliases=comm.aliases, scratch_shapes=comm.sems,
    )(*comm.operands)


def _call(body, *, name, grid, in_specs, out_specs, out_shape, scratch_shapes, semantics, vmem_mib, args,
          aliases=None, comm=None, prefetch=()):
    aliases = dict(aliases or {})
    n_pre, n_in, n_out, n_scr = len(prefetch), len(in_specs), len(out_specs), len(scratch_shapes)
    c_in, c_out = (len(comm.operands), len(comm.out_shape)) if comm else (0, 0)
    c_shapes, c_sems, c_operands = (comm.out_shape, comm.sems, comm.operands) if comm else ([], [], [])

    def hosted(*refs):
        pre, refs = refs[:n_pre], refs[n_pre:]
        a = n_in
        b = a + c_in
        c = b + n_out
        d = c + c_out
        e = d + n_scr
        if comm is None:
            body(*pre, *refs)
            return
        ids = [pl.program_id(k) for k in range(len(grid))]
        first = functools.reduce(jnp.logical_and, [i == 0 for i in ids])
        last = functools.reduce(jnp.logical_and, [i == g - 1 for i, g in zip(ids, grid)])

        @pl.when(first)
        def _():
            comm.start(refs[a:b], refs[c:d], refs[e:])

        body(*pre, *refs[:a], *refs[b:c], *refs[d:e])

        @pl.when(last)
        def _():
            comm.finish(refs[a:b], refs[c:d], refs[e:])

    if comm:
        semantics = ("arbitrary",) * len(grid)
        for i, o in comm.aliases.items():
            aliases[n_pre + n_in + i] = n_out + o
    outs = _pallas(
        hosted, name=name,
        grid_spec=pltpu.PrefetchScalarGridSpec(
            num_scalar_prefetch=n_pre, grid=grid, in_specs=list(in_specs) + [ANY] * c_in,
            out_specs=list(out_specs) + [ANY] * c_out, scratch_shapes=list(scratch_shapes) + c_sems),
        out_shape=list(out_shape) + c_shapes, input_output_aliases=aliases,
        compiler_params=_params(semantics, vmem_mib),
    )(*prefetch, *args, *c_operands)
    return list(outs[:n_out]), list(outs[n_out:])


def _dot_nn(a, b):
    return jnp.dot(a, b, preferred_element_type=F32)


def _dot_nt(a, b):
    return lax.dot_general(a, b, (((1,), (1,)), ((), ())), preferred_element_type=F32)


def _dot_tn(a, b):
    return lax.dot_general(a, b, (((0,), (0,)), ((), ())), preferred_element_type=F32)


def _fold_rows(a):
    r, c = a.shape
    return jnp.sum(a.reshape(r // 8, 8, c), axis=0)


def _cast_bf16(a, chip, name, rows):
    r, c = a.shape

    def body(chip_ref, a_ref, o_ref):
        o_ref[...] = a_ref[...].astype(BF16)

    return _pallas(
        body, name=name,
        grid_spec=pltpu.PrefetchScalarGridSpec(
            num_scalar_prefetch=1, grid=(r // rows,),
            in_specs=[pl.BlockSpec((rows, c), lambda i, chip_ref: (i, 0))],
            out_specs=pl.BlockSpec((None, rows, c), lambda i, chip_ref: (chip_ref[0], i, 0))),
        out_shape=jax.ShapeDtypeStruct((N_SHARDS, r, c), BF16),
        compiler_params=_params(("parallel",), 32),
    )(chip, a)


def _mesh_place():
    x, y, c = lax.axis_index("x"), lax.axis_index("y"), lax.axis_index("c")
    return x, y, c, [(1 - x, y), (x, 1 - y), (1 - x, 1 - y)]


def _remote(src, dst, send_sem, recv_sem, to):
    return pltpu.make_async_remote_copy(src_ref=src, dst_ref=dst, send_sem=send_sem, recv_sem=recv_sem,
                                        device_id=to, device_id_type=MESH)


def _allgather_weights(bufs, phase="all", chunks=1):
    n = len(bufs)
    items = [(a, q) for q in range(chunks) for a in range(n)]

    def rows(item, core):
        a, q = item
        size = bufs[a].shape[1] // 2 // chunks
        return pl.ds(core * chunks * size + q * size, size)

    def sem(sems, which, item, k):
        a, q = item
        return sems[which].at[a * chunks + q, k]

    DIAGONAL = 2

    def to_neighbours(dst, sems, item):
        x, y, c, chips = _mesh_place()
        own = dst[item[0]].at[2 * x + y, rows(item, c)]
        return [_remote(own, own, sem(sems, 0, item, k), sem(sems, 1, item, k), (cx, cy, c))
                for k, (cx, cy) in enumerate(chips[:DIAGONAL])]

    def relayed(dst, sems, item):
        x, y, c, _ = _mesh_place()
        piece = dst[item[0]].at[2 * (x ^ (1 - c)) + (y ^ c), rows(item, c)]
        return _remote(piece, piece, sem(sems, 0, item, DIAGONAL), sem(sems, 1, item, DIAGONAL), (x ^ c, y ^ (1 - c), c))

    def start(ins, dst, sems):
        for item in items:
            for cp in ([relayed(dst, sems, item)] if phase == "diagonal" else to_neighbours(dst, sems, item)):
                cp.start()

    def finish(ins, dst, sems):
        x, y, c, chips = _mesh_place()
        sibling = (x, y, 1 - c)
        passed_on = []

        def landed_then_pass_on(item, k):
            cx, cy = chips[k]
            landed = dst[item[0]].at[2 * cx + cy, rows(item, c)]
            _remote(landed, landed, sem(sems, 0, item, k), sem(sems, 1, item, k), (cx, cy, c)).wait_recv()
            cp = _remote(landed, landed, sem(sems, 2, item, k), sem(sems, 3, item, k), sibling)
            cp.start()
            passed_on.append(cp)

        sent = []
        for item in items:
            if phase != "diagonal":
                for k in range(DIAGONAL):
                    landed_then_pass_on(item, k)
                sent += to_neighbours(dst, sems, item)
            if phase == "all":
                relayed(dst, sems, item).start()
        for item in items:
            if phase != "neighbours":
                landed_then_pass_on(item, DIAGONAL)
                sent.append(relayed(dst, sems, item))
        for k in {"all": (0, 1, 2), "neighbours": (0, 1), "diagonal": (2,)}[phase]:
            cx, cy = chips[k]
            for item in items:
                passed = dst[item[0]].at[2 * cx + cy, rows(item, 1 - c)]
                _remote(passed, passed, sem(sems, 2, item, k), sem(sems, 3, item, k), sibling).wait_recv()
        for cp in sent + passed_on:
            cp.wait_send()

    return _Exchange(bufs, [jax.ShapeDtypeStruct(b.shape, b.dtype) for b in bufs], {a: a for a in range(n)},
                     [pltpu.SemaphoreType.DMA((n * chunks, 3))] * 4, start, finish)


def _rope_tables(seq):
    half = ROPE_DIM // 2
    inv_freq = ROPE_THETA ** (-(2.0 * jnp.arange(half, dtype=F32)) / ROPE_DIM)
    ang = jnp.arange(seq, dtype=jnp.int32).astype(F32)[:, None] * inv_freq[None, :]
    cos, sin = jnp.cos(ang), jnp.sin(ang)
    pad = jnp.zeros((seq, HEAD_DIM - ROPE_DIM), F32)
    zeros = jnp.zeros((seq, half), F32)
    c_tab = jnp.concatenate([cos, cos, pad + 1.0], axis=1)
    up_tab = jnp.concatenate([-sin, zeros, pad], axis=1)
    down_tab = jnp.concatenate([zeros, sin, pad], axis=1)
    return c_tab, up_tab, down_tab


def _rotate_heads(t, c_tab, up_tab, down_tab):
    outs = []
    for h in range(t.shape[1] // HEAD_DIM):
        th = t[:, h * HEAD_DIM:(h + 1) * HEAD_DIM]
        up = pltpu.roll(th, HEAD_DIM - ROPE_DIM // 2, axis=1)
        down = pltpu.roll(th, ROPE_DIM // 2, axis=1)
        outs.append(th * c_tab + up * up_tab + down * down_tab)
    return outs[0] if len(outs) == 1 else jnp.concatenate(outs, axis=1)


def _to_pattern(slabs_ref, dst_ref, dil, dtype):
    n_slabs, rows, _ = slabs_ref.shape
    for s in range(n_slabs):
        for r in range(dil):
            dst_ref[r, :, s * 128:(s + 1) * 128] = slabs_ref[s, pl.ds(r, rows // dil, dil), :].astype(dtype)


def _from_pattern(src_ref, slabs_ref, dil):
    n_slabs, rows, _ = slabs_ref.shape
    for s in range(n_slabs):
        for r in range(dil):
            slabs_ref[s, pl.ds(r, rows // dil, dil), :] = src_ref[r, :, s * 128:(s + 1) * 128].astype(F32)


def _store_slabs(slabs_ref, value):
    for s in range(slabs_ref.shape[0]):
        slabs_ref[s] = value[:, s * 128:(s + 1) * 128]


def _in_proj_qkv(x, w_in_g, tabs, comm=None):
    seq = x.shape[0]
    tm, tn = 512, SHARD_IN
    heads = tn // HEAD_DIM
    k_heads_in_second = 2 * D_ATTN // HEAD_DIM - heads
    d4, d16 = DILATIONS[1], DILATIONS[2]

    def body(x_ref, w_ref, c_ref, up_ref, down_ref, o1_ref, o4_ref, o16_ref, res_ref):
        shard = pl.program_id(0)
        xb = x_ref[...].astype(BF16)
        group = 4 * HEAD_DIM
        accs = [_dot_nn(xb, w_ref[:, g * group:(g + 1) * group]) for g in range(tn // group)]

        plain = shard == 1
        c_plain = jnp.where(plain, 1.0, c_ref[...])
        up_plain = jnp.where(plain, 0.0, up_ref[...])
        down_plain = jnp.where(plain, 0.0, down_ref[...])
        for h in range(heads):
            lanes = (h * HEAD_DIM) % group
            th = accs[h * HEAD_DIM // group][:, lanes:lanes + HEAD_DIM]
            if h < k_heads_in_second:
                th = _rotate_heads(th, c_ref[...], up_ref[...], down_ref[...])
            else:
                th = _rotate_heads(th, c_plain, up_plain, down_plain)
            res_ref[h] = th
            o1_ref[:, h * HEAD_DIM:(h + 1) * HEAD_DIM] = th.astype(BF16)
        _to_pattern(res_ref, o4_ref, d4, BF16)
        _to_pattern(res_ref, o16_ref, d16, BF16)

    tab_spec = pl.BlockSpec((tm, HEAD_DIM), lambda s, i: (i, 0))
    (o1, o4, o16), exchanged = _call(
        body, name="in_proj_qkv", grid=(D_QKV // tn, seq // tm),
        in_specs=[pl.BlockSpec((tm, D_MODEL), lambda s, i: (i, 0)),
                  pl.BlockSpec((None, D_MODEL, tn), lambda s, i: (s, 0, 0)),
                  tab_spec, tab_spec, tab_spec],
        out_specs=[pl.BlockSpec((tm, tn), lambda s, i: (i, s)),
                   pl.BlockSpec((d4, tm // d4, tn), lambda s, i: (0, i, s)),
                   pl.BlockSpec((d16, tm // d16, tn), lambda s, i: (0, i, s))],
        out_shape=[jax.ShapeDtypeStruct((seq, D_QKV), BF16),
                   jax.ShapeDtypeStruct((d4, seq // d4, D_QKV), BF16),
                   jax.ShapeDtypeStruct((d16, seq // d16, D_QKV), BF16)],
        scratch_shapes=[pltpu.VMEM((heads, tm, HEAD_DIM), F32)],
        semantics=("parallel", "parallel"), vmem_mib=52, args=(x, w_in_g, *tabs), comm=comm)
    return [o1[None], o4, o16], exchanged


def _in_proj_pool_gate(x, w_in_g, comm=None):
    seq = x.shape[0]
    tm, tn = 512, SHARD_IN
    first_shard = D_QKV // tn

    def body(x_ref, w_ref, o_ref):
        o_ref[...] = _dot_nn(x_ref[...].astype(BF16), w_ref[...]).astype(BF16)

    (hug,), exchanged = _call(
        body, name="in_proj_pool_gate", grid=(D_UG // tn, seq // tm),
        in_specs=[pl.BlockSpec((tm, D_MODEL), lambda s, i: (i, 0)),
                  pl.BlockSpec((None, D_MODEL, tn), lambda s, i: (s + first_shard, 0, 0))],
        out_specs=[pl.BlockSpec((tm, tn), lambda s, i: (i, s))],
        out_shape=[jax.ShapeDtypeStruct((seq, D_UG), BF16)],
        scratch_shapes=[], semantics=("parallel", "parallel"), vmem_mib=48, args=(x, w_in_g), comm=comm)
    return hug, exchanged


W_IN_CHUNKS = 4


def _in_proj_plan(x, y):
    shards = [2 * x + y, 2 * (1 - x) + y, 2 * x + (1 - y), 2 * (1 - x) + (1 - y)]
    last_row = jnp.int32(-2)

    def table(active, col_of):
        cols, rows = [], []
        first_col = functools.reduce(lambda acc, j: jnp.where(active[j], col_of(shards[j]), acc), reversed(range(4)),
                                     jnp.int32(0))
        held_col, seen = first_col, jnp.bool_(False)
        for j in range(4):
            cols.append(jnp.where(active[j], col_of(shards[j]), held_col))
            rows.append(jnp.where(active[j], -1, jnp.where(seen, last_row, 0)))
            held_col = jnp.where(active[j], col_of(shards[j]), held_col)
            seen = jnp.logical_or(seen, active[j])
        return cols, rows

    q_cols, q_rows = table([s < 2 for s in shards], lambda s: s)
    h_cols, h_rows = table([s >= 2 for s in shards], lambda s: s - 2)
    return jnp.stack([jnp.asarray(v, jnp.int32) for v in shards + q_cols + q_rows + h_cols + h_rows])


def _in_proj_gathering(x, w_buf, tabs, plan):
    seq = x.shape[0]
    tm, tn = 512, SHARD_IN
    n_tiles = seq // tm
    heads = tn // HEAD_DIM
    k_heads_in_second = 2 * D_ATTN // HEAD_DIM - heads
    d4, d16 = DILATIONS[1], DILATIONS[2]
    half_rows = D_MODEL // 2
    chunk = half_rows // W_IN_CHUNKS
    DIAGONAL = 2

    def body(plan_ref, x_ref, w_in_ref, c_ref, up_ref, down_ref, o1_ref, o4_ref, o16_ref, hug_ref, w_ref,
             wbuf_ref, res_ref, w_sem, ici_send, ici_recv, d2d_send, d2d_recv):
        j, i = pl.program_id(0), pl.program_id(1)
        mx, my, mc, chips = _mesh_place()
        sibling = (mx, my, 1 - mc)
        rows = lambda core, q: pl.ds(core * half_rows + q * chunk, chunk)
        chip_of = lambda k: 2 * chips[k][0] + chips[k][1]

        def to_neighbour(k, q):
            piece = w_ref.at[2 * mx + my, rows(mc, q)]
            return _remote(piece, piece, ici_send.at[q, k], ici_recv.at[q, k], (*chips[k], mc))

        def relay(q):
            piece = w_ref.at[2 * (mx ^ (1 - mc)) + (my ^ mc), rows(mc, q)]
            return _remote(piece, piece, ici_send.at[q, DIAGONAL], ici_recv.at[q, DIAGONAL], (mx ^ mc, my ^ (1 - mc), mc))

        def arrival(k, q):
            piece = w_ref.at[chip_of(k), rows(mc, q)]
            return _remote(piece, piece, ici_send.at[q, k], ici_recv.at[q, k], (*chips[k], mc))

        def to_sibling(k, q, core):
            piece = w_ref.at[chip_of(k), rows(core, q)]
            return _remote(piece, piece, d2d_send.at[q, k], d2d_recv.at[q, k], sibling)

        first_tile = i == 0

        @pl.when(jnp.logical_and(j == 0, first_tile))
        def _():
            for q in range(W_IN_CHUNKS):
                for k in range(DIAGONAL):
                    to_neighbour(k, q).start()

        for step in (1, 2, 3):
            @pl.when(jnp.logical_and(j == step, first_tile))
            def _(k=step - 1):
                for q in range(W_IN_CHUNKS):
                    arrival(k, q).wait_recv()
                    to_sibling(k, q, mc).start()
                if k == 1:
                    for q in range(W_IN_CHUNKS):
                        relay(q).start()
                for q in range(W_IN_CHUNKS):
                    to_sibling(k, q, 1 - mc).wait_recv()

        shard = plan_ref[j]

        @pl.when(first_tile)
        def _():
            cp = pltpu.make_async_copy(w_ref.at[shard], wbuf_ref, w_sem)
            cp.start()
            cp.wait()

        xb = x_ref[...].astype(BF16)
        group = 4 * HEAD_DIM
        accs = [_dot_nn(xb, wbuf_ref[:, g * group:(g + 1) * group]) for g in range(tn // group)]

        def emit_qkv(rotated_heads):
            for h in range(heads):
                lanes = (h * HEAD_DIM) % group
                th = accs[h * HEAD_DIM // group][:, lanes:lanes + HEAD_DIM]
                if h < rotated_heads:
                    th = _rotate_heads(th, c_ref[...], up_ref[...], down_ref[...])
                res_ref[h] = th
                o1_ref[:, h * HEAD_DIM:(h + 1) * HEAD_DIM] = th.astype(BF16)
            _to_pattern(res_ref, o4_ref, d4, BF16)
            _to_pattern(res_ref, o16_ref, d16, BF16)

        @pl.when(shard == 0)
        def _():
            emit_qkv(heads)

        @pl.when(shard == 1)
        def _():
            emit_qkv(k_heads_in_second)

        @pl.when(shard >= 2)
        def _():
            for g, acc in enumerate(accs):
                hug_ref[:, g * group:(g + 1) * group] = acc.astype(BF16)

        @pl.when(jnp.logical_and(j == 3, i == n_tiles - 1))
        def _():
            for q in range(W_IN_CHUNKS):
                for k in range(DIAGONAL):
                    to_neighbour(k, q).wait_send()
                relay(q).wait_send()
                for k in range(DIAGONAL + 1):
                    to_sibling(k, q, mc).wait_send()

    def held(base, last):
        return lambda j, i, plan_ref: jnp.where(plan_ref[base + j] == -1, i,
                                                jnp.where(plan_ref[base + j] == -2, last, 0))

    q_row, h_row = held(8, n_tiles - 1), held(16, n_tiles - 1)
    tab_spec = pl.BlockSpec((tm, HEAD_DIM), lambda j, i, plan_ref: (i, 0))
    sems = [pltpu.SemaphoreType.DMA((W_IN_CHUNKS, 3))] * 4
    o1, o4, o16, hug, w_in_g = _pallas(
        body, name="in_proj_gathering",
        grid_spec=pltpu.PrefetchScalarGridSpec(
            num_scalar_prefetch=1, grid=(N_SHARDS, n_tiles),
            in_specs=[pl.BlockSpec((tm, D_MODEL), lambda j, i, plan_ref: (i, 0)), ANY, tab_spec, tab_spec, tab_spec],
            out_specs=[pl.BlockSpec((tm, tn), lambda j, i, p: (q_row(j, i, p), p[4 + j])),
                       pl.BlockSpec((d4, tm // d4, tn), lambda j, i, p: (0, q_row(j, i, p), p[4 + j])),
                       pl.BlockSpec((d16, tm // d16, tn), lambda j, i, p: (0, q_row(j, i, p), p[4 + j])),
                       pl.BlockSpec((tm, tn), lambda j, i, p: (h_row(j, i, p), p[12 + j])),
                       ANY],
            scratch_shapes=[pltpu.VMEM((D_MODEL, tn), BF16), pltpu.VMEM((heads, tm, HEAD_DIM), F32),
                            pltpu.SemaphoreType.DMA(())] + sems),
        out_shape=[jax.ShapeDtypeStruct((seq, D_QKV), BF16),
                   jax.ShapeDtypeStruct((d4, seq // d4, D_QKV), BF16),
                   jax.ShapeDtypeStruct((d16, seq // d16, D_QKV), BF16),
                   jax.ShapeDtypeStruct((seq, D_UG), BF16),
                   jax.ShapeDtypeStruct(w_buf.shape, w_buf.dtype)],
        input_output_aliases={2: 4},
        compiler_params=_params(("arbitrary", "arbitrary"), 52),
    )(plan, x, w_buf, *tabs)
    return [o1[None], o4, o16], hug, w_in_g


def _band_masks():
    row = lax.broadcasted_iota(jnp.int32, (KEY_BLOCK, KEY_BLOCK), 0)
    col = lax.broadcasted_iota(jnp.int32, (KEY_BLOCK, KEY_BLOCK), 1)
    return col <= row, col >= row


def _attn_fwd(qkv, name, comm=None):
    dil, n, _ = qkv.shape
    scale = HEAD_DIM ** -0.5
    lo, hi = slice(0, KEY_BLOCK), slice(KEY_BLOCK, CHUNK)

    def body(q_ref, k_ref, v_ref, kb_ref, vb_ref, o_ref, st_ref):
        i = pl.program_id(1)
        cur_mask, prev_mask = _band_masks()
        before_mask = jnp.logical_and(prev_mask, i > 0)
        lane = lax.broadcasted_iota(jnp.int32, (KEY_BLOCK, STAT_LANES), 1)
        tasks = [(rows, h) for rows in (lo, hi) for h in range(N_HEADS)]
        head = lambda h: slice(h * HEAD_DIM, (h + 1) * HEAD_DIM)

        def prev_of(rows, h):
            if rows is lo:
                return kb_ref[:, head(h)], vb_ref[:, head(h)], before_mask
            return k_ref[lo, head(h)], v_ref[lo, head(h)], prev_mask

        scores = []
        for rows, h in tasks:
            q = q_ref[rows, head(h)]
            scores.append((_dot_nt(q, prev_of(rows, h)[0]), _dot_nt(q, k_ref[rows, head(h)])))
        probs = []
        for (rows, h), (qk_prev, qk_cur) in zip(tasks, scores):
            s_prev = jnp.where(prev_of(rows, h)[2], qk_prev * scale, NEG)
            s_cur = jnp.where(cur_mask, qk_cur * scale, NEG)
            m = jnp.max(jnp.maximum(s_prev, s_cur), axis=-1, keepdims=True)
            p_prev = jnp.exp(s_prev - m)
            p_cur = jnp.exp(s_cur - m)
            den = jnp.sum(p_prev + p_cur, axis=-1, keepdims=True)
            probs.append((p_prev.astype(BF16), p_cur.astype(BF16), den, m + jnp.log(den)))
        stats = [jnp.zeros((KEY_BLOCK, STAT_LANES), F32), jnp.zeros((KEY_BLOCK, STAT_LANES), F32)]
        for (rows, h), (p_prev, p_cur, den, lse) in zip(tasks, probs):
            o = _dot_nn(p_cur, v_ref[rows, head(h)]) + _dot_nn(p_prev, prev_of(rows, h)[1])
            o_ref[rows, head(h)] = (o / den).astype(BF16)
            b = 0 if rows is lo else 1
            stats[b] = jnp.where(lane == h, lse, stats[b])
        st_ref[lo, :] = stats[0]
        st_ref[hi, :] = stats[1]

    main = lambda cb: pl.BlockSpec((None, CHUNK, D_ATTN), lambda r, i: (r, i, cb))
    before = lambda cb: pl.BlockSpec((None, KEY_BLOCK, D_ATTN), lambda r, i: (r, jnp.maximum(2 * i - 1, 0), cb))
    (o, st), exchanged = _call(
        body, name=name, grid=(dil, n // CHUNK),
        in_specs=[main(0), main(1), main(2), before(1), before(2)],
        out_specs=[main(0), pl.BlockSpec((None, CHUNK, STAT_LANES), lambda r, i: (r, i, 0))],
        out_shape=[jax.ShapeDtypeStruct((dil, n, D_ATTN), BF16), jax.ShapeDtypeStruct((dil, n, STAT_LANES), F32)],
        scratch_shapes=[], semantics=("parallel", "parallel"), vmem_mib=40, args=(qkv, qkv, qkv, qkv, qkv), comm=comm)
    return o, st, exchanged


def _attn_bwd(qkv, do, stats, name, comm=None):
    dil, n, _ = qkv.shape
    n_blocks = n // KEY_BLOCK
    last = n // CHUNK - 1
    scale = HEAD_DIM ** -0.5
    lo, hi = slice(0, KEY_BLOCK), slice(KEY_BLOCK, CHUNK)

    def body(q_ref, k_ref, v_ref, kb_ref, vb_ref, qa_ref, do_ref, doa_ref, st_ref, sta_ref, dq_ref, dk_ref, dv_ref):
        i = pl.program_id(1)
        cur_mask, prev_mask = _band_masks()
        before_mask = jnp.logical_and(prev_mask, i > 0)
        after_mask = jnp.logical_and(prev_mask, i < last)

        rows_cat = lambda a, b: jnp.concatenate([a, b], axis=0)
        masks = (jnp.concatenate([before_mask, cur_mask], axis=1), jnp.concatenate([prev_mask, cur_mask], axis=1),
                 after_mask)

        def operands(h):
            cols = slice(h * HEAD_DIM, (h + 1) * HEAD_DIM)
            lse_c, del_c = slice(h, h + 1), slice(N_HEADS + h, N_HEADS + h + 1)
            q = (q_ref[lo, cols], q_ref[hi, cols], qa_ref[:, cols])
            do = (do_ref[lo, cols], do_ref[hi, cols], doa_ref[:, cols])
            keys = (rows_cat(kb_ref[:, cols], k_ref[lo, cols]), k_ref[:, cols], k_ref[hi, cols])
            vals = (rows_cat(vb_ref[:, cols], v_ref[lo, cols]), v_ref[:, cols], v_ref[hi, cols])
            st = ((st_ref[lo, lse_c], st_ref[lo, del_c]), (st_ref[hi, lse_c], st_ref[hi, del_c]),
                  (sta_ref[:, lse_c], sta_ref[:, del_c]))
            return cols, q, do, keys, vals, st

        group = N_HEADS // 2
        for first_head in range(0, N_HEADS, group):
            heads = range(first_head, first_head + group)
            raw = {}
            for h in heads:
                _, q, do, keys, vals, _ = operands(h)
                raw[h] = [(_dot_nt(q[j], keys[j]), _dot_nt(do[j], vals[j])) for j in range(3)]
            grads = {}
            for h in heads:
                st = operands(h)[5]
                grads[h] = []
                for j in range(3):
                    qk, dp = raw[h][j]
                    lse, delta = st[j]
                    p = jnp.exp(jnp.where(masks[j], qk * scale, NEG) - lse)
                    grads[h].append((p.astype(BF16), (p * (dp - delta) * scale).astype(BF16)))
            for h in heads:
                cols, q, do, keys, _, _ = operands(h)
                (p0, ds0), (p1, ds1), (pa, dsa) = grads[h]
                own, nxt = slice(KEY_BLOCK, CHUNK), slice(0, KEY_BLOCK)

                def put(ref, rows, val, cols=cols):
                    ref[rows, cols] = val.astype(ref.dtype)

                put(dq_ref, lo, _dot_nn(ds0, keys[0]))
                put(dq_ref, hi, _dot_nn(ds1, keys[1]))
                put(dk_ref, lo, _dot_tn(rows_cat(ds0[:, own], ds1[:, nxt]), q_ref[:, cols]))
                put(dk_ref, hi, _dot_tn(rows_cat(ds1[:, own], dsa), rows_cat(q[1], q[2])))
                put(dv_ref, lo, _dot_tn(rows_cat(p0[:, own], p1[:, nxt]), do_ref[:, cols]))
                put(dv_ref, hi, _dot_tn(rows_cat(p1[:, own], pa), rows_cat(do[1], do[2])))

    def spec(rows, width, row_of, cb):
        return pl.BlockSpec((None, rows, width), lambda r, i: (r, row_of(i), cb))

    same = lambda i: i
    before = lambda i: jnp.maximum(2 * i - 1, 0)
    after = lambda i: jnp.minimum(2 * i + 2, n_blocks - 1)
    out = spec(CHUNK, D_ATTN, same, 0)
    return _call(
        body, name=name, grid=(dil, n // CHUNK),
        in_specs=[spec(CHUNK, D_ATTN, same, 0), spec(CHUNK, D_ATTN, same, 1), spec(CHUNK, D_ATTN, same, 2),
                  spec(KEY_BLOCK, D_ATTN, before, 1), spec(KEY_BLOCK, D_ATTN, before, 2),
                  spec(KEY_BLOCK, D_ATTN, after, 0),
                  spec(CHUNK, D_ATTN, same, 0), spec(KEY_BLOCK, D_ATTN, after, 0),
                  spec(CHUNK, STAT_LANES, same, 0), spec(KEY_BLOCK, STAT_LANES, after, 0)],
        out_specs=[out, out, out],
        out_shape=[jax.ShapeDtypeStruct((dil, n, D_ATTN), BF16)] * 3,
        scratch_shapes=[], semantics=("parallel", "parallel"), vmem_mib=40,
        args=(qkv, qkv, qkv, qkv, qkv, qkv, do, do, stats, stats), comm=comm)


def _window_sums(ext, window, backward):
    rows = ext.shape[0]
    acc, span = ext, 1
    while span < window:
        acc = acc + pltpu.roll(acc, (rows - span) if backward else span, axis=0)
        span *= 2
    return acc


def _mix_gate(o_list, st_list, hug, w_pool_g, pool_scale):
    seq = hug.shape[0]
    tm = 256
    halo_blocks = tm // POOL_HALO
    d4, d16 = DILATIONS[1], DILATIONS[2]

    def body(o1_ref, o4_ref, o16_ref, l1_ref, l4_ref, l16_ref, u_ref, halo_ref, ga_ref, gp_ref, wp_ref, sc_ref,
             y_ref, mix_ref, lse_ref, pooled_ref, n4_ref, n16_ref, nl4_ref, nl16_ref):
        i = pl.program_id(0)
        _from_pattern(o4_ref, n4_ref, d4)
        _from_pattern(o16_ref, n16_ref, d16)
        _from_pattern(l4_ref, nl4_ref, d4)
        _from_pattern(l16_ref, nl16_ref, d16)
        la, lb, lc = l1_ref[...], nl4_ref[0], nl16_ref[0]
        mx = jnp.maximum(jnp.maximum(la, lb), lc)
        ea, eb, ec = jnp.exp(la - mx), jnp.exp(lb - mx), jnp.exp(lc - mx)
        tot = ea + eb + ec
        lse_ref[...] = mx + jnp.log(tot)
        wa, wb, wc = ea / tot, eb / tot, ec / tot
        ga = ga_ref[...].astype(F32)
        silu_a = ga * jax.nn.sigmoid(ga)
        for h in range(N_HEADS):
            cols = slice(h * HEAD_DIM, (h + 1) * HEAD_DIM)
            hc = slice(h, h + 1)
            attn = wa[:, hc] * o1_ref[:, cols].astype(F32) + wb[:, hc] * n4_ref[h] + wc[:, hc] * n16_ref[h]
            mix_ref[:, cols] = attn.astype(BF16)
            y_ref[:, cols] = (attn * silu_a[:, cols]).astype(BF16)

        u = u_ref[...].astype(F32)
        halo = jnp.where(i > 0, halo_ref[...].astype(F32), 0.0)
        ext = jnp.concatenate([halo, u], axis=0)
        pos = i * tm + lax.broadcasted_iota(jnp.int32, (tm, 1), 0)
        gp = gp_ref[...].astype(F32)
        gated_scale = sc_ref[...] * (gp * jax.nn.sigmoid(gp))
        for g, window in enumerate(POOL_WINDOWS):
            cols = slice(g * POOL_GROUP_DIM, (g + 1) * POOL_GROUP_DIM)
            sums = _window_sums(ext[:, cols], window, backward=False)[POOL_HALO:, :]
            count = jnp.minimum(pos + 1, window).astype(F32)
            pooled = (sums / count - u[:, cols]).astype(BF16)
            pooled_ref[:, cols] = pooled
            pre = _dot_nn(pooled, wp_ref[g])
            out_cols = slice(D_ATTN + g * POOL_GROUP_DIM, D_ATTN + (g + 1) * POOL_GROUP_DIM)
            mix_ref[:, out_cols] = pre.astype(BF16)
            y_ref[:, out_cols] = (pre * gated_scale[:, cols]).astype(BF16)

    row = lambda width, cb=0: pl.BlockSpec((tm, width), lambda i: (i, cb))
    pat = lambda d, width: pl.BlockSpec((d, tm // d, width), lambda i: (0, i, 0))
    return _pallas(
        body, name="mix_gate", grid=(seq // tm,),
        in_specs=[row(D_ATTN), pat(d4, D_ATTN), pat(d16, D_ATTN),
                  row(STAT_LANES), pat(d4, STAT_LANES), pat(d16, STAT_LANES),
                  row(D_POOL),
                  pl.BlockSpec((POOL_HALO, D_POOL), lambda i: (jnp.maximum(i * halo_blocks - 1, 0), 0)),
                  row(D_ATTN, 1), row(D_POOL, 2),
                  pl.BlockSpec((len(POOL_WINDOWS), POOL_GROUP_DIM, POOL_GROUP_DIM), lambda i: (0, 0, 0)),
                  pl.BlockSpec((1, D_POOL), lambda i: (0, 0))],
        out_specs=[row(D_MODEL), row(D_MODEL), row(STAT_LANES), row(D_POOL)],
        out_shape=[jax.ShapeDtypeStruct((seq, D_MODEL), BF16), jax.ShapeDtypeStruct((seq, D_MODEL), BF16),
                   jax.ShapeDtypeStruct((seq, STAT_LANES), F32), jax.ShapeDtypeStruct((seq, D_POOL), BF16)],
        scratch_shapes=[pltpu.VMEM((N_HEADS, tm, HEAD_DIM), F32), pltpu.VMEM((N_HEADS, tm, HEAD_DIM), F32),
                        pltpu.VMEM((1, tm, STAT_LANES), F32), pltpu.VMEM((1, tm, STAT_LANES), F32)],
        compiler_params=_params(("parallel",), 48),
    )(o_list[0][0], o_list[1], o_list[2], st_list[0][0], st_list[1], st_list[2],
      hug, hug, hug, hug, w_pool_g, pool_scale)


def _out_proj_loss(y, w_out_g, x, target, gain, bias):
    seq = x.shape[0]
    tm = 512

    def body(y_ref, w_ref, x_ref, t_ref, g_ref, b_ref, dz_ref, dzb_ref, gg_ref, gb_ref, loss_ref):
        @pl.when(pl.program_id(0) == 0)
        def _():
            gg_ref[...] = jnp.zeros_like(gg_ref)
            gb_ref[...] = jnp.zeros_like(gb_ref)
            loss_ref[...] = jnp.zeros_like(loss_ref)

        halves = [slice(0, tm // 2), slice(tm // 2, tm)]
        projected = [_dot_nn(y_ref[rows, :], w_ref[...]) for rows in halves]
        for rows, out in zip(halves, projected):
            z = DEEPNORM_ALPHA * x_ref[rows, :] + out
            mu = jnp.mean(z, axis=-1, keepdims=True)
            zc = z - mu
            rstd = lax.rsqrt(jnp.mean(zc * zc, axis=-1, keepdims=True) + LN_EPS)
            xhat = zc * rstd
            gain_v = g_ref[...]
            diff = xhat * gain_v + b_ref[...] - t_ref[rows, :]
            sq = _fold_rows(diff * diff)
            part = sq[:, :128]
            for k in range(1, D_MODEL // 128):
                part = part + sq[:, k * 128:(k + 1) * 128]
            loss_ref[...] += part
            dln = diff * (1.0 / D_MODEL)
            gg_ref[...] += _fold_rows(dln * xhat)
            gb_ref[...] += _fold_rows(dln)
            dxhat = dln * gain_v
            dz = rstd * (dxhat - jnp.mean(dxhat, axis=-1, keepdims=True)
                         - xhat * jnp.mean(dxhat * xhat, axis=-1, keepdims=True))
            dz_ref[rows, :] = dz
            dzb_ref[rows, :] = dz.astype(BF16)

    row = lambda: pl.BlockSpec((tm, D_MODEL), lambda i: (i, 0))
    vec = lambda: pl.BlockSpec((1, D_MODEL), lambda i: (0, 0))
    acc = lambda width: pl.BlockSpec((8, width), lambda i: (0, 0))
    return _pallas(
        body, name="out_proj_loss", grid=(seq // tm,),
        in_specs=[row(), pl.BlockSpec((D_MODEL, D_MODEL), lambda i: (0, 0), pipeline_mode=pl.Buffered(1)),
                  row(), row(), vec(), vec()],
        out_specs=[row(), row(), acc(D_MODEL), acc(D_MODEL), acc(128)],
        out_shape=[jax.ShapeDtypeStruct((seq, D_MODEL), F32), jax.ShapeDtypeStruct((seq, D_MODEL), BF16),
                   jax.ShapeDtypeStruct((8, D_MODEL), F32), jax.ShapeDtypeStruct((8, D_MODEL), F32),
                   jax.ShapeDtypeStruct((8, 128), F32)],
        compiler_params=_params(("arbitrary",), 56),
    )(y, w_out_g.reshape(D_MODEL, D_MODEL), x, target, gain, bias)


def _dy_gate_bwd(dzb, w_out_g, hug, mixpre, pool_scale, lse_all):
    seq = dzb.shape[0]
    tm = 256
    d4, d16 = DILATIONS[1], DILATIONS[2]

    def body(dz_ref, w_ref, ga_ref, gp_ref, mix_ref, sc_ref, lse_ref,
             dh_ref, dpo_ref, do1_ref, do4_ref, do16_ref, st1_ref, st4_ref, st16_ref, da_ref, st_ref):
        dy = _dot_nt(dz_ref[...], w_ref[...])
        ga = ga_ref[...].astype(F32)
        sig = jax.nn.sigmoid(ga)
        attn = mix_ref[:, :D_ATTN].astype(F32)
        dya = dy[:, :D_ATTN]
        dattn = dya * (ga * sig)
        dh_ref[:, :D_ATTN] = (dya * attn * (sig * (1.0 + ga * (1.0 - sig)))).astype(BF16)
        _store_slabs(da_ref, dattn)
        lane = lax.broadcasted_iota(jnp.int32, (tm, STAT_LANES), 1)
        stats = lse_ref[...]
        prod = dattn * attn
        for h in range(N_HEADS):
            delta = jnp.sum(prod[:, h * HEAD_DIM:(h + 1) * HEAD_DIM], axis=-1, keepdims=True)
            stats = jnp.where(lane == N_HEADS + h, delta, stats)
        st_ref[0] = stats
        do1_ref[...] = dattn.astype(BF16)
        st1_ref[...] = stats
        _to_pattern(da_ref, do4_ref, d4, BF16)
        _to_pattern(da_ref, do16_ref, d16, BF16)
        _to_pattern(st_ref, st4_ref, d4, F32)
        _to_pattern(st_ref, st16_ref, d16, F32)

        gp = gp_ref[...].astype(F32)
        sig = jax.nn.sigmoid(gp)
        dyp = dy[:, D_ATTN:]
        dpo_ref[...] = (dyp * (gp * sig)).astype(BF16)
        dh_ref[:, D_ATTN:] = (dyp * (mix_ref[:, D_ATTN:].astype(F32) * sc_ref[...])
                              * (sig * (1.0 + gp * (1.0 - sig)))).astype(BF16)

    row = lambda width, cb=0: pl.BlockSpec((tm, width), lambda i: (i, cb))
    pat = lambda d, width: pl.BlockSpec((d, tm // d, width), lambda i: (0, i, 0))
    pat_shape = lambda d, width, dtype: jax.ShapeDtypeStruct((d, seq // d, width), dtype)
    outs = _pallas(
        body, name="dy_gate_bwd", grid=(seq // tm,),
        in_specs=[row(D_MODEL), pl.BlockSpec((D_MODEL, D_MODEL), lambda i: (0, 0)),
                  row(D_ATTN, 1), row(D_POOL, 2), row(D_MODEL), pl.BlockSpec((1, D_POOL), lambda i: (0, 0)),
                  row(STAT_LANES)],
        out_specs=[row(D_MODEL, D_IN // D_MODEL - 1), row(D_POOL),
                   row(D_ATTN), pat(d4, D_ATTN), pat(d16, D_ATTN),
                   row(STAT_LANES), pat(d4, STAT_LANES), pat(d16, STAT_LANES)],
        out_shape=[jax.ShapeDtypeStruct((seq, D_IN), BF16), jax.ShapeDtypeStruct((seq, D_POOL), BF16),
                   jax.ShapeDtypeStruct((seq, D_ATTN), BF16), pat_shape(d4, D_ATTN, BF16), pat_shape(d16, D_ATTN, BF16),
                   jax.ShapeDtypeStruct((seq, STAT_LANES), F32), pat_shape(d4, STAT_LANES, F32),
                   pat_shape(d16, STAT_LANES, F32)],
        scratch_shapes=[pltpu.VMEM((N_HEADS, tm, HEAD_DIM), F32), pltpu.VMEM((1, tm, STAT_LANES), F32)],
        compiler_params=_params(("parallel",), 48),
    )(dzb, w_out_g.reshape(D_MODEL, D_MODEL), hug, hug, mixpre, pool_scale, lse_all)
    dh, dpo, do1, do4, do16, st1, st4, st16 = outs
    return dh, dpo, [do1[None], do4, do16], [st1[None], st4, st16]


def _pool_bwd(dh, dpo, mixpre, pooled, w_pool_g, pool_scale):
    seq = dpo.shape[0]
    tm = 256
    halo_blocks = tm // POOL_HALO
    last = seq // tm - 1
    n_groups = len(POOL_WINDOWS)

    def body(dh_in_ref, dpo_ref, halo_ref, pre_ref, pooled_ref, wp_ref, sc_ref, du_ref, gw_ref, gs_ref):
        i = pl.program_id(0)

        @pl.when(i == 0)
        def _():
            gw_ref[...] = jnp.zeros_like(gw_ref)
            gs_ref[...] = jnp.zeros_like(gs_ref)

        dpo = dpo_ref[...].astype(F32)
        scale = sc_ref[...]
        gs_ref[...] += _fold_rows(dpo * pre_ref[...].astype(F32))
        halo = jnp.where(i < last, halo_ref[...].astype(F32), 0.0)
        dpw = (jnp.concatenate([dpo, halo], axis=0) * scale).astype(BF16)
        pos = i * tm + lax.broadcasted_iota(jnp.int32, (tm + POOL_HALO, 1), 0)
        for g, window in enumerate(POOL_WINDOWS):
            cols = slice(g * POOL_GROUP_DIM, (g + 1) * POOL_GROUP_DIM)
            dpw_g = dpw[:, cols]
            gw_ref[g] += _dot_tn(pooled_ref[:, cols], dpw_g[:tm, :])
            dpooled = _dot_nt(dpw_g, wp_ref[g])
            count = jnp.minimum(pos + 1, window).astype(F32)
            sums = _window_sums(dpooled / count, window, backward=True)
            du_ref[:, cols] = (sums[:tm, :] - dpooled[:tm, :]).astype(BF16)

    row = lambda width, cb=0: pl.BlockSpec((tm, width), lambda i: (i, cb))
    return _pallas(
        body, name="pool_bwd", grid=(seq // tm,),
        in_specs=[ANY, row(D_POOL),
                  pl.BlockSpec((POOL_HALO, D_POOL),
                               lambda i: (jnp.minimum((i + 1) * halo_blocks, seq // POOL_HALO - 1), 0)),
                  row(D_POOL, 1), row(D_POOL),
                  pl.BlockSpec((n_groups, POOL_GROUP_DIM, POOL_GROUP_DIM), lambda i: (0, 0, 0)),
                  pl.BlockSpec((1, D_POOL), lambda i: (0, 0))],
        out_specs=[row(D_POOL, D_QKV // D_POOL),
                   pl.BlockSpec((n_groups, POOL_GROUP_DIM, POOL_GROUP_DIM), lambda i: (0, 0, 0)),
                   pl.BlockSpec((8, D_POOL), lambda i: (0, 0))],
        out_shape=[jax.ShapeDtypeStruct(dh.shape, dh.dtype),
                   jax.ShapeDtypeStruct((n_groups, POOL_GROUP_DIM, POOL_GROUP_DIM), F32),
                   jax.ShapeDtypeStruct((8, D_POOL), F32)],
        input_output_aliases={0: 0},
        compiler_params=_params(("arbitrary",), 40),
    )(dh, dpo, dpo, mixpre, pooled, w_pool_g, pool_scale)


def _sum_patterns(dh, parts, tabs, unrotate, col_block, name, comm=None):
    seq = dh.shape[0]
    tm, tn = 256, D_ATTN
    per = D_ATTN // tn
    d4, d16 = DILATIONS[1], DILATIONS[2]

    def body(dh_in_ref, a1_ref, a4_ref, a16_ref, ct_ref, up_ref, down_ref, o_ref, n4_ref, n16_ref):
        _from_pattern(a4_ref, n4_ref, d4)
        _from_pattern(a16_ref, n16_ref, d16)
        for s in range(tn // HEAD_DIM):
            cols = slice(s * HEAD_DIM, (s + 1) * HEAD_DIM)
            tot = a1_ref[:, cols].astype(F32) + n4_ref[s] + n16_ref[s]
            if unrotate:
                tot = _rotate_heads(tot, ct_ref[...], -up_ref[...], -down_ref[...])
            o_ref[:, cols] = tot.astype(BF16)

    tab = pl.BlockSpec((tm, HEAD_DIM), lambda i, j: (i, 0))
    pat = lambda d: pl.BlockSpec((d, tm // d, tn), lambda i, j: (0, i, j))
    (dh,), exchanged = _call(
        body, name=name, grid=(seq // tm, per),
        in_specs=[ANY, pl.BlockSpec((tm, tn), lambda i, j: (i, j)), pat(d4), pat(d16), tab, tab, tab],
        out_specs=[pl.BlockSpec((tm, tn), lambda i, j: (i, col_block * per + j))],
        out_shape=[jax.ShapeDtypeStruct(dh.shape, dh.dtype)],
        scratch_shapes=[pltpu.VMEM((tn // HEAD_DIM, tm, HEAD_DIM), F32), pltpu.VMEM((tn // HEAD_DIM, tm, HEAD_DIM), F32)],
        semantics=("parallel", "parallel"), vmem_mib=32, args=(dh, parts[0][0], parts[1], parts[2], *tabs),
        aliases={0: 0}, comm=comm)
    return dh, exchanged


def _grad_w_in(x, dh, half, name, comm=None):
    seq = x.shape[0]
    ts, td, te = 2048, D_MODEL // 2, SHARD_IN

    def body(half_ref, x_ref, dh_ref, o_ref):
        k = pl.program_id(1)
        part = _dot_tn(x_ref[...].astype(BF16), dh_ref[...])

        @pl.when(k == 0)
        def _():
            o_ref[...] = part

        @pl.when(k > 0)
        def _():
            o_ref[...] += part

    (g,), exchanged = _call(
        body, name=name, grid=(N_SHARDS, seq // ts),
        in_specs=[pl.BlockSpec((ts, td), lambda e, k, half_ref: (k, half_ref[0])),
                  pl.BlockSpec((ts, te), lambda e, k, half_ref: (k, e))],
        out_specs=[pl.BlockSpec((None, td, te), lambda e, k, half_ref: (e, 0, 0))],
        out_shape=[jax.ShapeDtypeStruct((N_SHARDS, td, te), F32)],
        scratch_shapes=[], semantics=("parallel", "arbitrary"), vmem_mib=56, args=(x, dh), comm=comm,
        prefetch=(half,))
    return g, exchanged


def _grad_w_out(y, dzb):
    seq = y.shape[0]
    ts, te = 512, 1024
    nk = seq // ts

    def body(y_ref, dz_ref, o_ref, acc_ref):
        k = pl.program_id(1)

        @pl.when(k == 0)
        def _():
            acc_ref[...] = jnp.zeros_like(acc_ref)

        acc_ref[...] += _dot_tn(y_ref[...], dz_ref[...])

        @pl.when(k == nk - 1)
        def _():
            o_ref[...] = acc_ref[...]

    return _pallas(
        body, name="grad_w_out", grid=(D_MODEL // te, nk),
        in_specs=[pl.BlockSpec((ts, te), lambda e, k: (k, e)), pl.BlockSpec((ts, D_MODEL), lambda e, k: (k, 0))],
        out_specs=pl.BlockSpec((te, D_MODEL), lambda e, k: (e, 0)),
        out_shape=jax.ShapeDtypeStruct((D_MODEL, D_MODEL), F32),
        scratch_shapes=[pltpu.VMEM((te, D_MODEL), F32)],
        compiler_params=_params(("parallel", "arbitrary"), 48),
    )(y, dzb)


GRAD_X_LATE_SHARDS = 1
GRAD_X_PARTIAL_ROWS = 1024


def _grad_x_partial(dh, w_in_g, first, tiles, prev=None, comm=None):
    seq = dh.shape[0]
    tm, tk = GRAD_X_PARTIAL_ROWS, SHARD_IN

    def body(*refs):
        dh_ref, w_ref, o_ref = refs[-3:]
        k = pl.program_id(1)
        part = _dot_nt(dh_ref[...], w_ref[...])

        @pl.when(k == 0)
        def _():
            o_ref[...] = part

        @pl.when(k > 0)
        def _():
            o_ref[...] += part

    carried = [] if prev is None else [prev]
    (partial,), exchanged = _call(
        body, name="grad_x_partial_%d" % first, grid=(tiles, N_SHARDS - GRAD_X_LATE_SHARDS),
        in_specs=[ANY] * len(carried) + [
            pl.BlockSpec((tm, tk), lambda i, k: (i + first, k)),
            pl.BlockSpec((None, D_MODEL, tk), lambda i, k: (k, 0, 0))],
        out_specs=[pl.BlockSpec((tm, D_MODEL), lambda i, k: (i + first, 0))],
        out_shape=[jax.ShapeDtypeStruct((seq, D_MODEL), F32)],
        scratch_shapes=[], semantics=("parallel", "arbitrary"), vmem_mib=48, args=(*carried, dh, w_in_g),
        aliases={0: 0} if carried else None, comm=comm)
    return partial, exchanged


def _grad_x_final(dh, w_in_g, dz, partial):
    seq = dh.shape[0]
    tm, tk = 512, SHARD_IN
    k0 = N_SHARDS - GRAD_X_LATE_SHARDS

    def body(dh_ref, w_ref, dz_ref, p_ref, o_ref):
        k = pl.program_id(1)
        part = _dot_nt(dh_ref[...], w_ref[...])

        @pl.when(k == 0)
        def _():
            o_ref[...] = (DEEPNORM_ALPHA * dz_ref[...] + p_ref[...]) + part

        @pl.when(k > 0)
        def _():
            o_ref[...] += part

    row = pl.BlockSpec((tm, D_MODEL), lambda i, k: (i, 0))
    return _pallas(
        body, name="grad_x_final", grid=(seq // tm, GRAD_X_LATE_SHARDS),
        in_specs=[pl.BlockSpec((tm, tk), lambda i, k: (i, k + k0)),
                  pl.BlockSpec((None, D_MODEL, tk), lambda i, k: (k + k0, 0, 0)), row, row],
        out_specs=row, out_shape=jax.ShapeDtypeStruct((seq, D_MODEL), F32),
        compiler_params=_params(("parallel", "arbitrary"), 48),
    )(dh, w_in_g, dz, partial)


def _pool_weight(w_pool_sh):
    n_groups = len(POOL_WINDOWS)
    shard_c = POOL_GROUP_DIM // N_SHARDS
    return (w_pool_sh.reshape(N_SHARDS, n_groups, shard_c, POOL_GROUP_DIM).transpose(1, 0, 2, 3)
            .reshape(n_groups, POOL_GROUP_DIM, POOL_GROUP_DIM))


def _pool_grad_pieces(g_w_pool):
    n_groups = len(POOL_WINDOWS)
    half_c = POOL_GROUP_DIM // N_SHARDS // 2
    return (g_w_pool.reshape(n_groups, N_SHARDS, 2, half_c, POOL_GROUP_DIM).transpose(1, 2, 0, 3, 4)
            .reshape(N_SHARDS, 2, n_groups * half_c, POOL_GROUP_DIM))


def _step(x, target, w_in_g, w_rest, pool_scale, gain, bias, place=None):
    seq = x.shape[0]
    tabs = _rope_tables(seq)
    if place is None:
        qkv, _ = _in_proj_qkv(x, w_in_g, tabs)
        hug, _ = _in_proj_pool_gate(x, w_in_g)
    else:
        qkv, hug, w_in_g = _in_proj_gathering(x, w_in_g, tabs, place[3])
    gather = lambda bufs, phase: _allgather_weights(bufs, phase) if place else None
    o_1, st_1, gathered = _attn_fwd(qkv[0], "attn_fwd_d1", gather(w_rest, "neighbours"))
    o_4, st_4, gathered = _attn_fwd(qkv[1], "attn_fwd_d4", gather(gathered, "diagonal"))
    o_16, st_16, _ = _attn_fwd(qkv[2], "attn_fwd_d16")
    o_list, st_list = [o_1, o_4, o_16], [st_1, st_4, st_16]
    w_out_g, w_pool_sh = gathered if place else w_rest
    w_pool_g = _pool_weight(w_pool_sh)
    y, mixpre, lse_all, pooled = _mix_gate(o_list, st_list, hug, w_pool_g, pool_scale)
    dz, dzb, gain_part, bias_part, loss_part = _out_proj_loss(y, w_out_g, x, target, gain, bias)
    dh, dpo, do_list, stat_list = _dy_gate_bwd(dzb, w_out_g, hug, mixpre, pool_scale, lse_all)
    g_w_out = _grad_w_out(y, dzb)
    dh, g_w_pool, scale_part = _pool_bwd(dh, dpo, mixpre, pooled, w_pool_g, pool_scale)
    small = jnp.concatenate([scale_part, gain_part, bias_part, loss_part], axis=1)
    early = [g_w_out.reshape(N_SHARDS, 2, D_MODEL // (2 * N_SHARDS), D_MODEL), _pool_grad_pieces(g_w_pool)]

    bwd = lambda p, comm: _attn_bwd(qkv[p], do_list[p], stat_list[p], "attn_bwd_d%d" % DILATIONS[p], comm)
    if place is None:
        parts = [bwd(p, None)[0] for p in range(3)]
    else:
        core, chip_core, onward = place[:3]
        part_a, recv = bwd(0, _exchange_halves(early))
        sums = [_add_own_half(g, r, core, "add_own_half_%d" % a) for a, (g, r) in enumerate(zip(early, recv))]
        part_b, recv = bwd(1, _scatter_to_chips([s[1] for s in sums]))
        bufs = [_add_chips(s[0], r, chip_core, "add_chips_%d" % a) for a, (s, r) in enumerate(zip(sums, recv))]
        part_c, early = bwd(2, _share_with_sibling(bufs))
        parts = [part_a, part_b, part_c]
    dh, gathered = _sum_patterns(dh, [t[0] for t in parts], tabs, True, 0, "sum_dq",
                                 _gather_small(small) if place else None)
    dh, _ = _sum_patterns(dh, [t[1] for t in parts], tabs, True, 1, "sum_dk")
    dh, _ = _sum_patterns(dh, [t[2] for t in parts], tabs, False, 2, "sum_dv")
    if place:
        small = (small, gathered[0])
    if place is None:
        halves = [_grad_w_in(x, dh, jnp.full((1,), h, jnp.int32), "grad_w_in_%d" % h)[0] for h in range(2)]
        g_w_in = jnp.stack(halves, axis=1)
        g_x = _grad_x_final(dh, w_in_g, dz, _grad_x_partial(dh, w_in_g, 0, seq // GRAD_X_PARTIAL_ROWS)[0])
    else:
        give, _ = _grad_w_in(x, dh, 1 - core, "grad_w_in_give")
        keep, recv = _grad_w_in(x, dh, core, "grad_w_in_keep", _send_to_sibling([give]))
        total, total_b = _add_pair(keep, recv[0], "add_own_half_w_in")
        n_tiles = seq // GRAD_X_PARTIAL_ROWS
        tiles = max(n_tiles // 4, 1)
        part, relayed = _grad_x_partial(dh, w_in_g, 0, tiles, None, _relay_diagonal(total_b))
        total_b = _fold_relayed(total, total_b, relayed[0], onward)
        part, recv = _grad_x_partial(dh, w_in_g, tiles, n_tiles - tiles, part, _scatter_to_neighbours(total_b))
        buf = _add_chips(total, recv[0], chip_core, "add_chips_w_in")
        g_x = _grad_x_final(dh, w_in_g, dz, part)
        g_w_in = _run_exchange(_share_with_sibling([buf]), "share_w_in")[0]
    return g_x, g_w_in, early[0], early[1], small


def _exchange_halves(grads):
    n = len(grads)

    def copies(src, dst, sems):
        x, y, c, _ = _mesh_place()
        return [_remote(src[a].at[j, 1 - c], dst[a].at[j], sems[0].at[a, j], sems[1].at[a, j], (x, y, 1 - c))
                for a in range(n) for j in range(N_SHARDS)]

    def start(src, dst, sems):
        for cp in copies(src, dst, sems):
            cp.start()

    def finish(src, dst, sems):
        for cp in copies(src, dst, sems):
            cp.wait()

    return _Exchange(grads, [jax.ShapeDtypeStruct((N_SHARDS,) + g.shape[2:], g.dtype) for g in grads], {},
                     [pltpu.SemaphoreType.DMA((n, N_SHARDS))] * 2, start, finish)


def _add_own_half(grad, recv, core, name):
    _, _, r, c = grad.shape
    tr = min(r, 256)

    def body(core_ref, g_ref, r_ref, o_ref, ob_ref):
        tot = g_ref[...] + r_ref[...]
        o_ref[...] = tot
        ob_ref[...] = tot.astype(BF16)

    out = pl.BlockSpec((None, tr, c), lambda j, i, core_ref: (j, i, 0))
    return _pallas(
        body, name=name,
        grid_spec=pltpu.PrefetchScalarGridSpec(
            num_scalar_prefetch=1, grid=(N_SHARDS, r // tr),
            in_specs=[pl.BlockSpec((None, None, tr, c), lambda j, i, core_ref: (j, core_ref[0], i, 0)),
                      pl.BlockSpec((None, tr, c), lambda j, i, core_ref: (j, i, 0))],
            out_specs=[out, out]),
        out_shape=[jax.ShapeDtypeStruct((N_SHARDS, r, c), F32), jax.ShapeDtypeStruct((N_SHARDS, r, c), BF16)],
        compiler_params=_params(("parallel", "parallel"), 32),
    )(core, grad, recv)


def _send_to_sibling(arrays):
    n = len(arrays)

    def copies(src, dst, sems):
        x, y, c, _ = _mesh_place()
        return [_remote(src[a], dst[a], sems[0].at[a], sems[1].at[a], (x, y, 1 - c)) for a in range(n)]

    def start(src, dst, sems):
        for cp in copies(src, dst, sems):
            cp.start()

    def finish(src, dst, sems):
        for cp in copies(src, dst, sems):
            cp.wait()

    return _Exchange(arrays, [jax.ShapeDtypeStruct(t.shape, t.dtype) for t in arrays], {},
                     [pltpu.SemaphoreType.DMA((n,))] * 2, start, finish)


def _add_pair(a, b, name):
    _, r, c = a.shape
    tr = min(r, 256)

    def body(a_ref, b_ref, o_ref, ob_ref):
        tot = a_ref[...] + b_ref[...]
        o_ref[...] = tot
        ob_ref[...] = tot.astype(BF16)

    spec = pl.BlockSpec((None, tr, c), lambda j, i: (j, i, 0))
    return _pallas(
        body, name=name, grid=(N_SHARDS, r // tr), in_specs=[spec, spec], out_specs=[spec, spec],
        out_shape=[jax.ShapeDtypeStruct(a.shape, F32), jax.ShapeDtypeStruct(a.shape, BF16)],
        compiler_params=_params(("parallel", "parallel"), 32),
    )(a, b)


def _scatter_to_chips(sums, rows=None, into=None):
    n = len(sums)

    def copies(src, dst, sems):
        x, y, c, chips = _mesh_place()
        part = (lambda ref: ref) if rows is None else (lambda ref: ref.at[pl.ds(rows[0], rows[1])])
        return [_remote(part(src[a].at[2 * cx + cy]), part(dst[a].at[k]), sems[0].at[a, k], sems[1].at[a, k],
                        (cx, cy, c))
                for a in range(n) for k, (cx, cy) in enumerate(chips)]

    def start(src, dst, sems):
        for cp in copies(src, dst, sems):
            cp.start()

    def finish(src, dst, sems):
        for cp in copies(src, dst, sems):
            cp.wait()

    return _Exchange(sums + (into or []), [jax.ShapeDtypeStruct((3,) + s.shape[1:], s.dtype) for s in sums],
                     {n + a: a for a in range(n)} if into else {},
                     [pltpu.SemaphoreType.DMA((n, 3))] * 2, start, finish)


def _add_chips(sums, recv, chip_core, name):
    _, r, c = sums.shape
    n_recv = recv.shape[0]
    tr = min(r, 256)

    def body(cc_ref, s_ref, r_ref, o_ref):
        tot = s_ref[...]
        for k in range(n_recv):
            tot = tot + r_ref[k].astype(F32)
        o_ref[...] = tot

    return _pallas(
        body, name=name,
        grid_spec=pltpu.PrefetchScalarGridSpec(
            num_scalar_prefetch=1, grid=(r // tr,),
            in_specs=[pl.BlockSpec((None, tr, c), lambda i, cc_ref: (cc_ref[0], i, 0)),
                      pl.BlockSpec((n_recv, tr, c), lambda i, cc_ref: (0, i, 0))],
            out_specs=pl.BlockSpec((None, tr, c), lambda i, cc_ref: (cc_ref[1], i, 0))),
        out_shape=jax.ShapeDtypeStruct((2, r, c), F32),
        compiler_params=_params(("parallel",), 32),
    )(chip_core, sums, recv)


def _relay_diagonal(sums_b):
    def copy(src, dst, sems):
        x, y, c, _ = _mesh_place()
        diagonal = 2 * (1 - x) + (1 - y)
        return _remote(src[0].at[diagonal], dst[0], sems[0].at[0], sems[1].at[0], (x ^ (1 - c), y ^ c, c))

    def start(src, dst, sems):
        copy(src, dst, sems).start()

    def finish(src, dst, sems):
        copy(src, dst, sems).wait()

    return _Exchange([sums_b], [jax.ShapeDtypeStruct(sums_b.shape[1:], sums_b.dtype)], {},
                     [pltpu.SemaphoreType.DMA((1,))] * 2, start, finish)


def _fold_relayed(sums, sums_b, relayed, onward):
    _, r, c = sums.shape
    tr = min(r, 256)

    def body(on_ref, b_in_ref, s_ref, r_ref, o_ref):
        o_ref[...] = (s_ref[...] + r_ref[...].astype(F32)).astype(BF16)

    return _pallas(
        body, name="fold_relayed",
        grid_spec=pltpu.PrefetchScalarGridSpec(
            num_scalar_prefetch=1, grid=(r // tr,),
            in_specs=[ANY, pl.BlockSpec((None, tr, c), lambda i, on_ref: (on_ref[0], i, 0)),
                      pl.BlockSpec((tr, c), lambda i, on_ref: (i, 0))],
            out_specs=pl.BlockSpec((None, tr, c), lambda i, on_ref: (on_ref[0], i, 0))),
        out_shape=jax.ShapeDtypeStruct(sums_b.shape, sums_b.dtype),
        input_output_aliases={1: 0},
        compiler_params=_params(("parallel",), 32),
    )(onward, sums_b, sums, relayed)


def _scatter_to_neighbours(sums_b):
    def copies(src, dst, sems):
        x, y, c, chips = _mesh_place()
        return [_remote(src[0].at[2 * cx + cy], dst[0].at[k], sems[0].at[k], sems[1].at[k], (cx, cy, c))
                for k, (cx, cy) in enumerate(chips[:2])]

    def start(src, dst, sems):
        for cp in copies(src, dst, sems):
            cp.start()

    def finish(src, dst, sems):
        for cp in copies(src, dst, sems):
            cp.wait()

    return _Exchange([sums_b], [jax.ShapeDtypeStruct((2,) + sums_b.shape[1:], sums_b.dtype)], {},
                     [pltpu.SemaphoreType.DMA((2,))] * 2, start, finish)


def _share_with_sibling(bufs):
    n = len(bufs)

    def copies(dst, sems, half):
        x, y, c, _ = _mesh_place()
        h = c if half == "mine" else 1 - c
        return [_remote(dst[a].at[h], dst[a].at[h], sems[0].at[a], sems[1].at[a], (x, y, 1 - c)) for a in range(n)]

    def start(ins, dst, sems):
        for cp in copies(dst, sems, "mine"):
            cp.start()

    def finish(ins, dst, sems):
        for cp in copies(dst, sems, "theirs"):
            cp.wait_recv()
        for cp in copies(dst, sems, "mine"):
            cp.wait_send()

    return _Exchange(bufs, [jax.ShapeDtypeStruct(b.shape, b.dtype) for b in bufs], {a: a for a in range(n)},
                     [pltpu.SemaphoreType.DMA((n,))] * 2, start, finish)


def _adam_math(w, g, m, v):
    m = ADAM_B1 * m + (1.0 - ADAM_B1) * g
    v = ADAM_B2 * v + (1.0 - ADAM_B2) * (g * g)
    m_hat = m / (1.0 - ADAM_B1 ** ADAM_STEP)
    v_hat = v / (1.0 - ADAM_B2 ** ADAM_STEP)
    delta = -ADAM_LR * (m_hat / (jnp.sqrt(v_hat) + ADAM_EPS) + ADAM_WD * w)
    return delta, m, v


def _gather_small(small):
    def peers():
        x, y, c, _ = _mesh_place()
        return [(x ^ ((r >> 2) & 1), y ^ ((r >> 1) & 1), c ^ (r & 1)) for r in range(1, 8)], 4 * x + 2 * y + c

    def start(src, dst, sems):
        to, me = peers()
        for r, peer in enumerate(to):
            _remote(src[0], dst[0].at[me], sems[0].at[r], sems[1].at[r], peer).start()

    def finish(src, dst, sems):
        to, me = peers()
        for r, (px, py, pc) in enumerate(to):
            theirs = dst[0].at[4 * px + 2 * py + pc]
            _remote(theirs, theirs, sems[0].at[r], sems[1].at[r], (px, py, pc)).wait_recv()
        for r, peer in enumerate(to):
            _remote(src[0], dst[0].at[me], sems[0].at[r], sems[1].at[r], peer).wait_send()

    return _Exchange([small], [jax.ShapeDtypeStruct((8,) + small.shape, small.dtype)], {},
                     [pltpu.SemaphoreType.DMA((7,))] * 2, start, finish)


def _small_adamw(gathered, small, me, w_vec, m_vec, v_vec):
    n_par = w_vec.shape[1]

    def body(me_ref, a_ref, s_ref, w_ref, m_ref, v_ref, loss_ref, g_ref, d_ref, nm_ref, nv_ref):
        mine = s_ref[...]
        tot = jnp.where(me_ref[0] == 0, mine, a_ref[0])
        for d in range(1, 8):
            tot = tot + jnp.where(me_ref[0] == d, mine, a_ref[d])
        tot = jnp.sum(tot, axis=0, keepdims=True)
        sq = jnp.sum(tot[:, n_par:], axis=1, keepdims=True)
        loss_ref[...] = jnp.broadcast_to(sq * (0.5 / D_MODEL), loss_ref.shape)
        g = tot[:, :n_par]
        g_ref[...] = g
        d_ref[...], nm_ref[...], nv_ref[...] = _adam_math(w_ref[...], g, m_ref[...], v_ref[...])

    vm = pl.BlockSpec(memory_space=pltpu.VMEM)
    vec = jax.ShapeDtypeStruct((1, n_par), F32)
    return pl.pallas_call(
        body, name="small_adamw",
        grid_spec=pltpu.PrefetchScalarGridSpec(num_scalar_prefetch=1, grid=(), in_specs=[vm] * 5, out_specs=[vm] * 5),
        out_shape=[jax.ShapeDtypeStruct((1, 128), F32), vec, vec, vec, vec],
    )(me, gathered, small, w_vec, m_vec, v_vec)


def _adamw(w, g, m, v, name):
    r, c = w.shape
    tr = min(r, 256)

    def body(w_ref, g_ref, m_ref, v_ref, d_ref, nm_ref, nv_ref):
        d_ref[...], nm_ref[...], nv_ref[...] = _adam_math(w_ref[...], g_ref[...], m_ref[...], v_ref[...])

    spec = pl.BlockSpec((tr, c), lambda i: (i, 0))
    shape = jax.ShapeDtypeStruct((r, c), F32)
    return _pallas(
        body, name=name, grid=(r // tr,),
        in_specs=[spec] * 4, out_specs=[spec] * 3, out_shape=[shape] * 3,
        compiler_params=_params(("parallel",), 48),
    )(w, g, m, v)


def kernel(x, w_in, w_pool, pool_scale, w_out, ln_gain, ln_bias, loss_target, m_w_in, m_w_pool, m_pool_scale, m_w_out, m_ln_gain, m_ln_bias, v_w_in, v_w_pool, v_pool_scale, v_w_out, v_ln_gain, v_ln_bias):
    xi, yi, ci = lax.axis_index("x"), lax.axis_index("y"), lax.axis_index("c")
    chip = (2 * xi + yi).astype(jnp.int32).reshape(1)
    core = ci.astype(jnp.int32).reshape(1)
    n_groups = len(POOL_WINDOWS)
    shard_c = w_pool.shape[2]

    w_in_b = _cast_bf16(w_in[0], chip, "cast_w_in", 256)
    w_out_b = _cast_bf16(w_out[0], chip, "cast_w_out", 256)
    w_pool_b = _cast_bf16(w_pool[0].reshape(n_groups * shard_c, POOL_GROUP_DIM), chip, "cast_w_pool", 256)

    chip_core = jnp.concatenate([chip, core])
    onward = (2 * (xi ^ ci) + (yi ^ (1 - ci))).astype(jnp.int32).reshape(1)
    g_x, full_in, full_out, full_pool, small = _step(
        x[0], loss_target[0], w_in_b, [w_out_b, w_pool_b], pool_scale, ln_gain, ln_bias,
        (core, chip_core, onward, _in_proj_plan(xi, yi)))
    half_c = shard_c // 2
    grad_w_in = full_in.reshape(D_MODEL, SHARD_IN)
    grad_w_out = full_out.reshape(D_MODEL // N_SHARDS, D_MODEL)
    grad_w_pool = (full_pool.reshape(2, n_groups, half_c, POOL_GROUP_DIM).transpose(1, 0, 2, 3)
                   .reshape(n_groups * shard_c, POOL_GROUP_DIM))

    d_in, nm_in, nv_in = _adamw(w_in[0], grad_w_in, m_w_in[0], v_w_in[0], "adamw_w_in")
    d_out, nm_out, nv_out = _adamw(w_out[0], grad_w_out, m_w_out[0], v_w_out[0], "adamw_w_out")
    flat = lambda t: t[0].reshape(n_groups * shard_c, POOL_GROUP_DIM)
    d_pool, nm_pool, nv_pool = _adamw(flat(w_pool), grad_w_pool, flat(m_w_pool), flat(v_w_pool), "adamw_w_pool")

    cat = lambda a, b, c: jnp.concatenate([a, b, c], axis=1)
    me = (4 * xi + 2 * yi + ci).astype(jnp.int32).reshape(1)
    loss_v, g_vec, d_vec, nm_vec, nv_vec = _small_adamw(
        small[1], small[0], me, cat(pool_scale, ln_gain, ln_bias), cat(m_pool_scale, m_ln_gain, m_ln_bias),
        cat(v_pool_scale, v_ln_gain, v_ln_bias))

    def split(vec):
        return vec[:, :D_POOL], vec[:, D_POOL:D_POOL + D_MODEL], vec[:, D_POOL + D_MODEL:]

    g_scale, g_gain, g_bias = split(g_vec)
    d_scale, d_gain, d_bias = split(d_vec)
    nm_scale, nm_gain, nm_bias = split(nm_vec)
    nv_scale, nv_gain, nv_bias = split(nv_vec)
    pool_shape = w_pool.shape
    return (loss_v[0, 0], g_x[None],
            grad_w_in[None], grad_w_pool.reshape(pool_shape), g_scale, grad_w_out[None], g_gain, g_bias,
            d_in[None], d_pool.reshape(pool_shape), d_scale, d_out[None], d_gain, d_bias,
            nm_in[None], nm_pool.reshape(pool_shape), nm_scale, nm_out[None], nm_gain, nm_bias,
            nv_in[None], nv_pool.reshape(pool_shape), nv_scale, nv_out[None], nv_gain, nv_bias)
```

```python
import functools

import jax
import jax.numpy as jnp
from jax import lax
from jax.experimental import pallas as pl
from jax.experimental.pallas import tpu as pltpu

F32 = jnp.float32
BF16 = jnp.bfloat16
MESH = pl.DeviceIdType.MESH
ANY = pl.BlockSpec(memory_space=pl.ANY)

D_MODEL = 2048
D_ATTN = 1024
D_POOL = 1024
HEAD_DIM = 128
N_HEADS = 8
ROPE_DIM = 32
ROPE_THETA = 500000.0
DILATIONS = (1, 4, 16)
KEY_BLOCK = 128
CHUNK = 2 * KEY_BLOCK
STAT_LANES = 128
POOL_WINDOWS = (2, 4, 8, 16)
POOL_GROUP_DIM = 256
POOL_HALO = 16
D_QKV = 3 * D_ATTN
D_UG = D_POOL + D_MODEL
D_IN = D_QKV + D_UG
N_SHARDS = 4
SHARD_IN = D_IN // N_SHARDS
LN_EPS = 1e-5
DEEPNORM_ALPHA = 2.0 ** 0.25
ADAM_LR = 0.001
ADAM_B1 = 0.9
ADAM_B2 = 0.999
ADAM_EPS = 1e-08
ADAM_WD = 0.01
ADAM_STEP = 10
NEG = -1e30
MIB = 1024 * 1024


def _params(sem, vmem_mib):
    return pltpu.CompilerParams(dimension_semantics=sem, vmem_limit_bytes=vmem_mib * MIB)


def _pallas(body, **kwargs):
    pin = lambda s: pltpu.HBM(s.shape, s.dtype) if len(s.shape) >= 2 else s
    out_shape = kwargs.pop("out_shape")
    out_shape = [pin(s) for s in out_shape] if isinstance(out_shape, (list, tuple)) else pin(out_shape)
    call = pl.pallas_call(body, out_shape=out_shape, **kwargs)

    def run(*operands):
        return call(*[pltpu.with_memory_space_constraint(o, pltpu.HBM) if o.ndim >= 2 else o for o in operands])

    return run


class _Exchange:
    def __init__(self, operands, out_shape, aliases, sems, start, finish):
        self.operands, self.out_shape, self.aliases, self.sems = list(operands), list(out_shape), dict(aliases), list(sems)
        self.start, self.finish = start, finish


def _run_exchange(comm, name):
    n_in, n_out = len(comm.operands), len(comm.out_shape)

    def body(*refs):
        ins, outs, sems = refs[:n_in], refs[n_in:n_in + n_out], refs[n_in + n_out:]
        comm.start(ins, outs, sems)
        comm.finish(ins, outs, sems)

    return _pallas(
        body, name=name, in_specs=[ANY] * n_in, out_specs=[ANY] * n_out, out_shape=comm.out_shape,
        input_output_aliases=comm.aliases, scratch_shapes=comm.sems,
    )(*comm.operands)


def _call(body, *, name, grid, in_specs, out_specs, out_shape, scratch_shapes, semantics, vmem_mib, args,
          aliases=None, comm=None, prefetch=()):
    aliases = dict(aliases or {})
    n_pre, n_in, n_out, n_scr = len(prefetch), len(in_specs), len(out_specs), len(scratch_shapes)
    c_in, c_out = (len(comm.operands), len(comm.out_shape)) if comm else (0, 0)
    c_shapes, c_sems, c_operands = (comm.out_shape, comm.sems, comm.operands) if comm else ([], [], [])

    def hosted(*refs):
        pre, refs = refs[:n_pre], refs[n_pre:]
        a = n_in
        b = a + c_in
        c = b + n_out
        d = c + c_out
        e = d + n_scr
        if comm is None:
            body(*pre, *refs)
            return
        ids = [pl.program_id(k) for k in range(len(grid))]
        first = functools.reduce(jnp.logical_and, [i == 0 for i in ids])
        last = functools.reduce(jnp.logical_and, [i == g - 1 for i, g in zip(ids, grid)])

        @pl.when(first)
        def _():
            comm.start(refs[a:b], refs[c:d], refs[e:])

        body(*pre, *refs[:a], *refs[b:c], *refs[d:e])

        @pl.when(last)
        def _():
            comm.finish(refs[a:b], refs[c:d], refs[e:])

    if comm:
        semantics = ("arbitrary",) * len(grid)
        for i, o in comm.aliases.items():
            aliases[n_pre + n_in + i] = n_out + o
    outs = _pallas(
        hosted, name=name,
        grid_spec=pltpu.PrefetchScalarGridSpec(
            num_scalar_prefetch=n_pre, grid=grid, in_specs=list(in_specs) + [ANY] * c_in,
            out_specs=list(out_specs) + [ANY] * c_out, scratch_shapes=list(scratch_shapes) + c_sems),
        out_shape=list(out_shape) + c_shapes, input_output_aliases=aliases,
        compiler_params=_params(semantics, vmem_mib),
    )(*prefetch, *args, *c_operands)
    return list(outs[:n_out]), list(outs[n_out:])


def _dot_nn(a, b):
    return jnp.dot(a, b, preferred_element_type=F32)


def _dot_nt(a, b):
    return lax.dot_general(a, b, (((1,), (1,)), ((), ())), preferred_element_type=F32)


def _dot_tn(a, b):
    return lax.dot_general(a, b, (((0,), (0,)), ((), ())), preferred_element_type=F32)


def _fold_rows(a):
    r, c = a.shape
    return jnp.sum(a.reshape(r // 8, 8, c), axis=0)


def _cast_bf16(a, chip, name, rows):
    r, c = a.shape

    def body(chip_ref, a_ref, o_ref):
        o_ref[...] = a_ref[...].astype(BF16)

    return _pallas(
        body, name=name,
        grid_spec=pltpu.PrefetchScalarGridSpec(
            num_scalar_prefetch=1, grid=(r // rows,),
            in_specs=[pl.BlockSpec((rows, c), lambda i, chip_ref: (i, 0))],
            out_specs=pl.BlockSpec((None, rows, c), lambda i, chip_ref: (chip_ref[0], i, 0))),
        out_shape=jax.ShapeDtypeStruct((N_SHARDS, r, c), BF16),
        compiler_params=_params(("parallel",), 32),
    )(chip, a)


def _mesh_place():
    x, y, c = lax.axis_index("x"), lax.axis_index("y"), lax.axis_index("c")
    return x, y, c, [(1 - x, y), (x, 1 - y), (1 - x, 1 - y)]


def _remote(src, dst, send_sem, recv_sem, to):
    return pltpu.make_async_remote_copy(src_ref=src, dst_ref=dst, send_sem=send_sem, recv_sem=recv_sem,
                                        device_id=to, device_id_type=MESH)


def _allgather_weights(bufs, phase="all", chunks=1):
    n = len(bufs)
    items = [(a, q) for q in range(chunks) for a in range(n)]

    def rows(item, core):
        a, q = item
        size = bufs[a].shape[1] // 2 // chunks
        return pl.ds(core * chunks * size + q * size, size)

    def sem(sems, which, item, k):
        a, q = item
        return sems[which].at[a * chunks + q, k]

    DIAGONAL = 2

    def to_neighbours(dst, sems, item):
        x, y, c, chips = _mesh_place()
        own = dst[item[0]].at[2 * x + y, rows(item, c)]
        return [_remote(own, own, sem(sems, 0, item, k), sem(sems, 1, item, k), (cx, cy, c))
                for k, (cx, cy) in enumerate(chips[:DIAGONAL])]

    def relayed(dst, sems, item):
        x, y, c, _ = _mesh_place()
        piece = dst[item[0]].at[2 * (x ^ (1 - c)) + (y ^ c), rows(item, c)]
        return _remote(piece, piece, sem(sems, 0, item, DIAGONAL), sem(sems, 1, item, DIAGONAL), (x ^ c, y ^ (1 - c), c))

    def start(ins, dst, sems):
        for item in items:
            for cp in ([relayed(dst, sems, item)] if phase == "diagonal" else to_neighbours(dst, sems, item)):
                cp.start()

    def finish(ins, dst, sems):
        x, y, c, chips = _mesh_place()
        sibling = (x, y, 1 - c)
        passed_on = []

        def landed_then_pass_on(item, k):
            cx, cy = chips[k]
            landed = dst[item[0]].at[2 * cx + cy, rows(item, c)]
            _remote(landed, landed, sem(sems, 0, item, k), sem(sems, 1, item, k), (cx, cy, c)).wait_recv()
            cp = _remote(landed, landed, sem(sems, 2, item, k), sem(sems, 3, item, k), sibling)
            cp.start()
            passed_on.append(cp)

        sent = []
        for item in items:
            if phase != "diagonal":
                for k in range(DIAGONAL):
                    landed_then_pass_on(item, k)
                sent += to_neighbours(dst, sems, item)
            if phase == "all":
                relayed(dst, sems, item).start()
        for item in items:
            if phase != "neighbours":
                landed_then_pass_on(item, DIAGONAL)
                sent.append(relayed(dst, sems, item))
        for k in {"all": (0, 1, 2), "neighbours": (0, 1), "diagonal": (2,)}[phase]:
            cx, cy = chips[k]
            for item in items:
                passed = dst[item[0]].at[2 * cx + cy, rows(item, 1 - c)]
                _remote(passed, passed, sem(sems, 2, item, k), sem(sems, 3, item, k), sibling).wait_recv()
        for cp in sent + passed_on:
            cp.wait_send()

    return _Exchange(bufs, [jax.ShapeDtypeStruct(b.shape, b.dtype) for b in bufs], {a: a for a in range(n)},
                     [pltpu.SemaphoreType.DMA((n * chunks, 3))] * 4, start, finish)


def _rope_tables(seq):
    half = ROPE_DIM // 2
    inv_freq = ROPE_THETA ** (-(2.0 * jnp.arange(half, dtype=F32)) / ROPE_DIM)
    ang = jnp.arange(seq, dtype=jnp.int32).astype(F32)[:, None] * inv_freq[None, :]
    cos, sin = jnp.cos(ang), jnp.sin(ang)
    pad = jnp.zeros((seq, HEAD_DIM - ROPE_DIM), F32)
    zeros = jnp.zeros((seq, half), F32)
    c_tab = jnp.concatenate([cos, cos, pad + 1.0], axis=1)
    up_tab = jnp.concatenate([-sin, zeros, pad], axis=1)
    down_tab = jnp.concatenate([zeros, sin, pad], axis=1)
    return c_tab, up_tab, down_tab


def _rotate_heads(t, c_tab, up_tab, down_tab):
    outs = []
    for h in range(t.shape[1] // HEAD_DIM):
        th = t[:, h * HEAD_DIM:(h + 1) * HEAD_DIM]
        up = pltpu.roll(th, HEAD_DIM - ROPE_DIM // 2, axis=1)
        down = pltpu.roll(th, ROPE_DIM // 2, axis=1)
        outs.append(th * c_tab + up * up_tab + down * down_tab)
    return outs[0] if len(outs) == 1 else jnp.concatenate(outs, axis=1)


def _to_pattern(slabs_ref, dst_ref, dil, dtype):
    n_slabs, rows, _ = slabs_ref.shape
    for s in range(n_slabs):
        for r in range(dil):
            dst_ref[r, :, s * 128:(s + 1) * 128] = slabs_ref[s, pl.ds(r, rows // dil, dil), :].astype(dtype)


def _from_pattern(src_ref, slabs_ref, dil):
    n_slabs, rows, _ = slabs_ref.shape
    for s in range(n_slabs):
        for r in range(dil):
            slabs_ref[s, pl.ds(r, rows // dil, dil), :] = src_ref[r, :, s * 128:(s + 1) * 128].astype(F32)


def _store_slabs(slabs_ref, value):
    for s in range(slabs_ref.shape[0]):
        slabs_ref[s] = value[:, s * 128:(s + 1) * 128]


def _in_proj_qkv(x, w_in_g, tabs, comm=None):
    seq = x.shape[0]
    tm, tn = 512, SHARD_IN
    heads = tn // HEAD_DIM
    k_heads_in_second = 2 * D_ATTN // HEAD_DIM - heads
    d4, d16 = DILATIONS[1], DILATIONS[2]

    def body(x_ref, w_ref, c_ref, up_ref, down_ref, o1_ref, o4_ref, o16_ref, res_ref):
        shard = pl.program_id(0)
        xb = x_ref[...].astype(BF16)
        group = 4 * HEAD_DIM
        accs = [_dot_nn(xb, w_ref[:, g * group:(g + 1) * group]) for g in range(tn // group)]

        plain = shard == 1
        c_plain = jnp.where(plain, 1.0, c_ref[...])
        up_plain = jnp.where(plain, 0.0, up_ref[...])
        down_plain = jnp.where(plain, 0.0, down_ref[...])
        for h in range(heads):
            lanes = (h * HEAD_DIM) % group
            th = accs[h * HEAD_DIM // group][:, lanes:lanes + HEAD_DIM]
            if h < k_heads_in_second:
                th = _rotate_heads(th, c_ref[...], up_ref[...], down_ref[...])
            else:
                th = _rotate_heads(th, c_plain, up_plain, down_plain)
            res_ref[h] = th
            o1_ref[:, h * HEAD_DIM:(h + 1) * HEAD_DIM] = th.astype(BF16)
        _to_pattern(res_ref, o4_ref, d4, BF16)
        _to_pattern(res_ref, o16_ref, d16, BF16)

    tab_spec = pl.BlockSpec((tm, HEAD_DIM), lambda s, i: (i, 0))
    (o1, o4, o16), exchanged = _call(
        body, name="in_proj_qkv", grid=(D_QKV // tn, seq // tm),
        in_specs=[pl.BlockSpec((tm, D_MODEL), lambda s, i: (i, 0)),
                  pl.BlockSpec((None, D_MODEL, tn), lambda s, i: (s, 0, 0)),
                  tab_spec, tab_spec, tab_spec],
        out_specs=[pl.BlockSpec((tm, tn), lambda s, i: (i, s)),
                   pl.BlockSpec((d4, tm // d4, tn), lambda s, i: (0, i, s)),
                   pl.BlockSpec((d16, tm // d16, tn), lambda s, i: (0, i, s))],
        out_shape=[jax.ShapeDtypeStruct((seq, D_QKV), BF16),
                   jax.ShapeDtypeStruct((d4, seq // d4, D_QKV), BF16),
                   jax.ShapeDtypeStruct((d16, seq // d16, D_QKV), BF16)],
        scratch_shapes=[pltpu.VMEM((heads, tm, HEAD_DIM), F32)],
        semantics=("parallel", "parallel"), vmem_mib=52, args=(x, w_in_g, *tabs), comm=comm)
    return [o1[None], o4, o16], exchanged


def _in_proj_pool_gate(x, w_in_g, comm=None):
    seq = x.shape[0]
    tm, tn = 512, SHARD_IN
    first_shard = D_QKV // tn

    def body(x_ref, w_ref, o_ref):
        o_ref[...] = _dot_nn(x_ref[...].astype(BF16), w_ref[...]).astype(BF16)

    (hug,), exchanged = _call(
        body, name="in_proj_pool_gate", grid=(D_UG // tn, seq // tm),
        in_specs=[pl.BlockSpec((tm, D_MODEL), lambda s, i: (i, 0)),
                  pl.BlockSpec((None, D_MODEL, tn), lambda s, i: (s + first_shard, 0, 0))],
        out_specs=[pl.BlockSpec((tm, tn), lambda s, i: (i, s))],
        out_shape=[jax.ShapeDtypeStruct((seq, D_UG), BF16)],
        scratch_shapes=[], semantics=("parallel", "parallel"), vmem_mib=48, args=(x, w_in_g), comm=comm)
    return hug, exchanged


IN_PROJ_UNITS = ((-1, 0), (-1, 1), (0, 0), (1, 0), (0, 1), (1, 1), (2, 0), (2, 1))
UNIT_COLS = SHARD_IN // 2
HOLD_LAST, HOLD_FIRST = -2, -3


def _in_proj_plan(x, y):
    by_relation = [2 * x + y, 2 * (1 - x) + y, 2 * x + (1 - y), 2 * (1 - x) + (1 - y)]
    shards = [by_relation[r + 1] for r, _ in IN_PROJ_UNITS]
    halves = [h for _, h in IN_PROJ_UNITS]
    n = len(IN_PROJ_UNITS)

    def table(active, col_of):
        cols, rows = [], []
        held_col = functools.reduce(lambda acc, u: jnp.where(active[u], col_of(u), acc), reversed(range(n)), jnp.int32(0))
        seen = jnp.bool_(False)
        for u in range(n):
            cols.append(jnp.where(active[u], col_of(u), held_col))
            rows.append(jnp.where(active[u], -1, jnp.where(seen, HOLD_LAST, HOLD_FIRST)))
            held_col = jnp.where(active[u], col_of(u), held_col)
            seen = jnp.logical_or(seen, active[u])
        return cols, rows

    q_cols, q_rows = table([s < 2 for s in shards], lambda u: 2 * shards[u] + halves[u])
    h_cols, h_rows = table([s >= 2 for s in shards], lambda u: 2 * (shards[u] - 2) + halves[u])
    return jnp.stack([jnp.asarray(v, jnp.int32) for v in shards + halves + q_cols + q_rows + h_cols + h_rows])


def _in_proj_gathering(x, w_buf, tabs, plan):
    seq = x.shape[0]
    tm, tn = 512, UNIT_COLS
    n_tiles = seq // tm
    n_units = len(IN_PROJ_UNITS)
    heads = tn // HEAD_DIM
    d4, d16 = DILATIONS[1], DILATIONS[2]
    half_rows = D_MODEL // 2
    DIAGONAL = 2
    rotated = {(0, 0): heads, (0, 1): heads, (1, 0): 2 * D_ATTN // HEAD_DIM - SHARD_IN // HEAD_DIM, (1, 1): 0}

    def body(plan_ref, x_ref, w_in_ref, c_ref, up_ref, down_ref, o1_ref, o4_ref, o16_ref, hug_ref, w_ref,
             wbuf_ref, res_ref, w_sem, ici_send, ici_recv, d2d_send, d2d_recv):
        u, i = pl.program_id(0), pl.program_id(1)
        mx, my, mc, chips = _mesh_place()
        sibling = (mx, my, 1 - mc)
        chip_of = lambda k: 2 * chips[k][0] + chips[k][1]

        def piece(chip, core, half):
            return w_ref.at[chip, pl.ds(core * half_rows, half_rows), pl.ds(half * tn, tn)]

        def to_neighbour(k, half):
            mine = piece(2 * mx + my, mc, half)
            return _remote(mine, mine, ici_send.at[half, k], ici_recv.at[half, k], (*chips[k], mc))

        def relay(half):
            theirs = piece(2 * (mx ^ (1 - mc)) + (my ^ mc), mc, half)
            return _remote(theirs, theirs, ici_send.at[half, DIAGONAL], ici_recv.at[half, DIAGONAL],
                           (mx ^ mc, my ^ (1 - mc), mc))

        def arrival(k, half):
            theirs = piece(chip_of(k), mc, half)
            return _remote(theirs, theirs, ici_send.at[half, k], ici_recv.at[half, k], (*chips[k], mc))

        def to_sibling(k, half, core):
            theirs = piece(chip_of(k), core, half)
            return _remote(theirs, theirs, d2d_send.at[half, k], d2d_recv.at[half, k], sibling)

        first_tile = i == 0

        @pl.when(jnp.logical_and(u == 0, first_tile))
        def _():
            for half in range(2):
                for k in range(DIAGONAL):
                    to_neighbour(k, half).start()

        for unit, (k, half) in enumerate(IN_PROJ_UNITS):
            if k < 0:
                continue

            @pl.when(jnp.logical_and(u == unit, first_tile))
            def _(k=k, half=half):
                arrival(k, half).wait_recv()
                to_sibling(k, half, mc).start()
                if k == 1:
                    relay(half).start()
                to_sibling(k, half, 1 - mc).wait_recv()

        shard, col_half = plan_ref[u], plan_ref[n_units + u]

        @pl.when(first_tile)
        def _():
            cp = pltpu.make_async_copy(w_ref.at[shard, :, pl.ds(pl.multiple_of(col_half * tn, 128), tn)], wbuf_ref, w_sem)
            cp.start()
            cp.wait()

        xb = x_ref[...].astype(BF16)
        group = tn // 2
        accs = [_dot_nn(xb, wbuf_ref[:, g * group:(g + 1) * group]) for g in range(2)]

        def emit_qkv(rotated_heads):
            for h in range(heads):
                lanes = (h * HEAD_DIM) % group
                th = accs[h * HEAD_DIM // group][:, lanes:lanes + HEAD_DIM]
                if h < rotated_heads:
                    th = _rotate_heads(th, c_ref[...], up_ref[...], down_ref[...])
                res_ref[h] = th
                o1_ref[:, h * HEAD_DIM:(h + 1) * HEAD_DIM] = th.astype(BF16)
            _to_pattern(res_ref, o4_ref, d4, BF16)
            _to_pattern(res_ref, o16_ref, d16, BF16)

        for (s, half), rotated_heads in rotated.items():
            @pl.when(jnp.logical_and(shard == s, col_half == half))
            def _(rotated_heads=rotated_heads):
                emit_qkv(rotated_heads)

        @pl.when(shard >= 2)
        def _():
            for g, acc in enumerate(accs):
                hug_ref[:, g * group:(g + 1) * group] = acc.astype(BF16)

        @pl.when(jnp.logical_and(u == n_units - 1, i == n_tiles - 1))
        def _():
            for half in range(2):
                for k in range(DIAGONAL):
                    to_neighbour(k, half).wait_send()
                relay(half).wait_send()
                for k in range(DIAGONAL + 1):
                    to_sibling(k, half, mc).wait_send()

    def held(base, last):
        def row(u, i, plan_ref):
            code = plan_ref[base + u]
            return jnp.where(code == -1, i, jnp.where(code == HOLD_LAST, last, 0))
        return row

    q_row, h_row = held(3 * n_units, n_tiles - 1), held(5 * n_units, n_tiles - 1)
    q_col = lambda u, p: p[2 * n_units + u]
    h_col = lambda u, p: p[4 * n_units + u]
    tab_spec = pl.BlockSpec((tm, HEAD_DIM), lambda u, i, plan_ref: (i, 0))
    sems = [pltpu.SemaphoreType.DMA((2, 3))] * 4
    o1, o4, o16, hug, w_in_g = _pallas(
        body, name="in_proj_gathering",
        grid_spec=pltpu.PrefetchScalarGridSpec(
            num_scalar_prefetch=1, grid=(n_units, n_tiles),
            in_specs=[pl.BlockSpec((tm, D_MODEL), lambda u, i, plan_ref: (i, 0)), ANY, tab_spec, tab_spec, tab_spec],
            out_specs=[pl.BlockSpec((tm, tn), lambda u, i, p: (q_row(u, i, p), q_col(u, p))),
                       pl.BlockSpec((d4, tm // d4, tn), lambda u, i, p: (0, q_row(u, i, p), q_col(u, p))),
                       pl.BlockSpec((d16, tm // d16, tn), lambda u, i, p: (0, q_row(u, i, p), q_col(u, p))),
                       pl.BlockSpec((tm, tn), lambda u, i, p: (h_row(u, i, p), h_col(u, p))),
                       ANY],
            scratch_shapes=[pltpu.VMEM((D_MODEL, tn), BF16), pltpu.VMEM((heads, tm, HEAD_DIM), F32),
                            pltpu.SemaphoreType.DMA(())] + sems),
        out_shape=[jax.ShapeDtypeStruct((seq, D_QKV), BF16),
                   jax.ShapeDtypeStruct((d4, seq // d4, D_QKV), BF16),
                   jax.ShapeDtypeStruct((d16, seq // d16, D_QKV), BF16),
                   jax.ShapeDtypeStruct((seq, D_UG), BF16),
                   jax.ShapeDtypeStruct(w_buf.shape, w_buf.dtype)],
        input_output_aliases={2: 4},
        compiler_params=_params(("arbitrary", "arbitrary"), 52),
    )(plan, x, w_buf, *tabs)
    return [o1[None], o4, o16], hug, w_in_g


def _band_masks():
    row = lax.broadcasted_iota(jnp.int32, (KEY_BLOCK, KEY_BLOCK), 0)
    col = lax.broadcasted_iota(jnp.int32, (KEY_BLOCK, KEY_BLOCK), 1)
    return col <= row, col >= row


def _attn_fwd(qkv, name, comm=None):
    dil, n, _ = qkv.shape
    scale = HEAD_DIM ** -0.5
    lo, hi = slice(0, KEY_BLOCK), slice(KEY_BLOCK, CHUNK)

    def body(q_ref, k_ref, v_ref, kb_ref, vb_ref, o_ref, st_ref):
        i = pl.program_id(1)
        cur_mask, prev_mask = _band_masks()
        before_mask = jnp.logical_and(prev_mask, i > 0)
        lane = lax.broadcasted_iota(jnp.int32, (KEY_BLOCK, STAT_LANES), 1)
        tasks = [(rows, h) for rows in (lo, hi) for h in range(N_HEADS)]
        head = lambda h: slice(h * HEAD_DIM, (h + 1) * HEAD_DIM)

        def prev_of(rows, h):
            if rows is lo:
                return kb_ref[:, head(h)], vb_ref[:, head(h)], before_mask
            return k_ref[lo, head(h)], v_ref[lo, head(h)], prev_mask

        scores = []
        for rows, h in tasks:
            q = q_ref[rows, head(h)]
            scores.append((_dot_nt(q, prev_of(rows, h)[0]), _dot_nt(q, k_ref[rows, head(h)])))
        probs = []
        for (rows, h), (qk_prev, qk_cur) in zip(tasks, scores):
            s_prev = jnp.where(prev_of(rows, h)[2], qk_prev * scale, NEG)
            s_cur = jnp.where(cur_mask, qk_cur * scale, NEG)
            m = jnp.max(jnp.maximum(s_prev, s_cur), axis=-1, keepdims=True)
            p_prev = jnp.exp(s_prev - m)
            p_cur = jnp.exp(s_cur - m)
            den = jnp.sum(p_prev + p_cur, axis=-1, keepdims=True)
            probs.append((p_prev.astype(BF16), p_cur.astype(BF16), den, m + jnp.log(den)))
        stats = [jnp.zeros((KEY_BLOCK, STAT_LANES), F32), jnp.zeros((KEY_BLOCK, STAT_LANES), F32)]
        for (rows, h), (p_prev, p_cur, den, lse) in zip(tasks, probs):
            o = _dot_nn(p_cur, v_ref[rows, head(h)]) + _dot_nn(p_prev, prev_of(rows, h)[1])
            o_ref[rows, head(h)] = (o / den).astype(BF16)
            b = 0 if rows is lo else 1
            stats[b] = jnp.where(lane == h, lse, stats[b])
        st_ref[lo, :] = stats[0]
        st_ref[hi, :] = stats[1]

    main = lambda cb: pl.BlockSpec((None, CHUNK, D_ATTN), lambda r, i: (r, i, cb))
    before = lambda cb: pl.BlockSpec((None, KEY_BLOCK, D_ATTN), lambda r, i: (r, jnp.maximum(2 * i - 1, 0), cb))
    (o, st), exchanged = _call(
        body, name=name, grid=(dil, n // CHUNK),
        in_specs=[main(0), main(1), main(2), before(1), before(2)],
        out_specs=[main(0), pl.BlockSpec((None, CHUNK, STAT_LANES), lambda r, i: (r, i, 0))],
        out_shape=[jax.ShapeDtypeStruct((dil, n, D_ATTN), BF16), jax.ShapeDtypeStruct((dil, n, STAT_LANES), F32)],
        scratch_shapes=[], semantics=("parallel", "parallel"), vmem_mib=40, args=(qkv, qkv, qkv, qkv, qkv), comm=comm)
    return o, st, exchanged


def _attn_bwd(qkv, do, stats, name, comm=None):
    dil, n, _ = qkv.shape
    n_blocks = n // KEY_BLOCK
    last = n // CHUNK - 1
    scale = HEAD_DIM ** -0.5
    lo, hi = slice(0, KEY_BLOCK), slice(KEY_BLOCK, CHUNK)

    def body(q_ref, k_ref, v_ref, kb_ref, vb_ref, qa_ref, do_ref, doa_ref, st_ref, sta_ref, dq_ref, dk_ref, dv_ref):
        i = pl.program_id(1)
        cur_mask, prev_mask = _band_masks()
        before_mask = jnp.logical_and(prev_mask, i > 0)
        after_mask = jnp.logical_and(prev_mask, i < last)

        rows_cat = lambda a, b: jnp.concatenate([a, b], axis=0)
        masks = (jnp.concatenate([before_mask, cur_mask], axis=1), jnp.concatenate([prev_mask, cur_mask], axis=1),
                 after_mask)

        def operands(h):
            cols = slice(h * HEAD_DIM, (h + 1) * HEAD_DIM)
            lse_c, del_c = slice(h, h + 1), slice(N_HEADS + h, N_HEADS + h + 1)
            q = (q_ref[lo, cols], q_ref[hi, cols], qa_ref[:, cols])
            do = (do_ref[lo, cols], do_ref[hi, cols], doa_ref[:, cols])
            keys = (rows_cat(kb_ref[:, cols], k_ref[lo, cols]), k_ref[:, cols], k_ref[hi, cols])
            vals = (rows_cat(vb_ref[:, cols], v_ref[lo, cols]), v_ref[:, cols], v_ref[hi, cols])
            st = ((st_ref[lo, lse_c], st_ref[lo, del_c]), (st_ref[hi, lse_c], st_ref[hi, del_c]),
                  (sta_ref[:, lse_c], sta_ref[:, del_c]))
            return cols, q, do, keys, vals, st

        group = N_HEADS // 2
        for first_head in range(0, N_HEADS, group):
            heads = range(first_head, first_head + group)
            raw = {}
            for h in heads:
                _, q, do, keys, vals, _ = operands(h)
                raw[h] = [(_dot_nt(q[j], keys[j]), _dot_nt(do[j], vals[j])) for j in range(3)]
            grads = {}
            for h in heads:
                st = operands(h)[5]
                grads[h] = []
                for j in range(3):
                    qk, dp = raw[h][j]
                    lse, delta = st[j]
                    p = jnp.exp(jnp.where(masks[j], qk * scale, NEG) - lse)
                    grads[h].append((p.astype(BF16), (p * (dp - delta) * scale).astype(BF16)))
            for h in heads:
                cols, q, do, keys, _, _ = operands(h)
                (p0, ds0), (p1, ds1), (pa, dsa) = grads[h]
                own, nxt = slice(KEY_BLOCK, CHUNK), slice(0, KEY_BLOCK)

                def put(ref, rows, val, cols=cols):
                    ref[rows, cols] = val.astype(ref.dtype)

                put(dq_ref, lo, _dot_nn(ds0, keys[0]))
                put(dq_ref, hi, _dot_nn(ds1, keys[1]))
                put(dk_ref, lo, _dot_tn(rows_cat(ds0[:, own], ds1[:, nxt]), q_ref[:, cols]))
                put(dk_ref, hi, _dot_tn(rows_cat(ds1[:, own], dsa), rows_cat(q[1], q[2])))
                put(dv_ref, lo, _dot_tn(rows_cat(p0[:, own], p1[:, nxt]), do_ref[:, cols]))
                put(dv_ref, hi, _dot_tn(rows_cat(p1[:, own], pa), rows_cat(do[1], do[2])))

    def spec(rows, width, row_of, cb):
        return pl.BlockSpec((None, rows, width), lambda r, i: (r, row_of(i), cb))

    same = lambda i: i
    before = lambda i: jnp.maximum(2 * i - 1, 0)
    after = lambda i: jnp.minimum(2 * i + 2, n_blocks - 1)
    out = spec(CHUNK, D_ATTN, same, 0)
    return _call(
        body, name=name, grid=(dil, n // CHUNK),
        in_specs=[spec(CHUNK, D_ATTN, same, 0), spec(CHUNK, D_ATTN, same, 1), spec(CHUNK, D_ATTN, same, 2),
                  spec(KEY_BLOCK, D_ATTN, before, 1), spec(KEY_BLOCK, D_ATTN, before, 2),
                  spec(KEY_BLOCK, D_ATTN, after, 0),
                  spec(CHUNK, D_ATTN, same, 0), spec(KEY_BLOCK, D_ATTN, after, 0),
                  spec(CHUNK, STAT_LANES, same, 0), spec(KEY_BLOCK, STAT_LANES, after, 0)],
        out_specs=[out, out, out],
        out_shape=[jax.ShapeDtypeStruct((dil, n, D_ATTN), BF16)] * 3,
        scratch_shapes=[], semantics=("parallel", "parallel"), vmem_mib=40,
        args=(qkv, qkv, qkv, qkv, qkv, qkv, do, do, stats, stats), comm=comm)


def _window_sums(ext, window, backward):
    rows = ext.shape[0]
    acc, span = ext, 1
    while span < window:
        acc = acc + pltpu.roll(acc, (rows - span) if backward else span, axis=0)
        span *= 2
    return acc


def _mix_gate(o_list, st_list, hug, w_pool_g, pool_scale):
    seq = hug.shape[0]
    tm = 256
    halo_blocks = tm // POOL_HALO
    d4, d16 = DILATIONS[1], DILATIONS[2]

    def body(o1_ref, o4_ref, o16_ref, l1_ref, l4_ref, l16_ref, u_ref, halo_ref, ga_ref, gp_ref, wp_ref, sc_ref,
             y_ref, mix_ref, lse_ref, pooled_ref, n4_ref, n16_ref, nl4_ref, nl16_ref):
        i = pl.program_id(0)
        _from_pattern(o4_ref, n4_ref, d4)
        _from_pattern(o16_ref, n16_ref, d16)
        _from_pattern(l4_ref, nl4_ref, d4)
        _from_pattern(l16_ref, nl16_ref, d16)
        la, lb, lc = l1_ref[...], nl4_ref[0], nl16_ref[0]
        mx = jnp.maximum(jnp.maximum(la, lb), lc)
        ea, eb, ec = jnp.exp(la - mx), jnp.exp(lb - mx), jnp.exp(lc - mx)
        tot = ea + eb + ec
        lse_ref[...] = mx + jnp.log(tot)
        wa, wb, wc = ea / tot, eb / tot, ec / tot
        ga = ga_ref[...].astype(F32)
        silu_a = ga * jax.nn.sigmoid(ga)
        for h in range(N_HEADS):
            cols = slice(h * HEAD_DIM, (h + 1) * HEAD_DIM)
            hc = slice(h, h + 1)
            attn = wa[:, hc] * o1_ref[:, cols].astype(F32) + wb[:, hc] * n4_ref[h] + wc[:, hc] * n16_ref[h]
            mix_ref[:, cols] = attn.astype(BF16)
            y_ref[:, cols] = (attn * silu_a[:, cols]).astype(BF16)

        u = u_ref[...].astype(F32)
        halo = jnp.where(i > 0, halo_ref[...].astype(F32), 0.0)
        ext = jnp.concatenate([halo, u], axis=0)
        pos = i * tm + lax.broadcasted_iota(jnp.int32, (tm, 1), 0)
        gp = gp_ref[...].astype(F32)
        gated_scale = sc_ref[...] * (gp * jax.nn.sigmoid(gp))
        for g, window in enumerate(POOL_WINDOWS):
            cols = slice(g * POOL_GROUP_DIM, (g + 1) * POOL_GROUP_DIM)
            sums = _window_sums(ext[:, cols], window, backward=False)[POOL_HALO:, :]
            count = jnp.minimum(pos + 1, window).astype(F32)
            pooled = (sums / count - u[:, cols]).astype(BF16)
            pooled_ref[:, cols] = pooled
            pre = _dot_nn(pooled, wp_ref[g])
            out_cols = slice(D_ATTN + g * POOL_GROUP_DIM, D_ATTN + (g + 1) * POOL_GROUP_DIM)
            mix_ref[:, out_cols] = pre.astype(BF16)
            y_ref[:, out_cols] = (pre * gated_scale[:, cols]).astype(BF16)

    row = lambda width, cb=0: pl.BlockSpec((tm, width), lambda i: (i, cb))
    pat = lambda d, width: pl.BlockSpec((d, tm // d, width), lambda i: (0, i, 0))
    return _pallas(
        body, name="mix_gate", grid=(seq // tm,),
        in_specs=[row(D_ATTN), pat(d4, D_ATTN), pat(d16, D_ATTN),
                  row(STAT_LANES), pat(d4, STAT_LANES), pat(d16, STAT_LANES),
                  row(D_POOL),
                  pl.BlockSpec((POOL_HALO, D_POOL), lambda i: (jnp.maximum(i * halo_blocks - 1, 0), 0)),
                  row(D_ATTN, 1), row(D_POOL, 2),
                  pl.BlockSpec((len(POOL_WINDOWS), POOL_GROUP_DIM, POOL_GROUP_DIM), lambda i: (0, 0, 0)),
                  pl.BlockSpec((1, D_POOL), lambda i: (0, 0))],
        out_specs=[row(D_MODEL), row(D_MODEL), row(STAT_LANES), row(D_POOL)],
        out_shape=[jax.ShapeDtypeStruct((seq, D_MODEL), BF16), jax.ShapeDtypeStruct((seq, D_MODEL), BF16),
                   jax.ShapeDtypeStruct((seq, STAT_LANES), F32), jax.ShapeDtypeStruct((seq, D_POOL), BF16)],
        scratch_shapes=[pltpu.VMEM((N_HEADS, tm, HEAD_DIM), F32), pltpu.VMEM((N_HEADS, tm, HEAD_DIM), F32),
                        pltpu.VMEM((1, tm, STAT_LANES), F32), pltpu.VMEM((1, tm, STAT_LANES), F32)],
        compiler_params=_params(("parallel",), 48),
    )(o_list[0][0], o_list[1], o_list[2], st_list[0][0], st_list[1], st_list[2],
      hug, hug, hug, hug, w_pool_g, pool_scale)


def _out_proj_loss(y, w_out_g, x, target, gain, bias):
    seq = x.shape[0]
    tm = 512

    def body(y_ref, w_ref, x_ref, t_ref, g_ref, b_ref, dz_ref, dzb_ref, gg_ref, gb_ref, loss_ref):
        @pl.when(pl.program_id(0) == 0)
        def _():
            gg_ref[...] = jnp.zeros_like(gg_ref)
            gb_ref[...] = jnp.zeros_like(gb_ref)
            loss_ref[...] = jnp.zeros_like(loss_ref)

        halves = [slice(0, tm // 2), slice(tm // 2, tm)]
        projected = [_dot_nn(y_ref[rows, :], w_ref[...]) for rows in halves]
        for rows, out in zip(halves, projected):
            z = DEEPNORM_ALPHA * x_ref[rows, :] + out
            mu = jnp.mean(z, axis=-1, keepdims=True)
            zc = z - mu
            rstd = lax.rsqrt(jnp.mean(zc * zc, axis=-1, keepdims=True) + LN_EPS)
            xhat = zc * rstd
            gain_v = g_ref[...]
            diff = xhat * gain_v + b_ref[...] - t_ref[rows, :]
            sq = _fold_rows(diff * diff)
            part = sq[:, :128]
            for k in range(1, D_MODEL // 128):
                part = part + sq[:, k * 128:(k + 1) * 128]
            loss_ref[...] += part
            dln = diff * (1.0 / D_MODEL)
            gg_ref[...] += _fold_rows(dln * xhat)
            gb_ref[...] += _fold_rows(dln)
            dxhat = dln * gain_v
            dz = rstd * (dxhat - jnp.mean(dxhat, axis=-1, keepdims=True)
                         - xhat * jnp.mean(dxhat * xhat, axis=-1, keepdims=True))
            dz_ref[rows, :] = dz
            dzb_ref[rows, :] = dz.astype(BF16)

    row = lambda: pl.BlockSpec((tm, D_MODEL), lambda i: (i, 0))
    vec = lambda: pl.BlockSpec((1, D_MODEL), lambda i: (0, 0))
    acc = lambda width: pl.BlockSpec((8, width), lambda i: (0, 0))
    return _pallas(
        body, name="out_proj_loss", grid=(seq // tm,),
        in_specs=[row(), pl.BlockSpec((D_MODEL, D_MODEL), lambda i: (0, 0), pipeline_mode=pl.Buffered(1)),
                  row(), row(), vec(), vec()],
        out_specs=[row(), row(), acc(D_MODEL), acc(D_MODEL), acc(128)],
        out_shape=[jax.ShapeDtypeStruct((seq, D_MODEL), F32), jax.ShapeDtypeStruct((seq, D_MODEL), BF16),
                   jax.ShapeDtypeStruct((8, D_MODEL), F32), jax.ShapeDtypeStruct((8, D_MODEL), F32),
                   jax.ShapeDtypeStruct((8, 128), F32)],
        compiler_params=_params(("arbitrary",), 56),
    )(y, w_out_g.reshape(D_MODEL, D_MODEL), x, target, gain, bias)


def _dy_gate_bwd(dzb, w_out_g, hug, mixpre, pool_scale, lse_all):
    seq = dzb.shape[0]
    tm = 256
    d4, d16 = DILATIONS[1], DILATIONS[2]

    def body(dz_ref, w_ref, ga_ref, gp_ref, mix_ref, sc_ref, lse_ref,
             dh_ref, dpo_ref, do1_ref, do4_ref, do16_ref, st1_ref, st4_ref, st16_ref, da_ref, st_ref):
        dy = _dot_nt(dz_ref[...], w_ref[...])
        ga = ga_ref[...].astype(F32)
        sig = jax.nn.sigmoid(ga)
        attn = mix_ref[:, :D_ATTN].astype(F32)
        dya = dy[:, :D_ATTN]
        dattn = dya * (ga * sig)
        dh_ref[:, :D_ATTN] = (dya * attn * (sig * (1.0 + ga * (1.0 - sig)))).astype(BF16)
        _store_slabs(da_ref, dattn)
        lane = lax.broadcasted_iota(jnp.int32, (tm, STAT_LANES), 1)
        stats = lse_ref[...]
        prod = dattn * attn
        for h in range(N_HEADS):
            delta = jnp.sum(prod[:, h * HEAD_DIM:(h + 1) * HEAD_DIM], axis=-1, keepdims=True)
            stats = jnp.where(lane == N_HEADS + h, delta, stats)
        st_ref[0] = stats
        do1_ref[...] = dattn.astype(BF16)
        st1_ref[...] = stats
        _to_pattern(da_ref, do4_ref, d4, BF16)
        _to_pattern(da_ref, do16_ref, d16, BF16)
        _to_pattern(st_ref, st4_ref, d4, F32)
        _to_pattern(st_ref, st16_ref, d16, F32)

        gp = gp_ref[...].astype(F32)
        sig = jax.nn.sigmoid(gp)
        dyp = dy[:, D_ATTN:]
        dpo_ref[...] = (dyp * (gp * sig)).astype(BF16)
        dh_ref[:, D_ATTN:] = (dyp * (mix_ref[:, D_ATTN:].astype(F32) * sc_ref[...])
                              * (sig * (1.0 + gp * (1.0 - sig)))).astype(BF16)

    row = lambda width, cb=0: pl.BlockSpec((tm, width), lambda i: (i, cb))
    pat = lambda d, width: pl.BlockSpec((d, tm // d, width), lambda i: (0, i, 0))
    pat_shape = lambda d, width, dtype: jax.ShapeDtypeStruct((d, seq // d, width), dtype)
    outs = _pallas(
        body, name="dy_gate_bwd", grid=(seq // tm,),
        in_specs=[row(D_MODEL), pl.BlockSpec((D_MODEL, D_MODEL), lambda i: (0, 0)),
                  row(D_ATTN, 1), row(D_POOL, 2), row(D_MODEL), pl.BlockSpec((1, D_POOL), lambda i: (0, 0)),
                  row(STAT_LANES)],
        out_specs=[row(D_MODEL, D_IN // D_MODEL - 1), row(D_POOL),
                   row(D_ATTN), pat(d4, D_ATTN), pat(d16, D_ATTN),
                   row(STAT_LANES), pat(d4, STAT_LANES), pat(d16, STAT_LANES)],
        out_shape=[jax.ShapeDtypeStruct((seq, D_IN), BF16), jax.ShapeDtypeStruct((seq, D_POOL), BF16),
                   jax.ShapeDtypeStruct((seq, D_ATTN), BF16), pat_shape(d4, D_ATTN, BF16), pat_shape(d16, D_ATTN, BF16),
                   jax.ShapeDtypeStruct((seq, STAT_LANES), F32), pat_shape(d4, STAT_LANES, F32),
                   pat_shape(d16, STAT_LANES, F32)],
        scratch_shapes=[pltpu.VMEM((N_HEADS, tm, HEAD_DIM), F32), pltpu.VMEM((1, tm, STAT_LANES), F32)],
        compiler_params=_params(("parallel",), 48),
    )(dzb, w_out_g.reshape(D_MODEL, D_MODEL), hug, hug, mixpre, pool_scale, lse_all)
    dh, dpo, do1, do4, do16, st1, st4, st16 = outs
    return dh, dpo, [do1[None], do4, do16], [st1[None], st4, st16]


def _pool_bwd(dh, dpo, mixpre, pooled, w_pool_g, pool_scale):
    seq = dpo.shape[0]
    tm = 256
    halo_blocks = tm // POOL_HALO
    last = seq // tm - 1
    n_groups = len(POOL_WINDOWS)

    def body(dh_in_ref, dpo_ref, halo_ref, pre_ref, pooled_ref, wp_ref, sc_ref, du_ref, gw_ref, gs_ref):
        i = pl.program_id(0)

        @pl.when(i == 0)
        def _():
            gw_ref[...] = jnp.zeros_like(gw_ref)
            gs_ref[...] = jnp.zeros_like(gs_ref)

        dpo = dpo_ref[...].astype(F32)
        scale = sc_ref[...]
        gs_ref[...] += _fold_rows(dpo * pre_ref[...].astype(F32))
        halo = jnp.where(i < last, halo_ref[...].astype(F32), 0.0)
        dpw = (jnp.concatenate([dpo, halo], axis=0) * scale).astype(BF16)
        pos = i * tm + lax.broadcasted_iota(jnp.int32, (tm + POOL_HALO, 1), 0)
        for g, window in enumerate(POOL_WINDOWS):
            cols = slice(g * POOL_GROUP_DIM, (g + 1) * POOL_GROUP_DIM)
            dpw_g = dpw[:, cols]
            gw_ref[g] += _dot_tn(pooled_ref[:, cols], dpw_g[:tm, :])
            dpooled = _dot_nt(dpw_g, wp_ref[g])
            count = jnp.minimum(pos + 1, window).astype(F32)
            sums = _window_sums(dpooled / count, window, backward=True)
            du_ref[:, cols] = (sums[:tm, :] - dpooled[:tm, :]).astype(BF16)

    row = lambda width, cb=0: pl.BlockSpec((tm, width), lambda i: (i, cb))
    return _pallas(
        body, name="pool_bwd", grid=(seq // tm,),
        in_specs=[ANY, row(D_POOL),
                  pl.BlockSpec((POOL_HALO, D_POOL),
                               lambda i: (jnp.minimum((i + 1) * halo_blocks, seq // POOL_HALO - 1), 0)),
                  row(D_POOL, 1), row(D_POOL),
                  pl.BlockSpec((n_groups, POOL_GROUP_DIM, POOL_GROUP_DIM), lambda i: (0, 0, 0)),
                  pl.BlockSpec((1, D_POOL), lambda i: (0, 0))],
        out_specs=[row(D_POOL, D_QKV // D_POOL),
                   pl.BlockSpec((n_groups, POOL_GROUP_DIM, POOL_GROUP_DIM), lambda i: (0, 0, 0)),
                   pl.BlockSpec((8, D_POOL), lambda i: (0, 0))],
        out_shape=[jax.ShapeDtypeStruct(dh.shape, dh.dtype),
                   jax.ShapeDtypeStruct((n_groups, POOL_GROUP_DIM, POOL_GROUP_DIM), F32),
                   jax.ShapeDtypeStruct((8, D_POOL), F32)],
        input_output_aliases={0: 0},
        compiler_params=_params(("arbitrary",), 40),
    )(dh, dpo, dpo, mixpre, pooled, w_pool_g, pool_scale)


def _sum_patterns(dh, parts, tabs, unrotate, col_block, name, comm=None):
    seq = dh.shape[0]
    tm, tn = 256, D_ATTN
    per = D_ATTN // tn
    d4, d16 = DILATIONS[1], DILATIONS[2]

    def body(dh_in_ref, a1_ref, a4_ref, a16_ref, ct_ref, up_ref, down_ref, o_ref, n4_ref, n16_ref):
        _from_pattern(a4_ref, n4_ref, d4)
        _from_pattern(a16_ref, n16_ref, d16)
        for s in range(tn // HEAD_DIM):
            cols = slice(s * HEAD_DIM, (s + 1) * HEAD_DIM)
            tot = a1_ref[:, cols].astype(F32) + n4_ref[s] + n16_ref[s]
            if unrotate:
                tot = _rotate_heads(tot, ct_ref[...], -up_ref[...], -down_ref[...])
            o_ref[:, cols] = tot.astype(BF16)

    tab = pl.BlockSpec((tm, HEAD_DIM), lambda i, j: (i, 0))
    pat = lambda d: pl.BlockSpec((d, tm // d, tn), lambda i, j: (0, i, j))
    (dh,), exchanged = _call(
        body, name=name, grid=(seq // tm, per),
        in_specs=[ANY, pl.BlockSpec((tm, tn), lambda i, j: (i, j)), pat(d4), pat(d16), tab, tab, tab],
        out_specs=[pl.BlockSpec((tm, tn), lambda i, j: (i, col_block * per + j))],
        out_shape=[jax.ShapeDtypeStruct(dh.shape, dh.dtype)],
        scratch_shapes=[pltpu.VMEM((tn // HEAD_DIM, tm, HEAD_DIM), F32), pltpu.VMEM((tn // HEAD_DIM, tm, HEAD_DIM), F32)],
        semantics=("parallel", "parallel"), vmem_mib=32, args=(dh, parts[0][0], parts[1], parts[2], *tabs),
        aliases={0: 0}, comm=comm)
    return dh, exchanged


def _grad_w_in(x, dh, half, name, comm=None):
    seq = x.shape[0]
    ts, td, te = 2048, D_MODEL // 2, SHARD_IN

    def body(half_ref, x_ref, dh_ref, o_ref):
        k = pl.program_id(1)
        part = _dot_tn(x_ref[...].astype(BF16), dh_ref[...])

        @pl.when(k == 0)
        def _():
            o_ref[...] = part

        @pl.when(k > 0)
        def _():
            o_ref[...] += part

    (g,), exchanged = _call(
        body, name=name, grid=(N_SHARDS, seq // ts),
        in_specs=[pl.BlockSpec((ts, td), lambda e, k, half_ref: (k, half_ref[0])),
                  pl.BlockSpec((ts, te), lambda e, k, half_ref: (k, e))],
        out_specs=[pl.BlockSpec((None, td, te), lambda e, k, half_ref: (e, 0, 0))],
        out_shape=[jax.ShapeDtypeStruct((N_SHARDS, td, te), F32)],
        scratch_shapes=[], semantics=("parallel", "arbitrary"), vmem_mib=56, args=(x, dh), comm=comm,
        prefetch=(half,))
    return g, exchanged


def _grad_w_out(y, dzb):
    seq = y.shape[0]
    ts, te = 512, 1024
    nk = seq // ts

    def body(y_ref, dz_ref, o_ref, acc_ref):
        k = pl.program_id(1)

        @pl.when(k == 0)
        def _():
            acc_ref[...] = jnp.zeros_like(acc_ref)

        acc_ref[...] += _dot_tn(y_ref[...], dz_ref[...])

        @pl.when(k == nk - 1)
        def _():
            o_ref[...] = acc_ref[...]

    return _pallas(
        body, name="grad_w_out", grid=(D_MODEL // te, nk),
        in_specs=[pl.BlockSpec((ts, te), lambda e, k: (k, e)), pl.BlockSpec((ts, D_MODEL), lambda e, k: (k, 0))],
        out_specs=pl.BlockSpec((te, D_MODEL), lambda e, k: (e, 0)),
        out_shape=jax.ShapeDtypeStruct((D_MODEL, D_MODEL), F32),
        scratch_shapes=[pltpu.VMEM((te, D_MODEL), F32)],
        compiler_params=_params(("parallel", "arbitrary"), 48),
    )(y, dzb)


GRAD_X_LATE_SHARDS = 1
GRAD_X_PARTIAL_ROWS = 1024


def _grad_x_partial(dh, w_in_g, first, tiles, prev=None, comm=None):
    seq = dh.shape[0]
    tm, tk = GRAD_X_PARTIAL_ROWS, SHARD_IN

    def body(*refs):
        dh_ref, w_ref, o_ref = refs[-3:]
        k = pl.program_id(1)
        part = _dot_nt(dh_ref[...], w_ref[...])

        @pl.when(k == 0)
        def _():
            o_ref[...] = part

        @pl.when(k > 0)
        def _():
            o_ref[...] += part

    carried = [] if prev is None else [prev]
    (partial,), exchanged = _call(
        body, name="grad_x_partial_%d" % first, grid=(tiles, N_SHARDS - GRAD_X_LATE_SHARDS),
        in_specs=[ANY] * len(carried) + [
            pl.BlockSpec((tm, tk), lambda i, k: (i + first, k)),
            pl.BlockSpec((None, D_MODEL, tk), lambda i, k: (k, 0, 0))],
        out_specs=[pl.BlockSpec((tm, D_MODEL), lambda i, k: (i + first, 0))],
        out_shape=[jax.ShapeDtypeStruct((seq, D_MODEL), F32)],
        scratch_shapes=[], semantics=("parallel", "arbitrary"), vmem_mib=48, args=(*carried, dh, w_in_g),
        aliases={0: 0} if carried else None, comm=comm)
    return partial, exchanged


def _grad_x_final(dh, w_in_g, dz, partial):
    seq = dh.shape[0]
    tm, tk = 512, SHARD_IN
    k0 = N_SHARDS - GRAD_X_LATE_SHARDS

    def body(dh_ref, w_ref, dz_ref, p_ref, o_ref):
        k = pl.program_id(1)
        part = _dot_nt(dh_ref[...], w_ref[...])

        @pl.when(k == 0)
        def _():
            o_ref[...] = (DEEPNORM_ALPHA * dz_ref[...] + p_ref[...]) + part

        @pl.when(k > 0)
        def _():
            o_ref[...] += part

    row = pl.BlockSpec((tm, D_MODEL), lambda i, k: (i, 0))
    return _pallas(
        body, name="grad_x_final", grid=(seq // tm, GRAD_X_LATE_SHARDS),
        in_specs=[pl.BlockSpec((tm, tk), lambda i, k: (i, k + k0)),
                  pl.BlockSpec((None, D_MODEL, tk), lambda i, k: (k + k0, 0, 0)), row, row],
        out_specs=row, out_shape=jax.ShapeDtypeStruct((seq, D_MODEL), F32),
        compiler_params=_params(("parallel", "arbitrary"), 48),
    )(dh, w_in_g, dz, partial)


def _pool_weight(w_pool_sh):
    n_groups = len(POOL_WINDOWS)
    shard_c = POOL_GROUP_DIM // N_SHARDS
    return (w_pool_sh.reshape(N_SHARDS, n_groups, shard_c, POOL_GROUP_DIM).transpose(1, 0, 2, 3)
            .reshape(n_groups, POOL_GROUP_DIM, POOL_GROUP_DIM))


def _pool_grad_pieces(g_w_pool):
    n_groups = len(POOL_WINDOWS)
    half_c = POOL_GROUP_DIM // N_SHARDS // 2
    return (g_w_pool.reshape(n_groups, N_SHARDS, 2, half_c, POOL_GROUP_DIM).transpose(1, 2, 0, 3, 4)
            .reshape(N_SHARDS, 2, n_groups * half_c, POOL_GROUP_DIM))


def _step(x, target, w_in_g, w_rest, pool_scale, gain, bias, place=None):
    seq = x.shape[0]
    tabs = _rope_tables(seq)
    if place is None:
        qkv, _ = _in_proj_qkv(x, w_in_g, tabs)
        hug, _ = _in_proj_pool_gate(x, w_in_g)
    else:
        qkv, hug, w_in_g = _in_proj_gathering(x, w_in_g, tabs, place[3])
    gather = lambda bufs, phase: _allgather_weights(bufs, phase) if place else None
    o_1, st_1, gathered = _attn_fwd(qkv[0], "attn_fwd_d1", gather(w_rest, "neighbours"))
    o_4, st_4, gathered = _attn_fwd(qkv[1], "attn_fwd_d4", gather(gathered, "diagonal"))
    o_16, st_16, _ = _attn_fwd(qkv[2], "attn_fwd_d16")
    o_list, st_list = [o_1, o_4, o_16], [st_1, st_4, st_16]
    w_out_g, w_pool_sh = gathered if place else w_rest
    w_pool_g = _pool_weight(w_pool_sh)
    y, mixpre, lse_all, pooled = _mix_gate(o_list, st_list, hug, w_pool_g, pool_scale)
    dz, dzb, gain_part, bias_part, loss_part = _out_proj_loss(y, w_out_g, x, target, gain, bias)
    dh, dpo, do_list, stat_list = _dy_gate_bwd(dzb, w_out_g, hug, mixpre, pool_scale, lse_all)
    g_w_out = _grad_w_out(y, dzb)
    dh, g_w_pool, scale_part = _pool_bwd(dh, dpo, mixpre, pooled, w_pool_g, pool_scale)
    small = jnp.concatenate([scale_part, gain_part, bias_part, loss_part], axis=1)
    early = [g_w_out.reshape(N_SHARDS, 2, D_MODEL // (2 * N_SHARDS), D_MODEL), _pool_grad_pieces(g_w_pool)]

    bwd = lambda p, comm: _attn_bwd(qkv[p], do_list[p], stat_list[p], "attn_bwd_d%d" % DILATIONS[p], comm)
    if place is None:
        parts = [bwd(p, None)[0] for p in range(3)]
    else:
        core, chip_core, onward = place[:3]
        part_a, recv = bwd(0, _exchange_halves(early))
        sums = [_add_own_half(g, r, core, "add_own_half_%d" % a) for a, (g, r) in enumerate(zip(early, recv))]
        part_b, recv = bwd(1, _scatter_to_chips([s[1] for s in sums]))
        bufs = [_add_chips(s[0], r, chip_core, "add_chips_%d" % a) for a, (s, r) in enumerate(zip(sums, recv))]
        part_c, early = bwd(2, _share_with_sibling(bufs))
        parts = [part_a, part_b, part_c]
    dh, gathered = _sum_patterns(dh, [t[0] for t in parts], tabs, True, 0, "sum_dq",
                                 _gather_small(small) if place else None)
    dh, _ = _sum_patterns(dh, [t[1] for t in parts], tabs, True, 1, "sum_dk")
    dh, _ = _sum_patterns(dh, [t[2] for t in parts], tabs, False, 2, "sum_dv")
    if place:
        small = (small, gathered[0])
    if place is None:
        halves = [_grad_w_in(x, dh, jnp.full((1,), h, jnp.int32), "grad_w_in_%d" % h)[0] for h in range(2)]
        g_w_in = jnp.stack(halves, axis=1)
        g_x = _grad_x_final(dh, w_in_g, dz, _grad_x_partial(dh, w_in_g, 0, seq // GRAD_X_PARTIAL_ROWS)[0])
    else:
        give, _ = _grad_w_in(x, dh, 1 - core, "grad_w_in_give")
        keep, recv = _grad_w_in(x, dh, core, "grad_w_in_keep", _send_to_sibling([give]))
        total, total_b = _add_pair(keep, recv[0], "add_own_half_w_in")
        n_tiles = seq // GRAD_X_PARTIAL_ROWS
        tiles = max(n_tiles // 4, 1)
        part, relayed = _grad_x_partial(dh, w_in_g, 0, tiles, None, _relay_diagonal(total_b))
        total_b = _fold_relayed(total, total_b, relayed[0], onward)
        part, recv = _grad_x_partial(dh, w_in_g, tiles, n_tiles - tiles, part, _scatter_to_neighbours(total_b))
        buf = _add_chips(total, recv[0], chip_core, "add_chips_w_in")
        g_x = _grad_x_final(dh, w_in_g, dz, part)
        g_w_in = _run_exchange(_share_with_sibling([buf]), "share_w_in")[0]
    return g_x, g_w_in, early[0], early[1], small


def _exchange_halves(grads):
    n = len(grads)

    def copies(src, dst, sems):
        x, y, c, _ = _mesh_place()
        return [_remote(src[a].at[j, 1 - c], dst[a].at[j], sems[0].at[a, j], sems[1].at[a, j], (x, y, 1 - c))
                for a in range(n) for j in range(N_SHARDS)]

    def start(src, dst, sems):
        for cp in copies(src, dst, sems):
            cp.start()

    def finish(src, dst, sems):
        for cp in copies(src, dst, sems):
            cp.wait()

    return _Exchange(grads, [jax.ShapeDtypeStruct((N_SHARDS,) + g.shape[2:], g.dtype) for g in grads], {},
                     [pltpu.SemaphoreType.DMA((n, N_SHARDS))] * 2, start, finish)


def _add_own_half(grad, recv, core, name):
    _, _, r, c = grad.shape
    tr = min(r, 256)

    def body(core_ref, g_ref, r_ref, o_ref, ob_ref):
        tot = g_ref[...] + r_ref[...]
        o_ref[...] = tot
        ob_ref[...] = tot.astype(BF16)

    out = pl.BlockSpec((None, tr, c), lambda j, i, core_ref: (j, i, 0))
    return _pallas(
        body, name=name,
        grid_spec=pltpu.PrefetchScalarGridSpec(
            num_scalar_prefetch=1, grid=(N_SHARDS, r // tr),
            in_specs=[pl.BlockSpec((None, None, tr, c), lambda j, i, core_ref: (j, core_ref[0], i, 0)),
                      pl.BlockSpec((None, tr, c), lambda j, i, core_ref: (j, i, 0))],
            out_specs=[out, out]),
        out_shape=[jax.ShapeDtypeStruct((N_SHARDS, r, c), F32), jax.ShapeDtypeStruct((N_SHARDS, r, c), BF16)],
        compiler_params=_params(("parallel", "parallel"), 32),
    )(core, grad, recv)


def _send_to_sibling(arrays):
    n = len(arrays)

    def copies(src, dst, sems):
        x, y, c, _ = _mesh_place()
        return [_remote(src[a], dst[a], sems[0].at[a], sems[1].at[a], (x, y, 1 - c)) for a in range(n)]

    def start(src, dst, sems):
        for cp in copies(src, dst, sems):
            cp.start()

    def finish(src, dst, sems):
        for cp in copies(src, dst, sems):
            cp.wait()

    return _Exchange(arrays, [jax.ShapeDtypeStruct(t.shape, t.dtype) for t in arrays], {},
                     [pltpu.SemaphoreType.DMA((n,))] * 2, start, finish)


def _add_pair(a, b, name):
    _, r, c = a.shape
    tr = min(r, 256)

    def body(a_ref, b_ref, o_ref, ob_ref):
        tot = a_ref[...] + b_ref[...]
        o_ref[...] = tot
        ob_ref[...] = tot.astype(BF16)

    spec = pl.BlockSpec((None, tr, c), lambda j, i: (j, i, 0))
    return _pallas(
        body, name=name, grid=(N_SHARDS, r // tr), in_specs=[spec, spec], out_specs=[spec, spec],
        out_shape=[jax.ShapeDtypeStruct(a.shape, F32), jax.ShapeDtypeStruct(a.shape, BF16)],
        compiler_params=_params(("parallel", "parallel"), 32),
    )(a, b)


def _scatter_to_chips(sums, rows=None, into=None):
    n = len(sums)

    def copies(src, dst, sems):
        x, y, c, chips = _mesh_place()
        part = (lambda ref: ref) if rows is None else (lambda ref: ref.at[pl.ds(rows[0], rows[1])])
        return [_remote(part(src[a].at[2 * cx + cy]), part(dst[a].at[k]), sems[0].at[a, k], sems[1].at[a, k],
                        (cx, cy, c))
                for a in range(n) for k, (cx, cy) in enumerate(chips)]

    def start(src, dst, sems):
        for cp in copies(src, dst, sems):
            cp.start()

    def finish(src, dst, sems):
        for cp in copies(src, dst, sems):
            cp.wait()

    return _Exchange(sums + (into or []), [jax.ShapeDtypeStruct((3,) + s.shape[1:], s.dtype) for s in sums],
                     {n + a: a for a in range(n)} if into else {},
                     [pltpu.SemaphoreType.DMA((n, 3))] * 2, start, finish)


def _add_chips(sums, recv, chip_core, name):
    _, r, c = sums.shape
    n_recv = recv.shape[0]
    tr = min(r, 256)

    def body(cc_ref, s_ref, r_ref, o_ref):
        tot = s_ref[...]
        for k in range(n_recv):
            tot = tot + r_ref[k].astype(F32)
        o_ref[...] = tot

    return _pallas(
        body, name=name,
        grid_spec=pltpu.PrefetchScalarGridSpec(
            num_scalar_prefetch=1, grid=(r // tr,),
            in_specs=[pl.BlockSpec((None, tr, c), lambda i, cc_ref: (cc_ref[0], i, 0)),
                      pl.BlockSpec((n_recv, tr, c), lambda i, cc_ref: (0, i, 0))],
            out_specs=pl.BlockSpec((None, tr, c), lambda i, cc_ref: (cc_ref[1], i, 0))),
        out_shape=jax.ShapeDtypeStruct((2, r, c), F32),
        compiler_params=_params(("parallel",), 32),
    )(chip_core, sums, recv)


def _relay_diagonal(sums_b):
    def copy(src, dst, sems):
        x, y, c, _ = _mesh_place()
        diagonal = 2 * (1 - x) + (1 - y)
        return _remote(src[0].at[diagonal], dst[0], sems[0].at[0], sems[1].at[0], (x ^ (1 - c), y ^ c, c))

    def start(src, dst, sems):
        copy(src, dst, sems).start()

    def finish(src, dst, sems):
        copy(src, dst, sems).wait()

    return _Exchange([sums_b], [jax.ShapeDtypeStruct(sums_b.shape[1:], sums_b.dtype)], {},
                     [pltpu.SemaphoreType.DMA((1,))] * 2, start, finish)


def _fold_relayed(sums, sums_b, relayed, onward):
    _, r, c = sums.shape
    tr = min(r, 256)

    def body(on_ref, b_in_ref, s_ref, r_ref, o_ref):
        o_ref[...] = (s_ref[...] + r_ref[...].astype(F32)).astype(BF16)

    return _pallas(
        body, name="fold_relayed",
        grid_spec=pltpu.PrefetchScalarGridSpec(
            num_scalar_prefetch=1, grid=(r // tr,),
            in_specs=[ANY, pl.BlockSpec((None, tr, c), lambda i, on_ref: (on_ref[0], i, 0)),
                      pl.BlockSpec((tr, c), lambda i, on_ref: (i, 0))],
            out_specs=pl.BlockSpec((None, tr, c), lambda i, on_ref: (on_ref[0], i, 0))),
        out_shape=jax.ShapeDtypeStruct(sums_b.shape, sums_b.dtype),
        input_output_aliases={1: 0},
        compiler_params=_params(("parallel",), 32),
    )(onward, sums_b, sums, relayed)


def _scatter_to_neighbours(sums_b):
    def copies(src, dst, sems):
        x, y, c, chips = _mesh_place()
        return [_remote(src[0].at[2 * cx + cy], dst[0].at[k], sems[0].at[k], sems[1].at[k], (cx, cy, c))
                for k, (cx, cy) in enumerate(chips[:2])]

    def start(src, dst, sems):
        for cp in copies(src, dst, sems):
            cp.start()

    def finish(src, dst, sems):
        for cp in copies(src, dst, sems):
            cp.wait()

    return _Exchange([sums_b], [jax.ShapeDtypeStruct((2,) + sums_b.shape[1:], sums_b.dtype)], {},
                     [pltpu.SemaphoreType.DMA((2,))] * 2, start, finish)


def _share_with_sibling(bufs):
    n = len(bufs)

    def copies(dst, sems, half):
        x, y, c, _ = _mesh_place()
        h = c if half == "mine" else 1 - c
        return [_remote(dst[a].at[h], dst[a].at[h], sems[0].at[a], sems[1].at[a], (x, y, 1 - c)) for a in range(n)]

    def start(ins, dst, sems):
        for cp in copies(dst, sems, "mine"):
            cp.start()

    def finish(ins, dst, sems):
        for cp in copies(dst, sems, "theirs"):
            cp.wait_recv()
        for cp in copies(dst, sems, "mine"):
            cp.wait_send()

    return _Exchange(bufs, [jax.ShapeDtypeStruct(b.shape, b.dtype) for b in bufs], {a: a for a in range(n)},
                     [pltpu.SemaphoreType.DMA((n,))] * 2, start, finish)


def _adam_math(w, g, m, v):
    m = ADAM_B1 * m + (1.0 - ADAM_B1) * g
    v = ADAM_B2 * v + (1.0 - ADAM_B2) * (g * g)
    m_hat = m / (1.0 - ADAM_B1 ** ADAM_STEP)
    v_hat = v / (1.0 - ADAM_B2 ** ADAM_STEP)
    delta = -ADAM_LR * (m_hat / (jnp.sqrt(v_hat) + ADAM_EPS) + ADAM_WD * w)
    return delta, m, v


def _gather_small(small):
    def peers():
        x, y, c, _ = _mesh_place()
        return [(x ^ ((r >> 2) & 1), y ^ ((r >> 1) & 1), c ^ (r & 1)) for r in range(1, 8)], 4 * x + 2 * y + c

    def start(src, dst, sems):
        to, me = peers()
        for r, peer in enumerate(to):
            _remote(src[0], dst[0].at[me], sems[0].at[r], sems[1].at[r], peer).start()

    def finish(src, dst, sems):
        to, me = peers()
        for r, (px, py, pc) in enumerate(to):
            theirs = dst[0].at[4 * px + 2 * py + pc]
            _remote(theirs, theirs, sems[0].at[r], sems[1].at[r], (px, py, pc)).wait_recv()
        for r, peer in enumerate(to):
            _remote(src[0], dst[0].at[me], sems[0].at[r], sems[1].at[r], peer).wait_send()

    return _Exchange([small], [jax.ShapeDtypeStruct((8,) + small.shape, small.dtype)], {},
                     [pltpu.SemaphoreType.DMA((7,))] * 2, start, finish)


def _small_adamw(gathered, small, me, w_vec, m_vec, v_vec):
    n_par = w_vec.shape[1]

    def body(me_ref, a_ref, s_ref, w_ref, m_ref, v_ref, loss_ref, g_ref, d_ref, nm_ref, nv_ref):
        mine = s_ref[...]
        tot = jnp.where(me_ref[0] == 0, mine, a_ref[0])
        for d in range(1, 8):
            tot = tot + jnp.where(me_ref[0] == d, mine, a_ref[d])
        tot = jnp.sum(tot, axis=0, keepdims=True)
        sq = jnp.sum(tot[:, n_par:], axis=1, keepdims=True)
        loss_ref[...] = jnp.broadcast_to(sq * (0.5 / D_MODEL), loss_ref.shape)
        g = tot[:, :n_par]
        g_ref[...] = g
        d_ref[...], nm_ref[...], nv_ref[...] = _adam_math(w_ref[...], g, m_ref[...], v_ref[...])

    vm = pl.BlockSpec(memory_space=pltpu.VMEM)
    vec = jax.ShapeDtypeStruct((1, n_par), F32)
    return pl.pallas_call(
        body, name="small_adamw",
        grid_spec=pltpu.PrefetchScalarGridSpec(num_scalar_prefetch=1, grid=(), in_specs=[vm] * 5, out_specs=[vm] * 5),
        out_shape=[jax.ShapeDtypeStruct((1, 128), F32), vec, vec, vec, vec],
    )(me, gathered, small, w_vec, m_vec, v_vec)


def _adamw(w, g, m, v, name):
    r, c = w.shape
    tr = min(r, 256)

    def body(w_ref, g_ref, m_ref, v_ref, d_ref, nm_ref, nv_ref):
        d_ref[...], nm_ref[...], nv_ref[...] = _adam_math(w_ref[...], g_ref[...], m_ref[...], v_ref[...])

    spec = pl.BlockSpec((tr, c), lambda i: (i, 0))
    shape = jax.ShapeDtypeStruct((r, c), F32)
    return _pallas(
        body, name=name, grid=(r // tr,),
        in_specs=[spec] * 4, out_specs=[spec] * 3, out_shape=[shape] * 3,
        compiler_params=_params(("parallel",), 48),
    )(w, g, m, v)


def kernel(x, w_in, w_pool, pool_scale, w_out, ln_gain, ln_bias, loss_target, m_w_in, m_w_pool, m_pool_scale, m_w_out, m_ln_gain, m_ln_bias, v_w_in, v_w_pool, v_pool_scale, v_w_out, v_ln_gain, v_ln_bias):
    xi, yi, ci = lax.axis_index("x"), lax.axis_index("y"), lax.axis_index("c")
    chip = (2 * xi + yi).astype(jnp.int32).reshape(1)
    core = ci.astype(jnp.int32).reshape(1)
    n_groups = len(POOL_WINDOWS)
    shard_c = w_pool.shape[2]

    w_in_b = _cast_bf16(w_in[0], chip, "cast_w_in", 256)
    w_out_b = _cast_bf16(w_out[0], chip, "cast_w_out", 256)
    w_pool_b = _cast_bf16(w_pool[0].reshape(n_groups * shard_c, POOL_GROUP_DIM), chip, "cast_w_pool", 256)

    chip_core = jnp.concatenate([chip, core])
    onward = (2 * (xi ^ ci) + (yi ^ (1 - ci))).astype(jnp.int32).reshape(1)
    g_x, full_in, full_out, full_pool, small = _step(
        x[0], loss_target[0], w_in_b, [w_out_b, w_pool_b], pool_scale, ln_gain, ln_bias,
        (core, chip_core, onward, _in_proj_plan(xi, yi)))
    half_c = shard_c // 2
    grad_w_in = full_in.reshape(D_MODEL, SHARD_IN)
    grad_w_out = full_out.reshape(D_MODEL // N_SHARDS, D_MODEL)
    grad_w_pool = (full_pool.reshape(2, n_groups, half_c, POOL_GROUP_DIM).transpose(1, 0, 2, 3)
                   .reshape(n_groups * shard_c, POOL_GROUP_DIM))

    d_in, nm_in, nv_in = _adamw(w_in[0], grad_w_in, m_w_in[0], v_w_in[0], "adamw_w_in")
    d_out, nm_out, nv_out = _adamw(w_out[0], grad_w_out, m_w_out[0], v_w_out[0], "adamw_w_out")
    flat = lambda t: t[0].reshape(n_groups * shard_c, POOL_GROUP_DIM)
    d_pool, nm_pool, nv_pool = _adamw(flat(w_pool), grad_w_pool, flat(m_w_pool), flat(v_w_pool), "adamw_w_pool")

    cat = lambda a, b, c: jnp.concatenate([a, b, c], axis=1)
    me = (4 * xi + 2 * yi + ci).astype(jnp.int32).reshape(1)
    loss_v, g_vec, d_vec, nm_vec, nv_vec = _small_adamw(
        small[1], small[0], me, cat(pool_scale, ln_gain, ln_bias), cat(m_pool_scale, m_ln_gain, m_ln_bias),
        cat(v_pool_scale, v_ln_gain, v_ln_bias))

    def split(vec):
        return vec[:, :D_POOL], vec[:, D_POOL:D_POOL + D_MODEL], vec[:, D_POOL + D_MODEL:]

    g_scale, g_gain, g_bias = split(g_vec)
    d_scale, d_gain, d_bias = split(d_vec)
    nm_scale, nm_gain, nm_bias = split(nm_vec)
    nv_scale, nv_gain, nv_bias = split(nv_vec)
    pool_shape = w_pool.shape
    return (loss_v[0, 0], g_x[None],
            grad_w_in[None], grad_w_pool.reshape(pool_shape), g_scale, grad_w_out[None], g_gain, g_bias,
            d_in[None], d_pool.reshape(pool_shape), d_scale, d_out[None], d_gain, d_bias,
            nm_in[None], nm_pool.reshape(pool_shape), nm_scale, nm_out[None], nm_gain, nm_bias,
            nv_in[None], nv_pool.reshape(pool_shape), nv_scale, nv_out[None], nv_gain, nv_bias)
```

```python
import functools

import jax
import jax.numpy as jnp
from jax import lax
from jax.experimental import pallas as pl
from jax.experimental.pallas import tpu as pltpu

F32 = jnp.float32
BF16 = jnp.bfloat16
MESH = pl.DeviceIdType.MESH
ANY = pl.BlockSpec(memory_space=pl.ANY)

D_MODEL = 2048
D_ATTN = 1024
D_POOL = 1024
HEAD_DIM = 128
N_HEADS = 8
ROPE_DIM = 32
ROPE_THETA = 500000.0
DILATIONS = (1, 4, 16)
KEY_BLOCK = 128
CHUNK = 2 * KEY_BLOCK
STAT_LANES = 128
POOL_WINDOWS = (2, 4, 8, 16)
POOL_GROUP_DIM = 256
POOL_HALO = 16
D_QKV = 3 * D_ATTN
D_UG = D_POOL + D_MODEL
D_IN = D_QKV + D_UG
N_SHARDS = 4
SHARD_IN = D_IN // N_SHARDS
LN_EPS = 1e-5
DEEPNORM_ALPHA = 2.0 ** 0.25
ADAM_LR = 0.001
ADAM_B1 = 0.9
ADAM_B2 = 0.999
ADAM_EPS = 1e-08
ADAM_WD = 0.01
ADAM_STEP = 10
NEG = -1e30
MIB = 1024 * 1024


def _params(sem, vmem_mib):
    return pltpu.CompilerParams(dimension_semantics=sem, vmem_limit_bytes=vmem_mib * MIB)


def _pallas(body, **kwargs):
    pin = lambda s: pltpu.HBM(s.shape, s.dtype) if len(s.shape) >= 2 else s
    out_shape = kwargs.pop("out_shape")
    out_shape = [pin(s) for s in out_shape] if isinstance(out_shape, (list, tuple)) else pin(out_shape)
    call = pl.pallas_call(body, out_shape=out_shape, **kwargs)

    def run(*operands):
        return call(*[pltpu.with_memory_space_constraint(o, pltpu.HBM) if o.ndim >= 2 else o for o in operands])

    return run


class _Exchange:
    def __init__(self, operands, out_shape, aliases, sems, start, finish):
        self.operands, self.out_shape, self.aliases, self.sems = list(operands), list(out_shape), dict(aliases), list(sems)
        self.start, self.finish = start, finish


def _run_exchange(comm, name):
    n_in, n_out = len(comm.operands), len(comm.out_shape)

    def body(*refs):
        ins, outs, sems = refs[:n_in], refs[n_in:n_in + n_out], refs[n_in + n_out:]
        comm.start(ins, outs, sems)
        comm.finish(ins, outs, sems)

    return _pallas(
        body, name=name, in_specs=[ANY] * n_in, out_specs=[ANY] * n_out, out_shape=comm.out_shape,
        input_output_aliases=comm.aliases, scratch_shapes=comm.sems,
    )(*comm.operands)


def _call(body, *, name, grid, in_specs, out_specs, out_shape, scratch_shapes, semantics, vmem_mib, args,
          aliases=None, comm=None, prefetch=()):
    aliases = dict(aliases or {})
    n_pre, n_in, n_out, n_scr = len(prefetch), len(in_specs), len(out_specs), len(scratch_shapes)
    c_in, c_out = (len(comm.operands), len(comm.out_shape)) if comm else (0, 0)
    c_shapes, c_sems, c_operands = (comm.out_shape, comm.sems, comm.operands) if comm else ([], [], [])

    def hosted(*refs):
        pre, refs = refs[:n_pre], refs[n_pre:]
        a = n_in
        b = a + c_in
        c = b + n_out
        d = c + c_out
        e = d + n_scr
        if comm is None:
            body(*pre, *refs)
            return
        ids = [pl.program_id(k) for k in range(len(grid))]
        first = functools.reduce(jnp.logical_and, [i == 0 for i in ids])
        last = functools.reduce(jnp.logical_and, [i == g - 1 for i, g in zip(ids, grid)])

        @pl.when(first)
        def _():
            comm.start(refs[a:b], refs[c:d], refs[e:])

        body(*pre, *refs[:a], *refs[b:c], *refs[d:e])

        @pl.when(last)
        def _():
            comm.finish(refs[a:b], refs[c:d], refs[e:])

    if comm:
        semantics = ("arbitrary",) * len(grid)
        for i, o in comm.aliases.items():
            aliases[n_pre + n_in + i] = n_out + o
    outs = _pallas(
        hosted, name=name,
        grid_spec=pltpu.PrefetchScalarGridSpec(
            num_scalar_prefetch=n_pre, grid=grid, in_specs=list(in_specs) + [ANY] * c_in,
            out_specs=list(out_specs) + [ANY] * c_out, scratch_shapes=list(scratch_shapes) + c_sems),
        out_shape=list(out_shape) + c_shapes, input_output_aliases=aliases,
        compiler_params=_params(semantics, vmem_mib),
    )(*prefetch, *args, *c_operands)
    return list(outs[:n_out]), list(outs[n_out:])


def _dot_nn(a, b):
    return jnp.dot(a, b, preferred_element_type=F32)


def _dot_nt(a, b):
    return lax.dot_general(a, b, (((1,), (1,)), ((), ())), preferred_element_type=F32)


def _dot_tn(a, b):
    return lax.dot_general(a, b, (((0,), (0,)), ((), ())), preferred_element_type=F32)


def _fold_rows(a):
    r, c = a.shape
    return jnp.sum(a.reshape(r // 8, 8, c), axis=0)


def _cast_bf16(a, chip, name, rows):
    r, c = a.shape

    def body(chip_ref, a_ref, o_ref):
        o_ref[...] = a_ref[...].astype(BF16)

    return _pallas(
        body, name=name,
        grid_spec=pltpu.PrefetchScalarGridSpec(
            num_scalar_prefetch=1, grid=(r // rows,),
            in_specs=[pl.BlockSpec((rows, c), lambda i, chip_ref: (i, 0))],
            out_specs=pl.BlockSpec((None, rows, c), lambda i, chip_ref: (chip_ref[0], i, 0))),
        out_shape=jax.ShapeDtypeStruct((N_SHARDS, r, c), BF16),
        compiler_params=_params(("parallel",), 32),
    )(chip, a)


def _mesh_place():
    x, y, c = lax.axis_index("x"), lax.axis_index("y"), lax.axis_index("c")
    return x, y, c, [(1 - x, y), (x, 1 - y), (1 - x, 1 - y)]


def _remote(src, dst, send_sem, recv_sem, to):
    return pltpu.make_async_remote_copy(src_ref=src, dst_ref=dst, send_sem=send_sem, recv_sem=recv_sem,
                                        device_id=to, device_id_type=MESH)


def _rope_tables(seq):
    half = ROPE_DIM // 2
    inv_freq = ROPE_THETA ** (-(2.0 * jnp.arange(half, dtype=F32)) / ROPE_DIM)
    ang = jnp.arange(seq, dtype=jnp.int32).astype(F32)[:, None] * inv_freq[None, :]
    cos, sin = jnp.cos(ang), jnp.sin(ang)
    pad = jnp.zeros((seq, HEAD_DIM - ROPE_DIM), F32)
    zeros = jnp.zeros((seq, half), F32)
    c_tab = jnp.concatenate([cos, cos, pad + 1.0], axis=1)
    up_tab = jnp.concatenate([-sin, zeros, pad], axis=1)
    down_tab = jnp.concatenate([zeros, sin, pad], axis=1)
    return c_tab, up_tab, down_tab


def _rotate_heads(t, c_tab, up_tab, down_tab):
    outs = []
    for h in range(t.shape[1] // HEAD_DIM):
        th = t[:, h * HEAD_DIM:(h + 1) * HEAD_DIM]
        up = pltpu.roll(th, HEAD_DIM - ROPE_DIM // 2, axis=1)
        down = pltpu.roll(th, ROPE_DIM // 2, axis=1)
        outs.append(th * c_tab + up * up_tab + down * down_tab)
    return outs[0] if len(outs) == 1 else jnp.concatenate(outs, axis=1)


def _to_pattern(slabs_ref, dst_ref, dil, dtype):
    n_slabs, rows, _ = slabs_ref.shape
    for s in range(n_slabs):
        for r in range(dil):
            dst_ref[r, :, s * 128:(s + 1) * 128] = slabs_ref[s, pl.ds(r, rows // dil, dil), :].astype(dtype)


def _from_pattern(src_ref, slabs_ref, dil):
    n_slabs, rows, _ = slabs_ref.shape
    for s in range(n_slabs):
        for r in range(dil):
            slabs_ref[s, pl.ds(r, rows // dil, dil), :] = src_ref[r, :, s * 128:(s + 1) * 128].astype(F32)


def _store_slabs(slabs_ref, value):
    for s in range(slabs_ref.shape[0]):
        slabs_ref[s] = value[:, s * 128:(s + 1) * 128]


W_IN_CHUNKS = 4


def _in_proj_plan(x, y):
    shards = [2 * x + y, 2 * (1 - x) + y, 2 * x + (1 - y), 2 * (1 - x) + (1 - y)]
    last_row = jnp.int32(-2)

    def table(active, col_of):
        cols, rows = [], []
        first_col = functools.reduce(lambda acc, j: jnp.where(active[j], col_of(shards[j]), acc), reversed(range(4)),
                                     jnp.int32(0))
        held_col, seen = first_col, jnp.bool_(False)
        for j in range(4):
            cols.append(jnp.where(active[j], col_of(shards[j]), held_col))
            rows.append(jnp.where(active[j], -1, jnp.where(seen, last_row, 0)))
            held_col = jnp.where(active[j], col_of(shards[j]), held_col)
            seen = jnp.logical_or(seen, active[j])
        return cols, rows

    q_cols, q_rows = table([s < 2 for s in shards], lambda s: s)
    h_cols, h_rows = table([s >= 2 for s in shards], lambda s: s - 2)
    return jnp.stack([jnp.asarray(v, jnp.int32) for v in shards + q_cols + q_rows + h_cols + h_rows])


def _in_proj_gathering(x, w_bufs, tabs, plan):
    seq = x.shape[0]
    tm, tn = 512, SHARD_IN
    n_tiles = seq // tm
    heads = tn // HEAD_DIM
    k_heads_in_second = 2 * D_ATTN // HEAD_DIM - heads
    d4, d16 = DILATIONS[1], DILATIONS[2]
    DIAGONAL = 2
    chunk = D_MODEL // 2 // W_IN_CHUNKS
    early = [(0, D_MODEL // 2, q * chunk, chunk) for q in range(W_IN_CHUNKS)]
    late = [(a, w_bufs[a].shape[1] // 2, 0, w_bufs[a].shape[1] // 2) for a in (1, 2)]
    pieces = early + late
    early_ids, late_ids = range(len(early)), range(len(early), len(pieces))

    def body(plan_ref, x_ref, w_in_in, w_out_in, w_pool_in, c_ref, up_ref, down_ref,
             o1_ref, o4_ref, o16_ref, hug_ref, w_ref, w_out_ref, w_pool_ref,
             wbuf_ref, res_ref, w_sem, ici_send, ici_recv, d2d_send, d2d_recv):
        j, i = pl.program_id(0), pl.program_id(1)
        mx, my, mc, chips = _mesh_place()
        sibling = (mx, my, 1 - mc)
        gathered = (w_ref, w_out_ref, w_pool_ref)
        chip_of = lambda k: 2 * chips[k][0] + chips[k][1]

        def piece(n, chip, core):
            a, per_core, offset, size = pieces[n]
            return gathered[a].at[chip, pl.ds(core * per_core + offset, size)]

        def to_neighbour(k, n):
            mine = piece(n, 2 * mx + my, mc)
            return _remote(mine, mine, ici_send.at[n, k], ici_recv.at[n, k], (*chips[k], mc))

        def relay(n):
            theirs = piece(n, 2 * (mx ^ (1 - mc)) + (my ^ mc), mc)
            return _remote(theirs, theirs, ici_send.at[n, DIAGONAL], ici_recv.at[n, DIAGONAL], (mx ^ mc, my ^ (1 - mc), mc))

        def arrival(k, n):
            theirs = piece(n, chip_of(k), mc)
            return _remote(theirs, theirs, ici_send.at[n, k], ici_recv.at[n, k], (*chips[k], mc))

        def to_sibling(k, n, core):
            theirs = piece(n, chip_of(k), core)
            return _remote(theirs, theirs, d2d_send.at[n, k], d2d_recv.at[n, k], sibling)

        def take(k, ids):
            for n in ids:
                arrival(k, n).wait_recv()
                to_sibling(k, n, mc).start()

        def taken(k, ids):
            for n in ids:
                to_sibling(k, n, 1 - mc).wait_recv()

        first_tile = i == 0

        @pl.when(jnp.logical_and(j == 0, first_tile))
        def _():
            for n in range(len(pieces)):
                for k in range(DIAGONAL):
                    to_neighbour(k, n).start()

        @pl.when(jnp.logical_and(j == 1, first_tile))
        def _():
            take(0, early_ids)
            taken(0, early_ids)

        @pl.when(jnp.logical_and(j == 2, first_tile))
        def _():
            take(1, early_ids)
            for n in early_ids:
                relay(n).start()
            taken(1, early_ids)
            for k in range(DIAGONAL):
                take(k, late_ids)
            for n in late_ids:
                relay(n).start()
            for k in range(DIAGONAL):
                taken(k, late_ids)

        @pl.when(jnp.logical_and(j == 3, first_tile))
        def _():
            take(DIAGONAL, range(len(pieces)))
            taken(DIAGONAL, range(len(pieces)))

        shard = plan_ref[j]

        @pl.when(first_tile)
        def _():
            cp = pltpu.make_async_copy(w_ref.at[shard], wbuf_ref, w_sem)
            cp.start()
            cp.wait()

        xb = x_ref[...].astype(BF16)
        group = 4 * HEAD_DIM
        accs = [_dot_nn(xb, wbuf_ref[:, g * group:(g + 1) * group]) for g in range(tn // group)]

        def emit_qkv(rotated_heads):
            for h in range(heads):
                lanes = (h * HEAD_DIM) % group
                th = accs[h * HEAD_DIM // group][:, lanes:lanes + HEAD_DIM]
                if h < rotated_heads:
                    th = _rotate_heads(th, c_ref[...], up_ref[...], down_ref[...])
                res_ref[h] = th
                o1_ref[:, h * HEAD_DIM:(h + 1) * HEAD_DIM] = th.astype(BF16)
            _to_pattern(res_ref, o4_ref, d4, BF16)
            _to_pattern(res_ref, o16_ref, d16, BF16)

        @pl.when(shard == 0)
        def _():
            emit_qkv(heads)

        @pl.when(shard == 1)
        def _():
            emit_qkv(k_heads_in_second)

        @pl.when(shard >= 2)
        def _():
            for g, acc in enumerate(accs):
                hug_ref[:, g * group:(g + 1) * group] = acc.astype(BF16)

        @pl.when(jnp.logical_and(j == 3, i == n_tiles - 1))
        def _():
            for n in range(len(pieces)):
                for k in range(DIAGONAL):
                    to_neighbour(k, n).wait_send()
                relay(n).wait_send()
                for k in range(DIAGONAL + 1):
                    to_sibling(k, n, mc).wait_send()

    def held(base, last):
        return lambda j, i, plan_ref: jnp.where(plan_ref[base + j] == -1, i,
                                                jnp.where(plan_ref[base + j] == -2, last, 0))

    q_row, h_row = held(8, n_tiles - 1), held(16, n_tiles - 1)
    tab_spec = pl.BlockSpec((tm, HEAD_DIM), lambda j, i, plan_ref: (i, 0))
    sems = [pltpu.SemaphoreType.DMA((len(pieces), 3))] * 4
    o1, o4, o16, hug, w_in_g, w_out_g, w_pool_g = _pallas(
        body, name="in_proj_gathering",
        grid_spec=pltpu.PrefetchScalarGridSpec(
            num_scalar_prefetch=1, grid=(N_SHARDS, n_tiles),
            in_specs=[pl.BlockSpec((tm, D_MODEL), lambda j, i, plan_ref: (i, 0)), ANY, ANY, ANY,
                      tab_spec, tab_spec, tab_spec],
            out_specs=[pl.BlockSpec((tm, tn), lambda j, i, p: (q_row(j, i, p), p[4 + j])),
                       pl.BlockSpec((d4, tm // d4, tn), lambda j, i, p: (0, q_row(j, i, p), p[4 + j])),
                       pl.BlockSpec((d16, tm // d16, tn), lambda j, i, p: (0, q_row(j, i, p), p[4 + j])),
                       pl.BlockSpec((tm, tn), lambda j, i, p: (h_row(j, i, p), p[12 + j])),
                       ANY, ANY, ANY],
            scratch_shapes=[pltpu.VMEM((D_MODEL, tn), BF16), pltpu.VMEM((heads, tm, HEAD_DIM), F32),
                            pltpu.SemaphoreType.DMA(())] + sems),
        out_shape=[jax.ShapeDtypeStruct((seq, D_QKV), BF16),
                   jax.ShapeDtypeStruct((d4, seq // d4, D_QKV), BF16),
                   jax.ShapeDtypeStruct((d16, seq // d16, D_QKV), BF16),
                   jax.ShapeDtypeStruct((seq, D_UG), BF16)]
        + [jax.ShapeDtypeStruct(b.shape, b.dtype) for b in w_bufs],
        input_output_aliases={2: 4, 3: 5, 4: 6},
        compiler_params=_params(("arbitrary", "arbitrary"), 52),
    )(plan, x, *w_bufs, *tabs)
    return [o1[None], o4, o16], hug, w_in_g, w_out_g, w_pool_g


def _band_masks():
    row = lax.broadcasted_iota(jnp.int32, (KEY_BLOCK, KEY_BLOCK), 0)
    col = lax.broadcasted_iota(jnp.int32, (KEY_BLOCK, KEY_BLOCK), 1)
    return col <= row, col >= row


def _attn_fwd(qkv, name):
    dil, n, _ = qkv.shape
    scale = HEAD_DIM ** -0.5
    lo, hi = slice(0, KEY_BLOCK), slice(KEY_BLOCK, CHUNK)

    def body(q_ref, k_ref, v_ref, kb_ref, vb_ref, o_ref, st_ref):
        i = pl.program_id(1)
        cur_mask, prev_mask = _band_masks()
        before_mask = jnp.logical_and(prev_mask, i > 0)
        lane = lax.broadcasted_iota(jnp.int32, (KEY_BLOCK, STAT_LANES), 1)
        tasks = [(rows, h) for rows in (lo, hi) for h in range(N_HEADS)]
        head = lambda h: slice(h * HEAD_DIM, (h + 1) * HEAD_DIM)

        def prev_of(rows, h):
            if rows is lo:
                return kb_ref[:, head(h)], vb_ref[:, head(h)], before_mask
            return k_ref[lo, head(h)], v_ref[lo, head(h)], prev_mask

        scores = []
        for rows, h in tasks:
            q = q_ref[rows, head(h)]
            scores.append((_dot_nt(q, prev_of(rows, h)[0]), _dot_nt(q, k_ref[rows, head(h)])))
        probs = []
        for (rows, h), (qk_prev, qk_cur) in zip(tasks, scores):
            s_prev = jnp.where(prev_of(rows, h)[2], qk_prev * scale, NEG)
            s_cur = jnp.where(cur_mask, qk_cur * scale, NEG)
            m = jnp.max(jnp.maximum(s_prev, s_cur), axis=-1, keepdims=True)
            p_prev = jnp.exp(s_prev - m)
            p_cur = jnp.exp(s_cur - m)
            den = jnp.sum(p_prev + p_cur, axis=-1, keepdims=True)
            probs.append((p_prev.astype(BF16), p_cur.astype(BF16), den, m + jnp.log(den)))
        stats = [jnp.zeros((KEY_BLOCK, STAT_LANES), F32), jnp.zeros((KEY_BLOCK, STAT_LANES), F32)]
        for (rows, h), (p_prev, p_cur, den, lse) in zip(tasks, probs):
            o = _dot_nn(p_cur, v_ref[rows, head(h)]) + _dot_nn(p_prev, prev_of(rows, h)[1])
            o_ref[rows, head(h)] = (o / den).astype(BF16)
            b = 0 if rows is lo else 1
            stats[b] = jnp.where(lane == h, lse, stats[b])
        st_ref[lo, :] = stats[0]
        st_ref[hi, :] = stats[1]

    main = lambda cb: pl.BlockSpec((None, CHUNK, D_ATTN), lambda r, i: (r, i, cb))
    before = lambda cb: pl.BlockSpec((None, KEY_BLOCK, D_ATTN), lambda r, i: (r, jnp.maximum(2 * i - 1, 0), cb))
    return _pallas(
        body, name=name, grid=(dil, n // CHUNK),
        in_specs=[main(0), main(1), main(2), before(1), before(2)],
        out_specs=[main(0), pl.BlockSpec((None, CHUNK, STAT_LANES), lambda r, i: (r, i, 0))],
        out_shape=[jax.ShapeDtypeStruct((dil, n, D_ATTN), BF16), jax.ShapeDtypeStruct((dil, n, STAT_LANES), F32)],
        compiler_params=_params(("parallel", "parallel"), 40),
    )(qkv, qkv, qkv, qkv, qkv)


def _attn_bwd(qkv, do, stats, name, comm=None):
    dil, n, _ = qkv.shape
    n_blocks = n // KEY_BLOCK
    last = n // CHUNK - 1
    scale = HEAD_DIM ** -0.5
    lo, hi = slice(0, KEY_BLOCK), slice(KEY_BLOCK, CHUNK)

    def body(q_ref, k_ref, v_ref, kb_ref, vb_ref, qa_ref, do_ref, doa_ref, st_ref, sta_ref, dq_ref, dk_ref, dv_ref):
        i = pl.program_id(1)
        cur_mask, prev_mask = _band_masks()
        before_mask = jnp.logical_and(prev_mask, i > 0)
        after_mask = jnp.logical_and(prev_mask, i < last)

        rows_cat = lambda a, b: jnp.concatenate([a, b], axis=0)
        masks = (jnp.concatenate([before_mask, cur_mask], axis=1), jnp.concatenate([prev_mask, cur_mask], axis=1),
                 after_mask)

        def operands(h):
            cols = slice(h * HEAD_DIM, (h + 1) * HEAD_DIM)
            lse_c, del_c = slice(h, h + 1), slice(N_HEADS + h, N_HEADS + h + 1)
            q = (q_ref[lo, cols], q_ref[hi, cols], qa_ref[:, cols])
            do = (do_ref[lo, cols], do_ref[hi, cols], doa_ref[:, cols])
            keys = (rows_cat(kb_ref[:, cols], k_ref[lo, cols]), k_ref[:, cols], k_ref[hi, cols])
            vals = (rows_cat(vb_ref[:, cols], v_ref[lo, cols]), v_ref[:, cols], v_ref[hi, cols])
            st = ((st_ref[lo, lse_c], st_ref[lo, del_c]), (st_ref[hi, lse_c], st_ref[hi, del_c]),
                  (sta_ref[:, lse_c], sta_ref[:, del_c]))
            return cols, q, do, keys, vals, st

        group = N_HEADS // 2
        for first_head in range(0, N_HEADS, group):
            heads = range(first_head, first_head + group)
            raw = {}
            for h in heads:
                _, q, do, keys, vals, _ = operands(h)
                raw[h] = [(_dot_nt(q[j], keys[j]), _dot_nt(do[j], vals[j])) for j in range(3)]
            grads = {}
            for h in heads:
                st = operands(h)[5]
                grads[h] = []
                for j in range(3):
                    qk, dp = raw[h][j]
                    lse, delta = st[j]
                    p = jnp.exp(jnp.where(masks[j], qk * scale, NEG) - lse)
                    grads[h].append((p.astype(BF16), (p * (dp - delta) * scale).astype(BF16)))
            for h in heads:
                cols, q, do, keys, _, _ = operands(h)
                (p0, ds0), (p1, ds1), (pa, dsa) = grads[h]
                own, nxt = slice(KEY_BLOCK, CHUNK), slice(0, KEY_BLOCK)

                def put(ref, rows, val, cols=cols):
                    ref[rows, cols] = val.astype(ref.dtype)

                put(dq_ref, lo, _dot_nn(ds0, keys[0]))
                put(dq_ref, hi, _dot_nn(ds1, keys[1]))
                put(dk_ref, lo, _dot_tn(rows_cat(ds0[:, own], ds1[:, nxt]), q_ref[:, cols]))
                put(dk_ref, hi, _dot_tn(rows_cat(ds1[:, own], dsa), rows_cat(q[1], q[2])))
                put(dv_ref, lo, _dot_tn(rows_cat(p0[:, own], p1[:, nxt]), do_ref[:, cols]))
                put(dv_ref, hi, _dot_tn(rows_cat(p1[:, own], pa), rows_cat(do[1], do[2])))

    def spec(rows, width, row_of, cb):
        return pl.BlockSpec((None, rows, width), lambda r, i: (r, row_of(i), cb))

    same = lambda i: i
    before = lambda i: jnp.maximum(2 * i - 1, 0)
    after = lambda i: jnp.minimum(2 * i + 2, n_blocks - 1)
    out = spec(CHUNK, D_ATTN, same, 0)
    return _call(
        body, name=name, grid=(dil, n // CHUNK),
        in_specs=[spec(CHUNK, D_ATTN, same, 0), spec(CHUNK, D_ATTN, same, 1), spec(CHUNK, D_ATTN, same, 2),
                  spec(KEY_BLOCK, D_ATTN, before, 1), spec(KEY_BLOCK, D_ATTN, before, 2),
                  spec(KEY_BLOCK, D_ATTN, after, 0),
                  spec(CHUNK, D_ATTN, same, 0), spec(KEY_BLOCK, D_ATTN, after, 0),
                  spec(CHUNK, STAT_LANES, same, 0), spec(KEY_BLOCK, STAT_LANES, after, 0)],
        out_specs=[out, out, out],
        out_shape=[jax.ShapeDtypeStruct((dil, n, D_ATTN), BF16)] * 3,
        scratch_shapes=[], semantics=("parallel", "parallel"), vmem_mib=40,
        args=(qkv, qkv, qkv, qkv, qkv, qkv, do, do, stats, stats), comm=comm)


def _window_sums(ext, window, backward):
    rows = ext.shape[0]
    acc, span = ext, 1
    while span < window:
        acc = acc + pltpu.roll(acc, (rows - span) if backward else span, axis=0)
        span *= 2
    return acc


def _mix_gate(o_list, st_list, hug, w_pool_g, pool_scale):
    seq = hug.shape[0]
    tm = 256
    halo_blocks = tm // POOL_HALO
    d4, d16 = DILATIONS[1], DILATIONS[2]

    def body(o1_ref, o4_ref, o16_ref, l1_ref, l4_ref, l16_ref, u_ref, halo_ref, ga_ref, gp_ref, wp_ref, sc_ref,
             y_ref, mix_ref, lse_ref, pooled_ref, n4_ref, n16_ref, nl4_ref, nl16_ref):
        i = pl.program_id(0)
        _from_pattern(o4_ref, n4_ref, d4)
        _from_pattern(o16_ref, n16_ref, d16)
        _from_pattern(l4_ref, nl4_ref, d4)
        _from_pattern(l16_ref, nl16_ref, d16)
        la, lb, lc = l1_ref[...], nl4_ref[0], nl16_ref[0]
        mx = jnp.maximum(jnp.maximum(la, lb), lc)
        ea, eb, ec = jnp.exp(la - mx), jnp.exp(lb - mx), jnp.exp(lc - mx)
        tot = ea + eb + ec
        lse_ref[...] = mx + jnp.log(tot)
        wa, wb, wc = ea / tot, eb / tot, ec / tot
        ga = ga_ref[...].astype(F32)
        silu_a = ga * jax.nn.sigmoid(ga)
        for h in range(N_HEADS):
            cols = slice(h * HEAD_DIM, (h + 1) * HEAD_DIM)
            hc = slice(h, h + 1)
            attn = wa[:, hc] * o1_ref[:, cols].astype(F32) + wb[:, hc] * n4_ref[h] + wc[:, hc] * n16_ref[h]
            mix_ref[:, cols] = attn.astype(BF16)
            y_ref[:, cols] = (attn * silu_a[:, cols]).astype(BF16)

        u = u_ref[...].astype(F32)
        halo = jnp.where(i > 0, halo_ref[...].astype(F32), 0.0)
        ext = jnp.concatenate([halo, u], axis=0)
        pos = i * tm + lax.broadcasted_iota(jnp.int32, (tm, 1), 0)
        gp = gp_ref[...].astype(F32)
        gated_scale = sc_ref[...] * (gp * jax.nn.sigmoid(gp))
        for g, window in enumerate(POOL_WINDOWS):
            cols = slice(g * POOL_GROUP_DIM, (g + 1) * POOL_GROUP_DIM)
            sums = _window_sums(ext[:, cols], window, backward=False)[POOL_HALO:, :]
            count = jnp.minimum(pos + 1, window).astype(F32)
            pooled = (sums / count - u[:, cols]).astype(BF16)
            pooled_ref[:, cols] = pooled
            pre = _dot_nn(pooled, wp_ref[g])
            out_cols = slice(D_ATTN + g * POOL_GROUP_DIM, D_ATTN + (g + 1) * POOL_GROUP_DIM)
            mix_ref[:, out_cols] = pre.astype(BF16)
            y_ref[:, out_cols] = (pre * gated_scale[:, cols]).astype(BF16)

    row = lambda width, cb=0: pl.BlockSpec((tm, width), lambda i: (i, cb))
    pat = lambda d, width: pl.BlockSpec((d, tm // d, width), lambda i: (0, i, 0))
    return _pallas(
        body, name="mix_gate", grid=(seq // tm,),
        in_specs=[row(D_ATTN), pat(d4, D_ATTN), pat(d16, D_ATTN),
                  row(STAT_LANES), pat(d4, STAT_LANES), pat(d16, STAT_LANES),
                  row(D_POOL),
                  pl.BlockSpec((POOL_HALO, D_POOL), lambda i: (jnp.maximum(i * halo_blocks - 1, 0), 0)),
                  row(D_ATTN, 1), row(D_POOL, 2),
                  pl.BlockSpec((len(POOL_WINDOWS), POOL_GROUP_DIM, POOL_GROUP_DIM), lambda i: (0, 0, 0)),
                  pl.BlockSpec((1, D_POOL), lambda i: (0, 0))],
        out_specs=[row(D_MODEL), row(D_MODEL), row(STAT_LANES), row(D_POOL)],
        out_shape=[jax.ShapeDtypeStruct((seq, D_MODEL), BF16), jax.ShapeDtypeStruct((seq, D_MODEL), BF16),
                   jax.ShapeDtypeStruct((seq, STAT_LANES), F32), jax.ShapeDtypeStruct((seq, D_POOL), BF16)],
        scratch_shapes=[pltpu.VMEM((N_HEADS, tm, HEAD_DIM), F32), pltpu.VMEM((N_HEADS, tm, HEAD_DIM), F32),
                        pltpu.VMEM((1, tm, STAT_LANES), F32), pltpu.VMEM((1, tm, STAT_LANES), F32)],
        compiler_params=_params(("parallel",), 48),
    )(o_list[0][0], o_list[1], o_list[2], st_list[0][0], st_list[1], st_list[2],
      hug, hug, hug, hug, w_pool_g, pool_scale)


def _out_proj_loss(y, w_out_g, x, target, gain, bias):
    seq = x.shape[0]
    tm = 512

    def body(y_ref, w_ref, x_ref, t_ref, g_ref, b_ref, dz_ref, dzb_ref, gg_ref, gb_ref, loss_ref):
        @pl.when(pl.program_id(0) == 0)
        def _():
            gg_ref[...] = jnp.zeros_like(gg_ref)
            gb_ref[...] = jnp.zeros_like(gb_ref)
            loss_ref[...] = jnp.zeros_like(loss_ref)

        halves = [slice(0, tm // 2), slice(tm // 2, tm)]
        projected = [_dot_nn(y_ref[rows, :], w_ref[...]) for rows in halves]
        for rows, out in zip(halves, projected):
            z = DEEPNORM_ALPHA * x_ref[rows, :] + out
            mu = jnp.mean(z, axis=-1, keepdims=True)
            zc = z - mu
            rstd = lax.rsqrt(jnp.mean(zc * zc, axis=-1, keepdims=True) + LN_EPS)
            xhat = zc * rstd
            gain_v = g_ref[...]
            diff = xhat * gain_v + b_ref[...] - t_ref[rows, :]
            sq = _fold_rows(diff * diff)
            part = sq[:, :128]
            for k in range(1, D_MODEL // 128):
                part = part + sq[:, k * 128:(k + 1) * 128]
            loss_ref[...] += part
            dln = diff * (1.0 / D_MODEL)
            gg_ref[...] += _fold_rows(dln * xhat)
            gb_ref[...] += _fold_rows(dln)
            dxhat = dln * gain_v
            dz = rstd * (dxhat - jnp.mean(dxhat, axis=-1, keepdims=True)
                         - xhat * jnp.mean(dxhat * xhat, axis=-1, keepdims=True))
            dz_ref[rows, :] = dz
            dzb_ref[rows, :] = dz.astype(BF16)

    row = lambda: pl.BlockSpec((tm, D_MODEL), lambda i: (i, 0))
    vec = lambda: pl.BlockSpec((1, D_MODEL), lambda i: (0, 0))
    acc = lambda width: pl.BlockSpec((8, width), lambda i: (0, 0))
    return _pallas(
        body, name="out_proj_loss", grid=(seq // tm,),
        in_specs=[row(), pl.BlockSpec((D_MODEL, D_MODEL), lambda i: (0, 0), pipeline_mode=pl.Buffered(1)),
                  row(), row(), vec(), vec()],
        out_specs=[row(), row(), acc(D_MODEL), acc(D_MODEL), acc(128)],
        out_shape=[jax.ShapeDtypeStruct((seq, D_MODEL), F32), jax.ShapeDtypeStruct((seq, D_MODEL), BF16),
                   jax.ShapeDtypeStruct((8, D_MODEL), F32), jax.ShapeDtypeStruct((8, D_MODEL), F32),
                   jax.ShapeDtypeStruct((8, 128), F32)],
        compiler_params=_params(("arbitrary",), 56),
    )(y, w_out_g.reshape(D_MODEL, D_MODEL), x, target, gain, bias)


def _dy_gate_bwd(dzb, w_out_g, hug, mixpre, pool_scale, lse_all):
    seq = dzb.shape[0]
    tm = 256
    d4, d16 = DILATIONS[1], DILATIONS[2]

    def body(dz_ref, w_ref, ga_ref, gp_ref, mix_ref, sc_ref, lse_ref,
             dh_ref, dpo_ref, do1_ref, do4_ref, do16_ref, st1_ref, st4_ref, st16_ref, da_ref, st_ref):
        dy = _dot_nt(dz_ref[...], w_ref[...])
        ga = ga_ref[...].astype(F32)
        sig = jax.nn.sigmoid(ga)
        attn = mix_ref[:, :D_ATTN].astype(F32)
        dya = dy[:, :D_ATTN]
        dattn = dya * (ga * sig)
        dh_ref[:, :D_ATTN] = (dya * attn * (sig * (1.0 + ga * (1.0 - sig)))).astype(BF16)
        _store_slabs(da_ref, dattn)
        lane = lax.broadcasted_iota(jnp.int32, (tm, STAT_LANES), 1)
        stats = lse_ref[...]
        prod = dattn * attn
        for h in range(N_HEADS):
            delta = jnp.sum(prod[:, h * HEAD_DIM:(h + 1) * HEAD_DIM], axis=-1, keepdims=True)
            stats = jnp.where(lane == N_HEADS + h, delta, stats)
        st_ref[0] = stats
        do1_ref[...] = dattn.astype(BF16)
        st1_ref[...] = stats
        _to_pattern(da_ref, do4_ref, d4, BF16)
        _to_pattern(da_ref, do16_ref, d16, BF16)
        _to_pattern(st_ref, st4_ref, d4, F32)
        _to_pattern(st_ref, st16_ref, d16, F32)

        gp = gp_ref[...].astype(F32)
        sig = jax.nn.sigmoid(gp)
        dyp = dy[:, D_ATTN:]
        dpo_ref[...] = (dyp * (gp * sig)).astype(BF16)
        dh_ref[:, D_ATTN:] = (dyp * (mix_ref[:, D_ATTN:].astype(F32) * sc_ref[...])
                              * (sig * (1.0 + gp * (1.0 - sig)))).astype(BF16)

    row = lambda width, cb=0: pl.BlockSpec((tm, width), lambda i: (i, cb))
    pat = lambda d, width: pl.BlockSpec((d, tm // d, width), lambda i: (0, i, 0))
    pat_shape = lambda d, width, dtype: jax.ShapeDtypeStruct((d, seq // d, width), dtype)
    outs = _pallas(
        body, name="dy_gate_bwd", grid=(seq // tm,),
        in_specs=[row(D_MODEL), pl.BlockSpec((D_MODEL, D_MODEL), lambda i: (0, 0)),
                  row(D_ATTN, 1), row(D_POOL, 2), row(D_MODEL), pl.BlockSpec((1, D_POOL), lambda i: (0, 0)),
                  row(STAT_LANES)],
        out_specs=[row(D_MODEL, D_IN // D_MODEL - 1), row(D_POOL),
                   row(D_ATTN), pat(d4, D_ATTN), pat(d16, D_ATTN),
                   row(STAT_LANES), pat(d4, STAT_LANES), pat(d16, STAT_LANES)],
        out_shape=[jax.ShapeDtypeStruct((seq, D_IN), BF16), jax.ShapeDtypeStruct((seq, D_POOL), BF16),
                   jax.ShapeDtypeStruct((seq, D_ATTN), BF16), pat_shape(d4, D_ATTN, BF16), pat_shape(d16, D_ATTN, BF16),
                   jax.ShapeDtypeStruct((seq, STAT_LANES), F32), pat_shape(d4, STAT_LANES, F32),
                   pat_shape(d16, STAT_LANES, F32)],
        scratch_shapes=[pltpu.VMEM((N_HEADS, tm, HEAD_DIM), F32), pltpu.VMEM((1, tm, STAT_LANES), F32)],
        compiler_params=_params(("parallel",), 48),
    )(dzb, w_out_g.reshape(D_MODEL, D_MODEL), hug, hug, mixpre, pool_scale, lse_all)
    dh, dpo, do1, do4, do16, st1, st4, st16 = outs
    return dh, dpo, [do1[None], do4, do16], [st1[None], st4, st16]


def _pool_bwd(dh, dpo, mixpre, pooled, w_pool_g, pool_scale):
    seq = dpo.shape[0]
    tm = 256
    halo_blocks = tm // POOL_HALO
    last = seq // tm - 1
    n_groups = len(POOL_WINDOWS)

    def body(dh_in_ref, dpo_ref, halo_ref, pre_ref, pooled_ref, wp_ref, sc_ref, du_ref, gw_ref, gs_ref):
        i = pl.program_id(0)

        @pl.when(i == 0)
        def _():
            gw_ref[...] = jnp.zeros_like(gw_ref)
            gs_ref[...] = jnp.zeros_like(gs_ref)

        dpo = dpo_ref[...].astype(F32)
        scale = sc_ref[...]
        gs_ref[...] += _fold_rows(dpo * pre_ref[...].astype(F32))
        halo = jnp.where(i < last, halo_ref[...].astype(F32), 0.0)
        dpw = (jnp.concatenate([dpo, halo], axis=0) * scale).astype(BF16)
        pos = i * tm + lax.broadcasted_iota(jnp.int32, (tm + POOL_HALO, 1), 0)
        for g, window in enumerate(POOL_WINDOWS):
            cols = slice(g * POOL_GROUP_DIM, (g + 1) * POOL_GROUP_DIM)
            dpw_g = dpw[:, cols]
            gw_ref[g] += _dot_tn(pooled_ref[:, cols], dpw_g[:tm, :])
            dpooled = _dot_nt(dpw_g, wp_ref[g])
            count = jnp.minimum(pos + 1, window).astype(F32)
            sums = _window_sums(dpooled / count, window, backward=True)
            du_ref[:, cols] = (sums[:tm, :] - dpooled[:tm, :]).astype(BF16)

    row = lambda width, cb=0: pl.BlockSpec((tm, width), lambda i: (i, cb))
    return _pallas(
        body, name="pool_bwd", grid=(seq // tm,),
        in_specs=[ANY, row(D_POOL),
                  pl.BlockSpec((POOL_HALO, D_POOL),
                               lambda i: (jnp.minimum((i + 1) * halo_blocks, seq // POOL_HALO - 1), 0)),
                  row(D_POOL, 1), row(D_POOL),
                  pl.BlockSpec((n_groups, POOL_GROUP_DIM, POOL_GROUP_DIM), lambda i: (0, 0, 0)),
                  pl.BlockSpec((1, D_POOL), lambda i: (0, 0))],
        out_specs=[row(D_POOL, D_QKV // D_POOL),
                   pl.BlockSpec((n_groups, POOL_GROUP_DIM, POOL_GROUP_DIM), lambda i: (0, 0, 0)),
                   pl.BlockSpec((8, D_POOL), lambda i: (0, 0))],
        out_shape=[jax.ShapeDtypeStruct(dh.shape, dh.dtype),
                   jax.ShapeDtypeStruct((n_groups, POOL_GROUP_DIM, POOL_GROUP_DIM), F32),
                   jax.ShapeDtypeStruct((8, D_POOL), F32)],
        input_output_aliases={0: 0},
        compiler_params=_params(("arbitrary",), 40),
    )(dh, dpo, dpo, mixpre, pooled, w_pool_g, pool_scale)


def _sum_patterns(dh, parts, tabs, unrotate, col_block, name, comm=None):
    seq = dh.shape[0]
    tm, tn = 256, D_ATTN
    per = D_ATTN // tn
    d4, d16 = DILATIONS[1], DILATIONS[2]

    def body(dh_in_ref, a1_ref, a4_ref, a16_ref, ct_ref, up_ref, down_ref, o_ref, n4_ref, n16_ref):
        _from_pattern(a4_ref, n4_ref, d4)
        _from_pattern(a16_ref, n16_ref, d16)
        for s in range(tn // HEAD_DIM):
            cols = slice(s * HEAD_DIM, (s + 1) * HEAD_DIM)
            tot = a1_ref[:, cols].astype(F32) + n4_ref[s] + n16_ref[s]
            if unrotate:
                tot = _rotate_heads(tot, ct_ref[...], -up_ref[...], -down_ref[...])
            o_ref[:, cols] = tot.astype(BF16)

    tab = pl.BlockSpec((tm, HEAD_DIM), lambda i, j: (i, 0))
    pat = lambda d: pl.BlockSpec((d, tm // d, tn), lambda i, j: (0, i, j))
    (dh,), exchanged = _call(
        body, name=name, grid=(seq // tm, per),
        in_specs=[ANY, pl.BlockSpec((tm, tn), lambda i, j: (i, j)), pat(d4), pat(d16), tab, tab, tab],
        out_specs=[pl.BlockSpec((tm, tn), lambda i, j: (i, col_block * per + j))],
        out_shape=[jax.ShapeDtypeStruct(dh.shape, dh.dtype)],
        scratch_shapes=[pltpu.VMEM((tn // HEAD_DIM, tm, HEAD_DIM), F32), pltpu.VMEM((tn // HEAD_DIM, tm, HEAD_DIM), F32)],
        semantics=("parallel", "parallel"), vmem_mib=32, args=(dh, parts[0][0], parts[1], parts[2], *tabs),
        aliases={0: 0}, comm=comm)
    return dh, exchanged


def _grad_w_in(x, dh, half, name, comm=None):
    seq = x.shape[0]
    ts, td, te = 2048, D_MODEL // 2, SHARD_IN

    def body(half_ref, x_ref, dh_ref, o_ref):
        k = pl.program_id(1)
        part = _dot_tn(x_ref[...].astype(BF16), dh_ref[...])

        @pl.when(k == 0)
        def _():
            o_ref[...] = part

        @pl.when(k > 0)
        def _():
            o_ref[...] += part

    (g,), exchanged = _call(
        body, name=name, grid=(N_SHARDS, seq // ts),
        in_specs=[pl.BlockSpec((ts, td), lambda e, k, half_ref: (k, half_ref[0])),
                  pl.BlockSpec((ts, te), lambda e, k, half_ref: (k, e))],
        out_specs=[pl.BlockSpec((None, td, te), lambda e, k, half_ref: (e, 0, 0))],
        out_shape=[jax.ShapeDtypeStruct((N_SHARDS, td, te), F32)],
        scratch_shapes=[], semantics=("parallel", "arbitrary"), vmem_mib=56, args=(x, dh), comm=comm,
        prefetch=(half,))
    return g, exchanged


def _grad_w_out(y, dzb):
    seq = y.shape[0]
    ts, te = 512, 1024
    nk = seq // ts

    def body(y_ref, dz_ref, o_ref, acc_ref):
        k = pl.program_id(1)

        @pl.when(k == 0)
        def _():
            acc_ref[...] = jnp.zeros_like(acc_ref)

        acc_ref[...] += _dot_tn(y_ref[...], dz_ref[...])

        @pl.when(k == nk - 1)
        def _():
            o_ref[...] = acc_ref[...]

    return _pallas(
        body, name="grad_w_out", grid=(D_MODEL // te, nk),
        in_specs=[pl.BlockSpec((ts, te), lambda e, k: (k, e)), pl.BlockSpec((ts, D_MODEL), lambda e, k: (k, 0))],
        out_specs=pl.BlockSpec((te, D_MODEL), lambda e, k: (e, 0)),
        out_shape=jax.ShapeDtypeStruct((D_MODEL, D_MODEL), F32),
        scratch_shapes=[pltpu.VMEM((te, D_MODEL), F32)],
        compiler_params=_params(("parallel", "arbitrary"), 48),
    )(y, dzb)


GRAD_X_LATE_SHARDS = 1
GRAD_X_PARTIAL_ROWS = 1024


def _grad_x_partial(dh, w_in_g, first, tiles, prev=None, comm=None):
    seq = dh.shape[0]
    tm, tk = GRAD_X_PARTIAL_ROWS, SHARD_IN

    def body(*refs):
        dh_ref, w_ref, o_ref = refs[-3:]
        k = pl.program_id(1)
        part = _dot_nt(dh_ref[...], w_ref[...])

        @pl.when(k == 0)
        def _():
            o_ref[...] = part

        @pl.when(k > 0)
        def _():
            o_ref[...] += part

    carried = [] if prev is None else [prev]
    (partial,), exchanged = _call(
        body, name="grad_x_partial_%d" % first, grid=(tiles, N_SHARDS - GRAD_X_LATE_SHARDS),
        in_specs=[ANY] * len(carried) + [
            pl.BlockSpec((tm, tk), lambda i, k: (i + first, k)),
            pl.BlockSpec((None, D_MODEL, tk), lambda i, k: (k, 0, 0))],
        out_specs=[pl.BlockSpec((tm, D_MODEL), lambda i, k: (i + first, 0))],
        out_shape=[jax.ShapeDtypeStruct((seq, D_MODEL), F32)],
        scratch_shapes=[], semantics=("parallel", "arbitrary"), vmem_mib=48, args=(*carried, dh, w_in_g),
        aliases={0: 0} if carried else None, comm=comm)
    return partial, exchanged


def _grad_x_final(dh, w_in_g, dz, partial):
    seq = dh.shape[0]
    tm, tk = 512, SHARD_IN
    k0 = N_SHARDS - GRAD_X_LATE_SHARDS

    def body(dh_ref, w_ref, dz_ref, p_ref, o_ref):
        k = pl.program_id(1)
        part = _dot_nt(dh_ref[...], w_ref[...])

        @pl.when(k == 0)
        def _():
            o_ref[...] = (DEEPNORM_ALPHA * dz_ref[...] + p_ref[...]) + part

        @pl.when(k > 0)
        def _():
            o_ref[...] += part

    row = pl.BlockSpec((tm, D_MODEL), lambda i, k: (i, 0))
    return _pallas(
        body, name="grad_x_final", grid=(seq // tm, GRAD_X_LATE_SHARDS),
        in_specs=[pl.BlockSpec((tm, tk), lambda i, k: (i, k + k0)),
                  pl.BlockSpec((None, D_MODEL, tk), lambda i, k: (k + k0, 0, 0)), row, row],
        out_specs=row, out_shape=jax.ShapeDtypeStruct((seq, D_MODEL), F32),
        compiler_params=_params(("parallel", "arbitrary"), 48),
    )(dh, w_in_g, dz, partial)


def _pool_weight(w_pool_sh):
    n_groups = len(POOL_WINDOWS)
    shard_c = POOL_GROUP_DIM // N_SHARDS
    return (w_pool_sh.reshape(N_SHARDS, n_groups, shard_c, POOL_GROUP_DIM).transpose(1, 0, 2, 3)
            .reshape(n_groups, POOL_GROUP_DIM, POOL_GROUP_DIM))


def _pool_grad_pieces(g_w_pool):
    n_groups = len(POOL_WINDOWS)
    half_c = POOL_GROUP_DIM // N_SHARDS // 2
    return (g_w_pool.reshape(n_groups, N_SHARDS, 2, half_c, POOL_GROUP_DIM).transpose(1, 2, 0, 3, 4)
            .reshape(N_SHARDS, 2, n_groups * half_c, POOL_GROUP_DIM))


def _step(x, target, w_bufs, pool_scale, gain, bias, place):
    seq = x.shape[0]
    tabs = _rope_tables(seq)
    core, chip_core, onward, plan = place
    qkv, hug, w_in_g, w_out_g, w_pool_sh = _in_proj_gathering(x, w_bufs, tabs, plan)
    o_list, st_list = [], []
    for p, dil in enumerate(DILATIONS):
        o, st = _attn_fwd(qkv[p], "attn_fwd_d%d" % dil)
        o_list.append(o)
        st_list.append(st)
    w_pool_g = _pool_weight(w_pool_sh)
    y, mixpre, lse_all, pooled = _mix_gate(o_list, st_list, hug, w_pool_g, pool_scale)
    dz, dzb, gain_part, bias_part, loss_part = _out_proj_loss(y, w_out_g, x, target, gain, bias)
    dh, dpo, do_list, stat_list = _dy_gate_bwd(dzb, w_out_g, hug, mixpre, pool_scale, lse_all)
    g_w_out = _grad_w_out(y, dzb)
    dh, g_w_pool, scale_part = _pool_bwd(dh, dpo, mixpre, pooled, w_pool_g, pool_scale)
    small = jnp.concatenate([scale_part, gain_part, bias_part, loss_part], axis=1)
    early = [g_w_out.reshape(N_SHARDS, 2, D_MODEL // (2 * N_SHARDS), D_MODEL), _pool_grad_pieces(g_w_pool)]

    bwd = lambda p, comm: _attn_bwd(qkv[p], do_list[p], stat_list[p], "attn_bwd_d%d" % DILATIONS[p], comm)
    part_a, recv = bwd(0, _exchange_halves(early))
    sums = [_add_own_half(g, r, core, "add_own_half_%d" % a) for a, (g, r) in enumerate(zip(early, recv))]
    part_b, recv = bwd(1, _scatter_to_chips([s[1] for s in sums]))
    bufs = [_add_chips(s[0], r, chip_core, "add_chips_%d" % a) for a, (s, r) in enumerate(zip(sums, recv))]
    part_c, reduced = bwd(2, _share_with_sibling(bufs))
    parts = [part_a, part_b, part_c]
    dh, gathered = _sum_patterns(dh, [t[0] for t in parts], tabs, True, 0, "sum_dq", _gather_small(small))
    dh, _ = _sum_patterns(dh, [t[1] for t in parts], tabs, True, 1, "sum_dk")
    dh, _ = _sum_patterns(dh, [t[2] for t in parts], tabs, False, 2, "sum_dv")

    give, _ = _grad_w_in(x, dh, 1 - core, "grad_w_in_give")
    keep, recv = _grad_w_in(x, dh, core, "grad_w_in_keep", _send_to_sibling([give]))
    total, total_b = _add_pair(keep, recv[0], "add_own_half_w_in")
    n_tiles = seq // GRAD_X_PARTIAL_ROWS
    tiles = max(n_tiles // 4, 1)
    part, relayed = _grad_x_partial(dh, w_in_g, 0, tiles, None, _relay_diagonal(total_b))
    total_b = _fold_relayed(total, total_b, relayed[0], onward)
    part, recv = _grad_x_partial(dh, w_in_g, tiles, n_tiles - tiles, part, _scatter_to_neighbours(total_b))
    buf = _add_chips(total, recv[0], chip_core, "add_chips_w_in")
    g_x = _grad_x_final(dh, w_in_g, dz, part)
    g_w_in = _run_exchange(_share_with_sibling([buf]), "share_w_in")[0]
    return g_x, g_w_in, reduced[0], reduced[1], small, gathered[0]


def _exchange_halves(grads):
    n = len(grads)

    def copies(src, dst, sems):
        x, y, c, _ = _mesh_place()
        return [_remote(src[a].at[j, 1 - c], dst[a].at[j], sems[0].at[a, j], sems[1].at[a, j], (x, y, 1 - c))
                for a in range(n) for j in range(N_SHARDS)]

    def start(src, dst, sems):
        for cp in copies(src, dst, sems):
            cp.start()

    def finish(src, dst, sems):
        for cp in copies(src, dst, sems):
            cp.wait()

    return _Exchange(grads, [jax.ShapeDtypeStruct((N_SHARDS,) + g.shape[2:], g.dtype) for g in grads], {},
                     [pltpu.SemaphoreType.DMA((n, N_SHARDS))] * 2, start, finish)


def _add_own_half(grad, recv, core, name):
    _, _, r, c = grad.shape
    tr = min(r, 256)

    def body(core_ref, g_ref, r_ref, o_ref, ob_ref):
        tot = g_ref[...] + r_ref[...]
        o_ref[...] = tot
        ob_ref[...] = tot.astype(BF16)

    out = pl.BlockSpec((None, tr, c), lambda j, i, core_ref: (j, i, 0))
    return _pallas(
        body, name=name,
        grid_spec=pltpu.PrefetchScalarGridSpec(
            num_scalar_prefetch=1, grid=(N_SHARDS, r // tr),
            in_specs=[pl.BlockSpec((None, None, tr, c), lambda j, i, core_ref: (j, core_ref[0], i, 0)),
                      pl.BlockSpec((None, tr, c), lambda j, i, core_ref: (j, i, 0))],
            out_specs=[out, out]),
        out_shape=[jax.ShapeDtypeStruct((N_SHARDS, r, c), F32), jax.ShapeDtypeStruct((N_SHARDS, r, c), BF16)],
        compiler_params=_params(("parallel", "parallel"), 32),
    )(core, grad, recv)


def _send_to_sibling(arrays):
    n = len(arrays)

    def copies(src, dst, sems):
        x, y, c, _ = _mesh_place()
        return [_remote(src[a], dst[a], sems[0].at[a], sems[1].at[a], (x, y, 1 - c)) for a in range(n)]

    def start(src, dst, sems):
        for cp in copies(src, dst, sems):
            cp.start()

    def finish(src, dst, sems):
        for cp in copies(src, dst, sems):
            cp.wait()

    return _Exchange(arrays, [jax.ShapeDtypeStruct(t.shape, t.dtype) for t in arrays], {},
                     [pltpu.SemaphoreType.DMA((n,))] * 2, start, finish)


def _add_pair(a, b, name):
    _, r, c = a.shape
    tr = min(r, 256)

    def body(a_ref, b_ref, o_ref, ob_ref):
        tot = a_ref[...] + b_ref[...]
        o_ref[...] = tot
        ob_ref[...] = tot.astype(BF16)

    spec = pl.BlockSpec((None, tr, c), lambda j, i: (j, i, 0))
    return _pallas(
        body, name=name, grid=(N_SHARDS, r // tr), in_specs=[spec, spec], out_specs=[spec, spec],
        out_shape=[jax.ShapeDtypeStruct(a.shape, F32), jax.ShapeDtypeStruct(a.shape, BF16)],
        compiler_params=_params(("parallel", "parallel"), 32),
    )(a, b)


def _scatter_to_chips(sums):
    n = len(sums)

    def copies(src, dst, sems):
        x, y, c, chips = _mesh_place()
        return [_remote(src[a].at[2 * cx + cy], dst[a].at[k], sems[0].at[a, k], sems[1].at[a, k], (cx, cy, c))
                for a in range(n) for k, (cx, cy) in enumerate(chips)]

    def start(src, dst, sems):
        for cp in copies(src, dst, sems):
            cp.start()

    def finish(src, dst, sems):
        for cp in copies(src, dst, sems):
            cp.wait()

    return _Exchange(sums, [jax.ShapeDtypeStruct((3,) + s.shape[1:], s.dtype) for s in sums], {},
                     [pltpu.SemaphoreType.DMA((n, 3))] * 2, start, finish)


def _add_chips(sums, recv, chip_core, name):
    _, r, c = sums.shape
    n_recv = recv.shape[0]
    tr = min(r, 256)

    def body(cc_ref, s_ref, r_ref, o_ref):
        tot = s_ref[...]
        for k in range(n_recv):
            tot = tot + r_ref[k].astype(F32)
        o_ref[...] = tot

    return _pallas(
        body, name=name,
        grid_spec=pltpu.PrefetchScalarGridSpec(
            num_scalar_prefetch=1, grid=(r // tr,),
            in_specs=[pl.BlockSpec((None, tr, c), lambda i, cc_ref: (cc_ref[0], i, 0)),
                      pl.BlockSpec((n_recv, tr, c), lambda i, cc_ref: (0, i, 0))],
            out_specs=pl.BlockSpec((None, tr, c), lambda i, cc_ref: (cc_ref[1], i, 0))),
        out_shape=jax.ShapeDtypeStruct((2, r, c), F32),
        compiler_params=_params(("parallel",), 32),
    )(chip_core, sums, recv)


def _relay_diagonal(sums_b):
    def copy(src, dst, sems):
        x, y, c, _ = _mesh_place()
        diagonal = 2 * (1 - x) + (1 - y)
        return _remote(src[0].at[diagonal], dst[0], sems[0].at[0], sems[1].at[0], (x ^ (1 - c), y ^ c, c))

    def start(src, dst, sems):
        copy(src, dst, sems).start()

    def finish(src, dst, sems):
        copy(src, dst, sems).wait()

    return _Exchange([sums_b], [jax.ShapeDtypeStruct(sums_b.shape[1:], sums_b.dtype)], {},
                     [pltpu.SemaphoreType.DMA((1,))] * 2, start, finish)


def _fold_relayed(sums, sums_b, relayed, onward):
    _, r, c = sums.shape
    tr = min(r, 256)

    def body(on_ref, b_in_ref, s_ref, r_ref, o_ref):
        o_ref[...] = (s_ref[...] + r_ref[...].astype(F32)).astype(BF16)

    return _pallas(
        body, name="fold_relayed",
        grid_spec=pltpu.PrefetchScalarGridSpec(
            num_scalar_prefetch=1, grid=(r // tr,),
            in_specs=[ANY, pl.BlockSpec((None, tr, c), lambda i, on_ref: (on_ref[0], i, 0)),
                      pl.BlockSpec((tr, c), lambda i, on_ref: (i, 0))],
            out_specs=pl.BlockSpec((None, tr, c), lambda i, on_ref: (on_ref[0], i, 0))),
        out_shape=jax.ShapeDtypeStruct(sums_b.shape, sums_b.dtype),
        input_output_aliases={1: 0},
        compiler_params=_params(("parallel",), 32),
    )(onward, sums_b, sums, relayed)


def _scatter_to_neighbours(sums_b):
    def copies(src, dst, sems):
        x, y, c, chips = _mesh_place()
        return [_remote(src[0].at[2 * cx + cy], dst[0].at[k], sems[0].at[k], sems[1].at[k], (cx, cy, c))
                for k, (cx, cy) in enumerate(chips[:2])]

    def start(src, dst, sems):
        for cp in copies(src, dst, sems):
            cp.start()

    def finish(src, dst, sems):
        for cp in copies(src, dst, sems):
            cp.wait()

    return _Exchange([sums_b], [jax.ShapeDtypeStruct((2,) + sums_b.shape[1:], sums_b.dtype)], {},
                     [pltpu.SemaphoreType.DMA((2,))] * 2, start, finish)


def _share_with_sibling(bufs):
    n = len(bufs)

    def copies(dst, sems, half):
        x, y, c, _ = _mesh_place()
        h = c if half == "mine" else 1 - c
        return [_remote(dst[a].at[h], dst[a].at[h], sems[0].at[a], sems[1].at[a], (x, y, 1 - c)) for a in range(n)]

    def start(ins, dst, sems):
        for cp in copies(dst, sems, "mine"):
            cp.start()

    def finish(ins, dst, sems):
        for cp in copies(dst, sems, "theirs"):
            cp.wait_recv()
        for cp in copies(dst, sems, "mine"):
            cp.wait_send()

    return _Exchange(bufs, [jax.ShapeDtypeStruct(b.shape, b.dtype) for b in bufs], {a: a for a in range(n)},
                     [pltpu.SemaphoreType.DMA((n,))] * 2, start, finish)


def _adam_math(w, g, m, v):
    m = ADAM_B1 * m + (1.0 - ADAM_B1) * g
    v = ADAM_B2 * v + (1.0 - ADAM_B2) * (g * g)
    m_hat = m / (1.0 - ADAM_B1 ** ADAM_STEP)
    v_hat = v / (1.0 - ADAM_B2 ** ADAM_STEP)
    delta = -ADAM_LR * (m_hat / (jnp.sqrt(v_hat) + ADAM_EPS) + ADAM_WD * w)
    return delta, m, v


def _gather_small(small):
    def peers():
        x, y, c, _ = _mesh_place()
        return [(x ^ ((r >> 2) & 1), y ^ ((r >> 1) & 1), c ^ (r & 1)) for r in range(1, 8)], 4 * x + 2 * y + c

    def start(src, dst, sems):
        to, me = peers()
        for r, peer in enumerate(to):
            _remote(src[0], dst[0].at[me], sems[0].at[r], sems[1].at[r], peer).start()

    def finish(src, dst, sems):
        to, me = peers()
        for r, (px, py, pc) in enumerate(to):
            theirs = dst[0].at[4 * px + 2 * py + pc]
            _remote(theirs, theirs, sems[0].at[r], sems[1].at[r], (px, py, pc)).wait_recv()
        for r, peer in enumerate(to):
            _remote(src[0], dst[0].at[me], sems[0].at[r], sems[1].at[r], peer).wait_send()

    return _Exchange([small], [jax.ShapeDtypeStruct((8,) + small.shape, small.dtype)], {},
                     [pltpu.SemaphoreType.DMA((7,))] * 2, start, finish)


def _small_adamw(gathered, small, me, w_vec, m_vec, v_vec):
    n_par = w_vec.shape[1]

    def body(me_ref, a_ref, s_ref, w_ref, m_ref, v_ref, loss_ref, g_ref, d_ref, nm_ref, nv_ref):
        mine = s_ref[...]
        tot = jnp.where(me_ref[0] == 0, mine, a_ref[0])
        for d in range(1, 8):
            tot = tot + jnp.where(me_ref[0] == d, mine, a_ref[d])
        tot = jnp.sum(tot, axis=0, keepdims=True)
        sq = jnp.sum(tot[:, n_par:], axis=1, keepdims=True)
        loss_ref[...] = jnp.broadcast_to(sq * (0.5 / D_MODEL), loss_ref.shape)
        g = tot[:, :n_par]
        g_ref[...] = g
        d_ref[...], nm_ref[...], nv_ref[...] = _adam_math(w_ref[...], g, m_ref[...], v_ref[...])

    vm = pl.BlockSpec(memory_space=pltpu.VMEM)
    vec = jax.ShapeDtypeStruct((1, n_par), F32)
    return pl.pallas_call(
        body, name="small_adamw",
        grid_spec=pltpu.PrefetchScalarGridSpec(num_scalar_prefetch=1, grid=(), in_specs=[vm] * 5, out_specs=[vm] * 5),
        out_shape=[jax.ShapeDtypeStruct((1, 128), F32), vec, vec, vec, vec],
    )(me, gathered, small, w_vec, m_vec, v_vec)


def _adamw(w, g, m, v, name):
    r, c = w.shape
    tr = min(r, 256)

    def body(w_ref, g_ref, m_ref, v_ref, d_ref, nm_ref, nv_ref):
        d_ref[...], nm_ref[...], nv_ref[...] = _adam_math(w_ref[...], g_ref[...], m_ref[...], v_ref[...])

    spec = pl.BlockSpec((tr, c), lambda i: (i, 0))
    shape = jax.ShapeDtypeStruct((r, c), F32)
    return _pallas(
        body, name=name, grid=(r // tr,),
        in_specs=[spec] * 4, out_specs=[spec] * 3, out_shape=[shape] * 3,
        compiler_params=_params(("parallel",), 48),
    )(w, g, m, v)


def kernel(x, w_in, w_pool, pool_scale, w_out, ln_gain, ln_bias, loss_target, m_w_in, m_w_pool, m_pool_scale, m_w_out, m_ln_gain, m_ln_bias, v_w_in, v_w_pool, v_pool_scale, v_w_out, v_ln_gain, v_ln_bias):
    xi, yi, ci = lax.axis_index("x"), lax.axis_index("y"), lax.axis_index("c")
    chip = (2 * xi + yi).astype(jnp.int32).reshape(1)
    core = ci.astype(jnp.int32).reshape(1)
    n_groups = len(POOL_WINDOWS)
    shard_c = w_pool.shape[2]

    w_in_b = _cast_bf16(w_in[0], chip, "cast_w_in", 256)
    w_out_b = _cast_bf16(w_out[0], chip, "cast_w_out", 256)
    w_pool_b = _cast_bf16(w_pool[0].reshape(n_groups * shard_c, POOL_GROUP_DIM), chip, "cast_w_pool", 256)

    chip_core = jnp.concatenate([chip, core])
    onward = (2 * (xi ^ ci) + (yi ^ (1 - ci))).astype(jnp.int32).reshape(1)
    g_x, full_in, full_out, full_pool, small, small_all = _step(
        x[0], loss_target[0], [w_in_b, w_out_b, w_pool_b], pool_scale, ln_gain, ln_bias,
        (core, chip_core, onward, _in_proj_plan(xi, yi)))
    half_c = shard_c // 2
    grad_w_in = full_in.reshape(D_MODEL, SHARD_IN)
    grad_w_out = full_out.reshape(D_MODEL // N_SHARDS, D_MODEL)
    grad_w_pool = (full_pool.reshape(2, n_groups, half_c, POOL_GROUP_DIM).transpose(1, 0, 2, 3)
                   .reshape(n_groups * shard_c, POOL_GROUP_DIM))

    d_in, nm_in, nv_in = _adamw(w_in[0], grad_w_in, m_w_in[0], v_w_in[0], "adamw_w_in")
    d_out, nm_out, nv_out = _adamw(w_out[0], grad_w_out, m_w_out[0], v_w_out[0], "adamw_w_out")
    flat = lambda t: t[0].reshape(n_groups * shard_c, POOL_GROUP_DIM)
    d_pool, nm_pool, nv_pool = _adamw(flat(w_pool), grad_w_pool, flat(m_w_pool), flat(v_w_pool), "adamw_w_pool")

    cat = lambda a, b, c: jnp.concatenate([a, b, c], axis=1)
    me = (4 * xi + 2 * yi + ci).astype(jnp.int32).reshape(1)
    loss_v, g_vec, d_vec, nm_vec, nv_vec = _small_adamw(
        small_all, small, me, cat(pool_scale, ln_gain, ln_bias), cat(m_pool_scale, m_ln_gain, m_ln_bias),
        cat(v_pool_scale, v_ln_gain, v_ln_bias))

    def split(vec):
        return vec[:, :D_POOL], vec[:, D_POOL:D_POOL + D_MODEL], vec[:, D_POOL + D_MODEL:]

    g_scale, g_gain, g_bias = split(g_vec)
    d_scale, d_gain, d_bias = split(d_vec)
    nm_scale, nm_gain, nm_bias = split(nm_vec)
    nv_scale, nv_gain, nv_bias = split(nv_vec)
    pool_shape = w_pool.shape
    return (loss_v[0, 0], g_x[None],
            grad_w_in[None], grad_w_pool.reshape(pool_shape), g_scale, grad_w_out[None], g_gain, g_bias,
            d_in[None], d_pool.reshape(pool_shape), d_scale, d_out[None], d_gain, d_bias,
            nm_in[None], nm_pool.reshape(pool_shape), nm_scale, nm_out[None], nm_gain, nm_bias,
            nv_in[None], nv_pool.reshape(pool_shape), nv_scale, nv_out[None], nv_gain, nv_bias)
```

```python
import functools

import jax
import jax.numpy as jnp
from jax import lax
from jax.experimental import pallas as pl
from jax.experimental.pallas import tpu as pltpu

F32 = jnp.float32
BF16 = jnp.bfloat16
MESH = pl.DeviceIdType.MESH
ANY = pl.BlockSpec(memory_space=pl.ANY)

D_MODEL = 2048
D_ATTN = 1024
D_POOL = 1024
HEAD_DIM = 128
N_HEADS = 8
ROPE_DIM = 32
ROPE_THETA = 500000.0
DILATIONS = (1, 4, 16)
KEY_BLOCK = 128
CHUNK = 2 * KEY_BLOCK
STAT_LANES = 128
POOL_WINDOWS = (2, 4, 8, 16)
POOL_GROUP_DIM = 256
POOL_HALO = 16
D_QKV = 3 * D_ATTN
D_UG = D_POOL + D_MODEL
D_IN = D_QKV + D_UG
N_SHARDS = 4
SHARD_IN = D_IN // N_SHARDS
LN_EPS = 1e-5
DEEPNORM_ALPHA = 2.0 ** 0.25
ADAM_LR = 0.001
ADAM_B1 = 0.9
ADAM_B2 = 0.999
ADAM_EPS = 1e-08
ADAM_WD = 0.01
ADAM_STEP = 10
NEG = -1e30
MIB = 1024 * 1024


def _params(sem, vmem_mib):
    return pltpu.CompilerParams(dimension_semantics=sem, vmem_limit_bytes=vmem_mib * MIB)


def _pallas(body, **kwargs):
    pin = lambda s: pltpu.HBM(s.shape, s.dtype) if len(s.shape) >= 2 else s
    out_shape = kwargs.pop("out_shape")
    out_shape = [pin(s) for s in out_shape] if isinstance(out_shape, (list, tuple)) else pin(out_shape)
    call = pl.pallas_call(body, out_shape=out_shape, **kwargs)

    def run(*operands):
        return call(*[pltpu.with_memory_space_constraint(o, pltpu.HBM) if o.ndim >= 2 else o for o in operands])

    return run


class _Exchange:
    def __init__(self, operands, out_shape, aliases, sems, start, finish):
        self.operands, self.out_shape, self.aliases, self.sems = list(operands), list(out_shape), dict(aliases), list(sems)
        self.start, self.finish = start, finish


def _run_exchange(comm, name):
    n_in, n_out = len(comm.operands), len(comm.out_shape)

    def body(*refs):
        ins, outs, sems = refs[:n_in], refs[n_in:n_in + n_out], refs[n_in + n_out:]
        comm.start(ins, outs, sems)
        comm.finish(ins, outs, sems)

    return _pallas(
        body, name=name, in_specs=[ANY] * n_in, out_specs=[ANY] * n_out, out_shape=comm.out_shape,
        input_output_aliases=comm.aliases, scratch_shapes=comm.sems,
    )(*comm.operands)


def _call(body, *, name, grid, in_specs, out_specs, out_shape, scratch_shapes, semantics, vmem_mib, args,
          aliases=None, comm=None, prefetch=()):
    aliases = dict(aliases or {})
    n_pre, n_in, n_out, n_scr = len(prefetch), len(in_specs), len(out_specs), len(scratch_shapes)
    c_in, c_out = (len(comm.operands), len(comm.out_shape)) if comm else (0, 0)
    c_shapes, c_sems, c_operands = (comm.out_shape, comm.sems, comm.operands) if comm else ([], [], [])

    def hosted(*refs):
        pre, refs = refs[:n_pre], refs[n_pre:]
        a = n_in
        b = a + c_in
        c = b + n_out
        d = c + c_out
        e = d + n_scr
        if comm is None:
            body(*pre, *refs)
            return
        ids = [pl.program_id(k) for k in range(len(grid))]
        first = functools.reduce(jnp.logical_and, [i == 0 for i in ids])
        last = functools.reduce(jnp.logical_and, [i == g - 1 for i, g in zip(ids, grid)])

        @pl.when(first)
        def _():
            comm.start(refs[a:b], refs[c:d], refs[e:])

        body(*pre, *refs[:a], *refs[b:c], *refs[d:e])

        @pl.when(last)
        def _():
            comm.finish(refs[a:b], refs[c:d], refs[e:])

    if comm:
        semantics = ("arbitrary",) * len(grid)
        for i, o in comm.aliases.items():
            aliases[n_pre + n_in + i] = n_out + o
    outs = _pallas(
        hosted, name=name,
        grid_spec=pltpu.PrefetchScalarGridSpec(
            num_scalar_prefetch=n_pre, grid=grid, in_specs=list(in_specs) + [ANY] * c_in,
            out_specs=list(out_specs) + [ANY] * c_out, scratch_shapes=list(scratch_shapes) + c_sems),
        out_shape=list(out_shape) + c_shapes, input_output_aliases=aliases,
        compiler_params=_params(semantics, vmem_mib),
    )(*prefetch, *args, *c_operands)
    return list(outs[:n_out]), list(outs[n_out:])


def _dot_nn(a, b):
    return jnp.dot(a, b, preferred_element_type=F32)


def _dot_nt(a, b):
    return lax.dot_general(a, b, (((1,), (1,)), ((), ())), preferred_element_type=F32)


def _dot_tn(a, b):
    return lax.dot_general(a, b, (((0,), (0,)), ((), ())), preferred_element_type=F32)


def _fold_rows(a):
    r, c = a.shape
    return jnp.sum(a.reshape(r // 8, 8, c), axis=0)


def _cast_bf16(a, chip, name, rows):
    r, c = a.shape

    def body(chip_ref, a_ref, o_ref):
        o_ref[...] = a_ref[...].astype(BF16)

    return _pallas(
        body, name=name,
        grid_spec=pltpu.PrefetchScalarGridSpec(
            num_scalar_prefetch=1, grid=(r // rows,),
            in_specs=[pl.BlockSpec((rows, c), lambda i, chip_ref: (i, 0))],
            out_specs=pl.BlockSpec((None, rows, c), lambda i, chip_ref: (chip_ref[0], i, 0))),
        out_shape=jax.ShapeDtypeStruct((N_SHARDS, r, c), BF16),
        compiler_params=_params(("parallel",), 32),
    )(chip, a)


def _mesh_place():
    x, y, c = lax.axis_index("x"), lax.axis_index("y"), lax.axis_index("c")
    return x, y, c, [(1 - x, y), (x, 1 - y), (1 - x, 1 - y)]


def _remote(src, dst, send_sem, recv_sem, to):
    return pltpu.make_async_remote_copy(src_ref=src, dst_ref=dst, send_sem=send_sem, recv_sem=recv_sem,
                                        device_id=to, device_id_type=MESH)


def _rope_tables(seq):
    half = ROPE_DIM // 2
    inv_freq = ROPE_THETA ** (-(2.0 * jnp.arange(half, dtype=F32)) / ROPE_DIM)
    ang = jnp.arange(seq, dtype=jnp.int32).astype(F32)[:, None] * inv_freq[None, :]
    cos, sin = jnp.cos(ang), jnp.sin(ang)
    pad = jnp.zeros((seq, HEAD_DIM - ROPE_DIM), F32)
    zeros = jnp.zeros((seq, half), F32)
    c_tab = jnp.concatenate([cos, cos, pad + 1.0], axis=1)
    up_tab = jnp.concatenate([-sin, zeros, pad], axis=1)
    down_tab = jnp.concatenate([zeros, sin, pad], axis=1)
    return c_tab, up_tab, down_tab


def _rotate_heads(t, c_tab, up_tab, down_tab):
    outs = []
    for h in range(t.shape[1] // HEAD_DIM):
        th = t[:, h * HEAD_DIM:(h + 1) * HEAD_DIM]
        up = pltpu.roll(th, HEAD_DIM - ROPE_DIM // 2, axis=1)
        down = pltpu.roll(th, ROPE_DIM // 2, axis=1)
        outs.append(th * c_tab + up * up_tab + down * down_tab)
    return outs[0] if len(outs) == 1 else jnp.concatenate(outs, axis=1)


def _to_pattern(slabs_ref, dst_ref, dil, dtype):
    n_slabs, rows, _ = slabs_ref.shape
    for s in range(n_slabs):
        for r in range(dil):
            dst_ref[r, :, s * 128:(s + 1) * 128] = slabs_ref[s, pl.ds(r, rows // dil, dil), :].astype(dtype)


def _from_pattern(src_ref, slabs_ref, dil):
    n_slabs, rows, _ = slabs_ref.shape
    for s in range(n_slabs):
        for r in range(dil):
            slabs_ref[s, pl.ds(r, rows // dil, dil), :] = src_ref[r, :, s * 128:(s + 1) * 128].astype(F32)


def _store_slabs(slabs_ref, value):
    for s in range(slabs_ref.shape[0]):
        slabs_ref[s] = value[:, s * 128:(s + 1) * 128]


W_IN_CHUNKS = 4


def _in_proj_plan(x, y):
    shards = [2 * x + y, 2 * (1 - x) + y, 2 * x + (1 - y), 2 * (1 - x) + (1 - y)]
    last_row = jnp.int32(-2)

    def table(active, col_of):
        cols, rows = [], []
        first_col = functools.reduce(lambda acc, j: jnp.where(active[j], col_of(shards[j]), acc), reversed(range(4)),
                                     jnp.int32(0))
        held_col, seen = first_col, jnp.bool_(False)
        for j in range(4):
            cols.append(jnp.where(active[j], col_of(shards[j]), held_col))
            rows.append(jnp.where(active[j], -1, jnp.where(seen, last_row, 0)))
            held_col = jnp.where(active[j], col_of(shards[j]), held_col)
            seen = jnp.logical_or(seen, active[j])
        return cols, rows

    q_cols, q_rows = table([s < 2 for s in shards], lambda s: s)
    h_cols, h_rows = table([s >= 2 for s in shards], lambda s: s - 2)
    return jnp.stack([jnp.asarray(v, jnp.int32) for v in shards + q_cols + q_rows + h_cols + h_rows])


def _in_proj_gathering(x, w_bufs, tabs, plan):
    seq = x.shape[0]
    tm, tn = 512, SHARD_IN
    n_tiles = seq // tm
    heads = tn // HEAD_DIM
    k_heads_in_second = 2 * D_ATTN // HEAD_DIM - heads
    d4, d16 = DILATIONS[1], DILATIONS[2]
    DIAGONAL = 2
    chunk = D_MODEL // 2 // W_IN_CHUNKS
    early = [(0, D_MODEL // 2, q * chunk, chunk) for q in range(W_IN_CHUNKS)]
    late = [(a, w_bufs[a].shape[1] // 2, 0, w_bufs[a].shape[1] // 2) for a in (1, 2)]
    pieces = early + late
    early_ids, late_ids = range(len(early)), range(len(early), len(pieces))

    def body(plan_ref, x_ref, w_in_in, w_out_in, w_pool_in, c_ref, up_ref, down_ref,
             o1_ref, o4_ref, o16_ref, hug_ref, w_ref, w_out_ref, w_pool_ref,
             wbuf_ref, res_ref, w_sem, ici_send, ici_recv, d2d_send, d2d_recv):
        j, i = pl.program_id(0), pl.program_id(1)
        mx, my, mc, chips = _mesh_place()
        sibling = (mx, my, 1 - mc)
        gathered = (w_ref, w_out_ref, w_pool_ref)
        chip_of = lambda k: 2 * chips[k][0] + chips[k][1]

        def piece(n, chip, core):
            a, per_core, offset, size = pieces[n]
            return gathered[a].at[chip, pl.ds(core * per_core + offset, size)]

        def to_neighbour(k, n):
            mine = piece(n, 2 * mx + my, mc)
            return _remote(mine, mine, ici_send.at[n, k], ici_recv.at[n, k], (*chips[k], mc))

        def relay(n):
            theirs = piece(n, 2 * (mx ^ (1 - mc)) + (my ^ mc), mc)
            return _remote(theirs, theirs, ici_send.at[n, DIAGONAL], ici_recv.at[n, DIAGONAL], (mx ^ mc, my ^ (1 - mc), mc))

        def arrival(k, n):
            theirs = piece(n, chip_of(k), mc)
            return _remote(theirs, theirs, ici_send.at[n, k], ici_recv.at[n, k], (*chips[k], mc))

        def to_sibling(k, n, core):
            theirs = piece(n, chip_of(k), core)
            return _remote(theirs, theirs, d2d_send.at[n, k], d2d_recv.at[n, k], sibling)

        def take(k, ids):
            for n in ids:
                arrival(k, n).wait_recv()
                to_sibling(k, n, mc).start()

        def taken(k, ids):
            for n in ids:
                to_sibling(k, n, 1 - mc).wait_recv()

        first_tile = i == 0

        @pl.when(jnp.logical_and(j == 0, first_tile))
        def _():
            for n in range(len(pieces)):
                for k in range(DIAGONAL):
                    to_neighbour(k, n).start()

        @pl.when(jnp.logical_and(j == 1, first_tile))
        def _():
            take(0, early_ids)
            taken(0, early_ids)

        @pl.when(jnp.logical_and(j == 2, first_tile))
        def _():
            take(1, early_ids)
            for n in early_ids:
                relay(n).start()
            taken(1, early_ids)
            for k in range(DIAGONAL):
                take(k, late_ids)
            for n in late_ids:
                relay(n).start()
            for k in range(DIAGONAL):
                taken(k, late_ids)

        @pl.when(jnp.logical_and(j == 3, first_tile))
        def _():
            take(DIAGONAL, range(len(pieces)))
            taken(DIAGONAL, range(len(pieces)))

        shard = plan_ref[j]

        @pl.when(first_tile)
        def _():
            cp = pltpu.make_async_copy(w_ref.at[shard], wbuf_ref, w_sem)
            cp.start()
            cp.wait()

        xb = x_ref[...].astype(BF16)
        group = 4 * HEAD_DIM
        accs = [_dot_nn(xb, wbuf_ref[:, g * group:(g + 1) * group]) for g in range(tn // group)]

        def emit_qkv(rotated_heads):
            for h in range(heads):
                lanes = (h * HEAD_DIM) % group
                th = accs[h * HEAD_DIM // group][:, lanes:lanes + HEAD_DIM]
                if h < rotated_heads:
                    th = _rotate_heads(th, c_ref[...], up_ref[...], down_ref[...])
                res_ref[h] = th
                o1_ref[:, h * HEAD_DIM:(h + 1) * HEAD_DIM] = th.astype(BF16)
            _to_pattern(res_ref, o4_ref, d4, BF16)
            _to_pattern(res_ref, o16_ref, d16, BF16)

        @pl.when(shard == 0)
        def _():
            emit_qkv(heads)

        @pl.when(shard == 1)
        def _():
            emit_qkv(k_heads_in_second)

        @pl.when(shard >= 2)
        def _():
            for g, acc in enumerate(accs):
                hug_ref[:, g * group:(g + 1) * group] = acc.astype(BF16)

        @pl.when(jnp.logical_and(j == 3, i == n_tiles - 1))
        def _():
            for n in range(len(pieces)):
                for k in range(DIAGONAL):
                    to_neighbour(k, n).wait_send()
                relay(n).wait_send()
                for k in range(DIAGONAL + 1):
                    to_sibling(k, n, mc).wait_send()

    def held(base, last):
        return lambda j, i, plan_ref: jnp.where(plan_ref[base + j] == -1, i,
                                                jnp.where(plan_ref[base + j] == -2, last, 0))

    q_row, h_row = held(8, n_tiles - 1), held(16, n_tiles - 1)
    tab_spec = pl.BlockSpec((tm, HEAD_DIM), lambda j, i, plan_ref: (i, 0))
    sems = [pltpu.SemaphoreType.DMA((len(pieces), 3))] * 4
    o1, o4, o16, hug, w_in_g, w_out_g, w_pool_g = _pallas(
        body, name="in_proj_gathering",
        grid_spec=pltpu.PrefetchScalarGridSpec(
            num_scalar_prefetch=1, grid=(N_SHARDS, n_tiles),
            in_specs=[pl.BlockSpec((tm, D_MODEL), lambda j, i, plan_ref: (i, 0)), ANY, ANY, ANY,
                      tab_spec, tab_spec, tab_spec],
            out_specs=[pl.BlockSpec((tm, tn), lambda j, i, p: (q_row(j, i, p), p[4 + j])),
                       pl.BlockSpec((d4, tm // d4, tn), lambda j, i, p: (0, q_row(j, i, p), p[4 + j])),
                       pl.BlockSpec((d16, tm // d16, tn), lambda j, i, p: (0, q_row(j, i, p), p[4 + j])),
                       pl.BlockSpec((tm, tn), lambda j, i, p: (h_row(j, i, p), p[12 + j])),
                       ANY, ANY, ANY],
            scratch_shapes=[pltpu.VMEM((D_MODEL, tn), BF16), pltpu.VMEM((heads, tm, HEAD_DIM), F32),
                            pltpu.SemaphoreType.DMA(())] + sems),
        out_shape=[jax.ShapeDtypeStruct((seq, D_QKV), BF16),
                   jax.ShapeDtypeStruct((d4, seq // d4, D_QKV), BF16),
                   jax.ShapeDtypeStruct((d16, seq // d16, D_QKV), BF16),
                   jax.ShapeDtypeStruct((seq, D_UG), BF16)]
        + [jax.ShapeDtypeStruct(b.shape, b.dtype) for b in w_bufs],
        input_output_aliases={2: 4, 3: 5, 4: 6},
        compiler_params=_params(("arbitrary", "arbitrary"), 52),
    )(plan, x, *w_bufs, *tabs)
    return [o1[None], o4, o16], hug, w_in_g, w_out_g, w_pool_g


def _band_masks():
    row = lax.broadcasted_iota(jnp.int32, (KEY_BLOCK, KEY_BLOCK), 0)
    col = lax.broadcasted_iota(jnp.int32, (KEY_BLOCK, KEY_BLOCK), 1)
    return col <= row, col >= row


def _attn_fwd(qkv, name):
    dil, n, _ = qkv.shape
    scale = HEAD_DIM ** -0.5
    lo, hi = slice(0, KEY_BLOCK), slice(KEY_BLOCK, CHUNK)

    def body(q_ref, k_ref, v_ref, kb_ref, vb_ref, o_ref, st_ref):
        i = pl.program_id(1)
        cur_mask, prev_mask = _band_masks()
        before_mask = jnp.logical_and(prev_mask, i > 0)
        lane = lax.broadcasted_iota(jnp.int32, (KEY_BLOCK, STAT_LANES), 1)
        tasks = [(rows, h) for rows in (lo, hi) for h in range(N_HEADS)]
        head = lambda h: slice(h * HEAD_DIM, (h + 1) * HEAD_DIM)

        def prev_of(rows, h):
            if rows is lo:
                return kb_ref[:, head(h)], vb_ref[:, head(h)], before_mask
            return k_ref[lo, head(h)], v_ref[lo, head(h)], prev_mask

        scores = []
        for rows, h in tasks:
            q = q_ref[rows, head(h)]
            scores.append((_dot_nt(q, prev_of(rows, h)[0]), _dot_nt(q, k_ref[rows, head(h)])))
        probs = []
        for (rows, h), (qk_prev, qk_cur) in zip(tasks, scores):
            s_prev = jnp.where(prev_of(rows, h)[2], qk_prev * scale, NEG)
            s_cur = jnp.where(cur_mask, qk_cur * scale, NEG)
            m = jnp.max(jnp.maximum(s_prev, s_cur), axis=-1, keepdims=True)
            p_prev = jnp.exp(s_prev - m)
            p_cur = jnp.exp(s_cur - m)
            den = jnp.sum(p_prev + p_cur, axis=-1, keepdims=True)
            probs.append((p_prev.astype(BF16), p_cur.astype(BF16), den, m + jnp.log(den)))
        stats = [jnp.zeros((KEY_BLOCK, STAT_LANES), F32), jnp.zeros((KEY_BLOCK, STAT_LANES), F32)]
        for (rows, h), (p_prev, p_cur, den, lse) in zip(tasks, probs):
            o = _dot_nn(p_cur, v_ref[rows, head(h)]) + _dot_nn(p_prev, prev_of(rows, h)[1])
            o_ref[rows, head(h)] = (o / den).astype(BF16)
            b = 0 if rows is lo else 1
            stats[b] = jnp.where(lane == h, lse, stats[b])
        st_ref[lo, :] = stats[0]
        st_ref[hi, :] = stats[1]

    main = lambda cb: pl.BlockSpec((None, CHUNK, D_ATTN), lambda r, i: (r, i, cb))
    before = lambda cb: pl.BlockSpec((None, KEY_BLOCK, D_ATTN), lambda r, i: (r, jnp.maximum(2 * i - 1, 0), cb))
    return _pallas(
        body, name=name, grid=(dil, n // CHUNK),
        in_specs=[main(0), main(1), main(2), before(1), before(2)],
        out_specs=[main(0), pl.BlockSpec((None, CHUNK, STAT_LANES), lambda r, i: (r, i, 0))],
        out_shape=[jax.ShapeDtypeStruct((dil, n, D_ATTN), BF16), jax.ShapeDtypeStruct((dil, n, STAT_LANES), F32)],
        compiler_params=_params(("parallel", "parallel"), 40),
    )(qkv, qkv, qkv, qkv, qkv)


def _attn_bwd(qkv, do, stats, name, comm=None):
    dil, n, _ = qkv.shape
    n_blocks = n // KEY_BLOCK
    last = n // CHUNK - 1
    scale = HEAD_DIM ** -0.5
    lo, hi = slice(0, KEY_BLOCK), slice(KEY_BLOCK, CHUNK)

    def body(q_ref, k_ref, v_ref, kb_ref, vb_ref, qa_ref, do_ref, doa_ref, st_ref, sta_ref, dq_ref, dk_ref, dv_ref):
        i = pl.program_id(1)
        cur_mask, prev_mask = _band_masks()
        before_mask = jnp.logical_and(prev_mask, i > 0)
        after_mask = jnp.logical_and(prev_mask, i < last)

        rows_cat = lambda a, b: jnp.concatenate([a, b], axis=0)
        masks = (jnp.concatenate([before_mask, cur_mask], axis=1), jnp.concatenate([prev_mask, cur_mask], axis=1),
                 after_mask)

        def operands(h):
            cols = slice(h * HEAD_DIM, (h + 1) * HEAD_DIM)
            lse_c, del_c = slice(h, h + 1), slice(N_HEADS + h, N_HEADS + h + 1)
            q = (q_ref[lo, cols], q_ref[hi, cols], qa_ref[:, cols])
            do = (do_ref[lo, cols], do_ref[hi, cols], doa_ref[:, cols])
            keys = (rows_cat(kb_ref[:, cols], k_ref[lo, cols]), k_ref[:, cols], k_ref[hi, cols])
            vals = (rows_cat(vb_ref[:, cols], v_ref[lo, cols]), v_ref[:, cols], v_ref[hi, cols])
            st = ((st_ref[lo, lse_c], st_ref[lo, del_c]), (st_ref[hi, lse_c], st_ref[hi, del_c]),
                  (sta_ref[:, lse_c], sta_ref[:, del_c]))
            return cols, q, do, keys, vals, st

        group = N_HEADS // 2
        for first_head in range(0, N_HEADS, group):
            heads = range(first_head, first_head + group)
            raw = {}
            for h in heads:
                _, q, do, keys, vals, _ = operands(h)
                raw[h] = [(_dot_nt(q[j], keys[j]), _dot_nt(do[j], vals[j])) for j in range(3)]
            grads = {}
            for h in heads:
                st = operands(h)[5]
                grads[h] = []
                for j in range(3):
                    qk, dp = raw[h][j]
                    lse, delta = st[j]
                    p = jnp.exp(jnp.where(masks[j], qk * scale, NEG) - lse)
                    grads[h].append((p.astype(BF16), (p * (dp - delta) * scale).astype(BF16)))
            for h in heads:
                cols, q, do, keys, _, _ = operands(h)
                (p0, ds0), (p1, ds1), (pa, dsa) = grads[h]
                own, nxt = slice(KEY_BLOCK, CHUNK), slice(0, KEY_BLOCK)

                def put(ref, rows, val, cols=cols):
                    ref[rows, cols] = val.astype(ref.dtype)

                put(dq_ref, lo, _dot_nn(ds0, keys[0]))
                put(dq_ref, hi, _dot_nn(ds1, keys[1]))
                put(dk_ref, lo, _dot_tn(rows_cat(ds0[:, own], ds1[:, nxt]), q_ref[:, cols]))
                put(dk_ref, hi, _dot_tn(rows_cat(ds1[:, own], dsa), rows_cat(q[1], q[2])))
                put(dv_ref, lo, _dot_tn(rows_cat(p0[:, own], p1[:, nxt]), do_ref[:, cols]))
                put(dv_ref, hi, _dot_tn(rows_cat(p1[:, own], pa), rows_cat(do[1], do[2])))

    def spec(rows, width, row_of, cb):
        return pl.BlockSpec((None, rows, width), lambda r, i: (r, row_of(i), cb))

    same = lambda i: i
    before = lambda i: jnp.maximum(2 * i - 1, 0)
    after = lambda i: jnp.minimum(2 * i + 2, n_blocks - 1)
    out = spec(CHUNK, D_ATTN, same, 0)
    return _call(
        body, name=name, grid=(dil, n // CHUNK),
        in_specs=[spec(CHUNK, D_ATTN, same, 0), spec(CHUNK, D_ATTN, same, 1), spec(CHUNK, D_ATTN, same, 2),
                  spec(KEY_BLOCK, D_ATTN, before, 1), spec(KEY_BLOCK, D_ATTN, before, 2),
                  spec(KEY_BLOCK, D_ATTN, after, 0),
                  spec(CHUNK, D_ATTN, same, 0), spec(KEY_BLOCK, D_ATTN, after, 0),
                  spec(CHUNK, STAT_LANES, same, 0), spec(KEY_BLOCK, STAT_LANES, after, 0)],
        out_specs=[out, out, out],
        out_shape=[jax.ShapeDtypeStruct((dil, n, D_ATTN), BF16)] * 3,
        scratch_shapes=[], semantics=("parallel", "parallel"), vmem_mib=40,
        args=(qkv, qkv, qkv, qkv, qkv, qkv, do, do, stats, stats), comm=comm)


def _window_sums(ext, window, backward):
    rows = ext.shape[0]
    acc, span = ext, 1
    while span < window:
        acc = acc + pltpu.roll(acc, (rows - span) if backward else span, axis=0)
        span *= 2
    return acc


def _mix_gate(o_list, st_list, hug, w_pool_g, pool_scale):
    seq = hug.shape[0]
    tm = 256
    halo_blocks = tm // POOL_HALO
    d4, d16 = DILATIONS[1], DILATIONS[2]

    def body(o1_ref, o4_ref, o16_ref, l1_ref, l4_ref, l16_ref, u_ref, halo_ref, ga_ref, gp_ref, wp_ref, sc_ref,
             y_ref, mix_ref, lse_ref, pooled_ref, n4_ref, n16_ref, nl4_ref, nl16_ref):
        i = pl.program_id(0)
        _from_pattern(o4_ref, n4_ref, d4)
        _from_pattern(o16_ref, n16_ref, d16)
        _from_pattern(l4_ref, nl4_ref, d4)
        _from_pattern(l16_ref, nl16_ref, d16)
        la, lb, lc = l1_ref[...], nl4_ref[0], nl16_ref[0]
        mx = jnp.maximum(jnp.maximum(la, lb), lc)
        ea, eb, ec = jnp.exp(la - mx), jnp.exp(lb - mx), jnp.exp(lc - mx)
        tot = ea + eb + ec
        lse_ref[...] = mx + jnp.log(tot)
        wa, wb, wc = ea / tot, eb / tot, ec / tot
        ga = ga_ref[...].astype(F32)
        silu_a = ga * jax.nn.sigmoid(ga)
        for h in range(N_HEADS):
            cols = slice(h * HEAD_DIM, (h + 1) * HEAD_DIM)
            hc = slice(h, h + 1)
            attn = wa[:, hc] * o1_ref[:, cols].astype(F32) + wb[:, hc] * n4_ref[h] + wc[:, hc] * n16_ref[h]
            mix_ref[:, cols] = attn.astype(BF16)
            y_ref[:, cols] = (attn * silu_a[:, cols]).astype(BF16)

        u = u_ref[...].astype(F32)
        halo = jnp.where(i > 0, halo_ref[...].astype(F32), 0.0)
        ext = jnp.concatenate([halo, u], axis=0)
        pos = i * tm + lax.broadcasted_iota(jnp.int32, (tm, 1), 0)
        gp = gp_ref[...].astype(F32)
        gated_scale = sc_ref[...] * (gp * jax.nn.sigmoid(gp))
        for g, window in enumerate(POOL_WINDOWS):
            cols = slice(g * POOL_GROUP_DIM, (g + 1) * POOL_GROUP_DIM)
            sums = _window_sums(ext[:, cols], window, backward=False)[POOL_HALO:, :]
            count = jnp.minimum(pos + 1, window).astype(F32)
            pooled = (sums / count - u[:, cols]).astype(BF16)
            pooled_ref[:, cols] = pooled
            pre = _dot_nn(pooled, wp_ref[g])
            out_cols = slice(D_ATTN + g * POOL_GROUP_DIM, D_ATTN + (g + 1) * POOL_GROUP_DIM)
            mix_ref[:, out_cols] = pre.astype(BF16)
            y_ref[:, out_cols] = (pre * gated_scale[:, cols]).astype(BF16)

    row = lambda width, cb=0: pl.BlockSpec((tm, width), lambda i: (i, cb))
    pat = lambda d, width: pl.BlockSpec((d, tm // d, width), lambda i: (0, i, 0))
    return _pallas(
        body, name="mix_gate", grid=(seq // tm,),
        in_specs=[row(D_ATTN), pat(d4, D_ATTN), pat(d16, D_ATTN),
                  row(STAT_LANES), pat(d4, STAT_LANES), pat(d16, STAT_LANES),
                  row(D_POOL),
                  pl.BlockSpec((POOL_HALO, D_POOL), lambda i: (jnp.maximum(i * halo_blocks - 1, 0), 0)),
                  row(D_ATTN, 1), row(D_POOL, 2),
                  pl.BlockSpec((len(POOL_WINDOWS), POOL_GROUP_DIM, POOL_GROUP_DIM), lambda i: (0, 0, 0)),
                  pl.BlockSpec((1, D_POOL), lambda i: (0, 0))],
        out_specs=[row(D_MODEL), row(D_MODEL), row(STAT_LANES), row(D_POOL)],
        out_shape=[jax.ShapeDtypeStruct((seq, D_MODEL), BF16), jax.ShapeDtypeStruct((seq, D_MODEL), BF16),
                   jax.ShapeDtypeStruct((seq, STAT_LANES), F32), jax.ShapeDtypeStruct((seq, D_POOL), BF16)],
        scratch_shapes=[pltpu.VMEM((N_HEADS, tm, HEAD_DIM), F32), pltpu.VMEM((N_HEADS, tm, HEAD_DIM), F32),
                        pltpu.VMEM((1, tm, STAT_LANES), F32), pltpu.VMEM((1, tm, STAT_LANES), F32)],
        compiler_params=_params(("parallel",), 48),
    )(o_list[0][0], o_list[1], o_list[2], st_list[0][0], st_list[1], st_list[2],
      hug, hug, hug, hug, w_pool_g, pool_scale)


def _out_proj_loss(y, w_out_g, x, target, gain, bias):
    seq = x.shape[0]
    tm = 512

    def body(y_ref, w_ref, x_ref, t_ref, g_ref, b_ref, dz_ref, dzb_ref, gg_ref, gb_ref, loss_ref):
        @pl.when(pl.program_id(0) == 0)
        def _():
            gg_ref[...] = jnp.zeros_like(gg_ref)
            gb_ref[...] = jnp.zeros_like(gb_ref)
            loss_ref[...] = jnp.zeros_like(loss_ref)

        halves = [slice(0, tm // 2), slice(tm // 2, tm)]
        projected = [_dot_nn(y_ref[rows, :], w_ref[...]) for rows in halves]
        for rows, out in zip(halves, projected):
            z = DEEPNORM_ALPHA * x_ref[rows, :] + out
            mu = jnp.mean(z, axis=-1, keepdims=True)
            zc = z - mu
            rstd = lax.rsqrt(jnp.mean(zc * zc, axis=-1, keepdims=True) + LN_EPS)
            xhat = zc * rstd
            gain_v = g_ref[...]
            diff = xhat * gain_v + b_ref[...] - t_ref[rows, :]
            sq = _fold_rows(diff * diff)
            part = sq[:, :128]
            for k in range(1, D_MODEL // 128):
                part = part + sq[:, k * 128:(k + 1) * 128]
            loss_ref[...] += part
            dln = diff * (1.0 / D_MODEL)
            gg_ref[...] += _fold_rows(dln * xhat)
            gb_ref[...] += _fold_rows(dln)
            dxhat = dln * gain_v
            dz = rstd * (dxhat - jnp.mean(dxhat, axis=-1, keepdims=True)
                         - xhat * jnp.mean(dxhat * xhat, axis=-1, keepdims=True))
            dz_ref[rows, :] = dz
            dzb_ref[rows, :] = dz.astype(BF16)

    row = lambda: pl.BlockSpec((tm, D_MODEL), lambda i: (i, 0))
    vec = lambda: pl.BlockSpec((1, D_MODEL), lambda i: (0, 0))
    acc = lambda width: pl.BlockSpec((8, width), lambda i: (0, 0))
    return _pallas(
        body, name="out_proj_loss", grid=(seq // tm,),
        in_specs=[row(), pl.BlockSpec((D_MODEL, D_MODEL), lambda i: (0, 0), pipeline_mode=pl.Buffered(1)),
                  row(), row(), vec(), vec()],
        out_specs=[row(), row(), acc(D_MODEL), acc(D_MODEL), acc(128)],
        out_shape=[jax.ShapeDtypeStruct((seq, D_MODEL), F32), jax.ShapeDtypeStruct((seq, D_MODEL), BF16),
                   jax.ShapeDtypeStruct((8, D_MODEL), F32), jax.ShapeDtypeStruct((8, D_MODEL), F32),
                   jax.ShapeDtypeStruct((8, 128), F32)],
        compiler_params=_params(("arbitrary",), 56),
    )(y, w_out_g.reshape(D_MODEL, D_MODEL), x, target, gain, bias)


def _dy_gate_bwd(dzb, w_out_g, hug, mixpre, pool_scale, lse_all):
    seq = dzb.shape[0]
    tm = 256
    d4, d16 = DILATIONS[1], DILATIONS[2]

    def body(dz_ref, w_ref, ga_ref, gp_ref, mix_ref, sc_ref, lse_ref,
             dh_ref, dpo_ref, do1_ref, do4_ref, do16_ref, st1_ref, st4_ref, st16_ref, da_ref, st_ref):
        dy = _dot_nt(dz_ref[...], w_ref[...])
        ga = ga_ref[...].astype(F32)
        sig = jax.nn.sigmoid(ga)
        attn = mix_ref[:, :D_ATTN].astype(F32)
        dya = dy[:, :D_ATTN]
        dattn = dya * (ga * sig)
        dh_ref[:, :D_ATTN] = (dya * attn * (sig * (1.0 + ga * (1.0 - sig)))).astype(BF16)
        _store_slabs(da_ref, dattn)
        lane = lax.broadcasted_iota(jnp.int32, (tm, STAT_LANES), 1)
        stats = lse_ref[...]
        prod = dattn * attn
        for h in range(N_HEADS):
            delta = jnp.sum(prod[:, h * HEAD_DIM:(h + 1) * HEAD_DIM], axis=-1, keepdims=True)
            stats = jnp.where(lane == N_HEADS + h, delta, stats)
        st_ref[0] = stats
        do1_ref[...] = dattn.astype(BF16)
        st1_ref[...] = stats
        _to_pattern(da_ref, do4_ref, d4, BF16)
        _to_pattern(da_ref, do16_ref, d16, BF16)
        _to_pattern(st_ref, st4_ref, d4, F32)
        _to_pattern(st_ref, st16_ref, d16, F32)

        gp = gp_ref[...].astype(F32)
        sig = jax.nn.sigmoid(gp)
        dyp = dy[:, D_ATTN:]
        dpo_ref[...] = (dyp * (gp * sig)).astype(BF16)
        dh_ref[:, D_ATTN:] = (dyp * (mix_ref[:, D_ATTN:].astype(F32) * sc_ref[...])
                              * (sig * (1.0 + gp * (1.0 - sig)))).astype(BF16)

    row = lambda width, cb=0: pl.BlockSpec((tm, width), lambda i: (i, cb))
    pat = lambda d, width: pl.BlockSpec((d, tm // d, width), lambda i: (0, i, 0))
    pat_shape = lambda d, width, dtype: jax.ShapeDtypeStruct((d, seq // d, width), dtype)
    outs = _pallas(
        body, name="dy_gate_bwd", grid=(seq // tm,),
        in_specs=[row(D_MODEL), pl.BlockSpec((D_MODEL, D_MODEL), lambda i: (0, 0)),
                  row(D_ATTN, 1), row(D_POOL, 2), row(D_MODEL), pl.BlockSpec((1, D_POOL), lambda i: (0, 0)),
                  row(STAT_LANES)],
        out_specs=[row(D_MODEL, D_IN // D_MODEL - 1), row(D_POOL),
                   row(D_ATTN), pat(d4, D_ATTN), pat(d16, D_ATTN),
                   row(STAT_LANES), pat(d4, STAT_LANES), pat(d16, STAT_LANES)],
        out_shape=[jax.ShapeDtypeStruct((seq, D_IN), BF16), jax.ShapeDtypeStruct((seq, D_POOL), BF16),
                   jax.ShapeDtypeStruct((seq, D_ATTN), BF16), pat_shape(d4, D_ATTN, BF16), pat_shape(d16, D_ATTN, BF16),
                   jax.ShapeDtypeStruct((seq, STAT_LANES), F32), pat_shape(d4, STAT_LANES, F32),
                   pat_shape(d16, STAT_LANES, F32)],
        scratch_shapes=[pltpu.VMEM((N_HEADS, tm, HEAD_DIM), F32), pltpu.VMEM((1, tm, STAT_LANES), F32)],
        compiler_params=_params(("parallel",), 48),
    )(dzb, w_out_g.reshape(D_MODEL, D_MODEL), hug, hug, mixpre, pool_scale, lse_all)
    dh, dpo, do1, do4, do16, st1, st4, st16 = outs
    return dh, dpo, [do1[None], do4, do16], [st1[None], st4, st16]


def _pool_bwd(dh, dpo, mixpre, pooled, w_pool_g, pool_scale):
    seq = dpo.shape[0]
    tm = 256
    halo_blocks = tm // POOL_HALO
    last = seq // tm - 1
    n_groups = len(POOL_WINDOWS)

    def body(dh_in_ref, dpo_ref, halo_ref, pre_ref, pooled_ref, wp_ref, sc_ref, du_ref, gw_ref, gs_ref):
        i = pl.program_id(0)

        @pl.when(i == 0)
        def _():
            gw_ref[...] = jnp.zeros_like(gw_ref)
            gs_ref[...] = jnp.zeros_like(gs_ref)

        dpo = dpo_ref[...].astype(F32)
        scale = sc_ref[...]
        gs_ref[...] += _fold_rows(dpo * pre_ref[...].astype(F32))
        halo = jnp.where(i < last, halo_ref[...].astype(F32), 0.0)
        dpw = (jnp.concatenate([dpo, halo], axis=0) * scale).astype(BF16)
        pos = i * tm + lax.broadcasted_iota(jnp.int32, (tm + POOL_HALO, 1), 0)
        for g, window in enumerate(POOL_WINDOWS):
            cols = slice(g * POOL_GROUP_DIM, (g + 1) * POOL_GROUP_DIM)
            dpw_g = dpw[:, cols]
            gw_ref[g] += _dot_tn(pooled_ref[:, cols], dpw_g[:tm, :])
            dpooled = _dot_nt(dpw_g, wp_ref[g])
            count = jnp.minimum(pos + 1, window).astype(F32)
            sums = _window_sums(dpooled / count, window, backward=True)
            du_ref[:, cols] = (sums[:tm, :] - dpooled[:tm, :]).astype(BF16)

    row = lambda width, cb=0: pl.BlockSpec((tm, width), lambda i: (i, cb))
    return _pallas(
        body, name="pool_bwd", grid=(seq // tm,),
        in_specs=[ANY, row(D_POOL),
                  pl.BlockSpec((POOL_HALO, D_POOL),
                               lambda i: (jnp.minimum((i + 1) * halo_blocks, seq // POOL_HALO - 1), 0)),
                  row(D_POOL, 1), row(D_POOL),
                  pl.BlockSpec((n_groups, POOL_GROUP_DIM, POOL_GROUP_DIM), lambda i: (0, 0, 0)),
                  pl.BlockSpec((1, D_POOL), lambda i: (0, 0))],
        out_specs=[row(D_POOL, D_QKV // D_POOL),
                   pl.BlockSpec((n_groups, POOL_GROUP_DIM, POOL_GROUP_DIM), lambda i: (0, 0, 0)),
                   pl.BlockSpec((8, D_POOL), lambda i: (0, 0))],
        out_shape=[jax.ShapeDtypeStruct(dh.shape, dh.dtype),
                   jax.ShapeDtypeStruct((n_groups, POOL_GROUP_DIM, POOL_GROUP_DIM), F32),
                   jax.ShapeDtypeStruct((8, D_POOL), F32)],
        input_output_aliases={0: 0},
        compiler_params=_params(("arbitrary",), 40),
    )(dh, dpo, dpo, mixpre, pooled, w_pool_g, pool_scale)


def _sum_patterns(dh, parts, tabs, unrotate, col_block, name, comm=None):
    seq = dh.shape[0]
    tm, tn = 256, D_ATTN
    per = D_ATTN // tn
    d4, d16 = DILATIONS[1], DILATIONS[2]

    def body(dh_in_ref, a1_ref, a4_ref, a16_ref, ct_ref, up_ref, down_ref, o_ref, n4_ref, n16_ref):
        _from_pattern(a4_ref, n4_ref, d4)
        _from_pattern(a16_ref, n16_ref, d16)
        for s in range(tn // HEAD_DIM):
            cols = slice(s * HEAD_DIM, (s + 1) * HEAD_DIM)
            tot = a1_ref[:, cols].astype(F32) + n4_ref[s] + n16_ref[s]
            if unrotate:
                tot = _rotate_heads(tot, ct_ref[...], -up_ref[...], -down_ref[...])
            o_ref[:, cols] = tot.astype(BF16)

    tab = pl.BlockSpec((tm, HEAD_DIM), lambda i, j: (i, 0))
    pat = lambda d: pl.BlockSpec((d, tm // d, tn), lambda i, j: (0, i, j))
    (dh,), exchanged = _call(
        body, name=name, grid=(seq // tm, per),
        in_specs=[ANY, pl.BlockSpec((tm, tn), lambda i, j: (i, j)), pat(d4), pat(d16), tab, tab, tab],
        out_specs=[pl.BlockSpec((tm, tn), lambda i, j: (i, col_block * per + j))],
        out_shape=[jax.ShapeDtypeStruct(dh.shape, dh.dtype)],
        scratch_shapes=[pltpu.VMEM((tn // HEAD_DIM, tm, HEAD_DIM), F32), pltpu.VMEM((tn // HEAD_DIM, tm, HEAD_DIM), F32)],
        semantics=("parallel", "parallel"), vmem_mib=32, args=(dh, parts[0][0], parts[1], parts[2], *tabs),
        aliases={0: 0}, comm=comm)
    return dh, exchanged


def _grad_w_in(x, dh, half, name, comm=None):
    seq = x.shape[0]
    ts, td, te = 2048, D_MODEL // 2, SHARD_IN

    def body(half_ref, x_ref, dh_ref, o_ref):
        k = pl.program_id(1)
        part = _dot_tn(x_ref[...].astype(BF16), dh_ref[...])

        @pl.when(k == 0)
        def _():
            o_ref[...] = part

        @pl.when(k > 0)
        def _():
            o_ref[...] += part

    (g,), exchanged = _call(
        body, name=name, grid=(N_SHARDS, seq // ts),
        in_specs=[pl.BlockSpec((ts, td), lambda e, k, half_ref: (k, half_ref[0])),
                  pl.BlockSpec((ts, te), lambda e, k, half_ref: (k, e))],
        out_specs=[pl.BlockSpec((None, td, te), lambda e, k, half_ref: (e, 0, 0))],
        out_shape=[jax.ShapeDtypeStruct((N_SHARDS, td, te), F32)],
        scratch_shapes=[], semantics=("parallel", "arbitrary"), vmem_mib=56, args=(x, dh), comm=comm,
        prefetch=(half,))
    return g, exchanged


def _grad_w_out(y, dzb):
    seq = y.shape[0]
    ts, te = 512, 1024
    nk = seq // ts

    def body(y_ref, dz_ref, o_ref, acc_ref):
        k = pl.program_id(1)

        @pl.when(k == 0)
        def _():
            acc_ref[...] = jnp.zeros_like(acc_ref)

        acc_ref[...] += _dot_tn(y_ref[...], dz_ref[...])

        @pl.when(k == nk - 1)
        def _():
            o_ref[...] = acc_ref[...]

    return _pallas(
        body, name="grad_w_out", grid=(D_MODEL // te, nk),
        in_specs=[pl.BlockSpec((ts, te), lambda e, k: (k, e)), pl.BlockSpec((ts, D_MODEL), lambda e, k: (k, 0))],
        out_specs=pl.BlockSpec((te, D_MODEL), lambda e, k: (e, 0)),
        out_shape=jax.ShapeDtypeStruct((D_MODEL, D_MODEL), F32),
        scratch_shapes=[pltpu.VMEM((te, D_MODEL), F32)],
        compiler_params=_params(("parallel", "arbitrary"), 48),
    )(y, dzb)


GRAD_X_LATE_SHARDS = 1
GRAD_X_PARTIAL_ROWS = 1024


def _grad_x_partial(dh, w_in_g, first, tiles, prev=None, comm=None):
    seq = dh.shape[0]
    tm, tk = GRAD_X_PARTIAL_ROWS, SHARD_IN

    def body(*refs):
        dh_ref, w_ref, o_ref = refs[-3:]
        k = pl.program_id(1)
        part = _dot_nt(dh_ref[...], w_ref[...])

        @pl.when(k == 0)
        def _():
            o_ref[...] = part

        @pl.when(k > 0)
        def _():
            o_ref[...] += part

    carried = [] if prev is None else [prev]
    (partial,), exchanged = _call(
        body, name="grad_x_partial_%d" % first, grid=(tiles, N_SHARDS - GRAD_X_LATE_SHARDS),
        in_specs=[ANY] * len(carried) + [
            pl.BlockSpec((tm, tk), lambda i, k: (i + first, k)),
            pl.BlockSpec((None, D_MODEL, tk), lambda i, k: (k, 0, 0))],
        out_specs=[pl.BlockSpec((tm, D_MODEL), lambda i, k: (i + first, 0))],
        out_shape=[jax.ShapeDtypeStruct((seq, D_MODEL), F32)],
        scratch_shapes=[], semantics=("parallel", "arbitrary"), vmem_mib=48, args=(*carried, dh, w_in_g),
        aliases={0: 0} if carried else None, comm=comm)
    return partial, exchanged


def _grad_x_final(dh, w_in_g, dz, partial):
    seq = dh.shape[0]
    tm, tk = 512, SHARD_IN
    k0 = N_SHARDS - GRAD_X_LATE_SHARDS

    def body(dh_ref, w_ref, dz_ref, p_ref, o_ref):
        k = pl.program_id(1)
        part = _dot_nt(dh_ref[...], w_ref[...])

        @pl.when(k == 0)
        def _():
            o_ref[...] = (DEEPNORM_ALPHA * dz_ref[...] + p_ref[...]) + part

        @pl.when(k > 0)
        def _():
            o_ref[...] += part

    row = pl.BlockSpec((tm, D_MODEL), lambda i, k: (i, 0))
    return _pallas(
        body, name="grad_x_final", grid=(seq // tm, GRAD_X_LATE_SHARDS),
        in_specs=[pl.BlockSpec((tm, tk), lambda i, k: (i, k + k0)),
                  pl.BlockSpec((None, D_MODEL, tk), lambda i, k: (k + k0, 0, 0)), row, row],
        out_specs=row, out_shape=jax.ShapeDtypeStruct((seq, D_MODEL), F32),
        compiler_params=_params(("parallel", "arbitrary"), 48),
    )(dh, w_in_g, dz, partial)


def _pool_weight(w_pool_sh):
    n_groups = len(POOL_WINDOWS)
    shard_c = POOL_GROUP_DIM // N_SHARDS
    return (w_pool_sh.reshape(N_SHARDS, n_groups, shard_c, POOL_GROUP_DIM).transpose(1, 0, 2, 3)
            .reshape(n_groups, POOL_GROUP_DIM, POOL_GROUP_DIM))


def _pool_grad_pieces(g_w_pool):
    n_groups = len(POOL_WINDOWS)
    half_c = POOL_GROUP_DIM // N_SHARDS // 2
    return (g_w_pool.reshape(n_groups, N_SHARDS, 2, half_c, POOL_GROUP_DIM).transpose(1, 2, 0, 3, 4)
            .reshape(N_SHARDS, 2, n_groups * half_c, POOL_GROUP_DIM))


def _step(x, target, w_bufs, pool_scale, gain, bias, place):
    seq = x.shape[0]
    tabs = _rope_tables(seq)
    core, chip_core, onward, plan = place
    qkv, hug, w_in_g, w_out_g, w_pool_sh = _in_proj_gathering(x, w_bufs, tabs, plan)
    o_list, st_list = [], []
    for p, dil in enumerate(DILATIONS):
        o, st = _attn_fwd(qkv[p], "attn_fwd_d%d" % dil)
        o_list.append(o)
        st_list.append(st)
    w_pool_g = _pool_weight(w_pool_sh)
    y, mixpre, lse_all, pooled = _mix_gate(o_list, st_list, hug, w_pool_g, pool_scale)
    dz, dzb, gain_part, bias_part, loss_part = _out_proj_loss(y, w_out_g, x, target, gain, bias)
    dh, dpo, do_list, stat_list = _dy_gate_bwd(dzb, w_out_g, hug, mixpre, pool_scale, lse_all)
    g_w_out = _grad_w_out(y, dzb)
    dh, g_w_pool, scale_part = _pool_bwd(dh, dpo, mixpre, pooled, w_pool_g, pool_scale)
    small = jnp.concatenate([scale_part, gain_part, bias_part, loss_part], axis=1)
    early = [g_w_out.reshape(N_SHARDS, 2, D_MODEL // (2 * N_SHARDS), D_MODEL), _pool_grad_pieces(g_w_pool)]

    bwd = lambda p, comm: _attn_bwd(qkv[p], do_list[p], stat_list[p], "attn_bwd_d%d" % DILATIONS[p], comm)
    part_a, recv = bwd(0, _exchange_halves(early))
    sums = [_add_own_half(g, r, core, "add_own_half_%d" % a) for a, (g, r) in enumerate(zip(early, recv))]
    part_b, recv = bwd(1, _scatter_to_chips([s[1] for s in sums]))
    bufs = [_add_chips(s[0], r, chip_core, "add_chips_%d" % a) for a, (s, r) in enumerate(zip(sums, recv))]
    part_c, reduced = bwd(2, _share_with_sibling(bufs))
    parts = [part_a, part_b, part_c]
    dh, gathered = _sum_patterns(dh, [t[0] for t in parts], tabs, True, 0, "sum_dq", _gather_small(small))
    dh, _ = _sum_patterns(dh, [t[1] for t in parts], tabs, True, 1, "sum_dk")
    dh, _ = _sum_patterns(dh, [t[2] for t in parts], tabs, False, 2, "sum_dv")

    give, _ = _grad_w_in(x, dh, 1 - core, "grad_w_in_give")
    keep, recv = _grad_w_in(x, dh, core, "grad_w_in_keep", _send_to_sibling([give]))
    total, total_b = _add_pair(keep, recv[0], "add_own_half_w_in")
    n_tiles = seq // GRAD_X_PARTIAL_ROWS
    tiles = max(n_tiles // 4, 1)
    part, relayed = _grad_x_partial(dh, w_in_g, 0, tiles, None, _relay_diagonal(total_b))
    total_b = _fold_relayed(total, total_b, relayed[0], onward)
    part, recv = _grad_x_partial(dh, w_in_g, tiles, n_tiles - tiles, part, _scatter_to_neighbours(total_b))
    buf = _add_chips(total, recv[0], chip_core, "add_chips_w_in")
    g_x = _grad_x_final(dh, w_in_g, dz, part)
    g_w_in = _run_exchange(_share_with_sibling([buf]), "share_w_in")[0]
    return g_x, g_w_in, reduced[0], reduced[1], small, gathered[0]


def _exchange_halves(grads):
    n = len(grads)

    def copies(src, dst, sems):
        x, y, c, _ = _mesh_place()
        return [_remote(src[a].at[j, 1 - c], dst[a].at[j], sems[0].at[a, j], sems[1].at[a, j], (x, y, 1 - c))
                for a in range(n) for j in range(N_SHARDS)]

    def start(src, dst, sems):
        for cp in copies(src, dst, sems):
            cp.start()

    def finish(src, dst, sems):
        for cp in copies(src, dst, sems):
            cp.wait()

    return _Exchange(grads, [jax.ShapeDtypeStruct((N_SHARDS,) + g.shape[2:], g.dtype) for g in grads], {},
                     [pltpu.SemaphoreType.DMA((n, N_SHARDS))] * 2, start, finish)


def _add_own_half(grad, recv, core, name):
    _, _, r, c = grad.shape
    tr = min(r, 256)

    def body(core_ref, g_ref, r_ref, o_ref, ob_ref):
        tot = g_ref[...] + r_ref[...]
        o_ref[...] = tot
        ob_ref[...] = tot.astype(BF16)

    out = pl.BlockSpec((None, tr, c), lambda j, i, core_ref: (j, i, 0))
    return _pallas(
        body, name=name,
        grid_spec=pltpu.PrefetchScalarGridSpec(
            num_scalar_prefetch=1, grid=(N_SHARDS, r // tr),
            in_specs=[pl.BlockSpec((None, None, tr, c), lambda j, i, core_ref: (j, core_ref[0], i, 0)),
                      pl.BlockSpec((None, tr, c), lambda j, i, core_ref: (j, i, 0))],
            out_specs=[out, out]),
        out_shape=[jax.ShapeDtypeStruct((N_SHARDS, r, c), F32), jax.ShapeDtypeStruct((N_SHARDS, r, c), BF16)],
        compiler_params=_params(("parallel", "parallel"), 32),
    )(core, grad, recv)


def _send_to_sibling(arrays):
    n = len(arrays)

    def copies(src, dst, sems):
        x, y, c, _ = _mesh_place()
        return [_remote(src[a], dst[a], sems[0].at[a], sems[1].at[a], (x, y, 1 - c)) for a in range(n)]

    def start(src, dst, sems):
        for cp in copies(src, dst, sems):
            cp.start()

    def finish(src, dst, sems):
        for cp in copies(src, dst, sems):
            cp.wait()

    return _Exchange(arrays, [jax.ShapeDtypeStruct(t.shape, t.dtype) for t in arrays], {},
                     [pltpu.SemaphoreType.DMA((n,))] * 2, start, finish)


def _add_pair(a, b, name):
    _, r, c = a.shape
    tr = min(r, 256)

    def body(a_ref, b_ref, o_ref, ob_ref):
        tot = a_ref[...] + b_ref[...]
        o_ref[...] = tot
        ob_ref[...] = tot.astype(BF16)

    spec = pl.BlockSpec((None, tr, c), lambda j, i: (j, i, 0))
    return _pallas(
        body, name=name, grid=(N_SHARDS, r // tr), in_specs=[spec, spec], out_specs=[spec, spec],
        out_shape=[jax.ShapeDtypeStruct(a.shape, F32), jax.ShapeDtypeStruct(a.shape, BF16)],
        compiler_params=_params(("parallel", "parallel"), 32),
    )(a, b)


def _scatter_to_chips(sums):
    n = len(sums)

    def copies(src, dst, sems):
        x, y, c, chips = _mesh_place()
        return [_remote(src[a].at[2 * cx + cy], dst[a].at[k], sems[0].at[a, k], sems[1].at[a, k], (cx, cy, c))
                for a in range(n) for k, (cx, cy) in enumerate(chips)]

    def start(src, dst, sems):
        for cp in copies(src, dst, sems):
            cp.start()

    def finish(src, dst, sems):
        for cp in copies(src, dst, sems):
            cp.wait()

    return _Exchange(sums, [jax.ShapeDtypeStruct((3,) + s.shape[1:], s.dtype) for s in sums], {},
                     [pltpu.SemaphoreType.DMA((n, 3))] * 2, start, finish)


def _add_chips(sums, recv, chip_core, name):
    _, r, c = sums.shape
    n_recv = recv.shape[0]
    tr = min(r, 256)

    def body(cc_ref, s_ref, r_ref, o_ref):
        tot = s_ref[...]
        for k in range(n_recv):
            tot = tot + r_ref[k].astype(F32)
        o_ref[...] = tot

    return _pallas(
        body, name=name,
        grid_spec=pltpu.PrefetchScalarGridSpec(
            num_scalar_prefetch=1, grid=(r // tr,),
            in_specs=[pl.BlockSpec((None, tr, c), lambda i, cc_ref: (cc_ref[0], i, 0)),
                      pl.BlockSpec((n_recv, tr, c), lambda i, cc_ref: (0, i, 0))],
            out_specs=pl.BlockSpec((None, tr, c), lambda i, cc_ref: (cc_ref[1], i, 0))),
        out_shape=jax.ShapeDtypeStruct((2, r, c), F32),
        compiler_params=_params(("parallel",), 32),
    )(chip_core, sums, recv)


def _relay_diagonal(sums_b):
    def copy(src, dst, sems):
        x, y, c, _ = _mesh_place()
        diagonal = 2 * (1 - x) + (1 - y)
        return _remote(src[0].at[diagonal], dst[0], sems[0].at[0], sems[1].at[0], (x ^ (1 - c), y ^ c, c))

    def start(src, dst, sems):
        copy(src, dst, sems).start()

    def finish(src, dst, sems):
        copy(src, dst, sems).wait()

    return _Exchange([sums_b], [jax.ShapeDtypeStruct(sums_b.shape[1:], sums_b.dtype)], {},
                     [pltpu.SemaphoreType.DMA((1,))] * 2, start, finish)


def _fold_relayed(sums, sums_b, relayed, onward):
    _, r, c = sums.shape
    tr = min(r, 256)

    def body(on_ref, b_in_ref, s_ref, r_ref, o_ref):
        o_ref[...] = (s_ref[...] + r_ref[...].astype(F32)).astype(BF16)

    return _pallas(
        body, name="fold_relayed",
        grid_spec=pltpu.PrefetchScalarGridSpec(
            num_scalar_prefetch=1, grid=(r // tr,),
            in_specs=[ANY, pl.BlockSpec((None, tr, c), lambda i, on_ref: (on_ref[0], i, 0)),
                      pl.BlockSpec((tr, c), lambda i, on_ref: (i, 0))],
            out_specs=pl.BlockSpec((None, tr, c), lambda i, on_ref: (on_ref[0], i, 0))),
        out_shape=jax.ShapeDtypeStruct(sums_b.shape, sums_b.dtype),
        input_output_aliases={1: 0},
        compiler_params=_params(("parallel",), 32),
    )(onward, sums_b, sums, relayed)


def _scatter_to_neighbours(sums_b):
    def copies(src, dst, sems):
        x, y, c, chips = _mesh_place()
        return [_remote(src[0].at[2 * cx + cy], dst[0].at[k], sems[0].at[k], sems[1].at[k], (cx, cy, c))
                for k, (cx, cy) in enumerate(chips[:2])]

    def start(src, dst, sems):
        for cp in copies(src, dst, sems):
            cp.start()

    def finish(src, dst, sems):
        for cp in copies(src, dst, sems):
            cp.wait()

    return _Exchange([sums_b], [jax.ShapeDtypeStruct((2,) + sums_b.shape[1:], sums_b.dtype)], {},
                     [pltpu.SemaphoreType.DMA((2,))] * 2, start, finish)


def _share_with_sibling(bufs):
    n = len(bufs)

    def copies(dst, sems, half):
        x, y, c, _ = _mesh_place()
        h = c if half == "mine" else 1 - c
        return [_remote(dst[a].at[h], dst[a].at[h], sems[0].at[a], sems[1].at[a], (x, y, 1 - c)) for a in range(n)]

    def start(ins, dst, sems):
        for cp in copies(dst, sems, "mine"):
            cp.start()

    def finish(ins, dst, sems):
        for cp in copies(dst, sems, "theirs"):
            cp.wait_recv()
        for cp in copies(dst, sems, "mine"):
            cp.wait_send()

    return _Exchange(bufs, [jax.ShapeDtypeStruct(b.shape, b.dtype) for b in bufs], {a: a for a in range(n)},
                     [pltpu.SemaphoreType.DMA((n,))] * 2, start, finish)


def _adam_math(w, g, m, v):
    m = ADAM_B1 * m + (1.0 - ADAM_B1) * g
    v = ADAM_B2 * v + (1.0 - ADAM_B2) * (g * g)
    m_hat = m / (1.0 - ADAM_B1 ** ADAM_STEP)
    v_hat = v / (1.0 - ADAM_B2 ** ADAM_STEP)
    delta = -ADAM_LR * (m_hat / (jnp.sqrt(v_hat) + ADAM_EPS) + ADAM_WD * w)
    return delta, m, v


def _gather_small(small):
    def peers():
        x, y, c, _ = _mesh_place()
        return [(x ^ ((r >> 2) & 1), y ^ ((r >> 1) & 1), c ^ (r & 1)) for r in range(1, 8)], 4 * x + 2 * y + c

    def start(src, dst, sems):
        to, me = peers()
        for r, peer in enumerate(to):
            _remote(src[0], dst[0].at[me], sems[0].at[r], sems[1].at[r], peer).start()

    def finish(src, dst, sems):
        to, me = peers()
        for r, (px, py, pc) in enumerate(to):
            theirs = dst[0].at[4 * px + 2 * py + pc]
            _remote(theirs, theirs, sems[0].at[r], sems[1].at[r], (px, py, pc)).wait_recv()
        for r, peer in enumerate(to):
            _remote(src[0], dst[0].at[me], sems[0].at[r], sems[1].at[r], peer).wait_send()

    return _Exchange([small], [jax.ShapeDtypeStruct((8,) + small.shape, small.dtype)], {},
                     [pltpu.SemaphoreType.DMA((7,))] * 2, start, finish)


def _small_adamw(gathered, small, me, w_vec, m_vec, v_vec):
    n_par = w_vec.shape[1]

    def body(me_ref, a_ref, s_ref, w_ref, m_ref, v_ref, loss_ref, g_ref, d_ref, nm_ref, nv_ref):
        mine = s_ref[...]
        tot = jnp.where(me_ref[0] == 0, mine, a_ref[0])
        for d in range(1, 8):
            tot = tot + jnp.where(me_ref[0] == d, mine, a_ref[d])
        tot = jnp.sum(tot, axis=0, keepdims=True)
        sq = jnp.sum(tot[:, n_par:], axis=1, keepdims=True)
        loss_ref[...] = jnp.broadcast_to(sq * (0.5 / D_MODEL), loss_ref.shape)
        g = tot[:, :n_par]
        g_ref[...] = g
        d_ref[...], nm_ref[...], nv_ref[...] = _adam_math(w_ref[...], g, m_ref[...], v_ref[...])

    vm = pl.BlockSpec(memory_space=pltpu.VMEM)
    vec = jax.ShapeDtypeStruct((1, n_par), F32)
    return pl.pallas_call(
        body, name="small_adamw",
        grid_spec=pltpu.PrefetchScalarGridSpec(num_scalar_prefetch=1, grid=(), in_specs=[vm] * 5, out_specs=[vm] * 5),
        out_shape=[jax.ShapeDtypeStruct((1, 128), F32), vec, vec, vec, vec],
    )(me, gathered, small, w_vec, m_vec, v_vec)


def _adamw(w, g, m, v, name):
    r, c = w.shape
    tr = min(r, 256)

    def body(w_ref, g_ref, m_ref, v_ref, go_ref, d_ref, nm_ref, nv_ref):
        g = g_ref[...]
        go_ref[...] = g
        d_ref[...], nm_ref[...], nv_ref[...] = _adam_math(w_ref[...], g, m_ref[...], v_ref[...])

    spec = pl.BlockSpec((tr, c), lambda i: (i, 0))
    shape = jax.ShapeDtypeStruct((r, c), F32)
    return _pallas(
        body, name=name, grid=(r // tr,),
        in_specs=[spec] * 4, out_specs=[spec] * 4, out_shape=[shape] * 4,
        compiler_params=_params(("parallel",), 48),
    )(w, g, m, v)


def kernel(x, w_in, w_pool, pool_scale, w_out, ln_gain, ln_bias, loss_target, m_w_in, m_w_pool, m_pool_scale, m_w_out, m_ln_gain, m_ln_bias, v_w_in, v_w_pool, v_pool_scale, v_w_out, v_ln_gain, v_ln_bias):
    xi, yi, ci = lax.axis_index("x"), lax.axis_index("y"), lax.axis_index("c")
    chip = (2 * xi + yi).astype(jnp.int32).reshape(1)
    core = ci.astype(jnp.int32).reshape(1)
    n_groups = len(POOL_WINDOWS)
    shard_c = w_pool.shape[2]

    w_in_b = _cast_bf16(w_in[0], chip, "cast_w_in", 256)
    w_out_b = _cast_bf16(w_out[0], chip, "cast_w_out", 256)
    w_pool_b = _cast_bf16(w_pool[0].reshape(n_groups * shard_c, POOL_GROUP_DIM), chip, "cast_w_pool", 256)

    chip_core = jnp.concatenate([chip, core])
    onward = (2 * (xi ^ ci) + (yi ^ (1 - ci))).astype(jnp.int32).reshape(1)
    g_x, full_in, full_out, full_pool, small, small_all = _step(
        x[0], loss_target[0], [w_in_b, w_out_b, w_pool_b], pool_scale, ln_gain, ln_bias,
        (core, chip_core, onward, _in_proj_plan(xi, yi)))
    half_c = shard_c // 2
    grad_w_in = full_in.reshape(D_MODEL, SHARD_IN)
    grad_w_out = full_out.reshape(D_MODEL // N_SHARDS, D_MODEL)
    grad_w_pool = (full_pool.reshape(2, n_groups, half_c, POOL_GROUP_DIM).transpose(1, 0, 2, 3)
                   .reshape(n_groups * shard_c, POOL_GROUP_DIM))

    grad_w_in, d_in, nm_in, nv_in = _adamw(w_in[0], grad_w_in, m_w_in[0], v_w_in[0], "adamw_w_in")
    grad_w_out, d_out, nm_out, nv_out = _adamw(w_out[0], grad_w_out, m_w_out[0], v_w_out[0], "adamw_w_out")
    flat = lambda t: t[0].reshape(n_groups * shard_c, POOL_GROUP_DIM)
    grad_w_pool, d_pool, nm_pool, nv_pool = _adamw(flat(w_pool), grad_w_pool, flat(m_w_pool), flat(v_w_pool),
                                                   "adamw_w_pool")

    cat = lambda a, b, c: jnp.concatenate([a, b, c], axis=1)
    me = (4 * xi + 2 * yi + ci).astype(jnp.int32).reshape(1)
    loss_v, g_vec, d_vec, nm_vec, nv_vec = _small_adamw(
        small_all, small, me, cat(pool_scale, ln_gain, ln_bias), cat(m_pool_scale, m_ln_gain, m_ln_bias),
        cat(v_pool_scale, v_ln_gain, v_ln_bias))

    def split(vec):
        return vec[:, :D_POOL], vec[:, D_POOL:D_POOL + D_MODEL], vec[:, D_POOL + D_MODEL:]

    g_scale, g_gain, g_bias = split(g_vec)
    d_scale, d_gain, d_bias = split(d_vec)
    nm_scale, nm_gain, nm_bias = split(nm_vec)
    nv_scale, nv_gain, nv_bias = split(nv_vec)
    pool_shape = w_pool.shape
    return (loss_v[0, 0], g_x[None],
            grad_w_in[None], grad_w_pool.reshape(pool_shape), g_scale, grad_w_out[None], g_gain, g_bias,
            d_in[None], d_pool.reshape(pool_shape), d_scale, d_out[None], d_gain, d_bias,
            nm_in[None], nm_pool.reshape(pool_shape), nm_scale, nm_out[None], nm_gain, nm_bias,
            nv_in[None], nv_pool.reshape(pool_shape), nv_scale, nv_out[None], nv_gain, nv_bias)
```

```python
import functools

import jax
import jax.numpy as jnp
from jax import lax
from jax.experimental import pallas as pl
from jax.experimental.pallas import tpu as pltpu

F32 = jnp.float32
BF16 = jnp.bfloat16
MESH = pl.DeviceIdType.MESH
ANY = pl.BlockSpec(memory_space=pl.ANY)

D_MODEL = 2048
D_ATTN = 1024
D_POOL = 1024
HEAD_DIM = 128
N_HEADS = 8
ROPE_DIM = 32
ROPE_THETA = 500000.0
DILATIONS = (1, 4, 16)
KEY_BLOCK = 128
CHUNK = 2 * KEY_BLOCK
STAT_LANES = 128
POOL_WINDOWS = (2, 4, 8, 16)
POOL_GROUP_DIM = 256
POOL_HALO = 16
D_QKV = 3 * D_ATTN
D_UG = D_POOL + D_MODEL
D_IN = D_QKV + D_UG
N_SHARDS = 4
SHARD_IN = D_IN // N_SHARDS
LN_EPS = 1e-5
DEEPNORM_ALPHA = 2.0 ** 0.25
ADAM_LR = 0.001
ADAM_B1 = 0.9
ADAM_B2 = 0.999
ADAM_EPS = 1e-08
ADAM_WD = 0.01
ADAM_STEP = 10
NEG = -1e30
MIB = 1024 * 1024


def _params(sem, vmem_mib):
    return pltpu.CompilerParams(dimension_semantics=sem, vmem_limit_bytes=vmem_mib * MIB)


def _pallas(body, **kwargs):
    pin = lambda s: pltpu.HBM(s.shape, s.dtype) if len(s.shape) >= 2 else s
    out_shape = kwargs.pop("out_shape")
    out_shape = [pin(s) for s in out_shape] if isinstance(out_shape, (list, tuple)) else pin(out_shape)
    call = pl.pallas_call(body, out_shape=out_shape, **kwargs)

    def run(*operands):
        return call(*[pltpu.with_memory_space_constraint(o, pltpu.HBM) if o.ndim >= 2 else o for o in operands])

    return run


class _Exchange:
    def __init__(self, operands, out_shape, aliases, sems, start, finish):
        self.operands, self.out_shape, self.aliases, self.sems = list(operands), list(out_shape), dict(aliases), list(sems)
        self.start, self.finish = start, finish


def _run_exchange(comm, name):
    n_in, n_out = len(comm.operands), len(comm.out_shape)

    def body(*refs):
        ins, outs, sems = refs[:n_in], refs[n_in:n_in + n_out], refs[n_in + n_out:]
        comm.start(ins, outs, sems)
        comm.finish(ins, outs, sems)

    return _pallas(
        body, name=name, in_specs=[ANY] * n_in, out_specs=[ANY] * n_out, out_shape=comm.out_shape,
        input_output_aliases=comm.aliases, scratch_shapes=comm.sems,
    )(*comm.operands)


def _call(body, *, name, grid, in_specs, out_specs, out_shape, scratch_shapes, semantics, vmem_mib, args,
          aliases=None, comm=None, prefetch=()):
    aliases = dict(aliases or {})
    n_pre, n_in, n_out, n_scr = len(prefetch), len(in_specs), len(out_specs), len(scratch_shapes)
    c_in, c_out = (len(comm.operands), len(comm.out_shape)) if comm else (0, 0)
    c_shapes, c_sems, c_operands = (comm.out_shape, comm.sems, comm.operands) if comm else ([], [], [])

    def hosted(*refs):
        pre, refs = refs[:n_pre], refs[n_pre:]
        a = n_in
        b = a + c_in
        c = b + n_out
        d = c + c_out
        e = d + n_scr
        if comm is None:
            body(*pre, *refs)
            return
        ids = [pl.program_id(k) for k in range(len(grid))]
        first = functools.reduce(jnp.logical_and, [i == 0 for i in ids])
        last = functools.reduce(jnp.logical_and, [i == g - 1 for i, g in zip(ids, grid)])

        @pl.when(first)
        def _():
            comm.start(refs[a:b], refs[c:d], refs[e:])

        body(*pre, *refs[:a], *refs[b:c], *refs[d:e])

        @pl.when(last)
        def _():
            comm.finish(refs[a:b], refs[c:d], refs[e:])

    if comm:
        semantics = ("arbitrary",) * len(grid)
        for i, o in comm.aliases.items():
            aliases[n_pre + n_in + i] = n_out + o
    outs = _pallas(
        hosted, name=name,
        grid_spec=pltpu.PrefetchScalarGridSpec(
            num_scalar_prefetch=n_pre, grid=grid, in_specs=list(in_specs) + [ANY] * c_in,
            out_specs=list(out_specs) + [ANY] * c_out, scratch_shapes=list(scratch_shapes) + c_sems),
        out_shape=list(out_shape) + c_shapes, input_output_aliases=aliases,
        compiler_params=_params(semantics, vmem_mib),
    )(*prefetch, *args, *c_operands)
    return list(outs[:n_out]), list(outs[n_out:])


def _dot_nn(a, b):
    return jnp.dot(a, b, preferred_element_type=F32)


def _dot_nt(a, b):
    return lax.dot_general(a, b, (((1,), (1,)), ((), ())), preferred_element_type=F32)


def _dot_tn(a, b):
    return lax.dot_general(a, b, (((0,), (0,)), ((), ())), preferred_element_type=F32)


def _fold_rows(a):
    r, c = a.shape
    return jnp.sum(a.reshape(r // 8, 8, c), axis=0)


def _cast_bf16(a, chip, name, rows):
    r, c = a.shape

    def body(chip_ref, a_ref, o_ref):
        o_ref[...] = a_ref[...].astype(BF16)

    return _pallas(
        body, name=name,
        grid_spec=pltpu.PrefetchScalarGridSpec(
            num_scalar_prefetch=1, grid=(r // rows,),
            in_specs=[pl.BlockSpec((rows, c), lambda i, chip_ref: (i, 0))],
            out_specs=pl.BlockSpec((None, rows, c), lambda i, chip_ref: (chip_ref[0], i, 0))),
        out_shape=jax.ShapeDtypeStruct((N_SHARDS, r, c), BF16),
        compiler_params=_params(("parallel",), 32),
    )(chip, a)


def _mesh_place():
    x, y, c = lax.axis_index("x"), lax.axis_index("y"), lax.axis_index("c")
    return x, y, c, [(1 - x, y), (x, 1 - y), (1 - x, 1 - y)]


def _remote(src, dst, send_sem, recv_sem, to):
    return pltpu.make_async_remote_copy(src_ref=src, dst_ref=dst, send_sem=send_sem, recv_sem=recv_sem,
                                        device_id=to, device_id_type=MESH)


def _rope_tables(seq):
    half = ROPE_DIM // 2
    inv_freq = ROPE_THETA ** (-(2.0 * jnp.arange(half, dtype=F32)) / ROPE_DIM)
    ang = jnp.arange(seq, dtype=jnp.int32).astype(F32)[:, None] * inv_freq[None, :]
    cos, sin = jnp.cos(ang), jnp.sin(ang)
    pad = jnp.zeros((seq, HEAD_DIM - ROPE_DIM), F32)
    zeros = jnp.zeros((seq, half), F32)
    c_tab = jnp.concatenate([cos, cos, pad + 1.0], axis=1)
    up_tab = jnp.concatenate([-sin, zeros, pad], axis=1)
    down_tab = jnp.concatenate([zeros, sin, pad], axis=1)
    return c_tab, up_tab, down_tab


def _rotate_heads(t, c_tab, up_tab, down_tab):
    outs = []
    for h in range(t.shape[1] // HEAD_DIM):
        th = t[:, h * HEAD_DIM:(h + 1) * HEAD_DIM]
        up = pltpu.roll(th, HEAD_DIM - ROPE_DIM // 2, axis=1)
        down = pltpu.roll(th, ROPE_DIM // 2, axis=1)
        outs.append(th * c_tab + up * up_tab + down * down_tab)
    return outs[0] if len(outs) == 1 else jnp.concatenate(outs, axis=1)


def _to_pattern(slabs_ref, dst_ref, dil, dtype):
    n_slabs, rows, _ = slabs_ref.shape
    for s in range(n_slabs):
        for r in range(dil):
            dst_ref[r, :, s * 128:(s + 1) * 128] = slabs_ref[s, pl.ds(r, rows // dil, dil), :].astype(dtype)


def _from_pattern(src_ref, slabs_ref, dil):
    n_slabs, rows, _ = slabs_ref.shape
    for s in range(n_slabs):
        for r in range(dil):
            slabs_ref[s, pl.ds(r, rows // dil, dil), :] = src_ref[r, :, s * 128:(s + 1) * 128].astype(F32)


def _store_slabs(slabs_ref, value):
    for s in range(slabs_ref.shape[0]):
        slabs_ref[s] = value[:, s * 128:(s + 1) * 128]


W_IN_CHUNKS = 4


def _in_proj_plan(x, y):
    shards = [2 * x + y, 2 * (1 - x) + y, 2 * x + (1 - y), 2 * (1 - x) + (1 - y)]
    last_row = jnp.int32(-2)

    def table(active, col_of):
        cols, rows = [], []
        first_col = functools.reduce(lambda acc, j: jnp.where(active[j], col_of(shards[j]), acc), reversed(range(4)),
                                     jnp.int32(0))
        held_col, seen = first_col, jnp.bool_(False)
        for j in range(4):
            cols.append(jnp.where(active[j], col_of(shards[j]), held_col))
            rows.append(jnp.where(active[j], -1, jnp.where(seen, last_row, 0)))
            held_col = jnp.where(active[j], col_of(shards[j]), held_col)
            seen = jnp.logical_or(seen, active[j])
        return cols, rows

    q_cols, q_rows = table([s < 2 for s in shards], lambda s: s)
    h_cols, h_rows = table([s >= 2 for s in shards], lambda s: s - 2)
    return jnp.stack([jnp.asarray(v, jnp.int32) for v in shards + q_cols + q_rows + h_cols + h_rows])


def _in_proj_gathering(x, w_bufs, tabs, plan):
    seq = x.shape[0]
    tm, tn = 512, SHARD_IN
    n_tiles = seq // tm
    heads = tn // HEAD_DIM
    k_heads_in_second = 2 * D_ATTN // HEAD_DIM - heads
    d4, d16 = DILATIONS[1], DILATIONS[2]
    DIAGONAL = 2
    chunk = D_MODEL // 2 // W_IN_CHUNKS
    early = [(0, D_MODEL // 2, q * chunk, chunk) for q in range(W_IN_CHUNKS)]
    late = [(a, w_bufs[a].shape[1] // 2, 0, w_bufs[a].shape[1] // 2) for a in (1, 2)]
    pieces = early + late
    early_ids, late_ids = range(len(early)), range(len(early), len(pieces))

    def body(plan_ref, x_ref, w_in_in, w_out_in, w_pool_in, c_ref, up_ref, down_ref,
             o1_ref, o4_ref, o16_ref, hug_ref, w_ref, w_out_ref, w_pool_ref,
             wbuf_ref, res_ref, w_sem, ici_send, ici_recv, d2d_send, d2d_recv):
        j, i = pl.program_id(0), pl.program_id(1)
        mx, my, mc, chips = _mesh_place()
        sibling = (mx, my, 1 - mc)
        gathered = (w_ref, w_out_ref, w_pool_ref)
        chip_of = lambda k: 2 * chips[k][0] + chips[k][1]

        def piece(n, chip, core):
            a, per_core, offset, size = pieces[n]
            return gathered[a].at[chip, pl.ds(core * per_core + offset, size)]

        def to_neighbour(k, n):
            mine = piece(n, 2 * mx + my, mc)
            return _remote(mine, mine, ici_send.at[n, k], ici_recv.at[n, k], (*chips[k], mc))

        def relay(n):
            theirs = piece(n, 2 * (mx ^ (1 - mc)) + (my ^ mc), mc)
            return _remote(theirs, theirs, ici_send.at[n, DIAGONAL], ici_recv.at[n, DIAGONAL], (mx ^ mc, my ^ (1 - mc), mc))

        def arrival(k, n):
            theirs = piece(n, chip_of(k), mc)
            return _remote(theirs, theirs, ici_send.at[n, k], ici_recv.at[n, k], (*chips[k], mc))

        def to_sibling(k, n, core):
            theirs = piece(n, chip_of(k), core)
            return _remote(theirs, theirs, d2d_send.at[n, k], d2d_recv.at[n, k], sibling)

        def take(k, ids):
            for n in ids:
                arrival(k, n).wait_recv()
                to_sibling(k, n, mc).start()

        def taken(k, ids):
            for n in ids:
                to_sibling(k, n, 1 - mc).wait_recv()

        first_tile = i == 0

        @pl.when(jnp.logical_and(j == 0, first_tile))
        def _():
            for n in range(len(pieces)):
                for k in range(DIAGONAL):
                    to_neighbour(k, n).start()

        @pl.when(jnp.logical_and(j == 1, first_tile))
        def _():
            take(0, early_ids)
            taken(0, early_ids)

        @pl.when(jnp.logical_and(j == 2, first_tile))
        def _():
            take(1, early_ids)
            for n in early_ids:
                relay(n).start()
            taken(1, early_ids)
            for k in range(DIAGONAL):
                take(k, late_ids)
            for n in late_ids:
                relay(n).start()
            for k in range(DIAGONAL):
                taken(k, late_ids)

        @pl.when(jnp.logical_and(j == 3, first_tile))
        def _():
            take(DIAGONAL, range(len(pieces)))
            taken(DIAGONAL, range(len(pieces)))

        shard = plan_ref[j]

        @pl.when(first_tile)
        def _():
            cp = pltpu.make_async_copy(w_ref.at[shard], wbuf_ref, w_sem)
            cp.start()
            cp.wait()

        xb = x_ref[...].astype(BF16)
        group = 4 * HEAD_DIM
        accs = [_dot_nn(xb, wbuf_ref[:, g * group:(g + 1) * group]) for g in range(tn // group)]

        def emit_qkv(rotated_heads):
            for h in range(heads):
                lanes = (h * HEAD_DIM) % group
                th = accs[h * HEAD_DIM // group][:, lanes:lanes + HEAD_DIM]
                if h < rotated_heads:
                    th = _rotate_heads(th, c_ref[...], up_ref[...], down_ref[...])
                res_ref[h] = th
                o1_ref[:, h * HEAD_DIM:(h + 1) * HEAD_DIM] = th.astype(BF16)
            _to_pattern(res_ref, o4_ref, d4, BF16)
            _to_pattern(res_ref, o16_ref, d16, BF16)

        @pl.when(shard == 0)
        def _():
            emit_qkv(heads)

        @pl.when(shard == 1)
        def _():
            emit_qkv(k_heads_in_second)

        @pl.when(shard >= 2)
        def _():
            for g, acc in enumerate(accs):
                hug_ref[:, g * group:(g + 1) * group] = acc.astype(BF16)

        @pl.when(jnp.logical_and(j == 3, i == n_tiles - 1))
        def _():
            for n in range(len(pieces)):
                for k in range(DIAGONAL):
                    to_neighbour(k, n).wait_send()
                relay(n).wait_send()
                for k in range(DIAGONAL + 1):
                    to_sibling(k, n, mc).wait_send()

    def held(base, last):
        return lambda j, i, plan_ref: jnp.where(plan_ref[base + j] == -1, i,
                                                jnp.where(plan_ref[base + j] == -2, last, 0))

    q_row, h_row = held(8, n_tiles - 1), held(16, n_tiles - 1)
    tab_spec = pl.BlockSpec((tm, HEAD_DIM), lambda j, i, plan_ref: (i, 0))
    sems = [pltpu.SemaphoreType.DMA((len(pieces), 3))] * 4
    o1, o4, o16, hug, w_in_g, w_out_g, w_pool_g = _pallas(
        body, name="in_proj_gathering",
        grid_spec=pltpu.PrefetchScalarGridSpec(
            num_scalar_prefetch=1, grid=(N_SHARDS, n_tiles),
            in_specs=[pl.BlockSpec((tm, D_MODEL), lambda j, i, plan_ref: (i, 0)), ANY, ANY, ANY,
                      tab_spec, tab_spec, tab_spec],
            out_specs=[pl.BlockSpec((tm, tn), lambda j, i, p: (q_row(j, i, p), p[4 + j])),
                       pl.BlockSpec((d4, tm // d4, tn), lambda j, i, p: (0, q_row(j, i, p), p[4 + j])),
                       pl.BlockSpec((d16, tm // d16, tn), lambda j, i, p: (0, q_row(j, i, p), p[4 + j])),
                       pl.BlockSpec((tm, tn), lambda j, i, p: (h_row(j, i, p), p[12 + j])),
                       ANY, ANY, ANY],
            scratch_shapes=[pltpu.VMEM((D_MODEL, tn), BF16), pltpu.VMEM((heads, tm, HEAD_DIM), F32),
                            pltpu.SemaphoreType.DMA(())] + sems),
        out_shape=[jax.ShapeDtypeStruct((seq, D_QKV), BF16),
                   jax.ShapeDtypeStruct((d4, seq // d4, D_QKV), BF16),
                   jax.ShapeDtypeStruct((d16, seq // d16, D_QKV), BF16),
                   jax.ShapeDtypeStruct((seq, D_UG), BF16)]
        + [jax.ShapeDtypeStruct(b.shape, b.dtype) for b in w_bufs],
        input_output_aliases={2: 4, 3: 5, 4: 6},
        compiler_params=_params(("arbitrary", "arbitrary"), 52),
    )(plan, x, *w_bufs, *tabs)
    return [o1[None], o4, o16], hug, w_in_g, w_out_g, w_pool_g


def _band_masks():
    row = lax.broadcasted_iota(jnp.int32, (KEY_BLOCK, KEY_BLOCK), 0)
    col = lax.broadcasted_iota(jnp.int32, (KEY_BLOCK, KEY_BLOCK), 1)
    return col <= row, col >= row


def _attn_fwd(qkv, name):
    dil, n, _ = qkv.shape
    scale = HEAD_DIM ** -0.5
    lo, hi = slice(0, KEY_BLOCK), slice(KEY_BLOCK, CHUNK)

    def body(q_ref, k_ref, v_ref, kb_ref, vb_ref, o_ref, st_ref):
        i = pl.program_id(1)
        cur_mask, prev_mask = _band_masks()
        before_mask = jnp.logical_and(prev_mask, i > 0)
        lane = lax.broadcasted_iota(jnp.int32, (KEY_BLOCK, STAT_LANES), 1)
        tasks = [(rows, h) for rows in (lo, hi) for h in range(N_HEADS)]
        head = lambda h: slice(h * HEAD_DIM, (h + 1) * HEAD_DIM)

        def prev_of(rows, h):
            if rows is lo:
                return kb_ref[:, head(h)], vb_ref[:, head(h)], before_mask
            return k_ref[lo, head(h)], v_ref[lo, head(h)], prev_mask

        scores = []
        for rows, h in tasks:
            q = q_ref[rows, head(h)]
            scores.append((_dot_nt(q, prev_of(rows, h)[0]), _dot_nt(q, k_ref[rows, head(h)])))
        probs = []
        for (rows, h), (qk_prev, qk_cur) in zip(tasks, scores):
            s_prev = jnp.where(prev_of(rows, h)[2], qk_prev * scale, NEG)
            s_cur = jnp.where(cur_mask, qk_cur * scale, NEG)
            m = jnp.max(jnp.maximum(s_prev, s_cur), axis=-1, keepdims=True)
            p_prev = jnp.exp(s_prev - m)
            p_cur = jnp.exp(s_cur - m)
            den = jnp.sum(p_prev + p_cur, axis=-1, keepdims=True)
            probs.append((p_prev.astype(BF16), p_cur.astype(BF16), den, m + jnp.log(den)))
        stats = [jnp.zeros((KEY_BLOCK, STAT_LANES), F32), jnp.zeros((KEY_BLOCK, STAT_LANES), F32)]
        for (rows, h), (p_prev, p_cur, den, lse) in zip(tasks, probs):
            o = _dot_nn(p_cur, v_ref[rows, head(h)]) + _dot_nn(p_prev, prev_of(rows, h)[1])
            o_ref[rows, head(h)] = (o / den).astype(BF16)
            b = 0 if rows is lo else 1
            stats[b] = jnp.where(lane == h, lse, stats[b])
        st_ref[lo, :] = stats[0]
        st_ref[hi, :] = stats[1]

    main = lambda cb: pl.BlockSpec((None, CHUNK, D_ATTN), lambda r, i: (r, i, cb))
    before = lambda cb: pl.BlockSpec((None, KEY_BLOCK, D_ATTN), lambda r, i: (r, jnp.maximum(2 * i - 1, 0), cb))
    return _pallas(
        body, name=name, grid=(dil, n // CHUNK),
        in_specs=[main(0), main(1), main(2), before(1), before(2)],
        out_specs=[main(0), pl.BlockSpec((None, CHUNK, STAT_LANES), lambda r, i: (r, i, 0))],
        out_shape=[jax.ShapeDtypeStruct((dil, n, D_ATTN), BF16), jax.ShapeDtypeStruct((dil, n, STAT_LANES), F32)],
        compiler_params=_params(("parallel", "parallel"), 40),
    )(qkv, qkv, qkv, qkv, qkv)


def _attn_bwd(qkv, do, stats, name, comm=None):
    dil, n, _ = qkv.shape
    n_blocks = n // KEY_BLOCK
    last = n // CHUNK - 1
    scale = HEAD_DIM ** -0.5
    lo, hi = slice(0, KEY_BLOCK), slice(KEY_BLOCK, CHUNK)

    def body(q_ref, k_ref, v_ref, kb_ref, vb_ref, qa_ref, do_ref, doa_ref, st_ref, sta_ref, dq_ref, dk_ref, dv_ref):
        i = pl.program_id(1)
        cur_mask, prev_mask = _band_masks()
        before_mask = jnp.logical_and(prev_mask, i > 0)
        after_mask = jnp.logical_and(prev_mask, i < last)

        rows_cat = lambda a, b: jnp.concatenate([a, b], axis=0)
        masks = (jnp.concatenate([before_mask, cur_mask], axis=1), jnp.concatenate([prev_mask, cur_mask], axis=1),
                 after_mask)

        def operands(h):
            cols = slice(h * HEAD_DIM, (h + 1) * HEAD_DIM)
            lse_c, del_c = slice(h, h + 1), slice(N_HEADS + h, N_HEADS + h + 1)
            q = (q_ref[lo, cols], q_ref[hi, cols], qa_ref[:, cols])
            do = (do_ref[lo, cols], do_ref[hi, cols], doa_ref[:, cols])
            keys = (rows_cat(kb_ref[:, cols], k_ref[lo, cols]), k_ref[:, cols], k_ref[hi, cols])
            vals = (rows_cat(vb_ref[:, cols], v_ref[lo, cols]), v_ref[:, cols], v_ref[hi, cols])
            st = ((st_ref[lo, lse_c], st_ref[lo, del_c]), (st_ref[hi, lse_c], st_ref[hi, del_c]),
                  (sta_ref[:, lse_c], sta_ref[:, del_c]))
            return cols, q, do, keys, vals, st

        group = N_HEADS // 2
        for first_head in range(0, N_HEADS, group):
            heads = range(first_head, first_head + group)
            raw = {}
            for h in heads:
                _, q, do, keys, vals, _ = operands(h)
                raw[h] = [(_dot_nt(q[j], keys[j]), _dot_nt(do[j], vals[j])) for j in range(3)]
            grads = {}
            for h in heads:
                st = operands(h)[5]
                grads[h] = []
                for j in range(3):
                    qk, dp = raw[h][j]
                    lse, delta = st[j]
                    p = jnp.exp(jnp.where(masks[j], qk * scale, NEG) - lse)
                    grads[h].append((p.astype(BF16), (p * (dp - delta) * scale).astype(BF16)))
            for h in heads:
                cols, q, do, keys, _, _ = operands(h)
                (p0, ds0), (p1, ds1), (pa, dsa) = grads[h]
                own, nxt = slice(KEY_BLOCK, CHUNK), slice(0, KEY_BLOCK)

                def put(ref, rows, val, cols=cols):
                    ref[rows, cols] = val.astype(ref.dtype)

                put(dq_ref, lo, _dot_nn(ds0, keys[0]))
                put(dq_ref, hi, _dot_nn(ds1, keys[1]))
                put(dk_ref, lo, _dot_tn(rows_cat(ds0[:, own], ds1[:, nxt]), q_ref[:, cols]))
                put(dk_ref, hi, _dot_tn(rows_cat(ds1[:, own], dsa), rows_cat(q[1], q[2])))
                put(dv_ref, lo, _dot_tn(rows_cat(p0[:, own], p1[:, nxt]), do_ref[:, cols]))
                put(dv_ref, hi, _dot_tn(rows_cat(p1[:, own], pa), rows_cat(do[1], do[2])))

    def spec(rows, width, row_of, cb):
        return pl.BlockSpec((None, rows, width), lambda r, i: (r, row_of(i), cb))

    same = lambda i: i
    before = lambda i: jnp.maximum(2 * i - 1, 0)
    after = lambda i: jnp.minimum(2 * i + 2, n_blocks - 1)
    out = spec(CHUNK, D_ATTN, same, 0)
    return _call(
        body, name=name, grid=(dil, n // CHUNK),
        in_specs=[spec(CHUNK, D_ATTN, same, 0), spec(CHUNK, D_ATTN, same, 1), spec(CHUNK, D_ATTN, same, 2),
                  spec(KEY_BLOCK, D_ATTN, before, 1), spec(KEY_BLOCK, D_ATTN, before, 2),
                  spec(KEY_BLOCK, D_ATTN, after, 0),
                  spec(CHUNK, D_ATTN, same, 0), spec(KEY_BLOCK, D_ATTN, after, 0),
                  spec(CHUNK, STAT_LANES, same, 0), spec(KEY_BLOCK, STAT_LANES, after, 0)],
        out_specs=[out, out, out],
        out_shape=[jax.ShapeDtypeStruct((dil, n, D_ATTN), BF16)] * 3,
        scratch_shapes=[], semantics=("parallel", "parallel"), vmem_mib=40,
        args=(qkv, qkv, qkv, qkv, qkv, qkv, do, do, stats, stats), comm=comm)


def _window_sums(ext, window, backward):
    rows = ext.shape[0]
    acc, span = ext, 1
    while span < window:
        acc = acc + pltpu.roll(acc, (rows - span) if backward else span, axis=0)
        span *= 2
    return acc


def _mix_gate(o_list, st_list, hug, w_pool_g, pool_scale):
    seq = hug.shape[0]
    tm = 256
    halo_blocks = tm // POOL_HALO
    d4, d16 = DILATIONS[1], DILATIONS[2]

    def body(o1_ref, o4_ref, o16_ref, l1_ref, l4_ref, l16_ref, u_ref, halo_ref, ga_ref, gp_ref, wp_ref, sc_ref,
             y_ref, mix_ref, lse_ref, pooled_ref, n4_ref, n16_ref, nl4_ref, nl16_ref):
        i = pl.program_id(0)
        _from_pattern(o4_ref, n4_ref, d4)
        _from_pattern(o16_ref, n16_ref, d16)
        _from_pattern(l4_ref, nl4_ref, d4)
        _from_pattern(l16_ref, nl16_ref, d16)
        la, lb, lc = l1_ref[...], nl4_ref[0], nl16_ref[0]
        mx = jnp.maximum(jnp.maximum(la, lb), lc)
        ea, eb, ec = jnp.exp(la - mx), jnp.exp(lb - mx), jnp.exp(lc - mx)
        tot = ea + eb + ec
        lse_ref[...] = mx + jnp.log(tot)
        wa, wb, wc = ea / tot, eb / tot, ec / tot
        ga = ga_ref[...].astype(F32)
        silu_a = ga * jax.nn.sigmoid(ga)
        for h in range(N_HEADS):
            cols = slice(h * HEAD_DIM, (h + 1) * HEAD_DIM)
            hc = slice(h, h + 1)
            attn = wa[:, hc] * o1_ref[:, cols].astype(F32) + wb[:, hc] * n4_ref[h] + wc[:, hc] * n16_ref[h]
            mix_ref[:, cols] = attn.astype(BF16)
            y_ref[:, cols] = (attn * silu_a[:, cols]).astype(BF16)

        u = u_ref[...].astype(F32)
        halo = jnp.where(i > 0, halo_ref[...].astype(F32), 0.0)
        ext = jnp.concatenate([halo, u], axis=0)
        pos = i * tm + lax.broadcasted_iota(jnp.int32, (tm, 1), 0)
        gp = gp_ref[...].astype(F32)
        gated_scale = sc_ref[...] * (gp * jax.nn.sigmoid(gp))
        for g, window in enumerate(POOL_WINDOWS):
            cols = slice(g * POOL_GROUP_DIM, (g + 1) * POOL_GROUP_DIM)
            sums = _window_sums(ext[:, cols], window, backward=False)[POOL_HALO:, :]
            count = jnp.minimum(pos + 1, window).astype(F32)
            pooled = (sums / count - u[:, cols]).astype(BF16)
            pooled_ref[:, cols] = pooled
            pre = _dot_nn(pooled, wp_ref[g])
            out_cols = slice(D_ATTN + g * POOL_GROUP_DIM, D_ATTN + (g + 1) * POOL_GROUP_DIM)
            mix_ref[:, out_cols] = pre.astype(BF16)
            y_ref[:, out_cols] = (pre * gated_scale[:, cols]).astype(BF16)

    row = lambda width, cb=0: pl.BlockSpec((tm, width), lambda i: (i, cb))
    pat = lambda d, width: pl.BlockSpec((d, tm // d, width), lambda i: (0, i, 0))
    return _pallas(
        body, name="mix_gate", grid=(seq // tm,),
        in_specs=[row(D_ATTN), pat(d4, D_ATTN), pat(d16, D_ATTN),
                  row(STAT_LANES), pat(d4, STAT_LANES), pat(d16, STAT_LANES),
                  row(D_POOL),
                  pl.BlockSpec((POOL_HALO, D_POOL), lambda i: (jnp.maximum(i * halo_blocks - 1, 0), 0)),
                  row(D_ATTN, 1), row(D_POOL, 2),
                  pl.BlockSpec((len(POOL_WINDOWS), POOL_GROUP_DIM, POOL_GROUP_DIM), lambda i: (0, 0, 0)),
                  pl.BlockSpec((1, D_POOL), lambda i: (0, 0))],
        out_specs=[row(D_MODEL), row(D_MODEL), row(STAT_LANES), row(D_POOL)],
        out_shape=[jax.ShapeDtypeStruct((seq, D_MODEL), BF16), jax.ShapeDtypeStruct((seq, D_MODEL), BF16),
                   jax.ShapeDtypeStruct((seq, STAT_LANES), F32), jax.ShapeDtypeStruct((seq, D_POOL), BF16)],
        scratch_shapes=[pltpu.VMEM((N_HEADS, tm, HEAD_DIM), F32), pltpu.VMEM((N_HEADS, tm, HEAD_DIM), F32),
                        pltpu.VMEM((1, tm, STAT_LANES), F32), pltpu.VMEM((1, tm, STAT_LANES), F32)],
        compiler_params=_params(("parallel",), 48),
    )(o_list[0][0], o_list[1], o_list[2], st_list[0][0], st_list[1], st_list[2],
      hug, hug, hug, hug, w_pool_g, pool_scale)


def _out_proj_loss(y, w_out_g, x, target, gain, bias):
    seq = x.shape[0]
    tm = 512

    def body(y_ref, w_ref, x_ref, t_ref, g_ref, b_ref, dz_ref, dzb_ref, gg_ref, gb_ref, loss_ref):
        @pl.when(pl.program_id(0) == 0)
        def _():
            gg_ref[...] = jnp.zeros_like(gg_ref)
            gb_ref[...] = jnp.zeros_like(gb_ref)
            loss_ref[...] = jnp.zeros_like(loss_ref)

        halves = [slice(0, tm // 2), slice(tm // 2, tm)]
        projected = [_dot_nn(y_ref[rows, :], w_ref[...]) for rows in halves]
        for rows, out in zip(halves, projected):
            z = DEEPNORM_ALPHA * x_ref[rows, :] + out
            mu = jnp.mean(z, axis=-1, keepdims=True)
            zc = z - mu
            rstd = lax.rsqrt(jnp.mean(zc * zc, axis=-1, keepdims=True) + LN_EPS)
            xhat = zc * rstd
            gain_v = g_ref[...]
            diff = xhat * gain_v + b_ref[...] - t_ref[rows, :]
            sq = _fold_rows(diff * diff)
            part = sq[:, :128]
            for k in range(1, D_MODEL // 128):
                part = part + sq[:, k * 128:(k + 1) * 128]
            loss_ref[...] += part
            dln = diff * (1.0 / D_MODEL)
            gg_ref[...] += _fold_rows(dln * xhat)
            gb_ref[...] += _fold_rows(dln)
            dxhat = dln * gain_v
            dz = rstd * (dxhat - jnp.mean(dxhat, axis=-1, keepdims=True)
                         - xhat * jnp.mean(dxhat * xhat, axis=-1, keepdims=True))
            dz_ref[rows, :] = dz
            dzb_ref[rows, :] = dz.astype(BF16)

    row = lambda: pl.BlockSpec((tm, D_MODEL), lambda i: (i, 0))
    vec = lambda: pl.BlockSpec((1, D_MODEL), lambda i: (0, 0))
    acc = lambda width: pl.BlockSpec((8, width), lambda i: (0, 0))
    return _pallas(
        body, name="out_proj_loss", grid=(seq // tm,),
        in_specs=[row(), pl.BlockSpec((D_MODEL, D_MODEL), lambda i: (0, 0), pipeline_mode=pl.Buffered(1)),
                  row(), row(), vec(), vec()],
        out_specs=[row(), row(), acc(D_MODEL), acc(D_MODEL), acc(128)],
        out_shape=[jax.ShapeDtypeStruct((seq, D_MODEL), F32), jax.ShapeDtypeStruct((seq, D_MODEL), BF16),
                   jax.ShapeDtypeStruct((8, D_MODEL), F32), jax.ShapeDtypeStruct((8, D_MODEL), F32),
                   jax.ShapeDtypeStruct((8, 128), F32)],
        compiler_params=_params(("arbitrary",), 56),
    )(y, w_out_g.reshape(D_MODEL, D_MODEL), x, target, gain, bias)


def _dy_gate_bwd(dzb, w_out_g, hug, mixpre, pool_scale, lse_all):
    seq = dzb.shape[0]
    tm = 256
    d4, d16 = DILATIONS[1], DILATIONS[2]

    def body(dz_ref, w_ref, ga_ref, gp_ref, mix_ref, sc_ref, lse_ref,
             dh_ref, dpo_ref, do1_ref, do4_ref, do16_ref, st1_ref, st4_ref, st16_ref, da_ref, st_ref):
        dy = _dot_nt(dz_ref[...], w_ref[...])
        ga = ga_ref[...].astype(F32)
        sig = jax.nn.sigmoid(ga)
        attn = mix_ref[:, :D_ATTN].astype(F32)
        dya = dy[:, :D_ATTN]
        dattn = dya * (ga * sig)
        dh_ref[:, :D_ATTN] = (dya * attn * (sig * (1.0 + ga * (1.0 - sig)))).astype(BF16)
        _store_slabs(da_ref, dattn)
        lane = lax.broadcasted_iota(jnp.int32, (tm, STAT_LANES), 1)
        stats = lse_ref[...]
        prod = dattn * attn
        for h in range(N_HEADS):
            delta = jnp.sum(prod[:, h * HEAD_DIM:(h + 1) * HEAD_DIM], axis=-1, keepdims=True)
            stats = jnp.where(lane == N_HEADS + h, delta, stats)
        st_ref[0] = stats
        do1_ref[...] = dattn.astype(BF16)
        st1_ref[...] = stats
        _to_pattern(da_ref, do4_ref, d4, BF16)
        _to_pattern(da_ref, do16_ref, d16, BF16)
        _to_pattern(st_ref, st4_ref, d4, F32)
        _to_pattern(st_ref, st16_ref, d16, F32)

        gp = gp_ref[...].astype(F32)
        sig = jax.nn.sigmoid(gp)
        dyp = dy[:, D_ATTN:]
        dpo_ref[...] = (dyp * (gp * sig)).astype(BF16)
        dh_ref[:, D_ATTN:] = (dyp * (mix_ref[:, D_ATTN:].astype(F32) * sc_ref[...])
                              * (sig * (1.0 + gp * (1.0 - sig)))).astype(BF16)

    row = lambda width, cb=0: pl.BlockSpec((tm, width), lambda i: (i, cb))
    pat = lambda d, width: pl.BlockSpec((d, tm // d, width), lambda i: (0, i, 0))
    pat_shape = lambda d, width, dtype: jax.ShapeDtypeStruct((d, seq // d, width), dtype)
    outs = _pallas(
        body, name="dy_gate_bwd", grid=(seq // tm,),
        in_specs=[row(D_MODEL), pl.BlockSpec((D_MODEL, D_MODEL), lambda i: (0, 0)),
                  row(D_ATTN, 1), row(D_POOL, 2), row(D_MODEL), pl.BlockSpec((1, D_POOL), lambda i: (0, 0)),
                  row(STAT_LANES)],
        out_specs=[row(D_MODEL, D_IN // D_MODEL - 1), row(D_POOL),
                   row(D_ATTN), pat(d4, D_ATTN), pat(d16, D_ATTN),
                   row(STAT_LANES), pat(d4, STAT_LANES), pat(d16, STAT_LANES)],
        out_shape=[jax.ShapeDtypeStruct((seq, D_IN), BF16), jax.ShapeDtypeStruct((seq, D_POOL), BF16),
                   jax.ShapeDtypeStruct((seq, D_ATTN), BF16), pat_shape(d4, D_ATTN, BF16), pat_shape(d16, D_ATTN, BF16),
                   jax.ShapeDtypeStruct((seq, STAT_LANES), F32), pat_shape(d4, STAT_LANES, F32),
                   pat_shape(d16, STAT_LANES, F32)],
        scratch_shapes=[pltpu.VMEM((N_HEADS, tm, HEAD_DIM), F32), pltpu.VMEM((1, tm, STAT_LANES), F32)],
        compiler_params=_params(("parallel",), 48),
    )(dzb, w_out_g.reshape(D_MODEL, D_MODEL), hug, hug, mixpre, pool_scale, lse_all)
    dh, dpo, do1, do4, do16, st1, st4, st16 = outs
    return dh, dpo, [do1[None], do4, do16], [st1[None], st4, st16]


def _pool_bwd(dh, dpo, mixpre, pooled, w_pool_g, pool_scale):
    seq = dpo.shape[0]
    tm = 256
    halo_blocks = tm // POOL_HALO
    last = seq // tm - 1
    n_groups = len(POOL_WINDOWS)

    def body(dh_in_ref, dpo_ref, halo_ref, pre_ref, pooled_ref, wp_ref, sc_ref, du_ref, gw_ref, gs_ref):
        i = pl.program_id(0)

        @pl.when(i == 0)
        def _():
            gw_ref[...] = jnp.zeros_like(gw_ref)
            gs_ref[...] = jnp.zeros_like(gs_ref)

        dpo = dpo_ref[...].astype(F32)
        scale = sc_ref[...]
        gs_ref[...] += _fold_rows(dpo * pre_ref[...].astype(F32))
        halo = jnp.where(i < last, halo_ref[...].astype(F32), 0.0)
        dpw = (jnp.concatenate([dpo, halo], axis=0) * scale).astype(BF16)
        pos = i * tm + lax.broadcasted_iota(jnp.int32, (tm + POOL_HALO, 1), 0)
        for g, window in enumerate(POOL_WINDOWS):
            cols = slice(g * POOL_GROUP_DIM, (g + 1) * POOL_GROUP_DIM)
            dpw_g = dpw[:, cols]
            gw_ref[g] += _dot_tn(pooled_ref[:, cols], dpw_g[:tm, :])
            dpooled = _dot_nt(dpw_g, wp_ref[g])
            count = jnp.minimum(pos + 1, window).astype(F32)
            sums = _window_sums(dpooled / count, window, backward=True)
            du_ref[:, cols] = (sums[:tm, :] - dpooled[:tm, :]).astype(BF16)

    row = lambda width, cb=0: pl.BlockSpec((tm, width), lambda i: (i, cb))
    return _pallas(
        body, name="pool_bwd", grid=(seq // tm,),
        in_specs=[ANY, row(D_POOL),
                  pl.BlockSpec((POOL_HALO, D_POOL),
                               lambda i: (jnp.minimum((i + 1) * halo_blocks, seq // POOL_HALO - 1), 0)),
                  row(D_POOL, 1), row(D_POOL),
                  pl.BlockSpec((n_groups, POOL_GROUP_DIM, POOL_GROUP_DIM), lambda i: (0, 0, 0)),
                  pl.BlockSpec((1, D_POOL), lambda i: (0, 0))],
        out_specs=[row(D_POOL, D_QKV // D_POOL),
                   pl.BlockSpec((n_groups, POOL_GROUP_DIM, POOL_GROUP_DIM), lambda i: (0, 0, 0)),
                   pl.BlockSpec((8, D_POOL), lambda i: (0, 0))],
        out_shape=[jax.ShapeDtypeStruct(dh.shape, dh.dtype),
                   jax.ShapeDtypeStruct((n_groups, POOL_GROUP_DIM, POOL_GROUP_DIM), F32),
                   jax.ShapeDtypeStruct((8, D_POOL), F32)],
        input_output_aliases={0: 0},
        compiler_params=_params(("arbitrary",), 40),
    )(dh, dpo, dpo, mixpre, pooled, w_pool_g, pool_scale)


def _sum_patterns(dh, parts, tabs, unrotate, col_block, name, comm=None):
    seq = dh.shape[0]
    tm, tn = 256, D_ATTN
    per = D_ATTN // tn
    d4, d16 = DILATIONS[1], DILATIONS[2]

    def body(dh_in_ref, a1_ref, a4_ref, a16_ref, ct_ref, up_ref, down_ref, o_ref, n4_ref, n16_ref):
        _from_pattern(a4_ref, n4_ref, d4)
        _from_pattern(a16_ref, n16_ref, d16)
        for s in range(tn // HEAD_DIM):
            cols = slice(s * HEAD_DIM, (s + 1) * HEAD_DIM)
            tot = a1_ref[:, cols].astype(F32) + n4_ref[s] + n16_ref[s]
            if unrotate:
                tot = _rotate_heads(tot, ct_ref[...], -up_ref[...], -down_ref[...])
            o_ref[:, cols] = tot.astype(BF16)

    tab = pl.BlockSpec((tm, HEAD_DIM), lambda i, j: (i, 0))
    pat = lambda d: pl.BlockSpec((d, tm // d, tn), lambda i, j: (0, i, j))
    (dh,), exchanged = _call(
        body, name=name, grid=(seq // tm, per),
        in_specs=[ANY, pl.BlockSpec((tm, tn), lambda i, j: (i, j)), pat(d4), pat(d16), tab, tab, tab],
        out_specs=[pl.BlockSpec((tm, tn), lambda i, j: (i, col_block * per + j))],
        out_shape=[jax.ShapeDtypeStruct(dh.shape, dh.dtype)],
        scratch_shapes=[pltpu.VMEM((tn // HEAD_DIM, tm, HEAD_DIM), F32), pltpu.VMEM((tn // HEAD_DIM, tm, HEAD_DIM), F32)],
        semantics=("parallel", "parallel"), vmem_mib=32, args=(dh, parts[0][0], parts[1], parts[2], *tabs),
        aliases={0: 0}, comm=comm)
    return dh, exchanged


def _grad_w_in(x, dh, half, name, comm=None):
    seq = x.shape[0]
    ts, td, te = 2048, D_MODEL // 2, SHARD_IN

    def body(half_ref, x_ref, dh_ref, o_ref):
        k = pl.program_id(1)
        part = _dot_tn(x_ref[...].astype(BF16), dh_ref[...])

        @pl.when(k == 0)
        def _():
            o_ref[...] = part

        @pl.when(k > 0)
        def _():
            o_ref[...] += part

    (g,), exchanged = _call(
        body, name=name, grid=(N_SHARDS, seq // ts),
        in_specs=[pl.BlockSpec((ts, td), lambda e, k, half_ref: (k, half_ref[0])),
                  pl.BlockSpec((ts, te), lambda e, k, half_ref: (k, e))],
        out_specs=[pl.BlockSpec((None, td, te), lambda e, k, half_ref: (e, 0, 0))],
        out_shape=[jax.ShapeDtypeStruct((N_SHARDS, td, te), F32)],
        scratch_shapes=[], semantics=("parallel", "arbitrary"), vmem_mib=56, args=(x, dh), comm=comm,
        prefetch=(half,))
    return g, exchanged


def _grad_w_out(y, dzb):
    seq = y.shape[0]
    ts, te = 2048, 1024

    def body(y_ref, dz_ref, o_ref):
        k = pl.program_id(1)
        part = _dot_tn(y_ref[...], dz_ref[...])

        @pl.when(k == 0)
        def _():
            o_ref[...] = part

        @pl.when(k > 0)
        def _():
            o_ref[...] += part

    return _pallas(
        body, name="grad_w_out", grid=(D_MODEL // te, seq // ts),
        in_specs=[pl.BlockSpec((ts, te), lambda e, k: (k, e)), pl.BlockSpec((ts, D_MODEL), lambda e, k: (k, 0))],
        out_specs=pl.BlockSpec((te, D_MODEL), lambda e, k: (e, 0)),
        out_shape=jax.ShapeDtypeStruct((D_MODEL, D_MODEL), F32),
        compiler_params=_params(("parallel", "arbitrary"), 56),
    )(y, dzb)


GRAD_X_LATE_SHARDS = 1
GRAD_X_PARTIAL_ROWS = 1024


def _grad_x_partial(dh, w_in_g, first, tiles, prev=None, comm=None):
    seq = dh.shape[0]
    tm, tk = GRAD_X_PARTIAL_ROWS, SHARD_IN

    def body(*refs):
        dh_ref, w_ref, o_ref = refs[-3:]
        k = pl.program_id(1)
        part = _dot_nt(dh_ref[...], w_ref[...])

        @pl.when(k == 0)
        def _():
            o_ref[...] = part

        @pl.when(k > 0)
        def _():
            o_ref[...] += part

    carried = [] if prev is None else [prev]
    (partial,), exchanged = _call(
        body, name="grad_x_partial_%d" % first, grid=(tiles, N_SHARDS - GRAD_X_LATE_SHARDS),
        in_specs=[ANY] * len(carried) + [
            pl.BlockSpec((tm, tk), lambda i, k: (i + first, k)),
            pl.BlockSpec((None, D_MODEL, tk), lambda i, k: (k, 0, 0))],
        out_specs=[pl.BlockSpec((tm, D_MODEL), lambda i, k: (i + first, 0))],
        out_shape=[jax.ShapeDtypeStruct((seq, D_MODEL), F32)],
        scratch_shapes=[], semantics=("parallel", "arbitrary"), vmem_mib=48, args=(*carried, dh, w_in_g),
        aliases={0: 0} if carried else None, comm=comm)
    return partial, exchanged


def _grad_x_final(dh, w_in_g, dz, partial):
    seq = dh.shape[0]
    tm, tk = 512, SHARD_IN
    k0 = N_SHARDS - GRAD_X_LATE_SHARDS

    def body(dh_ref, w_ref, dz_ref, p_ref, o_ref):
        k = pl.program_id(1)
        part = _dot_nt(dh_ref[...], w_ref[...])

        @pl.when(k == 0)
        def _():
            o_ref[...] = (DEEPNORM_ALPHA * dz_ref[...] + p_ref[...]) + part

        @pl.when(k > 0)
        def _():
            o_ref[...] += part

    row = pl.BlockSpec((tm, D_MODEL), lambda i, k: (i, 0))
    return _pallas(
        body, name="grad_x_final", grid=(seq // tm, GRAD_X_LATE_SHARDS),
        in_specs=[pl.BlockSpec((tm, tk), lambda i, k: (i, k + k0)),
                  pl.BlockSpec((None, D_MODEL, tk), lambda i, k: (k + k0, 0, 0)), row, row],
        out_specs=row, out_shape=jax.ShapeDtypeStruct((seq, D_MODEL), F32),
        compiler_params=_params(("parallel", "arbitrary"), 48),
    )(dh, w_in_g, dz, partial)


def _pool_weight(w_pool_sh):
    n_groups = len(POOL_WINDOWS)
    shard_c = POOL_GROUP_DIM // N_SHARDS
    return (w_pool_sh.reshape(N_SHARDS, n_groups, shard_c, POOL_GROUP_DIM).transpose(1, 0, 2, 3)
            .reshape(n_groups, POOL_GROUP_DIM, POOL_GROUP_DIM))


def _pool_grad_pieces(g_w_pool):
    n_groups = len(POOL_WINDOWS)
    half_c = POOL_GROUP_DIM // N_SHARDS // 2
    return (g_w_pool.reshape(n_groups, N_SHARDS, 2, half_c, POOL_GROUP_DIM).transpose(1, 2, 0, 3, 4)
            .reshape(N_SHARDS, 2, n_groups * half_c, POOL_GROUP_DIM))


def _step(x, target, w_bufs, pool_scale, gain, bias, place):
    seq = x.shape[0]
    tabs = _rope_tables(seq)
    core, chip_core, onward, plan = place
    qkv, hug, w_in_g, w_out_g, w_pool_sh = _in_proj_gathering(x, w_bufs, tabs, plan)
    o_list, st_list = [], []
    for p, dil in enumerate(DILATIONS):
        o, st = _attn_fwd(qkv[p], "attn_fwd_d%d" % dil)
        o_list.append(o)
        st_list.append(st)
    w_pool_g = _pool_weight(w_pool_sh)
    y, mixpre, lse_all, pooled = _mix_gate(o_list, st_list, hug, w_pool_g, pool_scale)
    dz, dzb, gain_part, bias_part, loss_part = _out_proj_loss(y, w_out_g, x, target, gain, bias)
    dh, dpo, do_list, stat_list = _dy_gate_bwd(dzb, w_out_g, hug, mixpre, pool_scale, lse_all)
    g_w_out = _grad_w_out(y, dzb)
    dh, g_w_pool, scale_part = _pool_bwd(dh, dpo, mixpre, pooled, w_pool_g, pool_scale)
    small = jnp.concatenate([scale_part, gain_part, bias_part, loss_part], axis=1)
    early = [g_w_out.reshape(N_SHARDS, 2, D_MODEL // (2 * N_SHARDS), D_MODEL), _pool_grad_pieces(g_w_pool)]

    bwd = lambda p, comm: _attn_bwd(qkv[p], do_list[p], stat_list[p], "attn_bwd_d%d" % DILATIONS[p], comm)
    part_a, recv = bwd(0, _exchange_halves(early))
    sums = [_add_own_half(g, r, core, "add_own_half_%d" % a) for a, (g, r) in enumerate(zip(early, recv))]
    part_b, recv = bwd(1, _scatter_to_chips([s[1] for s in sums]))
    bufs = [_add_chips(s[0], r, chip_core, "add_chips_%d" % a) for a, (s, r) in enumerate(zip(sums, recv))]
    part_c, reduced = bwd(2, _share_with_sibling(bufs))
    parts = [part_a, part_b, part_c]
    dh, gathered = _sum_patterns(dh, [t[0] for t in parts], tabs, True, 0, "sum_dq", _gather_small(small))
    dh, _ = _sum_patterns(dh, [t[1] for t in parts], tabs, True, 1, "sum_dk")
    dh, _ = _sum_patterns(dh, [t[2] for t in parts], tabs, False, 2, "sum_dv")

    give, _ = _grad_w_in(x, dh, 1 - core, "grad_w_in_give")
    keep, recv = _grad_w_in(x, dh, core, "grad_w_in_keep", _send_to_sibling([give]))
    total, total_b = _add_pair(keep, recv[0], "add_own_half_w_in")
    n_tiles = seq // GRAD_X_PARTIAL_ROWS
    tiles = max(n_tiles // 4, 1)
    part, relayed = _grad_x_partial(dh, w_in_g, 0, tiles, None, _relay_diagonal(total_b))
    total_b = _fold_relayed(total, total_b, relayed[0], onward)
    part, recv = _grad_x_partial(dh, w_in_g, tiles, n_tiles - tiles, part, _scatter_to_neighbours(total_b))
    buf = _add_chips(total, recv[0], chip_core, "add_chips_w_in")
    g_x = _grad_x_final(dh, w_in_g, dz, part)
    g_w_in = _run_exchange(_share_with_sibling([buf]), "share_w_in")[0]
    return g_x, g_w_in, reduced[0], reduced[1], small, gathered[0]


def _exchange_halves(grads):
    n = len(grads)

    def copies(src, dst, sems):
        x, y, c, _ = _mesh_place()
        return [_remote(src[a].at[j, 1 - c], dst[a].at[j], sems[0].at[a, j], sems[1].at[a, j], (x, y, 1 - c))
                for a in range(n) for j in range(N_SHARDS)]

    def start(src, dst, sems):
        for cp in copies(src, dst, sems):
            cp.start()

    def finish(src, dst, sems):
        for cp in copies(src, dst, sems):
            cp.wait()

    return _Exchange(grads, [jax.ShapeDtypeStruct((N_SHARDS,) + g.shape[2:], g.dtype) for g in grads], {},
                     [pltpu.SemaphoreType.DMA((n, N_SHARDS))] * 2, start, finish)


def _add_own_half(grad, recv, core, name):
    _, _, r, c = grad.shape
    tr = min(r, 256)

    def body(core_ref, g_ref, r_ref, o_ref, ob_ref):
        tot = g_ref[...] + r_ref[...]
        o_ref[...] = tot
        ob_ref[...] = tot.astype(BF16)

    out = pl.BlockSpec((None, tr, c), lambda j, i, core_ref: (j, i, 0))
    return _pallas(
        body, name=name,
        grid_spec=pltpu.PrefetchScalarGridSpec(
            num_scalar_prefetch=1, grid=(N_SHARDS, r // tr),
            in_specs=[pl.BlockSpec((None, None, tr, c), lambda j, i, core_ref: (j, core_ref[0], i, 0)),
                      pl.BlockSpec((None, tr, c), lambda j, i, core_ref: (j, i, 0))],
            out_specs=[out, out]),
        out_shape=[jax.ShapeDtypeStruct((N_SHARDS, r, c), F32), jax.ShapeDtypeStruct((N_SHARDS, r, c), BF16)],
        compiler_params=_params(("parallel", "parallel"), 32),
    )(core, grad, recv)


def _send_to_sibling(arrays):
    n = len(arrays)

    def copies(src, dst, sems):
        x, y, c, _ = _mesh_place()
        return [_remote(src[a], dst[a], sems[0].at[a], sems[1].at[a], (x, y, 1 - c)) for a in range(n)]

    def start(src, dst, sems):
        for cp in copies(src, dst, sems):
            cp.start()

    def finish(src, dst, sems):
        for cp in copies(src, dst, sems):
            cp.wait()

    return _Exchange(arrays, [jax.ShapeDtypeStruct(t.shape, t.dtype) for t in arrays], {},
                     [pltpu.SemaphoreType.DMA((n,))] * 2, start, finish)


def _add_pair(a, b, name):
    _, r, c = a.shape
    tr = min(r, 256)

    def body(a_ref, b_ref, o_ref, ob_ref):
        tot = a_ref[...] + b_ref[...]
        o_ref[...] = tot
        ob_ref[...] = tot.astype(BF16)

    spec = pl.BlockSpec((None, tr, c), lambda j, i: (j, i, 0))
    return _pallas(
        body, name=name, grid=(N_SHARDS, r // tr), in_specs=[spec, spec], out_specs=[spec, spec],
        out_shape=[jax.ShapeDtypeStruct(a.shape, F32), jax.ShapeDtypeStruct(a.shape, BF16)],
        compiler_params=_params(("parallel", "parallel"), 32),
    )(a, b)


def _scatter_to_chips(sums):
    n = len(sums)

    def copies(src, dst, sems):
        x, y, c, chips = _mesh_place()
        return [_remote(src[a].at[2 * cx + cy], dst[a].at[k], sems[0].at[a, k], sems[1].at[a, k], (cx, cy, c))
                for a in range(n) for k, (cx, cy) in enumerate(chips)]

    def start(src, dst, sems):
        for cp in copies(src, dst, sems):
            cp.start()

    def finish(src, dst, sems):
        for cp in copies(src, dst, sems):
            cp.wait()

    return _Exchange(sums, [jax.ShapeDtypeStruct((3,) + s.shape[1:], s.dtype) for s in sums], {},
                     [pltpu.SemaphoreType.DMA((n, 3))] * 2, start, finish)


def _add_chips(sums, recv, chip_core, name):
    _, r, c = sums.shape
    n_recv = recv.shape[0]
    tr = min(r, 256)

    def body(cc_ref, s_ref, r_ref, o_ref):
        tot = s_ref[...]
        for k in range(n_recv):
            tot = tot + r_ref[k].astype(F32)
        o_ref[...] = tot

    return _pallas(
        body, name=name,
        grid_spec=pltpu.PrefetchScalarGridSpec(
            num_scalar_prefetch=1, grid=(r // tr,),
            in_specs=[pl.BlockSpec((None, tr, c), lambda i, cc_ref: (cc_ref[0], i, 0)),
                      pl.BlockSpec((n_recv, tr, c), lambda i, cc_ref: (0, i, 0))],
            out_specs=pl.BlockSpec((None, tr, c), lambda i, cc_ref: (cc_ref[1], i, 0))),
        out_shape=jax.ShapeDtypeStruct((2, r, c), F32),
        compiler_params=_params(("parallel",), 32),
    )(chip_core, sums, recv)


def _relay_diagonal(sums_b):
    def copy(src, dst, sems):
        x, y, c, _ = _mesh_place()
        diagonal = 2 * (1 - x) + (1 - y)
        return _remote(src[0].at[diagonal], dst[0], sems[0].at[0], sems[1].at[0], (x ^ (1 - c), y ^ c, c))

    def start(src, dst, sems):
        copy(src, dst, sems).start()

    def finish(src, dst, sems):
        copy(src, dst, sems).wait()

    return _Exchange([sums_b], [jax.ShapeDtypeStruct(sums_b.shape[1:], sums_b.dtype)], {},
                     [pltpu.SemaphoreType.DMA((1,))] * 2, start, finish)


def _fold_relayed(sums, sums_b, relayed, onward):
    _, r, c = sums.shape
    tr = min(r, 256)

    def body(on_ref, b_in_ref, s_ref, r_ref, o_ref):
        o_ref[...] = (s_ref[...] + r_ref[...].astype(F32)).astype(BF16)

    return _pallas(
        body, name="fold_relayed",
        grid_spec=pltpu.PrefetchScalarGridSpec(
            num_scalar_prefetch=1, grid=(r // tr,),
            in_specs=[ANY, pl.BlockSpec((None, tr, c), lambda i, on_ref: (on_ref[0], i, 0)),
                      pl.BlockSpec((tr, c), lambda i, on_ref: (i, 0))],
            out_specs=pl.BlockSpec((None, tr, c), lambda i, on_ref: (on_ref[0], i, 0))),
        out_shape=jax.ShapeDtypeStruct(sums_b.shape, sums_b.dtype),
        input_output_aliases={1: 0},
        compiler_params=_params(("parallel",), 32),
    )(onward, sums_b, sums, relayed)


def _scatter_to_neighbours(sums_b):
    def copies(src, dst, sems):
        x, y, c, chips = _mesh_place()
        return [_remote(src[0].at[2 * cx + cy], dst[0].at[k], sems[0].at[k], sems[1].at[k], (cx, cy, c))
                for k, (cx, cy) in enumerate(chips[:2])]

    def start(src, dst, sems):
        for cp in copies(src, dst, sems):
            cp.start()

    def finish(src, dst, sems):
        for cp in copies(src, dst, sems):
            cp.wait()

    return _Exchange([sums_b], [jax.ShapeDtypeStruct((2,) + sums_b.shape[1:], sums_b.dtype)], {},
                     [pltpu.SemaphoreType.DMA((2,))] * 2, start, finish)


def _share_with_sibling(bufs):
    n = len(bufs)

    def copies(dst, sems, half):
        x, y, c, _ = _mesh_place()
        h = c if half == "mine" else 1 - c
        return [_remote(dst[a].at[h], dst[a].at[h], sems[0].at[a], sems[1].at[a], (x, y, 1 - c)) for a in range(n)]

    def start(ins, dst, sems):
        for cp in copies(dst, sems, "mine"):
            cp.start()

    def finish(ins, dst, sems):
        for cp in copies(dst, sems, "theirs"):
            cp.wait_recv()
        for cp in copies(dst, sems, "mine"):
            cp.wait_send()

    return _Exchange(bufs, [jax.ShapeDtypeStruct(b.shape, b.dtype) for b in bufs], {a: a for a in range(n)},
                     [pltpu.SemaphoreType.DMA((n,))] * 2, start, finish)


def _adam_math(w, g, m, v):
    m = ADAM_B1 * m + (1.0 - ADAM_B1) * g
    v = ADAM_B2 * v + (1.0 - ADAM_B2) * (g * g)
    m_hat = m / (1.0 - ADAM_B1 ** ADAM_STEP)
    v_hat = v / (1.0 - ADAM_B2 ** ADAM_STEP)
    delta = -ADAM_LR * (m_hat / (jnp.sqrt(v_hat) + ADAM_EPS) + ADAM_WD * w)
    return delta, m, v


def _gather_small(small):
    def peers():
        x, y, c, _ = _mesh_place()
        return [(x ^ ((r >> 2) & 1), y ^ ((r >> 1) & 1), c ^ (r & 1)) for r in range(1, 8)], 4 * x + 2 * y + c

    def start(src, dst, sems):
        to, me = peers()
        for r, peer in enumerate(to):
            _remote(src[0], dst[0].at[me], sems[0].at[r], sems[1].at[r], peer).start()

    def finish(src, dst, sems):
        to, me = peers()
        for r, (px, py, pc) in enumerate(to):
            theirs = dst[0].at[4 * px + 2 * py + pc]
            _remote(theirs, theirs, sems[0].at[r], sems[1].at[r], (px, py, pc)).wait_recv()
        for r, peer in enumerate(to):
            _remote(src[0], dst[0].at[me], sems[0].at[r], sems[1].at[r], peer).wait_send()

    return _Exchange([small], [jax.ShapeDtypeStruct((8,) + small.shape, small.dtype)], {},
                     [pltpu.SemaphoreType.DMA((7,))] * 2, start, finish)


def _small_adamw(gathered, small, me, w_vec, m_vec, v_vec):
    n_par = w_vec.shape[1]

    def body(me_ref, a_ref, s_ref, w_ref, m_ref, v_ref, loss_ref, g_ref, d_ref, nm_ref, nv_ref):
        mine = s_ref[...]
        tot = jnp.where(me_ref[0] == 0, mine, a_ref[0])
        for d in range(1, 8):
            tot = tot + jnp.where(me_ref[0] == d, mine, a_ref[d])
        tot = jnp.sum(tot, axis=0, keepdims=True)
        sq = jnp.sum(tot[:, n_par:], axis=1, keepdims=True)
        loss_ref[...] = jnp.broadcast_to(sq * (0.5 / D_MODEL), loss_ref.shape)
        g = tot[:, :n_par]
        g_ref[...] = g
        d_ref[...], nm_ref[...], nv_ref[...] = _adam_math(w_ref[...], g, m_ref[...], v_ref[...])

    vm = pl.BlockSpec(memory_space=pltpu.VMEM)
    vec = jax.ShapeDtypeStruct((1, n_par), F32)
    return pl.pallas_call(
        body, name="small_adamw",
        grid_spec=pltpu.PrefetchScalarGridSpec(num_scalar_prefetch=1, grid=(), in_specs=[vm] * 5, out_specs=[vm] * 5),
        out_shape=[jax.ShapeDtypeStruct((1, 128), F32), vec, vec, vec, vec],
    )(me, gathered, small, w_vec, m_vec, v_vec)


def _adamw(w, g, m, v, name):
    r, c = w.shape
    tr = min(r, 256)

    def body(w_ref, g_ref, m_ref, v_ref, go_ref, d_ref, nm_ref, nv_ref):
        g = g_ref[...]
        go_ref[...] = g
        d_ref[...], nm_ref[...], nv_ref[...] = _adam_math(w_ref[...], g, m_ref[...], v_ref[...])

    spec = pl.BlockSpec((tr, c), lambda i: (i, 0))
    shape = jax.ShapeDtypeStruct((r, c), F32)
    return _pallas(
        body, name=name, grid=(r // tr,),
        in_specs=[spec] * 4, out_specs=[spec] * 4, out_shape=[shape] * 4,
        compiler_params=_params(("parallel",), 48),
    )(w, g, m, v)


def kernel(x, w_in, w_pool, pool_scale, w_out, ln_gain, ln_bias, loss_target, m_w_in, m_w_pool, m_pool_scale, m_w_out, m_ln_gain, m_ln_bias, v_w_in, v_w_pool, v_pool_scale, v_w_out, v_ln_gain, v_ln_bias):
    xi, yi, ci = lax.axis_index("x"), lax.axis_index("y"), lax.axis_index("c")
    chip = (2 * xi + yi).astype(jnp.int32).reshape(1)
    core = ci.astype(jnp.int32).reshape(1)
    n_groups = len(POOL_WINDOWS)
    shard_c = w_pool.shape[2]

    w_in_b = _cast_bf16(w_in[0], chip, "cast_w_in", 256)
    w_out_b = _cast_bf16(w_out[0], chip, "cast_w_out", 256)
    w_pool_b = _cast_bf16(w_pool[0].reshape(n_groups * shard_c, POOL_GROUP_DIM), chip, "cast_w_pool", 256)

    chip_core = jnp.concatenate([chip, core])
    onward = (2 * (xi ^ ci) + (yi ^ (1 - ci))).astype(jnp.int32).reshape(1)
    g_x, full_in, full_out, full_pool, small, small_all = _step(
        x[0], loss_target[0], [w_in_b, w_out_b, w_pool_b], pool_scale, ln_gain, ln_bias,
        (core, chip_core, onward, _in_proj_plan(xi, yi)))
    half_c = shard_c // 2
    grad_w_in = full_in.reshape(D_MODEL, SHARD_IN)
    grad_w_out = full_out.reshape(D_MODEL // N_SHARDS, D_MODEL)
    grad_w_pool = (full_pool.reshape(2, n_groups, half_c, POOL_GROUP_DIM).transpose(1, 0, 2, 3)
                   .reshape(n_groups * shard_c, POOL_GROUP_DIM))

    grad_w_in, d_in, nm_in, nv_in = _adamw(w_in[0], grad_w_in, m_w_in[0], v_w_in[0], "adamw_w_in")
    grad_w_out, d_out, nm_out, nv_out = _adamw(w_out[0], grad_w_out, m_w_out[0], v_w_out[0], "adamw_w_out")
    flat = lambda t: t[0].reshape(n_groups * shard_c, POOL_GROUP_DIM)
    grad_w_pool, d_pool, nm_pool, nv_pool = _adamw(flat(w_pool), grad_w_pool, flat(m_w_pool), flat(v_w_pool),
                                                   "adamw_w_pool")

    cat = lambda a, b, c: jnp.concatenate([a, b, c], axis=1)
    me = (4 * xi + 2 * yi + ci).astype(jnp.int32).reshape(1)
    loss_v, g_vec, d_vec, nm_vec, nv_vec = _small_adamw(
        small_all, small, me, cat(pool_scale, ln_gain, ln_bias), cat(m_pool_scale, m_ln_gain, m_ln_bias),
        cat(v_pool_scale, v_ln_gain, v_ln_bias))

    def split(vec):
        return vec[:, :D_POOL], vec[:, D_POOL:D_POOL + D_MODEL], vec[:, D_POOL + D_MODEL:]

    g_scale, g_gain, g_bias = split(g_vec)
    d_scale, d_gain, d_bias = split(d_vec)
    nm_scale, nm_gain, nm_bias = split(nm_vec)
    nv_scale, nv_gain, nv_bias = split(nv_vec)
    pool_shape = w_pool.shape
    return (loss_v[0, 0], g_x[None],
            grad_w_in[None], grad_w_pool.reshape(pool_shape), g_scale, grad_w_out[None], g_gain, g_bias,
            d_in[None], d_pool.reshape(pool_shape), d_scale, d_out[None], d_gain, d_bias,
            nm_in[None], nm_pool.reshape(pool_shape), nm_scale, nm_out[None], nm_gain, nm_bias,
            nv_in[None], nv_pool.reshape(pool_shape), nv_scale, nv_out[None], nv_gain, nv_bias)
```

```python
import functools

import jax
import jax.numpy as jnp
from jax import lax
from jax.experimental import pallas as pl
from jax.experimental.pallas import tpu as pltpu

F32 = jnp.float32
BF16 = jnp.bfloat16
MESH = pl.DeviceIdType.MESH
ANY = pl.BlockSpec(memory_space=pl.ANY)

D_MODEL = 2048
D_ATTN = 1024
D_POOL = 1024
HEAD_DIM = 128
N_HEADS = 8
ROPE_DIM = 32
ROPE_THETA = 500000.0
DILATIONS = (1, 4, 16)
KEY_BLOCK = 128
CHUNK = 2 * KEY_BLOCK
STAT_LANES = 128
POOL_WINDOWS = (2, 4, 8, 16)
POOL_GROUP_DIM = 256
POOL_HALO = 16
D_QKV = 3 * D_ATTN
D_UG = D_POOL + D_MODEL
D_IN = D_QKV + D_UG
N_SHARDS = 4
SHARD_IN = D_IN // N_SHARDS
LN_EPS = 1e-5
DEEPNORM_ALPHA = 2.0 ** 0.25
ADAM_LR = 0.001
ADAM_B1 = 0.9
ADAM_B2 = 0.999
ADAM_EPS = 1e-08
ADAM_WD = 0.01
ADAM_STEP = 10
NEG = -1e30
MIB = 1024 * 1024


def _params(sem, vmem_mib):
    return pltpu.CompilerParams(dimension_semantics=sem, vmem_limit_bytes=vmem_mib * MIB)


def _pallas(body, **kwargs):
    pin = lambda s: pltpu.HBM(s.shape, s.dtype) if len(s.shape) >= 2 else s
    out_shape = kwargs.pop("out_shape")
    out_shape = [pin(s) for s in out_shape] if isinstance(out_shape, (list, tuple)) else pin(out_shape)
    call = pl.pallas_call(body, out_shape=out_shape, **kwargs)

    def run(*operands):
        return call(*[pltpu.with_memory_space_constraint(o, pltpu.HBM) if o.ndim >= 2 else o for o in operands])

    return run


class _Exchange:
    def __init__(self, operands, out_shape, aliases, sems, start, finish):
        self.operands, self.out_shape, self.aliases, self.sems = list(operands), list(out_shape), dict(aliases), list(sems)
        self.start, self.finish = start, finish


def _run_exchange(comm, name):
    n_in, n_out = len(comm.operands), len(comm.out_shape)

    def body(*refs):
        ins, outs, sems = refs[:n_in], refs[n_in:n_in + n_out], refs[n_in + n_out:]
        comm.start(ins, outs, sems)
        comm.finish(ins, outs, sems)

    return _pallas(
        body, name=name, in_specs=[ANY] * n_in, out_specs=[ANY] * n_out, out_shape=comm.out_shape,
        input_output_aliases=comm.aliases, scratch_shapes=comm.sems,
    )(*comm.operands)


def _call(body, *, name, grid, in_specs, out_specs, out_shape, scratch_shapes, semantics, vmem_mib, args,
          aliases=None, comm=None, prefetch=()):
    aliases = dict(aliases or {})
    n_pre, n_in, n_out, n_scr = len(prefetch), len(in_specs), len(out_specs), len(scratch_shapes)
    c_in, c_out = (len(comm.operands), len(comm.out_shape)) if comm else (0, 0)
    c_shapes, c_sems, c_operands = (comm.out_shape, comm.sems, comm.operands) if comm else ([], [], [])

    def hosted(*refs):
        pre, refs = refs[:n_pre], refs[n_pre:]
        a = n_in
        b = a + c_in
        c = b + n_out
        d = c + c_out
        e = d + n_scr
        if comm is None:
            body(*pre, *refs)
            return
        ids = [pl.program_id(k) for k in range(len(grid))]
        first = functools.reduce(jnp.logical_and, [i == 0 for i in ids])
        last = functools.reduce(jnp.logical_and, [i == g - 1 for i, g in zip(ids, grid)])

        @pl.when(first)
        def _():
            comm.start(refs[a:b], refs[c:d], refs[e:])

        body(*pre, *refs[:a], *refs[b:c], *refs[d:e])

        @pl.when(last)
        def _():
            comm.finish(refs[a:b], refs[c:d], refs[e:])

    if comm:
        semantics = ("arbitrary",) * len(grid)
        for i, o in comm.aliases.items():
            aliases[n_pre + n_in + i] = n_out + o
    outs = _pallas(
        hosted, name=name,
        grid_spec=pltpu.PrefetchScalarGridSpec(
            num_scalar_prefetch=n_pre, grid=grid, in_specs=list(in_specs) + [ANY] * c_in,
            out_specs=list(out_specs) + [ANY] * c_out, scratch_shapes=list(scratch_shapes) + c_sems),
        out_shape=list(out_shape) + c_shapes, input_output_aliases=aliases,
        compiler_params=_params(semantics, vmem_mib),
    )(*prefetch, *args, *c_operands)
    return list(outs[:n_out]), list(outs[n_out:])


def _dot_nn(a, b):
    return jnp.dot(a, b, preferred_element_type=F32)


def _dot_nt(a, b):
    return lax.dot_general(a, b, (((1,), (1,)), ((), ())), preferred_element_type=F32)


def _dot_tn(a, b):
    return lax.dot_general(a, b, (((0,), (0,)), ((), ())), preferred_element_type=F32)


def _fold_rows(a):
    r, c = a.shape
    return jnp.sum(a.reshape(r // 8, 8, c), axis=0)


def _cast_bf16(a, chip, name, rows):
    r, c = a.shape

    def body(chip_ref, a_ref, o_ref):
        o_ref[...] = a_ref[...].astype(BF16)

    return _pallas(
        body, name=name,
        grid_spec=pltpu.PrefetchScalarGridSpec(
            num_scalar_prefetch=1, grid=(r // rows,),
            in_specs=[pl.BlockSpec((rows, c), lambda i, chip_ref: (i, 0))],
            out_specs=pl.BlockSpec((None, rows, c), lambda i, chip_ref: (chip_ref[0], i, 0))),
        out_shape=jax.ShapeDtypeStruct((N_SHARDS, r, c), BF16),
        compiler_params=_params(("parallel",), 32),
    )(chip, a)


def _mesh_place():
    x, y, c = lax.axis_index("x"), lax.axis_index("y"), lax.axis_index("c")
    return x, y, c, [(1 - x, y), (x, 1 - y), (1 - x, 1 - y)]


def _remote(src, dst, send_sem, recv_sem, to):
    return pltpu.make_async_remote_copy(src_ref=src, dst_ref=dst, send_sem=send_sem, recv_sem=recv_sem,
                                        device_id=to, device_id_type=MESH)


def _rope_tables(seq):
    half = ROPE_DIM // 2
    inv_freq = ROPE_THETA ** (-(2.0 * jnp.arange(half, dtype=F32)) / ROPE_DIM)
    ang = jnp.arange(seq, dtype=jnp.int32).astype(F32)[:, None] * inv_freq[None, :]
    cos, sin = jnp.cos(ang), jnp.sin(ang)
    pad = jnp.zeros((seq, HEAD_DIM - ROPE_DIM), F32)
    zeros = jnp.zeros((seq, half), F32)
    c_tab = jnp.concatenate([cos, cos, pad + 1.0], axis=1)
    up_tab = jnp.concatenate([-sin, zeros, pad], axis=1)
    down_tab = jnp.concatenate([zeros, sin, pad], axis=1)
    return c_tab, up_tab, down_tab


def _rotate_heads(t, c_tab, up_tab, down_tab):
    outs = []
    for h in range(t.shape[1] // HEAD_DIM):
        th = t[:, h * HEAD_DIM:(h + 1) * HEAD_DIM]
        up = pltpu.roll(th, HEAD_DIM - ROPE_DIM // 2, axis=1)
        down = pltpu.roll(th, ROPE_DIM // 2, axis=1)
        outs.append(th * c_tab + up * up_tab + down * down_tab)
    return outs[0] if len(outs) == 1 else jnp.concatenate(outs, axis=1)


def _to_pattern(slabs_ref, dst_ref, dil, dtype):
    n_slabs, rows, _ = slabs_ref.shape
    for s in range(n_slabs):
        for r in range(dil):
            dst_ref[r, :, s * 128:(s + 1) * 128] = slabs_ref[s, pl.ds(r, rows // dil, dil), :].astype(dtype)


def _from_pattern(src_ref, slabs_ref, dil):
    n_slabs, rows, _ = slabs_ref.shape
    for s in range(n_slabs):
        for r in range(dil):
            slabs_ref[s, pl.ds(r, rows // dil, dil), :] = src_ref[r, :, s * 128:(s + 1) * 128].astype(F32)


def _store_slabs(slabs_ref, value):
    for s in range(slabs_ref.shape[0]):
        slabs_ref[s] = value[:, s * 128:(s + 1) * 128]


W_IN_CHUNKS = 4


def _in_proj_plan(x, y):
    shards = [2 * x + y, 2 * (1 - x) + y, 2 * x + (1 - y), 2 * (1 - x) + (1 - y)]
    last_row = jnp.int32(-2)

    def table(active, col_of):
        cols, rows = [], []
        first_col = functools.reduce(lambda acc, j: jnp.where(active[j], col_of(shards[j]), acc), reversed(range(4)),
                                     jnp.int32(0))
        held_col, seen = first_col, jnp.bool_(False)
        for j in range(4):
            cols.append(jnp.where(active[j], col_of(shards[j]), held_col))
            rows.append(jnp.where(active[j], -1, jnp.where(seen, last_row, 0)))
            held_col = jnp.where(active[j], col_of(shards[j]), held_col)
            seen = jnp.logical_or(seen, active[j])
        return cols, rows

    q_cols, q_rows = table([s < 2 for s in shards], lambda s: s)
    h_cols, h_rows = table([s >= 2 for s in shards], lambda s: s - 2)
    return jnp.stack([jnp.asarray(v, jnp.int32) for v in shards + q_cols + q_rows + h_cols + h_rows])


def _in_proj_gathering(x, w_bufs, tabs, plan):
    seq = x.shape[0]
    tm, tn = 512, SHARD_IN
    n_tiles = seq // tm
    heads = tn // HEAD_DIM
    k_heads_in_second = 2 * D_ATTN // HEAD_DIM - heads
    d4, d16 = DILATIONS[1], DILATIONS[2]
    DIAGONAL = 2
    chunk = D_MODEL // 2 // W_IN_CHUNKS
    early = [(0, D_MODEL // 2, q * chunk, chunk) for q in range(W_IN_CHUNKS)]
    late = [(a, w_bufs[a].shape[1] // 2, 0, w_bufs[a].shape[1] // 2) for a in (1, 2)]
    pieces = early + late
    early_ids, late_ids = range(len(early)), range(len(early), len(pieces))

    def body(plan_ref, x_ref, w_in_in, w_out_in, w_pool_in, c_ref, up_ref, down_ref,
             o1_ref, o4_ref, o16_ref, hug_ref, w_ref, w_out_ref, w_pool_ref,
             wbuf_ref, res_ref, w_sem, ici_send, ici_recv, d2d_send, d2d_recv):
        j, i = pl.program_id(0), pl.program_id(1)
        mx, my, mc, chips = _mesh_place()
        sibling = (mx, my, 1 - mc)
        gathered = (w_ref, w_out_ref, w_pool_ref)
        chip_of = lambda k: 2 * chips[k][0] + chips[k][1]

        def piece(n, chip, core):
            a, per_core, offset, size = pieces[n]
            return gathered[a].at[chip, pl.ds(core * per_core + offset, size)]

        def to_neighbour(k, n):
            mine = piece(n, 2 * mx + my, mc)
            return _remote(mine, mine, ici_send.at[n, k], ici_recv.at[n, k], (*chips[k], mc))

        def relay(n):
            theirs = piece(n, 2 * (mx ^ (1 - mc)) + (my ^ mc), mc)
            return _remote(theirs, theirs, ici_send.at[n, DIAGONAL], ici_recv.at[n, DIAGONAL], (mx ^ mc, my ^ (1 - mc), mc))

        def arrival(k, n):
            theirs = piece(n, chip_of(k), mc)
            return _remote(theirs, theirs, ici_send.at[n, k], ici_recv.at[n, k], (*chips[k], mc))

        def to_sibling(k, n, core):
            theirs = piece(n, chip_of(k), core)
            return _remote(theirs, theirs, d2d_send.at[n, k], d2d_recv.at[n, k], sibling)

        def take(k, ids):
            for n in ids:
                arrival(k, n).wait_recv()
                to_sibling(k, n, mc).start()

        def taken(k, ids):
            for n in ids:
                to_sibling(k, n, 1 - mc).wait_recv()

        first_tile = i == 0

        @pl.when(jnp.logical_and(j == 0, first_tile))
        def _():
            for n in range(len(pieces)):
                for k in range(DIAGONAL):
                    to_neighbour(k, n).start()

        @pl.when(jnp.logical_and(j == 1, first_tile))
        def _():
            take(0, early_ids)
            taken(0, early_ids)

        @pl.when(jnp.logical_and(j == 2, first_tile))
        def _():
            take(1, early_ids)
            for n in early_ids:
                relay(n).start()
            taken(1, early_ids)
            for k in range(DIAGONAL):
                take(k, late_ids)
            for n in late_ids:
                relay(n).start()
            for k in range(DIAGONAL):
                taken(k, late_ids)

        @pl.when(jnp.logical_and(j == 3, first_tile))
        def _():
            take(DIAGONAL, range(len(pieces)))
            taken(DIAGONAL, range(len(pieces)))

        shard = plan_ref[j]

        @pl.when(first_tile)
        def _():
            cp = pltpu.make_async_copy(w_ref.at[shard], wbuf_ref, w_sem)
            cp.start()
            cp.wait()

        xb = x_ref[...].astype(BF16)
        group = 4 * HEAD_DIM
        accs = [_dot_nn(xb, wbuf_ref[:, g * group:(g + 1) * group]) for g in range(tn // group)]

        def emit_qkv(rotated_heads):
            for h in range(heads):
                lanes = (h * HEAD_DIM) % group
                th = accs[h * HEAD_DIM // group][:, lanes:lanes + HEAD_DIM]
                if h < rotated_heads:
                    th = _rotate_heads(th, c_ref[...], up_ref[...], down_ref[...])
                res_ref[h] = th
                o1_ref[:, h * HEAD_DIM:(h + 1) * HEAD_DIM] = th.astype(BF16)
            _to_pattern(res_ref, o4_ref, d4, BF16)
            _to_pattern(res_ref, o16_ref, d16, BF16)

        @pl.when(shard == 0)
        def _():
            emit_qkv(heads)

        @pl.when(shard == 1)
        def _():
            emit_qkv(k_heads_in_second)

        @pl.when(shard >= 2)
        def _():
            for g, acc in enumerate(accs):
                hug_ref[:, g * group:(g + 1) * group] = acc.astype(BF16)

        @pl.when(jnp.logical_and(j == 3, i == n_tiles - 1))
        def _():
            for n in range(len(pieces)):
                for k in range(DIAGONAL):
                    to_neighbour(k, n).wait_send()
                relay(n).wait_send()
                for k in range(DIAGONAL + 1):
                    to_sibling(k, n, mc).wait_send()

    def held(base, last):
        return lambda j, i, plan_ref: jnp.where(plan_ref[base + j] == -1, i,
                                                jnp.where(plan_ref[base + j] == -2, last, 0))

    q_row, h_row = held(8, n_tiles - 1), held(16, n_tiles - 1)
    tab_spec = pl.BlockSpec((tm, HEAD_DIM), lambda j, i, plan_ref: (i, 0))
    sems = [pltpu.SemaphoreType.DMA((len(pieces), 3))] * 4
    o1, o4, o16, hug, w_in_g, w_out_g, w_pool_g = _pallas(
        body, name="in_proj_gathering",
        grid_spec=pltpu.PrefetchScalarGridSpec(
            num_scalar_prefetch=1, grid=(N_SHARDS, n_tiles),
            in_specs=[pl.BlockSpec((tm, D_MODEL), lambda j, i, plan_ref: (i, 0)), ANY, ANY, ANY,
                      tab_spec, tab_spec, tab_spec],
            out_specs=[pl.BlockSpec((tm, tn), lambda j, i, p: (q_row(j, i, p), p[4 + j])),
                       pl.BlockSpec((d4, tm // d4, tn), lambda j, i, p: (0, q_row(j, i, p), p[4 + j])),
                       pl.BlockSpec((d16, tm // d16, tn), lambda j, i, p: (0, q_row(j, i, p), p[4 + j])),
                       pl.BlockSpec((tm, tn), lambda j, i, p: (h_row(j, i, p), p[12 + j])),
                       ANY, ANY, ANY],
            scratch_shapes=[pltpu.VMEM((D_MODEL, tn), BF16), pltpu.VMEM((heads, tm, HEAD_DIM), F32),
                            pltpu.SemaphoreType.DMA(())] + sems),
        out_shape=[jax.ShapeDtypeStruct((seq, D_QKV), BF16),
                   jax.ShapeDtypeStruct((d4, seq // d4, D_QKV), BF16),
                   jax.ShapeDtypeStruct((d16, seq // d16, D_QKV), BF16),
                   jax.ShapeDtypeStruct((seq, D_UG), BF16)]
        + [jax.ShapeDtypeStruct(b.shape, b.dtype) for b in w_bufs],
        input_output_aliases={2: 4, 3: 5, 4: 6},
        compiler_params=_params(("arbitrary", "arbitrary"), 52),
    )(plan, x, *w_bufs, *tabs)
    return [o1[None], o4, o16], hug, w_in_g, w_out_g, w_pool_g


def _band_masks():
    row = lax.broadcasted_iota(jnp.int32, (KEY_BLOCK, KEY_BLOCK), 0)
    col = lax.broadcasted_iota(jnp.int32, (KEY_BLOCK, KEY_BLOCK), 1)
    return col <= row, col >= row


def _attn_fwd(qkv, name):
    dil, n, _ = qkv.shape
    scale = HEAD_DIM ** -0.5
    lo, hi = slice(0, KEY_BLOCK), slice(KEY_BLOCK, CHUNK)

    def body(q_ref, k_ref, v_ref, kb_ref, vb_ref, o_ref, st_ref):
        i = pl.program_id(1)
        cur_mask, prev_mask = _band_masks()
        before_mask = jnp.logical_and(prev_mask, i > 0)
        lane = lax.broadcasted_iota(jnp.int32, (KEY_BLOCK, STAT_LANES), 1)
        tasks = [(rows, h) for rows in (lo, hi) for h in range(N_HEADS)]
        head = lambda h: slice(h * HEAD_DIM, (h + 1) * HEAD_DIM)

        def prev_of(rows, h):
            if rows is lo:
                return kb_ref[:, head(h)], vb_ref[:, head(h)], before_mask
            return k_ref[lo, head(h)], v_ref[lo, head(h)], prev_mask

        scores = []
        for rows, h in tasks:
            q = q_ref[rows, head(h)]
            scores.append((_dot_nt(q, prev_of(rows, h)[0]), _dot_nt(q, k_ref[rows, head(h)])))
        probs = []
        for (rows, h), (qk_prev, qk_cur) in zip(tasks, scores):
            s_prev = jnp.where(prev_of(rows, h)[2], qk_prev * scale, NEG)
            s_cur = jnp.where(cur_mask, qk_cur * scale, NEG)
            m = jnp.max(jnp.maximum(s_prev, s_cur), axis=-1, keepdims=True)
            p_prev = jnp.exp(s_prev - m)
            p_cur = jnp.exp(s_cur - m)
            den = jnp.sum(p_prev + p_cur, axis=-1, keepdims=True)
            probs.append((p_prev.astype(BF16), p_cur.astype(BF16), den, m + jnp.log(den)))
        stats = [jnp.zeros((KEY_BLOCK, STAT_LANES), F32), jnp.zeros((KEY_BLOCK, STAT_LANES), F32)]
        for (rows, h), (p_prev, p_cur, den, lse) in zip(tasks, probs):
            o = _dot_nn(p_cur, v_ref[rows, head(h)]) + _dot_nn(p_prev, prev_of(rows, h)[1])
            o_ref[rows, head(h)] = (o / den).astype(BF16)
            b = 0 if rows is lo else 1
            stats[b] = jnp.where(lane == h, lse, stats[b])
        st_ref[lo, :] = stats[0]
        st_ref[hi, :] = stats[1]

    main = lambda cb: pl.BlockSpec((None, CHUNK, D_ATTN), lambda r, i: (r, i, cb))
    before = lambda cb: pl.BlockSpec((None, KEY_BLOCK, D_ATTN), lambda r, i: (r, jnp.maximum(2 * i - 1, 0), cb))
    return _pallas(
        body, name=name, grid=(dil, n // CHUNK),
        in_specs=[main(0), main(1), main(2), before(1), before(2)],
        out_specs=[main(0), pl.BlockSpec((None, CHUNK, STAT_LANES), lambda r, i: (r, i, 0))],
        out_shape=[jax.ShapeDtypeStruct((dil, n, D_ATTN), BF16), jax.ShapeDtypeStruct((dil, n, STAT_LANES), F32)],
        compiler_params=_params(("parallel", "parallel"), 40),
    )(qkv, qkv, qkv, qkv, qkv)


def _attn_bwd(qkv, do, stats, name, comm=None):
    dil, n, _ = qkv.shape
    n_blocks = n // KEY_BLOCK
    last = n // CHUNK - 1
    scale = HEAD_DIM ** -0.5
    lo, hi = slice(0, KEY_BLOCK), slice(KEY_BLOCK, CHUNK)

    def body(q_ref, k_ref, v_ref, kb_ref, vb_ref, qa_ref, do_ref, doa_ref, st_ref, sta_ref, dq_ref, dk_ref, dv_ref):
        i = pl.program_id(1)
        cur_mask, prev_mask = _band_masks()
        before_mask = jnp.logical_and(prev_mask, i > 0)
        after_mask = jnp.logical_and(prev_mask, i < last)

        rows_cat = lambda a, b: jnp.concatenate([a, b], axis=0)
        masks = (jnp.concatenate([before_mask, cur_mask], axis=1), jnp.concatenate([prev_mask, cur_mask], axis=1),
                 after_mask)

        def operands(h):
            cols = slice(h * HEAD_DIM, (h + 1) * HEAD_DIM)
            lse_c, del_c = slice(h, h + 1), slice(N_HEADS + h, N_HEADS + h + 1)
            q = (q_ref[lo, cols], q_ref[hi, cols], qa_ref[:, cols])
            do = (do_ref[lo, cols], do_ref[hi, cols], doa_ref[:, cols])
            keys = (rows_cat(kb_ref[:, cols], k_ref[lo, cols]), k_ref[:, cols], k_ref[hi, cols])
            vals = (rows_cat(vb_ref[:, cols], v_ref[lo, cols]), v_ref[:, cols], v_ref[hi, cols])
            st = ((st_ref[lo, lse_c], st_ref[lo, del_c]), (st_ref[hi, lse_c], st_ref[hi, del_c]),
                  (sta_ref[:, lse_c], sta_ref[:, del_c]))
            return cols, q, do, keys, vals, st

        group = N_HEADS // 2
        for first_head in range(0, N_HEADS, group):
            heads = range(first_head, first_head + group)
            raw = {}
            for h in heads:
                _, q, do, keys, vals, _ = operands(h)
                raw[h] = [(_dot_nt(q[j], keys[j]), _dot_nt(do[j], vals[j])) for j in range(3)]
            grads = {}
            for h in heads:
                st = operands(h)[5]
                grads[h] = []
                for j in range(3):
                    qk, dp = raw[h][j]
                    lse, delta = st[j]
                    p = jnp.exp(jnp.where(masks[j], qk * scale, NEG) - lse)
                    grads[h].append((p.astype(BF16), (p * (dp - delta) * scale).astype(BF16)))
            for h in heads:
                cols, q, do, keys, _, _ = operands(h)
                (p0, ds0), (p1, ds1), (pa, dsa) = grads[h]
                own, nxt = slice(KEY_BLOCK, CHUNK), slice(0, KEY_BLOCK)

                def put(ref, rows, val, cols=cols):
                    ref[rows, cols] = val.astype(ref.dtype)

                put(dq_ref, lo, _dot_nn(ds0, keys[0]))
                put(dq_ref, hi, _dot_nn(ds1, keys[1]))
                put(dk_ref, lo, _dot_tn(rows_cat(ds0[:, own], ds1[:, nxt]), q_ref[:, cols]))
                put(dk_ref, hi, _dot_tn(rows_cat(ds1[:, own], dsa), rows_cat(q[1], q[2])))
                put(dv_ref, lo, _dot_tn(rows_cat(p0[:, own], p1[:, nxt]), do_ref[:, cols]))
                put(dv_ref, hi, _dot_tn(rows_cat(p1[:, own], pa), rows_cat(do[1], do[2])))

    def spec(rows, width, row_of, cb):
        return pl.BlockSpec((None, rows, width), lambda r, i: (r, row_of(i), cb))

    same = lambda i: i
    before = lambda i: jnp.maximum(2 * i - 1, 0)
    after = lambda i: jnp.minimum(2 * i + 2, n_blocks - 1)
    out = spec(CHUNK, D_ATTN, same, 0)
    return _call(
        body, name=name, grid=(dil, n // CHUNK),
        in_specs=[spec(CHUNK, D_ATTN, same, 0), spec(CHUNK, D_ATTN, same, 1), spec(CHUNK, D_ATTN, same, 2),
                  spec(KEY_BLOCK, D_ATTN, before, 1), spec(KEY_BLOCK, D_ATTN, before, 2),
                  spec(KEY_BLOCK, D_ATTN, after, 0),
                  spec(CHUNK, D_ATTN, same, 0), spec(KEY_BLOCK, D_ATTN, after, 0),
                  spec(CHUNK, STAT_LANES, same, 0), spec(KEY_BLOCK, STAT_LANES, after, 0)],
        out_specs=[out, out, out],
        out_shape=[jax.ShapeDtypeStruct((dil, n, D_ATTN), BF16)] * 3,
        scratch_shapes=[], semantics=("parallel", "parallel"), vmem_mib=40,
        args=(qkv, qkv, qkv, qkv, qkv, qkv, do, do, stats, stats), comm=comm)


def _window_sums(ext, window, backward):
    rows = ext.shape[0]
    acc, span = ext, 1
    while span < window:
        acc = acc + pltpu.roll(acc, (rows - span) if backward else span, axis=0)
        span *= 2
    return acc


def _mix_gate(o_list, st_list, hug, w_pool_g, pool_scale):
    seq = hug.shape[0]
    tm = 256
    halo_blocks = tm // POOL_HALO
    d4, d16 = DILATIONS[1], DILATIONS[2]

    def body(o1_ref, o4_ref, o16_ref, l1_ref, l4_ref, l16_ref, u_ref, halo_ref, ga_ref, gp_ref, wp_ref, sc_ref,
             y_ref, mix_ref, lse_ref, pooled_ref, n4_ref, n16_ref, nl4_ref, nl16_ref):
        i = pl.program_id(0)
        _from_pattern(o4_ref, n4_ref, d4)
        _from_pattern(o16_ref, n16_ref, d16)
        _from_pattern(l4_ref, nl4_ref, d4)
        _from_pattern(l16_ref, nl16_ref, d16)
        la, lb, lc = l1_ref[...], nl4_ref[0], nl16_ref[0]
        mx = jnp.maximum(jnp.maximum(la, lb), lc)
        ea, eb, ec = jnp.exp(la - mx), jnp.exp(lb - mx), jnp.exp(lc - mx)
        tot = ea + eb + ec
        lse_ref[...] = mx + jnp.log(tot)
        wa, wb, wc = ea / tot, eb / tot, ec / tot
        ga = ga_ref[...].astype(F32)
        silu_a = ga * jax.nn.sigmoid(ga)
        for h in range(N_HEADS):
            cols = slice(h * HEAD_DIM, (h + 1) * HEAD_DIM)
            hc = slice(h, h + 1)
            attn = wa[:, hc] * o1_ref[:, cols].astype(F32) + wb[:, hc] * n4_ref[h] + wc[:, hc] * n16_ref[h]
            mix_ref[:, cols] = attn.astype(BF16)
            y_ref[:, cols] = (attn * silu_a[:, cols]).astype(BF16)

        u = u_ref[...].astype(F32)
        halo = jnp.where(i > 0, halo_ref[...].astype(F32), 0.0)
        ext = jnp.concatenate([halo, u], axis=0)
        pos = i * tm + lax.broadcasted_iota(jnp.int32, (tm, 1), 0)
        gp = gp_ref[...].astype(F32)
        gated_scale = sc_ref[...] * (gp * jax.nn.sigmoid(gp))
        for g, window in enumerate(POOL_WINDOWS):
            cols = slice(g * POOL_GROUP_DIM, (g + 1) * POOL_GROUP_DIM)
            sums = _window_sums(ext[:, cols], window, backward=False)[POOL_HALO:, :]
            count = jnp.minimum(pos + 1, window).astype(F32)
            pooled = (sums / count - u[:, cols]).astype(BF16)
            pooled_ref[:, cols] = pooled
            pre = _dot_nn(pooled, wp_ref[g])
            out_cols = slice(D_ATTN + g * POOL_GROUP_DIM, D_ATTN + (g + 1) * POOL_GROUP_DIM)
            mix_ref[:, out_cols] = pre.astype(BF16)
            y_ref[:, out_cols] = (pre * gated_scale[:, cols]).astype(BF16)

    row = lambda width, cb=0: pl.BlockSpec((tm, width), lambda i: (i, cb))
    pat = lambda d, width: pl.BlockSpec((d, tm // d, width), lambda i: (0, i, 0))
    return _pallas(
        body, name="mix_gate", grid=(seq // tm,),
        in_specs=[row(D_ATTN), pat(d4, D_ATTN), pat(d16, D_ATTN),
                  row(STAT_LANES), pat(d4, STAT_LANES), pat(d16, STAT_LANES),
                  row(D_POOL),
                  pl.BlockSpec((POOL_HALO, D_POOL), lambda i: (jnp.maximum(i * halo_blocks - 1, 0), 0)),
                  row(D_ATTN, 1), row(D_POOL, 2),
                  pl.BlockSpec((len(POOL_WINDOWS), POOL_GROUP_DIM, POOL_GROUP_DIM), lambda i: (0, 0, 0)),
                  pl.BlockSpec((1, D_POOL), lambda i: (0, 0))],
        out_specs=[row(D_MODEL), row(D_MODEL), row(STAT_LANES), row(D_POOL)],
        out_shape=[jax.ShapeDtypeStruct((seq, D_MODEL), BF16), jax.ShapeDtypeStruct((seq, D_MODEL), BF16),
                   jax.ShapeDtypeStruct((seq, STAT_LANES), F32), jax.ShapeDtypeStruct((seq, D_POOL), BF16)],
        scratch_shapes=[pltpu.VMEM((N_HEADS, tm, HEAD_DIM), F32), pltpu.VMEM((N_HEADS, tm, HEAD_DIM), F32),
                        pltpu.VMEM((1, tm, STAT_LANES), F32), pltpu.VMEM((1, tm, STAT_LANES), F32)],
        compiler_params=_params(("parallel",), 48),
    )(o_list[0][0], o_list[1], o_list[2], st_list[0][0], st_list[1], st_list[2],
      hug, hug, hug, hug, w_pool_g, pool_scale)


def _out_proj_loss(y, w_out_g, x, target, gain, bias):
    seq = x.shape[0]
    tm = 512

    def body(y_ref, w_ref, x_ref, t_ref, g_ref, b_ref, dz_ref, dzb_ref, gg_ref, gb_ref, loss_ref):
        @pl.when(pl.program_id(0) == 0)
        def _():
            gg_ref[...] = jnp.zeros_like(gg_ref)
            gb_ref[...] = jnp.zeros_like(gb_ref)
            loss_ref[...] = jnp.zeros_like(loss_ref)

        halves = [slice(0, tm // 2), slice(tm // 2, tm)]
        projected = [_dot_nn(y_ref[rows, :], w_ref[...]) for rows in halves]
        for rows, out in zip(halves, projected):
            z = DEEPNORM_ALPHA * x_ref[rows, :] + out
            mu = jnp.mean(z, axis=-1, keepdims=True)
            zc = z - mu
            rstd = lax.rsqrt(jnp.mean(zc * zc, axis=-1, keepdims=True) + LN_EPS)
            xhat = zc * rstd
            gain_v = g_ref[...]
            diff = xhat * gain_v + b_ref[...] - t_ref[rows, :]
            sq = _fold_rows(diff * diff)
            part = sq[:, :128]
            for k in range(1, D_MODEL // 128):
                part = part + sq[:, k * 128:(k + 1) * 128]
            loss_ref[...] += part
            dln = diff * (1.0 / D_MODEL)
            gg_ref[...] += _fold_rows(dln * xhat)
            gb_ref[...] += _fold_rows(dln)
            dxhat = dln * gain_v
            dz = rstd * (dxhat - jnp.mean(dxhat, axis=-1, keepdims=True)
                         - xhat * jnp.mean(dxhat * xhat, axis=-1, keepdims=True))
            dz_ref[rows, :] = dz
            dzb_ref[rows, :] = dz.astype(BF16)

    row = lambda: pl.BlockSpec((tm, D_MODEL), lambda i: (i, 0))
    vec = lambda: pl.BlockSpec((1, D_MODEL), lambda i: (0, 0))
    acc = lambda width: pl.BlockSpec((8, width), lambda i: (0, 0))
    return _pallas(
        body, name="out_proj_loss", grid=(seq // tm,),
        in_specs=[row(), pl.BlockSpec((D_MODEL, D_MODEL), lambda i: (0, 0), pipeline_mode=pl.Buffered(1)),
                  row(), row(), vec(), vec()],
        out_specs=[row(), row(), acc(D_MODEL), acc(D_MODEL), acc(128)],
        out_shape=[jax.ShapeDtypeStruct((seq, D_MODEL), F32), jax.ShapeDtypeStruct((seq, D_MODEL), BF16),
                   jax.ShapeDtypeStruct((8, D_MODEL), F32), jax.ShapeDtypeStruct((8, D_MODEL), F32),
                   jax.ShapeDtypeStruct((8, 128), F32)],
        compiler_params=_params(("arbitrary",), 56),
    )(y, w_out_g.reshape(D_MODEL, D_MODEL), x, target, gain, bias)


def _dy_gate_bwd(dzb, w_out_g, hug, mixpre, pool_scale, lse_all):
    seq = dzb.shape[0]
    tm = 256
    d4, d16 = DILATIONS[1], DILATIONS[2]

    def body(dz_ref, w_ref, ga_ref, gp_ref, mix_ref, sc_ref, lse_ref,
             dh_ref, dpo_ref, do1_ref, do4_ref, do16_ref, st1_ref, st4_ref, st16_ref, da_ref, st_ref):
        dy = _dot_nt(dz_ref[...], w_ref[...])
        ga = ga_ref[...].astype(F32)
        sig = jax.nn.sigmoid(ga)
        attn = mix_ref[:, :D_ATTN].astype(F32)
        dya = dy[:, :D_ATTN]
        dattn = dya * (ga * sig)
        dh_ref[:, :D_ATTN] = (dya * attn * (sig * (1.0 + ga * (1.0 - sig)))).astype(BF16)
        _store_slabs(da_ref, dattn)
        lane = lax.broadcasted_iota(jnp.int32, (tm, STAT_LANES), 1)
        stats = lse_ref[...]
        prod = dattn * attn
        for h in range(N_HEADS):
            delta = jnp.sum(prod[:, h * HEAD_DIM:(h + 1) * HEAD_DIM], axis=-1, keepdims=True)
            stats = jnp.where(lane == N_HEADS + h, delta, stats)
        st_ref[0] = stats
        do1_ref[...] = dattn.astype(BF16)
        st1_ref[...] = stats
        _to_pattern(da_ref, do4_ref, d4, BF16)
        _to_pattern(da_ref, do16_ref, d16, BF16)
        _to_pattern(st_ref, st4_ref, d4, F32)
        _to_pattern(st_ref, st16_ref, d16, F32)

        gp = gp_ref[...].astype(F32)
        sig = jax.nn.sigmoid(gp)
        dyp = dy[:, D_ATTN:]
        dpo_ref[...] = (dyp * (gp * sig)).astype(BF16)
        dh_ref[:, D_ATTN:] = (dyp * (mix_ref[:, D_ATTN:].astype(F32) * sc_ref[...])
                              * (sig * (1.0 + gp * (1.0 - sig)))).astype(BF16)

    row = lambda width, cb=0: pl.BlockSpec((tm, width), lambda i: (i, cb))
    pat = lambda d, width: pl.BlockSpec((d, tm // d, width), lambda i: (0, i, 0))
    pat_shape = lambda d, width, dtype: jax.ShapeDtypeStruct((d, seq // d, width), dtype)
    outs = _pallas(
        body, name="dy_gate_bwd", grid=(seq // tm,),
        in_specs=[row(D_MODEL), pl.BlockSpec((D_MODEL, D_MODEL), lambda i: (0, 0)),
                  row(D_ATTN, 1), row(D_POOL, 2), row(D_MODEL), pl.BlockSpec((1, D_POOL), lambda i: (0, 0)),
                  row(STAT_LANES)],
        out_specs=[row(D_MODEL, D_IN // D_MODEL - 1), row(D_POOL),
                   row(D_ATTN), pat(d4, D_ATTN), pat(d16, D_ATTN),
                   row(STAT_LANES), pat(d4, STAT_LANES), pat(d16, STAT_LANES)],
        out_shape=[jax.ShapeDtypeStruct((seq, D_IN), BF16), jax.ShapeDtypeStruct((seq, D_POOL), BF16),
                   jax.ShapeDtypeStruct((seq, D_ATTN), BF16), pat_shape(d4, D_ATTN, BF16), pat_shape(d16, D_ATTN, BF16),
                   jax.ShapeDtypeStruct((seq, STAT_LANES), F32), pat_shape(d4, STAT_LANES, F32),
                   pat_shape(d16, STAT_LANES, F32)],
        scratch_shapes=[pltpu.VMEM((N_HEADS, tm, HEAD_DIM), F32), pltpu.VMEM((1, tm, STAT_LANES), F32)],
        compiler_params=_params(("parallel",), 48),
    )(dzb, w_out_g.reshape(D_MODEL, D_MODEL), hug, hug, mixpre, pool_scale, lse_all)
    dh, dpo, do1, do4, do16, st1, st4, st16 = outs
    return dh, dpo, [do1[None], do4, do16], [st1[None], st4, st16]


def _pool_bwd(dh, dpo, mixpre, pooled, w_pool_g, pool_scale):
    seq = dpo.shape[0]
    tm = 256
    halo_blocks = tm // POOL_HALO
    last = seq // tm - 1
    n_groups = len(POOL_WINDOWS)

    def body(dh_in_ref, dpo_ref, halo_ref, pre_ref, pooled_ref, wp_ref, sc_ref, du_ref, gw_ref, gs_ref):
        i = pl.program_id(0)

        @pl.when(i == 0)
        def _():
            gw_ref[...] = jnp.zeros_like(gw_ref)
            gs_ref[...] = jnp.zeros_like(gs_ref)

        dpo = dpo_ref[...].astype(F32)
        scale = sc_ref[...]
        gs_ref[...] += _fold_rows(dpo * pre_ref[...].astype(F32))
        halo = jnp.where(i < last, halo_ref[...].astype(F32), 0.0)
        dpw = (jnp.concatenate([dpo, halo], axis=0) * scale).astype(BF16)
        pos = i * tm + lax.broadcasted_iota(jnp.int32, (tm + POOL_HALO, 1), 0)
        for g, window in enumerate(POOL_WINDOWS):
            cols = slice(g * POOL_GROUP_DIM, (g + 1) * POOL_GROUP_DIM)
            dpw_g = dpw[:, cols]
            gw_ref[g] += _dot_tn(pooled_ref[:, cols], dpw_g[:tm, :])
            dpooled = _dot_nt(dpw_g, wp_ref[g])
            count = jnp.minimum(pos + 1, window).astype(F32)
            sums = _window_sums(dpooled / count, window, backward=True)
            du_ref[:, cols] = (sums[:tm, :] - dpooled[:tm, :]).astype(BF16)

    row = lambda width, cb=0: pl.BlockSpec((tm, width), lambda i: (i, cb))
    return _pallas(
        body, name="pool_bwd", grid=(seq // tm,),
        in_specs=[ANY, row(D_POOL),
                  pl.BlockSpec((POOL_HALO, D_POOL),
                               lambda i: (jnp.minimum((i + 1) * halo_blocks, seq // POOL_HALO - 1), 0)),
                  row(D_POOL, 1), row(D_POOL),
                  pl.BlockSpec((n_groups, POOL_GROUP_DIM, POOL_GROUP_DIM), lambda i: (0, 0, 0)),
                  pl.BlockSpec((1, D_POOL), lambda i: (0, 0))],
        out_specs=[row(D_POOL, D_QKV // D_POOL),
                   pl.BlockSpec((n_groups, POOL_GROUP_DIM, POOL_GROUP_DIM), lambda i: (0, 0, 0)),
                   pl.BlockSpec((8, D_POOL), lambda i: (0, 0))],
        out_shape=[jax.ShapeDtypeStruct(dh.shape, dh.dtype),
                   jax.ShapeDtypeStruct((n_groups, POOL_GROUP_DIM, POOL_GROUP_DIM), F32),
                   jax.ShapeDtypeStruct((8, D_POOL), F32)],
        input_output_aliases={0: 0},
        compiler_params=_params(("arbitrary",), 40),
    )(dh, dpo, dpo, mixpre, pooled, w_pool_g, pool_scale)


def _sum_patterns(dh, parts, tabs, unrotate, col_block, name, comm=None):
    seq = dh.shape[0]
    tm, tn = 256, D_ATTN
    per = D_ATTN // tn
    d4, d16 = DILATIONS[1], DILATIONS[2]

    def body(dh_in_ref, a1_ref, a4_ref, a16_ref, ct_ref, up_ref, down_ref, o_ref, n4_ref, n16_ref):
        _from_pattern(a4_ref, n4_ref, d4)
        _from_pattern(a16_ref, n16_ref, d16)
        for s in range(tn // HEAD_DIM):
            cols = slice(s * HEAD_DIM, (s + 1) * HEAD_DIM)
            tot = a1_ref[:, cols].astype(F32) + n4_ref[s] + n16_ref[s]
            if unrotate:
                tot = _rotate_heads(tot, ct_ref[...], -up_ref[...], -down_ref[...])
            o_ref[:, cols] = tot.astype(BF16)

    tab = pl.BlockSpec((tm, HEAD_DIM), lambda i, j: (i, 0))
    pat = lambda d: pl.BlockSpec((d, tm // d, tn), lambda i, j: (0, i, j))
    (dh,), exchanged = _call(
        body, name=name, grid=(seq // tm, per),
        in_specs=[ANY, pl.BlockSpec((tm, tn), lambda i, j: (i, j)), pat(d4), pat(d16), tab, tab, tab],
        out_specs=[pl.BlockSpec((tm, tn), lambda i, j: (i, col_block * per + j))],
        out_shape=[jax.ShapeDtypeStruct(dh.shape, dh.dtype)],
        scratch_shapes=[pltpu.VMEM((tn // HEAD_DIM, tm, HEAD_DIM), F32), pltpu.VMEM((tn // HEAD_DIM, tm, HEAD_DIM), F32)],
        semantics=("parallel", "parallel"), vmem_mib=32, args=(dh, parts[0][0], parts[1], parts[2], *tabs),
        aliases={0: 0}, comm=comm)
    return dh, exchanged


def _grad_w_in(x, dh, half, name, comm=None):
    seq = x.shape[0]
    ts, td, te = 2048, D_MODEL // 2, SHARD_IN

    def body(half_ref, x_ref, dh_ref, o_ref):
        k = pl.program_id(1)
        part = _dot_tn(x_ref[...].astype(BF16), dh_ref[...])

        @pl.when(k == 0)
        def _():
            o_ref[...] = part

        @pl.when(k > 0)
        def _():
            o_ref[...] += part

    (g,), exchanged = _call(
        body, name=name, grid=(N_SHARDS, seq // ts),
        in_specs=[pl.BlockSpec((ts, td), lambda e, k, half_ref: (k, half_ref[0])),
                  pl.BlockSpec((ts, te), lambda e, k, half_ref: (k, e))],
        out_specs=[pl.BlockSpec((None, td, te), lambda e, k, half_ref: (e, 0, 0))],
        out_shape=[jax.ShapeDtypeStruct((N_SHARDS, td, te), F32)],
        scratch_shapes=[], semantics=("parallel", "arbitrary"), vmem_mib=56, args=(x, dh), comm=comm,
        prefetch=(half,))
    return g, exchanged


def _grad_w_out(y, dzb):
    seq = y.shape[0]
    ts, te = 2048, 1024

    def body(y_ref, dz_ref, o_ref):
        k = pl.program_id(1)
        part = _dot_tn(y_ref[...], dz_ref[...])

        @pl.when(k == 0)
        def _():
            o_ref[...] = part

        @pl.when(k > 0)
        def _():
            o_ref[...] += part

    return _pallas(
        body, name="grad_w_out", grid=(D_MODEL // te, seq // ts),
        in_specs=[pl.BlockSpec((ts, te), lambda e, k: (k, e)), pl.BlockSpec((ts, D_MODEL), lambda e, k: (k, 0))],
        out_specs=pl.BlockSpec((te, D_MODEL), lambda e, k: (e, 0)),
        out_shape=jax.ShapeDtypeStruct((D_MODEL, D_MODEL), F32),
        compiler_params=_params(("parallel", "arbitrary"), 56),
    )(y, dzb)


GRAD_X_LATE_SHARDS = 1
GRAD_X_PARTIAL_ROWS = 512


def _grad_x_partial(dh, w_in_g, dz, first, tiles, prev=None, comm=None):
    seq = dh.shape[0]
    tm, tk = GRAD_X_PARTIAL_ROWS, SHARD_IN

    def body(*refs):
        dh_ref, w_ref, dz_ref, o_ref = refs[-4:]
        k = pl.program_id(1)
        part = _dot_nt(dh_ref[...], w_ref[...])

        @pl.when(k == 0)
        def _():
            o_ref[...] = DEEPNORM_ALPHA * dz_ref[...] + part

        @pl.when(k > 0)
        def _():
            o_ref[...] += part

    carried = [] if prev is None else [prev]
    row = pl.BlockSpec((tm, D_MODEL), lambda i, k: (i + first, 0))
    (partial,), exchanged = _call(
        body, name="grad_x_partial_%d" % first, grid=(tiles, N_SHARDS - GRAD_X_LATE_SHARDS),
        in_specs=[ANY] * len(carried) + [
            pl.BlockSpec((tm, tk), lambda i, k: (i + first, k)),
            pl.BlockSpec((None, D_MODEL, tk), lambda i, k: (k, 0, 0)), row],
        out_specs=[row],
        out_shape=[jax.ShapeDtypeStruct((seq, D_MODEL), F32)],
        scratch_shapes=[], semantics=("parallel", "arbitrary"), vmem_mib=48, args=(*carried, dh, w_in_g, dz),
        aliases={0: 0} if carried else None, comm=comm)
    return partial, exchanged


def _grad_x_final(dh, w_in_g, partial):
    seq = dh.shape[0]
    tm, tk = 512, SHARD_IN
    k0 = N_SHARDS - GRAD_X_LATE_SHARDS

    def body(dh_ref, w_ref, p_ref, o_ref):
        k = pl.program_id(1)
        part = _dot_nt(dh_ref[...], w_ref[...])

        @pl.when(k == 0)
        def _():
            o_ref[...] = p_ref[...] + part

        @pl.when(k > 0)
        def _():
            o_ref[...] += part

    row = pl.BlockSpec((tm, D_MODEL), lambda i, k: (i, 0))
    return _pallas(
        body, name="grad_x_final", grid=(seq // tm, GRAD_X_LATE_SHARDS),
        in_specs=[pl.BlockSpec((tm, tk), lambda i, k: (i, k + k0)),
                  pl.BlockSpec((None, D_MODEL, tk), lambda i, k: (k + k0, 0, 0)), row],
        out_specs=row, out_shape=jax.ShapeDtypeStruct((seq, D_MODEL), F32),
        compiler_params=_params(("parallel", "arbitrary"), 48),
    )(dh, w_in_g, partial)


def _pool_weight(w_pool_sh):
    n_groups = len(POOL_WINDOWS)
    shard_c = POOL_GROUP_DIM // N_SHARDS
    return (w_pool_sh.reshape(N_SHARDS, n_groups, shard_c, POOL_GROUP_DIM).transpose(1, 0, 2, 3)
            .reshape(n_groups, POOL_GROUP_DIM, POOL_GROUP_DIM))


def _pool_grad_pieces(g_w_pool):
    n_groups = len(POOL_WINDOWS)
    half_c = POOL_GROUP_DIM // N_SHARDS // 2
    return (g_w_pool.reshape(n_groups, N_SHARDS, 2, half_c, POOL_GROUP_DIM).transpose(1, 2, 0, 3, 4)
            .reshape(N_SHARDS, 2, n_groups * half_c, POOL_GROUP_DIM))


def _step(x, target, w_bufs, pool_scale, gain, bias, place):
    seq = x.shape[0]
    tabs = _rope_tables(seq)
    core, chip_core, onward, plan = place
    qkv, hug, w_in_g, w_out_g, w_pool_sh = _in_proj_gathering(x, w_bufs, tabs, plan)
    o_list, st_list = [], []
    for p, dil in enumerate(DILATIONS):
        o, st = _attn_fwd(qkv[p], "attn_fwd_d%d" % dil)
        o_list.append(o)
        st_list.append(st)
    w_pool_g = _pool_weight(w_pool_sh)
    y, mixpre, lse_all, pooled = _mix_gate(o_list, st_list, hug, w_pool_g, pool_scale)
    dz, dzb, gain_part, bias_part, loss_part = _out_proj_loss(y, w_out_g, x, target, gain, bias)
    dh, dpo, do_list, stat_list = _dy_gate_bwd(dzb, w_out_g, hug, mixpre, pool_scale, lse_all)
    g_w_out = _grad_w_out(y, dzb)
    dh, g_w_pool, scale_part = _pool_bwd(dh, dpo, mixpre, pooled, w_pool_g, pool_scale)
    small = jnp.concatenate([scale_part, gain_part, bias_part, loss_part], axis=1)
    early = [g_w_out.reshape(N_SHARDS, 2, D_MODEL // (2 * N_SHARDS), D_MODEL), _pool_grad_pieces(g_w_pool)]

    bwd = lambda p, comm: _attn_bwd(qkv[p], do_list[p], stat_list[p], "attn_bwd_d%d" % DILATIONS[p], comm)
    part_a, recv = bwd(0, _exchange_halves(early))
    sums = [_add_own_half(g, r, core, "add_own_half_%d" % a) for a, (g, r) in enumerate(zip(early, recv))]
    part_b, recv = bwd(1, _scatter_to_chips([s[1] for s in sums]))
    bufs = [_add_chips(s[0], r, chip_core, "add_chips_%d" % a) for a, (s, r) in enumerate(zip(sums, recv))]
    part_c, reduced = bwd(2, _share_with_sibling(bufs))
    parts = [part_a, part_b, part_c]
    dh, gathered = _sum_patterns(dh, [t[0] for t in parts], tabs, True, 0, "sum_dq", _gather_small(small))
    dh, _ = _sum_patterns(dh, [t[1] for t in parts], tabs, True, 1, "sum_dk")
    dh, _ = _sum_patterns(dh, [t[2] for t in parts], tabs, False, 2, "sum_dv")

    give, _ = _grad_w_in(x, dh, 1 - core, "grad_w_in_give")
    keep, recv = _grad_w_in(x, dh, core, "grad_w_in_keep", _send_to_sibling([give]))
    total, total_b = _add_pair(keep, recv[0], "add_own_half_w_in")
    n_tiles = seq // GRAD_X_PARTIAL_ROWS
    tiles = 3 * n_tiles // 8
    part, relayed = _grad_x_partial(dh, w_in_g, dz, 0, tiles, None, _relay_diagonal(total_b))
    total_b = _fold_relayed(total, total_b, relayed[0], onward)
    part, recv = _grad_x_partial(dh, w_in_g, dz, tiles, n_tiles - tiles, part, _scatter_to_neighbours(total_b))
    buf = _add_chips(total, recv[0], chip_core, "add_chips_w_in")
    g_x = _grad_x_final(dh, w_in_g, part)
    g_w_in = _run_exchange(_share_with_sibling([buf]), "share_w_in")[0]
    return g_x, g_w_in, reduced[0], reduced[1], small, gathered[0]


def _exchange_halves(grads):
    n = len(grads)

    def copies(src, dst, sems):
        x, y, c, _ = _mesh_place()
        return [_remote(src[a].at[j, 1 - c], dst[a].at[j], sems[0].at[a, j], sems[1].at[a, j], (x, y, 1 - c))
                for a in range(n) for j in range(N_SHARDS)]

    def start(src, dst, sems):
        for cp in copies(src, dst, sems):
            cp.start()

    def finish(src, dst, sems):
        for cp in copies(src, dst, sems):
            cp.wait()

    return _Exchange(grads, [jax.ShapeDtypeStruct((N_SHARDS,) + g.shape[2:], g.dtype) for g in grads], {},
                     [pltpu.SemaphoreType.DMA((n, N_SHARDS))] * 2, start, finish)


def _add_own_half(grad, recv, core, name):
    _, _, r, c = grad.shape
    tr = min(r, 256)

    def body(core_ref, g_ref, r_ref, o_ref, ob_ref):
        tot = g_ref[...] + r_ref[...]
        o_ref[...] = tot
        ob_ref[...] = tot.astype(BF16)

    out = pl.BlockSpec((None, tr, c), lambda j, i, core_ref: (j, i, 0))
    return _pallas(
        body, name=name,
        grid_spec=pltpu.PrefetchScalarGridSpec(
            num_scalar_prefetch=1, grid=(N_SHARDS, r // tr),
            in_specs=[pl.BlockSpec((None, None, tr, c), lambda j, i, core_ref: (j, core_ref[0], i, 0)),
                      pl.BlockSpec((None, tr, c), lambda j, i, core_ref: (j, i, 0))],
            out_specs=[out, out]),
        out_shape=[jax.ShapeDtypeStruct((N_SHARDS, r, c), F32), jax.ShapeDtypeStruct((N_SHARDS, r, c), BF16)],
        compiler_params=_params(("parallel", "parallel"), 32),
    )(core, grad, recv)


def _send_to_sibling(arrays):
    n = len(arrays)

    def copies(src, dst, sems):
        x, y, c, _ = _mesh_place()
        return [_remote(src[a], dst[a], sems[0].at[a], sems[1].at[a], (x, y, 1 - c)) for a in range(n)]

    def start(src, dst, sems):
        for cp in copies(src, dst, sems):
            cp.start()

    def finish(src, dst, sems):
        for cp in copies(src, dst, sems):
            cp.wait()

    return _Exchange(arrays, [jax.ShapeDtypeStruct(t.shape, t.dtype) for t in arrays], {},
                     [pltpu.SemaphoreType.DMA((n,))] * 2, start, finish)


def _add_pair(a, b, name):
    _, r, c = a.shape
    tr = min(r, 256)

    def body(a_ref, b_ref, o_ref, ob_ref):
        tot = a_ref[...] + b_ref[...]
        o_ref[...] = tot
        ob_ref[...] = tot.astype(BF16)

    spec = pl.BlockSpec((None, tr, c), lambda j, i: (j, i, 0))
    return _pallas(
        body, name=name, grid=(N_SHARDS, r // tr), in_specs=[spec, spec], out_specs=[spec, spec],
        out_shape=[jax.ShapeDtypeStruct(a.shape, F32), jax.ShapeDtypeStruct(a.shape, BF16)],
        compiler_params=_params(("parallel", "parallel"), 32),
    )(a, b)


def _scatter_to_chips(sums):
    n = len(sums)

    def copies(src, dst, sems):
        x, y, c, chips = _mesh_place()
        return [_remote(src[a].at[2 * cx + cy], dst[a].at[k], sems[0].at[a, k], sems[1].at[a, k], (cx, cy, c))
                for a in range(n) for k, (cx, cy) in enumerate(chips)]

    def start(src, dst, sems):
        for cp in copies(src, dst, sems):
            cp.start()

    def finish(src, dst, sems):
        for cp in copies(src, dst, sems):
            cp.wait()

    return _Exchange(sums, [jax.ShapeDtypeStruct((3,) + s.shape[1:], s.dtype) for s in sums], {},
                     [pltpu.SemaphoreType.DMA((n, 3))] * 2, start, finish)


def _add_chips(sums, recv, chip_core, name):
    _, r, c = sums.shape
    n_recv = recv.shape[0]
    tr = min(r, 256)

    def body(cc_ref, s_ref, r_ref, o_ref):
        tot = s_ref[...]
        for k in range(n_recv):
            tot = tot + r_ref[k].astype(F32)
        o_ref[...] = tot

    return _pallas(
        body, name=name,
        grid_spec=pltpu.PrefetchScalarGridSpec(
            num_scalar_prefetch=1, grid=(r // tr,),
            in_specs=[pl.BlockSpec((None, tr, c), lambda i, cc_ref: (cc_ref[0], i, 0)),
                      pl.BlockSpec((n_recv, tr, c), lambda i, cc_ref: (0, i, 0))],
            out_specs=pl.BlockSpec((None, tr, c), lambda i, cc_ref: (cc_ref[1], i, 0))),
        out_shape=jax.ShapeDtypeStruct((2, r, c), F32),
        compiler_params=_params(("parallel",), 32),
    )(chip_core, sums, recv)


def _relay_diagonal(sums_b):
    def copy(src, dst, sems):
        x, y, c, _ = _mesh_place()
        diagonal = 2 * (1 - x) + (1 - y)
        return _remote(src[0].at[diagonal], dst[0], sems[0].at[0], sems[1].at[0], (x ^ (1 - c), y ^ c, c))

    def start(src, dst, sems):
        copy(src, dst, sems).start()

    def finish(src, dst, sems):
        copy(src, dst, sems).wait()

    return _Exchange([sums_b], [jax.ShapeDtypeStruct(sums_b.shape[1:], sums_b.dtype)], {},
                     [pltpu.SemaphoreType.DMA((1,))] * 2, start, finish)


def _fold_relayed(sums, sums_b, relayed, onward):
    _, r, c = sums.shape
    tr = min(r, 256)

    def body(on_ref, b_in_ref, s_ref, r_ref, o_ref):
        o_ref[...] = (s_ref[...] + r_ref[...].astype(F32)).astype(BF16)

    return _pallas(
        body, name="fold_relayed",
        grid_spec=pltpu.PrefetchScalarGridSpec(
            num_scalar_prefetch=1, grid=(r // tr,),
            in_specs=[ANY, pl.BlockSpec((None, tr, c), lambda i, on_ref: (on_ref[0], i, 0)),
                      pl.BlockSpec((tr, c), lambda i, on_ref: (i, 0))],
            out_specs=pl.BlockSpec((None, tr, c), lambda i, on_ref: (on_ref[0], i, 0))),
        out_shape=jax.ShapeDtypeStruct(sums_b.shape, sums_b.dtype),
        input_output_aliases={1: 0},
        compiler_params=_params(("parallel",), 32),
    )(onward, sums_b, sums, relayed)


def _scatter_to_neighbours(sums_b):
    def copies(src, dst, sems):
        x, y, c, chips = _mesh_place()
        return [_remote(src[0].at[2 * cx + cy], dst[0].at[k], sems[0].at[k], sems[1].at[k], (cx, cy, c))
                for k, (cx, cy) in enumerate(chips[:2])]

    def start(src, dst, sems):
        for cp in copies(src, dst, sems):
            cp.start()

    def finish(src, dst, sems):
        for cp in copies(src, dst, sems):
            cp.wait()

    return _Exchange([sums_b], [jax.ShapeDtypeStruct((2,) + sums_b.shape[1:], sums_b.dtype)], {},
                     [pltpu.SemaphoreType.DMA((2,))] * 2, start, finish)


def _share_with_sibling(bufs):
    n = len(bufs)

    def copies(dst, sems, half):
        x, y, c, _ = _mesh_place()
        h = c if half == "mine" else 1 - c
        return [_remote(dst[a].at[h], dst[a].at[h], sems[0].at[a], sems[1].at[a], (x, y, 1 - c)) for a in range(n)]

    def start(ins, dst, sems):
        for cp in copies(dst, sems, "mine"):
            cp.start()

    def finish(ins, dst, sems):
        for cp in copies(dst, sems, "theirs"):
            cp.wait_recv()
        for cp in copies(dst, sems, "mine"):
            cp.wait_send()

    return _Exchange(bufs, [jax.ShapeDtypeStruct(b.shape, b.dtype) for b in bufs], {a: a for a in range(n)},
                     [pltpu.SemaphoreType.DMA((n,))] * 2, start, finish)


def _adam_math(w, g, m, v):
    m = ADAM_B1 * m + (1.0 - ADAM_B1) * g
    v = ADAM_B2 * v + (1.0 - ADAM_B2) * (g * g)
    m_hat = m / (1.0 - ADAM_B1 ** ADAM_STEP)
    v_hat = v / (1.0 - ADAM_B2 ** ADAM_STEP)
    delta = -ADAM_LR * (m_hat / (jnp.sqrt(v_hat) + ADAM_EPS) + ADAM_WD * w)
    return delta, m, v


def _gather_small(small):
    def peers():
        x, y, c, _ = _mesh_place()
        return [(x ^ ((r >> 2) & 1), y ^ ((r >> 1) & 1), c ^ (r & 1)) for r in range(1, 8)], 4 * x + 2 * y + c

    def start(src, dst, sems):
        to, me = peers()
        for r, peer in enumerate(to):
            _remote(src[0], dst[0].at[me], sems[0].at[r], sems[1].at[r], peer).start()

    def finish(src, dst, sems):
        to, me = peers()
        for r, (px, py, pc) in enumerate(to):
            theirs = dst[0].at[4 * px + 2 * py + pc]
            _remote(theirs, theirs, sems[0].at[r], sems[1].at[r], (px, py, pc)).wait_recv()
        for r, peer in enumerate(to):
            _remote(src[0], dst[0].at[me], sems[0].at[r], sems[1].at[r], peer).wait_send()

    return _Exchange([small], [jax.ShapeDtypeStruct((8,) + small.shape, small.dtype)], {},
                     [pltpu.SemaphoreType.DMA((7,))] * 2, start, finish)


def _small_adamw(gathered, small, me, w_vec, m_vec, v_vec):
    n_par = w_vec.shape[1]

    def body(me_ref, a_ref, s_ref, w_ref, m_ref, v_ref, loss_ref, g_ref, d_ref, nm_ref, nv_ref):
        mine = s_ref[...]
        tot = jnp.where(me_ref[0] == 0, mine, a_ref[0])
        for d in range(1, 8):
            tot = tot + jnp.where(me_ref[0] == d, mine, a_ref[d])
        tot = jnp.sum(tot, axis=0, keepdims=True)
        sq = jnp.sum(tot[:, n_par:], axis=1, keepdims=True)
        loss_ref[...] = jnp.broadcast_to(sq * (0.5 / D_MODEL), loss_ref.shape)
        g = tot[:, :n_par]
        g_ref[...] = g
        d_ref[...], nm_ref[...], nv_ref[...] = _adam_math(w_ref[...], g, m_ref[...], v_ref[...])

    vm = pl.BlockSpec(memory_space=pltpu.VMEM)
    vec = jax.ShapeDtypeStruct((1, n_par), F32)
    return pl.pallas_call(
        body, name="small_adamw",
        grid_spec=pltpu.PrefetchScalarGridSpec(num_scalar_prefetch=1, grid=(), in_specs=[vm] * 5, out_specs=[vm] * 5),
        out_shape=[jax.ShapeDtypeStruct((1, 128), F32), vec, vec, vec, vec],
    )(me, gathered, small, w_vec, m_vec, v_vec)


def _adamw(w, g, m, v, name):
    r, c = w.shape
    tr = min(r, 256)

    def body(w_ref, g_ref, m_ref, v_ref, go_ref, d_ref, nm_ref, nv_ref):
        g = g_ref[...]
        go_ref[...] = g
        d_ref[...], nm_ref[...], nv_ref[...] = _adam_math(w_ref[...], g, m_ref[...], v_ref[...])

    spec = pl.BlockSpec((tr, c), lambda i: (i, 0))
    shape = jax.ShapeDtypeStruct((r, c), F32)
    return _pallas(
        body, name=name, grid=(r // tr,),
        in_specs=[spec] * 4, out_specs=[spec] * 4, out_shape=[shape] * 4,
        compiler_params=_params(("parallel",), 48),
    )(w, g, m, v)


def kernel(x, w_in, w_pool, pool_scale, w_out, ln_gain, ln_bias, loss_target, m_w_in, m_w_pool, m_pool_scale, m_w_out, m_ln_gain, m_ln_bias, v_w_in, v_w_pool, v_pool_scale, v_w_out, v_ln_gain, v_ln_bias):
    xi, yi, ci = lax.axis_index("x"), lax.axis_index("y"), lax.axis_index("c")
    chip = (2 * xi + yi).astype(jnp.int32).reshape(1)
    core = ci.astype(jnp.int32).reshape(1)
    n_groups = len(POOL_WINDOWS)
    shard_c = w_pool.shape[2]

    w_in_b = _cast_bf16(w_in[0], chip, "cast_w_in", 256)
    w_out_b = _cast_bf16(w_out[0], chip, "cast_w_out", 256)
    w_pool_b = _cast_bf16(w_pool[0].reshape(n_groups * shard_c, POOL_GROUP_DIM), chip, "cast_w_pool", 256)

    chip_core = jnp.concatenate([chip, core])
    onward = (2 * (xi ^ ci) + (yi ^ (1 - ci))).astype(jnp.int32).reshape(1)
    g_x, full_in, full_out, full_pool, small, small_all = _step(
        x[0], loss_target[0], [w_in_b, w_out_b, w_pool_b], pool_scale, ln_gain, ln_bias,
        (core, chip_core, onward, _in_proj_plan(xi, yi)))
    half_c = shard_c // 2
    grad_w_in = full_in.reshape(D_MODEL, SHARD_IN)
    grad_w_out = full_out.reshape(D_MODEL // N_SHARDS, D_MODEL)
    grad_w_pool = (full_pool.reshape(2, n_groups, half_c, POOL_GROUP_DIM).transpose(1, 0, 2, 3)
                   .reshape(n_groups * shard_c, POOL_GROUP_DIM))

    grad_w_in, d_in, nm_in, nv_in = _adamw(w_in[0], grad_w_in, m_w_in[0], v_w_in[0], "adamw_w_in")
    grad_w_out, d_out, nm_out, nv_out = _adamw(w_out[0], grad_w_out, m_w_out[0], v_w_out[0], "adamw_w_out")
    flat = lambda t: t[0].reshape(n_groups * shard_c, POOL_GROUP_DIM)
    grad_w_pool, d_pool, nm_pool, nv_pool = _adamw(flat(w_pool), grad_w_pool, flat(m_w_pool), flat(v_w_pool),
                                                   "adamw_w_pool")

    cat = lambda a, b, c: jnp.concatenate([a, b, c], axis=1)
    me = (4 * xi + 2 * yi + ci).astype(jnp.int32).reshape(1)
    loss_v, g_vec, d_vec, nm_vec, nv_vec = _small_adamw(
        small_all, small, me, cat(pool_scale, ln_gain, ln_bias), cat(m_pool_scale, m_ln_gain, m_ln_bias),
        cat(v_pool_scale, v_ln_gain, v_ln_bias))

    def split(vec):
        return vec[:, :D_POOL], vec[:, D_POOL:D_POOL + D_MODEL], vec[:, D_POOL + D_MODEL:]

    g_scale, g_gain, g_bias = split(g_vec)
    d_scale, d_gain, d_bias = split(d_vec)
    nm_scale, nm_gain, nm_bias = split(nm_vec)
    nv_scale, nv_gain, nv_bias = split(nv_vec)
    pool_shape = w_pool.shape
    return (loss_v[0, 0], g_x[None],
            grad_w_in[None], grad_w_pool.reshape(pool_shape), g_scale, grad_w_out[None], g_gain, g_bias,
            d_in[None], d_pool.reshape(pool_shape), d_scale, d_out[None], d_gain, d_bias,
            nm_in[None], nm_pool.reshape(pool_shape), nm_scale, nm_out[None], nm_gain, nm_bias,
            nv_in[None], nv_pool.reshape(pool_shape), nv_scale, nv_out[None], nv_gain, nv_bias)
```

```python
import functools

import jax
import jax.numpy as jnp
from jax import lax
from jax.experimental import pallas as pl
from jax.experimental.pallas import tpu as pltpu

F32 = jnp.float32
BF16 = jnp.bfloat16
MESH = pl.DeviceIdType.MESH
ANY = pl.BlockSpec(memory_space=pl.ANY)

D_MODEL = 2048
D_ATTN = 1024
D_POOL = 1024
HEAD_DIM = 128
N_HEADS = 8
ROPE_DIM = 32
ROPE_THETA = 500000.0
DILATIONS = (1, 4, 16)
KEY_BLOCK = 128
CHUNK = 2 * KEY_BLOCK
STAT_LANES = 128
POOL_WINDOWS = (2, 4, 8, 16)
POOL_GROUP_DIM = 256
POOL_HALO = 16
D_QKV = 3 * D_ATTN
D_UG = D_POOL + D_MODEL
D_IN = D_QKV + D_UG
N_SHARDS = 4
SHARD_IN = D_IN // N_SHARDS
LN_EPS = 1e-5
DEEPNORM_ALPHA = 2.0 ** 0.25
ADAM_LR = 0.001
ADAM_B1 = 0.9
ADAM_B2 = 0.999
ADAM_EPS = 1e-08
ADAM_WD = 0.01
ADAM_STEP = 10
NEG = -1e30
MIB = 1024 * 1024


def _params(sem, vmem_mib):
    return pltpu.CompilerParams(dimension_semantics=sem, vmem_limit_bytes=vmem_mib * MIB)


def _pallas(body, **kwargs):
    pin = lambda s: pltpu.HBM(s.shape, s.dtype) if len(s.shape) >= 2 else s
    out_shape = kwargs.pop("out_shape")
    out_shape = [pin(s) for s in out_shape] if isinstance(out_shape, (list, tuple)) else pin(out_shape)
    call = pl.pallas_call(body, out_shape=out_shape, **kwargs)

    def run(*operands):
        return call(*[pltpu.with_memory_space_constraint(o, pltpu.HBM) if o.ndim >= 2 else o for o in operands])

    return run


class _Exchange:
    def __init__(self, operands, out_shape, aliases, sems, start, finish):
        self.operands, self.out_shape, self.aliases, self.sems = list(operands), list(out_shape), dict(aliases), list(sems)
        self.start, self.finish = start, finish


def _run_exchange(comm, name):
    n_in, n_out = len(comm.operands), len(comm.out_shape)

    def body(*refs):
        ins, outs, sems = refs[:n_in], refs[n_in:n_in + n_out], refs[n_in + n_out:]
        comm.start(ins, outs, sems)
        comm.finish(ins, outs, sems)

    return _pallas(
        body, name=name, in_specs=[ANY] * n_in, out_specs=[ANY] * n_out, out_shape=comm.out_shape,
        input_output_aliases=comm.aliases, scratch_shapes=comm.sems,
    )(*comm.operands)


def _call(body, *, name, grid, in_specs, out_specs, out_shape, scratch_shapes, semantics, vmem_mib, args,
          aliases=None, comm=None, prefetch=()):
    aliases = dict(aliases or {})
    n_pre, n_in, n_out, n_scr = len(prefetch), len(in_specs), len(out_specs), len(scratch_shapes)
    c_in, c_out = (len(comm.operands), len(comm.out_shape)) if comm else (0, 0)
    c_shapes, c_sems, c_operands = (comm.out_shape, comm.sems, comm.operands) if comm else ([], [], [])

    def hosted(*refs):
        pre, refs = refs[:n_pre], refs[n_pre:]
        a = n_in
        b = a + c_in
        c = b + n_out
        d = c + c_out
        e = d + n_scr
        if comm is None:
            body(*pre, *refs)
            return
        ids = [pl.program_id(k) for k in range(len(grid))]
        first = functools.reduce(jnp.logical_and, [i == 0 for i in ids])
        last = functools.reduce(jnp.logical_and, [i == g - 1 for i, g in zip(ids, grid)])

        @pl.when(first)
        def _():
            comm.start(refs[a:b], refs[c:d], refs[e:])

        body(*pre, *refs[:a], *refs[b:c], *refs[d:e])

        @pl.when(last)
        def _():
            comm.finish(refs[a:b], refs[c:d], refs[e:])

    if comm:
        semantics = ("arbitrary",) * len(grid)
        for i, o in comm.aliases.items():
            aliases[n_pre + n_in + i] = n_out + o
    outs = _pallas(
        hosted, name=name,
        grid_spec=pltpu.PrefetchScalarGridSpec(
            num_scalar_prefetch=n_pre, grid=grid, in_specs=list(in_specs) + [ANY] * c_in,
            out_specs=list(out_specs) + [ANY] * c_out, scratch_shapes=list(scratch_shapes) + c_sems),
        out_shape=list(out_shape) + c_shapes, input_output_aliases=aliases,
        compiler_params=_params(semantics, vmem_mib),
    )(*prefetch, *args, *c_operands)
    return list(outs[:n_out]), list(outs[n_out:])


def _dot_nn(a, b):
    return jnp.dot(a, b, preferred_element_type=F32)


def _dot_nt(a, b):
    return lax.dot_general(a, b, (((1,), (1,)), ((), ())), preferred_element_type=F32)


def _dot_tn(a, b):
    return lax.dot_general(a, b, (((0,), (0,)), ((), ())), preferred_element_type=F32)


def _fold_rows(a):
    r, c = a.shape
    return jnp.sum(a.reshape(r // 8, 8, c), axis=0)


def _cast_bf16(a, chip, name, rows):
    r, c = a.shape

    def body(chip_ref, a_ref, o_ref):
        o_ref[...] = a_ref[...].astype(BF16)

    return _pallas(
        body, name=name,
        grid_spec=pltpu.PrefetchScalarGridSpec(
            num_scalar_prefetch=1, grid=(r // rows,),
            in_specs=[pl.BlockSpec((rows, c), lambda i, chip_ref: (i, 0))],
            out_specs=pl.BlockSpec((None, rows, c), lambda i, chip_ref: (chip_ref[0], i, 0))),
        out_shape=jax.ShapeDtypeStruct((N_SHARDS, r, c), BF16),
        compiler_params=_params(("parallel",), 32),
    )(chip, a)


def _mesh_place():
    x, y, c = lax.axis_index("x"), lax.axis_index("y"), lax.axis_index("c")
    return x, y, c, [(1 - x, y), (x, 1 - y), (1 - x, 1 - y)]


def _remote(src, dst, send_sem, recv_sem, to):
    return pltpu.make_async_remote_copy(src_ref=src, dst_ref=dst, send_sem=send_sem, recv_sem=recv_sem,
                                        device_id=to, device_id_type=MESH)


def _rope_tables(seq):
    half = ROPE_DIM // 2
    inv_freq = ROPE_THETA ** (-(2.0 * jnp.arange(half, dtype=F32)) / ROPE_DIM)
    ang = jnp.arange(seq, dtype=jnp.int32).astype(F32)[:, None] * inv_freq[None, :]
    cos, sin = jnp.cos(ang), jnp.sin(ang)
    pad = jnp.zeros((seq, HEAD_DIM - ROPE_DIM), F32)
    zeros = jnp.zeros((seq, half), F32)
    c_tab = jnp.concatenate([cos, cos, pad + 1.0], axis=1)
    up_tab = jnp.concatenate([-sin, zeros, pad], axis=1)
    down_tab = jnp.concatenate([zeros, sin, pad], axis=1)
    return c_tab, up_tab, down_tab


def _rotate_heads(t, c_tab, up_tab, down_tab):
    outs = []
    for h in range(t.shape[1] // HEAD_DIM):
        th = t[:, h * HEAD_DIM:(h + 1) * HEAD_DIM]
        up = pltpu.roll(th, HEAD_DIM - ROPE_DIM // 2, axis=1)
        down = pltpu.roll(th, ROPE_DIM // 2, axis=1)
        outs.append(th * c_tab + up * up_tab + down * down_tab)
    return outs[0] if len(outs) == 1 else jnp.concatenate(outs, axis=1)


def _to_pattern(slabs_ref, dst_ref, dil, dtype):
    n_slabs, rows, _ = slabs_ref.shape
    for s in range(n_slabs):
        for r in range(dil):
            dst_ref[r, :, s * 128:(s + 1) * 128] = slabs_ref[s, pl.ds(r, rows // dil, dil), :].astype(dtype)


def _from_pattern(src_ref, slabs_ref, dil):
    n_slabs, rows, _ = slabs_ref.shape
    for s in range(n_slabs):
        for r in range(dil):
            slabs_ref[s, pl.ds(r, rows // dil, dil), :] = src_ref[r, :, s * 128:(s + 1) * 128].astype(F32)


def _store_slabs(slabs_ref, value):
    for s in range(slabs_ref.shape[0]):
        slabs_ref[s] = value[:, s * 128:(s + 1) * 128]


W_IN_CHUNKS = 4


def _in_proj_plan(x, y):
    shards = [2 * x + y, 2 * (1 - x) + y, 2 * x + (1 - y), 2 * (1 - x) + (1 - y)]
    last_row = jnp.int32(-2)

    def table(active, col_of):
        cols, rows = [], []
        first_col = functools.reduce(lambda acc, j: jnp.where(active[j], col_of(shards[j]), acc), reversed(range(4)),
                                     jnp.int32(0))
        held_col, seen = first_col, jnp.bool_(False)
        for j in range(4):
            cols.append(jnp.where(active[j], col_of(shards[j]), held_col))
            rows.append(jnp.where(active[j], -1, jnp.where(seen, last_row, 0)))
            held_col = jnp.where(active[j], col_of(shards[j]), held_col)
            seen = jnp.logical_or(seen, active[j])
        return cols, rows

    q_cols, q_rows = table([s < 2 for s in shards], lambda s: s)
    h_cols, h_rows = table([s >= 2 for s in shards], lambda s: s - 2)
    return jnp.stack([jnp.asarray(v, jnp.int32) for v in shards + q_cols + q_rows + h_cols + h_rows])


def _in_proj_gathering(x, w_bufs, tabs, plan):
    seq = x.shape[0]
    tm, tn = 512, SHARD_IN
    n_tiles = seq // tm
    heads = tn // HEAD_DIM
    k_heads_in_second = 2 * D_ATTN // HEAD_DIM - heads
    d4, d16 = DILATIONS[1], DILATIONS[2]
    DIAGONAL = 2
    chunk = D_MODEL // 2 // W_IN_CHUNKS
    early = [(0, D_MODEL // 2, q * chunk, chunk) for q in range(W_IN_CHUNKS)]
    late = [(a, w_bufs[a].shape[1] // 2, 0, w_bufs[a].shape[1] // 2) for a in (1, 2)]
    pieces = early + late
    early_ids, late_ids = range(len(early)), range(len(early), len(pieces))

    def body(plan_ref, x_ref, w_in_in, w_out_in, w_pool_in, c_ref, up_ref, down_ref,
             o1_ref, o4_ref, o16_ref, hug_ref, w_ref, w_out_ref, w_pool_ref,
             wbuf_ref, res_ref, w_sem, ici_send, ici_recv, d2d_send, d2d_recv):
        j, i = pl.program_id(0), pl.program_id(1)
        mx, my, mc, chips = _mesh_place()
        sibling = (mx, my, 1 - mc)
        gathered = (w_ref, w_out_ref, w_pool_ref)
        chip_of = lambda k: 2 * chips[k][0] + chips[k][1]

        def piece(n, chip, core):
            a, per_core, offset, size = pieces[n]
            return gathered[a].at[chip, pl.ds(core * per_core + offset, size)]

        def to_neighbour(k, n):
            mine = piece(n, 2 * mx + my, mc)
            return _remote(mine, mine, ici_send.at[n, k], ici_recv.at[n, k], (*chips[k], mc))

        def relay(n):
            theirs = piece(n, 2 * (mx ^ (1 - mc)) + (my ^ mc), mc)
            return _remote(theirs, theirs, ici_send.at[n, DIAGONAL], ici_recv.at[n, DIAGONAL], (mx ^ mc, my ^ (1 - mc), mc))

        def arrival(k, n):
            theirs = piece(n, chip_of(k), mc)
            return _remote(theirs, theirs, ici_send.at[n, k], ici_recv.at[n, k], (*chips[k], mc))

        def to_sibling(k, n, core):
            theirs = piece(n, chip_of(k), core)
            return _remote(theirs, theirs, d2d_send.at[n, k], d2d_recv.at[n, k], sibling)

        def take(k, ids):
            for n in ids:
                arrival(k, n).wait_recv()
                to_sibling(k, n, mc).start()

        def taken(k, ids):
            for n in ids:
                to_sibling(k, n, 1 - mc).wait_recv()

        first_tile = i == 0

        @pl.when(jnp.logical_and(j == 0, first_tile))
        def _():
            for n in range(len(pieces)):
                for k in range(DIAGONAL):
                    to_neighbour(k, n).start()

        @pl.when(jnp.logical_and(j == 1, first_tile))
        def _():
            take(0, early_ids)
            taken(0, early_ids)

        @pl.when(jnp.logical_and(j == 2, first_tile))
        def _():
            take(1, early_ids)
            for n in early_ids:
                relay(n).start()
            taken(1, early_ids)
            for k in range(DIAGONAL):
                take(k, late_ids)
            for n in late_ids:
                relay(n).start()
            for k in range(DIAGONAL):
                taken(k, late_ids)

        @pl.when(jnp.logical_and(j == 3, first_tile))
        def _():
            take(DIAGONAL, range(len(pieces)))
            taken(DIAGONAL, range(len(pieces)))

        shard = plan_ref[j]

        @pl.when(first_tile)
        def _():
            cp = pltpu.make_async_copy(w_ref.at[shard], wbuf_ref, w_sem)
            cp.start()
            cp.wait()

        xb = x_ref[...].astype(BF16)
        group = 4 * HEAD_DIM
        accs = [_dot_nn(xb, wbuf_ref[:, g * group:(g + 1) * group]) for g in range(tn // group)]

        def emit_qkv(rotated_heads):
            for h in range(heads):
                lanes = (h * HEAD_DIM) % group
                th = accs[h * HEAD_DIM // group][:, lanes:lanes + HEAD_DIM]
                if h < rotated_heads:
                    th = _rotate_heads(th, c_ref[...], up_ref[...], down_ref[...])
                res_ref[h] = th
                o1_ref[:, h * HEAD_DIM:(h + 1) * HEAD_DIM] = th.astype(BF16)
            _to_pattern(res_ref, o4_ref, d4, BF16)
            _to_pattern(res_ref, o16_ref, d16, BF16)

        @pl.when(shard == 0)
        def _():
            emit_qkv(heads)

        @pl.when(shard == 1)
        def _():
            emit_qkv(k_heads_in_second)

        @pl.when(shard >= 2)
        def _():
            for g, acc in enumerate(accs):
                hug_ref[:, g * group:(g + 1) * group] = acc.astype(BF16)

        @pl.when(jnp.logical_and(j == 3, i == n_tiles - 1))
        def _():
            for n in range(len(pieces)):
                for k in range(DIAGONAL):
                    to_neighbour(k, n).wait_send()
                relay(n).wait_send()
                for k in range(DIAGONAL + 1):
                    to_sibling(k, n, mc).wait_send()

    def held(base, last):
        return lambda j, i, plan_ref: jnp.where(plan_ref[base + j] == -1, i,
                                                jnp.where(plan_ref[base + j] == -2, last, 0))

    q_row, h_row = held(8, n_tiles - 1), held(16, n_tiles - 1)
    tab_spec = pl.BlockSpec((tm, HEAD_DIM), lambda j, i, plan_ref: (i, 0))
    sems = [pltpu.SemaphoreType.DMA((len(pieces), 3))] * 4
    o1, o4, o16, hug, w_in_g, w_out_g, w_pool_g = _pallas(
        body, name="in_proj_gathering",
        grid_spec=pltpu.PrefetchScalarGridSpec(
            num_scalar_prefetch=1, grid=(N_SHARDS, n_tiles),
            in_specs=[pl.BlockSpec((tm, D_MODEL), lambda j, i, plan_ref: (i, 0)), ANY, ANY, ANY,
                      tab_spec, tab_spec, tab_spec],
            out_specs=[pl.BlockSpec((tm, tn), lambda j, i, p: (q_row(j, i, p), p[4 + j])),
                       pl.BlockSpec((d4, tm // d4, tn), lambda j, i, p: (0, q_row(j, i, p), p[4 + j])),
                       pl.BlockSpec((d16, tm // d16, tn), lambda j, i, p: (0, q_row(j, i, p), p[4 + j])),
                       pl.BlockSpec((tm, tn), lambda j, i, p: (h_row(j, i, p), p[12 + j])),
                       ANY, ANY, ANY],
            scratch_shapes=[pltpu.VMEM((D_MODEL, tn), BF16), pltpu.VMEM((heads, tm, HEAD_DIM), F32),
                            pltpu.SemaphoreType.DMA(())] + sems),
        out_shape=[jax.ShapeDtypeStruct((seq, D_QKV), BF16),
                   jax.ShapeDtypeStruct((d4, seq // d4, D_QKV), BF16),
                   jax.ShapeDtypeStruct((d16, seq // d16, D_QKV), BF16),
                   jax.ShapeDtypeStruct((seq, D_UG), BF16)]
        + [jax.ShapeDtypeStruct(b.shape, b.dtype) for b in w_bufs],
        input_output_aliases={2: 4, 3: 5, 4: 6},
        compiler_params=_params(("arbitrary", "arbitrary"), 52),
    )(plan, x, *w_bufs, *tabs)
    return [o1[None], o4, o16], hug, w_in_g, w_out_g, w_pool_g


def _band_masks():
    row = lax.broadcasted_iota(jnp.int32, (KEY_BLOCK, KEY_BLOCK), 0)
    col = lax.broadcasted_iota(jnp.int32, (KEY_BLOCK, KEY_BLOCK), 1)
    return col <= row, col >= row


def _attn_fwd(qkv, name):
    dil, n, _ = qkv.shape
    scale = HEAD_DIM ** -0.5
    lo, hi = slice(0, KEY_BLOCK), slice(KEY_BLOCK, CHUNK)

    def body(q_ref, k_ref, v_ref, kb_ref, vb_ref, o_ref, st_ref):
        i = pl.program_id(1)
        cur_mask, prev_mask = _band_masks()
        before_mask = jnp.logical_and(prev_mask, i > 0)
        lane = lax.broadcasted_iota(jnp.int32, (KEY_BLOCK, STAT_LANES), 1)
        tasks = [(rows, h) for rows in (lo, hi) for h in range(N_HEADS)]
        head = lambda h: slice(h * HEAD_DIM, (h + 1) * HEAD_DIM)

        def prev_of(rows, h):
            if rows is lo:
                return kb_ref[:, head(h)], vb_ref[:, head(h)], before_mask
            return k_ref[lo, head(h)], v_ref[lo, head(h)], prev_mask

        scores = []
        for rows, h in tasks:
            q = q_ref[rows, head(h)]
            scores.append((_dot_nt(q, prev_of(rows, h)[0]), _dot_nt(q, k_ref[rows, head(h)])))
        probs = []
        for (rows, h), (qk_prev, qk_cur) in zip(tasks, scores):
            s_prev = jnp.where(prev_of(rows, h)[2], qk_prev * scale, NEG)
            s_cur = jnp.where(cur_mask, qk_cur * scale, NEG)
            m = jnp.max(jnp.maximum(s_prev, s_cur), axis=-1, keepdims=True)
            p_prev = jnp.exp(s_prev - m)
            p_cur = jnp.exp(s_cur - m)
            den = jnp.sum(p_prev + p_cur, axis=-1, keepdims=True)
            probs.append((p_prev.astype(BF16), p_cur.astype(BF16), den, m + jnp.log(den)))
        stats = [jnp.zeros((KEY_BLOCK, STAT_LANES), F32), jnp.zeros((KEY_BLOCK, STAT_LANES), F32)]
        for (rows, h), (p_prev, p_cur, den, lse) in zip(tasks, probs):
            o = _dot_nn(p_cur, v_ref[rows, head(h)]) + _dot_nn(p_prev, prev_of(rows, h)[1])
            o_ref[rows, head(h)] = (o / den).astype(BF16)
            b = 0 if rows is lo else 1
            stats[b] = jnp.where(lane == h, lse, stats[b])
        st_ref[lo, :] = stats[0]
        st_ref[hi, :] = stats[1]

    main = lambda cb: pl.BlockSpec((None, CHUNK, D_ATTN), lambda r, i: (r, i, cb))
    before = lambda cb: pl.BlockSpec((None, KEY_BLOCK, D_ATTN), lambda r, i: (r, jnp.maximum(2 * i - 1, 0), cb))
    return _pallas(
        body, name=name, grid=(dil, n // CHUNK),
        in_specs=[main(0), main(1), main(2), before(1), before(2)],
        out_specs=[main(0), pl.BlockSpec((None, CHUNK, STAT_LANES), lambda r, i: (r, i, 0))],
        out_shape=[jax.ShapeDtypeStruct((dil, n, D_ATTN), BF16), jax.ShapeDtypeStruct((dil, n, STAT_LANES), F32)],
        compiler_params=_params(("parallel", "parallel"), 40),
    )(qkv, qkv, qkv, qkv, qkv)


def _attn_bwd(qkv, do, stats, name, comm=None):
    dil, n, _ = qkv.shape
    n_blocks = n // KEY_BLOCK
    last = n // CHUNK - 1
    scale = HEAD_DIM ** -0.5
    lo, hi = slice(0, KEY_BLOCK), slice(KEY_BLOCK, CHUNK)

    def body(q_ref, k_ref, v_ref, kb_ref, vb_ref, qa_ref, do_ref, doa_ref, st_ref, sta_ref, dq_ref, dk_ref, dv_ref):
        i = pl.program_id(1)
        cur_mask, prev_mask = _band_masks()
        before_mask = jnp.logical_and(prev_mask, i > 0)
        after_mask = jnp.logical_and(prev_mask, i < last)

        rows_cat = lambda a, b: jnp.concatenate([a, b], axis=0)
        masks = (jnp.concatenate([before_mask, cur_mask], axis=1), jnp.concatenate([prev_mask, cur_mask], axis=1),
                 after_mask)

        def operands(h):
            cols = slice(h * HEAD_DIM, (h + 1) * HEAD_DIM)
            lse_c, del_c = slice(h, h + 1), slice(N_HEADS + h, N_HEADS + h + 1)
            q = (q_ref[lo, cols], q_ref[hi, cols], qa_ref[:, cols])
            do = (do_ref[lo, cols], do_ref[hi, cols], doa_ref[:, cols])
            keys = (rows_cat(kb_ref[:, cols], k_ref[lo, cols]), k_ref[:, cols], k_ref[hi, cols])
            vals = (rows_cat(vb_ref[:, cols], v_ref[lo, cols]), v_ref[:, cols], v_ref[hi, cols])
            st = ((st_ref[lo, lse_c], st_ref[lo, del_c]), (st_ref[hi, lse_c], st_ref[hi, del_c]),
                  (sta_ref[:, lse_c], sta_ref[:, del_c]))
            return cols, q, do, keys, vals, st

        group = N_HEADS // 2
        for first_head in range(0, N_HEADS, group):
            heads = range(first_head, first_head + group)
            raw = {}
            for h in heads:
                _, q, do, keys, vals, _ = operands(h)
                raw[h] = [(_dot_nt(q[j], keys[j]), _dot_nt(do[j], vals[j])) for j in range(3)]
            grads = {}
            for h in heads:
                st = operands(h)[5]
                grads[h] = []
                for j in range(3):
                    qk, dp = raw[h][j]
                    lse, delta = st[j]
                    p = jnp.exp(jnp.where(masks[j], qk * scale, NEG) - lse)
                    grads[h].append((p.astype(BF16), (p * (dp - delta) * scale).astype(BF16)))
            for h in heads:
                cols, q, do, keys, _, _ = operands(h)
                (p0, ds0), (p1, ds1), (pa, dsa) = grads[h]
                own, nxt = slice(KEY_BLOCK, CHUNK), slice(0, KEY_BLOCK)

                def put(ref, rows, val, cols=cols):
                    ref[rows, cols] = val.astype(ref.dtype)

                put(dq_ref, lo, _dot_nn(ds0, keys[0]))
                put(dq_ref, hi, _dot_nn(ds1, keys[1]))
                put(dk_ref, lo, _dot_tn(rows_cat(ds0[:, own], ds1[:, nxt]), q_ref[:, cols]))
                put(dk_ref, hi, _dot_tn(rows_cat(ds1[:, own], dsa), rows_cat(q[1], q[2])))
                put(dv_ref, lo, _dot_tn(rows_cat(p0[:, own], p1[:, nxt]), do_ref[:, cols]))
                put(dv_ref, hi, _dot_tn(rows_cat(p1[:, own], pa), rows_cat(do[1], do[2])))

    def spec(rows, width, row_of, cb):
        return pl.BlockSpec((None, rows, width), lambda r, i: (r, row_of(i), cb))

    same = lambda i: i
    before = lambda i: jnp.maximum(2 * i - 1, 0)
    after = lambda i: jnp.minimum(2 * i + 2, n_blocks - 1)
    out = spec(CHUNK, D_ATTN, same, 0)
    return _call(
        body, name=name, grid=(dil, n // CHUNK),
        in_specs=[spec(CHUNK, D_ATTN, same, 0), spec(CHUNK, D_ATTN, same, 1), spec(CHUNK, D_ATTN, same, 2),
                  spec(KEY_BLOCK, D_ATTN, before, 1), spec(KEY_BLOCK, D_ATTN, before, 2),
                  spec(KEY_BLOCK, D_ATTN, after, 0),
                  spec(CHUNK, D_ATTN, same, 0), spec(KEY_BLOCK, D_ATTN, after, 0),
                  spec(CHUNK, STAT_LANES, same, 0), spec(KEY_BLOCK, STAT_LANES, after, 0)],
        out_specs=[out, out, out],
        out_shape=[jax.ShapeDtypeStruct((dil, n, D_ATTN), BF16)] * 3,
        scratch_shapes=[], semantics=("parallel", "parallel"), vmem_mib=40,
        args=(qkv, qkv, qkv, qkv, qkv, qkv, do, do, stats, stats), comm=comm)


def _window_sums(ext, window, backward):
    rows = ext.shape[0]
    acc, span = ext, 1
    while span < window:
        acc = acc + pltpu.roll(acc, (rows - span) if backward else span, axis=0)
        span *= 2
    return acc


def _mix_gate(o_list, st_list, hug, w_pool_g, pool_scale):
    seq = hug.shape[0]
    tm = 256
    halo_blocks = tm // POOL_HALO
    d4, d16 = DILATIONS[1], DILATIONS[2]

    def body(o1_ref, o4_ref, o16_ref, l1_ref, l4_ref, l16_ref, u_ref, halo_ref, ga_ref, gp_ref, wp_ref, sc_ref,
             y_ref, mix_ref, lse_ref, pooled_ref, n4_ref, n16_ref, nl4_ref, nl16_ref):
        i = pl.program_id(0)
        _from_pattern(o4_ref, n4_ref, d4)
        _from_pattern(o16_ref, n16_ref, d16)
        _from_pattern(l4_ref, nl4_ref, d4)
        _from_pattern(l16_ref, nl16_ref, d16)
        la, lb, lc = l1_ref[...], nl4_ref[0], nl16_ref[0]
        mx = jnp.maximum(jnp.maximum(la, lb), lc)
        ea, eb, ec = jnp.exp(la - mx), jnp.exp(lb - mx), jnp.exp(lc - mx)
        tot = ea + eb + ec
        lse_ref[...] = mx + jnp.log(tot)
        wa, wb, wc = ea / tot, eb / tot, ec / tot
        ga = ga_ref[...].astype(F32)
        silu_a = ga * jax.nn.sigmoid(ga)
        for h in range(N_HEADS):
            cols = slice(h * HEAD_DIM, (h + 1) * HEAD_DIM)
            hc = slice(h, h + 1)
            attn = wa[:, hc] * o1_ref[:, cols].astype(F32) + wb[:, hc] * n4_ref[h] + wc[:, hc] * n16_ref[h]
            mix_ref[:, cols] = attn.astype(BF16)
            y_ref[:, cols] = (attn * silu_a[:, cols]).astype(BF16)

        u = u_ref[...].astype(F32)
        halo = jnp.where(i > 0, halo_ref[...].astype(F32), 0.0)
        ext = jnp.concatenate([halo, u], axis=0)
        pos = i * tm + lax.broadcasted_iota(jnp.int32, (tm, 1), 0)
        gp = gp_ref[...].astype(F32)
        gated_scale = sc_ref[...] * (gp * jax.nn.sigmoid(gp))
        for g, window in enumerate(POOL_WINDOWS):
            cols = slice(g * POOL_GROUP_DIM, (g + 1) * POOL_GROUP_DIM)
            sums = _window_sums(ext[:, cols], window, backward=False)[POOL_HALO:, :]
            count = jnp.minimum(pos + 1, window).astype(F32)
            pooled = (sums / count - u[:, cols]).astype(BF16)
            pooled_ref[:, cols] = pooled
            pre = _dot_nn(pooled, wp_ref[g])
            out_cols = slice(D_ATTN + g * POOL_GROUP_DIM, D_ATTN + (g + 1) * POOL_GROUP_DIM)
            mix_ref[:, out_cols] = pre.astype(BF16)
            y_ref[:, out_cols] = (pre * gated_scale[:, cols]).astype(BF16)

    row = lambda width, cb=0: pl.BlockSpec((tm, width), lambda i: (i, cb))
    pat = lambda d, width: pl.BlockSpec((d, tm // d, width), lambda i: (0, i, 0))
    return _pallas(
        body, name="mix_gate", grid=(seq // tm,),
        in_specs=[row(D_ATTN), pat(d4, D_ATTN), pat(d16, D_ATTN),
                  row(STAT_LANES), pat(d4, STAT_LANES), pat(d16, STAT_LANES),
                  row(D_POOL),
                  pl.BlockSpec((POOL_HALO, D_POOL), lambda i: (jnp.maximum(i * halo_blocks - 1, 0), 0)),
                  row(D_ATTN, 1), row(D_POOL, 2),
                  pl.BlockSpec((len(POOL_WINDOWS), POOL_GROUP_DIM, POOL_GROUP_DIM), lambda i: (0, 0, 0)),
                  pl.BlockSpec((1, D_POOL), lambda i: (0, 0))],
        out_specs=[row(D_MODEL), row(D_MODEL), row(STAT_LANES), row(D_POOL)],
        out_shape=[jax.ShapeDtypeStruct((seq, D_MODEL), BF16), jax.ShapeDtypeStruct((seq, D_MODEL), BF16),
                   jax.ShapeDtypeStruct((seq, STAT_LANES), F32), jax.ShapeDtypeStruct((seq, D_POOL), BF16)],
        scratch_shapes=[pltpu.VMEM((N_HEADS, tm, HEAD_DIM), F32), pltpu.VMEM((N_HEADS, tm, HEAD_DIM), F32),
                        pltpu.VMEM((1, tm, STAT_LANES), F32), pltpu.VMEM((1, tm, STAT_LANES), F32)],
        compiler_params=_params(("parallel",), 48),
    )(o_list[0][0], o_list[1], o_list[2], st_list[0][0], st_list[1], st_list[2],
      hug, hug, hug, hug, w_pool_g, pool_scale)


def _out_proj_loss(y, w_out_g, x, target, gain, bias):
    seq = x.shape[0]
    tm = 512

    def body(y_ref, w_ref, x_ref, t_ref, g_ref, b_ref, dz_ref, dzb_ref, gg_ref, gb_ref, loss_ref):
        @pl.when(pl.program_id(0) == 0)
        def _():
            gg_ref[...] = jnp.zeros_like(gg_ref)
            gb_ref[...] = jnp.zeros_like(gb_ref)
            loss_ref[...] = jnp.zeros_like(loss_ref)

        halves = [slice(0, tm // 2), slice(tm // 2, tm)]
        projected = [_dot_nn(y_ref[rows, :], w_ref[...]) for rows in halves]
        for rows, out in zip(halves, projected):
            z = DEEPNORM_ALPHA * x_ref[rows, :] + out
            mu = jnp.mean(z, axis=-1, keepdims=True)
            zc = z - mu
            rstd = lax.rsqrt(jnp.mean(zc * zc, axis=-1, keepdims=True) + LN_EPS)
            xhat = zc * rstd
            gain_v = g_ref[...]
            diff = xhat * gain_v + b_ref[...] - t_ref[rows, :]
            sq = _fold_rows(diff * diff)
            part = sq[:, :128]
            for k in range(1, D_MODEL // 128):
                part = part + sq[:, k * 128:(k + 1) * 128]
            loss_ref[...] += part
            dln = diff * (1.0 / D_MODEL)
            gg_ref[...] += _fold_rows(dln * xhat)
            gb_ref[...] += _fold_rows(dln)
            dxhat = dln * gain_v
            dz = rstd * (dxhat - jnp.mean(dxhat, axis=-1, keepdims=True)
                         - xhat * jnp.mean(dxhat * xhat, axis=-1, keepdims=True))
            dz_ref[rows, :] = dz
            dzb_ref[rows, :] = dz.astype(BF16)

    row = lambda: pl.BlockSpec((tm, D_MODEL), lambda i: (i, 0))
    vec = lambda: pl.BlockSpec((1, D_MODEL), lambda i: (0, 0))
    acc = lambda width: pl.BlockSpec((8, width), lambda i: (0, 0))
    return _pallas(
        body, name="out_proj_loss", grid=(seq // tm,),
        in_specs=[row(), pl.BlockSpec((D_MODEL, D_MODEL), lambda i: (0, 0), pipeline_mode=pl.Buffered(1)),
                  row(), row(), vec(), vec()],
        out_specs=[row(), row(), acc(D_MODEL), acc(D_MODEL), acc(128)],
        out_shape=[jax.ShapeDtypeStruct((seq, D_MODEL), F32), jax.ShapeDtypeStruct((seq, D_MODEL), BF16),
                   jax.ShapeDtypeStruct((8, D_MODEL), F32), jax.ShapeDtypeStruct((8, D_MODEL), F32),
                   jax.ShapeDtypeStruct((8, 128), F32)],
        compiler_params=_params(("arbitrary",), 56),
    )(y, w_out_g.reshape(D_MODEL, D_MODEL), x, target, gain, bias)


def _dy_gate_bwd(dzb, w_out_g, hug, mixpre, pool_scale, lse_all):
    seq = dzb.shape[0]
    tm = 256
    d4, d16 = DILATIONS[1], DILATIONS[2]

    def body(dz_ref, w_ref, ga_ref, gp_ref, mix_ref, sc_ref, lse_ref,
             dh_ref, dpo_ref, do1_ref, do4_ref, do16_ref, st1_ref, st4_ref, st16_ref, da_ref, st_ref):
        dy = _dot_nt(dz_ref[...], w_ref[...])
        ga = ga_ref[...].astype(F32)
        sig = jax.nn.sigmoid(ga)
        attn = mix_ref[:, :D_ATTN].astype(F32)
        dya = dy[:, :D_ATTN]
        dattn = dya * (ga * sig)
        dh_ref[:, :D_ATTN] = (dya * attn * (sig * (1.0 + ga * (1.0 - sig)))).astype(BF16)
        _store_slabs(da_ref, dattn)
        lane = lax.broadcasted_iota(jnp.int32, (tm, STAT_LANES), 1)
        stats = lse_ref[...]
        prod = dattn * attn
        for h in range(N_HEADS):
            delta = jnp.sum(prod[:, h * HEAD_DIM:(h + 1) * HEAD_DIM], axis=-1, keepdims=True)
            stats = jnp.where(lane == N_HEADS + h, delta, stats)
        st_ref[0] = stats
        do1_ref[...] = dattn.astype(BF16)
        st1_ref[...] = stats
        _to_pattern(da_ref, do4_ref, d4, BF16)
        _to_pattern(da_ref, do16_ref, d16, BF16)
        _to_pattern(st_ref, st4_ref, d4, F32)
        _to_pattern(st_ref, st16_ref, d16, F32)

        gp = gp_ref[...].astype(F32)
        sig = jax.nn.sigmoid(gp)
        dyp = dy[:, D_ATTN:]
        dpo_ref[...] = (dyp * (gp * sig)).astype(BF16)
        dh_ref[:, D_ATTN:] = (dyp * (mix_ref[:, D_ATTN:].astype(F32) * sc_ref[...])
                              * (sig * (1.0 + gp * (1.0 - sig)))).astype(BF16)

    row = lambda width, cb=0: pl.BlockSpec((tm, width), lambda i: (i, cb))
    pat = lambda d, width: pl.BlockSpec((d, tm // d, width), lambda i: (0, i, 0))
    pat_shape = lambda d, width, dtype: jax.ShapeDtypeStruct((d, seq // d, width), dtype)
    outs = _pallas(
        body, name="dy_gate_bwd", grid=(seq // tm,),
        in_specs=[row(D_MODEL), pl.BlockSpec((D_MODEL, D_MODEL), lambda i: (0, 0)),
                  row(D_ATTN, 1), row(D_POOL, 2), row(D_MODEL), pl.BlockSpec((1, D_POOL), lambda i: (0, 0)),
                  row(STAT_LANES)],
        out_specs=[row(D_MODEL, D_IN // D_MODEL - 1), row(D_POOL),
                   row(D_ATTN), pat(d4, D_ATTN), pat(d16, D_ATTN),
                   row(STAT_LANES), pat(d4, STAT_LANES), pat(d16, STAT_LANES)],
        out_shape=[jax.ShapeDtypeStruct((seq, D_IN), BF16), jax.ShapeDtypeStruct((seq, D_POOL), BF16),
                   jax.ShapeDtypeStruct((seq, D_ATTN), BF16), pat_shape(d4, D_ATTN, BF16), pat_shape(d16, D_ATTN, BF16),
                   jax.ShapeDtypeStruct((seq, STAT_LANES), F32), pat_shape(d4, STAT_LANES, F32),
                   pat_shape(d16, STAT_LANES, F32)],
        scratch_shapes=[pltpu.VMEM((N_HEADS, tm, HEAD_DIM), F32), pltpu.VMEM((1, tm, STAT_LANES), F32)],
        compiler_params=_params(("parallel",), 48),
    )(dzb, w_out_g.reshape(D_MODEL, D_MODEL), hug, hug, mixpre, pool_scale, lse_all)
    dh, dpo, do1, do4, do16, st1, st4, st16 = outs
    return dh, dpo, [do1[None], do4, do16], [st1[None], st4, st16]


def _pool_bwd(dh, dpo, mixpre, pooled, w_pool_g, pool_scale):
    seq = dpo.shape[0]
    tm = 256
    halo_blocks = tm // POOL_HALO
    last = seq // tm - 1
    n_groups = len(POOL_WINDOWS)

    def body(dh_in_ref, dpo_ref, halo_ref, pre_ref, pooled_ref, wp_ref, sc_ref, du_ref, gw_ref, gs_ref):
        i = pl.program_id(0)

        @pl.when(i == 0)
        def _():
            gw_ref[...] = jnp.zeros_like(gw_ref)
            gs_ref[...] = jnp.zeros_like(gs_ref)

        dpo = dpo_ref[...].astype(F32)
        scale = sc_ref[...]
        gs_ref[...] += _fold_rows(dpo * pre_ref[...].astype(F32))
        halo = jnp.where(i < last, halo_ref[...].astype(F32), 0.0)
        dpw = (jnp.concatenate([dpo, halo], axis=0) * scale).astype(BF16)
        pos = i * tm + lax.broadcasted_iota(jnp.int32, (tm + POOL_HALO, 1), 0)
        for g, window in enumerate(POOL_WINDOWS):
            cols = slice(g * POOL_GROUP_DIM, (g + 1) * POOL_GROUP_DIM)
            dpw_g = dpw[:, cols]
            gw_ref[g] += _dot_tn(pooled_ref[:, cols], dpw_g[:tm, :])
            dpooled = _dot_nt(dpw_g, wp_ref[g])
            count = jnp.minimum(pos + 1, window).astype(F32)
            sums = _window_sums(dpooled / count, window, backward=True)
            du_ref[:, cols] = (sums[:tm, :] - dpooled[:tm, :]).astype(BF16)

    row = lambda width, cb=0: pl.BlockSpec((tm, width), lambda i: (i, cb))
    return _pallas(
        body, name="pool_bwd", grid=(seq // tm,),
        in_specs=[ANY, row(D_POOL),
                  pl.BlockSpec((POOL_HALO, D_POOL),
                               lambda i: (jnp.minimum((i + 1) * halo_blocks, seq // POOL_HALO - 1), 0)),
                  row(D_POOL, 1), row(D_POOL),
                  pl.BlockSpec((n_groups, POOL_GROUP_DIM, POOL_GROUP_DIM), lambda i: (0, 0, 0)),
                  pl.BlockSpec((1, D_POOL), lambda i: (0, 0))],
        out_specs=[row(D_POOL, D_QKV // D_POOL),
                   pl.BlockSpec((n_groups, POOL_GROUP_DIM, POOL_GROUP_DIM), lambda i: (0, 0, 0)),
                   pl.BlockSpec((8, D_POOL), lambda i: (0, 0))],
        out_shape=[jax.ShapeDtypeStruct(dh.shape, dh.dtype),
                   jax.ShapeDtypeStruct((n_groups, POOL_GROUP_DIM, POOL_GROUP_DIM), F32),
                   jax.ShapeDtypeStruct((8, D_POOL), F32)],
        input_output_aliases={0: 0},
        compiler_params=_params(("arbitrary",), 40),
    )(dh, dpo, dpo, mixpre, pooled, w_pool_g, pool_scale)


def _sum_patterns(dh, parts, tabs, unrotate, col_block, name, comm=None):
    seq = dh.shape[0]
    tm, tn = 256, D_ATTN
    per = D_ATTN // tn
    d4, d16 = DILATIONS[1], DILATIONS[2]

    def body(dh_in_ref, a1_ref, a4_ref, a16_ref, ct_ref, up_ref, down_ref, o_ref, n4_ref, n16_ref):
        _from_pattern(a4_ref, n4_ref, d4)
        _from_pattern(a16_ref, n16_ref, d16)
        for s in range(tn // HEAD_DIM):
            cols = slice(s * HEAD_DIM, (s + 1) * HEAD_DIM)
            tot = a1_ref[:, cols].astype(F32) + n4_ref[s] + n16_ref[s]
            if unrotate:
                tot = _rotate_heads(tot, ct_ref[...], -up_ref[...], -down_ref[...])
            o_ref[:, cols] = tot.astype(BF16)

    tab = pl.BlockSpec((tm, HEAD_DIM), lambda i, j: (i, 0))
    pat = lambda d: pl.BlockSpec((d, tm // d, tn), lambda i, j: (0, i, j))
    (dh,), exchanged = _call(
        body, name=name, grid=(seq // tm, per),
        in_specs=[ANY, pl.BlockSpec((tm, tn), lambda i, j: (i, j)), pat(d4), pat(d16), tab, tab, tab],
        out_specs=[pl.BlockSpec((tm, tn), lambda i, j: (i, col_block * per + j))],
        out_shape=[jax.ShapeDtypeStruct(dh.shape, dh.dtype)],
        scratch_shapes=[pltpu.VMEM((tn // HEAD_DIM, tm, HEAD_DIM), F32), pltpu.VMEM((tn // HEAD_DIM, tm, HEAD_DIM), F32)],
        semantics=("parallel", "parallel"), vmem_mib=32, args=(dh, parts[0][0], parts[1], parts[2], *tabs),
        aliases={0: 0}, comm=comm)
    return dh, exchanged


def _grad_w_in(x, dh, half, name, comm=None):
    seq = x.shape[0]
    ts, td, te = 2048, D_MODEL // 2, SHARD_IN

    def body(half_ref, x_ref, dh_ref, o_ref):
        k = pl.program_id(1)
        part = _dot_tn(x_ref[...].astype(BF16), dh_ref[...])

        @pl.when(k == 0)
        def _():
            o_ref[...] = part

        @pl.when(k > 0)
        def _():
            o_ref[...] += part

    (g,), exchanged = _call(
        body, name=name, grid=(N_SHARDS, seq // ts),
        in_specs=[pl.BlockSpec((ts, td), lambda e, k, half_ref: (k, half_ref[0])),
                  pl.BlockSpec((ts, te), lambda e, k, half_ref: (k, e))],
        out_specs=[pl.BlockSpec((None, td, te), lambda e, k, half_ref: (e, 0, 0))],
        out_shape=[jax.ShapeDtypeStruct((N_SHARDS, td, te), F32)],
        scratch_shapes=[], semantics=("parallel", "arbitrary"), vmem_mib=56, args=(x, dh), comm=comm,
        prefetch=(half,))
    return g, exchanged


def _grad_w_out(y, dzb):
    seq = y.shape[0]
    ts, te = 2048, 1024

    def body(y_ref, dz_ref, o_ref):
        k = pl.program_id(1)
        part = _dot_tn(y_ref[...], dz_ref[...])

        @pl.when(k == 0)
        def _():
            o_ref[...] = part

        @pl.when(k > 0)
        def _():
            o_ref[...] += part

    return _pallas(
        body, name="grad_w_out", grid=(D_MODEL // te, seq // ts),
        in_specs=[pl.BlockSpec((ts, te), lambda e, k: (k, e)), pl.BlockSpec((ts, D_MODEL), lambda e, k: (k, 0))],
        out_specs=pl.BlockSpec((te, D_MODEL), lambda e, k: (e, 0)),
        out_shape=jax.ShapeDtypeStruct((D_MODEL, D_MODEL), F32),
        compiler_params=_params(("parallel", "arbitrary"), 56),
    )(y, dzb)


GRAD_X_LATE_SHARDS = 1
GRAD_X_PARTIAL_ROWS = 512


def _grad_x_partial(dh, w_in_g, dz, first, tiles, prev=None, comm=None):
    seq = dh.shape[0]
    tm, tk = GRAD_X_PARTIAL_ROWS, SHARD_IN

    def body(*refs):
        dh_ref, w_ref, dz_ref, o_ref = refs[-4:]
        k = pl.program_id(1)
        part = _dot_nt(dh_ref[...], w_ref[...])

        @pl.when(k == 0)
        def _():
            o_ref[...] = DEEPNORM_ALPHA * dz_ref[...] + part

        @pl.when(k > 0)
        def _():
            o_ref[...] += part

    carried = [] if prev is None else [prev]
    row = pl.BlockSpec((tm, D_MODEL), lambda i, k: (i + first, 0))
    (partial,), exchanged = _call(
        body, name="grad_x_partial_%d" % first, grid=(tiles, N_SHARDS - GRAD_X_LATE_SHARDS),
        in_specs=[ANY] * len(carried) + [
            pl.BlockSpec((tm, tk), lambda i, k: (i + first, k)),
            pl.BlockSpec((None, D_MODEL, tk), lambda i, k: (k, 0, 0)), row],
        out_specs=[row],
        out_shape=[jax.ShapeDtypeStruct((seq, D_MODEL), F32)],
        scratch_shapes=[], semantics=("parallel", "arbitrary"), vmem_mib=48, args=(*carried, dh, w_in_g, dz),
        aliases={0: 0} if carried else None, comm=comm)
    return partial, exchanged


def _grad_x_final(dh, w_in_g, partial):
    seq = dh.shape[0]
    tm, tk = 512, SHARD_IN
    k0 = N_SHARDS - GRAD_X_LATE_SHARDS

    def body(dh_ref, w_ref, p_ref, o_ref):
        k = pl.program_id(1)
        part = _dot_nt(dh_ref[...], w_ref[...])

        @pl.when(k == 0)
        def _():
            o_ref[...] = p_ref[...] + part

        @pl.when(k > 0)
        def _():
            o_ref[...] += part

    row = pl.BlockSpec((tm, D_MODEL), lambda i, k: (i, 0))
    return _pallas(
        body, name="grad_x_final", grid=(seq // tm, GRAD_X_LATE_SHARDS),
        in_specs=[pl.BlockSpec((tm, tk), lambda i, k: (i, k + k0)),
                  pl.BlockSpec((None, D_MODEL, tk), lambda i, k: (k + k0, 0, 0)), row],
        out_specs=row, out_shape=jax.ShapeDtypeStruct((seq, D_MODEL), F32),
        compiler_params=_params(("parallel", "arbitrary"), 48),
    )(dh, w_in_g, partial)


def _pool_weight(w_pool_sh):
    n_groups = len(POOL_WINDOWS)
    shard_c = POOL_GROUP_DIM // N_SHARDS
    return (w_pool_sh.reshape(N_SHARDS, n_groups, shard_c, POOL_GROUP_DIM).transpose(1, 0, 2, 3)
            .reshape(n_groups, POOL_GROUP_DIM, POOL_GROUP_DIM))


def _pool_grad_pieces(g_w_pool):
    n_groups = len(POOL_WINDOWS)
    half_c = POOL_GROUP_DIM // N_SHARDS // 2
    return (g_w_pool.reshape(n_groups, N_SHARDS, 2, half_c, POOL_GROUP_DIM).transpose(1, 2, 0, 3, 4)
            .reshape(N_SHARDS, 2, n_groups * half_c, POOL_GROUP_DIM))


def _step(x, target, w_bufs, pool_scale, gain, bias, place):
    seq = x.shape[0]
    tabs = _rope_tables(seq)
    core, chip_core, onward, plan = place
    qkv, hug, w_in_g, w_out_g, w_pool_sh = _in_proj_gathering(x, w_bufs, tabs, plan)
    o_list, st_list = [], []
    for p, dil in enumerate(DILATIONS):
        o, st = _attn_fwd(qkv[p], "attn_fwd_d%d" % dil)
        o_list.append(o)
        st_list.append(st)
    w_pool_g = _pool_weight(w_pool_sh)
    y, mixpre, lse_all, pooled = _mix_gate(o_list, st_list, hug, w_pool_g, pool_scale)
    dz, dzb, gain_part, bias_part, loss_part = _out_proj_loss(y, w_out_g, x, target, gain, bias)
    dh, dpo, do_list, stat_list = _dy_gate_bwd(dzb, w_out_g, hug, mixpre, pool_scale, lse_all)
    g_w_out = _grad_w_out(y, dzb)
    dh, g_w_pool, scale_part = _pool_bwd(dh, dpo, mixpre, pooled, w_pool_g, pool_scale)
    small = jnp.concatenate([scale_part, gain_part, bias_part, loss_part], axis=1)
    early = [g_w_out.reshape(N_SHARDS, 2, D_MODEL // (2 * N_SHARDS), D_MODEL), _pool_grad_pieces(g_w_pool)]

    bwd = lambda p, comm: _attn_bwd(qkv[p], do_list[p], stat_list[p], "attn_bwd_d%d" % DILATIONS[p], comm)
    part_a, recv = bwd(0, _exchange_halves(early))
    sums = [_add_own_half(g, r, core, "add_own_half_%d" % a) for a, (g, r) in enumerate(zip(early, recv))]
    part_b, recv = bwd(1, _scatter_to_chips([s[1] for s in sums], 0))
    part_c, recv = bwd(2, _scatter_to_chips([s[1] for s in sums], 1, recv))
    bufs = [_add_chips(s[0], r, chip_core, "add_chips_%d" % a) for a, (s, r) in enumerate(zip(sums, recv))]
    parts = [part_a, part_b, part_c]
    dh, gathered = _sum_patterns(dh, [t[0] for t in parts], tabs, True, 0, "sum_dq", _gather_small(small))
    dh, reduced = _sum_patterns(dh, [t[1] for t in parts], tabs, True, 1, "sum_dk", _share_with_sibling(bufs))
    dh, _ = _sum_patterns(dh, [t[2] for t in parts], tabs, False, 2, "sum_dv")

    give, _ = _grad_w_in(x, dh, 1 - core, "grad_w_in_give")
    keep, recv = _grad_w_in(x, dh, core, "grad_w_in_keep", _send_to_sibling([give]))
    total, total_b = _add_pair(keep, recv[0], "add_own_half_w_in")
    n_tiles = seq // GRAD_X_PARTIAL_ROWS
    tiles = 3 * n_tiles // 8
    part, relayed = _grad_x_partial(dh, w_in_g, dz, 0, tiles, None, _relay_diagonal(total_b))
    total_b = _fold_relayed(total, total_b, relayed[0], onward)
    part, recv = _grad_x_partial(dh, w_in_g, dz, tiles, n_tiles - tiles, part, _scatter_to_neighbours(total_b))
    buf = _add_chips(total, recv[0], chip_core, "add_chips_w_in")
    g_x = _grad_x_final(dh, w_in_g, part)
    g_w_in = _run_exchange(_share_with_sibling([buf]), "share_w_in")[0]
    return g_x, g_w_in, reduced[0], reduced[1], small, gathered[0]


def _exchange_halves(grads):
    n = len(grads)

    def copies(src, dst, sems):
        x, y, c, _ = _mesh_place()
        return [_remote(src[a].at[j, 1 - c], dst[a].at[j], sems[0].at[a, j], sems[1].at[a, j], (x, y, 1 - c))
                for a in range(n) for j in range(N_SHARDS)]

    def start(src, dst, sems):
        for cp in copies(src, dst, sems):
            cp.start()

    def finish(src, dst, sems):
        for cp in copies(src, dst, sems):
            cp.wait()

    return _Exchange(grads, [jax.ShapeDtypeStruct((N_SHARDS,) + g.shape[2:], g.dtype) for g in grads], {},
                     [pltpu.SemaphoreType.DMA((n, N_SHARDS))] * 2, start, finish)


def _add_own_half(grad, recv, core, name):
    _, _, r, c = grad.shape
    tr = min(r, 256)

    def body(core_ref, g_ref, r_ref, o_ref, ob_ref):
        tot = g_ref[...] + r_ref[...]
        o_ref[...] = tot
        ob_ref[...] = tot.astype(BF16)

    out = pl.BlockSpec((None, tr, c), lambda j, i, core_ref: (j, i, 0))
    return _pallas(
        body, name=name,
        grid_spec=pltpu.PrefetchScalarGridSpec(
            num_scalar_prefetch=1, grid=(N_SHARDS, r // tr),
            in_specs=[pl.BlockSpec((None, None, tr, c), lambda j, i, core_ref: (j, core_ref[0], i, 0)),
                      pl.BlockSpec((None, tr, c), lambda j, i, core_ref: (j, i, 0))],
            out_specs=[out, out]),
        out_shape=[jax.ShapeDtypeStruct((N_SHARDS, r, c), F32), jax.ShapeDtypeStruct((N_SHARDS, r, c), BF16)],
        compiler_params=_params(("parallel", "parallel"), 32),
    )(core, grad, recv)


def _send_to_sibling(arrays):
    n = len(arrays)

    def copies(src, dst, sems):
        x, y, c, _ = _mesh_place()
        return [_remote(src[a], dst[a], sems[0].at[a], sems[1].at[a], (x, y, 1 - c)) for a in range(n)]

    def start(src, dst, sems):
        for cp in copies(src, dst, sems):
            cp.start()

    def finish(src, dst, sems):
        for cp in copies(src, dst, sems):
            cp.wait()

    return _Exchange(arrays, [jax.ShapeDtypeStruct(t.shape, t.dtype) for t in arrays], {},
                     [pltpu.SemaphoreType.DMA((n,))] * 2, start, finish)


def _add_pair(a, b, name):
    _, r, c = a.shape
    tr = min(r, 256)

    def body(a_ref, b_ref, o_ref, ob_ref):
        tot = a_ref[...] + b_ref[...]
        o_ref[...] = tot
        ob_ref[...] = tot.astype(BF16)

    spec = pl.BlockSpec((None, tr, c), lambda j, i: (j, i, 0))
    return _pallas(
        body, name=name, grid=(N_SHARDS, r // tr), in_specs=[spec, spec], out_specs=[spec, spec],
        out_shape=[jax.ShapeDtypeStruct(a.shape, F32), jax.ShapeDtypeStruct(a.shape, BF16)],
        compiler_params=_params(("parallel", "parallel"), 32),
    )(a, b)


def _scatter_to_chips(sums, part, into=None):
    n = len(sums)

    def copies(src, dst, sems):
        x, y, c, chips = _mesh_place()
        rows = lambda a: pl.ds(part * (sums[a].shape[1] // 2), sums[a].shape[1] // 2)
        return [_remote(src[a].at[2 * cx + cy, rows(a)], dst[a].at[k, rows(a)], sems[0].at[a, k], sems[1].at[a, k],
                        (cx, cy, c))
                for a in range(n) for k, (cx, cy) in enumerate(chips)]

    def start(src, dst, sems):
        for cp in copies(src, dst, sems):
            cp.start()

    def finish(src, dst, sems):
        for cp in copies(src, dst, sems):
            cp.wait()

    return _Exchange(sums + (into or []), [jax.ShapeDtypeStruct((3,) + s.shape[1:], s.dtype) for s in sums],
                     {n + a: a for a in range(n)} if into else {},
                     [pltpu.SemaphoreType.DMA((n, 3))] * 2, start, finish)


def _add_chips(sums, recv, chip_core, name):
    _, r, c = sums.shape
    n_recv = recv.shape[0]
    tr = min(r, 256)

    def body(cc_ref, s_ref, r_ref, o_ref):
        tot = s_ref[...]
        for k in range(n_recv):
            tot = tot + r_ref[k].astype(F32)
        o_ref[...] = tot

    return _pallas(
        body, name=name,
        grid_spec=pltpu.PrefetchScalarGridSpec(
            num_scalar_prefetch=1, grid=(r // tr,),
            in_specs=[pl.BlockSpec((None, tr, c), lambda i, cc_ref: (cc_ref[0], i, 0)),
                      pl.BlockSpec((n_recv, tr, c), lambda i, cc_ref: (0, i, 0))],
            out_specs=pl.BlockSpec((None, tr, c), lambda i, cc_ref: (cc_ref[1], i, 0))),
        out_shape=jax.ShapeDtypeStruct((2, r, c), F32),
        compiler_params=_params(("parallel",), 32),
    )(chip_core, sums, recv)


def _relay_diagonal(sums_b):
    def copy(src, dst, sems):
        x, y, c, _ = _mesh_place()
        diagonal = 2 * (1 - x) + (1 - y)
        return _remote(src[0].at[diagonal], dst[0], sems[0].at[0], sems[1].at[0], (x ^ (1 - c), y ^ c, c))

    def start(src, dst, sems):
        copy(src, dst, sems).start()

    def finish(src, dst, sems):
        copy(src, dst, sems).wait()

    return _Exchange([sums_b], [jax.ShapeDtypeStruct(sums_b.shape[1:], sums_b.dtype)], {},
                     [pltpu.SemaphoreType.DMA((1,))] * 2, start, finish)


def _fold_relayed(sums, sums_b, relayed, onward):
    _, r, c = sums.shape
    tr = min(r, 256)

    def body(on_ref, b_in_ref, s_ref, r_ref, o_ref):
        o_ref[...] = (s_ref[...] + r_ref[...].astype(F32)).astype(BF16)

    return _pallas(
        body, name="fold_relayed",
        grid_spec=pltpu.PrefetchScalarGridSpec(
            num_scalar_prefetch=1, grid=(r // tr,),
            in_specs=[ANY, pl.BlockSpec((None, tr, c), lambda i, on_ref: (on_ref[0], i, 0)),
                      pl.BlockSpec((tr, c), lambda i, on_ref: (i, 0))],
            out_specs=pl.BlockSpec((None, tr, c), lambda i, on_ref: (on_ref[0], i, 0))),
        out_shape=jax.ShapeDtypeStruct(sums_b.shape, sums_b.dtype),
        input_output_aliases={1: 0},
        compiler_params=_params(("parallel",), 32),
    )(onward, sums_b, sums, relayed)


def _scatter_to_neighbours(sums_b):
    def copies(src, dst, sems):
        x, y, c, chips = _mesh_place()
        return [_remote(src[0].at[2 * cx + cy], dst[0].at[k], sems[0].at[k], sems[1].at[k], (cx, cy, c))
                for k, (cx, cy) in enumerate(chips[:2])]

    def start(src, dst, sems):
        for cp in copies(src, dst, sems):
            cp.start()

    def finish(src, dst, sems):
        for cp in copies(src, dst, sems):
            cp.wait()

    return _Exchange([sums_b], [jax.ShapeDtypeStruct((2,) + sums_b.shape[1:], sums_b.dtype)], {},
                     [pltpu.SemaphoreType.DMA((2,))] * 2, start, finish)


def _share_with_sibling(bufs):
    n = len(bufs)

    def copies(dst, sems, half):
        x, y, c, _ = _mesh_place()
        h = c if half == "mine" else 1 - c
        return [_remote(dst[a].at[h], dst[a].at[h], sems[0].at[a], sems[1].at[a], (x, y, 1 - c)) for a in range(n)]

    def start(ins, dst, sems):
        for cp in copies(dst, sems, "mine"):
            cp.start()

    def finish(ins, dst, sems):
        for cp in copies(dst, sems, "theirs"):
            cp.wait_recv()
        for cp in copies(dst, sems, "mine"):
            cp.wait_send()

    return _Exchange(bufs, [jax.ShapeDtypeStruct(b.shape, b.dtype) for b in bufs], {a: a for a in range(n)},
                     [pltpu.SemaphoreType.DMA((n,))] * 2, start, finish)


def _adam_math(w, g, m, v):
    m = ADAM_B1 * m + (1.0 - ADAM_B1) * g
    v = ADAM_B2 * v + (1.0 - ADAM_B2) * (g * g)
    m_hat = m / (1.0 - ADAM_B1 ** ADAM_STEP)
    v_hat = v / (1.0 - ADAM_B2 ** ADAM_STEP)
    delta = -ADAM_LR * (m_hat / (jnp.sqrt(v_hat) + ADAM_EPS) + ADAM_WD * w)
    return delta, m, v


def _gather_small(small):
    def peers():
        x, y, c, _ = _mesh_place()
        return [(x ^ ((r >> 2) & 1), y ^ ((r >> 1) & 1), c ^ (r & 1)) for r in range(1, 8)], 4 * x + 2 * y + c

    def start(src, dst, sems):
        to, me = peers()
        for r, peer in enumerate(to):
            _remote(src[0], dst[0].at[me], sems[0].at[r], sems[1].at[r], peer).start()

    def finish(src, dst, sems):
        to, me = peers()
        for r, (px, py, pc) in enumerate(to):
            theirs = dst[0].at[4 * px + 2 * py + pc]
            _remote(theirs, theirs, sems[0].at[r], sems[1].at[r], (px, py, pc)).wait_recv()
        for r, peer in enumerate(to):
            _remote(src[0], dst[0].at[me], sems[0].at[r], sems[1].at[r], peer).wait_send()

    return _Exchange([small], [jax.ShapeDtypeStruct((8,) + small.shape, small.dtype)], {},
                     [pltpu.SemaphoreType.DMA((7,))] * 2, start, finish)


def _small_adamw(gathered, small, me, w_vec, m_vec, v_vec):
    n_par = w_vec.shape[1]

    def body(me_ref, a_ref, s_ref, w_ref, m_ref, v_ref, loss_ref, g_ref, d_ref, nm_ref, nv_ref):
        mine = s_ref[...]
        tot = jnp.where(me_ref[0] == 0, mine, a_ref[0])
        for d in range(1, 8):
            tot = tot + jnp.where(me_ref[0] == d, mine, a_ref[d])
        tot = jnp.sum(tot, axis=0, keepdims=True)
        sq = jnp.sum(tot[:, n_par:], axis=1, keepdims=True)
        loss_ref[...] = jnp.broadcast_to(sq * (0.5 / D_MODEL), loss_ref.shape)
        g = tot[:, :n_par]
        g_ref[...] = g
        d_ref[...], nm_ref[...], nv_ref[...] = _adam_math(w_ref[...], g, m_ref[...], v_ref[...])

    vm = pl.BlockSpec(memory_space=pltpu.VMEM)
    vec = jax.ShapeDtypeStruct((1, n_par), F32)
    return pl.pallas_call(
        body, name="small_adamw",
        grid_spec=pltpu.PrefetchScalarGridSpec(num_scalar_prefetch=1, grid=(), in_specs=[vm] * 5, out_specs=[vm] * 5),
        out_shape=[jax.ShapeDtypeStruct((1, 128), F32), vec, vec, vec, vec],
    )(me, gathered, small, w_vec, m_vec, v_vec)


def _adamw(w, g, m, v, name):
    r, c = w.shape
    tr = min(r, 256)

    def body(w_ref, g_ref, m_ref, v_ref, go_ref, d_ref, nm_ref, nv_ref):
        g = g_ref[...]
        go_ref[...] = g
        d_ref[...], nm_ref[...], nv_ref[...] = _adam_math(w_ref[...], g, m_ref[...], v_ref[...])

    spec = pl.BlockSpec((tr, c), lambda i: (i, 0))
    shape = jax.ShapeDtypeStruct((r, c), F32)
    return _pallas(
        body, name=name, grid=(r // tr,),
        in_specs=[spec] * 4, out_specs=[spec] * 4, out_shape=[shape] * 4,
        compiler_params=_params(("parallel",), 48),
    )(w, g, m, v)


def kernel(x, w_in, w_pool, pool_scale, w_out, ln_gain, ln_bias, loss_target, m_w_in, m_w_pool, m_pool_scale, m_w_out, m_ln_gain, m_ln_bias, v_w_in, v_w_pool, v_pool_scale, v_w_out, v_ln_gain, v_ln_bias):
    xi, yi, ci = lax.axis_index("x"), lax.axis_index("y"), lax.axis_index("c")
    chip = (2 * xi + yi).astype(jnp.int32).reshape(1)
    core = ci.astype(jnp.int32).reshape(1)
    n_groups = len(POOL_WINDOWS)
    shard_c = w_pool.shape[2]

    w_in_b = _cast_bf16(w_in[0], chip, "cast_w_in", 256)
    w_out_b = _cast_bf16(w_out[0], chip, "cast_w_out", 256)
    w_pool_b = _cast_bf16(w_pool[0].reshape(n_groups * shard_c, POOL_GROUP_DIM), chip, "cast_w_pool", 256)

    chip_core = jnp.concatenate([chip, core])
    onward = (2 * (xi ^ ci) + (yi ^ (1 - ci))).astype(jnp.int32).reshape(1)
    g_x, full_in, full_out, full_pool, small, small_all = _step(
        x[0], loss_target[0], [w_in_b, w_out_b, w_pool_b], pool_scale, ln_gain, ln_bias,
        (core, chip_core, onward, _in_proj_plan(xi, yi)))
    half_c = shard_c // 2
    grad_w_in = full_in.reshape(D_MODEL, SHARD_IN)
    grad_w_out = full_out.reshape(D_MODEL // N_SHARDS, D_MODEL)
    grad_w_pool = (full_pool.reshape(2, n_groups, half_c, POOL_GROUP_DIM).transpose(1, 0, 2, 3)
                   .reshape(n_groups * shard_c, POOL_GROUP_DIM))

    grad_w_in, d_in, nm_in, nv_in = _adamw(w_in[0], grad_w_in, m_w_in[0], v_w_in[0], "adamw_w_in")
    grad_w_out, d_out, nm_out, nv_out = _adamw(w_out[0], grad_w_out, m_w_out[0], v_w_out[0], "adamw_w_out")
    flat = lambda t: t[0].reshape(n_groups * shard_c, POOL_GROUP_DIM)
    grad_w_pool, d_pool, nm_pool, nv_pool = _adamw(flat(w_pool), grad_w_pool, flat(m_w_pool), flat(v_w_pool),
                                                   "adamw_w_pool")

    cat = lambda a, b, c: jnp.concatenate([a, b, c], axis=1)
    me = (4 * xi + 2 * yi + ci).astype(jnp.int32).reshape(1)
    loss_v, g_vec, d_vec, nm_vec, nv_vec = _small_adamw(
        small_all, small, me, cat(pool_scale, ln_gain, ln_bias), cat(m_pool_scale, m_ln_gain, m_ln_bias),
        cat(v_pool_scale, v_ln_gain, v_ln_bias))

    def split(vec):
        return vec[:, :D_POOL], vec[:, D_POOL:D_POOL + D_MODEL], vec[:, D_POOL + D_MODEL:]

    g_scale, g_gain, g_bias = split(g_vec)
    d_scale, d_gain, d_bias = split(d_vec)
    nm_scale, nm_gain, nm_bias = split(nm_vec)
    nv_scale, nv_gain, nv_bias = split(nv_vec)
    pool_shape = w_pool.shape
    return (loss_v[0, 0], g_x[None],
            grad_w_in[None], grad_w_pool.reshape(pool_shape), g_scale, grad_w_out[None], g_gain, g_bias,
            d_in[None], d_pool.reshape(pool_shape), d_scale, d_out[None], d_gain, d_bias,
            nm_in[None], nm_pool.reshape(pool_shape), nm_scale, nm_out[None], nm_gain, nm_bias,
            nv_in[None], nv_pool.reshape(pool_shape), nv_scale, nv_out[None], nv_gain, nv_bias)
```

```python
import functools

import jax
import jax.numpy as jnp
from jax import lax
from jax.experimental import pallas as pl
from jax.experimental.pallas import tpu as pltpu

F32 = jnp.float32
BF16 = jnp.bfloat16
MESH = pl.DeviceIdType.MESH
ANY = pl.BlockSpec(memory_space=pl.ANY)

D_MODEL = 2048
D_ATTN = 1024
D_POOL = 1024
HEAD_DIM = 128
N_HEADS = 8
ROPE_DIM = 32
ROPE_THETA = 500000.0
DILATIONS = (1, 4, 16)
KEY_BLOCK = 128
CHUNK = 2 * KEY_BLOCK
STAT_LANES = 128
POOL_WINDOWS = (2, 4, 8, 16)
POOL_GROUP_DIM = 256
POOL_HALO = 16
D_QKV = 3 * D_ATTN
D_UG = D_POOL + D_MODEL
D_IN = D_QKV + D_UG
N_SHARDS = 4
SHARD_IN = D_IN // N_SHARDS
LN_EPS = 1e-5
DEEPNORM_ALPHA = 2.0 ** 0.25
ADAM_LR = 0.001
ADAM_B1 = 0.9
ADAM_B2 = 0.999
ADAM_EPS = 1e-08
ADAM_WD = 0.01
ADAM_STEP = 10
NEG = -1e30
MIB = 1024 * 1024


def _params(sem, vmem_mib):
    return pltpu.CompilerParams(dimension_semantics=sem, vmem_limit_bytes=vmem_mib * MIB)


def _pallas(body, **kwargs):
    pin = lambda s: pltpu.HBM(s.shape, s.dtype) if len(s.shape) >= 2 else s
    out_shape = kwargs.pop("out_shape")
    out_shape = [pin(s) for s in out_shape] if isinstance(out_shape, (list, tuple)) else pin(out_shape)
    call = pl.pallas_call(body, out_shape=out_shape, **kwargs)

    def run(*operands):
        return call(*[pltpu.with_memory_space_constraint(o, pltpu.HBM) if o.ndim >= 2 else o for o in operands])

    return run


class _Exchange:
    def __init__(self, operands, out_shape, aliases, sems, start, finish):
        self.operands, self.out_shape, self.aliases, self.sems = list(operands), list(out_shape), dict(aliases), list(sems)
        self.start, self.finish = start, finish


def _run_exchange(comm, name):
    n_in, n_out = len(comm.operands), len(comm.out_shape)

    def body(*refs):
        ins, outs, sems = refs[:n_in], refs[n_in:n_in + n_out], refs[n_in + n_out:]
        comm.start(ins, outs, sems)
        comm.finish(ins, outs, sems)

    return _pallas(
        body, name=name, in_specs=[ANY] * n_in, out_specs=[ANY] * n_out, out_shape=comm.out_shape,
        input_output_aliases=comm.aliases, scratch_shapes=comm.sems,
    )(*comm.operands)


def _call(body, *, name, grid, in_specs, out_specs, out_shape, scratch_shapes, semantics, vmem_mib, args,
          aliases=None, comm=None, prefetch=()):
    aliases = dict(aliases or {})
    n_pre, n_in, n_out, n_scr = len(prefetch), len(in_specs), len(out_specs), len(scratch_shapes)
    c_in, c_out = (len(comm.operands), len(comm.out_shape)) if comm else (0, 0)
    c_shapes, c_sems, c_operands = (comm.out_shape, comm.sems, comm.operands) if comm else ([], [], [])

    def hosted(*refs):
        pre, refs = refs[:n_pre], refs[n_pre:]
        a = n_in
        b = a + c_in
        c = b + n_out
        d = c + c_out
        e = d + n_scr
        if comm is None:
            body(*pre, *refs)
            return
        ids = [pl.program_id(k) for k in range(len(grid))]
        first = functools.reduce(jnp.logical_and, [i == 0 for i in ids])
        last = functools.reduce(jnp.logical_and, [i == g - 1 for i, g in zip(ids, grid)])

        @pl.when(first)
        def _():
            comm.start(refs[a:b], refs[c:d], refs[e:])

        body(*pre, *refs[:a], *refs[b:c], *refs[d:e])

        @pl.when(last)
        def _():
            comm.finish(refs[a:b], refs[c:d], refs[e:])

    if comm:
        semantics = ("arbitrary",) * len(grid)
        for i, o in comm.aliases.items():
            aliases[n_pre + n_in + i] = n_out + o
    outs = _pallas(
        hosted, name=name,
        grid_spec=pltpu.PrefetchScalarGridSpec(
            num_scalar_prefetch=n_pre, grid=grid, in_specs=list(in_specs) + [ANY] * c_in,
            out_specs=list(out_specs) + [ANY] * c_out, scratch_shapes=list(scratch_shapes) + c_sems),
        out_shape=list(out_shape) + c_shapes, input_output_aliases=aliases,
        compiler_params=_params(semantics, vmem_mib),
    )(*prefetch, *args, *c_operands)
    return list(outs[:n_out]), list(outs[n_out:])


def _dot_nn(a, b):
    return jnp.dot(a, b, preferred_element_type=F32)


def _dot_nt(a, b):
    return lax.dot_general(a, b, (((1,), (1,)), ((), ())), preferred_element_type=F32)


def _dot_tn(a, b):
    return lax.dot_general(a, b, (((0,), (0,)), ((), ())), preferred_element_type=F32)


def _fold_rows(a):
    r, c = a.shape
    return jnp.sum(a.reshape(r // 8, 8, c), axis=0)


def _cast_bf16(a, chip, name, rows):
    r, c = a.shape

    def body(chip_ref, a_ref, o_ref):
        o_ref[...] = a_ref[...].astype(BF16)

    return _pallas(
        body, name=name,
        grid_spec=pltpu.PrefetchScalarGridSpec(
            num_scalar_prefetch=1, grid=(r // rows,),
            in_specs=[pl.BlockSpec((rows, c), lambda i, chip_ref: (i, 0))],
            out_specs=pl.BlockSpec((None, rows, c), lambda i, chip_ref: (chip_ref[0], i, 0))),
        out_shape=jax.ShapeDtypeStruct((N_SHARDS, r, c), BF16),
        compiler_params=_params(("parallel",), 32),
    )(chip, a)


def _mesh_place():
    x, y, c = lax.axis_index("x"), lax.axis_index("y"), lax.axis_index("c")
    return x, y, c, [(1 - x, y), (x, 1 - y), (1 - x, 1 - y)]


def _remote(src, dst, send_sem, recv_sem, to):
    return pltpu.make_async_remote_copy(src_ref=src, dst_ref=dst, send_sem=send_sem, recv_sem=recv_sem,
                                        device_id=to, device_id_type=MESH)


def _rope_tables(seq):
    half = ROPE_DIM // 2
    inv_freq = ROPE_THETA ** (-(2.0 * jnp.arange(half, dtype=F32)) / ROPE_DIM)
    per_row = HEAD_DIM // half
    lane = jnp.arange(HEAD_DIM, dtype=jnp.int32)
    pos = (jnp.arange(seq // per_row, dtype=jnp.int32)[:, None] * per_row + (lane // half)[None, :]).astype(F32)
    ang = pos * jnp.tile(inv_freq, per_row)[None, :]
    cos, sin = jnp.cos(ang).reshape(seq, half), jnp.sin(ang).reshape(seq, half)
    pad = jnp.zeros((seq, HEAD_DIM - ROPE_DIM), F32)
    zeros = jnp.zeros((seq, half), F32)
    c_tab = jnp.concatenate([cos, cos, pad + 1.0], axis=1)
    up_tab = jnp.concatenate([-sin, zeros, pad], axis=1)
    down_tab = jnp.concatenate([zeros, sin, pad], axis=1)
    return c_tab, up_tab, down_tab


def _rotate_heads(t, c_tab, up_tab, down_tab):
    outs = []
    for h in range(t.shape[1] // HEAD_DIM):
        th = t[:, h * HEAD_DIM:(h + 1) * HEAD_DIM]
        up = pltpu.roll(th, HEAD_DIM - ROPE_DIM // 2, axis=1)
        down = pltpu.roll(th, ROPE_DIM // 2, axis=1)
        outs.append(th * c_tab + up * up_tab + down * down_tab)
    return outs[0] if len(outs) == 1 else jnp.concatenate(outs, axis=1)


def _to_pattern(slabs_ref, dst_ref, dil, dtype):
    n_slabs, rows, _ = slabs_ref.shape
    for s in range(n_slabs):
        for r in range(dil):
            dst_ref[r, :, s * 128:(s + 1) * 128] = slabs_ref[s, pl.ds(r, rows // dil, dil), :].astype(dtype)


def _from_pattern(src_ref, slabs_ref, dil):
    n_slabs, rows, _ = slabs_ref.shape
    for s in range(n_slabs):
        for r in range(dil):
            slabs_ref[s, pl.ds(r, rows // dil, dil), :] = src_ref[r, :, s * 128:(s + 1) * 128].astype(F32)


def _store_slabs(slabs_ref, value):
    for s in range(slabs_ref.shape[0]):
        slabs_ref[s] = value[:, s * 128:(s + 1) * 128]


W_IN_CHUNKS = 4


def _in_proj_plan(x, y):
    shards = [2 * x + y, 2 * (1 - x) + y, 2 * x + (1 - y), 2 * (1 - x) + (1 - y)]
    last_row = jnp.int32(-2)

    def table(active, col_of):
        cols, rows = [], []
        first_col = functools.reduce(lambda acc, j: jnp.where(active[j], col_of(shards[j]), acc), reversed(range(4)),
                                     jnp.int32(0))
        held_col, seen = first_col, jnp.bool_(False)
        for j in range(4):
            cols.append(jnp.where(active[j], col_of(shards[j]), held_col))
            rows.append(jnp.where(active[j], -1, jnp.where(seen, last_row, 0)))
            held_col = jnp.where(active[j], col_of(shards[j]), held_col)
            seen = jnp.logical_or(seen, active[j])
        return cols, rows

    q_cols, q_rows = table([s < 2 for s in shards], lambda s: s)
    h_cols, h_rows = table([s >= 2 for s in shards], lambda s: s - 2)
    return jnp.stack([jnp.asarray(v, jnp.int32) for v in shards + q_cols + q_rows + h_cols + h_rows])


def _in_proj_gathering(x, w_bufs, tabs, plan):
    seq = x.shape[0]
    tm, tn = 512, SHARD_IN
    n_tiles = seq // tm
    heads = tn // HEAD_DIM
    k_heads_in_second = 2 * D_ATTN // HEAD_DIM - heads
    d4, d16 = DILATIONS[1], DILATIONS[2]
    DIAGONAL = 2
    chunk = D_MODEL // 2 // W_IN_CHUNKS
    early = [(0, D_MODEL // 2, q * chunk, chunk) for q in range(W_IN_CHUNKS)]
    late = [(a, w_bufs[a].shape[1] // 2, 0, w_bufs[a].shape[1] // 2) for a in (1, 2)]
    pieces = early + late
    early_ids, late_ids = range(len(early)), range(len(early), len(pieces))

    def body(plan_ref, x_ref, w_in_in, w_out_in, w_pool_in, c_ref, up_ref, down_ref,
             o1_ref, o4_ref, o16_ref, hug_ref, w_ref, w_out_ref, w_pool_ref,
             wbuf_ref, res_ref, w_sem, ici_send, ici_recv, d2d_send, d2d_recv):
        j, i = pl.program_id(0), pl.program_id(1)
        mx, my, mc, chips = _mesh_place()
        sibling = (mx, my, 1 - mc)
        gathered = (w_ref, w_out_ref, w_pool_ref)
        chip_of = lambda k: 2 * chips[k][0] + chips[k][1]

        def piece(n, chip, core):
            a, per_core, offset, size = pieces[n]
            return gathered[a].at[chip, pl.ds(core * per_core + offset, size)]

        def to_neighbour(k, n):
            mine = piece(n, 2 * mx + my, mc)
            return _remote(mine, mine, ici_send.at[n, k], ici_recv.at[n, k], (*chips[k], mc))

        def relay(n):
            theirs = piece(n, 2 * (mx ^ (1 - mc)) + (my ^ mc), mc)
            return _remote(theirs, theirs, ici_send.at[n, DIAGONAL], ici_recv.at[n, DIAGONAL], (mx ^ mc, my ^ (1 - mc), mc))

        def arrival(k, n):
            theirs = piece(n, chip_of(k), mc)
            return _remote(theirs, theirs, ici_send.at[n, k], ici_recv.at[n, k], (*chips[k], mc))

        def to_sibling(k, n, core):
            theirs = piece(n, chip_of(k), core)
            return _remote(theirs, theirs, d2d_send.at[n, k], d2d_recv.at[n, k], sibling)

        def take(k, ids):
            for n in ids:
                arrival(k, n).wait_recv()
                to_sibling(k, n, mc).start()

        def taken(k, ids):
            for n in ids:
                to_sibling(k, n, 1 - mc).wait_recv()

        first_tile = i == 0

        @pl.when(jnp.logical_and(j == 0, first_tile))
        def _():
            for n in range(len(pieces)):
                for k in range(DIAGONAL):
                    to_neighbour(k, n).start()

        @pl.when(jnp.logical_and(j == 1, first_tile))
        def _():
            take(0, early_ids)
            taken(0, early_ids)

        @pl.when(jnp.logical_and(j == 2, first_tile))
        def _():
            take(1, early_ids)
            for n in early_ids:
                relay(n).start()
            taken(1, early_ids)
            for k in range(DIAGONAL):
                take(k, late_ids)
            for n in late_ids:
                relay(n).start()
            for k in range(DIAGONAL):
                taken(k, late_ids)

        @pl.when(jnp.logical_and(j == 3, first_tile))
        def _():
            take(DIAGONAL, range(len(pieces)))
            taken(DIAGONAL, range(len(pieces)))

        shard = plan_ref[j]

        @pl.when(first_tile)
        def _():
            cp = pltpu.make_async_copy(w_ref.at[shard], wbuf_ref, w_sem)
            cp.start()
            cp.wait()

        xb = x_ref[...].astype(BF16)
        group = 4 * HEAD_DIM
        accs = [_dot_nn(xb, wbuf_ref[:, g * group:(g + 1) * group]) for g in range(tn // group)]

        def emit_qkv(rotated_heads):
            for h in range(heads):
                lanes = (h * HEAD_DIM) % group
                th = accs[h * HEAD_DIM // group][:, lanes:lanes + HEAD_DIM]
                if h < rotated_heads:
                    th = _rotate_heads(th, c_ref[...], up_ref[...], down_ref[...])
                res_ref[h] = th
                o1_ref[:, h * HEAD_DIM:(h + 1) * HEAD_DIM] = th.astype(BF16)
            _to_pattern(res_ref, o4_ref, d4, BF16)
            _to_pattern(res_ref, o16_ref, d16, BF16)

        @pl.when(shard == 0)
        def _():
            emit_qkv(heads)

        @pl.when(shard == 1)
        def _():
            emit_qkv(k_heads_in_second)

        @pl.when(shard >= 2)
        def _():
            for g, acc in enumerate(accs):
                hug_ref[:, g * group:(g + 1) * group] = acc.astype(BF16)

        @pl.when(jnp.logical_and(j == 3, i == n_tiles - 1))
        def _():
            for n in range(len(pieces)):
                for k in range(DIAGONAL):
                    to_neighbour(k, n).wait_send()
                relay(n).wait_send()
                for k in range(DIAGONAL + 1):
                    to_sibling(k, n, mc).wait_send()

    def held(base, last):
        return lambda j, i, plan_ref: jnp.where(plan_ref[base + j] == -1, i,
                                                jnp.where(plan_ref[base + j] == -2, last, 0))

    q_row, h_row = held(8, n_tiles - 1), held(16, n_tiles - 1)
    tab_spec = pl.BlockSpec((tm, HEAD_DIM), lambda j, i, plan_ref: (i, 0))
    sems = [pltpu.SemaphoreType.DMA((len(pieces), 3))] * 4
    o1, o4, o16, hug, w_in_g, w_out_g, w_pool_g = _pallas(
        body, name="in_proj_gathering",
        grid_spec=pltpu.PrefetchScalarGridSpec(
            num_scalar_prefetch=1, grid=(N_SHARDS, n_tiles),
            in_specs=[pl.BlockSpec((tm, D_MODEL), lambda j, i, plan_ref: (i, 0)), ANY, ANY, ANY,
                      tab_spec, tab_spec, tab_spec],
            out_specs=[pl.BlockSpec((tm, tn), lambda j, i, p: (q_row(j, i, p), p[4 + j])),
                       pl.BlockSpec((d4, tm // d4, tn), lambda j, i, p: (0, q_row(j, i, p), p[4 + j])),
                       pl.BlockSpec((d16, tm // d16, tn), lambda j, i, p: (0, q_row(j, i, p), p[4 + j])),
                       pl.BlockSpec((tm, tn), lambda j, i, p: (h_row(j, i, p), p[12 + j])),
                       ANY, ANY, ANY],
            scratch_shapes=[pltpu.VMEM((D_MODEL, tn), BF16), pltpu.VMEM((heads, tm, HEAD_DIM), F32),
                            pltpu.SemaphoreType.DMA(())] + sems),
        out_shape=[jax.ShapeDtypeStruct((seq, D_QKV), BF16),
                   jax.ShapeDtypeStruct((d4, seq // d4, D_QKV), BF16),
                   jax.ShapeDtypeStruct((d16, seq // d16, D_QKV), BF16),
                   jax.ShapeDtypeStruct((seq, D_UG), BF16)]
        + [jax.ShapeDtypeStruct(b.shape, b.dtype) for b in w_bufs],
        input_output_aliases={2: 4, 3: 5, 4: 6},
        compiler_params=_params(("arbitrary", "arbitrary"), 52),
    )(plan, x, *w_bufs, *tabs)
    return [o1[None], o4, o16], hug, w_in_g, w_out_g, w_pool_g


def _band_masks():
    row = lax.broadcasted_iota(jnp.int32, (KEY_BLOCK, KEY_BLOCK), 0)
    col = lax.broadcasted_iota(jnp.int32, (KEY_BLOCK, KEY_BLOCK), 1)
    return col <= row, col >= row


def _attn_fwd(qkv, name):
    dil, n, _ = qkv.shape
    scale = HEAD_DIM ** -0.5
    lo, hi = slice(0, KEY_BLOCK), slice(KEY_BLOCK, CHUNK)

    def body(q_ref, k_ref, v_ref, kb_ref, vb_ref, o_ref, st_ref):
        i = pl.program_id(1)
        cur_mask, prev_mask = _band_masks()
        before_mask = jnp.logical_and(prev_mask, i > 0)
        lane = lax.broadcasted_iota(jnp.int32, (KEY_BLOCK, STAT_LANES), 1)
        tasks = [(rows, h) for rows in (lo, hi) for h in range(N_HEADS)]
        head = lambda h: slice(h * HEAD_DIM, (h + 1) * HEAD_DIM)

        def prev_of(rows, h):
            if rows is lo:
                return kb_ref[:, head(h)], vb_ref[:, head(h)], before_mask
            return k_ref[lo, head(h)], v_ref[lo, head(h)], prev_mask

        scores = []
        for rows, h in tasks:
            q = q_ref[rows, head(h)]
            scores.append((_dot_nt(q, prev_of(rows, h)[0]), _dot_nt(q, k_ref[rows, head(h)])))
        probs = []
        for (rows, h), (qk_prev, qk_cur) in zip(tasks, scores):
            s_prev = jnp.where(prev_of(rows, h)[2], qk_prev * scale, NEG)
            s_cur = jnp.where(cur_mask, qk_cur * scale, NEG)
            m = jnp.max(jnp.maximum(s_prev, s_cur), axis=-1, keepdims=True)
            p_prev = jnp.exp(s_prev - m)
            p_cur = jnp.exp(s_cur - m)
            den = jnp.sum(p_prev + p_cur, axis=-1, keepdims=True)
            probs.append((p_prev.astype(BF16), p_cur.astype(BF16), den, m + jnp.log(den)))
        stats = [jnp.zeros((KEY_BLOCK, STAT_LANES), F32), jnp.zeros((KEY_BLOCK, STAT_LANES), F32)]
        for (rows, h), (p_prev, p_cur, den, lse) in zip(tasks, probs):
            o = _dot_nn(p_cur, v_ref[rows, head(h)]) + _dot_nn(p_prev, prev_of(rows, h)[1])
            o_ref[rows, head(h)] = (o / den).astype(BF16)
            b = 0 if rows is lo else 1
            stats[b] = jnp.where(lane == h, lse, stats[b])
        st_ref[lo, :] = stats[0]
        st_ref[hi, :] = stats[1]

    main = lambda cb: pl.BlockSpec((None, CHUNK, D_ATTN), lambda r, i: (r, i, cb))
    before = lambda cb: pl.BlockSpec((None, KEY_BLOCK, D_ATTN), lambda r, i: (r, jnp.maximum(2 * i - 1, 0), cb))
    return _pallas(
        body, name=name, grid=(dil, n // CHUNK),
        in_specs=[main(0), main(1), main(2), before(1), before(2)],
        out_specs=[main(0), pl.BlockSpec((None, CHUNK, STAT_LANES), lambda r, i: (r, i, 0))],
        out_shape=[jax.ShapeDtypeStruct((dil, n, D_ATTN), BF16), jax.ShapeDtypeStruct((dil, n, STAT_LANES), F32)],
        compiler_params=_params(("parallel", "parallel"), 40),
    )(qkv, qkv, qkv, qkv, qkv)


def _attn_bwd(qkv, do, stats, name, comm=None):
    dil, n, _ = qkv.shape
    n_blocks = n // KEY_BLOCK
    last = n // CHUNK - 1
    scale = HEAD_DIM ** -0.5
    lo, hi = slice(0, KEY_BLOCK), slice(KEY_BLOCK, CHUNK)

    def body(q_ref, k_ref, v_ref, kb_ref, vb_ref, qa_ref, do_ref, doa_ref, st_ref, sta_ref, dq_ref, dk_ref, dv_ref):
        i = pl.program_id(1)
        cur_mask, prev_mask = _band_masks()
        before_mask = jnp.logical_and(prev_mask, i > 0)
        after_mask = jnp.logical_and(prev_mask, i < last)

        rows_cat = lambda a, b: jnp.concatenate([a, b], axis=0)
        masks = (jnp.concatenate([before_mask, cur_mask], axis=1), jnp.concatenate([prev_mask, cur_mask], axis=1),
                 after_mask)

        def operands(h):
            cols = slice(h * HEAD_DIM, (h + 1) * HEAD_DIM)
            lse_c, del_c = slice(h, h + 1), slice(N_HEADS + h, N_HEADS + h + 1)
            q = (q_ref[lo, cols], q_ref[hi, cols], qa_ref[:, cols])
            do = (do_ref[lo, cols], do_ref[hi, cols], doa_ref[:, cols])
            keys = (rows_cat(kb_ref[:, cols], k_ref[lo, cols]), k_ref[:, cols], k_ref[hi, cols])
            vals = (rows_cat(vb_ref[:, cols], v_ref[lo, cols]), v_ref[:, cols], v_ref[hi, cols])
            st = ((st_ref[lo, lse_c], st_ref[lo, del_c]), (st_ref[hi, lse_c], st_ref[hi, del_c]),
                  (sta_ref[:, lse_c], sta_ref[:, del_c]))
            return cols, q, do, keys, vals, st

        group = N_HEADS // 2
        for first_head in range(0, N_HEADS, group):
            heads = range(first_head, first_head + group)
            raw = {}
            for h in heads:
                _, q, do, keys, vals, _ = operands(h)
                raw[h] = [(_dot_nt(q[j], keys[j]), _dot_nt(do[j], vals[j])) for j in range(3)]
            grads = {}
            for h in heads:
                st = operands(h)[5]
                grads[h] = []
                for j in range(3):
                    qk, dp = raw[h][j]
                    lse, delta = st[j]
                    p = jnp.exp(jnp.where(masks[j], qk * scale, NEG) - lse)
                    grads[h].append((p.astype(BF16), (p * (dp - delta) * scale).astype(BF16)))
            for h in heads:
                cols, q, do, keys, _, _ = operands(h)
                (p0, ds0), (p1, ds1), (pa, dsa) = grads[h]
                own, nxt = slice(KEY_BLOCK, CHUNK), slice(0, KEY_BLOCK)

                def put(ref, rows, val, cols=cols):
                    ref[rows, cols] = val.astype(ref.dtype)

                put(dq_ref, lo, _dot_nn(ds0, keys[0]))
                put(dq_ref, hi, _dot_nn(ds1, keys[1]))
                put(dk_ref, lo, _dot_tn(rows_cat(ds0[:, own], ds1[:, nxt]), q_ref[:, cols]))
                put(dk_ref, hi, _dot_tn(rows_cat(ds1[:, own], dsa), rows_cat(q[1], q[2])))
                put(dv_ref, lo, _dot_tn(rows_cat(p0[:, own], p1[:, nxt]), do_ref[:, cols]))
                put(dv_ref, hi, _dot_tn(rows_cat(p1[:, own], pa), rows_cat(do[1], do[2])))

    def spec(rows, width, row_of, cb):
        return pl.BlockSpec((None, rows, width), lambda r, i: (r, row_of(i), cb))

    same = lambda i: i
    before = lambda i: jnp.maximum(2 * i - 1, 0)
    after = lambda i: jnp.minimum(2 * i + 2, n_blocks - 1)
    out = spec(CHUNK, D_ATTN, same, 0)
    return _call(
        body, name=name, grid=(dil, n // CHUNK),
        in_specs=[spec(CHUNK, D_ATTN, same, 0), spec(CHUNK, D_ATTN, same, 1), spec(CHUNK, D_ATTN, same, 2),
                  spec(KEY_BLOCK, D_ATTN, before, 1), spec(KEY_BLOCK, D_ATTN, before, 2),
                  spec(KEY_BLOCK, D_ATTN, after, 0),
                  spec(CHUNK, D_ATTN, same, 0), spec(KEY_BLOCK, D_ATTN, after, 0),
                  spec(CHUNK, STAT_LANES, same, 0), spec(KEY_BLOCK, STAT_LANES, after, 0)],
        out_specs=[out, out, out],
        out_shape=[jax.ShapeDtypeStruct((dil, n, D_ATTN), BF16)] * 3,
        scratch_shapes=[], semantics=("parallel", "parallel"), vmem_mib=40,
        args=(qkv, qkv, qkv, qkv, qkv, qkv, do, do, stats, stats), comm=comm)


def _window_sums(ext, window, backward):
    rows = ext.shape[0]
    acc, span = ext, 1
    while span < window:
        acc = acc + pltpu.roll(acc, (rows - span) if backward else span, axis=0)
        span *= 2
    return acc


def _mix_gate(o_list, st_list, hug, w_pool_g, pool_scale):
    seq = hug.shape[0]
    tm = 256
    halo_blocks = tm // POOL_HALO
    d4, d16 = DILATIONS[1], DILATIONS[2]

    def body(o1_ref, o4_ref, o16_ref, l1_ref, l4_ref, l16_ref, u_ref, halo_ref, ga_ref, gp_ref, wp_ref, sc_ref,
             y_ref, mix_ref, lse_ref, pooled_ref, n4_ref, n16_ref, nl4_ref, nl16_ref):
        i = pl.program_id(0)
        _from_pattern(o4_ref, n4_ref, d4)
        _from_pattern(o16_ref, n16_ref, d16)
        _from_pattern(l4_ref, nl4_ref, d4)
        _from_pattern(l16_ref, nl16_ref, d16)
        la, lb, lc = l1_ref[...], nl4_ref[0], nl16_ref[0]
        mx = jnp.maximum(jnp.maximum(la, lb), lc)
        ea, eb, ec = jnp.exp(la - mx), jnp.exp(lb - mx), jnp.exp(lc - mx)
        tot = ea + eb + ec
        lse_ref[...] = mx + jnp.log(tot)
        wa, wb, wc = ea / tot, eb / tot, ec / tot
        ga = ga_ref[...].astype(F32)
        silu_a = ga * jax.nn.sigmoid(ga)
        for h in range(N_HEADS):
            cols = slice(h * HEAD_DIM, (h + 1) * HEAD_DIM)
            hc = slice(h, h + 1)
            attn = wa[:, hc] * o1_ref[:, cols].astype(F32) + wb[:, hc] * n4_ref[h] + wc[:, hc] * n16_ref[h]
            mix_ref[:, cols] = attn.astype(BF16)
            y_ref[:, cols] = (attn * silu_a[:, cols]).astype(BF16)

        u = u_ref[...].astype(F32)
        halo = jnp.where(i > 0, halo_ref[...].astype(F32), 0.0)
        ext = jnp.concatenate([halo, u], axis=0)
        pos = i * tm + lax.broadcasted_iota(jnp.int32, (tm, 1), 0)
        gp = gp_ref[...].astype(F32)
        gated_scale = sc_ref[...] * (gp * jax.nn.sigmoid(gp))
        for g, window in enumerate(POOL_WINDOWS):
            cols = slice(g * POOL_GROUP_DIM, (g + 1) * POOL_GROUP_DIM)
            sums = _window_sums(ext[:, cols], window, backward=False)[POOL_HALO:, :]
            count = jnp.minimum(pos + 1, window).astype(F32)
            pooled = (sums / count - u[:, cols]).astype(BF16)
            pooled_ref[:, cols] = pooled
            pre = _dot_nn(pooled, wp_ref[g])
            out_cols = slice(D_ATTN + g * POOL_GROUP_DIM, D_ATTN + (g + 1) * POOL_GROUP_DIM)
            mix_ref[:, out_cols] = pre.astype(BF16)
            y_ref[:, out_cols] = (pre * gated_scale[:, cols]).astype(BF16)

    row = lambda width, cb=0: pl.BlockSpec((tm, width), lambda i: (i, cb))
    pat = lambda d, width: pl.BlockSpec((d, tm // d, width), lambda i: (0, i, 0))
    return _pallas(
        body, name="mix_gate", grid=(seq // tm,),
        in_specs=[row(D_ATTN), pat(d4, D_ATTN), pat(d16, D_ATTN),
                  row(STAT_LANES), pat(d4, STAT_LANES), pat(d16, STAT_LANES),
                  row(D_POOL),
                  pl.BlockSpec((POOL_HALO, D_POOL), lambda i: (jnp.maximum(i * halo_blocks - 1, 0), 0)),
                  row(D_ATTN, 1), row(D_POOL, 2),
                  pl.BlockSpec((len(POOL_WINDOWS), POOL_GROUP_DIM, POOL_GROUP_DIM), lambda i: (0, 0, 0)),
                  pl.BlockSpec((1, D_POOL), lambda i: (0, 0))],
        out_specs=[row(D_MODEL), row(D_MODEL), row(STAT_LANES), row(D_POOL)],
        out_shape=[jax.ShapeDtypeStruct((seq, D_MODEL), BF16), jax.ShapeDtypeStruct((seq, D_MODEL), BF16),
                   jax.ShapeDtypeStruct((seq, STAT_LANES), F32), jax.ShapeDtypeStruct((seq, D_POOL), BF16)],
        scratch_shapes=[pltpu.VMEM((N_HEADS, tm, HEAD_DIM), F32), pltpu.VMEM((N_HEADS, tm, HEAD_DIM), F32),
                        pltpu.VMEM((1, tm, STAT_LANES), F32), pltpu.VMEM((1, tm, STAT_LANES), F32)],
        compiler_params=_params(("parallel",), 48),
    )(o_list[0][0], o_list[1], o_list[2], st_list[0][0], st_list[1], st_list[2],
      hug, hug, hug, hug, w_pool_g, pool_scale)


def _out_proj_loss(y, w_out_g, x, target, gain, bias):
    seq = x.shape[0]
    tm = 512

    def body(y_ref, w_ref, x_ref, t_ref, g_ref, b_ref, dz_ref, dzb_ref, gg_ref, gb_ref, loss_ref):
        @pl.when(pl.program_id(0) == 0)
        def _():
            gg_ref[...] = jnp.zeros_like(gg_ref)
            gb_ref[...] = jnp.zeros_like(gb_ref)
            loss_ref[...] = jnp.zeros_like(loss_ref)

        halves = [slice(0, tm // 2), slice(tm // 2, tm)]
        projected = [_dot_nn(y_ref[rows, :], w_ref[...]) for rows in halves]
        for rows, out in zip(halves, projected):
            z = DEEPNORM_ALPHA * x_ref[rows, :] + out
            mu = jnp.mean(z, axis=-1, keepdims=True)
            zc = z - mu
            rstd = lax.rsqrt(jnp.mean(zc * zc, axis=-1, keepdims=True) + LN_EPS)
            xhat = zc * rstd
            gain_v = g_ref[...]
            diff = xhat * gain_v + b_ref[...] - t_ref[rows, :]
            sq = _fold_rows(diff * diff)
            part = sq[:, :128]
            for k in range(1, D_MODEL // 128):
                part = part + sq[:, k * 128:(k + 1) * 128]
            loss_ref[...] += part
            dln = diff * (1.0 / D_MODEL)
            gg_ref[...] += _fold_rows(dln * xhat)
            gb_ref[...] += _fold_rows(dln)
            dxhat = dln * gain_v
            dz = rstd * (dxhat - jnp.mean(dxhat, axis=-1, keepdims=True)
                         - xhat * jnp.mean(dxhat * xhat, axis=-1, keepdims=True))
            dz_ref[rows, :] = dz
            dzb_ref[rows, :] = dz.astype(BF16)

    row = lambda: pl.BlockSpec((tm, D_MODEL), lambda i: (i, 0))
    vec = lambda: pl.BlockSpec((1, D_MODEL), lambda i: (0, 0))
    acc = lambda width: pl.BlockSpec((8, width), lambda i: (0, 0))
    return _pallas(
        body, name="out_proj_loss", grid=(seq // tm,),
        in_specs=[row(), pl.BlockSpec((D_MODEL, D_MODEL), lambda i: (0, 0), pipeline_mode=pl.Buffered(1)),
                  row(), row(), vec(), vec()],
        out_specs=[row(), row(), acc(D_MODEL), acc(D_MODEL), acc(128)],
        out_shape=[jax.ShapeDtypeStruct((seq, D_MODEL), F32), jax.ShapeDtypeStruct((seq, D_MODEL), BF16),
                   jax.ShapeDtypeStruct((8, D_MODEL), F32), jax.ShapeDtypeStruct((8, D_MODEL), F32),
                   jax.ShapeDtypeStruct((8, 128), F32)],
        compiler_params=_params(("arbitrary",), 56),
    )(y, w_out_g.reshape(D_MODEL, D_MODEL), x, target, gain, bias)


def _dy_gate_bwd(dzb, w_out_g, hug, mixpre, pool_scale, lse_all):
    seq = dzb.shape[0]
    tm = 256
    d4, d16 = DILATIONS[1], DILATIONS[2]

    def body(dz_ref, w_ref, ga_ref, gp_ref, mix_ref, sc_ref, lse_ref,
             dh_ref, dpo_ref, do1_ref, do4_ref, do16_ref, st1_ref, st4_ref, st16_ref, da_ref, st_ref):
        dy = _dot_nt(dz_ref[...], w_ref[...])
        ga = ga_ref[...].astype(F32)
        sig = jax.nn.sigmoid(ga)
        attn = mix_ref[:, :D_ATTN].astype(F32)
        dya = dy[:, :D_ATTN]
        dattn = dya * (ga * sig)
        dh_ref[:, :D_ATTN] = (dya * attn * (sig * (1.0 + ga * (1.0 - sig)))).astype(BF16)
        _store_slabs(da_ref, dattn)
        lane = lax.broadcasted_iota(jnp.int32, (tm, STAT_LANES), 1)
        stats = lse_ref[...]
        prod = dattn * attn
        for h in range(N_HEADS):
            delta = jnp.sum(prod[:, h * HEAD_DIM:(h + 1) * HEAD_DIM], axis=-1, keepdims=True)
            stats = jnp.where(lane == N_HEADS + h, delta, stats)
        st_ref[0] = stats
        do1_ref[...] = dattn.astype(BF16)
        st1_ref[...] = stats
        _to_pattern(da_ref, do4_ref, d4, BF16)
        _to_pattern(da_ref, do16_ref, d16, BF16)
        _to_pattern(st_ref, st4_ref, d4, F32)
        _to_pattern(st_ref, st16_ref, d16, F32)

        gp = gp_ref[...].astype(F32)
        sig = jax.nn.sigmoid(gp)
        dyp = dy[:, D_ATTN:]
        dpo_ref[...] = (dyp * (gp * sig)).astype(BF16)
        dh_ref[:, D_ATTN:] = (dyp * (mix_ref[:, D_ATTN:].astype(F32) * sc_ref[...])
                              * (sig * (1.0 + gp * (1.0 - sig)))).astype(BF16)

    row = lambda width, cb=0: pl.BlockSpec((tm, width), lambda i: (i, cb))
    pat = lambda d, width: pl.BlockSpec((d, tm // d, width), lambda i: (0, i, 0))
    pat_shape = lambda d, width, dtype: jax.ShapeDtypeStruct((d, seq // d, width), dtype)
    outs = _pallas(
        body, name="dy_gate_bwd", grid=(seq // tm,),
        in_specs=[row(D_MODEL), pl.BlockSpec((D_MODEL, D_MODEL), lambda i: (0, 0)),
                  row(D_ATTN, 1), row(D_POOL, 2), row(D_MODEL), pl.BlockSpec((1, D_POOL), lambda i: (0, 0)),
                  row(STAT_LANES)],
        out_specs=[row(D_MODEL, D_IN // D_MODEL - 1), row(D_POOL),
                   row(D_ATTN), pat(d4, D_ATTN), pat(d16, D_ATTN),
                   row(STAT_LANES), pat(d4, STAT_LANES), pat(d16, STAT_LANES)],
        out_shape=[jax.ShapeDtypeStruct((seq, D_IN), BF16), jax.ShapeDtypeStruct((seq, D_POOL), BF16),
                   jax.ShapeDtypeStruct((seq, D_ATTN), BF16), pat_shape(d4, D_ATTN, BF16), pat_shape(d16, D_ATTN, BF16),
                   jax.ShapeDtypeStruct((seq, STAT_LANES), F32), pat_shape(d4, STAT_LANES, F32),
                   pat_shape(d16, STAT_LANES, F32)],
        scratch_shapes=[pltpu.VMEM((N_HEADS, tm, HEAD_DIM), F32), pltpu.VMEM((1, tm, STAT_LANES), F32)],
        compiler_params=_params(("parallel",), 48),
    )(dzb, w_out_g.reshape(D_MODEL, D_MODEL), hug, hug, mixpre, pool_scale, lse_all)
    dh, dpo, do1, do4, do16, st1, st4, st16 = outs
    return dh, dpo, [do1[None], do4, do16], [st1[None], st4, st16]


def _pool_bwd(dh, dpo, mixpre, pooled, w_pool_g, pool_scale):
    seq = dpo.shape[0]
    tm = 256
    halo_blocks = tm // POOL_HALO
    last = seq // tm - 1
    n_groups = len(POOL_WINDOWS)

    def body(dh_in_ref, dpo_ref, halo_ref, pre_ref, pooled_ref, wp_ref, sc_ref, du_ref, gw_ref, gs_ref):
        i = pl.program_id(0)

        @pl.when(i == 0)
        def _():
            gw_ref[...] = jnp.zeros_like(gw_ref)
            gs_ref[...] = jnp.zeros_like(gs_ref)

        dpo = dpo_ref[...].astype(F32)
        scale = sc_ref[...]
        gs_ref[...] += _fold_rows(dpo * pre_ref[...].astype(F32))
        halo = jnp.where(i < last, halo_ref[...].astype(F32), 0.0)
        dpw = (jnp.concatenate([dpo, halo], axis=0) * scale).astype(BF16)
        pos = i * tm + lax.broadcasted_iota(jnp.int32, (tm + POOL_HALO, 1), 0)
        for g, window in enumerate(POOL_WINDOWS):
            cols = slice(g * POOL_GROUP_DIM, (g + 1) * POOL_GROUP_DIM)
            dpw_g = dpw[:, cols]
            gw_ref[g] += _dot_tn(pooled_ref[:, cols], dpw_g[:tm, :])
            dpooled = _dot_nt(dpw_g, wp_ref[g])
            count = jnp.minimum(pos + 1, window).astype(F32)
            sums = _window_sums(dpooled / count, window, backward=True)
            du_ref[:, cols] = (sums[:tm, :] - dpooled[:tm, :]).astype(BF16)

    row = lambda width, cb=0: pl.BlockSpec((tm, width), lambda i: (i, cb))
    return _pallas(
        body, name="pool_bwd", grid=(seq // tm,),
        in_specs=[ANY, row(D_POOL),
                  pl.BlockSpec((POOL_HALO, D_POOL),
                               lambda i: (jnp.minimum((i + 1) * halo_blocks, seq // POOL_HALO - 1), 0)),
                  row(D_POOL, 1), row(D_POOL),
                  pl.BlockSpec((n_groups, POOL_GROUP_DIM, POOL_GROUP_DIM), lambda i: (0, 0, 0)),
                  pl.BlockSpec((1, D_POOL), lambda i: (0, 0))],
        out_specs=[row(D_POOL, D_QKV // D_POOL),
                   pl.BlockSpec((n_groups, POOL_GROUP_DIM, POOL_GROUP_DIM), lambda i: (0, 0, 0)),
                   pl.BlockSpec((8, D_POOL), lambda i: (0, 0))],
        out_shape=[jax.ShapeDtypeStruct(dh.shape, dh.dtype),
                   jax.ShapeDtypeStruct((n_groups, POOL_GROUP_DIM, POOL_GROUP_DIM), F32),
                   jax.ShapeDtypeStruct((8, D_POOL), F32)],
        input_output_aliases={0: 0},
        compiler_params=_params(("arbitrary",), 40),
    )(dh, dpo, dpo, mixpre, pooled, w_pool_g, pool_scale)


def _sum_patterns(dh, parts, tabs, unrotate, col_block, name, comm=None):
    seq = dh.shape[0]
    tm, tn = 256, D_ATTN
    per = D_ATTN // tn
    d4, d16 = DILATIONS[1], DILATIONS[2]

    def body(dh_in_ref, a1_ref, a4_ref, a16_ref, ct_ref, up_ref, down_ref, o_ref, n4_ref, n16_ref):
        _from_pattern(a4_ref, n4_ref, d4)
        _from_pattern(a16_ref, n16_ref, d16)
        for s in range(tn // HEAD_DIM):
            cols = slice(s * HEAD_DIM, (s + 1) * HEAD_DIM)
            tot = a1_ref[:, cols].astype(F32) + n4_ref[s] + n16_ref[s]
            if unrotate:
                tot = _rotate_heads(tot, ct_ref[...], -up_ref[...], -down_ref[...])
            o_ref[:, cols] = tot.astype(BF16)

    tab = pl.BlockSpec((tm, HEAD_DIM), lambda i, j: (i, 0))
    pat = lambda d: pl.BlockSpec((d, tm // d, tn), lambda i, j: (0, i, j))
    (dh,), exchanged = _call(
        body, name=name, grid=(seq // tm, per),
        in_specs=[ANY, pl.BlockSpec((tm, tn), lambda i, j: (i, j)), pat(d4), pat(d16), tab, tab, tab],
        out_specs=[pl.BlockSpec((tm, tn), lambda i, j: (i, col_block * per + j))],
        out_shape=[jax.ShapeDtypeStruct(dh.shape, dh.dtype)],
        scratch_shapes=[pltpu.VMEM((tn // HEAD_DIM, tm, HEAD_DIM), F32), pltpu.VMEM((tn // HEAD_DIM, tm, HEAD_DIM), F32)],
        semantics=("parallel", "parallel"), vmem_mib=32, args=(dh, parts[0][0], parts[1], parts[2], *tabs),
        aliases={0: 0}, comm=comm)
    return dh, exchanged


def _grad_w_in(x, dh, half, name, comm=None):
    seq = x.shape[0]
    ts, td, te = 2048, D_MODEL // 2, SHARD_IN

    def body(half_ref, x_ref, dh_ref, o_ref):
        k = pl.program_id(1)
        part = _dot_tn(x_ref[...].astype(BF16), dh_ref[...])

        @pl.when(k == 0)
        def _():
            o_ref[...] = part

        @pl.when(k > 0)
        def _():
            o_ref[...] += part

    (g,), exchanged = _call(
        body, name=name, grid=(N_SHARDS, seq // ts),
        in_specs=[pl.BlockSpec((ts, td), lambda e, k, half_ref: (k, half_ref[0])),
                  pl.BlockSpec((ts, te), lambda e, k, half_ref: (k, e))],
        out_specs=[pl.BlockSpec((None, td, te), lambda e, k, half_ref: (e, 0, 0))],
        out_shape=[jax.ShapeDtypeStruct((N_SHARDS, td, te), F32)],
        scratch_shapes=[], semantics=("parallel", "arbitrary"), vmem_mib=56, args=(x, dh), comm=comm,
        prefetch=(half,))
    return g, exchanged


def _grad_w_out(y, dzb):
    seq = y.shape[0]
    ts, te = 2048, 1024

    def body(y_ref, dz_ref, o_ref):
        k = pl.program_id(1)
        part = _dot_tn(y_ref[...], dz_ref[...])

        @pl.when(k == 0)
        def _():
            o_ref[...] = part

        @pl.when(k > 0)
        def _():
            o_ref[...] += part

    return _pallas(
        body, name="grad_w_out", grid=(D_MODEL // te, seq // ts),
        in_specs=[pl.BlockSpec((ts, te), lambda e, k: (k, e)), pl.BlockSpec((ts, D_MODEL), lambda e, k: (k, 0))],
        out_specs=pl.BlockSpec((te, D_MODEL), lambda e, k: (e, 0)),
        out_shape=jax.ShapeDtypeStruct((D_MODEL, D_MODEL), F32),
        compiler_params=_params(("parallel", "arbitrary"), 56),
    )(y, dzb)


GRAD_X_LATE_SHARDS = 1
GRAD_X_PARTIAL_ROWS = 512


def _grad_x_partial(dh, w_in_g, dz, first, tiles, prev=None, comm=None):
    seq = dh.shape[0]
    tm, tk = GRAD_X_PARTIAL_ROWS, SHARD_IN

    def body(*refs):
        dh_ref, w_ref, dz_ref, o_ref = refs[-4:]
        k = pl.program_id(1)
        part = _dot_nt(dh_ref[...], w_ref[...])

        @pl.when(k == 0)
        def _():
            o_ref[...] = DEEPNORM_ALPHA * dz_ref[...] + part

        @pl.when(k > 0)
        def _():
            o_ref[...] += part

    carried = [] if prev is None else [prev]
    row = pl.BlockSpec((tm, D_MODEL), lambda i, k: (i + first, 0))
    (partial,), exchanged = _call(
        body, name="grad_x_partial_%d" % first, grid=(tiles, N_SHARDS - GRAD_X_LATE_SHARDS),
        in_specs=[ANY] * len(carried) + [
            pl.BlockSpec((tm, tk), lambda i, k: (i + first, k)),
            pl.BlockSpec((None, D_MODEL, tk), lambda i, k: (k, 0, 0)), row],
        out_specs=[row],
        out_shape=[jax.ShapeDtypeStruct((seq, D_MODEL), F32)],
        scratch_shapes=[], semantics=("parallel", "arbitrary"), vmem_mib=48, args=(*carried, dh, w_in_g, dz),
        aliases={0: 0} if carried else None, comm=comm)
    return partial, exchanged


def _grad_x_final(dh, w_in_g, partial):
    seq = dh.shape[0]
    tm, tk = 512, SHARD_IN
    k0 = N_SHARDS - GRAD_X_LATE_SHARDS

    def body(dh_ref, w_ref, p_ref, o_ref):
        k = pl.program_id(1)
        part = _dot_nt(dh_ref[...], w_ref[...])

        @pl.when(k == 0)
        def _():
            o_ref[...] = p_ref[...] + part

        @pl.when(k > 0)
        def _():
            o_ref[...] += part

    row = pl.BlockSpec((tm, D_MODEL), lambda i, k: (i, 0))
    return _pallas(
        body, name="grad_x_final", grid=(seq // tm, GRAD_X_LATE_SHARDS),
        in_specs=[pl.BlockSpec((tm, tk), lambda i, k: (i, k + k0)),
                  pl.BlockSpec((None, D_MODEL, tk), lambda i, k: (k + k0, 0, 0)), row],
        out_specs=row, out_shape=jax.ShapeDtypeStruct((seq, D_MODEL), F32),
        compiler_params=_params(("parallel", "arbitrary"), 48),
    )(dh, w_in_g, partial)


def _pool_weight(w_pool_sh):
    n_groups = len(POOL_WINDOWS)
    shard_c = POOL_GROUP_DIM // N_SHARDS
    return (w_pool_sh.reshape(N_SHARDS, n_groups, shard_c, POOL_GROUP_DIM).transpose(1, 0, 2, 3)
            .reshape(n_groups, POOL_GROUP_DIM, POOL_GROUP_DIM))


def _pool_grad_pieces(g_w_pool):
    n_groups = len(POOL_WINDOWS)
    half_c = POOL_GROUP_DIM // N_SHARDS // 2
    return (g_w_pool.reshape(n_groups, N_SHARDS, 2, half_c, POOL_GROUP_DIM).transpose(1, 2, 0, 3, 4)
            .reshape(N_SHARDS, 2, n_groups * half_c, POOL_GROUP_DIM))


def _step(x, target, w_bufs, pool_scale, gain, bias, place):
    seq = x.shape[0]
    tabs = _rope_tables(seq)
    core, chip_core, onward, plan = place
    qkv, hug, w_in_g, w_out_g, w_pool_sh = _in_proj_gathering(x, w_bufs, tabs, plan)
    o_list, st_list = [], []
    for p, dil in enumerate(DILATIONS):
        o, st = _attn_fwd(qkv[p], "attn_fwd_d%d" % dil)
        o_list.append(o)
        st_list.append(st)
    w_pool_g = _pool_weight(w_pool_sh)
    y, mixpre, lse_all, pooled = _mix_gate(o_list, st_list, hug, w_pool_g, pool_scale)
    dz, dzb, gain_part, bias_part, loss_part = _out_proj_loss(y, w_out_g, x, target, gain, bias)
    dh, dpo, do_list, stat_list = _dy_gate_bwd(dzb, w_out_g, hug, mixpre, pool_scale, lse_all)
    g_w_out = _grad_w_out(y, dzb)
    dh, g_w_pool, scale_part = _pool_bwd(dh, dpo, mixpre, pooled, w_pool_g, pool_scale)
    small = jnp.concatenate([scale_part, gain_part, bias_part, loss_part], axis=1)
    early = [g_w_out.reshape(N_SHARDS, 2, D_MODEL // (2 * N_SHARDS), D_MODEL), _pool_grad_pieces(g_w_pool)]

    bwd = lambda p, comm: _attn_bwd(qkv[p], do_list[p], stat_list[p], "attn_bwd_d%d" % DILATIONS[p], comm)
    part_a, recv = bwd(0, _exchange_halves(early))
    sums = [_add_own_half(g, r, core, "add_own_half_%d" % a) for a, (g, r) in enumerate(zip(early, recv))]
    part_b, recv = bwd(1, _scatter_to_chips([s[1] for s in sums]))
    bufs = [_add_chips(s[0], r, chip_core, "add_chips_%d" % a) for a, (s, r) in enumerate(zip(sums, recv))]
    part_c, reduced = bwd(2, _share_with_sibling(bufs))
    parts = [part_a, part_b, part_c]
    dh, gathered = _sum_patterns(dh, [t[0] for t in parts], tabs, True, 0, "sum_dq", _gather_small(small))
    dh, _ = _sum_patterns(dh, [t[1] for t in parts], tabs, True, 1, "sum_dk")
    dh, _ = _sum_patterns(dh, [t[2] for t in parts], tabs, False, 2, "sum_dv")

    give, _ = _grad_w_in(x, dh, 1 - core, "grad_w_in_give")
    keep, recv = _grad_w_in(x, dh, core, "grad_w_in_keep", _send_to_sibling([give]))
    total, total_b = _add_pair(keep, recv[0], "add_own_half_w_in")
    n_tiles = seq // GRAD_X_PARTIAL_ROWS
    tiles = 3 * n_tiles // 8
    part, relayed = _grad_x_partial(dh, w_in_g, dz, 0, tiles, None, _relay_diagonal(total_b))
    total_b = _fold_relayed(total, total_b, relayed[0], onward)
    part, recv = _grad_x_partial(dh, w_in_g, dz, tiles, n_tiles - tiles, part, _scatter_to_neighbours(total_b))
    buf = _add_chips(total, recv[0], chip_core, "add_chips_w_in")
    g_x = _grad_x_final(dh, w_in_g, part)
    g_w_in = _run_exchange(_share_with_sibling([buf]), "share_w_in")[0]
    return g_x, g_w_in, reduced[0], reduced[1], small, gathered[0]


def _exchange_halves(grads):
    n = len(grads)

    def copies(src, dst, sems):
        x, y, c, _ = _mesh_place()
        return [_remote(src[a].at[j, 1 - c], dst[a].at[j], sems[0].at[a, j], sems[1].at[a, j], (x, y, 1 - c))
                for a in range(n) for j in range(N_SHARDS)]

    def start(src, dst, sems):
        for cp in copies(src, dst, sems):
            cp.start()

    def finish(src, dst, sems):
        for cp in copies(src, dst, sems):
            cp.wait()

    return _Exchange(grads, [jax.ShapeDtypeStruct((N_SHARDS,) + g.shape[2:], g.dtype) for g in grads], {},
                     [pltpu.SemaphoreType.DMA((n, N_SHARDS))] * 2, start, finish)


def _add_own_half(grad, recv, core, name):
    _, _, r, c = grad.shape
    tr = min(r, 256)

    def body(core_ref, g_ref, r_ref, o_ref, ob_ref):
        tot = g_ref[...] + r_ref[...]
        o_ref[...] = tot
        ob_ref[...] = tot.astype(BF16)

    out = pl.BlockSpec((None, tr, c), lambda j, i, core_ref: (j, i, 0))
    return _pallas(
        body, name=name,
        grid_spec=pltpu.PrefetchScalarGridSpec(
            num_scalar_prefetch=1, grid=(N_SHARDS, r // tr),
            in_specs=[pl.BlockSpec((None, None, tr, c), lambda j, i, core_ref: (j, core_ref[0], i, 0)),
                      pl.BlockSpec((None, tr, c), lambda j, i, core_ref: (j, i, 0))],
            out_specs=[out, out]),
        out_shape=[jax.ShapeDtypeStruct((N_SHARDS, r, c), F32), jax.ShapeDtypeStruct((N_SHARDS, r, c), BF16)],
        compiler_params=_params(("parallel", "parallel"), 32),
    )(core, grad, recv)


def _send_to_sibling(arrays):
    n = len(arrays)

    def copies(src, dst, sems):
        x, y, c, _ = _mesh_place()
        return [_remote(src[a], dst[a], sems[0].at[a], sems[1].at[a], (x, y, 1 - c)) for a in range(n)]

    def start(src, dst, sems):
        for cp in copies(src, dst, sems):
            cp.start()

    def finish(src, dst, sems):
        for cp in copies(src, dst, sems):
            cp.wait()

    return _Exchange(arrays, [jax.ShapeDtypeStruct(t.shape, t.dtype) for t in arrays], {},
                     [pltpu.SemaphoreType.DMA((n,))] * 2, start, finish)


def _add_pair(a, b, name):
    _, r, c = a.shape
    tr = min(r, 256)

    def body(a_ref, b_ref, o_ref, ob_ref):
        tot = a_ref[...] + b_ref[...]
        o_ref[...] = tot
        ob_ref[...] = tot.astype(BF16)

    spec = pl.BlockSpec((None, tr, c), lambda j, i: (j, i, 0))
    return _pallas(
        body, name=name, grid=(N_SHARDS, r // tr), in_specs=[spec, spec], out_specs=[spec, spec],
        out_shape=[jax.ShapeDtypeStruct(a.shape, F32), jax.ShapeDtypeStruct(a.shape, BF16)],
        compiler_params=_params(("parallel", "parallel"), 32),
    )(a, b)


def _scatter_to_chips(sums):
    n = len(sums)

    def copies(src, dst, sems):
        x, y, c, chips = _mesh_place()
        return [_remote(src[a].at[2 * cx + cy], dst[a].at[k], sems[0].at[a, k], sems[1].at[a, k], (cx, cy, c))
                for a in range(n) for k, (cx, cy) in enumerate(chips)]

    def start(src, dst, sems):
        for cp in copies(src, dst, sems):
            cp.start()

    def finish(src, dst, sems):
        for cp in copies(src, dst, sems):
            cp.wait()

    return _Exchange(sums, [jax.ShapeDtypeStruct((3,) + s.shape[1:], s.dtype) for s in sums], {},
                     [pltpu.SemaphoreType.DMA((n, 3))] * 2, start, finish)


def _add_chips(sums, recv, chip_core, name):
    _, r, c = sums.shape
    n_recv = recv.shape[0]
    tr = min(r, 256)

    def body(cc_ref, s_ref, r_ref, o_ref):
        tot = s_ref[...]
        for k in range(n_recv):
            tot = tot + r_ref[k].astype(F32)
        o_ref[...] = tot

    return _pallas(
        body, name=name,
        grid_spec=pltpu.PrefetchScalarGridSpec(
            num_scalar_prefetch=1, grid=(r // tr,),
            in_specs=[pl.BlockSpec((None, tr, c), lambda i, cc_ref: (cc_ref[0], i, 0)),
                      pl.BlockSpec((n_recv, tr, c), lambda i, cc_ref: (0, i, 0))],
            out_specs=pl.BlockSpec((None, tr, c), lambda i, cc_ref: (cc_ref[1], i, 0))),
        out_shape=jax.ShapeDtypeStruct((2, r, c), F32),
        compiler_params=_params(("parallel",), 32),
    )(chip_core, sums, recv)


def _relay_diagonal(sums_b):
    def copy(src, dst, sems):
        x, y, c, _ = _mesh_place()
        diagonal = 2 * (1 - x) + (1 - y)
        return _remote(src[0].at[diagonal], dst[0], sems[0].at[0], sems[1].at[0], (x ^ (1 - c), y ^ c, c))

    def start(src, dst, sems):
        copy(src, dst, sems).start()

    def finish(src, dst, sems):
        copy(src, dst, sems).wait()

    return _Exchange([sums_b], [jax.ShapeDtypeStruct(sums_b.shape[1:], sums_b.dtype)], {},
                     [pltpu.SemaphoreType.DMA((1,))] * 2, start, finish)


def _fold_relayed(sums, sums_b, relayed, onward):
    _, r, c = sums.shape
    tr = min(r, 256)

    def body(on_ref, b_in_ref, s_ref, r_ref, o_ref):
        o_ref[...] = (s_ref[...] + r_ref[...].astype(F32)).astype(BF16)

    return _pallas(
        body, name="fold_relayed",
        grid_spec=pltpu.PrefetchScalarGridSpec(
            num_scalar_prefetch=1, grid=(r // tr,),
            in_specs=[ANY, pl.BlockSpec((None, tr, c), lambda i, on_ref: (on_ref[0], i, 0)),
                      pl.BlockSpec((tr, c), lambda i, on_ref: (i, 0))],
            out_specs=pl.BlockSpec((None, tr, c), lambda i, on_ref: (on_ref[0], i, 0))),
        out_shape=jax.ShapeDtypeStruct(sums_b.shape, sums_b.dtype),
        input_output_aliases={1: 0},
        compiler_params=_params(("parallel",), 32),
    )(onward, sums_b, sums, relayed)


def _scatter_to_neighbours(sums_b):
    def copies(src, dst, sems):
        x, y, c, chips = _mesh_place()
        return [_remote(src[0].at[2 * cx + cy], dst[0].at[k], sems[0].at[k], sems[1].at[k], (cx, cy, c))
                for k, (cx, cy) in enumerate(chips[:2])]

    def start(src, dst, sems):
        for cp in copies(src, dst, sems):
            cp.start()

    def finish(src, dst, sems):
        for cp in copies(src, dst, sems):
            cp.wait()

    return _Exchange([sums_b], [jax.ShapeDtypeStruct((2,) + sums_b.shape[1:], sums_b.dtype)], {},
                     [pltpu.SemaphoreType.DMA((2,))] * 2, start, finish)


def _share_with_sibling(bufs):
    n = len(bufs)

    def copies(dst, sems, half):
        x, y, c, _ = _mesh_place()
        h = c if half == "mine" else 1 - c
        return [_remote(dst[a].at[h], dst[a].at[h], sems[0].at[a], sems[1].at[a], (x, y, 1 - c)) for a in range(n)]

    def start(ins, dst, sems):
        for cp in copies(dst, sems, "mine"):
            cp.start()

    def finish(ins, dst, sems):
        for cp in copies(dst, sems, "theirs"):
            cp.wait_recv()
        for cp in copies(dst, sems, "mine"):
            cp.wait_send()

    return _Exchange(bufs, [jax.ShapeDtypeStruct(b.shape, b.dtype) for b in bufs], {a: a for a in range(n)},
                     [pltpu.SemaphoreType.DMA((n,))] * 2, start, finish)


def _adam_math(w, g, m, v):
    m = ADAM_B1 * m + (1.0 - ADAM_B1) * g
    v = ADAM_B2 * v + (1.0 - ADAM_B2) * (g * g)
    m_hat = m / (1.0 - ADAM_B1 ** ADAM_STEP)
    v_hat = v / (1.0 - ADAM_B2 ** ADAM_STEP)
    delta = -ADAM_LR * (m_hat / (jnp.sqrt(v_hat) + ADAM_EPS) + ADAM_WD * w)
    return delta, m, v


def _gather_small(small):
    def peers():
        x, y, c, _ = _mesh_place()
        return [(x ^ ((r >> 2) & 1), y ^ ((r >> 1) & 1), c ^ (r & 1)) for r in range(1, 8)], 4 * x + 2 * y + c

    def start(src, dst, sems):
        to, me = peers()
        for r, peer in enumerate(to):
            _remote(src[0], dst[0].at[me], sems[0].at[r], sems[1].at[r], peer).start()

    def finish(src, dst, sems):
        to, me = peers()
        for r, (px, py, pc) in enumerate(to):
            theirs = dst[0].at[4 * px + 2 * py + pc]
            _remote(theirs, theirs, sems[0].at[r], sems[1].at[r], (px, py, pc)).wait_recv()
        for r, peer in enumerate(to):
            _remote(src[0], dst[0].at[me], sems[0].at[r], sems[1].at[r], peer).wait_send()

    return _Exchange([small], [jax.ShapeDtypeStruct((8,) + small.shape, small.dtype)], {},
                     [pltpu.SemaphoreType.DMA((7,))] * 2, start, finish)


def _small_adamw(gathered, small, me, w_vec, m_vec, v_vec):
    n_par = w_vec.shape[1]

    def body(me_ref, a_ref, s_ref, w_ref, m_ref, v_ref, loss_ref, g_ref, d_ref, nm_ref, nv_ref):
        mine = s_ref[...]
        tot = jnp.where(me_ref[0] == 0, mine, a_ref[0])
        for d in range(1, 8):
            tot = tot + jnp.where(me_ref[0] == d, mine, a_ref[d])
        tot = jnp.sum(tot, axis=0, keepdims=True)
        sq = jnp.sum(tot[:, n_par:], axis=1, keepdims=True)
        loss_ref[...] = jnp.broadcast_to(sq * (0.5 / D_MODEL), loss_ref.shape)
        g = tot[:, :n_par]
        g_ref[...] = g
        d_ref[...], nm_ref[...], nv_ref[...] = _adam_math(w_ref[...], g, m_ref[...], v_ref[...])

    vm = pl.BlockSpec(memory_space=pltpu.VMEM)
    vec = jax.ShapeDtypeStruct((1, n_par), F32)
    return pl.pallas_call(
        body, name="small_adamw",
        grid_spec=pltpu.PrefetchScalarGridSpec(num_scalar_prefetch=1, grid=(), in_specs=[vm] * 5, out_specs=[vm] * 5),
        out_shape=[jax.ShapeDtypeStruct((1, 128), F32), vec, vec, vec, vec],
    )(me, gathered, small, w_vec, m_vec, v_vec)


def _adamw(w, g, m, v, name):
    r, c = w.shape
    tr = min(r, 256)

    def body(w_ref, g_ref, m_ref, v_ref, go_ref, d_ref, nm_ref, nv_ref):
        g = g_ref[...]
        go_ref[...] = g
        d_ref[...], nm_ref[...], nv_ref[...] = _adam_math(w_ref[...], g, m_ref[...], v_ref[...])

    spec = pl.BlockSpec((tr, c), lambda i: (i, 0))
    shape = jax.ShapeDtypeStruct((r, c), F32)
    return _pallas(
        body, name=name, grid=(r // tr,),
        in_specs=[spec] * 4, out_specs=[spec] * 4, out_shape=[shape] * 4,
        compiler_params=_params(("parallel",), 48),
    )(w, g, m, v)


def kernel(x, w_in, w_pool, pool_scale, w_out, ln_gain, ln_bias, loss_target, m_w_in, m_w_pool, m_pool_scale, m_w_out, m_ln_gain, m_ln_bias, v_w_in, v_w_pool, v_pool_scale, v_w_out, v_ln_gain, v_ln_bias):
    xi, yi, ci = lax.axis_index("x"), lax.axis_index("y"), lax.axis_index("c")
    chip = (2 * xi + yi).astype(jnp.int32).reshape(1)
    core = ci.astype(jnp.int32).reshape(1)
    n_groups = len(POOL_WINDOWS)
    shard_c = w_pool.shape[2]

    w_in_b = _cast_bf16(w_in[0], chip, "cast_w_in", 256)
    w_out_b = _cast_bf16(w_out[0], chip, "cast_w_out", 256)
    w_pool_b = _cast_bf16(w_pool[0].reshape(n_groups * shard_c, POOL_GROUP_DIM), chip, "cast_w_pool", 256)

    chip_core = jnp.concatenate([chip, core])
    onward = (2 * (xi ^ ci) + (yi ^ (1 - ci))).astype(jnp.int32).reshape(1)
    g_x, full_in, full_out, full_pool, small, small_all = _step(
        x[0], loss_target[0], [w_in_b, w_out_b, w_pool_b], pool_scale, ln_gain, ln_bias,
        (core, chip_core, onward, _in_proj_plan(xi, yi)))
    half_c = shard_c // 2
    grad_w_in = full_in.reshape(D_MODEL, SHARD_IN)
    grad_w_out = full_out.reshape(D_MODEL // N_SHARDS, D_MODEL)
    grad_w_pool = (full_pool.reshape(2, n_groups, half_c, POOL_GROUP_DIM).transpose(1, 0, 2, 3)
                   .reshape(n_groups * shard_c, POOL_GROUP_DIM))

    grad_w_in, d_in, nm_in, nv_in = _adamw(w_in[0], grad_w_in, m_w_in[0], v_w_in[0], "adamw_w_in")
    grad_w_out, d_out, nm_out, nv_out = _adamw(w_out[0], grad_w_out, m_w_out[0], v_w_out[0], "adamw_w_out")
    flat = lambda t: t[0].reshape(n_groups * shard_c, POOL_GROUP_DIM)
    grad_w_pool, d_pool, nm_pool, nv_pool = _adamw(flat(w_pool), grad_w_pool, flat(m_w_pool), flat(v_w_pool),
                                                   "adamw_w_pool")

    cat = lambda a, b, c: jnp.concatenate([a, b, c], axis=1)
    me = (4 * xi + 2 * yi + ci).astype(jnp.int32).reshape(1)
    loss_v, g_vec, d_vec, nm_vec, nv_vec = _small_adamw(
        small_all, small, me, cat(pool_scale, ln_gain, ln_bias), cat(m_pool_scale, m_ln_gain, m_ln_bias),
        cat(v_pool_scale, v_ln_gain, v_ln_bias))

    def split(vec):
        return vec[:, :D_POOL], vec[:, D_POOL:D_POOL + D_MODEL], vec[:, D_POOL + D_MODEL:]

    g_scale, g_gain, g_bias = split(g_vec)
    d_scale, d_gain, d_bias = split(d_vec)
    nm_scale, nm_gain, nm_bias = split(nm_vec)
    nv_scale, nv_gain, nv_bias = split(nv_vec)
    pool_shape = w_pool.shape
    return (loss_v[0, 0], g_x[None],
            grad_w_in[None], grad_w_pool.reshape(pool_shape), g_scale, grad_w_out[None], g_gain, g_bias,
            d_in[None], d_pool.reshape(pool_shape), d_scale, d_out[None], d_gain, d_bias,
            nm_in[None], nm_pool.reshape(pool_shape), nm_scale, nm_out[None], nm_gain, nm_bias,
            nv_in[None], nv_pool.reshape(pool_shape), nv_scale, nv_out[None], nv_gain, nv_bias)
```

```python
import functools

import jax
import jax.numpy as jnp
from jax import lax
from jax.experimental import pallas as pl
from jax.experimental.pallas import tpu as pltpu

F32 = jnp.float32
BF16 = jnp.bfloat16
MESH = pl.DeviceIdType.MESH
ANY = pl.BlockSpec(memory_space=pl.ANY)

D_MODEL = 2048
D_ATTN = 1024
D_POOL = 1024
HEAD_DIM = 128
N_HEADS = 8
ROPE_DIM = 32
ROPE_THETA = 500000.0
DILATIONS = (1, 4, 16)
KEY_BLOCK = 128
CHUNK = 2 * KEY_BLOCK
STAT_LANES = 128
POOL_WINDOWS = (2, 4, 8, 16)
POOL_GROUP_DIM = 256
POOL_HALO = 16
D_QKV = 3 * D_ATTN
D_UG = D_POOL + D_MODEL
D_IN = D_QKV + D_UG
N_SHARDS = 4
SHARD_IN = D_IN // N_SHARDS
LN_EPS = 1e-5
DEEPNORM_ALPHA = 2.0 ** 0.25
ADAM_LR = 0.001
ADAM_B1 = 0.9
ADAM_B2 = 0.999
ADAM_EPS = 1e-08
ADAM_WD = 0.01
ADAM_STEP = 10
NEG = -1e30
MIB = 1024 * 1024


def _params(sem, vmem_mib):
    return pltpu.CompilerParams(dimension_semantics=sem, vmem_limit_bytes=vmem_mib * MIB)


def _pallas(body, **kwargs):
    pin = lambda s: pltpu.HBM(s.shape, s.dtype) if len(s.shape) >= 2 else s
    out_shape = kwargs.pop("out_shape")
    out_shape = [pin(s) for s in out_shape] if isinstance(out_shape, (list, tuple)) else pin(out_shape)
    call = pl.pallas_call(body, out_shape=out_shape, **kwargs)

    def run(*operands):
        return call(*[pltpu.with_memory_space_constraint(o, pltpu.HBM) if o.ndim >= 2 else o for o in operands])

    return run


class _Exchange:
    def __init__(self, operands, out_shape, aliases, sems, start, finish):
        self.operands, self.out_shape, self.aliases, self.sems = list(operands), list(out_shape), dict(aliases), list(sems)
        self.start, self.finish = start, finish


def _run_exchange(comm, name):
    n_in, n_out = len(comm.operands), len(comm.out_shape)

    def body(*refs):
        ins, outs, sems = refs[:n_in], refs[n_in:n_in + n_out], refs[n_in + n_out:]
        comm.start(ins, outs, sems)
        comm.finish(ins, outs, sems)

    return _pallas(
        body, name=name, in_specs=[ANY] * n_in, out_specs=[ANY] * n_out, out_shape=comm.out_shape,
        input_output_aliases=comm.aliases, scratch_shapes=comm.sems,
    )(*comm.operands)


def _call(body, *, name, grid, in_specs, out_specs, out_shape, scratch_shapes, semantics, vmem_mib, args,
          aliases=None, comm=None, prefetch=()):
    aliases = dict(aliases or {})
    n_pre, n_in, n_out, n_scr = len(prefetch), len(in_specs), len(out_specs), len(scratch_shapes)
    c_in, c_out = (len(comm.operands), len(comm.out_shape)) if comm else (0, 0)
    c_shapes, c_sems, c_operands = (comm.out_shape, comm.sems, comm.operands) if comm else ([], [], [])

    def hosted(*refs):
        pre, refs = refs[:n_pre], refs[n_pre:]
        a = n_in
        b = a + c_in
        c = b + n_out
        d = c + c_out
        e = d + n_scr
        if comm is None:
            body(*pre, *refs)
            return
        ids = [pl.program_id(k) for k in range(len(grid))]
        first = functools.reduce(jnp.logical_and, [i == 0 for i in ids])
        last = functools.reduce(jnp.logical_and, [i == g - 1 for i, g in zip(ids, grid)])

        @pl.when(first)
        def _():
            comm.start(refs[a:b], refs[c:d], refs[e:])

        body(*pre, *refs[:a], *refs[b:c], *refs[d:e])

        @pl.when(last)
        def _():
            comm.finish(refs[a:b], refs[c:d], refs[e:])

    if comm:
        semantics = ("arbitrary",) * len(grid)
        for i, o in comm.aliases.items():
            aliases[n_pre + n_in + i] = n_out + o
    outs = _pallas(
        hosted, name=name,
        grid_spec=pltpu.PrefetchScalarGridSpec(
            num_scalar_prefetch=n_pre, grid=grid, in_specs=list(in_specs) + [ANY] * c_in,
            out_specs=list(out_specs) + [ANY] * c_out, scratch_shapes=list(scratch_shapes) + c_sems),
        out_shape=list(out_shape) + c_shapes, input_output_aliases=aliases,
        compiler_params=_params(semantics, vmem_mib),
    )(*prefetch, *args, *c_operands)
    return list(outs[:n_out]), list(outs[n_out:])


def _dot_nn(a, b):
    return jnp.dot(a, b, preferred_element_type=F32)


def _dot_nt(a, b):
    return lax.dot_general(a, b, (((1,), (1,)), ((), ())), preferred_element_type=F32)


def _dot_tn(a, b):
    return lax.dot_general(a, b, (((0,), (0,)), ((), ())), preferred_element_type=F32)


def _fold_rows(a):
    r, c = a.shape
    return jnp.sum(a.reshape(r // 8, 8, c), axis=0)


def _cast_bf16(a, chip, name, rows):
    r, c = a.shape

    def body(chip_ref, a_ref, o_ref):
        o_ref[...] = a_ref[...].astype(BF16)

    return _pallas(
        body, name=name,
        grid_spec=pltpu.PrefetchScalarGridSpec(
            num_scalar_prefetch=1, grid=(r // rows,),
            in_specs=[pl.BlockSpec((rows, c), lambda i, chip_ref: (i, 0))],
            out_specs=pl.BlockSpec((None, rows, c), lambda i, chip_ref: (chip_ref[0], i, 0))),
        out_shape=jax.ShapeDtypeStruct((N_SHARDS, r, c), BF16),
        compiler_params=_params(("parallel",), 32),
    )(chip, a)


def _mesh_place():
    x, y, c = lax.axis_index("x"), lax.axis_index("y"), lax.axis_index("c")
    return x, y, c, [(1 - x, y), (x, 1 - y), (1 - x, 1 - y)]


def _remote(src, dst, send_sem, recv_sem, to):
    return pltpu.make_async_remote_copy(src_ref=src, dst_ref=dst, send_sem=send_sem, recv_sem=recv_sem,
                                        device_id=to, device_id_type=MESH)


def _rope_tables(seq):
    half = ROPE_DIM // 2
    inv_freq = ROPE_THETA ** (-(2.0 * jnp.arange(half, dtype=F32)) / ROPE_DIM)
    ang = jnp.arange(seq, dtype=jnp.int32).astype(F32)[:, None] * inv_freq[None, :]
    cos, sin = jnp.cos(ang), jnp.sin(ang)
    pad = jnp.zeros((seq, HEAD_DIM - ROPE_DIM), F32)
    zeros = jnp.zeros((seq, half), F32)
    c_tab = jnp.concatenate([cos, cos, pad + 1.0], axis=1)
    up_tab = jnp.concatenate([-sin, zeros, pad], axis=1)
    down_tab = jnp.concatenate([zeros, sin, pad], axis=1)
    return c_tab, up_tab, down_tab


def _rotate_heads(t, c_tab, up_tab, down_tab):
    outs = []
    for h in range(t.shape[1] // HEAD_DIM):
        th = t[:, h * HEAD_DIM:(h + 1) * HEAD_DIM]
        up = pltpu.roll(th, HEAD_DIM - ROPE_DIM // 2, axis=1)
        down = pltpu.roll(th, ROPE_DIM // 2, axis=1)
        outs.append(th * c_tab + up * up_tab + down * down_tab)
    return outs[0] if len(outs) == 1 else jnp.concatenate(outs, axis=1)


def _to_pattern(slabs_ref, dst_ref, dil, dtype):
    n_slabs, rows, _ = slabs_ref.shape
    for s in range(n_slabs):
        for r in range(dil):
            dst_ref[r, :, s * 128:(s + 1) * 128] = slabs_ref[s, pl.ds(r, rows // dil, dil), :].astype(dtype)


def _from_pattern(src_ref, slabs_ref, dil):
    n_slabs, rows, _ = slabs_ref.shape
    for s in range(n_slabs):
        for r in range(dil):
            slabs_ref[s, pl.ds(r, rows // dil, dil), :] = src_ref[r, :, s * 128:(s + 1) * 128].astype(F32)


def _store_slabs(slabs_ref, value):
    for s in range(slabs_ref.shape[0]):
        slabs_ref[s] = value[:, s * 128:(s + 1) * 128]


W_IN_CHUNKS = 4


def _in_proj_plan(x, y):
    shards = [2 * x + y, 2 * (1 - x) + y, 2 * x + (1 - y), 2 * (1 - x) + (1 - y)]
    last_row = jnp.int32(-2)

    def table(active, col_of):
        cols, rows = [], []
        first_col = functools.reduce(lambda acc, j: jnp.where(active[j], col_of(shards[j]), acc), reversed(range(4)),
                                     jnp.int32(0))
        held_col, seen = first_col, jnp.bool_(False)
        for j in range(4):
            cols.append(jnp.where(active[j], col_of(shards[j]), held_col))
            rows.append(jnp.where(active[j], -1, jnp.where(seen, last_row, 0)))
            held_col = jnp.where(active[j], col_of(shards[j]), held_col)
            seen = jnp.logical_or(seen, active[j])
        return cols, rows

    q_cols, q_rows = table([s < 2 for s in shards], lambda s: s)
    h_cols, h_rows = table([s >= 2 for s in shards], lambda s: s - 2)
    return jnp.stack([jnp.asarray(v, jnp.int32) for v in shards + q_cols + q_rows + h_cols + h_rows])


def _in_proj_gathering(x, w_bufs, tabs, plan):
    seq = x.shape[0]
    tm, tn = 512, SHARD_IN
    n_tiles = seq // tm
    heads = tn // HEAD_DIM
    k_heads_in_second = 2 * D_ATTN // HEAD_DIM - heads
    d4, d16 = DILATIONS[1], DILATIONS[2]
    DIAGONAL = 2
    chunk = D_MODEL // 2 // W_IN_CHUNKS
    early = [(0, D_MODEL // 2, q * chunk, chunk) for q in range(W_IN_CHUNKS)]
    late = [(a, w_bufs[a].shape[1] // 2, 0, w_bufs[a].shape[1] // 2) for a in (1, 2)]
    pieces = early + late
    early_ids, late_ids = range(len(early)), range(len(early), len(pieces))

    def body(plan_ref, x_ref, w_in_in, w_out_in, w_pool_in, c_ref, up_ref, down_ref,
             o1_ref, o4_ref, o16_ref, hug_ref, w_ref, w_out_ref, w_pool_ref,
             wbuf_ref, res_ref, w_sem, ici_send, ici_recv, d2d_send, d2d_recv):
        j, i = pl.program_id(0), pl.program_id(1)
        mx, my, mc, chips = _mesh_place()
        sibling = (mx, my, 1 - mc)
        gathered = (w_ref, w_out_ref, w_pool_ref)
        chip_of = lambda k: 2 * chips[k][0] + chips[k][1]

        def piece(n, chip, core):
            a, per_core, offset, size = pieces[n]
            return gathered[a].at[chip, pl.ds(core * per_core + offset, size)]

        def to_neighbour(k, n):
            mine = piece(n, 2 * mx + my, mc)
            return _remote(mine, mine, ici_send.at[n, k], ici_recv.at[n, k], (*chips[k], mc))

        def relay(n):
            theirs = piece(n, 2 * (mx ^ (1 - mc)) + (my ^ mc), mc)
            return _remote(theirs, theirs, ici_send.at[n, DIAGONAL], ici_recv.at[n, DIAGONAL], (mx ^ mc, my ^ (1 - mc), mc))

        def arrival(k, n):
            theirs = piece(n, chip_of(k), mc)
            return _remote(theirs, theirs, ici_send.at[n, k], ici_recv.at[n, k], (*chips[k], mc))

        def to_sibling(k, n, core):
            theirs = piece(n, chip_of(k), core)
            return _remote(theirs, theirs, d2d_send.at[n, k], d2d_recv.at[n, k], sibling)

        def take(k, ids):
            for n in ids:
                arrival(k, n).wait_recv()
                to_sibling(k, n, mc).start()

        def taken(k, ids):
            for n in ids:
                to_sibling(k, n, 1 - mc).wait_recv()

        first_tile = i == 0

        @pl.when(jnp.logical_and(j == 0, first_tile))
        def _():
            for n in range(len(pieces)):
                for k in range(DIAGONAL):
                    to_neighbour(k, n).start()

        @pl.when(jnp.logical_and(j == 1, first_tile))
        def _():
            take(0, early_ids)
            taken(0, early_ids)

        @pl.when(jnp.logical_and(j == 2, first_tile))
        def _():
            take(1, early_ids)
            for n in early_ids:
                relay(n).start()
            taken(1, early_ids)
            for k in range(DIAGONAL):
                take(k, late_ids)
            for n in late_ids:
                relay(n).start()
            for k in range(DIAGONAL):
                taken(k, late_ids)

        @pl.when(jnp.logical_and(j == 3, first_tile))
        def _():
            take(DIAGONAL, range(len(pieces)))
            taken(DIAGONAL, range(len(pieces)))

        shard = plan_ref[j]

        @pl.when(first_tile)
        def _():
            cp = pltpu.make_async_copy(w_ref.at[shard], wbuf_ref, w_sem)
            cp.start()
            cp.wait()

        xb = x_ref[...].astype(BF16)
        group = 4 * HEAD_DIM
        accs = [_dot_nn(xb, wbuf_ref[:, g * group:(g + 1) * group]) for g in range(tn // group)]

        def emit_qkv(rotated_heads):
            for h in range(heads):
                lanes = (h * HEAD_DIM) % group
                th = accs[h * HEAD_DIM // group][:, lanes:lanes + HEAD_DIM]
                if h < rotated_heads:
                    th = _rotate_heads(th, c_ref[...], up_ref[...], down_ref[...])
                res_ref[h] = th
                o1_ref[:, h * HEAD_DIM:(h + 1) * HEAD_DIM] = th.astype(BF16)
            _to_pattern(res_ref, o4_ref, d4, BF16)
            _to_pattern(res_ref, o16_ref, d16, BF16)

        @pl.when(shard == 0)
        def _():
            emit_qkv(heads)

        @pl.when(shard == 1)
        def _():
            emit_qkv(k_heads_in_second)

        @pl.when(shard >= 2)
        def _():
            for g, acc in enumerate(accs):
                hug_ref[:, g * group:(g + 1) * group] = acc.astype(BF16)

        @pl.when(jnp.logical_and(j == 3, i == n_tiles - 1))
        def _():
            for n in range(len(pieces)):
                for k in range(DIAGONAL):
                    to_neighbour(k, n).wait_send()
                relay(n).wait_send()
                for k in range(DIAGONAL + 1):
                    to_sibling(k, n, mc).wait_send()

    def held(base, last):
        return lambda j, i, plan_ref: jnp.where(plan_ref[base + j] == -1, i,
                                                jnp.where(plan_ref[base + j] == -2, last, 0))

    q_row, h_row = held(8, n_tiles - 1), held(16, n_tiles - 1)
    tab_spec = pl.BlockSpec((tm, HEAD_DIM), lambda j, i, plan_ref: (i, 0))
    sems = [pltpu.SemaphoreType.DMA((len(pieces), 3))] * 4
    o1, o4, o16, hug, w_in_g, w_out_g, w_pool_g = _pallas(
        body, name="in_proj_gathering",
        grid_spec=pltpu.PrefetchScalarGridSpec(
            num_scalar_prefetch=1, grid=(N_SHARDS, n_tiles),
            in_specs=[pl.BlockSpec((tm, D_MODEL), lambda j, i, plan_ref: (i, 0)), ANY, ANY, ANY,
                      tab_spec, tab_spec, tab_spec],
            out_specs=[pl.BlockSpec((tm, tn), lambda j, i, p: (q_row(j, i, p), p[4 + j])),
                       pl.BlockSpec((d4, tm // d4, tn), lambda j, i, p: (0, q_row(j, i, p), p[4 + j])),
                       pl.BlockSpec((d16, tm // d16, tn), lambda j, i, p: (0, q_row(j, i, p), p[4 + j])),
                       pl.BlockSpec((tm, tn), lambda j, i, p: (h_row(j, i, p), p[12 + j])),
                       ANY, ANY, ANY],
            scratch_shapes=[pltpu.VMEM((D_MODEL, tn), BF16), pltpu.VMEM((heads, tm, HEAD_DIM), F32),
                            pltpu.SemaphoreType.DMA(())] + sems),
        out_shape=[jax.ShapeDtypeStruct((seq, D_QKV), BF16),
                   jax.ShapeDtypeStruct((d4, seq // d4, D_QKV), BF16),
                   jax.ShapeDtypeStruct((d16, seq // d16, D_QKV), BF16),
                   jax.ShapeDtypeStruct((seq, D_UG), BF16)]
        + [jax.ShapeDtypeStruct(b.shape, b.dtype) for b in w_bufs],
        input_output_aliases={2: 4, 3: 5, 4: 6},
        compiler_params=_params(("arbitrary", "arbitrary"), 52),
    )(plan, x, *w_bufs, *tabs)
    return [o1[None], o4, o16], hug, w_in_g, w_out_g, w_pool_g


def _band_masks():
    row = lax.broadcasted_iota(jnp.int32, (KEY_BLOCK, KEY_BLOCK), 0)
    col = lax.broadcasted_iota(jnp.int32, (KEY_BLOCK, KEY_BLOCK), 1)
    return col <= row, col >= row


def _attn_fwd(qkv, name):
    dil, n, _ = qkv.shape
    scale = HEAD_DIM ** -0.5
    lo, hi = slice(0, KEY_BLOCK), slice(KEY_BLOCK, CHUNK)

    def body(q_ref, k_ref, v_ref, kb_ref, vb_ref, o_ref, st_ref):
        i = pl.program_id(1)
        cur_mask, prev_mask = _band_masks()
        before_mask = jnp.logical_and(prev_mask, i > 0)
        lane = lax.broadcasted_iota(jnp.int32, (KEY_BLOCK, STAT_LANES), 1)
        tasks = [(rows, h) for rows in (lo, hi) for h in range(N_HEADS)]
        head = lambda h: slice(h * HEAD_DIM, (h + 1) * HEAD_DIM)

        def prev_of(rows, h):
            if rows is lo:
                return kb_ref[:, head(h)], vb_ref[:, head(h)], before_mask
            return k_ref[lo, head(h)], v_ref[lo, head(h)], prev_mask

        scores = []
        for rows, h in tasks:
            q = q_ref[rows, head(h)]
            scores.append((_dot_nt(q, prev_of(rows, h)[0]), _dot_nt(q, k_ref[rows, head(h)])))
        probs = []
        for (rows, h), (qk_prev, qk_cur) in zip(tasks, scores):
            s_prev = jnp.where(prev_of(rows, h)[2], qk_prev * scale, NEG)
            s_cur = jnp.where(cur_mask, qk_cur * scale, NEG)
            m = jnp.max(jnp.maximum(s_prev, s_cur), axis=-1, keepdims=True)
            p_prev = jnp.exp(s_prev - m)
            p_cur = jnp.exp(s_cur - m)
            den = jnp.sum(p_prev + p_cur, axis=-1, keepdims=True)
            probs.append((p_prev.astype(BF16), p_cur.astype(BF16), den, m + jnp.log(den)))
        stats = [jnp.zeros((KEY_BLOCK, STAT_LANES), F32), jnp.zeros((KEY_BLOCK, STAT_LANES), F32)]
        for (rows, h), (p_prev, p_cur, den, lse) in zip(tasks, probs):
            o = _dot_nn(p_cur, v_ref[rows, head(h)]) + _dot_nn(p_prev, prev_of(rows, h)[1])
            o_ref[rows, head(h)] = (o / den).astype(BF16)
            b = 0 if rows is lo else 1
            stats[b] = jnp.where(lane == h, lse, stats[b])
        st_ref[lo, :] = stats[0]
        st_ref[hi, :] = stats[1]

    main = lambda cb: pl.BlockSpec((None, CHUNK, D_ATTN), lambda r, i: (r, i, cb))
    before = lambda cb: pl.BlockSpec((None, KEY_BLOCK, D_ATTN), lambda r, i: (r, jnp.maximum(2 * i - 1, 0), cb))
    return _pallas(
        body, name=name, grid=(dil, n // CHUNK),
        in_specs=[main(0), main(1), main(2), before(1), before(2)],
        out_specs=[main(0), pl.BlockSpec((None, CHUNK, STAT_LANES), lambda r, i: (r, i, 0))],
        out_shape=[jax.ShapeDtypeStruct((dil, n, D_ATTN), BF16), jax.ShapeDtypeStruct((dil, n, STAT_LANES), F32)],
        compiler_params=_params(("parallel", "parallel"), 40),
    )(qkv, qkv, qkv, qkv, qkv)


def _attn_bwd(qkv, do, stats, name, comm=None):
    dil, n, _ = qkv.shape
    n_blocks = n // KEY_BLOCK
    last = n // CHUNK - 1
    scale = HEAD_DIM ** -0.5
    lo, hi = slice(0, KEY_BLOCK), slice(KEY_BLOCK, CHUNK)

    def body(q_ref, k_ref, v_ref, kb_ref, vb_ref, qa_ref, do_ref, doa_ref, st_ref, sta_ref, dq_ref, dk_ref, dv_ref):
        i = pl.program_id(1)
        cur_mask, prev_mask = _band_masks()
        before_mask = jnp.logical_and(prev_mask, i > 0)
        after_mask = jnp.logical_and(prev_mask, i < last)

        rows_cat = lambda a, b: jnp.concatenate([a, b], axis=0)
        masks = (jnp.concatenate([before_mask, cur_mask], axis=1), jnp.concatenate([prev_mask, cur_mask], axis=1),
                 after_mask)

        def operands(h):
            cols = slice(h * HEAD_DIM, (h + 1) * HEAD_DIM)
            lse_c, del_c = slice(h, h + 1), slice(N_HEADS + h, N_HEADS + h + 1)
            q = (q_ref[lo, cols], q_ref[hi, cols], qa_ref[:, cols])
            do = (do_ref[lo, cols], do_ref[hi, cols], doa_ref[:, cols])
            keys = (rows_cat(kb_ref[:, cols], k_ref[lo, cols]), k_ref[:, cols], k_ref[hi, cols])
            vals = (rows_cat(vb_ref[:, cols], v_ref[lo, cols]), v_ref[:, cols], v_ref[hi, cols])
            st = ((st_ref[lo, lse_c], st_ref[lo, del_c]), (st_ref[hi, lse_c], st_ref[hi, del_c]),
                  (sta_ref[:, lse_c], sta_ref[:, del_c]))
            return cols, q, do, keys, vals, st

        group = N_HEADS // 2
        for first_head in range(0, N_HEADS, group):
            heads = range(first_head, first_head + group)
            raw = {}
            for h in heads:
                _, q, do, keys, vals, _ = operands(h)
                raw[h] = [(_dot_nt(q[j], keys[j]), _dot_nt(do[j], vals[j])) for j in range(3)]
            grads = {}
            for h in heads:
                st = operands(h)[5]
                grads[h] = []
                for j in range(3):
                    qk, dp = raw[h][j]
                    lse, delta = st[j]
                    p = jnp.exp(jnp.where(masks[j], qk * scale, NEG) - lse)
                    grads[h].append((p.astype(BF16), (p * (dp - delta) * scale).astype(BF16)))
            for h in heads:
                cols, q, do, keys, _, _ = operands(h)
                (p0, ds0), (p1, ds1), (pa, dsa) = grads[h]
                own, nxt = slice(KEY_BLOCK, CHUNK), slice(0, KEY_BLOCK)

                def put(ref, rows, val, cols=cols):
                    ref[rows, cols] = val.astype(ref.dtype)

                put(dq_ref, lo, _dot_nn(ds0, keys[0]))
                put(dq_ref, hi, _dot_nn(ds1, keys[1]))
                put(dk_ref, lo, _dot_tn(rows_cat(ds0[:, own], ds1[:, nxt]), q_ref[:, cols]))
                put(dk_ref, hi, _dot_tn(rows_cat(ds1[:, own], dsa), rows_cat(q[1], q[2])))
                put(dv_ref, lo, _dot_tn(rows_cat(p0[:, own], p1[:, nxt]), do_ref[:, cols]))
                put(dv_ref, hi, _dot_tn(rows_cat(p1[:, own], pa), rows_cat(do[1], do[2])))

    def spec(rows, width, row_of, cb):
        return pl.BlockSpec((None, rows, width), lambda r, i: (r, row_of(i), cb))

    same = lambda i: i
    before = lambda i: jnp.maximum(2 * i - 1, 0)
    after = lambda i: jnp.minimum(2 * i + 2, n_blocks - 1)
    out = spec(CHUNK, D_ATTN, same, 0)
    return _call(
        body, name=name, grid=(dil, n // CHUNK),
        in_specs=[spec(CHUNK, D_ATTN, same, 0), spec(CHUNK, D_ATTN, same, 1), spec(CHUNK, D_ATTN, same, 2),
                  spec(KEY_BLOCK, D_ATTN, before, 1), spec(KEY_BLOCK, D_ATTN, before, 2),
                  spec(KEY_BLOCK, D_ATTN, after, 0),
                  spec(CHUNK, D_ATTN, same, 0), spec(KEY_BLOCK, D_ATTN, after, 0),
                  spec(CHUNK, STAT_LANES, same, 0), spec(KEY_BLOCK, STAT_LANES, after, 0)],
        out_specs=[out, out, out],
        out_shape=[jax.ShapeDtypeStruct((dil, n, D_ATTN), BF16)] * 3,
        scratch_shapes=[], semantics=("parallel", "parallel"), vmem_mib=40,
        args=(qkv, qkv, qkv, qkv, qkv, qkv, do, do, stats, stats), comm=comm)


def _window_sums(ext, window, backward):
    rows = ext.shape[0]
    acc, span = ext, 1
    while span < window:
        acc = acc + pltpu.roll(acc, (rows - span) if backward else span, axis=0)
        span *= 2
    return acc


def _mix_gate(o_list, st_list, hug, w_pool_g, pool_scale):
    seq = hug.shape[0]
    tm = 256
    halo_blocks = tm // POOL_HALO
    d4, d16 = DILATIONS[1], DILATIONS[2]

    def body(o1_ref, o4_ref, o16_ref, l1_ref, l4_ref, l16_ref, u_ref, halo_ref, ga_ref, gp_ref, wp_ref, sc_ref,
             y_ref, mix_ref, lse_ref, pooled_ref, n4_ref, n16_ref, nl4_ref, nl16_ref):
        i = pl.program_id(0)
        _from_pattern(o4_ref, n4_ref, d4)
        _from_pattern(o16_ref, n16_ref, d16)
        _from_pattern(l4_ref, nl4_ref, d4)
        _from_pattern(l16_ref, nl16_ref, d16)
        la, lb, lc = l1_ref[...], nl4_ref[0], nl16_ref[0]
        mx = jnp.maximum(jnp.maximum(la, lb), lc)
        ea, eb, ec = jnp.exp(la - mx), jnp.exp(lb - mx), jnp.exp(lc - mx)
        tot = ea + eb + ec
        lse_ref[...] = mx + jnp.log(tot)
        wa, wb, wc = ea / tot, eb / tot, ec / tot
        ga = ga_ref[...].astype(F32)
        silu_a = ga * jax.nn.sigmoid(ga)
        for h in range(N_HEADS):
            cols = slice(h * HEAD_DIM, (h + 1) * HEAD_DIM)
            hc = slice(h, h + 1)
            attn = wa[:, hc] * o1_ref[:, cols].astype(F32) + wb[:, hc] * n4_ref[h] + wc[:, hc] * n16_ref[h]
            mix_ref[:, cols] = attn.astype(BF16)
            y_ref[:, cols] = (attn * silu_a[:, cols]).astype(BF16)

        u = u_ref[...].astype(F32)
        halo = jnp.where(i > 0, halo_ref[...].astype(F32), 0.0)
        ext = jnp.concatenate([halo, u], axis=0)
        pos = i * tm + lax.broadcasted_iota(jnp.int32, (tm, 1), 0)
        gp = gp_ref[...].astype(F32)
        gated_scale = sc_ref[...] * (gp * jax.nn.sigmoid(gp))
        for g, window in enumerate(POOL_WINDOWS):
            cols = slice(g * POOL_GROUP_DIM, (g + 1) * POOL_GROUP_DIM)
            sums = _window_sums(ext[:, cols], window, backward=False)[POOL_HALO:, :]
            count = jnp.minimum(pos + 1, window).astype(F32)
            pooled = (sums / count - u[:, cols]).astype(BF16)
            pooled_ref[:, cols] = pooled
            pre = _dot_nn(pooled, wp_ref[g])
            out_cols = slice(D_ATTN + g * POOL_GROUP_DIM, D_ATTN + (g + 1) * POOL_GROUP_DIM)
            mix_ref[:, out_cols] = pre.astype(BF16)
            y_ref[:, out_cols] = (pre * gated_scale[:, cols]).astype(BF16)

    row = lambda width, cb=0: pl.BlockSpec((tm, width), lambda i: (i, cb))
    pat = lambda d, width: pl.BlockSpec((d, tm // d, width), lambda i: (0, i, 0))
    return _pallas(
        body, name="mix_gate", grid=(seq // tm,),
        in_specs=[row(D_ATTN), pat(d4, D_ATTN), pat(d16, D_ATTN),
                  row(STAT_LANES), pat(d4, STAT_LANES), pat(d16, STAT_LANES),
                  row(D_POOL),
                  pl.BlockSpec((POOL_HALO, D_POOL), lambda i: (jnp.maximum(i * halo_blocks - 1, 0), 0)),
                  row(D_ATTN, 1), row(D_POOL, 2),
                  pl.BlockSpec((len(POOL_WINDOWS), POOL_GROUP_DIM, POOL_GROUP_DIM), lambda i: (0, 0, 0)),
                  pl.BlockSpec((1, D_POOL), lambda i: (0, 0))],
        out_specs=[row(D_MODEL), row(D_MODEL), row(STAT_LANES), row(D_POOL)],
        out_shape=[jax.ShapeDtypeStruct((seq, D_MODEL), BF16), jax.ShapeDtypeStruct((seq, D_MODEL), BF16),
                   jax.ShapeDtypeStruct((seq, STAT_LANES), F32), jax.ShapeDtypeStruct((seq, D_POOL), BF16)],
        scratch_shapes=[pltpu.VMEM((N_HEADS, tm, HEAD_DIM), F32), pltpu.VMEM((N_HEADS, tm, HEAD_DIM), F32),
                        pltpu.VMEM((1, tm, STAT_LANES), F32), pltpu.VMEM((1, tm, STAT_LANES), F32)],
        compiler_params=_params(("parallel",), 48),
    )(o_list[0][0], o_list[1], o_list[2], st_list[0][0], st_list[1], st_list[2],
      hug, hug, hug, hug, w_pool_g, pool_scale)


def _out_proj_loss(y, w_out_g, x, target, gain, bias):
    seq = x.shape[0]
    tm = 512

    def body(y_ref, w_ref, x_ref, t_ref, g_ref, b_ref, dz_ref, dzb_ref, gg_ref, gb_ref, loss_ref):
        @pl.when(pl.program_id(0) == 0)
        def _():
            gg_ref[...] = jnp.zeros_like(gg_ref)
            gb_ref[...] = jnp.zeros_like(gb_ref)
            loss_ref[...] = jnp.zeros_like(loss_ref)

        halves = [slice(0, tm // 2), slice(tm // 2, tm)]
        projected = [_dot_nn(y_ref[rows, :], w_ref[...]) for rows in halves]
        for rows, out in zip(halves, projected):
            z = DEEPNORM_ALPHA * x_ref[rows, :] + out
            mu = jnp.mean(z, axis=-1, keepdims=True)
            zc = z - mu
            rstd = lax.rsqrt(jnp.mean(zc * zc, axis=-1, keepdims=True) + LN_EPS)
            xhat = zc * rstd
            gain_v = g_ref[...]
            diff = xhat * gain_v + b_ref[...] - t_ref[rows, :]
            sq = _fold_rows(diff * diff)
            part = sq[:, :128]
            for k in range(1, D_MODEL // 128):
                part = part + sq[:, k * 128:(k + 1) * 128]
            loss_ref[...] += part
            dln = diff * (1.0 / D_MODEL)
            gg_ref[...] += _fold_rows(dln * xhat)
            gb_ref[...] += _fold_rows(dln)
            dxhat = dln * gain_v
            dz = rstd * (dxhat - jnp.mean(dxhat, axis=-1, keepdims=True)
                         - xhat * jnp.mean(dxhat * xhat, axis=-1, keepdims=True))
            dz_ref[rows, :] = dz
            dzb_ref[rows, :] = dz.astype(BF16)

    row = lambda: pl.BlockSpec((tm, D_MODEL), lambda i: (i, 0))
    vec = lambda: pl.BlockSpec((1, D_MODEL), lambda i: (0, 0))
    acc = lambda width: pl.BlockSpec((8, width), lambda i: (0, 0))
    return _pallas(
        body, name="out_proj_loss", grid=(seq // tm,),
        in_specs=[row(), pl.BlockSpec((D_MODEL, D_MODEL), lambda i: (0, 0), pipeline_mode=pl.Buffered(1)),
                  row(), row(), vec(), vec()],
        out_specs=[row(), row(), acc(D_MODEL), acc(D_MODEL), acc(128)],
        out_shape=[jax.ShapeDtypeStruct((seq, D_MODEL), F32), jax.ShapeDtypeStruct((seq, D_MODEL), BF16),
                   jax.ShapeDtypeStruct((8, D_MODEL), F32), jax.ShapeDtypeStruct((8, D_MODEL), F32),
                   jax.ShapeDtypeStruct((8, 128), F32)],
        compiler_params=_params(("arbitrary",), 56),
    )(y, w_out_g.reshape(D_MODEL, D_MODEL), x, target, gain, bias)


def _dy_gate_bwd(dzb, w_out_g, hug, mixpre, pool_scale, lse_all):
    seq = dzb.shape[0]
    tm = 256
    d4, d16 = DILATIONS[1], DILATIONS[2]

    def body(dz_ref, w_ref, ga_ref, gp_ref, mix_ref, sc_ref, lse_ref,
             dh_ref, dpo_ref, do1_ref, do4_ref, do16_ref, st1_ref, st4_ref, st16_ref, da_ref, st_ref):
        dy = _dot_nt(dz_ref[...], w_ref[...])
        ga = ga_ref[...].astype(F32)
        sig = jax.nn.sigmoid(ga)
        attn = mix_ref[:, :D_ATTN].astype(F32)
        dya = dy[:, :D_ATTN]
        dattn = dya * (ga * sig)
        dh_ref[:, :D_ATTN] = (dya * attn * (sig * (1.0 + ga * (1.0 - sig)))).astype(BF16)
        _store_slabs(da_ref, dattn)
        lane = lax.broadcasted_iota(jnp.int32, (tm, STAT_LANES), 1)
        stats = lse_ref[...]
        prod = dattn * attn
        for h in range(N_HEADS):
            delta = jnp.sum(prod[:, h * HEAD_DIM:(h + 1) * HEAD_DIM], axis=-1, keepdims=True)
            stats = jnp.where(lane == N_HEADS + h, delta, stats)
        st_ref[0] = stats
        do1_ref[...] = dattn.astype(BF16)
        st1_ref[...] = stats
        _to_pattern(da_ref, do4_ref, d4, BF16)
        _to_pattern(da_ref, do16_ref, d16, BF16)
        _to_pattern(st_ref, st4_ref, d4, F32)
        _to_pattern(st_ref, st16_ref, d16, F32)

        gp = gp_ref[...].astype(F32)
        sig = jax.nn.sigmoid(gp)
        dyp = dy[:, D_ATTN:]
        dpo_ref[...] = (dyp * (gp * sig)).astype(BF16)
        dh_ref[:, D_ATTN:] = (dyp * (mix_ref[:, D_ATTN:].astype(F32) * sc_ref[...])
                              * (sig * (1.0 + gp * (1.0 - sig)))).astype(BF16)

    row = lambda width, cb=0: pl.BlockSpec((tm, width), lambda i: (i, cb))
    pat = lambda d, width: pl.BlockSpec((d, tm // d, width), lambda i: (0, i, 0))
    pat_shape = lambda d, width, dtype: jax.ShapeDtypeStruct((d, seq // d, width), dtype)
    outs = _pallas(
        body, name="dy_gate_bwd", grid=(seq // tm,),
        in_specs=[row(D_MODEL), pl.BlockSpec((D_MODEL, D_MODEL), lambda i: (0, 0)),
                  row(D_ATTN, 1), row(D_POOL, 2), row(D_MODEL), pl.BlockSpec((1, D_POOL), lambda i: (0, 0)),
                  row(STAT_LANES)],
        out_specs=[row(D_MODEL, D_IN // D_MODEL - 1), row(D_POOL),
                   row(D_ATTN), pat(d4, D_ATTN), pat(d16, D_ATTN),
                   row(STAT_LANES), pat(d4, STAT_LANES), pat(d16, STAT_LANES)],
        out_shape=[jax.ShapeDtypeStruct((seq, D_IN), BF16), jax.ShapeDtypeStruct((seq, D_POOL), BF16),
                   jax.ShapeDtypeStruct((seq, D_ATTN), BF16), pat_shape(d4, D_ATTN, BF16), pat_shape(d16, D_ATTN, BF16),
                   jax.ShapeDtypeStruct((seq, STAT_LANES), F32), pat_shape(d4, STAT_LANES, F32),
                   pat_shape(d16, STAT_LANES, F32)],
        scratch_shapes=[pltpu.VMEM((N_HEADS, tm, HEAD_DIM), F32), pltpu.VMEM((1, tm, STAT_LANES), F32)],
        compiler_params=_params(("parallel",), 48),
    )(dzb, w_out_g.reshape(D_MODEL, D_MODEL), hug, hug, mixpre, pool_scale, lse_all)
    dh, dpo, do1, do4, do16, st1, st4, st16 = outs
    return dh, dpo, [do1[None], do4, do16], [st1[None], st4, st16]


def _pool_bwd(dh, dpo, mixpre, pooled, w_pool_g, pool_scale):
    seq = dpo.shape[0]
    tm = 256
    halo_blocks = tm // POOL_HALO
    last = seq // tm - 1
    n_groups = len(POOL_WINDOWS)

    def body(dh_in_ref, dpo_ref, halo_ref, pre_ref, pooled_ref, wp_ref, sc_ref, du_ref, gw_ref, gs_ref):
        i = pl.program_id(0)

        @pl.when(i == 0)
        def _():
            gw_ref[...] = jnp.zeros_like(gw_ref)
            gs_ref[...] = jnp.zeros_like(gs_ref)

        dpo = dpo_ref[...].astype(F32)
        scale = sc_ref[...]
        gs_ref[...] += _fold_rows(dpo * pre_ref[...].astype(F32))
        halo = jnp.where(i < last, halo_ref[...].astype(F32), 0.0)
        dpw = (jnp.concatenate([dpo, halo], axis=0) * scale).astype(BF16)
        pos = i * tm + lax.broadcasted_iota(jnp.int32, (tm + POOL_HALO, 1), 0)
        for g, window in enumerate(POOL_WINDOWS):
            cols = slice(g * POOL_GROUP_DIM, (g + 1) * POOL_GROUP_DIM)
            dpw_g = dpw[:, cols]
            gw_ref[g] += _dot_tn(pooled_ref[:, cols], dpw_g[:tm, :])
            dpooled = _dot_nt(dpw_g, wp_ref[g])
            count = jnp.minimum(pos + 1, window).astype(F32)
            sums = _window_sums(dpooled / count, window, backward=True)
            du_ref[:, cols] = (sums[:tm, :] - dpooled[:tm, :]).astype(BF16)

    row = lambda width, cb=0: pl.BlockSpec((tm, width), lambda i: (i, cb))
    return _pallas(
        body, name="pool_bwd", grid=(seq // tm,),
        in_specs=[ANY, row(D_POOL),
                  pl.BlockSpec((POOL_HALO, D_POOL),
                               lambda i: (jnp.minimum((i + 1) * halo_blocks, seq // POOL_HALO - 1), 0)),
                  row(D_POOL, 1), row(D_POOL),
                  pl.BlockSpec((n_groups, POOL_GROUP_DIM, POOL_GROUP_DIM), lambda i: (0, 0, 0)),
                  pl.BlockSpec((1, D_POOL), lambda i: (0, 0))],
        out_specs=[row(D_POOL, D_QKV // D_POOL),
                   pl.BlockSpec((n_groups, POOL_GROUP_DIM, POOL_GROUP_DIM), lambda i: (0, 0, 0)),
                   pl.BlockSpec((8, D_POOL), lambda i: (0, 0))],
        out_shape=[jax.ShapeDtypeStruct(dh.shape, dh.dtype),
                   jax.ShapeDtypeStruct((n_groups, POOL_GROUP_DIM, POOL_GROUP_DIM), F32),
                   jax.ShapeDtypeStruct((8, D_POOL), F32)],
        input_output_aliases={0: 0},
        compiler_params=_params(("arbitrary",), 40),
    )(dh, dpo, dpo, mixpre, pooled, w_pool_g, pool_scale)


def _sum_patterns(dh, parts, tabs, unrotate, col_block, name, comm=None):
    seq = dh.shape[0]
    tm, tn = 256, D_ATTN
    per = D_ATTN // tn
    d4, d16 = DILATIONS[1], DILATIONS[2]

    def body(dh_in_ref, a1_ref, a4_ref, a16_ref, ct_ref, up_ref, down_ref, o_ref, n4_ref, n16_ref):
        _from_pattern(a4_ref, n4_ref, d4)
        _from_pattern(a16_ref, n16_ref, d16)
        for s in range(tn // HEAD_DIM):
            cols = slice(s * HEAD_DIM, (s + 1) * HEAD_DIM)
            tot = a1_ref[:, cols].astype(F32) + n4_ref[s] + n16_ref[s]
            if unrotate:
                tot = _rotate_heads(tot, ct_ref[...], -up_ref[...], -down_ref[...])
            o_ref[:, cols] = tot.astype(BF16)

    tab = pl.BlockSpec((tm, HEAD_DIM), lambda i, j: (i, 0))
    pat = lambda d: pl.BlockSpec((d, tm // d, tn), lambda i, j: (0, i, j))
    (dh,), exchanged = _call(
        body, name=name, grid=(seq // tm, per),
        in_specs=[ANY, pl.BlockSpec((tm, tn), lambda i, j: (i, j)), pat(d4), pat(d16), tab, tab, tab],
        out_specs=[pl.BlockSpec((tm, tn), lambda i, j: (i, col_block * per + j))],
        out_shape=[jax.ShapeDtypeStruct(dh.shape, dh.dtype)],
        scratch_shapes=[pltpu.VMEM((tn // HEAD_DIM, tm, HEAD_DIM), F32), pltpu.VMEM((tn // HEAD_DIM, tm, HEAD_DIM), F32)],
        semantics=("parallel", "parallel"), vmem_mib=32, args=(dh, parts[0][0], parts[1], parts[2], *tabs),
        aliases={0: 0}, comm=comm)
    return dh, exchanged


def _grad_w_in(x, dh, half, name, comm=None):
    seq = x.shape[0]
    ts, td, te = 2048, D_MODEL // 2, SHARD_IN

    def body(half_ref, x_ref, dh_ref, o_ref):
        k = pl.program_id(1)
        part = _dot_tn(x_ref[...].astype(BF16), dh_ref[...])

        @pl.when(k == 0)
        def _():
            o_ref[...] = part

        @pl.when(k > 0)
        def _():
            o_ref[...] += part

    (g,), exchanged = _call(
        body, name=name, grid=(N_SHARDS, seq // ts),
        in_specs=[pl.BlockSpec((ts, td), lambda e, k, half_ref: (k, half_ref[0])),
                  pl.BlockSpec((ts, te), lambda e, k, half_ref: (k, e))],
        out_specs=[pl.BlockSpec((None, td, te), lambda e, k, half_ref: (e, 0, 0))],
        out_shape=[jax.ShapeDtypeStruct((N_SHARDS, td, te), F32)],
        scratch_shapes=[], semantics=("parallel", "arbitrary"), vmem_mib=56, args=(x, dh), comm=comm,
        prefetch=(half,))
    return g, exchanged


def _grad_w_out(y, dzb):
    seq = y.shape[0]
    ts, te = 2048, 1024

    def body(y_ref, dz_ref, o_ref):
        k = pl.program_id(1)
        part = _dot_tn(y_ref[...], dz_ref[...])

        @pl.when(k == 0)
        def _():
            o_ref[...] = part

        @pl.when(k > 0)
        def _():
            o_ref[...] += part

    return _pallas(
        body, name="grad_w_out", grid=(D_MODEL // te, seq // ts),
        in_specs=[pl.BlockSpec((ts, te), lambda e, k: (k, e)), pl.BlockSpec((ts, D_MODEL), lambda e, k: (k, 0))],
        out_specs=pl.BlockSpec((te, D_MODEL), lambda e, k: (e, 0)),
        out_shape=jax.ShapeDtypeStruct((D_MODEL, D_MODEL), F32),
        compiler_params=_params(("parallel", "arbitrary"), 56),
    )(y, dzb)


GRAD_X_LATE_SHARDS = 1
GRAD_X_PARTIAL_ROWS = 512


def _grad_x_partial(dh, w_in_g, dz, first, tiles, prev=None, comm=None):
    seq = dh.shape[0]
    tm, tk = GRAD_X_PARTIAL_ROWS, SHARD_IN

    def body(*refs):
        dh_ref, w_ref, dz_ref, o_ref = refs[-4:]
        k = pl.program_id(1)
        part = _dot_nt(dh_ref[...], w_ref[...])

        @pl.when(k == 0)
        def _():
            o_ref[...] = DEEPNORM_ALPHA * dz_ref[...] + part

        @pl.when(k > 0)
        def _():
            o_ref[...] += part

    carried = [] if prev is None else [prev]
    row = pl.BlockSpec((tm, D_MODEL), lambda i, k: (i + first, 0))
    (partial,), exchanged = _call(
        body, name="grad_x_partial_%d" % first, grid=(tiles, N_SHARDS - GRAD_X_LATE_SHARDS),
        in_specs=[ANY] * len(carried) + [
            pl.BlockSpec((tm, tk), lambda i, k: (i + first, k)),
            pl.BlockSpec((None, D_MODEL, tk), lambda i, k: (k, 0, 0)), row],
        out_specs=[row],
        out_shape=[jax.ShapeDtypeStruct((seq, D_MODEL), F32)],
        scratch_shapes=[], semantics=("parallel", "arbitrary"), vmem_mib=48, args=(*carried, dh, w_in_g, dz),
        aliases={0: 0} if carried else None, comm=comm)
    return partial, exchanged


def _grad_x_final(dh, w_in_g, partial):
    seq = dh.shape[0]
    tm, tk = 512, SHARD_IN
    k0 = N_SHARDS - GRAD_X_LATE_SHARDS

    def body(dh_ref, w_ref, p_ref, o_ref):
        k = pl.program_id(1)
        part = _dot_nt(dh_ref[...], w_ref[...])

        @pl.when(k == 0)
        def _():
            o_ref[...] = p_ref[...] + part

        @pl.when(k > 0)
        def _():
            o_ref[...] += part

    row = pl.BlockSpec((tm, D_MODEL), lambda i, k: (i, 0))
    return _pallas(
        body, name="grad_x_final", grid=(seq // tm, GRAD_X_LATE_SHARDS),
        in_specs=[pl.BlockSpec((tm, tk), lambda i, k: (i, k + k0)),
                  pl.BlockSpec((None, D_MODEL, tk), lambda i, k: (k + k0, 0, 0)), row],
        out_specs=row, out_shape=jax.ShapeDtypeStruct((seq, D_MODEL), F32),
        compiler_params=_params(("parallel", "arbitrary"), 48),
    )(dh, w_in_g, partial)


def _pool_weight(w_pool_sh):
    n_groups = len(POOL_WINDOWS)
    shard_c = POOL_GROUP_DIM // N_SHARDS
    return (w_pool_sh.reshape(N_SHARDS, n_groups, shard_c, POOL_GROUP_DIM).transpose(1, 0, 2, 3)
            .reshape(n_groups, POOL_GROUP_DIM, POOL_GROUP_DIM))


def _pool_grad_pieces(g_w_pool):
    n_groups = len(POOL_WINDOWS)
    half_c = POOL_GROUP_DIM // N_SHARDS // 2
    return (g_w_pool.reshape(n_groups, N_SHARDS, 2, half_c, POOL_GROUP_DIM).transpose(1, 2, 0, 3, 4)
            .reshape(N_SHARDS, 2, n_groups * half_c, POOL_GROUP_DIM))


def _step(x, target, w_bufs, pool_scale, gain, bias, place):
    seq = x.shape[0]
    tabs = _rope_tables(seq)
    core, chip_core, onward, plan = place
    qkv, hug, w_in_g, w_out_g, w_pool_sh = _in_proj_gathering(x, w_bufs, tabs, plan)
    o_list, st_list = [], []
    for p, dil in enumerate(DILATIONS):
        o, st = _attn_fwd(qkv[p], "attn_fwd_d%d" % dil)
        o_list.append(o)
        st_list.append(st)
    w_pool_g = _pool_weight(w_pool_sh)
    y, mixpre, lse_all, pooled = _mix_gate(o_list, st_list, hug, w_pool_g, pool_scale)
    dz, dzb, gain_part, bias_part, loss_part = _out_proj_loss(y, w_out_g, x, target, gain, bias)
    dh, dpo, do_list, stat_list = _dy_gate_bwd(dzb, w_out_g, hug, mixpre, pool_scale, lse_all)
    g_w_out = _grad_w_out(y, dzb)
    dh, g_w_pool, scale_part = _pool_bwd(dh, dpo, mixpre, pooled, w_pool_g, pool_scale)
    small = jnp.concatenate([scale_part, gain_part, bias_part, loss_part], axis=1)
    early = [g_w_out.reshape(N_SHARDS, 2, D_MODEL // (2 * N_SHARDS), D_MODEL), _pool_grad_pieces(g_w_pool)]

    bwd = lambda p, comm: _attn_bwd(qkv[p], do_list[p], stat_list[p], "attn_bwd_d%d" % DILATIONS[p], comm)
    part_a, recv = bwd(0, _exchange_halves(early))
    sums = [_add_own_half(g, r, core, "add_own_half_%d" % a) for a, (g, r) in enumerate(zip(early, recv))]
    part_b, recv = bwd(1, _scatter_to_chips([s[1] for s in sums]))
    bufs = [_add_chips([s[0]], r, chip_core, "add_chips_%d" % a) for a, (s, r) in enumerate(zip(sums, recv))]
    part_c, reduced = bwd(2, _share_with_sibling(bufs))
    parts = [part_a, part_b, part_c]
    dh, gathered = _sum_patterns(dh, [t[0] for t in parts], tabs, True, 0, "sum_dq", _gather_small(small))
    dh, _ = _sum_patterns(dh, [t[1] for t in parts], tabs, True, 1, "sum_dk")
    dh, _ = _sum_patterns(dh, [t[2] for t in parts], tabs, False, 2, "sum_dv")

    give, _ = _grad_w_in(x, dh, 1 - core, "grad_w_in_give")
    keep, recv = _grad_w_in(x, dh, core, "grad_w_in_keep", _send_to_sibling([give]))
    total = [keep, recv[0]]
    total_b = _add_pair(keep, recv[0], "add_own_half_w_in")
    n_tiles = seq // GRAD_X_PARTIAL_ROWS
    tiles = 3 * n_tiles // 8
    part, relayed = _grad_x_partial(dh, w_in_g, dz, 0, tiles, None, _relay_diagonal(total_b))
    total_b = _fold_relayed(total, total_b, relayed[0], onward)
    part, recv = _grad_x_partial(dh, w_in_g, dz, tiles, n_tiles - tiles, part, _scatter_to_neighbours(total_b))
    buf = _add_chips(total, recv[0], chip_core, "add_chips_w_in")
    g_x = _grad_x_final(dh, w_in_g, part)
    g_w_in = _run_exchange(_share_with_sibling([buf]), "share_w_in")[0]
    return g_x, g_w_in, reduced[0], reduced[1], small, gathered[0]


def _exchange_halves(grads):
    n = len(grads)

    def copies(src, dst, sems):
        x, y, c, _ = _mesh_place()
        return [_remote(src[a].at[j, 1 - c], dst[a].at[j], sems[0].at[a, j], sems[1].at[a, j], (x, y, 1 - c))
                for a in range(n) for j in range(N_SHARDS)]

    def start(src, dst, sems):
        for cp in copies(src, dst, sems):
            cp.start()

    def finish(src, dst, sems):
        for cp in copies(src, dst, sems):
            cp.wait()

    return _Exchange(grads, [jax.ShapeDtypeStruct((N_SHARDS,) + g.shape[2:], g.dtype) for g in grads], {},
                     [pltpu.SemaphoreType.DMA((n, N_SHARDS))] * 2, start, finish)


def _add_own_half(grad, recv, core, name):
    _, _, r, c = grad.shape
    tr = min(r, 256)

    def body(core_ref, g_ref, r_ref, o_ref, ob_ref):
        tot = g_ref[...] + r_ref[...]
        o_ref[...] = tot
        ob_ref[...] = tot.astype(BF16)

    out = pl.BlockSpec((None, tr, c), lambda j, i, core_ref: (j, i, 0))
    return _pallas(
        body, name=name,
        grid_spec=pltpu.PrefetchScalarGridSpec(
            num_scalar_prefetch=1, grid=(N_SHARDS, r // tr),
            in_specs=[pl.BlockSpec((None, None, tr, c), lambda j, i, core_ref: (j, core_ref[0], i, 0)),
                      pl.BlockSpec((None, tr, c), lambda j, i, core_ref: (j, i, 0))],
            out_specs=[out, out]),
        out_shape=[jax.ShapeDtypeStruct((N_SHARDS, r, c), F32), jax.ShapeDtypeStruct((N_SHARDS, r, c), BF16)],
        compiler_params=_params(("parallel", "parallel"), 32),
    )(core, grad, recv)


def _send_to_sibling(arrays):
    n = len(arrays)

    def copies(src, dst, sems):
        x, y, c, _ = _mesh_place()
        return [_remote(src[a], dst[a], sems[0].at[a], sems[1].at[a], (x, y, 1 - c)) for a in range(n)]

    def start(src, dst, sems):
        for cp in copies(src, dst, sems):
            cp.start()

    def finish(src, dst, sems):
        for cp in copies(src, dst, sems):
            cp.wait()

    return _Exchange(arrays, [jax.ShapeDtypeStruct(t.shape, t.dtype) for t in arrays], {},
                     [pltpu.SemaphoreType.DMA((n,))] * 2, start, finish)


def _add_pair(a, b, name):
    _, r, c = a.shape
    tr = min(r, 256)

    def body(a_ref, b_ref, ob_ref):
        ob_ref[...] = (a_ref[...] + b_ref[...]).astype(BF16)

    spec = pl.BlockSpec((None, tr, c), lambda j, i: (j, i, 0))
    return _pallas(
        body, name=name, grid=(N_SHARDS, r // tr), in_specs=[spec, spec], out_specs=spec,
        out_shape=jax.ShapeDtypeStruct(a.shape, BF16),
        compiler_params=_params(("parallel", "parallel"), 32),
    )(a, b)


def _scatter_to_chips(sums):
    n = len(sums)

    def copies(src, dst, sems):
        x, y, c, chips = _mesh_place()
        return [_remote(src[a].at[2 * cx + cy], dst[a].at[k], sems[0].at[a, k], sems[1].at[a, k], (cx, cy, c))
                for a in range(n) for k, (cx, cy) in enumerate(chips)]

    def start(src, dst, sems):
        for cp in copies(src, dst, sems):
            cp.start()

    def finish(src, dst, sems):
        for cp in copies(src, dst, sems):
            cp.wait()

    return _Exchange(sums, [jax.ShapeDtypeStruct((3,) + s.shape[1:], s.dtype) for s in sums], {},
                     [pltpu.SemaphoreType.DMA((n, 3))] * 2, start, finish)


def _add_chips(sums, recv, chip_core, name):
    _, r, c = sums[0].shape
    n_sums, n_recv = len(sums), recv.shape[0]
    tr = min(r, 256)

    def body(cc_ref, *refs):
        r_ref, o_ref = refs[n_sums:]
        tot = refs[0][...]
        for s_ref in refs[1:n_sums]:
            tot = tot + s_ref[...]
        for k in range(n_recv):
            tot = tot + r_ref[k].astype(F32)
        o_ref[...] = tot

    return _pallas(
        body, name=name,
        grid_spec=pltpu.PrefetchScalarGridSpec(
            num_scalar_prefetch=1, grid=(r // tr,),
            in_specs=[pl.BlockSpec((None, tr, c), lambda i, cc_ref: (cc_ref[0], i, 0))] * n_sums
            + [pl.BlockSpec((n_recv, tr, c), lambda i, cc_ref: (0, i, 0))],
            out_specs=pl.BlockSpec((None, tr, c), lambda i, cc_ref: (cc_ref[1], i, 0))),
        out_shape=jax.ShapeDtypeStruct((2, r, c), F32),
        compiler_params=_params(("parallel",), 32),
    )(chip_core, *sums, recv)


def _relay_diagonal(sums_b):
    def copy(src, dst, sems):
        x, y, c, _ = _mesh_place()
        diagonal = 2 * (1 - x) + (1 - y)
        return _remote(src[0].at[diagonal], dst[0], sems[0].at[0], sems[1].at[0], (x ^ (1 - c), y ^ c, c))

    def start(src, dst, sems):
        copy(src, dst, sems).start()

    def finish(src, dst, sems):
        copy(src, dst, sems).wait()

    return _Exchange([sums_b], [jax.ShapeDtypeStruct(sums_b.shape[1:], sums_b.dtype)], {},
                     [pltpu.SemaphoreType.DMA((1,))] * 2, start, finish)


def _fold_relayed(sums, sums_b, relayed, onward):
    _, r, c = sums[0].shape
    n_sums = len(sums)
    tr = min(r, 256)

    def body(on_ref, b_in_ref, *refs):
        r_ref, o_ref = refs[n_sums:]
        tot = refs[0][...]
        for s_ref in refs[1:n_sums]:
            tot = tot + s_ref[...]
        o_ref[...] = (tot + r_ref[...].astype(F32)).astype(BF16)

    return _pallas(
        body, name="fold_relayed",
        grid_spec=pltpu.PrefetchScalarGridSpec(
            num_scalar_prefetch=1, grid=(r // tr,),
            in_specs=[ANY] + [pl.BlockSpec((None, tr, c), lambda i, on_ref: (on_ref[0], i, 0))] * n_sums
            + [pl.BlockSpec((tr, c), lambda i, on_ref: (i, 0))],
            out_specs=pl.BlockSpec((None, tr, c), lambda i, on_ref: (on_ref[0], i, 0))),
        out_shape=jax.ShapeDtypeStruct(sums_b.shape, sums_b.dtype),
        input_output_aliases={1: 0},
        compiler_params=_params(("parallel",), 32),
    )(onward, sums_b, *sums, relayed)


def _scatter_to_neighbours(sums_b):
    def copies(src, dst, sems):
        x, y, c, chips = _mesh_place()
        return [_remote(src[0].at[2 * cx + cy], dst[0].at[k], sems[0].at[k], sems[1].at[k], (cx, cy, c))
                for k, (cx, cy) in enumerate(chips[:2])]

    def start(src, dst, sems):
        for cp in copies(src, dst, sems):
            cp.start()

    def finish(src, dst, sems):
        for cp in copies(src, dst, sems):
            cp.wait()

    return _Exchange([sums_b], [jax.ShapeDtypeStruct((2,) + sums_b.shape[1:], sums_b.dtype)], {},
                     [pltpu.SemaphoreType.DMA((2,))] * 2, start, finish)


def _share_with_sibling(bufs):
    n = len(bufs)

    def copies(dst, sems, half):
        x, y, c, _ = _mesh_place()
        h = c if half == "mine" else 1 - c
        return [_remote(dst[a].at[h], dst[a].at[h], sems[0].at[a], sems[1].at[a], (x, y, 1 - c)) for a in range(n)]

    def start(ins, dst, sems):
        for cp in copies(dst, sems, "mine"):
            cp.start()

    def finish(ins, dst, sems):
        for cp in copies(dst, sems, "theirs"):
            cp.wait_recv()
        for cp in copies(dst, sems, "mine"):
            cp.wait_send()

    return _Exchange(bufs, [jax.ShapeDtypeStruct(b.shape, b.dtype) for b in bufs], {a: a for a in range(n)},
                     [pltpu.SemaphoreType.DMA((n,))] * 2, start, finish)


def _adam_math(w, g, m, v):
    m = ADAM_B1 * m + (1.0 - ADAM_B1) * g
    v = ADAM_B2 * v + (1.0 - ADAM_B2) * (g * g)
    m_hat = m / (1.0 - ADAM_B1 ** ADAM_STEP)
    v_hat = v / (1.0 - ADAM_B2 ** ADAM_STEP)
    delta = -ADAM_LR * (m_hat / (jnp.sqrt(v_hat) + ADAM_EPS) + ADAM_WD * w)
    return delta, m, v


def _gather_small(small):
    def peers():
        x, y, c, _ = _mesh_place()
        return [(x ^ ((r >> 2) & 1), y ^ ((r >> 1) & 1), c ^ (r & 1)) for r in range(1, 8)], 4 * x + 2 * y + c

    def start(src, dst, sems):
        to, me = peers()
        for r, peer in enumerate(to):
            _remote(src[0], dst[0].at[me], sems[0].at[r], sems[1].at[r], peer).start()

    def finish(src, dst, sems):
        to, me = peers()
        for r, (px, py, pc) in enumerate(to):
            theirs = dst[0].at[4 * px + 2 * py + pc]
            _remote(theirs, theirs, sems[0].at[r], sems[1].at[r], (px, py, pc)).wait_recv()
        for r, peer in enumerate(to):
            _remote(src[0], dst[0].at[me], sems[0].at[r], sems[1].at[r], peer).wait_send()

    return _Exchange([small], [jax.ShapeDtypeStruct((8,) + small.shape, small.dtype)], {},
                     [pltpu.SemaphoreType.DMA((7,))] * 2, start, finish)


def _small_adamw(gathered, small, me, w_vec, m_vec, v_vec):
    n_par = w_vec.shape[1]

    def body(me_ref, a_ref, s_ref, w_ref, m_ref, v_ref, loss_ref, g_ref, d_ref, nm_ref, nv_ref):
        mine = s_ref[...]
        tot = jnp.where(me_ref[0] == 0, mine, a_ref[0])
        for d in range(1, 8):
            tot = tot + jnp.where(me_ref[0] == d, mine, a_ref[d])
        tot = jnp.sum(tot, axis=0, keepdims=True)
        sq = jnp.sum(tot[:, n_par:], axis=1, keepdims=True)
        loss_ref[...] = jnp.broadcast_to(sq * (0.5 / D_MODEL), loss_ref.shape)
        g = tot[:, :n_par]
        g_ref[...] = g
        d_ref[...], nm_ref[...], nv_ref[...] = _adam_math(w_ref[...], g, m_ref[...], v_ref[...])

    vm = pl.BlockSpec(memory_space=pltpu.VMEM)
    vec = jax.ShapeDtypeStruct((1, n_par), F32)
    return pl.pallas_call(
        body, name="small_adamw",
        grid_spec=pltpu.PrefetchScalarGridSpec(num_scalar_prefetch=1, grid=(), in_specs=[vm] * 5, out_specs=[vm] * 5),
        out_shape=[jax.ShapeDtypeStruct((1, 128), F32), vec, vec, vec, vec],
    )(me, gathered, small, w_vec, m_vec, v_vec)


def _adamw(w, g, m, v, name):
    r, c = w.shape
    tr = min(r, 256)

    def body(w_ref, g_ref, m_ref, v_ref, go_ref, d_ref, nm_ref, nv_ref):
        g = g_ref[...]
        go_ref[...] = g
        d_ref[...], nm_ref[...], nv_ref[...] = _adam_math(w_ref[...], g, m_ref[...], v_ref[...])

    spec = pl.BlockSpec((tr, c), lambda i: (i, 0))
    shape = jax.ShapeDtypeStruct((r, c), F32)
    return _pallas(
        body, name=name, grid=(r // tr,),
        in_specs=[spec] * 4, out_specs=[spec] * 4, out_shape=[shape] * 4,
        compiler_params=_params(("parallel",), 48),
    )(w, g, m, v)


def kernel(x, w_in, w_pool, pool_scale, w_out, ln_gain, ln_bias, loss_target, m_w_in, m_w_pool, m_pool_scale, m_w_out, m_ln_gain, m_ln_bias, v_w_in, v_w_pool, v_pool_scale, v_w_out, v_ln_gain, v_ln_bias):
    xi, yi, ci = lax.axis_index("x"), lax.axis_index("y"), lax.axis_index("c")
    chip = (2 * xi + yi).astype(jnp.int32).reshape(1)
    core = ci.astype(jnp.int32).reshape(1)
    n_groups = len(POOL_WINDOWS)
    shard_c = w_pool.shape[2]

    w_in_b = _cast_bf16(w_in[0], chip, "cast_w_in", 256)
    w_out_b = _cast_bf16(w_out[0], chip, "cast_w_out", 256)
    w_pool_b = _cast_bf16(w_pool[0].reshape(n_groups * shard_c, POOL_GROUP_DIM), chip, "cast_w_pool", 256)

    chip_core = jnp.concatenate([chip, core])
    onward = (2 * (xi ^ ci) + (yi ^ (1 - ci))).astype(jnp.int32).reshape(1)
    g_x, full_in, full_out, full_pool, small, small_all = _step(
        x[0], loss_target[0], [w_in_b, w_out_b, w_pool_b], pool_scale, ln_gain, ln_bias,
        (core, chip_core, onward, _in_proj_plan(xi, yi)))
    half_c = shard_c // 2
    grad_w_in = full_in.reshape(D_MODEL, SHARD_IN)
    grad_w_out = full_out.reshape(D_MODEL // N_SHARDS, D_MODEL)
    grad_w_pool = (full_pool.reshape(2, n_groups, half_c, POOL_GROUP_DIM).transpose(1, 0, 2, 3)
                   .reshape(n_groups * shard_c, POOL_GROUP_DIM))

    grad_w_in, d_in, nm_in, nv_in = _adamw(w_in[0], grad_w_in, m_w_in[0], v_w_in[0], "adamw_w_in")
    grad_w_out, d_out, nm_out, nv_out = _adamw(w_out[0], grad_w_out, m_w_out[0], v_w_out[0], "adamw_w_out")
    flat = lambda t: t[0].reshape(n_groups * shard_c, POOL_GROUP_DIM)
    grad_w_pool, d_pool, nm_pool, nv_pool = _adamw(flat(w_pool), grad_w_pool, flat(m_w_pool), flat(v_w_pool),
                                                   "adamw_w_pool")

    cat = lambda a, b, c: jnp.concatenate([a, b, c], axis=1)
    me = (4 * xi + 2 * yi + ci).astype(jnp.int32).reshape(1)
    loss_v, g_vec, d_vec, nm_vec, nv_vec = _small_adamw(
        small_all, small, me, cat(pool_scale, ln_gain, ln_bias), cat(m_pool_scale, m_ln_gain, m_ln_bias),
        cat(v_pool_scale, v_ln_gain, v_ln_bias))

    def split(vec):
        return vec[:, :D_POOL], vec[:, D_POOL:D_POOL + D_MODEL], vec[:, D_POOL + D_MODEL:]

    g_scale, g_gain, g_bias = split(g_vec)
    d_scale, d_gain, d_bias = split(d_vec)
    nm_scale, nm_gain, nm_bias = split(nm_vec)
    nv_scale, nv_gain, nv_bias = split(nv_vec)
    pool_shape = w_pool.shape
    return (loss_v[0, 0], g_x[None],
            grad_w_in[None], grad_w_pool.reshape(pool_shape), g_scale, grad_w_out[None], g_gain, g_bias,
            d_in[None], d_pool.reshape(pool_shape), d_scale, d_out[None], d_gain, d_bias,
            nm_in[None], nm_pool.reshape(pool_shape), nm_scale, nm_out[None], nm_gain, nm_bias,
            nv_in[None], nv_pool.reshape(pool_shape), nv_scale, nv_out[None], nv_gain, nv_bias)
```

```python
import functools

import jax
import jax.numpy as jnp
import numpy as np
from jax import lax
from jax.experimental import pallas as pl
from jax.experimental.pallas import tpu as pltpu

F32 = jnp.float32
BF16 = jnp.bfloat16
MESH = pl.DeviceIdType.MESH
ANY = pl.BlockSpec(memory_space=pl.ANY)

D_MODEL = 2048
D_ATTN = 1024
D_POOL = 1024
HEAD_DIM = 128
N_HEADS = 8
ROPE_DIM = 32
ROPE_THETA = 500000.0
DILATIONS = (1, 4, 16)
KEY_BLOCK = 128
CHUNK = 2 * KEY_BLOCK
STAT_LANES = 128
POOL_WINDOWS = (2, 4, 8, 16)
POOL_GROUP_DIM = 256
POOL_HALO = 16
D_QKV = 3 * D_ATTN
D_UG = D_POOL + D_MODEL
D_IN = D_QKV + D_UG
N_SHARDS = 4
SHARD_IN = D_IN // N_SHARDS
LN_EPS = 1e-5
DEEPNORM_ALPHA = 2.0 ** 0.25
ADAM_LR = 0.001
ADAM_B1 = 0.9
ADAM_B2 = 0.999
ADAM_EPS = 1e-08
ADAM_WD = 0.01
ADAM_STEP = 10
NEG = -1e30
MIB = 1024 * 1024


def _params(sem, vmem_mib):
    return pltpu.CompilerParams(dimension_semantics=sem, vmem_limit_bytes=vmem_mib * MIB)


def _pallas(body, **kwargs):
    pin = lambda s: pltpu.HBM(s.shape, s.dtype) if len(s.shape) >= 2 else s
    out_shape = kwargs.pop("out_shape")
    out_shape = [pin(s) for s in out_shape] if isinstance(out_shape, (list, tuple)) else pin(out_shape)
    call = pl.pallas_call(body, out_shape=out_shape, **kwargs)

    def run(*operands):
        return call(*[pltpu.with_memory_space_constraint(o, pltpu.HBM) if o.ndim >= 2 else o for o in operands])

    return run


class _Exchange:
    def __init__(self, operands, out_shape, aliases, sems, start, finish):
        self.operands, self.out_shape, self.aliases, self.sems = list(operands), list(out_shape), dict(aliases), list(sems)
        self.start, self.finish = start, finish


def _run_exchange(comm, name):
    n_in, n_out = len(comm.operands), len(comm.out_shape)

    def body(*refs):
        ins, outs, sems = refs[:n_in], refs[n_in:n_in + n_out], refs[n_in + n_out:]
        comm.start(ins, outs, sems)
        comm.finish(ins, outs, sems)

    return _pallas(
        body, name=name, in_specs=[ANY] * n_in, out_specs=[ANY] * n_out, out_shape=comm.out_shape,
        input_output_aliases=comm.aliases, scratch_shapes=comm.sems,
    )(*comm.operands)


def _call(body, *, name, grid, in_specs, out_specs, out_shape, scratch_shapes, semantics, vmem_mib, args,
          aliases=None, comm=None, prefetch=()):
    aliases = dict(aliases or {})
    n_pre, n_in, n_out, n_scr = len(prefetch), len(in_specs), len(out_specs), len(scratch_shapes)
    c_in, c_out = (len(comm.operands), len(comm.out_shape)) if comm else (0, 0)
    c_shapes, c_sems, c_operands = (comm.out_shape, comm.sems, comm.operands) if comm else ([], [], [])

    def hosted(*refs):
        pre, refs = refs[:n_pre], refs[n_pre:]
        a = n_in
        b = a + c_in
        c = b + n_out
        d = c + c_out
        e = d + n_scr
        if comm is None:
            body(*pre, *refs)
            return
        ids = [pl.program_id(k) for k in range(len(grid))]
        first = functools.reduce(jnp.logical_and, [i == 0 for i in ids])
        last = functools.reduce(jnp.logical_and, [i == g - 1 for i, g in zip(ids, grid)])

        @pl.when(first)
        def _():
            comm.start(refs[a:b], refs[c:d], refs[e:])

        body(*pre, *refs[:a], *refs[b:c], *refs[d:e])

        @pl.when(last)
        def _():
            comm.finish(refs[a:b], refs[c:d], refs[e:])

    if comm:
        semantics = ("arbitrary",) * len(grid)
        for i, o in comm.aliases.items():
            aliases[n_pre + n_in + i] = n_out + o
    outs = _pallas(
        hosted, name=name,
        grid_spec=pltpu.PrefetchScalarGridSpec(
            num_scalar_prefetch=n_pre, grid=grid, in_specs=list(in_specs) + [ANY] * c_in,
            out_specs=list(out_specs) + [ANY] * c_out, scratch_shapes=list(scratch_shapes) + c_sems),
        out_shape=list(out_shape) + c_shapes, input_output_aliases=aliases,
        compiler_params=_params(semantics, vmem_mib),
    )(*prefetch, *args, *c_operands)
    return list(outs[:n_out]), list(outs[n_out:])


def _dot_nn(a, b):
    return jnp.dot(a, b, preferred_element_type=F32)


def _dot_nt(a, b):
    return lax.dot_general(a, b, (((1,), (1,)), ((), ())), preferred_element_type=F32)


def _dot_tn(a, b):
    return lax.dot_general(a, b, (((0,), (0,)), ((), ())), preferred_element_type=F32)


def _fold_rows(a):
    r, c = a.shape
    return jnp.sum(a.reshape(r // 8, 8, c), axis=0)


def _cast_bf16(a, chip, name, rows):
    r, c = a.shape

    def body(chip_ref, a_ref, o_ref):
        o_ref[...] = a_ref[...].astype(BF16)

    return _pallas(
        body, name=name,
        grid_spec=pltpu.PrefetchScalarGridSpec(
            num_scalar_prefetch=1, grid=(r // rows,),
            in_specs=[pl.BlockSpec((rows, c), lambda i, chip_ref: (i, 0))],
            out_specs=pl.BlockSpec((None, rows, c), lambda i, chip_ref: (chip_ref[0], i, 0))),
        out_shape=jax.ShapeDtypeStruct((N_SHARDS, r, c), BF16),
        compiler_params=_params(("parallel",), 32),
    )(chip, a)


def _mesh_place():
    x, y, c = lax.axis_index("x"), lax.axis_index("y"), lax.axis_index("c")
    return x, y, c, [(1 - x, y), (x, 1 - y), (1 - x, 1 - y)]


def _remote(src, dst, send_sem, recv_sem, to):
    return pltpu.make_async_remote_copy(src_ref=src, dst_ref=dst, send_sem=send_sem, recv_sem=recv_sem,
                                        device_id=to, device_id_type=MESH)


def _rope_tables(seq):
    half = ROPE_DIM // 2
    inv_freq = (np.float64(ROPE_THETA) ** (-(2.0 * np.arange(half, dtype=np.float64)) / ROPE_DIM)).astype(np.float32)
    ang = np.arange(seq, dtype=np.float32)[:, None] * inv_freq[None, :]
    cos = np.cos(ang.astype(np.float64)).astype(np.float32)
    sin = np.sin(ang.astype(np.float64)).astype(np.float32)
    pad = np.zeros((seq, HEAD_DIM - ROPE_DIM), np.float32)
    zeros = np.zeros((seq, half), np.float32)
    c_tab = np.concatenate([cos, cos, pad + 1.0], axis=1)
    up_tab = np.concatenate([-sin, zeros, pad], axis=1)
    down_tab = np.concatenate([zeros, sin, pad], axis=1)
    return jnp.asarray(c_tab), jnp.asarray(up_tab), jnp.asarray(down_tab)


def _rotate_heads(t, c_tab, up_tab, down_tab):
    outs = []
    for h in range(t.shape[1] // HEAD_DIM):
        th = t[:, h * HEAD_DIM:(h + 1) * HEAD_DIM]
        up = pltpu.roll(th, HEAD_DIM - ROPE_DIM // 2, axis=1)
        down = pltpu.roll(th, ROPE_DIM // 2, axis=1)
        outs.append(th * c_tab + up * up_tab + down * down_tab)
    return outs[0] if len(outs) == 1 else jnp.concatenate(outs, axis=1)


def _to_pattern(slabs_ref, dst_ref, dil, dtype):
    n_slabs, rows, _ = slabs_ref.shape
    for s in range(n_slabs):
        for r in range(dil):
            dst_ref[r, :, s * 128:(s + 1) * 128] = slabs_ref[s, pl.ds(r, rows // dil, dil), :].astype(dtype)


def _from_pattern(src_ref, slabs_ref, dil):
    n_slabs, rows, _ = slabs_ref.shape
    for s in range(n_slabs):
        for r in range(dil):
            slabs_ref[s, pl.ds(r, rows // dil, dil), :] = src_ref[r, :, s * 128:(s + 1) * 128].astype(F32)


def _store_slabs(slabs_ref, value):
    for s in range(slabs_ref.shape[0]):
        slabs_ref[s] = value[:, s * 128:(s + 1) * 128]


W_IN_CHUNKS = 4


def _in_proj_plan(x, y):
    shards = [2 * x + y, 2 * (1 - x) + y, 2 * x + (1 - y), 2 * (1 - x) + (1 - y)]
    last_row = jnp.int32(-2)

    def table(active, col_of):
        cols, rows = [], []
        first_col = functools.reduce(lambda acc, j: jnp.where(active[j], col_of(shards[j]), acc), reversed(range(4)),
                                     jnp.int32(0))
        held_col, seen = first_col, jnp.bool_(False)
        for j in range(4):
            cols.append(jnp.where(active[j], col_of(shards[j]), held_col))
            rows.append(jnp.where(active[j], -1, jnp.where(seen, last_row, 0)))
            held_col = jnp.where(active[j], col_of(shards[j]), held_col)
            seen = jnp.logical_or(seen, active[j])
        return cols, rows

    q_cols, q_rows = table([s < 2 for s in shards], lambda s: s)
    h_cols, h_rows = table([s >= 2 for s in shards], lambda s: s - 2)
    return jnp.stack([jnp.asarray(v, jnp.int32) for v in shards + q_cols + q_rows + h_cols + h_rows])


def _in_proj_gathering(x, w_bufs, tabs, plan):
    seq = x.shape[0]
    tm, tn = 512, SHARD_IN
    n_tiles = seq // tm
    heads = tn // HEAD_DIM
    k_heads_in_second = 2 * D_ATTN // HEAD_DIM - heads
    d4, d16 = DILATIONS[1], DILATIONS[2]
    DIAGONAL = 2
    chunk = D_MODEL // 2 // W_IN_CHUNKS
    early = [(0, D_MODEL // 2, q * chunk, chunk) for q in range(W_IN_CHUNKS)]
    late = [(a, w_bufs[a].shape[1] // 2, 0, w_bufs[a].shape[1] // 2) for a in (1, 2)]
    pieces = early + late
    early_ids, late_ids = range(len(early)), range(len(early), len(pieces))

    def body(plan_ref, x_ref, w_in_in, w_out_in, w_pool_in, c_ref, up_ref, down_ref,
             o1_ref, o4_ref, o16_ref, hug_ref, w_ref, w_out_ref, w_pool_ref,
             wbuf_ref, res_ref, w_sem, ici_send, ici_recv, d2d_send, d2d_recv):
        j, i = pl.program_id(0), pl.program_id(1)
        mx, my, mc, chips = _mesh_place()
        sibling = (mx, my, 1 - mc)
        gathered = (w_ref, w_out_ref, w_pool_ref)
        chip_of = lambda k: 2 * chips[k][0] + chips[k][1]

        def piece(n, chip, core):
            a, per_core, offset, size = pieces[n]
            return gathered[a].at[chip, pl.ds(core * per_core + offset, size)]

        def to_neighbour(k, n):
            mine = piece(n, 2 * mx + my, mc)
            return _remote(mine, mine, ici_send.at[n, k], ici_recv.at[n, k], (*chips[k], mc))

        def relay(n):
            theirs = piece(n, 2 * (mx ^ (1 - mc)) + (my ^ mc), mc)
            return _remote(theirs, theirs, ici_send.at[n, DIAGONAL], ici_recv.at[n, DIAGONAL], (mx ^ mc, my ^ (1 - mc), mc))

        def arrival(k, n):
            theirs = piece(n, chip_of(k), mc)
            return _remote(theirs, theirs, ici_send.at[n, k], ici_recv.at[n, k], (*chips[k], mc))

        def to_sibling(k, n, core):
            theirs = piece(n, chip_of(k), core)
            return _remote(theirs, theirs, d2d_send.at[n, k], d2d_recv.at[n, k], sibling)

        def take(k, ids):
            for n in ids:
                arrival(k, n).wait_recv()
                to_sibling(k, n, mc).start()

        def taken(k, ids):
            for n in ids:
                to_sibling(k, n, 1 - mc).wait_recv()

        first_tile = i == 0

        @pl.when(jnp.logical_and(j == 0, first_tile))
        def _():
            for n in range(len(pieces)):
                for k in range(DIAGONAL):
                    to_neighbour(k, n).start()

        @pl.when(jnp.logical_and(j == 1, first_tile))
        def _():
            take(0, early_ids)
            taken(0, early_ids)

        @pl.when(jnp.logical_and(j == 2, first_tile))
        def _():
            take(1, early_ids)
            for n in early_ids:
                relay(n).start()
            taken(1, early_ids)
            for k in range(DIAGONAL):
                take(k, late_ids)
            for n in late_ids:
                relay(n).start()
            for k in range(DIAGONAL):
                taken(k, late_ids)

        @pl.when(jnp.logical_and(j == 3, first_tile))
        def _():
            take(DIAGONAL, range(len(pieces)))
            taken(DIAGONAL, range(len(pieces)))

        shard = plan_ref[j]

        @pl.when(first_tile)
        def _():
            cp = pltpu.make_async_copy(w_ref.at[shard], wbuf_ref, w_sem)
            cp.start()
            cp.wait()

        xb = x_ref[...].astype(BF16)
        group = 4 * HEAD_DIM
        accs = [_dot_nn(xb, wbuf_ref[:, g * group:(g + 1) * group]) for g in range(tn // group)]

        def emit_qkv(rotated_heads):
            for h in range(heads):
                lanes = (h * HEAD_DIM) % group
                th = accs[h * HEAD_DIM // group][:, lanes:lanes + HEAD_DIM]
                if h < rotated_heads:
                    th = _rotate_heads(th, c_ref[...], up_ref[...], down_ref[...])
                res_ref[h] = th
                o1_ref[:, h * HEAD_DIM:(h + 1) * HEAD_DIM] = th.astype(BF16)
            _to_pattern(res_ref, o4_ref, d4, BF16)
            _to_pattern(res_ref, o16_ref, d16, BF16)

        @pl.when(shard == 0)
        def _():
            emit_qkv(heads)

        @pl.when(shard == 1)
        def _():
            emit_qkv(k_heads_in_second)

        @pl.when(shard >= 2)
        def _():
            for g, acc in enumerate(accs):
                hug_ref[:, g * group:(g + 1) * group] = acc.astype(BF16)

        @pl.when(jnp.logical_and(j == 3, i == n_tiles - 1))
        def _():
            for n in range(len(pieces)):
                for k in range(DIAGONAL):
                    to_neighbour(k, n).wait_send()
                relay(n).wait_send()
                for k in range(DIAGONAL + 1):
                    to_sibling(k, n, mc).wait_send()

    def held(base, last):
        return lambda j, i, plan_ref: jnp.where(plan_ref[base + j] == -1, i,
                                                jnp.where(plan_ref[base + j] == -2, last, 0))

    q_row, h_row = held(8, n_tiles - 1), held(16, n_tiles - 1)
    tab_spec = pl.BlockSpec((tm, HEAD_DIM), lambda j, i, plan_ref: (i, 0))
    sems = [pltpu.SemaphoreType.DMA((len(pieces), 3))] * 4
    o1, o4, o16, hug, w_in_g, w_out_g, w_pool_g = _pallas(
        body, name="in_proj_gathering",
        grid_spec=pltpu.PrefetchScalarGridSpec(
            num_scalar_prefetch=1, grid=(N_SHARDS, n_tiles),
            in_specs=[pl.BlockSpec((tm, D_MODEL), lambda j, i, plan_ref: (i, 0)), ANY, ANY, ANY,
                      tab_spec, tab_spec, tab_spec],
            out_specs=[pl.BlockSpec((tm, tn), lambda j, i, p: (q_row(j, i, p), p[4 + j])),
                       pl.BlockSpec((d4, tm // d4, tn), lambda j, i, p: (0, q_row(j, i, p), p[4 + j])),
                       pl.BlockSpec((d16, tm // d16, tn), lambda j, i, p: (0, q_row(j, i, p), p[4 + j])),
                       pl.BlockSpec((tm, tn), lambda j, i, p: (h_row(j, i, p), p[12 + j])),
                       ANY, ANY, ANY],
            scratch_shapes=[pltpu.VMEM((D_MODEL, tn), BF16), pltpu.VMEM((heads, tm, HEAD_DIM), F32),
                            pltpu.SemaphoreType.DMA(())] + sems),
        out_shape=[jax.ShapeDtypeStruct((seq, D_QKV), BF16),
                   jax.ShapeDtypeStruct((d4, seq // d4, D_QKV), BF16),
                   jax.ShapeDtypeStruct((d16, seq // d16, D_QKV), BF16),
                   jax.ShapeDtypeStruct((seq, D_UG), BF16)]
        + [jax.ShapeDtypeStruct(b.shape, b.dtype) for b in w_bufs],
        input_output_aliases={2: 4, 3: 5, 4: 6},
        compiler_params=_params(("arbitrary", "arbitrary"), 52),
    )(plan, x, *w_bufs, *tabs)
    return [o1[None], o4, o16], hug, w_in_g, w_out_g, w_pool_g


def _band_masks():
    row = lax.broadcasted_iota(jnp.int32, (KEY_BLOCK, KEY_BLOCK), 0)
    col = lax.broadcasted_iota(jnp.int32, (KEY_BLOCK, KEY_BLOCK), 1)
    return col <= row, col >= row


def _attn_fwd(qkv, name):
    dil, n, _ = qkv.shape
    scale = HEAD_DIM ** -0.5
    lo, hi = slice(0, KEY_BLOCK), slice(KEY_BLOCK, CHUNK)

    def body(q_ref, k_ref, v_ref, kb_ref, vb_ref, o_ref, st_ref):
        i = pl.program_id(1)
        cur_mask, prev_mask = _band_masks()
        before_mask = jnp.logical_and(prev_mask, i > 0)
        lane = lax.broadcasted_iota(jnp.int32, (KEY_BLOCK, STAT_LANES), 1)
        tasks = [(rows, h) for rows in (lo, hi) for h in range(N_HEADS)]
        head = lambda h: slice(h * HEAD_DIM, (h + 1) * HEAD_DIM)

        def prev_of(rows, h):
            if rows is lo:
                return kb_ref[:, head(h)], vb_ref[:, head(h)], before_mask
            return k_ref[lo, head(h)], v_ref[lo, head(h)], prev_mask

        scores = []
        for rows, h in tasks:
            q = q_ref[rows, head(h)]
            scores.append((_dot_nt(q, prev_of(rows, h)[0]), _dot_nt(q, k_ref[rows, head(h)])))
        probs = []
        for (rows, h), (qk_prev, qk_cur) in zip(tasks, scores):
            s_prev = jnp.where(prev_of(rows, h)[2], qk_prev * scale, NEG)
            s_cur = jnp.where(cur_mask, qk_cur * scale, NEG)
            m = jnp.max(jnp.maximum(s_prev, s_cur), axis=-1, keepdims=True)
            p_prev = jnp.exp(s_prev - m)
            p_cur = jnp.exp(s_cur - m)
            den = jnp.sum(p_prev + p_cur, axis=-1, keepdims=True)
            probs.append((p_prev.astype(BF16), p_cur.astype(BF16), den, m + jnp.log(den)))
        stats = [jnp.zeros((KEY_BLOCK, STAT_LANES), F32), jnp.zeros((KEY_BLOCK, STAT_LANES), F32)]
        for (rows, h), (p_prev, p_cur, den, lse) in zip(tasks, probs):
            o = _dot_nn(p_cur, v_ref[rows, head(h)]) + _dot_nn(p_prev, prev_of(rows, h)[1])
            o_ref[rows, head(h)] = (o / den).astype(BF16)
            b = 0 if rows is lo else 1
            stats[b] = jnp.where(lane == h, lse, stats[b])
        st_ref[lo, :] = stats[0]
        st_ref[hi, :] = stats[1]

    main = lambda cb: pl.BlockSpec((None, CHUNK, D_ATTN), lambda r, i: (r, i, cb))
    before = lambda cb: pl.BlockSpec((None, KEY_BLOCK, D_ATTN), lambda r, i: (r, jnp.maximum(2 * i - 1, 0), cb))
    return _pallas(
        body, name=name, grid=(dil, n // CHUNK),
        in_specs=[main(0), main(1), main(2), before(1), before(2)],
        out_specs=[main(0), pl.BlockSpec((None, CHUNK, STAT_LANES), lambda r, i: (r, i, 0))],
        out_shape=[jax.ShapeDtypeStruct((dil, n, D_ATTN), BF16), jax.ShapeDtypeStruct((dil, n, STAT_LANES), F32)],
        compiler_params=_params(("parallel", "parallel"), 40),
    )(qkv, qkv, qkv, qkv, qkv)


def _attn_bwd(qkv, do, stats, name, comm=None):
    dil, n, _ = qkv.shape
    n_blocks = n // KEY_BLOCK
    last = n // CHUNK - 1
    scale = HEAD_DIM ** -0.5
    lo, hi = slice(0, KEY_BLOCK), slice(KEY_BLOCK, CHUNK)

    def body(q_ref, k_ref, v_ref, kb_ref, vb_ref, qa_ref, do_ref, doa_ref, st_ref, sta_ref, dq_ref, dk_ref, dv_ref):
        i = pl.program_id(1)
        cur_mask, prev_mask = _band_masks()
        before_mask = jnp.logical_and(prev_mask, i > 0)
        after_mask = jnp.logical_and(prev_mask, i < last)

        rows_cat = lambda a, b: jnp.concatenate([a, b], axis=0)
        masks = (jnp.concatenate([before_mask, cur_mask], axis=1), jnp.concatenate([prev_mask, cur_mask], axis=1),
                 after_mask)

        def operands(h):
            cols = slice(h * HEAD_DIM, (h + 1) * HEAD_DIM)
            lse_c, del_c = slice(h, h + 1), slice(N_HEADS + h, N_HEADS + h + 1)
            q = (q_ref[lo, cols], q_ref[hi, cols], qa_ref[:, cols])
            do = (do_ref[lo, cols], do_ref[hi, cols], doa_ref[:, cols])
            keys = (rows_cat(kb_ref[:, cols], k_ref[lo, cols]), k_ref[:, cols], k_ref[hi, cols])
            vals = (rows_cat(vb_ref[:, cols], v_ref[lo, cols]), v_ref[:, cols], v_ref[hi, cols])
            st = ((st_ref[lo, lse_c], st_ref[lo, del_c]), (st_ref[hi, lse_c], st_ref[hi, del_c]),
                  (sta_ref[:, lse_c], sta_ref[:, del_c]))
            return cols, q, do, keys, vals, st

        group = N_HEADS // 2
        for first_head in range(0, N_HEADS, group):
            heads = range(first_head, first_head + group)
            raw = {}
            for h in heads:
                _, q, do, keys, vals, _ = operands(h)
                raw[h] = [(_dot_nt(q[j], keys[j]), _dot_nt(do[j], vals[j])) for j in range(3)]
            grads = {}
            for h in heads:
                st = operands(h)[5]
                grads[h] = []
                for j in range(3):
                    qk, dp = raw[h][j]
                    lse, delta = st[j]
                    p = jnp.exp(jnp.where(masks[j], qk * scale, NEG) - lse)
                    grads[h].append((p.astype(BF16), (p * (dp - delta) * scale).astype(BF16)))
            for h in heads:
                cols, q, do, keys, _, _ = operands(h)
                (p0, ds0), (p1, ds1), (pa, dsa) = grads[h]
                own, nxt = slice(KEY_BLOCK, CHUNK), slice(0, KEY_BLOCK)

                def put(ref, rows, val, cols=cols):
                    ref[rows, cols] = val.astype(ref.dtype)

                put(dq_ref, lo, _dot_nn(ds0, keys[0]))
                put(dq_ref, hi, _dot_nn(ds1, keys[1]))
                put(dk_ref, lo, _dot_tn(rows_cat(ds0[:, own], ds1[:, nxt]), q_ref[:, cols]))
                put(dk_ref, hi, _dot_tn(rows_cat(ds1[:, own], dsa), rows_cat(q[1], q[2])))
                put(dv_ref, lo, _dot_tn(rows_cat(p0[:, own], p1[:, nxt]), do_ref[:, cols]))
                put(dv_ref, hi, _dot_tn(rows_cat(p1[:, own], pa), rows_cat(do[1], do[2])))

    def spec(rows, width, row_of, cb):
        return pl.BlockSpec((None, rows, width), lambda r, i: (r, row_of(i), cb))

    same = lambda i: i
    before = lambda i: jnp.maximum(2 * i - 1, 0)
    after = lambda i: jnp.minimum(2 * i + 2, n_blocks - 1)
    out = spec(CHUNK, D_ATTN, same, 0)
    return _call(
        body, name=name, grid=(dil, n // CHUNK),
        in_specs=[spec(CHUNK, D_ATTN, same, 0), spec(CHUNK, D_ATTN, same, 1), spec(CHUNK, D_ATTN, same, 2),
                  spec(KEY_BLOCK, D_ATTN, before, 1), spec(KEY_BLOCK, D_ATTN, before, 2),
                  spec(KEY_BLOCK, D_ATTN, after, 0),
                  spec(CHUNK, D_ATTN, same, 0), spec(KEY_BLOCK, D_ATTN, after, 0),
                  spec(CHUNK, STAT_LANES, same, 0), spec(KEY_BLOCK, STAT_LANES, after, 0)],
        out_specs=[out, out, out],
        out_shape=[jax.ShapeDtypeStruct((dil, n, D_ATTN), BF16)] * 3,
        scratch_shapes=[], semantics=("parallel", "parallel"), vmem_mib=40,
        args=(qkv, qkv, qkv, qkv, qkv, qkv, do, do, stats, stats), comm=comm)


def _window_sums(ext, window, backward):
    rows = ext.shape[0]
    acc, span = ext, 1
    while span < window:
        acc = acc + pltpu.roll(acc, (rows - span) if backward else span, axis=0)
        span *= 2
    return acc


def _mix_gate(o_list, st_list, hug, w_pool_g, pool_scale):
    seq = hug.shape[0]
    tm = 256
    halo_blocks = tm // POOL_HALO
    d4, d16 = DILATIONS[1], DILATIONS[2]

    def body(o1_ref, o4_ref, o16_ref, l1_ref, l4_ref, l16_ref, u_ref, halo_ref, ga_ref, gp_ref, wp_ref, sc_ref,
             y_ref, mix_ref, lse_ref, pooled_ref, n4_ref, n16_ref, nl4_ref, nl16_ref):
        i = pl.program_id(0)
        _from_pattern(o4_ref, n4_ref, d4)
        _from_pattern(o16_ref, n16_ref, d16)
        _from_pattern(l4_ref, nl4_ref, d4)
        _from_pattern(l16_ref, nl16_ref, d16)
        la, lb, lc = l1_ref[...], nl4_ref[0], nl16_ref[0]
        mx = jnp.maximum(jnp.maximum(la, lb), lc)
        ea, eb, ec = jnp.exp(la - mx), jnp.exp(lb - mx), jnp.exp(lc - mx)
        tot = ea + eb + ec
        lse_ref[...] = mx + jnp.log(tot)
        wa, wb, wc = ea / tot, eb / tot, ec / tot
        ga = ga_ref[...].astype(F32)
        silu_a = ga * jax.nn.sigmoid(ga)
        for h in range(N_HEADS):
            cols = slice(h * HEAD_DIM, (h + 1) * HEAD_DIM)
            hc = slice(h, h + 1)
            attn = wa[:, hc] * o1_ref[:, cols].astype(F32) + wb[:, hc] * n4_ref[h] + wc[:, hc] * n16_ref[h]
            mix_ref[:, cols] = attn.astype(BF16)
            y_ref[:, cols] = (attn * silu_a[:, cols]).astype(BF16)

        u = u_ref[...].astype(F32)
        halo = jnp.where(i > 0, halo_ref[...].astype(F32), 0.0)
        ext = jnp.concatenate([halo, u], axis=0)
        pos = i * tm + lax.broadcasted_iota(jnp.int32, (tm, 1), 0)
        gp = gp_ref[...].astype(F32)
        gated_scale = sc_ref[...] * (gp * jax.nn.sigmoid(gp))
        for g, window in enumerate(POOL_WINDOWS):
            cols = slice(g * POOL_GROUP_DIM, (g + 1) * POOL_GROUP_DIM)
            sums = _window_sums(ext[:, cols], window, backward=False)[POOL_HALO:, :]
            count = jnp.minimum(pos + 1, window).astype(F32)
            pooled = (sums / count - u[:, cols]).astype(BF16)
            pooled_ref[:, cols] = pooled
            pre = _dot_nn(pooled, wp_ref[g])
            out_cols = slice(D_ATTN + g * POOL_GROUP_DIM, D_ATTN + (g + 1) * POOL_GROUP_DIM)
            mix_ref[:, out_cols] = pre.astype(BF16)
            y_ref[:, out_cols] = (pre * gated_scale[:, cols]).astype(BF16)

    row = lambda width, cb=0: pl.BlockSpec((tm, width), lambda i: (i, cb))
    pat = lambda d, width: pl.BlockSpec((d, tm // d, width), lambda i: (0, i, 0))
    return _pallas(
        body, name="mix_gate", grid=(seq // tm,),
        in_specs=[row(D_ATTN), pat(d4, D_ATTN), pat(d16, D_ATTN),
                  row(STAT_LANES), pat(d4, STAT_LANES), pat(d16, STAT_LANES),
                  row(D_POOL),
                  pl.BlockSpec((POOL_HALO, D_POOL), lambda i: (jnp.maximum(i * halo_blocks - 1, 0), 0)),
                  row(D_ATTN, 1), row(D_POOL, 2),
                  pl.BlockSpec((len(POOL_WINDOWS), POOL_GROUP_DIM, POOL_GROUP_DIM), lambda i: (0, 0, 0)),
                  pl.BlockSpec((1, D_POOL), lambda i: (0, 0))],
        out_specs=[row(D_MODEL), row(D_MODEL), row(STAT_LANES), row(D_POOL)],
        out_shape=[jax.ShapeDtypeStruct((seq, D_MODEL), BF16), jax.ShapeDtypeStruct((seq, D_MODEL), BF16),
                   jax.ShapeDtypeStruct((seq, STAT_LANES), F32), jax.ShapeDtypeStruct((seq, D_POOL), BF16)],
        scratch_shapes=[pltpu.VMEM((N_HEADS, tm, HEAD_DIM), F32), pltpu.VMEM((N_HEADS, tm, HEAD_DIM), F32),
                        pltpu.VMEM((1, tm, STAT_LANES), F32), pltpu.VMEM((1, tm, STAT_LANES), F32)],
        compiler_params=_params(("parallel",), 48),
    )(o_list[0][0], o_list[1], o_list[2], st_list[0][0], st_list[1], st_list[2],
      hug, hug, hug, hug, w_pool_g, pool_scale)


def _out_proj_loss(y, w_out_g, x, target, gain, bias):
    seq = x.shape[0]
    tm = 512

    def body(y_ref, w_ref, x_ref, t_ref, g_ref, b_ref, dz_ref, dzb_ref, gg_ref, gb_ref, loss_ref):
        @pl.when(pl.program_id(0) == 0)
        def _():
            gg_ref[...] = jnp.zeros_like(gg_ref)
            gb_ref[...] = jnp.zeros_like(gb_ref)
            loss_ref[...] = jnp.zeros_like(loss_ref)

        halves = [slice(0, tm // 2), slice(tm // 2, tm)]
        projected = [_dot_nn(y_ref[rows, :], w_ref[...]) for rows in halves]
        for rows, out in zip(halves, projected):
            z = DEEPNORM_ALPHA * x_ref[rows, :] + out
            mu = jnp.mean(z, axis=-1, keepdims=True)
            zc = z - mu
            rstd = lax.rsqrt(jnp.mean(zc * zc, axis=-1, keepdims=True) + LN_EPS)
            xhat = zc * rstd
            gain_v = g_ref[...]
            diff = xhat * gain_v + b_ref[...] - t_ref[rows, :]
            sq = _fold_rows(diff * diff)
            part = sq[:, :128]
            for k in range(1, D_MODEL // 128):
                part = part + sq[:, k * 128:(k + 1) * 128]
            loss_ref[...] += part
            dln = diff * (1.0 / D_MODEL)
            gg_ref[...] += _fold_rows(dln * xhat)
            gb_ref[...] += _fold_rows(dln)
            dxhat = dln * gain_v
            dz = rstd * (dxhat - jnp.mean(dxhat, axis=-1, keepdims=True)
                         - xhat * jnp.mean(dxhat * xhat, axis=-1, keepdims=True))
            dz_ref[rows, :] = dz
            dzb_ref[rows, :] = dz.astype(BF16)

    row = lambda: pl.BlockSpec((tm, D_MODEL), lambda i: (i, 0))
    vec = lambda: pl.BlockSpec((1, D_MODEL), lambda i: (0, 0))
    acc = lambda width: pl.BlockSpec((8, width), lambda i: (0, 0))
    return _pallas(
        body, name="out_proj_loss", grid=(seq // tm,),
        in_specs=[row(), pl.BlockSpec((D_MODEL, D_MODEL), lambda i: (0, 0), pipeline_mode=pl.Buffered(1)),
                  row(), row(), vec(), vec()],
        out_specs=[row(), row(), acc(D_MODEL), acc(D_MODEL), acc(128)],
        out_shape=[jax.ShapeDtypeStruct((seq, D_MODEL), F32), jax.ShapeDtypeStruct((seq, D_MODEL), BF16),
                   jax.ShapeDtypeStruct((8, D_MODEL), F32), jax.ShapeDtypeStruct((8, D_MODEL), F32),
                   jax.ShapeDtypeStruct((8, 128), F32)],
        compiler_params=_params(("arbitrary",), 56),
    )(y, w_out_g.reshape(D_MODEL, D_MODEL), x, target, gain, bias)


def _dy_gate_bwd(dzb, w_out_g, hug, mixpre, pool_scale, lse_all):
    seq = dzb.shape[0]
    tm = 256
    d4, d16 = DILATIONS[1], DILATIONS[2]

    def body(dz_ref, w_ref, ga_ref, gp_ref, mix_ref, sc_ref, lse_ref,
             dh_ref, dpo_ref, do1_ref, do4_ref, do16_ref, st1_ref, st4_ref, st16_ref, da_ref, st_ref):
        dy = _dot_nt(dz_ref[...], w_ref[...])
        ga = ga_ref[...].astype(F32)
        sig = jax.nn.sigmoid(ga)
        attn = mix_ref[:, :D_ATTN].astype(F32)
        dya = dy[:, :D_ATTN]
        dattn = dya * (ga * sig)
        dh_ref[:, :D_ATTN] = (dya * attn * (sig * (1.0 + ga * (1.0 - sig)))).astype(BF16)
        _store_slabs(da_ref, dattn)
        lane = lax.broadcasted_iota(jnp.int32, (tm, STAT_LANES), 1)
        stats = lse_ref[...]
        prod = dattn * attn
        for h in range(N_HEADS):
            delta = jnp.sum(prod[:, h * HEAD_DIM:(h + 1) * HEAD_DIM], axis=-1, keepdims=True)
            stats = jnp.where(lane == N_HEADS + h, delta, stats)
        st_ref[0] = stats
        do1_ref[...] = dattn.astype(BF16)
        st1_ref[...] = stats
        _to_pattern(da_ref, do4_ref, d4, BF16)
        _to_pattern(da_ref, do16_ref, d16, BF16)
        _to_pattern(st_ref, st4_ref, d4, F32)
        _to_pattern(st_ref, st16_ref, d16, F32)

        gp = gp_ref[...].astype(F32)
        sig = jax.nn.sigmoid(gp)
        dyp = dy[:, D_ATTN:]
        dpo_ref[...] = (dyp * (gp * sig)).astype(BF16)
        dh_ref[:, D_ATTN:] = (dyp * (mix_ref[:, D_ATTN:].astype(F32) * sc_ref[...])
                              * (sig * (1.0 + gp * (1.0 - sig)))).astype(BF16)

    row = lambda width, cb=0: pl.BlockSpec((tm, width), lambda i: (i, cb))
    pat = lambda d, width: pl.BlockSpec((d, tm // d, width), lambda i: (0, i, 0))
    pat_shape = lambda d, width, dtype: jax.ShapeDtypeStruct((d, seq // d, width), dtype)
    outs = _pallas(
        body, name="dy_gate_bwd", grid=(seq // tm,),
        in_specs=[row(D_MODEL), pl.BlockSpec((D_MODEL, D_MODEL), lambda i: (0, 0)),
                  row(D_ATTN, 1), row(D_POOL, 2), row(D_MODEL), pl.BlockSpec((1, D_POOL), lambda i: (0, 0)),
                  row(STAT_LANES)],
        out_specs=[row(D_MODEL, D_IN // D_MODEL - 1), row(D_POOL),
                   row(D_ATTN), pat(d4, D_ATTN), pat(d16, D_ATTN),
                   row(STAT_LANES), pat(d4, STAT_LANES), pat(d16, STAT_LANES)],
        out_shape=[jax.ShapeDtypeStruct((seq, D_IN), BF16), jax.ShapeDtypeStruct((seq, D_POOL), BF16),
                   jax.ShapeDtypeStruct((seq, D_ATTN), BF16), pat_shape(d4, D_ATTN, BF16), pat_shape(d16, D_ATTN, BF16),
                   jax.ShapeDtypeStruct((seq, STAT_LANES), F32), pat_shape(d4, STAT_LANES, F32),
                   pat_shape(d16, STAT_LANES, F32)],
        scratch_shapes=[pltpu.VMEM((N_HEADS, tm, HEAD_DIM), F32), pltpu.VMEM((1, tm, STAT_LANES), F32)],
        compiler_params=_params(("parallel",), 48),
    )(dzb, w_out_g.reshape(D_MODEL, D_MODEL), hug, hug, mixpre, pool_scale, lse_all)
    dh, dpo, do1, do4, do16, st1, st4, st16 = outs
    return dh, dpo, [do1[None], do4, do16], [st1[None], st4, st16]


def _pool_bwd(dh, dpo, mixpre, pooled, w_pool_g, pool_scale):
    seq = dpo.shape[0]
    tm = 256
    halo_blocks = tm // POOL_HALO
    last = seq // tm - 1
    n_groups = len(POOL_WINDOWS)

    def body(dh_in_ref, dpo_ref, halo_ref, pre_ref, pooled_ref, wp_ref, sc_ref, du_ref, gw_ref, gs_ref):
        i = pl.program_id(0)

        @pl.when(i == 0)
        def _():
            gw_ref[...] = jnp.zeros_like(gw_ref)
            gs_ref[...] = jnp.zeros_like(gs_ref)

        dpo = dpo_ref[...].astype(F32)
        scale = sc_ref[...]
        gs_ref[...] += _fold_rows(dpo * pre_ref[...].astype(F32))
        halo = jnp.where(i < last, halo_ref[...].astype(F32), 0.0)
        dpw = (jnp.concatenate([dpo, halo], axis=0) * scale).astype(BF16)
        pos = i * tm + lax.broadcasted_iota(jnp.int32, (tm + POOL_HALO, 1), 0)
        for g, window in enumerate(POOL_WINDOWS):
            cols = slice(g * POOL_GROUP_DIM, (g + 1) * POOL_GROUP_DIM)
            dpw_g = dpw[:, cols]
            gw_ref[g] += _dot_tn(pooled_ref[:, cols], dpw_g[:tm, :])
            dpooled = _dot_nt(dpw_g, wp_ref[g])
            count = jnp.minimum(pos + 1, window).astype(F32)
            sums = _window_sums(dpooled / count, window, backward=True)
            du_ref[:, cols] = (sums[:tm, :] - dpooled[:tm, :]).astype(BF16)

    row = lambda width, cb=0: pl.BlockSpec((tm, width), lambda i: (i, cb))
    return _pallas(
        body, name="pool_bwd", grid=(seq // tm,),
        in_specs=[ANY, row(D_POOL),
                  pl.BlockSpec((POOL_HALO, D_POOL),
                               lambda i: (jnp.minimum((i + 1) * halo_blocks, seq // POOL_HALO - 1), 0)),
                  row(D_POOL, 1), row(D_POOL),
                  pl.BlockSpec((n_groups, POOL_GROUP_DIM, POOL_GROUP_DIM), lambda i: (0, 0, 0)),
                  pl.BlockSpec((1, D_POOL), lambda i: (0, 0))],
        out_specs=[row(D_POOL, D_QKV // D_POOL),
                   pl.BlockSpec((n_groups, POOL_GROUP_DIM, POOL_GROUP_DIM), lambda i: (0, 0, 0)),
                   pl.BlockSpec((8, D_POOL), lambda i: (0, 0))],
        out_shape=[jax.ShapeDtypeStruct(dh.shape, dh.dtype),
                   jax.ShapeDtypeStruct((n_groups, POOL_GROUP_DIM, POOL_GROUP_DIM), F32),
                   jax.ShapeDtypeStruct((8, D_POOL), F32)],
        input_output_aliases={0: 0},
        compiler_params=_params(("arbitrary",), 40),
    )(dh, dpo, dpo, mixpre, pooled, w_pool_g, pool_scale)


def _sum_patterns(dh, parts, tabs, unrotate, col_block, name, comm=None):
    seq = dh.shape[0]
    tm, tn = 256, D_ATTN
    per = D_ATTN // tn
    d4, d16 = DILATIONS[1], DILATIONS[2]

    def body(dh_in_ref, a1_ref, a4_ref, a16_ref, ct_ref, up_ref, down_ref, o_ref, n4_ref, n16_ref):
        _from_pattern(a4_ref, n4_ref, d4)
        _from_pattern(a16_ref, n16_ref, d16)
        for s in range(tn // HEAD_DIM):
            cols = slice(s * HEAD_DIM, (s + 1) * HEAD_DIM)
            tot = a1_ref[:, cols].astype(F32) + n4_ref[s] + n16_ref[s]
            if unrotate:
                tot = _rotate_heads(tot, ct_ref[...], -up_ref[...], -down_ref[...])
            o_ref[:, cols] = tot.astype(BF16)

    tab = pl.BlockSpec((tm, HEAD_DIM), lambda i, j: (i, 0))
    pat = lambda d: pl.BlockSpec((d, tm // d, tn), lambda i, j: (0, i, j))
    (dh,), exchanged = _call(
        body, name=name, grid=(seq // tm, per),
        in_specs=[ANY, pl.BlockSpec((tm, tn), lambda i, j: (i, j)), pat(d4), pat(d16), tab, tab, tab],
        out_specs=[pl.BlockSpec((tm, tn), lambda i, j: (i, col_block * per + j))],
        out_shape=[jax.ShapeDtypeStruct(dh.shape, dh.dtype)],
        scratch_shapes=[pltpu.VMEM((tn // HEAD_DIM, tm, HEAD_DIM), F32), pltpu.VMEM((tn // HEAD_DIM, tm, HEAD_DIM), F32)],
        semantics=("parallel", "parallel"), vmem_mib=32, args=(dh, parts[0][0], parts[1], parts[2], *tabs),
        aliases={0: 0}, comm=comm)
    return dh, exchanged


def _grad_w_in(x, dh, half, name, comm=None):
    seq = x.shape[0]
    ts, td, te = 2048, D_MODEL // 2, SHARD_IN

    def body(half_ref, x_ref, dh_ref, o_ref):
        k = pl.program_id(1)
        part = _dot_tn(x_ref[...].astype(BF16), dh_ref[...])

        @pl.when(k == 0)
        def _():
            o_ref[...] = part

        @pl.when(k > 0)
        def _():
            o_ref[...] += part

    (g,), exchanged = _call(
        body, name=name, grid=(N_SHARDS, seq // ts),
        in_specs=[pl.BlockSpec((ts, td), lambda e, k, half_ref: (k, half_ref[0])),
                  pl.BlockSpec((ts, te), lambda e, k, half_ref: (k, e))],
        out_specs=[pl.BlockSpec((None, td, te), lambda e, k, half_ref: (e, 0, 0))],
        out_shape=[jax.ShapeDtypeStruct((N_SHARDS, td, te), F32)],
        scratch_shapes=[], semantics=("parallel", "arbitrary"), vmem_mib=56, args=(x, dh), comm=comm,
        prefetch=(half,))
    return g, exchanged


def _grad_w_out(y, dzb):
    seq = y.shape[0]
    ts, te = 2048, 1024

    def body(y_ref, dz_ref, o_ref):
        k = pl.program_id(1)
        part = _dot_tn(y_ref[...], dz_ref[...])

        @pl.when(k == 0)
        def _():
            o_ref[...] = part

        @pl.when(k > 0)
        def _():
            o_ref[...] += part

    return _pallas(
        body, name="grad_w_out", grid=(D_MODEL // te, seq // ts),
        in_specs=[pl.BlockSpec((ts, te), lambda e, k: (k, e)), pl.BlockSpec((ts, D_MODEL), lambda e, k: (k, 0))],
        out_specs=pl.BlockSpec((te, D_MODEL), lambda e, k: (e, 0)),
        out_shape=jax.ShapeDtypeStruct((D_MODEL, D_MODEL), F32),
        compiler_params=_params(("parallel", "arbitrary"), 56),
    )(y, dzb)


GRAD_X_LATE_SHARDS = 1
GRAD_X_PARTIAL_ROWS = 512


def _grad_x_partial(dh, w_in_g, dz, first, tiles, prev=None, comm=None):
    seq = dh.shape[0]
    tm, tk = GRAD_X_PARTIAL_ROWS, SHARD_IN

    def body(*refs):
        dh_ref, w_ref, dz_ref, o_ref = refs[-4:]
        k = pl.program_id(1)
        part = _dot_nt(dh_ref[...], w_ref[...])

        @pl.when(k == 0)
        def _():
            o_ref[...] = DEEPNORM_ALPHA * dz_ref[...] + part

        @pl.when(k > 0)
        def _():
            o_ref[...] += part

    carried = [] if prev is None else [prev]
    row = pl.BlockSpec((tm, D_MODEL), lambda i, k: (i + first, 0))
    (partial,), exchanged = _call(
        body, name="grad_x_partial_%d" % first, grid=(tiles, N_SHARDS - GRAD_X_LATE_SHARDS),
        in_specs=[ANY] * len(carried) + [
            pl.BlockSpec((tm, tk), lambda i, k: (i + first, k)),
            pl.BlockSpec((None, D_MODEL, tk), lambda i, k: (k, 0, 0)), row],
        out_specs=[row],
        out_shape=[jax.ShapeDtypeStruct((seq, D_MODEL), F32)],
        scratch_shapes=[], semantics=("parallel", "arbitrary"), vmem_mib=48, args=(*carried, dh, w_in_g, dz),
        aliases={0: 0} if carried else None, comm=comm)
    return partial, exchanged


def _grad_x_final(dh, w_in_g, partial):
    seq = dh.shape[0]
    tm, tk = 512, SHARD_IN
    k0 = N_SHARDS - GRAD_X_LATE_SHARDS

    def body(dh_ref, w_ref, p_ref, o_ref):
        k = pl.program_id(1)
        part = _dot_nt(dh_ref[...], w_ref[...])

        @pl.when(k == 0)
        def _():
            o_ref[...] = p_ref[...] + part

        @pl.when(k > 0)
        def _():
            o_ref[...] += part

    row = pl.BlockSpec((tm, D_MODEL), lambda i, k: (i, 0))
    return _pallas(
        body, name="grad_x_final", grid=(seq // tm, GRAD_X_LATE_SHARDS),
        in_specs=[pl.BlockSpec((tm, tk), lambda i, k: (i, k + k0)),
                  pl.BlockSpec((None, D_MODEL, tk), lambda i, k: (k + k0, 0, 0)), row],
        out_specs=row, out_shape=jax.ShapeDtypeStruct((seq, D_MODEL), F32),
        compiler_params=_params(("parallel", "arbitrary"), 48),
    )(dh, w_in_g, partial)


def _pool_weight(w_pool_sh):
    n_groups = len(POOL_WINDOWS)
    shard_c = POOL_GROUP_DIM // N_SHARDS
    return (w_pool_sh.reshape(N_SHARDS, n_groups, shard_c, POOL_GROUP_DIM).transpose(1, 0, 2, 3)
            .reshape(n_groups, POOL_GROUP_DIM, POOL_GROUP_DIM))


def _pool_grad_pieces(g_w_pool):
    n_groups = len(POOL_WINDOWS)
    half_c = POOL_GROUP_DIM // N_SHARDS // 2
    return (g_w_pool.reshape(n_groups, N_SHARDS, 2, half_c, POOL_GROUP_DIM).transpose(1, 2, 0, 3, 4)
            .reshape(N_SHARDS, 2, n_groups * half_c, POOL_GROUP_DIM))


def _step(x, target, w_bufs, pool_scale, gain, bias, place):
    seq = x.shape[0]
    tabs = _rope_tables(seq)
    core, chip_core, onward, plan = place
    qkv, hug, w_in_g, w_out_g, w_pool_sh = _in_proj_gathering(x, w_bufs, tabs, plan)
    o_list, st_list = [], []
    for p, dil in enumerate(DILATIONS):
        o, st = _attn_fwd(qkv[p], "attn_fwd_d%d" % dil)
        o_list.append(o)
        st_list.append(st)
    w_pool_g = _pool_weight(w_pool_sh)
    y, mixpre, lse_all, pooled = _mix_gate(o_list, st_list, hug, w_pool_g, pool_scale)
    dz, dzb, gain_part, bias_part, loss_part = _out_proj_loss(y, w_out_g, x, target, gain, bias)
    dh, dpo, do_list, stat_list = _dy_gate_bwd(dzb, w_out_g, hug, mixpre, pool_scale, lse_all)
    g_w_out = _grad_w_out(y, dzb)
    dh, g_w_pool, scale_part = _pool_bwd(dh, dpo, mixpre, pooled, w_pool_g, pool_scale)
    small = jnp.concatenate([scale_part, gain_part, bias_part, loss_part], axis=1)
    early = [g_w_out.reshape(N_SHARDS, 2, D_MODEL // (2 * N_SHARDS), D_MODEL), _pool_grad_pieces(g_w_pool)]

    bwd = lambda p, comm: _attn_bwd(qkv[p], do_list[p], stat_list[p], "attn_bwd_d%d" % DILATIONS[p], comm)
    part_a, recv = bwd(0, _exchange_halves(early))
    sums = [_add_own_half(g, r, core, "add_own_half_%d" % a) for a, (g, r) in enumerate(zip(early, recv))]
    part_b, recv = bwd(1, _scatter_to_chips([s[1] for s in sums]))
    bufs = [_add_chips([s[0]], r, chip_core, "add_chips_%d" % a) for a, (s, r) in enumerate(zip(sums, recv))]
    part_c, reduced = bwd(2, _share_with_sibling(bufs))
    parts = [part_a, part_b, part_c]
    dh, gathered = _sum_patterns(dh, [t[0] for t in parts], tabs, True, 0, "sum_dq", _gather_small(small))
    dh, _ = _sum_patterns(dh, [t[1] for t in parts], tabs, True, 1, "sum_dk")
    dh, _ = _sum_patterns(dh, [t[2] for t in parts], tabs, False, 2, "sum_dv")

    give, _ = _grad_w_in(x, dh, 1 - core, "grad_w_in_give")
    keep, recv = _grad_w_in(x, dh, core, "grad_w_in_keep", _send_to_sibling([give]))
    total = [keep, recv[0]]
    total_b = _add_pair(keep, recv[0], "add_own_half_w_in")
    n_tiles = seq // GRAD_X_PARTIAL_ROWS
    tiles = 3 * n_tiles // 8
    part, relayed = _grad_x_partial(dh, w_in_g, dz, 0, tiles, None, _relay_diagonal(total_b))
    total_b = _fold_relayed(total, total_b, relayed[0], onward)
    part, recv = _grad_x_partial(dh, w_in_g, dz, tiles, n_tiles - tiles, part, _scatter_to_neighbours(total_b))
    buf = _add_chips(total, recv[0], chip_core, "add_chips_w_in")
    g_x = _grad_x_final(dh, w_in_g, part)
    g_w_in = _run_exchange(_share_with_sibling([buf]), "share_w_in")[0]
    return g_x, g_w_in, reduced[0], reduced[1], small, gathered[0]


def _exchange_halves(grads):
    n = len(grads)

    def copies(src, dst, sems):
        x, y, c, _ = _mesh_place()
        return [_remote(src[a].at[j, 1 - c], dst[a].at[j], sems[0].at[a, j], sems[1].at[a, j], (x, y, 1 - c))
                for a in range(n) for j in range(N_SHARDS)]

    def start(src, dst, sems):
        for cp in copies(src, dst, sems):
            cp.start()

    def finish(src, dst, sems):
        for cp in copies(src, dst, sems):
            cp.wait()

    return _Exchange(grads, [jax.ShapeDtypeStruct((N_SHARDS,) + g.shape[2:], g.dtype) for g in grads], {},
                     [pltpu.SemaphoreType.DMA((n, N_SHARDS))] * 2, start, finish)


def _add_own_half(grad, recv, core, name):
    _, _, r, c = grad.shape
    tr = min(r, 256)

    def body(core_ref, g_ref, r_ref, o_ref, ob_ref):
        tot = g_ref[...] + r_ref[...]
        o_ref[...] = tot
        ob_ref[...] = tot.astype(BF16)

    out = pl.BlockSpec((None, tr, c), lambda j, i, core_ref: (j, i, 0))
    return _pallas(
        body, name=name,
        grid_spec=pltpu.PrefetchScalarGridSpec(
            num_scalar_prefetch=1, grid=(N_SHARDS, r // tr),
            in_specs=[pl.BlockSpec((None, None, tr, c), lambda j, i, core_ref: (j, core_ref[0], i, 0)),
                      pl.BlockSpec((None, tr, c), lambda j, i, core_ref: (j, i, 0))],
            out_specs=[out, out]),
        out_shape=[jax.ShapeDtypeStruct((N_SHARDS, r, c), F32), jax.ShapeDtypeStruct((N_SHARDS, r, c), BF16)],
        compiler_params=_params(("parallel", "parallel"), 32),
    )(core, grad, recv)


def _send_to_sibling(arrays):
    n = len(arrays)

    def copies(src, dst, sems):
        x, y, c, _ = _mesh_place()
        return [_remote(src[a], dst[a], sems[0].at[a], sems[1].at[a], (x, y, 1 - c)) for a in range(n)]

    def start(src, dst, sems):
        for cp in copies(src, dst, sems):
            cp.start()

    def finish(src, dst, sems):
        for cp in copies(src, dst, sems):
            cp.wait()

    return _Exchange(arrays, [jax.ShapeDtypeStruct(t.shape, t.dtype) for t in arrays], {},
                     [pltpu.SemaphoreType.DMA((n,))] * 2, start, finish)


def _add_pair(a, b, name):
    _, r, c = a.shape
    tr = min(r, 256)

    def body(a_ref, b_ref, ob_ref):
        ob_ref[...] = (a_ref[...] + b_ref[...]).astype(BF16)

    spec = pl.BlockSpec((None, tr, c), lambda j, i: (j, i, 0))
    return _pallas(
        body, name=name, grid=(N_SHARDS, r // tr), in_specs=[spec, spec], out_specs=spec,
        out_shape=jax.ShapeDtypeStruct(a.shape, BF16),
        compiler_params=_params(("parallel", "parallel"), 32),
    )(a, b)


def _scatter_to_chips(sums):
    n = len(sums)

    def copies(src, dst, sems):
        x, y, c, chips = _mesh_place()
        return [_remote(src[a].at[2 * cx + cy], dst[a].at[k], sems[0].at[a, k], sems[1].at[a, k], (cx, cy, c))
                for a in range(n) for k, (cx, cy) in enumerate(chips)]

    def start(src, dst, sems):
        for cp in copies(src, dst, sems):
            cp.start()

    def finish(src, dst, sems):
        for cp in copies(src, dst, sems):
            cp.wait()

    return _Exchange(sums, [jax.ShapeDtypeStruct((3,) + s.shape[1:], s.dtype) for s in sums], {},
                     [pltpu.SemaphoreType.DMA((n, 3))] * 2, start, finish)


def _add_chips(sums, recv, chip_core, name):
    _, r, c = sums[0].shape
    n_sums, n_recv = len(sums), recv.shape[0]
    tr = min(r, 256)

    def body(cc_ref, *refs):
        r_ref, o_ref = refs[n_sums:]
        tot = refs[0][...]
        for s_ref in refs[1:n_sums]:
            tot = tot + s_ref[...]
        for k in range(n_recv):
            tot = tot + r_ref[k].astype(F32)
        o_ref[...] = tot

    return _pallas(
        body, name=name,
        grid_spec=pltpu.PrefetchScalarGridSpec(
            num_scalar_prefetch=1, grid=(r // tr,),
            in_specs=[pl.BlockSpec((None, tr, c), lambda i, cc_ref: (cc_ref[0], i, 0))] * n_sums
            + [pl.BlockSpec((n_recv, tr, c), lambda i, cc_ref: (0, i, 0))],
            out_specs=pl.BlockSpec((None, tr, c), lambda i, cc_ref: (cc_ref[1], i, 0))),
        out_shape=jax.ShapeDtypeStruct((2, r, c), F32),
        compiler_params=_params(("parallel",), 32),
    )(chip_core, *sums, recv)


def _relay_diagonal(sums_b):
    def copy(src, dst, sems):
        x, y, c, _ = _mesh_place()
        diagonal = 2 * (1 - x) + (1 - y)
        return _remote(src[0].at[diagonal], dst[0], sems[0].at[0], sems[1].at[0], (x ^ (1 - c), y ^ c, c))

    def start(src, dst, sems):
        copy(src, dst, sems).start()

    def finish(src, dst, sems):
        copy(src, dst, sems).wait()

    return _Exchange([sums_b], [jax.ShapeDtypeStruct(sums_b.shape[1:], sums_b.dtype)], {},
                     [pltpu.SemaphoreType.DMA((1,))] * 2, start, finish)


def _fold_relayed(sums, sums_b, relayed, onward):
    _, r, c = sums[0].shape
    n_sums = len(sums)
    tr = min(r, 256)

    def body(on_ref, b_in_ref, *refs):
        r_ref, o_ref = refs[n_sums:]
        tot = refs[0][...]
        for s_ref in refs[1:n_sums]:
            tot = tot + s_ref[...]
        o_ref[...] = (tot + r_ref[...].astype(F32)).astype(BF16)

    return _pallas(
        body, name="fold_relayed",
        grid_spec=pltpu.PrefetchScalarGridSpec(
            num_scalar_prefetch=1, grid=(r // tr,),
            in_specs=[ANY] + [pl.BlockSpec((None, tr, c), lambda i, on_ref: (on_ref[0], i, 0))] * n_sums
            + [pl.BlockSpec((tr, c), lambda i, on_ref: (i, 0))],
            out_specs=pl.BlockSpec((None, tr, c), lambda i, on_ref: (on_ref[0], i, 0))),
        out_shape=jax.ShapeDtypeStruct(sums_b.shape, sums_b.dtype),
        input_output_aliases={1: 0},
        compiler_params=_params(("parallel",), 32),
    )(onward, sums_b, *sums, relayed)


def _scatter_to_neighbours(sums_b):
    def copies(src, dst, sems):
        x, y, c, chips = _mesh_place()
        return [_remote(src[0].at[2 * cx + cy], dst[0].at[k], sems[0].at[k], sems[1].at[k], (cx, cy, c))
                for k, (cx, cy) in enumerate(chips[:2])]

    def start(src, dst, sems):
        for cp in copies(src, dst, sems):
            cp.start()

    def finish(src, dst, sems):
        for cp in copies(src, dst, sems):
            cp.wait()

    return _Exchange([sums_b], [jax.ShapeDtypeStruct((2,) + sums_b.shape[1:], sums_b.dtype)], {},
                     [pltpu.SemaphoreType.DMA((2,))] * 2, start, finish)


def _share_with_sibling(bufs):
    n = len(bufs)

    def copies(dst, sems, half):
        x, y, c, _ = _mesh_place()
        h = c if half == "mine" else 1 - c
        return [_remote(dst[a].at[h], dst[a].at[h], sems[0].at[a], sems[1].at[a], (x, y, 1 - c)) for a in range(n)]

    def start(ins, dst, sems):
        for cp in copies(dst, sems, "mine"):
            cp.start()

    def finish(ins, dst, sems):
        for cp in copies(dst, sems, "theirs"):
            cp.wait_recv()
        for cp in copies(dst, sems, "mine"):
            cp.wait_send()

    return _Exchange(bufs, [jax.ShapeDtypeStruct(b.shape, b.dtype) for b in bufs], {a: a for a in range(n)},
                     [pltpu.SemaphoreType.DMA((n,))] * 2, start, finish)


def _adam_math(w, g, m, v):
    m = ADAM_B1 * m + (1.0 - ADAM_B1) * g
    v = ADAM_B2 * v + (1.0 - ADAM_B2) * (g * g)
    m_hat = m / (1.0 - ADAM_B1 ** ADAM_STEP)
    v_hat = v / (1.0 - ADAM_B2 ** ADAM_STEP)
    delta = -ADAM_LR * (m_hat / (jnp.sqrt(v_hat) + ADAM_EPS) + ADAM_WD * w)
    return delta, m, v


def _gather_small(small):
    def peers():
        x, y, c, _ = _mesh_place()
        return [(x ^ ((r >> 2) & 1), y ^ ((r >> 1) & 1), c ^ (r & 1)) for r in range(1, 8)], 4 * x + 2 * y + c

    def start(src, dst, sems):
        to, me = peers()
        for r, peer in enumerate(to):
            _remote(src[0], dst[0].at[me], sems[0].at[r], sems[1].at[r], peer).start()

    def finish(src, dst, sems):
        to, me = peers()
        for r, (px, py, pc) in enumerate(to):
            theirs = dst[0].at[4 * px + 2 * py + pc]
            _remote(theirs, theirs, sems[0].at[r], sems[1].at[r], (px, py, pc)).wait_recv()
        for r, peer in enumerate(to):
            _remote(src[0], dst[0].at[me], sems[0].at[r], sems[1].at[r], peer).wait_send()

    return _Exchange([small], [jax.ShapeDtypeStruct((8,) + small.shape, small.dtype)], {},
                     [pltpu.SemaphoreType.DMA((7,))] * 2, start, finish)


def _small_adamw(gathered, small, me, w_vecs, m_vecs, v_vecs):
    n = len(w_vecs)
    widths = [w.shape[1] for w in w_vecs]
    n_par = sum(widths)

    def body(me_ref, a_ref, s_ref, *refs):
        w_refs, m_refs, v_refs = refs[:n], refs[n:2 * n], refs[2 * n:3 * n]
        loss_ref, outs = refs[3 * n], refs[3 * n + 1:]
        mine = s_ref[...]
        tot = jnp.where(me_ref[0] == 0, mine, a_ref[0])
        for d in range(1, 8):
            tot = tot + jnp.where(me_ref[0] == d, mine, a_ref[d])
        tot = jnp.sum(tot, axis=0, keepdims=True)
        sq = jnp.sum(tot[:, n_par:], axis=1, keepdims=True)
        loss_ref[...] = jnp.broadcast_to(sq * (0.5 / D_MODEL), loss_ref.shape)
        lo = 0
        for k in range(n):
            g = tot[:, lo:lo + widths[k]]
            lo += widths[k]
            outs[k][...] = g
            outs[n + k][...], outs[2 * n + k][...], outs[3 * n + k][...] = _adam_math(
                w_refs[k][...], g, m_refs[k][...], v_refs[k][...])

    vm = pl.BlockSpec(memory_space=pltpu.VMEM)
    vecs = [jax.ShapeDtypeStruct((1, w), F32) for w in widths] * 4
    res = pl.pallas_call(
        body, name="small_adamw",
        grid_spec=pltpu.PrefetchScalarGridSpec(num_scalar_prefetch=1, grid=(), in_specs=[vm] * (2 + 3 * n),
                                               out_specs=[vm] * (1 + 4 * n)),
        out_shape=[jax.ShapeDtypeStruct((1, 128), F32)] + vecs,
    )(me, gathered, small, *w_vecs, *m_vecs, *v_vecs)
    return res[0], res[1:1 + n], res[1 + n:1 + 2 * n], res[1 + 2 * n:1 + 3 * n], res[1 + 3 * n:]


def _adamw(w, g, m, v, name):
    r, c = w.shape
    tr = min(r, 256)

    def body(w_ref, g_ref, m_ref, v_ref, go_ref, d_ref, nm_ref, nv_ref):
        g = g_ref[...]
        go_ref[...] = g
        d_ref[...], nm_ref[...], nv_ref[...] = _adam_math(w_ref[...], g, m_ref[...], v_ref[...])

    spec = pl.BlockSpec((tr, c), lambda i: (i, 0))
    shape = jax.ShapeDtypeStruct((r, c), F32)
    return _pallas(
        body, name=name, grid=(r // tr,),
        in_specs=[spec] * 4, out_specs=[spec] * 4, out_shape=[shape] * 4,
        compiler_params=_params(("parallel",), 48),
    )(w, g, m, v)


def kernel(x, w_in, w_pool, pool_scale, w_out, ln_gain, ln_bias, loss_target, m_w_in, m_w_pool, m_pool_scale, m_w_out, m_ln_gain, m_ln_bias, v_w_in, v_w_pool, v_pool_scale, v_w_out, v_ln_gain, v_ln_bias):
    xi, yi, ci = lax.axis_index("x"), lax.axis_index("y"), lax.axis_index("c")
    chip = (2 * xi + yi).astype(jnp.int32).reshape(1)
    core = ci.astype(jnp.int32).reshape(1)
    n_groups = len(POOL_WINDOWS)
    shard_c = w_pool.shape[2]

    w_in_b = _cast_bf16(w_in[0], chip, "cast_w_in", 256)
    w_out_b = _cast_bf16(w_out[0], chip, "cast_w_out", 256)
    w_pool_b = _cast_bf16(w_pool[0].reshape(n_groups * shard_c, POOL_GROUP_DIM), chip, "cast_w_pool", 256)

    chip_core = jnp.concatenate([chip, core])
    onward = (2 * (xi ^ ci) + (yi ^ (1 - ci))).astype(jnp.int32).reshape(1)
    g_x, full_in, full_out, full_pool, small, small_all = _step(
        x[0], loss_target[0], [w_in_b, w_out_b, w_pool_b], pool_scale, ln_gain, ln_bias,
        (core, chip_core, onward, _in_proj_plan(xi, yi)))
    half_c = shard_c // 2
    grad_w_in = full_in.reshape(D_MODEL, SHARD_IN)
    grad_w_out = full_out.reshape(D_MODEL // N_SHARDS, D_MODEL)
    grad_w_pool = (full_pool.reshape(2, n_groups, half_c, POOL_GROUP_DIM).transpose(1, 0, 2, 3)
                   .reshape(n_groups * shard_c, POOL_GROUP_DIM))

    grad_w_in, d_in, nm_in, nv_in = _adamw(w_in[0], grad_w_in, m_w_in[0], v_w_in[0], "adamw_w_in")
    grad_w_out, d_out, nm_out, nv_out = _adamw(w_out[0], grad_w_out, m_w_out[0], v_w_out[0], "adamw_w_out")
    flat = lambda t: t[0].reshape(n_groups * shard_c, POOL_GROUP_DIM)
    grad_w_pool, d_pool, nm_pool, nv_pool = _adamw(flat(w_pool), grad_w_pool, flat(m_w_pool), flat(v_w_pool),
                                                   "adamw_w_pool")

    me = (4 * xi + 2 * yi + ci).astype(jnp.int32).reshape(1)
    loss_v, g_vecs, d_vecs, nm_vecs, nv_vecs = _small_adamw(
        small_all, small, me, [pool_scale, ln_gain, ln_bias], [m_pool_scale, m_ln_gain, m_ln_bias],
        [v_pool_scale, v_ln_gain, v_ln_bias])
    g_scale, g_gain, g_bias = g_vecs
    d_scale, d_gain, d_bias = d_vecs
    nm_scale, nm_gain, nm_bias = nm_vecs
    nv_scale, nv_gain, nv_bias = nv_vecs
    pool_shape = w_pool.shape
    return (loss_v[0, 0], g_x[None],
            grad_w_in[None], grad_w_pool.reshape(pool_shape), g_scale, grad_w_out[None], g_gain, g_bias,
            d_in[None], d_pool.reshape(pool_shape), d_scale, d_out[None], d_gain, d_bias,
            nm_in[None], nm_pool.reshape(pool_shape), nm_scale, nm_out[None], nm_gain, nm_bias,
            nv_in[None], nv_pool.reshape(pool_shape), nv_scale, nv_out[None], nv_gain, nv_bias)
```

```python
import functools

import jax
import jax.numpy as jnp
import numpy as np
from jax import lax
from jax.experimental import pallas as pl
from jax.experimental.pallas import tpu as pltpu

F32 = jnp.float32
BF16 = jnp.bfloat16
MESH = pl.DeviceIdType.MESH
ANY = pl.BlockSpec(memory_space=pl.ANY)

D_MODEL = 2048
D_ATTN = 1024
D_POOL = 1024
HEAD_DIM = 128
N_HEADS = 8
ROPE_DIM = 32
ROPE_THETA = 500000.0
DILATIONS = (1, 4, 16)
KEY_BLOCK = 128
CHUNK = 2 * KEY_BLOCK
STAT_LANES = 128
POOL_WINDOWS = (2, 4, 8, 16)
POOL_GROUP_DIM = 256
POOL_HALO = 16
D_QKV = 3 * D_ATTN
D_UG = D_POOL + D_MODEL
D_IN = D_QKV + D_UG
N_SHARDS = 4
SHARD_IN = D_IN // N_SHARDS
LN_EPS = 1e-5
DEEPNORM_ALPHA = 2.0 ** 0.25
ADAM_LR = 0.001
ADAM_B1 = 0.9
ADAM_B2 = 0.999
ADAM_EPS = 1e-08
ADAM_WD = 0.01
ADAM_STEP = 10
NEG = -1e30
MIB = 1024 * 1024


def _params(sem, vmem_mib):
    return pltpu.CompilerParams(dimension_semantics=sem, vmem_limit_bytes=vmem_mib * MIB)


def _pallas(body, **kwargs):
    pin = lambda s: pltpu.HBM(s.shape, s.dtype) if len(s.shape) >= 2 else s
    out_shape = kwargs.pop("out_shape")
    out_shape = [pin(s) for s in out_shape] if isinstance(out_shape, (list, tuple)) else pin(out_shape)
    call = pl.pallas_call(body, out_shape=out_shape, **kwargs)

    def run(*operands):
        return call(*[pltpu.with_memory_space_constraint(o, pltpu.HBM) if o.ndim >= 2 else o for o in operands])

    return run


class _Exchange:
    def __init__(self, operands, out_shape, aliases, sems, start, finish):
        self.operands, self.out_shape, self.aliases, self.sems = list(operands), list(out_shape), dict(aliases), list(sems)
        self.start, self.finish = start, finish


def _run_exchange(comm, name):
    n_in, n_out = len(comm.operands), len(comm.out_shape)

    def body(*refs):
        ins, outs, sems = refs[:n_in], refs[n_in:n_in + n_out], refs[n_in + n_out:]
        comm.start(ins, outs, sems)
        comm.finish(ins, outs, sems)

    return _pallas(
        body, name=name, in_specs=[ANY] * n_in, out_specs=[ANY] * n_out, out_shape=comm.out_shape,
        input_output_aliases=comm.aliases, scratch_shapes=comm.sems,
    )(*comm.operands)


def _call(body, *, name, grid, in_specs, out_specs, out_shape, scratch_shapes, semantics, vmem_mib, args,
          aliases=None, comm=None, prefetch=()):
    aliases = dict(aliases or {})
    n_pre, n_in, n_out, n_scr = len(prefetch), len(in_specs), len(out_specs), len(scratch_shapes)
    c_in, c_out = (len(comm.operands), len(comm.out_shape)) if comm else (0, 0)
    c_shapes, c_sems, c_operands = (comm.out_shape, comm.sems, comm.operands) if comm else ([], [], [])

    def hosted(*refs):
        pre, refs = refs[:n_pre], refs[n_pre:]
        a = n_in
        b = a + c_in
        c = b + n_out
        d = c + c_out
        e = d + n_scr
        if comm is None:
            body(*pre, *refs)
            return
        ids = [pl.program_id(k) for k in range(len(grid))]
        first = functools.reduce(jnp.logical_and, [i == 0 for i in ids])
        last = functools.reduce(jnp.logical_and, [i == g - 1 for i, g in zip(ids, grid)])

        @pl.when(first)
        def _():
            comm.start(refs[a:b], refs[c:d], refs[e:])

        body(*pre, *refs[:a], *refs[b:c], *refs[d:e])

        @pl.when(last)
        def _():
            comm.finish(refs[a:b], refs[c:d], refs[e:])

    if comm:
        semantics = ("arbitrary",) * len(grid)
        for i, o in comm.aliases.items():
            aliases[n_pre + n_in + i] = n_out + o
    outs = _pallas(
        hosted, name=name,
        grid_spec=pltpu.PrefetchScalarGridSpec(
            num_scalar_prefetch=n_pre, grid=grid, in_specs=list(in_specs) + [ANY] * c_in,
            out_specs=list(out_specs) + [ANY] * c_out, scratch_shapes=list(scratch_shapes) + c_sems),
        out_shape=list(out_shape) + c_shapes, input_output_aliases=aliases,
        compiler_params=_params(semantics, vmem_mib),
    )(*prefetch, *args, *c_operands)
    return list(outs[:n_out]), list(outs[n_out:])


def _dot_nn(a, b):
    return jnp.dot(a, b, preferred_element_type=F32)


def _dot_nt(a, b):
    return lax.dot_general(a, b, (((1,), (1,)), ((), ())), preferred_element_type=F32)


def _dot_tn(a, b):
    return lax.dot_general(a, b, (((0,), (0,)), ((), ())), preferred_element_type=F32)


def _fold_rows(a):
    r, c = a.shape
    return jnp.sum(a.reshape(r // 8, 8, c), axis=0)


def _cast_bf16(a, chip, name, rows):
    r, c = a.shape

    def body(chip_ref, a_ref, o_ref):
        o_ref[...] = a_ref[...].astype(BF16)

    return _pallas(
        body, name=name,
        grid_spec=pltpu.PrefetchScalarGridSpec(
            num_scalar_prefetch=1, grid=(r // rows,),
            in_specs=[pl.BlockSpec((rows, c), lambda i, chip_ref: (i, 0))],
            out_specs=pl.BlockSpec((None, rows, c), lambda i, chip_ref: (chip_ref[0], i, 0))),
        out_shape=jax.ShapeDtypeStruct((N_SHARDS, r, c), BF16),
        compiler_params=_params(("parallel",), 32),
    )(chip, a)


def _mesh_place():
    x, y, c = lax.axis_index("x"), lax.axis_index("y"), lax.axis_index("c")
    return x, y, c, [(1 - x, y), (x, 1 - y), (1 - x, 1 - y)]


def _remote(src, dst, send_sem, recv_sem, to):
    return pltpu.make_async_remote_copy(src_ref=src, dst_ref=dst, send_sem=send_sem, recv_sem=recv_sem,
                                        device_id=to, device_id_type=MESH)


def _rope_tables(seq):
    half = ROPE_DIM // 2
    inv_freq = (np.float64(ROPE_THETA) ** (-(2.0 * np.arange(half, dtype=np.float64)) / ROPE_DIM)).astype(np.float32)
    ang = np.arange(seq, dtype=np.float32)[:, None] * inv_freq[None, :]
    cos = np.cos(ang.astype(np.float64)).astype(np.float32)
    sin = np.sin(ang.astype(np.float64)).astype(np.float32)
    pad = np.zeros((seq, HEAD_DIM - ROPE_DIM), np.float32)
    zeros = np.zeros((seq, half), np.float32)
    c_tab = np.concatenate([cos, cos, pad + 1.0], axis=1)
    up_tab = np.concatenate([-sin, zeros, pad], axis=1)
    down_tab = np.concatenate([zeros, sin, pad], axis=1)
    return jnp.asarray(c_tab), jnp.asarray(up_tab), jnp.asarray(down_tab)


def _rotate_heads(t, c_tab, up_tab, down_tab):
    outs = []
    for h in range(t.shape[1] // HEAD_DIM):
        th = t[:, h * HEAD_DIM:(h + 1) * HEAD_DIM]
        up = pltpu.roll(th, HEAD_DIM - ROPE_DIM // 2, axis=1)
        down = pltpu.roll(th, ROPE_DIM // 2, axis=1)
        outs.append(th * c_tab + up * up_tab + down * down_tab)
    return outs[0] if len(outs) == 1 else jnp.concatenate(outs, axis=1)


def _to_pattern(slabs_ref, dst_ref, dil, dtype):
    n_slabs, rows, _ = slabs_ref.shape
    for s in range(n_slabs):
        for r in range(dil):
            dst_ref[r, :, s * 128:(s + 1) * 128] = slabs_ref[s, pl.ds(r, rows // dil, dil), :].astype(dtype)


def _from_pattern(src_ref, slabs_ref, dil):
    n_slabs, rows, _ = slabs_ref.shape
    for s in range(n_slabs):
        for r in range(dil):
            slabs_ref[s, pl.ds(r, rows // dil, dil), :] = src_ref[r, :, s * 128:(s + 1) * 128].astype(F32)


def _store_slabs(slabs_ref, value):
    for s in range(slabs_ref.shape[0]):
        slabs_ref[s] = value[:, s * 128:(s + 1) * 128]


W_IN_CHUNKS = 4


def _in_proj_plan(x, y):
    shards = [2 * x + y, 2 * (1 - x) + y, 2 * x + (1 - y), 2 * (1 - x) + (1 - y)]
    last_row = jnp.int32(-2)

    def table(active, col_of):
        cols, rows = [], []
        first_col = functools.reduce(lambda acc, j: jnp.where(active[j], col_of(shards[j]), acc), reversed(range(4)),
                                     jnp.int32(0))
        held_col, seen = first_col, jnp.bool_(False)
        for j in range(4):
            cols.append(jnp.where(active[j], col_of(shards[j]), held_col))
            rows.append(jnp.where(active[j], -1, jnp.where(seen, last_row, 0)))
            held_col = jnp.where(active[j], col_of(shards[j]), held_col)
            seen = jnp.logical_or(seen, active[j])
        return cols, rows

    q_cols, q_rows = table([s < 2 for s in shards], lambda s: s)
    h_cols, h_rows = table([s >= 2 for s in shards], lambda s: s - 2)
    return jnp.stack([jnp.asarray(v, jnp.int32) for v in shards + q_cols + q_rows + h_cols + h_rows])


def _in_proj_gathering(x, w_bufs, tabs, plan):
    seq = x.shape[0]
    tm, tn = 512, SHARD_IN
    n_tiles = seq // tm
    heads = tn // HEAD_DIM
    k_heads_in_second = 2 * D_ATTN // HEAD_DIM - heads
    d4, d16 = DILATIONS[1], DILATIONS[2]
    DIAGONAL = 2
    chunk = D_MODEL // 2 // W_IN_CHUNKS
    early = [(0, D_MODEL // 2, q * chunk, chunk) for q in range(W_IN_CHUNKS)]
    late = [(a, w_bufs[a].shape[1] // 2, 0, w_bufs[a].shape[1] // 2) for a in (1, 2)]
    pieces = early + late
    early_ids, late_ids = range(len(early)), range(len(early), len(pieces))

    def body(plan_ref, x_ref, w_in_in, w_out_in, w_pool_in, c_ref, up_ref, down_ref,
             o1_ref, o4_ref, o16_ref, hug_ref, w_ref, w_out_ref, w_pool_ref,
             wbuf_ref, res_ref, w_sem, ici_send, ici_recv, d2d_send, d2d_recv):
        j, i = pl.program_id(0), pl.program_id(1)
        mx, my, mc, chips = _mesh_place()
        sibling = (mx, my, 1 - mc)
        gathered = (w_ref, w_out_ref, w_pool_ref)
        chip_of = lambda k: 2 * chips[k][0] + chips[k][1]

        def piece(n, chip, core):
            a, per_core, offset, size = pieces[n]
            return gathered[a].at[chip, pl.ds(core * per_core + offset, size)]

        def to_neighbour(k, n):
            mine = piece(n, 2 * mx + my, mc)
            return _remote(mine, mine, ici_send.at[n, k], ici_recv.at[n, k], (*chips[k], mc))

        def relay(n):
            theirs = piece(n, 2 * (mx ^ (1 - mc)) + (my ^ mc), mc)
            return _remote(theirs, theirs, ici_send.at[n, DIAGONAL], ici_recv.at[n, DIAGONAL], (mx ^ mc, my ^ (1 - mc), mc))

        def arrival(k, n):
            theirs = piece(n, chip_of(k), mc)
            return _remote(theirs, theirs, ici_send.at[n, k], ici_recv.at[n, k], (*chips[k], mc))

        def to_sibling(k, n, core):
            theirs = piece(n, chip_of(k), core)
            return _remote(theirs, theirs, d2d_send.at[n, k], d2d_recv.at[n, k], sibling)

        def take(k, ids):
            for n in ids:
                arrival(k, n).wait_recv()
                to_sibling(k, n, mc).start()

        def taken(k, ids):
            for n in ids:
                to_sibling(k, n, 1 - mc).wait_recv()

        first_tile = i == 0

        @pl.when(jnp.logical_and(j == 0, first_tile))
        def _():
            for n in range(len(pieces)):
                for k in range(DIAGONAL):
                    to_neighbour(k, n).start()

        @pl.when(jnp.logical_and(j == 1, first_tile))
        def _():
            take(0, early_ids)
            taken(0, early_ids)

        @pl.when(jnp.logical_and(j == 2, first_tile))
        def _():
            take(1, early_ids)
            for n in early_ids:
                relay(n).start()
            taken(1, early_ids)
            for k in range(DIAGONAL):
                take(k, late_ids)
            for n in late_ids:
                relay(n).start()
            for k in range(DIAGONAL):
                taken(k, late_ids)

        @pl.when(jnp.logical_and(j == 3, first_tile))
        def _():
            take(DIAGONAL, range(len(pieces)))
            taken(DIAGONAL, range(len(pieces)))

        shard = plan_ref[j]

        @pl.when(first_tile)
        def _():
            cp = pltpu.make_async_copy(w_ref.at[shard], wbuf_ref, w_sem)
            cp.start()
            cp.wait()

        xb = x_ref[...].astype(BF16)
        group = 4 * HEAD_DIM
        accs = [_dot_nn(xb, wbuf_ref[:, g * group:(g + 1) * group]) for g in range(tn // group)]

        def emit_qkv(rotated_heads):
            for h in range(heads):
                lanes = (h * HEAD_DIM) % group
                th = accs[h * HEAD_DIM // group][:, lanes:lanes + HEAD_DIM]
                if h < rotated_heads:
                    th = _rotate_heads(th, c_ref[...], up_ref[...], down_ref[...])
                res_ref[h] = th
                o1_ref[:, h * HEAD_DIM:(h + 1) * HEAD_DIM] = th.astype(BF16)
            _to_pattern(res_ref, o4_ref, d4, BF16)
            _to_pattern(res_ref, o16_ref, d16, BF16)

        @pl.when(shard == 0)
        def _():
            emit_qkv(heads)

        @pl.when(shard == 1)
        def _():
            emit_qkv(k_heads_in_second)

        @pl.when(shard >= 2)
        def _():
            for g, acc in enumerate(accs):
                hug_ref[:, g * group:(g + 1) * group] = acc.astype(BF16)

        @pl.when(jnp.logical_and(j == 3, i == n_tiles - 1))
        def _():
            for n in range(len(pieces)):
                for k in range(DIAGONAL):
                    to_neighbour(k, n).wait_send()
                relay(n).wait_send()
                for k in range(DIAGONAL + 1):
                    to_sibling(k, n, mc).wait_send()

    def held(base, last):
        return lambda j, i, plan_ref: jnp.where(plan_ref[base + j] == -1, i,
                                                jnp.where(plan_ref[base + j] == -2, last, 0))

    q_row, h_row = held(8, n_tiles - 1), held(16, n_tiles - 1)
    tab_spec = pl.BlockSpec((tm, HEAD_DIM), lambda j, i, plan_ref: (i, 0))
    sems = [pltpu.SemaphoreType.DMA((len(pieces), 3))] * 4
    o1, o4, o16, hug, w_in_g, w_out_g, w_pool_g = _pallas(
        body, name="in_proj_gathering",
        grid_spec=pltpu.PrefetchScalarGridSpec(
            num_scalar_prefetch=1, grid=(N_SHARDS, n_tiles),
            in_specs=[pl.BlockSpec((tm, D_MODEL), lambda j, i, plan_ref: (i, 0)), ANY, ANY, ANY,
                      tab_spec, tab_spec, tab_spec],
            out_specs=[pl.BlockSpec((tm, tn), lambda j, i, p: (q_row(j, i, p), p[4 + j])),
                       pl.BlockSpec((d4, tm // d4, tn), lambda j, i, p: (0, q_row(j, i, p), p[4 + j])),
                       pl.BlockSpec((d16, tm // d16, tn), lambda j, i, p: (0, q_row(j, i, p), p[4 + j])),
                       pl.BlockSpec((tm, tn), lambda j, i, p: (h_row(j, i, p), p[12 + j])),
                       ANY, ANY, ANY],
            scratch_shapes=[pltpu.VMEM((D_MODEL, tn), BF16), pltpu.VMEM((heads, tm, HEAD_DIM), F32),
                            pltpu.SemaphoreType.DMA(())] + sems),
        out_shape=[jax.ShapeDtypeStruct((seq, D_QKV), BF16),
                   jax.ShapeDtypeStruct((d4, seq // d4, D_QKV), BF16),
                   jax.ShapeDtypeStruct((d16, seq // d16, D_QKV), BF16),
                   jax.ShapeDtypeStruct((seq, D_UG), BF16)]
        + [jax.ShapeDtypeStruct(b.shape, b.dtype) for b in w_bufs],
        input_output_aliases={2: 4, 3: 5, 4: 6},
        compiler_params=_params(("arbitrary", "arbitrary"), 52),
    )(plan, x, *w_bufs, *tabs)
    return [o1[None], o4, o16], hug, w_in_g, w_out_g, w_pool_g


def _band_masks():
    row = lax.broadcasted_iota(jnp.int32, (KEY_BLOCK, KEY_BLOCK), 0)
    col = lax.broadcasted_iota(jnp.int32, (KEY_BLOCK, KEY_BLOCK), 1)
    return col <= row, col >= row


def _attn_fwd(qkv, name):
    dil, n, _ = qkv.shape
    scale = HEAD_DIM ** -0.5
    lo, hi = slice(0, KEY_BLOCK), slice(KEY_BLOCK, CHUNK)

    def body(q_ref, k_ref, v_ref, kb_ref, vb_ref, o_ref, st_ref):
        i = pl.program_id(1)
        cur_mask, prev_mask = _band_masks()
        before_mask = jnp.logical_and(prev_mask, i > 0)
        lane = lax.broadcasted_iota(jnp.int32, (KEY_BLOCK, STAT_LANES), 1)
        tasks = [(rows, h) for rows in (lo, hi) for h in range(N_HEADS)]
        head = lambda h: slice(h * HEAD_DIM, (h + 1) * HEAD_DIM)

        def prev_of(rows, h):
            if rows is lo:
                return kb_ref[:, head(h)], vb_ref[:, head(h)], before_mask
            return k_ref[lo, head(h)], v_ref[lo, head(h)], prev_mask

        scores = []
        for rows, h in tasks:
            q = q_ref[rows, head(h)]
            scores.append((_dot_nt(q, prev_of(rows, h)[0]), _dot_nt(q, k_ref[rows, head(h)])))
        probs = []
        for (rows, h), (qk_prev, qk_cur) in zip(tasks, scores):
            s_prev = jnp.where(prev_of(rows, h)[2], qk_prev * scale, NEG)
            s_cur = jnp.where(cur_mask, qk_cur * scale, NEG)
            m = jnp.max(jnp.maximum(s_prev, s_cur), axis=-1, keepdims=True)
            p_prev = jnp.exp(s_prev - m)
            p_cur = jnp.exp(s_cur - m)
            den = jnp.sum(p_prev + p_cur, axis=-1, keepdims=True)
            probs.append((p_prev.astype(BF16), p_cur.astype(BF16), den, m + jnp.log(den)))
        stats = [jnp.zeros((KEY_BLOCK, STAT_LANES), F32), jnp.zeros((KEY_BLOCK, STAT_LANES), F32)]
        for (rows, h), (p_prev, p_cur, den, lse) in zip(tasks, probs):
            o = _dot_nn(p_cur, v_ref[rows, head(h)]) + _dot_nn(p_prev, prev_of(rows, h)[1])
            o_ref[rows, head(h)] = (o / den).astype(BF16)
            b = 0 if rows is lo else 1
            stats[b] = jnp.where(lane == h, lse, stats[b])
        st_ref[lo, :] = stats[0]
        st_ref[hi, :] = stats[1]

    main = lambda cb: pl.BlockSpec((None, CHUNK, D_ATTN), lambda r, i: (r, i, cb))
    before = lambda cb: pl.BlockSpec((None, KEY_BLOCK, D_ATTN), lambda r, i: (r, jnp.maximum(2 * i - 1, 0), cb))
    return _pallas(
        body, name=name, grid=(dil, n // CHUNK),
        in_specs=[main(0), main(1), main(2), before(1), before(2)],
        out_specs=[main(0), pl.BlockSpec((None, CHUNK, STAT_LANES), lambda r, i: (r, i, 0))],
        out_shape=[jax.ShapeDtypeStruct((dil, n, D_ATTN), BF16), jax.ShapeDtypeStruct((dil, n, STAT_LANES), F32)],
        compiler_params=_params(("parallel", "parallel"), 40),
    )(qkv, qkv, qkv, qkv, qkv)


def _attn_bwd(qkv, do, stats, name, comm=None):
    dil, n, _ = qkv.shape
    n_blocks = n // KEY_BLOCK
    last = n // CHUNK - 1
    scale = HEAD_DIM ** -0.5
    lo, hi = slice(0, KEY_BLOCK), slice(KEY_BLOCK, CHUNK)

    def body(q_ref, k_ref, v_ref, kb_ref, vb_ref, qa_ref, do_ref, doa_ref, st_ref, sta_ref, dq_ref, dk_ref, dv_ref):
        i = pl.program_id(1)
        cur_mask, prev_mask = _band_masks()
        before_mask = jnp.logical_and(prev_mask, i > 0)
        after_mask = jnp.logical_and(prev_mask, i < last)

        rows_cat = lambda a, b: jnp.concatenate([a, b], axis=0)
        masks = (jnp.concatenate([before_mask, cur_mask], axis=1), jnp.concatenate([prev_mask, cur_mask], axis=1),
                 after_mask)

        def operands(h):
            cols = slice(h * HEAD_DIM, (h + 1) * HEAD_DIM)
            lse_c, del_c = slice(h, h + 1), slice(N_HEADS + h, N_HEADS + h + 1)
            q = (q_ref[lo, cols], q_ref[hi, cols], qa_ref[:, cols])
            do = (do_ref[lo, cols], do_ref[hi, cols], doa_ref[:, cols])
            keys = (rows_cat(kb_ref[:, cols], k_ref[lo, cols]), k_ref[:, cols], k_ref[hi, cols])
            vals = (rows_cat(vb_ref[:, cols], v_ref[lo, cols]), v_ref[:, cols], v_ref[hi, cols])
            st = ((st_ref[lo, lse_c], st_ref[lo, del_c]), (st_ref[hi, lse_c], st_ref[hi, del_c]),
                  (sta_ref[:, lse_c], sta_ref[:, del_c]))
            return cols, q, do, keys, vals, st

        group = N_HEADS // 2
        for first_head in range(0, N_HEADS, group):
            heads = range(first_head, first_head + group)
            raw = {}
            for h in heads:
                _, q, do, keys, vals, _ = operands(h)
                raw[h] = [(_dot_nt(q[j], keys[j]), _dot_nt(do[j], vals[j])) for j in range(3)]
            grads = {}
            for h in heads:
                st = operands(h)[5]
                grads[h] = []
                for j in range(3):
                    qk, dp = raw[h][j]
                    lse, delta = st[j]
                    p = jnp.exp(jnp.where(masks[j], qk * scale, NEG) - lse)
                    grads[h].append((p.astype(BF16), (p * (dp - delta) * scale).astype(BF16)))
            for h in heads:
                cols, q, do, keys, _, _ = operands(h)
                (p0, ds0), (p1, ds1), (pa, dsa) = grads[h]
                own, nxt = slice(KEY_BLOCK, CHUNK), slice(0, KEY_BLOCK)

                def put(ref, rows, val, cols=cols):
                    ref[rows, cols] = val.astype(ref.dtype)

                put(dq_ref, lo, _dot_nn(ds0, keys[0]))
                put(dq_ref, hi, _dot_nn(ds1, keys[1]))
                put(dk_ref, lo, _dot_tn(rows_cat(ds0[:, own], ds1[:, nxt]), q_ref[:, cols]))
                put(dk_ref, hi, _dot_tn(rows_cat(ds1[:, own], dsa), rows_cat(q[1], q[2])))
                put(dv_ref, lo, _dot_tn(rows_cat(p0[:, own], p1[:, nxt]), do_ref[:, cols]))
                put(dv_ref, hi, _dot_tn(rows_cat(p1[:, own], pa), rows_cat(do[1], do[2])))

    def spec(rows, width, row_of, cb):
        return pl.BlockSpec((None, rows, width), lambda r, i: (r, row_of(i), cb))

    same = lambda i: i
    before = lambda i: jnp.maximum(2 * i - 1, 0)
    after = lambda i: jnp.minimum(2 * i + 2, n_blocks - 1)
    out = spec(CHUNK, D_ATTN, same, 0)
    return _call(
        body, name=name, grid=(dil, n // CHUNK),
        in_specs=[spec(CHUNK, D_ATTN, same, 0), spec(CHUNK, D_ATTN, same, 1), spec(CHUNK, D_ATTN, same, 2),
                  spec(KEY_BLOCK, D_ATTN, before, 1), spec(KEY_BLOCK, D_ATTN, before, 2),
                  spec(KEY_BLOCK, D_ATTN, after, 0),
                  spec(CHUNK, D_ATTN, same, 0), spec(KEY_BLOCK, D_ATTN, after, 0),
                  spec(CHUNK, STAT_LANES, same, 0), spec(KEY_BLOCK, STAT_LANES, after, 0)],
        out_specs=[out, out, out],
        out_shape=[jax.ShapeDtypeStruct((dil, n, D_ATTN), BF16)] * 3,
        scratch_shapes=[], semantics=("parallel", "parallel"), vmem_mib=40,
        args=(qkv, qkv, qkv, qkv, qkv, qkv, do, do, stats, stats), comm=comm)


def _window_sums(ext, window, backward):
    rows = ext.shape[0]
    acc, span = ext, 1
    while span < window:
        acc = acc + pltpu.roll(acc, (rows - span) if backward else span, axis=0)
        span *= 2
    return acc


def _pool_group_weight(wp_ref, g):
    return jnp.concatenate([wp_ref[k, g] for k in range(N_SHARDS)], axis=0)


def _mix_gate(o_list, st_list, hug, w_pool_g, pool_scale):
    seq = hug.shape[0]
    tm = 256
    halo_blocks = tm // POOL_HALO
    d4, d16 = DILATIONS[1], DILATIONS[2]

    def body(o1_ref, o4_ref, o16_ref, l1_ref, l4_ref, l16_ref, u_ref, halo_ref, ga_ref, gp_ref, wp_ref, sc_ref,
             y_ref, mix_ref, lse_ref, pooled_ref, n4_ref, n16_ref, nl4_ref, nl16_ref):
        i = pl.program_id(0)
        _from_pattern(o4_ref, n4_ref, d4)
        _from_pattern(o16_ref, n16_ref, d16)
        _from_pattern(l4_ref, nl4_ref, d4)
        _from_pattern(l16_ref, nl16_ref, d16)
        la, lb, lc = l1_ref[...], nl4_ref[0], nl16_ref[0]
        mx = jnp.maximum(jnp.maximum(la, lb), lc)
        ea, eb, ec = jnp.exp(la - mx), jnp.exp(lb - mx), jnp.exp(lc - mx)
        tot = ea + eb + ec
        lse_ref[...] = mx + jnp.log(tot)
        wa, wb, wc = ea / tot, eb / tot, ec / tot
        ga = ga_ref[...].astype(F32)
        silu_a = ga * jax.nn.sigmoid(ga)
        for h in range(N_HEADS):
            cols = slice(h * HEAD_DIM, (h + 1) * HEAD_DIM)
            hc = slice(h, h + 1)
            attn = wa[:, hc] * o1_ref[:, cols].astype(F32) + wb[:, hc] * n4_ref[h] + wc[:, hc] * n16_ref[h]
            mix_ref[:, cols] = attn.astype(BF16)
            y_ref[:, cols] = (attn * silu_a[:, cols]).astype(BF16)

        u = u_ref[...].astype(F32)
        halo = jnp.where(i > 0, halo_ref[...].astype(F32), 0.0)
        ext = jnp.concatenate([halo, u], axis=0)
        pos = i * tm + lax.broadcasted_iota(jnp.int32, (tm, 1), 0)
        gp = gp_ref[...].astype(F32)
        gated_scale = sc_ref[...] * (gp * jax.nn.sigmoid(gp))
        for g, window in enumerate(POOL_WINDOWS):
            cols = slice(g * POOL_GROUP_DIM, (g + 1) * POOL_GROUP_DIM)
            sums = _window_sums(ext[:, cols], window, backward=False)[POOL_HALO:, :]
            count = jnp.minimum(pos + 1, window).astype(F32)
            pooled = (sums / count - u[:, cols]).astype(BF16)
            pooled_ref[:, cols] = pooled
            pre = _dot_nn(pooled, _pool_group_weight(wp_ref, g))
            out_cols = slice(D_ATTN + g * POOL_GROUP_DIM, D_ATTN + (g + 1) * POOL_GROUP_DIM)
            mix_ref[:, out_cols] = pre.astype(BF16)
            y_ref[:, out_cols] = (pre * gated_scale[:, cols]).astype(BF16)

    row = lambda width, cb=0: pl.BlockSpec((tm, width), lambda i: (i, cb))
    pat = lambda d, width: pl.BlockSpec((d, tm // d, width), lambda i: (0, i, 0))
    return _pallas(
        body, name="mix_gate", grid=(seq // tm,),
        in_specs=[row(D_ATTN), pat(d4, D_ATTN), pat(d16, D_ATTN),
                  row(STAT_LANES), pat(d4, STAT_LANES), pat(d16, STAT_LANES),
                  row(D_POOL),
                  pl.BlockSpec((POOL_HALO, D_POOL), lambda i: (jnp.maximum(i * halo_blocks - 1, 0), 0)),
                  row(D_ATTN, 1), row(D_POOL, 2),
                  pl.BlockSpec(w_pool_g.shape, lambda i: (0, 0, 0, 0)),
                  pl.BlockSpec((1, D_POOL), lambda i: (0, 0))],
        out_specs=[row(D_MODEL), row(D_MODEL), row(STAT_LANES), row(D_POOL)],
        out_shape=[jax.ShapeDtypeStruct((seq, D_MODEL), BF16), jax.ShapeDtypeStruct((seq, D_MODEL), BF16),
                   jax.ShapeDtypeStruct((seq, STAT_LANES), F32), jax.ShapeDtypeStruct((seq, D_POOL), BF16)],
        scratch_shapes=[pltpu.VMEM((N_HEADS, tm, HEAD_DIM), F32), pltpu.VMEM((N_HEADS, tm, HEAD_DIM), F32),
                        pltpu.VMEM((1, tm, STAT_LANES), F32), pltpu.VMEM((1, tm, STAT_LANES), F32)],
        compiler_params=_params(("parallel",), 48),
    )(o_list[0][0], o_list[1], o_list[2], st_list[0][0], st_list[1], st_list[2],
      hug, hug, hug, hug, w_pool_g, pool_scale)


def _out_proj_loss(y, w_out_g, x, target, gain, bias):
    seq = x.shape[0]
    tm = 512

    def body(y_ref, w_ref, x_ref, t_ref, g_ref, b_ref, dz_ref, dzb_ref, gg_ref, gb_ref, loss_ref):
        @pl.when(pl.program_id(0) == 0)
        def _():
            gg_ref[...] = jnp.zeros_like(gg_ref)
            gb_ref[...] = jnp.zeros_like(gb_ref)
            loss_ref[...] = jnp.zeros_like(loss_ref)

        halves = [slice(0, tm // 2), slice(tm // 2, tm)]
        projected = [_dot_nn(y_ref[rows, :], w_ref[...]) for rows in halves]
        for rows, out in zip(halves, projected):
            z = DEEPNORM_ALPHA * x_ref[rows, :] + out
            mu = jnp.mean(z, axis=-1, keepdims=True)
            zc = z - mu
            rstd = lax.rsqrt(jnp.mean(zc * zc, axis=-1, keepdims=True) + LN_EPS)
            xhat = zc * rstd
            gain_v = g_ref[...]
            diff = xhat * gain_v + b_ref[...] - t_ref[rows, :]
            sq = _fold_rows(diff * diff)
            part = sq[:, :128]
            for k in range(1, D_MODEL // 128):
                part = part + sq[:, k * 128:(k + 1) * 128]
            loss_ref[...] += part
            dln = diff * (1.0 / D_MODEL)
            gg_ref[...] += _fold_rows(dln * xhat)
            gb_ref[...] += _fold_rows(dln)
            dxhat = dln * gain_v
            dz = rstd * (dxhat - jnp.mean(dxhat, axis=-1, keepdims=True)
                         - xhat * jnp.mean(dxhat * xhat, axis=-1, keepdims=True))
            dz_ref[rows, :] = dz
            dzb_ref[rows, :] = dz.astype(BF16)

    row = lambda: pl.BlockSpec((tm, D_MODEL), lambda i: (i, 0))
    vec = lambda: pl.BlockSpec((1, D_MODEL), lambda i: (0, 0))
    acc = lambda width: pl.BlockSpec((8, width), lambda i: (0, 0))
    return _pallas(
        body, name="out_proj_loss", grid=(seq // tm,),
        in_specs=[row(), pl.BlockSpec((D_MODEL, D_MODEL), lambda i: (0, 0), pipeline_mode=pl.Buffered(1)),
                  row(), row(), vec(), vec()],
        out_specs=[row(), row(), acc(D_MODEL), acc(D_MODEL), acc(128)],
        out_shape=[jax.ShapeDtypeStruct((seq, D_MODEL), F32), jax.ShapeDtypeStruct((seq, D_MODEL), BF16),
                   jax.ShapeDtypeStruct((8, D_MODEL), F32), jax.ShapeDtypeStruct((8, D_MODEL), F32),
                   jax.ShapeDtypeStruct((8, 128), F32)],
        compiler_params=_params(("arbitrary",), 56),
    )(y, w_out_g.reshape(D_MODEL, D_MODEL), x, target, gain, bias)


def _dy_gate_bwd(dzb, w_out_g, hug, mixpre, pool_scale, lse_all):
    seq = dzb.shape[0]
    tm = 256
    d4, d16 = DILATIONS[1], DILATIONS[2]

    def body(dz_ref, w_ref, ga_ref, gp_ref, mix_ref, sc_ref, lse_ref,
             dh_ref, dpo_ref, do1_ref, do4_ref, do16_ref, st1_ref, st4_ref, st16_ref, da_ref, st_ref):
        dy = _dot_nt(dz_ref[...], w_ref[...])
        ga = ga_ref[...].astype(F32)
        sig = jax.nn.sigmoid(ga)
        attn = mix_ref[:, :D_ATTN].astype(F32)
        dya = dy[:, :D_ATTN]
        dattn = dya * (ga * sig)
        dh_ref[:, :D_ATTN] = (dya * attn * (sig * (1.0 + ga * (1.0 - sig)))).astype(BF16)
        _store_slabs(da_ref, dattn)
        lane = lax.broadcasted_iota(jnp.int32, (tm, STAT_LANES), 1)
        stats = lse_ref[...]
        prod = dattn * attn
        for h in range(N_HEADS):
            delta = jnp.sum(prod[:, h * HEAD_DIM:(h + 1) * HEAD_DIM], axis=-1, keepdims=True)
            stats = jnp.where(lane == N_HEADS + h, delta, stats)
        st_ref[0] = stats
        do1_ref[...] = dattn.astype(BF16)
        st1_ref[...] = stats
        _to_pattern(da_ref, do4_ref, d4, BF16)
        _to_pattern(da_ref, do16_ref, d16, BF16)
        _to_pattern(st_ref, st4_ref, d4, F32)
        _to_pattern(st_ref, st16_ref, d16, F32)

        gp = gp_ref[...].astype(F32)
        sig = jax.nn.sigmoid(gp)
        dyp = dy[:, D_ATTN:]
        dpo_ref[...] = (dyp * (gp * sig)).astype(BF16)
        dh_ref[:, D_ATTN:] = (dyp * (mix_ref[:, D_ATTN:].astype(F32) * sc_ref[...])
                              * (sig * (1.0 + gp * (1.0 - sig)))).astype(BF16)

    row = lambda width, cb=0: pl.BlockSpec((tm, width), lambda i: (i, cb))
    pat = lambda d, width: pl.BlockSpec((d, tm // d, width), lambda i: (0, i, 0))
    pat_shape = lambda d, width, dtype: jax.ShapeDtypeStruct((d, seq // d, width), dtype)
    outs = _pallas(
        body, name="dy_gate_bwd", grid=(seq // tm,),
        in_specs=[row(D_MODEL), pl.BlockSpec((D_MODEL, D_MODEL), lambda i: (0, 0)),
                  row(D_ATTN, 1), row(D_POOL, 2), row(D_MODEL), pl.BlockSpec((1, D_POOL), lambda i: (0, 0)),
                  row(STAT_LANES)],
        out_specs=[row(D_MODEL, D_IN // D_MODEL - 1), row(D_POOL),
                   row(D_ATTN), pat(d4, D_ATTN), pat(d16, D_ATTN),
                   row(STAT_LANES), pat(d4, STAT_LANES), pat(d16, STAT_LANES)],
        out_shape=[jax.ShapeDtypeStruct((seq, D_IN), BF16), jax.ShapeDtypeStruct((seq, D_POOL), BF16),
                   jax.ShapeDtypeStruct((seq, D_ATTN), BF16), pat_shape(d4, D_ATTN, BF16), pat_shape(d16, D_ATTN, BF16),
                   jax.ShapeDtypeStruct((seq, STAT_LANES), F32), pat_shape(d4, STAT_LANES, F32),
                   pat_shape(d16, STAT_LANES, F32)],
        scratch_shapes=[pltpu.VMEM((N_HEADS, tm, HEAD_DIM), F32), pltpu.VMEM((1, tm, STAT_LANES), F32)],
        compiler_params=_params(("parallel",), 48),
    )(dzb, w_out_g.reshape(D_MODEL, D_MODEL), hug, hug, mixpre, pool_scale, lse_all)
    dh, dpo, do1, do4, do16, st1, st4, st16 = outs
    return dh, dpo, [do1[None], do4, do16], [st1[None], st4, st16]


def _pool_bwd(dh, dpo, mixpre, pooled, w_pool_g, pool_scale):
    seq = dpo.shape[0]
    tm = 256
    halo_blocks = tm // POOL_HALO
    last = seq // tm - 1
    n_groups = len(POOL_WINDOWS)
    half_c = POOL_GROUP_DIM // N_SHARDS // 2
    pieces = (N_SHARDS, 2, n_groups * half_c, POOL_GROUP_DIM)

    def body(dh_in_ref, dpo_ref, halo_ref, pre_ref, pooled_ref, wp_ref, sc_ref, du_ref, gw_ref, gs_ref):
        i = pl.program_id(0)

        @pl.when(i == 0)
        def _():
            gw_ref[...] = jnp.zeros_like(gw_ref)
            gs_ref[...] = jnp.zeros_like(gs_ref)

        dpo = dpo_ref[...].astype(F32)
        scale = sc_ref[...]
        gs_ref[...] += _fold_rows(dpo * pre_ref[...].astype(F32))
        halo = jnp.where(i < last, halo_ref[...].astype(F32), 0.0)
        dpw = (jnp.concatenate([dpo, halo], axis=0) * scale).astype(BF16)
        pos = i * tm + lax.broadcasted_iota(jnp.int32, (tm + POOL_HALO, 1), 0)
        for g, window in enumerate(POOL_WINDOWS):
            cols = slice(g * POOL_GROUP_DIM, (g + 1) * POOL_GROUP_DIM)
            dpw_g = dpw[:, cols]
            gw = _dot_tn(pooled_ref[:, cols], dpw_g[:tm, :])
            for piece in range(2 * N_SHARDS):
                gw_ref[piece // 2, piece % 2, g * half_c:(g + 1) * half_c, :] += gw[piece * half_c:(piece + 1) * half_c]
            dpooled = _dot_nt(dpw_g, _pool_group_weight(wp_ref, g))
            count = jnp.minimum(pos + 1, window).astype(F32)
            sums = _window_sums(dpooled / count, window, backward=True)
            du_ref[:, cols] = (sums[:tm, :] - dpooled[:tm, :]).astype(BF16)

    row = lambda width, cb=0: pl.BlockSpec((tm, width), lambda i: (i, cb))
    return _pallas(
        body, name="pool_bwd", grid=(seq // tm,),
        in_specs=[ANY, row(D_POOL),
                  pl.BlockSpec((POOL_HALO, D_POOL),
                               lambda i: (jnp.minimum((i + 1) * halo_blocks, seq // POOL_HALO - 1), 0)),
                  row(D_POOL, 1), row(D_POOL),
                  pl.BlockSpec(w_pool_g.shape, lambda i: (0, 0, 0, 0)),
                  pl.BlockSpec((1, D_POOL), lambda i: (0, 0))],
        out_specs=[row(D_POOL, D_QKV // D_POOL),
                   pl.BlockSpec(pieces, lambda i: (0, 0, 0, 0)),
                   pl.BlockSpec((8, D_POOL), lambda i: (0, 0))],
        out_shape=[jax.ShapeDtypeStruct(dh.shape, dh.dtype),
                   jax.ShapeDtypeStruct(pieces, F32),
                   jax.ShapeDtypeStruct((8, D_POOL), F32)],
        input_output_aliases={0: 0},
        compiler_params=_params(("arbitrary",), 40),
    )(dh, dpo, dpo, mixpre, pooled, w_pool_g, pool_scale)


def _sum_patterns(dh, parts, tabs, unrotate, col_block, name, comm=None):
    seq = dh.shape[0]
    tm, tn = 256, D_ATTN
    per = D_ATTN // tn
    d4, d16 = DILATIONS[1], DILATIONS[2]

    def body(dh_in_ref, a1_ref, a4_ref, a16_ref, ct_ref, up_ref, down_ref, o_ref, n4_ref, n16_ref):
        _from_pattern(a4_ref, n4_ref, d4)
        _from_pattern(a16_ref, n16_ref, d16)
        for s in range(tn // HEAD_DIM):
            cols = slice(s * HEAD_DIM, (s + 1) * HEAD_DIM)
            tot = a1_ref[:, cols].astype(F32) + n4_ref[s] + n16_ref[s]
            if unrotate:
                tot = _rotate_heads(tot, ct_ref[...], -up_ref[...], -down_ref[...])
            o_ref[:, cols] = tot.astype(BF16)

    tab = pl.BlockSpec((tm, HEAD_DIM), lambda i, j: (i, 0))
    pat = lambda d: pl.BlockSpec((d, tm // d, tn), lambda i, j: (0, i, j))
    (dh,), exchanged = _call(
        body, name=name, grid=(seq // tm, per),
        in_specs=[ANY, pl.BlockSpec((tm, tn), lambda i, j: (i, j)), pat(d4), pat(d16), tab, tab, tab],
        out_specs=[pl.BlockSpec((tm, tn), lambda i, j: (i, col_block * per + j))],
        out_shape=[jax.ShapeDtypeStruct(dh.shape, dh.dtype)],
        scratch_shapes=[pltpu.VMEM((tn // HEAD_DIM, tm, HEAD_DIM), F32), pltpu.VMEM((tn // HEAD_DIM, tm, HEAD_DIM), F32)],
        semantics=("parallel", "parallel"), vmem_mib=32, args=(dh, parts[0][0], parts[1], parts[2], *tabs),
        aliases={0: 0}, comm=comm)
    return dh, exchanged


def _grad_w_in(x, dh, half, name, comm=None):
    seq = x.shape[0]
    ts, td, te = 2048, D_MODEL // 2, SHARD_IN

    def body(half_ref, x_ref, dh_ref, o_ref):
        k = pl.program_id(1)
        part = _dot_tn(x_ref[...].astype(BF16), dh_ref[...])

        @pl.when(k == 0)
        def _():
            o_ref[...] = part

        @pl.when(k > 0)
        def _():
            o_ref[...] += part

    (g,), exchanged = _call(
        body, name=name, grid=(N_SHARDS, seq // ts),
        in_specs=[pl.BlockSpec((ts, td), lambda e, k, half_ref: (k, half_ref[0])),
                  pl.BlockSpec((ts, te), lambda e, k, half_ref: (k, e))],
        out_specs=[pl.BlockSpec((None, td, te), lambda e, k, half_ref: (e, 0, 0))],
        out_shape=[jax.ShapeDtypeStruct((N_SHARDS, td, te), F32)],
        scratch_shapes=[], semantics=("parallel", "arbitrary"), vmem_mib=56, args=(x, dh), comm=comm,
        prefetch=(half,))
    return g, exchanged


def _grad_w_out(y, dzb):
    seq = y.shape[0]
    ts, te = 2048, 1024

    def body(y_ref, dz_ref, o_ref):
        k = pl.program_id(1)
        part = _dot_tn(y_ref[...], dz_ref[...])

        @pl.when(k == 0)
        def _():
            o_ref[...] = part

        @pl.when(k > 0)
        def _():
            o_ref[...] += part

    return _pallas(
        body, name="grad_w_out", grid=(D_MODEL // te, seq // ts),
        in_specs=[pl.BlockSpec((ts, te), lambda e, k: (k, e)), pl.BlockSpec((ts, D_MODEL), lambda e, k: (k, 0))],
        out_specs=pl.BlockSpec((te, D_MODEL), lambda e, k: (e, 0)),
        out_shape=jax.ShapeDtypeStruct((D_MODEL, D_MODEL), F32),
        compiler_params=_params(("parallel", "arbitrary"), 56),
    )(y, dzb)


GRAD_X_LATE_SHARDS = 1
GRAD_X_PARTIAL_ROWS = 512


def _grad_x_partial(dh, w_in_g, dz, first, tiles, prev=None, comm=None):
    seq = dh.shape[0]
    tm, tk = GRAD_X_PARTIAL_ROWS, SHARD_IN

    def body(*refs):
        dh_ref, w_ref, dz_ref, o_ref = refs[-4:]
        k = pl.program_id(1)
        part = _dot_nt(dh_ref[...], w_ref[...])

        @pl.when(k == 0)
        def _():
            o_ref[...] = DEEPNORM_ALPHA * dz_ref[...] + part

        @pl.when(k > 0)
        def _():
            o_ref[...] += part

    carried = [] if prev is None else [prev]
    row = pl.BlockSpec((tm, D_MODEL), lambda i, k: (i + first, 0))
    (partial,), exchanged = _call(
        body, name="grad_x_partial_%d" % first, grid=(tiles, N_SHARDS - GRAD_X_LATE_SHARDS),
        in_specs=[ANY] * len(carried) + [
            pl.BlockSpec((tm, tk), lambda i, k: (i + first, k)),
            pl.BlockSpec((None, D_MODEL, tk), lambda i, k: (k, 0, 0)), row],
        out_specs=[row],
        out_shape=[jax.ShapeDtypeStruct((seq, D_MODEL), F32)],
        scratch_shapes=[], semantics=("parallel", "arbitrary"), vmem_mib=48, args=(*carried, dh, w_in_g, dz),
        aliases={0: 0} if carried else None, comm=comm)
    return partial, exchanged


def _grad_x_final(dh, w_in_g, partial):
    seq = dh.shape[0]
    tm, tk = 512, SHARD_IN
    k0 = N_SHARDS - GRAD_X_LATE_SHARDS

    def body(dh_ref, w_ref, p_ref, o_ref):
        k = pl.program_id(1)
        part = _dot_nt(dh_ref[...], w_ref[...])

        @pl.when(k == 0)
        def _():
            o_ref[...] = p_ref[...] + part

        @pl.when(k > 0)
        def _():
            o_ref[...] += part

    row = pl.BlockSpec((tm, D_MODEL), lambda i, k: (i, 0))
    return _pallas(
        body, name="grad_x_final", grid=(seq // tm, GRAD_X_LATE_SHARDS),
        in_specs=[pl.BlockSpec((tm, tk), lambda i, k: (i, k + k0)),
                  pl.BlockSpec((None, D_MODEL, tk), lambda i, k: (k + k0, 0, 0)), row],
        out_specs=row, out_shape=jax.ShapeDtypeStruct((seq, D_MODEL), F32),
        compiler_params=_params(("parallel", "arbitrary"), 48),
    )(dh, w_in_g, partial)


def _pool_weight(w_pool_sh):
    n_groups = len(POOL_WINDOWS)
    shard_c = POOL_GROUP_DIM // N_SHARDS
    return w_pool_sh.reshape(N_SHARDS, n_groups, shard_c, POOL_GROUP_DIM)


def _step(x, target, w_bufs, pool_scale, gain, bias, place):
    seq = x.shape[0]
    tabs = _rope_tables(seq)
    core, chip_core, onward, plan = place
    qkv, hug, w_in_g, w_out_g, w_pool_sh = _in_proj_gathering(x, w_bufs, tabs, plan)
    o_list, st_list = [], []
    for p, dil in enumerate(DILATIONS):
        o, st = _attn_fwd(qkv[p], "attn_fwd_d%d" % dil)
        o_list.append(o)
        st_list.append(st)
    w_pool_g = _pool_weight(w_pool_sh)
    y, mixpre, lse_all, pooled = _mix_gate(o_list, st_list, hug, w_pool_g, pool_scale)
    dz, dzb, gain_part, bias_part, loss_part = _out_proj_loss(y, w_out_g, x, target, gain, bias)
    dh, dpo, do_list, stat_list = _dy_gate_bwd(dzb, w_out_g, hug, mixpre, pool_scale, lse_all)
    g_w_out = _grad_w_out(y, dzb)
    dh, g_w_pool, scale_part = _pool_bwd(dh, dpo, mixpre, pooled, w_pool_g, pool_scale)
    small = jnp.concatenate([scale_part, gain_part, bias_part, loss_part], axis=1)
    early = [g_w_out.reshape(N_SHARDS, 2, D_MODEL // (2 * N_SHARDS), D_MODEL), g_w_pool]

    bwd = lambda p, comm: _attn_bwd(qkv[p], do_list[p], stat_list[p], "attn_bwd_d%d" % DILATIONS[p], comm)
    part_a, recv = bwd(0, _exchange_halves(early))
    sums = [_add_own_half(g, r, core, "add_own_half_%d" % a) for a, (g, r) in enumerate(zip(early, recv))]
    part_b, recv = bwd(1, _scatter_to_chips([s[1] for s in sums]))
    bufs = [_add_chips([s[0]], r, chip_core, "add_chips_%d" % a) for a, (s, r) in enumerate(zip(sums, recv))]
    part_c, reduced = bwd(2, _share_with_sibling(bufs))
    parts = [part_a, part_b, part_c]
    dh, gathered = _sum_patterns(dh, [t[0] for t in parts], tabs, True, 0, "sum_dq", _gather_small(small))
    dh, _ = _sum_patterns(dh, [t[1] for t in parts], tabs, True, 1, "sum_dk")
    dh, _ = _sum_patterns(dh, [t[2] for t in parts], tabs, False, 2, "sum_dv")

    give, _ = _grad_w_in(x, dh, 1 - core, "grad_w_in_give")
    keep, recv = _grad_w_in(x, dh, core, "grad_w_in_keep", _send_to_sibling([give]))
    total = [keep, recv[0]]
    total_b = _add_pair(keep, recv[0], "add_own_half_w_in")
    n_tiles = seq // GRAD_X_PARTIAL_ROWS
    tiles = 3 * n_tiles // 8
    part, relayed = _grad_x_partial(dh, w_in_g, dz, 0, tiles, None, _relay_diagonal(total_b))
    total_b = _fold_relayed(total, total_b, relayed[0], onward)
    part, recv = _grad_x_partial(dh, w_in_g, dz, tiles, n_tiles - tiles, part, _scatter_to_neighbours(total_b))
    buf = _add_chips(total, recv[0], chip_core, "add_chips_w_in")
    g_x = _grad_x_final(dh, w_in_g, part)
    g_w_in = _run_exchange(_share_with_sibling([buf]), "share_w_in")[0]
    return g_x, g_w_in, reduced[0], reduced[1], small, gathered[0]


def _exchange_halves(grads):
    n = len(grads)

    def copies(src, dst, sems):
        x, y, c, _ = _mesh_place()
        return [_remote(src[a].at[j, 1 - c], dst[a].at[j], sems[0].at[a, j], sems[1].at[a, j], (x, y, 1 - c))
                for a in range(n) for j in range(N_SHARDS)]

    def start(src, dst, sems):
        for cp in copies(src, dst, sems):
            cp.start()

    def finish(src, dst, sems):
        for cp in copies(src, dst, sems):
            cp.wait()

    return _Exchange(grads, [jax.ShapeDtypeStruct((N_SHARDS,) + g.shape[2:], g.dtype) for g in grads], {},
                     [pltpu.SemaphoreType.DMA((n, N_SHARDS))] * 2, start, finish)


def _add_own_half(grad, recv, core, name):
    _, _, r, c = grad.shape
    tr = min(r, 256)

    def body(core_ref, g_ref, r_ref, o_ref, ob_ref):
        tot = g_ref[...] + r_ref[...]
        o_ref[...] = tot
        ob_ref[...] = tot.astype(BF16)

    out = pl.BlockSpec((None, tr, c), lambda j, i, core_ref: (j, i, 0))
    return _pallas(
        body, name=name,
        grid_spec=pltpu.PrefetchScalarGridSpec(
            num_scalar_prefetch=1, grid=(N_SHARDS, r // tr),
            in_specs=[pl.BlockSpec((None, None, tr, c), lambda j, i, core_ref: (j, core_ref[0], i, 0)),
                      pl.BlockSpec((None, tr, c), lambda j, i, core_ref: (j, i, 0))],
            out_specs=[out, out]),
        out_shape=[jax.ShapeDtypeStruct((N_SHARDS, r, c), F32), jax.ShapeDtypeStruct((N_SHARDS, r, c), BF16)],
        compiler_params=_params(("parallel", "parallel"), 32),
    )(core, grad, recv)


def _send_to_sibling(arrays):
    n = len(arrays)

    def copies(src, dst, sems):
        x, y, c, _ = _mesh_place()
        return [_remote(src[a], dst[a], sems[0].at[a], sems[1].at[a], (x, y, 1 - c)) for a in range(n)]

    def start(src, dst, sems):
        for cp in copies(src, dst, sems):
            cp.start()

    def finish(src, dst, sems):
        for cp in copies(src, dst, sems):
            cp.wait()

    return _Exchange(arrays, [jax.ShapeDtypeStruct(t.shape, t.dtype) for t in arrays], {},
                     [pltpu.SemaphoreType.DMA((n,))] * 2, start, finish)


def _add_pair(a, b, name):
    _, r, c = a.shape
    tr = min(r, 256)

    def body(a_ref, b_ref, ob_ref):
        ob_ref[...] = (a_ref[...] + b_ref[...]).astype(BF16)

    spec = pl.BlockSpec((None, tr, c), lambda j, i: (j, i, 0))
    return _pallas(
        body, name=name, grid=(N_SHARDS, r // tr), in_specs=[spec, spec], out_specs=spec,
        out_shape=jax.ShapeDtypeStruct(a.shape, BF16),
        compiler_params=_params(("parallel", "parallel"), 32),
    )(a, b)


def _scatter_to_chips(sums):
    n = len(sums)

    def copies(src, dst, sems):
        x, y, c, chips = _mesh_place()
        return [_remote(src[a].at[2 * cx + cy], dst[a].at[k], sems[0].at[a, k], sems[1].at[a, k], (cx, cy, c))
                for a in range(n) for k, (cx, cy) in enumerate(chips)]

    def start(src, dst, sems):
        for cp in copies(src, dst, sems):
            cp.start()

    def finish(src, dst, sems):
        for cp in copies(src, dst, sems):
            cp.wait()

    return _Exchange(sums, [jax.ShapeDtypeStruct((3,) + s.shape[1:], s.dtype) for s in sums], {},
                     [pltpu.SemaphoreType.DMA((n, 3))] * 2, start, finish)


def _add_chips(sums, recv, chip_core, name):
    _, r, c = sums[0].shape
    n_sums, n_recv = len(sums), recv.shape[0]
    tr = min(r, 256)

    def body(cc_ref, *refs):
        r_ref, o_ref = refs[n_sums:]
        tot = refs[0][...]
        for s_ref in refs[1:n_sums]:
            tot = tot + s_ref[...]
        for k in range(n_recv):
            tot = tot + r_ref[k].astype(F32)
        o_ref[...] = tot

    return _pallas(
        body, name=name,
        grid_spec=pltpu.PrefetchScalarGridSpec(
            num_scalar_prefetch=1, grid=(r // tr,),
            in_specs=[pl.BlockSpec((None, tr, c), lambda i, cc_ref: (cc_ref[0], i, 0))] * n_sums
            + [pl.BlockSpec((n_recv, tr, c), lambda i, cc_ref: (0, i, 0))],
            out_specs=pl.BlockSpec((None, tr, c), lambda i, cc_ref: (cc_ref[1], i, 0))),
        out_shape=jax.ShapeDtypeStruct((2, r, c), F32),
        compiler_params=_params(("parallel",), 32),
    )(chip_core, *sums, recv)


def _relay_diagonal(sums_b):
    def copy(src, dst, sems):
        x, y, c, _ = _mesh_place()
        diagonal = 2 * (1 - x) + (1 - y)
        return _remote(src[0].at[diagonal], dst[0], sems[0].at[0], sems[1].at[0], (x ^ (1 - c), y ^ c, c))

    def start(src, dst, sems):
        copy(src, dst, sems).start()

    def finish(src, dst, sems):
        copy(src, dst, sems).wait()

    return _Exchange([sums_b], [jax.ShapeDtypeStruct(sums_b.shape[1:], sums_b.dtype)], {},
                     [pltpu.SemaphoreType.DMA((1,))] * 2, start, finish)


def _fold_relayed(sums, sums_b, relayed, onward):
    _, r, c = sums[0].shape
    n_sums = len(sums)
    tr = min(r, 256)

    def body(on_ref, b_in_ref, *refs):
        r_ref, o_ref = refs[n_sums:]
        tot = refs[0][...]
        for s_ref in refs[1:n_sums]:
            tot = tot + s_ref[...]
        o_ref[...] = (tot + r_ref[...].astype(F32)).astype(BF16)

    return _pallas(
        body, name="fold_relayed",
        grid_spec=pltpu.PrefetchScalarGridSpec(
            num_scalar_prefetch=1, grid=(r // tr,),
            in_specs=[ANY] + [pl.BlockSpec((None, tr, c), lambda i, on_ref: (on_ref[0], i, 0))] * n_sums
            + [pl.BlockSpec((tr, c), lambda i, on_ref: (i, 0))],
            out_specs=pl.BlockSpec((None, tr, c), lambda i, on_ref: (on_ref[0], i, 0))),
        out_shape=jax.ShapeDtypeStruct(sums_b.shape, sums_b.dtype),
        input_output_aliases={1: 0},
        compiler_params=_params(("parallel",), 32),
    )(onward, sums_b, *sums, relayed)


def _scatter_to_neighbours(sums_b):
    def copies(src, dst, sems):
        x, y, c, chips = _mesh_place()
        return [_remote(src[0].at[2 * cx + cy], dst[0].at[k], sems[0].at[k], sems[1].at[k], (cx, cy, c))
                for k, (cx, cy) in enumerate(chips[:2])]

    def start(src, dst, sems):
        for cp in copies(src, dst, sems):
            cp.start()

    def finish(src, dst, sems):
        for cp in copies(src, dst, sems):
            cp.wait()

    return _Exchange([sums_b], [jax.ShapeDtypeStruct((2,) + sums_b.shape[1:], sums_b.dtype)], {},
                     [pltpu.SemaphoreType.DMA((2,))] * 2, start, finish)


def _share_with_sibling(bufs):
    n = len(bufs)

    def copies(dst, sems, half):
        x, y, c, _ = _mesh_place()
        h = c if half == "mine" else 1 - c
        return [_remote(dst[a].at[h], dst[a].at[h], sems[0].at[a], sems[1].at[a], (x, y, 1 - c)) for a in range(n)]

    def start(ins, dst, sems):
        for cp in copies(dst, sems, "mine"):
            cp.start()

    def finish(ins, dst, sems):
        for cp in copies(dst, sems, "theirs"):
            cp.wait_recv()
        for cp in copies(dst, sems, "mine"):
            cp.wait_send()

    return _Exchange(bufs, [jax.ShapeDtypeStruct(b.shape, b.dtype) for b in bufs], {a: a for a in range(n)},
                     [pltpu.SemaphoreType.DMA((n,))] * 2, start, finish)


def _adam_math(w, g, m, v):
    m = ADAM_B1 * m + (1.0 - ADAM_B1) * g
    v = ADAM_B2 * v + (1.0 - ADAM_B2) * (g * g)
    m_hat = m / (1.0 - ADAM_B1 ** ADAM_STEP)
    v_hat = v / (1.0 - ADAM_B2 ** ADAM_STEP)
    delta = -ADAM_LR * (m_hat / (jnp.sqrt(v_hat) + ADAM_EPS) + ADAM_WD * w)
    return delta, m, v


def _gather_small(small):
    def peers():
        x, y, c, _ = _mesh_place()
        return [(x ^ ((r >> 2) & 1), y ^ ((r >> 1) & 1), c ^ (r & 1)) for r in range(1, 8)], 4 * x + 2 * y + c

    def start(src, dst, sems):
        to, me = peers()
        for r, peer in enumerate(to):
            _remote(src[0], dst[0].at[me], sems[0].at[r], sems[1].at[r], peer).start()

    def finish(src, dst, sems):
        to, me = peers()
        for r, (px, py, pc) in enumerate(to):
            theirs = dst[0].at[4 * px + 2 * py + pc]
            _remote(theirs, theirs, sems[0].at[r], sems[1].at[r], (px, py, pc)).wait_recv()
        for r, peer in enumerate(to):
            _remote(src[0], dst[0].at[me], sems[0].at[r], sems[1].at[r], peer).wait_send()

    return _Exchange([small], [jax.ShapeDtypeStruct((8,) + small.shape, small.dtype)], {},
                     [pltpu.SemaphoreType.DMA((7,))] * 2, start, finish)


def _small_adamw(gathered, small, me, w_vecs, m_vecs, v_vecs):
    n = len(w_vecs)
    widths = [w.shape[1] for w in w_vecs]
    n_par = sum(widths)

    def body(me_ref, a_ref, s_ref, *refs):
        w_refs, m_refs, v_refs = refs[:n], refs[n:2 * n], refs[2 * n:3 * n]
        loss_ref, outs = refs[3 * n], refs[3 * n + 1:]
        mine = s_ref[...]
        tot = jnp.where(me_ref[0] == 0, mine, a_ref[0])
        for d in range(1, 8):
            tot = tot + jnp.where(me_ref[0] == d, mine, a_ref[d])
        tot = jnp.sum(tot, axis=0, keepdims=True)
        sq = jnp.sum(tot[:, n_par:], axis=1, keepdims=True)
        loss_ref[...] = jnp.broadcast_to(sq * (0.5 / D_MODEL), loss_ref.shape)
        lo = 0
        for k in range(n):
            g = tot[:, lo:lo + widths[k]]
            lo += widths[k]
            outs[k][...] = g
            outs[n + k][...], outs[2 * n + k][...], outs[3 * n + k][...] = _adam_math(
                w_refs[k][...], g, m_refs[k][...], v_refs[k][...])

    vm = pl.BlockSpec(memory_space=pltpu.VMEM)
    vecs = [jax.ShapeDtypeStruct((1, w), F32) for w in widths] * 4
    res = pl.pallas_call(
        body, name="small_adamw",
        grid_spec=pltpu.PrefetchScalarGridSpec(num_scalar_prefetch=1, grid=(), in_specs=[vm] * (2 + 3 * n),
                                               out_specs=[vm] * (1 + 4 * n)),
        out_shape=[jax.ShapeDtypeStruct((1, 128), F32)] + vecs,
    )(me, gathered, small, *w_vecs, *m_vecs, *v_vecs)
    return res[0], res[1:1 + n], res[1 + n:1 + 2 * n], res[1 + 2 * n:1 + 3 * n], res[1 + 3 * n:]


def _adamw(w, g, m, v, name):
    r, c = w.shape
    tr = min(r, 256)

    def body(w_ref, g_ref, m_ref, v_ref, go_ref, d_ref, nm_ref, nv_ref):
        g = g_ref[...]
        go_ref[...] = g
        d_ref[...], nm_ref[...], nv_ref[...] = _adam_math(w_ref[...], g, m_ref[...], v_ref[...])

    spec = pl.BlockSpec((tr, c), lambda i: (i, 0))
    shape = jax.ShapeDtypeStruct((r, c), F32)
    return _pallas(
        body, name=name, grid=(r // tr,),
        in_specs=[spec] * 4, out_specs=[spec] * 4, out_shape=[shape] * 4,
        compiler_params=_params(("parallel",), 48),
    )(w, g, m, v)


def kernel(x, w_in, w_pool, pool_scale, w_out, ln_gain, ln_bias, loss_target, m_w_in, m_w_pool, m_pool_scale, m_w_out, m_ln_gain, m_ln_bias, v_w_in, v_w_pool, v_pool_scale, v_w_out, v_ln_gain, v_ln_bias):
    xi, yi, ci = lax.axis_index("x"), lax.axis_index("y"), lax.axis_index("c")
    chip = (2 * xi + yi).astype(jnp.int32).reshape(1)
    core = ci.astype(jnp.int32).reshape(1)
    n_groups = len(POOL_WINDOWS)
    shard_c = w_pool.shape[2]

    w_in_b = _cast_bf16(w_in[0], chip, "cast_w_in", 256)
    w_out_b = _cast_bf16(w_out[0], chip, "cast_w_out", 256)
    w_pool_b = _cast_bf16(w_pool[0].reshape(n_groups * shard_c, POOL_GROUP_DIM), chip, "cast_w_pool", 256)

    chip_core = jnp.concatenate([chip, core])
    onward = (2 * (xi ^ ci) + (yi ^ (1 - ci))).astype(jnp.int32).reshape(1)
    g_x, full_in, full_out, full_pool, small, small_all = _step(
        x[0], loss_target[0], [w_in_b, w_out_b, w_pool_b], pool_scale, ln_gain, ln_bias,
        (core, chip_core, onward, _in_proj_plan(xi, yi)))
    half_c = shard_c // 2
    grad_w_in = full_in.reshape(D_MODEL, SHARD_IN)
    grad_w_out = full_out.reshape(D_MODEL // N_SHARDS, D_MODEL)
    grad_w_pool = (full_pool.reshape(2, n_groups, half_c, POOL_GROUP_DIM).transpose(1, 0, 2, 3)
                   .reshape(n_groups * shard_c, POOL_GROUP_DIM))

    grad_w_in, d_in, nm_in, nv_in = _adamw(w_in[0], grad_w_in, m_w_in[0], v_w_in[0], "adamw_w_in")
    grad_w_out, d_out, nm_out, nv_out = _adamw(w_out[0], grad_w_out, m_w_out[0], v_w_out[0], "adamw_w_out")
    flat = lambda t: t[0].reshape(n_groups * shard_c, POOL_GROUP_DIM)
    grad_w_pool, d_pool, nm_pool, nv_pool = _adamw(flat(w_pool), grad_w_pool, flat(m_w_pool), flat(v_w_pool),
                                                   "adamw_w_pool")

    me = (4 * xi + 2 * yi + ci).astype(jnp.int32).reshape(1)
    loss_v, g_vecs, d_vecs, nm_vecs, nv_vecs = _small_adamw(
        small_all, small, me, [pool_scale, ln_gain, ln_bias], [m_pool_scale, m_ln_gain, m_ln_bias],
        [v_pool_scale, v_ln_gain, v_ln_bias])
    g_scale, g_gain, g_bias = g_vecs
    d_scale, d_gain, d_bias = d_vecs
    nm_scale, nm_gain, nm_bias = nm_vecs
    nv_scale, nv_gain, nv_bias = nv_vecs
    pool_shape = w_pool.shape
    return (loss_v[0, 0], g_x[None],
            grad_w_in[None], grad_w_pool.reshape(pool_shape), g_scale, grad_w_out[None], g_gain, g_bias,
            d_in[None], d_pool.reshape(pool_shape), d_scale, d_out[None], d_gain, d_bias,
            nm_in[None], nm_pool.reshape(pool_shape), nm_scale, nm_out[None], nm_gain, nm_bias,
            nv_in[None], nv_pool.reshape(pool_shape), nv_scale, nv_out[None], nv_gain, nv_bias)
```

```python
import functools

import jax
import jax.numpy as jnp
import numpy as np
from jax import lax
from jax.experimental import pallas as pl
from jax.experimental.pallas import tpu as pltpu

F32 = jnp.float32
BF16 = jnp.bfloat16
MESH = pl.DeviceIdType.MESH
ANY = pl.BlockSpec(memory_space=pl.ANY)

D_MODEL = 2048
D_ATTN = 1024
D_POOL = 1024
HEAD_DIM = 128
N_HEADS = 8
ROPE_DIM = 32
ROPE_THETA = 500000.0
DILATIONS = (1, 4, 16)
KEY_BLOCK = 128
CHUNK = 2 * KEY_BLOCK
STAT_LANES = 128
POOL_WINDOWS = (2, 4, 8, 16)
POOL_GROUP_DIM = 256
POOL_HALO = 16
D_QKV = 3 * D_ATTN
D_UG = D_POOL + D_MODEL
D_IN = D_QKV + D_UG
N_SHARDS = 4
SHARD_IN = D_IN // N_SHARDS
LN_EPS = 1e-5
DEEPNORM_ALPHA = 2.0 ** 0.25
ADAM_LR = 0.001
ADAM_B1 = 0.9
ADAM_B2 = 0.999
ADAM_EPS = 1e-08
ADAM_WD = 0.01
ADAM_STEP = 10
NEG = -1e30
MIB = 1024 * 1024


def _params(sem, vmem_mib):
    return pltpu.CompilerParams(dimension_semantics=sem, vmem_limit_bytes=vmem_mib * MIB)


def _pallas(body, **kwargs):
    pin = lambda s: pltpu.HBM(s.shape, s.dtype) if len(s.shape) >= 2 else s
    out_shape = kwargs.pop("out_shape")
    out_shape = [pin(s) for s in out_shape] if isinstance(out_shape, (list, tuple)) else pin(out_shape)
    call = pl.pallas_call(body, out_shape=out_shape, **kwargs)

    def run(*operands):
        return call(*[pltpu.with_memory_space_constraint(o, pltpu.HBM) if o.ndim >= 2 else o for o in operands])

    return run


class _Exchange:
    def __init__(self, operands, out_shape, aliases, sems, start, finish):
        self.operands, self.out_shape, self.aliases, self.sems = list(operands), list(out_shape), dict(aliases), list(sems)
        self.start, self.finish = start, finish


def _run_exchange(comm, name):
    n_in, n_out = len(comm.operands), len(comm.out_shape)

    def body(*refs):
        ins, outs, sems = refs[:n_in], refs[n_in:n_in + n_out], refs[n_in + n_out:]
        comm.start(ins, outs, sems)
        comm.finish(ins, outs, sems)

    return _pallas(
        body, name=name, in_specs=[ANY] * n_in, out_specs=[ANY] * n_out, out_shape=comm.out_shape,
        input_output_aliases=comm.aliases, scratch_shapes=comm.sems,
    )(*comm.operands)


def _call(body, *, name, grid, in_specs, out_specs, out_shape, scratch_shapes, semantics, vmem_mib, args,
          aliases=None, comm=None, prefetch=()):
    aliases = dict(aliases or {})
    n_pre, n_in, n_out, n_scr = len(prefetch), len(in_specs), len(out_specs), len(scratch_shapes)
    c_in, c_out = (len(comm.operands), len(comm.out_shape)) if comm else (0, 0)
    c_shapes, c_sems, c_operands = (comm.out_shape, comm.sems, comm.operands) if comm else ([], [], [])

    def hosted(*refs):
        pre, refs = refs[:n_pre], refs[n_pre:]
        a = n_in
        b = a + c_in
        c = b + n_out
        d = c + c_out
        e = d + n_scr
        if comm is None:
            body(*pre, *refs)
            return
        ids = [pl.program_id(k) for k in range(len(grid))]
        first = functools.reduce(jnp.logical_and, [i == 0 for i in ids])
        last = functools.reduce(jnp.logical_and, [i == g - 1 for i, g in zip(ids, grid)])

        @pl.when(first)
        def _():
            comm.start(refs[a:b], refs[c:d], refs[e:])

        body(*pre, *refs[:a], *refs[b:c], *refs[d:e])

        @pl.when(last)
        def _():
            comm.finish(refs[a:b], refs[c:d], refs[e:])

    if comm:
        semantics = ("arbitrary",) * len(grid)
        for i, o in comm.aliases.items():
            aliases[n_pre + n_in + i] = n_out + o
    outs = _pallas(
        hosted, name=name,
        grid_spec=pltpu.PrefetchScalarGridSpec(
            num_scalar_prefetch=n_pre, grid=grid, in_specs=list(in_specs) + [ANY] * c_in,
            out_specs=list(out_specs) + [ANY] * c_out, scratch_shapes=list(scratch_shapes) + c_sems),
        out_shape=list(out_shape) + c_shapes, input_output_aliases=aliases,
        compiler_params=_params(semantics, vmem_mib),
    )(*prefetch, *args, *c_operands)
    return list(outs[:n_out]), list(outs[n_out:])


def _dot_nn(a, b):
    return jnp.dot(a, b, preferred_element_type=F32)


def _dot_nt(a, b):
    return lax.dot_general(a, b, (((1,), (1,)), ((), ())), preferred_element_type=F32)


def _dot_tn(a, b):
    return lax.dot_general(a, b, (((0,), (0,)), ((), ())), preferred_element_type=F32)


def _fold_rows(a):
    r, c = a.shape
    return jnp.sum(a.reshape(r // 8, 8, c), axis=0)


def _cast_bf16(a, chip, name, rows):
    r, c = a.shape

    def body(chip_ref, a_ref, o_ref):
        o_ref[...] = a_ref[...].astype(BF16)

    return _pallas(
        body, name=name,
        grid_spec=pltpu.PrefetchScalarGridSpec(
            num_scalar_prefetch=1, grid=(r // rows,),
            in_specs=[pl.BlockSpec((rows, c), lambda i, chip_ref: (i, 0))],
            out_specs=pl.BlockSpec((None, rows, c), lambda i, chip_ref: (chip_ref[0], i, 0))),
        out_shape=jax.ShapeDtypeStruct((N_SHARDS, r, c), BF16),
        compiler_params=_params(("parallel",), 32),
    )(chip, a)


def _mesh_place():
    x, y, c = lax.axis_index("x"), lax.axis_index("y"), lax.axis_index("c")
    return x, y, c, [(1 - x, y), (x, 1 - y), (1 - x, 1 - y)]


def _remote(src, dst, send_sem, recv_sem, to):
    return pltpu.make_async_remote_copy(src_ref=src, dst_ref=dst, send_sem=send_sem, recv_sem=recv_sem,
                                        device_id=to, device_id_type=MESH)


def _rope_tables(seq):
    half = ROPE_DIM // 2
    inv_freq = (np.float64(ROPE_THETA) ** (-(2.0 * np.arange(half, dtype=np.float64)) / ROPE_DIM)).astype(np.float32)
    ang = np.arange(seq, dtype=np.float32)[:, None] * inv_freq[None, :]
    cos = np.cos(ang.astype(np.float64)).astype(np.float32)
    sin = np.sin(ang.astype(np.float64)).astype(np.float32)
    pad = np.zeros((seq, HEAD_DIM - ROPE_DIM), np.float32)
    zeros = np.zeros((seq, half), np.float32)
    c_tab = np.concatenate([cos, cos, pad + 1.0], axis=1)
    up_tab = np.concatenate([-sin, zeros, pad], axis=1)
    down_tab = np.concatenate([zeros, sin, pad], axis=1)
    return jnp.asarray(c_tab), jnp.asarray(up_tab), jnp.asarray(down_tab)


def _rotate_heads(t, c_tab, up_tab, down_tab):
    outs = []
    for h in range(t.shape[1] // HEAD_DIM):
        th = t[:, h * HEAD_DIM:(h + 1) * HEAD_DIM]
        up = pltpu.roll(th, HEAD_DIM - ROPE_DIM // 2, axis=1)
        down = pltpu.roll(th, ROPE_DIM // 2, axis=1)
        outs.append(th * c_tab + up * up_tab + down * down_tab)
    return outs[0] if len(outs) == 1 else jnp.concatenate(outs, axis=1)


def _to_pattern(slabs_ref, dst_ref, dil, dtype):
    n_slabs, rows, _ = slabs_ref.shape
    for s in range(n_slabs):
        for r in range(dil):
            dst_ref[r, :, s * 128:(s + 1) * 128] = slabs_ref[s, pl.ds(r, rows // dil, dil), :].astype(dtype)


def _from_pattern(src_ref, slabs_ref, dil):
    n_slabs, rows, _ = slabs_ref.shape
    for s in range(n_slabs):
        for r in range(dil):
            slabs_ref[s, pl.ds(r, rows // dil, dil), :] = src_ref[r, :, s * 128:(s + 1) * 128].astype(F32)


def _store_slabs(slabs_ref, value):
    for s in range(slabs_ref.shape[0]):
        slabs_ref[s] = value[:, s * 128:(s + 1) * 128]


W_IN_CHUNKS = 4


def _in_proj_plan(x, y):
    shards = [2 * x + y, 2 * (1 - x) + y, 2 * x + (1 - y), 2 * (1 - x) + (1 - y)]
    last_row = jnp.int32(-2)

    def table(active, col_of):
        cols, rows = [], []
        first_col = functools.reduce(lambda acc, j: jnp.where(active[j], col_of(shards[j]), acc), reversed(range(4)),
                                     jnp.int32(0))
        held_col, seen = first_col, jnp.bool_(False)
        for j in range(4):
            cols.append(jnp.where(active[j], col_of(shards[j]), held_col))
            rows.append(jnp.where(active[j], -1, jnp.where(seen, last_row, 0)))
            held_col = jnp.where(active[j], col_of(shards[j]), held_col)
            seen = jnp.logical_or(seen, active[j])
        return cols, rows

    q_cols, q_rows = table([s < 2 for s in shards], lambda s: s)
    h_cols, h_rows = table([s >= 2 for s in shards], lambda s: s - 2)
    return jnp.stack([jnp.asarray(v, jnp.int32) for v in shards + q_cols + q_rows + h_cols + h_rows])


def _in_proj_gathering(x, w_bufs, tabs, plan):
    seq = x.shape[0]
    tm, tn = 512, SHARD_IN
    n_tiles = seq // tm
    heads = tn // HEAD_DIM
    k_heads_in_second = 2 * D_ATTN // HEAD_DIM - heads
    d4, d16 = DILATIONS[1], DILATIONS[2]
    DIAGONAL = 2
    chunk = D_MODEL // 2 // W_IN_CHUNKS
    early = [(0, D_MODEL // 2, q * chunk, chunk) for q in range(W_IN_CHUNKS)]
    late = [(a, w_bufs[a].shape[1] // 2, 0, w_bufs[a].shape[1] // 2) for a in (1, 2)]
    pieces = early + late
    early_ids, late_ids = range(len(early)), range(len(early), len(pieces))

    def body(plan_ref, x_ref, w_in_in, w_out_in, w_pool_in, c_ref, up_ref, down_ref,
             o1_ref, o4_ref, o16_ref, hug_ref, w_ref, w_out_ref, w_pool_ref,
             wbuf_ref, res_ref, w_sem, ici_send, ici_recv, d2d_send, d2d_recv):
        j, i = pl.program_id(0), pl.program_id(1)
        mx, my, mc, chips = _mesh_place()
        sibling = (mx, my, 1 - mc)
        gathered = (w_ref, w_out_ref, w_pool_ref)
        chip_of = lambda k: 2 * chips[k][0] + chips[k][1]

        def piece(n, chip, core):
            a, per_core, offset, size = pieces[n]
            return gathered[a].at[chip, pl.ds(core * per_core + offset, size)]

        def to_neighbour(k, n):
            mine = piece(n, 2 * mx + my, mc)
            return _remote(mine, mine, ici_send.at[n, k], ici_recv.at[n, k], (*chips[k], mc))

        def relay(n):
            theirs = piece(n, 2 * (mx ^ (1 - mc)) + (my ^ mc), mc)
            return _remote(theirs, theirs, ici_send.at[n, DIAGONAL], ici_recv.at[n, DIAGONAL], (mx ^ mc, my ^ (1 - mc), mc))

        def arrival(k, n):
            theirs = piece(n, chip_of(k), mc)
            return _remote(theirs, theirs, ici_send.at[n, k], ici_recv.at[n, k], (*chips[k], mc))

        def to_sibling(k, n, core):
            theirs = piece(n, chip_of(k), core)
            return _remote(theirs, theirs, d2d_send.at[n, k], d2d_recv.at[n, k], sibling)

        def take(k, ids):
            for n in ids:
                arrival(k, n).wait_recv()
                to_sibling(k, n, mc).start()

        def taken(k, ids):
            for n in ids:
                to_sibling(k, n, 1 - mc).wait_recv()

        first_tile = i == 0

        @pl.when(jnp.logical_and(j == 0, first_tile))
        def _():
            for n in range(len(pieces)):
                for k in range(DIAGONAL):
                    to_neighbour(k, n).start()

        @pl.when(jnp.logical_and(j == 1, first_tile))
        def _():
            take(0, early_ids)
            taken(0, early_ids)

        @pl.when(jnp.logical_and(j == 2, first_tile))
        def _():
            take(1, early_ids)
            for n in early_ids:
                relay(n).start()
            taken(1, early_ids)
            for k in range(DIAGONAL):
                take(k, late_ids)
            for n in late_ids:
                relay(n).start()
            for k in range(DIAGONAL):
                taken(k, late_ids)

        @pl.when(jnp.logical_and(j == 3, first_tile))
        def _():
            take(DIAGONAL, range(len(pieces)))
            taken(DIAGONAL, range(len(pieces)))

        shard = plan_ref[j]

        @pl.when(first_tile)
        def _():
            cp = pltpu.make_async_copy(w_ref.at[shard], wbuf_ref, w_sem)
            cp.start()
            cp.wait()

        xb = x_ref[...].astype(BF16)
        group = 4 * HEAD_DIM
        accs = [_dot_nn(xb, wbuf_ref[:, g * group:(g + 1) * group]) for g in range(tn // group)]

        def emit_qkv(rotated_heads):
            for h in range(heads):
                lanes = (h * HEAD_DIM) % group
                th = accs[h * HEAD_DIM // group][:, lanes:lanes + HEAD_DIM]
                if h < rotated_heads:
                    th = _rotate_heads(th, c_ref[...], up_ref[...], down_ref[...])
                res_ref[h] = th
                o1_ref[:, h * HEAD_DIM:(h + 1) * HEAD_DIM] = th.astype(BF16)
            _to_pattern(res_ref, o4_ref, d4, BF16)
            _to_pattern(res_ref, o16_ref, d16, BF16)

        @pl.when(shard == 0)
        def _():
            emit_qkv(heads)

        @pl.when(shard == 1)
        def _():
            emit_qkv(k_heads_in_second)

        @pl.when(shard >= 2)
        def _():
            for g, acc in enumerate(accs):
                hug_ref[:, g * group:(g + 1) * group] = acc.astype(BF16)

        @pl.when(jnp.logical_and(j == 3, i == n_tiles - 1))
        def _():
            for n in range(len(pieces)):
                for k in range(DIAGONAL):
                    to_neighbour(k, n).wait_send()
                relay(n).wait_send()
                for k in range(DIAGONAL + 1):
                    to_sibling(k, n, mc).wait_send()

    def held(base, last):
        return lambda j, i, plan_ref: jnp.where(plan_ref[base + j] == -1, i,
                                                jnp.where(plan_ref[base + j] == -2, last, 0))

    q_row, h_row = held(8, n_tiles - 1), held(16, n_tiles - 1)
    tab_spec = pl.BlockSpec((tm, HEAD_DIM), lambda j, i, plan_ref: (i, 0))
    sems = [pltpu.SemaphoreType.DMA((len(pieces), 3))] * 4
    o1, o4, o16, hug, w_in_g, w_out_g, w_pool_g = _pallas(
        body, name="in_proj_gathering",
        grid_spec=pltpu.PrefetchScalarGridSpec(
            num_scalar_prefetch=1, grid=(N_SHARDS, n_tiles),
            in_specs=[pl.BlockSpec((tm, D_MODEL), lambda j, i, plan_ref: (i, 0)), ANY, ANY, ANY,
                      tab_spec, tab_spec, tab_spec],
            out_specs=[pl.BlockSpec((tm, tn), lambda j, i, p: (q_row(j, i, p), p[4 + j])),
                       pl.BlockSpec((d4, tm // d4, tn), lambda j, i, p: (0, q_row(j, i, p), p[4 + j])),
                       pl.BlockSpec((d16, tm // d16, tn), lambda j, i, p: (0, q_row(j, i, p), p[4 + j])),
                       pl.BlockSpec((tm, tn), lambda j, i, p: (h_row(j, i, p), p[12 + j])),
                       ANY, ANY, ANY],
            scratch_shapes=[pltpu.VMEM((D_MODEL, tn), BF16), pltpu.VMEM((heads, tm, HEAD_DIM), F32),
                            pltpu.SemaphoreType.DMA(())] + sems),
        out_shape=[jax.ShapeDtypeStruct((seq, D_QKV), BF16),
                   jax.ShapeDtypeStruct((d4, seq // d4, D_QKV), BF16),
                   jax.ShapeDtypeStruct((d16, seq // d16, D_QKV), BF16),
                   jax.ShapeDtypeStruct((seq, D_UG), BF16)]
        + [jax.ShapeDtypeStruct(b.shape, b.dtype) for b in w_bufs],
        input_output_aliases={2: 4, 3: 5, 4: 6},
        compiler_params=_params(("arbitrary", "arbitrary"), 52),
    )(plan, x, *w_bufs, *tabs)
    return [o1[None], o4, o16], hug, w_in_g, w_out_g, w_pool_g


def _band_masks():
    row = lax.broadcasted_iota(jnp.int32, (KEY_BLOCK, KEY_BLOCK), 0)
    col = lax.broadcasted_iota(jnp.int32, (KEY_BLOCK, KEY_BLOCK), 1)
    return col <= row, col >= row


def _attn_fwd(qkv, name):
    dil, n, _ = qkv.shape
    scale = HEAD_DIM ** -0.5
    lo, hi = slice(0, KEY_BLOCK), slice(KEY_BLOCK, CHUNK)

    def body(q_ref, k_ref, v_ref, kb_ref, vb_ref, o_ref, st_ref):
        i = pl.program_id(1)
        cur_mask, prev_mask = _band_masks()
        before_mask = jnp.logical_and(prev_mask, i > 0)
        lane = lax.broadcasted_iota(jnp.int32, (KEY_BLOCK, STAT_LANES), 1)
        tasks = [(rows, h) for rows in (lo, hi) for h in range(N_HEADS)]
        head = lambda h: slice(h * HEAD_DIM, (h + 1) * HEAD_DIM)

        def prev_of(rows, h):
            if rows is lo:
                return kb_ref[:, head(h)], vb_ref[:, head(h)], before_mask
            return k_ref[lo, head(h)], v_ref[lo, head(h)], prev_mask

        scores = []
        for rows, h in tasks:
            q = q_ref[rows, head(h)]
            scores.append((_dot_nt(q, prev_of(rows, h)[0]), _dot_nt(q, k_ref[rows, head(h)])))
        probs = []
        for (rows, h), (qk_prev, qk_cur) in zip(tasks, scores):
            s_prev = jnp.where(prev_of(rows, h)[2], qk_prev * scale, NEG)
            s_cur = jnp.where(cur_mask, qk_cur * scale, NEG)
            m = jnp.max(jnp.maximum(s_prev, s_cur), axis=-1, keepdims=True)
            p_prev = jnp.exp(s_prev - m)
            p_cur = jnp.exp(s_cur - m)
            den = jnp.sum(p_prev + p_cur, axis=-1, keepdims=True)
            probs.append((p_prev.astype(BF16), p_cur.astype(BF16), den, m + jnp.log(den)))
        stats = [jnp.zeros((KEY_BLOCK, STAT_LANES), F32), jnp.zeros((KEY_BLOCK, STAT_LANES), F32)]
        for (rows, h), (p_prev, p_cur, den, lse) in zip(tasks, probs):
            o = _dot_nn(p_cur, v_ref[rows, head(h)]) + _dot_nn(p_prev, prev_of(rows, h)[1])
            o_ref[rows, head(h)] = (o / den).astype(BF16)
            b = 0 if rows is lo else 1
            stats[b] = jnp.where(lane == h, lse, stats[b])
        st_ref[lo, :] = stats[0]
        st_ref[hi, :] = stats[1]

    main = lambda cb: pl.BlockSpec((None, CHUNK, D_ATTN), lambda r, i: (r, i, cb))
    before = lambda cb: pl.BlockSpec((None, KEY_BLOCK, D_ATTN), lambda r, i: (r, jnp.maximum(2 * i - 1, 0), cb))
    return _pallas(
        body, name=name, grid=(dil, n // CHUNK),
        in_specs=[main(0), main(1), main(2), before(1), before(2)],
        out_specs=[main(0), pl.BlockSpec((None, CHUNK, STAT_LANES), lambda r, i: (r, i, 0))],
        out_shape=[jax.ShapeDtypeStruct((dil, n, D_ATTN), BF16), jax.ShapeDtypeStruct((dil, n, STAT_LANES), F32)],
        compiler_params=_params(("parallel", "parallel"), 40),
    )(qkv, qkv, qkv, qkv, qkv)


def _attn_bwd(qkv, do, stats, name, comm=None):
    dil, n, _ = qkv.shape
    n_blocks = n // KEY_BLOCK
    last = n // CHUNK - 1
    scale = HEAD_DIM ** -0.5
    lo, hi = slice(0, KEY_BLOCK), slice(KEY_BLOCK, CHUNK)

    def body(q_ref, k_ref, v_ref, kb_ref, vb_ref, qa_ref, do_ref, doa_ref, st_ref, sta_ref, dq_ref, dk_ref, dv_ref):
        i = pl.program_id(1)
        cur_mask, prev_mask = _band_masks()
        before_mask = jnp.logical_and(prev_mask, i > 0)
        after_mask = jnp.logical_and(prev_mask, i < last)

        rows_cat = lambda a, b: jnp.concatenate([a, b], axis=0)
        masks = (jnp.concatenate([before_mask, cur_mask], axis=1), jnp.concatenate([prev_mask, cur_mask], axis=1),
                 after_mask)

        def operands(h):
            cols = slice(h * HEAD_DIM, (h + 1) * HEAD_DIM)
            lse_c, del_c = slice(h, h + 1), slice(N_HEADS + h, N_HEADS + h + 1)
            q = (q_ref[lo, cols], q_ref[hi, cols], qa_ref[:, cols])
            do = (do_ref[lo, cols], do_ref[hi, cols], doa_ref[:, cols])
            keys = (rows_cat(kb_ref[:, cols], k_ref[lo, cols]), k_ref[:, cols], k_ref[hi, cols])
            vals = (rows_cat(vb_ref[:, cols], v_ref[lo, cols]), v_ref[:, cols], v_ref[hi, cols])
            st = ((st_ref[lo, lse_c], st_ref[lo, del_c]), (st_ref[hi, lse_c], st_ref[hi, del_c]),
                  (sta_ref[:, lse_c], sta_ref[:, del_c]))
            return cols, q, do, keys, vals, st

        group = N_HEADS // 2
        for first_head in range(0, N_HEADS, group):
            heads = range(first_head, first_head + group)
            raw = {}
            for h in heads:
                _, q, do, keys, vals, _ = operands(h)
                raw[h] = [(_dot_nt(q[j], keys[j]), _dot_nt(do[j], vals[j])) for j in range(3)]
            grads = {}
            for h in heads:
                st = operands(h)[5]
                grads[h] = []
                for j in range(3):
                    qk, dp = raw[h][j]
                    lse, delta = st[j]
                    p = jnp.exp(jnp.where(masks[j], qk * scale, NEG) - lse)
                    grads[h].append((p.astype(BF16), (p * (dp - delta) * scale).astype(BF16)))
            for h in heads:
                cols, q, do, keys, _, _ = operands(h)
                (p0, ds0), (p1, ds1), (pa, dsa) = grads[h]
                own, nxt = slice(KEY_BLOCK, CHUNK), slice(0, KEY_BLOCK)

                def put(ref, rows, val, cols=cols):
                    ref[rows, cols] = val.astype(ref.dtype)

                put(dq_ref, lo, _dot_nn(ds0, keys[0]))
                put(dq_ref, hi, _dot_nn(ds1, keys[1]))
                put(dk_ref, lo, _dot_tn(rows_cat(ds0[:, own], ds1[:, nxt]), q_ref[:, cols]))
                put(dk_ref, hi, _dot_tn(rows_cat(ds1[:, own], dsa), rows_cat(q[1], q[2])))
                put(dv_ref, lo, _dot_tn(rows_cat(p0[:, own], p1[:, nxt]), do_ref[:, cols]))
                put(dv_ref, hi, _dot_tn(rows_cat(p1[:, own], pa), rows_cat(do[1], do[2])))

    def spec(rows, width, row_of, cb):
        return pl.BlockSpec((None, rows, width), lambda r, i: (r, row_of(i), cb))

    same = lambda i: i
    before = lambda i: jnp.maximum(2 * i - 1, 0)
    after = lambda i: jnp.minimum(2 * i + 2, n_blocks - 1)
    out = spec(CHUNK, D_ATTN, same, 0)
    return _call(
        body, name=name, grid=(dil, n // CHUNK),
        in_specs=[spec(CHUNK, D_ATTN, same, 0), spec(CHUNK, D_ATTN, same, 1), spec(CHUNK, D_ATTN, same, 2),
                  spec(KEY_BLOCK, D_ATTN, before, 1), spec(KEY_BLOCK, D_ATTN, before, 2),
                  spec(KEY_BLOCK, D_ATTN, after, 0),
                  spec(CHUNK, D_ATTN, same, 0), spec(KEY_BLOCK, D_ATTN, after, 0),
                  spec(CHUNK, STAT_LANES, same, 0), spec(KEY_BLOCK, STAT_LANES, after, 0)],
        out_specs=[out, out, out],
        out_shape=[jax.ShapeDtypeStruct((dil, n, D_ATTN), BF16)] * 3,
        scratch_shapes=[], semantics=("parallel", "parallel"), vmem_mib=40,
        args=(qkv, qkv, qkv, qkv, qkv, qkv, do, do, stats, stats), comm=comm)


def _window_sums(ext, window, backward):
    rows = ext.shape[0]
    acc, span = ext, 1
    while span < window:
        acc = acc + pltpu.roll(acc, (rows - span) if backward else span, axis=0)
        span *= 2
    return acc


def _pool_group_weight(wp_ref, g):
    return jnp.concatenate([wp_ref[k, g] for k in range(N_SHARDS)], axis=0)


def _mix_gate(o_list, st_list, hug, w_pool_g, pool_scale):
    seq = hug.shape[0]
    tm = 256
    halo_blocks = tm // POOL_HALO
    d4, d16 = DILATIONS[1], DILATIONS[2]

    def body(o1_ref, o4_ref, o16_ref, l1_ref, l4_ref, l16_ref, u_ref, halo_ref, ga_ref, gp_ref, wp_ref, sc_ref,
             y_ref, mix_ref, lse_ref, pooled_ref, n4_ref, n16_ref, nl4_ref, nl16_ref):
        i = pl.program_id(0)
        _from_pattern(o4_ref, n4_ref, d4)
        _from_pattern(o16_ref, n16_ref, d16)
        _from_pattern(l4_ref, nl4_ref, d4)
        _from_pattern(l16_ref, nl16_ref, d16)
        la, lb, lc = l1_ref[...], nl4_ref[0], nl16_ref[0]
        mx = jnp.maximum(jnp.maximum(la, lb), lc)
        ea, eb, ec = jnp.exp(la - mx), jnp.exp(lb - mx), jnp.exp(lc - mx)
        tot = ea + eb + ec
        lse_ref[...] = mx + jnp.log(tot)
        wa, wb, wc = ea / tot, eb / tot, ec / tot
        ga = ga_ref[...].astype(F32)
        silu_a = ga * jax.nn.sigmoid(ga)
        for h in range(N_HEADS):
            cols = slice(h * HEAD_DIM, (h + 1) * HEAD_DIM)
            hc = slice(h, h + 1)
            attn = wa[:, hc] * o1_ref[:, cols].astype(F32) + wb[:, hc] * n4_ref[h] + wc[:, hc] * n16_ref[h]
            mix_ref[:, cols] = attn.astype(BF16)
            y_ref[:, cols] = (attn * silu_a[:, cols]).astype(BF16)

        u = u_ref[...].astype(F32)
        halo = jnp.where(i > 0, halo_ref[...].astype(F32), 0.0)
        ext = jnp.concatenate([halo, u], axis=0)
        pos = i * tm + lax.broadcasted_iota(jnp.int32, (tm, 1), 0)
        gp = gp_ref[...].astype(F32)
        gated_scale = sc_ref[...] * (gp * jax.nn.sigmoid(gp))
        for g, window in enumerate(POOL_WINDOWS):
            cols = slice(g * POOL_GROUP_DIM, (g + 1) * POOL_GROUP_DIM)
            sums = _window_sums(ext[:, cols], window, backward=False)[POOL_HALO:, :]
            count = jnp.minimum(pos + 1, window).astype(F32)
            pooled = (sums / count - u[:, cols]).astype(BF16)
            pooled_ref[:, cols] = pooled
            pre = _dot_nn(pooled, _pool_group_weight(wp_ref, g))
            out_cols = slice(D_ATTN + g * POOL_GROUP_DIM, D_ATTN + (g + 1) * POOL_GROUP_DIM)
            mix_ref[:, out_cols] = pre.astype(BF16)
            y_ref[:, out_cols] = (pre * gated_scale[:, cols]).astype(BF16)

    row = lambda width, cb=0: pl.BlockSpec((tm, width), lambda i: (i, cb))
    pat = lambda d, width: pl.BlockSpec((d, tm // d, width), lambda i: (0, i, 0))
    return _pallas(
        body, name="mix_gate", grid=(seq // tm,),
        in_specs=[row(D_ATTN), pat(d4, D_ATTN), pat(d16, D_ATTN),
                  row(STAT_LANES), pat(d4, STAT_LANES), pat(d16, STAT_LANES),
                  row(D_POOL),
                  pl.BlockSpec((POOL_HALO, D_POOL), lambda i: (jnp.maximum(i * halo_blocks - 1, 0), 0)),
                  row(D_ATTN, 1), row(D_POOL, 2),
                  pl.BlockSpec(w_pool_g.shape, lambda i: (0, 0, 0, 0)),
                  pl.BlockSpec((1, D_POOL), lambda i: (0, 0))],
        out_specs=[row(D_MODEL), row(D_MODEL), row(STAT_LANES), row(D_POOL)],
        out_shape=[jax.ShapeDtypeStruct((seq, D_MODEL), BF16), jax.ShapeDtypeStruct((seq, D_MODEL), BF16),
                   jax.ShapeDtypeStruct((seq, STAT_LANES), F32), jax.ShapeDtypeStruct((seq, D_POOL), BF16)],
        scratch_shapes=[pltpu.VMEM((N_HEADS, tm, HEAD_DIM), F32), pltpu.VMEM((N_HEADS, tm, HEAD_DIM), F32),
                        pltpu.VMEM((1, tm, STAT_LANES), F32), pltpu.VMEM((1, tm, STAT_LANES), F32)],
        compiler_params=_params(("parallel",), 48),
    )(o_list[0][0], o_list[1], o_list[2], st_list[0][0], st_list[1], st_list[2],
      hug, hug, hug, hug, w_pool_g, pool_scale)


def _out_proj_loss(y, w_out_g, x, target, gain, bias):
    seq = x.shape[0]
    tm = 512

    def body(y_ref, w_ref, x_ref, t_ref, g_ref, b_ref, dz_ref, dzb_ref, gg_ref, gb_ref, loss_ref):
        @pl.when(pl.program_id(0) == 0)
        def _():
            gg_ref[...] = jnp.zeros_like(gg_ref)
            gb_ref[...] = jnp.zeros_like(gb_ref)
            loss_ref[...] = jnp.zeros_like(loss_ref)

        halves = [slice(0, tm // 2), slice(tm // 2, tm)]
        projected = [_dot_nn(y_ref[rows, :], w_ref[...]) for rows in halves]
        for rows, out in zip(halves, projected):
            z = DEEPNORM_ALPHA * x_ref[rows, :] + out
            mu = jnp.mean(z, axis=-1, keepdims=True)
            zc = z - mu
            rstd = lax.rsqrt(jnp.mean(zc * zc, axis=-1, keepdims=True) + LN_EPS)
            xhat = zc * rstd
            gain_v = g_ref[...]
            diff = xhat * gain_v + b_ref[...] - t_ref[rows, :]
            sq = _fold_rows(diff * diff)
            part = sq[:, :128]
            for k in range(1, D_MODEL // 128):
                part = part + sq[:, k * 128:(k + 1) * 128]
            loss_ref[...] += part
            dln = diff * (1.0 / D_MODEL)
            gg_ref[...] += _fold_rows(dln * xhat)
            gb_ref[...] += _fold_rows(dln)
            dxhat = dln * gain_v
            dz = rstd * (dxhat - jnp.mean(dxhat, axis=-1, keepdims=True)
                         - xhat * jnp.mean(dxhat * xhat, axis=-1, keepdims=True))
            dz_ref[rows, :] = dz
            dzb_ref[rows, :] = dz.astype(BF16)

    row = lambda: pl.BlockSpec((tm, D_MODEL), lambda i: (i, 0))
    vec = lambda: pl.BlockSpec((1, D_MODEL), lambda i: (0, 0))
    acc = lambda width: pl.BlockSpec((8, width), lambda i: (0, 0))
    return _pallas(
        body, name="out_proj_loss", grid=(seq // tm,),
        in_specs=[row(), pl.BlockSpec((D_MODEL, D_MODEL), lambda i: (0, 0), pipeline_mode=pl.Buffered(1)),
                  row(), row(), vec(), vec()],
        out_specs=[row(), row(), acc(D_MODEL), acc(D_MODEL), acc(128)],
        out_shape=[jax.ShapeDtypeStruct((seq, D_MODEL), F32), jax.ShapeDtypeStruct((seq, D_MODEL), BF16),
                   jax.ShapeDtypeStruct((8, D_MODEL), F32), jax.ShapeDtypeStruct((8, D_MODEL), F32),
                   jax.ShapeDtypeStruct((8, 128), F32)],
        compiler_params=_params(("arbitrary",), 56),
    )(y, w_out_g.reshape(D_MODEL, D_MODEL), x, target, gain, bias)


def _dy_gate_bwd(dzb, w_out_g, hug, mixpre, pool_scale, lse_all):
    seq = dzb.shape[0]
    tm = 256
    d4, d16 = DILATIONS[1], DILATIONS[2]

    def body(dz_ref, w_ref, ga_ref, gp_ref, mix_ref, sc_ref, lse_ref,
             dh_ref, dpo_ref, do1_ref, do4_ref, do16_ref, st1_ref, st4_ref, st16_ref, da_ref, st_ref):
        dy = _dot_nt(dz_ref[...], w_ref[...])
        ga = ga_ref[...].astype(F32)
        sig = jax.nn.sigmoid(ga)
        attn = mix_ref[:, :D_ATTN].astype(F32)
        dya = dy[:, :D_ATTN]
        dattn = dya * (ga * sig)
        dh_ref[:, :D_ATTN] = (dya * attn * (sig * (1.0 + ga * (1.0 - sig)))).astype(BF16)
        _store_slabs(da_ref, dattn)
        lane = lax.broadcasted_iota(jnp.int32, (tm, STAT_LANES), 1)
        stats = lse_ref[...]
        prod = dattn * attn
        for h in range(N_HEADS):
            delta = jnp.sum(prod[:, h * HEAD_DIM:(h + 1) * HEAD_DIM], axis=-1, keepdims=True)
            stats = jnp.where(lane == N_HEADS + h, delta, stats)
        st_ref[0] = stats
        do1_ref[...] = dattn.astype(BF16)
        st1_ref[...] = stats
        _to_pattern(da_ref, do4_ref, d4, BF16)
        _to_pattern(da_ref, do16_ref, d16, BF16)
        _to_pattern(st_ref, st4_ref, d4, F32)
        _to_pattern(st_ref, st16_ref, d16, F32)

        gp = gp_ref[...].astype(F32)
        sig = jax.nn.sigmoid(gp)
        dyp = dy[:, D_ATTN:]
        dpo_ref[...] = (dyp * (gp * sig)).astype(BF16)
        dh_ref[:, D_ATTN:] = (dyp * (mix_ref[:, D_ATTN:].astype(F32) * sc_ref[...])
                              * (sig * (1.0 + gp * (1.0 - sig)))).astype(BF16)

    row = lambda width, cb=0: pl.BlockSpec((tm, width), lambda i: (i, cb))
    pat = lambda d, width: pl.BlockSpec((d, tm // d, width), lambda i: (0, i, 0))
    pat_shape = lambda d, width, dtype: jax.ShapeDtypeStruct((d, seq // d, width), dtype)
    outs = _pallas(
        body, name="dy_gate_bwd", grid=(seq // tm,),
        in_specs=[row(D_MODEL), pl.BlockSpec((D_MODEL, D_MODEL), lambda i: (0, 0)),
                  row(D_ATTN, 1), row(D_POOL, 2), row(D_MODEL), pl.BlockSpec((1, D_POOL), lambda i: (0, 0)),
                  row(STAT_LANES)],
        out_specs=[row(D_MODEL, D_IN // D_MODEL - 1), row(D_POOL),
                   row(D_ATTN), pat(d4, D_ATTN), pat(d16, D_ATTN),
                   row(STAT_LANES), pat(d4, STAT_LANES), pat(d16, STAT_LANES)],
        out_shape=[jax.ShapeDtypeStruct((seq, D_IN), BF16), jax.ShapeDtypeStruct((seq, D_POOL), BF16),
                   jax.ShapeDtypeStruct((seq, D_ATTN), BF16), pat_shape(d4, D_ATTN, BF16), pat_shape(d16, D_ATTN, BF16),
                   jax.ShapeDtypeStruct((seq, STAT_LANES), F32), pat_shape(d4, STAT_LANES, F32),
                   pat_shape(d16, STAT_LANES, F32)],
        scratch_shapes=[pltpu.VMEM((N_HEADS, tm, HEAD_DIM), F32), pltpu.VMEM((1, tm, STAT_LANES), F32)],
        compiler_params=_params(("parallel",), 48),
    )(dzb, w_out_g.reshape(D_MODEL, D_MODEL), hug, hug, mixpre, pool_scale, lse_all)
    dh, dpo, do1, do4, do16, st1, st4, st16 = outs
    return dh, dpo, [do1[None], do4, do16], [st1[None], st4, st16]


def _pool_bwd(dh, dpo, mixpre, pooled, w_pool_g, pool_scale):
    seq = dpo.shape[0]
    tm = 256
    halo_blocks = tm // POOL_HALO
    last = seq // tm - 1
    n_groups = len(POOL_WINDOWS)
    half_c = POOL_GROUP_DIM // N_SHARDS // 2
    pieces = (N_SHARDS, 2, n_groups * half_c, POOL_GROUP_DIM)

    def body(dh_in_ref, dpo_ref, halo_ref, pre_ref, pooled_ref, wp_ref, sc_ref, du_ref, gw_ref, gs_ref):
        i = pl.program_id(0)

        @pl.when(i == 0)
        def _():
            gw_ref[...] = jnp.zeros_like(gw_ref)
            gs_ref[...] = jnp.zeros_like(gs_ref)

        dpo = dpo_ref[...].astype(F32)
        scale = sc_ref[...]
        gs_ref[...] += _fold_rows(dpo * pre_ref[...].astype(F32))
        halo = jnp.where(i < last, halo_ref[...].astype(F32), 0.0)
        dpw = (jnp.concatenate([dpo, halo], axis=0) * scale).astype(BF16)
        pos = i * tm + lax.broadcasted_iota(jnp.int32, (tm + POOL_HALO, 1), 0)
        for g, window in enumerate(POOL_WINDOWS):
            cols = slice(g * POOL_GROUP_DIM, (g + 1) * POOL_GROUP_DIM)
            dpw_g = dpw[:, cols]
            gw = _dot_tn(pooled_ref[:, cols], dpw_g[:tm, :])
            for piece in range(2 * N_SHARDS):
                gw_ref[piece // 2, piece % 2, g * half_c:(g + 1) * half_c, :] += gw[piece * half_c:(piece + 1) * half_c]
            dpooled = _dot_nt(dpw_g, _pool_group_weight(wp_ref, g))
            count = jnp.minimum(pos + 1, window).astype(F32)
            sums = _window_sums(dpooled / count, window, backward=True)
            du_ref[:, cols] = (sums[:tm, :] - dpooled[:tm, :]).astype(BF16)

    row = lambda width, cb=0: pl.BlockSpec((tm, width), lambda i: (i, cb))
    return _pallas(
        body, name="pool_bwd", grid=(seq // tm,),
        in_specs=[ANY, row(D_POOL),
                  pl.BlockSpec((POOL_HALO, D_POOL),
                               lambda i: (jnp.minimum((i + 1) * halo_blocks, seq // POOL_HALO - 1), 0)),
                  row(D_POOL, 1), row(D_POOL),
                  pl.BlockSpec(w_pool_g.shape, lambda i: (0, 0, 0, 0)),
                  pl.BlockSpec((1, D_POOL), lambda i: (0, 0))],
        out_specs=[row(D_POOL, D_QKV // D_POOL),
                   pl.BlockSpec(pieces, lambda i: (0, 0, 0, 0)),
                   pl.BlockSpec((8, D_POOL), lambda i: (0, 0))],
        out_shape=[jax.ShapeDtypeStruct(dh.shape, dh.dtype),
                   jax.ShapeDtypeStruct(pieces, F32),
                   jax.ShapeDtypeStruct((8, D_POOL), F32)],
        input_output_aliases={0: 0},
        compiler_params=_params(("arbitrary",), 40),
    )(dh, dpo, dpo, mixpre, pooled, w_pool_g, pool_scale)


def _sum_patterns(dh, parts, tabs, unrotate, col_block, name, comm=None):
    seq = dh.shape[0]
    tm, tn = 256, D_ATTN
    per = D_ATTN // tn
    d4, d16 = DILATIONS[1], DILATIONS[2]

    def body(dh_in_ref, a1_ref, a4_ref, a16_ref, ct_ref, up_ref, down_ref, o_ref, n4_ref, n16_ref):
        _from_pattern(a4_ref, n4_ref, d4)
        _from_pattern(a16_ref, n16_ref, d16)
        for s in range(tn // HEAD_DIM):
            cols = slice(s * HEAD_DIM, (s + 1) * HEAD_DIM)
            tot = a1_ref[:, cols].astype(F32) + n4_ref[s] + n16_ref[s]
            if unrotate:
                tot = _rotate_heads(tot, ct_ref[...], -up_ref[...], -down_ref[...])
            o_ref[:, cols] = tot.astype(BF16)

    tab = pl.BlockSpec((tm, HEAD_DIM), lambda i, j: (i, 0))
    pat = lambda d: pl.BlockSpec((d, tm // d, tn), lambda i, j: (0, i, j))
    (dh,), exchanged = _call(
        body, name=name, grid=(seq // tm, per),
        in_specs=[ANY, pl.BlockSpec((tm, tn), lambda i, j: (i, j)), pat(d4), pat(d16), tab, tab, tab],
        out_specs=[pl.BlockSpec((tm, tn), lambda i, j: (i, col_block * per + j))],
        out_shape=[jax.ShapeDtypeStruct(dh.shape, dh.dtype)],
        scratch_shapes=[pltpu.VMEM((tn // HEAD_DIM, tm, HEAD_DIM), F32), pltpu.VMEM((tn // HEAD_DIM, tm, HEAD_DIM), F32)],
        semantics=("parallel", "parallel"), vmem_mib=32, args=(dh, parts[0][0], parts[1], parts[2], *tabs),
        aliases={0: 0}, comm=comm)
    return dh, exchanged


def _grad_w_in(x, dh, half, name, comm=None):
    seq = x.shape[0]
    ts, td, te = 2048, D_MODEL // 2, SHARD_IN

    def body(half_ref, x_ref, dh_ref, o_ref):
        k = pl.program_id(1)
        part = _dot_tn(x_ref[...].astype(BF16), dh_ref[...])

        @pl.when(k == 0)
        def _():
            o_ref[...] = part

        @pl.when(k > 0)
        def _():
            o_ref[...] += part

    (g,), exchanged = _call(
        body, name=name, grid=(N_SHARDS, seq // ts),
        in_specs=[pl.BlockSpec((ts, td), lambda e, k, half_ref: (k, half_ref[0])),
                  pl.BlockSpec((ts, te), lambda e, k, half_ref: (k, e))],
        out_specs=[pl.BlockSpec((None, td, te), lambda e, k, half_ref: (e, 0, 0))],
        out_shape=[jax.ShapeDtypeStruct((N_SHARDS, td, te), F32)],
        scratch_shapes=[], semantics=("parallel", "arbitrary"), vmem_mib=56, args=(x, dh), comm=comm,
        prefetch=(half,))
    return g, exchanged


def _grad_w_out(y, dzb):
    seq = y.shape[0]
    ts, te = 2048, 1024

    def body(y_ref, dz_ref, o_ref):
        k = pl.program_id(1)
        part = _dot_tn(y_ref[...], dz_ref[...])

        @pl.when(k == 0)
        def _():
            o_ref[...] = part

        @pl.when(k > 0)
        def _():
            o_ref[...] += part

    return _pallas(
        body, name="grad_w_out", grid=(D_MODEL // te, seq // ts),
        in_specs=[pl.BlockSpec((ts, te), lambda e, k: (k, e)), pl.BlockSpec((ts, D_MODEL), lambda e, k: (k, 0))],
        out_specs=pl.BlockSpec((te, D_MODEL), lambda e, k: (e, 0)),
        out_shape=jax.ShapeDtypeStruct((D_MODEL, D_MODEL), F32),
        compiler_params=_params(("parallel", "arbitrary"), 56),
    )(y, dzb)


GRAD_X_LATE_SHARDS = 1
GRAD_X_PARTIAL_ROWS = 512


def _grad_x_partial(dh, w_in_g, dz, first, tiles, prev=None, comm=None):
    seq = dh.shape[0]
    tm, tk = GRAD_X_PARTIAL_ROWS, SHARD_IN

    def body(*refs):
        dh_ref, w_ref, dz_ref, o_ref = refs[-4:]
        k = pl.program_id(1)
        part = _dot_nt(dh_ref[...], w_ref[...])

        @pl.when(k == 0)
        def _():
            o_ref[...] = DEEPNORM_ALPHA * dz_ref[...] + part

        @pl.when(k > 0)
        def _():
            o_ref[...] += part

    carried = [] if prev is None else [prev]
    row = pl.BlockSpec((tm, D_MODEL), lambda i, k: (i + first, 0))
    (partial,), exchanged = _call(
        body, name="grad_x_partial_%d" % first, grid=(tiles, N_SHARDS - GRAD_X_LATE_SHARDS),
        in_specs=[ANY] * len(carried) + [
            pl.BlockSpec((tm, tk), lambda i, k: (i + first, k)),
            pl.BlockSpec((None, D_MODEL, tk), lambda i, k: (k, 0, 0)), row],
        out_specs=[row],
        out_shape=[jax.ShapeDtypeStruct((seq, D_MODEL), F32)],
        scratch_shapes=[], semantics=("parallel", "arbitrary"), vmem_mib=48, args=(*carried, dh, w_in_g, dz),
        aliases={0: 0} if carried else None, comm=comm)
    return partial, exchanged


def _grad_x_final(dh, w_in_g, partial):
    seq = dh.shape[0]
    tm, tk = 512, SHARD_IN
    k0 = N_SHARDS - GRAD_X_LATE_SHARDS

    def body(dh_ref, w_ref, p_ref, o_ref):
        k = pl.program_id(1)
        part = _dot_nt(dh_ref[...], w_ref[...])

        @pl.when(k == 0)
        def _():
            o_ref[...] = p_ref[...] + part

        @pl.when(k > 0)
        def _():
            o_ref[...] += part

    row = pl.BlockSpec((tm, D_MODEL), lambda i, k: (i, 0))
    return _pallas(
        body, name="grad_x_final", grid=(seq // tm, GRAD_X_LATE_SHARDS),
        in_specs=[pl.BlockSpec((tm, tk), lambda i, k: (i, k + k0)),
                  pl.BlockSpec((None, D_MODEL, tk), lambda i, k: (k + k0, 0, 0)), row],
        out_specs=row, out_shape=jax.ShapeDtypeStruct((seq, D_MODEL), F32),
        compiler_params=_params(("parallel", "arbitrary"), 48),
    )(dh, w_in_g, partial)


def _pool_weight(w_pool_sh):
    n_groups = len(POOL_WINDOWS)
    shard_c = POOL_GROUP_DIM // N_SHARDS
    return w_pool_sh.reshape(N_SHARDS, n_groups, shard_c, POOL_GROUP_DIM)


def _step(x, target, w_bufs, pool_scale, gain, bias, place):
    seq = x.shape[0]
    tabs = _rope_tables(seq)
    core, chip_core, onward, plan = place
    qkv, hug, w_in_g, w_out_g, w_pool_sh = _in_proj_gathering(x, w_bufs, tabs, plan)
    o_list, st_list = [], []
    for p, dil in enumerate(DILATIONS):
        o, st = _attn_fwd(qkv[p], "attn_fwd_d%d" % dil)
        o_list.append(o)
        st_list.append(st)
    w_pool_g = _pool_weight(w_pool_sh)
    y, mixpre, lse_all, pooled = _mix_gate(o_list, st_list, hug, w_pool_g, pool_scale)
    dz, dzb, gain_part, bias_part, loss_part = _out_proj_loss(y, w_out_g, x, target, gain, bias)
    dh, dpo, do_list, stat_list = _dy_gate_bwd(dzb, w_out_g, hug, mixpre, pool_scale, lse_all)
    g_w_out = _grad_w_out(y, dzb)
    dh, g_w_pool, scale_part = _pool_bwd(dh, dpo, mixpre, pooled, w_pool_g, pool_scale)
    small = jnp.concatenate([scale_part, gain_part, bias_part, loss_part], axis=1)
    early = [g_w_out.reshape(N_SHARDS, 2, D_MODEL // (2 * N_SHARDS), D_MODEL), g_w_pool]

    bwd = lambda p, comm: _attn_bwd(qkv[p], do_list[p], stat_list[p], "attn_bwd_d%d" % DILATIONS[p], comm)
    part_a, halves = bwd(0, _exchange_halves(early))
    sums_b = [_add_own_half(g, h, core, "add_own_half_%d" % a) for a, (g, h) in enumerate(zip(early, halves))]
    part_b, recv = bwd(1, _scatter_to_chips(sums_b))
    bufs = [_add_chips([g, h], r, chip_core, "add_chips_%d" % a)
            for a, (g, h, r) in enumerate(zip(early, halves, recv))]
    part_c, reduced = bwd(2, _share_with_sibling(bufs))
    parts = [part_a, part_b, part_c]
    dh, gathered = _sum_patterns(dh, [t[0] for t in parts], tabs, True, 0, "sum_dq", _gather_small(small))
    dh, _ = _sum_patterns(dh, [t[1] for t in parts], tabs, True, 1, "sum_dk")
    dh, _ = _sum_patterns(dh, [t[2] for t in parts], tabs, False, 2, "sum_dv")

    give, _ = _grad_w_in(x, dh, 1 - core, "grad_w_in_give")
    keep, recv = _grad_w_in(x, dh, core, "grad_w_in_keep", _send_to_sibling([give]))
    total = [keep, recv[0]]
    total_b = _add_pair(keep, recv[0], "add_own_half_w_in")
    n_tiles = seq // GRAD_X_PARTIAL_ROWS
    tiles = 3 * n_tiles // 8
    part, relayed = _grad_x_partial(dh, w_in_g, dz, 0, tiles, None, _relay_diagonal(total_b))
    total_b = _fold_relayed(total, total_b, relayed[0], onward)
    part, recv = _grad_x_partial(dh, w_in_g, dz, tiles, n_tiles - tiles, part, _scatter_to_neighbours(total_b))
    buf = _add_chips(total, recv[0], chip_core, "add_chips_w_in")
    g_x = _grad_x_final(dh, w_in_g, part)
    g_w_in = _run_exchange(_share_with_sibling([buf]), "share_w_in")[0]
    return g_x, g_w_in, reduced[0], reduced[1], small, gathered[0]


def _exchange_halves(grads):
    n = len(grads)

    def copies(src, dst, sems):
        x, y, c, _ = _mesh_place()
        return [_remote(src[a].at[j, 1 - c], dst[a].at[j], sems[0].at[a, j], sems[1].at[a, j], (x, y, 1 - c))
                for a in range(n) for j in range(N_SHARDS)]

    def start(src, dst, sems):
        for cp in copies(src, dst, sems):
            cp.start()

    def finish(src, dst, sems):
        for cp in copies(src, dst, sems):
            cp.wait()

    return _Exchange(grads, [jax.ShapeDtypeStruct((N_SHARDS,) + g.shape[2:], g.dtype) for g in grads], {},
                     [pltpu.SemaphoreType.DMA((n, N_SHARDS))] * 2, start, finish)


def _add_own_half(grad, recv, core, name):
    _, _, r, c = grad.shape
    tr = min(r, 256)

    def body(core_ref, g_ref, r_ref, ob_ref):
        ob_ref[...] = (g_ref[...] + r_ref[...]).astype(BF16)

    return _pallas(
        body, name=name,
        grid_spec=pltpu.PrefetchScalarGridSpec(
            num_scalar_prefetch=1, grid=(N_SHARDS, r // tr),
            in_specs=[pl.BlockSpec((None, None, tr, c), lambda j, i, core_ref: (j, core_ref[0], i, 0)),
                      pl.BlockSpec((None, tr, c), lambda j, i, core_ref: (j, i, 0))],
            out_specs=pl.BlockSpec((None, tr, c), lambda j, i, core_ref: (j, i, 0))),
        out_shape=jax.ShapeDtypeStruct((N_SHARDS, r, c), BF16),
        compiler_params=_params(("parallel", "parallel"), 32),
    )(core, grad, recv)


def _send_to_sibling(arrays):
    n = len(arrays)

    def copies(src, dst, sems):
        x, y, c, _ = _mesh_place()
        return [_remote(src[a], dst[a], sems[0].at[a], sems[1].at[a], (x, y, 1 - c)) for a in range(n)]

    def start(src, dst, sems):
        for cp in copies(src, dst, sems):
            cp.start()

    def finish(src, dst, sems):
        for cp in copies(src, dst, sems):
            cp.wait()

    return _Exchange(arrays, [jax.ShapeDtypeStruct(t.shape, t.dtype) for t in arrays], {},
                     [pltpu.SemaphoreType.DMA((n,))] * 2, start, finish)


def _add_pair(a, b, name):
    _, r, c = a.shape
    tr = min(r, 256)

    def body(a_ref, b_ref, ob_ref):
        ob_ref[...] = (a_ref[...] + b_ref[...]).astype(BF16)

    spec = pl.BlockSpec((None, tr, c), lambda j, i: (j, i, 0))
    return _pallas(
        body, name=name, grid=(N_SHARDS, r // tr), in_specs=[spec, spec], out_specs=spec,
        out_shape=jax.ShapeDtypeStruct(a.shape, BF16),
        compiler_params=_params(("parallel", "parallel"), 32),
    )(a, b)


def _scatter_to_chips(sums):
    n = len(sums)

    def copies(src, dst, sems):
        x, y, c, chips = _mesh_place()
        return [_remote(src[a].at[2 * cx + cy], dst[a].at[k], sems[0].at[a, k], sems[1].at[a, k], (cx, cy, c))
                for a in range(n) for k, (cx, cy) in enumerate(chips)]

    def start(src, dst, sems):
        for cp in copies(src, dst, sems):
            cp.start()

    def finish(src, dst, sems):
        for cp in copies(src, dst, sems):
            cp.wait()

    return _Exchange(sums, [jax.ShapeDtypeStruct((3,) + s.shape[1:], s.dtype) for s in sums], {},
                     [pltpu.SemaphoreType.DMA((n, 3))] * 2, start, finish)


def _add_chips(sums, recv, chip_core, name):
    r, c = sums[0].shape[-2:]
    n_sums, n_recv = len(sums), recv.shape[0]
    tr = min(r, 256)
    mine = {3: pl.BlockSpec((None, tr, c), lambda i, cc_ref: (cc_ref[0], i, 0)),
            4: pl.BlockSpec((None, None, tr, c), lambda i, cc_ref: (cc_ref[0], cc_ref[1], i, 0))}

    def body(cc_ref, *refs):
        r_ref, o_ref = refs[n_sums:]
        tot = refs[0][...]
        for s_ref in refs[1:n_sums]:
            tot = tot + s_ref[...]
        for k in range(n_recv):
            tot = tot + r_ref[k].astype(F32)
        o_ref[...] = tot

    return _pallas(
        body, name=name,
        grid_spec=pltpu.PrefetchScalarGridSpec(
            num_scalar_prefetch=1, grid=(r // tr,),
            in_specs=[mine[s.ndim] for s in sums] + [pl.BlockSpec((n_recv, tr, c), lambda i, cc_ref: (0, i, 0))],
            out_specs=pl.BlockSpec((None, tr, c), lambda i, cc_ref: (cc_ref[1], i, 0))),
        out_shape=jax.ShapeDtypeStruct((2, r, c), F32),
        compiler_params=_params(("parallel",), 32),
    )(chip_core, *sums, recv)


def _relay_diagonal(sums_b):
    def copy(src, dst, sems):
        x, y, c, _ = _mesh_place()
        diagonal = 2 * (1 - x) + (1 - y)
        return _remote(src[0].at[diagonal], dst[0], sems[0].at[0], sems[1].at[0], (x ^ (1 - c), y ^ c, c))

    def start(src, dst, sems):
        copy(src, dst, sems).start()

    def finish(src, dst, sems):
        copy(src, dst, sems).wait()

    return _Exchange([sums_b], [jax.ShapeDtypeStruct(sums_b.shape[1:], sums_b.dtype)], {},
                     [pltpu.SemaphoreType.DMA((1,))] * 2, start, finish)


def _fold_relayed(sums, sums_b, relayed, onward):
    _, r, c = sums[0].shape
    n_sums = len(sums)
    tr = min(r, 256)

    def body(on_ref, b_in_ref, *refs):
        r_ref, o_ref = refs[n_sums:]
        tot = refs[0][...]
        for s_ref in refs[1:n_sums]:
            tot = tot + s_ref[...]
        o_ref[...] = (tot + r_ref[...].astype(F32)).astype(BF16)

    return _pallas(
        body, name="fold_relayed",
        grid_spec=pltpu.PrefetchScalarGridSpec(
            num_scalar_prefetch=1, grid=(r // tr,),
            in_specs=[ANY] + [pl.BlockSpec((None, tr, c), lambda i, on_ref: (on_ref[0], i, 0))] * n_sums
            + [pl.BlockSpec((tr, c), lambda i, on_ref: (i, 0))],
            out_specs=pl.BlockSpec((None, tr, c), lambda i, on_ref: (on_ref[0], i, 0))),
        out_shape=jax.ShapeDtypeStruct(sums_b.shape, sums_b.dtype),
        input_output_aliases={1: 0},
        compiler_params=_params(("parallel",), 32),
    )(onward, sums_b, *sums, relayed)


def _scatter_to_neighbours(sums_b):
    def copies(src, dst, sems):
        x, y, c, chips = _mesh_place()
        return [_remote(src[0].at[2 * cx + cy], dst[0].at[k], sems[0].at[k], sems[1].at[k], (cx, cy, c))
                for k, (cx, cy) in enumerate(chips[:2])]

    def start(src, dst, sems):
        for cp in copies(src, dst, sems):
            cp.start()

    def finish(src, dst, sems):
        for cp in copies(src, dst, sems):
            cp.wait()

    return _Exchange([sums_b], [jax.ShapeDtypeStruct((2,) + sums_b.shape[1:], sums_b.dtype)], {},
                     [pltpu.SemaphoreType.DMA((2,))] * 2, start, finish)


def _share_with_sibling(bufs):
    n = len(bufs)

    def copies(dst, sems, half):
        x, y, c, _ = _mesh_place()
        h = c if half == "mine" else 1 - c
        return [_remote(dst[a].at[h], dst[a].at[h], sems[0].at[a], sems[1].at[a], (x, y, 1 - c)) for a in range(n)]

    def start(ins, dst, sems):
        for cp in copies(dst, sems, "mine"):
            cp.start()

    def finish(ins, dst, sems):
        for cp in copies(dst, sems, "theirs"):
            cp.wait_recv()
        for cp in copies(dst, sems, "mine"):
            cp.wait_send()

    return _Exchange(bufs, [jax.ShapeDtypeStruct(b.shape, b.dtype) for b in bufs], {a: a for a in range(n)},
                     [pltpu.SemaphoreType.DMA((n,))] * 2, start, finish)


def _adam_math(w, g, m, v):
    m = ADAM_B1 * m + (1.0 - ADAM_B1) * g
    v = ADAM_B2 * v + (1.0 - ADAM_B2) * (g * g)
    m_hat = m / (1.0 - ADAM_B1 ** ADAM_STEP)
    v_hat = v / (1.0 - ADAM_B2 ** ADAM_STEP)
    delta = -ADAM_LR * (m_hat / (jnp.sqrt(v_hat) + ADAM_EPS) + ADAM_WD * w)
    return delta, m, v


def _gather_small(small):
    def peers():
        x, y, c, _ = _mesh_place()
        return [(x ^ ((r >> 2) & 1), y ^ ((r >> 1) & 1), c ^ (r & 1)) for r in range(1, 8)], 4 * x + 2 * y + c

    def start(src, dst, sems):
        to, me = peers()
        for r, peer in enumerate(to):
            _remote(src[0], dst[0].at[me], sems[0].at[r], sems[1].at[r], peer).start()

    def finish(src, dst, sems):
        to, me = peers()
        for r, (px, py, pc) in enumerate(to):
            theirs = dst[0].at[4 * px + 2 * py + pc]
            _remote(theirs, theirs, sems[0].at[r], sems[1].at[r], (px, py, pc)).wait_recv()
        for r, peer in enumerate(to):
            _remote(src[0], dst[0].at[me], sems[0].at[r], sems[1].at[r], peer).wait_send()

    return _Exchange([small], [jax.ShapeDtypeStruct((8,) + small.shape, small.dtype)], {},
                     [pltpu.SemaphoreType.DMA((7,))] * 2, start, finish)


def _small_adamw(gathered, small, me, w_vecs, m_vecs, v_vecs):
    n = len(w_vecs)
    widths = [w.shape[1] for w in w_vecs]
    n_par = sum(widths)

    def body(me_ref, a_ref, s_ref, *refs):
        w_refs, m_refs, v_refs = refs[:n], refs[n:2 * n], refs[2 * n:3 * n]
        loss_ref, outs = refs[3 * n], refs[3 * n + 1:]
        mine = s_ref[...]
        tot = jnp.where(me_ref[0] == 0, mine, a_ref[0])
        for d in range(1, 8):
            tot = tot + jnp.where(me_ref[0] == d, mine, a_ref[d])
        tot = jnp.sum(tot, axis=0, keepdims=True)
        sq = jnp.sum(tot[:, n_par:], axis=1, keepdims=True)
        loss_ref[...] = jnp.broadcast_to(sq * (0.5 / D_MODEL), loss_ref.shape)
        lo = 0
        for k in range(n):
            g = tot[:, lo:lo + widths[k]]
            lo += widths[k]
            outs[k][...] = g
            outs[n + k][...], outs[2 * n + k][...], outs[3 * n + k][...] = _adam_math(
                w_refs[k][...], g, m_refs[k][...], v_refs[k][...])

    vm = pl.BlockSpec(memory_space=pltpu.VMEM)
    vecs = [jax.ShapeDtypeStruct((1, w), F32) for w in widths] * 4
    res = pl.pallas_call(
        body, name="small_adamw",
        grid_spec=pltpu.PrefetchScalarGridSpec(num_scalar_prefetch=1, grid=(), in_specs=[vm] * (2 + 3 * n),
                                               out_specs=[vm] * (1 + 4 * n)),
        out_shape=[jax.ShapeDtypeStruct((1, 128), F32)] + vecs,
    )(me, gathered, small, *w_vecs, *m_vecs, *v_vecs)
    return res[0], res[1:1 + n], res[1 + n:1 + 2 * n], res[1 + 2 * n:1 + 3 * n], res[1 + 3 * n:]


def _adamw(w, g, m, v, name):
    r, c = w.shape
    tr = min(r, 256)

    def body(w_ref, g_ref, m_ref, v_ref, go_ref, d_ref, nm_ref, nv_ref):
        g = g_ref[...]
        go_ref[...] = g
        d_ref[...], nm_ref[...], nv_ref[...] = _adam_math(w_ref[...], g, m_ref[...], v_ref[...])

    spec = pl.BlockSpec((tr, c), lambda i: (i, 0))
    shape = jax.ShapeDtypeStruct((r, c), F32)
    return _pallas(
        body, name=name, grid=(r // tr,),
        in_specs=[spec] * 4, out_specs=[spec] * 4, out_shape=[shape] * 4,
        compiler_params=_params(("parallel",), 48),
    )(w, g, m, v)


def kernel(x, w_in, w_pool, pool_scale, w_out, ln_gain, ln_bias, loss_target, m_w_in, m_w_pool, m_pool_scale, m_w_out, m_ln_gain, m_ln_bias, v_w_in, v_w_pool, v_pool_scale, v_w_out, v_ln_gain, v_ln_bias):
    xi, yi, ci = lax.axis_index("x"), lax.axis_index("y"), lax.axis_index("c")
    chip = (2 * xi + yi).astype(jnp.int32).reshape(1)
    core = ci.astype(jnp.int32).reshape(1)
    n_groups = len(POOL_WINDOWS)
    shard_c = w_pool.shape[2]

    w_in_b = _cast_bf16(w_in[0], chip, "cast_w_in", 256)
    w_out_b = _cast_bf16(w_out[0], chip, "cast_w_out", 256)
    w_pool_b = _cast_bf16(w_pool[0].reshape(n_groups * shard_c, POOL_GROUP_DIM), chip, "cast_w_pool", 256)

    chip_core = jnp.concatenate([chip, core])
    onward = (2 * (xi ^ ci) + (yi ^ (1 - ci))).astype(jnp.int32).reshape(1)
    g_x, full_in, full_out, full_pool, small, small_all = _step(
        x[0], loss_target[0], [w_in_b, w_out_b, w_pool_b], pool_scale, ln_gain, ln_bias,
        (core, chip_core, onward, _in_proj_plan(xi, yi)))
    half_c = shard_c // 2
    grad_w_in = full_in.reshape(D_MODEL, SHARD_IN)
    grad_w_out = full_out.reshape(D_MODEL // N_SHARDS, D_MODEL)
    grad_w_pool = (full_pool.reshape(2, n_groups, half_c, POOL_GROUP_DIM).transpose(1, 0, 2, 3)
                   .reshape(n_groups * shard_c, POOL_GROUP_DIM))

    grad_w_in, d_in, nm_in, nv_in = _adamw(w_in[0], grad_w_in, m_w_in[0], v_w_in[0], "adamw_w_in")
    grad_w_out, d_out, nm_out, nv_out = _adamw(w_out[0], grad_w_out, m_w_out[0], v_w_out[0], "adamw_w_out")
    flat = lambda t: t[0].reshape(n_groups * shard_c, POOL_GROUP_DIM)
    grad_w_pool, d_pool, nm_pool, nv_pool = _adamw(flat(w_pool), grad_w_pool, flat(m_w_pool), flat(v_w_pool),
                                                   "adamw_w_pool")

    me = (4 * xi + 2 * yi + ci).astype(jnp.int32).reshape(1)
    loss_v, g_vecs, d_vecs, nm_vecs, nv_vecs = _small_adamw(
        small_all, small, me, [pool_scale, ln_gain, ln_bias], [m_pool_scale, m_ln_gain, m_ln_bias],
        [v_pool_scale, v_ln_gain, v_ln_bias])
    g_scale, g_gain, g_bias = g_vecs
    d_scale, d_gain, d_bias = d_vecs
    nm_scale, nm_gain, nm_bias = nm_vecs
    nv_scale, nv_gain, nv_bias = nv_vecs
    pool_shape = w_pool.shape
    return (loss_v[0, 0], g_x[None],
            grad_w_in[None], grad_w_pool.reshape(pool_shape), g_scale, grad_w_out[None], g_gain, g_bias,
            d_in[None], d_pool.reshape(pool_shape), d_scale, d_out[None], d_gain, d_bias,
            nm_in[None], nm_pool.reshape(pool_shape), nm_scale, nm_out[None], nm_gain, nm_bias,
            nv_in[None], nv_pool.reshape(pool_shape), nv_scale, nv_out[None], nv_gain, nv_bias)
```

```python
import functools

import jax
import jax.numpy as jnp
import numpy as np
from jax import lax
from jax.experimental import pallas as pl
from jax.experimental.pallas import tpu as pltpu

F32 = jnp.float32
BF16 = jnp.bfloat16
MESH = pl.DeviceIdType.MESH
ANY = pl.BlockSpec(memory_space=pl.ANY)

D_MODEL = 2048
D_ATTN = 1024
D_POOL = 1024
HEAD_DIM = 128
N_HEADS = 8
ROPE_DIM = 32
ROPE_THETA = 500000.0
DILATIONS = (1, 4, 16)
KEY_BLOCK = 128
CHUNK = 2 * KEY_BLOCK
STAT_LANES = 128
POOL_WINDOWS = (2, 4, 8, 16)
POOL_GROUP_DIM = 256
POOL_HALO = 16
D_QKV = 3 * D_ATTN
D_UG = D_POOL + D_MODEL
D_IN = D_QKV + D_UG
N_SHARDS = 4
SHARD_IN = D_IN // N_SHARDS
LN_EPS = 1e-5
DEEPNORM_ALPHA = 2.0 ** 0.25
ADAM_LR = 0.001
ADAM_B1 = 0.9
ADAM_B2 = 0.999
ADAM_EPS = 1e-08
ADAM_WD = 0.01
ADAM_STEP = 10
NEG = -1e30
MIB = 1024 * 1024


def _params(sem, vmem_mib):
    return pltpu.CompilerParams(dimension_semantics=sem, vmem_limit_bytes=vmem_mib * MIB)


def _pallas(body, **kwargs):
    pin = lambda s: pltpu.HBM(s.shape, s.dtype) if len(s.shape) >= 2 else s
    out_shape = kwargs.pop("out_shape")
    out_shape = [pin(s) for s in out_shape] if isinstance(out_shape, (list, tuple)) else pin(out_shape)
    call = pl.pallas_call(body, out_shape=out_shape, **kwargs)

    def run(*operands):
        return call(*[pltpu.with_memory_space_constraint(o, pltpu.HBM) if o.ndim >= 2 else o for o in operands])

    return run


class _Exchange:
    def __init__(self, operands, out_shape, aliases, sems, start, finish):
        self.operands, self.out_shape, self.aliases, self.sems = list(operands), list(out_shape), dict(aliases), list(sems)
        self.start, self.finish = start, finish


def _run_exchange(comm, name):
    n_in, n_out = len(comm.operands), len(comm.out_shape)

    def body(*refs):
        ins, outs, sems = refs[:n_in], refs[n_in:n_in + n_out], refs[n_in + n_out:]
        comm.start(ins, outs, sems)
        comm.finish(ins, outs, sems)

    return _pallas(
        body, name=name, in_specs=[ANY] * n_in, out_specs=[ANY] * n_out, out_shape=comm.out_shape,
        input_output_aliases=comm.aliases, scratch_shapes=comm.sems,
    )(*comm.operands)


def _call(body, *, name, grid, in_specs, out_specs, out_shape, scratch_shapes, semantics, vmem_mib, args,
          aliases=None, comm=None, prefetch=()):
    aliases = dict(aliases or {})
    n_pre, n_in, n_out, n_scr = len(prefetch), len(in_specs), len(out_specs), len(scratch_shapes)
    c_in, c_out = (len(comm.operands), len(comm.out_shape)) if comm else (0, 0)
    c_shapes, c_sems, c_operands = (comm.out_shape, comm.sems, comm.operands) if comm else ([], [], [])

    def hosted(*refs):
        pre, refs = refs[:n_pre], refs[n_pre:]
        a = n_in
        b = a + c_in
        c = b + n_out
        d = c + c_out
        e = d + n_scr
        if comm is None:
            body(*pre, *refs)
            return
        ids = [pl.program_id(k) for k in range(len(grid))]
        first = functools.reduce(jnp.logical_and, [i == 0 for i in ids])
        last = functools.reduce(jnp.logical_and, [i == g - 1 for i, g in zip(ids, grid)])

        @pl.when(first)
        def _():
            comm.start(refs[a:b], refs[c:d], refs[e:])

        body(*pre, *refs[:a], *refs[b:c], *refs[d:e])

        @pl.when(last)
        def _():
            comm.finish(refs[a:b], refs[c:d], refs[e:])

    if comm:
        semantics = ("arbitrary",) * len(grid)
        for i, o in comm.aliases.items():
            aliases[n_pre + n_in + i] = n_out + o
    outs = _pallas(
        hosted, name=name,
        grid_spec=pltpu.PrefetchScalarGridSpec(
            num_scalar_prefetch=n_pre, grid=grid, in_specs=list(in_specs) + [ANY] * c_in,
            out_specs=list(out_specs) + [ANY] * c_out, scratch_shapes=list(scratch_shapes) + c_sems),
        out_shape=list(out_shape) + c_shapes, input_output_aliases=aliases,
        compiler_params=_params(semantics, vmem_mib),
    )(*prefetch, *args, *c_operands)
    return list(outs[:n_out]), list(outs[n_out:])


def _dot_nn(a, b):
    return jnp.dot(a, b, preferred_element_type=F32)


def _dot_nt(a, b):
    return lax.dot_general(a, b, (((1,), (1,)), ((), ())), preferred_element_type=F32)


def _dot_tn(a, b):
    return lax.dot_general(a, b, (((0,), (0,)), ((), ())), preferred_element_type=F32)


def _fold_rows(a):
    r, c = a.shape
    return jnp.sum(a.reshape(r // 8, 8, c), axis=0)


def _cast_bf16(a, chip, name, rows):
    r, c = a.shape

    def body(chip_ref, a_ref, o_ref):
        o_ref[...] = a_ref[...].astype(BF16)

    return _pallas(
        body, name=name,
        grid_spec=pltpu.PrefetchScalarGridSpec(
            num_scalar_prefetch=1, grid=(r // rows,),
            in_specs=[pl.BlockSpec((rows, c), lambda i, chip_ref: (i, 0))],
            out_specs=pl.BlockSpec((None, rows, c), lambda i, chip_ref: (chip_ref[0], i, 0))),
        out_shape=jax.ShapeDtypeStruct((N_SHARDS, r, c), BF16),
        compiler_params=_params(("parallel",), 32),
    )(chip, a)


def _mesh_place():
    x, y, c = lax.axis_index("x"), lax.axis_index("y"), lax.axis_index("c")
    return x, y, c, [(1 - x, y), (x, 1 - y), (1 - x, 1 - y)]


def _remote(src, dst, send_sem, recv_sem, to):
    return pltpu.make_async_remote_copy(src_ref=src, dst_ref=dst, send_sem=send_sem, recv_sem=recv_sem,
                                        device_id=to, device_id_type=MESH)


def _rope_tables(seq):
    half = ROPE_DIM // 2
    inv_freq = (np.float64(ROPE_THETA) ** (-(2.0 * np.arange(half, dtype=np.float64)) / ROPE_DIM)).astype(np.float32)
    ang = np.arange(seq, dtype=np.float32)[:, None] * inv_freq[None, :]
    cos = np.cos(ang.astype(np.float64)).astype(np.float32)
    sin = np.sin(ang.astype(np.float64)).astype(np.float32)
    pad = np.zeros((seq, HEAD_DIM - ROPE_DIM), np.float32)
    zeros = np.zeros((seq, half), np.float32)
    c_tab = np.concatenate([cos, cos, pad + 1.0], axis=1)
    up_tab = np.concatenate([-sin, zeros, pad], axis=1)
    down_tab = np.concatenate([zeros, sin, pad], axis=1)
    return jnp.asarray(c_tab), jnp.asarray(up_tab), jnp.asarray(down_tab)


def _rotate_heads(t, c_tab, up_tab, down_tab):
    outs = []
    for h in range(t.shape[1] // HEAD_DIM):
        th = t[:, h * HEAD_DIM:(h + 1) * HEAD_DIM]
        up = pltpu.roll(th, HEAD_DIM - ROPE_DIM // 2, axis=1)
        down = pltpu.roll(th, ROPE_DIM // 2, axis=1)
        outs.append(th * c_tab + up * up_tab + down * down_tab)
    return outs[0] if len(outs) == 1 else jnp.concatenate(outs, axis=1)


def _to_pattern(slabs_ref, dst_ref, dil, dtype):
    n_slabs, rows, _ = slabs_ref.shape
    for s in range(n_slabs):
        for r in range(dil):
            dst_ref[r, :, s * 128:(s + 1) * 128] = slabs_ref[s, pl.ds(r, rows // dil, dil), :].astype(dtype)


def _from_pattern(src_ref, slabs_ref, dil):
    n_slabs, rows, _ = slabs_ref.shape
    for s in range(n_slabs):
        for r in range(dil):
            slabs_ref[s, pl.ds(r, rows // dil, dil), :] = src_ref[r, :, s * 128:(s + 1) * 128].astype(F32)


def _store_slabs(slabs_ref, value):
    for s in range(slabs_ref.shape[0]):
        slabs_ref[s] = value[:, s * 128:(s + 1) * 128]


W_IN_CHUNKS = 4


def _in_proj_plan(x, y):
    shards = [2 * x + y, 2 * (1 - x) + y, 2 * x + (1 - y), 2 * (1 - x) + (1 - y)]
    last_row = jnp.int32(-2)

    def table(active, col_of):
        cols, rows = [], []
        first_col = functools.reduce(lambda acc, j: jnp.where(active[j], col_of(shards[j]), acc), reversed(range(4)),
                                     jnp.int32(0))
        held_col, seen = first_col, jnp.bool_(False)
        for j in range(4):
            cols.append(jnp.where(active[j], col_of(shards[j]), held_col))
            rows.append(jnp.where(active[j], -1, jnp.where(seen, last_row, 0)))
            held_col = jnp.where(active[j], col_of(shards[j]), held_col)
            seen = jnp.logical_or(seen, active[j])
        return cols, rows

    q_cols, q_rows = table([s < 2 for s in shards], lambda s: s)
    h_cols, h_rows = table([s >= 2 for s in shards], lambda s: s - 2)
    return jnp.stack([jnp.asarray(v, jnp.int32) for v in shards + q_cols + q_rows + h_cols + h_rows])


def _in_proj_gathering(x, w_bufs, tabs, plan):
    seq = x.shape[0]
    tm, tn = 512, SHARD_IN
    n_tiles = seq // tm
    heads = tn // HEAD_DIM
    k_heads_in_second = 2 * D_ATTN // HEAD_DIM - heads
    d4, d16 = DILATIONS[1], DILATIONS[2]
    DIAGONAL = 2
    chunk = D_MODEL // 2 // W_IN_CHUNKS
    early = [(0, D_MODEL // 2, q * chunk, chunk) for q in range(W_IN_CHUNKS)]
    late = [(a, w_bufs[a].shape[1] // 2, 0, w_bufs[a].shape[1] // 2) for a in (1, 2)]
    pieces = early + late
    early_ids, late_ids = range(len(early)), range(len(early), len(pieces))

    def body(plan_ref, x_ref, w_in_in, w_out_in, w_pool_in, c_ref, up_ref, down_ref,
             o1_ref, o4_ref, o16_ref, hug_ref, w_ref, w_out_ref, w_pool_ref,
             wbuf_ref, res_ref, w_sem, ici_send, ici_recv, d2d_send, d2d_recv):
        j, i = pl.program_id(0), pl.program_id(1)
        mx, my, mc, chips = _mesh_place()
        sibling = (mx, my, 1 - mc)
        gathered = (w_ref, w_out_ref, w_pool_ref)
        chip_of = lambda k: 2 * chips[k][0] + chips[k][1]

        def piece(n, chip, core):
            a, per_core, offset, size = pieces[n]
            return gathered[a].at[chip, pl.ds(core * per_core + offset, size)]

        def to_neighbour(k, n):
            mine = piece(n, 2 * mx + my, mc)
            return _remote(mine, mine, ici_send.at[n, k], ici_recv.at[n, k], (*chips[k], mc))

        def relay(n):
            theirs = piece(n, 2 * (mx ^ (1 - mc)) + (my ^ mc), mc)
            return _remote(theirs, theirs, ici_send.at[n, DIAGONAL], ici_recv.at[n, DIAGONAL], (mx ^ mc, my ^ (1 - mc), mc))

        def arrival(k, n):
            theirs = piece(n, chip_of(k), mc)
            return _remote(theirs, theirs, ici_send.at[n, k], ici_recv.at[n, k], (*chips[k], mc))

        def to_sibling(k, n, core):
            theirs = piece(n, chip_of(k), core)
            return _remote(theirs, theirs, d2d_send.at[n, k], d2d_recv.at[n, k], sibling)

        def take(k, ids):
            for n in ids:
                arrival(k, n).wait_recv()
                to_sibling(k, n, mc).start()

        def taken(k, ids):
            for n in ids:
                to_sibling(k, n, 1 - mc).wait_recv()

        first_tile = i == 0

        @pl.when(jnp.logical_and(j == 0, first_tile))
        def _():
            for n in range(len(pieces)):
                for k in range(DIAGONAL):
                    to_neighbour(k, n).start()

        @pl.when(jnp.logical_and(j == 1, first_tile))
        def _():
            take(0, early_ids)
            taken(0, early_ids)

        @pl.when(jnp.logical_and(j == 2, first_tile))
        def _():
            take(1, early_ids)
            for n in early_ids:
                relay(n).start()
            taken(1, early_ids)
            for k in range(DIAGONAL):
                take(k, late_ids)
            for n in late_ids:
                relay(n).start()
            for k in range(DIAGONAL):
                taken(k, late_ids)

        @pl.when(jnp.logical_and(j == 3, first_tile))
        def _():
            take(DIAGONAL, range(len(pieces)))
            taken(DIAGONAL, range(len(pieces)))

        shard = plan_ref[j]

        @pl.when(first_tile)
        def _():
            cp = pltpu.make_async_copy(w_ref.at[shard], wbuf_ref, w_sem)
            cp.start()
            cp.wait()

        xb = x_ref[...].astype(BF16)
        group = 4 * HEAD_DIM
        accs = [_dot_nn(xb, wbuf_ref[:, g * group:(g + 1) * group]) for g in range(tn // group)]

        def emit_qkv(rotated_heads):
            for h in range(heads):
                lanes = (h * HEAD_DIM) % group
                th = accs[h * HEAD_DIM // group][:, lanes:lanes + HEAD_DIM]
                if h < rotated_heads:
                    th = _rotate_heads(th, c_ref[...], up_ref[...], down_ref[...])
                res_ref[h] = th
                o1_ref[:, h * HEAD_DIM:(h + 1) * HEAD_DIM] = th.astype(BF16)
            _to_pattern(res_ref, o4_ref, d4, BF16)
            _to_pattern(res_ref, o16_ref, d16, BF16)

        @pl.when(shard == 0)
        def _():
            emit_qkv(heads)

        @pl.when(shard == 1)
        def _():
            emit_qkv(k_heads_in_second)

        @pl.when(shard >= 2)
        def _():
            for g, acc in enumerate(accs):
                hug_ref[:, g * group:(g + 1) * group] = acc.astype(BF16)

        @pl.when(jnp.logical_and(j == 3, i == n_tiles - 1))
        def _():
            for n in range(len(pieces)):
                for k in range(DIAGONAL):
                    to_neighbour(k, n).wait_send()
                relay(n).wait_send()
                for k in range(DIAGONAL + 1):
                    to_sibling(k, n, mc).wait_send()

    def held(base, last):
        return lambda j, i, plan_ref: jnp.where(plan_ref[base + j] == -1, i,
                                                jnp.where(plan_ref[base + j] == -2, last, 0))

    q_row, h_row = held(8, n_tiles - 1), held(16, n_tiles - 1)
    tab_spec = pl.BlockSpec((tm, HEAD_DIM), lambda j, i, plan_ref: (i, 0))
    sems = [pltpu.SemaphoreType.DMA((len(pieces), 3))] * 4
    o1, o4, o16, hug, w_in_g, w_out_g, w_pool_g = _pallas(
        body, name="in_proj_gathering",
        grid_spec=pltpu.PrefetchScalarGridSpec(
            num_scalar_prefetch=1, grid=(N_SHARDS, n_tiles),
            in_specs=[pl.BlockSpec((tm, D_MODEL), lambda j, i, plan_ref: (i, 0)), ANY, ANY, ANY,
                      tab_spec, tab_spec, tab_spec],
            out_specs=[pl.BlockSpec((tm, tn), lambda j, i, p: (q_row(j, i, p), p[4 + j])),
                       pl.BlockSpec((d4, tm // d4, tn), lambda j, i, p: (0, q_row(j, i, p), p[4 + j])),
                       pl.BlockSpec((d16, tm // d16, tn), lambda j, i, p: (0, q_row(j, i, p), p[4 + j])),
                       pl.BlockSpec((tm, tn), lambda j, i, p: (h_row(j, i, p), p[12 + j])),
                       ANY, ANY, ANY],
            scratch_shapes=[pltpu.VMEM((D_MODEL, tn), BF16), pltpu.VMEM((heads, tm, HEAD_DIM), F32),
                            pltpu.SemaphoreType.DMA(())] + sems),
        out_shape=[jax.ShapeDtypeStruct((seq, D_QKV), BF16),
                   jax.ShapeDtypeStruct((d4, seq // d4, D_QKV), BF16),
                   jax.ShapeDtypeStruct((d16, seq // d16, D_QKV), BF16),
                   jax.ShapeDtypeStruct((seq, D_UG), BF16)]
        + [jax.ShapeDtypeStruct(b.shape, b.dtype) for b in w_bufs],
        input_output_aliases={2: 4, 3: 5, 4: 6},
        compiler_params=_params(("arbitrary", "arbitrary"), 52),
    )(plan, x, *w_bufs, *tabs)
    return [o1[None], o4, o16], hug, w_in_g, w_out_g, w_pool_g


def _band_masks():
    row = lax.broadcasted_iota(jnp.int32, (KEY_BLOCK, KEY_BLOCK), 0)
    col = lax.broadcasted_iota(jnp.int32, (KEY_BLOCK, KEY_BLOCK), 1)
    return col <= row, col >= row


def _attn_fwd(qkv, name):
    dil, n, _ = qkv.shape
    scale = HEAD_DIM ** -0.5
    lo, hi = slice(0, KEY_BLOCK), slice(KEY_BLOCK, CHUNK)

    def body(q_ref, k_ref, v_ref, kb_ref, vb_ref, o_ref, st_ref):
        i = pl.program_id(1)
        cur_mask, prev_mask = _band_masks()
        before_mask = jnp.logical_and(prev_mask, i > 0)
        lane = lax.broadcasted_iota(jnp.int32, (KEY_BLOCK, STAT_LANES), 1)
        tasks = [(rows, h) for rows in (lo, hi) for h in range(N_HEADS)]
        head = lambda h: slice(h * HEAD_DIM, (h + 1) * HEAD_DIM)

        def prev_of(rows, h):
            if rows is lo:
                return kb_ref[:, head(h)], vb_ref[:, head(h)], before_mask
            return k_ref[lo, head(h)], v_ref[lo, head(h)], prev_mask

        scores = []
        for rows, h in tasks:
            q = q_ref[rows, head(h)]
            scores.append((_dot_nt(q, prev_of(rows, h)[0]), _dot_nt(q, k_ref[rows, head(h)])))
        probs = []
        for (rows, h), (qk_prev, qk_cur) in zip(tasks, scores):
            s_prev = jnp.where(prev_of(rows, h)[2], qk_prev * scale, NEG)
            s_cur = jnp.where(cur_mask, qk_cur * scale, NEG)
            m = jnp.max(jnp.maximum(s_prev, s_cur), axis=-1, keepdims=True)
            p_prev = jnp.exp(s_prev - m)
            p_cur = jnp.exp(s_cur - m)
            den = jnp.sum(p_prev + p_cur, axis=-1, keepdims=True)
            probs.append((p_prev.astype(BF16), p_cur.astype(BF16), den, m + jnp.log(den)))
        stats = [jnp.zeros((KEY_BLOCK, STAT_LANES), F32), jnp.zeros((KEY_BLOCK, STAT_LANES), F32)]
        for (rows, h), (p_prev, p_cur, den, lse) in zip(tasks, probs):
            o = _dot_nn(p_cur, v_ref[rows, head(h)]) + _dot_nn(p_prev, prev_of(rows, h)[1])
            o_ref[rows, head(h)] = (o / den).astype(BF16)
            b = 0 if rows is lo else 1
            stats[b] = jnp.where(lane == h, lse, stats[b])
        st_ref[lo, :] = stats[0]
        st_ref[hi, :] = stats[1]

    main = lambda cb: pl.BlockSpec((None, CHUNK, D_ATTN), lambda r, i: (r, i, cb))
    before = lambda cb: pl.BlockSpec((None, KEY_BLOCK, D_ATTN), lambda r, i: (r, jnp.maximum(2 * i - 1, 0), cb))
    return _pallas(
        body, name=name, grid=(dil, n // CHUNK),
        in_specs=[main(0), main(1), main(2), before(1), before(2)],
        out_specs=[main(0), pl.BlockSpec((None, CHUNK, STAT_LANES), lambda r, i: (r, i, 0))],
        out_shape=[jax.ShapeDtypeStruct((dil, n, D_ATTN), BF16), jax.ShapeDtypeStruct((dil, n, STAT_LANES), F32)],
        compiler_params=_params(("parallel", "parallel"), 40),
    )(qkv, qkv, qkv, qkv, qkv)


def _attn_bwd(qkv, do, stats, name, comm=None):
    dil, n, _ = qkv.shape
    n_blocks = n // KEY_BLOCK
    last = n // CHUNK - 1
    scale = HEAD_DIM ** -0.5
    lo, hi = slice(0, KEY_BLOCK), slice(KEY_BLOCK, CHUNK)

    def body(q_ref, k_ref, v_ref, kb_ref, vb_ref, qa_ref, do_ref, doa_ref, st_ref, sta_ref, dq_ref, dk_ref, dv_ref):
        i = pl.program_id(1)
        cur_mask, prev_mask = _band_masks()
        before_mask = jnp.logical_and(prev_mask, i > 0)
        after_mask = jnp.logical_and(prev_mask, i < last)

        rows_cat = lambda a, b: jnp.concatenate([a, b], axis=0)
        masks = (jnp.concatenate([before_mask, cur_mask], axis=1), jnp.concatenate([prev_mask, cur_mask], axis=1),
                 after_mask)

        def operands(h):
            cols = slice(h * HEAD_DIM, (h + 1) * HEAD_DIM)
            lse_c, del_c = slice(h, h + 1), slice(N_HEADS + h, N_HEADS + h + 1)
            q = (q_ref[lo, cols], q_ref[hi, cols], qa_ref[:, cols])
            do = (do_ref[lo, cols], do_ref[hi, cols], doa_ref[:, cols])
            keys = (rows_cat(kb_ref[:, cols], k_ref[lo, cols]), k_ref[:, cols], k_ref[hi, cols])
            vals = (rows_cat(vb_ref[:, cols], v_ref[lo, cols]), v_ref[:, cols], v_ref[hi, cols])
            st = ((st_ref[lo, lse_c], st_ref[lo, del_c]), (st_ref[hi, lse_c], st_ref[hi, del_c]),
                  (sta_ref[:, lse_c], sta_ref[:, del_c]))
            return cols, q, do, keys, vals, st

        group = N_HEADS // 2
        for first_head in range(0, N_HEADS, group):
            heads = range(first_head, first_head + group)
            raw = {}
            for h in heads:
                _, q, do, keys, vals, _ = operands(h)
                raw[h] = [(_dot_nt(q[j], keys[j]), _dot_nt(do[j], vals[j])) for j in range(3)]
            grads = {}
            for h in heads:
                st = operands(h)[5]
                grads[h] = []
                for j in range(3):
                    qk, dp = raw[h][j]
                    lse, delta = st[j]
                    p = jnp.exp(jnp.where(masks[j], qk * scale, NEG) - lse)
                    grads[h].append((p.astype(BF16), (p * (dp - delta) * scale).astype(BF16)))
            for h in heads:
                cols, q, do, keys, _, _ = operands(h)
                (p0, ds0), (p1, ds1), (pa, dsa) = grads[h]
                own, nxt = slice(KEY_BLOCK, CHUNK), slice(0, KEY_BLOCK)

                def put(ref, rows, val, cols=cols):
                    ref[rows, cols] = val.astype(ref.dtype)

                put(dq_ref, lo, _dot_nn(ds0, keys[0]))
                put(dq_ref, hi, _dot_nn(ds1, keys[1]))
                put(dk_ref, lo, _dot_tn(rows_cat(ds0[:, own], ds1[:, nxt]), q_ref[:, cols]))
                put(dk_ref, hi, _dot_tn(rows_cat(ds1[:, own], dsa), rows_cat(q[1], q[2])))
                put(dv_ref, lo, _dot_tn(rows_cat(p0[:, own], p1[:, nxt]), do_ref[:, cols]))
                put(dv_ref, hi, _dot_tn(rows_cat(p1[:, own], pa), rows_cat(do[1], do[2])))

    def spec(rows, width, row_of, cb):
        return pl.BlockSpec((None, rows, width), lambda r, i: (r, row_of(i), cb))

    same = lambda i: i
    before = lambda i: jnp.maximum(2 * i - 1, 0)
    after = lambda i: jnp.minimum(2 * i + 2, n_blocks - 1)
    out = spec(CHUNK, D_ATTN, same, 0)
    return _call(
        body, name=name, grid=(dil, n // CHUNK),
        in_specs=[spec(CHUNK, D_ATTN, same, 0), spec(CHUNK, D_ATTN, same, 1), spec(CHUNK, D_ATTN, same, 2),
                  spec(KEY_BLOCK, D_ATTN, before, 1), spec(KEY_BLOCK, D_ATTN, before, 2),
                  spec(KEY_BLOCK, D_ATTN, after, 0),
                  spec(CHUNK, D_ATTN, same, 0), spec(KEY_BLOCK, D_ATTN, after, 0),
                  spec(CHUNK, STAT_LANES, same, 0), spec(KEY_BLOCK, STAT_LANES, after, 0)],
        out_specs=[out, out, out],
        out_shape=[jax.ShapeDtypeStruct((dil, n, D_ATTN), BF16)] * 3,
        scratch_shapes=[], semantics=("parallel", "parallel"), vmem_mib=40,
        args=(qkv, qkv, qkv, qkv, qkv, qkv, do, do, stats, stats), comm=comm)


def _window_sums(ext, window, backward):
    rows = ext.shape[0]
    acc, span = ext, 1
    while span < window:
        acc = acc + pltpu.roll(acc, (rows - span) if backward else span, axis=0)
        span *= 2
    return acc


def _pool_group_weight(wp_ref, g):
    return jnp.concatenate([wp_ref[k, g] for k in range(N_SHARDS)], axis=0)


def _mix_gate(o_list, st_list, hug, w_pool_g, pool_scale):
    seq = hug.shape[0]
    tm = 256
    halo_blocks = tm // POOL_HALO
    d4, d16 = DILATIONS[1], DILATIONS[2]

    def body(o1_ref, o4_ref, o16_ref, l1_ref, l4_ref, l16_ref, u_ref, halo_ref, ga_ref, gp_ref, wp_ref, sc_ref,
             y_ref, mix_ref, lse_ref, pooled_ref, n4_ref, n16_ref, nl4_ref, nl16_ref):
        i = pl.program_id(0)
        _from_pattern(o4_ref, n4_ref, d4)
        _from_pattern(o16_ref, n16_ref, d16)
        _from_pattern(l4_ref, nl4_ref, d4)
        _from_pattern(l16_ref, nl16_ref, d16)
        la, lb, lc = l1_ref[...], nl4_ref[0], nl16_ref[0]
        mx = jnp.maximum(jnp.maximum(la, lb), lc)
        ea, eb, ec = jnp.exp(la - mx), jnp.exp(lb - mx), jnp.exp(lc - mx)
        tot = ea + eb + ec
        lse_ref[...] = mx + jnp.log(tot)
        wa, wb, wc = ea / tot, eb / tot, ec / tot
        ga = ga_ref[...].astype(F32)
        silu_a = ga * jax.nn.sigmoid(ga)
        for h in range(N_HEADS):
            cols = slice(h * HEAD_DIM, (h + 1) * HEAD_DIM)
            hc = slice(h, h + 1)
            attn = wa[:, hc] * o1_ref[:, cols].astype(F32) + wb[:, hc] * n4_ref[h] + wc[:, hc] * n16_ref[h]
            mix_ref[:, cols] = attn.astype(BF16)
            y_ref[:, cols] = (attn * silu_a[:, cols]).astype(BF16)

        u = u_ref[...].astype(F32)
        halo = jnp.where(i > 0, halo_ref[...].astype(F32), 0.0)
        ext = jnp.concatenate([halo, u], axis=0)
        pos = i * tm + lax.broadcasted_iota(jnp.int32, (tm, 1), 0)
        gp = gp_ref[...].astype(F32)
        gated_scale = sc_ref[...] * (gp * jax.nn.sigmoid(gp))
        for g, window in enumerate(POOL_WINDOWS):
            cols = slice(g * POOL_GROUP_DIM, (g + 1) * POOL_GROUP_DIM)
            sums = _window_sums(ext[:, cols], window, backward=False)[POOL_HALO:, :]
            count = jnp.minimum(pos + 1, window).astype(F32)
            pooled = (sums / count - u[:, cols]).astype(BF16)
            pooled_ref[:, cols] = pooled
            pre = _dot_nn(pooled, _pool_group_weight(wp_ref, g))
            out_cols = slice(D_ATTN + g * POOL_GROUP_DIM, D_ATTN + (g + 1) * POOL_GROUP_DIM)
            mix_ref[:, out_cols] = pre.astype(BF16)
            y_ref[:, out_cols] = (pre * gated_scale[:, cols]).astype(BF16)

    row = lambda width, cb=0: pl.BlockSpec((tm, width), lambda i: (i, cb))
    pat = lambda d, width: pl.BlockSpec((d, tm // d, width), lambda i: (0, i, 0))
    return _pallas(
        body, name="mix_gate", grid=(seq // tm,),
        in_specs=[row(D_ATTN), pat(d4, D_ATTN), pat(d16, D_ATTN),
                  row(STAT_LANES), pat(d4, STAT_LANES), pat(d16, STAT_LANES),
                  row(D_POOL),
                  pl.BlockSpec((POOL_HALO, D_POOL), lambda i: (jnp.maximum(i * halo_blocks - 1, 0), 0)),
                  row(D_ATTN, 1), row(D_POOL, 2),
                  pl.BlockSpec(w_pool_g.shape, lambda i: (0, 0, 0, 0)),
                  pl.BlockSpec((1, D_POOL), lambda i: (0, 0))],
        out_specs=[row(D_MODEL), row(D_MODEL), row(STAT_LANES), row(D_POOL)],
        out_shape=[jax.ShapeDtypeStruct((seq, D_MODEL), BF16), jax.ShapeDtypeStruct((seq, D_MODEL), BF16),
                   jax.ShapeDtypeStruct((seq, STAT_LANES), F32), jax.ShapeDtypeStruct((seq, D_POOL), BF16)],
        scratch_shapes=[pltpu.VMEM((N_HEADS, tm, HEAD_DIM), F32), pltpu.VMEM((N_HEADS, tm, HEAD_DIM), F32),
                        pltpu.VMEM((1, tm, STAT_LANES), F32), pltpu.VMEM((1, tm, STAT_LANES), F32)],
        compiler_params=_params(("parallel",), 48),
    )(o_list[0][0], o_list[1], o_list[2], st_list[0][0], st_list[1], st_list[2],
      hug, hug, hug, hug, w_pool_g, pool_scale)


def _out_proj_loss(y, w_out_g, x, target, gain, bias):
    seq = x.shape[0]
    tm = 512

    def body(y_ref, w_ref, x_ref, t_ref, g_ref, b_ref, dz_ref, dzb_ref, gg_ref, gb_ref, loss_ref):
        @pl.when(pl.program_id(0) == 0)
        def _():
            gg_ref[...] = jnp.zeros_like(gg_ref)
            gb_ref[...] = jnp.zeros_like(gb_ref)
            loss_ref[...] = jnp.zeros_like(loss_ref)

        halves = [slice(0, tm // 2), slice(tm // 2, tm)]
        projected = [_dot_nn(y_ref[rows, :], w_ref[...]) for rows in halves]
        for rows, out in zip(halves, projected):
            z = DEEPNORM_ALPHA * x_ref[rows, :] + out
            mu = jnp.mean(z, axis=-1, keepdims=True)
            zc = z - mu
            rstd = lax.rsqrt(jnp.mean(zc * zc, axis=-1, keepdims=True) + LN_EPS)
            xhat = zc * rstd
            gain_v = g_ref[...]
            diff = xhat * gain_v + b_ref[...] - t_ref[rows, :]
            sq = _fold_rows(diff * diff)
            part = sq[:, :128]
            for k in range(1, D_MODEL // 128):
                part = part + sq[:, k * 128:(k + 1) * 128]
            loss_ref[...] += part
            dln = diff * (1.0 / D_MODEL)
            gg_ref[...] += _fold_rows(dln * xhat)
            gb_ref[...] += _fold_rows(dln)
            dxhat = dln * gain_v
            dz = rstd * (dxhat - jnp.mean(dxhat, axis=-1, keepdims=True)
                         - xhat * jnp.mean(dxhat * xhat, axis=-1, keepdims=True))
            dz_ref[rows, :] = dz
            dzb_ref[rows, :] = dz.astype(BF16)

    row = lambda: pl.BlockSpec((tm, D_MODEL), lambda i: (i, 0))
    vec = lambda: pl.BlockSpec((1, D_MODEL), lambda i: (0, 0))
    acc = lambda width: pl.BlockSpec((8, width), lambda i: (0, 0))
    return _pallas(
        body, name="out_proj_loss", grid=(seq // tm,),
        in_specs=[row(), pl.BlockSpec((D_MODEL, D_MODEL), lambda i: (0, 0), pipeline_mode=pl.Buffered(1)),
                  row(), row(), vec(), vec()],
        out_specs=[row(), row(), acc(D_MODEL), acc(D_MODEL), acc(128)],
        out_shape=[jax.ShapeDtypeStruct((seq, D_MODEL), F32), jax.ShapeDtypeStruct((seq, D_MODEL), BF16),
                   jax.ShapeDtypeStruct((8, D_MODEL), F32), jax.ShapeDtypeStruct((8, D_MODEL), F32),
                   jax.ShapeDtypeStruct((8, 128), F32)],
        compiler_params=_params(("arbitrary",), 56),
    )(y, w_out_g.reshape(D_MODEL, D_MODEL), x, target, gain, bias)


def _dy_gate_bwd(dzb, w_out_g, hug, mixpre, pool_scale, lse_all):
    seq = dzb.shape[0]
    tm = 256
    d4, d16 = DILATIONS[1], DILATIONS[2]

    def body(dz_ref, w_ref, ga_ref, gp_ref, mix_ref, sc_ref, lse_ref,
             dh_ref, dpo_ref, do1_ref, do4_ref, do16_ref, st1_ref, st4_ref, st16_ref, da_ref, st_ref):
        dy = _dot_nt(dz_ref[...], w_ref[...])
        ga = ga_ref[...].astype(F32)
        sig = jax.nn.sigmoid(ga)
        attn = mix_ref[:, :D_ATTN].astype(F32)
        dya = dy[:, :D_ATTN]
        dattn = dya * (ga * sig)
        dh_ref[:, :D_ATTN] = (dya * attn * (sig * (1.0 + ga * (1.0 - sig)))).astype(BF16)
        _store_slabs(da_ref, dattn)
        lane = lax.broadcasted_iota(jnp.int32, (tm, STAT_LANES), 1)
        stats = lse_ref[...]
        prod = dattn * attn
        for h in range(N_HEADS):
            delta = jnp.sum(prod[:, h * HEAD_DIM:(h + 1) * HEAD_DIM], axis=-1, keepdims=True)
            stats = jnp.where(lane == N_HEADS + h, delta, stats)
        st_ref[0] = stats
        do1_ref[...] = dattn.astype(BF16)
        st1_ref[...] = stats
        _to_pattern(da_ref, do4_ref, d4, BF16)
        _to_pattern(da_ref, do16_ref, d16, BF16)
        _to_pattern(st_ref, st4_ref, d4, F32)
        _to_pattern(st_ref, st16_ref, d16, F32)

        gp = gp_ref[...].astype(F32)
        sig = jax.nn.sigmoid(gp)
        dyp = dy[:, D_ATTN:]
        dpo_ref[...] = (dyp * (gp * sig)).astype(BF16)
        dh_ref[:, D_ATTN:] = (dyp * (mix_ref[:, D_ATTN:].astype(F32) * sc_ref[...])
                              * (sig * (1.0 + gp * (1.0 - sig)))).astype(BF16)

    row = lambda width, cb=0: pl.BlockSpec((tm, width), lambda i: (i, cb))
    pat = lambda d, width: pl.BlockSpec((d, tm // d, width), lambda i: (0, i, 0))
    pat_shape = lambda d, width, dtype: jax.ShapeDtypeStruct((d, seq // d, width), dtype)
    outs = _pallas(
        body, name="dy_gate_bwd", grid=(seq // tm,),
        in_specs=[row(D_MODEL), pl.BlockSpec((D_MODEL, D_MODEL), lambda i: (0, 0)),
                  row(D_ATTN, 1), row(D_POOL, 2), row(D_MODEL), pl.BlockSpec((1, D_POOL), lambda i: (0, 0)),
                  row(STAT_LANES)],
        out_specs=[row(D_MODEL, D_IN // D_MODEL - 1), row(D_POOL),
                   row(D_ATTN), pat(d4, D_ATTN), pat(d16, D_ATTN),
                   row(STAT_LANES), pat(d4, STAT_LANES), pat(d16, STAT_LANES)],
        out_shape=[jax.ShapeDtypeStruct((seq, D_IN), BF16), jax.ShapeDtypeStruct((seq, D_POOL), BF16),
                   jax.ShapeDtypeStruct((seq, D_ATTN), BF16), pat_shape(d4, D_ATTN, BF16), pat_shape(d16, D_ATTN, BF16),
                   jax.ShapeDtypeStruct((seq, STAT_LANES), F32), pat_shape(d4, STAT_LANES, F32),
                   pat_shape(d16, STAT_LANES, F32)],
        scratch_shapes=[pltpu.VMEM((N_HEADS, tm, HEAD_DIM), F32), pltpu.VMEM((1, tm, STAT_LANES), F32)],
        compiler_params=_params(("parallel",), 48),
    )(dzb, w_out_g.reshape(D_MODEL, D_MODEL), hug, hug, mixpre, pool_scale, lse_all)
    dh, dpo, do1, do4, do16, st1, st4, st16 = outs
    return dh, dpo, [do1[None], do4, do16], [st1[None], st4, st16]


def _pool_bwd(dh, dpo, mixpre, pooled, w_pool_g, pool_scale):
    seq = dpo.shape[0]
    tm = 256
    halo_blocks = tm // POOL_HALO
    last = seq // tm - 1
    n_groups = len(POOL_WINDOWS)
    half_c = POOL_GROUP_DIM // N_SHARDS // 2
    pieces = (N_SHARDS, 2, n_groups * half_c, POOL_GROUP_DIM)

    def body(dh_in_ref, dpo_ref, halo_ref, pre_ref, pooled_ref, wp_ref, sc_ref, du_ref, gw_ref, gs_ref):
        i = pl.program_id(0)

        @pl.when(i == 0)
        def _():
            gw_ref[...] = jnp.zeros_like(gw_ref)
            gs_ref[...] = jnp.zeros_like(gs_ref)

        dpo = dpo_ref[...].astype(F32)
        scale = sc_ref[...]
        gs_ref[...] += _fold_rows(dpo * pre_ref[...].astype(F32))
        halo = jnp.where(i < last, halo_ref[...].astype(F32), 0.0)
        dpw = (jnp.concatenate([dpo, halo], axis=0) * scale).astype(BF16)
        pos = i * tm + lax.broadcasted_iota(jnp.int32, (tm + POOL_HALO, 1), 0)
        for g, window in enumerate(POOL_WINDOWS):
            cols = slice(g * POOL_GROUP_DIM, (g + 1) * POOL_GROUP_DIM)
            dpw_g = dpw[:, cols]
            gw = _dot_tn(pooled_ref[:, cols], dpw_g[:tm, :])
            for piece in range(2 * N_SHARDS):
                gw_ref[piece // 2, piece % 2, g * half_c:(g + 1) * half_c, :] += gw[piece * half_c:(piece + 1) * half_c]
            dpooled = _dot_nt(dpw_g, _pool_group_weight(wp_ref, g))
            count = jnp.minimum(pos + 1, window).astype(F32)
            sums = _window_sums(dpooled / count, window, backward=True)
            du_ref[:, cols] = (sums[:tm, :] - dpooled[:tm, :]).astype(BF16)

    row = lambda width, cb=0: pl.BlockSpec((tm, width), lambda i: (i, cb))
    return _pallas(
        body, name="pool_bwd", grid=(seq // tm,),
        in_specs=[ANY, row(D_POOL),
                  pl.BlockSpec((POOL_HALO, D_POOL),
                               lambda i: (jnp.minimum((i + 1) * halo_blocks, seq // POOL_HALO - 1), 0)),
                  row(D_POOL, 1), row(D_POOL),
                  pl.BlockSpec(w_pool_g.shape, lambda i: (0, 0, 0, 0)),
                  pl.BlockSpec((1, D_POOL), lambda i: (0, 0))],
        out_specs=[row(D_POOL, D_QKV // D_POOL),
                   pl.BlockSpec(pieces, lambda i: (0, 0, 0, 0)),
                   pl.BlockSpec((8, D_POOL), lambda i: (0, 0))],
        out_shape=[jax.ShapeDtypeStruct(dh.shape, dh.dtype),
                   jax.ShapeDtypeStruct(pieces, F32),
                   jax.ShapeDtypeStruct((8, D_POOL), F32)],
        input_output_aliases={0: 0},
        compiler_params=_params(("arbitrary",), 40),
    )(dh, dpo, dpo, mixpre, pooled, w_pool_g, pool_scale)


def _sum_patterns(dh, parts, tabs, unrotate, col_block, name, comm=None):
    seq = dh.shape[0]
    tm, tn = 256, D_ATTN
    per = D_ATTN // tn
    d4, d16 = DILATIONS[1], DILATIONS[2]

    def body(dh_in_ref, a1_ref, a4_ref, a16_ref, ct_ref, up_ref, down_ref, o_ref, n4_ref, n16_ref):
        _from_pattern(a4_ref, n4_ref, d4)
        _from_pattern(a16_ref, n16_ref, d16)
        for s in range(tn // HEAD_DIM):
            cols = slice(s * HEAD_DIM, (s + 1) * HEAD_DIM)
            tot = a1_ref[:, cols].astype(F32) + n4_ref[s] + n16_ref[s]
            if unrotate:
                tot = _rotate_heads(tot, ct_ref[...], -up_ref[...], -down_ref[...])
            o_ref[:, cols] = tot.astype(BF16)

    tab = pl.BlockSpec((tm, HEAD_DIM), lambda i, j: (i, 0))
    pat = lambda d: pl.BlockSpec((d, tm // d, tn), lambda i, j: (0, i, j))
    (dh,), exchanged = _call(
        body, name=name, grid=(seq // tm, per),
        in_specs=[ANY, pl.BlockSpec((tm, tn), lambda i, j: (i, j)), pat(d4), pat(d16), tab, tab, tab],
        out_specs=[pl.BlockSpec((tm, tn), lambda i, j: (i, col_block * per + j))],
        out_shape=[jax.ShapeDtypeStruct(dh.shape, dh.dtype)],
        scratch_shapes=[pltpu.VMEM((tn // HEAD_DIM, tm, HEAD_DIM), F32), pltpu.VMEM((tn // HEAD_DIM, tm, HEAD_DIM), F32)],
        semantics=("parallel", "parallel"), vmem_mib=32, args=(dh, parts[0][0], parts[1], parts[2], *tabs),
        aliases={0: 0}, comm=comm)
    return dh, exchanged


def _grad_w_in(x, dh, half, name, comm=None):
    seq = x.shape[0]
    ts, td, te = 2048, D_MODEL // 2, SHARD_IN

    def body(half_ref, x_ref, dh_ref, o_ref):
        k = pl.program_id(1)
        part = _dot_tn(x_ref[...].astype(BF16), dh_ref[...])

        @pl.when(k == 0)
        def _():
            o_ref[...] = part

        @pl.when(k > 0)
        def _():
            o_ref[...] += part

    (g,), exchanged = _call(
        body, name=name, grid=(N_SHARDS, seq // ts),
        in_specs=[pl.BlockSpec((ts, td), lambda e, k, half_ref: (k, half_ref[0])),
                  pl.BlockSpec((ts, te), lambda e, k, half_ref: (k, e))],
        out_specs=[pl.BlockSpec((None, td, te), lambda e, k, half_ref: (e, 0, 0))],
        out_shape=[jax.ShapeDtypeStruct((N_SHARDS, td, te), F32)],
        scratch_shapes=[], semantics=("parallel", "arbitrary"), vmem_mib=56, args=(x, dh), comm=comm,
        prefetch=(half,))
    return g, exchanged


def _grad_w_out(y, dzb):
    seq = y.shape[0]
    ts, te = 2048, 1024

    def body(y_ref, dz_ref, o_ref):
        k = pl.program_id(1)
        part = _dot_tn(y_ref[...], dz_ref[...])

        @pl.when(k == 0)
        def _():
            o_ref[...] = part

        @pl.when(k > 0)
        def _():
            o_ref[...] += part

    return _pallas(
        body, name="grad_w_out", grid=(D_MODEL // te, seq // ts),
        in_specs=[pl.BlockSpec((ts, te), lambda e, k: (k, e)), pl.BlockSpec((ts, D_MODEL), lambda e, k: (k, 0))],
        out_specs=pl.BlockSpec((te, D_MODEL), lambda e, k: (e, 0)),
        out_shape=jax.ShapeDtypeStruct((D_MODEL, D_MODEL), F32),
        compiler_params=_params(("parallel", "arbitrary"), 56),
    )(y, dzb)


GRAD_X_LATE_SHARDS = 1
GRAD_X_PARTIAL_ROWS = 512


def _grad_x_partial(dh, w_in_g, dz, first, tiles, prev=None, comm=None):
    seq = dh.shape[0]
    tm, tk = GRAD_X_PARTIAL_ROWS, SHARD_IN

    def body(*refs):
        dh_ref, w_ref, dz_ref, o_ref = refs[-4:]
        k = pl.program_id(1)
        part = _dot_nt(dh_ref[...], w_ref[...])

        @pl.when(k == 0)
        def _():
            o_ref[...] = DEEPNORM_ALPHA * dz_ref[...] + part

        @pl.when(k > 0)
        def _():
            o_ref[...] += part

    carried = [] if prev is None else [prev]
    row = pl.BlockSpec((tm, D_MODEL), lambda i, k: (i + first, 0))
    (partial,), exchanged = _call(
        body, name="grad_x_partial_%d" % first, grid=(tiles, N_SHARDS - GRAD_X_LATE_SHARDS),
        in_specs=[ANY] * len(carried) + [
            pl.BlockSpec((tm, tk), lambda i, k: (i + first, k)),
            pl.BlockSpec((None, D_MODEL, tk), lambda i, k: (k, 0, 0)), row],
        out_specs=[row],
        out_shape=[jax.ShapeDtypeStruct((seq, D_MODEL), F32)],
        scratch_shapes=[], semantics=("parallel", "arbitrary"), vmem_mib=48, args=(*carried, dh, w_in_g, dz),
        aliases={0: 0} if carried else None, comm=comm)
    return partial, exchanged


def _grad_x_final(dh, w_in_g, partial):
    seq = dh.shape[0]
    tm, tk = 512, SHARD_IN
    k0 = N_SHARDS - GRAD_X_LATE_SHARDS

    def body(dh_ref, w_ref, p_ref, o_ref):
        k = pl.program_id(1)
        part = _dot_nt(dh_ref[...], w_ref[...])

        @pl.when(k == 0)
        def _():
            o_ref[...] = p_ref[...] + part

        @pl.when(k > 0)
        def _():
            o_ref[...] += part

    row = pl.BlockSpec((tm, D_MODEL), lambda i, k: (i, 0))
    return _pallas(
        body, name="grad_x_final", grid=(seq // tm, GRAD_X_LATE_SHARDS),
        in_specs=[pl.BlockSpec((tm, tk), lambda i, k: (i, k + k0)),
                  pl.BlockSpec((None, D_MODEL, tk), lambda i, k: (k + k0, 0, 0)), row],
        out_specs=row, out_shape=jax.ShapeDtypeStruct((seq, D_MODEL), F32),
        compiler_params=_params(("parallel", "arbitrary"), 48),
    )(dh, w_in_g, partial)


def _pool_weight(w_pool_sh):
    n_groups = len(POOL_WINDOWS)
    shard_c = POOL_GROUP_DIM // N_SHARDS
    return w_pool_sh.reshape(N_SHARDS, n_groups, shard_c, POOL_GROUP_DIM)


def _step(x, target, w_bufs, pool_scale, gain, bias, place):
    seq = x.shape[0]
    tabs = _rope_tables(seq)
    core, chip_core, onward, plan = place
    qkv, hug, w_in_g, w_out_g, w_pool_sh = _in_proj_gathering(x, w_bufs, tabs, plan)
    o_list, st_list = [], []
    for p, dil in enumerate(DILATIONS):
        o, st = _attn_fwd(qkv[p], "attn_fwd_d%d" % dil)
        o_list.append(o)
        st_list.append(st)
    w_pool_g = _pool_weight(w_pool_sh)
    y, mixpre, lse_all, pooled = _mix_gate(o_list, st_list, hug, w_pool_g, pool_scale)
    dz, dzb, gain_part, bias_part, loss_part = _out_proj_loss(y, w_out_g, x, target, gain, bias)
    dh, dpo, do_list, stat_list = _dy_gate_bwd(dzb, w_out_g, hug, mixpre, pool_scale, lse_all)
    g_w_out = _grad_w_out(y, dzb)
    dh, g_w_pool, scale_part = _pool_bwd(dh, dpo, mixpre, pooled, w_pool_g, pool_scale)
    small = jnp.concatenate([scale_part, gain_part, bias_part, loss_part], axis=1)
    early = [g_w_out.reshape(N_SHARDS, 2, D_MODEL // (2 * N_SHARDS), D_MODEL), g_w_pool]

    bwd = lambda p, comm: _attn_bwd(qkv[p], do_list[p], stat_list[p], "attn_bwd_d%d" % DILATIONS[p], comm)
    part_a, halves = bwd(0, _exchange_halves(early))
    sums_b = [_add_own_half(g, h, core, "add_own_half_%d" % a) for a, (g, h) in enumerate(zip(early, halves))]
    part_b, recv = bwd(1, _scatter_to_chips(sums_b))
    bufs = [_add_chips([g, h], r, chip_core, "add_chips_%d" % a)
            for a, (g, h, r) in enumerate(zip(early, halves, recv))]
    part_c, reduced = bwd(2, _share_with_sibling(bufs))
    parts = [part_a, part_b, part_c]
    dh, gathered = _sum_patterns(dh, [t[0] for t in parts], tabs, True, 0, "sum_dq", _gather_small(small))
    dh, _ = _sum_patterns(dh, [t[1] for t in parts], tabs, True, 1, "sum_dk")
    dh, _ = _sum_patterns(dh, [t[2] for t in parts], tabs, False, 2, "sum_dv")

    give, _ = _grad_w_in(x, dh, 1 - core, "grad_w_in_give")
    keep, recv = _grad_w_in(x, dh, core, "grad_w_in_keep", _send_to_sibling([give]))
    total = [keep, recv[0]]
    total_b = _add_pair(keep, recv[0], "add_own_half_w_in")
    n_tiles = seq // GRAD_X_PARTIAL_ROWS
    tiles = 3 * n_tiles // 8
    part, relayed = _grad_x_partial(dh, w_in_g, dz, 0, tiles, None, _relay_diagonal(total_b))
    total_b = _fold_relayed(total, total_b, relayed[0], onward)
    part, recv = _grad_x_partial(dh, w_in_g, dz, tiles, n_tiles - tiles, part, _scatter_to_neighbours(total_b))
    buf = _add_chips(total, recv[0], chip_core, "add_chips_w_in")
    g_x = _grad_x_final(dh, w_in_g, part)
    return g_x, buf, reduced[0], reduced[1], small, gathered[0]


def _exchange_halves(grads):
    n = len(grads)

    def copies(src, dst, sems):
        x, y, c, _ = _mesh_place()
        return [_remote(src[a].at[j, 1 - c], dst[a].at[j], sems[0].at[a, j], sems[1].at[a, j], (x, y, 1 - c))
                for a in range(n) for j in range(N_SHARDS)]

    def start(src, dst, sems):
        for cp in copies(src, dst, sems):
            cp.start()

    def finish(src, dst, sems):
        for cp in copies(src, dst, sems):
            cp.wait()

    return _Exchange(grads, [jax.ShapeDtypeStruct((N_SHARDS,) + g.shape[2:], g.dtype) for g in grads], {},
                     [pltpu.SemaphoreType.DMA((n, N_SHARDS))] * 2, start, finish)


def _add_own_half(grad, recv, core, name):
    _, _, r, c = grad.shape
    tr = min(r, 256)

    def body(core_ref, g_ref, r_ref, ob_ref):
        ob_ref[...] = (g_ref[...] + r_ref[...]).astype(BF16)

    return _pallas(
        body, name=name,
        grid_spec=pltpu.PrefetchScalarGridSpec(
            num_scalar_prefetch=1, grid=(N_SHARDS, r // tr),
            in_specs=[pl.BlockSpec((None, None, tr, c), lambda j, i, core_ref: (j, core_ref[0], i, 0)),
                      pl.BlockSpec((None, tr, c), lambda j, i, core_ref: (j, i, 0))],
            out_specs=pl.BlockSpec((None, tr, c), lambda j, i, core_ref: (j, i, 0))),
        out_shape=jax.ShapeDtypeStruct((N_SHARDS, r, c), BF16),
        compiler_params=_params(("parallel", "parallel"), 32),
    )(core, grad, recv)


def _send_to_sibling(arrays):
    n = len(arrays)

    def copies(src, dst, sems):
        x, y, c, _ = _mesh_place()
        return [_remote(src[a], dst[a], sems[0].at[a], sems[1].at[a], (x, y, 1 - c)) for a in range(n)]

    def start(src, dst, sems):
        for cp in copies(src, dst, sems):
            cp.start()

    def finish(src, dst, sems):
        for cp in copies(src, dst, sems):
            cp.wait()

    return _Exchange(arrays, [jax.ShapeDtypeStruct(t.shape, t.dtype) for t in arrays], {},
                     [pltpu.SemaphoreType.DMA((n,))] * 2, start, finish)


def _add_pair(a, b, name):
    _, r, c = a.shape
    tr = min(r, 256)

    def body(a_ref, b_ref, ob_ref):
        ob_ref[...] = (a_ref[...] + b_ref[...]).astype(BF16)

    spec = pl.BlockSpec((None, tr, c), lambda j, i: (j, i, 0))
    return _pallas(
        body, name=name, grid=(N_SHARDS, r // tr), in_specs=[spec, spec], out_specs=spec,
        out_shape=jax.ShapeDtypeStruct(a.shape, BF16),
        compiler_params=_params(("parallel", "parallel"), 32),
    )(a, b)


def _scatter_to_chips(sums):
    n = len(sums)

    def copies(src, dst, sems):
        x, y, c, chips = _mesh_place()
        return [_remote(src[a].at[2 * cx + cy], dst[a].at[k], sems[0].at[a, k], sems[1].at[a, k], (cx, cy, c))
                for a in range(n) for k, (cx, cy) in enumerate(chips)]

    def start(src, dst, sems):
        for cp in copies(src, dst, sems):
            cp.start()

    def finish(src, dst, sems):
        for cp in copies(src, dst, sems):
            cp.wait()

    return _Exchange(sums, [jax.ShapeDtypeStruct((3,) + s.shape[1:], s.dtype) for s in sums], {},
                     [pltpu.SemaphoreType.DMA((n, 3))] * 2, start, finish)


def _add_chips(sums, recv, chip_core, name):
    r, c = sums[0].shape[-2:]
    n_sums, n_recv = len(sums), recv.shape[0]
    tr = min(r, 256)
    mine = {3: pl.BlockSpec((None, tr, c), lambda i, cc_ref: (cc_ref[0], i, 0)),
            4: pl.BlockSpec((None, None, tr, c), lambda i, cc_ref: (cc_ref[0], cc_ref[1], i, 0))}

    def body(cc_ref, *refs):
        r_ref, o_ref = refs[n_sums:]
        tot = refs[0][...]
        for s_ref in refs[1:n_sums]:
            tot = tot + s_ref[...]
        for k in range(n_recv):
            tot = tot + r_ref[k].astype(F32)
        o_ref[...] = tot

    return _pallas(
        body, name=name,
        grid_spec=pltpu.PrefetchScalarGridSpec(
            num_scalar_prefetch=1, grid=(r // tr,),
            in_specs=[mine[s.ndim] for s in sums] + [pl.BlockSpec((n_recv, tr, c), lambda i, cc_ref: (0, i, 0))],
            out_specs=pl.BlockSpec((None, tr, c), lambda i, cc_ref: (cc_ref[1], i, 0))),
        out_shape=jax.ShapeDtypeStruct((2, r, c), F32),
        compiler_params=_params(("parallel",), 32),
    )(chip_core, *sums, recv)


def _relay_diagonal(sums_b):
    def copy(src, dst, sems):
        x, y, c, _ = _mesh_place()
        diagonal = 2 * (1 - x) + (1 - y)
        return _remote(src[0].at[diagonal], dst[0], sems[0].at[0], sems[1].at[0], (x ^ (1 - c), y ^ c, c))

    def start(src, dst, sems):
        copy(src, dst, sems).start()

    def finish(src, dst, sems):
        copy(src, dst, sems).wait()

    return _Exchange([sums_b], [jax.ShapeDtypeStruct(sums_b.shape[1:], sums_b.dtype)], {},
                     [pltpu.SemaphoreType.DMA((1,))] * 2, start, finish)


def _fold_relayed(sums, sums_b, relayed, onward):
    _, r, c = sums[0].shape
    n_sums = len(sums)
    tr = min(r, 256)

    def body(on_ref, b_in_ref, *refs):
        r_ref, o_ref = refs[n_sums:]
        tot = refs[0][...]
        for s_ref in refs[1:n_sums]:
            tot = tot + s_ref[...]
        o_ref[...] = (tot + r_ref[...].astype(F32)).astype(BF16)

    return _pallas(
        body, name="fold_relayed",
        grid_spec=pltpu.PrefetchScalarGridSpec(
            num_scalar_prefetch=1, grid=(r // tr,),
            in_specs=[ANY] + [pl.BlockSpec((None, tr, c), lambda i, on_ref: (on_ref[0], i, 0))] * n_sums
            + [pl.BlockSpec((tr, c), lambda i, on_ref: (i, 0))],
            out_specs=pl.BlockSpec((None, tr, c), lambda i, on_ref: (on_ref[0], i, 0))),
        out_shape=jax.ShapeDtypeStruct(sums_b.shape, sums_b.dtype),
        input_output_aliases={1: 0},
        compiler_params=_params(("parallel",), 32),
    )(onward, sums_b, *sums, relayed)


def _scatter_to_neighbours(sums_b):
    def copies(src, dst, sems):
        x, y, c, chips = _mesh_place()
        return [_remote(src[0].at[2 * cx + cy], dst[0].at[k], sems[0].at[k], sems[1].at[k], (cx, cy, c))
                for k, (cx, cy) in enumerate(chips[:2])]

    def start(src, dst, sems):
        for cp in copies(src, dst, sems):
            cp.start()

    def finish(src, dst, sems):
        for cp in copies(src, dst, sems):
            cp.wait()

    return _Exchange([sums_b], [jax.ShapeDtypeStruct((2,) + sums_b.shape[1:], sums_b.dtype)], {},
                     [pltpu.SemaphoreType.DMA((2,))] * 2, start, finish)


def _share_with_sibling(bufs):
    n = len(bufs)

    def copies(dst, sems, half):
        x, y, c, _ = _mesh_place()
        h = c if half == "mine" else 1 - c
        return [_remote(dst[a].at[h], dst[a].at[h], sems[0].at[a], sems[1].at[a], (x, y, 1 - c)) for a in range(n)]

    def start(ins, dst, sems):
        for cp in copies(dst, sems, "mine"):
            cp.start()

    def finish(ins, dst, sems):
        for cp in copies(dst, sems, "theirs"):
            cp.wait_recv()
        for cp in copies(dst, sems, "mine"):
            cp.wait_send()

    return _Exchange(bufs, [jax.ShapeDtypeStruct(b.shape, b.dtype) for b in bufs], {a: a for a in range(n)},
                     [pltpu.SemaphoreType.DMA((n,))] * 2, start, finish)


def _adam_math(w, g, m, v):
    m = ADAM_B1 * m + (1.0 - ADAM_B1) * g
    v = ADAM_B2 * v + (1.0 - ADAM_B2) * (g * g)
    m_hat = m / (1.0 - ADAM_B1 ** ADAM_STEP)
    v_hat = v / (1.0 - ADAM_B2 ** ADAM_STEP)
    delta = -ADAM_LR * (m_hat / (jnp.sqrt(v_hat) + ADAM_EPS) + ADAM_WD * w)
    return delta, m, v


def _gather_small(small):
    def peers():
        x, y, c, _ = _mesh_place()
        return [(x ^ ((r >> 2) & 1), y ^ ((r >> 1) & 1), c ^ (r & 1)) for r in range(1, 8)], 4 * x + 2 * y + c

    def start(src, dst, sems):
        to, me = peers()
        for r, peer in enumerate(to):
            _remote(src[0], dst[0].at[me], sems[0].at[r], sems[1].at[r], peer).start()

    def finish(src, dst, sems):
        to, me = peers()
        for r, (px, py, pc) in enumerate(to):
            theirs = dst[0].at[4 * px + 2 * py + pc]
            _remote(theirs, theirs, sems[0].at[r], sems[1].at[r], (px, py, pc)).wait_recv()
        for r, peer in enumerate(to):
            _remote(src[0], dst[0].at[me], sems[0].at[r], sems[1].at[r], peer).wait_send()

    return _Exchange([small], [jax.ShapeDtypeStruct((8,) + small.shape, small.dtype)], {},
                     [pltpu.SemaphoreType.DMA((7,))] * 2, start, finish)


def _small_adamw(gathered, small, me, w_vecs, m_vecs, v_vecs):
    n = len(w_vecs)
    widths = [w.shape[1] for w in w_vecs]
    n_par = sum(widths)

    def body(me_ref, a_ref, s_ref, *refs):
        w_refs, m_refs, v_refs = refs[:n], refs[n:2 * n], refs[2 * n:3 * n]
        loss_ref, outs = refs[3 * n], refs[3 * n + 1:]
        mine = s_ref[...]
        tot = jnp.where(me_ref[0] == 0, mine, a_ref[0])
        for d in range(1, 8):
            tot = tot + jnp.where(me_ref[0] == d, mine, a_ref[d])
        tot = jnp.sum(tot, axis=0, keepdims=True)
        sq = jnp.sum(tot[:, n_par:], axis=1, keepdims=True)
        loss_ref[...] = jnp.broadcast_to(sq * (0.5 / D_MODEL), loss_ref.shape)
        lo = 0
        for k in range(n):
            g = tot[:, lo:lo + widths[k]]
            lo += widths[k]
            outs[k][...] = g
            outs[n + k][...], outs[2 * n + k][...], outs[3 * n + k][...] = _adam_math(
                w_refs[k][...], g, m_refs[k][...], v_refs[k][...])

    vm = pl.BlockSpec(memory_space=pltpu.VMEM)
    vecs = [jax.ShapeDtypeStruct((1, w), F32) for w in widths] * 4
    res = pl.pallas_call(
        body, name="small_adamw",
        grid_spec=pltpu.PrefetchScalarGridSpec(num_scalar_prefetch=1, grid=(), in_specs=[vm] * (2 + 3 * n),
                                               out_specs=[vm] * (1 + 4 * n)),
        out_shape=[jax.ShapeDtypeStruct((1, 128), F32)] + vecs,
    )(me, gathered, small, *w_vecs, *m_vecs, *v_vecs)
    return res[0], res[1:1 + n], res[1 + n:1 + 2 * n], res[1 + 2 * n:1 + 3 * n], res[1 + 3 * n:]


def _adamw(w, g, m, v, name):
    r, c = w.shape
    tr = min(r, 256)

    def body(w_ref, g_ref, m_ref, v_ref, go_ref, d_ref, nm_ref, nv_ref):
        g = g_ref[...]
        go_ref[...] = g
        d_ref[...], nm_ref[...], nv_ref[...] = _adam_math(w_ref[...], g, m_ref[...], v_ref[...])

    spec = pl.BlockSpec((tr, c), lambda i: (i, 0))
    shape = jax.ShapeDtypeStruct((r, c), F32)
    return _pallas(
        body, name=name, grid=(r // tr,),
        in_specs=[spec] * 4, out_specs=[spec] * 4, out_shape=[shape] * 4,
        compiler_params=_params(("parallel",), 48),
    )(w, g, m, v)


def _adamw_halves(w, buf, m, v, core):
    r, c = w.shape
    tr = 256
    n_blk = r // 2 // tr

    def body(core_ref, w_ref, m_ref, v_ref, g_ref, go_ref, d_ref, nm_ref, nv_ref):
        g = g_ref[...]
        go_ref[...] = g
        d_ref[...], nm_ref[...], nv_ref[...] = _adam_math(w_ref[...], g, m_ref[...], v_ref[...])

    shape = jax.ShapeDtypeStruct((r, c), F32)
    share = _share_with_sibling([buf])
    share.operands, share.aliases = [], {}
    mine = pl.BlockSpec((tr, c), lambda i, core_ref: (core_ref[0] * n_blk + i, 0))
    own, shared = _call(
        body, name="adamw_w_in_own", grid=(n_blk,),
        in_specs=[mine] * 3 + [pl.BlockSpec((None, tr, c), lambda i, core_ref: (core_ref[0], i, 0))],
        out_specs=[mine] * 4, out_shape=[shape] * 4, scratch_shapes=[], semantics=("arbitrary",), vmem_mib=48,
        args=[w, m, v, buf], aliases={4: 4}, comm=share, prefetch=[core])

    def body_b(core_ref, w_ref, m_ref, v_ref, g_ref, a0, a1, a2, a3, go_ref, d_ref, nm_ref, nv_ref):
        body(core_ref, w_ref, m_ref, v_ref, g_ref, go_ref, d_ref, nm_ref, nv_ref)

    theirs = pl.BlockSpec((tr, c), lambda i, core_ref: ((1 - core_ref[0]) * n_blk + i, 0))
    return _pallas(
        body_b, name="adamw_w_in_sibling",
        grid_spec=pltpu.PrefetchScalarGridSpec(
            num_scalar_prefetch=1, grid=(n_blk,),
            in_specs=[theirs] * 3 + [pl.BlockSpec((None, tr, c), lambda i, core_ref: (1 - core_ref[0], i, 0))]
            + [ANY] * 4,
            out_specs=[theirs] * 4),
        out_shape=[shape] * 4, input_output_aliases={5 + k: k for k in range(4)},
        compiler_params=_params(("parallel",), 48),
    )(core, w, m, v, shared[0], *own)


def kernel(x, w_in, w_pool, pool_scale, w_out, ln_gain, ln_bias, loss_target, m_w_in, m_w_pool, m_pool_scale, m_w_out, m_ln_gain, m_ln_bias, v_w_in, v_w_pool, v_pool_scale, v_w_out, v_ln_gain, v_ln_bias):
    xi, yi, ci = lax.axis_index("x"), lax.axis_index("y"), lax.axis_index("c")
    chip = (2 * xi + yi).astype(jnp.int32).reshape(1)
    core = ci.astype(jnp.int32).reshape(1)
    n_groups = len(POOL_WINDOWS)
    shard_c = w_pool.shape[2]

    w_in_b = _cast_bf16(w_in[0], chip, "cast_w_in", 256)
    w_out_b = _cast_bf16(w_out[0], chip, "cast_w_out", 256)
    w_pool_b = _cast_bf16(w_pool[0].reshape(n_groups * shard_c, POOL_GROUP_DIM), chip, "cast_w_pool", 256)

    chip_core = jnp.concatenate([chip, core])
    onward = (2 * (xi ^ ci) + (yi ^ (1 - ci))).astype(jnp.int32).reshape(1)
    g_x, half_in, full_out, full_pool, small, small_all = _step(
        x[0], loss_target[0], [w_in_b, w_out_b, w_pool_b], pool_scale, ln_gain, ln_bias,
        (core, chip_core, onward, _in_proj_plan(xi, yi)))
    half_c = shard_c // 2
    grad_w_out = full_out.reshape(D_MODEL // N_SHARDS, D_MODEL)
    grad_w_pool = (full_pool.reshape(2, n_groups, half_c, POOL_GROUP_DIM).transpose(1, 0, 2, 3)
                   .reshape(n_groups * shard_c, POOL_GROUP_DIM))

    grad_w_in, d_in, nm_in, nv_in = _adamw_halves(w_in[0], half_in, m_w_in[0], v_w_in[0], core)
    grad_w_out, d_out, nm_out, nv_out = _adamw(w_out[0], grad_w_out, m_w_out[0], v_w_out[0], "adamw_w_out")
    flat = lambda t: t[0].reshape(n_groups * shard_c, POOL_GROUP_DIM)
    grad_w_pool, d_pool, nm_pool, nv_pool = _adamw(flat(w_pool), grad_w_pool, flat(m_w_pool), flat(v_w_pool),
                                                   "adamw_w_pool")

    me = (4 * xi + 2 * yi + ci).astype(jnp.int32).reshape(1)
    loss_v, g_vecs, d_vecs, nm_vecs, nv_vecs = _small_adamw(
        small_all, small, me, [pool_scale, ln_gain, ln_bias], [m_pool_scale, m_ln_gain, m_ln_bias],
        [v_pool_scale, v_ln_gain, v_ln_bias])
    g_scale, g_gain, g_bias = g_vecs
    d_scale, d_gain, d_bias = d_vecs
    nm_scale, nm_gain, nm_bias = nm_vecs
    nv_scale, nv_gain, nv_bias = nv_vecs
    pool_shape = w_pool.shape
    return (loss_v[0, 0], g_x[None],
            grad_w_in[None], grad_w_pool.reshape(pool_shape), g_scale, grad_w_out[None], g_gain, g_bias,
            d_in[None], d_pool.reshape(pool_shape), d_scale, d_out[None], d_gain, d_bias,
            nm_in[None], nm_pool.reshape(pool_shape), nm_scale, nm_out[None], nm_gain, nm_bias,
            nv_in[None], nv_pool.reshape(pool_shape), nv_scale, nv_out[None], nv_gain, nv_bias)
```

```python
import functools

import jax
import jax.numpy as jnp
import numpy as np
from jax import lax
from jax.experimental import pallas as pl
from jax.experimental.pallas import tpu as pltpu

F32 = jnp.float32
BF16 = jnp.bfloat16
MESH = pl.DeviceIdType.MESH
ANY = pl.BlockSpec(memory_space=pl.ANY)

D_MODEL = 2048
D_ATTN = 1024
D_POOL = 1024
HEAD_DIM = 128
N_HEADS = 8
ROPE_DIM = 32
ROPE_THETA = 500000.0
DILATIONS = (1, 4, 16)
KEY_BLOCK = 128
CHUNK = 2 * KEY_BLOCK
STAT_LANES = 128
POOL_WINDOWS = (2, 4, 8, 16)
POOL_GROUP_DIM = 256
POOL_HALO = 16
D_QKV = 3 * D_ATTN
D_UG = D_POOL + D_MODEL
D_IN = D_QKV + D_UG
N_SHARDS = 4
SHARD_IN = D_IN // N_SHARDS
LN_EPS = 1e-5
DEEPNORM_ALPHA = 2.0 ** 0.25
ADAM_LR = 0.001
ADAM_B1 = 0.9
ADAM_B2 = 0.999
ADAM_EPS = 1e-08
ADAM_WD = 0.01
ADAM_STEP = 10
NEG = -1e30
MIB = 1024 * 1024


def _params(sem, vmem_mib):
    return pltpu.CompilerParams(dimension_semantics=sem, vmem_limit_bytes=vmem_mib * MIB)


def _pallas(body, **kwargs):
    pin = lambda s: pltpu.HBM(s.shape, s.dtype) if len(s.shape) >= 2 else s
    out_shape = kwargs.pop("out_shape")
    out_shape = [pin(s) for s in out_shape] if isinstance(out_shape, (list, tuple)) else pin(out_shape)
    call = pl.pallas_call(body, out_shape=out_shape, **kwargs)

    def run(*operands):
        return call(*[pltpu.with_memory_space_constraint(o, pltpu.HBM) if o.ndim >= 2 else o for o in operands])

    return run


class _Exchange:
    def __init__(self, operands, out_shape, aliases, sems, start, finish):
        self.operands, self.out_shape, self.aliases, self.sems = list(operands), list(out_shape), dict(aliases), list(sems)
        self.start, self.finish = start, finish


def _run_exchange(comm, name):
    n_in, n_out = len(comm.operands), len(comm.out_shape)

    def body(*refs):
        ins, outs, sems = refs[:n_in], refs[n_in:n_in + n_out], refs[n_in + n_out:]
        comm.start(ins, outs, sems)
        comm.finish(ins, outs, sems)

    return _pallas(
        body, name=name, in_specs=[ANY] * n_in, out_specs=[ANY] * n_out, out_shape=comm.out_shape,
        input_output_aliases=comm.aliases, scratch_shapes=comm.sems,
    )(*comm.operands)


def _call(body, *, name, grid, in_specs, out_specs, out_shape, scratch_shapes, semantics, vmem_mib, args,
          aliases=None, comm=None, prefetch=()):
    aliases = dict(aliases or {})
    n_pre, n_in, n_out, n_scr = len(prefetch), len(in_specs), len(out_specs), len(scratch_shapes)
    c_in, c_out = (len(comm.operands), len(comm.out_shape)) if comm else (0, 0)
    c_shapes, c_sems, c_operands = (comm.out_shape, comm.sems, comm.operands) if comm else ([], [], [])

    def hosted(*refs):
        pre, refs = refs[:n_pre], refs[n_pre:]
        a = n_in
        b = a + c_in
        c = b + n_out
        d = c + c_out
        e = d + n_scr
        if comm is None:
            body(*pre, *refs)
            return
        ids = [pl.program_id(k) for k in range(len(grid))]
        first = functools.reduce(jnp.logical_and, [i == 0 for i in ids])
        last = functools.reduce(jnp.logical_and, [i == g - 1 for i, g in zip(ids, grid)])

        @pl.when(first)
        def _():
            comm.start(refs[a:b], refs[c:d], refs[e:])

        body(*pre, *refs[:a], *refs[b:c], *refs[d:e])

        @pl.when(last)
        def _():
            comm.finish(refs[a:b], refs[c:d], refs[e:])

    if comm:
        semantics = ("arbitrary",) * len(grid)
        for i, o in comm.aliases.items():
            aliases[n_pre + n_in + i] = n_out + o
    outs = _pallas(
        hosted, name=name,
        grid_spec=pltpu.PrefetchScalarGridSpec(
            num_scalar_prefetch=n_pre, grid=grid, in_specs=list(in_specs) + [ANY] * c_in,
            out_specs=list(out_specs) + [ANY] * c_out, scratch_shapes=list(scratch_shapes) + c_sems),
        out_shape=list(out_shape) + c_shapes, input_output_aliases=aliases,
        compiler_params=_params(semantics, vmem_mib),
    )(*prefetch, *args, *c_operands)
    return list(outs[:n_out]), list(outs[n_out:])


def _dot_nn(a, b):
    return jnp.dot(a, b, preferred_element_type=F32)


def _dot_nt(a, b):
    return lax.dot_general(a, b, (((1,), (1,)), ((), ())), preferred_element_type=F32)


def _dot_tn(a, b):
    return lax.dot_general(a, b, (((0,), (0,)), ((), ())), preferred_element_type=F32)


def _fold_rows(a):
    r, c = a.shape
    return jnp.sum(a.reshape(r // 8, 8, c), axis=0)


def _cast_bf16(a, chip, name, rows):
    r, c = a.shape

    def body(chip_ref, a_ref, o_ref):
        o_ref[...] = a_ref[...].astype(BF16)

    return _pallas(
        body, name=name,
        grid_spec=pltpu.PrefetchScalarGridSpec(
            num_scalar_prefetch=1, grid=(r // rows,),
            in_specs=[pl.BlockSpec((rows, c), lambda i, chip_ref: (i, 0))],
            out_specs=pl.BlockSpec((None, rows, c), lambda i, chip_ref: (chip_ref[0], i, 0))),
        out_shape=jax.ShapeDtypeStruct((N_SHARDS, r, c), BF16),
        compiler_params=_params(("parallel",), 32),
    )(chip, a)


def _mesh_place():
    x, y, c = lax.axis_index("x"), lax.axis_index("y"), lax.axis_index("c")
    return x, y, c, [(1 - x, y), (x, 1 - y), (1 - x, 1 - y)]


def _remote(src, dst, send_sem, recv_sem, to):
    return pltpu.make_async_remote_copy(src_ref=src, dst_ref=dst, send_sem=send_sem, recv_sem=recv_sem,
                                        device_id=to, device_id_type=MESH)


def _rope_tables(seq):
    half = ROPE_DIM // 2
    inv_freq = (np.float64(ROPE_THETA) ** (-(2.0 * np.arange(half, dtype=np.float64)) / ROPE_DIM)).astype(np.float32)
    ang = np.arange(seq, dtype=np.float32)[:, None] * inv_freq[None, :]
    cos = np.cos(ang.astype(np.float64)).astype(np.float32)
    sin = np.sin(ang.astype(np.float64)).astype(np.float32)
    pad = np.zeros((seq, HEAD_DIM - ROPE_DIM), np.float32)
    zeros = np.zeros((seq, half), np.float32)
    c_tab = np.concatenate([cos, cos, pad + 1.0], axis=1)
    up_tab = np.concatenate([-sin, zeros, pad], axis=1)
    down_tab = np.concatenate([zeros, sin, pad], axis=1)
    return jnp.asarray(c_tab), jnp.asarray(up_tab), jnp.asarray(down_tab)


def _rotate_heads(t, c_tab, up_tab, down_tab):
    outs = []
    for h in range(t.shape[1] // HEAD_DIM):
        th = t[:, h * HEAD_DIM:(h + 1) * HEAD_DIM]
        up = pltpu.roll(th, HEAD_DIM - ROPE_DIM // 2, axis=1)
        down = pltpu.roll(th, ROPE_DIM // 2, axis=1)
        outs.append(th * c_tab + up * up_tab + down * down_tab)
    return outs[0] if len(outs) == 1 else jnp.concatenate(outs, axis=1)


def _to_pattern(slabs_ref, dst_ref, dil, dtype):
    n_slabs, rows, _ = slabs_ref.shape
    for s in range(n_slabs):
        for r in range(dil):
            dst_ref[r, :, s * 128:(s + 1) * 128] = slabs_ref[s, pl.ds(r, rows // dil, dil), :].astype(dtype)


def _from_pattern(src_ref, slabs_ref, dil):
    n_slabs, rows, _ = slabs_ref.shape
    for s in range(n_slabs):
        for r in range(dil):
            slabs_ref[s, pl.ds(r, rows // dil, dil), :] = src_ref[r, :, s * 128:(s + 1) * 128].astype(F32)


def _store_slabs(slabs_ref, value):
    for s in range(slabs_ref.shape[0]):
        slabs_ref[s] = value[:, s * 128:(s + 1) * 128]


W_IN_CHUNKS = 4


def _in_proj_plan(x, y):
    shards = [2 * x + y, 2 * (1 - x) + y, 2 * x + (1 - y), 2 * (1 - x) + (1 - y)]
    last_row = jnp.int32(-2)

    def table(active, col_of):
        cols, rows = [], []
        first_col = functools.reduce(lambda acc, j: jnp.where(active[j], col_of(shards[j]), acc), reversed(range(4)),
                                     jnp.int32(0))
        held_col, seen = first_col, jnp.bool_(False)
        for j in range(4):
            cols.append(jnp.where(active[j], col_of(shards[j]), held_col))
            rows.append(jnp.where(active[j], -1, jnp.where(seen, last_row, 0)))
            held_col = jnp.where(active[j], col_of(shards[j]), held_col)
            seen = jnp.logical_or(seen, active[j])
        return cols, rows

    q_cols, q_rows = table([s < 2 for s in shards], lambda s: s)
    h_cols, h_rows = table([s >= 2 for s in shards], lambda s: s - 2)
    return jnp.stack([jnp.asarray(v, jnp.int32) for v in shards + q_cols + q_rows + h_cols + h_rows])


def _in_proj_gathering(x, w_bufs, tabs, plan):
    seq = x.shape[0]
    tm, tn = 512, SHARD_IN
    n_tiles = seq // tm
    heads = tn // HEAD_DIM
    k_heads_in_second = 2 * D_ATTN // HEAD_DIM - heads
    d4, d16 = DILATIONS[1], DILATIONS[2]
    DIAGONAL = 2
    chunk = D_MODEL // 2 // W_IN_CHUNKS
    early = [(0, D_MODEL // 2, q * chunk, chunk) for q in range(W_IN_CHUNKS)]
    late = [(a, w_bufs[a].shape[1] // 2, 0, w_bufs[a].shape[1] // 2) for a in (1, 2)]
    pieces = early + late
    early_ids, late_ids = range(len(early)), range(len(early), len(pieces))

    def body(plan_ref, x_ref, w_in_in, w_out_in, w_pool_in, c_ref, up_ref, down_ref,
             o1_ref, o4_ref, o16_ref, hug_ref, w_ref, w_out_ref, w_pool_ref,
             wbuf_ref, res_ref, w_sem, ici_send, ici_recv, d2d_send, d2d_recv):
        j, i = pl.program_id(0), pl.program_id(1)
        mx, my, mc, chips = _mesh_place()
        sibling = (mx, my, 1 - mc)
        gathered = (w_ref, w_out_ref, w_pool_ref)
        chip_of = lambda k: 2 * chips[k][0] + chips[k][1]

        def piece(n, chip, core):
            a, per_core, offset, size = pieces[n]
            return gathered[a].at[chip, pl.ds(core * per_core + offset, size)]

        def to_neighbour(k, n):
            mine = piece(n, 2 * mx + my, mc)
            return _remote(mine, mine, ici_send.at[n, k], ici_recv.at[n, k], (*chips[k], mc))

        def relay(n):
            theirs = piece(n, 2 * (mx ^ (1 - mc)) + (my ^ mc), mc)
            return _remote(theirs, theirs, ici_send.at[n, DIAGONAL], ici_recv.at[n, DIAGONAL], (mx ^ mc, my ^ (1 - mc), mc))

        def arrival(k, n):
            theirs = piece(n, chip_of(k), mc)
            return _remote(theirs, theirs, ici_send.at[n, k], ici_recv.at[n, k], (*chips[k], mc))

        def to_sibling(k, n, core):
            theirs = piece(n, chip_of(k), core)
            return _remote(theirs, theirs, d2d_send.at[n, k], d2d_recv.at[n, k], sibling)

        def take(k, ids):
            for n in ids:
                arrival(k, n).wait_recv()
                to_sibling(k, n, mc).start()

        def taken(k, ids):
            for n in ids:
                to_sibling(k, n, 1 - mc).wait_recv()

        first_tile = i == 0

        @pl.when(jnp.logical_and(j == 0, first_tile))
        def _():
            for n in range(len(pieces)):
                for k in range(DIAGONAL):
                    to_neighbour(k, n).start()

        @pl.when(jnp.logical_and(j == 1, first_tile))
        def _():
            take(0, early_ids)
            taken(0, early_ids)

        ahead = i == n_tiles - 3
        slot = j % 2
        fetch = lambda step, half: pltpu.make_async_copy(w_ref.at[plan_ref[step]], wbuf_ref.at[half], w_sem.at[half])

        @pl.when(jnp.logical_and(j == 1, ahead))
        def _():
            take(1, early_ids)
            for n in early_ids:
                relay(n).start()
            taken(1, early_ids)
            for k in range(DIAGONAL):
                take(k, late_ids)
            for n in late_ids:
                relay(n).start()
            for k in range(DIAGONAL):
                taken(k, late_ids)
            fetch(2, 0).start()

        @pl.when(jnp.logical_and(j == 2, ahead))
        def _():
            take(DIAGONAL, range(len(pieces)))
            taken(DIAGONAL, range(len(pieces)))
            fetch(3, 1).start()

        shard = plan_ref[j]

        @pl.when(jnp.logical_and(j <= 1, first_tile))
        def _():
            cp = fetch(j, slot)
            cp.start()
            cp.wait()

        @pl.when(jnp.logical_and(j >= 2, first_tile))
        def _():
            fetch(j, slot).wait()

        xb = x_ref[...].astype(BF16)
        group = 4 * HEAD_DIM
        accs = [_dot_nn(xb, wbuf_ref[slot, :, g * group:(g + 1) * group]) for g in range(tn // group)]

        def emit_qkv(rotated_heads):
            for h in range(heads):
                lanes = (h * HEAD_DIM) % group
                th = accs[h * HEAD_DIM // group][:, lanes:lanes + HEAD_DIM]
                if h < rotated_heads:
                    th = _rotate_heads(th, c_ref[...], up_ref[...], down_ref[...])
                res_ref[h] = th
                o1_ref[:, h * HEAD_DIM:(h + 1) * HEAD_DIM] = th.astype(BF16)
            _to_pattern(res_ref, o4_ref, d4, BF16)
            _to_pattern(res_ref, o16_ref, d16, BF16)

        @pl.when(shard == 0)
        def _():
            emit_qkv(heads)

        @pl.when(shard == 1)
        def _():
            emit_qkv(k_heads_in_second)

        @pl.when(shard >= 2)
        def _():
            for g, acc in enumerate(accs):
                hug_ref[:, g * group:(g + 1) * group] = acc.astype(BF16)

        @pl.when(jnp.logical_and(j == 3, i == n_tiles - 1))
        def _():
            for n in range(len(pieces)):
                for k in range(DIAGONAL):
                    to_neighbour(k, n).wait_send()
                relay(n).wait_send()
                for k in range(DIAGONAL + 1):
                    to_sibling(k, n, mc).wait_send()

    def held(base, last):
        return lambda j, i, plan_ref: jnp.where(plan_ref[base + j] == -1, i,
                                                jnp.where(plan_ref[base + j] == -2, last, 0))

    q_row, h_row = held(8, n_tiles - 1), held(16, n_tiles - 1)
    tab_spec = pl.BlockSpec((tm, HEAD_DIM), lambda j, i, plan_ref: (i, 0))
    sems = [pltpu.SemaphoreType.DMA((len(pieces), 3))] * 4
    o1, o4, o16, hug, w_in_g, w_out_g, w_pool_g = _pallas(
        body, name="in_proj_gathering",
        grid_spec=pltpu.PrefetchScalarGridSpec(
            num_scalar_prefetch=1, grid=(N_SHARDS, n_tiles),
            in_specs=[pl.BlockSpec((tm, D_MODEL), lambda j, i, plan_ref: (i, 0)), ANY, ANY, ANY,
                      tab_spec, tab_spec, tab_spec],
            out_specs=[pl.BlockSpec((tm, tn), lambda j, i, p: (q_row(j, i, p), p[4 + j])),
                       pl.BlockSpec((d4, tm // d4, tn), lambda j, i, p: (0, q_row(j, i, p), p[4 + j])),
                       pl.BlockSpec((d16, tm // d16, tn), lambda j, i, p: (0, q_row(j, i, p), p[4 + j])),
                       pl.BlockSpec((tm, tn), lambda j, i, p: (h_row(j, i, p), p[12 + j])),
                       ANY, ANY, ANY],
            scratch_shapes=[pltpu.VMEM((2, D_MODEL, tn), BF16), pltpu.VMEM((heads, tm, HEAD_DIM), F32),
                            pltpu.SemaphoreType.DMA((2,))] + sems),
        out_shape=[jax.ShapeDtypeStruct((seq, D_QKV), BF16),
                   jax.ShapeDtypeStruct((d4, seq // d4, D_QKV), BF16),
                   jax.ShapeDtypeStruct((d16, seq // d16, D_QKV), BF16),
                   jax.ShapeDtypeStruct((seq, D_UG), BF16)]
        + [jax.ShapeDtypeStruct(b.shape, b.dtype) for b in w_bufs],
        input_output_aliases={2: 4, 3: 5, 4: 6},
        compiler_params=_params(("arbitrary", "arbitrary"), 58),
    )(plan, x, *w_bufs, *tabs)
    return [o1[None], o4, o16], hug, w_in_g, w_out_g, w_pool_g


def _band_masks():
    row = lax.broadcasted_iota(jnp.int32, (KEY_BLOCK, KEY_BLOCK), 0)
    col = lax.broadcasted_iota(jnp.int32, (KEY_BLOCK, KEY_BLOCK), 1)
    return col <= row, col >= row


def _attn_fwd(qkv, name):
    dil, n, _ = qkv.shape
    scale = HEAD_DIM ** -0.5
    lo, hi = slice(0, KEY_BLOCK), slice(KEY_BLOCK, CHUNK)

    def body(q_ref, k_ref, v_ref, kb_ref, vb_ref, o_ref, st_ref):
        i = pl.program_id(1)
        cur_mask, prev_mask = _band_masks()
        before_mask = jnp.logical_and(prev_mask, i > 0)
        lane = lax.broadcasted_iota(jnp.int32, (KEY_BLOCK, STAT_LANES), 1)
        tasks = [(rows, h) for rows in (lo, hi) for h in range(N_HEADS)]
        head = lambda h: slice(h * HEAD_DIM, (h + 1) * HEAD_DIM)

        def prev_of(rows, h):
            if rows is lo:
                return kb_ref[:, head(h)], vb_ref[:, head(h)], before_mask
            return k_ref[lo, head(h)], v_ref[lo, head(h)], prev_mask

        scores = []
        for rows, h in tasks:
            q = q_ref[rows, head(h)]
            scores.append((_dot_nt(q, prev_of(rows, h)[0]), _dot_nt(q, k_ref[rows, head(h)])))
        probs = []
        for (rows, h), (qk_prev, qk_cur) in zip(tasks, scores):
            s_prev = jnp.where(prev_of(rows, h)[2], qk_prev * scale, NEG)
            s_cur = jnp.where(cur_mask, qk_cur * scale, NEG)
            m = jnp.max(jnp.maximum(s_prev, s_cur), axis=-1, keepdims=True)
            p_prev = jnp.exp(s_prev - m)
            p_cur = jnp.exp(s_cur - m)
            den = jnp.sum(p_prev + p_cur, axis=-1, keepdims=True)
            probs.append((p_prev.astype(BF16), p_cur.astype(BF16), den, m + jnp.log(den)))
        stats = [jnp.zeros((KEY_BLOCK, STAT_LANES), F32), jnp.zeros((KEY_BLOCK, STAT_LANES), F32)]
        for (rows, h), (p_prev, p_cur, den, lse) in zip(tasks, probs):
            o = _dot_nn(p_cur, v_ref[rows, head(h)]) + _dot_nn(p_prev, prev_of(rows, h)[1])
            o_ref[rows, head(h)] = (o / den).astype(BF16)
            b = 0 if rows is lo else 1
            stats[b] = jnp.where(lane == h, lse, stats[b])
        st_ref[lo, :] = stats[0]
        st_ref[hi, :] = stats[1]

    main = lambda cb: pl.BlockSpec((None, CHUNK, D_ATTN), lambda r, i: (r, i, cb))
    before = lambda cb: pl.BlockSpec((None, KEY_BLOCK, D_ATTN), lambda r, i: (r, jnp.maximum(2 * i - 1, 0), cb))
    return _pallas(
        body, name=name, grid=(dil, n // CHUNK),
        in_specs=[main(0), main(1), main(2), before(1), before(2)],
        out_specs=[main(0), pl.BlockSpec((None, CHUNK, STAT_LANES), lambda r, i: (r, i, 0))],
        out_shape=[jax.ShapeDtypeStruct((dil, n, D_ATTN), BF16), jax.ShapeDtypeStruct((dil, n, STAT_LANES), F32)],
        compiler_params=_params(("parallel", "parallel"), 40),
    )(qkv, qkv, qkv, qkv, qkv)


def _attn_bwd(qkv, do, stats, name, comm=None):
    dil, n, _ = qkv.shape
    n_blocks = n // KEY_BLOCK
    last = n // CHUNK - 1
    scale = HEAD_DIM ** -0.5
    lo, hi = slice(0, KEY_BLOCK), slice(KEY_BLOCK, CHUNK)

    def body(q_ref, k_ref, v_ref, kb_ref, vb_ref, qa_ref, do_ref, doa_ref, st_ref, sta_ref, dq_ref, dk_ref, dv_ref):
        i = pl.program_id(1)
        cur_mask, prev_mask = _band_masks()
        before_mask = jnp.logical_and(prev_mask, i > 0)
        after_mask = jnp.logical_and(prev_mask, i < last)

        rows_cat = lambda a, b: jnp.concatenate([a, b], axis=0)
        masks = (jnp.concatenate([before_mask, cur_mask], axis=1), jnp.concatenate([prev_mask, cur_mask], axis=1),
                 after_mask)

        def operands(h):
            cols = slice(h * HEAD_DIM, (h + 1) * HEAD_DIM)
            lse_c, del_c = slice(h, h + 1), slice(N_HEADS + h, N_HEADS + h + 1)
            q = (q_ref[lo, cols], q_ref[hi, cols], qa_ref[:, cols])
            do = (do_ref[lo, cols], do_ref[hi, cols], doa_ref[:, cols])
            keys = (rows_cat(kb_ref[:, cols], k_ref[lo, cols]), k_ref[:, cols], k_ref[hi, cols])
            vals = (rows_cat(vb_ref[:, cols], v_ref[lo, cols]), v_ref[:, cols], v_ref[hi, cols])
            st = ((st_ref[lo, lse_c], st_ref[lo, del_c]), (st_ref[hi, lse_c], st_ref[hi, del_c]),
                  (sta_ref[:, lse_c], sta_ref[:, del_c]))
            return cols, q, do, keys, vals, st

        group = N_HEADS // 2
        for first_head in range(0, N_HEADS, group):
            heads = range(first_head, first_head + group)
            raw = {}
            for h in heads:
                _, q, do, keys, vals, _ = operands(h)
                raw[h] = [(_dot_nt(q[j], keys[j]), _dot_nt(do[j], vals[j])) for j in range(3)]
            grads = {}
            for h in heads:
                st = operands(h)[5]
                grads[h] = []
                for j in range(3):
                    qk, dp = raw[h][j]
                    lse, delta = st[j]
                    p = jnp.exp(jnp.where(masks[j], qk * scale, NEG) - lse)
                    grads[h].append((p.astype(BF16), (p * (dp - delta) * scale).astype(BF16)))
            for h in heads:
                cols, q, do, keys, _, _ = operands(h)
                (p0, ds0), (p1, ds1), (pa, dsa) = grads[h]
                own, nxt = slice(KEY_BLOCK, CHUNK), slice(0, KEY_BLOCK)

                def put(ref, rows, val, cols=cols):
                    ref[rows, cols] = val.astype(ref.dtype)

                put(dq_ref, lo, _dot_nn(ds0, keys[0]))
                put(dq_ref, hi, _dot_nn(ds1, keys[1]))
                put(dk_ref, lo, _dot_tn(rows_cat(ds0[:, own], ds1[:, nxt]), q_ref[:, cols]))
                put(dk_ref, hi, _dot_tn(rows_cat(ds1[:, own], dsa), rows_cat(q[1], q[2])))
                put(dv_ref, lo, _dot_tn(rows_cat(p0[:, own], p1[:, nxt]), do_ref[:, cols]))
                put(dv_ref, hi, _dot_tn(rows_cat(p1[:, own], pa), rows_cat(do[1], do[2])))

    def spec(rows, width, row_of, cb):
        return pl.BlockSpec((None, rows, width), lambda r, i: (r, row_of(i), cb))

    same = lambda i: i
    before = lambda i: jnp.maximum(2 * i - 1, 0)
    after = lambda i: jnp.minimum(2 * i + 2, n_blocks - 1)
    out = spec(CHUNK, D_ATTN, same, 0)
    return _call(
        body, name=name, grid=(dil, n // CHUNK),
        in_specs=[spec(CHUNK, D_ATTN, same, 0), spec(CHUNK, D_ATTN, same, 1), spec(CHUNK, D_ATTN, same, 2),
                  spec(KEY_BLOCK, D_ATTN, before, 1), spec(KEY_BLOCK, D_ATTN, before, 2),
                  spec(KEY_BLOCK, D_ATTN, after, 0),
                  spec(CHUNK, D_ATTN, same, 0), spec(KEY_BLOCK, D_ATTN, after, 0),
                  spec(CHUNK, STAT_LANES, same, 0), spec(KEY_BLOCK, STAT_LANES, after, 0)],
        out_specs=[out, out, out],
        out_shape=[jax.ShapeDtypeStruct((dil, n, D_ATTN), BF16)] * 3,
        scratch_shapes=[], semantics=("parallel", "parallel"), vmem_mib=40,
        args=(qkv, qkv, qkv, qkv, qkv, qkv, do, do, stats, stats), comm=comm)


def _window_sums(ext, window, backward):
    rows = ext.shape[0]
    acc, span = ext, 1
    while span < window:
        acc = acc + pltpu.roll(acc, (rows - span) if backward else span, axis=0)
        span *= 2
    return acc


def _pool_group_weight(wp_ref, g):
    return jnp.concatenate([wp_ref[k, g] for k in range(N_SHARDS)], axis=0)


def _mix_gate(o_list, st_list, hug, w_pool_g, pool_scale):
    seq = hug.shape[0]
    tm = 256
    halo_blocks = tm // POOL_HALO
    d4, d16 = DILATIONS[1], DILATIONS[2]

    def body(o1_ref, o4_ref, o16_ref, l1_ref, l4_ref, l16_ref, u_ref, halo_ref, ga_ref, gp_ref, wp_ref, sc_ref,
             y_ref, mix_ref, lse_ref, pooled_ref, n4_ref, n16_ref, nl4_ref, nl16_ref):
        i = pl.program_id(0)
        _from_pattern(o4_ref, n4_ref, d4)
        _from_pattern(o16_ref, n16_ref, d16)
        _from_pattern(l4_ref, nl4_ref, d4)
        _from_pattern(l16_ref, nl16_ref, d16)
        la, lb, lc = l1_ref[...], nl4_ref[0], nl16_ref[0]
        mx = jnp.maximum(jnp.maximum(la, lb), lc)
        ea, eb, ec = jnp.exp(la - mx), jnp.exp(lb - mx), jnp.exp(lc - mx)
        tot = ea + eb + ec
        lse_ref[...] = mx + jnp.log(tot)
        wa, wb, wc = ea / tot, eb / tot, ec / tot
        ga = ga_ref[...].astype(F32)
        silu_a = ga * jax.nn.sigmoid(ga)
        for h in range(N_HEADS):
            cols = slice(h * HEAD_DIM, (h + 1) * HEAD_DIM)
            hc = slice(h, h + 1)
            attn = wa[:, hc] * o1_ref[:, cols].astype(F32) + wb[:, hc] * n4_ref[h] + wc[:, hc] * n16_ref[h]
            mix_ref[:, cols] = attn.astype(BF16)
            y_ref[:, cols] = (attn * silu_a[:, cols]).astype(BF16)

        u = u_ref[...].astype(F32)
        halo = jnp.where(i > 0, halo_ref[...].astype(F32), 0.0)
        ext = jnp.concatenate([halo, u], axis=0)
        pos = i * tm + lax.broadcasted_iota(jnp.int32, (tm, 1), 0)
        gp = gp_ref[...].astype(F32)
        gated_scale = sc_ref[...] * (gp * jax.nn.sigmoid(gp))
        for g, window in enumerate(POOL_WINDOWS):
            cols = slice(g * POOL_GROUP_DIM, (g + 1) * POOL_GROUP_DIM)
            sums = _window_sums(ext[:, cols], window, backward=False)[POOL_HALO:, :]
            count = jnp.minimum(pos + 1, window).astype(F32)
            pooled = (sums / count - u[:, cols]).astype(BF16)
            pooled_ref[:, cols] = pooled
            pre = _dot_nn(pooled, _pool_group_weight(wp_ref, g))
            out_cols = slice(D_ATTN + g * POOL_GROUP_DIM, D_ATTN + (g + 1) * POOL_GROUP_DIM)
            mix_ref[:, out_cols] = pre.astype(BF16)
            y_ref[:, out_cols] = (pre * gated_scale[:, cols]).astype(BF16)

    row = lambda width, cb=0: pl.BlockSpec((tm, width), lambda i: (i, cb))
    pat = lambda d, width: pl.BlockSpec((d, tm // d, width), lambda i: (0, i, 0))
    return _pallas(
        body, name="mix_gate", grid=(seq // tm,),
        in_specs=[row(D_ATTN), pat(d4, D_ATTN), pat(d16, D_ATTN),
                  row(STAT_LANES), pat(d4, STAT_LANES), pat(d16, STAT_LANES),
                  row(D_POOL),
                  pl.BlockSpec((POOL_HALO, D_POOL), lambda i: (jnp.maximum(i * halo_blocks - 1, 0), 0)),
                  row(D_ATTN, 1), row(D_POOL, 2),
                  pl.BlockSpec(w_pool_g.shape, lambda i: (0, 0, 0, 0)),
                  pl.BlockSpec((1, D_POOL), lambda i: (0, 0))],
        out_specs=[row(D_MODEL), row(D_MODEL), row(STAT_LANES), row(D_POOL)],
        out_shape=[jax.ShapeDtypeStruct((seq, D_MODEL), BF16), jax.ShapeDtypeStruct((seq, D_MODEL), BF16),
                   jax.ShapeDtypeStruct((seq, STAT_LANES), F32), jax.ShapeDtypeStruct((seq, D_POOL), BF16)],
        scratch_shapes=[pltpu.VMEM((N_HEADS, tm, HEAD_DIM), F32), pltpu.VMEM((N_HEADS, tm, HEAD_DIM), F32),
                        pltpu.VMEM((1, tm, STAT_LANES), F32), pltpu.VMEM((1, tm, STAT_LANES), F32)],
        compiler_params=_params(("parallel",), 48),
    )(o_list[0][0], o_list[1], o_list[2], st_list[0][0], st_list[1], st_list[2],
      hug, hug, hug, hug, w_pool_g, pool_scale)


def _out_proj_loss(y, w_out_g, x, target, gain, bias):
    seq = x.shape[0]
    tm = 512

    def body(y_ref, w_ref, x_ref, t_ref, g_ref, b_ref, dz_ref, dzb_ref, gg_ref, gb_ref, loss_ref):
        @pl.when(pl.program_id(0) == 0)
        def _():
            gg_ref[...] = jnp.zeros_like(gg_ref)
            gb_ref[...] = jnp.zeros_like(gb_ref)
            loss_ref[...] = jnp.zeros_like(loss_ref)

        halves = [slice(0, tm // 2), slice(tm // 2, tm)]
        projected = [_dot_nn(y_ref[rows, :], w_ref[...]) for rows in halves]
        for rows, out in zip(halves, projected):
            z = DEEPNORM_ALPHA * x_ref[rows, :] + out
            mu = jnp.mean(z, axis=-1, keepdims=True)
            zc = z - mu
            rstd = lax.rsqrt(jnp.mean(zc * zc, axis=-1, keepdims=True) + LN_EPS)
            xhat = zc * rstd
            gain_v = g_ref[...]
            diff = xhat * gain_v + b_ref[...] - t_ref[rows, :]
            sq = _fold_rows(diff * diff)
            part = sq[:, :128]
            for k in range(1, D_MODEL // 128):
                part = part + sq[:, k * 128:(k + 1) * 128]
            loss_ref[...] += part
            dln = diff * (1.0 / D_MODEL)
            gg_ref[...] += _fold_rows(dln * xhat)
            gb_ref[...] += _fold_rows(dln)
            dxhat = dln * gain_v
            dz = rstd * (dxhat - jnp.mean(dxhat, axis=-1, keepdims=True)
                         - xhat * jnp.mean(dxhat * xhat, axis=-1, keepdims=True))
            dz_ref[rows, :] = dz
            dzb_ref[rows, :] = dz.astype(BF16)

    row = lambda: pl.BlockSpec((tm, D_MODEL), lambda i: (i, 0))
    vec = lambda: pl.BlockSpec((1, D_MODEL), lambda i: (0, 0))
    acc = lambda width: pl.BlockSpec((8, width), lambda i: (0, 0))
    return _pallas(
        body, name="out_proj_loss", grid=(seq // tm,),
        in_specs=[row(), pl.BlockSpec((D_MODEL, D_MODEL), lambda i: (0, 0), pipeline_mode=pl.Buffered(1)),
                  row(), row(), vec(), vec()],
        out_specs=[row(), row(), acc(D_MODEL), acc(D_MODEL), acc(128)],
        out_shape=[jax.ShapeDtypeStruct((seq, D_MODEL), F32), jax.ShapeDtypeStruct((seq, D_MODEL), BF16),
                   jax.ShapeDtypeStruct((8, D_MODEL), F32), jax.ShapeDtypeStruct((8, D_MODEL), F32),
                   jax.ShapeDtypeStruct((8, 128), F32)],
        compiler_params=_params(("arbitrary",), 56),
    )(y, w_out_g.reshape(D_MODEL, D_MODEL), x, target, gain, bias)


def _dy_gate_bwd(dzb, w_out_g, hug, mixpre, pool_scale, lse_all):
    seq = dzb.shape[0]
    tm = 256
    d4, d16 = DILATIONS[1], DILATIONS[2]

    def body(dz_ref, w_ref, ga_ref, gp_ref, mix_ref, sc_ref, lse_ref,
             dh_ref, dpo_ref, do1_ref, do4_ref, do16_ref, st1_ref, st4_ref, st16_ref, da_ref, st_ref):
        dy = _dot_nt(dz_ref[...], w_ref[...])
        ga = ga_ref[...].astype(F32)
        sig = jax.nn.sigmoid(ga)
        attn = mix_ref[:, :D_ATTN].astype(F32)
        dya = dy[:, :D_ATTN]
        dattn = dya * (ga * sig)
        dh_ref[:, :D_ATTN] = (dya * attn * (sig * (1.0 + ga * (1.0 - sig)))).astype(BF16)
        _store_slabs(da_ref, dattn)
        lane = lax.broadcasted_iota(jnp.int32, (tm, STAT_LANES), 1)
        stats = lse_ref[...]
        prod = dattn * attn
        for h in range(N_HEADS):
            delta = jnp.sum(prod[:, h * HEAD_DIM:(h + 1) * HEAD_DIM], axis=-1, keepdims=True)
            stats = jnp.where(lane == N_HEADS + h, delta, stats)
        st_ref[0] = stats
        do1_ref[...] = dattn.astype(BF16)
        st1_ref[...] = stats
        _to_pattern(da_ref, do4_ref, d4, BF16)
        _to_pattern(da_ref, do16_ref, d16, BF16)
        _to_pattern(st_ref, st4_ref, d4, F32)
        _to_pattern(st_ref, st16_ref, d16, F32)

        gp = gp_ref[...].astype(F32)
        sig = jax.nn.sigmoid(gp)
        dyp = dy[:, D_ATTN:]
        dpo_ref[...] = (dyp * (gp * sig)).astype(BF16)
        dh_ref[:, D_ATTN:] = (dyp * (mix_ref[:, D_ATTN:].astype(F32) * sc_ref[...])
                              * (sig * (1.0 + gp * (1.0 - sig)))).astype(BF16)

    row = lambda width, cb=0: pl.BlockSpec((tm, width), lambda i: (i, cb))
    pat = lambda d, width: pl.BlockSpec((d, tm // d, width), lambda i: (0, i, 0))
    pat_shape = lambda d, width, dtype: jax.ShapeDtypeStruct((d, seq // d, width), dtype)
    outs = _pallas(
        body, name="dy_gate_bwd", grid=(seq // tm,),
        in_specs=[row(D_MODEL), pl.BlockSpec((D_MODEL, D_MODEL), lambda i: (0, 0)),
                  row(D_ATTN, 1), row(D_POOL, 2), row(D_MODEL), pl.BlockSpec((1, D_POOL), lambda i: (0, 0)),
                  row(STAT_LANES)],
        out_specs=[row(D_MODEL, D_IN // D_MODEL - 1), row(D_POOL),
                   row(D_ATTN), pat(d4, D_ATTN), pat(d16, D_ATTN),
                   row(STAT_LANES), pat(d4, STAT_LANES), pat(d16, STAT_LANES)],
        out_shape=[jax.ShapeDtypeStruct((seq, D_IN), BF16), jax.ShapeDtypeStruct((seq, D_POOL), BF16),
                   jax.ShapeDtypeStruct((seq, D_ATTN), BF16), pat_shape(d4, D_ATTN, BF16), pat_shape(d16, D_ATTN, BF16),
                   jax.ShapeDtypeStruct((seq, STAT_LANES), F32), pat_shape(d4, STAT_LANES, F32),
                   pat_shape(d16, STAT_LANES, F32)],
        scratch_shapes=[pltpu.VMEM((N_HEADS, tm, HEAD_DIM), F32), pltpu.VMEM((1, tm, STAT_LANES), F32)],
        compiler_params=_params(("parallel",), 48),
    )(dzb, w_out_g.reshape(D_MODEL, D_MODEL), hug, hug, mixpre, pool_scale, lse_all)
    dh, dpo, do1, do4, do16, st1, st4, st16 = outs
    return dh, dpo, [do1[None], do4, do16], [st1[None], st4, st16]


def _pool_bwd(dh, dpo, mixpre, pooled, w_pool_g, pool_scale):
    seq = dpo.shape[0]
    tm = 256
    halo_blocks = tm // POOL_HALO
    last = seq // tm - 1
    n_groups = len(POOL_WINDOWS)
    half_c = POOL_GROUP_DIM // N_SHARDS // 2
    pieces = (N_SHARDS, 2, n_groups * half_c, POOL_GROUP_DIM)

    def body(dh_in_ref, dpo_ref, halo_ref, pre_ref, pooled_ref, wp_ref, sc_ref, du_ref, gw_ref, gs_ref):
        i = pl.program_id(0)

        @pl.when(i == 0)
        def _():
            gw_ref[...] = jnp.zeros_like(gw_ref)
            gs_ref[...] = jnp.zeros_like(gs_ref)

        dpo = dpo_ref[...].astype(F32)
        scale = sc_ref[...]
        gs_ref[...] += _fold_rows(dpo * pre_ref[...].astype(F32))
        halo = jnp.where(i < last, halo_ref[...].astype(F32), 0.0)
        dpw = (jnp.concatenate([dpo, halo], axis=0) * scale).astype(BF16)
        pos = i * tm + lax.broadcasted_iota(jnp.int32, (tm + POOL_HALO, 1), 0)
        for g, window in enumerate(POOL_WINDOWS):
            cols = slice(g * POOL_GROUP_DIM, (g + 1) * POOL_GROUP_DIM)
            dpw_g = dpw[:, cols]
            gw = _dot_tn(pooled_ref[:, cols], dpw_g[:tm, :])
            for piece in range(2 * N_SHARDS):
                gw_ref[piece // 2, piece % 2, g * half_c:(g + 1) * half_c, :] += gw[piece * half_c:(piece + 1) * half_c]
            dpooled = _dot_nt(dpw_g, _pool_group_weight(wp_ref, g))
            count = jnp.minimum(pos + 1, window).astype(F32)
            sums = _window_sums(dpooled / count, window, backward=True)
            du_ref[:, cols] = (sums[:tm, :] - dpooled[:tm, :]).astype(BF16)

    row = lambda width, cb=0: pl.BlockSpec((tm, width), lambda i: (i, cb))
    return _pallas(
        body, name="pool_bwd", grid=(seq // tm,),
        in_specs=[ANY, row(D_POOL),
                  pl.BlockSpec((POOL_HALO, D_POOL),
                               lambda i: (jnp.minimum((i + 1) * halo_blocks, seq // POOL_HALO - 1), 0)),
                  row(D_POOL, 1), row(D_POOL),
                  pl.BlockSpec(w_pool_g.shape, lambda i: (0, 0, 0, 0)),
                  pl.BlockSpec((1, D_POOL), lambda i: (0, 0))],
        out_specs=[row(D_POOL, D_QKV // D_POOL),
                   pl.BlockSpec(pieces, lambda i: (0, 0, 0, 0)),
                   pl.BlockSpec((8, D_POOL), lambda i: (0, 0))],
        out_shape=[jax.ShapeDtypeStruct(dh.shape, dh.dtype),
                   jax.ShapeDtypeStruct(pieces, F32),
                   jax.ShapeDtypeStruct((8, D_POOL), F32)],
        input_output_aliases={0: 0},
        compiler_params=_params(("arbitrary",), 40),
    )(dh, dpo, dpo, mixpre, pooled, w_pool_g, pool_scale)


def _sum_patterns(dh, parts, tabs, unrotate, col_block, name, comm=None):
    seq = dh.shape[0]
    tm, tn = 256, D_ATTN
    per = D_ATTN // tn
    d4, d16 = DILATIONS[1], DILATIONS[2]

    def body(dh_in_ref, a1_ref, a4_ref, a16_ref, ct_ref, up_ref, down_ref, o_ref, n4_ref, n16_ref):
        _from_pattern(a4_ref, n4_ref, d4)
        _from_pattern(a16_ref, n16_ref, d16)
        for s in range(tn // HEAD_DIM):
            cols = slice(s * HEAD_DIM, (s + 1) * HEAD_DIM)
            tot = a1_ref[:, cols].astype(F32) + n4_ref[s] + n16_ref[s]
            if unrotate:
                tot = _rotate_heads(tot, ct_ref[...], -up_ref[...], -down_ref[...])
            o_ref[:, cols] = tot.astype(BF16)

    tab = pl.BlockSpec((tm, HEAD_DIM), lambda i, j: (i, 0))
    pat = lambda d: pl.BlockSpec((d, tm // d, tn), lambda i, j: (0, i, j))
    (dh,), exchanged = _call(
        body, name=name, grid=(seq // tm, per),
        in_specs=[ANY, pl.BlockSpec((tm, tn), lambda i, j: (i, j)), pat(d4), pat(d16), tab, tab, tab],
        out_specs=[pl.BlockSpec((tm, tn), lambda i, j: (i, col_block * per + j))],
        out_shape=[jax.ShapeDtypeStruct(dh.shape, dh.dtype)],
        scratch_shapes=[pltpu.VMEM((tn // HEAD_DIM, tm, HEAD_DIM), F32), pltpu.VMEM((tn // HEAD_DIM, tm, HEAD_DIM), F32)],
        semantics=("parallel", "parallel"), vmem_mib=32, args=(dh, parts[0][0], parts[1], parts[2], *tabs),
        aliases={0: 0}, comm=comm)
    return dh, exchanged


def _grad_w_in(x, dh, half, name, comm=None):
    seq = x.shape[0]
    ts, td, te = 2048, D_MODEL // 2, SHARD_IN

    def body(half_ref, x_ref, dh_ref, o_ref):
        k = pl.program_id(1)
        part = _dot_tn(x_ref[...].astype(BF16), dh_ref[...])

        @pl.when(k == 0)
        def _():
            o_ref[...] = part

        @pl.when(k > 0)
        def _():
            o_ref[...] += part

    (g,), exchanged = _call(
        body, name=name, grid=(N_SHARDS, seq // ts),
        in_specs=[pl.BlockSpec((ts, td), lambda e, k, half_ref: (k, half_ref[0])),
                  pl.BlockSpec((ts, te), lambda e, k, half_ref: (k, e))],
        out_specs=[pl.BlockSpec((None, td, te), lambda e, k, half_ref: (e, 0, 0))],
        out_shape=[jax.ShapeDtypeStruct((N_SHARDS, td, te), F32)],
        scratch_shapes=[], semantics=("parallel", "arbitrary"), vmem_mib=56, args=(x, dh), comm=comm,
        prefetch=(half,))
    return g, exchanged


def _grad_w_out(y, dzb):
    seq = y.shape[0]
    ts, te = 2048, 1024

    def body(y_ref, dz_ref, o_ref):
        k = pl.program_id(1)
        part = _dot_tn(y_ref[...], dz_ref[...])

        @pl.when(k == 0)
        def _():
            o_ref[...] = part

        @pl.when(k > 0)
        def _():
            o_ref[...] += part

    return _pallas(
        body, name="grad_w_out", grid=(D_MODEL // te, seq // ts),
        in_specs=[pl.BlockSpec((ts, te), lambda e, k: (k, e)), pl.BlockSpec((ts, D_MODEL), lambda e, k: (k, 0))],
        out_specs=pl.BlockSpec((te, D_MODEL), lambda e, k: (e, 0)),
        out_shape=jax.ShapeDtypeStruct((D_MODEL, D_MODEL), F32),
        compiler_params=_params(("parallel", "arbitrary"), 56),
    )(y, dzb)


GRAD_X_LATE_SHARDS = 1
GRAD_X_PARTIAL_ROWS = 512


def _grad_x_partial(dh, w_in_g, dz, first, tiles, prev=None, comm=None):
    seq = dh.shape[0]
    tm, tk = GRAD_X_PARTIAL_ROWS, SHARD_IN

    def body(*refs):
        dh_ref, w_ref, dz_ref, o_ref = refs[-4:]
        k = pl.program_id(1)
        part = _dot_nt(dh_ref[...], w_ref[...])

        @pl.when(k == 0)
        def _():
            o_ref[...] = DEEPNORM_ALPHA * dz_ref[...] + part

        @pl.when(k > 0)
        def _():
            o_ref[...] += part

    carried = [] if prev is None else [prev]
    row = pl.BlockSpec((tm, D_MODEL), lambda i, k: (i + first, 0))
    (partial,), exchanged = _call(
        body, name="grad_x_partial_%d" % first, grid=(tiles, N_SHARDS - GRAD_X_LATE_SHARDS),
        in_specs=[ANY] * len(carried) + [
            pl.BlockSpec((tm, tk), lambda i, k: (i + first, k)),
            pl.BlockSpec((None, D_MODEL, tk), lambda i, k: (k, 0, 0)), row],
        out_specs=[row],
        out_shape=[jax.ShapeDtypeStruct((seq, D_MODEL), F32)],
        scratch_shapes=[], semantics=("parallel", "arbitrary"), vmem_mib=48, args=(*carried, dh, w_in_g, dz),
        aliases={0: 0} if carried else None, comm=comm)
    return partial, exchanged


def _grad_x_final(dh, w_in_g, partial):
    seq = dh.shape[0]
    tm, tk = 512, SHARD_IN
    k0 = N_SHARDS - GRAD_X_LATE_SHARDS

    def body(dh_ref, w_ref, p_ref, o_ref):
        k = pl.program_id(1)
        part = _dot_nt(dh_ref[...], w_ref[...])

        @pl.when(k == 0)
        def _():
            o_ref[...] = p_ref[...] + part

        @pl.when(k > 0)
        def _():
            o_ref[...] += part

    row = pl.BlockSpec((tm, D_MODEL), lambda i, k: (i, 0))
    return _pallas(
        body, name="grad_x_final", grid=(seq // tm, GRAD_X_LATE_SHARDS),
        in_specs=[pl.BlockSpec((tm, tk), lambda i, k: (i, k + k0)),
                  pl.BlockSpec((None, D_MODEL, tk), lambda i, k: (k + k0, 0, 0)), row],
        out_specs=row, out_shape=jax.ShapeDtypeStruct((seq, D_MODEL), F32),
        compiler_params=_params(("parallel", "arbitrary"), 48),
    )(dh, w_in_g, partial)


def _pool_weight(w_pool_sh):
    n_groups = len(POOL_WINDOWS)
    shard_c = POOL_GROUP_DIM // N_SHARDS
    return w_pool_sh.reshape(N_SHARDS, n_groups, shard_c, POOL_GROUP_DIM)


def _step(x, target, w_bufs, pool_scale, gain, bias, place):
    seq = x.shape[0]
    tabs = _rope_tables(seq)
    core, chip_core, onward, plan = place
    qkv, hug, w_in_g, w_out_g, w_pool_sh = _in_proj_gathering(x, w_bufs, tabs, plan)
    o_list, st_list = [], []
    for p, dil in enumerate(DILATIONS):
        o, st = _attn_fwd(qkv[p], "attn_fwd_d%d" % dil)
        o_list.append(o)
        st_list.append(st)
    w_pool_g = _pool_weight(w_pool_sh)
    y, mixpre, lse_all, pooled = _mix_gate(o_list, st_list, hug, w_pool_g, pool_scale)
    dz, dzb, gain_part, bias_part, loss_part = _out_proj_loss(y, w_out_g, x, target, gain, bias)
    dh, dpo, do_list, stat_list = _dy_gate_bwd(dzb, w_out_g, hug, mixpre, pool_scale, lse_all)
    g_w_out = _grad_w_out(y, dzb)
    dh, g_w_pool, scale_part = _pool_bwd(dh, dpo, mixpre, pooled, w_pool_g, pool_scale)
    small = jnp.concatenate([scale_part, gain_part, bias_part, loss_part], axis=1)
    early = [g_w_out.reshape(N_SHARDS, 2, D_MODEL // (2 * N_SHARDS), D_MODEL), g_w_pool]

    bwd = lambda p, comm: _attn_bwd(qkv[p], do_list[p], stat_list[p], "attn_bwd_d%d" % DILATIONS[p], comm)
    part_a, halves = bwd(0, _exchange_halves(early))
    sums_b = [_add_own_half(g, h, core, "add_own_half_%d" % a) for a, (g, h) in enumerate(zip(early, halves))]
    part_b, recv = bwd(1, _scatter_to_chips(sums_b))
    bufs = [_add_chips([g, h], r, chip_core, "add_chips_%d" % a)
            for a, (g, h, r) in enumerate(zip(early, halves, recv))]
    part_c, reduced = bwd(2, _share_with_sibling(bufs))
    parts = [part_a, part_b, part_c]
    dh, gathered = _sum_patterns(dh, [t[0] for t in parts], tabs, True, 0, "sum_dq", _gather_small(small))
    dh, _ = _sum_patterns(dh, [t[1] for t in parts], tabs, True, 1, "sum_dk")
    dh, _ = _sum_patterns(dh, [t[2] for t in parts], tabs, False, 2, "sum_dv")

    give, _ = _grad_w_in(x, dh, 1 - core, "grad_w_in_give")
    keep, recv = _grad_w_in(x, dh, core, "grad_w_in_keep", _send_to_sibling([give]))
    total = [keep, recv[0]]
    total_b = _add_pair(keep, recv[0], "add_own_half_w_in")
    n_tiles = seq // GRAD_X_PARTIAL_ROWS
    tiles = 3 * n_tiles // 8
    part, relayed = _grad_x_partial(dh, w_in_g, dz, 0, tiles, None, _relay_diagonal(total_b))
    total_b = _fold_relayed(total, total_b, relayed[0], onward)
    part, recv = _grad_x_partial(dh, w_in_g, dz, tiles, n_tiles - tiles, part, _scatter_to_neighbours(total_b))
    buf = _add_chips(total, recv[0], chip_core, "add_chips_w_in")
    g_x = _grad_x_final(dh, w_in_g, part)
    g_w_in = _run_exchange(_share_with_sibling([buf]), "share_w_in")[0]
    return g_x, g_w_in, reduced[0], reduced[1], small, gathered[0]


def _exchange_halves(grads):
    n = len(grads)

    def copies(src, dst, sems):
        x, y, c, _ = _mesh_place()
        return [_remote(src[a].at[j, 1 - c], dst[a].at[j], sems[0].at[a, j], sems[1].at[a, j], (x, y, 1 - c))
                for a in range(n) for j in range(N_SHARDS)]

    def start(src, dst, sems):
        for cp in copies(src, dst, sems):
            cp.start()

    def finish(src, dst, sems):
        for cp in copies(src, dst, sems):
            cp.wait()

    return _Exchange(grads, [jax.ShapeDtypeStruct((N_SHARDS,) + g.shape[2:], g.dtype) for g in grads], {},
                     [pltpu.SemaphoreType.DMA((n, N_SHARDS))] * 2, start, finish)


def _add_own_half(grad, recv, core, name):
    _, _, r, c = grad.shape
    tr = min(r, 256)

    def body(core_ref, g_ref, r_ref, ob_ref):
        ob_ref[...] = (g_ref[...] + r_ref[...]).astype(BF16)

    return _pallas(
        body, name=name,
        grid_spec=pltpu.PrefetchScalarGridSpec(
            num_scalar_prefetch=1, grid=(N_SHARDS, r // tr),
            in_specs=[pl.BlockSpec((None, None, tr, c), lambda j, i, core_ref: (j, core_ref[0], i, 0)),
                      pl.BlockSpec((None, tr, c), lambda j, i, core_ref: (j, i, 0))],
            out_specs=pl.BlockSpec((None, tr, c), lambda j, i, core_ref: (j, i, 0))),
        out_shape=jax.ShapeDtypeStruct((N_SHARDS, r, c), BF16),
        compiler_params=_params(("parallel", "parallel"), 32),
    )(core, grad, recv)


def _send_to_sibling(arrays):
    n = len(arrays)

    def copies(src, dst, sems):
        x, y, c, _ = _mesh_place()
        return [_remote(src[a], dst[a], sems[0].at[a], sems[1].at[a], (x, y, 1 - c)) for a in range(n)]

    def start(src, dst, sems):
        for cp in copies(src, dst, sems):
            cp.start()

    def finish(src, dst, sems):
        for cp in copies(src, dst, sems):
            cp.wait()

    return _Exchange(arrays, [jax.ShapeDtypeStruct(t.shape, t.dtype) for t in arrays], {},
                     [pltpu.SemaphoreType.DMA((n,))] * 2, start, finish)


def _add_pair(a, b, name):
    _, r, c = a.shape
    tr = min(r, 256)

    def body(a_ref, b_ref, ob_ref):
        ob_ref[...] = (a_ref[...] + b_ref[...]).astype(BF16)

    spec = pl.BlockSpec((None, tr, c), lambda j, i: (j, i, 0))
    return _pallas(
        body, name=name, grid=(N_SHARDS, r // tr), in_specs=[spec, spec], out_specs=spec,
        out_shape=jax.ShapeDtypeStruct(a.shape, BF16),
        compiler_params=_params(("parallel", "parallel"), 32),
    )(a, b)


def _scatter_to_chips(sums):
    n = len(sums)

    def copies(src, dst, sems):
        x, y, c, chips = _mesh_place()
        return [_remote(src[a].at[2 * cx + cy], dst[a].at[k], sems[0].at[a, k], sems[1].at[a, k], (cx, cy, c))
                for a in range(n) for k, (cx, cy) in enumerate(chips)]

    def start(src, dst, sems):
        for cp in copies(src, dst, sems):
            cp.start()

    def finish(src, dst, sems):
        for cp in copies(src, dst, sems):
            cp.wait()

    return _Exchange(sums, [jax.ShapeDtypeStruct((3,) + s.shape[1:], s.dtype) for s in sums], {},
                     [pltpu.SemaphoreType.DMA((n, 3))] * 2, start, finish)


def _add_chips(sums, recv, chip_core, name):
    r, c = sums[0].shape[-2:]
    n_sums, n_recv = len(sums), recv.shape[0]
    tr = min(r, 256)
    mine = {3: pl.BlockSpec((None, tr, c), lambda i, cc_ref: (cc_ref[0], i, 0)),
            4: pl.BlockSpec((None, None, tr, c), lambda i, cc_ref: (cc_ref[0], cc_ref[1], i, 0))}

    def body(cc_ref, *refs):
        r_ref, o_ref = refs[n_sums:]
        tot = refs[0][...]
        for s_ref in refs[1:n_sums]:
            tot = tot + s_ref[...]
        for k in range(n_recv):
            tot = tot + r_ref[k].astype(F32)
        o_ref[...] = tot

    return _pallas(
        body, name=name,
        grid_spec=pltpu.PrefetchScalarGridSpec(
            num_scalar_prefetch=1, grid=(r // tr,),
            in_specs=[mine[s.ndim] for s in sums] + [pl.BlockSpec((n_recv, tr, c), lambda i, cc_ref: (0, i, 0))],
            out_specs=pl.BlockSpec((None, tr, c), lambda i, cc_ref: (cc_ref[1], i, 0))),
        out_shape=jax.ShapeDtypeStruct((2, r, c), F32),
        compiler_params=_params(("parallel",), 32),
    )(chip_core, *sums, recv)


def _relay_diagonal(sums_b):
    def copy(src, dst, sems):
        x, y, c, _ = _mesh_place()
        diagonal = 2 * (1 - x) + (1 - y)
        return _remote(src[0].at[diagonal], dst[0], sems[0].at[0], sems[1].at[0], (x ^ (1 - c), y ^ c, c))

    def start(src, dst, sems):
        copy(src, dst, sems).start()

    def finish(src, dst, sems):
        copy(src, dst, sems).wait()

    return _Exchange([sums_b], [jax.ShapeDtypeStruct(sums_b.shape[1:], sums_b.dtype)], {},
                     [pltpu.SemaphoreType.DMA((1,))] * 2, start, finish)


def _fold_relayed(sums, sums_b, relayed, onward):
    _, r, c = sums[0].shape
    n_sums = len(sums)
    tr = min(r, 256)

    def body(on_ref, b_in_ref, *refs):
        r_ref, o_ref = refs[n_sums:]
        tot = refs[0][...]
        for s_ref in refs[1:n_sums]:
            tot = tot + s_ref[...]
        o_ref[...] = (tot + r_ref[...].astype(F32)).astype(BF16)

    return _pallas(
        body, name="fold_relayed",
        grid_spec=pltpu.PrefetchScalarGridSpec(
            num_scalar_prefetch=1, grid=(r // tr,),
            in_specs=[ANY] + [pl.BlockSpec((None, tr, c), lambda i, on_ref: (on_ref[0], i, 0))] * n_sums
            + [pl.BlockSpec((tr, c), lambda i, on_ref: (i, 0))],
            out_specs=pl.BlockSpec((None, tr, c), lambda i, on_ref: (on_ref[0], i, 0))),
        out_shape=jax.ShapeDtypeStruct(sums_b.shape, sums_b.dtype),
        input_output_aliases={1: 0},
        compiler_params=_params(("parallel",), 32),
    )(onward, sums_b, *sums, relayed)


def _scatter_to_neighbours(sums_b):
    def copies(src, dst, sems):
        x, y, c, chips = _mesh_place()
        return [_remote(src[0].at[2 * cx + cy], dst[0].at[k], sems[0].at[k], sems[1].at[k], (cx, cy, c))
                for k, (cx, cy) in enumerate(chips[:2])]

    def start(src, dst, sems):
        for cp in copies(src, dst, sems):
            cp.start()

    def finish(src, dst, sems):
        for cp in copies(src, dst, sems):
            cp.wait()

    return _Exchange([sums_b], [jax.ShapeDtypeStruct((2,) + sums_b.shape[1:], sums_b.dtype)], {},
                     [pltpu.SemaphoreType.DMA((2,))] * 2, start, finish)


def _share_with_sibling(bufs):
    n = len(bufs)

    def copies(dst, sems, half):
        x, y, c, _ = _mesh_place()
        h = c if half == "mine" else 1 - c
        return [_remote(dst[a].at[h], dst[a].at[h], sems[0].at[a], sems[1].at[a], (x, y, 1 - c)) for a in range(n)]

    def start(ins, dst, sems):
        for cp in copies(dst, sems, "mine"):
            cp.start()

    def finish(ins, dst, sems):
        for cp in copies(dst, sems, "theirs"):
            cp.wait_recv()
        for cp in copies(dst, sems, "mine"):
            cp.wait_send()

    return _Exchange(bufs, [jax.ShapeDtypeStruct(b.shape, b.dtype) for b in bufs], {a: a for a in range(n)},
                     [pltpu.SemaphoreType.DMA((n,))] * 2, start, finish)


def _adam_math(w, g, m, v):
    m = ADAM_B1 * m + (1.0 - ADAM_B1) * g
    v = ADAM_B2 * v + (1.0 - ADAM_B2) * (g * g)
    m_hat = m / (1.0 - ADAM_B1 ** ADAM_STEP)
    v_hat = v / (1.0 - ADAM_B2 ** ADAM_STEP)
    delta = -ADAM_LR * (m_hat / (jnp.sqrt(v_hat) + ADAM_EPS) + ADAM_WD * w)
    return delta, m, v


def _gather_small(small):
    def peers():
        x, y, c, _ = _mesh_place()
        return [(x ^ ((r >> 2) & 1), y ^ ((r >> 1) & 1), c ^ (r & 1)) for r in range(1, 8)], 4 * x + 2 * y + c

    def start(src, dst, sems):
        to, me = peers()
        for r, peer in enumerate(to):
            _remote(src[0], dst[0].at[me], sems[0].at[r], sems[1].at[r], peer).start()

    def finish(src, dst, sems):
        to, me = peers()
        for r, (px, py, pc) in enumerate(to):
            theirs = dst[0].at[4 * px + 2 * py + pc]
            _remote(theirs, theirs, sems[0].at[r], sems[1].at[r], (px, py, pc)).wait_recv()
        for r, peer in enumerate(to):
            _remote(src[0], dst[0].at[me], sems[0].at[r], sems[1].at[r], peer).wait_send()

    return _Exchange([small], [jax.ShapeDtypeStruct((8,) + small.shape, small.dtype)], {},
                     [pltpu.SemaphoreType.DMA((7,))] * 2, start, finish)


def _small_adamw(gathered, small, me, w_vecs, m_vecs, v_vecs):
    n = len(w_vecs)
    widths = [w.shape[1] for w in w_vecs]
    n_par = sum(widths)

    def body(me_ref, a_ref, s_ref, *refs):
        w_refs, m_refs, v_refs = refs[:n], refs[n:2 * n], refs[2 * n:3 * n]
        loss_ref, outs = refs[3 * n], refs[3 * n + 1:]
        mine = s_ref[...]
        tot = jnp.where(me_ref[0] == 0, mine, a_ref[0])
        for d in range(1, 8):
            tot = tot + jnp.where(me_ref[0] == d, mine, a_ref[d])
        tot = jnp.sum(tot, axis=0, keepdims=True)
        sq = jnp.sum(tot[:, n_par:], axis=1, keepdims=True)
        loss_ref[...] = jnp.broadcast_to(sq * (0.5 / D_MODEL), loss_ref.shape)
        lo = 0
        for k in range(n):
            g = tot[:, lo:lo + widths[k]]
            lo += widths[k]
            outs[k][...] = g
            outs[n + k][...], outs[2 * n + k][...], outs[3 * n + k][...] = _adam_math(
                w_refs[k][...], g, m_refs[k][...], v_refs[k][...])

    vm = pl.BlockSpec(memory_space=pltpu.VMEM)
    vecs = [jax.ShapeDtypeStruct((1, w), F32) for w in widths] * 4
    res = pl.pallas_call(
        body, name="small_adamw",
        grid_spec=pltpu.PrefetchScalarGridSpec(num_scalar_prefetch=1, grid=(), in_specs=[vm] * (2 + 3 * n),
                                               out_specs=[vm] * (1 + 4 * n)),
        out_shape=[jax.ShapeDtypeStruct((1, 128), F32)] + vecs,
    )(me, gathered, small, *w_vecs, *m_vecs, *v_vecs)
    return res[0], res[1:1 + n], res[1 + n:1 + 2 * n], res[1 + 2 * n:1 + 3 * n], res[1 + 3 * n:]


def _adamw(w, g, m, v, name):
    r, c = w.shape
    tr = min(r, 256)

    def body(w_ref, g_ref, m_ref, v_ref, go_ref, d_ref, nm_ref, nv_ref):
        g = g_ref[...]
        go_ref[...] = g
        d_ref[...], nm_ref[...], nv_ref[...] = _adam_math(w_ref[...], g, m_ref[...], v_ref[...])

    spec = pl.BlockSpec((tr, c), lambda i: (i, 0))
    shape = jax.ShapeDtypeStruct((r, c), F32)
    return _pallas(
        body, name=name, grid=(r // tr,),
        in_specs=[spec] * 4, out_specs=[spec] * 4, out_shape=[shape] * 4,
        compiler_params=_params(("parallel",), 48),
    )(w, g, m, v)


def kernel(x, w_in, w_pool, pool_scale, w_out, ln_gain, ln_bias, loss_target, m_w_in, m_w_pool, m_pool_scale, m_w_out, m_ln_gain, m_ln_bias, v_w_in, v_w_pool, v_pool_scale, v_w_out, v_ln_gain, v_ln_bias):
    xi, yi, ci = lax.axis_index("x"), lax.axis_index("y"), lax.axis_index("c")
    chip = (2 * xi + yi).astype(jnp.int32).reshape(1)
    core = ci.astype(jnp.int32).reshape(1)
    n_groups = len(POOL_WINDOWS)
    shard_c = w_pool.shape[2]

    w_in_b = _cast_bf16(w_in[0], chip, "cast_w_in", 256)
    w_out_b = _cast_bf16(w_out[0], chip, "cast_w_out", 256)
    w_pool_b = _cast_bf16(w_pool[0].reshape(n_groups * shard_c, POOL_GROUP_DIM), chip, "cast_w_pool", 256)

    chip_core = jnp.concatenate([chip, core])
    onward = (2 * (xi ^ ci) + (yi ^ (1 - ci))).astype(jnp.int32).reshape(1)
    g_x, full_in, full_out, full_pool, small, small_all = _step(
        x[0], loss_target[0], [w_in_b, w_out_b, w_pool_b], pool_scale, ln_gain, ln_bias,
        (core, chip_core, onward, _in_proj_plan(xi, yi)))
    half_c = shard_c // 2
    grad_w_in = full_in.reshape(D_MODEL, SHARD_IN)
    grad_w_out = full_out.reshape(D_MODEL // N_SHARDS, D_MODEL)
    grad_w_pool = (full_pool.reshape(2, n_groups, half_c, POOL_GROUP_DIM).transpose(1, 0, 2, 3)
                   .reshape(n_groups * shard_c, POOL_GROUP_DIM))

    grad_w_in, d_in, nm_in, nv_in = _adamw(w_in[0], grad_w_in, m_w_in[0], v_w_in[0], "adamw_w_in")
    grad_w_out, d_out, nm_out, nv_out = _adamw(w_out[0], grad_w_out, m_w_out[0], v_w_out[0], "adamw_w_out")
    flat = lambda t: t[0].reshape(n_groups * shard_c, POOL_GROUP_DIM)
    grad_w_pool, d_pool, nm_pool, nv_pool = _adamw(flat(w_pool), grad_w_pool, flat(m_w_pool), flat(v_w_pool),
                                                   "adamw_w_pool")

    me = (4 * xi + 2 * yi + ci).astype(jnp.int32).reshape(1)
    loss_v, g_vecs, d_vecs, nm_vecs, nv_vecs = _small_adamw(
        small_all, small, me, [pool_scale, ln_gain, ln_bias], [m_pool_scale, m_ln_gain, m_ln_bias],
        [v_pool_scale, v_ln_gain, v_ln_bias])
    g_scale, g_gain, g_bias = g_vecs
    d_scale, d_gain, d_bias = d_vecs
    nm_scale, nm_gain, nm_bias = nm_vecs
    nv_scale, nv_gain, nv_bias = nv_vecs
    pool_shape = w_pool.shape
    return (loss_v[0, 0], g_x[None],
            grad_w_in[None], grad_w_pool.reshape(pool_shape), g_scale, grad_w_out[None], g_gain, g_bias,
            d_in[None], d_pool.reshape(pool_shape), d_scale, d_out[None], d_gain, d_bias,
            nm_in[None], nm_pool.reshape(pool_shape), nm_scale, nm_out[None], nm_gain, nm_bias,
            nv_in[None], nv_pool.reshape(pool_shape), nv_scale, nv_out[None], nv_gain, nv_bias)
```

```python
import functools

import jax
import jax.numpy as jnp
import numpy as np
from jax import lax
from jax.experimental import pallas as pl
from jax.experimental.pallas import tpu as pltpu

F32 = jnp.float32
BF16 = jnp.bfloat16
MESH = pl.DeviceIdType.MESH
ANY = pl.BlockSpec(memory_space=pl.ANY)

D_MODEL = 2048
D_ATTN = 1024
D_POOL = 1024
HEAD_DIM = 128
N_HEADS = 8
ROPE_DIM = 32
ROPE_THETA = 500000.0
DILATIONS = (1, 4, 16)
KEY_BLOCK = 128
CHUNK = 2 * KEY_BLOCK
STAT_LANES = 128
POOL_WINDOWS = (2, 4, 8, 16)
POOL_GROUP_DIM = 256
POOL_HALO = 16
D_QKV = 3 * D_ATTN
D_UG = D_POOL + D_MODEL
D_IN = D_QKV + D_UG
N_SHARDS = 4
SHARD_IN = D_IN // N_SHARDS
LN_EPS = 1e-5
DEEPNORM_ALPHA = 2.0 ** 0.25
ADAM_LR = 0.001
ADAM_B1 = 0.9
ADAM_B2 = 0.999
ADAM_EPS = 1e-08
ADAM_WD = 0.01
ADAM_STEP = 10
NEG = -1e30
MIB = 1024 * 1024


def _params(sem, vmem_mib):
    return pltpu.CompilerParams(dimension_semantics=sem, vmem_limit_bytes=vmem_mib * MIB)


def _pallas(body, **kwargs):
    pin = lambda s: pltpu.HBM(s.shape, s.dtype) if len(s.shape) >= 2 else s
    out_shape = kwargs.pop("out_shape")
    out_shape = [pin(s) for s in out_shape] if isinstance(out_shape, (list, tuple)) else pin(out_shape)
    call = pl.pallas_call(body, out_shape=out_shape, **kwargs)

    def run(*operands):
        return call(*[pltpu.with_memory_space_constraint(o, pltpu.HBM) if o.ndim >= 2 else o for o in operands])

    return run


class _Exchange:
    def __init__(self, operands, out_shape, aliases, sems, start, finish):
        self.operands, self.out_shape, self.aliases, self.sems = list(operands), list(out_shape), dict(aliases), list(sems)
        self.start, self.finish = start, finish


def _run_exchange(comm, name):
    n_in, n_out = len(comm.operands), len(comm.out_shape)

    def body(*refs):
        ins, outs, sems = refs[:n_in], refs[n_in:n_in + n_out], refs[n_in + n_out:]
        comm.start(ins, outs, sems)
        comm.finish(ins, outs, sems)

    return _pallas(
        body, name=name, in_specs=[ANY] * n_in, out_specs=[ANY] * n_out, out_shape=comm.out_shape,
        input_output_aliases=comm.aliases, scratch_shapes=comm.sems,
    )(*comm.operands)


def _call(body, *, name, grid, in_specs, out_specs, out_shape, scratch_shapes, semantics, vmem_mib, args,
          aliases=None, comm=None, prefetch=()):
    aliases = dict(aliases or {})
    n_pre, n_in, n_out, n_scr = len(prefetch), len(in_specs), len(out_specs), len(scratch_shapes)
    c_in, c_out = (len(comm.operands), len(comm.out_shape)) if comm else (0, 0)
    c_shapes, c_sems, c_operands = (comm.out_shape, comm.sems, comm.operands) if comm else ([], [], [])

    def hosted(*refs):
        pre, refs = refs[:n_pre], refs[n_pre:]
        a = n_in
        b = a + c_in
        c = b + n_out
        d = c + c_out
        e = d + n_scr
        if comm is None:
            body(*pre, *refs)
            return
        ids = [pl.program_id(k) for k in range(len(grid))]
        first = functools.reduce(jnp.logical_and, [i == 0 for i in ids])
        last = functools.reduce(jnp.logical_and, [i == g - 1 for i, g in zip(ids, grid)])

        @pl.when(first)
        def _():
            comm.start(refs[a:b], refs[c:d], refs[e:])

        body(*pre, *refs[:a], *refs[b:c], *refs[d:e])

        @pl.when(last)
        def _():
            comm.finish(refs[a:b], refs[c:d], refs[e:])

    if comm:
        semantics = ("arbitrary",) * len(grid)
        for i, o in comm.aliases.items():
            aliases[n_pre + n_in + i] = n_out + o
    outs = _pallas(
        hosted, name=name,
        grid_spec=pltpu.PrefetchScalarGridSpec(
            num_scalar_prefetch=n_pre, grid=grid, in_specs=list(in_specs) + [ANY] * c_in,
            out_specs=list(out_specs) + [ANY] * c_out, scratch_shapes=list(scratch_shapes) + c_sems),
        out_shape=list(out_shape) + c_shapes, input_output_aliases=aliases,
        compiler_params=_params(semantics, vmem_mib),
    )(*prefetch, *args, *c_operands)
    return list(outs[:n_out]), list(outs[n_out:])


def _dot_nn(a, b):
    return jnp.dot(a, b, preferred_element_type=F32)


def _dot_nt(a, b):
    return lax.dot_general(a, b, (((1,), (1,)), ((), ())), preferred_element_type=F32)


def _dot_tn(a, b):
    return lax.dot_general(a, b, (((0,), (0,)), ((), ())), preferred_element_type=F32)


def _fold_rows(a):
    r, c = a.shape
    return jnp.sum(a.reshape(r // 8, 8, c), axis=0)


def _cast_bf16(a, chip, name, rows):
    r, c = a.shape

    def body(chip_ref, a_ref, o_ref):
        o_ref[...] = a_ref[...].astype(BF16)

    return _pallas(
        body, name=name,
        grid_spec=pltpu.PrefetchScalarGridSpec(
            num_scalar_prefetch=1, grid=(r // rows,),
            in_specs=[pl.BlockSpec((rows, c), lambda i, chip_ref: (i, 0))],
            out_specs=pl.BlockSpec((None, rows, c), lambda i, chip_ref: (chip_ref[0], i, 0))),
        out_shape=jax.ShapeDtypeStruct((N_SHARDS, r, c), BF16),
        compiler_params=_params(("parallel",), 32),
    )(chip, a)


def _mesh_place():
    x, y, c = lax.axis_index("x"), lax.axis_index("y"), lax.axis_index("c")
    return x, y, c, [(1 - x, y), (x, 1 - y), (1 - x, 1 - y)]


def _remote(src, dst, send_sem, recv_sem, to):
    return pltpu.make_async_remote_copy(src_ref=src, dst_ref=dst, send_sem=send_sem, recv_sem=recv_sem,
                                        device_id=to, device_id_type=MESH)


def _rope_tables(seq):
    half = ROPE_DIM // 2
    inv_freq = (np.float64(ROPE_THETA) ** (-(2.0 * np.arange(half, dtype=np.float64)) / ROPE_DIM)).astype(np.float32)
    ang = np.arange(seq, dtype=np.float32)[:, None] * inv_freq[None, :]
    cos = np.cos(ang.astype(np.float64)).astype(np.float32)
    sin = np.sin(ang.astype(np.float64)).astype(np.float32)
    pad = np.zeros((seq, HEAD_DIM - ROPE_DIM), np.float32)
    zeros = np.zeros((seq, half), np.float32)
    c_tab = np.concatenate([cos, cos, pad + 1.0], axis=1)
    up_tab = np.concatenate([-sin, zeros, pad], axis=1)
    down_tab = np.concatenate([zeros, sin, pad], axis=1)
    return jnp.asarray(c_tab), jnp.asarray(up_tab), jnp.asarray(down_tab)


def _rotate_heads(t, c_tab, up_tab, down_tab):
    outs = []
    for h in range(t.shape[1] // HEAD_DIM):
        th = t[:, h * HEAD_DIM:(h + 1) * HEAD_DIM]
        up = pltpu.roll(th, HEAD_DIM - ROPE_DIM // 2, axis=1)
        down = pltpu.roll(th, ROPE_DIM // 2, axis=1)
        outs.append(th * c_tab + up * up_tab + down * down_tab)
    return outs[0] if len(outs) == 1 else jnp.concatenate(outs, axis=1)


def _to_pattern(slabs_ref, dst_ref, dil, dtype):
    n_slabs, rows, _ = slabs_ref.shape
    for s in range(n_slabs):
        for r in range(dil):
            dst_ref[r, :, s * 128:(s + 1) * 128] = slabs_ref[s, pl.ds(r, rows // dil, dil), :].astype(dtype)


def _from_pattern(src_ref, slabs_ref, dil):
    n_slabs, rows, _ = slabs_ref.shape
    for s in range(n_slabs):
        for r in range(dil):
            slabs_ref[s, pl.ds(r, rows // dil, dil), :] = src_ref[r, :, s * 128:(s + 1) * 128].astype(F32)


def _store_slabs(slabs_ref, value):
    for s in range(slabs_ref.shape[0]):
        slabs_ref[s] = value[:, s * 128:(s + 1) * 128]


W_IN_CHUNKS = 4


def _in_proj_plan(x, y):
    shards = [2 * x + y, 2 * (1 - x) + y, 2 * x + (1 - y), 2 * (1 - x) + (1 - y)]
    last_row = jnp.int32(-2)

    def table(active, col_of):
        cols, rows = [], []
        first_col = functools.reduce(lambda acc, j: jnp.where(active[j], col_of(shards[j]), acc), reversed(range(4)),
                                     jnp.int32(0))
        held_col, seen = first_col, jnp.bool_(False)
        for j in range(4):
            cols.append(jnp.where(active[j], col_of(shards[j]), held_col))
            rows.append(jnp.where(active[j], -1, jnp.where(seen, last_row, 0)))
            held_col = jnp.where(active[j], col_of(shards[j]), held_col)
            seen = jnp.logical_or(seen, active[j])
        return cols, rows

    q_cols, q_rows = table([s < 2 for s in shards], lambda s: s)
    h_cols, h_rows = table([s >= 2 for s in shards], lambda s: s - 2)
    return jnp.stack([jnp.asarray(v, jnp.int32) for v in shards + q_cols + q_rows + h_cols + h_rows])


def _in_proj_gathering(x, w_bufs, tabs, plan):
    seq = x.shape[0]
    tm, tn = 512, SHARD_IN
    n_tiles = seq // tm
    heads = tn // HEAD_DIM
    k_heads_in_second = 2 * D_ATTN // HEAD_DIM - heads
    d4, d16 = DILATIONS[1], DILATIONS[2]
    DIAGONAL = 2
    chunk = D_MODEL // 2 // W_IN_CHUNKS
    early = [(0, D_MODEL // 2, q * chunk, chunk) for q in range(W_IN_CHUNKS)]
    late = [(a, w_bufs[a].shape[1] // 2, 0, w_bufs[a].shape[1] // 2) for a in (1, 2)]
    pieces = early + late
    early_ids, late_ids = range(len(early)), range(len(early), len(pieces))

    def body(plan_ref, x_ref, w_in_in, w_out_in, w_pool_in, c_ref, up_ref, down_ref,
             o1_ref, o4_ref, o16_ref, hug_ref, w_ref, w_out_ref, w_pool_ref,
             wbuf_ref, res_ref, w_sem, ici_send, ici_recv, d2d_send, d2d_recv):
        j, i = pl.program_id(0), pl.program_id(1)
        mx, my, mc, chips = _mesh_place()
        sibling = (mx, my, 1 - mc)
        gathered = (w_ref, w_out_ref, w_pool_ref)
        chip_of = lambda k: 2 * chips[k][0] + chips[k][1]

        def piece(n, chip, core):
            a, per_core, offset, size = pieces[n]
            return gathered[a].at[chip, pl.ds(core * per_core + offset, size)]

        def to_neighbour(k, n):
            mine = piece(n, 2 * mx + my, mc)
            return _remote(mine, mine, ici_send.at[n, k], ici_recv.at[n, k], (*chips[k], mc))

        def relay(n):
            theirs = piece(n, 2 * (mx ^ (1 - mc)) + (my ^ mc), mc)
            return _remote(theirs, theirs, ici_send.at[n, DIAGONAL], ici_recv.at[n, DIAGONAL], (mx ^ mc, my ^ (1 - mc), mc))

        def arrival(k, n):
            theirs = piece(n, chip_of(k), mc)
            return _remote(theirs, theirs, ici_send.at[n, k], ici_recv.at[n, k], (*chips[k], mc))

        def to_sibling(k, n, core):
            theirs = piece(n, chip_of(k), core)
            return _remote(theirs, theirs, d2d_send.at[n, k], d2d_recv.at[n, k], sibling)

        def take(k, ids):
            for n in ids:
                arrival(k, n).wait_recv()
                to_sibling(k, n, mc).start()

        def taken(k, ids):
            for n in ids:
                to_sibling(k, n, 1 - mc).wait_recv()

        first_tile = i == 0

        @pl.when(jnp.logical_and(j == 0, first_tile))
        def _():
            for n in early_ids:
                for k in range(DIAGONAL):
                    to_neighbour(k, n).start()

        @pl.when(jnp.logical_and(j == 1, first_tile))
        def _():
            take(0, early_ids)
            taken(0, early_ids)
            for n in late_ids:
                for k in range(DIAGONAL):
                    to_neighbour(k, n).start()

        ahead = i == n_tiles - 3
        slot = j % 2
        fetch = lambda step, half: pltpu.make_async_copy(w_ref.at[plan_ref[step]], wbuf_ref.at[half], w_sem.at[half])

        @pl.when(jnp.logical_and(j == 1, ahead))
        def _():
            take(1, early_ids)
            for n in early_ids:
                relay(n).start()
            taken(1, early_ids)
            for k in range(DIAGONAL):
                take(k, late_ids)
            for n in late_ids:
                relay(n).start()
            for k in range(DIAGONAL):
                taken(k, late_ids)
            fetch(2, 0).start()

        @pl.when(jnp.logical_and(j == 2, ahead))
        def _():
            take(DIAGONAL, range(len(pieces)))
            taken(DIAGONAL, range(len(pieces)))
            fetch(3, 1).start()

        shard = plan_ref[j]

        @pl.when(jnp.logical_and(j <= 1, first_tile))
        def _():
            cp = fetch(j, slot)
            cp.start()
            cp.wait()

        @pl.when(jnp.logical_and(j >= 2, first_tile))
        def _():
            fetch(j, slot).wait()

        xb = x_ref[...].astype(BF16)
        group = 4 * HEAD_DIM
        accs = [_dot_nn(xb, wbuf_ref[slot, :, g * group:(g + 1) * group]) for g in range(tn // group)]

        def emit_qkv(rotated_heads):
            for h in range(heads):
                lanes = (h * HEAD_DIM) % group
                th = accs[h * HEAD_DIM // group][:, lanes:lanes + HEAD_DIM]
                if h < rotated_heads:
                    th = _rotate_heads(th, c_ref[...], up_ref[...], down_ref[...])
                res_ref[h] = th
                o1_ref[:, h * HEAD_DIM:(h + 1) * HEAD_DIM] = th.astype(BF16)
            _to_pattern(res_ref, o4_ref, d4, BF16)
            _to_pattern(res_ref, o16_ref, d16, BF16)

        @pl.when(shard == 0)
        def _():
            emit_qkv(heads)

        @pl.when(shard == 1)
        def _():
            emit_qkv(k_heads_in_second)

        @pl.when(shard >= 2)
        def _():
            for g, acc in enumerate(accs):
                hug_ref[:, g * group:(g + 1) * group] = acc.astype(BF16)

        @pl.when(jnp.logical_and(j == 3, i == n_tiles - 1))
        def _():
            for n in range(len(pieces)):
                for k in range(DIAGONAL):
                    to_neighbour(k, n).wait_send()
                relay(n).wait_send()
                for k in range(DIAGONAL + 1):
                    to_sibling(k, n, mc).wait_send()

    def held(base, last):
        return lambda j, i, plan_ref: jnp.where(plan_ref[base + j] == -1, i,
                                                jnp.where(plan_ref[base + j] == -2, last, 0))

    q_row, h_row = held(8, n_tiles - 1), held(16, n_tiles - 1)
    tab_spec = pl.BlockSpec((tm, HEAD_DIM), lambda j, i, plan_ref: (i, 0))
    sems = [pltpu.SemaphoreType.DMA((len(pieces), 3))] * 4
    o1, o4, o16, hug, w_in_g, w_out_g, w_pool_g = _pallas(
        body, name="in_proj_gathering",
        grid_spec=pltpu.PrefetchScalarGridSpec(
            num_scalar_prefetch=1, grid=(N_SHARDS, n_tiles),
            in_specs=[pl.BlockSpec((tm, D_MODEL), lambda j, i, plan_ref: (i, 0)), ANY, ANY, ANY,
                      tab_spec, tab_spec, tab_spec],
            out_specs=[pl.BlockSpec((tm, tn), lambda j, i, p: (q_row(j, i, p), p[4 + j])),
                       pl.BlockSpec((d4, tm // d4, tn), lambda j, i, p: (0, q_row(j, i, p), p[4 + j])),
                       pl.BlockSpec((d16, tm // d16, tn), lambda j, i, p: (0, q_row(j, i, p), p[4 + j])),
                       pl.BlockSpec((tm, tn), lambda j, i, p: (h_row(j, i, p), p[12 + j])),
                       ANY, ANY, ANY],
            scratch_shapes=[pltpu.VMEM((2, D_MODEL, tn), BF16), pltpu.VMEM((heads, tm, HEAD_DIM), F32),
                            pltpu.SemaphoreType.DMA((2,))] + sems),
        out_shape=[jax.ShapeDtypeStruct((seq, D_QKV), BF16),
                   jax.ShapeDtypeStruct((d4, seq // d4, D_QKV), BF16),
                   jax.ShapeDtypeStruct((d16, seq // d16, D_QKV), BF16),
                   jax.ShapeDtypeStruct((seq, D_UG), BF16)]
        + [jax.ShapeDtypeStruct(b.shape, b.dtype) for b in w_bufs],
        input_output_aliases={2: 4, 3: 5, 4: 6},
        compiler_params=_params(("arbitrary", "arbitrary"), 58),
    )(plan, x, *w_bufs, *tabs)
    return [o1[None], o4, o16], hug, w_in_g, w_out_g, w_pool_g


def _band_masks():
    row = lax.broadcasted_iota(jnp.int32, (KEY_BLOCK, KEY_BLOCK), 0)
    col = lax.broadcasted_iota(jnp.int32, (KEY_BLOCK, KEY_BLOCK), 1)
    return col <= row, col >= row


def _attn_fwd(qkv, name):
    dil, n, _ = qkv.shape
    scale = HEAD_DIM ** -0.5
    lo, hi = slice(0, KEY_BLOCK), slice(KEY_BLOCK, CHUNK)

    def body(q_ref, k_ref, v_ref, kb_ref, vb_ref, o_ref, st_ref):
        i = pl.program_id(1)
        cur_mask, prev_mask = _band_masks()
        before_mask = jnp.logical_and(prev_mask, i > 0)
        lane = lax.broadcasted_iota(jnp.int32, (KEY_BLOCK, STAT_LANES), 1)
        tasks = [(rows, h) for rows in (lo, hi) for h in range(N_HEADS)]
        head = lambda h: slice(h * HEAD_DIM, (h + 1) * HEAD_DIM)

        def prev_of(rows, h):
            if rows is lo:
                return kb_ref[:, head(h)], vb_ref[:, head(h)], before_mask
            return k_ref[lo, head(h)], v_ref[lo, head(h)], prev_mask

        scores = []
        for rows, h in tasks:
            q = q_ref[rows, head(h)]
            scores.append((_dot_nt(q, prev_of(rows, h)[0]), _dot_nt(q, k_ref[rows, head(h)])))
        probs = []
        for (rows, h), (qk_prev, qk_cur) in zip(tasks, scores):
            s_prev = jnp.where(prev_of(rows, h)[2], qk_prev * scale, NEG)
            s_cur = jnp.where(cur_mask, qk_cur * scale, NEG)
            m = jnp.max(jnp.maximum(s_prev, s_cur), axis=-1, keepdims=True)
            p_prev = jnp.exp(s_prev - m)
            p_cur = jnp.exp(s_cur - m)
            den = jnp.sum(p_prev + p_cur, axis=-1, keepdims=True)
            probs.append((p_prev.astype(BF16), p_cur.astype(BF16), den, m + jnp.log(den)))
        stats = [jnp.zeros((KEY_BLOCK, STAT_LANES), F32), jnp.zeros((KEY_BLOCK, STAT_LANES), F32)]
        for (rows, h), (p_prev, p_cur, den, lse) in zip(tasks, probs):
            o = _dot_nn(p_cur, v_ref[rows, head(h)]) + _dot_nn(p_prev, prev_of(rows, h)[1])
            o_ref[rows, head(h)] = (o / den).astype(BF16)
            b = 0 if rows is lo else 1
            stats[b] = jnp.where(lane == h, lse, stats[b])
        st_ref[lo, :] = stats[0]
        st_ref[hi, :] = stats[1]

    main = lambda cb: pl.BlockSpec((None, CHUNK, D_ATTN), lambda r, i: (r, i, cb))
    before = lambda cb: pl.BlockSpec((None, KEY_BLOCK, D_ATTN), lambda r, i: (r, jnp.maximum(2 * i - 1, 0), cb))
    return _pallas(
        body, name=name, grid=(dil, n // CHUNK),
        in_specs=[main(0), main(1), main(2), before(1), before(2)],
        out_specs=[main(0), pl.BlockSpec((None, CHUNK, STAT_LANES), lambda r, i: (r, i, 0))],
        out_shape=[jax.ShapeDtypeStruct((dil, n, D_ATTN), BF16), jax.ShapeDtypeStruct((dil, n, STAT_LANES), F32)],
        compiler_params=_params(("parallel", "parallel"), 40),
    )(qkv, qkv, qkv, qkv, qkv)


def _attn_bwd(qkv, do, stats, name, comm=None):
    dil, n, _ = qkv.shape
    n_blocks = n // KEY_BLOCK
    last = n // CHUNK - 1
    scale = HEAD_DIM ** -0.5
    lo, hi = slice(0, KEY_BLOCK), slice(KEY_BLOCK, CHUNK)

    def body(q_ref, k_ref, v_ref, kb_ref, vb_ref, qa_ref, do_ref, doa_ref, st_ref, sta_ref, dq_ref, dk_ref, dv_ref):
        i = pl.program_id(1)
        cur_mask, prev_mask = _band_masks()
        before_mask = jnp.logical_and(prev_mask, i > 0)
        after_mask = jnp.logical_and(prev_mask, i < last)

        rows_cat = lambda a, b: jnp.concatenate([a, b], axis=0)
        masks = (jnp.concatenate([before_mask, cur_mask], axis=1), jnp.concatenate([prev_mask, cur_mask], axis=1),
                 after_mask)

        def operands(h):
            cols = slice(h * HEAD_DIM, (h + 1) * HEAD_DIM)
            lse_c, del_c = slice(h, h + 1), slice(N_HEADS + h, N_HEADS + h + 1)
            q = (q_ref[lo, cols], q_ref[hi, cols], qa_ref[:, cols])
            do = (do_ref[lo, cols], do_ref[hi, cols], doa_ref[:, cols])
            keys = (rows_cat(kb_ref[:, cols], k_ref[lo, cols]), k_ref[:, cols], k_ref[hi, cols])
            vals = (rows_cat(vb_ref[:, cols], v_ref[lo, cols]), v_ref[:, cols], v_ref[hi, cols])
            st = ((st_ref[lo, lse_c], st_ref[lo, del_c]), (st_ref[hi, lse_c], st_ref[hi, del_c]),
                  (sta_ref[:, lse_c], sta_ref[:, del_c]))
            return cols, q, do, keys, vals, st

        group = N_HEADS // 2
        for first_head in range(0, N_HEADS, group):
            heads = range(first_head, first_head + group)
            raw = {}
            for h in heads:
                _, q, do, keys, vals, _ = operands(h)
                raw[h] = [(_dot_nt(q[j], keys[j]), _dot_nt(do[j], vals[j])) for j in range(3)]
            grads = {}
            for h in heads:
                st = operands(h)[5]
                grads[h] = []
                for j in range(3):
                    qk, dp = raw[h][j]
                    lse, delta = st[j]
                    p = jnp.exp(jnp.where(masks[j], qk * scale, NEG) - lse)
                    grads[h].append((p.astype(BF16), (p * (dp - delta) * scale).astype(BF16)))
            for h in heads:
                cols, q, do, keys, _, _ = operands(h)
                (p0, ds0), (p1, ds1), (pa, dsa) = grads[h]
                own, nxt = slice(KEY_BLOCK, CHUNK), slice(0, KEY_BLOCK)

                def put(ref, rows, val, cols=cols):
                    ref[rows, cols] = val.astype(ref.dtype)

                put(dq_ref, lo, _dot_nn(ds0, keys[0]))
                put(dq_ref, hi, _dot_nn(ds1, keys[1]))
                put(dk_ref, lo, _dot_tn(rows_cat(ds0[:, own], ds1[:, nxt]), q_ref[:, cols]))
                put(dk_ref, hi, _dot_tn(rows_cat(ds1[:, own], dsa), rows_cat(q[1], q[2])))
                put(dv_ref, lo, _dot_tn(rows_cat(p0[:, own], p1[:, nxt]), do_ref[:, cols]))
                put(dv_ref, hi, _dot_tn(rows_cat(p1[:, own], pa), rows_cat(do[1], do[2])))

    def spec(rows, width, row_of, cb):
        return pl.BlockSpec((None, rows, width), lambda r, i: (r, row_of(i), cb))

    same = lambda i: i
    before = lambda i: jnp.maximum(2 * i - 1, 0)
    after = lambda i: jnp.minimum(2 * i + 2, n_blocks - 1)
    out = spec(CHUNK, D_ATTN, same, 0)
    return _call(
        body, name=name, grid=(dil, n // CHUNK),
        in_specs=[spec(CHUNK, D_ATTN, same, 0), spec(CHUNK, D_ATTN, same, 1), spec(CHUNK, D_ATTN, same, 2),
                  spec(KEY_BLOCK, D_ATTN, before, 1), spec(KEY_BLOCK, D_ATTN, before, 2),
                  spec(KEY_BLOCK, D_ATTN, after, 0),
                  spec(CHUNK, D_ATTN, same, 0), spec(KEY_BLOCK, D_ATTN, after, 0),
                  spec(CHUNK, STAT_LANES, same, 0), spec(KEY_BLOCK, STAT_LANES, after, 0)],
        out_specs=[out, out, out],
        out_shape=[jax.ShapeDtypeStruct((dil, n, D_ATTN), BF16)] * 3,
        scratch_shapes=[], semantics=("parallel", "parallel"), vmem_mib=40,
        args=(qkv, qkv, qkv, qkv, qkv, qkv, do, do, stats, stats), comm=comm)


def _window_sums(ext, window, backward):
    rows = ext.shape[0]
    acc, span = ext, 1
    while span < window:
        acc = acc + pltpu.roll(acc, (rows - span) if backward else span, axis=0)
        span *= 2
    return acc


def _pool_group_weight(wp_ref, g):
    return jnp.concatenate([wp_ref[k, g] for k in range(N_SHARDS)], axis=0)


def _mix_gate(o_list, st_list, hug, w_pool_g, pool_scale):
    seq = hug.shape[0]
    tm = 256
    halo_blocks = tm // POOL_HALO
    d4, d16 = DILATIONS[1], DILATIONS[2]

    def body(o1_ref, o4_ref, o16_ref, l1_ref, l4_ref, l16_ref, u_ref, halo_ref, ga_ref, gp_ref, wp_ref, sc_ref,
             y_ref, mix_ref, lse_ref, pooled_ref, n4_ref, n16_ref, nl4_ref, nl16_ref):
        i = pl.program_id(0)
        _from_pattern(o4_ref, n4_ref, d4)
        _from_pattern(o16_ref, n16_ref, d16)
        _from_pattern(l4_ref, nl4_ref, d4)
        _from_pattern(l16_ref, nl16_ref, d16)
        la, lb, lc = l1_ref[...], nl4_ref[0], nl16_ref[0]
        mx = jnp.maximum(jnp.maximum(la, lb), lc)
        ea, eb, ec = jnp.exp(la - mx), jnp.exp(lb - mx), jnp.exp(lc - mx)
        tot = ea + eb + ec
        lse_ref[...] = mx + jnp.log(tot)
        wa, wb, wc = ea / tot, eb / tot, ec / tot
        ga = ga_ref[...].astype(F32)
        silu_a = ga * jax.nn.sigmoid(ga)
        for h in range(N_HEADS):
            cols = slice(h * HEAD_DIM, (h + 1) * HEAD_DIM)
            hc = slice(h, h + 1)
            attn = wa[:, hc] * o1_ref[:, cols].astype(F32) + wb[:, hc] * n4_ref[h] + wc[:, hc] * n16_ref[h]
            mix_ref[:, cols] = attn.astype(BF16)
            y_ref[:, cols] = (attn * silu_a[:, cols]).astype(BF16)

        u = u_ref[...].astype(F32)
        halo = jnp.where(i > 0, halo_ref[...].astype(F32), 0.0)
        ext = jnp.concatenate([halo, u], axis=0)
        pos = i * tm + lax.broadcasted_iota(jnp.int32, (tm, 1), 0)
        gp = gp_ref[...].astype(F32)
        gated_scale = sc_ref[...] * (gp * jax.nn.sigmoid(gp))
        for g, window in enumerate(POOL_WINDOWS):
            cols = slice(g * POOL_GROUP_DIM, (g + 1) * POOL_GROUP_DIM)
            sums = _window_sums(ext[:, cols], window, backward=False)[POOL_HALO:, :]
            count = jnp.minimum(pos + 1, window).astype(F32)
            pooled = (sums / count - u[:, cols]).astype(BF16)
            pooled_ref[:, cols] = pooled
            pre = _dot_nn(pooled, _pool_group_weight(wp_ref, g))
            out_cols = slice(D_ATTN + g * POOL_GROUP_DIM, D_ATTN + (g + 1) * POOL_GROUP_DIM)
            mix_ref[:, out_cols] = pre.astype(BF16)
            y_ref[:, out_cols] = (pre * gated_scale[:, cols]).astype(BF16)

    row = lambda width, cb=0: pl.BlockSpec((tm, width), lambda i: (i, cb))
    pat = lambda d, width: pl.BlockSpec((d, tm // d, width), lambda i: (0, i, 0))
    return _pallas(
        body, name="mix_gate", grid=(seq // tm,),
        in_specs=[row(D_ATTN), pat(d4, D_ATTN), pat(d16, D_ATTN),
                  row(STAT_LANES), pat(d4, STAT_LANES), pat(d16, STAT_LANES),
                  row(D_POOL),
                  pl.BlockSpec((POOL_HALO, D_POOL), lambda i: (jnp.maximum(i * halo_blocks - 1, 0), 0)),
                  row(D_ATTN, 1), row(D_POOL, 2),
                  pl.BlockSpec(w_pool_g.shape, lambda i: (0, 0, 0, 0)),
                  pl.BlockSpec((1, D_POOL), lambda i: (0, 0))],
        out_specs=[row(D_MODEL), row(D_MODEL), row(STAT_LANES), row(D_POOL)],
        out_shape=[jax.ShapeDtypeStruct((seq, D_MODEL), BF16), jax.ShapeDtypeStruct((seq, D_MODEL), BF16),
                   jax.ShapeDtypeStruct((seq, STAT_LANES), F32), jax.ShapeDtypeStruct((seq, D_POOL), BF16)],
        scratch_shapes=[pltpu.VMEM((N_HEADS, tm, HEAD_DIM), F32), pltpu.VMEM((N_HEADS, tm, HEAD_DIM), F32),
                        pltpu.VMEM((1, tm, STAT_LANES), F32), pltpu.VMEM((1, tm, STAT_LANES), F32)],
        compiler_params=_params(("parallel",), 48),
    )(o_list[0][0], o_list[1], o_list[2], st_list[0][0], st_list[1], st_list[2],
      hug, hug, hug, hug, w_pool_g, pool_scale)


def _out_proj_loss(y, w_out_g, x, target, gain, bias):
    seq = x.shape[0]
    tm = 512

    def body(y_ref, w_ref, x_ref, t_ref, g_ref, b_ref, dz_ref, dzb_ref, gg_ref, gb_ref, loss_ref):
        @pl.when(pl.program_id(0) == 0)
        def _():
            gg_ref[...] = jnp.zeros_like(gg_ref)
            gb_ref[...] = jnp.zeros_like(gb_ref)
            loss_ref[...] = jnp.zeros_like(loss_ref)

        halves = [slice(0, tm // 2), slice(tm // 2, tm)]
        projected = [_dot_nn(y_ref[rows, :], w_ref[...]) for rows in halves]
        for rows, out in zip(halves, projected):
            z = DEEPNORM_ALPHA * x_ref[rows, :] + out
            mu = jnp.mean(z, axis=-1, keepdims=True)
            zc = z - mu
            rstd = lax.rsqrt(jnp.mean(zc * zc, axis=-1, keepdims=True) + LN_EPS)
            xhat = zc * rstd
            gain_v = g_ref[...]
            diff = xhat * gain_v + b_ref[...] - t_ref[rows, :]
            sq = _fold_rows(diff * diff)
            part = sq[:, :128]
            for k in range(1, D_MODEL // 128):
                part = part + sq[:, k * 128:(k + 1) * 128]
            loss_ref[...] += part
            dln = diff * (1.0 / D_MODEL)
            gg_ref[...] += _fold_rows(dln * xhat)
            gb_ref[...] += _fold_rows(dln)
            dxhat = dln * gain_v
            dz = rstd * (dxhat - jnp.mean(dxhat, axis=-1, keepdims=True)
                         - xhat * jnp.mean(dxhat * xhat, axis=-1, keepdims=True))
            dz_ref[rows, :] = dz
            dzb_ref[rows, :] = dz.astype(BF16)

    row = lambda: pl.BlockSpec((tm, D_MODEL), lambda i: (i, 0))
    vec = lambda: pl.BlockSpec((1, D_MODEL), lambda i: (0, 0))
    acc = lambda width: pl.BlockSpec((8, width), lambda i: (0, 0))
    return _pallas(
        body, name="out_proj_loss", grid=(seq // tm,),
        in_specs=[row(), pl.BlockSpec((D_MODEL, D_MODEL), lambda i: (0, 0), pipeline_mode=pl.Buffered(1)),
                  row(), row(), vec(), vec()],
        out_specs=[row(), row(), acc(D_MODEL), acc(D_MODEL), acc(128)],
        out_shape=[jax.ShapeDtypeStruct((seq, D_MODEL), F32), jax.ShapeDtypeStruct((seq, D_MODEL), BF16),
                   jax.ShapeDtypeStruct((8, D_MODEL), F32), jax.ShapeDtypeStruct((8, D_MODEL), F32),
                   jax.ShapeDtypeStruct((8, 128), F32)],
        compiler_params=_params(("arbitrary",), 56),
    )(y, w_out_g.reshape(D_MODEL, D_MODEL), x, target, gain, bias)


def _dy_gate_bwd(dzb, w_out_g, hug, mixpre, pool_scale, lse_all):
    seq = dzb.shape[0]
    tm = 256
    d4, d16 = DILATIONS[1], DILATIONS[2]

    def body(dz_ref, w_ref, ga_ref, gp_ref, mix_ref, sc_ref, lse_ref,
             dh_ref, dpo_ref, do1_ref, do4_ref, do16_ref, st1_ref, st4_ref, st16_ref, da_ref, st_ref):
        dy = _dot_nt(dz_ref[...], w_ref[...])
        ga = ga_ref[...].astype(F32)
        sig = jax.nn.sigmoid(ga)
        attn = mix_ref[:, :D_ATTN].astype(F32)
        dya = dy[:, :D_ATTN]
        dattn = dya * (ga * sig)
        dh_ref[:, :D_ATTN] = (dya * attn * (sig * (1.0 + ga * (1.0 - sig)))).astype(BF16)
        _store_slabs(da_ref, dattn)
        lane = lax.broadcasted_iota(jnp.int32, (tm, STAT_LANES), 1)
        stats = lse_ref[...]
        prod = dattn * attn
        for h in range(N_HEADS):
            delta = jnp.sum(prod[:, h * HEAD_DIM:(h + 1) * HEAD_DIM], axis=-1, keepdims=True)
            stats = jnp.where(lane == N_HEADS + h, delta, stats)
        st_ref[0] = stats
        do1_ref[...] = dattn.astype(BF16)
        st1_ref[...] = stats
        _to_pattern(da_ref, do4_ref, d4, BF16)
        _to_pattern(da_ref, do16_ref, d16, BF16)
        _to_pattern(st_ref, st4_ref, d4, F32)
        _to_pattern(st_ref, st16_ref, d16, F32)

        gp = gp_ref[...].astype(F32)
        sig = jax.nn.sigmoid(gp)
        dyp = dy[:, D_ATTN:]
        dpo_ref[...] = (dyp * (gp * sig)).astype(BF16)
        dh_ref[:, D_ATTN:] = (dyp * (mix_ref[:, D_ATTN:].astype(F32) * sc_ref[...])
                              * (sig * (1.0 + gp * (1.0 - sig)))).astype(BF16)

    row = lambda width, cb=0: pl.BlockSpec((tm, width), lambda i: (i, cb))
    pat = lambda d, width: pl.BlockSpec((d, tm // d, width), lambda i: (0, i, 0))
    pat_shape = lambda d, width, dtype: jax.ShapeDtypeStruct((d, seq // d, width), dtype)
    outs = _pallas(
        body, name="dy_gate_bwd", grid=(seq // tm,),
        in_specs=[row(D_MODEL), pl.BlockSpec((D_MODEL, D_MODEL), lambda i: (0, 0)),
                  row(D_ATTN, 1), row(D_POOL, 2), row(D_MODEL), pl.BlockSpec((1, D_POOL), lambda i: (0, 0)),
                  row(STAT_LANES)],
        out_specs=[row(D_MODEL, D_IN // D_MODEL - 1), row(D_POOL),
                   row(D_ATTN), pat(d4, D_ATTN), pat(d16, D_ATTN),
                   row(STAT_LANES), pat(d4, STAT_LANES), pat(d16, STAT_LANES)],
        out_shape=[jax.ShapeDtypeStruct((seq, D_IN), BF16), jax.ShapeDtypeStruct((seq, D_POOL), BF16),
                   jax.ShapeDtypeStruct((seq, D_ATTN), BF16), pat_shape(d4, D_ATTN, BF16), pat_shape(d16, D_ATTN, BF16),
                   jax.ShapeDtypeStruct((seq, STAT_LANES), F32), pat_shape(d4, STAT_LANES, F32),
                   pat_shape(d16, STAT_LANES, F32)],
        scratch_shapes=[pltpu.VMEM((N_HEADS, tm, HEAD_DIM), F32), pltpu.VMEM((1, tm, STAT_LANES), F32)],
        compiler_params=_params(("parallel",), 48),
    )(dzb, w_out_g.reshape(D_MODEL, D_MODEL), hug, hug, mixpre, pool_scale, lse_all)
    dh, dpo, do1, do4, do16, st1, st4, st16 = outs
    return dh, dpo, [do1[None], do4, do16], [st1[None], st4, st16]


def _pool_bwd(dh, dpo, mixpre, pooled, w_pool_g, pool_scale):
    seq = dpo.shape[0]
    tm = 256
    halo_blocks = tm // POOL_HALO
    last = seq // tm - 1
    n_groups = len(POOL_WINDOWS)
    half_c = POOL_GROUP_DIM // N_SHARDS // 2
    pieces = (N_SHARDS, 2, n_groups * half_c, POOL_GROUP_DIM)

    def body(dh_in_ref, dpo_ref, halo_ref, pre_ref, pooled_ref, wp_ref, sc_ref, du_ref, gw_ref, gs_ref):
        i = pl.program_id(0)

        @pl.when(i == 0)
        def _():
            gw_ref[...] = jnp.zeros_like(gw_ref)
            gs_ref[...] = jnp.zeros_like(gs_ref)

        dpo = dpo_ref[...].astype(F32)
        scale = sc_ref[...]
        gs_ref[...] += _fold_rows(dpo * pre_ref[...].astype(F32))
        halo = jnp.where(i < last, halo_ref[...].astype(F32), 0.0)
        dpw = (jnp.concatenate([dpo, halo], axis=0) * scale).astype(BF16)
        pos = i * tm + lax.broadcasted_iota(jnp.int32, (tm + POOL_HALO, 1), 0)
        for g, window in enumerate(POOL_WINDOWS):
            cols = slice(g * POOL_GROUP_DIM, (g + 1) * POOL_GROUP_DIM)
            dpw_g = dpw[:, cols]
            gw = _dot_tn(pooled_ref[:, cols], dpw_g[:tm, :])
            for piece in range(2 * N_SHARDS):
                gw_ref[piece // 2, piece % 2, g * half_c:(g + 1) * half_c, :] += gw[piece * half_c:(piece + 1) * half_c]
            dpooled = _dot_nt(dpw_g, _pool_group_weight(wp_ref, g))
            count = jnp.minimum(pos + 1, window).astype(F32)
            sums = _window_sums(dpooled / count, window, backward=True)
            du_ref[:, cols] = (sums[:tm, :] - dpooled[:tm, :]).astype(BF16)

    row = lambda width, cb=0: pl.BlockSpec((tm, width), lambda i: (i, cb))
    return _pallas(
        body, name="pool_bwd", grid=(seq // tm,),
        in_specs=[ANY, row(D_POOL),
                  pl.BlockSpec((POOL_HALO, D_POOL),
                               lambda i: (jnp.minimum((i + 1) * halo_blocks, seq // POOL_HALO - 1), 0)),
                  row(D_POOL, 1), row(D_POOL),
                  pl.BlockSpec(w_pool_g.shape, lambda i: (0, 0, 0, 0)),
                  pl.BlockSpec((1, D_POOL), lambda i: (0, 0))],
        out_specs=[row(D_POOL, D_QKV // D_POOL),
                   pl.BlockSpec(pieces, lambda i: (0, 0, 0, 0)),
                   pl.BlockSpec((8, D_POOL), lambda i: (0, 0))],
        out_shape=[jax.ShapeDtypeStruct(dh.shape, dh.dtype),
                   jax.ShapeDtypeStruct(pieces, F32),
                   jax.ShapeDtypeStruct((8, D_POOL), F32)],
        input_output_aliases={0: 0},
        compiler_params=_params(("arbitrary",), 40),
    )(dh, dpo, dpo, mixpre, pooled, w_pool_g, pool_scale)


def _sum_patterns(dh, parts, tabs, unrotate, col_block, name, comm=None):
    seq = dh.shape[0]
    tm, tn = 256, D_ATTN
    per = D_ATTN // tn
    d4, d16 = DILATIONS[1], DILATIONS[2]

    def body(dh_in_ref, a1_ref, a4_ref, a16_ref, ct_ref, up_ref, down_ref, o_ref, n4_ref, n16_ref):
        _from_pattern(a4_ref, n4_ref, d4)
        _from_pattern(a16_ref, n16_ref, d16)
        for s in range(tn // HEAD_DIM):
            cols = slice(s * HEAD_DIM, (s + 1) * HEAD_DIM)
            tot = a1_ref[:, cols].astype(F32) + n4_ref[s] + n16_ref[s]
            if unrotate:
                tot = _rotate_heads(tot, ct_ref[...], -up_ref[...], -down_ref[...])
            o_ref[:, cols] = tot.astype(BF16)

    tab = pl.BlockSpec((tm, HEAD_DIM), lambda i, j: (i, 0))
    pat = lambda d: pl.BlockSpec((d, tm // d, tn), lambda i, j: (0, i, j))
    (dh,), exchanged = _call(
        body, name=name, grid=(seq // tm, per),
        in_specs=[ANY, pl.BlockSpec((tm, tn), lambda i, j: (i, j)), pat(d4), pat(d16), tab, tab, tab],
        out_specs=[pl.BlockSpec((tm, tn), lambda i, j: (i, col_block * per + j))],
        out_shape=[jax.ShapeDtypeStruct(dh.shape, dh.dtype)],
        scratch_shapes=[pltpu.VMEM((tn // HEAD_DIM, tm, HEAD_DIM), F32), pltpu.VMEM((tn // HEAD_DIM, tm, HEAD_DIM), F32)],
        semantics=("parallel", "parallel"), vmem_mib=32, args=(dh, parts[0][0], parts[1], parts[2], *tabs),
        aliases={0: 0}, comm=comm)
    return dh, exchanged


def _grad_w_in(x, dh, half, name, comm=None):
    seq = x.shape[0]
    ts, td, te = 2048, D_MODEL // 2, SHARD_IN

    def body(half_ref, x_ref, dh_ref, o_ref):
        k = pl.program_id(1)
        part = _dot_tn(x_ref[...].astype(BF16), dh_ref[...])

        @pl.when(k == 0)
        def _():
            o_ref[...] = part

        @pl.when(k > 0)
        def _():
            o_ref[...] += part

    (g,), exchanged = _call(
        body, name=name, grid=(N_SHARDS, seq // ts),
        in_specs=[pl.BlockSpec((ts, td), lambda e, k, half_ref: (k, half_ref[0])),
                  pl.BlockSpec((ts, te), lambda e, k, half_ref: (k, e))],
        out_specs=[pl.BlockSpec((None, td, te), lambda e, k, half_ref: (e, 0, 0))],
        out_shape=[jax.ShapeDtypeStruct((N_SHARDS, td, te), F32)],
        scratch_shapes=[], semantics=("parallel", "arbitrary"), vmem_mib=56, args=(x, dh), comm=comm,
        prefetch=(half,))
    return g, exchanged


def _grad_w_out(y, dzb):
    seq = y.shape[0]
    ts, te = 2048, 1024

    def body(y_ref, dz_ref, o_ref):
        k = pl.program_id(1)
        part = _dot_tn(y_ref[...], dz_ref[...])

        @pl.when(k == 0)
        def _():
            o_ref[...] = part

        @pl.when(k > 0)
        def _():
            o_ref[...] += part

    return _pallas(
        body, name="grad_w_out", grid=(D_MODEL // te, seq // ts),
        in_specs=[pl.BlockSpec((ts, te), lambda e, k: (k, e)), pl.BlockSpec((ts, D_MODEL), lambda e, k: (k, 0))],
        out_specs=pl.BlockSpec((te, D_MODEL), lambda e, k: (e, 0)),
        out_shape=jax.ShapeDtypeStruct((D_MODEL, D_MODEL), F32),
        compiler_params=_params(("parallel", "arbitrary"), 56),
    )(y, dzb)


GRAD_X_LATE_SHARDS = 1
GRAD_X_PARTIAL_ROWS = 512


def _grad_x_partial(dh, w_in_g, dz, first, tiles, prev=None, comm=None):
    seq = dh.shape[0]
    tm, tk = GRAD_X_PARTIAL_ROWS, SHARD_IN

    def body(*refs):
        dh_ref, w_ref, dz_ref, o_ref = refs[-4:]
        k = pl.program_id(1)
        part = _dot_nt(dh_ref[...], w_ref[...])

        @pl.when(k == 0)
        def _():
            o_ref[...] = DEEPNORM_ALPHA * dz_ref[...] + part

        @pl.when(k > 0)
        def _():
            o_ref[...] += part

    carried = [] if prev is None else [prev]
    row = pl.BlockSpec((tm, D_MODEL), lambda i, k: (i + first, 0))
    (partial,), exchanged = _call(
        body, name="grad_x_partial_%d" % first, grid=(tiles, N_SHARDS - GRAD_X_LATE_SHARDS),
        in_specs=[ANY] * len(carried) + [
            pl.BlockSpec((tm, tk), lambda i, k: (i + first, k)),
            pl.BlockSpec((None, D_MODEL, tk), lambda i, k: (k, 0, 0)), row],
        out_specs=[row],
        out_shape=[jax.ShapeDtypeStruct((seq, D_MODEL), F32)],
        scratch_shapes=[], semantics=("parallel", "arbitrary"), vmem_mib=48, args=(*carried, dh, w_in_g, dz),
        aliases={0: 0} if carried else None, comm=comm)
    return partial, exchanged


def _grad_x_final(dh, w_in_g, partial):
    seq = dh.shape[0]
    tm, tk = 512, SHARD_IN
    k0 = N_SHARDS - GRAD_X_LATE_SHARDS

    def body(dh_ref, w_ref, p_ref, o_ref):
        k = pl.program_id(1)
        part = _dot_nt(dh_ref[...], w_ref[...])

        @pl.when(k == 0)
        def _():
            o_ref[...] = p_ref[...] + part

        @pl.when(k > 0)
        def _():
            o_ref[...] += part

    row = pl.BlockSpec((tm, D_MODEL), lambda i, k: (i, 0))
    return _pallas(
        body, name="grad_x_final", grid=(seq // tm, GRAD_X_LATE_SHARDS),
        in_specs=[pl.BlockSpec((tm, tk), lambda i, k: (i, k + k0)),
                  pl.BlockSpec((None, D_MODEL, tk), lambda i, k: (k + k0, 0, 0)), row],
        out_specs=row, out_shape=jax.ShapeDtypeStruct((seq, D_MODEL), F32),
        compiler_params=_params(("parallel", "arbitrary"), 48),
    )(dh, w_in_g, partial)


def _pool_weight(w_pool_sh):
    n_groups = len(POOL_WINDOWS)
    shard_c = POOL_GROUP_DIM // N_SHARDS
    return w_pool_sh.reshape(N_SHARDS, n_groups, shard_c, POOL_GROUP_DIM)


def _step(x, target, w_bufs, pool_scale, gain, bias, place):
    seq = x.shape[0]
    tabs = _rope_tables(seq)
    core, chip_core, onward, plan = place
    qkv, hug, w_in_g, w_out_g, w_pool_sh = _in_proj_gathering(x, w_bufs, tabs, plan)
    o_list, st_list = [], []
    for p, dil in enumerate(DILATIONS):
        o, st = _attn_fwd(qkv[p], "attn_fwd_d%d" % dil)
        o_list.append(o)
        st_list.append(st)
    w_pool_g = _pool_weight(w_pool_sh)
    y, mixpre, lse_all, pooled = _mix_gate(o_list, st_list, hug, w_pool_g, pool_scale)
    dz, dzb, gain_part, bias_part, loss_part = _out_proj_loss(y, w_out_g, x, target, gain, bias)
    dh, dpo, do_list, stat_list = _dy_gate_bwd(dzb, w_out_g, hug, mixpre, pool_scale, lse_all)
    g_w_out = _grad_w_out(y, dzb)
    dh, g_w_pool, scale_part = _pool_bwd(dh, dpo, mixpre, pooled, w_pool_g, pool_scale)
    small = jnp.concatenate([scale_part, gain_part, bias_part, loss_part], axis=1)
    early = [g_w_out.reshape(N_SHARDS, 2, D_MODEL // (2 * N_SHARDS), D_MODEL), g_w_pool]

    bwd = lambda p, comm: _attn_bwd(qkv[p], do_list[p], stat_list[p], "attn_bwd_d%d" % DILATIONS[p], comm)
    part_a, halves = bwd(0, _exchange_halves(early))
    sums_b = [_add_own_half(g, h, core, "add_own_half_%d" % a) for a, (g, h) in enumerate(zip(early, halves))]
    part_b, recv = bwd(1, _scatter_to_chips(sums_b))
    bufs = [_add_chips([g, h], r, chip_core, "add_chips_%d" % a)
            for a, (g, h, r) in enumerate(zip(early, halves, recv))]
    part_c, reduced = bwd(2, _share_with_sibling(bufs))
    parts = [part_a, part_b, part_c]
    dh, gathered = _sum_patterns(dh, [t[0] for t in parts], tabs, True, 0, "sum_dq", _gather_small(small))
    dh, _ = _sum_patterns(dh, [t[1] for t in parts], tabs, True, 1, "sum_dk")
    dh, _ = _sum_patterns(dh, [t[2] for t in parts], tabs, False, 2, "sum_dv")

    give, _ = _grad_w_in(x, dh, 1 - core, "grad_w_in_give")
    keep, recv = _grad_w_in(x, dh, core, "grad_w_in_keep", _send_to_sibling([give]))
    total = [keep, recv[0]]
    total_b = _add_pair(keep, recv[0], "add_own_half_w_in")
    n_tiles = seq // GRAD_X_PARTIAL_ROWS
    tiles = 3 * n_tiles // 8
    part, relayed = _grad_x_partial(dh, w_in_g, dz, 0, tiles, None, _relay_diagonal(total_b))
    total_b = _fold_relayed(total, total_b, relayed[0], onward)
    part, recv = _grad_x_partial(dh, w_in_g, dz, tiles, n_tiles - tiles, part, _scatter_to_neighbours(total_b))
    buf = _add_chips(total, recv[0], chip_core, "add_chips_w_in")
    g_x = _grad_x_final(dh, w_in_g, part)
    g_w_in = _run_exchange(_share_with_sibling([buf]), "share_w_in")[0]
    return g_x, g_w_in, reduced[0], reduced[1], small, gathered[0]


def _exchange_halves(grads):
    n = len(grads)

    def copies(src, dst, sems):
        x, y, c, _ = _mesh_place()
        return [_remote(src[a].at[j, 1 - c], dst[a].at[j], sems[0].at[a, j], sems[1].at[a, j], (x, y, 1 - c))
                for a in range(n) for j in range(N_SHARDS)]

    def start(src, dst, sems):
        for cp in copies(src, dst, sems):
            cp.start()

    def finish(src, dst, sems):
        for cp in copies(src, dst, sems):
            cp.wait()

    return _Exchange(grads, [jax.ShapeDtypeStruct((N_SHARDS,) + g.shape[2:], g.dtype) for g in grads], {},
                     [pltpu.SemaphoreType.DMA((n, N_SHARDS))] * 2, start, finish)


def _add_own_half(grad, recv, core, name):
    _, _, r, c = grad.shape
    tr = min(r, 256)

    def body(core_ref, g_ref, r_ref, ob_ref):
        ob_ref[...] = (g_ref[...] + r_ref[...]).astype(BF16)

    return _pallas(
        body, name=name,
        grid_spec=pltpu.PrefetchScalarGridSpec(
            num_scalar_prefetch=1, grid=(N_SHARDS, r // tr),
            in_specs=[pl.BlockSpec((None, None, tr, c), lambda j, i, core_ref: (j, core_ref[0], i, 0)),
                      pl.BlockSpec((None, tr, c), lambda j, i, core_ref: (j, i, 0))],
            out_specs=pl.BlockSpec((None, tr, c), lambda j, i, core_ref: (j, i, 0))),
        out_shape=jax.ShapeDtypeStruct((N_SHARDS, r, c), BF16),
        compiler_params=_params(("parallel", "parallel"), 32),
    )(core, grad, recv)


def _send_to_sibling(arrays):
    n = len(arrays)

    def copies(src, dst, sems):
        x, y, c, _ = _mesh_place()
        return [_remote(src[a], dst[a], sems[0].at[a], sems[1].at[a], (x, y, 1 - c)) for a in range(n)]

    def start(src, dst, sems):
        for cp in copies(src, dst, sems):
            cp.start()

    def finish(src, dst, sems):
        for cp in copies(src, dst, sems):
            cp.wait()

    return _Exchange(arrays, [jax.ShapeDtypeStruct(t.shape, t.dtype) for t in arrays], {},
                     [pltpu.SemaphoreType.DMA((n,))] * 2, start, finish)


def _add_pair(a, b, name):
    _, r, c = a.shape
    tr = min(r, 256)

    def body(a_ref, b_ref, ob_ref):
        ob_ref[...] = (a_ref[...] + b_ref[...]).astype(BF16)

    spec = pl.BlockSpec((None, tr, c), lambda j, i: (j, i, 0))
    return _pallas(
        body, name=name, grid=(N_SHARDS, r // tr), in_specs=[spec, spec], out_specs=spec,
        out_shape=jax.ShapeDtypeStruct(a.shape, BF16),
        compiler_params=_params(("parallel", "parallel"), 32),
    )(a, b)


def _scatter_to_chips(sums):
    n = len(sums)

    def copies(src, dst, sems):
        x, y, c, chips = _mesh_place()
        return [_remote(src[a].at[2 * cx + cy], dst[a].at[k], sems[0].at[a, k], sems[1].at[a, k], (cx, cy, c))
                for a in range(n) for k, (cx, cy) in enumerate(chips)]

    def start(src, dst, sems):
        for cp in copies(src, dst, sems):
            cp.start()

    def finish(src, dst, sems):
        for cp in copies(src, dst, sems):
            cp.wait()

    return _Exchange(sums, [jax.ShapeDtypeStruct((3,) + s.shape[1:], s.dtype) for s in sums], {},
                     [pltpu.SemaphoreType.DMA((n, 3))] * 2, start, finish)


def _add_chips(sums, recv, chip_core, name):
    r, c = sums[0].shape[-2:]
    n_sums, n_recv = len(sums), recv.shape[0]
    tr = min(r, 256)
    mine = {3: pl.BlockSpec((None, tr, c), lambda i, cc_ref: (cc_ref[0], i, 0)),
            4: pl.BlockSpec((None, None, tr, c), lambda i, cc_ref: (cc_ref[0], cc_ref[1], i, 0))}

    def body(cc_ref, *refs):
        r_ref, o_ref = refs[n_sums:]
        tot = refs[0][...]
        for s_ref in refs[1:n_sums]:
            tot = tot + s_ref[...]
        for k in range(n_recv):
            tot = tot + r_ref[k].astype(F32)
        o_ref[...] = tot

    return _pallas(
        body, name=name,
        grid_spec=pltpu.PrefetchScalarGridSpec(
            num_scalar_prefetch=1, grid=(r // tr,),
            in_specs=[mine[s.ndim] for s in sums] + [pl.BlockSpec((n_recv, tr, c), lambda i, cc_ref: (0, i, 0))],
            out_specs=pl.BlockSpec((None, tr, c), lambda i, cc_ref: (cc_ref[1], i, 0))),
        out_shape=jax.ShapeDtypeStruct((2, r, c), F32),
        compiler_params=_params(("parallel",), 32),
    )(chip_core, *sums, recv)


def _relay_diagonal(sums_b):
    def copy(src, dst, sems):
        x, y, c, _ = _mesh_place()
        diagonal = 2 * (1 - x) + (1 - y)
        return _remote(src[0].at[diagonal], dst[0], sems[0].at[0], sems[1].at[0], (x ^ (1 - c), y ^ c, c))

    def start(src, dst, sems):
        copy(src, dst, sems).start()

    def finish(src, dst, sems):
        copy(src, dst, sems).wait()

    return _Exchange([sums_b], [jax.ShapeDtypeStruct(sums_b.shape[1:], sums_b.dtype)], {},
                     [pltpu.SemaphoreType.DMA((1,))] * 2, start, finish)


def _fold_relayed(sums, sums_b, relayed, onward):
    _, r, c = sums[0].shape
    n_sums = len(sums)
    tr = min(r, 256)

    def body(on_ref, b_in_ref, *refs):
        r_ref, o_ref = refs[n_sums:]
        tot = refs[0][...]
        for s_ref in refs[1:n_sums]:
            tot = tot + s_ref[...]
        o_ref[...] = (tot + r_ref[...].astype(F32)).astype(BF16)

    return _pallas(
        body, name="fold_relayed",
        grid_spec=pltpu.PrefetchScalarGridSpec(
            num_scalar_prefetch=1, grid=(r // tr,),
            in_specs=[ANY] + [pl.BlockSpec((None, tr, c), lambda i, on_ref: (on_ref[0], i, 0))] * n_sums
            + [pl.BlockSpec((tr, c), lambda i, on_ref: (i, 0))],
            out_specs=pl.BlockSpec((None, tr, c), lambda i, on_ref: (on_ref[0], i, 0))),
        out_shape=jax.ShapeDtypeStruct(sums_b.shape, sums_b.dtype),
        input_output_aliases={1: 0},
        compiler_params=_params(("parallel",), 32),
    )(onward, sums_b, *sums, relayed)


def _scatter_to_neighbours(sums_b):
    def copies(src, dst, sems):
        x, y, c, chips = _mesh_place()
        return [_remote(src[0].at[2 * cx + cy], dst[0].at[k], sems[0].at[k], sems[1].at[k], (cx, cy, c))
                for k, (cx, cy) in enumerate(chips[:2])]

    def start(src, dst, sems):
        for cp in copies(src, dst, sems):
            cp.start()

    def finish(src, dst, sems):
        for cp in copies(src, dst, sems):
            cp.wait()

    return _Exchange([sums_b], [jax.ShapeDtypeStruct((2,) + sums_b.shape[1:], sums_b.dtype)], {},
                     [pltpu.SemaphoreType.DMA((2,))] * 2, start, finish)


def _share_with_sibling(bufs):
    n = len(bufs)

    def copies(dst, sems, half):
        x, y, c, _ = _mesh_place()
        h = c if half == "mine" else 1 - c
        return [_remote(dst[a].at[h], dst[a].at[h], sems[0].at[a], sems[1].at[a], (x, y, 1 - c)) for a in range(n)]

    def start(ins, dst, sems):
        for cp in copies(dst, sems, "mine"):
            cp.start()

    def finish(ins, dst, sems):
        for cp in copies(dst, sems, "theirs"):
            cp.wait_recv()
        for cp in copies(dst, sems, "mine"):
            cp.wait_send()

    return _Exchange(bufs, [jax.ShapeDtypeStruct(b.shape, b.dtype) for b in bufs], {a: a for a in range(n)},
                     [pltpu.SemaphoreType.DMA((n,))] * 2, start, finish)


def _adam_math(w, g, m, v):
    m = ADAM_B1 * m + (1.0 - ADAM_B1) * g
    v = ADAM_B2 * v + (1.0 - ADAM_B2) * (g * g)
    m_hat = m / (1.0 - ADAM_B1 ** ADAM_STEP)
    v_hat = v / (1.0 - ADAM_B2 ** ADAM_STEP)
    delta = -ADAM_LR * (m_hat / (jnp.sqrt(v_hat) + ADAM_EPS) + ADAM_WD * w)
    return delta, m, v


def _gather_small(small):
    def peers():
        x, y, c, _ = _mesh_place()
        return [(x ^ ((r >> 2) & 1), y ^ ((r >> 1) & 1), c ^ (r & 1)) for r in range(1, 8)], 4 * x + 2 * y + c

    def start(src, dst, sems):
        to, me = peers()
        for r, peer in enumerate(to):
            _remote(src[0], dst[0].at[me], sems[0].at[r], sems[1].at[r], peer).start()

    def finish(src, dst, sems):
        to, me = peers()
        for r, (px, py, pc) in enumerate(to):
            theirs = dst[0].at[4 * px + 2 * py + pc]
            _remote(theirs, theirs, sems[0].at[r], sems[1].at[r], (px, py, pc)).wait_recv()
        for r, peer in enumerate(to):
            _remote(src[0], dst[0].at[me], sems[0].at[r], sems[1].at[r], peer).wait_send()

    return _Exchange([small], [jax.ShapeDtypeStruct((8,) + small.shape, small.dtype)], {},
                     [pltpu.SemaphoreType.DMA((7,))] * 2, start, finish)


def _small_adamw(gathered, small, me, w_vecs, m_vecs, v_vecs):
    n = len(w_vecs)
    widths = [w.shape[1] for w in w_vecs]
    n_par = sum(widths)

    def body(me_ref, a_ref, s_ref, *refs):
        w_refs, m_refs, v_refs = refs[:n], refs[n:2 * n], refs[2 * n:3 * n]
        loss_ref, outs = refs[3 * n], refs[3 * n + 1:]
        mine = s_ref[...]
        tot = jnp.where(me_ref[0] == 0, mine, a_ref[0])
        for d in range(1, 8):
            tot = tot + jnp.where(me_ref[0] == d, mine, a_ref[d])
        tot = jnp.sum(tot, axis=0, keepdims=True)
        sq = jnp.sum(tot[:, n_par:], axis=1, keepdims=True)
        loss_ref[...] = jnp.broadcast_to(sq * (0.5 / D_MODEL), loss_ref.shape)
        lo = 0
        for k in range(n):
            g = tot[:, lo:lo + widths[k]]
            lo += widths[k]
            outs[k][...] = g
            outs[n + k][...], outs[2 * n + k][...], outs[3 * n + k][...] = _adam_math(
                w_refs[k][...], g, m_refs[k][...], v_refs[k][...])

    vm = pl.BlockSpec(memory_space=pltpu.VMEM)
    vecs = [jax.ShapeDtypeStruct((1, w), F32) for w in widths] * 4
    res = pl.pallas_call(
        body, name="small_adamw",
        grid_spec=pltpu.PrefetchScalarGridSpec(num_scalar_prefetch=1, grid=(), in_specs=[vm] * (2 + 3 * n),
                                               out_specs=[vm] * (1 + 4 * n)),
        out_shape=[jax.ShapeDtypeStruct((1, 128), F32)] + vecs,
    )(me, gathered, small, *w_vecs, *m_vecs, *v_vecs)
    return res[0], res[1:1 + n], res[1 + n:1 + 2 * n], res[1 + 2 * n:1 + 3 * n], res[1 + 3 * n:]


def _adamw(w, g, m, v, name):
    r, c = w.shape
    tr = min(r, 256)

    def body(w_ref, g_ref, m_ref, v_ref, go_ref, d_ref, nm_ref, nv_ref):
        g = g_ref[...]
        go_ref[...] = g
        d_ref[...], nm_ref[...], nv_ref[...] = _adam_math(w_ref[...], g, m_ref[...], v_ref[...])

    spec = pl.BlockSpec((tr, c), lambda i: (i, 0))
    shape = jax.ShapeDtypeStruct((r, c), F32)
    return _pallas(
        body, name=name, grid=(r // tr,),
        in_specs=[spec] * 4, out_specs=[spec] * 4, out_shape=[shape] * 4,
        compiler_params=_params(("parallel",), 48),
    )(w, g, m, v)


def kernel(x, w_in, w_pool, pool_scale, w_out, ln_gain, ln_bias, loss_target, m_w_in, m_w_pool, m_pool_scale, m_w_out, m_ln_gain, m_ln_bias, v_w_in, v_w_pool, v_pool_scale, v_w_out, v_ln_gain, v_ln_bias):
    xi, yi, ci = lax.axis_index("x"), lax.axis_index("y"), lax.axis_index("c")
    chip = (2 * xi + yi).astype(jnp.int32).reshape(1)
    core = ci.astype(jnp.int32).reshape(1)
    n_groups = len(POOL_WINDOWS)
    shard_c = w_pool.shape[2]

    w_in_b = _cast_bf16(w_in[0], chip, "cast_w_in", 256)
    w_out_b = _cast_bf16(w_out[0], chip, "cast_w_out", 256)
    w_pool_b = _cast_bf16(w_pool[0].reshape(n_groups * shard_c, POOL_GROUP_DIM), chip, "cast_w_pool", 256)

    chip_core = jnp.concatenate([chip, core])
    onward = (2 * (xi ^ ci) + (yi ^ (1 - ci))).astype(jnp.int32).reshape(1)
    g_x, full_in, full_out, full_pool, small, small_all = _step(
        x[0], loss_target[0], [w_in_b, w_out_b, w_pool_b], pool_scale, ln_gain, ln_bias,
        (core, chip_core, onward, _in_proj_plan(xi, yi)))
    half_c = shard_c // 2
    grad_w_in = full_in.reshape(D_MODEL, SHARD_IN)
    grad_w_out = full_out.reshape(D_MODEL // N_SHARDS, D_MODEL)
    grad_w_pool = (full_pool.reshape(2, n_groups, half_c, POOL_GROUP_DIM).transpose(1, 0, 2, 3)
                   .reshape(n_groups * shard_c, POOL_GROUP_DIM))

    grad_w_in, d_in, nm_in, nv_in = _adamw(w_in[0], grad_w_in, m_w_in[0], v_w_in[0], "adamw_w_in")
    grad_w_out, d_out, nm_out, nv_out = _adamw(w_out[0], grad_w_out, m_w_out[0], v_w_out[0], "adamw_w_out")
    flat = lambda t: t[0].reshape(n_groups * shard_c, POOL_GROUP_DIM)
    grad_w_pool, d_pool, nm_pool, nv_pool = _adamw(flat(w_pool), grad_w_pool, flat(m_w_pool), flat(v_w_pool),
                                                   "adamw_w_pool")

    me = (4 * xi + 2 * yi + ci).astype(jnp.int32).reshape(1)
    loss_v, g_vecs, d_vecs, nm_vecs, nv_vecs = _small_adamw(
        small_all, small, me, [pool_scale, ln_gain, ln_bias], [m_pool_scale, m_ln_gain, m_ln_bias],
        [v_pool_scale, v_ln_gain, v_ln_bias])
    g_scale, g_gain, g_bias = g_vecs
    d_scale, d_gain, d_bias = d_vecs
    nm_scale, nm_gain, nm_bias = nm_vecs
    nv_scale, nv_gain, nv_bias = nv_vecs
    pool_shape = w_pool.shape
    return (loss_v[0, 0], g_x[None],
            grad_w_in[None], grad_w_pool.reshape(pool_shape), g_scale, grad_w_out[None], g_gain, g_bias,
            d_in[None], d_pool.reshape(pool_shape), d_scale, d_out[None], d_gain, d_bias,
            nm_in[None], nm_pool.reshape(pool_shape), nm_scale, nm_out[None], nm_gain, nm_bias,
            nv_in[None], nv_pool.reshape(pool_shape), nv_scale, nv_out[None], nv_gain, nv_bias)
```

```python
import functools

import jax
import jax.numpy as jnp
import numpy as np
from jax import lax
from jax.experimental import pallas as pl
from jax.experimental.pallas import tpu as pltpu

F32 = jnp.float32
BF16 = jnp.bfloat16
MESH = pl.DeviceIdType.MESH
ANY = pl.BlockSpec(memory_space=pl.ANY)

D_MODEL = 2048
D_ATTN = 1024
D_POOL = 1024
HEAD_DIM = 128
N_HEADS = 8
ROPE_DIM = 32
ROPE_THETA = 500000.0
DILATIONS = (1, 4, 16)
KEY_BLOCK = 128
CHUNK = 2 * KEY_BLOCK
STAT_LANES = 128
POOL_WINDOWS = (2, 4, 8, 16)
POOL_GROUP_DIM = 256
POOL_HALO = 16
D_QKV = 3 * D_ATTN
D_UG = D_POOL + D_MODEL
D_IN = D_QKV + D_UG
N_SHARDS = 4
SHARD_IN = D_IN // N_SHARDS
LN_EPS = 1e-5
DEEPNORM_ALPHA = 2.0 ** 0.25
ADAM_LR = 0.001
ADAM_B1 = 0.9
ADAM_B2 = 0.999
ADAM_EPS = 1e-08
ADAM_WD = 0.01
ADAM_STEP = 10
NEG = -1e30
MIB = 1024 * 1024


def _params(sem, vmem_mib):
    return pltpu.CompilerParams(dimension_semantics=sem, vmem_limit_bytes=vmem_mib * MIB)


def _pallas(body, **kwargs):
    pin = lambda s: pltpu.HBM(s.shape, s.dtype) if len(s.shape) >= 2 else s
    out_shape = kwargs.pop("out_shape")
    out_shape = [pin(s) for s in out_shape] if isinstance(out_shape, (list, tuple)) else pin(out_shape)
    call = pl.pallas_call(body, out_shape=out_shape, **kwargs)

    def run(*operands):
        return call(*[pltpu.with_memory_space_constraint(o, pltpu.HBM) if o.ndim >= 2 else o for o in operands])

    return run


class _Exchange:
    def __init__(self, operands, out_shape, aliases, sems, start, finish):
        self.operands, self.out_shape, self.aliases, self.sems = list(operands), list(out_shape), dict(aliases), list(sems)
        self.start, self.finish = start, finish


def _run_exchange(comm, name):
    n_in, n_out = len(comm.operands), len(comm.out_shape)

    def body(*refs):
        ins, outs, sems = refs[:n_in], refs[n_in:n_in + n_out], refs[n_in + n_out:]
        comm.start(ins, outs, sems)
        comm.finish(ins, outs, sems)

    return _pallas(
        body, name=name, in_specs=[ANY] * n_in, out_specs=[ANY] * n_out, out_shape=comm.out_shape,
        input_output_aliases=comm.aliases, scratch_shapes=comm.sems,
    )(*comm.operands)


def _call(body, *, name, grid, in_specs, out_specs, out_shape, scratch_shapes, semantics, vmem_mib, args,
          aliases=None, comm=None, prefetch=()):
    aliases = dict(aliases or {})
    n_pre, n_in, n_out, n_scr = len(prefetch), len(in_specs), len(out_specs), len(scratch_shapes)
    c_in, c_out = (len(comm.operands), len(comm.out_shape)) if comm else (0, 0)
    c_shapes, c_sems, c_operands = (comm.out_shape, comm.sems, comm.operands) if comm else ([], [], [])

    def hosted(*refs):
        pre, refs = refs[:n_pre], refs[n_pre:]
        a = n_in
        b = a + c_in
        c = b + n_out
        d = c + c_out
        e = d + n_scr
        if comm is None:
            body(*pre, *refs)
            return
        ids = [pl.program_id(k) for k in range(len(grid))]
        first = functools.reduce(jnp.logical_and, [i == 0 for i in ids])
        last = functools.reduce(jnp.logical_and, [i == g - 1 for i, g in zip(ids, grid)])

        @pl.when(first)
        def _():
            comm.start(refs[a:b], refs[c:d], refs[e:])

        body(*pre, *refs[:a], *refs[b:c], *refs[d:e])

        @pl.when(last)
        def _():
            comm.finish(refs[a:b], refs[c:d], refs[e:])

    if comm:
        semantics = ("arbitrary",) * len(grid)
        for i, o in comm.aliases.items():
            aliases[n_pre + n_in + i] = n_out + o
    outs = _pallas(
        hosted, name=name,
        grid_spec=pltpu.PrefetchScalarGridSpec(
            num_scalar_prefetch=n_pre, grid=grid, in_specs=list(in_specs) + [ANY] * c_in,
            out_specs=list(out_specs) + [ANY] * c_out, scratch_shapes=list(scratch_shapes) + c_sems),
        out_shape=list(out_shape) + c_shapes, input_output_aliases=aliases,
        compiler_params=_params(semantics, vmem_mib),
    )(*prefetch, *args, *c_operands)
    return list(outs[:n_out]), list(outs[n_out:])


def _dot_nn(a, b):
    return jnp.dot(a, b, preferred_element_type=F32)


def _dot_nt(a, b):
    return lax.dot_general(a, b, (((1,), (1,)), ((), ())), preferred_element_type=F32)


def _dot_tn(a, b):
    return lax.dot_general(a, b, (((0,), (0,)), ((), ())), preferred_element_type=F32)


def _fold_rows(a):
    r, c = a.shape
    return jnp.sum(a.reshape(r // 8, 8, c), axis=0)


def _cast_bf16(a, chip, name, rows):
    r, c = a.shape

    def body(chip_ref, a_ref, o_ref):
        o_ref[...] = a_ref[...].astype(BF16)

    return _pallas(
        body, name=name,
        grid_spec=pltpu.PrefetchScalarGridSpec(
            num_scalar_prefetch=1, grid=(r // rows,),
            in_specs=[pl.BlockSpec((rows, c), lambda i, chip_ref: (i, 0))],
            out_specs=pl.BlockSpec((None, rows, c), lambda i, chip_ref: (chip_ref[0], i, 0))),
        out_shape=jax.ShapeDtypeStruct((N_SHARDS, r, c), BF16),
        compiler_params=_params(("parallel",), 32),
    )(chip, a)


def _mesh_place():
    x, y, c = lax.axis_index("x"), lax.axis_index("y"), lax.axis_index("c")
    return x, y, c, [(1 - x, y), (x, 1 - y), (1 - x, 1 - y)]


def _remote(src, dst, send_sem, recv_sem, to):
    return pltpu.make_async_remote_copy(src_ref=src, dst_ref=dst, send_sem=send_sem, recv_sem=recv_sem,
                                        device_id=to, device_id_type=MESH)


def _rope_tables(seq):
    half = ROPE_DIM // 2
    inv_freq = (np.float64(ROPE_THETA) ** (-(2.0 * np.arange(half, dtype=np.float64)) / ROPE_DIM)).astype(np.float32)
    ang = np.arange(seq, dtype=np.float32)[:, None] * inv_freq[None, :]
    cos = np.cos(ang.astype(np.float64)).astype(np.float32)
    sin = np.sin(ang.astype(np.float64)).astype(np.float32)
    pad = np.zeros((seq, HEAD_DIM - ROPE_DIM), np.float32)
    zeros = np.zeros((seq, half), np.float32)
    c_tab = np.concatenate([cos, cos, pad + 1.0], axis=1)
    up_tab = np.concatenate([-sin, zeros, pad], axis=1)
    down_tab = np.concatenate([zeros, sin, pad], axis=1)
    return jnp.asarray(c_tab), jnp.asarray(up_tab), jnp.asarray(down_tab)


def _rotate_heads(t, c_tab, up_tab, down_tab):
    outs = []
    for h in range(t.shape[1] // HEAD_DIM):
        th = t[:, h * HEAD_DIM:(h + 1) * HEAD_DIM]
        up = pltpu.roll(th, HEAD_DIM - ROPE_DIM // 2, axis=1)
        down = pltpu.roll(th, ROPE_DIM // 2, axis=1)
        outs.append(th * c_tab + up * up_tab + down * down_tab)
    return outs[0] if len(outs) == 1 else jnp.concatenate(outs, axis=1)


def _to_pattern(slabs_ref, dst_ref, dil, dtype):
    n_slabs, rows, _ = slabs_ref.shape
    for s in range(n_slabs):
        for r in range(dil):
            dst_ref[r, :, s * 128:(s + 1) * 128] = slabs_ref[s, pl.ds(r, rows // dil, dil), :].astype(dtype)


def _from_pattern(src_ref, slabs_ref, dil):
    n_slabs, rows, _ = slabs_ref.shape
    for s in range(n_slabs):
        for r in range(dil):
            slabs_ref[s, pl.ds(r, rows // dil, dil), :] = src_ref[r, :, s * 128:(s + 1) * 128].astype(F32)


def _store_slabs(slabs_ref, value):
    for s in range(slabs_ref.shape[0]):
        slabs_ref[s] = value[:, s * 128:(s + 1) * 128]


W_IN_CHUNKS = 4


def _in_proj_plan(x, y):
    shards = [2 * x + y, 2 * (1 - x) + y, 2 * x + (1 - y), 2 * (1 - x) + (1 - y)]
    last_row = jnp.int32(-2)

    def table(active, col_of):
        cols, rows = [], []
        first_col = functools.reduce(lambda acc, j: jnp.where(active[j], col_of(shards[j]), acc), reversed(range(4)),
                                     jnp.int32(0))
        held_col, seen = first_col, jnp.bool_(False)
        for j in range(4):
            cols.append(jnp.where(active[j], col_of(shards[j]), held_col))
            rows.append(jnp.where(active[j], -1, jnp.where(seen, last_row, 0)))
            held_col = jnp.where(active[j], col_of(shards[j]), held_col)
            seen = jnp.logical_or(seen, active[j])
        return cols, rows

    q_cols, q_rows = table([s < 2 for s in shards], lambda s: s)
    h_cols, h_rows = table([s >= 2 for s in shards], lambda s: s - 2)
    return jnp.stack([jnp.asarray(v, jnp.int32) for v in shards + q_cols + q_rows + h_cols + h_rows])


def _in_proj_gathering(x, w_bufs, tabs, plan):
    seq = x.shape[0]
    tm, tn = 512, SHARD_IN
    n_tiles = seq // tm
    heads = tn // HEAD_DIM
    k_heads_in_second = 2 * D_ATTN // HEAD_DIM - heads
    d4, d16 = DILATIONS[1], DILATIONS[2]
    DIAGONAL = 2
    chunk = D_MODEL // 2 // W_IN_CHUNKS
    early = [(0, D_MODEL // 2, q * chunk, chunk) for q in range(W_IN_CHUNKS)]
    late = [(a, w_bufs[a].shape[1] // 2, 0, w_bufs[a].shape[1] // 2) for a in (1, 2)]
    pieces = early + late
    early_ids, late_ids = range(len(early)), range(len(early), len(pieces))

    def body(plan_ref, x_ref, w_in_in, w_out_in, w_pool_in, c_ref, up_ref, down_ref,
             o1_ref, o4_ref, o16_ref, hug_ref, w_ref, w_out_ref, w_pool_ref,
             wbuf_ref, res_ref, w_sem, ici_send, ici_recv, d2d_send, d2d_recv):
        j, i = pl.program_id(0), pl.program_id(1)
        mx, my, mc, chips = _mesh_place()
        sibling = (mx, my, 1 - mc)
        gathered = (w_ref, w_out_ref, w_pool_ref)
        chip_of = lambda k: 2 * chips[k][0] + chips[k][1]

        def piece(n, chip, core):
            a, per_core, offset, size = pieces[n]
            return gathered[a].at[chip, pl.ds(core * per_core + offset, size)]

        def to_neighbour(k, n):
            mine = piece(n, 2 * mx + my, mc)
            return _remote(mine, mine, ici_send.at[n, k], ici_recv.at[n, k], (*chips[k], mc))

        def relay(n):
            theirs = piece(n, 2 * (mx ^ (1 - mc)) + (my ^ mc), mc)
            return _remote(theirs, theirs, ici_send.at[n, DIAGONAL], ici_recv.at[n, DIAGONAL], (mx ^ mc, my ^ (1 - mc), mc))

        def arrival(k, n):
            theirs = piece(n, chip_of(k), mc)
            return _remote(theirs, theirs, ici_send.at[n, k], ici_recv.at[n, k], (*chips[k], mc))

        def to_sibling(k, n, core):
            theirs = piece(n, chip_of(k), core)
            return _remote(theirs, theirs, d2d_send.at[n, k], d2d_recv.at[n, k], sibling)

        def take(k, ids):
            for n in ids:
                arrival(k, n).wait_recv()
                to_sibling(k, n, mc).start()

        def taken(k, ids):
            for n in ids:
                to_sibling(k, n, 1 - mc).wait_recv()

        first_tile = i == 0

        @pl.when(jnp.logical_and(j == 0, first_tile))
        def _():
            for n in range(len(pieces)):
                for k in range(DIAGONAL):
                    to_neighbour(k, n).start()

        slot = j % 2
        fetch = lambda step, half: pltpu.make_async_copy(w_ref.at[plan_ref[step]], wbuf_ref.at[half], w_sem.at[half])

        @pl.when(jnp.logical_and(j == 0, i == n_tiles - 1))
        def _():
            take(0, early_ids)
            taken(0, early_ids)
            fetch(1, 1).start()

        ahead = i == n_tiles - 3

        @pl.when(jnp.logical_and(j == 1, ahead))
        def _():
            take(1, early_ids)
            for n in early_ids:
                relay(n).start()
            taken(1, early_ids)
            for k in range(DIAGONAL):
                take(k, late_ids)
            for n in late_ids:
                relay(n).start()
            for k in range(DIAGONAL):
                taken(k, late_ids)
            fetch(2, 0).start()

        @pl.when(jnp.logical_and(j == 2, ahead))
        def _():
            take(DIAGONAL, range(len(pieces)))
            taken(DIAGONAL, range(len(pieces)))
            fetch(3, 1).start()

        shard = plan_ref[j]

        @pl.when(jnp.logical_and(j == 0, first_tile))
        def _():
            cp = fetch(j, slot)
            cp.start()
            cp.wait()

        @pl.when(jnp.logical_and(j >= 1, first_tile))
        def _():
            fetch(j, slot).wait()

        xb = x_ref[...].astype(BF16)
        group = 4 * HEAD_DIM
        accs = [_dot_nn(xb, wbuf_ref[slot, :, g * group:(g + 1) * group]) for g in range(tn // group)]

        def emit_qkv(rotated_heads):
            for h in range(heads):
                lanes = (h * HEAD_DIM) % group
                th = accs[h * HEAD_DIM // group][:, lanes:lanes + HEAD_DIM]
                if h < rotated_heads:
                    th = _rotate_heads(th, c_ref[...], up_ref[...], down_ref[...])
                res_ref[h] = th
                o1_ref[:, h * HEAD_DIM:(h + 1) * HEAD_DIM] = th.astype(BF16)
            _to_pattern(res_ref, o4_ref, d4, BF16)
            _to_pattern(res_ref, o16_ref, d16, BF16)

        @pl.when(shard == 0)
        def _():
            emit_qkv(heads)

        @pl.when(shard == 1)
        def _():
            emit_qkv(k_heads_in_second)

        @pl.when(shard >= 2)
        def _():
            for g, acc in enumerate(accs):
                hug_ref[:, g * group:(g + 1) * group] = acc.astype(BF16)

        @pl.when(jnp.logical_and(j == 3, i == n_tiles - 1))
        def _():
            for n in range(len(pieces)):
                for k in range(DIAGONAL):
                    to_neighbour(k, n).wait_send()
                relay(n).wait_send()
                for k in range(DIAGONAL + 1):
                    to_sibling(k, n, mc).wait_send()

    def held(base, last):
        return lambda j, i, plan_ref: jnp.where(plan_ref[base + j] == -1, i,
                                                jnp.where(plan_ref[base + j] == -2, last, 0))

    q_row, h_row = held(8, n_tiles - 1), held(16, n_tiles - 1)
    tab_spec = pl.BlockSpec((tm, HEAD_DIM), lambda j, i, plan_ref: (i, 0))
    sems = [pltpu.SemaphoreType.DMA((len(pieces), 3))] * 4
    o1, o4, o16, hug, w_in_g, w_out_g, w_pool_g = _pallas(
        body, name="in_proj_gathering",
        grid_spec=pltpu.PrefetchScalarGridSpec(
            num_scalar_prefetch=1, grid=(N_SHARDS, n_tiles),
            in_specs=[pl.BlockSpec((tm, D_MODEL), lambda j, i, plan_ref: (i, 0)), ANY, ANY, ANY,
                      tab_spec, tab_spec, tab_spec],
            out_specs=[pl.BlockSpec((tm, tn), lambda j, i, p: (q_row(j, i, p), p[4 + j])),
                       pl.BlockSpec((d4, tm // d4, tn), lambda j, i, p: (0, q_row(j, i, p), p[4 + j])),
                       pl.BlockSpec((d16, tm // d16, tn), lambda j, i, p: (0, q_row(j, i, p), p[4 + j])),
                       pl.BlockSpec((tm, tn), lambda j, i, p: (h_row(j, i, p), p[12 + j])),
                       ANY, ANY, ANY],
            scratch_shapes=[pltpu.VMEM((2, D_MODEL, tn), BF16), pltpu.VMEM((heads, tm, HEAD_DIM), F32),
                            pltpu.SemaphoreType.DMA((2,))] + sems),
        out_shape=[jax.ShapeDtypeStruct((seq, D_QKV), BF16),
                   jax.ShapeDtypeStruct((d4, seq // d4, D_QKV), BF16),
                   jax.ShapeDtypeStruct((d16, seq // d16, D_QKV), BF16),
                   jax.ShapeDtypeStruct((seq, D_UG), BF16)]
        + [jax.ShapeDtypeStruct(b.shape, b.dtype) for b in w_bufs],
        input_output_aliases={2: 4, 3: 5, 4: 6},
        compiler_params=_params(("arbitrary", "arbitrary"), 58),
    )(plan, x, *w_bufs, *tabs)
    return [o1[None], o4, o16], hug, w_in_g, w_out_g, w_pool_g


def _band_masks():
    row = lax.broadcasted_iota(jnp.int32, (KEY_BLOCK, KEY_BLOCK), 0)
    col = lax.broadcasted_iota(jnp.int32, (KEY_BLOCK, KEY_BLOCK), 1)
    return col <= row, col >= row


def _attn_fwd(qkv, name):
    dil, n, _ = qkv.shape
    scale = HEAD_DIM ** -0.5
    lo, hi = slice(0, KEY_BLOCK), slice(KEY_BLOCK, CHUNK)

    def body(q_ref, k_ref, v_ref, kb_ref, vb_ref, o_ref, st_ref):
        i = pl.program_id(1)
        cur_mask, prev_mask = _band_masks()
        before_mask = jnp.logical_and(prev_mask, i > 0)
        lane = lax.broadcasted_iota(jnp.int32, (KEY_BLOCK, STAT_LANES), 1)
        tasks = [(rows, h) for rows in (lo, hi) for h in range(N_HEADS)]
        head = lambda h: slice(h * HEAD_DIM, (h + 1) * HEAD_DIM)

        def prev_of(rows, h):
            if rows is lo:
                return kb_ref[:, head(h)], vb_ref[:, head(h)], before_mask
            return k_ref[lo, head(h)], v_ref[lo, head(h)], prev_mask

        scores = []
        for rows, h in tasks:
            q = q_ref[rows, head(h)]
            scores.append((_dot_nt(q, prev_of(rows, h)[0]), _dot_nt(q, k_ref[rows, head(h)])))
        probs = []
        for (rows, h), (qk_prev, qk_cur) in zip(tasks, scores):
            s_prev = jnp.where(prev_of(rows, h)[2], qk_prev * scale, NEG)
            s_cur = jnp.where(cur_mask, qk_cur * scale, NEG)
            m = jnp.max(jnp.maximum(s_prev, s_cur), axis=-1, keepdims=True)
            p_prev = jnp.exp(s_prev - m)
            p_cur = jnp.exp(s_cur - m)
            den = jnp.sum(p_prev + p_cur, axis=-1, keepdims=True)
            probs.append((p_prev.astype(BF16), p_cur.astype(BF16), den, m + jnp.log(den)))
        stats = [jnp.zeros((KEY_BLOCK, STAT_LANES), F32), jnp.zeros((KEY_BLOCK, STAT_LANES), F32)]
        for (rows, h), (p_prev, p_cur, den, lse) in zip(tasks, probs):
            o = _dot_nn(p_cur, v_ref[rows, head(h)]) + _dot_nn(p_prev, prev_of(rows, h)[1])
            o_ref[rows, head(h)] = (o / den).astype(BF16)
            b = 0 if rows is lo else 1
            stats[b] = jnp.where(lane == h, lse, stats[b])
        st_ref[lo, :] = stats[0]
        st_ref[hi, :] = stats[1]

    main = lambda cb: pl.BlockSpec((None, CHUNK, D_ATTN), lambda r, i: (r, i, cb))
    before = lambda cb: pl.BlockSpec((None, KEY_BLOCK, D_ATTN), lambda r, i: (r, jnp.maximum(2 * i - 1, 0), cb))
    return _pallas(
        body, name=name, grid=(dil, n // CHUNK),
        in_specs=[main(0), main(1), main(2), before(1), before(2)],
        out_specs=[main(0), pl.BlockSpec((None, CHUNK, STAT_LANES), lambda r, i: (r, i, 0))],
        out_shape=[jax.ShapeDtypeStruct((dil, n, D_ATTN), BF16), jax.ShapeDtypeStruct((dil, n, STAT_LANES), F32)],
        compiler_params=_params(("parallel", "parallel"), 40),
    )(qkv, qkv, qkv, qkv, qkv)


def _attn_bwd(qkv, do, stats, name, comm=None):
    dil, n, _ = qkv.shape
    n_blocks = n // KEY_BLOCK
    last = n // CHUNK - 1
    scale = HEAD_DIM ** -0.5
    lo, hi = slice(0, KEY_BLOCK), slice(KEY_BLOCK, CHUNK)

    def body(q_ref, k_ref, v_ref, kb_ref, vb_ref, qa_ref, do_ref, doa_ref, st_ref, sta_ref, dq_ref, dk_ref, dv_ref):
        i = pl.program_id(1)
        cur_mask, prev_mask = _band_masks()
        before_mask = jnp.logical_and(prev_mask, i > 0)
        after_mask = jnp.logical_and(prev_mask, i < last)

        rows_cat = lambda a, b: jnp.concatenate([a, b], axis=0)
        masks = (jnp.concatenate([before_mask, cur_mask], axis=1), jnp.concatenate([prev_mask, cur_mask], axis=1),
                 after_mask)

        def operands(h):
            cols = slice(h * HEAD_DIM, (h + 1) * HEAD_DIM)
            lse_c, del_c = slice(h, h + 1), slice(N_HEADS + h, N_HEADS + h + 1)
            q = (q_ref[lo, cols], q_ref[hi, cols], qa_ref[:, cols])
            do = (do_ref[lo, cols], do_ref[hi, cols], doa_ref[:, cols])
            keys = (rows_cat(kb_ref[:, cols], k_ref[lo, cols]), k_ref[:, cols], k_ref[hi, cols])
            vals = (rows_cat(vb_ref[:, cols], v_ref[lo, cols]), v_ref[:, cols], v_ref[hi, cols])
            st = ((st_ref[lo, lse_c], st_ref[lo, del_c]), (st_ref[hi, lse_c], st_ref[hi, del_c]),
                  (sta_ref[:, lse_c], sta_ref[:, del_c]))
            return cols, q, do, keys, vals, st

        group = N_HEADS // 2
        for first_head in range(0, N_HEADS, group):
            heads = range(first_head, first_head + group)
            raw = {}
            for h in heads:
                _, q, do, keys, vals, _ = operands(h)
                raw[h] = [(_dot_nt(q[j], keys[j]), _dot_nt(do[j], vals[j])) for j in range(3)]
            grads = {}
            for h in heads:
                st = operands(h)[5]
                grads[h] = []
                for j in range(3):
                    qk, dp = raw[h][j]
                    lse, delta = st[j]
                    p = jnp.exp(jnp.where(masks[j], qk * scale, NEG) - lse)
                    grads[h].append((p.astype(BF16), (p * (dp - delta) * scale).astype(BF16)))
            for h in heads:
                cols, q, do, keys, _, _ = operands(h)
                (p0, ds0), (p1, ds1), (pa, dsa) = grads[h]
                own, nxt = slice(KEY_BLOCK, CHUNK), slice(0, KEY_BLOCK)

                def put(ref, rows, val, cols=cols):
                    ref[rows, cols] = val.astype(ref.dtype)

                put(dq_ref, lo, _dot_nn(ds0, keys[0]))
                put(dq_ref, hi, _dot_nn(ds1, keys[1]))
                put(dk_ref, lo, _dot_tn(rows_cat(ds0[:, own], ds1[:, nxt]), q_ref[:, cols]))
                put(dk_ref, hi, _dot_tn(rows_cat(ds1[:, own], dsa), rows_cat(q[1], q[2])))
                put(dv_ref, lo, _dot_tn(rows_cat(p0[:, own], p1[:, nxt]), do_ref[:, cols]))
                put(dv_ref, hi, _dot_tn(rows_cat(p1[:, own], pa), rows_cat(do[1], do[2])))

    def spec(rows, width, row_of, cb):
        return pl.BlockSpec((None, rows, width), lambda r, i: (r, row_of(i), cb))

    same = lambda i: i
    before = lambda i: jnp.maximum(2 * i - 1, 0)
    after = lambda i: jnp.minimum(2 * i + 2, n_blocks - 1)
    out = spec(CHUNK, D_ATTN, same, 0)
    return _call(
        body, name=name, grid=(dil, n // CHUNK),
        in_specs=[spec(CHUNK, D_ATTN, same, 0), spec(CHUNK, D_ATTN, same, 1), spec(CHUNK, D_ATTN, same, 2),
                  spec(KEY_BLOCK, D_ATTN, before, 1), spec(KEY_BLOCK, D_ATTN, before, 2),
                  spec(KEY_BLOCK, D_ATTN, after, 0),
                  spec(CHUNK, D_ATTN, same, 0), spec(KEY_BLOCK, D_ATTN, after, 0),
                  spec(CHUNK, STAT_LANES, same, 0), spec(KEY_BLOCK, STAT_LANES, after, 0)],
        out_specs=[out, out, out],
        out_shape=[jax.ShapeDtypeStruct((dil, n, D_ATTN), BF16)] * 3,
        scratch_shapes=[], semantics=("parallel", "parallel"), vmem_mib=40,
        args=(qkv, qkv, qkv, qkv, qkv, qkv, do, do, stats, stats), comm=comm)


def _window_sums(ext, window, backward):
    rows = ext.shape[0]
    acc, span = ext, 1
    while span < window:
        acc = acc + pltpu.roll(acc, (rows - span) if backward else span, axis=0)
        span *= 2
    return acc


def _pool_group_weight(wp_ref, g):
    return jnp.concatenate([wp_ref[k, g] for k in range(N_SHARDS)], axis=0)


def _mix_gate(o_list, st_list, hug, w_pool_g, pool_scale):
    seq = hug.shape[0]
    tm = 256
    halo_blocks = tm // POOL_HALO
    d4, d16 = DILATIONS[1], DILATIONS[2]

    def body(o1_ref, o4_ref, o16_ref, l1_ref, l4_ref, l16_ref, u_ref, halo_ref, ga_ref, gp_ref, wp_ref, sc_ref,
             y_ref, mix_ref, lse_ref, pooled_ref, n4_ref, n16_ref, nl4_ref, nl16_ref):
        i = pl.program_id(0)
        _from_pattern(o4_ref, n4_ref, d4)
        _from_pattern(o16_ref, n16_ref, d16)
        _from_pattern(l4_ref, nl4_ref, d4)
        _from_pattern(l16_ref, nl16_ref, d16)
        la, lb, lc = l1_ref[...], nl4_ref[0], nl16_ref[0]
        mx = jnp.maximum(jnp.maximum(la, lb), lc)
        ea, eb, ec = jnp.exp(la - mx), jnp.exp(lb - mx), jnp.exp(lc - mx)
        tot = ea + eb + ec
        lse_ref[...] = mx + jnp.log(tot)
        wa, wb, wc = ea / tot, eb / tot, ec / tot
        ga = ga_ref[...].astype(F32)
        silu_a = ga * jax.nn.sigmoid(ga)
        for h in range(N_HEADS):
            cols = slice(h * HEAD_DIM, (h + 1) * HEAD_DIM)
            hc = slice(h, h + 1)
            attn = wa[:, hc] * o1_ref[:, cols].astype(F32) + wb[:, hc] * n4_ref[h] + wc[:, hc] * n16_ref[h]
            mix_ref[:, cols] = attn.astype(BF16)
            y_ref[:, cols] = (attn * silu_a[:, cols]).astype(BF16)

        u = u_ref[...].astype(F32)
        halo = jnp.where(i > 0, halo_ref[...].astype(F32), 0.0)
        ext = jnp.concatenate([halo, u], axis=0)
        pos = i * tm + lax.broadcasted_iota(jnp.int32, (tm, 1), 0)
        gp = gp_ref[...].astype(F32)
        gated_scale = sc_ref[...] * (gp * jax.nn.sigmoid(gp))
        for g, window in enumerate(POOL_WINDOWS):
            cols = slice(g * POOL_GROUP_DIM, (g + 1) * POOL_GROUP_DIM)
            sums = _window_sums(ext[:, cols], window, backward=False)[POOL_HALO:, :]
            count = jnp.minimum(pos + 1, window).astype(F32)
            pooled = (sums / count - u[:, cols]).astype(BF16)
            pooled_ref[:, cols] = pooled
            pre = _dot_nn(pooled, _pool_group_weight(wp_ref, g))
            out_cols = slice(D_ATTN + g * POOL_GROUP_DIM, D_ATTN + (g + 1) * POOL_GROUP_DIM)
            mix_ref[:, out_cols] = pre.astype(BF16)
            y_ref[:, out_cols] = (pre * gated_scale[:, cols]).astype(BF16)

    row = lambda width, cb=0: pl.BlockSpec((tm, width), lambda i: (i, cb))
    pat = lambda d, width: pl.BlockSpec((d, tm // d, width), lambda i: (0, i, 0))
    return _pallas(
        body, name="mix_gate", grid=(seq // tm,),
        in_specs=[row(D_ATTN), pat(d4, D_ATTN), pat(d16, D_ATTN),
                  row(STAT_LANES), pat(d4, STAT_LANES), pat(d16, STAT_LANES),
                  row(D_POOL),
                  pl.BlockSpec((POOL_HALO, D_POOL), lambda i: (jnp.maximum(i * halo_blocks - 1, 0), 0)),
                  row(D_ATTN, 1), row(D_POOL, 2),
                  pl.BlockSpec(w_pool_g.shape, lambda i: (0, 0, 0, 0)),
                  pl.BlockSpec((1, D_POOL), lambda i: (0, 0))],
        out_specs=[row(D_MODEL), row(D_MODEL), row(STAT_LANES), row(D_POOL)],
        out_shape=[jax.ShapeDtypeStruct((seq, D_MODEL), BF16), jax.ShapeDtypeStruct((seq, D_MODEL), BF16),
                   jax.ShapeDtypeStruct((seq, STAT_LANES), F32), jax.ShapeDtypeStruct((seq, D_POOL), BF16)],
        scratch_shapes=[pltpu.VMEM((N_HEADS, tm, HEAD_DIM), F32), pltpu.VMEM((N_HEADS, tm, HEAD_DIM), F32),
                        pltpu.VMEM((1, tm, STAT_LANES), F32), pltpu.VMEM((1, tm, STAT_LANES), F32)],
        compiler_params=_params(("parallel",), 48),
    )(o_list[0][0], o_list[1], o_list[2], st_list[0][0], st_list[1], st_list[2],
      hug, hug, hug, hug, w_pool_g, pool_scale)


def _out_proj_loss(y, w_out_g, x, target, gain, bias):
    seq = x.shape[0]
    tm = 512

    def body(y_ref, w_ref, x_ref, t_ref, g_ref, b_ref, dz_ref, dzb_ref, gg_ref, gb_ref, loss_ref):
        @pl.when(pl.program_id(0) == 0)
        def _():
            gg_ref[...] = jnp.zeros_like(gg_ref)
            gb_ref[...] = jnp.zeros_like(gb_ref)
            loss_ref[...] = jnp.zeros_like(loss_ref)

        halves = [slice(0, tm // 2), slice(tm // 2, tm)]
        projected = [_dot_nn(y_ref[rows, :], w_ref[...]) for rows in halves]
        for rows, out in zip(halves, projected):
            z = DEEPNORM_ALPHA * x_ref[rows, :] + out
            mu = jnp.mean(z, axis=-1, keepdims=True)
            zc = z - mu
            rstd = lax.rsqrt(jnp.mean(zc * zc, axis=-1, keepdims=True) + LN_EPS)
            xhat = zc * rstd
            gain_v = g_ref[...]
            diff = xhat * gain_v + b_ref[...] - t_ref[rows, :]
            sq = _fold_rows(diff * diff)
            part = sq[:, :128]
            for k in range(1, D_MODEL // 128):
                part = part + sq[:, k * 128:(k + 1) * 128]
            loss_ref[...] += part
            dln = diff * (1.0 / D_MODEL)
            gg_ref[...] += _fold_rows(dln * xhat)
            gb_ref[...] += _fold_rows(dln)
            dxhat = dln * gain_v
            dz = rstd * (dxhat - jnp.mean(dxhat, axis=-1, keepdims=True)
                         - xhat * jnp.mean(dxhat * xhat, axis=-1, keepdims=True))
            dz_ref[rows, :] = dz
            dzb_ref[rows, :] = dz.astype(BF16)

    row = lambda: pl.BlockSpec((tm, D_MODEL), lambda i: (i, 0))
    vec = lambda: pl.BlockSpec((1, D_MODEL), lambda i: (0, 0))
    acc = lambda width: pl.BlockSpec((8, width), lambda i: (0, 0))
    return _pallas(
        body, name="out_proj_loss", grid=(seq // tm,),
        in_specs=[row(), pl.BlockSpec((D_MODEL, D_MODEL), lambda i: (0, 0), pipeline_mode=pl.Buffered(1)),
                  row(), row(), vec(), vec()],
        out_specs=[row(), row(), acc(D_MODEL), acc(D_MODEL), acc(128)],
        out_shape=[jax.ShapeDtypeStruct((seq, D_MODEL), F32), jax.ShapeDtypeStruct((seq, D_MODEL), BF16),
                   jax.ShapeDtypeStruct((8, D_MODEL), F32), jax.ShapeDtypeStruct((8, D_MODEL), F32),
                   jax.ShapeDtypeStruct((8, 128), F32)],
        compiler_params=_params(("arbitrary",), 56),
    )(y, w_out_g.reshape(D_MODEL, D_MODEL), x, target, gain, bias)


def _dy_gate_bwd(dzb, w_out_g, hug, mixpre, pool_scale, lse_all):
    seq = dzb.shape[0]
    tm = 256
    d4, d16 = DILATIONS[1], DILATIONS[2]

    def body(dz_ref, w_ref, ga_ref, gp_ref, mix_ref, sc_ref, lse_ref,
             dh_ref, dpo_ref, do1_ref, do4_ref, do16_ref, st1_ref, st4_ref, st16_ref, da_ref, st_ref):
        dy = _dot_nt(dz_ref[...], w_ref[...])
        ga = ga_ref[...].astype(F32)
        sig = jax.nn.sigmoid(ga)
        attn = mix_ref[:, :D_ATTN].astype(F32)
        dya = dy[:, :D_ATTN]
        dattn = dya * (ga * sig)
        dh_ref[:, :D_ATTN] = (dya * attn * (sig * (1.0 + ga * (1.0 - sig)))).astype(BF16)
        _store_slabs(da_ref, dattn)
        lane = lax.broadcasted_iota(jnp.int32, (tm, STAT_LANES), 1)
        stats = lse_ref[...]
        prod = dattn * attn
        for h in range(N_HEADS):
            delta = jnp.sum(prod[:, h * HEAD_DIM:(h + 1) * HEAD_DIM], axis=-1, keepdims=True)
            stats = jnp.where(lane == N_HEADS + h, delta, stats)
        st_ref[0] = stats
        do1_ref[...] = dattn.astype(BF16)
        st1_ref[...] = stats
        _to_pattern(da_ref, do4_ref, d4, BF16)
        _to_pattern(da_ref, do16_ref, d16, BF16)
        _to_pattern(st_ref, st4_ref, d4, F32)
        _to_pattern(st_ref, st16_ref, d16, F32)

        gp = gp_ref[...].astype(F32)
        sig = jax.nn.sigmoid(gp)
        dyp = dy[:, D_ATTN:]
        dpo_ref[...] = (dyp * (gp * sig)).astype(BF16)
        dh_ref[:, D_ATTN:] = (dyp * (mix_ref[:, D_ATTN:].astype(F32) * sc_ref[...])
                              * (sig * (1.0 + gp * (1.0 - sig)))).astype(BF16)

    row = lambda width, cb=0: pl.BlockSpec((tm, width), lambda i: (i, cb))
    pat = lambda d, width: pl.BlockSpec((d, tm // d, width), lambda i: (0, i, 0))
    pat_shape = lambda d, width, dtype: jax.ShapeDtypeStruct((d, seq // d, width), dtype)
    outs = _pallas(
        body, name="dy_gate_bwd", grid=(seq // tm,),
        in_specs=[row(D_MODEL), pl.BlockSpec((D_MODEL, D_MODEL), lambda i: (0, 0)),
                  row(D_ATTN, 1), row(D_POOL, 2), row(D_MODEL), pl.BlockSpec((1, D_POOL), lambda i: (0, 0)),
                  row(STAT_LANES)],
        out_specs=[row(D_MODEL, D_IN // D_MODEL - 1), row(D_POOL),
                   row(D_ATTN), pat(d4, D_ATTN), pat(d16, D_ATTN),
                   row(STAT_LANES), pat(d4, STAT_LANES), pat(d16, STAT_LANES)],
        out_shape=[jax.ShapeDtypeStruct((seq, D_IN), BF16), jax.ShapeDtypeStruct((seq, D_POOL), BF16),
                   jax.ShapeDtypeStruct((seq, D_ATTN), BF16), pat_shape(d4, D_ATTN, BF16), pat_shape(d16, D_ATTN, BF16),
                   jax.ShapeDtypeStruct((seq, STAT_LANES), F32), pat_shape(d4, STAT_LANES, F32),
                   pat_shape(d16, STAT_LANES, F32)],
        scratch_shapes=[pltpu.VMEM((N_HEADS, tm, HEAD_DIM), F32), pltpu.VMEM((1, tm, STAT_LANES), F32)],
        compiler_params=_params(("parallel",), 48),
    )(dzb, w_out_g.reshape(D_MODEL, D_MODEL), hug, hug, mixpre, pool_scale, lse_all)
    dh, dpo, do1, do4, do16, st1, st4, st16 = outs
    return dh, dpo, [do1[None], do4, do16], [st1[None], st4, st16]


def _pool_bwd(dh, dpo, mixpre, pooled, w_pool_g, pool_scale):
    seq = dpo.shape[0]
    tm = 256
    halo_blocks = tm // POOL_HALO
    last = seq // tm - 1
    n_groups = len(POOL_WINDOWS)
    half_c = POOL_GROUP_DIM // N_SHARDS // 2
    pieces = (N_SHARDS, 2, n_groups * half_c, POOL_GROUP_DIM)

    def body(dh_in_ref, dpo_ref, halo_ref, pre_ref, pooled_ref, wp_ref, sc_ref, du_ref, gw_ref, gs_ref):
        i = pl.program_id(0)

        @pl.when(i == 0)
        def _():
            gw_ref[...] = jnp.zeros_like(gw_ref)
            gs_ref[...] = jnp.zeros_like(gs_ref)

        dpo = dpo_ref[...].astype(F32)
        scale = sc_ref[...]
        gs_ref[...] += _fold_rows(dpo * pre_ref[...].astype(F32))
        halo = jnp.where(i < last, halo_ref[...].astype(F32), 0.0)
        dpw = (jnp.concatenate([dpo, halo], axis=0) * scale).astype(BF16)
        pos = i * tm + lax.broadcasted_iota(jnp.int32, (tm + POOL_HALO, 1), 0)
        for g, window in enumerate(POOL_WINDOWS):
            cols = slice(g * POOL_GROUP_DIM, (g + 1) * POOL_GROUP_DIM)
            dpw_g = dpw[:, cols]
            gw = _dot_tn(pooled_ref[:, cols], dpw_g[:tm, :])
            for piece in range(2 * N_SHARDS):
                gw_ref[piece // 2, piece % 2, g * half_c:(g + 1) * half_c, :] += gw[piece * half_c:(piece + 1) * half_c]
            dpooled = _dot_nt(dpw_g, _pool_group_weight(wp_ref, g))
            count = jnp.minimum(pos + 1, window).astype(F32)
            sums = _window_sums(dpooled / count, window, backward=True)
            du_ref[:, cols] = (sums[:tm, :] - dpooled[:tm, :]).astype(BF16)

    row = lambda width, cb=0: pl.BlockSpec((tm, width), lambda i: (i, cb))
    return _pallas(
        body, name="pool_bwd", grid=(seq // tm,),
        in_specs=[ANY, row(D_POOL),
                  pl.BlockSpec((POOL_HALO, D_POOL),
                               lambda i: (jnp.minimum((i + 1) * halo_blocks, seq // POOL_HALO - 1), 0)),
                  row(D_POOL, 1), row(D_POOL),
                  pl.BlockSpec(w_pool_g.shape, lambda i: (0, 0, 0, 0)),
                  pl.BlockSpec((1, D_POOL), lambda i: (0, 0))],
        out_specs=[row(D_POOL, D_QKV // D_POOL),
                   pl.BlockSpec(pieces, lambda i: (0, 0, 0, 0)),
                   pl.BlockSpec((8, D_POOL), lambda i: (0, 0))],
        out_shape=[jax.ShapeDtypeStruct(dh.shape, dh.dtype),
                   jax.ShapeDtypeStruct(pieces, F32),
                   jax.ShapeDtypeStruct((8, D_POOL), F32)],
        input_output_aliases={0: 0},
        compiler_params=_params(("arbitrary",), 40),
    )(dh, dpo, dpo, mixpre, pooled, w_pool_g, pool_scale)


def _sum_patterns(dh, parts, tabs, unrotate, col_block, name, comm=None):
    seq = dh.shape[0]
    tm, tn = 256, D_ATTN
    per = D_ATTN // tn
    d4, d16 = DILATIONS[1], DILATIONS[2]

    def body(dh_in_ref, a1_ref, a4_ref, a16_ref, ct_ref, up_ref, down_ref, o_ref, n4_ref, n16_ref):
        _from_pattern(a4_ref, n4_ref, d4)
        _from_pattern(a16_ref, n16_ref, d16)
        for s in range(tn // HEAD_DIM):
            cols = slice(s * HEAD_DIM, (s + 1) * HEAD_DIM)
            tot = a1_ref[:, cols].astype(F32) + n4_ref[s] + n16_ref[s]
            if unrotate:
                tot = _rotate_heads(tot, ct_ref[...], -up_ref[...], -down_ref[...])
            o_ref[:, cols] = tot.astype(BF16)

    tab = pl.BlockSpec((tm, HEAD_DIM), lambda i, j: (i, 0))
    pat = lambda d: pl.BlockSpec((d, tm // d, tn), lambda i, j: (0, i, j))
    (dh,), exchanged = _call(
        body, name=name, grid=(seq // tm, per),
        in_specs=[ANY, pl.BlockSpec((tm, tn), lambda i, j: (i, j)), pat(d4), pat(d16), tab, tab, tab],
        out_specs=[pl.BlockSpec((tm, tn), lambda i, j: (i, col_block * per + j))],
        out_shape=[jax.ShapeDtypeStruct(dh.shape, dh.dtype)],
        scratch_shapes=[pltpu.VMEM((tn // HEAD_DIM, tm, HEAD_DIM), F32), pltpu.VMEM((tn // HEAD_DIM, tm, HEAD_DIM), F32)],
        semantics=("parallel", "parallel"), vmem_mib=32, args=(dh, parts[0][0], parts[1], parts[2], *tabs),
        aliases={0: 0}, comm=comm)
    return dh, exchanged


def _grad_w_in(x, dh, half, name, comm=None):
    seq = x.shape[0]
    ts, td, te = 2048, D_MODEL // 2, SHARD_IN

    def body(half_ref, x_ref, dh_ref, o_ref):
        k = pl.program_id(1)
        part = _dot_tn(x_ref[...].astype(BF16), dh_ref[...])

        @pl.when(k == 0)
        def _():
            o_ref[...] = part

        @pl.when(k > 0)
        def _():
            o_ref[...] += part

    (g,), exchanged = _call(
        body, name=name, grid=(N_SHARDS, seq // ts),
        in_specs=[pl.BlockSpec((ts, td), lambda e, k, half_ref: (k, half_ref[0])),
                  pl.BlockSpec((ts, te), lambda e, k, half_ref: (k, e))],
        out_specs=[pl.BlockSpec((None, td, te), lambda e, k, half_ref: (e, 0, 0))],
        out_shape=[jax.ShapeDtypeStruct((N_SHARDS, td, te), F32)],
        scratch_shapes=[], semantics=("parallel", "arbitrary"), vmem_mib=56, args=(x, dh), comm=comm,
        prefetch=(half,))
    return g, exchanged


def _grad_w_out(y, dzb):
    seq = y.shape[0]
    ts, te = 2048, 1024

    def body(y_ref, dz_ref, o_ref):
        k = pl.program_id(1)
        part = _dot_tn(y_ref[...], dz_ref[...])

        @pl.when(k == 0)
        def _():
            o_ref[...] = part

        @pl.when(k > 0)
        def _():
            o_ref[...] += part

    return _pallas(
        body, name="grad_w_out", grid=(D_MODEL // te, seq // ts),
        in_specs=[pl.BlockSpec((ts, te), lambda e, k: (k, e)), pl.BlockSpec((ts, D_MODEL), lambda e, k: (k, 0))],
        out_specs=pl.BlockSpec((te, D_MODEL), lambda e, k: (e, 0)),
        out_shape=jax.ShapeDtypeStruct((D_MODEL, D_MODEL), F32),
        compiler_params=_params(("parallel", "arbitrary"), 56),
    )(y, dzb)


GRAD_X_LATE_SHARDS = 1
GRAD_X_PARTIAL_ROWS = 512


def _grad_x_partial(dh, w_in_g, dz, first, tiles, prev=None, comm=None):
    seq = dh.shape[0]
    tm, tk = GRAD_X_PARTIAL_ROWS, SHARD_IN

    def body(*refs):
        dh_ref, w_ref, dz_ref, o_ref = refs[-4:]
        k = pl.program_id(1)
        part = _dot_nt(dh_ref[...], w_ref[...])

        @pl.when(k == 0)
        def _():
            o_ref[...] = DEEPNORM_ALPHA * dz_ref[...] + part

        @pl.when(k > 0)
        def _():
            o_ref[...] += part

    carried = [] if prev is None else [prev]
    row = pl.BlockSpec((tm, D_MODEL), lambda i, k: (i + first, 0))
    (partial,), exchanged = _call(
        body, name="grad_x_partial_%d" % first, grid=(tiles, N_SHARDS - GRAD_X_LATE_SHARDS),
        in_specs=[ANY] * len(carried) + [
            pl.BlockSpec((tm, tk), lambda i, k: (i + first, k)),
            pl.BlockSpec((None, D_MODEL, tk), lambda i, k: (k, 0, 0)), row],
        out_specs=[row],
        out_shape=[jax.ShapeDtypeStruct((seq, D_MODEL), F32)],
        scratch_shapes=[], semantics=("parallel", "arbitrary"), vmem_mib=48, args=(*carried, dh, w_in_g, dz),
        aliases={0: 0} if carried else None, comm=comm)
    return partial, exchanged


def _grad_x_final(dh, w_in_g, partial):
    seq = dh.shape[0]
    tm, tk = 512, SHARD_IN
    k0 = N_SHARDS - GRAD_X_LATE_SHARDS

    def body(dh_ref, w_ref, p_ref, o_ref):
        k = pl.program_id(1)
        part = _dot_nt(dh_ref[...], w_ref[...])

        @pl.when(k == 0)
        def _():
            o_ref[...] = p_ref[...] + part

        @pl.when(k > 0)
        def _():
            o_ref[...] += part

    row = pl.BlockSpec((tm, D_MODEL), lambda i, k: (i, 0))
    return _pallas(
        body, name="grad_x_final", grid=(seq // tm, GRAD_X_LATE_SHARDS),
        in_specs=[pl.BlockSpec((tm, tk), lambda i, k: (i, k + k0)),
                  pl.BlockSpec((None, D_MODEL, tk), lambda i, k: (k + k0, 0, 0)), row],
        out_specs=row, out_shape=jax.ShapeDtypeStruct((seq, D_MODEL), F32),
        compiler_params=_params(("parallel", "arbitrary"), 48),
    )(dh, w_in_g, partial)


def _pool_weight(w_pool_sh):
    n_groups = len(POOL_WINDOWS)
    shard_c = POOL_GROUP_DIM // N_SHARDS
    return w_pool_sh.reshape(N_SHARDS, n_groups, shard_c, POOL_GROUP_DIM)


def _step(x, target, w_bufs, pool_scale, gain, bias, place):
    seq = x.shape[0]
    tabs = _rope_tables(seq)
    core, chip_core, onward, plan = place
    qkv, hug, w_in_g, w_out_g, w_pool_sh = _in_proj_gathering(x, w_bufs, tabs, plan)
    o_list, st_list = [], []
    for p, dil in enumerate(DILATIONS):
        o, st = _attn_fwd(qkv[p], "attn_fwd_d%d" % dil)
        o_list.append(o)
        st_list.append(st)
    w_pool_g = _pool_weight(w_pool_sh)
    y, mixpre, lse_all, pooled = _mix_gate(o_list, st_list, hug, w_pool_g, pool_scale)
    dz, dzb, gain_part, bias_part, loss_part = _out_proj_loss(y, w_out_g, x, target, gain, bias)
    dh, dpo, do_list, stat_list = _dy_gate_bwd(dzb, w_out_g, hug, mixpre, pool_scale, lse_all)
    g_w_out = _grad_w_out(y, dzb)
    dh, g_w_pool, scale_part = _pool_bwd(dh, dpo, mixpre, pooled, w_pool_g, pool_scale)
    small = jnp.concatenate([scale_part, gain_part, bias_part, loss_part], axis=1)
    early = [g_w_out.reshape(N_SHARDS, 2, D_MODEL // (2 * N_SHARDS), D_MODEL), g_w_pool]

    bwd = lambda p, comm: _attn_bwd(qkv[p], do_list[p], stat_list[p], "attn_bwd_d%d" % DILATIONS[p], comm)
    part_a, halves = bwd(0, _exchange_halves(early))
    sums_b = [_add_own_half(g, h, core, "add_own_half_%d" % a) for a, (g, h) in enumerate(zip(early, halves))]
    part_b, recv = bwd(1, _scatter_to_chips(sums_b))
    bufs = [_add_chips([g, h], r, chip_core, "add_chips_%d" % a)
            for a, (g, h, r) in enumerate(zip(early, halves, recv))]
    part_c, reduced = bwd(2, _share_with_sibling(bufs))
    parts = [part_a, part_b, part_c]
    dh, gathered = _sum_patterns(dh, [t[0] for t in parts], tabs, True, 0, "sum_dq", _gather_small(small))
    dh, _ = _sum_patterns(dh, [t[1] for t in parts], tabs, True, 1, "sum_dk")
    dh, _ = _sum_patterns(dh, [t[2] for t in parts], tabs, False, 2, "sum_dv")

    give, _ = _grad_w_in(x, dh, 1 - core, "grad_w_in_give")
    keep, recv = _grad_w_in(x, dh, core, "grad_w_in_keep", _send_to_sibling([give]))
    total = [keep, recv[0]]
    total_b = _add_pair(keep, recv[0], "add_own_half_w_in")
    n_tiles = seq // GRAD_X_PARTIAL_ROWS
    tiles = 3 * n_tiles // 8
    part, relayed = _grad_x_partial(dh, w_in_g, dz, 0, tiles, None, _relay_diagonal(total_b))
    total_b = _fold_relayed(total, total_b, relayed[0], onward)
    part, recv = _grad_x_partial(dh, w_in_g, dz, tiles, n_tiles - tiles, part, _scatter_to_neighbours(total_b))
    buf = _add_chips(total, recv[0], chip_core, "add_chips_w_in")
    g_x = _grad_x_final(dh, w_in_g, part)
    g_w_in = _run_exchange(_share_with_sibling([buf]), "share_w_in")[0]
    return g_x, g_w_in, reduced[0], reduced[1], small, gathered[0]


def _exchange_halves(grads):
    n = len(grads)

    def copies(src, dst, sems):
        x, y, c, _ = _mesh_place()
        return [_remote(src[a].at[j, 1 - c], dst[a].at[j], sems[0].at[a, j], sems[1].at[a, j], (x, y, 1 - c))
                for a in range(n) for j in range(N_SHARDS)]

    def start(src, dst, sems):
        for cp in copies(src, dst, sems):
            cp.start()

    def finish(src, dst, sems):
        for cp in copies(src, dst, sems):
            cp.wait()

    return _Exchange(grads, [jax.ShapeDtypeStruct((N_SHARDS,) + g.shape[2:], g.dtype) for g in grads], {},
                     [pltpu.SemaphoreType.DMA((n, N_SHARDS))] * 2, start, finish)


def _add_own_half(grad, recv, core, name):
    _, _, r, c = grad.shape
    tr = min(r, 256)

    def body(core_ref, g_ref, r_ref, ob_ref):
        ob_ref[...] = (g_ref[...] + r_ref[...]).astype(BF16)

    return _pallas(
        body, name=name,
        grid_spec=pltpu.PrefetchScalarGridSpec(
            num_scalar_prefetch=1, grid=(N_SHARDS, r // tr),
            in_specs=[pl.BlockSpec((None, None, tr, c), lambda j, i, core_ref: (j, core_ref[0], i, 0)),
                      pl.BlockSpec((None, tr, c), lambda j, i, core_ref: (j, i, 0))],
            out_specs=pl.BlockSpec((None, tr, c), lambda j, i, core_ref: (j, i, 0))),
        out_shape=jax.ShapeDtypeStruct((N_SHARDS, r, c), BF16),
        compiler_params=_params(("parallel", "parallel"), 32),
    )(core, grad, recv)


def _send_to_sibling(arrays):
    n = len(arrays)

    def copies(src, dst, sems):
        x, y, c, _ = _mesh_place()
        return [_remote(src[a], dst[a], sems[0].at[a], sems[1].at[a], (x, y, 1 - c)) for a in range(n)]

    def start(src, dst, sems):
        for cp in copies(src, dst, sems):
            cp.start()

    def finish(src, dst, sems):
        for cp in copies(src, dst, sems):
            cp.wait()

    return _Exchange(arrays, [jax.ShapeDtypeStruct(t.shape, t.dtype) for t in arrays], {},
                     [pltpu.SemaphoreType.DMA((n,))] * 2, start, finish)


def _add_pair(a, b, name):
    _, r, c = a.shape
    tr = min(r, 256)

    def body(a_ref, b_ref, ob_ref):
        ob_ref[...] = (a_ref[...] + b_ref[...]).astype(BF16)

    spec = pl.BlockSpec((None, tr, c), lambda j, i: (j, i, 0))
    return _pallas(
        body, name=name, grid=(N_SHARDS, r // tr), in_specs=[spec, spec], out_specs=spec,
        out_shape=jax.ShapeDtypeStruct(a.shape, BF16),
        compiler_params=_params(("parallel", "parallel"), 32),
    )(a, b)


def _scatter_to_chips(sums):
    n = len(sums)

    def copies(src, dst, sems):
        x, y, c, chips = _mesh_place()
        return [_remote(src[a].at[2 * cx + cy], dst[a].at[k], sems[0].at[a, k], sems[1].at[a, k], (cx, cy, c))
                for a in range(n) for k, (cx, cy) in enumerate(chips)]

    def start(src, dst, sems):
        for cp in copies(src, dst, sems):
            cp.start()

    def finish(src, dst, sems):
        for cp in copies(src, dst, sems):
            cp.wait()

    return _Exchange(sums, [jax.ShapeDtypeStruct((3,) + s.shape[1:], s.dtype) for s in sums], {},
                     [pltpu.SemaphoreType.DMA((n, 3))] * 2, start, finish)


def _add_chips(sums, recv, chip_core, name):
    r, c = sums[0].shape[-2:]
    n_sums, n_recv = len(sums), recv.shape[0]
    tr = min(r, 256)
    mine = {3: pl.BlockSpec((None, tr, c), lambda i, cc_ref: (cc_ref[0], i, 0)),
            4: pl.BlockSpec((None, None, tr, c), lambda i, cc_ref: (cc_ref[0], cc_ref[1], i, 0))}

    def body(cc_ref, *refs):
        r_ref, o_ref = refs[n_sums:]
        tot = refs[0][...]
        for s_ref in refs[1:n_sums]:
            tot = tot + s_ref[...]
        for k in range(n_recv):
            tot = tot + r_ref[k].astype(F32)
        o_ref[...] = tot

    return _pallas(
        body, name=name,
        grid_spec=pltpu.PrefetchScalarGridSpec(
            num_scalar_prefetch=1, grid=(r // tr,),
            in_specs=[mine[s.ndim] for s in sums] + [pl.BlockSpec((n_recv, tr, c), lambda i, cc_ref: (0, i, 0))],
            out_specs=pl.BlockSpec((None, tr, c), lambda i, cc_ref: (cc_ref[1], i, 0))),
        out_shape=jax.ShapeDtypeStruct((2, r, c), F32),
        compiler_params=_params(("parallel",), 32),
    )(chip_core, *sums, recv)


def _relay_diagonal(sums_b):
    def copy(src, dst, sems):
        x, y, c, _ = _mesh_place()
        diagonal = 2 * (1 - x) + (1 - y)
        return _remote(src[0].at[diagonal], dst[0], sems[0].at[0], sems[1].at[0], (x ^ (1 - c), y ^ c, c))

    def start(src, dst, sems):
        copy(src, dst, sems).start()

    def finish(src, dst, sems):
        copy(src, dst, sems).wait()

    return _Exchange([sums_b], [jax.ShapeDtypeStruct(sums_b.shape[1:], sums_b.dtype)], {},
                     [pltpu.SemaphoreType.DMA((1,))] * 2, start, finish)


def _fold_relayed(sums, sums_b, relayed, onward):
    _, r, c = sums[0].shape
    n_sums = len(sums)
    tr = min(r, 256)

    def body(on_ref, b_in_ref, *refs):
        r_ref, o_ref = refs[n_sums:]
        tot = refs[0][...]
        for s_ref in refs[1:n_sums]:
            tot = tot + s_ref[...]
        o_ref[...] = (tot + r_ref[...].astype(F32)).astype(BF16)

    return _pallas(
        body, name="fold_relayed",
        grid_spec=pltpu.PrefetchScalarGridSpec(
            num_scalar_prefetch=1, grid=(r // tr,),
            in_specs=[ANY] + [pl.BlockSpec((None, tr, c), lambda i, on_ref: (on_ref[0], i, 0))] * n_sums
            + [pl.BlockSpec((tr, c), lambda i, on_ref: (i, 0))],
            out_specs=pl.BlockSpec((None, tr, c), lambda i, on_ref: (on_ref[0], i, 0))),
        out_shape=jax.ShapeDtypeStruct(sums_b.shape, sums_b.dtype),
        input_output_aliases={1: 0},
        compiler_params=_params(("parallel",), 32),
    )(onward, sums_b, *sums, relayed)


def _scatter_to_neighbours(sums_b):
    def copies(src, dst, sems):
        x, y, c, chips = _mesh_place()
        return [_remote(src[0].at[2 * cx + cy], dst[0].at[k], sems[0].at[k], sems[1].at[k], (cx, cy, c))
                for k, (cx, cy) in enumerate(chips[:2])]

    def start(src, dst, sems):
        for cp in copies(src, dst, sems):
            cp.start()

    def finish(src, dst, sems):
        for cp in copies(src, dst, sems):
            cp.wait()

    return _Exchange([sums_b], [jax.ShapeDtypeStruct((2,) + sums_b.shape[1:], sums_b.dtype)], {},
                     [pltpu.SemaphoreType.DMA((2,))] * 2, start, finish)


def _share_with_sibling(bufs):
    n = len(bufs)

    def copies(dst, sems, half):
        x, y, c, _ = _mesh_place()
        h = c if half == "mine" else 1 - c
        return [_remote(dst[a].at[h], dst[a].at[h], sems[0].at[a], sems[1].at[a], (x, y, 1 - c)) for a in range(n)]

    def start(ins, dst, sems):
        for cp in copies(dst, sems, "mine"):
            cp.start()

    def finish(ins, dst, sems):
        for cp in copies(dst, sems, "theirs"):
            cp.wait_recv()
        for cp in copies(dst, sems, "mine"):
            cp.wait_send()

    return _Exchange(bufs, [jax.ShapeDtypeStruct(b.shape, b.dtype) for b in bufs], {a: a for a in range(n)},
                     [pltpu.SemaphoreType.DMA((n,))] * 2, start, finish)


def _adam_math(w, g, m, v):
    m = ADAM_B1 * m + (1.0 - ADAM_B1) * g
    v = ADAM_B2 * v + (1.0 - ADAM_B2) * (g * g)
    m_hat = m / (1.0 - ADAM_B1 ** ADAM_STEP)
    v_hat = v / (1.0 - ADAM_B2 ** ADAM_STEP)
    delta = -ADAM_LR * (m_hat / (jnp.sqrt(v_hat) + ADAM_EPS) + ADAM_WD * w)
    return delta, m, v


def _gather_small(small):
    def peers():
        x, y, c, _ = _mesh_place()
        return [(x ^ ((r >> 2) & 1), y ^ ((r >> 1) & 1), c ^ (r & 1)) for r in range(1, 8)], 4 * x + 2 * y + c

    def start(src, dst, sems):
        to, me = peers()
        for r, peer in enumerate(to):
            _remote(src[0], dst[0].at[me], sems[0].at[r], sems[1].at[r], peer).start()

    def finish(src, dst, sems):
        to, me = peers()
        for r, (px, py, pc) in enumerate(to):
            theirs = dst[0].at[4 * px + 2 * py + pc]
            _remote(theirs, theirs, sems[0].at[r], sems[1].at[r], (px, py, pc)).wait_recv()
        for r, peer in enumerate(to):
            _remote(src[0], dst[0].at[me], sems[0].at[r], sems[1].at[r], peer).wait_send()

    return _Exchange([small], [jax.ShapeDtypeStruct((8,) + small.shape, small.dtype)], {},
                     [pltpu.SemaphoreType.DMA((7,))] * 2, start, finish)


def _small_adamw(gathered, small, me, w_vecs, m_vecs, v_vecs):
    n = len(w_vecs)
    widths = [w.shape[1] for w in w_vecs]
    n_par = sum(widths)

    def body(me_ref, a_ref, s_ref, *refs):
        w_refs, m_refs, v_refs = refs[:n], refs[n:2 * n], refs[2 * n:3 * n]
        loss_ref, outs = refs[3 * n], refs[3 * n + 1:]
        mine = s_ref[...]
        tot = jnp.where(me_ref[0] == 0, mine, a_ref[0])
        for d in range(1, 8):
            tot = tot + jnp.where(me_ref[0] == d, mine, a_ref[d])
        tot = jnp.sum(tot, axis=0, keepdims=True)
        sq = jnp.sum(tot[:, n_par:], axis=1, keepdims=True)
        loss_ref[...] = jnp.broadcast_to(sq * (0.5 / D_MODEL), loss_ref.shape)
        lo = 0
        for k in range(n):
            g = tot[:, lo:lo + widths[k]]
            lo += widths[k]
            outs[k][...] = g
            outs[n + k][...], outs[2 * n + k][...], outs[3 * n + k][...] = _adam_math(
                w_refs[k][...], g, m_refs[k][...], v_refs[k][...])

    vm = pl.BlockSpec(memory_space=pltpu.VMEM)
    vecs = [jax.ShapeDtypeStruct((1, w), F32) for w in widths] * 4
    res = pl.pallas_call(
        body, name="small_adamw",
        grid_spec=pltpu.PrefetchScalarGridSpec(num_scalar_prefetch=1, grid=(), in_specs=[vm] * (2 + 3 * n),
                                               out_specs=[vm] * (1 + 4 * n)),
        out_shape=[jax.ShapeDtypeStruct((1, 128), F32)] + vecs,
    )(me, gathered, small, *w_vecs, *m_vecs, *v_vecs)
    return res[0], res[1:1 + n], res[1 + n:1 + 2 * n], res[1 + 2 * n:1 + 3 * n], res[1 + 3 * n:]


def _adamw(w, g, m, v, name):
    r, c = w.shape
    tr = min(r, 256)

    def body(w_ref, g_ref, m_ref, v_ref, go_ref, d_ref, nm_ref, nv_ref):
        g = g_ref[...]
        go_ref[...] = g
        d_ref[...], nm_ref[...], nv_ref[...] = _adam_math(w_ref[...], g, m_ref[...], v_ref[...])

    spec = pl.BlockSpec((tr, c), lambda i: (i, 0))
    shape = jax.ShapeDtypeStruct((r, c), F32)
    return _pallas(
        body, name=name, grid=(r // tr,),
        in_specs=[spec] * 4, out_specs=[spec] * 4, out_shape=[shape] * 4,
        compiler_params=_params(("parallel",), 48),
    )(w, g, m, v)


def kernel(x, w_in, w_pool, pool_scale, w_out, ln_gain, ln_bias, loss_target, m_w_in, m_w_pool, m_pool_scale, m_w_out, m_ln_gain, m_ln_bias, v_w_in, v_w_pool, v_pool_scale, v_w_out, v_ln_gain, v_ln_bias):
    xi, yi, ci = lax.axis_index("x"), lax.axis_index("y"), lax.axis_index("c")
    chip = (2 * xi + yi).astype(jnp.int32).reshape(1)
    core = ci.astype(jnp.int32).reshape(1)
    n_groups = len(POOL_WINDOWS)
    shard_c = w_pool.shape[2]

    w_in_b = _cast_bf16(w_in[0], chip, "cast_w_in", 256)
    w_out_b = _cast_bf16(w_out[0], chip, "cast_w_out", 256)
    w_pool_b = _cast_bf16(w_pool[0].reshape(n_groups * shard_c, POOL_GROUP_DIM), chip, "cast_w_pool", 256)

    chip_core = jnp.concatenate([chip, core])
    onward = (2 * (xi ^ ci) + (yi ^ (1 - ci))).astype(jnp.int32).reshape(1)
    g_x, full_in, full_out, full_pool, small, small_all = _step(
        x[0], loss_target[0], [w_in_b, w_out_b, w_pool_b], pool_scale, ln_gain, ln_bias,
        (core, chip_core, onward, _in_proj_plan(xi, yi)))
    half_c = shard_c // 2
    grad_w_in = full_in.reshape(D_MODEL, SHARD_IN)
    grad_w_out = full_out.reshape(D_MODEL // N_SHARDS, D_MODEL)
    grad_w_pool = (full_pool.reshape(2, n_groups, half_c, POOL_GROUP_DIM).transpose(1, 0, 2, 3)
                   .reshape(n_groups * shard_c, POOL_GROUP_DIM))

    grad_w_in, d_in, nm_in, nv_in = _adamw(w_in[0], grad_w_in, m_w_in[0], v_w_in[0], "adamw_w_in")
    grad_w_out, d_out, nm_out, nv_out = _adamw(w_out[0], grad_w_out, m_w_out[0], v_w_out[0], "adamw_w_out")
    flat = lambda t: t[0].reshape(n_groups * shard_c, POOL_GROUP_DIM)
    grad_w_pool, d_pool, nm_pool, nv_pool = _adamw(flat(w_pool), grad_w_pool, flat(m_w_pool), flat(v_w_pool),
                                                   "adamw_w_pool")

    me = (4 * xi + 2 * yi + ci).astype(jnp.int32).reshape(1)
    loss_v, g_vecs, d_vecs, nm_vecs, nv_vecs = _small_adamw(
        small_all, small, me, [pool_scale, ln_gain, ln_bias], [m_pool_scale, m_ln_gain, m_ln_bias],
        [v_pool_scale, v_ln_gain, v_ln_bias])
    g_scale, g_gain, g_bias = g_vecs
    d_scale, d_gain, d_bias = d_vecs
    nm_scale, nm_gain, nm_bias = nm_vecs
    nv_scale, nv_gain, nv_bias = nv_vecs
    pool_shape = w_pool.shape
    return (loss_v[0, 0], g_x[None],
            grad_w_in[None], grad_w_pool.reshape(pool_shape), g_scale, grad_w_out[None], g_gain, g_bias,
            d_in[None], d_pool.reshape(pool_shape), d_scale, d_out[None], d_gain, d_bias,
            nm_in[None], nm_pool.reshape(pool_shape), nm_scale, nm_out[None], nm_gain, nm_bias,
            nv_in[None], nv_pool.reshape(pool_shape), nv_scale, nv_out[None], nv_gain, nv_bias)
```

```python
import functools

import jax
import jax.numpy as jnp
import numpy as np
from jax import lax
from jax.experimental import pallas as pl
from jax.experimental.pallas import tpu as pltpu

F32 = jnp.float32
BF16 = jnp.bfloat16
MESH = pl.DeviceIdType.MESH
ANY = pl.BlockSpec(memory_space=pl.ANY)

D_MODEL = 2048
D_ATTN = 1024
D_POOL = 1024
HEAD_DIM = 128
N_HEADS = 8
ROPE_DIM = 32
ROPE_THETA = 500000.0
DILATIONS = (1, 4, 16)
KEY_BLOCK = 128
CHUNK = 2 * KEY_BLOCK
STAT_LANES = 128
POOL_WINDOWS = (2, 4, 8, 16)
POOL_GROUP_DIM = 256
POOL_HALO = 16
D_QKV = 3 * D_ATTN
D_UG = D_POOL + D_MODEL
D_IN = D_QKV + D_UG
N_SHARDS = 4
SHARD_IN = D_IN // N_SHARDS
LN_EPS = 1e-5
DEEPNORM_ALPHA = 2.0 ** 0.25
ADAM_LR = 0.001
ADAM_B1 = 0.9
ADAM_B2 = 0.999
ADAM_EPS = 1e-08
ADAM_WD = 0.01
ADAM_STEP = 10
NEG = -1e30
MIB = 1024 * 1024


def _params(sem, vmem_mib):
    return pltpu.CompilerParams(dimension_semantics=sem, vmem_limit_bytes=vmem_mib * MIB)


def _pallas(body, **kwargs):
    pin = lambda s: pltpu.HBM(s.shape, s.dtype) if len(s.shape) >= 2 else s
    out_shape = kwargs.pop("out_shape")
    out_shape = [pin(s) for s in out_shape] if isinstance(out_shape, (list, tuple)) else pin(out_shape)
    call = pl.pallas_call(body, out_shape=out_shape, **kwargs)

    def run(*operands):
        return call(*[pltpu.with_memory_space_constraint(o, pltpu.HBM) if o.ndim >= 2 else o for o in operands])

    return run


class _Exchange:
    def __init__(self, operands, out_shape, aliases, sems, start, finish):
        self.operands, self.out_shape, self.aliases, self.sems = list(operands), list(out_shape), dict(aliases), list(sems)
        self.start, self.finish = start, finish


def _run_exchange(comm, name):
    n_in, n_out = len(comm.operands), len(comm.out_shape)

    def body(*refs):
        ins, outs, sems = refs[:n_in], refs[n_in:n_in + n_out], refs[n_in + n_out:]
        comm.start(ins, outs, sems)
        comm.finish(ins, outs, sems)

    return _pallas(
        body, name=name, in_specs=[ANY] * n_in, out_specs=[ANY] * n_out, out_shape=comm.out_shape,
        input_output_aliases=comm.aliases, scratch_shapes=comm.sems,
    )(*comm.operands)


def _call(body, *, name, grid, in_specs, out_specs, out_shape, scratch_shapes, semantics, vmem_mib, args,
          aliases=None, comm=None, prefetch=()):
    aliases = dict(aliases or {})
    n_pre, n_in, n_out, n_scr = len(prefetch), len(in_specs), len(out_specs), len(scratch_shapes)
    c_in, c_out = (len(comm.operands), len(comm.out_shape)) if comm else (0, 0)
    c_shapes, c_sems, c_operands = (comm.out_shape, comm.sems, comm.operands) if comm else ([], [], [])

    def hosted(*refs):
        pre, refs = refs[:n_pre], refs[n_pre:]
        a = n_in
        b = a + c_in
        c = b + n_out
        d = c + c_out
        e = d + n_scr
        if comm is None:
            body(*pre, *refs)
            return
        ids = [pl.program_id(k) for k in range(len(grid))]
        first = functools.reduce(jnp.logical_and, [i == 0 for i in ids])
        last = functools.reduce(jnp.logical_and, [i == g - 1 for i, g in zip(ids, grid)])

        @pl.when(first)
        def _():
            comm.start(refs[a:b], refs[c:d], refs[e:])

        body(*pre, *refs[:a], *refs[b:c], *refs[d:e])

        @pl.when(last)
        def _():
            comm.finish(refs[a:b], refs[c:d], refs[e:])

    if comm:
        semantics = ("arbitrary",) * len(grid)
        for i, o in comm.aliases.items():
            aliases[n_pre + n_in + i] = n_out + o
    outs = _pallas(
        hosted, name=name,
        grid_spec=pltpu.PrefetchScalarGridSpec(
            num_scalar_prefetch=n_pre, grid=grid, in_specs=list(in_specs) + [ANY] * c_in,
            out_specs=list(out_specs) + [ANY] * c_out, scratch_shapes=list(scratch_shapes) + c_sems),
        out_shape=list(out_shape) + c_shapes, input_output_aliases=aliases,
        compiler_params=_params(semantics, vmem_mib),
    )(*prefetch, *args, *c_operands)
    return list(outs[:n_out]), list(outs[n_out:])


def _dot_nn(a, b):
    return jnp.dot(a, b, preferred_element_type=F32)


def _dot_nt(a, b):
    return lax.dot_general(a, b, (((1,), (1,)), ((), ())), preferred_element_type=F32)


def _dot_tn(a, b):
    return lax.dot_general(a, b, (((0,), (0,)), ((), ())), preferred_element_type=F32)


def _fold_rows(a):
    r, c = a.shape
    return jnp.sum(a.reshape(r // 8, 8, c), axis=0)


def _cast_bf16(a, chip, name, rows):
    r, c = a.shape

    def body(chip_ref, a_ref, o_ref):
        o_ref[...] = a_ref[...].astype(BF16)

    return _pallas(
        body, name=name,
        grid_spec=pltpu.PrefetchScalarGridSpec(
            num_scalar_prefetch=1, grid=(r // rows,),
            in_specs=[pl.BlockSpec((rows, c), lambda i, chip_ref: (i, 0))],
            out_specs=pl.BlockSpec((None, rows, c), lambda i, chip_ref: (chip_ref[0], i, 0))),
        out_shape=jax.ShapeDtypeStruct((N_SHARDS, r, c), BF16),
        compiler_params=_params(("parallel",), 32),
    )(chip, a)


def _mesh_place():
    x, y, c = lax.axis_index("x"), lax.axis_index("y"), lax.axis_index("c")
    return x, y, c, [(1 - x, y), (x, 1 - y), (1 - x, 1 - y)]


def _remote(src, dst, send_sem, recv_sem, to):
    return pltpu.make_async_remote_copy(src_ref=src, dst_ref=dst, send_sem=send_sem, recv_sem=recv_sem,
                                        device_id=to, device_id_type=MESH)


def _rope_tables(seq):
    half = ROPE_DIM // 2
    inv_freq = (np.float64(ROPE_THETA) ** (-(2.0 * np.arange(half, dtype=np.float64)) / ROPE_DIM)).astype(np.float32)
    ang = np.arange(seq, dtype=np.float32)[:, None] * inv_freq[None, :]
    cos = np.cos(ang.astype(np.float64)).astype(np.float32)
    sin = np.sin(ang.astype(np.float64)).astype(np.float32)
    pad = np.zeros((seq, HEAD_DIM - ROPE_DIM), np.float32)
    zeros = np.zeros((seq, half), np.float32)
    c_tab = np.concatenate([cos, cos, pad + 1.0], axis=1)
    up_tab = np.concatenate([-sin, zeros, pad], axis=1)
    down_tab = np.concatenate([zeros, sin, pad], axis=1)
    return jnp.asarray(c_tab), jnp.asarray(up_tab), jnp.asarray(down_tab)


def _rotate_heads(t, c_tab, up_tab, down_tab):
    outs = []
    for h in range(t.shape[1] // HEAD_DIM):
        th = t[:, h * HEAD_DIM:(h + 1) * HEAD_DIM]
        up = pltpu.roll(th, HEAD_DIM - ROPE_DIM // 2, axis=1)
        down = pltpu.roll(th, ROPE_DIM // 2, axis=1)
        outs.append(th * c_tab + up * up_tab + down * down_tab)
    return outs[0] if len(outs) == 1 else jnp.concatenate(outs, axis=1)


def _to_pattern(slabs_ref, dst_ref, dil, dtype):
    n_slabs, rows, _ = slabs_ref.shape
    for s in range(n_slabs):
        for r in range(dil):
            dst_ref[r, :, s * 128:(s + 1) * 128] = slabs_ref[s, pl.ds(r, rows // dil, dil), :].astype(dtype)


def _from_pattern(src_ref, slabs_ref, dil):
    n_slabs, rows, _ = slabs_ref.shape
    for s in range(n_slabs):
        for r in range(dil):
            slabs_ref[s, pl.ds(r, rows // dil, dil), :] = src_ref[r, :, s * 128:(s + 1) * 128].astype(F32)


def _store_slabs(slabs_ref, value):
    for s in range(slabs_ref.shape[0]):
        slabs_ref[s] = value[:, s * 128:(s + 1) * 128]


W_IN_CHUNKS = 8


def _in_proj_plan(x, y):
    shards = [2 * x + y, 2 * (1 - x) + y, 2 * x + (1 - y), 2 * (1 - x) + (1 - y)]
    last_row = jnp.int32(-2)

    def table(active, col_of):
        cols, rows = [], []
        first_col = functools.reduce(lambda acc, j: jnp.where(active[j], col_of(shards[j]), acc), reversed(range(4)),
                                     jnp.int32(0))
        held_col, seen = first_col, jnp.bool_(False)
        for j in range(4):
            cols.append(jnp.where(active[j], col_of(shards[j]), held_col))
            rows.append(jnp.where(active[j], -1, jnp.where(seen, last_row, 0)))
            held_col = jnp.where(active[j], col_of(shards[j]), held_col)
            seen = jnp.logical_or(seen, active[j])
        return cols, rows

    q_cols, q_rows = table([s < 2 for s in shards], lambda s: s)
    h_cols, h_rows = table([s >= 2 for s in shards], lambda s: s - 2)
    return jnp.stack([jnp.asarray(v, jnp.int32) for v in shards + q_cols + q_rows + h_cols + h_rows])


def _in_proj_gathering(x, w_bufs, tabs, plan):
    seq = x.shape[0]
    tm, tn = 512, SHARD_IN
    n_tiles = seq // tm
    heads = tn // HEAD_DIM
    k_heads_in_second = 2 * D_ATTN // HEAD_DIM - heads
    d4, d16 = DILATIONS[1], DILATIONS[2]
    DIAGONAL = 2
    chunk = D_MODEL // 2 // W_IN_CHUNKS
    early = [(0, D_MODEL // 2, q * chunk, chunk) for q in range(W_IN_CHUNKS)]
    late = [(a, w_bufs[a].shape[1] // 2, 0, w_bufs[a].shape[1] // 2) for a in (1, 2)]
    pieces = early + late
    early_ids, late_ids = range(len(early)), range(len(early), len(pieces))

    def body(plan_ref, x_ref, w_in_in, w_out_in, w_pool_in, c_ref, up_ref, down_ref,
             o1_ref, o4_ref, o16_ref, hug_ref, w_ref, w_out_ref, w_pool_ref,
             wbuf_ref, res_ref, w_sem, ici_send, ici_recv, d2d_send, d2d_recv):
        j, i = pl.program_id(0), pl.program_id(1)
        mx, my, mc, chips = _mesh_place()
        sibling = (mx, my, 1 - mc)
        gathered = (w_ref, w_out_ref, w_pool_ref)
        chip_of = lambda k: 2 * chips[k][0] + chips[k][1]

        def piece(n, chip, core):
            a, per_core, offset, size = pieces[n]
            return gathered[a].at[chip, pl.ds(core * per_core + offset, size)]

        def to_neighbour(k, n):
            mine = piece(n, 2 * mx + my, mc)
            return _remote(mine, mine, ici_send.at[n, k], ici_recv.at[n, k], (*chips[k], mc))

        def relay(n):
            theirs = piece(n, 2 * (mx ^ (1 - mc)) + (my ^ mc), mc)
            return _remote(theirs, theirs, ici_send.at[n, DIAGONAL], ici_recv.at[n, DIAGONAL], (mx ^ mc, my ^ (1 - mc), mc))

        def arrival(k, n):
            theirs = piece(n, chip_of(k), mc)
            return _remote(theirs, theirs, ici_send.at[n, k], ici_recv.at[n, k], (*chips[k], mc))

        def to_sibling(k, n, core):
            theirs = piece(n, chip_of(k), core)
            return _remote(theirs, theirs, d2d_send.at[n, k], d2d_recv.at[n, k], sibling)

        def take(k, ids):
            for n in ids:
                arrival(k, n).wait_recv()
                to_sibling(k, n, mc).start()

        def taken(k, ids):
            for n in ids:
                to_sibling(k, n, 1 - mc).wait_recv()

        first_tile = i == 0

        @pl.when(jnp.logical_and(j == 0, first_tile))
        def _():
            for n in range(len(pieces)):
                for k in range(DIAGONAL):
                    to_neighbour(k, n).start()

        @pl.when(jnp.logical_and(j == 1, first_tile))
        def _():
            take(0, early_ids)
            taken(0, early_ids)

        ahead = i == n_tiles - 3
        slot = j % 2
        fetch = lambda step, half: pltpu.make_async_copy(w_ref.at[plan_ref[step]], wbuf_ref.at[half], w_sem.at[half])

        @pl.when(jnp.logical_and(j == 1, ahead))
        def _():
            take(1, early_ids)
            for n in early_ids:
                relay(n).start()
            taken(1, early_ids)
            for k in range(DIAGONAL):
                take(k, late_ids)
            for n in late_ids:
                relay(n).start()
            for k in range(DIAGONAL):
                taken(k, late_ids)
            fetch(2, 0).start()

        @pl.when(jnp.logical_and(j == 2, ahead))
        def _():
            take(DIAGONAL, range(len(pieces)))
            taken(DIAGONAL, range(len(pieces)))
            fetch(3, 1).start()

        shard = plan_ref[j]

        @pl.when(jnp.logical_and(j <= 1, first_tile))
        def _():
            cp = fetch(j, slot)
            cp.start()
            cp.wait()

        @pl.when(jnp.logical_and(j >= 2, first_tile))
        def _():
            fetch(j, slot).wait()

        xb = x_ref[...].astype(BF16)
        group = 4 * HEAD_DIM
        accs = [_dot_nn(xb, wbuf_ref[slot, :, g * group:(g + 1) * group]) for g in range(tn // group)]

        def emit_qkv(rotated_heads):
            for h in range(heads):
                lanes = (h * HEAD_DIM) % group
                th = accs[h * HEAD_DIM // group][:, lanes:lanes + HEAD_DIM]
                if h < rotated_heads:
                    th = _rotate_heads(th, c_ref[...], up_ref[...], down_ref[...])
                res_ref[h] = th
                o1_ref[:, h * HEAD_DIM:(h + 1) * HEAD_DIM] = th.astype(BF16)
            _to_pattern(res_ref, o4_ref, d4, BF16)
            _to_pattern(res_ref, o16_ref, d16, BF16)

        @pl.when(shard == 0)
        def _():
            emit_qkv(heads)

        @pl.when(shard == 1)
        def _():
            emit_qkv(k_heads_in_second)

        @pl.when(shard >= 2)
        def _():
            for g, acc in enumerate(accs):
                hug_ref[:, g * group:(g + 1) * group] = acc.astype(BF16)

        @pl.when(jnp.logical_and(j == 3, i == n_tiles - 1))
        def _():
            for n in range(len(pieces)):
                for k in range(DIAGONAL):
                    to_neighbour(k, n).wait_send()
                relay(n).wait_send()
                for k in range(DIAGONAL + 1):
                    to_sibling(k, n, mc).wait_send()

    def held(base, last):
        return lambda j, i, plan_ref: jnp.where(plan_ref[base + j] == -1, i,
                                                jnp.where(plan_ref[base + j] == -2, last, 0))

    q_row, h_row = held(8, n_tiles - 1), held(16, n_tiles - 1)
    tab_spec = pl.BlockSpec((tm, HEAD_DIM), lambda j, i, plan_ref: (i, 0))
    sems = [pltpu.SemaphoreType.DMA((len(pieces), 3))] * 4
    o1, o4, o16, hug, w_in_g, w_out_g, w_pool_g = _pallas(
        body, name="in_proj_gathering",
        grid_spec=pltpu.PrefetchScalarGridSpec(
            num_scalar_prefetch=1, grid=(N_SHARDS, n_tiles),
            in_specs=[pl.BlockSpec((tm, D_MODEL), lambda j, i, plan_ref: (i, 0)), ANY, ANY, ANY,
                      tab_spec, tab_spec, tab_spec],
            out_specs=[pl.BlockSpec((tm, tn), lambda j, i, p: (q_row(j, i, p), p[4 + j])),
                       pl.BlockSpec((d4, tm // d4, tn), lambda j, i, p: (0, q_row(j, i, p), p[4 + j])),
                       pl.BlockSpec((d16, tm // d16, tn), lambda j, i, p: (0, q_row(j, i, p), p[4 + j])),
                       pl.BlockSpec((tm, tn), lambda j, i, p: (h_row(j, i, p), p[12 + j])),
                       ANY, ANY, ANY],
            scratch_shapes=[pltpu.VMEM((2, D_MODEL, tn), BF16), pltpu.VMEM((heads, tm, HEAD_DIM), F32),
                            pltpu.SemaphoreType.DMA((2,))] + sems),
        out_shape=[jax.ShapeDtypeStruct((seq, D_QKV), BF16),
                   jax.ShapeDtypeStruct((d4, seq // d4, D_QKV), BF16),
                   jax.ShapeDtypeStruct((d16, seq // d16, D_QKV), BF16),
                   jax.ShapeDtypeStruct((seq, D_UG), BF16)]
        + [jax.ShapeDtypeStruct(b.shape, b.dtype) for b in w_bufs],
        input_output_aliases={2: 4, 3: 5, 4: 6},
        compiler_params=_params(("arbitrary", "arbitrary"), 58),
    )(plan, x, *w_bufs, *tabs)
    return [o1[None], o4, o16], hug, w_in_g, w_out_g, w_pool_g


def _band_masks():
    row = lax.broadcasted_iota(jnp.int32, (KEY_BLOCK, KEY_BLOCK), 0)
    col = lax.broadcasted_iota(jnp.int32, (KEY_BLOCK, KEY_BLOCK), 1)
    return col <= row, col >= row


def _attn_fwd(qkv, name):
    dil, n, _ = qkv.shape
    scale = HEAD_DIM ** -0.5
    lo, hi = slice(0, KEY_BLOCK), slice(KEY_BLOCK, CHUNK)

    def body(q_ref, k_ref, v_ref, kb_ref, vb_ref, o_ref, st_ref):
        i = pl.program_id(1)
        cur_mask, prev_mask = _band_masks()
        before_mask = jnp.logical_and(prev_mask, i > 0)
        lane = lax.broadcasted_iota(jnp.int32, (KEY_BLOCK, STAT_LANES), 1)
        tasks = [(rows, h) for rows in (lo, hi) for h in range(N_HEADS)]
        head = lambda h: slice(h * HEAD_DIM, (h + 1) * HEAD_DIM)

        def prev_of(rows, h):
            if rows is lo:
                return kb_ref[:, head(h)], vb_ref[:, head(h)], before_mask
            return k_ref[lo, head(h)], v_ref[lo, head(h)], prev_mask

        scores = []
        for rows, h in tasks:
            q = q_ref[rows, head(h)]
            scores.append((_dot_nt(q, prev_of(rows, h)[0]), _dot_nt(q, k_ref[rows, head(h)])))
        probs = []
        for (rows, h), (qk_prev, qk_cur) in zip(tasks, scores):
            s_prev = jnp.where(prev_of(rows, h)[2], qk_prev * scale, NEG)
            s_cur = jnp.where(cur_mask, qk_cur * scale, NEG)
            m = jnp.max(jnp.maximum(s_prev, s_cur), axis=-1, keepdims=True)
            p_prev = jnp.exp(s_prev - m)
            p_cur = jnp.exp(s_cur - m)
            den = jnp.sum(p_prev + p_cur, axis=-1, keepdims=True)
            probs.append((p_prev.astype(BF16), p_cur.astype(BF16), den, m + jnp.log(den)))
        stats = [jnp.zeros((KEY_BLOCK, STAT_LANES), F32), jnp.zeros((KEY_BLOCK, STAT_LANES), F32)]
        for (rows, h), (p_prev, p_cur, den, lse) in zip(tasks, probs):
            o = _dot_nn(p_cur, v_ref[rows, head(h)]) + _dot_nn(p_prev, prev_of(rows, h)[1])
            o_ref[rows, head(h)] = (o / den).astype(BF16)
            b = 0 if rows is lo else 1
            stats[b] = jnp.where(lane == h, lse, stats[b])
        st_ref[lo, :] = stats[0]
        st_ref[hi, :] = stats[1]

    main = lambda cb: pl.BlockSpec((None, CHUNK, D_ATTN), lambda r, i: (r, i, cb))
    before = lambda cb: pl.BlockSpec((None, KEY_BLOCK, D_ATTN), lambda r, i: (r, jnp.maximum(2 * i - 1, 0), cb))
    return _pallas(
        body, name=name, grid=(dil, n // CHUNK),
        in_specs=[main(0), main(1), main(2), before(1), before(2)],
        out_specs=[main(0), pl.BlockSpec((None, CHUNK, STAT_LANES), lambda r, i: (r, i, 0))],
        out_shape=[jax.ShapeDtypeStruct((dil, n, D_ATTN), BF16), jax.ShapeDtypeStruct((dil, n, STAT_LANES), F32)],
        compiler_params=_params(("parallel", "parallel"), 40),
    )(qkv, qkv, qkv, qkv, qkv)


def _attn_bwd(qkv, do, stats, name, comm=None):
    dil, n, _ = qkv.shape
    n_blocks = n // KEY_BLOCK
    last = n // CHUNK - 1
    scale = HEAD_DIM ** -0.5
    lo, hi = slice(0, KEY_BLOCK), slice(KEY_BLOCK, CHUNK)

    def body(q_ref, k_ref, v_ref, kb_ref, vb_ref, qa_ref, do_ref, doa_ref, st_ref, sta_ref, dq_ref, dk_ref, dv_ref):
        i = pl.program_id(1)
        cur_mask, prev_mask = _band_masks()
        before_mask = jnp.logical_and(prev_mask, i > 0)
        after_mask = jnp.logical_and(prev_mask, i < last)

        rows_cat = lambda a, b: jnp.concatenate([a, b], axis=0)
        masks = (jnp.concatenate([before_mask, cur_mask], axis=1), jnp.concatenate([prev_mask, cur_mask], axis=1),
                 after_mask)

        def operands(h):
            cols = slice(h * HEAD_DIM, (h + 1) * HEAD_DIM)
            lse_c, del_c = slice(h, h + 1), slice(N_HEADS + h, N_HEADS + h + 1)
            q = (q_ref[lo, cols], q_ref[hi, cols], qa_ref[:, cols])
            do = (do_ref[lo, cols], do_ref[hi, cols], doa_ref[:, cols])
            keys = (rows_cat(kb_ref[:, cols], k_ref[lo, cols]), k_ref[:, cols], k_ref[hi, cols])
            vals = (rows_cat(vb_ref[:, cols], v_ref[lo, cols]), v_ref[:, cols], v_ref[hi, cols])
            st = ((st_ref[lo, lse_c], st_ref[lo, del_c]), (st_ref[hi, lse_c], st_ref[hi, del_c]),
                  (sta_ref[:, lse_c], sta_ref[:, del_c]))
            return cols, q, do, keys, vals, st

        group = N_HEADS // 2
        for first_head in range(0, N_HEADS, group):
            heads = range(first_head, first_head + group)
            raw = {}
            for h in heads:
                _, q, do, keys, vals, _ = operands(h)
                raw[h] = [(_dot_nt(q[j], keys[j]), _dot_nt(do[j], vals[j])) for j in range(3)]
            grads = {}
            for h in heads:
                st = operands(h)[5]
                grads[h] = []
                for j in range(3):
                    qk, dp = raw[h][j]
                    lse, delta = st[j]
                    p = jnp.exp(jnp.where(masks[j], qk * scale, NEG) - lse)
                    grads[h].append((p.astype(BF16), (p * (dp - delta) * scale).astype(BF16)))
            for h in heads:
                cols, q, do, keys, _, _ = operands(h)
                (p0, ds0), (p1, ds1), (pa, dsa) = grads[h]
                own, nxt = slice(KEY_BLOCK, CHUNK), slice(0, KEY_BLOCK)

                def put(ref, rows, val, cols=cols):
                    ref[rows, cols] = val.astype(ref.dtype)

                put(dq_ref, lo, _dot_nn(ds0, keys[0]))
                put(dq_ref, hi, _dot_nn(ds1, keys[1]))
                put(dk_ref, lo, _dot_tn(rows_cat(ds0[:, own], ds1[:, nxt]), q_ref[:, cols]))
                put(dk_ref, hi, _dot_tn(rows_cat(ds1[:, own], dsa), rows_cat(q[1], q[2])))
                put(dv_ref, lo, _dot_tn(rows_cat(p0[:, own], p1[:, nxt]), do_ref[:, cols]))
                put(dv_ref, hi, _dot_tn(rows_cat(p1[:, own], pa), rows_cat(do[1], do[2])))

    def spec(rows, width, row_of, cb):
        return pl.BlockSpec((None, rows, width), lambda r, i: (r, row_of(i), cb))

    same = lambda i: i
    before = lambda i: jnp.maximum(2 * i - 1, 0)
    after = lambda i: jnp.minimum(2 * i + 2, n_blocks - 1)
    out = spec(CHUNK, D_ATTN, same, 0)
    return _call(
        body, name=name, grid=(dil, n // CHUNK),
        in_specs=[spec(CHUNK, D_ATTN, same, 0), spec(CHUNK, D_ATTN, same, 1), spec(CHUNK, D_ATTN, same, 2),
                  spec(KEY_BLOCK, D_ATTN, before, 1), spec(KEY_BLOCK, D_ATTN, before, 2),
                  spec(KEY_BLOCK, D_ATTN, after, 0),
                  spec(CHUNK, D_ATTN, same, 0), spec(KEY_BLOCK, D_ATTN, after, 0),
                  spec(CHUNK, STAT_LANES, same, 0), spec(KEY_BLOCK, STAT_LANES, after, 0)],
        out_specs=[out, out, out],
        out_shape=[jax.ShapeDtypeStruct((dil, n, D_ATTN), BF16)] * 3,
        scratch_shapes=[], semantics=("parallel", "parallel"), vmem_mib=40,
        args=(qkv, qkv, qkv, qkv, qkv, qkv, do, do, stats, stats), comm=comm)


def _window_sums(ext, window, backward):
    rows = ext.shape[0]
    acc, span = ext, 1
    while span < window:
        acc = acc + pltpu.roll(acc, (rows - span) if backward else span, axis=0)
        span *= 2
    return acc


def _pool_group_weight(wp_ref, g):
    return jnp.concatenate([wp_ref[k, g] for k in range(N_SHARDS)], axis=0)


def _mix_gate(o_list, st_list, hug, w_pool_g, pool_scale):
    seq = hug.shape[0]
    tm = 256
    halo_blocks = tm // POOL_HALO
    d4, d16 = DILATIONS[1], DILATIONS[2]

    def body(o1_ref, o4_ref, o16_ref, l1_ref, l4_ref, l16_ref, u_ref, halo_ref, ga_ref, gp_ref, wp_ref, sc_ref,
             y_ref, mix_ref, lse_ref, pooled_ref, n4_ref, n16_ref, nl4_ref, nl16_ref):
        i = pl.program_id(0)
        _from_pattern(o4_ref, n4_ref, d4)
        _from_pattern(o16_ref, n16_ref, d16)
        _from_pattern(l4_ref, nl4_ref, d4)
        _from_pattern(l16_ref, nl16_ref, d16)
        la, lb, lc = l1_ref[...], nl4_ref[0], nl16_ref[0]
        mx = jnp.maximum(jnp.maximum(la, lb), lc)
        ea, eb, ec = jnp.exp(la - mx), jnp.exp(lb - mx), jnp.exp(lc - mx)
        tot = ea + eb + ec
        lse_ref[...] = mx + jnp.log(tot)
        wa, wb, wc = ea / tot, eb / tot, ec / tot
        ga = ga_ref[...].astype(F32)
        silu_a = ga * jax.nn.sigmoid(ga)
        for h in range(N_HEADS):
            cols = slice(h * HEAD_DIM, (h + 1) * HEAD_DIM)
            hc = slice(h, h + 1)
            attn = wa[:, hc] * o1_ref[:, cols].astype(F32) + wb[:, hc] * n4_ref[h] + wc[:, hc] * n16_ref[h]
            mix_ref[:, cols] = attn.astype(BF16)
            y_ref[:, cols] = (attn * silu_a[:, cols]).astype(BF16)

        u = u_ref[...].astype(F32)
        halo = jnp.where(i > 0, halo_ref[...].astype(F32), 0.0)
        ext = jnp.concatenate([halo, u], axis=0)
        pos = i * tm + lax.broadcasted_iota(jnp.int32, (tm, 1), 0)
        gp = gp_ref[...].astype(F32)
        gated_scale = sc_ref[...] * (gp * jax.nn.sigmoid(gp))
        for g, window in enumerate(POOL_WINDOWS):
            cols = slice(g * POOL_GROUP_DIM, (g + 1) * POOL_GROUP_DIM)
            sums = _window_sums(ext[:, cols], window, backward=False)[POOL_HALO:, :]
            count = jnp.minimum(pos + 1, window).astype(F32)
            pooled = (sums / count - u[:, cols]).astype(BF16)
            pooled_ref[:, cols] = pooled
            pre = _dot_nn(pooled, _pool_group_weight(wp_ref, g))
            out_cols = slice(D_ATTN + g * POOL_GROUP_DIM, D_ATTN + (g + 1) * POOL_GROUP_DIM)
            mix_ref[:, out_cols] = pre.astype(BF16)
            y_ref[:, out_cols] = (pre * gated_scale[:, cols]).astype(BF16)

    row = lambda width, cb=0: pl.BlockSpec((tm, width), lambda i: (i, cb))
    pat = lambda d, width: pl.BlockSpec((d, tm // d, width), lambda i: (0, i, 0))
    return _pallas(
        body, name="mix_gate", grid=(seq // tm,),
        in_specs=[row(D_ATTN), pat(d4, D_ATTN), pat(d16, D_ATTN),
                  row(STAT_LANES), pat(d4, STAT_LANES), pat(d16, STAT_LANES),
                  row(D_POOL),
                  pl.BlockSpec((POOL_HALO, D_POOL), lambda i: (jnp.maximum(i * halo_blocks - 1, 0), 0)),
                  row(D_ATTN, 1), row(D_POOL, 2),
                  pl.BlockSpec(w_pool_g.shape, lambda i: (0, 0, 0, 0)),
                  pl.BlockSpec((1, D_POOL), lambda i: (0, 0))],
        out_specs=[row(D_MODEL), row(D_MODEL), row(STAT_LANES), row(D_POOL)],
        out_shape=[jax.ShapeDtypeStruct((seq, D_MODEL), BF16), jax.ShapeDtypeStruct((seq, D_MODEL), BF16),
                   jax.ShapeDtypeStruct((seq, STAT_LANES), F32), jax.ShapeDtypeStruct((seq, D_POOL), BF16)],
        scratch_shapes=[pltpu.VMEM((N_HEADS, tm, HEAD_DIM), F32), pltpu.VMEM((N_HEADS, tm, HEAD_DIM), F32),
                        pltpu.VMEM((1, tm, STAT_LANES), F32), pltpu.VMEM((1, tm, STAT_LANES), F32)],
        compiler_params=_params(("parallel",), 48),
    )(o_list[0][0], o_list[1], o_list[2], st_list[0][0], st_list[1], st_list[2],
      hug, hug, hug, hug, w_pool_g, pool_scale)


def _out_proj_loss(y, w_out_g, x, target, gain, bias):
    seq = x.shape[0]
    tm = 512

    def body(y_ref, w_ref, x_ref, t_ref, g_ref, b_ref, dz_ref, dzb_ref, gg_ref, gb_ref, loss_ref):
        @pl.when(pl.program_id(0) == 0)
        def _():
            gg_ref[...] = jnp.zeros_like(gg_ref)
            gb_ref[...] = jnp.zeros_like(gb_ref)
            loss_ref[...] = jnp.zeros_like(loss_ref)

        halves = [slice(0, tm // 2), slice(tm // 2, tm)]
        projected = [_dot_nn(y_ref[rows, :], w_ref[...]) for rows in halves]
        for rows, out in zip(halves, projected):
            z = DEEPNORM_ALPHA * x_ref[rows, :] + out
            mu = jnp.mean(z, axis=-1, keepdims=True)
            zc = z - mu
            rstd = lax.rsqrt(jnp.mean(zc * zc, axis=-1, keepdims=True) + LN_EPS)
            xhat = zc * rstd
            gain_v = g_ref[...]
            diff = xhat * gain_v + b_ref[...] - t_ref[rows, :]
            sq = _fold_rows(diff * diff)
            part = sq[:, :128]
            for k in range(1, D_MODEL // 128):
                part = part + sq[:, k * 128:(k + 1) * 128]
            loss_ref[...] += part
            dln = diff * (1.0 / D_MODEL)
            gg_ref[...] += _fold_rows(dln * xhat)
            gb_ref[...] += _fold_rows(dln)
            dxhat = dln * gain_v
            dz = rstd * (dxhat - jnp.mean(dxhat, axis=-1, keepdims=True)
                         - xhat * jnp.mean(dxhat * xhat, axis=-1, keepdims=True))
            dz_ref[rows, :] = dz
            dzb_ref[rows, :] = dz.astype(BF16)

    row = lambda: pl.BlockSpec((tm, D_MODEL), lambda i: (i, 0))
    vec = lambda: pl.BlockSpec((1, D_MODEL), lambda i: (0, 0))
    acc = lambda width: pl.BlockSpec((8, width), lambda i: (0, 0))
    return _pallas(
        body, name="out_proj_loss", grid=(seq // tm,),
        in_specs=[row(), pl.BlockSpec((D_MODEL, D_MODEL), lambda i: (0, 0), pipeline_mode=pl.Buffered(1)),
                  row(), row(), vec(), vec()],
        out_specs=[row(), row(), acc(D_MODEL), acc(D_MODEL), acc(128)],
        out_shape=[jax.ShapeDtypeStruct((seq, D_MODEL), F32), jax.ShapeDtypeStruct((seq, D_MODEL), BF16),
                   jax.ShapeDtypeStruct((8, D_MODEL), F32), jax.ShapeDtypeStruct((8, D_MODEL), F32),
                   jax.ShapeDtypeStruct((8, 128), F32)],
        compiler_params=_params(("arbitrary",), 56),
    )(y, w_out_g.reshape(D_MODEL, D_MODEL), x, target, gain, bias)


def _dy_gate_bwd(dzb, w_out_g, hug, mixpre, pool_scale, lse_all):
    seq = dzb.shape[0]
    tm = 256
    d4, d16 = DILATIONS[1], DILATIONS[2]

    def body(dz_ref, w_ref, ga_ref, gp_ref, mix_ref, sc_ref, lse_ref,
             dh_ref, dpo_ref, do1_ref, do4_ref, do16_ref, st1_ref, st4_ref, st16_ref, da_ref, st_ref):
        dy = _dot_nt(dz_ref[...], w_ref[...])
        ga = ga_ref[...].astype(F32)
        sig = jax.nn.sigmoid(ga)
        attn = mix_ref[:, :D_ATTN].astype(F32)
        dya = dy[:, :D_ATTN]
        dattn = dya * (ga * sig)
        dh_ref[:, :D_ATTN] = (dya * attn * (sig * (1.0 + ga * (1.0 - sig)))).astype(BF16)
        _store_slabs(da_ref, dattn)
        lane = lax.broadcasted_iota(jnp.int32, (tm, STAT_LANES), 1)
        stats = lse_ref[...]
        prod = dattn * attn
        for h in range(N_HEADS):
            delta = jnp.sum(prod[:, h * HEAD_DIM:(h + 1) * HEAD_DIM], axis=-1, keepdims=True)
            stats = jnp.where(lane == N_HEADS + h, delta, stats)
        st_ref[0] = stats
        do1_ref[...] = dattn.astype(BF16)
        st1_ref[...] = stats
        _to_pattern(da_ref, do4_ref, d4, BF16)
        _to_pattern(da_ref, do16_ref, d16, BF16)
        _to_pattern(st_ref, st4_ref, d4, F32)
        _to_pattern(st_ref, st16_ref, d16, F32)

        gp = gp_ref[...].astype(F32)
        sig = jax.nn.sigmoid(gp)
        dyp = dy[:, D_ATTN:]
        dpo_ref[...] = (dyp * (gp * sig)).astype(BF16)
        dh_ref[:, D_ATTN:] = (dyp * (mix_ref[:, D_ATTN:].astype(F32) * sc_ref[...])
                              * (sig * (1.0 + gp * (1.0 - sig)))).astype(BF16)

    row = lambda width, cb=0: pl.BlockSpec((tm, width), lambda i: (i, cb))
    pat = lambda d, width: pl.BlockSpec((d, tm // d, width), lambda i: (0, i, 0))
    pat_shape = lambda d, width, dtype: jax.ShapeDtypeStruct((d, seq // d, width), dtype)
    outs = _pallas(
        body, name="dy_gate_bwd", grid=(seq // tm,),
        in_specs=[row(D_MODEL), pl.BlockSpec((D_MODEL, D_MODEL), lambda i: (0, 0)),
                  row(D_ATTN, 1), row(D_POOL, 2), row(D_MODEL), pl.BlockSpec((1, D_POOL), lambda i: (0, 0)),
                  row(STAT_LANES)],
        out_specs=[row(D_MODEL, D_IN // D_MODEL - 1), row(D_POOL),
                   row(D_ATTN), pat(d4, D_ATTN), pat(d16, D_ATTN),
                   row(STAT_LANES), pat(d4, STAT_LANES), pat(d16, STAT_LANES)],
        out_shape=[jax.ShapeDtypeStruct((seq, D_IN), BF16), jax.ShapeDtypeStruct((seq, D_POOL), BF16),
                   jax.ShapeDtypeStruct((seq, D_ATTN), BF16), pat_shape(d4, D_ATTN, BF16), pat_shape(d16, D_ATTN, BF16),
                   jax.ShapeDtypeStruct((seq, STAT_LANES), F32), pat_shape(d4, STAT_LANES, F32),
                   pat_shape(d16, STAT_LANES, F32)],
        scratch_shapes=[pltpu.VMEM((N_HEADS, tm, HEAD_DIM), F32), pltpu.VMEM((1, tm, STAT_LANES), F32)],
        compiler_params=_params(("parallel",), 48),
    )(dzb, w_out_g.reshape(D_MODEL, D_MODEL), hug, hug, mixpre, pool_scale, lse_all)
    dh, dpo, do1, do4, do16, st1, st4, st16 = outs
    return dh, dpo, [do1[None], do4, do16], [st1[None], st4, st16]


def _pool_bwd(dh, dpo, mixpre, pooled, w_pool_g, pool_scale):
    seq = dpo.shape[0]
    tm = 256
    halo_blocks = tm // POOL_HALO
    last = seq // tm - 1
    n_groups = len(POOL_WINDOWS)
    half_c = POOL_GROUP_DIM // N_SHARDS // 2
    pieces = (N_SHARDS, 2, n_groups * half_c, POOL_GROUP_DIM)

    def body(dh_in_ref, dpo_ref, halo_ref, pre_ref, pooled_ref, wp_ref, sc_ref, du_ref, gw_ref, gs_ref):
        i = pl.program_id(0)

        @pl.when(i == 0)
        def _():
            gw_ref[...] = jnp.zeros_like(gw_ref)
            gs_ref[...] = jnp.zeros_like(gs_ref)

        dpo = dpo_ref[...].astype(F32)
        scale = sc_ref[...]
        gs_ref[...] += _fold_rows(dpo * pre_ref[...].astype(F32))
        halo = jnp.where(i < last, halo_ref[...].astype(F32), 0.0)
        dpw = (jnp.concatenate([dpo, halo], axis=0) * scale).astype(BF16)
        pos = i * tm + lax.broadcasted_iota(jnp.int32, (tm + POOL_HALO, 1), 0)
        for g, window in enumerate(POOL_WINDOWS):
            cols = slice(g * POOL_GROUP_DIM, (g + 1) * POOL_GROUP_DIM)
            dpw_g = dpw[:, cols]
            gw = _dot_tn(pooled_ref[:, cols], dpw_g[:tm, :])
            for piece in range(2 * N_SHARDS):
                gw_ref[piece // 2, piece % 2, g * half_c:(g + 1) * half_c, :] += gw[piece * half_c:(piece + 1) * half_c]
            dpooled = _dot_nt(dpw_g, _pool_group_weight(wp_ref, g))
            count = jnp.minimum(pos + 1, window).astype(F32)
            sums = _window_sums(dpooled / count, window, backward=True)
            du_ref[:, cols] = (sums[:tm, :] - dpooled[:tm, :]).astype(BF16)

    row = lambda width, cb=0: pl.BlockSpec((tm, width), lambda i: (i, cb))
    return _pallas(
        body, name="pool_bwd", grid=(seq // tm,),
        in_specs=[ANY, row(D_POOL),
                  pl.BlockSpec((POOL_HALO, D_POOL),
                               lambda i: (jnp.minimum((i + 1) * halo_blocks, seq // POOL_HALO - 1), 0)),
                  row(D_POOL, 1), row(D_POOL),
                  pl.BlockSpec(w_pool_g.shape, lambda i: (0, 0, 0, 0)),
                  pl.BlockSpec((1, D_POOL), lambda i: (0, 0))],
        out_specs=[row(D_POOL, D_QKV // D_POOL),
                   pl.BlockSpec(pieces, lambda i: (0, 0, 0, 0)),
                   pl.BlockSpec((8, D_POOL), lambda i: (0, 0))],
        out_shape=[jax.ShapeDtypeStruct(dh.shape, dh.dtype),
                   jax.ShapeDtypeStruct(pieces, F32),
                   jax.ShapeDtypeStruct((8, D_POOL), F32)],
        input_output_aliases={0: 0},
        compiler_params=_params(("arbitrary",), 40),
    )(dh, dpo, dpo, mixpre, pooled, w_pool_g, pool_scale)


def _sum_patterns(dh, parts, tabs, unrotate, col_block, name, comm=None):
    seq = dh.shape[0]
    tm, tn = 256, D_ATTN
    per = D_ATTN // tn
    d4, d16 = DILATIONS[1], DILATIONS[2]

    def body(dh_in_ref, a1_ref, a4_ref, a16_ref, ct_ref, up_ref, down_ref, o_ref, n4_ref, n16_ref):
        _from_pattern(a4_ref, n4_ref, d4)
        _from_pattern(a16_ref, n16_ref, d16)
        for s in range(tn // HEAD_DIM):
            cols = slice(s * HEAD_DIM, (s + 1) * HEAD_DIM)
            tot = a1_ref[:, cols].astype(F32) + n4_ref[s] + n16_ref[s]
            if unrotate:
                tot = _rotate_heads(tot, ct_ref[...], -up_ref[...], -down_ref[...])
            o_ref[:, cols] = tot.astype(BF16)

    tab = pl.BlockSpec((tm, HEAD_DIM), lambda i, j: (i, 0))
    pat = lambda d: pl.BlockSpec((d, tm // d, tn), lambda i, j: (0, i, j))
    (dh,), exchanged = _call(
        body, name=name, grid=(seq // tm, per),
        in_specs=[ANY, pl.BlockSpec((tm, tn), lambda i, j: (i, j)), pat(d4), pat(d16), tab, tab, tab],
        out_specs=[pl.BlockSpec((tm, tn), lambda i, j: (i, col_block * per + j))],
        out_shape=[jax.ShapeDtypeStruct(dh.shape, dh.dtype)],
        scratch_shapes=[pltpu.VMEM((tn // HEAD_DIM, tm, HEAD_DIM), F32), pltpu.VMEM((tn // HEAD_DIM, tm, HEAD_DIM), F32)],
        semantics=("parallel", "parallel"), vmem_mib=32, args=(dh, parts[0][0], parts[1], parts[2], *tabs),
        aliases={0: 0}, comm=comm)
    return dh, exchanged


def _grad_w_in(x, dh, half, name, comm=None):
    seq = x.shape[0]
    ts, td, te = 2048, D_MODEL // 2, SHARD_IN

    def body(half_ref, x_ref, dh_ref, o_ref):
        k = pl.program_id(1)
        part = _dot_tn(x_ref[...].astype(BF16), dh_ref[...])

        @pl.when(k == 0)
        def _():
            o_ref[...] = part

        @pl.when(k > 0)
        def _():
            o_ref[...] += part

    (g,), exchanged = _call(
        body, name=name, grid=(N_SHARDS, seq // ts),
        in_specs=[pl.BlockSpec((ts, td), lambda e, k, half_ref: (k, half_ref[0])),
                  pl.BlockSpec((ts, te), lambda e, k, half_ref: (k, e))],
        out_specs=[pl.BlockSpec((None, td, te), lambda e, k, half_ref: (e, 0, 0))],
        out_shape=[jax.ShapeDtypeStruct((N_SHARDS, td, te), F32)],
        scratch_shapes=[], semantics=("parallel", "arbitrary"), vmem_mib=56, args=(x, dh), comm=comm,
        prefetch=(half,))
    return g, exchanged


def _grad_w_out(y, dzb):
    seq = y.shape[0]
    ts, te = 2048, 1024

    def body(y_ref, dz_ref, o_ref):
        k = pl.program_id(1)
        part = _dot_tn(y_ref[...], dz_ref[...])

        @pl.when(k == 0)
        def _():
            o_ref[...] = part

        @pl.when(k > 0)
        def _():
            o_ref[...] += part

    return _pallas(
        body, name="grad_w_out", grid=(D_MODEL // te, seq // ts),
        in_specs=[pl.BlockSpec((ts, te), lambda e, k: (k, e)), pl.BlockSpec((ts, D_MODEL), lambda e, k: (k, 0))],
        out_specs=pl.BlockSpec((te, D_MODEL), lambda e, k: (e, 0)),
        out_shape=jax.ShapeDtypeStruct((D_MODEL, D_MODEL), F32),
        compiler_params=_params(("parallel", "arbitrary"), 56),
    )(y, dzb)


GRAD_X_LATE_SHARDS = 1
GRAD_X_PARTIAL_ROWS = 512


def _grad_x_partial(dh, w_in_g, dz, first, tiles, prev=None, comm=None):
    seq = dh.shape[0]
    tm, tk = GRAD_X_PARTIAL_ROWS, SHARD_IN

    def body(*refs):
        dh_ref, w_ref, dz_ref, o_ref = refs[-4:]
        k = pl.program_id(1)
        part = _dot_nt(dh_ref[...], w_ref[...])

        @pl.when(k == 0)
        def _():
            o_ref[...] = DEEPNORM_ALPHA * dz_ref[...] + part

        @pl.when(k > 0)
        def _():
            o_ref[...] += part

    carried = [] if prev is None else [prev]
    row = pl.BlockSpec((tm, D_MODEL), lambda i, k: (i + first, 0))
    (partial,), exchanged = _call(
        body, name="grad_x_partial_%d" % first, grid=(tiles, N_SHARDS - GRAD_X_LATE_SHARDS),
        in_specs=[ANY] * len(carried) + [
            pl.BlockSpec((tm, tk), lambda i, k: (i + first, k)),
            pl.BlockSpec((None, D_MODEL, tk), lambda i, k: (k, 0, 0)), row],
        out_specs=[row],
        out_shape=[jax.ShapeDtypeStruct((seq, D_MODEL), F32)],
        scratch_shapes=[], semantics=("parallel", "arbitrary"), vmem_mib=48, args=(*carried, dh, w_in_g, dz),
        aliases={0: 0} if carried else None, comm=comm)
    return partial, exchanged


def _grad_x_final(dh, w_in_g, partial):
    seq = dh.shape[0]
    tm, tk = 512, SHARD_IN
    k0 = N_SHARDS - GRAD_X_LATE_SHARDS

    def body(dh_ref, w_ref, p_ref, o_ref):
        k = pl.program_id(1)
        part = _dot_nt(dh_ref[...], w_ref[...])

        @pl.when(k == 0)
        def _():
            o_ref[...] = p_ref[...] + part

        @pl.when(k > 0)
        def _():
            o_ref[...] += part

    row = pl.BlockSpec((tm, D_MODEL), lambda i, k: (i, 0))
    return _pallas(
        body, name="grad_x_final", grid=(seq // tm, GRAD_X_LATE_SHARDS),
        in_specs=[pl.BlockSpec((tm, tk), lambda i, k: (i, k + k0)),
                  pl.BlockSpec((None, D_MODEL, tk), lambda i, k: (k + k0, 0, 0)), row],
        out_specs=row, out_shape=jax.ShapeDtypeStruct((seq, D_MODEL), F32),
        compiler_params=_params(("parallel", "arbitrary"), 48),
    )(dh, w_in_g, partial)


def _pool_weight(w_pool_sh):
    n_groups = len(POOL_WINDOWS)
    shard_c = POOL_GROUP_DIM // N_SHARDS
    return w_pool_sh.reshape(N_SHARDS, n_groups, shard_c, POOL_GROUP_DIM)


def _step(x, target, w_bufs, pool_scale, gain, bias, place):
    seq = x.shape[0]
    tabs = _rope_tables(seq)
    core, chip_core, onward, plan = place
    qkv, hug, w_in_g, w_out_g, w_pool_sh = _in_proj_gathering(x, w_bufs, tabs, plan)
    o_list, st_list = [], []
    for p, dil in enumerate(DILATIONS):
        o, st = _attn_fwd(qkv[p], "attn_fwd_d%d" % dil)
        o_list.append(o)
        st_list.append(st)
    w_pool_g = _pool_weight(w_pool_sh)
    y, mixpre, lse_all, pooled = _mix_gate(o_list, st_list, hug, w_pool_g, pool_scale)
    dz, dzb, gain_part, bias_part, loss_part = _out_proj_loss(y, w_out_g, x, target, gain, bias)
    dh, dpo, do_list, stat_list = _dy_gate_bwd(dzb, w_out_g, hug, mixpre, pool_scale, lse_all)
    g_w_out = _grad_w_out(y, dzb)
    dh, g_w_pool, scale_part = _pool_bwd(dh, dpo, mixpre, pooled, w_pool_g, pool_scale)
    small = jnp.concatenate([scale_part, gain_part, bias_part, loss_part], axis=1)
    early = [g_w_out.reshape(N_SHARDS, 2, D_MODEL // (2 * N_SHARDS), D_MODEL), g_w_pool]

    bwd = lambda p, comm: _attn_bwd(qkv[p], do_list[p], stat_list[p], "attn_bwd_d%d" % DILATIONS[p], comm)
    part_a, halves = bwd(0, _exchange_halves(early))
    sums_b = [_add_own_half(g, h, core, "add_own_half_%d" % a) for a, (g, h) in enumerate(zip(early, halves))]
    part_b, recv = bwd(1, _scatter_to_chips(sums_b))
    bufs = [_add_chips([g, h], r, chip_core, "add_chips_%d" % a)
            for a, (g, h, r) in enumerate(zip(early, halves, recv))]
    part_c, reduced = bwd(2, _share_with_sibling(bufs))
    parts = [part_a, part_b, part_c]
    dh, gathered = _sum_patterns(dh, [t[0] for t in parts], tabs, True, 0, "sum_dq", _gather_small(small))
    dh, _ = _sum_patterns(dh, [t[1] for t in parts], tabs, True, 1, "sum_dk")
    dh, _ = _sum_patterns(dh, [t[2] for t in parts], tabs, False, 2, "sum_dv")

    give, _ = _grad_w_in(x, dh, 1 - core, "grad_w_in_give")
    keep, recv = _grad_w_in(x, dh, core, "grad_w_in_keep", _send_to_sibling([give]))
    total = [keep, recv[0]]
    total_b = _add_pair(keep, recv[0], "add_own_half_w_in")
    n_tiles = seq // GRAD_X_PARTIAL_ROWS
    tiles = 3 * n_tiles // 8
    part, relayed = _grad_x_partial(dh, w_in_g, dz, 0, tiles, None, _relay_diagonal(total_b))
    total_b = _fold_relayed(total, total_b, relayed[0], onward)
    part, recv = _grad_x_partial(dh, w_in_g, dz, tiles, n_tiles - tiles, part, _scatter_to_neighbours(total_b))
    buf = _add_chips(total, recv[0], chip_core, "add_chips_w_in")
    g_x = _grad_x_final(dh, w_in_g, part)
    g_w_in = _run_exchange(_share_with_sibling([buf]), "share_w_in")[0]
    return g_x, g_w_in, reduced[0], reduced[1], small, gathered[0]


def _exchange_halves(grads):
    n = len(grads)

    def copies(src, dst, sems):
        x, y, c, _ = _mesh_place()
        return [_remote(src[a].at[j, 1 - c], dst[a].at[j], sems[0].at[a, j], sems[1].at[a, j], (x, y, 1 - c))
                for a in range(n) for j in range(N_SHARDS)]

    def start(src, dst, sems):
        for cp in copies(src, dst, sems):
            cp.start()

    def finish(src, dst, sems):
        for cp in copies(src, dst, sems):
            cp.wait()

    return _Exchange(grads, [jax.ShapeDtypeStruct((N_SHARDS,) + g.shape[2:], g.dtype) for g in grads], {},
                     [pltpu.SemaphoreType.DMA((n, N_SHARDS))] * 2, start, finish)


def _add_own_half(grad, recv, core, name):
    _, _, r, c = grad.shape
    tr = min(r, 256)

    def body(core_ref, g_ref, r_ref, ob_ref):
        ob_ref[...] = (g_ref[...] + r_ref[...]).astype(BF16)

    return _pallas(
        body, name=name,
        grid_spec=pltpu.PrefetchScalarGridSpec(
            num_scalar_prefetch=1, grid=(N_SHARDS, r // tr),
            in_specs=[pl.BlockSpec((None, None, tr, c), lambda j, i, core_ref: (j, core_ref[0], i, 0)),
                      pl.BlockSpec((None, tr, c), lambda j, i, core_ref: (j, i, 0))],
            out_specs=pl.BlockSpec((None, tr, c), lambda j, i, core_ref: (j, i, 0))),
        out_shape=jax.ShapeDtypeStruct((N_SHARDS, r, c), BF16),
        compiler_params=_params(("parallel", "parallel"), 32),
    )(core, grad, recv)


def _send_to_sibling(arrays):
    n = len(arrays)

    def copies(src, dst, sems):
        x, y, c, _ = _mesh_place()
        return [_remote(src[a], dst[a], sems[0].at[a], sems[1].at[a], (x, y, 1 - c)) for a in range(n)]

    def start(src, dst, sems):
        for cp in copies(src, dst, sems):
            cp.start()

    def finish(src, dst, sems):
        for cp in copies(src, dst, sems):
            cp.wait()

    return _Exchange(arrays, [jax.ShapeDtypeStruct(t.shape, t.dtype) for t in arrays], {},
                     [pltpu.SemaphoreType.DMA((n,))] * 2, start, finish)


def _add_pair(a, b, name):
    _, r, c = a.shape
    tr = min(r, 256)

    def body(a_ref, b_ref, ob_ref):
        ob_ref[...] = (a_ref[...] + b_ref[...]).astype(BF16)

    spec = pl.BlockSpec((None, tr, c), lambda j, i: (j, i, 0))
    return _pallas(
        body, name=name, grid=(N_SHARDS, r // tr), in_specs=[spec, spec], out_specs=spec,
        out_shape=jax.ShapeDtypeStruct(a.shape, BF16),
        compiler_params=_params(("parallel", "parallel"), 32),
    )(a, b)


def _scatter_to_chips(sums):
    n = len(sums)

    def copies(src, dst, sems):
        x, y, c, chips = _mesh_place()
        return [_remote(src[a].at[2 * cx + cy], dst[a].at[k], sems[0].at[a, k], sems[1].at[a, k], (cx, cy, c))
                for a in range(n) for k, (cx, cy) in enumerate(chips)]

    def start(src, dst, sems):
        for cp in copies(src, dst, sems):
            cp.start()

    def finish(src, dst, sems):
        for cp in copies(src, dst, sems):
            cp.wait()

    return _Exchange(sums, [jax.ShapeDtypeStruct((3,) + s.shape[1:], s.dtype) for s in sums], {},
                     [pltpu.SemaphoreType.DMA((n, 3))] * 2, start, finish)


def _add_chips(sums, recv, chip_core, name):
    r, c = sums[0].shape[-2:]
    n_sums, n_recv = len(sums), recv.shape[0]
    tr = min(r, 256)
    mine = {3: pl.BlockSpec((None, tr, c), lambda i, cc_ref: (cc_ref[0], i, 0)),
            4: pl.BlockSpec((None, None, tr, c), lambda i, cc_ref: (cc_ref[0], cc_ref[1], i, 0))}

    def body(cc_ref, *refs):
        r_ref, o_ref = refs[n_sums:]
        tot = refs[0][...]
        for s_ref in refs[1:n_sums]:
            tot = tot + s_ref[...]
        for k in range(n_recv):
            tot = tot + r_ref[k].astype(F32)
        o_ref[...] = tot

    return _pallas(
        body, name=name,
        grid_spec=pltpu.PrefetchScalarGridSpec(
            num_scalar_prefetch=1, grid=(r // tr,),
            in_specs=[mine[s.ndim] for s in sums] + [pl.BlockSpec((n_recv, tr, c), lambda i, cc_ref: (0, i, 0))],
            out_specs=pl.BlockSpec((None, tr, c), lambda i, cc_ref: (cc_ref[1], i, 0))),
        out_shape=jax.ShapeDtypeStruct((2, r, c), F32),
        compiler_params=_params(("parallel",), 32),
    )(chip_core, *sums, recv)


def _relay_diagonal(sums_b):
    def copy(src, dst, sems):
        x, y, c, _ = _mesh_place()
        diagonal = 2 * (1 - x) + (1 - y)
        return _remote(src[0].at[diagonal], dst[0], sems[0].at[0], sems[1].at[0], (x ^ (1 - c), y ^ c, c))

    def start(src, dst, sems):
        copy(src, dst, sems).start()

    def finish(src, dst, sems):
        copy(src, dst, sems).wait()

    return _Exchange([sums_b], [jax.ShapeDtypeStruct(sums_b.shape[1:], sums_b.dtype)], {},
                     [pltpu.SemaphoreType.DMA((1,))] * 2, start, finish)


def _fold_relayed(sums, sums_b, relayed, onward):
    _, r, c = sums[0].shape
    n_sums = len(sums)
    tr = min(r, 256)

    def body(on_ref, b_in_ref, *refs):
        r_ref, o_ref = refs[n_sums:]
        tot = refs[0][...]
        for s_ref in refs[1:n_sums]:
            tot = tot + s_ref[...]
        o_ref[...] = (tot + r_ref[...].astype(F32)).astype(BF16)

    return _pallas(
        body, name="fold_relayed",
        grid_spec=pltpu.PrefetchScalarGridSpec(
            num_scalar_prefetch=1, grid=(r // tr,),
            in_specs=[ANY] + [pl.BlockSpec((None, tr, c), lambda i, on_ref: (on_ref[0], i, 0))] * n_sums
            + [pl.BlockSpec((tr, c), lambda i, on_ref: (i, 0))],
            out_specs=pl.BlockSpec((None, tr, c), lambda i, on_ref: (on_ref[0], i, 0))),
        out_shape=jax.ShapeDtypeStruct(sums_b.shape, sums_b.dtype),
        input_output_aliases={1: 0},
        compiler_params=_params(("parallel",), 32),
    )(onward, sums_b, *sums, relayed)


def _scatter_to_neighbours(sums_b):
    def copies(src, dst, sems):
        x, y, c, chips = _mesh_place()
        return [_remote(src[0].at[2 * cx + cy], dst[0].at[k], sems[0].at[k], sems[1].at[k], (cx, cy, c))
                for k, (cx, cy) in enumerate(chips[:2])]

    def start(src, dst, sems):
        for cp in copies(src, dst, sems):
            cp.start()

    def finish(src, dst, sems):
        for cp in copies(src, dst, sems):
            cp.wait()

    return _Exchange([sums_b], [jax.ShapeDtypeStruct((2,) + sums_b.shape[1:], sums_b.dtype)], {},
                     [pltpu.SemaphoreType.DMA((2,))] * 2, start, finish)


def _share_with_sibling(bufs):
    n = len(bufs)

    def copies(dst, sems, half):
        x, y, c, _ = _mesh_place()
        h = c if half == "mine" else 1 - c
        return [_remote(dst[a].at[h], dst[a].at[h], sems[0].at[a], sems[1].at[a], (x, y, 1 - c)) for a in range(n)]

    def start(ins, dst, sems):
        for cp in copies(dst, sems, "mine"):
            cp.start()

    def finish(ins, dst, sems):
        for cp in copies(dst, sems, "theirs"):
            cp.wait_recv()
        for cp in copies(dst, sems, "mine"):
            cp.wait_send()

    return _Exchange(bufs, [jax.ShapeDtypeStruct(b.shape, b.dtype) for b in bufs], {a: a for a in range(n)},
                     [pltpu.SemaphoreType.DMA((n,))] * 2, start, finish)


def _adam_math(w, g, m, v):
    m = ADAM_B1 * m + (1.0 - ADAM_B1) * g
    v = ADAM_B2 * v + (1.0 - ADAM_B2) * (g * g)
    m_hat = m / (1.0 - ADAM_B1 ** ADAM_STEP)
    v_hat = v / (1.0 - ADAM_B2 ** ADAM_STEP)
    delta = -ADAM_LR * (m_hat / (jnp.sqrt(v_hat) + ADAM_EPS) + ADAM_WD * w)
    return delta, m, v


def _gather_small(small):
    def peers():
        x, y, c, _ = _mesh_place()
        return [(x ^ ((r >> 2) & 1), y ^ ((r >> 1) & 1), c ^ (r & 1)) for r in range(1, 8)], 4 * x + 2 * y + c

    def start(src, dst, sems):
        to, me = peers()
        for r, peer in enumerate(to):
            _remote(src[0], dst[0].at[me], sems[0].at[r], sems[1].at[r], peer).start()

    def finish(src, dst, sems):
        to, me = peers()
        for r, (px, py, pc) in enumerate(to):
            theirs = dst[0].at[4 * px + 2 * py + pc]
            _remote(theirs, theirs, sems[0].at[r], sems[1].at[r], (px, py, pc)).wait_recv()
        for r, peer in enumerate(to):
            _remote(src[0], dst[0].at[me], sems[0].at[r], sems[1].at[r], peer).wait_send()

    return _Exchange([small], [jax.ShapeDtypeStruct((8,) + small.shape, small.dtype)], {},
                     [pltpu.SemaphoreType.DMA((7,))] * 2, start, finish)


def _small_adamw(gathered, small, me, w_vecs, m_vecs, v_vecs):
    n = len(w_vecs)
    widths = [w.shape[1] for w in w_vecs]
    n_par = sum(widths)

    def body(me_ref, a_ref, s_ref, *refs):
        w_refs, m_refs, v_refs = refs[:n], refs[n:2 * n], refs[2 * n:3 * n]
        loss_ref, outs = refs[3 * n], refs[3 * n + 1:]
        mine = s_ref[...]
        tot = jnp.where(me_ref[0] == 0, mine, a_ref[0])
        for d in range(1, 8):
            tot = tot + jnp.where(me_ref[0] == d, mine, a_ref[d])
        tot = jnp.sum(tot, axis=0, keepdims=True)
        sq = jnp.sum(tot[:, n_par:], axis=1, keepdims=True)
        loss_ref[...] = jnp.broadcast_to(sq * (0.5 / D_MODEL), loss_ref.shape)
        lo = 0
        for k in range(n):
            g = tot[:, lo:lo + widths[k]]
            lo += widths[k]
            outs[k][...] = g
            outs[n + k][...], outs[2 * n + k][...], outs[3 * n + k][...] = _adam_math(
                w_refs[k][...], g, m_refs[k][...], v_refs[k][...])

    vm = pl.BlockSpec(memory_space=pltpu.VMEM)
    vecs = [jax.ShapeDtypeStruct((1, w), F32) for w in widths] * 4
    res = pl.pallas_call(
        body, name="small_adamw",
        grid_spec=pltpu.PrefetchScalarGridSpec(num_scalar_prefetch=1, grid=(), in_specs=[vm] * (2 + 3 * n),
                                               out_specs=[vm] * (1 + 4 * n)),
        out_shape=[jax.ShapeDtypeStruct((1, 128), F32)] + vecs,
    )(me, gathered, small, *w_vecs, *m_vecs, *v_vecs)
    return res[0], res[1:1 + n], res[1 + n:1 + 2 * n], res[1 + 2 * n:1 + 3 * n], res[1 + 3 * n:]


def _adamw(w, g, m, v, name):
    r, c = w.shape
    tr = min(r, 256)

    def body(w_ref, g_ref, m_ref, v_ref, go_ref, d_ref, nm_ref, nv_ref):
        g = g_ref[...]
        go_ref[...] = g
        d_ref[...], nm_ref[...], nv_ref[...] = _adam_math(w_ref[...], g, m_ref[...], v_ref[...])

    spec = pl.BlockSpec((tr, c), lambda i: (i, 0))
    shape = jax.ShapeDtypeStruct((r, c), F32)
    return _pallas(
        body, name=name, grid=(r // tr,),
        in_specs=[spec] * 4, out_specs=[spec] * 4, out_shape=[shape] * 4,
        compiler_params=_params(("parallel",), 48),
    )(w, g, m, v)


def kernel(x, w_in, w_pool, pool_scale, w_out, ln_gain, ln_bias, loss_target, m_w_in, m_w_pool, m_pool_scale, m_w_out, m_ln_gain, m_ln_bias, v_w_in, v_w_pool, v_pool_scale, v_w_out, v_ln_gain, v_ln_bias):
    xi, yi, ci = lax.axis_index("x"), lax.axis_index("y"), lax.axis_index("c")
    chip = (2 * xi + yi).astype(jnp.int32).reshape(1)
    core = ci.astype(jnp.int32).reshape(1)
    n_groups = len(POOL_WINDOWS)
    shard_c = w_pool.shape[2]

    w_in_b = _cast_bf16(w_in[0], chip, "cast_w_in", 256)
    w_out_b = _cast_bf16(w_out[0], chip, "cast_w_out", 256)
    w_pool_b = _cast_bf16(w_pool[0].reshape(n_groups * shard_c, POOL_GROUP_DIM), chip, "cast_w_pool", 256)

    chip_core = jnp.concatenate([chip, core])
    onward = (2 * (xi ^ ci) + (yi ^ (1 - ci))).astype(jnp.int32).reshape(1)
    g_x, full_in, full_out, full_pool, small, small_all = _step(
        x[0], loss_target[0], [w_in_b, w_out_b, w_pool_b], pool_scale, ln_gain, ln_bias,
        (core, chip_core, onward, _in_proj_plan(xi, yi)))
    half_c = shard_c // 2
    grad_w_in = full_in.reshape(D_MODEL, SHARD_IN)
    grad_w_out = full_out.reshape(D_MODEL // N_SHARDS, D_MODEL)
    grad_w_pool = (full_pool.reshape(2, n_groups, half_c, POOL_GROUP_DIM).transpose(1, 0, 2, 3)
                   .reshape(n_groups * shard_c, POOL_GROUP_DIM))

    grad_w_in, d_in, nm_in, nv_in = _adamw(w_in[0], grad_w_in, m_w_in[0], v_w_in[0], "adamw_w_in")
    grad_w_out, d_out, nm_out, nv_out = _adamw(w_out[0], grad_w_out, m_w_out[0], v_w_out[0], "adamw_w_out")
    flat = lambda t: t[0].reshape(n_groups * shard_c, POOL_GROUP_DIM)
    grad_w_pool, d_pool, nm_pool, nv_pool = _adamw(flat(w_pool), grad_w_pool, flat(m_w_pool), flat(v_w_pool),
                                                   "adamw_w_pool")

    me = (4 * xi + 2 * yi + ci).astype(jnp.int32).reshape(1)
    loss_v, g_vecs, d_vecs, nm_vecs, nv_vecs = _small_adamw(
        small_all, small, me, [pool_scale, ln_gain, ln_bias], [m_pool_scale, m_ln_gain, m_ln_bias],
        [v_pool_scale, v_ln_gain, v_ln_bias])
    g_scale, g_gain, g_bias = g_vecs
    d_scale, d_gain, d_bias = d_vecs
    nm_scale, nm_gain, nm_bias = nm_vecs
    nv_scale, nv_gain, nv_bias = nv_vecs
    pool_shape = w_pool.shape
    return (loss_v[0, 0], g_x[None],
            grad_w_in[None], grad_w_pool.reshape(pool_shape), g_scale, grad_w_out[None], g_gain, g_bias,
            d_in[None], d_pool.reshape(pool_shape), d_scale, d_out[None], d_gain, d_bias,
            nm_in[None], nm_pool.reshape(pool_shape), nm_scale, nm_out[None], nm_gain, nm_bias,
            nv_in[None], nv_pool.reshape(pool_shape), nv_scale, nv_out[None], nv_gain, nv_bias)
```
